```python
import jax
import jax.numpy as jnp
from jax import lax
import numpy as np

D_MODEL = 1024
BATCH = 8
SEQ = 4096
DEPTH = 1

N_META = 16
CHUNK = 128
PAD = CHUNK - N_META
RET_HEADS = 4
RET_QK_DIM = 128
RET_V_DIM = 256
RET_QK = RET_HEADS * RET_QK_DIM
RET_V = RET_HEADS * RET_V_DIM
SSD_D_INNER = 2 * D_MODEL
SSD_HEAD_DIM = 64
SSD_HEADS = SSD_D_INNER // SSD_HEAD_DIM
SSD_GROUPS = 4
SSD_HPG = SSD_HEADS // SSD_GROUPS
SSD_STATE = 128
SSD_CONV = 3
SSD_XBC = SSD_D_INNER + 2 * SSD_GROUPS * SSD_STATE
D_FF = 2816
FFN_CONV = 3
EPS = 1e-6
ROPE_BASE = 10000.0
IN_SIZES = (RET_QK, RET_QK, RET_V, RET_V, SSD_D_INNER, SSD_XBC, SSD_HEADS, SSD_HEADS, D_MODEL, D_MODEL)
D_IN = sum(IN_SIZES)

kernel_name = 'hybrid_retnet_ssd_encoder_block'


def _split(t, sizes):
    out = []
    start = 0
    for s in sizes:
        out.append(t[..., start:start + s])
        start += s
    return out


def rms_norm(x, w):
    x32 = x.astype(jnp.float32)
    y = x32 * lax.rsqrt(jnp.mean(x32 * x32, axis=-1, keepdims=True) + EPS)
    return y.astype(x.dtype) * w


def group_rms_norm(x, w, groups):
    shp = x.shape
    xg = x.reshape(shp[:-1] + (groups, shp[-1] // groups)).astype(jnp.float32)
    y = xg * lax.rsqrt(jnp.mean(xg * xg, axis=-1, keepdims=True) + EPS)
    return y.reshape(shp).astype(x.dtype) * w


def head_group_norm(y):
    y32 = y.astype(jnp.float32)
    mu = jnp.mean(y32, axis=-1, keepdims=True)
    var = jnp.mean(jnp.square(y32 - mu), axis=-1, keepdims=True)
    return ((y32 - mu) * lax.rsqrt(var + EPS)).astype(y.dtype)


def dw_conv_centred(x, w, b):
    k = w.shape[0]
    y = lax.conv_general_dilated(x, w[:, None, :], window_strides=(1,), padding=[(k // 2, k // 2)],
                                 dimension_numbers=('NWC', 'WIO', 'NWC'), feature_group_count=x.shape[-1])
    return y + b


def rotary(x, pos):
    half = x.shape[-1] // 2
    inv = ROPE_BASE ** (-jnp.arange(half, dtype=jnp.float32) / half)
    ang = pos.astype(jnp.float32)[:, None] * inv[None, :]
    cos = jnp.cos(ang)[None, :, None, :].astype(x.dtype)
    sin = jnp.sin(ang)[None, :, None, :].astype(x.dtype)
    x1, x2 = x[..., :half], x[..., half:]
    return jnp.concatenate([x1 * cos - x2 * sin, x1 * sin + x2 * cos], axis=-1)


def pad_front(t):
    return jnp.pad(t, ((0, 0), (PAD, 0)) + ((0, 0),) * (t.ndim - 2))


def to_chunks(t):
    return t.reshape((t.shape[0], t.shape[1] // CHUNK, CHUNK) + t.shape[2:])


def from_chunks(t):
    return t.reshape((t.shape[0], t.shape[1] * CHUNK) + t.shape[3:])


def flip_seq(t):
    return jnp.flip(t, axis=1)


def exclusive_chunk_scan(states, decays):
    s = jnp.moveaxis(states, 1, 0)
    d = jnp.moveaxis(decays, 1, 0)

    def step(carry, sd):
        s_n, d_n = sd
        return carry * d_n + s_n, carry

    _, prev = lax.scan(step, jnp.zeros_like(s[0]), (s, d))
    return jnp.moveaxis(prev, 0, 1)


def retention_intra(qc, kc, vc, log_gamma):
    pos = jnp.arange(CHUNK, dtype=jnp.float32)
    dist = jnp.abs(pos[:, None] - pos[None, :])
    dmat = jnp.exp(log_gamma[:, None, None] * dist[None]).astype(qc.dtype)
    s = jnp.einsum('bnlhd,bnshd->bnhls', qc, kc) * dmat
    return jnp.einsum('bnhls,bnshe->bnlhe', s, vc)


def retention_cross_forward(qc, kc, vc, log_gamma):
    pos = jnp.arange(CHUNK, dtype=jnp.float32)
    k_dec = jnp.exp((CHUNK - 1 - pos)[:, None] * log_gamma[None, :]).astype(kc.dtype)
    q_dec = jnp.exp((pos + 1)[:, None] * log_gamma[None, :]).astype(qc.dtype)
    states = jnp.einsum('bnshd,sh,bnshe->bnhde', kc, k_dec, vc)
    b, n = states.shape[:2]
    chunk_dec = jnp.broadcast_to(jnp.exp(CHUNK * log_gamma).astype(states.dtype)[None, None, :, None, None],
                                 (b, n, RET_HEADS, 1, 1))
    prev = exclusive_chunk_scan(states, chunk_dec)
    return jnp.einsum('bnlhd,lh,bnhde->bnlhe', qc, q_dec, prev)


def bidirectional_retention(q, k, v, log_gamma):
    qc, kc, vc = to_chunks(q), to_chunks(k), to_chunks(v)
    y = retention_intra(qc, kc, vc, log_gamma) + retention_cross_forward(qc, kc, vc, log_gamma)
    qr, kr, vr = to_chunks(flip_seq(q)), to_chunks(flip_seq(k)), to_chunks(flip_seq(v))
    y_back = flip_seq(from_chunks(retention_cross_forward(qr, kr, vr, log_gamma)))
    return from_chunks(y) + y_back


def ssd_scan_forward(x, dt, a, bm, cm):
    b, lp = x.shape[:2]
    n = lp // CHUNK
    xd = (x * dt[..., None]).reshape(b, n, CHUNK, SSD_GROUPS, SSD_HPG, SSD_HEAD_DIM)
    acs = jnp.cumsum((dt * a).astype(jnp.float32).reshape(b, n, CHUNK, SSD_GROUPS, SSD_HPG), axis=2)
    bc = to_chunks(bm)
    cc = to_chunks(cm)
    diff = acs[:, :, :, None] - acs[:, :, None, :]
    tri = jnp.tril(jnp.ones((CHUNK, CHUNK), dtype=bool))[:, :, None, None]
    lmat = jnp.exp(jnp.where(tri, diff, -jnp.inf)).astype(x.dtype)
    cb = jnp.einsum('bclgn,bcsgn->bclsg', cc, bc)
    y_intra = jnp.einsum('bclsgh,bcsghp->bclghp', cb[..., None] * lmat, xd)
    decay_end = jnp.exp(acs[:, :, -1:] - acs).astype(x.dtype)
    states = jnp.einsum('bcsgn,bcsgh,bcsghp->bcghpn', bc, decay_end, xd)
    chunk_dec = jnp.exp(acs[:, :, -1]).astype(x.dtype)[..., None, None]
    prev = exclusive_chunk_scan(states, chunk_dec)
    y_off = jnp.einsum('bclgn,bclgh,bcghpn->bclghp', cc, jnp.exp(acs).astype(x.dtype), prev)
    return (y_intra + y_off).reshape(b, lp, SSD_HEADS, SSD_HEAD_DIM)


def hybrid_layer(h, pos, norm_mix_w, w_in, ret_gn_w, w_ret_out, w_ssd_conv, b_ssd_conv,
                 dt_bias_f, dt_bias_b, a_log_f, a_log_b, d_skip, ssd_norm_w, w_ssd_out, w_out,
                 norm_ffn_w, w_ffn_up, w_ffn_conv, b_ffn_conv, w_ffn_down):
    b, l, _ = h.shape
    u = rms_norm(h, norm_mix_w)
    proj = u @ w_in
    q, k, v, g_ret, z, xbc, dt_f, dt_b, gate_ret, gate_ssd = _split(proj, IN_SIZES)

    log_gamma = jnp.log(1.0 - 2.0 ** (-5.0 - jnp.arange(RET_HEADS, dtype=jnp.float32)))
    q = rotary(q.reshape(b, l, RET_HEADS, RET_QK_DIM), pos)
    k = rotary(k.reshape(b, l, RET_HEADS, RET_QK_DIM), pos) * (RET_QK_DIM ** -0.5)
    v = v.reshape(b, l, RET_HEADS, RET_V_DIM)
    y_ret = bidirectional_retention(pad_front(q), pad_front(k), pad_front(v), log_gamma)[:, PAD:]
    y_ret = head_group_norm(y_ret).reshape(b, l, RET_V) * ret_gn_w
    y_ret = (jax.nn.silu(g_ret) * y_ret) @ w_ret_out

    xbc = jax.nn.silu(dw_conv_centred(xbc, w_ssd_conv, b_ssd_conv))
    xs, bm, cm = _split(xbc, (SSD_D_INNER, SSD_GROUPS * SSD_STATE, SSD_GROUPS * SSD_STATE))
    xs = xs.reshape(b, l, SSD_HEADS, SSD_HEAD_DIM)
    bm = bm.reshape(b, l, SSD_GROUPS, SSD_STATE)
    cm = cm.reshape(b, l, SSD_GROUPS, SSD_STATE)
    dtf = jax.nn.softplus(dt_f + dt_bias_f)
    dtb = jax.nn.softplus(dt_b + dt_bias_b)
    a_f = -jnp.exp(a_log_f.astype(jnp.float32))
    a_b = -jnp.exp(a_log_b.astype(jnp.float32))
    xp, bp, cp = pad_front(xs), pad_front(bm), pad_front(cm)
    y_f = ssd_scan_forward(xp, pad_front(dtf), a_f, bp, cp)
    y_b = flip_seq(ssd_scan_forward(flip_seq(xp), flip_seq(pad_front(dtb)), a_b, flip_seq(bp), flip_seq(cp)))
    y = (y_f + y_b)[:, PAD:] + xs * d_skip[:, None]
    y = y.reshape(b, l, SSD_D_INNER) * jax.nn.silu(z)
    y_ssd = group_rms_norm(y, ssd_norm_w, SSD_GROUPS) @ w_ssd_out

    merged = jax.nn.sigmoid(gate_ret) * y_ret + jax.nn.sigmoid(gate_ssd) * y_ssd
    h = h + merged @ w_out

    f = dw_conv_centred(rms_norm(h, norm_ffn_w) @ w_ffn_up, w_ffn_conv, b_ffn_conv)
    fg, fu = _split(f, (D_FF, D_FF))
    return h + (jax.nn.silu(fg) * fu) @ w_ffn_down


def _fwd_setup_inputs(seed: int = 0) -> dict:
    key = jax.random.key(seed)
    ks = jax.random.split(key, 24)
    f32 = jnp.float32

    def nrm(k, shape, scale):
        return jax.random.normal(k, shape, f32) * scale

    dt0 = jnp.exp(jax.random.uniform(ks[8], (2, DEPTH, SSD_HEADS), f32, minval=np.log(1e-3), maxval=np.log(1e-1)))
    dt_bias = dt0 + jnp.log(-jnp.expm1(-dt0))
    a_log = jnp.log(jax.random.uniform(ks[9], (2, DEPTH, SSD_HEADS), f32, minval=1.0, maxval=16.0))
    return {
        'x': nrm(ks[0], (BATCH, SEQ, D_MODEL), 1.0),
        'meta_tokens': nrm(ks[1], (N_META, D_MODEL), 1.0),
        'norm_mix_w': 1.0 + nrm(ks[2], (DEPTH, D_MODEL), 0.02),
        'w_in': nrm(ks[3], (DEPTH, D_MODEL, D_IN), D_MODEL ** -0.5),
        'ret_gn_w': 1.0 + nrm(ks[4], (DEPTH, RET_V), 0.02),
        'w_ret_out': nrm(ks[5], (DEPTH, RET_V, D_MODEL), RET_V ** -0.5),
        'w_ssd_conv': nrm(ks[6], (DEPTH, SSD_CONV, SSD_XBC), SSD_CONV ** -0.5),
        'b_ssd_conv': nrm(ks[7], (DEPTH, SSD_XBC), 0.02),
        'dt_bias_f': dt_bias[0],
        'dt_bias_b': dt_bias[1],
        'a_log_f': a_log[0],
        'a_log_b': a_log[1],
        'd_skip': 1.0 + nrm(ks[10], (DEPTH, SSD_HEADS), 0.02),
        'ssd_norm_w': 1.0 + nrm(ks[11], (DEPTH, SSD_D_INNER), 0.02),
        'w_ssd_out': nrm(ks[12], (DEPTH, SSD_D_INNER, D_MODEL), SSD_D_INNER ** -0.5),
        'w_out': nrm(ks[13], (DEPTH, D_MODEL, D_MODEL), D_MODEL ** -0.5),
        'norm_ffn_w': 1.0 + nrm(ks[14], (DEPTH, D_MODEL), 0.02),
        'w_ffn_up': nrm(ks[15], (DEPTH, D_MODEL, 2 * D_FF), D_MODEL ** -0.5),
        'w_ffn_conv': nrm(ks[16], (DEPTH, FFN_CONV, 2 * D_FF), FFN_CONV ** -0.5),
        'b_ffn_conv': nrm(ks[17], (DEPTH, 2 * D_FF), 0.02),
        'w_ffn_down': nrm(ks[18], (DEPTH, D_FF, D_MODEL), D_FF ** -0.5),
        'final_norm_w': 1.0 + nrm(ks[19], (D_MODEL,), 0.02),
    }


def _fwd_reference(x, meta_tokens, norm_mix_w, w_in, ret_gn_w, w_ret_out, w_ssd_conv, b_ssd_conv,
              dt_bias_f, dt_bias_b, a_log_f, a_log_b, d_skip, ssd_norm_w, w_ssd_out, w_out,
              norm_ffn_w, w_ffn_up, w_ffn_conv, b_ffn_conv, w_ffn_down, final_norm_w):
    b = x.shape[0]
    meta = jnp.broadcast_to(meta_tokens[None].astype(x.dtype), (b, N_META, D_MODEL))
    h = jnp.concatenate([meta, x], axis=1)
    pos = jnp.arange(h.shape[1])
    for i in range(DEPTH):
        h = hybrid_layer(h, pos, norm_mix_w[i], w_in[i], ret_gn_w[i], w_ret_out[i], w_ssd_conv[i], b_ssd_conv[i],
                         dt_bias_f[i], dt_bias_b[i], a_log_f[i], a_log_b[i], d_skip[i], ssd_norm_w[i], w_ssd_out[i],
                         w_out[i], norm_ffn_w[i], w_ffn_up[i], w_ffn_conv[i], b_ffn_conv[i], w_ffn_down[i])
    h = rms_norm(h, final_norm_w)
    return h[:, N_META:]


import jax as _jax
import jax.numpy as _jnp

TWIN_FORMAT = 'train_step'
FWD_PARAMS = ['x', 'meta_tokens', 'norm_mix_w', 'w_in', 'ret_gn_w', 'w_ret_out', 'w_ssd_conv', 'b_ssd_conv', 'dt_bias_f', 'dt_bias_b', 'a_log_f', 'a_log_b', 'd_skip', 'ssd_norm_w', 'w_ssd_out', 'w_out', 'norm_ffn_w', 'w_ffn_up', 'w_ffn_conv', 'b_ffn_conv', 'w_ffn_down', 'final_norm_w']
TWIN_WEIGHTS = ['meta_tokens', 'norm_mix_w', 'w_in', 'ret_gn_w', 'w_ret_out', 'w_ssd_conv', 'b_ssd_conv', 'dt_bias_f', 'dt_bias_b', 'a_log_f', 'a_log_b', 'd_skip', 'ssd_norm_w', 'w_ssd_out', 'w_out', 'norm_ffn_w', 'w_ffn_up', 'w_ffn_conv', 'b_ffn_conv', 'w_ffn_down', 'final_norm_w']
TWIN_DIFF_INPUT = 'x'
TWIN_INPUTS = ['x', 'meta_tokens', 'norm_mix_w', 'w_in', 'ret_gn_w', 'w_ret_out', 'w_ssd_conv', 'b_ssd_conv', 'dt_bias_f', 'dt_bias_b', 'a_log_f', 'a_log_b', 'd_skip', 'ssd_norm_w', 'w_ssd_out', 'w_out', 'norm_ffn_w', 'w_ffn_up', 'w_ffn_conv', 'b_ffn_conv', 'w_ffn_down', 'final_norm_w', 'loss_target', 'm_meta_tokens', 'm_norm_mix_w', 'm_w_in', 'm_ret_gn_w', 'm_w_ret_out', 'm_w_ssd_conv', 'm_b_ssd_conv', 'm_dt_bias_f', 'm_dt_bias_b', 'm_a_log_f', 'm_a_log_b', 'm_d_skip', 'm_ssd_norm_w', 'm_w_ssd_out', 'm_w_out', 'm_norm_ffn_w', 'm_w_ffn_up', 'm_w_ffn_conv', 'm_b_ffn_conv', 'm_w_ffn_down', 'm_final_norm_w', 'v_meta_tokens', 'v_norm_mix_w', 'v_w_in', 'v_ret_gn_w', 'v_w_ret_out', 'v_w_ssd_conv', 'v_b_ssd_conv', 'v_dt_bias_f', 'v_dt_bias_b', 'v_a_log_f', 'v_a_log_b', 'v_d_skip', 'v_ssd_norm_w', 'v_w_ssd_out', 'v_w_out', 'v_norm_ffn_w', 'v_w_ffn_up', 'v_w_ffn_conv', 'v_b_ffn_conv', 'v_w_ffn_down', 'v_final_norm_w']
TWIN_OUTPUTS = ['loss', 'grad_x', 'grad_meta_tokens', 'grad_norm_mix_w', 'grad_w_in', 'grad_ret_gn_w', 'grad_w_ret_out', 'grad_w_ssd_conv', 'grad_b_ssd_conv', 'grad_dt_bias_f', 'grad_dt_bias_b', 'grad_a_log_f', 'grad_a_log_b', 'grad_d_skip', 'grad_ssd_norm_w', 'grad_w_ssd_out', 'grad_w_out', 'grad_norm_ffn_w', 'grad_w_ffn_up', 'grad_w_ffn_conv', 'grad_b_ffn_conv', 'grad_w_ffn_down', 'grad_final_norm_w', 'delta_meta_tokens', 'delta_norm_mix_w', 'delta_w_in', 'delta_ret_gn_w', 'delta_w_ret_out', 'delta_w_ssd_conv', 'delta_b_ssd_conv', 'delta_dt_bias_f', 'delta_dt_bias_b', 'delta_a_log_f', 'delta_a_log_b', 'delta_d_skip', 'delta_ssd_norm_w', 'delta_w_ssd_out', 'delta_w_out', 'delta_norm_ffn_w', 'delta_w_ffn_up', 'delta_w_ffn_conv', 'delta_b_ffn_conv', 'delta_w_ffn_down', 'delta_final_norm_w', 'new_m_meta_tokens', 'new_m_norm_mix_w', 'new_m_w_in', 'new_m_ret_gn_w', 'new_m_w_ret_out', 'new_m_w_ssd_conv', 'new_m_b_ssd_conv', 'new_m_dt_bias_f', 'new_m_dt_bias_b', 'new_m_a_log_f', 'new_m_a_log_b', 'new_m_d_skip', 'new_m_ssd_norm_w', 'new_m_w_ssd_out', 'new_m_w_out', 'new_m_norm_ffn_w', 'new_m_w_ffn_up', 'new_m_w_ffn_conv', 'new_m_b_ffn_conv', 'new_m_w_ffn_down', 'new_m_final_norm_w', 'new_v_meta_tokens', 'new_v_norm_mix_w', 'new_v_w_in', 'new_v_ret_gn_w', 'new_v_w_ret_out', 'new_v_w_ssd_conv', 'new_v_b_ssd_conv', 'new_v_dt_bias_f', 'new_v_dt_bias_b', 'new_v_a_log_f', 'new_v_a_log_b', 'new_v_d_skip', 'new_v_ssd_norm_w', 'new_v_w_ssd_out', 'new_v_w_out', 'new_v_norm_ffn_w', 'new_v_w_ffn_up', 'new_v_w_ffn_conv', 'new_v_b_ffn_conv', 'new_v_w_ffn_down', 'new_v_final_norm_w']
TWIN_LEAF_KINDS = {'loss': 'loss', 'grad_x': 'grad_x', 'grad_meta_tokens': 'grad_w', 'grad_norm_mix_w': 'grad_w', 'grad_w_in': 'grad_w', 'grad_ret_gn_w': 'grad_w', 'grad_w_ret_out': 'grad_w', 'grad_w_ssd_conv': 'grad_w', 'grad_b_ssd_conv': 'grad_w', 'grad_dt_bias_f': 'grad_w', 'grad_dt_bias_b': 'grad_w', 'grad_a_log_f': 'grad_w', 'grad_a_log_b': 'grad_w', 'grad_d_skip': 'grad_w', 'grad_ssd_norm_w': 'grad_w', 'grad_w_ssd_out': 'grad_w', 'grad_w_out': 'grad_w', 'grad_norm_ffn_w': 'grad_w', 'grad_w_ffn_up': 'grad_w', 'grad_w_ffn_conv': 'grad_w', 'grad_b_ffn_conv': 'grad_w', 'grad_w_ffn_down': 'grad_w', 'grad_final_norm_w': 'grad_w', 'delta_meta_tokens': 'delta_w', 'delta_norm_mix_w': 'delta_w', 'delta_w_in': 'delta_w', 'delta_ret_gn_w': 'delta_w', 'delta_w_ret_out': 'delta_w', 'delta_w_ssd_conv': 'delta_w', 'delta_b_ssd_conv': 'delta_w', 'delta_dt_bias_f': 'delta_w', 'delta_dt_bias_b': 'delta_w', 'delta_a_log_f': 'delta_w', 'delta_a_log_b': 'delta_w', 'delta_d_skip': 'delta_w', 'delta_ssd_norm_w': 'delta_w', 'delta_w_ssd_out': 'delta_w', 'delta_w_out': 'delta_w', 'delta_norm_ffn_w': 'delta_w', 'delta_w_ffn_up': 'delta_w', 'delta_w_ffn_conv': 'delta_w', 'delta_b_ffn_conv': 'delta_w', 'delta_w_ffn_down': 'delta_w', 'delta_final_norm_w': 'delta_w', 'new_m_meta_tokens': 'new_m', 'new_m_norm_mix_w': 'new_m', 'new_m_w_in': 'new_m', 'new_m_ret_gn_w': 'new_m', 'new_m_w_ret_out': 'new_m', 'new_m_w_ssd_conv': 'new_m', 'new_m_b_ssd_conv': 'new_m', 'new_m_dt_bias_f': 'new_m', 'new_m_dt_bias_b': 'new_m', 'new_m_a_log_f': 'new_m', 'new_m_a_log_b': 'new_m', 'new_m_d_skip': 'new_m', 'new_m_ssd_norm_w': 'new_m', 'new_m_w_ssd_out': 'new_m', 'new_m_w_out': 'new_m', 'new_m_norm_ffn_w': 'new_m', 'new_m_w_ffn_up': 'new_m', 'new_m_w_ffn_conv': 'new_m', 'new_m_b_ffn_conv': 'new_m', 'new_m_w_ffn_down': 'new_m', 'new_m_final_norm_w': 'new_m', 'new_v_meta_tokens': 'new_v', 'new_v_norm_mix_w': 'new_v', 'new_v_w_in': 'new_v', 'new_v_ret_gn_w': 'new_v', 'new_v_w_ret_out': 'new_v', 'new_v_w_ssd_conv': 'new_v', 'new_v_b_ssd_conv': 'new_v', 'new_v_dt_bias_f': 'new_v', 'new_v_dt_bias_b': 'new_v', 'new_v_a_log_f': 'new_v', 'new_v_a_log_b': 'new_v', 'new_v_d_skip': 'new_v', 'new_v_ssd_norm_w': 'new_v', 'new_v_w_ssd_out': 'new_v', 'new_v_w_out': 'new_v', 'new_v_norm_ffn_w': 'new_v', 'new_v_w_ffn_up': 'new_v', 'new_v_w_ffn_conv': 'new_v', 'new_v_b_ffn_conv': 'new_v', 'new_v_w_ffn_down': 'new_v', 'new_v_final_norm_w': 'new_v'}


def _forward(args):
    return _fwd_reference(*[args[k] for k in FWD_PARAMS])


def _output_shape():
    out = _jax.eval_shape(lambda: _forward(_fwd_setup_inputs(0)))
    return out.shape, out.dtype

N_MICROBATCH = 1
ADAM_LR = 0.001
ADAM_B1 = 0.9
ADAM_B2 = 0.999
ADAM_EPS = 1e-08
ADAM_WD = 0.01
ADAM_STEP = 10
PER_EXAMPLE_BATCH_AXIS = {'x': 0, 'loss_target': 0}
SHARED_INPUTS = []
_WEIGHT_DTYPES = {'meta_tokens': _jnp.float32, 'norm_mix_w': _jnp.float32, 'w_in': _jnp.float32, 'ret_gn_w': _jnp.float32, 'w_ret_out': _jnp.float32, 'w_ssd_conv': _jnp.float32, 'b_ssd_conv': _jnp.float32, 'dt_bias_f': _jnp.float32, 'dt_bias_b': _jnp.float32, 'a_log_f': _jnp.float32, 'a_log_b': _jnp.float32, 'd_skip': _jnp.float32, 'ssd_norm_w': _jnp.float32, 'w_ssd_out': _jnp.float32, 'w_out': _jnp.float32, 'norm_ffn_w': _jnp.float32, 'w_ffn_up': _jnp.float32, 'w_ffn_conv': _jnp.float32, 'b_ffn_conv': _jnp.float32, 'w_ffn_down': _jnp.float32, 'final_norm_w': _jnp.float32}
MOMENT_SCALE = {'meta_tokens': 3.890414e-03, 'norm_mix_w': 1.901328e-01, 'w_in': 5.529901e-02, 'ret_gn_w': 5.418489e-02, 'w_ret_out': 5.350917e-02, 'w_ssd_conv': 5.452010e-02, 'b_ssd_conv': 9.510487e-02, 'dt_bias_f': 1.987088e-01, 'dt_bias_b': 1.158897e-01, 'a_log_f': 1.338527e-01, 'a_log_b': 1.313870e-01, 'd_skip': 3.616114e-01, 'ssd_norm_w': 6.345560e-02, 'w_ssd_out': 8.996349e-02, 'w_out': 1.046195e-01, 'norm_ffn_w': 1.163508e-01, 'w_ffn_up': 4.981203e-02, 'w_ffn_conv': 5.022454e-02, 'b_ffn_conv': 5.067700e-02, 'w_ffn_down': 8.123000e-02, 'final_norm_w': 3.190465e+01}


def _to_microbatches(a, axis):
    t = _jnp.moveaxis(a, axis, 0)
    t = t.reshape((N_MICROBATCH, t.shape[0] // N_MICROBATCH) + t.shape[1:])
    return _jnp.moveaxis(t, 1, axis + 1)


def setup_inputs(seed: int = 0) -> dict:
    inp = _fwd_setup_inputs(seed)
    key = _jax.random.fold_in(_jax.random.key(seed), 7919)
    shape, _ = _output_shape()
    out = dict(inp)
    out["loss_target"] = _jax.random.normal(_jax.random.fold_in(key, 0), shape, _jnp.float32)
    for i, name in enumerate(TWIN_WEIGHTS):
        w = inp[name].astype(_jnp.float32)
        if MOMENT_SCALE is None:
            s = _jnp.sqrt(_jnp.mean(_jnp.square(w)) + 1e-30)
        else:
            s = MOMENT_SCALE[name]
        km, kv = _jax.random.split(_jax.random.fold_in(key, i + 1))
        out[name] = w
        out["m_" + name] = s * _jax.random.normal(km, w.shape, _jnp.float32)
        out["v_" + name] = (s * s) * _jax.random.uniform(kv, w.shape, _jnp.float32, 0.5, 1.5)
    if N_MICROBATCH > 1:
        for name, axis in PER_EXAMPLE_BATCH_AXIS.items():
            out[name] = _to_microbatches(out[name], axis)
    return {'x': out['x'], 'meta_tokens': out['meta_tokens'], 'norm_mix_w': out['norm_mix_w'], 'w_in': out['w_in'], 'ret_gn_w': out['ret_gn_w'], 'w_ret_out': out['w_ret_out'], 'w_ssd_conv': out['w_ssd_conv'], 'b_ssd_conv': out['b_ssd_conv'], 'dt_bias_f': out['dt_bias_f'], 'dt_bias_b': out['dt_bias_b'], 'a_log_f': out['a_log_f'], 'a_log_b': out['a_log_b'], 'd_skip': out['d_skip'], 'ssd_norm_w': out['ssd_norm_w'], 'w_ssd_out': out['w_ssd_out'], 'w_out': out['w_out'], 'norm_ffn_w': out['norm_ffn_w'], 'w_ffn_up': out['w_ffn_up'], 'w_ffn_conv': out['w_ffn_conv'], 'b_ffn_conv': out['b_ffn_conv'], 'w_ffn_down': out['w_ffn_down'], 'final_norm_w': out['final_norm_w'], 'loss_target': out['loss_target'], 'm_meta_tokens': out['m_meta_tokens'], 'm_norm_mix_w': out['m_norm_mix_w'], 'm_w_in': out['m_w_in'], 'm_ret_gn_w': out['m_ret_gn_w'], 'm_w_ret_out': out['m_w_ret_out'], 'm_w_ssd_conv': out['m_w_ssd_conv'], 'm_b_ssd_conv': out['m_b_ssd_conv'], 'm_dt_bias_f': out['m_dt_bias_f'], 'm_dt_bias_b': out['m_dt_bias_b'], 'm_a_log_f': out['m_a_log_f'], 'm_a_log_b': out['m_a_log_b'], 'm_d_skip': out['m_d_skip'], 'm_ssd_norm_w': out['m_ssd_norm_w'], 'm_w_ssd_out': out['m_w_ssd_out'], 'm_w_out': out['m_w_out'], 'm_norm_ffn_w': out['m_norm_ffn_w'], 'm_w_ffn_up': out['m_w_ffn_up'], 'm_w_ffn_conv': out['m_w_ffn_conv'], 'm_b_ffn_conv': out['m_b_ffn_conv'], 'm_w_ffn_down': out['m_w_ffn_down'], 'm_final_norm_w': out['m_final_norm_w'], 'v_meta_tokens': out['v_meta_tokens'], 'v_norm_mix_w': out['v_norm_mix_w'], 'v_w_in': out['v_w_in'], 'v_ret_gn_w': out['v_ret_gn_w'], 'v_w_ret_out': out['v_w_ret_out'], 'v_w_ssd_conv': out['v_w_ssd_conv'], 'v_b_ssd_conv': out['v_b_ssd_conv'], 'v_dt_bias_f': out['v_dt_bias_f'], 'v_dt_bias_b': out['v_dt_bias_b'], 'v_a_log_f': out['v_a_log_f'], 'v_a_log_b': out['v_a_log_b'], 'v_d_skip': out['v_d_skip'], 'v_ssd_norm_w': out['v_ssd_norm_w'], 'v_w_ssd_out': out['v_w_ssd_out'], 'v_w_out': out['v_w_out'], 'v_norm_ffn_w': out['v_norm_ffn_w'], 'v_w_ffn_up': out['v_w_ffn_up'], 'v_w_ffn_conv': out['v_w_ffn_conv'], 'v_b_ffn_conv': out['v_b_ffn_conv'], 'v_w_ffn_down': out['v_w_ffn_down'], 'v_final_norm_w': out['v_final_norm_w']}


def _loss(weights, diff, rest, loss_target):
    with _jax.named_scope("forward"):
        args = {**rest, TWIN_DIFF_INPUT: diff, **{k: w.astype(_WEIGHT_DTYPES[k]) for k, w in weights.items()}}
        y = _forward(args)
    with _jax.named_scope("loss_head"):
        err = _jnp.square(y.astype(_jnp.float32) - loss_target)
        return 0.5 * _jnp.sum(_jnp.mean(err, axis=-1)) if err.ndim else 0.5 * err


def _adamw(w, g, m, v):
    m = ADAM_B1 * m + (1.0 - ADAM_B1) * g
    v = ADAM_B2 * v + (1.0 - ADAM_B2) * _jnp.square(g)
    m_hat = m / (1.0 - ADAM_B1 ** ADAM_STEP)
    v_hat = v / (1.0 - ADAM_B2 ** ADAM_STEP)
    delta = -ADAM_LR * (m_hat / (_jnp.sqrt(v_hat) + ADAM_EPS) + ADAM_WD * w)
    return delta, m, v


def reference(x, meta_tokens, norm_mix_w, w_in, ret_gn_w, w_ret_out, w_ssd_conv, b_ssd_conv, dt_bias_f, dt_bias_b, a_log_f, a_log_b, d_skip, ssd_norm_w, w_ssd_out, w_out, norm_ffn_w, w_ffn_up, w_ffn_conv, b_ffn_conv, w_ffn_down, final_norm_w, loss_target, m_meta_tokens, m_norm_mix_w, m_w_in, m_ret_gn_w, m_w_ret_out, m_w_ssd_conv, m_b_ssd_conv, m_dt_bias_f, m_dt_bias_b, m_a_log_f, m_a_log_b, m_d_skip, m_ssd_norm_w, m_w_ssd_out, m_w_out, m_norm_ffn_w, m_w_ffn_up, m_w_ffn_conv, m_b_ffn_conv, m_w_ffn_down, m_final_norm_w, v_meta_tokens, v_norm_mix_w, v_w_in, v_ret_gn_w, v_w_ret_out, v_w_ssd_conv, v_b_ssd_conv, v_dt_bias_f, v_dt_bias_b, v_a_log_f, v_a_log_b, v_d_skip, v_ssd_norm_w, v_w_ssd_out, v_w_out, v_norm_ffn_w, v_w_ffn_up, v_w_ffn_conv, v_b_ffn_conv, v_w_ffn_down, v_final_norm_w):
    given = dict(x=x, meta_tokens=meta_tokens, norm_mix_w=norm_mix_w, w_in=w_in, ret_gn_w=ret_gn_w, w_ret_out=w_ret_out, w_ssd_conv=w_ssd_conv, b_ssd_conv=b_ssd_conv, dt_bias_f=dt_bias_f, dt_bias_b=dt_bias_b, a_log_f=a_log_f, a_log_b=a_log_b, d_skip=d_skip, ssd_norm_w=ssd_norm_w, w_ssd_out=w_ssd_out, w_out=w_out, norm_ffn_w=norm_ffn_w, w_ffn_up=w_ffn_up, w_ffn_conv=w_ffn_conv, b_ffn_conv=b_ffn_conv, w_ffn_down=w_ffn_down, final_norm_w=final_norm_w, loss_target=loss_target, m_meta_tokens=m_meta_tokens, m_norm_mix_w=m_norm_mix_w, m_w_in=m_w_in, m_ret_gn_w=m_ret_gn_w, m_w_ret_out=m_w_ret_out, m_w_ssd_conv=m_w_ssd_conv, m_b_ssd_conv=m_b_ssd_conv, m_dt_bias_f=m_dt_bias_f, m_dt_bias_b=m_dt_bias_b, m_a_log_f=m_a_log_f, m_a_log_b=m_a_log_b, m_d_skip=m_d_skip, m_ssd_norm_w=m_ssd_norm_w, m_w_ssd_out=m_w_ssd_out, m_w_out=m_w_out, m_norm_ffn_w=m_norm_ffn_w, m_w_ffn_up=m_w_ffn_up, m_w_ffn_conv=m_w_ffn_conv, m_b_ffn_conv=m_b_ffn_conv, m_w_ffn_down=m_w_ffn_down, m_final_norm_w=m_final_norm_w, v_meta_tokens=v_meta_tokens, v_norm_mix_w=v_norm_mix_w, v_w_in=v_w_in, v_ret_gn_w=v_ret_gn_w, v_w_ret_out=v_w_ret_out, v_w_ssd_conv=v_w_ssd_conv, v_b_ssd_conv=v_b_ssd_conv, v_dt_bias_f=v_dt_bias_f, v_dt_bias_b=v_dt_bias_b, v_a_log_f=v_a_log_f, v_a_log_b=v_a_log_b, v_d_skip=v_d_skip, v_ssd_norm_w=v_ssd_norm_w, v_w_ssd_out=v_w_ssd_out, v_w_out=v_w_out, v_norm_ffn_w=v_norm_ffn_w, v_w_ffn_up=v_w_ffn_up, v_w_ffn_conv=v_w_ffn_conv, v_b_ffn_conv=v_b_ffn_conv, v_w_ffn_down=v_w_ffn_down, v_final_norm_w=v_final_norm_w)
    weights = {n: given[n] for n in TWIN_WEIGHTS}
    shared = {n: given[n] for n in SHARED_INPUTS}
    per_example = {n: given[n] for n in ['x']}
    grad_fn = _jax.value_and_grad(_loss, argnums=(0, 1))

    def one_microbatch(ex, loss_target):
        ex = dict(ex)
        diff = ex.pop(TWIN_DIFF_INPUT)
        return grad_fn(weights, diff, {**shared, **ex}, loss_target)

    if N_MICROBATCH == 1:
        loss, (grad_w, grad_x) = one_microbatch(per_example, given["loss_target"])
    else:
        def body(carry, xs):
            loss_sum, grad_sum = carry
            l_k, (gw_k, gx_k) = one_microbatch(xs[0], xs[1])
            with _jax.named_scope("update"):
                return (loss_sum + l_k, _jax.tree.map(_jnp.add, grad_sum, gw_k)), gx_k

        init = (_jnp.zeros((), _jnp.float32), _jax.tree.map(_jnp.zeros_like, weights))
        (loss, grad_w), grad_x = _jax.lax.scan(body, init, (per_example, given["loss_target"]))
    with _jax.named_scope("update"):
        delta_w, new_m, new_v = {}, {}, {}
        for n in TWIN_WEIGHTS:
            delta_w[n], new_m[n], new_v[n] = _adamw(weights[n], grad_w[n], given["m_" + n], given["v_" + n])
    return (loss, grad_x, *[grad_w[n] for n in TWIN_WEIGHTS], *[delta_w[n] for n in TWIN_WEIGHTS],
            *[new_m[n] for n in TWIN_WEIGHTS], *[new_v[n] for n in TWIN_WEIGHTS])
```

```python
import functools
import math

import jax
import jax.numpy as jnp
from jax import lax
from jax.experimental import pallas as pl
from jax.experimental.pallas import tpu as pltpu

F32 = jnp.float32
BF16 = jnp.bfloat16

D_MODEL = 1024
CHUNK = 128
N_META = 16
PAD_ROWS = CHUNK - N_META
RET_HEADS = 4
RET_QK_DIM = 128
RET_V_DIM = 256
SSD_HEADS = 32
SSD_HEAD_DIM = 64
SSD_GROUPS = 4
SSD_STATE = 128
HEADS_PER_GROUP = SSD_HEADS // SSD_GROUPS
PAIRS_PER_GROUP = HEADS_PER_GROUP // 2
D_FF = 2816
EPS = 1e-6
ROPE_BASE = 10000.0
N_DEV = 8

ADAM_LR = 0.001
ADAM_B1 = 0.9
ADAM_B2 = 0.999
ADAM_EPS = 1e-08
ADAM_WD = 0.01
ADAM_STEP = 10

VMEM_LIMIT = 56 * 1024 * 1024
HIGHEST = lax.Precision.HIGHEST

SEGMENTS = (("qk", 0, 1024), ("v", 1024, 2048), ("g", 2048, 3072), ("z", 3072, 5120), ("xs", 5120, 7168),
            ("B", 7168, 7680), ("C", 7680, 8192), ("dt", 8192, 8256), ("gates", 8256, 10304))


def _pick(n, cands):
    for c in cands:
        if n % c == 0:
            return c
    raise ValueError(f"no tile for {n}")


def _params(sem):
    return pltpu.CompilerParams(dimension_semantics=sem, vmem_limit_bytes=VMEM_LIMIT)


def _dot(a, b, dims=(((1,), (0,)), ((), ())), precision=None):
    return lax.dot_general(a, b, dims, preferred_element_type=F32, precision=precision)


def _dot_nt(a, b):
    return _dot(a, b, (((1,), (1,)), ((), ())))


def _dot_tn(a, b):
    return _dot(a, b, (((0,), (0,)), ((), ())))


def _mm(name, a, b, mode, add=None, out_dtype=F32):
    if mode == "nn":
        (M, K), N = a.shape, b.shape[1]
    elif mode == "nt":
        (M, K), N = a.shape, b.shape[0]
    else:
        (K, M), N = a.shape, b.shape[1]
    tm = _pick(M, (384, 256, 1408, 1024, 512, 128, 64, 16))
    tn = _pick(N, (1408, 1024, 512, 128, 64))
    if mode == "tn":
        tk = _pick(K, (1056, 512, 256, 128))
    else:
        tk = K if K <= 2048 else _pick(K, (1408, 1024))
    nk = K // tk
    if mode == "nn":
        a_spec = pl.BlockSpec((tm, tk), lambda n, m, k: (m, k))
        b_spec = pl.BlockSpec((tk, tn), lambda n, m, k: (k, n))
        dims = (((1,), (0,)), ((), ()))
    elif mode == "nt":
        a_spec = pl.BlockSpec((tm, tk), lambda n, m, k: (m, k))
        b_spec = pl.BlockSpec((tn, tk), lambda n, m, k: (n, k))
        dims = (((1,), (1,)), ((), ()))
    else:
        a_spec = pl.BlockSpec((tk, tm), lambda n, m, k: (k, m))
        b_spec = pl.BlockSpec((tk, tn), lambda n, m, k: (k, n))
        dims = (((0,), (0,)), ((), ()))
    o_spec = pl.BlockSpec((tm, tn), lambda n, m, k: (m, n))
    in_specs = [a_spec, b_spec] + ([o_spec] if add is not None else [])
    args = [a, b] + ([add] if add is not None else [])

    def body(*refs):
        if add is not None:
            a_ref, b_ref, r_ref, o_ref, acc = refs
        else:
            a_ref, b_ref, o_ref, acc = refs
        k = pl.program_id(2)
        p = _dot(a_ref[...].astype(BF16), b_ref[...].astype(BF16), dims)

        def finish(r):
            if add is not None:
                r = r + r_ref[...]
            o_ref[...] = r.astype(out_dtype)

        if nk == 1:
            finish(p)
        else:
            @pl.when(k == 0)
            def _():
                acc[...] = p

            @pl.when(k > 0)
            def _():
                acc[...] += p

            @pl.when(k == nk - 1)
            def _():
                finish(acc[...])

    return pl.pallas_call(
        body, name=name, grid=(N // tn, M // tm, nk), in_specs=in_specs, out_specs=o_spec,
        out_shape=jax.ShapeDtypeStruct((M, N), out_dtype),
        scratch_shapes=[pltpu.VMEM((tm, tn) if nk > 1 else (8, 128), F32)],
        compiler_params=_params(("arbitrary", "arbitrary", "arbitrary")),
    )(*args)


def _const(c):
    return lambda j: c


def _rows(name, fn, T, ncol, ins, params, outs, accs=(), halo=False):
    tm = _pick(T, (384, 256, 128))
    R = T // tm
    hb = tm // 8
    in_specs, args = [], []
    for spec in ins:
        arr, w, cf = spec[:3]
        lead = spec[3] if len(spec) > 3 else None
        if lead is None:
            mk = lambda blk, rf, cf=cf: pl.BlockSpec(blk, lambda j, i: (rf(i), cf(j)))
            shape = lambda r, w=w: (r, w)
        else:
            mk = lambda blk, rf, cf=cf, lead=lead: pl.BlockSpec(blk, lambda j, i: (lead, rf(i), cf(j)))
            shape = lambda r, w=w: (None, r, w)
        in_specs.append(mk(shape(tm), lambda i: i))
        args.append(arr)
        if halo:
            in_specs.append(mk(shape(8), lambda i: jnp.maximum(i * hb - 1, 0)))
            in_specs.append(mk(shape(8), lambda i: jnp.minimum((i + 1) * hb, T // 8 - 1)))
            args += [arr, arr]
    for arr, w, cf in params:
        in_specs.append(pl.BlockSpec((arr.shape[0], w), lambda j, i, cf=cf: (0, cf(j))))
        args.append(arr)
    out_shape, out_specs = [], []
    for tw, w, cf, dt in outs:
        out_shape.append(jax.ShapeDtypeStruct((T, tw), dt))
        out_specs.append(pl.BlockSpec((tm, w), lambda j, i, cf=cf: (i, cf(j))))
    for r, tw, w, cf in accs:
        out_shape.append(jax.ShapeDtypeStruct((r, tw), F32))
        out_specs.append(pl.BlockSpec((r, w), lambda j, i, cf=cf: (0, cf(j))))
    n_in, n_par, n_out, n_acc = len(ins), len(params), len(outs), len(accs)

    def body(*refs):
        i = pl.program_id(1)
        vals, p = [], 0
        for _ in range(n_in):
            if halo:
                vals.append(jnp.concatenate([refs[p + 1][...], refs[p][...], refs[p + 2][...]], axis=0))
                p += 3
            else:
                vals.append(refs[p][...])
                p += 1
        pvals = [refs[p + k][...] for k in range(n_par)]
        p += n_par
        res = fn(i, *vals, *pvals)
        for k in range(n_out):
            refs[p + k][...] = res[k].astype(refs[p + k].dtype)
        p += n_out
        for k in range(n_acc):
            ref, v = refs[p + k], res[n_out + k]

            @pl.when(i == 0)
            def _(ref=ref, v=v):
                ref[...] = v

            @pl.when(i > 0)
            def _(ref=ref, v=v):
                ref[...] += v

    res = pl.pallas_call(
        body, name=name, grid=(ncol, R), in_specs=in_specs, out_specs=out_specs, out_shape=out_shape,
        compiler_params=_params(("arbitrary", "arbitrary")),
    )(*args)
    return res


def _tile_rows(T):
    return _pick(T, (384, 256, 128))


def _row_ids(i, T, halo=False):
    tm = _tile_rows(T)
    if halo:
        return i * tm - 8 + lax.broadcasted_iota(jnp.int32, (tm + 16, 1), 0)
    return i * tm + lax.broadcasted_iota(jnp.int32, (tm, 1), 0)


def _rms(x, w):
    return x * lax.rsqrt(jnp.mean(x * x, axis=-1, keepdims=True) + EPS) * w


def _silu(x):
    return x * jax.nn.sigmoid(x)


def _conv3(x, w):
    n = x.shape[0]
    return w[0:1] * pltpu.roll(x, 1, 0) + w[1:2] * x + w[2:3] * pltpu.roll(x, n - 1, 0)


def _conv3_t(d, w):
    n = d.shape[0]
    return w[0:1] * pltpu.roll(d, n - 1, 0) + w[1:2] * d + w[2:3] * pltpu.roll(d, 1, 0)


def _center(x):
    return x[8:x.shape[0] - 8]


def _retention(name, a, b, v, T):
    da = a.shape[1] // RET_HEADS
    dv = v.shape[1] // RET_HEADS
    nc = T // CHUNK
    log_gammas = [math.log(1.0 - 2.0 ** (-5.0 - h)) for h in range(RET_HEADS)]

    def body(a_ref, b_ref, v_ref, o_ref, st):
        h = pl.program_id(0)
        lg = jnp.float32(log_gammas[RET_HEADS - 1])
        for k in range(RET_HEADS - 2, -1, -1):
            lg = jnp.where(h == k, jnp.float32(log_gammas[k]), lg)
        li = lax.broadcasted_iota(jnp.int32, (CHUNK, CHUNK), 0)
        si = lax.broadcasted_iota(jnp.int32, (CHUNK, CHUNK), 1)
        dmat = jnp.exp(lg * jnp.abs(li - si).astype(F32))
        pos = lax.broadcasted_iota(jnp.int32, (CHUNK, 1), 0).astype(F32)
        kdec_f = jnp.exp((CHUNK - 1 - pos) * lg)
        qdec_f = jnp.exp((pos + 1) * lg)
        kdec_b = jnp.exp(pos * lg)
        qdec_b = jnp.exp((CHUNK - pos) * lg)
        cdec = jnp.exp(CHUNK * lg)

        def rows(n):
            return pl.ds(pl.multiple_of(n * CHUNK, CHUNK), CHUNK)

        st[...] = jnp.zeros_like(st)

        def fwd(n, carry):
            r = rows(n)
            av, bv, vv = a_ref[r, :], b_ref[r, :], v_ref[r, :].astype(BF16)
            s = _dot_nt(av.astype(BF16), bv.astype(BF16)) * dmat
            y = _dot(s.astype(BF16), vv) + _dot((av * qdec_f).astype(BF16), st[...].astype(BF16))
            o_ref[r, :] = y
            st[...] = cdec * st[...] + _dot_tn((bv * kdec_f).astype(BF16), vv)
            return carry

        lax.fori_loop(0, nc, fwd, 0)
        st[...] = jnp.zeros_like(st)

        def bwd(m, carry):
            r = rows(nc - 1 - m)
            av, bv, vv = a_ref[r, :], b_ref[r, :], v_ref[r, :].astype(BF16)
            o_ref[r, :] += _dot((av * qdec_b).astype(BF16), st[...].astype(BF16))
            st[...] = cdec * st[...] + _dot_tn((bv * kdec_b).astype(BF16), vv)
            return carry

        lax.fori_loop(0, nc, bwd, 0)

    return pl.pallas_call(
        body, name=name, grid=(RET_HEADS,),
        in_specs=[pl.BlockSpec((T, da), lambda h: (0, h)), pl.BlockSpec((T, da), lambda h: (0, h)),
                  pl.BlockSpec((T, dv), lambda h: (0, h))],
        out_specs=pl.BlockSpec((T, dv), lambda h: (0, h)),
        out_shape=jax.ShapeDtypeStruct((T, RET_HEADS * dv), F32),
        scratch_shapes=[pltpu.VMEM((da, dv), F32)],
        compiler_params=_params(("arbitrary",)),
    )(a, b, v)


def _softplus(x):
    return jnp.maximum(x, 0.0) + jnp.log1p(jnp.exp(-jnp.abs(x)))


def _lane_lo():
    return lax.broadcasted_iota(jnp.int32, (1, CHUNK), 1) < SSD_HEAD_DIM


def _pair_cols(col, j):
    return jnp.where(_lane_lo(), col[:, 2 * j:2 * j + 1], col[:, 2 * j + 1:2 * j + 2])


def _pair_rows(colr, j):
    lo = lax.broadcasted_iota(jnp.int32, (CHUNK, 1), 0) < SSD_HEAD_DIM
    return jnp.where(lo, colr[2 * j:2 * j + 1, :], colr[2 * j + 1:2 * j + 2, :])


def _onehot8(h):
    return (lax.broadcasted_iota(jnp.int32, (1, HEADS_PER_GROUP), 1) == h).astype(F32)


def _ssd_pre(d, c, rawc, rawr, bc, br, alc, alr):
    li = lax.broadcasted_iota(jnp.int32, (CHUNK, CHUNK), 0)
    si = lax.broadcasted_iota(jnp.int32, (CHUNK, CHUNK), 1)
    dif = jnp.where(d == 0, li - si, si - li)
    mask = dif >= 0
    mask_t = dif <= 0
    rowc = c * CHUNK + lax.broadcasted_iota(jnp.int32, (CHUNK, 1), 0)
    rowr = c * CHUNK + lax.broadcasted_iota(jnp.int32, (1, CHUNK), 1)
    dtc = jnp.where(rowc >= PAD_ROWS, _softplus(rawc + bc), 0.0)
    dtr = jnp.where(rowr >= PAD_ROWS, _softplus(rawr + br), 0.0)
    ac = -jnp.exp(alc)
    ar = -jnp.exp(alr)
    dlc = dtc * ac
    dlr = dtr * ar
    alpc = _dot(mask.astype(F32), dlc, precision=HIGHEST)
    alpr = _dot(dlr, mask_t.astype(F32), precision=HIGHEST)
    endc = jnp.sum(dlc, axis=0, keepdims=True)
    endr = jnp.sum(dlr, axis=1, keepdims=True)
    return dict(mask=mask, mask_t=mask_t, dtc=dtc, ac=ac, alpc=alpc, alpr=alpr, endc=endc, endr=endr,
                valid=rowc >= PAD_ROWS)


def _chunk_of(d, n, nc):
    return n + d * (nc - 1 - 2 * n)


def _ssd_small_specs(cfn):
    return [
        pl.BlockSpec((None, None, CHUNK, HEADS_PER_GROUP), lambda d, g, n: (d, g, cfn(d, n), 0)),
        pl.BlockSpec((None, None, HEADS_PER_GROUP, CHUNK), lambda d, g, n: (d, g, 0, cfn(d, n))),
        pl.BlockSpec((None, None, 1, HEADS_PER_GROUP), lambda d, g, n: (d, g, 0, 0)),
        pl.BlockSpec((None, None, HEADS_PER_GROUP, 1), lambda d, g, n: (d, g, 0, 0)),
        pl.BlockSpec((None, None, 1, HEADS_PER_GROUP), lambda d, g, n: (d, g, 0, 0)),
        pl.BlockSpec((None, None, HEADS_PER_GROUP, 1), lambda d, g, n: (d, g, 0, 0)),
    ]


def _ssd_fwd(xs, bm, cm, small, T):
    nc = T // CHUNK
    cfn = lambda d, n: _chunk_of(d, n, nc)

    def body(x_ref, b_ref, c_ref, rawc_ref, rawr_ref, bc_ref, br_ref, alc_ref, alr_ref, y_ref, hs_ref, h_scr):
        d, n = pl.program_id(0), pl.program_id(2)
        c = cfn(d, n)

        @pl.when(n == 0)
        def _():
            h_scr[...] = jnp.zeros_like(h_scr)

        q = _ssd_pre(d, c, rawc_ref[...], rawr_ref[...], bc_ref[...], br_ref[...], alc_ref[...], alr_ref[...])
        bv = b_ref[...].astype(BF16)
        cv = c_ref[...].astype(BF16)
        cb = _dot_nt(cv, bv)
        lo = _lane_lo()
        for j in range(PAIRS_PER_GROUP):
            xp = x_ref[:, j * CHUNK:(j + 1) * CHUNK]
            xd = xp * _pair_cols(q["dtc"], j)
            xdb = xd.astype(BF16)
            yi = []
            for e in range(2):
                h = 2 * j + e
                lm = jnp.exp(jnp.where(q["mask"], q["alpc"][:, h:h + 1] - q["alpr"][h:h + 1, :], -jnp.inf))
                yi.append(_dot((cb * lm).astype(BF16), xdb))
            alp = _pair_cols(q["alpc"], j)
            hp = h_scr[j]
            hs_ref[j] = hp
            yo = jnp.exp(alp) * _dot_nt(cv, hp.astype(BF16))
            y_ref[:, j * CHUNK:(j + 1) * CHUNK] = jnp.where(lo, yi[0], yi[1]) + yo
            de = jnp.exp(_pair_cols(q["endc"], j) - alp)
            h_scr[j] = jnp.exp(_pair_rows(q["endr"], j)) * hp + _dot_tn((xd * de).astype(BF16), bv)

    gw = HEADS_PER_GROUP * SSD_HEAD_DIM
    return pl.pallas_call(
        body, name="ssd_fwd", grid=(2, SSD_GROUPS, nc),
        in_specs=[pl.BlockSpec((CHUNK, gw), lambda d, g, n: (cfn(d, n), g)),
                  pl.BlockSpec((CHUNK, SSD_STATE), lambda d, g, n: (cfn(d, n), g)),
                  pl.BlockSpec((CHUNK, SSD_STATE), lambda d, g, n: (cfn(d, n), g))] + _ssd_small_specs(cfn),
        out_specs=[pl.BlockSpec((None, CHUNK, gw), lambda d, g, n: (d, cfn(d, n), g)),
                   pl.BlockSpec((None, None, None, PAIRS_PER_GROUP, CHUNK, SSD_STATE),
                                lambda d, g, n: (d, g, cfn(d, n), 0, 0, 0))],
        out_shape=[jax.ShapeDtypeStruct((2, T, SSD_HEADS * SSD_HEAD_DIM), F32),
                   jax.ShapeDtypeStruct((2, SSD_GROUPS, nc, PAIRS_PER_GROUP, CHUNK, SSD_STATE), F32)],
        scratch_shapes=[pltpu.VMEM((PAIRS_PER_GROUP, CHUNK, SSD_STATE), F32)],
        compiler_params=_params(("arbitrary", "arbitrary", "arbitrary")),
    )(xs, bm, cm, *small)


def _ssd_bwd(xs, bm, cm, small, hs, dy, T):
    nc = T // CHUNK
    cfn = lambda d, n: _chunk_of(1 - d, n, nc)

    def body(x_ref, b_ref, c_ref, rawc_ref, rawr_ref, bc_ref, br_ref, alc_ref, alr_ref, hs_ref, dy_ref,
             dx_ref, db_ref, dc_ref, draw_ref, dbias_ref, dalog_ref, dh_scr):
        d, n = pl.program_id(0), pl.program_id(2)
        c = cfn(d, n)

        @pl.when(n == 0)
        def _():
            dh_scr[...] = jnp.zeros_like(dh_scr)

        rawc, bc = rawc_ref[...], bc_ref[...]
        q = _ssd_pre(d, c, rawc, rawr_ref[...], bc, br_ref[...], alc_ref[...], alr_ref[...])
        b32, c32 = b_ref[...], c_ref[...]
        bv, cv = b32.astype(BF16), c32.astype(BF16)
        cb = _dot_nt(cv, bv)
        cbt = _dot_nt(bv, cv)
        lo = _lane_lo()
        row_lo = lax.broadcasted_iota(jnp.int32, (CHUNK, 1), 0) < SSD_HEAD_DIM
        dcb = jnp.zeros((CHUNK, CHUNK), F32)
        dcp = jnp.zeros((CHUNK, SSD_STATE), F32)
        dbp = jnp.zeros((CHUNK, SSD_STATE), F32)
        dalp = jnp.zeros((CHUNK, HEADS_PER_GROUP), F32)
        dend = jnp.zeros((1, HEADS_PER_GROUP), F32)
        ddtx = jnp.zeros((CHUNK, HEADS_PER_GROUP), F32)

        def half_sums(t):
            return (jnp.sum(jnp.where(lo, t, 0.0), axis=1, keepdims=True),
                    jnp.sum(jnp.where(lo, 0.0, t), axis=1, keepdims=True))

        for j in range(PAIRS_PER_GROUP):
            xp = x_ref[:, j * CHUNK:(j + 1) * CHUNK]
            dtp = _pair_cols(q["dtc"], j)
            xd = xp * dtp
            xdb = xd.astype(BF16)
            dyp = dy_ref[:, j * CHUNK:(j + 1) * CHUNK]
            dyb = dyp.astype(BF16)
            hn = hs_ref[j]
            hnb = hn.astype(BF16)
            dh1 = dh_scr[j]
            dh1b = dh1.astype(BF16)
            alp = _pair_cols(q["alpc"], j)
            ea = jnp.exp(alp)
            de = jnp.exp(_pair_cols(q["endc"], j) - alp)
            dxi = []
            for e in range(2):
                h = 2 * j + e
                ac_, ar_ = q["alpc"][:, h:h + 1], q["alpr"][h:h + 1, :]
                lm = jnp.exp(jnp.where(q["mask"], ac_ - ar_, -jnp.inf))
                mt = cbt * jnp.exp(jnp.where(q["mask_t"], ar_ - ac_, -jnp.inf))
                dxi.append(_dot(mt.astype(BF16), dyb))
                dyeb_h = (jnp.where(lo, dyp, 0.0) if e == 0 else jnp.where(lo, 0.0, dyp)).astype(BF16)
                gl = _dot_nt(dyeb_h, xdb) * lm
                dcb = dcb + gl
                ra = jnp.sum(gl * cb, axis=1, keepdims=True) - jnp.sum(_dot_nt(xdb, dyeb_h) * mt, axis=1, keepdims=True)
                dalp = dalp + ra * _onehot8(h)
            y_off = ea * _dot_nt(cv, hnb)
            dxs_state = de * _dot_nt(bv, dh1b)
            dxd = jnp.where(lo, dxi[0], dxi[1]) + dxs_state
            dyeb = (dyp * ea).astype(BF16)
            dcp = dcp + _dot(dyeb, hnb)
            dbp = dbp + _dot((xd * de).astype(BF16), dh1b)
            dh_scr[j] = jnp.exp(_pair_rows(q["endr"], j)) * dh1 + _dot_tn(dyeb, cv)
            r0, r1 = half_sums(dyp * y_off - xd * dxs_state)
            dalp = dalp + r0 * _onehot8(2 * j) + r1 * _onehot8(2 * j + 1)
            t0, t1 = half_sums(xd * dxs_state)
            u = jnp.sum(dh1 * hn, axis=1, keepdims=True)
            u0 = jnp.sum(jnp.where(row_lo, u, 0.0), axis=0, keepdims=True)
            u1 = jnp.sum(jnp.where(row_lo, 0.0, u), axis=0, keepdims=True)
            eend = jnp.exp(q["endc"])
            dend = dend + (jnp.sum(t0, axis=0, keepdims=True) + eend * u0) * _onehot8(2 * j) \
                        + (jnp.sum(t1, axis=0, keepdims=True) + eend * u1) * _onehot8(2 * j + 1)
            dx_ref[:, j * CHUNK:(j + 1) * CHUNK] = dxd * dtp
            w0, w1 = half_sums(dxd * xp)
            ddtx = ddtx + w0 * _onehot8(2 * j) + w1 * _onehot8(2 * j + 1)

        dcbb = dcb.astype(BF16)
        dc_ref[...] = dcp + _dot(dcbb, bv)
        db_ref[...] = dbp + _dot_tn(dcbb, cv)
        ddl = _dot(q["mask_t"].astype(F32), dalp, precision=HIGHEST) + dend
        ddt = ddl * q["ac"] + ddtx
        draw = jnp.where(q["valid"], ddt * jax.nn.sigmoid(rawc + bc), 0.0)
        draw_ref[...] = draw
        dbias = jnp.sum(draw, axis=0, keepdims=True)
        dalog = jnp.sum(ddl * q["dtc"], axis=0, keepdims=True) * q["ac"]

        @pl.when(n == 0)
        def _():
            dbias_ref[...] = dbias
            dalog_ref[...] = dalog

        @pl.when(n > 0)
        def _():
            dbias_ref[...] += dbias
            dalog_ref[...] += dalog

    gw = HEADS_PER_GROUP * SSD_HEAD_DIM
    acc_spec = pl.BlockSpec((None, None, 1, HEADS_PER_GROUP), lambda d, g, n: (d, g, 0, 0))
    return pl.pallas_call(
        body, name="ssd_bwd", grid=(2, SSD_GROUPS, nc),
        in_specs=[pl.BlockSpec((CHUNK, gw), lambda d, g, n: (cfn(d, n), g)),
                  pl.BlockSpec((CHUNK, SSD_STATE), lambda d, g, n: (cfn(d, n), g)),
                  pl.BlockSpec((CHUNK, SSD_STATE), lambda d, g, n: (cfn(d, n), g))] + _ssd_small_specs(cfn) + [
                  pl.BlockSpec((None, None, None, PAIRS_PER_GROUP, CHUNK, SSD_STATE),
                               lambda d, g, n: (d, g, cfn(d, n), 0, 0, 0)),
                  pl.BlockSpec((CHUNK, gw), lambda d, g, n: (cfn(d, n), g))],
        out_specs=[pl.BlockSpec((None, CHUNK, gw), lambda d, g, n: (d, cfn(d, n), g)),
                   pl.BlockSpec((None, CHUNK, SSD_STATE), lambda d, g, n: (d, cfn(d, n), g)),
                   pl.BlockSpec((None, CHUNK, SSD_STATE), lambda d, g, n: (d, cfn(d, n), g)),
                   pl.BlockSpec((None, None, CHUNK, HEADS_PER_GROUP), lambda d, g, n: (d, g, cfn(d, n), 0)),
                   acc_spec, acc_spec],
        out_shape=[jax.ShapeDtypeStruct((2, T, SSD_HEADS * SSD_HEAD_DIM), F32),
                   jax.ShapeDtypeStruct((2, T, SSD_GROUPS * SSD_STATE), F32),
                   jax.ShapeDtypeStruct((2, T, SSD_GROUPS * SSD_STATE), F32),
                   jax.ShapeDtypeStruct((2, SSD_GROUPS, T, HEADS_PER_GROUP), F32),
                   jax.ShapeDtypeStruct((2, SSD_GROUPS, 1, HEADS_PER_GROUP), F32),
                   jax.ShapeDtypeStruct((2, SSD_GROUPS, 1, HEADS_PER_GROUP), F32)],
        scratch_shapes=[pltpu.VMEM((PAIRS_PER_GROUP, CHUNK, SSD_STATE), F32)],
        compiler_params=_params(("arbitrary", "arbitrary", "arbitrary")),
    )(xs, bm, cm, *small, hs, dy)


def _rot(x, cs, sn):
    return x * cs + pltpu.roll(x, RET_QK_DIM // 2, 1) * sn


def _rot_t(d, cs, sn):
    return d * cs + pltpu.roll(d * sn, RET_QK_DIM // 2, 1)


def _ret_post(y, g, w):
    parts = []
    for h in range(RET_HEADS):
        yh = y[:, h * RET_V_DIM:(h + 1) * RET_V_DIM]
        mu = jnp.mean(yh, axis=-1, keepdims=True)
        var = jnp.mean(jnp.square(yh - mu), axis=-1, keepdims=True)
        parts.append((yh - mu) * lax.rsqrt(var + EPS))
    return _silu(g) * (jnp.concatenate(parts, axis=1) * w)


def _ssd_post(yf, yb, xs, z, dskip, w):
    y = (yf + yb + xs * dskip) * _silu(z)
    return y * lax.rsqrt(jnp.mean(y * y, axis=-1, keepdims=True) + EPS) * w


def _merge(gates, yr, ys, valid):
    m = jax.nn.sigmoid(gates[:, :D_MODEL]) * yr + jax.nn.sigmoid(gates[:, D_MODEL:]) * ys
    return jnp.where(valid, m, 0.0)


def _rope_tables(T):
    half = RET_QK_DIM // 2
    inv = ROPE_BASE ** (-jnp.arange(half, dtype=F32) / half)
    pos = (jnp.arange(T) - PAD_ROWS).astype(F32)
    ang = pos[:, None] * inv[None, :]
    cos, sin = jnp.cos(ang), jnp.sin(ang)
    return jnp.concatenate([cos, cos], axis=1), jnp.concatenate([-sin, sin], axis=1)


def _per_group(v):
    c = v.reshape(SSD_GROUPS, 1, HEADS_PER_GROUP)
    return c, c.reshape(SSD_GROUPS, HEADS_PER_GROUP, 1)


def _local_step(x, target, w):
    S = x.shape[0]
    T = S + CHUNK
    tm = _tile_rows(T)
    c0 = _const(0)

    h0 = jnp.concatenate([jnp.zeros((PAD_ROWS, D_MODEL), F32), w["meta_tokens"], x], axis=0)
    tgt = jnp.concatenate([jnp.zeros((CHUNK, D_MODEL), F32), target], axis=0)
    w_in = {name: w["w_in_t"][a:b] for name, a, b in SEGMENTS}
    w_in["dt"] = jnp.pad(w_in["dt"], ((0, CHUNK - 2 * SSD_HEADS), (0, 0)))
    w_up_g, w_up_u = w["w_ffn_up_t"][:D_FF], w["w_ffn_up_t"][D_FF:]

    def norm_cast(name, h, nw):
        return _rows(name, lambda i, hv, wv: (_rms(hv, wv),), T, 1, [(h, D_MODEL, c0)], [(nw, D_MODEL, c0)],
                     [(D_MODEL, D_MODEL, c0, BF16)])[0]

    u = norm_cast("norm_mix", h0, w["norm_mix_w"])
    proj = {name: _mm("proj_" + name, u, w_in[name], "nt") for name, _, _ in SEGMENTS}

    cs, sn = _rope_tables(T)
    scale = RET_QK_DIM ** -0.5

    def rot_fn(i, qk, csv, snv):
        q = [_rot(qk[:, h * 128:(h + 1) * 128], csv, snv) for h in range(RET_HEADS)]
        k = [_rot(qk[:, (RET_HEADS + h) * 128:(RET_HEADS + h + 1) * 128], csv, snv) * scale for h in range(RET_HEADS)]
        return jnp.concatenate(q, axis=1), jnp.concatenate(k, axis=1)

    qr, kr = _rows("rotary", rot_fn, T, 1, [(proj["qk"], 1024, c0), (cs, 128, c0), (sn, 128, c0)], [],
                   [(512, 512, c0, F32), (512, 512, c0, F32)])
    y_ret = _retention("retention", qr, kr, proj["v"], T)
    a_ret = _rows("ret_post", lambda i, y, g, gw: (_ret_post(y, g, gw),), T, 1,
                  [(y_ret, 1024, c0), (proj["g"], 1024, c0)], [(w["ret_gn_w"], 1024, c0)],
                  [(1024, 1024, c0, BF16)])[0]
    y_ret_o = _mm("ret_out", a_ret, w["w_ret_out"], "nn")

    conv_w = {"xs": w["w_ssd_conv"][:, :2048], "B": w["w_ssd_conv"][:, 2048:2560], "C": w["w_ssd_conv"][:, 2560:]}
    conv_b = {"xs": w["b_ssd_conv"][:, :2048], "B": w["b_ssd_conv"][:, 2048:2560], "C": w["b_ssd_conv"][:, 2560:]}

    def ssd_conv_fn(i, xe, cw, cb):
        r = _row_ids(i, T, True)
        xe = jnp.where((r >= 0) & (r < T), xe, 0.0)
        return (_center(jnp.where(r >= PAD_ROWS, _silu(_conv3(xe, cw) + cb), 0.0)),)

    act = {}
    for name in ("xs", "B", "C"):
        wd = proj[name].shape[1]
        cw = 512
        act[name] = _rows("ssd_conv_" + name, ssd_conv_fn, T, wd // cw, [(proj[name], cw, lambda j: j)],
                          [(conv_w[name], cw, lambda j: j), (conv_b[name], cw, lambda j: j)],
                          [(wd, cw, lambda j: j, F32)], halo=True)[0]

    raw = proj["dt"][:, :2 * SSD_HEADS].reshape(T, 2, SSD_GROUPS, HEADS_PER_GROUP)
    rawc = raw.transpose(1, 2, 0, 3)
    rawr = raw.transpose(1, 2, 3, 0)
    bias = [_per_group(w["dt_bias_f"]), _per_group(w["dt_bias_b"])]
    alog = [_per_group(w["a_log_f"]), _per_group(w["a_log_b"])]
    small = (rawc, rawr, jnp.stack([bias[0][0], bias[1][0]]), jnp.stack([bias[0][1], bias[1][1]]),
             jnp.stack([alog[0][0], alog[1][0]]), jnp.stack([alog[0][1], alog[1][1]]))
    y_dir, states = _ssd_fwd(act["xs"], act["B"], act["C"], small, T)

    dskip_e = jnp.repeat(w["d_skip"], SSD_HEAD_DIM, axis=1)
    gcol = lambda j: j
    gw_ = 512
    a_ssd = _rows("ssd_post", lambda i, yf, yb, xv, zv, dk, nw: (_ssd_post(yf, yb, xv, zv, dk, nw),), T, SSD_GROUPS,
                  [(y_dir, gw_, gcol, 0), (y_dir, gw_, gcol, 1), (act["xs"], gw_, gcol), (proj["z"], gw_, gcol)],
                  [(dskip_e, gw_, gcol), (w["ssd_norm_w"], gw_, gcol)], [(2048, gw_, gcol, BF16)])[0]
    y_ssd_o = _mm("ssd_out", a_ssd, w["w_ssd_out"], "nn")

    def merge_fn(i, gates, yr, ys):
        return (_merge(gates, yr, ys, _row_ids(i, T) >= PAD_ROWS),)

    merged = _rows("merge", merge_fn, T, 1, [(proj["gates"], 2048, c0), (y_ret_o, 1024, c0), (y_ssd_o, 1024, c0)], [],
                   [(1024, 1024, c0, BF16)])[0]
    h1 = _mm("mix_out", merged, w["w_out"], "nn", add=h0)

    n2 = norm_cast("norm_ffn", h1, w["norm_ffn_w"])
    fg_pre = _mm("ffn_up_g", n2, w_up_g, "nt")
    fu_pre = _mm("ffn_up_u", n2, w_up_u, "nt")
    cwg, cwu = w["w_ffn_conv"][:, :D_FF], w["w_ffn_conv"][:, D_FF:]
    cbg, cbu = w["b_ffn_conv"][:, :D_FF], w["b_ffn_conv"][:, D_FF:]
    fcol = lambda j: j
    fw = 1408

    def ffn_act_fn(i, ge, ue, wg, wu, bg, bu):
        return (_center(_silu(_conv3(ge, wg) + bg) * (_conv3(ue, wu) + bu)),)

    def ext_valid(i):
        r = _row_ids(i, T, True)
        return (r >= 0) & (r < T)

    def ffn_act_masked(i, ge, ue, wg, wu, bg, bu):
        v = ext_valid(i)
        return ffn_act_fn(i, jnp.where(v, ge, 0.0), jnp.where(v, ue, 0.0), wg, wu, bg, bu)

    a2 = _rows("ffn_act", ffn_act_masked, T, D_FF // fw, [(fg_pre, fw, fcol), (fu_pre, fw, fcol)],
               [(cwg, fw, fcol), (cwu, fw, fcol), (cbg, fw, fcol), (cbu, fw, fcol)], [(D_FF, fw, fcol, BF16)],
               halo=True)[0]
    h2 = _mm("ffn_down", a2, w["w_ffn_down"], "nn", add=h1)

    fnw = w["final_norm_w"].reshape(1, D_MODEL)

    def loss_fn(i, hv, tv, nw):
        valid = _row_ids(i, T) >= CHUNK
        y, vjp = jax.vjp(_rms, hv, nw)
        diff = jnp.where(valid, y - tv, 0.0)
        dh, dw = vjp(diff * (1.0 / D_MODEL))
        part = 0.5 / D_MODEL * jnp.sum(jnp.sum(diff * diff, axis=1, keepdims=True), axis=0, keepdims=True)
        return dh, jnp.broadcast_to(part, (1, 128)), dw

    dh2, loss_acc, d_fnw = _rows("loss", loss_fn, T, 1, [(h2, D_MODEL, c0), (tgt, D_MODEL, c0)], [(fnw, D_MODEL, c0)],
                                 [(D_MODEL, D_MODEL, c0, F32)], [(1, 128, 128, c0), (1, D_MODEL, D_MODEL, c0)])
    loss = loss_acc[0, 0]
    grads = {"final_norm_w": d_fnw.reshape(D_MODEL)}

    da2 = _mm("d_ffn_act", dh2, w["w_ffn_down"], "nt")
    grads["w_ffn_down"] = _mm("g_ffn_down", a2, dh2, "tn")

    def ffn_bwd_fn(i, ge, ue, de, wg, wu, bg, bu):
        v = ext_valid(i)
        ge, ue, de = jnp.where(v, ge, 0.0), jnp.where(v, ue, 0.0), jnp.where(v, de, 0.0)
        fg = _conv3(ge, wg) + bg
        fu = _conv3(ue, wu) + bu
        sg = jax.nn.sigmoid(fg)
        dfg = de * fu * (sg * (1.0 + fg * (1.0 - sg)))
        dfu = de * (fg * sg)
        n = ge.shape[0]

        def wgrad(df, xe):
            df_c = _center(df)
            return jnp.concatenate([jnp.sum(df_c * _center(pltpu.roll(xe, 1, 0)), axis=0, keepdims=True),
                                    jnp.sum(df_c * _center(xe), axis=0, keepdims=True),
                                    jnp.sum(df_c * _center(pltpu.roll(xe, n - 1, 0)), axis=0, keepdims=True)], axis=0)

        return (_center(_conv3_t(dfg, wg)), _center(_conv3_t(dfu, wu)), wgrad(dfg, ge), wgrad(dfu, ue),
                jnp.sum(_center(dfg), axis=0, keepdims=True), jnp.sum(_center(dfu), axis=0, keepdims=True))

    dfg_pre, dfu_pre, g_cwg, g_cwu, g_cbg, g_cbu = _rows(
        "ffn_act_bwd", ffn_bwd_fn, T, D_FF // fw, [(fg_pre, fw, fcol), (fu_pre, fw, fcol), (da2, fw, fcol)],
        [(cwg, fw, fcol), (cwu, fw, fcol), (cbg, fw, fcol), (cbu, fw, fcol)],
        [(D_FF, fw, fcol, BF16), (D_FF, fw, fcol, BF16)],
        [(3, D_FF, fw, fcol), (3, D_FF, fw, fcol), (1, D_FF, fw, fcol), (1, D_FF, fw, fcol)], halo=True)
    grads["w_ffn_conv"] = jnp.concatenate([g_cwg, g_cwu], axis=1)
    grads["b_ffn_conv"] = jnp.concatenate([g_cbg, g_cbu], axis=1)
    dn2 = _mm("d_norm_ffn_g", dfg_pre, w_up_g, "nn")
    dn2 = _mm("d_norm_ffn_u", dfu_pre, w_up_u, "nn", add=dn2)
    grads["w_ffn_up_t"] = jnp.concatenate([_mm("g_ffn_up_g", dfg_pre, n2, "tn"), _mm("g_ffn_up_u", dfu_pre, n2, "tn")],
                                          axis=0)

    def norm_bwd(name, h, nw, dn, dres):
        def fn(i, hv, dnv, drv, wv):
            _, vjp = jax.vjp(_rms, hv, wv)
            dh, dw = vjp(dnv)
            return dh + drv, dw
        return _rows(name, fn, T, 1, [(h, D_MODEL, c0), (dn, D_MODEL, c0), (dres, D_MODEL, c0)], [(nw, D_MODEL, c0)],
                     [(D_MODEL, D_MODEL, c0, F32)], [(1, D_MODEL, D_MODEL, c0)])

    dh1, grads["norm_ffn_w"] = norm_bwd("norm_ffn_bwd", h1, w["norm_ffn_w"], dn2, dh2)

    dmerged = _mm("d_merged", dh1, w["w_out"], "nt")
    grads["w_out"] = _mm("g_out", merged, dh1, "tn")

    def merge_bwd_fn(i, gates, yr, ys, dm):
        valid = _row_ids(i, T) >= PAD_ROWS
        _, vjp = jax.vjp(lambda a, b, c: _merge(a, b, c, valid), gates, yr, ys)
        return vjp(dm)

    dgates, dyr, dys = _rows("merge_bwd", merge_bwd_fn, T, 1,
                             [(proj["gates"], 2048, c0), (y_ret_o, 1024, c0), (y_ssd_o, 1024, c0), (dmerged, 1024, c0)],
                             [], [(2048, 2048, c0, BF16), (1024, 1024, c0, BF16), (1024, 1024, c0, BF16)])
    dproj = {"gates": dgates}

    da_ssd = _mm("d_ssd_act", dys, w["w_ssd_out"], "nt")
    grads["w_ssd_out"] = _mm("g_ssd_out", a_ssd, dys, "tn")

    def ssd_post_bwd_fn(i, yf, yb, xv, zv, da, dk, nw):
        _, vjp = jax.vjp(_ssd_post, yf, yb, xv, zv, dk, nw)
        dyf, _, dxv, dzv, ddk, dnw = vjp(da)
        return dyf, dxv, dzv, ddk, dnw

    dy_ssd, dxs_skip, dproj["z"], g_dskip_e, grads["ssd_norm_w"] = _rows(
        "ssd_post_bwd", ssd_post_bwd_fn, T, SSD_GROUPS,
        [(y_dir, gw_, gcol, 0), (y_dir, gw_, gcol, 1), (act["xs"], gw_, gcol), (proj["z"], gw_, gcol),
         (da_ssd, gw_, gcol)],
        [(dskip_e, gw_, gcol), (w["ssd_norm_w"], gw_, gcol)],
        [(2048, gw_, gcol, F32), (2048, gw_, gcol, F32), (2048, gw_, gcol, BF16)],
        [(1, 2048, gw_, gcol), (1, 2048, gw_, gcol)])
    grads["d_skip"] = g_dskip_e.reshape(SSD_HEADS, SSD_HEAD_DIM).sum(axis=1).reshape(1, SSD_HEADS)

    dxs_dir, db_dir, dc_dir, draw, g_bias, g_alog = _ssd_bwd(act["xs"], act["B"], act["C"], small, states, dy_ssd, T)
    grads["dt_bias_f"], grads["dt_bias_b"] = g_bias[0].reshape(1, SSD_HEADS), g_bias[1].reshape(1, SSD_HEADS)
    grads["a_log_f"], grads["a_log_b"] = g_alog[0].reshape(1, SSD_HEADS), g_alog[1].reshape(1, SSD_HEADS)
    d_dt = draw.transpose(2, 0, 1, 3).reshape(T, 2 * SSD_HEADS)
    dproj["dt"] = jnp.pad(d_dt, ((0, 0), (0, CHUNK - 2 * SSD_HEADS))).astype(BF16)

    def make_conv_bwd(nsum):
        def fn(i, xe, *rest):
            ds, (cw, cb) = rest[:nsum], rest[nsum:]
            r = _row_ids(i, T, True)
            dact = ds[0]
            for t in ds[1:]:
                dact = dact + t
            dact = jnp.where((r >= PAD_ROWS) & (r < T), dact, 0.0)
            xe = jnp.where((r >= 0) & (r < T), xe, 0.0)
            pre = _conv3(xe, cw) + cb
            sg = jax.nn.sigmoid(pre)
            dpre = dact * (sg * (1.0 + pre * (1.0 - sg)))
            n = xe.shape[0]
            dpc = _center(dpre)
            dw = jnp.concatenate([jnp.sum(dpc * _center(pltpu.roll(xe, 1, 0)), axis=0, keepdims=True),
                                  jnp.sum(dpc * _center(xe), axis=0, keepdims=True),
                                  jnp.sum(dpc * _center(pltpu.roll(xe, n - 1, 0)), axis=0, keepdims=True)], axis=0)
            return _center(_conv3_t(dpre, cw)), dw, jnp.sum(dpc, axis=0, keepdims=True)
        return fn

    g_cw, g_cb = {}, {}
    cots = {"xs": [(dxs_dir, 512, gcol, 0), (dxs_dir, 512, gcol, 1), (dxs_skip, 512, gcol)],
            "B": [(db_dir, 512, gcol, 0), (db_dir, 512, gcol, 1)],
            "C": [(dc_dir, 512, gcol, 0), (dc_dir, 512, gcol, 1)]}
    for name in ("xs", "B", "C"):
        wd = proj[name].shape[1]
        dproj[name], g_cw[name], g_cb[name] = _rows(
            "ssd_conv_bwd_" + name, make_conv_bwd(len(cots[name])), T, wd // 512,
            [(proj[name], 512, gcol)] + cots[name], [(conv_w[name], 512, gcol), (conv_b[name], 512, gcol)],
            [(wd, 512, gcol, BF16)], [(3, wd, 512, gcol), (1, wd, 512, gcol)], halo=True)
    grads["w_ssd_conv"] = jnp.concatenate([g_cw["xs"], g_cw["B"], g_cw["C"]], axis=1)
    grads["b_ssd_conv"] = jnp.concatenate([g_cb["xs"], g_cb["B"], g_cb["C"]], axis=1)

    da_ret = _mm("d_ret_act", dyr, w["w_ret_out"], "nt")
    grads["w_ret_out"] = _mm("g_ret_out", a_ret, dyr, "tn")

    def ret_post_bwd_fn(i, y, g, da, gw):
        _, vjp = jax.vjp(_ret_post, y, g, gw)
        return vjp(da)

    dy_ret, dproj["g"], grads["ret_gn_w"] = _rows(
        "ret_post_bwd", ret_post_bwd_fn, T, 1, [(y_ret, 1024, c0), (proj["g"], 1024, c0), (da_ret, 1024, c0)],
        [(w["ret_gn_w"], 1024, c0)], [(1024, 1024, c0, F32), (1024, 1024, c0, BF16)], [(1, 1024, 1024, c0)])
    dproj["v"] = _retention("retention_dv", kr, qr, dy_ret, T)
    dqr = _retention("retention_dq", dy_ret, proj["v"], kr, T)
    dkr = _retention("retention_dk", proj["v"], dy_ret, qr, T)

    def rot_bwd_fn(i, dq, dk, csv, snv):
        parts = [_rot_t(dq[:, h * 128:(h + 1) * 128], csv, snv) for h in range(RET_HEADS)]
        parts += [_rot_t(dk[:, h * 128:(h + 1) * 128] * scale, csv, snv) for h in range(RET_HEADS)]
        return (jnp.concatenate(parts, axis=1),)

    dproj["qk"] = _rows("rotary_bwd", rot_bwd_fn, T, 1, [(dqr, 512, c0), (dkr, 512, c0), (cs, 128, c0), (sn, 128, c0)],
                        [], [(1024, 1024, c0, BF16)])[0]

    du = None
    g_in = []
    for name, _, _ in SEGMENTS:
        du = _mm("d_u_" + name, dproj[name], w_in[name], "nn", add=du)
        g_in.append(_mm("g_in_" + name, dproj[name], u, "tn"))
    g_in[7] = g_in[7][:2 * SSD_HEADS]
    grads["w_in_t"] = jnp.concatenate(g_in, axis=0)
    dh0, grads["norm_mix_w"] = norm_bwd("norm_mix_bwd", h0, w["norm_mix_w"], du, dh1)
    grads["meta_tokens"] = dh0[PAD_ROWS:CHUNK]
    return loss, dh0[CHUNK:], grads


MESH_ID = pl.DeviceIdType.MESH
ANY = pl.BlockSpec(memory_space=pl.ANY)


def _me_and_peers():
    x, y, c = lax.axis_index("x"), lax.axis_index("y"), lax.axis_index("c")
    peers = []
    for k in range(1, N_DEV):
        px = 1 - x if k & 4 else x
        py = 1 - y if k & 2 else y
        pc = 1 - c if k & 1 else c
        peers.append(((px, py, pc), 4 * px + 2 * py + pc))
    return 4 * x + 2 * y + c, peers


def _push_blocks(name, src, per_peer):
    blk = src.shape[1:] if per_peer else src.shape

    def body(src_ref, out_ref, send_sems, recv_sems, local_sem):
        me, peers = _me_and_peers()
        mine = src_ref.at[me] if per_peer else src_ref
        local = pltpu.make_async_copy(mine, out_ref.at[me], local_sem)
        local.start()
        sends = []
        for k, (dev, idx) in enumerate(peers):
            cp = pltpu.make_async_remote_copy(
                src_ref=src_ref.at[idx] if per_peer else src_ref, dst_ref=out_ref.at[me],
                send_sem=send_sems.at[k], recv_sem=recv_sems.at[k], device_id=dev, device_id_type=MESH_ID)
            cp.start()
            sends.append(cp)
        for k, (dev, idx) in enumerate(peers):
            pltpu.make_async_remote_copy(
                src_ref=mine, dst_ref=out_ref.at[idx], send_sem=send_sems.at[k], recv_sem=recv_sems.at[k],
                device_id=dev, device_id_type=MESH_ID).wait_recv()
        for cp in sends:
            cp.wait_send()
        local.wait()

    return pl.pallas_call(
        body, name=name, in_specs=[ANY], out_specs=ANY,
        out_shape=jax.ShapeDtypeStruct((N_DEV,) + tuple(blk), src.dtype),
        scratch_shapes=[pltpu.SemaphoreType.DMA((N_DEV - 1,)), pltpu.SemaphoreType.DMA((N_DEV - 1,)),
                        pltpu.SemaphoreType.DMA],
    )(src)


def _sum_blocks(name, blocks):
    _, R, C = blocks.shape
    tc = _pick(C, (128,))

    def body(b_ref, o_ref):
        acc = b_ref[0].astype(F32)
        for k in range(1, N_DEV):
            acc = acc + b_ref[k].astype(F32)
        o_ref[...] = acc

    return pl.pallas_call(
        body, name=name, grid=(C // tc,), in_specs=[pl.BlockSpec((N_DEV, R, tc), lambda j: (0, 0, j))],
        out_specs=pl.BlockSpec((R, tc), lambda j: (0, j)), out_shape=jax.ShapeDtypeStruct((R, C), F32),
        compiler_params=_params(("arbitrary",)),
    )(blocks)


def _adamw(name, w, g, m, v):
    R, C = w.shape
    tr = _pick(R, (168, 8))
    spec = pl.BlockSpec((tr, C), lambda i: (i, 0))

    def body(w_ref, g_ref, m_ref, v_ref, d_ref, mo_ref, vo_ref):
        gv = g_ref[...]
        mn = ADAM_B1 * m_ref[...] + (1.0 - ADAM_B1) * gv
        vn = ADAM_B2 * v_ref[...] + (1.0 - ADAM_B2) * jnp.square(gv)
        m_hat = mn / (1.0 - ADAM_B1 ** ADAM_STEP)
        v_hat = vn / (1.0 - ADAM_B2 ** ADAM_STEP)
        d_ref[...] = -ADAM_LR * (m_hat / (jnp.sqrt(v_hat) + ADAM_EPS) + ADAM_WD * w_ref[...])
        mo_ref[...] = mn
        vo_ref[...] = vn

    return pl.pallas_call(
        body, name=name, grid=(R // tr,), in_specs=[spec] * 4, out_specs=[spec] * 3,
        out_shape=[jax.ShapeDtypeStruct((R, C), F32)] * 3, compiler_params=_params(("arbitrary",)),
    )(w, g, m, v)


WEIGHTS = ("meta_tokens", "norm_mix_w", "w_in", "ret_gn_w", "w_ret_out", "w_ssd_conv", "b_ssd_conv", "dt_bias_f",
           "dt_bias_b", "a_log_f", "a_log_b", "d_skip", "ssd_norm_w", "w_ssd_out", "w_out", "norm_ffn_w", "w_ffn_up",
           "w_ffn_conv", "b_ffn_conv", "w_ffn_down", "final_norm_w")
BIG = (("w_in", 1288, True), ("w_ffn_up", 704, True), ("w_ret_out", 128, False), ("w_ssd_out", 256, False),
       ("w_out", 128, False), ("w_ffn_down", 352, False))
REPLICATED = ("norm_mix_w", "ret_gn_w", "b_ssd_conv", "dt_bias_f", "dt_bias_b", "a_log_f", "a_log_b", "d_skip",
              "ssd_norm_w", "norm_ffn_w", "b_ffn_conv", "final_norm_w")
SMALL_SHARDED = (("meta_tokens", 16, 1024), ("w_ssd_conv", 3, 3072), ("w_ffn_conv", 3, 5632))


def _pack_big(tree):
    parts = []
    for name, _, transposed in BIG:
        a = tree[name][0]
        parts.append(a.T if transposed else a)
    return jnp.concatenate(parts, axis=0)


def _unpack_big(slab):
    out, r0 = {}, 0
    for name, r, transposed in BIG:
        a = slab[r0:r0 + r]
        out[name] = (a.T if transposed else a)[None]
        r0 += r
    return out


def _pack_flat(arrays, rows):
    flat = jnp.concatenate([a.reshape(-1) for a in arrays])
    return jnp.pad(flat, (0, rows * D_MODEL - flat.shape[0])).reshape(rows, D_MODEL)


def _unpack_flat(slab, shapes):
    flat, out, o = slab.reshape(-1), [], 0
    for s in shapes:
        n = math.prod(s)
        out.append(flat[o:o + n].reshape(s))
        o += n
    return out


def kernel(x, meta_tokens, norm_mix_w, w_in, ret_gn_w, w_ret_out, w_ssd_conv, b_ssd_conv, dt_bias_f, dt_bias_b, a_log_f, a_log_b, d_skip, ssd_norm_w, w_ssd_out, w_out, norm_ffn_w, w_ffn_up, w_ffn_conv, b_ffn_conv, w_ffn_down, final_norm_w, loss_target, m_meta_tokens, m_norm_mix_w, m_w_in, m_ret_gn_w, m_w_ret_out, m_w_ssd_conv, m_b_ssd_conv, m_dt_bias_f, m_dt_bias_b, m_a_log_f, m_a_log_b, m_d_skip, m_ssd_norm_w, m_w_ssd_out, m_w_out, m_norm_ffn_w, m_w_ffn_up, m_w_ffn_conv, m_b_ffn_conv, m_w_ffn_down, m_final_norm_w, v_meta_tokens, v_norm_mix_w, v_w_in, v_ret_gn_w, v_w_ret_out, v_w_ssd_conv, v_b_ssd_conv, v_dt_bias_f, v_dt_bias_b, v_a_log_f, v_a_log_b, v_d_skip, v_ssd_norm_w, v_w_ssd_out, v_w_out, v_norm_ffn_w, v_w_ffn_up, v_w_ffn_conv, v_b_ffn_conv, v_w_ffn_down, v_final_norm_w):
    given = dict(locals())
    wt = {n: given[n] for n in WEIGHTS}
    mt = {n: given["m_" + n] for n in WEIGHTS}
    vt = {n: given["v_" + n] for n in WEIGHTS}
    me = 4 * lax.axis_index("x") + 2 * lax.axis_index("y") + lax.axis_index("c")

    w_slab = _pack_big(wt)
    small_names = [n for n, _, _ in SMALL_SHARDED]
    small_local = lambda tree: [tree[n].reshape(r, c // N_DEV) for n, r, c in SMALL_SHARDED]
    all_w = _push_blocks("gather_matrices", w_slab.astype(BF16), False)
    all_s = _push_blocks("gather_small", _pack_flat(small_local(wt), 8), False).reshape(N_DEV, -1)
    full, r0 = {}, 0
    for name, r, transposed in BIG:
        full[name + ("_t" if transposed else "")] = all_w[:, r0:r0 + r].reshape(N_DEV * r, D_MODEL)
        r0 += r
    o = 0
    for name, r, c in SMALL_SHARDED:
        n = r * c // N_DEV
        full[name] = all_s[:, o:o + n].reshape(N_DEV, r, c // N_DEV).transpose(1, 0, 2).reshape(r, c)
        o += n
    for name in REPLICATED:
        full[name] = wt[name]

    loss, grad_x, g = _local_step(x[0], loss_target[0], full)

    g_big = jnp.concatenate(
        [g[name + ("_t" if t else "")].reshape(N_DEV, r, D_MODEL) for name, r, t in BIG], axis=1)
    g_slab = _sum_blocks("sum_matrices", _push_blocks("exchange_matrices", g_big.astype(BF16), True))
    small_parts = [g[n] for n in REPLICATED] + [g[n] for n in small_names] + [loss.reshape(1)]
    g_small = _sum_blocks("sum_small", _push_blocks("gather_small_grads", _pack_flat(small_parts, 64), False))
    small_red = _unpack_flat(g_small, [wt[n].shape for n in REPLICATED] + [(r, c) for _, r, c in SMALL_SHARDED] + [(1,)])
    grads = dict(zip(REPLICATED, small_red[:len(REPLICATED)]))
    for (name, r, c), red in zip(SMALL_SHARDED, small_red[len(REPLICATED):-1]):
        grads[name] = lax.dynamic_slice(red, (0, me * (c // N_DEV)), (r, c // N_DEV)).reshape(wt[name].shape)
    loss_all = small_red[-1][0]
    grads.update(_unpack_big(g_slab))

    d_slab, m_slab, v_slab = _adamw("adamw_matrices", w_slab, g_slab, _pack_big(mt), _pack_big(vt))
    rest = list(REPLICATED) + small_names
    shapes = [wt[n].shape for n in rest]
    pack_rest = lambda tree: _pack_flat([tree[n] for n in rest], 24)
    d_rest, m_rest, v_rest = _adamw("adamw_rest", pack_rest(wt), pack_rest(grads), pack_rest(mt), pack_rest(vt))
    delta, new_m, new_v = _unpack_big(d_slab), _unpack_big(m_slab), _unpack_big(v_slab)
    delta.update(zip(rest, _unpack_flat(d_rest, shapes)))
    new_m.update(zip(rest, _unpack_flat(m_rest, shapes)))
    new_v.update(zip(rest, _unpack_flat(v_rest, shapes)))

    return (loss_all, grad_x[None], *[grads[n] for n in WEIGHTS], *[delta[n] for n in WEIGHTS],
            *[new_m[n] for n in WEIGHTS], *[new_v[n] for n in WEIGHTS])
```

```python
import functools
import math

import jax
import jax.numpy as jnp
from jax import lax
from jax.experimental import pallas as pl
from jax.experimental.pallas import tpu as pltpu

F32 = jnp.float32
BF16 = jnp.bfloat16

D_MODEL = 1024
CHUNK = 128
N_META = 16
PAD_ROWS = CHUNK - N_META
RET_HEADS = 4
RET_QK_DIM = 128
RET_V_DIM = 256
SSD_HEADS = 32
SSD_HEAD_DIM = 64
SSD_GROUPS = 4
SSD_STATE = 128
HEADS_PER_GROUP = SSD_HEADS // SSD_GROUPS
PAIRS_PER_GROUP = HEADS_PER_GROUP // 2
D_FF = 2816
EPS = 1e-6
ROPE_BASE = 10000.0
N_DEV = 8

ADAM_LR = 0.001
ADAM_B1 = 0.9
ADAM_B2 = 0.999
ADAM_EPS = 1e-08
ADAM_WD = 0.01
ADAM_STEP = 10

VMEM_LIMIT = 56 * 1024 * 1024
HIGHEST = lax.Precision.HIGHEST

SEGMENTS = (("qk", 0, 1024), ("v", 1024, 2048), ("g", 2048, 3072), ("z", 3072, 5120), ("xs", 5120, 7168),
            ("B", 7168, 7680), ("C", 7680, 8192), ("dt", 8192, 8256), ("gates", 8256, 10304))


def _pick(n, cands):
    for c in cands:
        if n % c == 0:
            return c
    raise ValueError(f"no tile for {n}")


def _params(sem):
    return pltpu.CompilerParams(dimension_semantics=sem, vmem_limit_bytes=VMEM_LIMIT)


def _dot(a, b, dims=(((1,), (0,)), ((), ())), precision=None):
    return lax.dot_general(a, b, dims, preferred_element_type=F32, precision=precision)


def _dot_nt(a, b):
    return _dot(a, b, (((1,), (1,)), ((), ())))


def _dot_tn(a, b):
    return _dot(a, b, (((0,), (0,)), ((), ())))


def _mm(name, a, b, mode, add=None, out_dtype=F32):
    if mode == "nn":
        (M, K), N = a.shape, b.shape[1]
    elif mode == "nt":
        (M, K), N = a.shape, b.shape[0]
    else:
        (K, M), N = a.shape, b.shape[1]
    tm = _pick(M, (384, 256, 1408, 1024, 512, 128, 64, 16))
    tn = _pick(N, (1408, 1024, 512, 128, 64))
    if mode == "tn":
        tk = _pick(K, (1056, 512, 256, 128))
    else:
        tk = K if K <= 2048 else _pick(K, (1408, 1024))
    nk = K // tk
    if mode == "nn":
        a_spec = pl.BlockSpec((tm, tk), lambda n, m, k: (m, k))
        b_spec = pl.BlockSpec((tk, tn), lambda n, m, k: (k, n))
        dims = (((1,), (0,)), ((), ()))
    elif mode == "nt":
        a_spec = pl.BlockSpec((tm, tk), lambda n, m, k: (m, k))
        b_spec = pl.BlockSpec((tn, tk), lambda n, m, k: (n, k))
        dims = (((1,), (1,)), ((), ()))
    else:
        a_spec = pl.BlockSpec((tk, tm), lambda n, m, k: (k, m))
        b_spec = pl.BlockSpec((tk, tn), lambda n, m, k: (k, n))
        dims = (((0,), (0,)), ((), ()))
    o_spec = pl.BlockSpec((tm, tn), lambda n, m, k: (m, n))
    in_specs = [a_spec, b_spec] + ([o_spec] if add is not None else [])
    args = [a, b] + ([add] if add is not None else [])

    def body(*refs):
        if add is not None:
            a_ref, b_ref, r_ref, o_ref, acc = refs
        else:
            a_ref, b_ref, o_ref, acc = refs
        k = pl.program_id(2)
        p = _dot(a_ref[...].astype(BF16), b_ref[...].astype(BF16), dims)

        def finish(r):
            if add is not None:
                r = r + r_ref[...]
            o_ref[...] = r.astype(out_dtype)

        if nk == 1:
            finish(p)
        else:
            @pl.when(k == 0)
            def _():
                acc[...] = p

            @pl.when(k > 0)
            def _():
                acc[...] += p

            @pl.when(k == nk - 1)
            def _():
                finish(acc[...])

    return pl.pallas_call(
        body, name=name, grid=(N // tn, M // tm, nk), in_specs=in_specs, out_specs=o_spec,
        out_shape=jax.ShapeDtypeStruct((M, N), out_dtype),
        scratch_shapes=[pltpu.VMEM((tm, tn) if nk > 1 else (8, 128), F32)],
        compiler_params=_params(("arbitrary", "arbitrary", "arbitrary")),
    )(*args)


def _const(c):
    return lambda j: c


def _rows(name, fn, T, ncol, ins, params, outs, accs=(), halo=False):
    tm = _pick(T, (384, 256, 128))
    R = T // tm
    hb = tm // 8
    in_specs, args = [], []
    for spec in ins:
        arr, w, cf = spec[:3]
        lead = spec[3] if len(spec) > 3 else None
        if lead is None:
            mk = lambda blk, rf, cf=cf: pl.BlockSpec(blk, lambda j, i: (rf(i), cf(j)))
            shape = lambda r, w=w: (r, w)
        else:
            mk = lambda blk, rf, cf=cf, lead=lead: pl.BlockSpec(blk, lambda j, i: (lead, rf(i), cf(j)))
            shape = lambda r, w=w: (None, r, w)
        in_specs.append(mk(shape(tm), lambda i: i))
        args.append(arr)
        if halo:
            in_specs.append(mk(shape(8), lambda i: jnp.maximum(i * hb - 1, 0)))
            in_specs.append(mk(shape(8), lambda i: jnp.minimum((i + 1) * hb, T // 8 - 1)))
            args += [arr, arr]
    for arr, w, cf in params:
        in_specs.append(pl.BlockSpec((arr.shape[0], w), lambda j, i, cf=cf: (0, cf(j))))
        args.append(arr)
    out_shape, out_specs = [], []
    for tw, w, cf, dt in outs:
        out_shape.append(jax.ShapeDtypeStruct((T, tw), dt))
        out_specs.append(pl.BlockSpec((tm, w), lambda j, i, cf=cf: (i, cf(j))))
    for r, tw, w, cf in accs:
        out_shape.append(jax.ShapeDtypeStruct((r, tw), F32))
        out_specs.append(pl.BlockSpec((r, w), lambda j, i, cf=cf: (0, cf(j))))
    n_in, n_par, n_out, n_acc = len(ins), len(params), len(outs), len(accs)

    def body(*refs):
        i = pl.program_id(1)
        vals, p = [], 0
        for _ in range(n_in):
            if halo:
                vals.append(jnp.concatenate([refs[p + 1][...], refs[p][...], refs[p + 2][...]], axis=0))
                p += 3
            else:
                vals.append(refs[p][...])
                p += 1
        pvals = [refs[p + k][...] for k in range(n_par)]
        p += n_par
        res = fn(i, *vals, *pvals)
        for k in range(n_out):
            refs[p + k][...] = res[k].astype(refs[p + k].dtype)
        p += n_out
        for k in range(n_acc):
            ref, v = refs[p + k], res[n_out + k]

            @pl.when(i == 0)
            def _(ref=ref, v=v):
                ref[...] = v

            @pl.when(i > 0)
            def _(ref=ref, v=v):
                ref[...] += v

    res = pl.pallas_call(
        body, name=name, grid=(ncol, R), in_specs=in_specs, out_specs=out_specs, out_shape=out_shape,
        compiler_params=_params(("arbitrary", "arbitrary")),
    )(*args)
    return res


def _tile_rows(T):
    return _pick(T, (384, 256, 128))


def _row_ids(i, T, halo=False):
    tm = _tile_rows(T)
    if halo:
        return i * tm - 8 + lax.broadcasted_iota(jnp.int32, (tm + 16, 1), 0)
    return i * tm + lax.broadcasted_iota(jnp.int32, (tm, 1), 0)


def _rms(x, w):
    return x * lax.rsqrt(jnp.mean(x * x, axis=-1, keepdims=True) + EPS) * w


def _silu(x):
    return x * jax.nn.sigmoid(x)


def _conv3(x, w):
    n = x.shape[0]
    return w[0:1] * pltpu.roll(x, 1, 0) + w[1:2] * x + w[2:3] * pltpu.roll(x, n - 1, 0)


def _conv3_t(d, w):
    n = d.shape[0]
    return w[0:1] * pltpu.roll(d, n - 1, 0) + w[1:2] * d + w[2:3] * pltpu.roll(d, 1, 0)


def _center(x):
    return x[8:x.shape[0] - 8]


def _retention(name, a, b, v, T):
    da = a.shape[1] // RET_HEADS
    dv = v.shape[1] // RET_HEADS
    nc = T // CHUNK
    log_gammas = [math.log(1.0 - 2.0 ** (-5.0 - h)) for h in range(RET_HEADS)]

    def body(a_ref, b_ref, v_ref, o_ref, st):
        h = pl.program_id(0)
        lg = jnp.float32(log_gammas[RET_HEADS - 1])
        for k in range(RET_HEADS - 2, -1, -1):
            lg = jnp.where(h == k, jnp.float32(log_gammas[k]), lg)
        li = lax.broadcasted_iota(jnp.int32, (CHUNK, CHUNK), 0)
        si = lax.broadcasted_iota(jnp.int32, (CHUNK, CHUNK), 1)
        dmat = jnp.exp(lg * jnp.abs(li - si).astype(F32))
        pos = lax.broadcasted_iota(jnp.int32, (CHUNK, 1), 0).astype(F32)
        kdec_f = jnp.exp((CHUNK - 1 - pos) * lg)
        qdec_f = jnp.exp((pos + 1) * lg)
        kdec_b = jnp.exp(pos * lg)
        qdec_b = jnp.exp((CHUNK - pos) * lg)
        cdec = jnp.exp(CHUNK * lg)

        def rows(n):
            return pl.ds(pl.multiple_of(n * CHUNK, CHUNK), CHUNK)

        st[...] = jnp.zeros_like(st)

        def fwd(n, carry):
            r = rows(n)
            av, bv, vv = a_ref[r, :], b_ref[r, :], v_ref[r, :].astype(BF16)
            s = _dot_nt(av.astype(BF16), bv.astype(BF16)) * dmat
            y = _dot(s.astype(BF16), vv) + _dot((av * qdec_f).astype(BF16), st[...].astype(BF16))
            o_ref[r, :] = y
            st[...] = cdec * st[...] + _dot_tn((bv * kdec_f).astype(BF16), vv)
            return carry

        lax.fori_loop(0, nc, fwd, 0)
        st[...] = jnp.zeros_like(st)

        def bwd(m, carry):
            r = rows(nc - 1 - m)
            av, bv, vv = a_ref[r, :], b_ref[r, :], v_ref[r, :].astype(BF16)
            o_ref[r, :] += _dot((av * qdec_b).astype(BF16), st[...].astype(BF16))
            st[...] = cdec * st[...] + _dot_tn((bv * kdec_b).astype(BF16), vv)
            return carry

        lax.fori_loop(0, nc, bwd, 0)

    return pl.pallas_call(
        body, name=name, grid=(RET_HEADS,),
        in_specs=[pl.BlockSpec((T, da), lambda h: (0, h)), pl.BlockSpec((T, da), lambda h: (0, h)),
                  pl.BlockSpec((T, dv), lambda h: (0, h))],
        out_specs=pl.BlockSpec((T, dv), lambda h: (0, h)),
        out_shape=jax.ShapeDtypeStruct((T, RET_HEADS * dv), F32),
        scratch_shapes=[pltpu.VMEM((da, dv), F32)],
        compiler_params=_params(("arbitrary",)),
    )(a, b, v)


def _softplus(x):
    return jnp.maximum(x, 0.0) + jnp.log1p(jnp.exp(-jnp.abs(x)))


def _lane_lo():
    return lax.broadcasted_iota(jnp.int32, (1, CHUNK), 1) < SSD_HEAD_DIM


def _pair_cols(col, j):
    return jnp.where(_lane_lo(), col[:, 2 * j:2 * j + 1], col[:, 2 * j + 1:2 * j + 2])


def _pair_rows(colr, j):
    lo = lax.broadcasted_iota(jnp.int32, (CHUNK, 1), 0) < SSD_HEAD_DIM
    return jnp.where(lo, colr[2 * j:2 * j + 1, :], colr[2 * j + 1:2 * j + 2, :])


def _onehot8(h):
    return (lax.broadcasted_iota(jnp.int32, (1, HEADS_PER_GROUP), 1) == h).astype(F32)


def _ssd_pre(d, c, rawc, rawr, bc, br, alc, alr):
    li = lax.broadcasted_iota(jnp.int32, (CHUNK, CHUNK), 0)
    si = lax.broadcasted_iota(jnp.int32, (CHUNK, CHUNK), 1)
    dif = jnp.where(d == 0, li - si, si - li)
    mask = dif >= 0
    mask_t = dif <= 0
    rowc = c * CHUNK + lax.broadcasted_iota(jnp.int32, (CHUNK, 1), 0)
    rowr = c * CHUNK + lax.broadcasted_iota(jnp.int32, (1, CHUNK), 1)
    dtc = jnp.where(rowc >= PAD_ROWS, _softplus(rawc + bc), 0.0)
    dtr = jnp.where(rowr >= PAD_ROWS, _softplus(rawr + br), 0.0)
    ac = -jnp.exp(alc)
    ar = -jnp.exp(alr)
    dlc = dtc * ac
    dlr = dtr * ar
    alpc = _dot(mask.astype(F32), dlc, precision=HIGHEST)
    alpr = _dot(dlr, mask_t.astype(F32), precision=HIGHEST)
    endc = jnp.sum(dlc, axis=0, keepdims=True)
    endr = jnp.sum(dlr, axis=1, keepdims=True)
    return dict(mask=mask, mask_t=mask_t, dtc=dtc, ac=ac, alpc=alpc, alpr=alpr, endc=endc, endr=endr,
                valid=rowc >= PAD_ROWS)


def _chunk_of(d, n, nc):
    return n + d * (nc - 1 - 2 * n)


def _ssd_small_specs(cfn):
    return [
        pl.BlockSpec((None, None, CHUNK, HEADS_PER_GROUP), lambda d, g, n: (d, g, cfn(d, n), 0)),
        pl.BlockSpec((None, None, HEADS_PER_GROUP, CHUNK), lambda d, g, n: (d, g, 0, cfn(d, n))),
        pl.BlockSpec((None, None, 1, HEADS_PER_GROUP), lambda d, g, n: (d, g, 0, 0)),
        pl.BlockSpec((None, None, HEADS_PER_GROUP, 1), lambda d, g, n: (d, g, 0, 0)),
        pl.BlockSpec((None, None, 1, HEADS_PER_GROUP), lambda d, g, n: (d, g, 0, 0)),
        pl.BlockSpec((None, None, HEADS_PER_GROUP, 1), lambda d, g, n: (d, g, 0, 0)),
    ]


def _ssd_fwd(xs, bm, cm, small, T):
    nc = T // CHUNK
    cfn = lambda d, n: _chunk_of(d, n, nc)

    def body(x_ref, b_ref, c_ref, rawc_ref, rawr_ref, bc_ref, br_ref, alc_ref, alr_ref, y_ref, hs_ref, h_scr):
        d, n = pl.program_id(0), pl.program_id(2)
        c = cfn(d, n)

        @pl.when(n == 0)
        def _():
            h_scr[...] = jnp.zeros_like(h_scr)

        q = _ssd_pre(d, c, rawc_ref[...], rawr_ref[...], bc_ref[...], br_ref[...], alc_ref[...], alr_ref[...])
        bv = b_ref[...].astype(BF16)
        cv = c_ref[...].astype(BF16)
        cb = _dot_nt(cv, bv)
        lo = _lane_lo()
        for j in range(PAIRS_PER_GROUP):
            xp = x_ref[:, j * CHUNK:(j + 1) * CHUNK]
            xd = xp * _pair_cols(q["dtc"], j)
            xdb = xd.astype(BF16)
            yi = []
            for e in range(2):
                h = 2 * j + e
                lm = jnp.exp(jnp.where(q["mask"], q["alpc"][:, h:h + 1] - q["alpr"][h:h + 1, :], -jnp.inf))
                yi.append(_dot((cb * lm).astype(BF16), xdb))
            alp = _pair_cols(q["alpc"], j)
            hp = h_scr[j]
            hs_ref[j] = hp
            yo = jnp.exp(alp) * _dot_nt(cv, hp.astype(BF16))
            y_ref[:, j * CHUNK:(j + 1) * CHUNK] = jnp.where(lo, yi[0], yi[1]) + yo
            de = jnp.exp(_pair_cols(q["endc"], j) - alp)
            h_scr[j] = jnp.exp(_pair_rows(q["endr"], j)) * hp + _dot_tn((xd * de).astype(BF16), bv)

    gw = HEADS_PER_GROUP * SSD_HEAD_DIM
    return pl.pallas_call(
        body, name="ssd_fwd", grid=(2, SSD_GROUPS, nc),
        in_specs=[pl.BlockSpec((CHUNK, gw), lambda d, g, n: (cfn(d, n), g)),
                  pl.BlockSpec((CHUNK, SSD_STATE), lambda d, g, n: (cfn(d, n), g)),
                  pl.BlockSpec((CHUNK, SSD_STATE), lambda d, g, n: (cfn(d, n), g))] + _ssd_small_specs(cfn),
        out_specs=[pl.BlockSpec((None, CHUNK, gw), lambda d, g, n: (d, cfn(d, n), g)),
                   pl.BlockSpec((None, None, None, PAIRS_PER_GROUP, CHUNK, SSD_STATE),
                                lambda d, g, n: (d, g, cfn(d, n), 0, 0, 0))],
        out_shape=[jax.ShapeDtypeStruct((2, T, SSD_HEADS * SSD_HEAD_DIM), F32),
                   jax.ShapeDtypeStruct((2, SSD_GROUPS, nc, PAIRS_PER_GROUP, CHUNK, SSD_STATE), F32)],
        scratch_shapes=[pltpu.VMEM((PAIRS_PER_GROUP, CHUNK, SSD_STATE), F32)],
        compiler_params=_params(("arbitrary", "arbitrary", "arbitrary")),
    )(xs, bm, cm, *small)


def _ssd_bwd(xs, bm, cm, small, hs, dy, T):
    nc = T // CHUNK
    cfn = lambda d, n: _chunk_of(1 - d, n, nc)

    def body(x_ref, b_ref, c_ref, rawc_ref, rawr_ref, bc_ref, br_ref, alc_ref, alr_ref, hs_ref, dy_ref,
             dx_ref, db_ref, dc_ref, draw_ref, dbias_ref, dalog_ref, dh_scr):
        d, n = pl.program_id(0), pl.program_id(2)
        c = cfn(d, n)

        @pl.when(n == 0)
        def _():
            dh_scr[...] = jnp.zeros_like(dh_scr)

        rawc, bc = rawc_ref[...], bc_ref[...]
        q = _ssd_pre(d, c, rawc, rawr_ref[...], bc, br_ref[...], alc_ref[...], alr_ref[...])
        b32, c32 = b_ref[...], c_ref[...]
        bv, cv = b32.astype(BF16), c32.astype(BF16)
        cb = _dot_nt(cv, bv)
        cbt = _dot_nt(bv, cv)
        lo = _lane_lo()
        row_lo = lax.broadcasted_iota(jnp.int32, (CHUNK, 1), 0) < SSD_HEAD_DIM
        dcb = jnp.zeros((CHUNK, CHUNK), F32)
        dcp = jnp.zeros((CHUNK, SSD_STATE), F32)
        dbp = jnp.zeros((CHUNK, SSD_STATE), F32)
        dalp = jnp.zeros((CHUNK, HEADS_PER_GROUP), F32)
        dend = jnp.zeros((1, HEADS_PER_GROUP), F32)
        ddtx = jnp.zeros((CHUNK, HEADS_PER_GROUP), F32)

        def half_sums(t):
            return (jnp.sum(jnp.where(lo, t, 0.0), axis=1, keepdims=True),
                    jnp.sum(jnp.where(lo, 0.0, t), axis=1, keepdims=True))

        for j in range(PAIRS_PER_GROUP):
            xp = x_ref[:, j * CHUNK:(j + 1) * CHUNK]
            dtp = _pair_cols(q["dtc"], j)
            xd = xp * dtp
            xdb = xd.astype(BF16)
            dyp = dy_ref[:, j * CHUNK:(j + 1) * CHUNK]
            dyb = dyp.astype(BF16)
            hn = hs_ref[j]
            hnb = hn.astype(BF16)
            dh1 = dh_scr[j]
            dh1b = dh1.astype(BF16)
            alp = _pair_cols(q["alpc"], j)
            ea = jnp.exp(alp)
            de = jnp.exp(_pair_cols(q["endc"], j) - alp)
            dxi = []
            for e in range(2):
                h = 2 * j + e
                ac_, ar_ = q["alpc"][:, h:h + 1], q["alpr"][h:h + 1, :]
                lm = jnp.exp(jnp.where(q["mask"], ac_ - ar_, -jnp.inf))
                mt = cbt * jnp.exp(jnp.where(q["mask_t"], ar_ - ac_, -jnp.inf))
                dxi.append(_dot(mt.astype(BF16), dyb))
                dyeb_h = (jnp.where(lo, dyp, 0.0) if e == 0 else jnp.where(lo, 0.0, dyp)).astype(BF16)
                gl = _dot_nt(dyeb_h, xdb) * lm
                dcb = dcb + gl
                ra = jnp.sum(gl * cb, axis=1, keepdims=True) - jnp.sum(_dot_nt(xdb, dyeb_h) * mt, axis=1, keepdims=True)
                dalp = dalp + ra * _onehot8(h)
            y_off = ea * _dot_nt(cv, hnb)
            dxs_state = de * _dot_nt(bv, dh1b)
            dxd = jnp.where(lo, dxi[0], dxi[1]) + dxs_state
            dyeb = (dyp * ea).astype(BF16)
            dcp = dcp + _dot(dyeb, hnb)
            dbp = dbp + _dot((xd * de).astype(BF16), dh1b)
            dh_scr[j] = jnp.exp(_pair_rows(q["endr"], j)) * dh1 + _dot_tn(dyeb, cv)
            r0, r1 = half_sums(dyp * y_off - xd * dxs_state)
            dalp = dalp + r0 * _onehot8(2 * j) + r1 * _onehot8(2 * j + 1)
            t0, t1 = half_sums(xd * dxs_state)
            u = jnp.sum(dh1 * hn, axis=1, keepdims=True)
            u0 = jnp.sum(jnp.where(row_lo, u, 0.0), axis=0, keepdims=True)
            u1 = jnp.sum(jnp.where(row_lo, 0.0, u), axis=0, keepdims=True)
            eend = jnp.exp(q["endc"])
            dend = dend + (jnp.sum(t0, axis=0, keepdims=True) + eend * u0) * _onehot8(2 * j) \
                        + (jnp.sum(t1, axis=0, keepdims=True) + eend * u1) * _onehot8(2 * j + 1)
            dx_ref[:, j * CHUNK:(j + 1) * CHUNK] = dxd * dtp
            w0, w1 = half_sums(dxd * xp)
            ddtx = ddtx + w0 * _onehot8(2 * j) + w1 * _onehot8(2 * j + 1)

        dcbb = dcb.astype(BF16)
        dc_ref[...] = dcp + _dot(dcbb, bv)
        db_ref[...] = dbp + _dot_tn(dcbb, cv)
        ddl = _dot(q["mask_t"].astype(F32), dalp, precision=HIGHEST) + dend
        ddt = ddl * q["ac"] + ddtx
        draw = jnp.where(q["valid"], ddt * jax.nn.sigmoid(rawc + bc), 0.0)
        draw_ref[...] = draw
        dbias = jnp.sum(draw, axis=0, keepdims=True)
        dalog = jnp.sum(ddl * q["dtc"], axis=0, keepdims=True) * q["ac"]

        @pl.when(n == 0)
        def _():
            dbias_ref[...] = dbias
            dalog_ref[...] = dalog

        @pl.when(n > 0)
        def _():
            dbias_ref[...] += dbias
            dalog_ref[...] += dalog

    gw = HEADS_PER_GROUP * SSD_HEAD_DIM
    acc_spec = pl.BlockSpec((None, None, 1, HEADS_PER_GROUP), lambda d, g, n: (d, g, 0, 0))
    return pl.pallas_call(
        body, name="ssd_bwd", grid=(2, SSD_GROUPS, nc),
        in_specs=[pl.BlockSpec((CHUNK, gw), lambda d, g, n: (cfn(d, n), g)),
                  pl.BlockSpec((CHUNK, SSD_STATE), lambda d, g, n: (cfn(d, n), g)),
                  pl.BlockSpec((CHUNK, SSD_STATE), lambda d, g, n: (cfn(d, n), g))] + _ssd_small_specs(cfn) + [
                  pl.BlockSpec((None, None, None, PAIRS_PER_GROUP, CHUNK, SSD_STATE),
                               lambda d, g, n: (d, g, cfn(d, n), 0, 0, 0)),
                  pl.BlockSpec((CHUNK, gw), lambda d, g, n: (cfn(d, n), g))],
        out_specs=[pl.BlockSpec((None, CHUNK, gw), lambda d, g, n: (d, cfn(d, n), g)),
                   pl.BlockSpec((None, CHUNK, SSD_STATE), lambda d, g, n: (d, cfn(d, n), g)),
                   pl.BlockSpec((None, CHUNK, SSD_STATE), lambda d, g, n: (d, cfn(d, n), g)),
                   pl.BlockSpec((None, None, CHUNK, HEADS_PER_GROUP), lambda d, g, n: (d, g, cfn(d, n), 0)),
                   acc_spec, acc_spec],
        out_shape=[jax.ShapeDtypeStruct((2, T, SSD_HEADS * SSD_HEAD_DIM), F32),
                   jax.ShapeDtypeStruct((2, T, SSD_GROUPS * SSD_STATE), F32),
                   jax.ShapeDtypeStruct((2, T, SSD_GROUPS * SSD_STATE), F32),
                   jax.ShapeDtypeStruct((2, SSD_GROUPS, T, HEADS_PER_GROUP), F32),
                   jax.ShapeDtypeStruct((2, SSD_GROUPS, 1, HEADS_PER_GROUP), F32),
                   jax.ShapeDtypeStruct((2, SSD_GROUPS, 1, HEADS_PER_GROUP), F32)],
        scratch_shapes=[pltpu.VMEM((PAIRS_PER_GROUP, CHUNK, SSD_STATE), F32)],
        compiler_params=_params(("arbitrary", "arbitrary", "arbitrary")),
    )(xs, bm, cm, *small, hs, dy)


def _rot(x, cs, sn):
    return x * cs + pltpu.roll(x, RET_QK_DIM // 2, 1) * sn


def _rot_t(d, cs, sn):
    return d * cs + pltpu.roll(d * sn, RET_QK_DIM // 2, 1)


def _ret_post(y, g, w):
    parts = []
    for h in range(RET_HEADS):
        yh = y[:, h * RET_V_DIM:(h + 1) * RET_V_DIM]
        mu = jnp.mean(yh, axis=-1, keepdims=True)
        var = jnp.mean(jnp.square(yh - mu), axis=-1, keepdims=True)
        parts.append((yh - mu) * lax.rsqrt(var + EPS))
    return _silu(g) * (jnp.concatenate(parts, axis=1) * w)


def _ssd_post(yf, yb, xs, z, dskip, w):
    y = (yf + yb + xs * dskip) * _silu(z)
    return y * lax.rsqrt(jnp.mean(y * y, axis=-1, keepdims=True) + EPS) * w


def _merge(gates, yr, ys, valid):
    m = jax.nn.sigmoid(gates[:, :D_MODEL]) * yr + jax.nn.sigmoid(gates[:, D_MODEL:]) * ys
    return jnp.where(valid, m, 0.0)


def _rope_tables(T):
    half = RET_QK_DIM // 2
    inv = ROPE_BASE ** (-jnp.arange(half, dtype=F32) / half)
    pos = (jnp.arange(T) - PAD_ROWS).astype(F32)
    ang = pos[:, None] * inv[None, :]
    cos, sin = jnp.cos(ang), jnp.sin(ang)
    return jnp.concatenate([cos, cos], axis=1), jnp.concatenate([-sin, sin], axis=1)


def _per_group(v):
    c = v.reshape(SSD_GROUPS, 1, HEADS_PER_GROUP)
    return c, c.reshape(SSD_GROUPS, HEADS_PER_GROUP, 1)


def _local_step(x, target, w, tick, late_weights, early_grads, in_grads):
    S = x.shape[0]
    T = S + CHUNK
    tm = _tile_rows(T)
    c0 = _const(0)

    h0 = jnp.concatenate([jnp.zeros((PAD_ROWS, D_MODEL), F32), w["meta_tokens"], x], axis=0)
    tgt = jnp.concatenate([jnp.zeros((CHUNK, D_MODEL), F32), target], axis=0)
    w_in = {name: w["w_in_t"][a:b] for name, a, b in SEGMENTS}
    w_in["dt"] = jnp.pad(w_in["dt"], ((0, CHUNK - 2 * SSD_HEADS), (0, 0)))

    def norm_cast(name, h, nw):
        return _rows(name, lambda i, hv, wv: (_rms(hv, wv),), T, 1, [(h, D_MODEL, c0)], [(nw, D_MODEL, c0)],
                     [(D_MODEL, D_MODEL, c0, BF16)])[0]

    u = norm_cast("norm_mix", h0, w["norm_mix_w"] + tick)
    proj = {name: _mm("proj_" + name, u, w_in[name], "nt") for name, _, _ in SEGMENTS}

    cs, sn = _rope_tables(T)
    scale = RET_QK_DIM ** -0.5

    def rot_fn(i, qk, csv, snv):
        q = [_rot(qk[:, h * 128:(h + 1) * 128], csv, snv) for h in range(RET_HEADS)]
        k = [_rot(qk[:, (RET_HEADS + h) * 128:(RET_HEADS + h + 1) * 128], csv, snv) * scale for h in range(RET_HEADS)]
        return jnp.concatenate(q, axis=1), jnp.concatenate(k, axis=1)

    qr, kr = _rows("rotary", rot_fn, T, 1, [(proj["qk"], 1024, c0), (cs, 128, c0), (sn, 128, c0)], [],
                   [(512, 512, c0, F32), (512, 512, c0, F32)])
    y_ret = _retention("retention", qr, kr, proj["v"], T)
    w = dict(w, **late_weights(y_ret))
    w_up_g, w_up_u = w["w_ffn_up_t"][:D_FF], w["w_ffn_up_t"][D_FF:]
    a_ret = _rows("ret_post", lambda i, y, g, gw: (_ret_post(y, g, gw),), T, 1,
                  [(y_ret, 1024, c0), (proj["g"], 1024, c0)], [(w["ret_gn_w"], 1024, c0)],
                  [(1024, 1024, c0, BF16)])[0]
    y_ret_o = _mm("ret_out", a_ret, w["w_ret_out"], "nn")

    conv_w = {"xs": w["w_ssd_conv"][:, :2048], "B": w["w_ssd_conv"][:, 2048:2560], "C": w["w_ssd_conv"][:, 2560:]}
    conv_b = {"xs": w["b_ssd_conv"][:, :2048], "B": w["b_ssd_conv"][:, 2048:2560], "C": w["b_ssd_conv"][:, 2560:]}

    def ssd_conv_fn(i, xe, cw, cb):
        r = _row_ids(i, T, True)
        xe = jnp.where((r >= 0) & (r < T), xe, 0.0)
        return (_center(jnp.where(r >= PAD_ROWS, _silu(_conv3(xe, cw) + cb), 0.0)),)

    act = {}
    for name in ("xs", "B", "C"):
        wd = proj[name].shape[1]
        cw = 512
        act[name] = _rows("ssd_conv_" + name, ssd_conv_fn, T, wd // cw, [(proj[name], cw, lambda j: j)],
                          [(conv_w[name], cw, lambda j: j), (conv_b[name], cw, lambda j: j)],
                          [(wd, cw, lambda j: j, F32)], halo=True)[0]

    raw = proj["dt"][:, :2 * SSD_HEADS].reshape(T, 2, SSD_GROUPS, HEADS_PER_GROUP)
    rawc = raw.transpose(1, 2, 0, 3)
    rawr = raw.transpose(1, 2, 3, 0)
    bias = [_per_group(w["dt_bias_f"]), _per_group(w["dt_bias_b"])]
    alog = [_per_group(w["a_log_f"]), _per_group(w["a_log_b"])]
    small = (rawc, rawr, jnp.stack([bias[0][0], bias[1][0]]), jnp.stack([bias[0][1], bias[1][1]]),
             jnp.stack([alog[0][0], alog[1][0]]), jnp.stack([alog[0][1], alog[1][1]]))
    y_dir, states = _ssd_fwd(act["xs"], act["B"], act["C"], small, T)

    dskip_e = jnp.repeat(w["d_skip"], SSD_HEAD_DIM, axis=1)
    gcol = lambda j: j
    gw_ = 512
    a_ssd = _rows("ssd_post", lambda i, yf, yb, xv, zv, dk, nw: (_ssd_post(yf, yb, xv, zv, dk, nw),), T, SSD_GROUPS,
                  [(y_dir, gw_, gcol, 0), (y_dir, gw_, gcol, 1), (act["xs"], gw_, gcol), (proj["z"], gw_, gcol)],
                  [(dskip_e, gw_, gcol), (w["ssd_norm_w"], gw_, gcol)], [(2048, gw_, gcol, BF16)])[0]
    y_ssd_o = _mm("ssd_out", a_ssd, w["w_ssd_out"], "nn")

    def merge_fn(i, gates, yr, ys):
        return (_merge(gates, yr, ys, _row_ids(i, T) >= PAD_ROWS),)

    merged = _rows("merge", merge_fn, T, 1, [(proj["gates"], 2048, c0), (y_ret_o, 1024, c0), (y_ssd_o, 1024, c0)], [],
                   [(1024, 1024, c0, BF16)])[0]
    h1 = _mm("mix_out", merged, w["w_out"], "nn", add=h0)

    n2 = norm_cast("norm_ffn", h1, w["norm_ffn_w"])
    fg_pre = _mm("ffn_up_g", n2, w_up_g, "nt")
    fu_pre = _mm("ffn_up_u", n2, w_up_u, "nt")
    cwg, cwu = w["w_ffn_conv"][:, :D_FF], w["w_ffn_conv"][:, D_FF:]
    cbg, cbu = w["b_ffn_conv"][:, :D_FF], w["b_ffn_conv"][:, D_FF:]
    fcol = lambda j: j
    fw = 1408

    def ffn_act_fn(i, ge, ue, wg, wu, bg, bu):
        return (_center(_silu(_conv3(ge, wg) + bg) * (_conv3(ue, wu) + bu)),)

    def ext_valid(i):
        r = _row_ids(i, T, True)
        return (r >= 0) & (r < T)

    def ffn_act_masked(i, ge, ue, wg, wu, bg, bu):
        v = ext_valid(i)
        return ffn_act_fn(i, jnp.where(v, ge, 0.0), jnp.where(v, ue, 0.0), wg, wu, bg, bu)

    a2 = _rows("ffn_act", ffn_act_masked, T, D_FF // fw, [(fg_pre, fw, fcol), (fu_pre, fw, fcol)],
               [(cwg, fw, fcol), (cwu, fw, fcol), (cbg, fw, fcol), (cbu, fw, fcol)], [(D_FF, fw, fcol, BF16)],
               halo=True)[0]
    h2 = _mm("ffn_down", a2, w["w_ffn_down"], "nn", add=h1)

    fnw = w["final_norm_w"].reshape(1, D_MODEL)

    def loss_fn(i, hv, tv, nw):
        valid = _row_ids(i, T) >= CHUNK
        y, vjp = jax.vjp(_rms, hv, nw)
        diff = jnp.where(valid, y - tv, 0.0)
        dh, dw = vjp(diff * (1.0 / D_MODEL))
        part = 0.5 / D_MODEL * jnp.sum(jnp.sum(diff * diff, axis=1, keepdims=True), axis=0, keepdims=True)
        return dh, jnp.broadcast_to(part, (1, 128)), dw

    dh2, loss_acc, d_fnw = _rows("loss", loss_fn, T, 1, [(h2, D_MODEL, c0), (tgt, D_MODEL, c0)], [(fnw, D_MODEL, c0)],
                                 [(D_MODEL, D_MODEL, c0, F32)], [(1, 128, 128, c0), (1, D_MODEL, D_MODEL, c0)])
    loss = loss_acc[0, 0]
    grads = {"final_norm_w": d_fnw.reshape(D_MODEL)}

    da2 = _mm("d_ffn_act", dh2, w["w_ffn_down"], "nt")
    grads["w_ffn_down"] = _mm("g_ffn_down", a2, dh2, "tn")

    def ffn_bwd_fn(i, ge, ue, de, wg, wu, bg, bu):
        v = ext_valid(i)
        ge, ue, de = jnp.where(v, ge, 0.0), jnp.where(v, ue, 0.0), jnp.where(v, de, 0.0)
        fg = _conv3(ge, wg) + bg
        fu = _conv3(ue, wu) + bu
        sg = jax.nn.sigmoid(fg)
        dfg = de * fu * (sg * (1.0 + fg * (1.0 - sg)))
        dfu = de * (fg * sg)
        n = ge.shape[0]

        def wgrad(df, xe):
            df_c = _center(df)
            return jnp.concatenate([jnp.sum(df_c * _center(pltpu.roll(xe, 1, 0)), axis=0, keepdims=True),
                                    jnp.sum(df_c * _center(xe), axis=0, keepdims=True),
                                    jnp.sum(df_c * _center(pltpu.roll(xe, n - 1, 0)), axis=0, keepdims=True)], axis=0)

        return (_center(_conv3_t(dfg, wg)), _center(_conv3_t(dfu, wu)), wgrad(dfg, ge), wgrad(dfu, ue),
                jnp.sum(_center(dfg), axis=0, keepdims=True), jnp.sum(_center(dfu), axis=0, keepdims=True))

    dfg_pre, dfu_pre, g_cwg, g_cwu, g_cbg, g_cbu = _rows(
        "ffn_act_bwd", ffn_bwd_fn, T, D_FF // fw, [(fg_pre, fw, fcol), (fu_pre, fw, fcol), (da2, fw, fcol)],
        [(cwg, fw, fcol), (cwu, fw, fcol), (cbg, fw, fcol), (cbu, fw, fcol)],
        [(D_FF, fw, fcol, BF16), (D_FF, fw, fcol, BF16)],
        [(3, D_FF, fw, fcol), (3, D_FF, fw, fcol), (1, D_FF, fw, fcol), (1, D_FF, fw, fcol)], halo=True)
    grads["w_ffn_conv"] = jnp.concatenate([g_cwg, g_cwu], axis=1)
    grads["b_ffn_conv"] = jnp.concatenate([g_cbg, g_cbu], axis=1)
    dn2 = _mm("d_norm_ffn_g", dfg_pre, w_up_g, "nn")
    dn2 = _mm("d_norm_ffn_u", dfu_pre, w_up_u, "nn", add=dn2)
    grads["w_ffn_up_t"] = jnp.concatenate([_mm("g_ffn_up_g", dfg_pre, n2, "tn"), _mm("g_ffn_up_u", dfu_pre, n2, "tn")],
                                          axis=0)

    def norm_bwd(name, h, nw, dn, dres):
        def fn(i, hv, dnv, drv, wv):
            _, vjp = jax.vjp(_rms, hv, wv)
            dh, dw = vjp(dnv)
            return dh + drv, dw
        return _rows(name, fn, T, 1, [(h, D_MODEL, c0), (dn, D_MODEL, c0), (dres, D_MODEL, c0)], [(nw, D_MODEL, c0)],
                     [(D_MODEL, D_MODEL, c0, F32)], [(1, D_MODEL, D_MODEL, c0)])

    dh1, grads["norm_ffn_w"] = norm_bwd("norm_ffn_bwd", h1, w["norm_ffn_w"], dn2, dh2)

    dmerged = _mm("d_merged", dh1, w["w_out"], "nt")
    grads["w_out"] = _mm("g_out", merged, dh1, "tn")

    def merge_bwd_fn(i, gates, yr, ys, dm):
        valid = _row_ids(i, T) >= PAD_ROWS
        _, vjp = jax.vjp(lambda a, b, c: _merge(a, b, c, valid), gates, yr, ys)
        return vjp(dm)

    dgates, dyr, dys = _rows("merge_bwd", merge_bwd_fn, T, 1,
                             [(proj["gates"], 2048, c0), (y_ret_o, 1024, c0), (y_ssd_o, 1024, c0), (dmerged, 1024, c0)],
                             [], [(2048, 2048, c0, BF16), (1024, 1024, c0, BF16), (1024, 1024, c0, BF16)])
    dproj = {"gates": dgates}

    da_ssd = _mm("d_ssd_act", dys, w["w_ssd_out"], "nt")
    grads["w_ssd_out"] = _mm("g_ssd_out", a_ssd, dys, "tn")

    def ssd_post_bwd_fn(i, yf, yb, xv, zv, da, dk, nw):
        _, vjp = jax.vjp(_ssd_post, yf, yb, xv, zv, dk, nw)
        dyf, _, dxv, dzv, ddk, dnw = vjp(da)
        return dyf, dxv, dzv, ddk, dnw

    dy_ssd, dxs_skip, dproj["z"], g_dskip_e, grads["ssd_norm_w"] = _rows(
        "ssd_post_bwd", ssd_post_bwd_fn, T, SSD_GROUPS,
        [(y_dir, gw_, gcol, 0), (y_dir, gw_, gcol, 1), (act["xs"], gw_, gcol), (proj["z"], gw_, gcol),
         (da_ssd, gw_, gcol)],
        [(dskip_e, gw_, gcol), (w["ssd_norm_w"], gw_, gcol)],
        [(2048, gw_, gcol, F32), (2048, gw_, gcol, F32), (2048, gw_, gcol, BF16)],
        [(1, 2048, gw_, gcol), (1, 2048, gw_, gcol)])
    grads["d_skip"] = g_dskip_e.reshape(SSD_HEADS, SSD_HEAD_DIM).sum(axis=1).reshape(1, SSD_HEADS)

    dxs_dir, db_dir, dc_dir, draw, g_bias, g_alog = _ssd_bwd(act["xs"], act["B"], act["C"], small, states, dy_ssd, T)
    grads["dt_bias_f"], grads["dt_bias_b"] = g_bias[0].reshape(1, SSD_HEADS), g_bias[1].reshape(1, SSD_HEADS)
    grads["a_log_f"], grads["a_log_b"] = g_alog[0].reshape(1, SSD_HEADS), g_alog[1].reshape(1, SSD_HEADS)
    d_dt = draw.transpose(2, 0, 1, 3).reshape(T, 2 * SSD_HEADS)
    dproj["dt"] = jnp.pad(d_dt, ((0, 0), (0, CHUNK - 2 * SSD_HEADS))).astype(BF16)

    def make_conv_bwd(nsum):
        def fn(i, xe, *rest):
            ds, (cw, cb) = rest[:nsum], rest[nsum:]
            r = _row_ids(i, T, True)
            dact = ds[0]
            for t in ds[1:]:
                dact = dact + t
            dact = jnp.where((r >= PAD_ROWS) & (r < T), dact, 0.0)
            xe = jnp.where((r >= 0) & (r < T), xe, 0.0)
            pre = _conv3(xe, cw) + cb
            sg = jax.nn.sigmoid(pre)
            dpre = dact * (sg * (1.0 + pre * (1.0 - sg)))
            n = xe.shape[0]
            dpc = _center(dpre)
            dw = jnp.concatenate([jnp.sum(dpc * _center(pltpu.roll(xe, 1, 0)), axis=0, keepdims=True),
                                  jnp.sum(dpc * _center(xe), axis=0, keepdims=True),
                                  jnp.sum(dpc * _center(pltpu.roll(xe, n - 1, 0)), axis=0, keepdims=True)], axis=0)
            return _center(_conv3_t(dpre, cw)), dw, jnp.sum(dpc, axis=0, keepdims=True)
        return fn

    g_cw, g_cb = {}, {}
    cots = {"xs": [(dxs_dir, 512, gcol, 0), (dxs_dir, 512, gcol, 1), (dxs_skip, 512, gcol)],
            "B": [(db_dir, 512, gcol, 0), (db_dir, 512, gcol, 1)],
            "C": [(dc_dir, 512, gcol, 0), (dc_dir, 512, gcol, 1)]}
    for name in ("xs", "B", "C"):
        wd = proj[name].shape[1]
        dproj[name], g_cw[name], g_cb[name] = _rows(
            "ssd_conv_bwd_" + name, make_conv_bwd(len(cots[name])), T, wd // 512,
            [(proj[name], 512, gcol)] + cots[name], [(conv_w[name], 512, gcol), (conv_b[name], 512, gcol)],
            [(wd, 512, gcol, BF16)], [(3, wd, 512, gcol), (1, wd, 512, gcol)], halo=True)
    grads["w_ssd_conv"] = jnp.concatenate([g_cw["xs"], g_cw["B"], g_cw["C"]], axis=1)
    grads["b_ssd_conv"] = jnp.concatenate([g_cb["xs"], g_cb["B"], g_cb["C"]], axis=1)

    da_ret = _mm("d_ret_act", dyr, w["w_ret_out"], "nt")
    grads["w_ret_out"] = _mm("g_ret_out", a_ret, dyr, "tn")
    tick = early_grads({n: grads.pop(n) for n in ("w_ffn_up_t", "w_ret_out", "w_ssd_out", "w_out", "w_ffn_down")})

    def ret_post_bwd_fn(i, y, g, da, gw):
        _, vjp = jax.vjp(_ret_post, y, g, gw)
        return vjp(da)

    dy_ret, dproj["g"], grads["ret_gn_w"] = _rows(
        "ret_post_bwd", ret_post_bwd_fn, T, 1, [(y_ret, 1024, c0), (proj["g"], 1024, c0), (da_ret, 1024, c0)],
        [(w["ret_gn_w"] + tick, 1024, c0)], [(1024, 1024, c0, F32), (1024, 1024, c0, BF16)], [(1, 1024, 1024, c0)])
    dproj["v"] = _retention("retention_dv", kr, qr, dy_ret, T)
    dqr = _retention("retention_dq", dy_ret, proj["v"], kr, T)
    dkr = _retention("retention_dk", proj["v"], dy_ret, qr, T)

    def rot_bwd_fn(i, dq, dk, csv, snv):
        parts = [_rot_t(dq[:, h * 128:(h + 1) * 128], csv, snv) for h in range(RET_HEADS)]
        parts += [_rot_t(dk[:, h * 128:(h + 1) * 128] * scale, csv, snv) for h in range(RET_HEADS)]
        return (jnp.concatenate(parts, axis=1),)

    dproj["qk"] = _rows("rotary_bwd", rot_bwd_fn, T, 1, [(dqr, 512, c0), (dkr, 512, c0), (cs, 128, c0), (sn, 128, c0)],
                        [], [(1024, 1024, c0, BF16)])[0]

    g_in = [_mm("g_in_" + name, dproj[name], u, "tn") for name, _, _ in SEGMENTS]
    g_in[7] = g_in[7][:2 * SSD_HEADS]
    tick = in_grads(jnp.concatenate(g_in, axis=0))
    du = _mm("d_u_dt", dproj["dt"] + tick.astype(BF16), w_in["dt"], "nn")
    for name, _, _ in SEGMENTS:
        if name != "dt":
            du = _mm("d_u_" + name, dproj[name], w_in[name], "nn", add=du)
    dh0, grads["norm_mix_w"] = norm_bwd("norm_mix_bwd", h0, w["norm_mix_w"], du, dh1)
    grads["meta_tokens"] = dh0[PAD_ROWS:CHUNK]
    return loss, dh0[CHUNK:], grads


MESH_ID = pl.DeviceIdType.MESH
ANY = pl.BlockSpec(memory_space=pl.ANY)


def _me_and_peers():
    x, y, c = lax.axis_index("x"), lax.axis_index("y"), lax.axis_index("c")
    peers = []
    for k in range(1, N_DEV):
        px = 1 - x if k & 4 else x
        py = 1 - y if k & 2 else y
        pc = 1 - c if k & 1 else c
        peers.append(((px, py, pc), 4 * px + 2 * py + pc))
    return 4 * x + 2 * y + c, peers


def _push_blocks(name, src, per_peer):
    blk = src.shape[1:] if per_peer else src.shape

    def body(src_ref, out_ref, send_sems, recv_sems, local_sem):
        me, peers = _me_and_peers()
        mine = src_ref.at[me] if per_peer else src_ref
        local = pltpu.make_async_copy(mine, out_ref.at[me], local_sem)
        local.start()
        sends = []
        for k, (dev, idx) in enumerate(peers):
            cp = pltpu.make_async_remote_copy(
                src_ref=src_ref.at[idx] if per_peer else src_ref, dst_ref=out_ref.at[me],
                send_sem=send_sems.at[k], recv_sem=recv_sems.at[k], device_id=dev, device_id_type=MESH_ID)
            cp.start()
            sends.append(cp)
        for k, (dev, idx) in enumerate(peers):
            pltpu.make_async_remote_copy(
                src_ref=mine, dst_ref=out_ref.at[idx], send_sem=send_sems.at[k], recv_sem=recv_sems.at[k],
                device_id=dev, device_id_type=MESH_ID).wait_recv()
        for cp in sends:
            cp.wait_send()
        local.wait()

    return pl.pallas_call(
        body, name=name, in_specs=[ANY], out_specs=ANY,
        out_shape=jax.ShapeDtypeStruct((N_DEV,) + tuple(blk), src.dtype),
        scratch_shapes=[pltpu.SemaphoreType.DMA((N_DEV - 1,)), pltpu.SemaphoreType.DMA((N_DEV - 1,)),
                        pltpu.SemaphoreType.DMA],
    )(src)


HBM = pl.BlockSpec(memory_space=pltpu.HBM)
SEM = pl.BlockSpec(memory_space=pltpu.SEMAPHORE)
EFFECT = pltpu.SideEffectType.DATAFLOW_SIDE_EFFECTING


def _peer_copy(src_ref, land_ref, send_sems, recv_sems, per_peer, me, k, dev, idx, receiving):
    return pltpu.make_async_remote_copy(
        src_ref=src_ref.at[idx] if per_peer else src_ref, dst_ref=land_ref.at[idx if receiving else me],
        send_sem=send_sems.at[k], recv_sem=recv_sems.at[k], device_id=dev, device_id_type=MESH_ID)


def _push_start(name, src, per_peer):
    blk = src.shape[1:] if per_peer else src.shape
    land_shape = (N_DEV,) + tuple(blk)

    def body(src_ref, land_ref, send_sems, recv_sems, src_thru, land_thru, token):
        me, peers = _me_and_peers()
        for k, (dev, idx) in enumerate(peers):
            _peer_copy(src_ref, land_ref, send_sems, recv_sems, per_peer, me, k, dev, idx, False).start()
        token[...] = jnp.zeros_like(token)

    return pl.pallas_call(
        body, name=name,
        out_shape=(pltpu.SemaphoreType.DMA((N_DEV - 1,)), pltpu.SemaphoreType.DMA((N_DEV - 1,)),
                   pltpu.HBM(src.shape, src.dtype), pltpu.HBM(land_shape, src.dtype),
                   jax.ShapeDtypeStruct((8, 128), F32)),
        in_specs=(HBM, HBM), out_specs=(SEM, SEM, HBM, HBM, pl.BlockSpec(memory_space=pltpu.VMEM)),
        input_output_aliases={0: 2, 1: 3}, compiler_params=pltpu.CompilerParams(has_side_effects=EFFECT),
    )(pltpu.with_memory_space_constraint(src, pltpu.HBM),
      pltpu.with_memory_space_constraint(lax.empty(land_shape, src.dtype), pltpu.HBM))


def _push_wait(name, send_sems, recv_sems, src_thru, land_thru, after, per_peer):
    def body(src_ref, land_ref, send_sems, recv_sems, after_ref, src_out, land_out):
        me, peers = _me_and_peers()
        for k, (dev, idx) in enumerate(peers):
            cp = _peer_copy(src_ref, land_ref, send_sems, recv_sems, per_peer, me, k, dev, idx, True)
            cp.wait_send()
            cp.wait_recv()

    return pl.pallas_call(
        body, name=name,
        out_shape=(pltpu.HBM(src_thru.shape, src_thru.dtype), pltpu.HBM(land_thru.shape, land_thru.dtype)),
        in_specs=(HBM, HBM, SEM, SEM, ANY), out_specs=(HBM, HBM), input_output_aliases={0: 0, 1: 1},
        compiler_params=pltpu.CompilerParams(has_side_effects=EFFECT),
    )(src_thru, land_thru, send_sems, recv_sems, after)


def _sum_blocks(name, blocks):
    _, R, C = blocks.shape
    tc = _pick(C, (128,))

    def body(b_ref, o_ref):
        acc = b_ref[0].astype(F32)
        for k in range(1, N_DEV):
            acc = acc + b_ref[k].astype(F32)
        o_ref[...] = acc

    return pl.pallas_call(
        body, name=name, grid=(C // tc,), in_specs=[pl.BlockSpec((N_DEV, R, tc), lambda j: (0, 0, j))],
        out_specs=pl.BlockSpec((R, tc), lambda j: (0, j)), out_shape=jax.ShapeDtypeStruct((R, C), F32),
        compiler_params=_params(("arbitrary",)),
    )(blocks)


def _adamw(name, w, g, m, v):
    R, C = w.shape
    tr = _pick(R, (224, 184, 8))
    spec = pl.BlockSpec((tr, C), lambda i: (i, 0))

    def body(w_ref, g_ref, m_ref, v_ref, d_ref, mo_ref, vo_ref):
        gv = g_ref[...]
        mn = ADAM_B1 * m_ref[...] + (1.0 - ADAM_B1) * gv
        vn = ADAM_B2 * v_ref[...] + (1.0 - ADAM_B2) * jnp.square(gv)
        m_hat = mn / (1.0 - ADAM_B1 ** ADAM_STEP)
        v_hat = vn / (1.0 - ADAM_B2 ** ADAM_STEP)
        d_ref[...] = -ADAM_LR * (m_hat / (jnp.sqrt(v_hat) + ADAM_EPS) + ADAM_WD * w_ref[...])
        mo_ref[...] = mn
        vo_ref[...] = vn

    return pl.pallas_call(
        body, name=name, grid=(R // tr,), in_specs=[spec] * 4, out_specs=[spec] * 3,
        out_shape=[jax.ShapeDtypeStruct((R, C), F32)] * 3, compiler_params=_params(("arbitrary",)),
    )(w, g, m, v)


WEIGHTS = ("meta_tokens", "norm_mix_w", "w_in", "ret_gn_w", "w_ret_out", "w_ssd_conv", "b_ssd_conv", "dt_bias_f",
           "dt_bias_b", "a_log_f", "a_log_b", "d_skip", "ssd_norm_w", "w_ssd_out", "w_out", "norm_ffn_w", "w_ffn_up",
           "w_ffn_conv", "b_ffn_conv", "w_ffn_down", "final_norm_w")
BIG = (("w_in", 1288, True), ("w_ffn_up", 704, True), ("w_ret_out", 128, False), ("w_ssd_out", 256, False),
       ("w_out", 128, False), ("w_ffn_down", 352, False))
REPLICATED = ("norm_mix_w", "ret_gn_w", "b_ssd_conv", "dt_bias_f", "dt_bias_b", "a_log_f", "a_log_b", "d_skip",
              "ssd_norm_w", "norm_ffn_w", "b_ffn_conv", "final_norm_w")
SMALL_SHARDED = (("meta_tokens", 16, 1024), ("w_ssd_conv", 3, 3072), ("w_ffn_conv", 3, 5632))


BIG_IN, BIG_REST = BIG[:1], BIG[1:]


def _pack_big(tree, group):
    parts = []
    for name, _, transposed in group:
        a = tree[name][0]
        parts.append(a.T if transposed else a)
    return jnp.concatenate(parts, axis=0)


def _unpack_big(slab, group):
    out, r0 = {}, 0
    for name, r, transposed in group:
        a = slab[r0:r0 + r]
        out[name] = (a.T if transposed else a)[None]
        r0 += r
    return out


def _pack_flat(arrays, rows):
    flat = jnp.concatenate([a.reshape(-1) for a in arrays])
    return jnp.pad(flat, (0, rows * D_MODEL - flat.shape[0])).reshape(rows, D_MODEL)


def _unpack_flat(slab, shapes):
    flat, out, o = slab.reshape(-1), [], 0
    for s in shapes:
        n = math.prod(s)
        out.append(flat[o:o + n].reshape(s))
        o += n
    return out


def kernel(x, meta_tokens, norm_mix_w, w_in, ret_gn_w, w_ret_out, w_ssd_conv, b_ssd_conv, dt_bias_f, dt_bias_b, a_log_f, a_log_b, d_skip, ssd_norm_w, w_ssd_out, w_out, norm_ffn_w, w_ffn_up, w_ffn_conv, b_ffn_conv, w_ffn_down, final_norm_w, loss_target, m_meta_tokens, m_norm_mix_w, m_w_in, m_ret_gn_w, m_w_ret_out, m_w_ssd_conv, m_b_ssd_conv, m_dt_bias_f, m_dt_bias_b, m_a_log_f, m_a_log_b, m_d_skip, m_ssd_norm_w, m_w_ssd_out, m_w_out, m_norm_ffn_w, m_w_ffn_up, m_w_ffn_conv, m_b_ffn_conv, m_w_ffn_down, m_final_norm_w, v_meta_tokens, v_norm_mix_w, v_w_in, v_ret_gn_w, v_w_ret_out, v_w_ssd_conv, v_b_ssd_conv, v_dt_bias_f, v_dt_bias_b, v_a_log_f, v_a_log_b, v_d_skip, v_ssd_norm_w, v_w_ssd_out, v_w_out, v_norm_ffn_w, v_w_ffn_up, v_w_ffn_conv, v_b_ffn_conv, v_w_ffn_down, v_final_norm_w):
    given = dict(locals())
    wt = {n: given[n] for n in WEIGHTS}
    mt = {n: given["m_" + n] for n in WEIGHTS}
    vt = {n: given["v_" + n] for n in WEIGHTS}
    me = 4 * lax.axis_index("x") + 2 * lax.axis_index("y") + lax.axis_index("c")

    w_slabs = {"in": _pack_big(wt, BIG_IN), "rest": _pack_big(wt, BIG_REST)}
    small_names = [n for n, _, _ in SMALL_SHARDED]
    small_local = lambda tree: [tree[n].reshape(r, c // N_DEV) for n, r, c in SMALL_SHARDED]
    all_in = _push_blocks("gather_w_in", w_slabs["in"].astype(BF16), False)
    rest_flight = _push_start("gather_rest_start", w_slabs["rest"].astype(BF16), False)
    all_s = _push_blocks("gather_small", _pack_flat(small_local(wt), 8), False).reshape(N_DEV, -1)
    full = {"w_in_t": all_in.reshape(-1, D_MODEL)}

    def land_with_own(flight, after, per_peer, name):
        src, land = _push_wait(name, *flight[:4], after, per_peer)
        own = lax.dynamic_slice_in_dim(src, me, 1, axis=0) if per_peer else src[None]
        return lax.dynamic_update_slice_in_dim(land, own, me, axis=0)

    def late_weights(after):
        all_rest = land_with_own(rest_flight, after, False, "gather_rest_wait")
        out, r0 = {}, 0
        for name, r, transposed in BIG_REST:
            out[name + ("_t" if transposed else "")] = all_rest[:, r0:r0 + r].reshape(N_DEV * r, D_MODEL)
            r0 += r
        return out

    flights = {}

    def start_exchange(key, group, gd):
        g_blocks = jnp.concatenate(
            [gd[name + ("_t" if t else "")].reshape(N_DEV, r, D_MODEL) for name, r, t in group], axis=1)
        flights[key] = _push_start("exchange_" + key + "_start", g_blocks.astype(BF16), True)
        return flights[key][4][0, 0]

    o = 0
    for name, r, c in SMALL_SHARDED:
        n = r * c // N_DEV
        full[name] = all_s[:, o:o + n].reshape(N_DEV, r, c // N_DEV).transpose(1, 0, 2).reshape(r, c)
        o += n
    for name in REPLICATED:
        full[name] = wt[name]

    loss, grad_x, g = _local_step(
        x[0], loss_target[0], full, rest_flight[4][0, 0], late_weights,
        lambda gd: start_exchange("rest", BIG_REST, gd), lambda gi: start_exchange("in", BIG_IN, {"w_in_t": gi}))

    g_slabs = {key: _sum_blocks("sum_" + key, land_with_own(flights[key], grad_x, True, "exchange_" + key + "_wait"))
               for key in ("rest", "in")}
    small_parts = [g[n] for n in REPLICATED] + [g[n] for n in small_names] + [loss.reshape(1)]
    g_small = _sum_blocks("sum_small", _push_blocks("gather_small_grads", _pack_flat(small_parts, 64), False))
    small_red = _unpack_flat(g_small, [wt[n].shape for n in REPLICATED] + [(r, c) for _, r, c in SMALL_SHARDED] + [(1,)])
    grads = dict(zip(REPLICATED, small_red[:len(REPLICATED)]))
    for (name, r, c), red in zip(SMALL_SHARDED, small_red[len(REPLICATED):-1]):
        grads[name] = lax.dynamic_slice(red, (0, me * (c // N_DEV)), (r, c // N_DEV)).reshape(wt[name].shape)
    loss_all = small_red[-1][0]
    delta, new_m, new_v = {}, {}, {}
    for key, group in (("in", BIG_IN), ("rest", BIG_REST)):
        grads.update(_unpack_big(g_slabs[key], group))
        d_slab, m_slab, v_slab = _adamw("adamw_" + key, w_slabs[key], g_slabs[key], _pack_big(mt, group),
                                        _pack_big(vt, group))
        delta.update(_unpack_big(d_slab, group))
        new_m.update(_unpack_big(m_slab, group))
        new_v.update(_unpack_big(v_slab, group))

    rest = list(REPLICATED) + small_names
    shapes = [wt[n].shape for n in rest]
    pack_rest = lambda tree: _pack_flat([tree[n] for n in rest], 24)
    d_rest, m_rest, v_rest = _adamw("adamw_small", pack_rest(wt), pack_rest(grads), pack_rest(mt), pack_rest(vt))
    delta.update(zip(rest, _unpack_flat(d_rest, shapes)))
    new_m.update(zip(rest, _unpack_flat(m_rest, shapes)))
    new_v.update(zip(rest, _unpack_flat(v_rest, shapes)))

    return (loss_all, grad_x[None], *[grads[n] for n in WEIGHTS], *[delta[n] for n in WEIGHTS],
            *[new_m[n] for n in WEIGHTS], *[new_v[n] for n in WEIGHTS])
```

```python
import functools
import math

import jax
import jax.numpy as jnp
from jax import lax
from jax.experimental import pallas as pl
from jax.experimental.pallas import tpu as pltpu

F32 = jnp.float32
BF16 = jnp.bfloat16

D_MODEL = 1024
CHUNK = 128
N_META = 16
PAD_ROWS = CHUNK - N_META
RET_HEADS = 4
RET_QK_DIM = 128
RET_V_DIM = 256
SSD_HEADS = 32
SSD_HEAD_DIM = 64
SSD_GROUPS = 4
SSD_STATE = 128
HEADS_PER_GROUP = SSD_HEADS // SSD_GROUPS
PAIRS_PER_GROUP = HEADS_PER_GROUP // 2
D_FF = 2816
EPS = 1e-6
ROPE_BASE = 10000.0
N_DEV = 8

ADAM_LR = 0.001
ADAM_B1 = 0.9
ADAM_B2 = 0.999
ADAM_EPS = 1e-08
ADAM_WD = 0.01
ADAM_STEP = 10

VMEM_LIMIT = 56 * 1024 * 1024
HIGHEST = lax.Precision.HIGHEST

SEGMENTS = (("qk", 0, 1024), ("v", 1024, 2048), ("g", 2048, 3072), ("z", 3072, 5120), ("xs", 5120, 7168),
            ("B", 7168, 7680), ("C", 7680, 8192), ("dt", 8192, 8256), ("gates", 8256, 10304))


def _pick(n, cands):
    for c in cands:
        if n % c == 0:
            return c
    raise ValueError(f"no tile for {n}")


def _params(sem):
    return pltpu.CompilerParams(dimension_semantics=sem, vmem_limit_bytes=VMEM_LIMIT)


def _dot(a, b, dims=(((1,), (0,)), ((), ())), precision=None):
    return lax.dot_general(a, b, dims, preferred_element_type=F32, precision=precision)


def _dot_nt(a, b):
    return _dot(a, b, (((1,), (1,)), ((), ())))


def _dot_tn(a, b):
    return _dot(a, b, (((0,), (0,)), ((), ())))


def _mm(name, a, b, mode, add=None, out_dtype=F32):
    if mode == "nn":
        (M, K), N = a.shape, b.shape[1]
    elif mode == "nt":
        (M, K), N = a.shape, b.shape[0]
    else:
        (K, M), N = a.shape, b.shape[1]
    tm = _pick(M, (384, 256, 1408, 1024, 512, 128, 64, 16))
    tn = _pick(N, (1408, 1024, 512, 128, 64))
    if mode == "tn":
        tk = _pick(K, (1056, 512, 256, 128))
    else:
        tk = K if K <= 2048 else _pick(K, (1408, 1024))
    nk = K // tk
    if mode == "nn":
        a_spec = pl.BlockSpec((tm, tk), lambda n, m, k: (m, k))
        b_spec = pl.BlockSpec((tk, tn), lambda n, m, k: (k, n))
        dims = (((1,), (0,)), ((), ()))
    elif mode == "nt":
        a_spec = pl.BlockSpec((tm, tk), lambda n, m, k: (m, k))
        b_spec = pl.BlockSpec((tn, tk), lambda n, m, k: (n, k))
        dims = (((1,), (1,)), ((), ()))
    else:
        a_spec = pl.BlockSpec((tk, tm), lambda n, m, k: (k, m))
        b_spec = pl.BlockSpec((tk, tn), lambda n, m, k: (k, n))
        dims = (((0,), (0,)), ((), ()))
    o_spec = pl.BlockSpec((tm, tn), lambda n, m, k: (m, n))
    in_specs = [a_spec, b_spec] + ([o_spec] if add is not None else [])
    args = [a, b] + ([add] if add is not None else [])

    def body(*refs):
        if add is not None:
            a_ref, b_ref, r_ref, o_ref, acc = refs
        else:
            a_ref, b_ref, o_ref, acc = refs
        k = pl.program_id(2)
        p = _dot(a_ref[...].astype(BF16), b_ref[...].astype(BF16), dims)

        def finish(r):
            if add is not None:
                r = r + r_ref[...]
            o_ref[...] = r.astype(out_dtype)

        if nk == 1:
            finish(p)
        else:
            @pl.when(k == 0)
            def _():
                acc[...] = p

            @pl.when(k > 0)
            def _():
                acc[...] += p

            @pl.when(k == nk - 1)
            def _():
                finish(acc[...])

    return pl.pallas_call(
        body, name=name, grid=(N // tn, M // tm, nk), in_specs=in_specs, out_specs=o_spec,
        out_shape=jax.ShapeDtypeStruct((M, N), out_dtype),
        scratch_shapes=[pltpu.VMEM((tm, tn) if nk > 1 else (8, 128), F32)],
        compiler_params=_params(("arbitrary", "arbitrary", "arbitrary")),
    )(*args)


def _const(c):
    return lambda j: c


def _rows(name, fn, T, ncol, ins, params, outs, accs=(), halo=False):
    tm = _pick(T, (384, 256, 128))
    R = T // tm
    hb = tm // 8
    in_specs, args = [], []
    for spec in ins:
        arr, w, cf = spec[:3]
        lead = spec[3] if len(spec) > 3 else None
        if lead is None:
            mk = lambda blk, rf, cf=cf: pl.BlockSpec(blk, lambda j, i: (rf(i), cf(j)))
            shape = lambda r, w=w: (r, w)
        else:
            mk = lambda blk, rf, cf=cf, lead=lead: pl.BlockSpec(blk, lambda j, i: (lead, rf(i), cf(j)))
            shape = lambda r, w=w: (None, r, w)
        in_specs.append(mk(shape(tm), lambda i: i))
        args.append(arr)
        if halo:
            in_specs.append(mk(shape(8), lambda i: jnp.maximum(i * hb - 1, 0)))
            in_specs.append(mk(shape(8), lambda i: jnp.minimum((i + 1) * hb, T // 8 - 1)))
            args += [arr, arr]
    for arr, w, cf in params:
        in_specs.append(pl.BlockSpec((arr.shape[0], w), lambda j, i, cf=cf: (0, cf(j))))
        args.append(arr)
    out_shape, out_specs = [], []
    for tw, w, cf, dt in outs:
        out_shape.append(jax.ShapeDtypeStruct((T, tw), dt))
        out_specs.append(pl.BlockSpec((tm, w), lambda j, i, cf=cf: (i, cf(j))))
    for r, tw, w, cf in accs:
        out_shape.append(jax.ShapeDtypeStruct((r, tw), F32))
        out_specs.append(pl.BlockSpec((r, w), lambda j, i, cf=cf: (0, cf(j))))
    n_in, n_par, n_out, n_acc = len(ins), len(params), len(outs), len(accs)

    def body(*refs):
        i = pl.program_id(1)
        vals, p = [], 0
        for _ in range(n_in):
            if halo:
                vals.append(jnp.concatenate([refs[p + 1][...], refs[p][...], refs[p + 2][...]], axis=0))
                p += 3
            else:
                vals.append(refs[p][...])
                p += 1
        pvals = [refs[p + k][...] for k in range(n_par)]
        p += n_par
        res = fn(i, *vals, *pvals)
        for k in range(n_out):
            refs[p + k][...] = res[k].astype(refs[p + k].dtype)
        p += n_out
        for k in range(n_acc):
            ref, v = refs[p + k], res[n_out + k]

            @pl.when(i == 0)
            def _(ref=ref, v=v):
                ref[...] = v

            @pl.when(i > 0)
            def _(ref=ref, v=v):
                ref[...] += v

    res = pl.pallas_call(
        body, name=name, grid=(ncol, R), in_specs=in_specs, out_specs=out_specs, out_shape=out_shape,
        compiler_params=_params(("arbitrary", "arbitrary")),
    )(*args)
    return res


def _tile_rows(T):
    return _pick(T, (384, 256, 128))


def _row_ids(i, T, halo=False):
    tm = _tile_rows(T)
    if halo:
        return i * tm - 8 + lax.broadcasted_iota(jnp.int32, (tm + 16, 1), 0)
    return i * tm + lax.broadcasted_iota(jnp.int32, (tm, 1), 0)


def _rms(x, w):
    return x * lax.rsqrt(jnp.mean(x * x, axis=-1, keepdims=True) + EPS) * w


def _silu(x):
    return x * jax.nn.sigmoid(x)


def _conv3(x, w):
    n = x.shape[0]
    return w[0:1] * pltpu.roll(x, 1, 0) + w[1:2] * x + w[2:3] * pltpu.roll(x, n - 1, 0)


def _conv3_t(d, w):
    n = d.shape[0]
    return w[0:1] * pltpu.roll(d, n - 1, 0) + w[1:2] * d + w[2:3] * pltpu.roll(d, 1, 0)


def _center(x):
    return x[8:x.shape[0] - 8]


def _retention(name, a, b, v, T):
    da = a.shape[1] // RET_HEADS
    dv = v.shape[1] // RET_HEADS
    nc = T // CHUNK
    log_gammas = [math.log(1.0 - 2.0 ** (-5.0 - h)) for h in range(RET_HEADS)]

    def body(a_ref, b_ref, v_ref, o_ref, st):
        h = pl.program_id(0)
        lg = jnp.float32(log_gammas[RET_HEADS - 1])
        for k in range(RET_HEADS - 2, -1, -1):
            lg = jnp.where(h == k, jnp.float32(log_gammas[k]), lg)
        li = lax.broadcasted_iota(jnp.int32, (CHUNK, CHUNK), 0)
        si = lax.broadcasted_iota(jnp.int32, (CHUNK, CHUNK), 1)
        dmat = jnp.exp(lg * jnp.abs(li - si).astype(F32))
        pos = lax.broadcasted_iota(jnp.int32, (CHUNK, 1), 0).astype(F32)
        kdec_f = jnp.exp((CHUNK - 1 - pos) * lg)
        qdec_f = jnp.exp((pos + 1) * lg)
        kdec_b = jnp.exp(pos * lg)
        qdec_b = jnp.exp((CHUNK - pos) * lg)
        cdec = jnp.exp(CHUNK * lg)

        def rows(n):
            return pl.ds(pl.multiple_of(n * CHUNK, CHUNK), CHUNK)

        st[...] = jnp.zeros_like(st)

        def fwd(n, carry):
            r = rows(n)
            av, bv, vv = a_ref[r, :], b_ref[r, :], v_ref[r, :].astype(BF16)
            s = _dot_nt(av.astype(BF16), bv.astype(BF16)) * dmat
            y = _dot(s.astype(BF16), vv) + _dot((av * qdec_f).astype(BF16), st[...].astype(BF16))
            o_ref[r, :] = y
            st[...] = cdec * st[...] + _dot_tn((bv * kdec_f).astype(BF16), vv)
            return carry

        lax.fori_loop(0, nc, fwd, 0)
        st[...] = jnp.zeros_like(st)

        def bwd(m, carry):
            r = rows(nc - 1 - m)
            av, bv, vv = a_ref[r, :], b_ref[r, :], v_ref[r, :].astype(BF16)
            o_ref[r, :] += _dot((av * qdec_b).astype(BF16), st[...].astype(BF16))
            st[...] = cdec * st[...] + _dot_tn((bv * kdec_b).astype(BF16), vv)
            return carry

        lax.fori_loop(0, nc, bwd, 0)

    return pl.pallas_call(
        body, name=name, grid=(RET_HEADS,),
        in_specs=[pl.BlockSpec((T, da), lambda h: (0, h)), pl.BlockSpec((T, da), lambda h: (0, h)),
                  pl.BlockSpec((T, dv), lambda h: (0, h))],
        out_specs=pl.BlockSpec((T, dv), lambda h: (0, h)),
        out_shape=jax.ShapeDtypeStruct((T, RET_HEADS * dv), F32),
        scratch_shapes=[pltpu.VMEM((da, dv), F32)],
        compiler_params=_params(("arbitrary",)),
    )(a, b, v)


def _softplus(x):
    return jnp.maximum(x, 0.0) + jnp.log1p(jnp.exp(-jnp.abs(x)))


def _lane_lo():
    return lax.broadcasted_iota(jnp.int32, (1, CHUNK), 1) < SSD_HEAD_DIM


def _pair_cols(col, j):
    return jnp.where(_lane_lo(), col[:, 2 * j:2 * j + 1], col[:, 2 * j + 1:2 * j + 2])


def _pair_rows(colr, j):
    lo = lax.broadcasted_iota(jnp.int32, (CHUNK, 1), 0) < SSD_HEAD_DIM
    return jnp.where(lo, colr[2 * j:2 * j + 1, :], colr[2 * j + 1:2 * j + 2, :])


def _onehot8(h):
    return (lax.broadcasted_iota(jnp.int32, (1, HEADS_PER_GROUP), 1) == h).astype(F32)


def _ssd_pre(d, c, rawc, rawr, bc, br, alc, alr):
    li = lax.broadcasted_iota(jnp.int32, (CHUNK, CHUNK), 0)
    si = lax.broadcasted_iota(jnp.int32, (CHUNK, CHUNK), 1)
    dif = jnp.where(d == 0, li - si, si - li)
    mask = dif >= 0
    mask_t = dif <= 0
    rowc = c * CHUNK + lax.broadcasted_iota(jnp.int32, (CHUNK, 1), 0)
    rowr = c * CHUNK + lax.broadcasted_iota(jnp.int32, (1, CHUNK), 1)
    dtc = jnp.where(rowc >= PAD_ROWS, _softplus(rawc + bc), 0.0)
    dtr = jnp.where(rowr >= PAD_ROWS, _softplus(rawr + br), 0.0)
    ac = -jnp.exp(alc)
    ar = -jnp.exp(alr)
    dlc = dtc * ac
    dlr = dtr * ar
    alpc = _dot(mask.astype(F32), dlc, precision=HIGHEST)
    alpr = _dot(dlr, mask_t.astype(F32), precision=HIGHEST)
    endc = jnp.sum(dlc, axis=0, keepdims=True)
    endr = jnp.sum(dlr, axis=1, keepdims=True)
    return dict(mask=mask, mask_t=mask_t, dtc=dtc, ac=ac, alpc=alpc, alpr=alpr, endc=endc, endr=endr,
                valid=rowc >= PAD_ROWS)


def _chunk_of(d, n, nc):
    return n + d * (nc - 1 - 2 * n)


def _ssd_small_specs(cfn):
    return [
        pl.BlockSpec((None, None, CHUNK, HEADS_PER_GROUP), lambda d, g, n: (d, g, cfn(d, n), 0)),
        pl.BlockSpec((None, None, HEADS_PER_GROUP, CHUNK), lambda d, g, n: (d, g, 0, cfn(d, n))),
        pl.BlockSpec((None, None, 1, HEADS_PER_GROUP), lambda d, g, n: (d, g, 0, 0)),
        pl.BlockSpec((None, None, HEADS_PER_GROUP, 1), lambda d, g, n: (d, g, 0, 0)),
        pl.BlockSpec((None, None, 1, HEADS_PER_GROUP), lambda d, g, n: (d, g, 0, 0)),
        pl.BlockSpec((None, None, HEADS_PER_GROUP, 1), lambda d, g, n: (d, g, 0, 0)),
    ]


def _ssd_fwd(xs, bm, cm, small, T):
    nc = T // CHUNK
    cfn = lambda d, n: _chunk_of(d, n, nc)

    def body(x_ref, b_ref, c_ref, rawc_ref, rawr_ref, bc_ref, br_ref, alc_ref, alr_ref, y_ref, hs_ref, h_scr):
        d, n = pl.program_id(0), pl.program_id(2)
        c = cfn(d, n)

        @pl.when(n == 0)
        def _():
            h_scr[...] = jnp.zeros_like(h_scr)

        q = _ssd_pre(d, c, rawc_ref[...], rawr_ref[...], bc_ref[...], br_ref[...], alc_ref[...], alr_ref[...])
        bv = b_ref[...].astype(BF16)
        cv = c_ref[...].astype(BF16)
        cb = _dot_nt(cv, bv)
        lo = _lane_lo()
        for j in range(PAIRS_PER_GROUP):
            xp = x_ref[:, j * CHUNK:(j + 1) * CHUNK]
            xd = xp * _pair_cols(q["dtc"], j)
            xdb = xd.astype(BF16)
            yi = []
            for e in range(2):
                h = 2 * j + e
                lm = jnp.exp(jnp.where(q["mask"], q["alpc"][:, h:h + 1] - q["alpr"][h:h + 1, :], -jnp.inf))
                yi.append(_dot((cb * lm).astype(BF16), xdb))
            alp = _pair_cols(q["alpc"], j)
            hp = h_scr[j]
            hs_ref[j] = hp
            yo = jnp.exp(alp) * _dot_nt(cv, hp.astype(BF16))
            y_ref[:, j * CHUNK:(j + 1) * CHUNK] = jnp.where(lo, yi[0], yi[1]) + yo
            de = jnp.exp(_pair_cols(q["endc"], j) - alp)
            h_scr[j] = jnp.exp(_pair_rows(q["endr"], j)) * hp + _dot_tn((xd * de).astype(BF16), bv)

    gw = HEADS_PER_GROUP * SSD_HEAD_DIM
    return pl.pallas_call(
        body, name="ssd_fwd", grid=(2, SSD_GROUPS, nc),
        in_specs=[pl.BlockSpec((CHUNK, gw), lambda d, g, n: (cfn(d, n), g)),
                  pl.BlockSpec((CHUNK, SSD_STATE), lambda d, g, n: (cfn(d, n), g)),
                  pl.BlockSpec((CHUNK, SSD_STATE), lambda d, g, n: (cfn(d, n), g))] + _ssd_small_specs(cfn),
        out_specs=[pl.BlockSpec((None, CHUNK, gw), lambda d, g, n: (d, cfn(d, n), g)),
                   pl.BlockSpec((None, None, None, PAIRS_PER_GROUP, CHUNK, SSD_STATE),
                                lambda d, g, n: (d, g, cfn(d, n), 0, 0, 0))],
        out_shape=[jax.ShapeDtypeStruct((2, T, SSD_HEADS * SSD_HEAD_DIM), F32),
                   jax.ShapeDtypeStruct((2, SSD_GROUPS, nc, PAIRS_PER_GROUP, CHUNK, SSD_STATE), F32)],
        scratch_shapes=[pltpu.VMEM((PAIRS_PER_GROUP, CHUNK, SSD_STATE), F32)],
        compiler_params=_params(("arbitrary", "arbitrary", "arbitrary")),
    )(xs, bm, cm, *small)


def _ssd_bwd(xs, bm, cm, small, hs, dy, T):
    nc = T // CHUNK
    cfn = lambda d, n: _chunk_of(1 - d, n, nc)

    def body(x_ref, b_ref, c_ref, rawc_ref, rawr_ref, bc_ref, br_ref, alc_ref, alr_ref, hs_ref, dy_ref,
             dx_ref, db_ref, dc_ref, draw_ref, dbias_ref, dalog_ref, dh_scr):
        d, n = pl.program_id(0), pl.program_id(2)
        c = cfn(d, n)

        @pl.when(n == 0)
        def _():
            dh_scr[...] = jnp.zeros_like(dh_scr)

        rawc, bc = rawc_ref[...], bc_ref[...]
        q = _ssd_pre(d, c, rawc, rawr_ref[...], bc, br_ref[...], alc_ref[...], alr_ref[...])
        b32, c32 = b_ref[...], c_ref[...]
        bv, cv = b32.astype(BF16), c32.astype(BF16)
        cb = _dot_nt(cv, bv)
        cbt = _dot_nt(bv, cv)
        lo = _lane_lo()
        row_lo = lax.broadcasted_iota(jnp.int32, (CHUNK, 1), 0) < SSD_HEAD_DIM
        dcb = jnp.zeros((CHUNK, CHUNK), F32)
        dcp = jnp.zeros((CHUNK, SSD_STATE), F32)
        dbp = jnp.zeros((CHUNK, SSD_STATE), F32)
        dalp = jnp.zeros((CHUNK, HEADS_PER_GROUP), F32)
        dend = jnp.zeros((1, HEADS_PER_GROUP), F32)
        ddtx = jnp.zeros((CHUNK, HEADS_PER_GROUP), F32)

        def half_sums(t):
            return (jnp.sum(jnp.where(lo, t, 0.0), axis=1, keepdims=True),
                    jnp.sum(jnp.where(lo, 0.0, t), axis=1, keepdims=True))

        for j in range(PAIRS_PER_GROUP):
            xp = x_ref[:, j * CHUNK:(j + 1) * CHUNK]
            dtp = _pair_cols(q["dtc"], j)
            xd = xp * dtp
            xdb = xd.astype(BF16)
            dyp = dy_ref[:, j * CHUNK:(j + 1) * CHUNK]
            dyb = dyp.astype(BF16)
            hn = hs_ref[j]
            hnb = hn.astype(BF16)
            dh1 = dh_scr[j]
            dh1b = dh1.astype(BF16)
            alp = _pair_cols(q["alpc"], j)
            ea = jnp.exp(alp)
            de = jnp.exp(_pair_cols(q["endc"], j) - alp)
            dxi = []
            for e in range(2):
                h = 2 * j + e
                ac_, ar_ = q["alpc"][:, h:h + 1], q["alpr"][h:h + 1, :]
                lm = jnp.exp(jnp.where(q["mask"], ac_ - ar_, -jnp.inf))
                mt = cbt * jnp.exp(jnp.where(q["mask_t"], ar_ - ac_, -jnp.inf))
                dxi.append(_dot(mt.astype(BF16), dyb))
                dyeb_h = (jnp.where(lo, dyp, 0.0) if e == 0 else jnp.where(lo, 0.0, dyp)).astype(BF16)
                gl = _dot_nt(dyeb_h, xdb) * lm
                dcb = dcb + gl
                ra = jnp.sum(gl * cb, axis=1, keepdims=True) - jnp.sum(_dot_nt(xdb, dyeb_h) * mt, axis=1, keepdims=True)
                dalp = dalp + ra * _onehot8(h)
            y_off = ea * _dot_nt(cv, hnb)
            dxs_state = de * _dot_nt(bv, dh1b)
            dxd = jnp.where(lo, dxi[0], dxi[1]) + dxs_state
            dyeb = (dyp * ea).astype(BF16)
            dcp = dcp + _dot(dyeb, hnb)
            dbp = dbp + _dot((xd * de).astype(BF16), dh1b)
            dh_scr[j] = jnp.exp(_pair_rows(q["endr"], j)) * dh1 + _dot_tn(dyeb, cv)
            r0, r1 = half_sums(dyp * y_off - xd * dxs_state)
            dalp = dalp + r0 * _onehot8(2 * j) + r1 * _onehot8(2 * j + 1)
            t0, t1 = half_sums(xd * dxs_state)
            u = jnp.sum(dh1 * hn, axis=1, keepdims=True)
            u0 = jnp.sum(jnp.where(row_lo, u, 0.0), axis=0, keepdims=True)
            u1 = jnp.sum(jnp.where(row_lo, 0.0, u), axis=0, keepdims=True)
            eend = jnp.exp(q["endc"])
            dend = dend + (jnp.sum(t0, axis=0, keepdims=True) + eend * u0) * _onehot8(2 * j) \
                        + (jnp.sum(t1, axis=0, keepdims=True) + eend * u1) * _onehot8(2 * j + 1)
            dx_ref[:, j * CHUNK:(j + 1) * CHUNK] = dxd * dtp
            w0, w1 = half_sums(dxd * xp)
            ddtx = ddtx + w0 * _onehot8(2 * j) + w1 * _onehot8(2 * j + 1)

        dcbb = dcb.astype(BF16)
        dc_ref[...] = dcp + _dot(dcbb, bv)
        db_ref[...] = dbp + _dot_tn(dcbb, cv)
        ddl = _dot(q["mask_t"].astype(F32), dalp, precision=HIGHEST) + dend
        ddt = ddl * q["ac"] + ddtx
        draw = jnp.where(q["valid"], ddt * jax.nn.sigmoid(rawc + bc), 0.0)
        draw_ref[...] = draw
        dbias = jnp.sum(draw, axis=0, keepdims=True)
        dalog = jnp.sum(ddl * q["dtc"], axis=0, keepdims=True) * q["ac"]

        @pl.when(n == 0)
        def _():
            dbias_ref[...] = dbias
            dalog_ref[...] = dalog

        @pl.when(n > 0)
        def _():
            dbias_ref[...] += dbias
            dalog_ref[...] += dalog

    gw = HEADS_PER_GROUP * SSD_HEAD_DIM
    acc_spec = pl.BlockSpec((None, None, 1, HEADS_PER_GROUP), lambda d, g, n: (d, g, 0, 0))
    return pl.pallas_call(
        body, name="ssd_bwd", grid=(2, SSD_GROUPS, nc),
        in_specs=[pl.BlockSpec((CHUNK, gw), lambda d, g, n: (cfn(d, n), g)),
                  pl.BlockSpec((CHUNK, SSD_STATE), lambda d, g, n: (cfn(d, n), g)),
                  pl.BlockSpec((CHUNK, SSD_STATE), lambda d, g, n: (cfn(d, n), g))] + _ssd_small_specs(cfn) + [
                  pl.BlockSpec((None, None, None, PAIRS_PER_GROUP, CHUNK, SSD_STATE),
                               lambda d, g, n: (d, g, cfn(d, n), 0, 0, 0)),
                  pl.BlockSpec((CHUNK, gw), lambda d, g, n: (cfn(d, n), g))],
        out_specs=[pl.BlockSpec((None, CHUNK, gw), lambda d, g, n: (d, cfn(d, n), g)),
                   pl.BlockSpec((None, CHUNK, SSD_STATE), lambda d, g, n: (d, cfn(d, n), g)),
                   pl.BlockSpec((None, CHUNK, SSD_STATE), lambda d, g, n: (d, cfn(d, n), g)),
                   pl.BlockSpec((None, None, CHUNK, HEADS_PER_GROUP), lambda d, g, n: (d, g, cfn(d, n), 0)),
                   acc_spec, acc_spec],
        out_shape=[jax.ShapeDtypeStruct((2, T, SSD_HEADS * SSD_HEAD_DIM), F32),
                   jax.ShapeDtypeStruct((2, T, SSD_GROUPS * SSD_STATE), F32),
                   jax.ShapeDtypeStruct((2, T, SSD_GROUPS * SSD_STATE), F32),
                   jax.ShapeDtypeStruct((2, SSD_GROUPS, T, HEADS_PER_GROUP), F32),
                   jax.ShapeDtypeStruct((2, SSD_GROUPS, 1, HEADS_PER_GROUP), F32),
                   jax.ShapeDtypeStruct((2, SSD_GROUPS, 1, HEADS_PER_GROUP), F32)],
        scratch_shapes=[pltpu.VMEM((PAIRS_PER_GROUP, CHUNK, SSD_STATE), F32)],
        compiler_params=_params(("arbitrary", "arbitrary", "arbitrary")),
    )(xs, bm, cm, *small, hs, dy)


def _rot(x, cs, sn):
    return x * cs + pltpu.roll(x, RET_QK_DIM // 2, 1) * sn


def _rot_t(d, cs, sn):
    return d * cs + pltpu.roll(d * sn, RET_QK_DIM // 2, 1)


def _ret_post(y, g, w):
    parts = []
    for h in range(RET_HEADS):
        yh = y[:, h * RET_V_DIM:(h + 1) * RET_V_DIM]
        mu = jnp.mean(yh, axis=-1, keepdims=True)
        var = jnp.mean(jnp.square(yh - mu), axis=-1, keepdims=True)
        parts.append((yh - mu) * lax.rsqrt(var + EPS))
    return _silu(g) * (jnp.concatenate(parts, axis=1) * w)


def _ssd_post(yf, yb, xs, z, dskip, w):
    y = (yf + yb + xs * dskip) * _silu(z)
    return y * lax.rsqrt(jnp.mean(y * y, axis=-1, keepdims=True) + EPS) * w


def _merge(gates, yr, ys, valid):
    m = jax.nn.sigmoid(gates[:, :D_MODEL]) * yr + jax.nn.sigmoid(gates[:, D_MODEL:]) * ys
    return jnp.where(valid, m, 0.0)


def _rope_tables(T):
    half = RET_QK_DIM // 2
    inv = ROPE_BASE ** (-jnp.arange(half, dtype=F32) / half)
    pos = (jnp.arange(T) - PAD_ROWS).astype(F32)
    ang = pos[:, None] * inv[None, :]
    cos, sin = jnp.cos(ang), jnp.sin(ang)
    return jnp.concatenate([cos, cos], axis=1), jnp.concatenate([-sin, sin], axis=1)


def _per_group(v):
    c = v.reshape(SSD_GROUPS, 1, HEADS_PER_GROUP)
    return c, c.reshape(SSD_GROUPS, HEADS_PER_GROUP, 1)


def _local_step(x, target, w, tick, late_weights, early_grads, in_grads):
    S = x.shape[0]
    T = S + CHUNK
    tm = _tile_rows(T)
    c0 = _const(0)

    h0 = jnp.concatenate([jnp.zeros((PAD_ROWS, D_MODEL), F32), w["meta_tokens"], x], axis=0)
    tgt = jnp.concatenate([jnp.zeros((CHUNK, D_MODEL), F32), target], axis=0)
    w_in = {name: w["w_in_t"][a:b] for name, a, b in SEGMENTS}
    w_in["dt"] = jnp.pad(w_in["dt"], ((0, CHUNK - 2 * SSD_HEADS), (0, 0)))

    def norm_cast(name, h, nw):
        return _rows(name, lambda i, hv, wv: (_rms(hv, wv),), T, 1, [(h, D_MODEL, c0)], [(nw, D_MODEL, c0)],
                     [(D_MODEL, D_MODEL, c0, BF16)])[0]

    u = norm_cast("norm_mix", h0, w["norm_mix_w"] + tick)
    proj = {name: _mm("proj_" + name, u, w_in[name], "nt") for name, _, _ in SEGMENTS}

    cs, sn = _rope_tables(T)
    scale = RET_QK_DIM ** -0.5

    def rot_fn(i, qk, csv, snv):
        q = [_rot(qk[:, h * 128:(h + 1) * 128], csv, snv) for h in range(RET_HEADS)]
        k = [_rot(qk[:, (RET_HEADS + h) * 128:(RET_HEADS + h + 1) * 128], csv, snv) * scale for h in range(RET_HEADS)]
        return jnp.concatenate(q, axis=1), jnp.concatenate(k, axis=1)

    qr, kr = _rows("rotary", rot_fn, T, 1, [(proj["qk"], 1024, c0), (cs, 128, c0), (sn, 128, c0)], [],
                   [(512, 512, c0, F32), (512, 512, c0, F32)])
    y_ret = _retention("retention", qr, kr, proj["v"], T)
    a_ret = _rows("ret_post", lambda i, y, g, gw: (_ret_post(y, g, gw),), T, 1,
                  [(y_ret, 1024, c0), (proj["g"], 1024, c0)], [(w["ret_gn_w"], 1024, c0)],
                  [(1024, 1024, c0, BF16)])[0]

    conv_w = {"xs": w["w_ssd_conv"][:, :2048], "B": w["w_ssd_conv"][:, 2048:2560], "C": w["w_ssd_conv"][:, 2560:]}
    conv_b = {"xs": w["b_ssd_conv"][:, :2048], "B": w["b_ssd_conv"][:, 2048:2560], "C": w["b_ssd_conv"][:, 2560:]}

    def ssd_conv_fn(i, xe, cw, cb):
        r = _row_ids(i, T, True)
        xe = jnp.where((r >= 0) & (r < T), xe, 0.0)
        return (_center(jnp.where(r >= PAD_ROWS, _silu(_conv3(xe, cw) + cb), 0.0)),)

    act = {}
    for name in ("xs", "B", "C"):
        wd = proj[name].shape[1]
        cw = 512
        act[name] = _rows("ssd_conv_" + name, ssd_conv_fn, T, wd // cw, [(proj[name], cw, lambda j: j)],
                          [(conv_w[name], cw, lambda j: j), (conv_b[name], cw, lambda j: j)],
                          [(wd, cw, lambda j: j, F32)], halo=True)[0]

    raw = proj["dt"][:, :2 * SSD_HEADS].reshape(T, 2, SSD_GROUPS, HEADS_PER_GROUP)
    rawc = raw.transpose(1, 2, 0, 3)
    rawr = raw.transpose(1, 2, 3, 0)
    bias = [_per_group(w["dt_bias_f"]), _per_group(w["dt_bias_b"])]
    alog = [_per_group(w["a_log_f"]), _per_group(w["a_log_b"])]
    small = (rawc, rawr, jnp.stack([bias[0][0], bias[1][0]]), jnp.stack([bias[0][1], bias[1][1]]),
             jnp.stack([alog[0][0], alog[1][0]]), jnp.stack([alog[0][1], alog[1][1]]))
    y_dir, states = _ssd_fwd(act["xs"], act["B"], act["C"], small, T)

    dskip_e = jnp.repeat(w["d_skip"], SSD_HEAD_DIM, axis=1)
    gcol = lambda j: j
    gw_ = 512
    a_ssd = _rows("ssd_post", lambda i, yf, yb, xv, zv, dk, nw: (_ssd_post(yf, yb, xv, zv, dk, nw),), T, SSD_GROUPS,
                  [(y_dir, gw_, gcol, 0), (y_dir, gw_, gcol, 1), (act["xs"], gw_, gcol), (proj["z"], gw_, gcol)],
                  [(dskip_e, gw_, gcol), (w["ssd_norm_w"], gw_, gcol)], [(2048, gw_, gcol, BF16)])[0]

    w = dict(w, **late_weights(a_ssd))
    w_up_g, w_up_u = w["w_ffn_up_t"][:D_FF], w["w_ffn_up_t"][D_FF:]
    y_ret_o = _mm("ret_out", a_ret, w["w_ret_out"], "nn")
    y_ssd_o = _mm("ssd_out", a_ssd, w["w_ssd_out"], "nn")

    def merge_fn(i, gates, yr, ys):
        return (_merge(gates, yr, ys, _row_ids(i, T) >= PAD_ROWS),)

    merged = _rows("merge", merge_fn, T, 1, [(proj["gates"], 2048, c0), (y_ret_o, 1024, c0), (y_ssd_o, 1024, c0)], [],
                   [(1024, 1024, c0, BF16)])[0]
    h1 = _mm("mix_out", merged, w["w_out"], "nn", add=h0)

    n2 = norm_cast("norm_ffn", h1, w["norm_ffn_w"])
    fg_pre = _mm("ffn_up_g", n2, w_up_g, "nt")
    fu_pre = _mm("ffn_up_u", n2, w_up_u, "nt")
    cwg, cwu = w["w_ffn_conv"][:, :D_FF], w["w_ffn_conv"][:, D_FF:]
    cbg, cbu = w["b_ffn_conv"][:, :D_FF], w["b_ffn_conv"][:, D_FF:]
    fcol = lambda j: j
    fw = 1408

    def ffn_act_fn(i, ge, ue, wg, wu, bg, bu):
        return (_center(_silu(_conv3(ge, wg) + bg) * (_conv3(ue, wu) + bu)),)

    def ext_valid(i):
        r = _row_ids(i, T, True)
        return (r >= 0) & (r < T)

    def ffn_act_masked(i, ge, ue, wg, wu, bg, bu):
        v = ext_valid(i)
        return ffn_act_fn(i, jnp.where(v, ge, 0.0), jnp.where(v, ue, 0.0), wg, wu, bg, bu)

    a2 = _rows("ffn_act", ffn_act_masked, T, D_FF // fw, [(fg_pre, fw, fcol), (fu_pre, fw, fcol)],
               [(cwg, fw, fcol), (cwu, fw, fcol), (cbg, fw, fcol), (cbu, fw, fcol)], [(D_FF, fw, fcol, BF16)],
               halo=True)[0]
    h2 = _mm("ffn_down", a2, w["w_ffn_down"], "nn", add=h1)

    fnw = w["final_norm_w"].reshape(1, D_MODEL)

    def loss_fn(i, hv, tv, nw):
        valid = _row_ids(i, T) >= CHUNK
        y, vjp = jax.vjp(_rms, hv, nw)
        diff = jnp.where(valid, y - tv, 0.0)
        dh, dw = vjp(diff * (1.0 / D_MODEL))
        part = 0.5 / D_MODEL * jnp.sum(jnp.sum(diff * diff, axis=1, keepdims=True), axis=0, keepdims=True)
        return dh, jnp.broadcast_to(part, (1, 128)), dw

    dh2, loss_acc, d_fnw = _rows("loss", loss_fn, T, 1, [(h2, D_MODEL, c0), (tgt, D_MODEL, c0)], [(fnw, D_MODEL, c0)],
                                 [(D_MODEL, D_MODEL, c0, F32)], [(1, 128, 128, c0), (1, D_MODEL, D_MODEL, c0)])
    loss = loss_acc[0, 0]
    grads = {"final_norm_w": d_fnw.reshape(D_MODEL)}

    da2 = _mm("d_ffn_act", dh2, w["w_ffn_down"], "nt")
    grads["w_ffn_down"] = _mm("g_ffn_down", a2, dh2, "tn")

    def ffn_bwd_fn(i, ge, ue, de, wg, wu, bg, bu):
        v = ext_valid(i)
        ge, ue, de = jnp.where(v, ge, 0.0), jnp.where(v, ue, 0.0), jnp.where(v, de, 0.0)
        fg = _conv3(ge, wg) + bg
        fu = _conv3(ue, wu) + bu
        sg = jax.nn.sigmoid(fg)
        dfg = de * fu * (sg * (1.0 + fg * (1.0 - sg)))
        dfu = de * (fg * sg)
        n = ge.shape[0]

        def wgrad(df, xe):
            df_c = _center(df)
            return jnp.concatenate([jnp.sum(df_c * _center(pltpu.roll(xe, 1, 0)), axis=0, keepdims=True),
                                    jnp.sum(df_c * _center(xe), axis=0, keepdims=True),
                                    jnp.sum(df_c * _center(pltpu.roll(xe, n - 1, 0)), axis=0, keepdims=True)], axis=0)

        return (_center(_conv3_t(dfg, wg)), _center(_conv3_t(dfu, wu)), wgrad(dfg, ge), wgrad(dfu, ue),
                jnp.sum(_center(dfg), axis=0, keepdims=True), jnp.sum(_center(dfu), axis=0, keepdims=True))

    dfg_pre, dfu_pre, g_cwg, g_cwu, g_cbg, g_cbu = _rows(
        "ffn_act_bwd", ffn_bwd_fn, T, D_FF // fw, [(fg_pre, fw, fcol), (fu_pre, fw, fcol), (da2, fw, fcol)],
        [(cwg, fw, fcol), (cwu, fw, fcol), (cbg, fw, fcol), (cbu, fw, fcol)],
        [(D_FF, fw, fcol, BF16), (D_FF, fw, fcol, BF16)],
        [(3, D_FF, fw, fcol), (3, D_FF, fw, fcol), (1, D_FF, fw, fcol), (1, D_FF, fw, fcol)], halo=True)
    grads["w_ffn_conv"] = jnp.concatenate([g_cwg, g_cwu], axis=1)
    grads["b_ffn_conv"] = jnp.concatenate([g_cbg, g_cbu], axis=1)
    dn2 = _mm("d_norm_ffn_g", dfg_pre, w_up_g, "nn")
    dn2 = _mm("d_norm_ffn_u", dfu_pre, w_up_u, "nn", add=dn2)
    grads["w_ffn_up_t"] = jnp.concatenate([_mm("g_ffn_up_g", dfg_pre, n2, "tn"), _mm("g_ffn_up_u", dfu_pre, n2, "tn")],
                                          axis=0)

    def norm_bwd(name, h, nw, dn, dres):
        def fn(i, hv, dnv, drv, wv):
            _, vjp = jax.vjp(_rms, hv, wv)
            dh, dw = vjp(dnv)
            return dh + drv, dw
        return _rows(name, fn, T, 1, [(h, D_MODEL, c0), (dn, D_MODEL, c0), (dres, D_MODEL, c0)], [(nw, D_MODEL, c0)],
                     [(D_MODEL, D_MODEL, c0, F32)], [(1, D_MODEL, D_MODEL, c0)])

    dh1, grads["norm_ffn_w"] = norm_bwd("norm_ffn_bwd", h1, w["norm_ffn_w"], dn2, dh2)

    dmerged = _mm("d_merged", dh1, w["w_out"], "nt")
    grads["w_out"] = _mm("g_out", merged, dh1, "tn")

    def merge_bwd_fn(i, gates, yr, ys, dm):
        valid = _row_ids(i, T) >= PAD_ROWS
        _, vjp = jax.vjp(lambda a, b, c: _merge(a, b, c, valid), gates, yr, ys)
        return vjp(dm)

    dgates, dyr, dys = _rows("merge_bwd", merge_bwd_fn, T, 1,
                             [(proj["gates"], 2048, c0), (y_ret_o, 1024, c0), (y_ssd_o, 1024, c0), (dmerged, 1024, c0)],
                             [], [(2048, 2048, c0, BF16), (1024, 1024, c0, BF16), (1024, 1024, c0, BF16)])
    dproj = {"gates": dgates}

    da_ssd = _mm("d_ssd_act", dys, w["w_ssd_out"], "nt")
    grads["w_ssd_out"] = _mm("g_ssd_out", a_ssd, dys, "tn")

    def ssd_post_bwd_fn(i, yf, yb, xv, zv, da, dk, nw):
        _, vjp = jax.vjp(_ssd_post, yf, yb, xv, zv, dk, nw)
        dyf, _, dxv, dzv, ddk, dnw = vjp(da)
        return dyf, dxv, dzv, ddk, dnw

    dy_ssd, dxs_skip, dproj["z"], g_dskip_e, grads["ssd_norm_w"] = _rows(
        "ssd_post_bwd", ssd_post_bwd_fn, T, SSD_GROUPS,
        [(y_dir, gw_, gcol, 0), (y_dir, gw_, gcol, 1), (act["xs"], gw_, gcol), (proj["z"], gw_, gcol),
         (da_ssd, gw_, gcol)],
        [(dskip_e, gw_, gcol), (w["ssd_norm_w"], gw_, gcol)],
        [(2048, gw_, gcol, F32), (2048, gw_, gcol, F32), (2048, gw_, gcol, BF16)],
        [(1, 2048, gw_, gcol), (1, 2048, gw_, gcol)])
    grads["d_skip"] = g_dskip_e.reshape(SSD_HEADS, SSD_HEAD_DIM).sum(axis=1).reshape(1, SSD_HEADS)

    dxs_dir, db_dir, dc_dir, draw, g_bias, g_alog = _ssd_bwd(act["xs"], act["B"], act["C"], small, states, dy_ssd, T)
    grads["dt_bias_f"], grads["dt_bias_b"] = g_bias[0].reshape(1, SSD_HEADS), g_bias[1].reshape(1, SSD_HEADS)
    grads["a_log_f"], grads["a_log_b"] = g_alog[0].reshape(1, SSD_HEADS), g_alog[1].reshape(1, SSD_HEADS)
    d_dt = draw.transpose(2, 0, 1, 3).reshape(T, 2 * SSD_HEADS)
    dproj["dt"] = jnp.pad(d_dt, ((0, 0), (0, CHUNK - 2 * SSD_HEADS))).astype(BF16)

    def make_conv_bwd(nsum):
        def fn(i, xe, *rest):
            ds, (cw, cb) = rest[:nsum], rest[nsum:]
            r = _row_ids(i, T, True)
            dact = ds[0]
            for t in ds[1:]:
                dact = dact + t
            dact = jnp.where((r >= PAD_ROWS) & (r < T), dact, 0.0)
            xe = jnp.where((r >= 0) & (r < T), xe, 0.0)
            pre = _conv3(xe, cw) + cb
            sg = jax.nn.sigmoid(pre)
            dpre = dact * (sg * (1.0 + pre * (1.0 - sg)))
            n = xe.shape[0]
            dpc = _center(dpre)
            dw = jnp.concatenate([jnp.sum(dpc * _center(pltpu.roll(xe, 1, 0)), axis=0, keepdims=True),
                                  jnp.sum(dpc * _center(xe), axis=0, keepdims=True),
                                  jnp.sum(dpc * _center(pltpu.roll(xe, n - 1, 0)), axis=0, keepdims=True)], axis=0)
            return _center(_conv3_t(dpre, cw)), dw, jnp.sum(dpc, axis=0, keepdims=True)
        return fn

    g_cw, g_cb = {}, {}
    cots = {"xs": [(dxs_dir, 512, gcol, 0), (dxs_dir, 512, gcol, 1), (dxs_skip, 512, gcol)],
            "B": [(db_dir, 512, gcol, 0), (db_dir, 512, gcol, 1)],
            "C": [(dc_dir, 512, gcol, 0), (dc_dir, 512, gcol, 1)]}
    for name in ("xs", "B", "C"):
        wd = proj[name].shape[1]
        dproj[name], g_cw[name], g_cb[name] = _rows(
            "ssd_conv_bwd_" + name, make_conv_bwd(len(cots[name])), T, wd // 512,
            [(proj[name], 512, gcol)] + cots[name], [(conv_w[name], 512, gcol), (conv_b[name], 512, gcol)],
            [(wd, 512, gcol, BF16)], [(3, wd, 512, gcol), (1, wd, 512, gcol)], halo=True)
    grads["w_ssd_conv"] = jnp.concatenate([g_cw["xs"], g_cw["B"], g_cw["C"]], axis=1)
    grads["b_ssd_conv"] = jnp.concatenate([g_cb["xs"], g_cb["B"], g_cb["C"]], axis=1)

    da_ret = _mm("d_ret_act", dyr, w["w_ret_out"], "nt")
    grads["w_ret_out"] = _mm("g_ret_out", a_ret, dyr, "tn")
    tick = early_grads({n: grads.pop(n) for n in ("w_ffn_up_t", "w_ret_out", "w_ssd_out", "w_out", "w_ffn_down")})

    def ret_post_bwd_fn(i, y, g, da, gw):
        _, vjp = jax.vjp(_ret_post, y, g, gw)
        return vjp(da)

    dy_ret, dproj["g"], grads["ret_gn_w"] = _rows(
        "ret_post_bwd", ret_post_bwd_fn, T, 1, [(y_ret, 1024, c0), (proj["g"], 1024, c0), (da_ret, 1024, c0)],
        [(w["ret_gn_w"] + tick, 1024, c0)], [(1024, 1024, c0, F32), (1024, 1024, c0, BF16)], [(1, 1024, 1024, c0)])
    dproj["v"] = _retention("retention_dv", kr, qr, dy_ret, T)
    dqr = _retention("retention_dq", dy_ret, proj["v"], kr, T)
    dkr = _retention("retention_dk", proj["v"], dy_ret, qr, T)

    def rot_bwd_fn(i, dq, dk, csv, snv):
        parts = [_rot_t(dq[:, h * 128:(h + 1) * 128], csv, snv) for h in range(RET_HEADS)]
        parts += [_rot_t(dk[:, h * 128:(h + 1) * 128] * scale, csv, snv) for h in range(RET_HEADS)]
        return (jnp.concatenate(parts, axis=1),)

    dproj["qk"] = _rows("rotary_bwd", rot_bwd_fn, T, 1, [(dqr, 512, c0), (dkr, 512, c0), (cs, 128, c0), (sn, 128, c0)],
                        [], [(1024, 1024, c0, BF16)])[0]

    g_in = [_mm("g_in_" + name, dproj[name], u, "tn") for name, _, _ in SEGMENTS]
    g_in[7] = g_in[7][:2 * SSD_HEADS]
    tick = in_grads(jnp.concatenate(g_in, axis=0))
    du = _mm("d_u_dt", dproj["dt"] + tick.astype(BF16), w_in["dt"], "nn")
    for name, _, _ in SEGMENTS:
        if name != "dt":
            du = _mm("d_u_" + name, dproj[name], w_in[name], "nn", add=du)
    dh0, grads["norm_mix_w"] = norm_bwd("norm_mix_bwd", h0, w["norm_mix_w"], du, dh1)
    grads["meta_tokens"] = dh0[PAD_ROWS:CHUNK]
    return loss, dh0[CHUNK:], grads


MESH_ID = pl.DeviceIdType.MESH
ANY = pl.BlockSpec(memory_space=pl.ANY)


def _me_and_peers():
    x, y, c = lax.axis_index("x"), lax.axis_index("y"), lax.axis_index("c")
    peers = []
    for k in range(1, N_DEV):
        px = 1 - x if k & 4 else x
        py = 1 - y if k & 2 else y
        pc = 1 - c if k & 1 else c
        peers.append(((px, py, pc), 4 * px + 2 * py + pc))
    return 4 * x + 2 * y + c, peers


def _push_blocks(name, src, per_peer):
    blk = src.shape[1:] if per_peer else src.shape

    def body(src_ref, out_ref, send_sems, recv_sems, local_sem):
        me, peers = _me_and_peers()
        mine = src_ref.at[me] if per_peer else src_ref
        local = pltpu.make_async_copy(mine, out_ref.at[me], local_sem)
        local.start()
        sends = []
        for k, (dev, idx) in enumerate(peers):
            cp = pltpu.make_async_remote_copy(
                src_ref=src_ref.at[idx] if per_peer else src_ref, dst_ref=out_ref.at[me],
                send_sem=send_sems.at[k], recv_sem=recv_sems.at[k], device_id=dev, device_id_type=MESH_ID)
            cp.start()
            sends.append(cp)
        for k, (dev, idx) in enumerate(peers):
            pltpu.make_async_remote_copy(
                src_ref=mine, dst_ref=out_ref.at[idx], send_sem=send_sems.at[k], recv_sem=recv_sems.at[k],
                device_id=dev, device_id_type=MESH_ID).wait_recv()
        for cp in sends:
            cp.wait_send()
        local.wait()

    return pl.pallas_call(
        body, name=name, in_specs=[ANY], out_specs=ANY,
        out_shape=jax.ShapeDtypeStruct((N_DEV,) + tuple(blk), src.dtype),
        scratch_shapes=[pltpu.SemaphoreType.DMA((N_DEV - 1,)), pltpu.SemaphoreType.DMA((N_DEV - 1,)),
                        pltpu.SemaphoreType.DMA],
    )(src)


def _gather_two_level(name, src):
    def body(x_ref, out_ref, send_sems, recv_sems, local_sem):
        x, y, c = lax.axis_index("x"), lax.axis_index("y"), lax.axis_index("c")
        me, sibling = (x, y, c), (x, y, 1 - c)
        chips = [(1 - x, y), (x, 1 - y), (1 - x, 1 - y)]

        def rows(px, py, pc):
            return out_ref.at[4 * px + 2 * py + pc]

        def copy(k, block, to, src_ref=None):
            return pltpu.make_async_remote_copy(
                src_ref=rows(*block) if src_ref is None else src_ref, dst_ref=rows(*block),
                send_sem=send_sems.at[k], recv_sem=recv_sems.at[k], device_id=to, device_id_type=MESH_ID)

        mine = pltpu.make_async_copy(x_ref, rows(*me), local_sem)
        mine.start()
        first = [copy(0, me, sibling, x_ref)] + [copy(1 + j, me, (*chip, c), x_ref) for j, chip in enumerate(chips)]
        for cp in first:
            cp.start()
        passed = [copy(4 + j, (*chip, c), sibling) for j, chip in enumerate(chips)]
        for j, chip in enumerate(chips):
            copy(1 + j, (*chip, c), me).wait_recv()
            passed[j].start()
        copy(0, sibling, me).wait_recv()
        for j, chip in enumerate(chips):
            copy(4 + j, (*chip, 1 - c), me).wait_recv()
        for cp in first + passed:
            cp.wait_send()
        mine.wait()

    return pl.pallas_call(
        body, name=name, in_specs=[ANY], out_specs=ANY,
        out_shape=jax.ShapeDtypeStruct((N_DEV,) + tuple(src.shape), src.dtype),
        scratch_shapes=[pltpu.SemaphoreType.DMA((N_DEV - 1,)), pltpu.SemaphoreType.DMA((N_DEV - 1,)),
                        pltpu.SemaphoreType.DMA],
    )(src)


HBM = pl.BlockSpec(memory_space=pltpu.HBM)
SEM = pl.BlockSpec(memory_space=pltpu.SEMAPHORE)
EFFECT = pltpu.SideEffectType.DATAFLOW_SIDE_EFFECTING


def _peer_copy(src_ref, land_ref, send_sems, recv_sems, per_peer, me, k, dev, idx, receiving):
    return pltpu.make_async_remote_copy(
        src_ref=src_ref.at[idx] if per_peer else src_ref, dst_ref=land_ref.at[idx if receiving else me],
        send_sem=send_sems.at[k], recv_sem=recv_sems.at[k], device_id=dev, device_id_type=MESH_ID)


def _push_start(name, src, per_peer):
    blk = src.shape[1:] if per_peer else src.shape
    land_shape = (N_DEV,) + tuple(blk)

    def body(src_ref, land_ref, send_sems, recv_sems, src_thru, land_thru, token):
        me, peers = _me_and_peers()
        for k, (dev, idx) in enumerate(peers):
            _peer_copy(src_ref, land_ref, send_sems, recv_sems, per_peer, me, k, dev, idx, False).start()
        token[...] = jnp.zeros_like(token)

    return pl.pallas_call(
        body, name=name,
        out_shape=(pltpu.SemaphoreType.DMA((N_DEV - 1,)), pltpu.SemaphoreType.DMA((N_DEV - 1,)),
                   pltpu.HBM(src.shape, src.dtype), pltpu.HBM(land_shape, src.dtype),
                   jax.ShapeDtypeStruct((8, 128), F32)),
        in_specs=(HBM, HBM), out_specs=(SEM, SEM, HBM, HBM, pl.BlockSpec(memory_space=pltpu.VMEM)),
        input_output_aliases={0: 2, 1: 3}, compiler_params=pltpu.CompilerParams(has_side_effects=EFFECT),
    )(pltpu.with_memory_space_constraint(src, pltpu.HBM),
      pltpu.with_memory_space_constraint(lax.empty(land_shape, src.dtype), pltpu.HBM))


def _push_wait(name, send_sems, recv_sems, src_thru, land_thru, after, per_peer):
    def body(src_ref, land_ref, send_sems, recv_sems, after_ref, src_out, land_out):
        me, peers = _me_and_peers()
        for k, (dev, idx) in enumerate(peers):
            cp = _peer_copy(src_ref, land_ref, send_sems, recv_sems, per_peer, me, k, dev, idx, True)
            cp.wait_send()
            cp.wait_recv()

    return pl.pallas_call(
        body, name=name,
        out_shape=(pltpu.HBM(src_thru.shape, src_thru.dtype), pltpu.HBM(land_thru.shape, land_thru.dtype)),
        in_specs=(HBM, HBM, SEM, SEM, ANY), out_specs=(HBM, HBM), input_output_aliases={0: 0, 1: 1},
        compiler_params=pltpu.CompilerParams(has_side_effects=EFFECT),
    )(src_thru, land_thru, send_sems, recv_sems, after)


def _sum_blocks(name, blocks):
    _, R, C = blocks.shape
    tc = _pick(C, (128,))

    def body(b_ref, o_ref):
        acc = b_ref[0].astype(F32)
        for k in range(1, N_DEV):
            acc = acc + b_ref[k].astype(F32)
        o_ref[...] = acc

    return pl.pallas_call(
        body, name=name, grid=(C // tc,), in_specs=[pl.BlockSpec((N_DEV, R, tc), lambda j: (0, 0, j))],
        out_specs=pl.BlockSpec((R, tc), lambda j: (0, j)), out_shape=jax.ShapeDtypeStruct((R, C), F32),
        compiler_params=_params(("arbitrary",)),
    )(blocks)


def _adamw(name, w, g, m, v):
    R, C = w.shape
    tr = _pick(R, (224, 184, 8))
    spec = pl.BlockSpec((tr, C), lambda i: (i, 0))

    def body(w_ref, g_ref, m_ref, v_ref, d_ref, mo_ref, vo_ref):
        gv = g_ref[...]
        mn = ADAM_B1 * m_ref[...] + (1.0 - ADAM_B1) * gv
        vn = ADAM_B2 * v_ref[...] + (1.0 - ADAM_B2) * jnp.square(gv)
        m_hat = mn / (1.0 - ADAM_B1 ** ADAM_STEP)
        v_hat = vn / (1.0 - ADAM_B2 ** ADAM_STEP)
        d_ref[...] = -ADAM_LR * (m_hat / (jnp.sqrt(v_hat) + ADAM_EPS) + ADAM_WD * w_ref[...])
        mo_ref[...] = mn
        vo_ref[...] = vn

    return pl.pallas_call(
        body, name=name, grid=(R // tr,), in_specs=[spec] * 4, out_specs=[spec] * 3,
        out_shape=[jax.ShapeDtypeStruct((R, C), F32)] * 3, compiler_params=_params(("arbitrary",)),
    )(w, g, m, v)


WEIGHTS = ("meta_tokens", "norm_mix_w", "w_in", "ret_gn_w", "w_ret_out", "w_ssd_conv", "b_ssd_conv", "dt_bias_f",
           "dt_bias_b", "a_log_f", "a_log_b", "d_skip", "ssd_norm_w", "w_ssd_out", "w_out", "norm_ffn_w", "w_ffn_up",
           "w_ffn_conv", "b_ffn_conv", "w_ffn_down", "final_norm_w")
BIG = (("w_in", 1288, True), ("w_ffn_up", 704, True), ("w_ret_out", 128, False), ("w_ssd_out", 256, False),
       ("w_out", 128, False), ("w_ffn_down", 352, False))
REPLICATED = ("norm_mix_w", "ret_gn_w", "b_ssd_conv", "dt_bias_f", "dt_bias_b", "a_log_f", "a_log_b", "d_skip",
              "ssd_norm_w", "norm_ffn_w", "b_ffn_conv", "final_norm_w")
SMALL_SHARDED = (("meta_tokens", 16, 1024), ("w_ssd_conv", 3, 3072), ("w_ffn_conv", 3, 5632))


BIG_IN, BIG_REST = BIG[:1], BIG[1:]


def _pack_big(tree, group):
    parts = []
    for name, _, transposed in group:
        a = tree[name][0]
        parts.append(a.T if transposed else a)
    return jnp.concatenate(parts, axis=0)


def _unpack_big(slab, group):
    out, r0 = {}, 0
    for name, r, transposed in group:
        a = slab[r0:r0 + r]
        out[name] = (a.T if transposed else a)[None]
        r0 += r
    return out


def _pack_flat(arrays, rows):
    flat = jnp.concatenate([a.reshape(-1) for a in arrays])
    return jnp.pad(flat, (0, rows * D_MODEL - flat.shape[0])).reshape(rows, D_MODEL)


def _unpack_flat(slab, shapes):
    flat, out, o = slab.reshape(-1), [], 0
    for s in shapes:
        n = math.prod(s)
        out.append(flat[o:o + n].reshape(s))
        o += n
    return out


def kernel(x, meta_tokens, norm_mix_w, w_in, ret_gn_w, w_ret_out, w_ssd_conv, b_ssd_conv, dt_bias_f, dt_bias_b, a_log_f, a_log_b, d_skip, ssd_norm_w, w_ssd_out, w_out, norm_ffn_w, w_ffn_up, w_ffn_conv, b_ffn_conv, w_ffn_down, final_norm_w, loss_target, m_meta_tokens, m_norm_mix_w, m_w_in, m_ret_gn_w, m_w_ret_out, m_w_ssd_conv, m_b_ssd_conv, m_dt_bias_f, m_dt_bias_b, m_a_log_f, m_a_log_b, m_d_skip, m_ssd_norm_w, m_w_ssd_out, m_w_out, m_norm_ffn_w, m_w_ffn_up, m_w_ffn_conv, m_b_ffn_conv, m_w_ffn_down, m_final_norm_w, v_meta_tokens, v_norm_mix_w, v_w_in, v_ret_gn_w, v_w_ret_out, v_w_ssd_conv, v_b_ssd_conv, v_dt_bias_f, v_dt_bias_b, v_a_log_f, v_a_log_b, v_d_skip, v_ssd_norm_w, v_w_ssd_out, v_w_out, v_norm_ffn_w, v_w_ffn_up, v_w_ffn_conv, v_b_ffn_conv, v_w_ffn_down, v_final_norm_w):
    given = dict(locals())
    wt = {n: given[n] for n in WEIGHTS}
    mt = {n: given["m_" + n] for n in WEIGHTS}
    vt = {n: given["v_" + n] for n in WEIGHTS}
    me = 4 * lax.axis_index("x") + 2 * lax.axis_index("y") + lax.axis_index("c")

    w_slabs = {"in": _pack_big(wt, BIG_IN), "rest": _pack_big(wt, BIG_REST)}
    small_names = [n for n, _, _ in SMALL_SHARDED]
    small_local = lambda tree: [tree[n].reshape(r, c // N_DEV) for n, r, c in SMALL_SHARDED]
    all_in = _gather_two_level("gather_w_in", w_slabs["in"].astype(BF16))
    rest_src, all_in = lax.optimization_barrier((w_slabs["rest"].astype(BF16), all_in))
    rest_flight = _push_start("gather_rest_start", rest_src, False)
    all_s = _push_blocks("gather_small", _pack_flat(small_local(wt), 8), False).reshape(N_DEV, -1)
    full = {"w_in_t": all_in.reshape(-1, D_MODEL)}

    def land_with_own(flight, after, per_peer, name):
        src, land = _push_wait(name, *flight[:4], after, per_peer)
        own = lax.dynamic_slice_in_dim(src, me, 1, axis=0) if per_peer else src[None]
        return lax.dynamic_update_slice_in_dim(land, own, me, axis=0)

    def late_weights(after):
        all_rest = land_with_own(rest_flight, after, False, "gather_rest_wait")
        out, r0 = {}, 0
        for name, r, transposed in BIG_REST:
            out[name + ("_t" if transposed else "")] = all_rest[:, r0:r0 + r].reshape(N_DEV * r, D_MODEL)
            r0 += r
        return out

    flights = {}

    def start_exchange(key, group, gd):
        g_blocks = jnp.concatenate(
            [gd[name + ("_t" if t else "")].reshape(N_DEV, r, D_MODEL) for name, r, t in group], axis=1)
        flights[key] = _push_start("exchange_" + key + "_start", g_blocks.astype(BF16), True)
        return flights[key][4][0, 0]

    o = 0
    for name, r, c in SMALL_SHARDED:
        n = r * c // N_DEV
        full[name] = all_s[:, o:o + n].reshape(N_DEV, r, c // N_DEV).transpose(1, 0, 2).reshape(r, c)
        o += n
    for name in REPLICATED:
        full[name] = wt[name]

    loss, grad_x, g = _local_step(
        x[0], loss_target[0], full, rest_flight[4][0, 0], late_weights,
        lambda gd: start_exchange("rest", BIG_REST, gd), lambda gi: start_exchange("in", BIG_IN, {"w_in_t": gi}))

    g_slabs = {key: _sum_blocks("sum_" + key, land_with_own(flights[key], grad_x, True, "exchange_" + key + "_wait"))
               for key in ("rest", "in")}
    small_parts = [g[n] for n in REPLICATED] + [g[n] for n in small_names] + [loss.reshape(1)]
    g_small = _sum_blocks("sum_small", _push_blocks("gather_small_grads", _pack_flat(small_parts, 64), False))
    small_red = _unpack_flat(g_small, [wt[n].shape for n in REPLICATED] + [(r, c) for _, r, c in SMALL_SHARDED] + [(1,)])
    grads = dict(zip(REPLICATED, small_red[:len(REPLICATED)]))
    for (name, r, c), red in zip(SMALL_SHARDED, small_red[len(REPLICATED):-1]):
        grads[name] = lax.dynamic_slice(red, (0, me * (c // N_DEV)), (r, c // N_DEV)).reshape(wt[name].shape)
    loss_all = small_red[-1][0]
    delta, new_m, new_v = {}, {}, {}
    for key, group in (("in", BIG_IN), ("rest", BIG_REST)):
        grads.update(_unpack_big(g_slabs[key], group))
        d_slab, m_slab, v_slab = _adamw("adamw_" + key, w_slabs[key], g_slabs[key], _pack_big(mt, group),
                                        _pack_big(vt, group))
        delta.update(_unpack_big(d_slab, group))
        new_m.update(_unpack_big(m_slab, group))
        new_v.update(_unpack_big(v_slab, group))

    rest = list(REPLICATED) + small_names
    shapes = [wt[n].shape for n in rest]
    pack_rest = lambda tree: _pack_flat([tree[n] for n in rest], 24)
    d_rest, m_rest, v_rest = _adamw("adamw_small", pack_rest(wt), pack_rest(grads), pack_rest(mt), pack_rest(vt))
    delta.update(zip(rest, _unpack_flat(d_rest, shapes)))
    new_m.update(zip(rest, _unpack_flat(m_rest, shapes)))
    new_v.update(zip(rest, _unpack_flat(v_rest, shapes)))

    return (loss_all, grad_x[None], *[grads[n] for n in WEIGHTS], *[delta[n] for n in WEIGHTS],
            *[new_m[n] for n in WEIGHTS], *[new_v[n] for n in WEIGHTS])
```

```python
import functools
import math

import jax
import jax.numpy as jnp
from jax import lax
from jax.experimental import pallas as pl
from jax.experimental.pallas import tpu as pltpu

F32 = jnp.float32
BF16 = jnp.bfloat16

D_MODEL = 1024
CHUNK = 128
N_META = 16
PAD_ROWS = CHUNK - N_META
RET_HEADS = 4
RET_QK_DIM = 128
RET_V_DIM = 256
SSD_HEADS = 32
SSD_HEAD_DIM = 64
SSD_GROUPS = 4
SSD_STATE = 128
HEADS_PER_GROUP = SSD_HEADS // SSD_GROUPS
PAIRS_PER_GROUP = HEADS_PER_GROUP // 2
D_FF = 2816
EPS = 1e-6
ROPE_BASE = 10000.0
N_DEV = 8

ADAM_LR = 0.001
ADAM_B1 = 0.9
ADAM_B2 = 0.999
ADAM_EPS = 1e-08
ADAM_WD = 0.01
ADAM_STEP = 10

VMEM_LIMIT = 56 * 1024 * 1024
HALO = 16
HIGHEST = lax.Precision.HIGHEST

SEGMENTS = (("qk", 0, 1024), ("v", 1024, 2048), ("g", 2048, 3072), ("z", 3072, 5120), ("xs", 5120, 7168),
            ("B", 7168, 7680), ("C", 7680, 8192), ("dt", 8192, 8256), ("gates", 8256, 10304))


def _pick(n, cands):
    for c in cands:
        if n % c == 0:
            return c
    raise ValueError(f"no tile for {n}")


def _params(sem):
    return pltpu.CompilerParams(dimension_semantics=sem, vmem_limit_bytes=VMEM_LIMIT)


def _dot(a, b, dims=(((1,), (0,)), ((), ())), precision=None):
    return lax.dot_general(a, b, dims, preferred_element_type=F32, precision=precision)


def _dot_nt(a, b):
    return _dot(a, b, (((1,), (1,)), ((), ())))


def _dot_tn(a, b):
    return _dot(a, b, (((0,), (0,)), ((), ())))


def _mm(name, a, b, mode, add=None, out_dtype=F32):
    if mode == "nn":
        (M, K), N = a.shape, b.shape[1]
    elif mode == "nt":
        (M, K), N = a.shape, b.shape[0]
    else:
        (K, M), N = a.shape, b.shape[1]
    tn = _pick(N, (1408, 1024, 512, 128, 64))
    if mode == "tn":
        tm = M if M <= 1024 else _pick(M, (1408, 1024))
        tk = _pick(K, (1056, 512, 256, 128))
    else:
        tm = _pick(M, (384, 256, 128))
        tk = K if K <= 2048 else _pick(K, (1408, 1024))
    nk = K // tk
    if mode == "nn":
        a_spec = pl.BlockSpec((tm, tk), lambda n, m, k: (m, k))
        b_spec = pl.BlockSpec((tk, tn), lambda n, m, k: (k, n))
        dims = (((1,), (0,)), ((), ()))
    elif mode == "nt":
        a_spec = pl.BlockSpec((tm, tk), lambda n, m, k: (m, k))
        b_spec = pl.BlockSpec((tn, tk), lambda n, m, k: (n, k))
        dims = (((1,), (1,)), ((), ()))
    else:
        a_spec = pl.BlockSpec((tk, tm), lambda n, m, k: (k, m))
        b_spec = pl.BlockSpec((tk, tn), lambda n, m, k: (k, n))
        dims = (((0,), (0,)), ((), ()))
    o_spec = pl.BlockSpec((tm, tn), lambda n, m, k: (m, n))
    in_specs = [a_spec, b_spec] + ([o_spec] if add is not None else [])
    args = [a, b] + ([add] if add is not None else [])

    def body(*refs):
        if add is not None:
            a_ref, b_ref, r_ref, o_ref, acc = refs
        else:
            a_ref, b_ref, o_ref, acc = refs
        k = pl.program_id(2)
        p = _dot(a_ref[...].astype(BF16), b_ref[...].astype(BF16), dims)

        def finish(r):
            if add is not None:
                r = r + r_ref[...]
            o_ref[...] = r.astype(out_dtype)

        if nk == 1:
            finish(p)
        else:
            @pl.when(k == 0)
            def _():
                acc[...] = p

            @pl.when(k > 0)
            def _():
                acc[...] += p

            @pl.when(k == nk - 1)
            def _():
                finish(acc[...])

    return pl.pallas_call(
        body, name=name, grid=(N // tn, M // tm, nk), in_specs=in_specs, out_specs=o_spec,
        out_shape=jax.ShapeDtypeStruct((M, N), out_dtype),
        scratch_shapes=[pltpu.VMEM((tm, tn) if nk > 1 else (8, 128), F32)],
        compiler_params=_params(("arbitrary", "arbitrary", "arbitrary")),
    )(*args)


def _const(c):
    return lambda j: c


def _rows(name, fn, T, ncol, ins, params, outs, accs=(), halo=False):
    tm = _pick(T, (384, 256, 128))
    R = T // tm
    hb = tm // HALO
    in_specs, args = [], []
    for spec in ins:
        arr, w, cf = spec[:3]
        lead = spec[3] if len(spec) > 3 else None
        if lead is None:
            mk = lambda blk, rf, cf=cf: pl.BlockSpec(blk, lambda j, i: (rf(i), cf(j)))
            shape = lambda r, w=w: (r, w)
        else:
            mk = lambda blk, rf, cf=cf, lead=lead: pl.BlockSpec(blk, lambda j, i: (lead, rf(i), cf(j)))
            shape = lambda r, w=w: (None, r, w)
        in_specs.append(mk(shape(tm), lambda i: i))
        args.append(arr)
        if halo:
            in_specs.append(mk(shape(HALO), lambda i: jnp.maximum(i * hb - 1, 0)))
            in_specs.append(mk(shape(HALO), lambda i: jnp.minimum((i + 1) * hb, T // HALO - 1)))
            args += [arr, arr]
    for arr, w, cf in params:
        in_specs.append(pl.BlockSpec((arr.shape[0], w), lambda j, i, cf=cf: (0, cf(j))))
        args.append(arr)
    out_shape, out_specs = [], []
    for tw, w, cf, dt in outs:
        out_shape.append(jax.ShapeDtypeStruct((T, tw), dt))
        out_specs.append(pl.BlockSpec((tm, w), lambda j, i, cf=cf: (i, cf(j))))
    for r, tw, w, cf in accs:
        out_shape.append(jax.ShapeDtypeStruct((r, tw), F32))
        out_specs.append(pl.BlockSpec((r, w), lambda j, i, cf=cf: (0, cf(j))))
    n_in, n_par, n_out, n_acc = len(ins), len(params), len(outs), len(accs)

    def body(*refs):
        i = pl.program_id(1)
        vals, p = [], 0
        for _ in range(n_in):
            if halo:
                vals.append(jnp.concatenate([refs[p + 1][...], refs[p][...], refs[p + 2][...]], axis=0).astype(F32))
                p += 3
            else:
                vals.append(refs[p][...].astype(F32))
                p += 1
        pvals = [refs[p + k][...] for k in range(n_par)]
        p += n_par
        res = fn(i, *vals, *pvals)
        for k in range(n_out):
            refs[p + k][...] = res[k].astype(refs[p + k].dtype)
        p += n_out
        for k in range(n_acc):
            ref, v = refs[p + k], res[n_out + k]

            @pl.when(i == 0)
            def _(ref=ref, v=v):
                ref[...] = v

            @pl.when(i > 0)
            def _(ref=ref, v=v):
                ref[...] += v

    res = pl.pallas_call(
        body, name=name, grid=(ncol, R), in_specs=in_specs, out_specs=out_specs, out_shape=out_shape,
        compiler_params=_params(("arbitrary", "arbitrary")),
    )(*args)
    return res


def _tile_rows(T):
    return _pick(T, (384, 256, 128))


def _row_ids(i, T, halo=False):
    tm = _tile_rows(T)
    if halo:
        return i * tm - HALO + lax.broadcasted_iota(jnp.int32, (tm + 2 * HALO, 1), 0)
    return i * tm + lax.broadcasted_iota(jnp.int32, (tm, 1), 0)


def _rms(x, w):
    return x * lax.rsqrt(jnp.mean(x * x, axis=-1, keepdims=True) + EPS) * w


def _silu(x):
    return x * jax.nn.sigmoid(x)


def _conv3(x, w):
    n = x.shape[0]
    return w[0:1] * pltpu.roll(x, 1, 0) + w[1:2] * x + w[2:3] * pltpu.roll(x, n - 1, 0)


def _conv3_t(d, w):
    n = d.shape[0]
    return w[0:1] * pltpu.roll(d, n - 1, 0) + w[1:2] * d + w[2:3] * pltpu.roll(d, 1, 0)


def _center(x):
    return x[HALO:x.shape[0] - HALO]


def _retention(name, a, b, v, T):
    da = a.shape[1] // RET_HEADS
    dv = v.shape[1] // RET_HEADS
    nc = T // CHUNK
    log_gammas = [math.log(1.0 - 2.0 ** (-5.0 - h)) for h in range(RET_HEADS)]

    def body(a_ref, b_ref, v_ref, o_ref, st):
        h = pl.program_id(0)
        lg = jnp.float32(log_gammas[RET_HEADS - 1])
        for k in range(RET_HEADS - 2, -1, -1):
            lg = jnp.where(h == k, jnp.float32(log_gammas[k]), lg)
        li = lax.broadcasted_iota(jnp.int32, (CHUNK, CHUNK), 0)
        si = lax.broadcasted_iota(jnp.int32, (CHUNK, CHUNK), 1)
        dmat = jnp.exp(lg * jnp.abs(li - si).astype(F32))
        pos = lax.broadcasted_iota(jnp.int32, (CHUNK, 1), 0).astype(F32)
        kdec_f = jnp.exp((CHUNK - 1 - pos) * lg)
        qdec_f = jnp.exp((pos + 1) * lg)
        kdec_b = jnp.exp(pos * lg)
        qdec_b = jnp.exp((CHUNK - pos) * lg)
        cdec = jnp.exp(CHUNK * lg)

        def rows(n):
            return pl.ds(pl.multiple_of(n * CHUNK, CHUNK), CHUNK)

        st[...] = jnp.zeros_like(st)

        def fwd(n, carry):
            r = rows(n)
            av, bv, vv = a_ref[r, :], b_ref[r, :], v_ref[r, :].astype(BF16)
            s = _dot_nt(av.astype(BF16), bv.astype(BF16)) * dmat
            y = _dot(s.astype(BF16), vv) + _dot((av * qdec_f).astype(BF16), st[...].astype(BF16))
            o_ref[r, :] = y
            st[...] = cdec * st[...] + _dot_tn((bv * kdec_f).astype(BF16), vv)
            return carry

        lax.fori_loop(0, nc, fwd, 0)
        st[...] = jnp.zeros_like(st)

        def bwd(m, carry):
            r = rows(nc - 1 - m)
            av, bv, vv = a_ref[r, :], b_ref[r, :], v_ref[r, :].astype(BF16)
            o_ref[r, :] += _dot((av * qdec_b).astype(BF16), st[...].astype(BF16))
            st[...] = cdec * st[...] + _dot_tn((bv * kdec_b).astype(BF16), vv)
            return carry

        lax.fori_loop(0, nc, bwd, 0)

    return pl.pallas_call(
        body, name=name, grid=(RET_HEADS,),
        in_specs=[pl.BlockSpec((T, da), lambda h: (0, h)), pl.BlockSpec((T, da), lambda h: (0, h)),
                  pl.BlockSpec((T, dv), lambda h: (0, h))],
        out_specs=pl.BlockSpec((T, dv), lambda h: (0, h)),
        out_shape=jax.ShapeDtypeStruct((T, RET_HEADS * dv), F32),
        scratch_shapes=[pltpu.VMEM((da, dv), F32)],
        compiler_params=_params(("arbitrary",)),
    )(a, b, v)


def _softplus(x):
    return jnp.maximum(x, 0.0) + jnp.log1p(jnp.exp(-jnp.abs(x)))


def _lane_lo():
    return lax.broadcasted_iota(jnp.int32, (1, CHUNK), 1) < SSD_HEAD_DIM


def _pair_cols(col, j):
    return jnp.where(_lane_lo(), col[:, 2 * j:2 * j + 1], col[:, 2 * j + 1:2 * j + 2])


def _pair_rows(colr, j):
    lo = lax.broadcasted_iota(jnp.int32, (CHUNK, 1), 0) < SSD_HEAD_DIM
    return jnp.where(lo, colr[2 * j:2 * j + 1, :], colr[2 * j + 1:2 * j + 2, :])


def _onehot8(h):
    return (lax.broadcasted_iota(jnp.int32, (1, HEADS_PER_GROUP), 1) == h).astype(F32)


def _ssd_pre(d, c, rawc, rawr, bc, br, alc, alr):
    li = lax.broadcasted_iota(jnp.int32, (CHUNK, CHUNK), 0)
    si = lax.broadcasted_iota(jnp.int32, (CHUNK, CHUNK), 1)
    dif = jnp.where(d == 0, li - si, si - li)
    mask = dif >= 0
    mask_t = dif <= 0
    rowc = c * CHUNK + lax.broadcasted_iota(jnp.int32, (CHUNK, 1), 0)
    rowr = c * CHUNK + lax.broadcasted_iota(jnp.int32, (1, CHUNK), 1)
    dtc = jnp.where(rowc >= PAD_ROWS, _softplus(rawc + bc), 0.0)
    dtr = jnp.where(rowr >= PAD_ROWS, _softplus(rawr + br), 0.0)
    ac = -jnp.exp(alc)
    ar = -jnp.exp(alr)
    dlc = dtc * ac
    dlr = dtr * ar
    alpc = _dot(mask.astype(F32), dlc, precision=HIGHEST)
    alpr = _dot(dlr, mask_t.astype(F32), precision=HIGHEST)
    endc = jnp.sum(dlc, axis=0, keepdims=True)
    endr = jnp.sum(dlr, axis=1, keepdims=True)
    return dict(mask=mask, mask_t=mask_t, dtc=dtc, ac=ac, alpc=alpc, alpr=alpr, endc=endc, endr=endr,
                valid=rowc >= PAD_ROWS)


def _chunk_of(d, n, nc):
    return n + d * (nc - 1 - 2 * n)


def _ssd_small_specs(cfn):
    return [
        pl.BlockSpec((None, None, CHUNK, HEADS_PER_GROUP), lambda d, g, n: (d, g, cfn(d, n), 0)),
        pl.BlockSpec((None, None, HEADS_PER_GROUP, CHUNK), lambda d, g, n: (d, g, 0, cfn(d, n))),
        pl.BlockSpec((None, None, 1, HEADS_PER_GROUP), lambda d, g, n: (d, g, 0, 0)),
        pl.BlockSpec((None, None, HEADS_PER_GROUP, 1), lambda d, g, n: (d, g, 0, 0)),
        pl.BlockSpec((None, None, 1, HEADS_PER_GROUP), lambda d, g, n: (d, g, 0, 0)),
        pl.BlockSpec((None, None, HEADS_PER_GROUP, 1), lambda d, g, n: (d, g, 0, 0)),
    ]


def _ssd_fwd(xs, bm, cm, small, T):
    nc = T // CHUNK
    cfn = lambda d, n: _chunk_of(d, n, nc)

    def body(x_ref, b_ref, c_ref, rawc_ref, rawr_ref, bc_ref, br_ref, alc_ref, alr_ref, y_ref, hs_ref, h_scr):
        d, n = pl.program_id(0), pl.program_id(2)
        c = cfn(d, n)

        @pl.when(n == 0)
        def _():
            h_scr[...] = jnp.zeros_like(h_scr)

        q = _ssd_pre(d, c, rawc_ref[...], rawr_ref[...], bc_ref[...], br_ref[...], alc_ref[...], alr_ref[...])
        bv = b_ref[...].astype(BF16)
        cv = c_ref[...].astype(BF16)
        cb = _dot_nt(cv, bv)
        lo = _lane_lo()
        for j in range(PAIRS_PER_GROUP):
            xp = x_ref[:, j * CHUNK:(j + 1) * CHUNK]
            xd = xp * _pair_cols(q["dtc"], j)
            xdb = xd.astype(BF16)
            yi = []
            for e in range(2):
                h = 2 * j + e
                lm = jnp.exp(jnp.where(q["mask"], q["alpc"][:, h:h + 1] - q["alpr"][h:h + 1, :], -jnp.inf))
                yi.append(_dot((cb * lm).astype(BF16), xdb))
            alp = _pair_cols(q["alpc"], j)
            hp = h_scr[j]
            hs_ref[j] = hp
            yo = jnp.exp(alp) * _dot_nt(cv, hp.astype(BF16))
            y_ref[:, j * CHUNK:(j + 1) * CHUNK] = jnp.where(lo, yi[0], yi[1]) + yo
            de = jnp.exp(_pair_cols(q["endc"], j) - alp)
            h_scr[j] = jnp.exp(_pair_rows(q["endr"], j)) * hp + _dot_tn((xd * de).astype(BF16), bv)

    gw = HEADS_PER_GROUP * SSD_HEAD_DIM
    return pl.pallas_call(
        body, name="ssd_fwd", grid=(2, SSD_GROUPS, nc),
        in_specs=[pl.BlockSpec((CHUNK, gw), lambda d, g, n: (cfn(d, n), g)),
                  pl.BlockSpec((CHUNK, SSD_STATE), lambda d, g, n: (cfn(d, n), g)),
                  pl.BlockSpec((CHUNK, SSD_STATE), lambda d, g, n: (cfn(d, n), g))] + _ssd_small_specs(cfn),
        out_specs=[pl.BlockSpec((None, CHUNK, gw), lambda d, g, n: (d, cfn(d, n), g)),
                   pl.BlockSpec((None, None, None, PAIRS_PER_GROUP, CHUNK, SSD_STATE),
                                lambda d, g, n: (d, g, cfn(d, n), 0, 0, 0))],
        out_shape=[jax.ShapeDtypeStruct((2, T, SSD_HEADS * SSD_HEAD_DIM), F32),
                   jax.ShapeDtypeStruct((2, SSD_GROUPS, nc, PAIRS_PER_GROUP, CHUNK, SSD_STATE), F32)],
        scratch_shapes=[pltpu.VMEM((PAIRS_PER_GROUP, CHUNK, SSD_STATE), F32)],
        compiler_params=_params(("arbitrary", "arbitrary", "arbitrary")),
    )(xs, bm, cm, *small)


def _ssd_bwd(xs, bm, cm, small, hs, dy, T):
    nc = T // CHUNK
    cfn = lambda d, n: _chunk_of(1 - d, n, nc)

    def body(x_ref, b_ref, c_ref, rawc_ref, rawr_ref, bc_ref, br_ref, alc_ref, alr_ref, hs_ref, dy_ref,
             dx_ref, db_ref, dc_ref, draw_ref, dbias_ref, dalog_ref, dh_scr):
        d, n = pl.program_id(0), pl.program_id(2)
        c = cfn(d, n)

        @pl.when(n == 0)
        def _():
            dh_scr[...] = jnp.zeros_like(dh_scr)

        rawc, bc = rawc_ref[...], bc_ref[...]
        q = _ssd_pre(d, c, rawc, rawr_ref[...], bc, br_ref[...], alc_ref[...], alr_ref[...])
        b32, c32 = b_ref[...], c_ref[...]
        bv, cv = b32.astype(BF16), c32.astype(BF16)
        cb = _dot_nt(cv, bv)
        cbt = _dot_nt(bv, cv)
        lo = _lane_lo()
        row_lo = lax.broadcasted_iota(jnp.int32, (CHUNK, 1), 0) < SSD_HEAD_DIM
        dcb = jnp.zeros((CHUNK, CHUNK), F32)
        dcp = jnp.zeros((CHUNK, SSD_STATE), F32)
        dbp = jnp.zeros((CHUNK, SSD_STATE), F32)
        dalp = jnp.zeros((CHUNK, HEADS_PER_GROUP), F32)
        dend = jnp.zeros((1, HEADS_PER_GROUP), F32)
        ddtx = jnp.zeros((CHUNK, HEADS_PER_GROUP), F32)

        def half_sums(t):
            return (jnp.sum(jnp.where(lo, t, 0.0), axis=1, keepdims=True),
                    jnp.sum(jnp.where(lo, 0.0, t), axis=1, keepdims=True))

        for j in range(PAIRS_PER_GROUP):
            xp = x_ref[:, j * CHUNK:(j + 1) * CHUNK]
            dtp = _pair_cols(q["dtc"], j)
            xd = xp * dtp
            xdb = xd.astype(BF16)
            dyp = dy_ref[:, j * CHUNK:(j + 1) * CHUNK]
            dyb = dyp.astype(BF16)
            hn = hs_ref[j]
            hnb = hn.astype(BF16)
            dh1 = dh_scr[j]
            dh1b = dh1.astype(BF16)
            alp = _pair_cols(q["alpc"], j)
            ea = jnp.exp(alp)
            de = jnp.exp(_pair_cols(q["endc"], j) - alp)
            dxi = []
            for e in range(2):
                h = 2 * j + e
                ac_, ar_ = q["alpc"][:, h:h + 1], q["alpr"][h:h + 1, :]
                lm = jnp.exp(jnp.where(q["mask"], ac_ - ar_, -jnp.inf))
                mt = cbt * jnp.exp(jnp.where(q["mask_t"], ar_ - ac_, -jnp.inf))
                dxi.append(_dot(mt.astype(BF16), dyb))
                dyeb_h = (jnp.where(lo, dyp, 0.0) if e == 0 else jnp.where(lo, 0.0, dyp)).astype(BF16)
                gl = _dot_nt(dyeb_h, xdb) * lm
                dcb = dcb + gl
                ra = jnp.sum(gl * cb, axis=1, keepdims=True) - jnp.sum(_dot_nt(xdb, dyeb_h) * mt, axis=1, keepdims=True)
                dalp = dalp + ra * _onehot8(h)
            y_off = ea * _dot_nt(cv, hnb)
            dxs_state = de * _dot_nt(bv, dh1b)
            dxd = jnp.where(lo, dxi[0], dxi[1]) + dxs_state
            dyeb = (dyp * ea).astype(BF16)
            dcp = dcp + _dot(dyeb, hnb)
            dbp = dbp + _dot((xd * de).astype(BF16), dh1b)
            dh_scr[j] = jnp.exp(_pair_rows(q["endr"], j)) * dh1 + _dot_tn(dyeb, cv)
            r0, r1 = half_sums(dyp * y_off - xd * dxs_state)
            dalp = dalp + r0 * _onehot8(2 * j) + r1 * _onehot8(2 * j + 1)
            t0, t1 = half_sums(xd * dxs_state)
            u = jnp.sum(dh1 * hn, axis=1, keepdims=True)
            u0 = jnp.sum(jnp.where(row_lo, u, 0.0), axis=0, keepdims=True)
            u1 = jnp.sum(jnp.where(row_lo, 0.0, u), axis=0, keepdims=True)
            eend = jnp.exp(q["endc"])
            dend = dend + (jnp.sum(t0, axis=0, keepdims=True) + eend * u0) * _onehot8(2 * j) \
                        + (jnp.sum(t1, axis=0, keepdims=True) + eend * u1) * _onehot8(2 * j + 1)
            dx_ref[:, j * CHUNK:(j + 1) * CHUNK] = dxd * dtp
            w0, w1 = half_sums(dxd * xp)
            ddtx = ddtx + w0 * _onehot8(2 * j) + w1 * _onehot8(2 * j + 1)

        dcbb = dcb.astype(BF16)
        dc_ref[...] = dcp + _dot(dcbb, bv)
        db_ref[...] = dbp + _dot_tn(dcbb, cv)
        ddl = _dot(q["mask_t"].astype(F32), dalp, precision=HIGHEST) + dend
        ddt = ddl * q["ac"] + ddtx
        draw = jnp.where(q["valid"], ddt * jax.nn.sigmoid(rawc + bc), 0.0)
        draw_ref[...] = draw
        dbias = jnp.sum(draw, axis=0, keepdims=True)
        dalog = jnp.sum(ddl * q["dtc"], axis=0, keepdims=True) * q["ac"]

        @pl.when(n == 0)
        def _():
            dbias_ref[...] = dbias
            dalog_ref[...] = dalog

        @pl.when(n > 0)
        def _():
            dbias_ref[...] += dbias
            dalog_ref[...] += dalog

    gw = HEADS_PER_GROUP * SSD_HEAD_DIM
    acc_spec = pl.BlockSpec((None, None, 1, HEADS_PER_GROUP), lambda d, g, n: (d, g, 0, 0))
    return pl.pallas_call(
        body, name="ssd_bwd", grid=(2, SSD_GROUPS, nc),
        in_specs=[pl.BlockSpec((CHUNK, gw), lambda d, g, n: (cfn(d, n), g)),
                  pl.BlockSpec((CHUNK, SSD_STATE), lambda d, g, n: (cfn(d, n), g)),
                  pl.BlockSpec((CHUNK, SSD_STATE), lambda d, g, n: (cfn(d, n), g))] + _ssd_small_specs(cfn) + [
                  pl.BlockSpec((None, None, None, PAIRS_PER_GROUP, CHUNK, SSD_STATE),
                               lambda d, g, n: (d, g, cfn(d, n), 0, 0, 0)),
                  pl.BlockSpec((CHUNK, gw), lambda d, g, n: (cfn(d, n), g))],
        out_specs=[pl.BlockSpec((None, CHUNK, gw), lambda d, g, n: (d, cfn(d, n), g)),
                   pl.BlockSpec((None, CHUNK, SSD_STATE), lambda d, g, n: (d, cfn(d, n), g)),
                   pl.BlockSpec((None, CHUNK, SSD_STATE), lambda d, g, n: (d, cfn(d, n), g)),
                   pl.BlockSpec((None, None, CHUNK, HEADS_PER_GROUP), lambda d, g, n: (d, g, cfn(d, n), 0)),
                   acc_spec, acc_spec],
        out_shape=[jax.ShapeDtypeStruct((2, T, SSD_HEADS * SSD_HEAD_DIM), F32),
                   jax.ShapeDtypeStruct((2, T, SSD_GROUPS * SSD_STATE), F32),
                   jax.ShapeDtypeStruct((2, T, SSD_GROUPS * SSD_STATE), F32),
                   jax.ShapeDtypeStruct((2, SSD_GROUPS, T, HEADS_PER_GROUP), F32),
                   jax.ShapeDtypeStruct((2, SSD_GROUPS, 1, HEADS_PER_GROUP), F32),
                   jax.ShapeDtypeStruct((2, SSD_GROUPS, 1, HEADS_PER_GROUP), F32)],
        scratch_shapes=[pltpu.VMEM((PAIRS_PER_GROUP, CHUNK, SSD_STATE), F32)],
        compiler_params=_params(("arbitrary", "arbitrary", "arbitrary")),
    )(xs, bm, cm, *small, hs, dy)


def _rot(x, cs, sn):
    return x * cs + pltpu.roll(x, RET_QK_DIM // 2, 1) * sn


def _rot_t(d, cs, sn):
    return d * cs + pltpu.roll(d * sn, RET_QK_DIM // 2, 1)


def _ret_post(y, g, w):
    parts = []
    for h in range(RET_HEADS):
        yh = y[:, h * RET_V_DIM:(h + 1) * RET_V_DIM]
        mu = jnp.mean(yh, axis=-1, keepdims=True)
        var = jnp.mean(jnp.square(yh - mu), axis=-1, keepdims=True)
        parts.append((yh - mu) * lax.rsqrt(var + EPS))
    return _silu(g) * (jnp.concatenate(parts, axis=1) * w)


def _ssd_post(yf, yb, xs, z, dskip, w):
    y = (yf + yb + xs * dskip) * _silu(z)
    return y * lax.rsqrt(jnp.mean(y * y, axis=-1, keepdims=True) + EPS) * w


def _merge(gates, yr, ys, valid):
    m = jax.nn.sigmoid(gates[:, :D_MODEL]) * yr + jax.nn.sigmoid(gates[:, D_MODEL:]) * ys
    return jnp.where(valid, m, 0.0)


def _rope_tables(T):
    half = RET_QK_DIM // 2
    inv = ROPE_BASE ** (-jnp.arange(half, dtype=F32) / half)
    pos = (jnp.arange(T) - PAD_ROWS).astype(F32)
    ang = pos[:, None] * inv[None, :]
    cos, sin = jnp.cos(ang), jnp.sin(ang)
    return jnp.concatenate([cos, cos], axis=1), jnp.concatenate([-sin, sin], axis=1)


def _per_group(v):
    c = v.reshape(SSD_GROUPS, 1, HEADS_PER_GROUP)
    return c, c.reshape(SSD_GROUPS, HEADS_PER_GROUP, 1)


def _local_step(x, target, w, tick, late_weights, early_grads, in_grads):
    S = x.shape[0]
    T = S + CHUNK
    tm = _tile_rows(T)
    c0 = _const(0)

    h0 = jnp.concatenate([jnp.zeros((PAD_ROWS, D_MODEL), F32), w["meta_tokens"], x], axis=0)
    tgt = jnp.concatenate([jnp.zeros((CHUNK, D_MODEL), F32), target], axis=0)
    w_in = {name: w["w_in_t"][a:b] for name, a, b in SEGMENTS}
    w_in["dt"] = jnp.pad(w_in["dt"], ((0, CHUNK - 2 * SSD_HEADS), (0, 0)))

    def norm_cast(name, h, nw):
        return _rows(name, lambda i, hv, wv: (_rms(hv, wv),), T, 1, [(h, D_MODEL, c0)], [(nw, D_MODEL, c0)],
                     [(D_MODEL, D_MODEL, c0, BF16)])[0]

    u = norm_cast("norm_mix", h0, w["norm_mix_w"] + tick)
    proj = {name: _mm("proj_" + name, u, w_in[name], "nt", out_dtype=F32 if name == "dt" else BF16)
            for name, _, _ in SEGMENTS}

    cs, sn = _rope_tables(T)
    scale = RET_QK_DIM ** -0.5

    def rot_fn(i, qk, csv, snv):
        q = [_rot(qk[:, h * 128:(h + 1) * 128], csv, snv) for h in range(RET_HEADS)]
        k = [_rot(qk[:, (RET_HEADS + h) * 128:(RET_HEADS + h + 1) * 128], csv, snv) * scale for h in range(RET_HEADS)]
        return jnp.concatenate(q, axis=1), jnp.concatenate(k, axis=1)

    qr, kr = _rows("rotary", rot_fn, T, 1, [(proj["qk"], 1024, c0), (cs, 128, c0), (sn, 128, c0)], [],
                   [(512, 512, c0, F32), (512, 512, c0, F32)])
    y_ret = _retention("retention", qr, kr, proj["v"], T)
    a_ret = _rows("ret_post", lambda i, y, g, gw: (_ret_post(y, g, gw),), T, 1,
                  [(y_ret, 1024, c0), (proj["g"], 1024, c0)], [(w["ret_gn_w"], 1024, c0)],
                  [(1024, 1024, c0, BF16)])[0]

    conv_w = {"xs": w["w_ssd_conv"][:, :2048], "B": w["w_ssd_conv"][:, 2048:2560], "C": w["w_ssd_conv"][:, 2560:]}
    conv_b = {"xs": w["b_ssd_conv"][:, :2048], "B": w["b_ssd_conv"][:, 2048:2560], "C": w["b_ssd_conv"][:, 2560:]}

    def ssd_conv_fn(i, xe, cw, cb):
        r = _row_ids(i, T, True)
        xe = jnp.where((r >= 0) & (r < T), xe, 0.0)
        return (_center(jnp.where(r >= PAD_ROWS, _silu(_conv3(xe, cw) + cb), 0.0)),)

    act = {}
    for name in ("xs", "B", "C"):
        wd = proj[name].shape[1]
        cw = 512
        act[name] = _rows("ssd_conv_" + name, ssd_conv_fn, T, wd // cw, [(proj[name], cw, lambda j: j)],
                          [(conv_w[name], cw, lambda j: j), (conv_b[name], cw, lambda j: j)],
                          [(wd, cw, lambda j: j, BF16)], halo=True)[0]

    raw = proj["dt"][:, :2 * SSD_HEADS].reshape(T, 2, SSD_GROUPS, HEADS_PER_GROUP)
    rawc = raw.transpose(1, 2, 0, 3)
    rawr = raw.transpose(1, 2, 3, 0)
    bias = [_per_group(w["dt_bias_f"]), _per_group(w["dt_bias_b"])]
    alog = [_per_group(w["a_log_f"]), _per_group(w["a_log_b"])]
    small = (rawc, rawr, jnp.stack([bias[0][0], bias[1][0]]), jnp.stack([bias[0][1], bias[1][1]]),
             jnp.stack([alog[0][0], alog[1][0]]), jnp.stack([alog[0][1], alog[1][1]]))
    y_dir, states = _ssd_fwd(act["xs"], act["B"], act["C"], small, T)

    dskip_e = jnp.repeat(w["d_skip"], SSD_HEAD_DIM, axis=1)
    gcol = lambda j: j
    gw_ = 512
    a_ssd = _rows("ssd_post", lambda i, yf, yb, xv, zv, dk, nw: (_ssd_post(yf, yb, xv, zv, dk, nw),), T, SSD_GROUPS,
                  [(y_dir, gw_, gcol, 0), (y_dir, gw_, gcol, 1), (act["xs"], gw_, gcol), (proj["z"], gw_, gcol)],
                  [(dskip_e, gw_, gcol), (w["ssd_norm_w"], gw_, gcol)], [(2048, gw_, gcol, BF16)])[0]

    w = dict(w, **late_weights(a_ssd))
    w_up_g, w_up_u = w["w_ffn_up_t"][:D_FF], w["w_ffn_up_t"][D_FF:]
    y_ret_o = _mm("ret_out", a_ret, w["w_ret_out"], "nn")
    y_ssd_o = _mm("ssd_out", a_ssd, w["w_ssd_out"], "nn")

    def merge_fn(i, gates, yr, ys):
        return (_merge(gates, yr, ys, _row_ids(i, T) >= PAD_ROWS),)

    merged = _rows("merge", merge_fn, T, 1, [(proj["gates"], 2048, c0), (y_ret_o, 1024, c0), (y_ssd_o, 1024, c0)], [],
                   [(1024, 1024, c0, BF16)])[0]
    h1 = _mm("mix_out", merged, w["w_out"], "nn", add=h0)

    n2 = norm_cast("norm_ffn", h1, w["norm_ffn_w"])
    fg_pre = _mm("ffn_up_g", n2, w_up_g, "nt", out_dtype=BF16)
    fu_pre = _mm("ffn_up_u", n2, w_up_u, "nt", out_dtype=BF16)
    cwg, cwu = w["w_ffn_conv"][:, :D_FF], w["w_ffn_conv"][:, D_FF:]
    cbg, cbu = w["b_ffn_conv"][:, :D_FF], w["b_ffn_conv"][:, D_FF:]
    fcol = lambda j: j
    fw = 1408

    def ffn_act_fn(i, ge, ue, wg, wu, bg, bu):
        return (_center(_silu(_conv3(ge, wg) + bg) * (_conv3(ue, wu) + bu)),)

    def ext_valid(i):
        r = _row_ids(i, T, True)
        return (r >= 0) & (r < T)

    def ffn_act_masked(i, ge, ue, wg, wu, bg, bu):
        v = ext_valid(i)
        return ffn_act_fn(i, jnp.where(v, ge, 0.0), jnp.where(v, ue, 0.0), wg, wu, bg, bu)

    a2 = _rows("ffn_act", ffn_act_masked, T, D_FF // fw, [(fg_pre, fw, fcol), (fu_pre, fw, fcol)],
               [(cwg, fw, fcol), (cwu, fw, fcol), (cbg, fw, fcol), (cbu, fw, fcol)], [(D_FF, fw, fcol, BF16)],
               halo=True)[0]
    h2 = _mm("ffn_down", a2, w["w_ffn_down"], "nn", add=h1)

    fnw = w["final_norm_w"].reshape(1, D_MODEL)

    def loss_fn(i, hv, tv, nw):
        valid = _row_ids(i, T) >= CHUNK
        y, vjp = jax.vjp(_rms, hv, nw)
        diff = jnp.where(valid, y - tv, 0.0)
        dh, dw = vjp(diff * (1.0 / D_MODEL))
        part = 0.5 / D_MODEL * jnp.sum(jnp.sum(diff * diff, axis=1, keepdims=True), axis=0, keepdims=True)
        return dh, jnp.broadcast_to(part, (1, 128)), dw

    dh2, loss_acc, d_fnw = _rows("loss", loss_fn, T, 1, [(h2, D_MODEL, c0), (tgt, D_MODEL, c0)], [(fnw, D_MODEL, c0)],
                                 [(D_MODEL, D_MODEL, c0, F32)], [(1, 128, 128, c0), (1, D_MODEL, D_MODEL, c0)])
    loss = loss_acc[0, 0]
    grads = {"final_norm_w": d_fnw.reshape(D_MODEL)}

    da2 = _mm("d_ffn_act", dh2, w["w_ffn_down"], "nt", out_dtype=BF16)
    grads["w_ffn_down"] = _mm("g_ffn_down", a2, dh2, "tn")

    def ffn_bwd_fn(i, ge, ue, de, wg, wu, bg, bu):
        v = ext_valid(i)
        ge, ue, de = jnp.where(v, ge, 0.0), jnp.where(v, ue, 0.0), jnp.where(v, de, 0.0)
        fg = _conv3(ge, wg) + bg
        fu = _conv3(ue, wu) + bu
        sg = jax.nn.sigmoid(fg)
        dfg = de * fu * (sg * (1.0 + fg * (1.0 - sg)))
        dfu = de * (fg * sg)
        n = ge.shape[0]

        def wgrad(df, xe):
            df_c = _center(df)
            return jnp.concatenate([jnp.sum(df_c * _center(pltpu.roll(xe, 1, 0)), axis=0, keepdims=True),
                                    jnp.sum(df_c * _center(xe), axis=0, keepdims=True),
                                    jnp.sum(df_c * _center(pltpu.roll(xe, n - 1, 0)), axis=0, keepdims=True)], axis=0)

        return (_center(_conv3_t(dfg, wg)), _center(_conv3_t(dfu, wu)), wgrad(dfg, ge), wgrad(dfu, ue),
                jnp.sum(_center(dfg), axis=0, keepdims=True), jnp.sum(_center(dfu), axis=0, keepdims=True))

    dfg_pre, dfu_pre, g_cwg, g_cwu, g_cbg, g_cbu = _rows(
        "ffn_act_bwd", ffn_bwd_fn, T, D_FF // fw, [(fg_pre, fw, fcol), (fu_pre, fw, fcol), (da2, fw, fcol)],
        [(cwg, fw, fcol), (cwu, fw, fcol), (cbg, fw, fcol), (cbu, fw, fcol)],
        [(D_FF, fw, fcol, BF16), (D_FF, fw, fcol, BF16)],
        [(3, D_FF, fw, fcol), (3, D_FF, fw, fcol), (1, D_FF, fw, fcol), (1, D_FF, fw, fcol)], halo=True)
    grads["w_ffn_conv"] = jnp.concatenate([g_cwg, g_cwu], axis=1)
    grads["b_ffn_conv"] = jnp.concatenate([g_cbg, g_cbu], axis=1)
    dn2 = _mm("d_norm_ffn_g", dfg_pre, w_up_g, "nn")
    dn2 = _mm("d_norm_ffn_u", dfu_pre, w_up_u, "nn", add=dn2)
    grads["w_ffn_up_t"] = jnp.concatenate([_mm("g_ffn_up_g", dfg_pre, n2, "tn"), _mm("g_ffn_up_u", dfu_pre, n2, "tn")],
                                          axis=0)

    def norm_bwd(name, h, nw, dn, dres):
        def fn(i, hv, dnv, drv, wv):
            _, vjp = jax.vjp(_rms, hv, wv)
            dh, dw = vjp(dnv)
            return dh + drv, dw
        return _rows(name, fn, T, 1, [(h, D_MODEL, c0), (dn, D_MODEL, c0), (dres, D_MODEL, c0)], [(nw, D_MODEL, c0)],
                     [(D_MODEL, D_MODEL, c0, F32)], [(1, D_MODEL, D_MODEL, c0)])

    dh1, grads["norm_ffn_w"] = norm_bwd("norm_ffn_bwd", h1, w["norm_ffn_w"], dn2, dh2)

    dmerged = _mm("d_merged", dh1, w["w_out"], "nt")
    grads["w_out"] = _mm("g_out", merged, dh1, "tn")

    def merge_bwd_fn(i, gates, yr, ys, dm):
        valid = _row_ids(i, T) >= PAD_ROWS
        _, vjp = jax.vjp(lambda a, b, c: _merge(a, b, c, valid), gates, yr, ys)
        return vjp(dm)

    dgates, dyr, dys = _rows("merge_bwd", merge_bwd_fn, T, 1,
                             [(proj["gates"], 2048, c0), (y_ret_o, 1024, c0), (y_ssd_o, 1024, c0), (dmerged, 1024, c0)],
                             [], [(2048, 2048, c0, BF16), (1024, 1024, c0, BF16), (1024, 1024, c0, BF16)])
    dproj = {"gates": dgates}

    da_ssd = _mm("d_ssd_act", dys, w["w_ssd_out"], "nt")
    grads["w_ssd_out"] = _mm("g_ssd_out", a_ssd, dys, "tn")

    def ssd_post_bwd_fn(i, yf, yb, xv, zv, da, dk, nw):
        _, vjp = jax.vjp(_ssd_post, yf, yb, xv, zv, dk, nw)
        dyf, _, dxv, dzv, ddk, dnw = vjp(da)
        return dyf, dxv, dzv, ddk, dnw

    dy_ssd, dxs_skip, dproj["z"], g_dskip_e, grads["ssd_norm_w"] = _rows(
        "ssd_post_bwd", ssd_post_bwd_fn, T, SSD_GROUPS,
        [(y_dir, gw_, gcol, 0), (y_dir, gw_, gcol, 1), (act["xs"], gw_, gcol), (proj["z"], gw_, gcol),
         (da_ssd, gw_, gcol)],
        [(dskip_e, gw_, gcol), (w["ssd_norm_w"], gw_, gcol)],
        [(2048, gw_, gcol, F32), (2048, gw_, gcol, F32), (2048, gw_, gcol, BF16)],
        [(1, 2048, gw_, gcol), (1, 2048, gw_, gcol)])
    grads["d_skip"] = g_dskip_e.reshape(SSD_HEADS, SSD_HEAD_DIM).sum(axis=1).reshape(1, SSD_HEADS)

    dxs_dir, db_dir, dc_dir, draw, g_bias, g_alog = _ssd_bwd(act["xs"], act["B"], act["C"], small, states, dy_ssd, T)
    grads["dt_bias_f"], grads["dt_bias_b"] = g_bias[0].reshape(1, SSD_HEADS), g_bias[1].reshape(1, SSD_HEADS)
    grads["a_log_f"], grads["a_log_b"] = g_alog[0].reshape(1, SSD_HEADS), g_alog[1].reshape(1, SSD_HEADS)
    d_dt = draw.transpose(2, 0, 1, 3).reshape(T, 2 * SSD_HEADS)
    dproj["dt"] = jnp.pad(d_dt, ((0, 0), (0, CHUNK - 2 * SSD_HEADS))).astype(BF16)

    def make_conv_bwd(nsum):
        def fn(i, xe, *rest):
            ds, (cw, cb) = rest[:nsum], rest[nsum:]
            r = _row_ids(i, T, True)
            dact = ds[0]
            for t in ds[1:]:
                dact = dact + t
            dact = jnp.where((r >= PAD_ROWS) & (r < T), dact, 0.0)
            xe = jnp.where((r >= 0) & (r < T), xe, 0.0)
            pre = _conv3(xe, cw) + cb
            sg = jax.nn.sigmoid(pre)
            dpre = dact * (sg * (1.0 + pre * (1.0 - sg)))
            n = xe.shape[0]
            dpc = _center(dpre)
            dw = jnp.concatenate([jnp.sum(dpc * _center(pltpu.roll(xe, 1, 0)), axis=0, keepdims=True),
                                  jnp.sum(dpc * _center(xe), axis=0, keepdims=True),
                                  jnp.sum(dpc * _center(pltpu.roll(xe, n - 1, 0)), axis=0, keepdims=True)], axis=0)
            return _center(_conv3_t(dpre, cw)), dw, jnp.sum(dpc, axis=0, keepdims=True)
        return fn

    g_cw, g_cb = {}, {}
    cots = {"xs": [(dxs_dir, 512, gcol, 0), (dxs_dir, 512, gcol, 1), (dxs_skip, 512, gcol)],
            "B": [(db_dir, 512, gcol, 0), (db_dir, 512, gcol, 1)],
            "C": [(dc_dir, 512, gcol, 0), (dc_dir, 512, gcol, 1)]}
    for name in ("xs", "B", "C"):
        wd = proj[name].shape[1]
        dproj[name], g_cw[name], g_cb[name] = _rows(
            "ssd_conv_bwd_" + name, make_conv_bwd(len(cots[name])), T, wd // 512,
            [(proj[name], 512, gcol)] + cots[name], [(conv_w[name], 512, gcol), (conv_b[name], 512, gcol)],
            [(wd, 512, gcol, BF16)], [(3, wd, 512, gcol), (1, wd, 512, gcol)], halo=True)
    grads["w_ssd_conv"] = jnp.concatenate([g_cw["xs"], g_cw["B"], g_cw["C"]], axis=1)
    grads["b_ssd_conv"] = jnp.concatenate([g_cb["xs"], g_cb["B"], g_cb["C"]], axis=1)

    da_ret = _mm("d_ret_act", dyr, w["w_ret_out"], "nt")
    grads["w_ret_out"] = _mm("g_ret_out", a_ret, dyr, "tn")
    tick = early_grads({n: grads.pop(n) for n in ("w_ffn_up_t", "w_ret_out", "w_ssd_out", "w_out", "w_ffn_down")})

    def ret_post_bwd_fn(i, y, g, da, gw):
        _, vjp = jax.vjp(_ret_post, y, g, gw)
        return vjp(da)

    dy_ret, dproj["g"], grads["ret_gn_w"] = _rows(
        "ret_post_bwd", ret_post_bwd_fn, T, 1, [(y_ret, 1024, c0), (proj["g"], 1024, c0), (da_ret, 1024, c0)],
        [(w["ret_gn_w"] + tick, 1024, c0)], [(1024, 1024, c0, F32), (1024, 1024, c0, BF16)], [(1, 1024, 1024, c0)])
    dproj["v"] = _retention("retention_dv", kr, qr, dy_ret, T)
    dqr = _retention("retention_dq", dy_ret, proj["v"], kr, T)
    dkr = _retention("retention_dk", proj["v"], dy_ret, qr, T)

    def rot_bwd_fn(i, dq, dk, csv, snv):
        parts = [_rot_t(dq[:, h * 128:(h + 1) * 128], csv, snv) for h in range(RET_HEADS)]
        parts += [_rot_t(dk[:, h * 128:(h + 1) * 128] * scale, csv, snv) for h in range(RET_HEADS)]
        return (jnp.concatenate(parts, axis=1),)

    dproj["qk"] = _rows("rotary_bwd", rot_bwd_fn, T, 1, [(dqr, 512, c0), (dkr, 512, c0), (cs, 128, c0), (sn, 128, c0)],
                        [], [(1024, 1024, c0, BF16)])[0]

    g_in = [_mm("g_in_" + name, dproj[name], u, "tn") for name, _, _ in SEGMENTS]
    g_in[7] = g_in[7][:2 * SSD_HEADS]
    tick = in_grads(jnp.concatenate(g_in, axis=0))
    du = _mm("d_u_dt", dproj["dt"] + tick.astype(BF16), w_in["dt"], "nn")
    for name, _, _ in SEGMENTS:
        if name != "dt":
            du = _mm("d_u_" + name, dproj[name], w_in[name], "nn", add=du)
    dh0, grads["norm_mix_w"] = norm_bwd("norm_mix_bwd", h0, w["norm_mix_w"], du, dh1)
    grads["meta_tokens"] = dh0[PAD_ROWS:CHUNK]
    return loss, dh0[CHUNK:], grads


MESH_ID = pl.DeviceIdType.MESH
ANY = pl.BlockSpec(memory_space=pl.ANY)


def _me_and_peers():
    x, y, c = lax.axis_index("x"), lax.axis_index("y"), lax.axis_index("c")
    peers = []
    for k in range(1, N_DEV):
        px = 1 - x if k & 4 else x
        py = 1 - y if k & 2 else y
        pc = 1 - c if k & 1 else c
        peers.append(((px, py, pc), 4 * px + 2 * py + pc))
    return 4 * x + 2 * y + c, peers


def _push_blocks(name, src, per_peer):
    blk = src.shape[1:] if per_peer else src.shape

    def body(src_ref, out_ref, send_sems, recv_sems, local_sem):
        me, peers = _me_and_peers()
        mine = src_ref.at[me] if per_peer else src_ref
        local = pltpu.make_async_copy(mine, out_ref.at[me], local_sem)
        local.start()
        sends = []
        for k, (dev, idx) in enumerate(peers):
            cp = pltpu.make_async_remote_copy(
                src_ref=src_ref.at[idx] if per_peer else src_ref, dst_ref=out_ref.at[me],
                send_sem=send_sems.at[k], recv_sem=recv_sems.at[k], device_id=dev, device_id_type=MESH_ID)
            cp.start()
            sends.append(cp)
        for k, (dev, idx) in enumerate(peers):
            pltpu.make_async_remote_copy(
                src_ref=mine, dst_ref=out_ref.at[idx], send_sem=send_sems.at[k], recv_sem=recv_sems.at[k],
                device_id=dev, device_id_type=MESH_ID).wait_recv()
        for cp in sends:
            cp.wait_send()
        local.wait()

    return pl.pallas_call(
        body, name=name, in_specs=[ANY], out_specs=ANY,
        out_shape=jax.ShapeDtypeStruct((N_DEV,) + tuple(blk), src.dtype),
        scratch_shapes=[pltpu.SemaphoreType.DMA((N_DEV - 1,)), pltpu.SemaphoreType.DMA((N_DEV - 1,)),
                        pltpu.SemaphoreType.DMA],
    )(src)


def _gather_two_level(name, src):
    def body(x_ref, out_ref, send_sems, recv_sems, local_sem):
        x, y, c = lax.axis_index("x"), lax.axis_index("y"), lax.axis_index("c")
        me, sibling = (x, y, c), (x, y, 1 - c)
        chips = [(1 - x, y), (x, 1 - y), (1 - x, 1 - y)]

        def rows(px, py, pc):
            return out_ref.at[4 * px + 2 * py + pc]

        def copy(k, block, to, src_ref=None):
            return pltpu.make_async_remote_copy(
                src_ref=rows(*block) if src_ref is None else src_ref, dst_ref=rows(*block),
                send_sem=send_sems.at[k], recv_sem=recv_sems.at[k], device_id=to, device_id_type=MESH_ID)

        mine = pltpu.make_async_copy(x_ref, rows(*me), local_sem)
        mine.start()
        first = [copy(0, me, sibling, x_ref)] + [copy(1 + j, me, (*chip, c), x_ref) for j, chip in enumerate(chips)]
        for cp in first:
            cp.start()
        passed = [copy(4 + j, (*chip, c), sibling) for j, chip in enumerate(chips)]
        for j, chip in enumerate(chips):
            copy(1 + j, (*chip, c), me).wait_recv()
            passed[j].start()
        copy(0, sibling, me).wait_recv()
        for j, chip in enumerate(chips):
            copy(4 + j, (*chip, 1 - c), me).wait_recv()
        for cp in first + passed:
            cp.wait_send()
        mine.wait()

    return pl.pallas_call(
        body, name=name, in_specs=[ANY], out_specs=ANY,
        out_shape=jax.ShapeDtypeStruct((N_DEV,) + tuple(src.shape), src.dtype),
        scratch_shapes=[pltpu.SemaphoreType.DMA((N_DEV - 1,)), pltpu.SemaphoreType.DMA((N_DEV - 1,)),
                        pltpu.SemaphoreType.DMA],
    )(src)


HBM = pl.BlockSpec(memory_space=pltpu.HBM)
SEM = pl.BlockSpec(memory_space=pltpu.SEMAPHORE)
EFFECT = pltpu.SideEffectType.DATAFLOW_SIDE_EFFECTING


def _peer_copy(src_ref, land_ref, send_sems, recv_sems, per_peer, me, k, dev, idx, receiving):
    return pltpu.make_async_remote_copy(
        src_ref=src_ref.at[idx] if per_peer else src_ref, dst_ref=land_ref.at[idx if receiving else me],
        send_sem=send_sems.at[k], recv_sem=recv_sems.at[k], device_id=dev, device_id_type=MESH_ID)


def _push_start(name, src, per_peer):
    blk = src.shape[1:] if per_peer else src.shape
    land_shape = (N_DEV,) + tuple(blk)

    def body(src_ref, land_ref, send_sems, recv_sems, src_thru, land_thru, token):
        me, peers = _me_and_peers()
        for k, (dev, idx) in enumerate(peers):
            _peer_copy(src_ref, land_ref, send_sems, recv_sems, per_peer, me, k, dev, idx, False).start()
        token[...] = jnp.zeros_like(token)

    return pl.pallas_call(
        body, name=name,
        out_shape=(pltpu.SemaphoreType.DMA((N_DEV - 1,)), pltpu.SemaphoreType.DMA((N_DEV - 1,)),
                   pltpu.HBM(src.shape, src.dtype), pltpu.HBM(land_shape, src.dtype),
                   jax.ShapeDtypeStruct((8, 128), F32)),
        in_specs=(HBM, HBM), out_specs=(SEM, SEM, HBM, HBM, pl.BlockSpec(memory_space=pltpu.VMEM)),
        input_output_aliases={0: 2, 1: 3}, compiler_params=pltpu.CompilerParams(has_side_effects=EFFECT),
    )(pltpu.with_memory_space_constraint(src, pltpu.HBM),
      pltpu.with_memory_space_constraint(lax.empty(land_shape, src.dtype), pltpu.HBM))


def _push_wait(name, send_sems, recv_sems, src_thru, land_thru, after, per_peer):
    def body(src_ref, land_ref, send_sems, recv_sems, after_ref, src_out, land_out):
        me, peers = _me_and_peers()
        for k, (dev, idx) in enumerate(peers):
            cp = _peer_copy(src_ref, land_ref, send_sems, recv_sems, per_peer, me, k, dev, idx, True)
            cp.wait_send()
            cp.wait_recv()

    return pl.pallas_call(
        body, name=name,
        out_shape=(pltpu.HBM(src_thru.shape, src_thru.dtype), pltpu.HBM(land_thru.shape, land_thru.dtype)),
        in_specs=(HBM, HBM, SEM, SEM, ANY), out_specs=(HBM, HBM), input_output_aliases={0: 0, 1: 1},
        compiler_params=pltpu.CompilerParams(has_side_effects=EFFECT),
    )(src_thru, land_thru, send_sems, recv_sems, after)


def _sum_blocks(name, blocks):
    _, R, C = blocks.shape
    tc = _pick(C, (128,))

    def body(b_ref, o_ref):
        acc = b_ref[0].astype(F32)
        for k in range(1, N_DEV):
            acc = acc + b_ref[k].astype(F32)
        o_ref[...] = acc

    return pl.pallas_call(
        body, name=name, grid=(C // tc,), in_specs=[pl.BlockSpec((N_DEV, R, tc), lambda j: (0, 0, j))],
        out_specs=pl.BlockSpec((R, tc), lambda j: (0, j)), out_shape=jax.ShapeDtypeStruct((R, C), F32),
        compiler_params=_params(("arbitrary",)),
    )(blocks)


def _adamw(name, w, g, m, v):
    R, C = w.shape
    tr = _pick(R, (224, 184, 8))
    spec = pl.BlockSpec((tr, C), lambda i: (i, 0))

    def body(w_ref, g_ref, m_ref, v_ref, d_ref, mo_ref, vo_ref):
        gv = g_ref[...]
        mn = ADAM_B1 * m_ref[...] + (1.0 - ADAM_B1) * gv
        vn = ADAM_B2 * v_ref[...] + (1.0 - ADAM_B2) * jnp.square(gv)
        m_hat = mn / (1.0 - ADAM_B1 ** ADAM_STEP)
        v_hat = vn / (1.0 - ADAM_B2 ** ADAM_STEP)
        d_ref[...] = -ADAM_LR * (m_hat / (jnp.sqrt(v_hat) + ADAM_EPS) + ADAM_WD * w_ref[...])
        mo_ref[...] = mn
        vo_ref[...] = vn

    return pl.pallas_call(
        body, name=name, grid=(R // tr,), in_specs=[spec] * 4, out_specs=[spec] * 3,
        out_shape=[jax.ShapeDtypeStruct((R, C), F32)] * 3, compiler_params=_params(("arbitrary",)),
    )(w, g, m, v)


WEIGHTS = ("meta_tokens", "norm_mix_w", "w_in", "ret_gn_w", "w_ret_out", "w_ssd_conv", "b_ssd_conv", "dt_bias_f",
           "dt_bias_b", "a_log_f", "a_log_b", "d_skip", "ssd_norm_w", "w_ssd_out", "w_out", "norm_ffn_w", "w_ffn_up",
           "w_ffn_conv", "b_ffn_conv", "w_ffn_down", "final_norm_w")
BIG = (("w_in", 1288, True), ("w_ffn_up", 704, True), ("w_ret_out", 128, False), ("w_ssd_out", 256, False),
       ("w_out", 128, False), ("w_ffn_down", 352, False))
REPLICATED = ("norm_mix_w", "ret_gn_w", "b_ssd_conv", "dt_bias_f", "dt_bias_b", "a_log_f", "a_log_b", "d_skip",
              "ssd_norm_w", "norm_ffn_w", "b_ffn_conv", "final_norm_w")
SMALL_SHARDED = (("meta_tokens", 16, 1024), ("w_ssd_conv", 3, 3072), ("w_ffn_conv", 3, 5632))


BIG_IN, BIG_REST = BIG[:1], BIG[1:]


def _pack_big(tree, group):
    parts = []
    for name, _, transposed in group:
        a = tree[name][0]
        parts.append(a.T if transposed else a)
    return jnp.concatenate(parts, axis=0)


def _unpack_big(slab, group):
    out, r0 = {}, 0
    for name, r, transposed in group:
        a = slab[r0:r0 + r]
        out[name] = (a.T if transposed else a)[None]
        r0 += r
    return out


def _pack_flat(arrays, rows):
    flat = jnp.concatenate([a.reshape(-1) for a in arrays])
    return jnp.pad(flat, (0, rows * D_MODEL - flat.shape[0])).reshape(rows, D_MODEL)


def _unpack_flat(slab, shapes):
    flat, out, o = slab.reshape(-1), [], 0
    for s in shapes:
        n = math.prod(s)
        out.append(flat[o:o + n].reshape(s))
        o += n
    return out


def kernel(x, meta_tokens, norm_mix_w, w_in, ret_gn_w, w_ret_out, w_ssd_conv, b_ssd_conv, dt_bias_f, dt_bias_b, a_log_f, a_log_b, d_skip, ssd_norm_w, w_ssd_out, w_out, norm_ffn_w, w_ffn_up, w_ffn_conv, b_ffn_conv, w_ffn_down, final_norm_w, loss_target, m_meta_tokens, m_norm_mix_w, m_w_in, m_ret_gn_w, m_w_ret_out, m_w_ssd_conv, m_b_ssd_conv, m_dt_bias_f, m_dt_bias_b, m_a_log_f, m_a_log_b, m_d_skip, m_ssd_norm_w, m_w_ssd_out, m_w_out, m_norm_ffn_w, m_w_ffn_up, m_w_ffn_conv, m_b_ffn_conv, m_w_ffn_down, m_final_norm_w, v_meta_tokens, v_norm_mix_w, v_w_in, v_ret_gn_w, v_w_ret_out, v_w_ssd_conv, v_b_ssd_conv, v_dt_bias_f, v_dt_bias_b, v_a_log_f, v_a_log_b, v_d_skip, v_ssd_norm_w, v_w_ssd_out, v_w_out, v_norm_ffn_w, v_w_ffn_up, v_w_ffn_conv, v_b_ffn_conv, v_w_ffn_down, v_final_norm_w):
    given = dict(locals())
    wt = {n: given[n] for n in WEIGHTS}
    mt = {n: given["m_" + n] for n in WEIGHTS}
    vt = {n: given["v_" + n] for n in WEIGHTS}
    me = 4 * lax.axis_index("x") + 2 * lax.axis_index("y") + lax.axis_index("c")

    w_slabs = {"in": _pack_big(wt, BIG_IN), "rest": _pack_big(wt, BIG_REST)}
    small_names = [n for n, _, _ in SMALL_SHARDED]
    small_local = lambda tree: [tree[n].reshape(r, c // N_DEV) for n, r, c in SMALL_SHARDED]
    all_in = _gather_two_level("gather_w_in", w_slabs["in"].astype(BF16))
    rest_src, all_in = lax.optimization_barrier((w_slabs["rest"].astype(BF16), all_in))
    rest_flight = _push_start("gather_rest_start", rest_src, False)
    all_s = _push_blocks("gather_small", _pack_flat(small_local(wt), 8), False).reshape(N_DEV, -1)
    full = {"w_in_t": all_in.reshape(-1, D_MODEL)}

    def land_with_own(flight, after, per_peer, name):
        src, land = _push_wait(name, *flight[:4], after, per_peer)
        own = lax.dynamic_slice_in_dim(src, me, 1, axis=0) if per_peer else src[None]
        return lax.dynamic_update_slice_in_dim(land, own, me, axis=0)

    def late_weights(after):
        all_rest = land_with_own(rest_flight, after, False, "gather_rest_wait")
        out, r0 = {}, 0
        for name, r, transposed in BIG_REST:
            out[name + ("_t" if transposed else "")] = all_rest[:, r0:r0 + r].reshape(N_DEV * r, D_MODEL)
            r0 += r
        return out

    flights = {}

    def start_exchange(key, group, gd):
        g_blocks = jnp.concatenate(
            [gd[name + ("_t" if t else "")].reshape(N_DEV, r, D_MODEL) for name, r, t in group], axis=1)
        flights[key] = _push_start("exchange_" + key + "_start", g_blocks.astype(BF16), True)
        return flights[key][4][0, 0]

    o = 0
    for name, r, c in SMALL_SHARDED:
        n = r * c // N_DEV
        full[name] = all_s[:, o:o + n].reshape(N_DEV, r, c // N_DEV).transpose(1, 0, 2).reshape(r, c)
        o += n
    for name in REPLICATED:
        full[name] = wt[name]

    loss, grad_x, g = _local_step(
        x[0], loss_target[0], full, rest_flight[4][0, 0], late_weights,
        lambda gd: start_exchange("rest", BIG_REST, gd), lambda gi: start_exchange("in", BIG_IN, {"w_in_t": gi}))

    g_slabs = {key: _sum_blocks("sum_" + key, land_with_own(flights[key], grad_x, True, "exchange_" + key + "_wait"))
               for key in ("rest", "in")}
    small_parts = [g[n] for n in REPLICATED] + [g[n] for n in small_names] + [loss.reshape(1)]
    g_small = _sum_blocks("sum_small", _push_blocks("gather_small_grads", _pack_flat(small_parts, 64), False))
    small_red = _unpack_flat(g_small, [wt[n].shape for n in REPLICATED] + [(r, c) for _, r, c in SMALL_SHARDED] + [(1,)])
    grads = dict(zip(REPLICATED, small_red[:len(REPLICATED)]))
    for (name, r, c), red in zip(SMALL_SHARDED, small_red[len(REPLICATED):-1]):
        grads[name] = lax.dynamic_slice(red, (0, me * (c // N_DEV)), (r, c // N_DEV)).reshape(wt[name].shape)
    loss_all = small_red[-1][0]
    delta, new_m, new_v = {}, {}, {}
    for key, group in (("in", BIG_IN), ("rest", BIG_REST)):
        grads.update(_unpack_big(g_slabs[key], group))
        d_slab, m_slab, v_slab = _adamw("adamw_" + key, w_slabs[key], g_slabs[key], _pack_big(mt, group),
                                        _pack_big(vt, group))
        delta.update(_unpack_big(d_slab, group))
        new_m.update(_unpack_big(m_slab, group))
        new_v.update(_unpack_big(v_slab, group))

    rest = list(REPLICATED) + small_names
    shapes = [wt[n].shape for n in rest]
    pack_rest = lambda tree: _pack_flat([tree[n] for n in rest], 24)
    d_rest, m_rest, v_rest = _adamw("adamw_small", pack_rest(wt), pack_rest(grads), pack_rest(mt), pack_rest(vt))
    delta.update(zip(rest, _unpack_flat(d_rest, shapes)))
    new_m.update(zip(rest, _unpack_flat(m_rest, shapes)))
    new_v.update(zip(rest, _unpack_flat(v_rest, shapes)))

    return (loss_all, grad_x[None], *[grads[n] for n in WEIGHTS], *[delta[n] for n in WEIGHTS],
            *[new_m[n] for n in WEIGHTS], *[new_v[n] for n in WEIGHTS])
```

```python
import functools
import math

import jax
import jax.numpy as jnp
from jax import lax
from jax.experimental import pallas as pl
from jax.experimental.pallas import tpu as pltpu

F32 = jnp.float32
BF16 = jnp.bfloat16

D_MODEL = 1024
CHUNK = 128
N_META = 16
PAD_ROWS = CHUNK - N_META
RET_HEADS = 4
RET_QK_DIM = 128
RET_V_DIM = 256
SSD_HEADS = 32
SSD_HEAD_DIM = 64
SSD_GROUPS = 4
SSD_STATE = 128
HEADS_PER_GROUP = SSD_HEADS // SSD_GROUPS
PAIRS_PER_GROUP = HEADS_PER_GROUP // 2
D_FF = 2816
EPS = 1e-6
ROPE_BASE = 10000.0
N_DEV = 8

ADAM_LR = 0.001
ADAM_B1 = 0.9
ADAM_B2 = 0.999
ADAM_EPS = 1e-08
ADAM_WD = 0.01
ADAM_STEP = 10

VMEM_LIMIT = 56 * 1024 * 1024
HALO = 16
HIGHEST = lax.Precision.HIGHEST

SEGMENTS = (("qk", 0, 1024), ("v", 1024, 2048), ("g", 2048, 3072), ("z", 3072, 5120), ("xs", 5120, 7168),
            ("B", 7168, 7680), ("C", 7680, 8192), ("dt", 8192, 8256), ("gates", 8256, 10304))


def _pick(n, cands):
    for c in cands:
        if n % c == 0:
            return c
    raise ValueError(f"no tile for {n}")


def _params(sem):
    return pltpu.CompilerParams(dimension_semantics=sem, vmem_limit_bytes=VMEM_LIMIT)


def _dot(a, b, dims=(((1,), (0,)), ((), ())), precision=None):
    return lax.dot_general(a, b, dims, preferred_element_type=F32, precision=precision)


def _dot_nt(a, b):
    return _dot(a, b, (((1,), (1,)), ((), ())))


def _dot_tn(a, b):
    return _dot(a, b, (((0,), (0,)), ((), ())))


def _mm(name, a, b, mode, add=None, out_dtype=F32):
    if mode == "nn":
        (M, K), N = a.shape, b.shape[1]
    elif mode == "nt":
        (M, K), N = a.shape, b.shape[0]
    else:
        (K, M), N = a.shape, b.shape[1]
    tn = _pick(N, (1408, 1024, 512, 128, 64))
    if mode == "tn":
        tm = M if M <= 1024 else _pick(M, (1408, 1024))
        tk = _pick(K, (1056, 512, 256, 128))
    else:
        tm = _pick(M, (1056, 512, 256, 128))
        tk = K if K <= 2048 else _pick(K, (1408, 1024))
    nk = K // tk
    if mode == "nn":
        a_spec = pl.BlockSpec((tm, tk), lambda n, m, k: (m, k))
        b_spec = pl.BlockSpec((tk, tn), lambda n, m, k: (k, n))
        dims = (((1,), (0,)), ((), ()))
    elif mode == "nt":
        a_spec = pl.BlockSpec((tm, tk), lambda n, m, k: (m, k))
        b_spec = pl.BlockSpec((tn, tk), lambda n, m, k: (n, k))
        dims = (((1,), (1,)), ((), ()))
    else:
        a_spec = pl.BlockSpec((tk, tm), lambda n, m, k: (k, m))
        b_spec = pl.BlockSpec((tk, tn), lambda n, m, k: (k, n))
        dims = (((0,), (0,)), ((), ()))
    o_spec = pl.BlockSpec((tm, tn), lambda n, m, k: (m, n))
    in_specs = [a_spec, b_spec] + ([o_spec] if add is not None else [])
    args = [a, b] + ([add] if add is not None else [])

    def body(*refs):
        if add is not None:
            a_ref, b_ref, r_ref, o_ref, acc = refs
        else:
            a_ref, b_ref, o_ref, acc = refs
        k = pl.program_id(2)
        p = _dot(a_ref[...].astype(BF16), b_ref[...].astype(BF16), dims)

        def finish(r):
            if add is not None:
                r = r + r_ref[...]
            o_ref[...] = r.astype(out_dtype)

        if nk == 1:
            finish(p)
        else:
            @pl.when(k == 0)
            def _():
                acc[...] = p

            @pl.when(k > 0)
            def _():
                acc[...] += p

            @pl.when(k == nk - 1)
            def _():
                finish(acc[...])

    return pl.pallas_call(
        body, name=name, grid=(N // tn, M // tm, nk), in_specs=in_specs, out_specs=o_spec,
        out_shape=jax.ShapeDtypeStruct((M, N), out_dtype),
        scratch_shapes=[pltpu.VMEM((tm, tn) if nk > 1 else (8, 128), F32)],
        compiler_params=_params(("arbitrary", "arbitrary", "arbitrary")),
    )(*args)


def _const(c):
    return lambda j: c


def _rows(name, fn, T, ncol, ins, params, outs, accs=(), halo=False):
    tm = _pick(T, (384, 256, 128))
    R = T // tm
    hb = tm // HALO
    in_specs, args = [], []
    for spec in ins:
        arr, w, cf = spec[:3]
        lead = spec[3] if len(spec) > 3 else None
        if lead is None:
            mk = lambda blk, rf, cf=cf: pl.BlockSpec(blk, lambda j, i: (rf(i), cf(j)))
            shape = lambda r, w=w: (r, w)
        else:
            mk = lambda blk, rf, cf=cf, lead=lead: pl.BlockSpec(blk, lambda j, i: (lead, rf(i), cf(j)))
            shape = lambda r, w=w: (None, r, w)
        in_specs.append(mk(shape(tm), lambda i: i))
        args.append(arr)
        if halo:
            in_specs.append(mk(shape(HALO), lambda i: jnp.maximum(i * hb - 1, 0)))
            in_specs.append(mk(shape(HALO), lambda i: jnp.minimum((i + 1) * hb, T // HALO - 1)))
            args += [arr, arr]
    for arr, w, cf in params:
        in_specs.append(pl.BlockSpec((arr.shape[0], w), lambda j, i, cf=cf: (0, cf(j))))
        args.append(arr)
    out_shape, out_specs = [], []
    for tw, w, cf, dt in outs:
        out_shape.append(jax.ShapeDtypeStruct((T, tw), dt))
        out_specs.append(pl.BlockSpec((tm, w), lambda j, i, cf=cf: (i, cf(j))))
    for r, tw, w, cf in accs:
        out_shape.append(jax.ShapeDtypeStruct((r, tw), F32))
        out_specs.append(pl.BlockSpec((r, w), lambda j, i, cf=cf: (0, cf(j))))
    n_in, n_par, n_out, n_acc = len(ins), len(params), len(outs), len(accs)

    def body(*refs):
        i = pl.program_id(1)
        vals, p = [], 0
        for _ in range(n_in):
            if halo:
                vals.append(jnp.concatenate([refs[p + 1][...], refs[p][...], refs[p + 2][...]], axis=0).astype(F32))
                p += 3
            else:
                vals.append(refs[p][...].astype(F32))
                p += 1
        pvals = [refs[p + k][...] for k in range(n_par)]
        p += n_par
        res = fn(i, *vals, *pvals)
        for k in range(n_out):
            refs[p + k][...] = res[k].astype(refs[p + k].dtype)
        p += n_out
        for k in range(n_acc):
            ref, v = refs[p + k], res[n_out + k]

            @pl.when(i == 0)
            def _(ref=ref, v=v):
                ref[...] = v

            @pl.when(i > 0)
            def _(ref=ref, v=v):
                ref[...] += v

    res = pl.pallas_call(
        body, name=name, grid=(ncol, R), in_specs=in_specs, out_specs=out_specs, out_shape=out_shape,
        compiler_params=_params(("arbitrary", "arbitrary")),
    )(*args)
    return res


def _tile_rows(T):
    return _pick(T, (384, 256, 128))


def _row_ids(i, T, halo=False):
    tm = _tile_rows(T)
    if halo:
        return i * tm - HALO + lax.broadcasted_iota(jnp.int32, (tm + 2 * HALO, 1), 0)
    return i * tm + lax.broadcasted_iota(jnp.int32, (tm, 1), 0)


def _rms(x, w):
    return x * lax.rsqrt(jnp.mean(x * x, axis=-1, keepdims=True) + EPS) * w


def _silu(x):
    return x * jax.nn.sigmoid(x)


def _conv3(x, w):
    n = x.shape[0]
    return w[0:1] * pltpu.roll(x, 1, 0) + w[1:2] * x + w[2:3] * pltpu.roll(x, n - 1, 0)


def _conv3_t(d, w):
    n = d.shape[0]
    return w[0:1] * pltpu.roll(d, n - 1, 0) + w[1:2] * d + w[2:3] * pltpu.roll(d, 1, 0)


def _center(x):
    return x[HALO:x.shape[0] - HALO]


def _retention(name, a, b, v, T):
    da = a.shape[1] // RET_HEADS
    dv = v.shape[1] // RET_HEADS
    nc = T // CHUNK
    log_gammas = [math.log(1.0 - 2.0 ** (-5.0 - h)) for h in range(RET_HEADS)]

    def body(a_ref, b_ref, v_ref, o_ref, st, st_b):
        h = pl.program_id(0)
        lg = jnp.float32(log_gammas[RET_HEADS - 1])
        for k in range(RET_HEADS - 2, -1, -1):
            lg = jnp.where(h == k, jnp.float32(log_gammas[k]), lg)
        li = lax.broadcasted_iota(jnp.int32, (CHUNK, CHUNK), 0)
        si = lax.broadcasted_iota(jnp.int32, (CHUNK, CHUNK), 1)
        dmat = jnp.exp(lg * jnp.abs(li - si).astype(F32))
        pos = lax.broadcasted_iota(jnp.int32, (CHUNK, 1), 0).astype(F32)
        kdec_f = jnp.exp((CHUNK - 1 - pos) * lg)
        qdec_f = jnp.exp((pos + 1) * lg)
        kdec_b = jnp.exp(pos * lg)
        qdec_b = jnp.exp((CHUNK - pos) * lg)
        cdec = jnp.exp(CHUNK * lg)

        def rows(n):
            return pl.ds(pl.multiple_of(n * CHUNK, CHUNK), CHUNK)

        st[...] = jnp.zeros_like(st)
        st_b[...] = jnp.zeros_like(st_b)
        o_ref[...] = jnp.zeros_like(o_ref)

        def step(m, carry):
            r = rows(m)
            av, bv, vv = a_ref[r, :], b_ref[r, :], v_ref[r, :].astype(BF16)
            s = _dot_nt(av.astype(BF16), bv.astype(BF16)) * dmat
            o_ref[r, :] += _dot(s.astype(BF16), vv) + _dot((av * qdec_f).astype(BF16), st[...].astype(BF16))
            st[...] = cdec * st[...] + _dot_tn((bv * kdec_f).astype(BF16), vv)
            r = rows(nc - 1 - m)
            av, bv, vv = a_ref[r, :], b_ref[r, :], v_ref[r, :].astype(BF16)
            o_ref[r, :] += _dot((av * qdec_b).astype(BF16), st_b[...].astype(BF16))
            st_b[...] = cdec * st_b[...] + _dot_tn((bv * kdec_b).astype(BF16), vv)
            return carry

        lax.fori_loop(0, nc, step, 0)

    return pl.pallas_call(
        body, name=name, grid=(RET_HEADS,),
        in_specs=[pl.BlockSpec((T, da), lambda h: (0, h)), pl.BlockSpec((T, da), lambda h: (0, h)),
                  pl.BlockSpec((T, dv), lambda h: (0, h))],
        out_specs=pl.BlockSpec((T, dv), lambda h: (0, h)),
        out_shape=jax.ShapeDtypeStruct((T, RET_HEADS * dv), F32),
        scratch_shapes=[pltpu.VMEM((da, dv), F32), pltpu.VMEM((da, dv), F32)],
        compiler_params=_params(("arbitrary",)),
    )(a, b, v)


def _softplus(x):
    return jnp.maximum(x, 0.0) + jnp.log1p(jnp.exp(-jnp.abs(x)))


def _lane_lo():
    return lax.broadcasted_iota(jnp.int32, (1, CHUNK), 1) < SSD_HEAD_DIM


def _pair_cols(col, j):
    return jnp.where(_lane_lo(), col[:, 2 * j:2 * j + 1], col[:, 2 * j + 1:2 * j + 2])


def _pair_rows(colr, j):
    lo = lax.broadcasted_iota(jnp.int32, (CHUNK, 1), 0) < SSD_HEAD_DIM
    return jnp.where(lo, colr[2 * j:2 * j + 1, :], colr[2 * j + 1:2 * j + 2, :])


def _onehot8(h):
    return (lax.broadcasted_iota(jnp.int32, (1, HEADS_PER_GROUP), 1) == h).astype(F32)


def _ssd_pre(d, c, rawc, rawr, bc, br, alc, alr):
    li = lax.broadcasted_iota(jnp.int32, (CHUNK, CHUNK), 0)
    si = lax.broadcasted_iota(jnp.int32, (CHUNK, CHUNK), 1)
    dif = jnp.where(d == 0, li - si, si - li)
    mask = dif >= 0
    mask_t = dif <= 0
    rowc = c * CHUNK + lax.broadcasted_iota(jnp.int32, (CHUNK, 1), 0)
    rowr = c * CHUNK + lax.broadcasted_iota(jnp.int32, (1, CHUNK), 1)
    dtc = jnp.where(rowc >= PAD_ROWS, _softplus(rawc + bc), 0.0)
    dtr = jnp.where(rowr >= PAD_ROWS, _softplus(rawr + br), 0.0)
    ac = -jnp.exp(alc)
    ar = -jnp.exp(alr)
    dlc = dtc * ac
    dlr = dtr * ar
    alpc = _dot(mask.astype(F32), dlc, precision=HIGHEST)
    alpr = _dot(dlr, mask_t.astype(F32), precision=HIGHEST)
    endc = jnp.sum(dlc, axis=0, keepdims=True)
    endr = jnp.sum(dlr, axis=1, keepdims=True)
    return dict(mask=mask, mask_t=mask_t, dtc=dtc, ac=ac, alpc=alpc, alpr=alpr, endc=endc, endr=endr,
                valid=rowc >= PAD_ROWS)


def _chunk_of(d, n, nc):
    return n + d * (nc - 1 - 2 * n)


def _ssd_small_specs(cfn):
    return [
        pl.BlockSpec((None, None, CHUNK, HEADS_PER_GROUP), lambda d, g, n: (d, g, cfn(d, n), 0)),
        pl.BlockSpec((None, None, HEADS_PER_GROUP, CHUNK), lambda d, g, n: (d, g, 0, cfn(d, n))),
        pl.BlockSpec((None, None, 1, HEADS_PER_GROUP), lambda d, g, n: (d, g, 0, 0)),
        pl.BlockSpec((None, None, HEADS_PER_GROUP, 1), lambda d, g, n: (d, g, 0, 0)),
        pl.BlockSpec((None, None, 1, HEADS_PER_GROUP), lambda d, g, n: (d, g, 0, 0)),
        pl.BlockSpec((None, None, HEADS_PER_GROUP, 1), lambda d, g, n: (d, g, 0, 0)),
    ]


def _ssd_fwd(xs, bm, cm, small, T):
    nc = T // CHUNK
    cfn = lambda d, n: _chunk_of(d, n, nc)

    def body(x_ref, b_ref, c_ref, rawc_ref, rawr_ref, bc_ref, br_ref, alc_ref, alr_ref, y_ref, hs_ref, h_scr):
        d, n = pl.program_id(0), pl.program_id(2)
        c = cfn(d, n)

        @pl.when(n == 0)
        def _():
            h_scr[...] = jnp.zeros_like(h_scr)

        q = _ssd_pre(d, c, rawc_ref[...], rawr_ref[...], bc_ref[...], br_ref[...], alc_ref[...], alr_ref[...])
        bv = b_ref[...].astype(BF16)
        cv = c_ref[...].astype(BF16)
        cb = _dot_nt(cv, bv)
        lo = _lane_lo()
        for j in range(PAIRS_PER_GROUP):
            xp = x_ref[:, j * CHUNK:(j + 1) * CHUNK]
            xd = xp * _pair_cols(q["dtc"], j)
            xdb = xd.astype(BF16)
            yi = []
            for e in range(2):
                h = 2 * j + e
                lm = jnp.exp(jnp.where(q["mask"], q["alpc"][:, h:h + 1] - q["alpr"][h:h + 1, :], -jnp.inf))
                yi.append(_dot((cb * lm).astype(BF16), xdb))
            alp = _pair_cols(q["alpc"], j)
            hp = h_scr[j]
            hs_ref[j] = hp
            yo = jnp.exp(alp) * _dot_nt(cv, hp.astype(BF16))
            y_ref[:, j * CHUNK:(j + 1) * CHUNK] = jnp.where(lo, yi[0], yi[1]) + yo
            de = jnp.exp(_pair_cols(q["endc"], j) - alp)
            h_scr[j] = jnp.exp(_pair_rows(q["endr"], j)) * hp + _dot_tn((xd * de).astype(BF16), bv)

    gw = HEADS_PER_GROUP * SSD_HEAD_DIM
    return pl.pallas_call(
        body, name="ssd_fwd", grid=(2, SSD_GROUPS, nc),
        in_specs=[pl.BlockSpec((CHUNK, gw), lambda d, g, n: (cfn(d, n), g)),
                  pl.BlockSpec((CHUNK, SSD_STATE), lambda d, g, n: (cfn(d, n), g)),
                  pl.BlockSpec((CHUNK, SSD_STATE), lambda d, g, n: (cfn(d, n), g))] + _ssd_small_specs(cfn),
        out_specs=[pl.BlockSpec((None, CHUNK, gw), lambda d, g, n: (d, cfn(d, n), g)),
                   pl.BlockSpec((None, None, None, PAIRS_PER_GROUP, CHUNK, SSD_STATE),
                                lambda d, g, n: (d, g, cfn(d, n), 0, 0, 0))],
        out_shape=[jax.ShapeDtypeStruct((2, T, SSD_HEADS * SSD_HEAD_DIM), F32),
                   jax.ShapeDtypeStruct((2, SSD_GROUPS, nc, PAIRS_PER_GROUP, CHUNK, SSD_STATE), F32)],
        scratch_shapes=[pltpu.VMEM((PAIRS_PER_GROUP, CHUNK, SSD_STATE), F32)],
        compiler_params=_params(("arbitrary", "arbitrary", "arbitrary")),
    )(xs, bm, cm, *small)


def _ssd_bwd(xs, bm, cm, small, hs, dy, T):
    nc = T // CHUNK
    cfn = lambda d, n: _chunk_of(1 - d, n, nc)

    def body(x_ref, b_ref, c_ref, rawc_ref, rawr_ref, bc_ref, br_ref, alc_ref, alr_ref, hs_ref, dy_ref,
             dx_ref, db_ref, dc_ref, draw_ref, dbias_ref, dalog_ref, dh_scr):
        d, n = pl.program_id(0), pl.program_id(2)
        c = cfn(d, n)

        @pl.when(n == 0)
        def _():
            dh_scr[...] = jnp.zeros_like(dh_scr)

        rawc, bc = rawc_ref[...], bc_ref[...]
        q = _ssd_pre(d, c, rawc, rawr_ref[...], bc, br_ref[...], alc_ref[...], alr_ref[...])
        b32, c32 = b_ref[...], c_ref[...]
        bv, cv = b32.astype(BF16), c32.astype(BF16)
        cb = _dot_nt(cv, bv)
        cbt = _dot_nt(bv, cv)
        lo = _lane_lo()
        row_lo = lax.broadcasted_iota(jnp.int32, (CHUNK, 1), 0) < SSD_HEAD_DIM
        dcb = jnp.zeros((CHUNK, CHUNK), F32)
        dcp = jnp.zeros((CHUNK, SSD_STATE), F32)
        dbp = jnp.zeros((CHUNK, SSD_STATE), F32)
        dalp = jnp.zeros((CHUNK, HEADS_PER_GROUP), F32)
        dend = jnp.zeros((1, HEADS_PER_GROUP), F32)
        ddtx = jnp.zeros((CHUNK, HEADS_PER_GROUP), F32)

        def half_sums(t):
            return (jnp.sum(jnp.where(lo, t, 0.0), axis=1, keepdims=True),
                    jnp.sum(jnp.where(lo, 0.0, t), axis=1, keepdims=True))

        for j in range(PAIRS_PER_GROUP):
            xp = x_ref[:, j * CHUNK:(j + 1) * CHUNK]
            dtp = _pair_cols(q["dtc"], j)
            xd = xp * dtp
            xdb = xd.astype(BF16)
            dyp = dy_ref[:, j * CHUNK:(j + 1) * CHUNK]
            dyb = dyp.astype(BF16)
            hn = hs_ref[j]
            hnb = hn.astype(BF16)
            dh1 = dh_scr[j]
            dh1b = dh1.astype(BF16)
            alp = _pair_cols(q["alpc"], j)
            ea = jnp.exp(alp)
            de = jnp.exp(_pair_cols(q["endc"], j) - alp)
            dxi = []
            for e in range(2):
                h = 2 * j + e
                ac_, ar_ = q["alpc"][:, h:h + 1], q["alpr"][h:h + 1, :]
                lm = jnp.exp(jnp.where(q["mask"], ac_ - ar_, -jnp.inf))
                mt = cbt * jnp.exp(jnp.where(q["mask_t"], ar_ - ac_, -jnp.inf))
                dxi.append(_dot(mt.astype(BF16), dyb))
                dyeb_h = (jnp.where(lo, dyp, 0.0) if e == 0 else jnp.where(lo, 0.0, dyp)).astype(BF16)
                gl = _dot_nt(dyeb_h, xdb) * lm
                dcb = dcb + gl
                ra = jnp.sum(gl * cb, axis=1, keepdims=True) - jnp.sum(_dot_nt(xdb, dyeb_h) * mt, axis=1, keepdims=True)
                dalp = dalp + ra * _onehot8(h)
            y_off = ea * _dot_nt(cv, hnb)
            dxs_state = de * _dot_nt(bv, dh1b)
            dxd = jnp.where(lo, dxi[0], dxi[1]) + dxs_state
            dyeb = (dyp * ea).astype(BF16)
            dcp = dcp + _dot(dyeb, hnb)
            dbp = dbp + _dot((xd * de).astype(BF16), dh1b)
            dh_scr[j] = jnp.exp(_pair_rows(q["endr"], j)) * dh1 + _dot_tn(dyeb, cv)
            r0, r1 = half_sums(dyp * y_off - xd * dxs_state)
            dalp = dalp + r0 * _onehot8(2 * j) + r1 * _onehot8(2 * j + 1)
            t0, t1 = half_sums(xd * dxs_state)
            u = jnp.sum(dh1 * hn, axis=1, keepdims=True)
            u0 = jnp.sum(jnp.where(row_lo, u, 0.0), axis=0, keepdims=True)
            u1 = jnp.sum(jnp.where(row_lo, 0.0, u), axis=0, keepdims=True)
            eend = jnp.exp(q["endc"])
            dend = dend + (jnp.sum(t0, axis=0, keepdims=True) + eend * u0) * _onehot8(2 * j) \
                        + (jnp.sum(t1, axis=0, keepdims=True) + eend * u1) * _onehot8(2 * j + 1)
            dx_ref[:, j * CHUNK:(j + 1) * CHUNK] = dxd * dtp
            w0, w1 = half_sums(dxd * xp)
            ddtx = ddtx + w0 * _onehot8(2 * j) + w1 * _onehot8(2 * j + 1)

        dcbb = dcb.astype(BF16)
        dc_ref[...] = dcp + _dot(dcbb, bv)
        db_ref[...] = dbp + _dot_tn(dcbb, cv)
        ddl = _dot(q["mask_t"].astype(F32), dalp, precision=HIGHEST) + dend
        ddt = ddl * q["ac"] + ddtx
        draw = jnp.where(q["valid"], ddt * jax.nn.sigmoid(rawc + bc), 0.0)
        draw_ref[...] = draw
        dbias = jnp.sum(draw, axis=0, keepdims=True)
        dalog = jnp.sum(ddl * q["dtc"], axis=0, keepdims=True) * q["ac"]

        @pl.when(n == 0)
        def _():
            dbias_ref[...] = dbias
            dalog_ref[...] = dalog

        @pl.when(n > 0)
        def _():
            dbias_ref[...] += dbias
            dalog_ref[...] += dalog

    gw = HEADS_PER_GROUP * SSD_HEAD_DIM
    acc_spec = pl.BlockSpec((None, None, 1, HEADS_PER_GROUP), lambda d, g, n: (d, g, 0, 0))
    return pl.pallas_call(
        body, name="ssd_bwd", grid=(2, SSD_GROUPS, nc),
        in_specs=[pl.BlockSpec((CHUNK, gw), lambda d, g, n: (cfn(d, n), g)),
                  pl.BlockSpec((CHUNK, SSD_STATE), lambda d, g, n: (cfn(d, n), g)),
                  pl.BlockSpec((CHUNK, SSD_STATE), lambda d, g, n: (cfn(d, n), g))] + _ssd_small_specs(cfn) + [
                  pl.BlockSpec((None, None, None, PAIRS_PER_GROUP, CHUNK, SSD_STATE),
                               lambda d, g, n: (d, g, cfn(d, n), 0, 0, 0)),
                  pl.BlockSpec((CHUNK, gw), lambda d, g, n: (cfn(d, n), g))],
        out_specs=[pl.BlockSpec((None, CHUNK, gw), lambda d, g, n: (d, cfn(d, n), g)),
                   pl.BlockSpec((None, CHUNK, SSD_STATE), lambda d, g, n: (d, cfn(d, n), g)),
                   pl.BlockSpec((None, CHUNK, SSD_STATE), lambda d, g, n: (d, cfn(d, n), g)),
                   pl.BlockSpec((None, None, CHUNK, HEADS_PER_GROUP), lambda d, g, n: (d, g, cfn(d, n), 0)),
                   acc_spec, acc_spec],
        out_shape=[jax.ShapeDtypeStruct((2, T, SSD_HEADS * SSD_HEAD_DIM), F32),
                   jax.ShapeDtypeStruct((2, T, SSD_GROUPS * SSD_STATE), F32),
                   jax.ShapeDtypeStruct((2, T, SSD_GROUPS * SSD_STATE), F32),
                   jax.ShapeDtypeStruct((2, SSD_GROUPS, T, HEADS_PER_GROUP), F32),
                   jax.ShapeDtypeStruct((2, SSD_GROUPS, 1, HEADS_PER_GROUP), F32),
                   jax.ShapeDtypeStruct((2, SSD_GROUPS, 1, HEADS_PER_GROUP), F32)],
        scratch_shapes=[pltpu.VMEM((PAIRS_PER_GROUP, CHUNK, SSD_STATE), F32)],
        compiler_params=_params(("arbitrary", "arbitrary", "arbitrary")),
    )(xs, bm, cm, *small, hs, dy)


def _rot(x, cs, sn):
    return x * cs + pltpu.roll(x, RET_QK_DIM // 2, 1) * sn


def _rot_t(d, cs, sn):
    return d * cs + pltpu.roll(d * sn, RET_QK_DIM // 2, 1)


def _ret_post(y, g, w):
    parts = []
    for h in range(RET_HEADS):
        yh = y[:, h * RET_V_DIM:(h + 1) * RET_V_DIM]
        mu = jnp.mean(yh, axis=-1, keepdims=True)
        var = jnp.mean(jnp.square(yh - mu), axis=-1, keepdims=True)
        parts.append((yh - mu) * lax.rsqrt(var + EPS))
    return _silu(g) * (jnp.concatenate(parts, axis=1) * w)


def _ssd_post(yf, yb, xs, z, dskip, w):
    y = (yf + yb + xs * dskip) * _silu(z)
    return y * lax.rsqrt(jnp.mean(y * y, axis=-1, keepdims=True) + EPS) * w


def _merge(gates, yr, ys, valid):
    m = jax.nn.sigmoid(gates[:, :D_MODEL]) * yr + jax.nn.sigmoid(gates[:, D_MODEL:]) * ys
    return jnp.where(valid, m, 0.0)


def _rope_tables(T):
    half = RET_QK_DIM // 2
    inv = ROPE_BASE ** (-jnp.arange(half, dtype=F32) / half)
    pos = (jnp.arange(T) - PAD_ROWS).astype(F32)
    ang = pos[:, None] * inv[None, :]
    cos, sin = jnp.cos(ang), jnp.sin(ang)
    return jnp.concatenate([cos, cos], axis=1), jnp.concatenate([-sin, sin], axis=1)


def _per_group(v):
    c = v.reshape(SSD_GROUPS, 1, HEADS_PER_GROUP)
    return c, c.reshape(SSD_GROUPS, HEADS_PER_GROUP, 1)


def _local_step(x, target, w, tick, late_weights, early_grads, in_grads):
    S = x.shape[0]
    T = S + CHUNK
    tm = _tile_rows(T)
    c0 = _const(0)

    h0 = jnp.concatenate([jnp.zeros((PAD_ROWS, D_MODEL), F32), w["meta_tokens"], x], axis=0)
    tgt = jnp.concatenate([jnp.zeros((CHUNK, D_MODEL), F32), target], axis=0)
    w_in = {name: w["w_in_t"][a:b] for name, a, b in SEGMENTS}
    w_in["dt"] = jnp.pad(w_in["dt"], ((0, CHUNK - 2 * SSD_HEADS), (0, 0)))

    def norm_cast(name, h, nw):
        return _rows(name, lambda i, hv, wv: (_rms(hv, wv),), T, 1, [(h, D_MODEL, c0)], [(nw, D_MODEL, c0)],
                     [(D_MODEL, D_MODEL, c0, BF16)])[0]

    u = norm_cast("norm_mix", h0, w["norm_mix_w"] + tick)
    proj = {name: _mm("proj_" + name, u, w_in[name], "nt", out_dtype=F32 if name == "dt" else BF16)
            for name, _, _ in SEGMENTS}

    cs, sn = _rope_tables(T)
    scale = RET_QK_DIM ** -0.5

    def rot_fn(i, qk, csv, snv):
        q = [_rot(qk[:, h * 128:(h + 1) * 128], csv, snv) for h in range(RET_HEADS)]
        k = [_rot(qk[:, (RET_HEADS + h) * 128:(RET_HEADS + h + 1) * 128], csv, snv) * scale for h in range(RET_HEADS)]
        return jnp.concatenate(q, axis=1), jnp.concatenate(k, axis=1)

    qr, kr = _rows("rotary", rot_fn, T, 1, [(proj["qk"], 1024, c0), (cs, 128, c0), (sn, 128, c0)], [],
                   [(512, 512, c0, F32), (512, 512, c0, F32)])
    y_ret = _retention("retention", qr, kr, proj["v"], T)
    a_ret = _rows("ret_post", lambda i, y, g, gw: (_ret_post(y, g, gw),), T, 1,
                  [(y_ret, 1024, c0), (proj["g"], 1024, c0)], [(w["ret_gn_w"], 1024, c0)],
                  [(1024, 1024, c0, BF16)])[0]

    conv_w = {"xs": w["w_ssd_conv"][:, :2048], "B": w["w_ssd_conv"][:, 2048:2560], "C": w["w_ssd_conv"][:, 2560:]}
    conv_b = {"xs": w["b_ssd_conv"][:, :2048], "B": w["b_ssd_conv"][:, 2048:2560], "C": w["b_ssd_conv"][:, 2560:]}

    def ssd_conv_fn(i, xe, cw, cb):
        r = _row_ids(i, T, True)
        xe = jnp.where((r >= 0) & (r < T), xe, 0.0)
        return (_center(jnp.where(r >= PAD_ROWS, _silu(_conv3(xe, cw) + cb), 0.0)),)

    act = {}
    for name in ("xs", "B", "C"):
        wd = proj[name].shape[1]
        cw = 512
        act[name] = _rows("ssd_conv_" + name, ssd_conv_fn, T, wd // cw, [(proj[name], cw, lambda j: j)],
                          [(conv_w[name], cw, lambda j: j), (conv_b[name], cw, lambda j: j)],
                          [(wd, cw, lambda j: j, BF16)], halo=True)[0]

    raw = proj["dt"][:, :2 * SSD_HEADS].reshape(T, 2, SSD_GROUPS, HEADS_PER_GROUP)
    rawc = raw.transpose(1, 2, 0, 3)
    rawr = raw.transpose(1, 2, 3, 0)
    bias = [_per_group(w["dt_bias_f"]), _per_group(w["dt_bias_b"])]
    alog = [_per_group(w["a_log_f"]), _per_group(w["a_log_b"])]
    small = (rawc, rawr, jnp.stack([bias[0][0], bias[1][0]]), jnp.stack([bias[0][1], bias[1][1]]),
             jnp.stack([alog[0][0], alog[1][0]]), jnp.stack([alog[0][1], alog[1][1]]))
    y_dir, states = _ssd_fwd(act["xs"], act["B"], act["C"], small, T)

    dskip_e = jnp.repeat(w["d_skip"], SSD_HEAD_DIM, axis=1)
    gcol = lambda j: j
    gw_ = 512
    a_ssd = _rows("ssd_post", lambda i, yf, yb, xv, zv, dk, nw: (_ssd_post(yf, yb, xv, zv, dk, nw),), T, SSD_GROUPS,
                  [(y_dir, gw_, gcol, 0), (y_dir, gw_, gcol, 1), (act["xs"], gw_, gcol), (proj["z"], gw_, gcol)],
                  [(dskip_e, gw_, gcol), (w["ssd_norm_w"], gw_, gcol)], [(2048, gw_, gcol, BF16)])[0]

    w = dict(w, **late_weights(a_ssd))
    w_up_g, w_up_u = w["w_ffn_up_t"][:D_FF], w["w_ffn_up_t"][D_FF:]
    y_ret_o = _mm("ret_out", a_ret, w["w_ret_out"], "nn", out_dtype=BF16)
    y_ssd_o = _mm("ssd_out", a_ssd, w["w_ssd_out"], "nn", out_dtype=BF16)

    def merge_fn(i, gates, yr, ys):
        return (_merge(gates, yr, ys, _row_ids(i, T) >= PAD_ROWS),)

    merged = _rows("merge", merge_fn, T, 1, [(proj["gates"], 2048, c0), (y_ret_o, 1024, c0), (y_ssd_o, 1024, c0)], [],
                   [(1024, 1024, c0, BF16)])[0]
    h1 = _mm("mix_out", merged, w["w_out"], "nn", add=h0)

    n2 = norm_cast("norm_ffn", h1, w["norm_ffn_w"])
    fg_pre = _mm("ffn_up_g", n2, w_up_g, "nt", out_dtype=BF16)
    fu_pre = _mm("ffn_up_u", n2, w_up_u, "nt", out_dtype=BF16)
    cwg, cwu = w["w_ffn_conv"][:, :D_FF], w["w_ffn_conv"][:, D_FF:]
    cbg, cbu = w["b_ffn_conv"][:, :D_FF], w["b_ffn_conv"][:, D_FF:]
    fcol = lambda j: j
    fw = 1408

    def ffn_act_fn(i, ge, ue, wg, wu, bg, bu):
        return (_center(_silu(_conv3(ge, wg) + bg) * (_conv3(ue, wu) + bu)),)

    def ext_valid(i):
        r = _row_ids(i, T, True)
        return (r >= 0) & (r < T)

    def ffn_act_masked(i, ge, ue, wg, wu, bg, bu):
        v = ext_valid(i)
        return ffn_act_fn(i, jnp.where(v, ge, 0.0), jnp.where(v, ue, 0.0), wg, wu, bg, bu)

    a2 = _rows("ffn_act", ffn_act_masked, T, D_FF // fw, [(fg_pre, fw, fcol), (fu_pre, fw, fcol)],
               [(cwg, fw, fcol), (cwu, fw, fcol), (cbg, fw, fcol), (cbu, fw, fcol)], [(D_FF, fw, fcol, BF16)],
               halo=True)[0]
    h2 = _mm("ffn_down", a2, w["w_ffn_down"], "nn", add=h1)

    fnw = w["final_norm_w"].reshape(1, D_MODEL)

    def loss_fn(i, hv, tv, nw):
        valid = _row_ids(i, T) >= CHUNK
        y, vjp = jax.vjp(_rms, hv, nw)
        diff = jnp.where(valid, y - tv, 0.0)
        dh, dw = vjp(diff * (1.0 / D_MODEL))
        part = 0.5 / D_MODEL * jnp.sum(jnp.sum(diff * diff, axis=1, keepdims=True), axis=0, keepdims=True)
        return dh, jnp.broadcast_to(part, (1, 128)), dw

    dh2, loss_acc, d_fnw = _rows("loss", loss_fn, T, 1, [(h2, D_MODEL, c0), (tgt, D_MODEL, c0)], [(fnw, D_MODEL, c0)],
                                 [(D_MODEL, D_MODEL, c0, F32)], [(1, 128, 128, c0), (1, D_MODEL, D_MODEL, c0)])
    loss = loss_acc[0, 0]
    grads = {"final_norm_w": d_fnw.reshape(D_MODEL)}

    da2 = _mm("d_ffn_act", dh2, w["w_ffn_down"], "nt", out_dtype=BF16)
    grads["w_ffn_down"] = _mm("g_ffn_down", a2, dh2, "tn", out_dtype=BF16)

    def ffn_bwd_fn(i, ge, ue, de, wg, wu, bg, bu):
        v = ext_valid(i)
        ge, ue, de = jnp.where(v, ge, 0.0), jnp.where(v, ue, 0.0), jnp.where(v, de, 0.0)
        fg = _conv3(ge, wg) + bg
        fu = _conv3(ue, wu) + bu
        sg = jax.nn.sigmoid(fg)
        dfg = de * fu * (sg * (1.0 + fg * (1.0 - sg)))
        dfu = de * (fg * sg)
        n = ge.shape[0]

        def wgrad(df, xe):
            df_c = _center(df)
            return jnp.concatenate([jnp.sum(df_c * _center(pltpu.roll(xe, 1, 0)), axis=0, keepdims=True),
                                    jnp.sum(df_c * _center(xe), axis=0, keepdims=True),
                                    jnp.sum(df_c * _center(pltpu.roll(xe, n - 1, 0)), axis=0, keepdims=True)], axis=0)

        return (_center(_conv3_t(dfg, wg)), _center(_conv3_t(dfu, wu)), wgrad(dfg, ge), wgrad(dfu, ue),
                jnp.sum(_center(dfg), axis=0, keepdims=True), jnp.sum(_center(dfu), axis=0, keepdims=True))

    dfg_pre, dfu_pre, g_cwg, g_cwu, g_cbg, g_cbu = _rows(
        "ffn_act_bwd", ffn_bwd_fn, T, D_FF // fw, [(fg_pre, fw, fcol), (fu_pre, fw, fcol), (da2, fw, fcol)],
        [(cwg, fw, fcol), (cwu, fw, fcol), (cbg, fw, fcol), (cbu, fw, fcol)],
        [(D_FF, fw, fcol, BF16), (D_FF, fw, fcol, BF16)],
        [(3, D_FF, fw, fcol), (3, D_FF, fw, fcol), (1, D_FF, fw, fcol), (1, D_FF, fw, fcol)], halo=True)
    grads["w_ffn_conv"] = jnp.concatenate([g_cwg, g_cwu], axis=1)
    grads["b_ffn_conv"] = jnp.concatenate([g_cbg, g_cbu], axis=1)
    dn2 = _mm("d_norm_ffn_g", dfg_pre, w_up_g, "nn")
    dn2 = _mm("d_norm_ffn_u", dfu_pre, w_up_u, "nn", add=dn2)
    grads["w_ffn_up_t"] = jnp.concatenate([_mm("g_ffn_up_g", dfg_pre, n2, "tn", out_dtype=BF16), _mm("g_ffn_up_u", dfu_pre, n2, "tn", out_dtype=BF16)],
                                          axis=0)

    def norm_bwd(name, h, nw, dn, dres):
        def fn(i, hv, dnv, drv, wv):
            _, vjp = jax.vjp(_rms, hv, wv)
            dh, dw = vjp(dnv)
            return dh + drv, dw
        return _rows(name, fn, T, 1, [(h, D_MODEL, c0), (dn, D_MODEL, c0), (dres, D_MODEL, c0)], [(nw, D_MODEL, c0)],
                     [(D_MODEL, D_MODEL, c0, F32)], [(1, D_MODEL, D_MODEL, c0)])

    dh1, grads["norm_ffn_w"] = norm_bwd("norm_ffn_bwd", h1, w["norm_ffn_w"], dn2, dh2)

    dmerged = _mm("d_merged", dh1, w["w_out"], "nt", out_dtype=BF16)
    grads["w_out"] = _mm("g_out", merged, dh1, "tn", out_dtype=BF16)

    def merge_bwd_fn(i, gates, yr, ys, dm):
        valid = _row_ids(i, T) >= PAD_ROWS
        _, vjp = jax.vjp(lambda a, b, c: _merge(a, b, c, valid), gates, yr, ys)
        return vjp(dm)

    dgates, dyr, dys = _rows("merge_bwd", merge_bwd_fn, T, 1,
                             [(proj["gates"], 2048, c0), (y_ret_o, 1024, c0), (y_ssd_o, 1024, c0), (dmerged, 1024, c0)],
                             [], [(2048, 2048, c0, BF16), (1024, 1024, c0, BF16), (1024, 1024, c0, BF16)])
    dproj = {"gates": dgates}

    da_ssd = _mm("d_ssd_act", dys, w["w_ssd_out"], "nt", out_dtype=BF16)
    grads["w_ssd_out"] = _mm("g_ssd_out", a_ssd, dys, "tn", out_dtype=BF16)

    def ssd_post_bwd_fn(i, yf, yb, xv, zv, da, dk, nw):
        _, vjp = jax.vjp(_ssd_post, yf, yb, xv, zv, dk, nw)
        dyf, _, dxv, dzv, ddk, dnw = vjp(da)
        return dyf, dxv, dzv, ddk, dnw

    dy_ssd, dxs_skip, dproj["z"], g_dskip_e, grads["ssd_norm_w"] = _rows(
        "ssd_post_bwd", ssd_post_bwd_fn, T, SSD_GROUPS,
        [(y_dir, gw_, gcol, 0), (y_dir, gw_, gcol, 1), (act["xs"], gw_, gcol), (proj["z"], gw_, gcol),
         (da_ssd, gw_, gcol)],
        [(dskip_e, gw_, gcol), (w["ssd_norm_w"], gw_, gcol)],
        [(2048, gw_, gcol, BF16), (2048, gw_, gcol, BF16), (2048, gw_, gcol, BF16)],
        [(1, 2048, gw_, gcol), (1, 2048, gw_, gcol)])
    grads["d_skip"] = g_dskip_e.reshape(SSD_HEADS, SSD_HEAD_DIM).sum(axis=1).reshape(1, SSD_HEADS)

    dxs_dir, db_dir, dc_dir, draw, g_bias, g_alog = _ssd_bwd(act["xs"], act["B"], act["C"], small, states, dy_ssd, T)
    grads["dt_bias_f"], grads["dt_bias_b"] = g_bias[0].reshape(1, SSD_HEADS), g_bias[1].reshape(1, SSD_HEADS)
    grads["a_log_f"], grads["a_log_b"] = g_alog[0].reshape(1, SSD_HEADS), g_alog[1].reshape(1, SSD_HEADS)
    d_dt = draw.transpose(2, 0, 1, 3).reshape(T, 2 * SSD_HEADS)
    dproj["dt"] = jnp.pad(d_dt, ((0, 0), (0, CHUNK - 2 * SSD_HEADS))).astype(BF16)

    def make_conv_bwd(nsum):
        def fn(i, xe, *rest):
            ds, (cw, cb) = rest[:nsum], rest[nsum:]
            r = _row_ids(i, T, True)
            dact = ds[0]
            for t in ds[1:]:
                dact = dact + t
            dact = jnp.where((r >= PAD_ROWS) & (r < T), dact, 0.0)
            xe = jnp.where((r >= 0) & (r < T), xe, 0.0)
            pre = _conv3(xe, cw) + cb
            sg = jax.nn.sigmoid(pre)
            dpre = dact * (sg * (1.0 + pre * (1.0 - sg)))
            n = xe.shape[0]
            dpc = _center(dpre)
            dw = jnp.concatenate([jnp.sum(dpc * _center(pltpu.roll(xe, 1, 0)), axis=0, keepdims=True),
                                  jnp.sum(dpc * _center(xe), axis=0, keepdims=True),
                                  jnp.sum(dpc * _center(pltpu.roll(xe, n - 1, 0)), axis=0, keepdims=True)], axis=0)
            return _center(_conv3_t(dpre, cw)), dw, jnp.sum(dpc, axis=0, keepdims=True)
        return fn

    g_cw, g_cb = {}, {}
    cots = {"xs": [(dxs_dir, 512, gcol, 0), (dxs_dir, 512, gcol, 1), (dxs_skip, 512, gcol)],
            "B": [(db_dir, 512, gcol, 0), (db_dir, 512, gcol, 1)],
            "C": [(dc_dir, 512, gcol, 0), (dc_dir, 512, gcol, 1)]}
    for name in ("xs", "B", "C"):
        wd = proj[name].shape[1]
        dproj[name], g_cw[name], g_cb[name] = _rows(
            "ssd_conv_bwd_" + name, make_conv_bwd(len(cots[name])), T, wd // 512,
            [(proj[name], 512, gcol)] + cots[name], [(conv_w[name], 512, gcol), (conv_b[name], 512, gcol)],
            [(wd, 512, gcol, BF16)], [(3, wd, 512, gcol), (1, wd, 512, gcol)], halo=True)
    grads["w_ssd_conv"] = jnp.concatenate([g_cw["xs"], g_cw["B"], g_cw["C"]], axis=1)
    grads["b_ssd_conv"] = jnp.concatenate([g_cb["xs"], g_cb["B"], g_cb["C"]], axis=1)

    da_ret = _mm("d_ret_act", dyr, w["w_ret_out"], "nt", out_dtype=BF16)
    grads["w_ret_out"] = _mm("g_ret_out", a_ret, dyr, "tn", out_dtype=BF16)
    tick = early_grads({n: grads.pop(n) for n in ("w_ffn_up_t", "w_ret_out", "w_ssd_out", "w_out", "w_ffn_down")})

    def ret_post_bwd_fn(i, y, g, da, gw):
        _, vjp = jax.vjp(_ret_post, y, g, gw)
        return vjp(da)

    dy_ret, dproj["g"], grads["ret_gn_w"] = _rows(
        "ret_post_bwd", ret_post_bwd_fn, T, 1, [(y_ret, 1024, c0), (proj["g"], 1024, c0), (da_ret, 1024, c0)],
        [(w["ret_gn_w"] + tick, 1024, c0)], [(1024, 1024, c0, BF16), (1024, 1024, c0, BF16)], [(1, 1024, 1024, c0)])
    dproj["v"] = _retention("retention_dv", kr, qr, dy_ret, T)
    dqr = _retention("retention_dq", dy_ret, proj["v"], kr, T)
    dkr = _retention("retention_dk", proj["v"], dy_ret, qr, T)

    def rot_bwd_fn(i, dq, dk, csv, snv):
        parts = [_rot_t(dq[:, h * 128:(h + 1) * 128], csv, snv) for h in range(RET_HEADS)]
        parts += [_rot_t(dk[:, h * 128:(h + 1) * 128] * scale, csv, snv) for h in range(RET_HEADS)]
        return (jnp.concatenate(parts, axis=1),)

    dproj["qk"] = _rows("rotary_bwd", rot_bwd_fn, T, 1, [(dqr, 512, c0), (dkr, 512, c0), (cs, 128, c0), (sn, 128, c0)],
                        [], [(1024, 1024, c0, BF16)])[0]

    g_in = [_mm("g_in_" + name, dproj[name], u, "tn", out_dtype=BF16) for name, _, _ in SEGMENTS]
    g_in[7] = g_in[7][:2 * SSD_HEADS]
    tick = in_grads(jnp.concatenate(g_in, axis=0))
    du = _mm("d_u_dt", dproj["dt"] + tick.astype(BF16), w_in["dt"], "nn")
    for name, _, _ in SEGMENTS:
        if name != "dt":
            du = _mm("d_u_" + name, dproj[name], w_in[name], "nn", add=du)
    dh0, grads["norm_mix_w"] = norm_bwd("norm_mix_bwd", h0, w["norm_mix_w"], du, dh1)
    grads["meta_tokens"] = dh0[PAD_ROWS:CHUNK]
    return loss, dh0[CHUNK:], grads


MESH_ID = pl.DeviceIdType.MESH
ANY = pl.BlockSpec(memory_space=pl.ANY)


def _me_and_peers():
    x, y, c = lax.axis_index("x"), lax.axis_index("y"), lax.axis_index("c")
    peers = []
    for k in range(1, N_DEV):
        px = 1 - x if k & 4 else x
        py = 1 - y if k & 2 else y
        pc = 1 - c if k & 1 else c
        peers.append(((px, py, pc), 4 * px + 2 * py + pc))
    return 4 * x + 2 * y + c, peers


def _push_blocks(name, src, per_peer):
    blk = src.shape[1:] if per_peer else src.shape

    def body(src_ref, out_ref, send_sems, recv_sems, local_sem):
        me, peers = _me_and_peers()
        mine = src_ref.at[me] if per_peer else src_ref
        local = pltpu.make_async_copy(mine, out_ref.at[me], local_sem)
        local.start()
        sends = []
        for k, (dev, idx) in enumerate(peers):
            cp = pltpu.make_async_remote_copy(
                src_ref=src_ref.at[idx] if per_peer else src_ref, dst_ref=out_ref.at[me],
                send_sem=send_sems.at[k], recv_sem=recv_sems.at[k], device_id=dev, device_id_type=MESH_ID)
            cp.start()
            sends.append(cp)
        for k, (dev, idx) in enumerate(peers):
            pltpu.make_async_remote_copy(
                src_ref=mine, dst_ref=out_ref.at[idx], send_sem=send_sems.at[k], recv_sem=recv_sems.at[k],
                device_id=dev, device_id_type=MESH_ID).wait_recv()
        for cp in sends:
            cp.wait_send()
        local.wait()

    return pl.pallas_call(
        body, name=name, in_specs=[ANY], out_specs=ANY,
        out_shape=jax.ShapeDtypeStruct((N_DEV,) + tuple(blk), src.dtype),
        scratch_shapes=[pltpu.SemaphoreType.DMA((N_DEV - 1,)), pltpu.SemaphoreType.DMA((N_DEV - 1,)),
                        pltpu.SemaphoreType.DMA],
    )(src)


def _gather_two_level(name, src):
    def body(x_ref, out_ref, send_sems, recv_sems, local_sem):
        x, y, c = lax.axis_index("x"), lax.axis_index("y"), lax.axis_index("c")
        me, sibling = (x, y, c), (x, y, 1 - c)
        chips = [(1 - x, y), (x, 1 - y), (1 - x, 1 - y)]

        def rows(px, py, pc):
            return out_ref.at[4 * px + 2 * py + pc]

        def copy(k, block, to, src_ref=None):
            return pltpu.make_async_remote_copy(
                src_ref=rows(*block) if src_ref is None else src_ref, dst_ref=rows(*block),
                send_sem=send_sems.at[k], recv_sem=recv_sems.at[k], device_id=to, device_id_type=MESH_ID)

        mine = pltpu.make_async_copy(x_ref, rows(*me), local_sem)
        mine.start()
        first = [copy(0, me, sibling, x_ref)] + [copy(1 + j, me, (*chip, c), x_ref) for j, chip in enumerate(chips)]
        for cp in first:
            cp.start()
        passed = [copy(4 + j, (*chip, c), sibling) for j, chip in enumerate(chips)]
        for j, chip in enumerate(chips):
            copy(1 + j, (*chip, c), me).wait_recv()
            passed[j].start()
        copy(0, sibling, me).wait_recv()
        for j, chip in enumerate(chips):
            copy(4 + j, (*chip, 1 - c), me).wait_recv()
        for cp in first + passed:
            cp.wait_send()
        mine.wait()

    return pl.pallas_call(
        body, name=name, in_specs=[ANY], out_specs=ANY,
        out_shape=jax.ShapeDtypeStruct((N_DEV,) + tuple(src.shape), src.dtype),
        scratch_shapes=[pltpu.SemaphoreType.DMA((N_DEV - 1,)), pltpu.SemaphoreType.DMA((N_DEV - 1,)),
                        pltpu.SemaphoreType.DMA],
    )(src)


HBM = pl.BlockSpec(memory_space=pltpu.HBM)
SEM = pl.BlockSpec(memory_space=pltpu.SEMAPHORE)
EFFECT = pltpu.SideEffectType.DATAFLOW_SIDE_EFFECTING


def _peer_copy(src_ref, land_ref, send_sems, recv_sems, per_peer, me, k, dev, idx, receiving):
    return pltpu.make_async_remote_copy(
        src_ref=src_ref.at[idx] if per_peer else src_ref, dst_ref=land_ref.at[idx if receiving else me],
        send_sem=send_sems.at[k], recv_sem=recv_sems.at[k], device_id=dev, device_id_type=MESH_ID)


def _push_start(name, src, per_peer):
    blk = src.shape[1:] if per_peer else src.shape
    land_shape = (N_DEV,) + tuple(blk)

    def body(src_ref, land_ref, send_sems, recv_sems, src_thru, land_thru, token):
        me, peers = _me_and_peers()
        for k, (dev, idx) in enumerate(peers):
            _peer_copy(src_ref, land_ref, send_sems, recv_sems, per_peer, me, k, dev, idx, False).start()
        token[...] = jnp.zeros_like(token)

    return pl.pallas_call(
        body, name=name,
        out_shape=(pltpu.SemaphoreType.DMA((N_DEV - 1,)), pltpu.SemaphoreType.DMA((N_DEV - 1,)),
                   pltpu.HBM(src.shape, src.dtype), pltpu.HBM(land_shape, src.dtype),
                   jax.ShapeDtypeStruct((8, 128), F32)),
        in_specs=(HBM, HBM), out_specs=(SEM, SEM, HBM, HBM, pl.BlockSpec(memory_space=pltpu.VMEM)),
        input_output_aliases={0: 2, 1: 3}, compiler_params=pltpu.CompilerParams(has_side_effects=EFFECT),
    )(pltpu.with_memory_space_constraint(src, pltpu.HBM),
      pltpu.with_memory_space_constraint(lax.empty(land_shape, src.dtype), pltpu.HBM))


def _push_wait(name, send_sems, recv_sems, src_thru, land_thru, after, per_peer):
    def body(src_ref, land_ref, send_sems, recv_sems, after_ref, src_out, land_out):
        me, peers = _me_and_peers()
        for k, (dev, idx) in enumerate(peers):
            cp = _peer_copy(src_ref, land_ref, send_sems, recv_sems, per_peer, me, k, dev, idx, True)
            cp.wait_send()
            cp.wait_recv()

    return pl.pallas_call(
        body, name=name,
        out_shape=(pltpu.HBM(src_thru.shape, src_thru.dtype), pltpu.HBM(land_thru.shape, land_thru.dtype)),
        in_specs=(HBM, HBM, SEM, SEM, ANY), out_specs=(HBM, HBM), input_output_aliases={0: 0, 1: 1},
        compiler_params=pltpu.CompilerParams(has_side_effects=EFFECT),
    )(src_thru, land_thru, send_sems, recv_sems, after)


def _sum_blocks(name, blocks):
    _, R, C = blocks.shape
    tc = _pick(C, (128,))

    def body(b_ref, o_ref):
        acc = b_ref[0].astype(F32)
        for k in range(1, N_DEV):
            acc = acc + b_ref[k].astype(F32)
        o_ref[...] = acc

    return pl.pallas_call(
        body, name=name, grid=(C // tc,), in_specs=[pl.BlockSpec((N_DEV, R, tc), lambda j: (0, 0, j))],
        out_specs=pl.BlockSpec((R, tc), lambda j: (0, j)), out_shape=jax.ShapeDtypeStruct((R, C), F32),
        compiler_params=_params(("arbitrary",)),
    )(blocks)


def _adamw(name, w, g, m, v):
    R, C = w.shape
    tr = R if R <= 512 else _pick(R, (256, 128, 8))
    spec = pl.BlockSpec((tr, C), lambda i: (i, 0))

    def body(w_ref, g_ref, m_ref, v_ref, d_ref, mo_ref, vo_ref):
        gv = g_ref[...]
        mn = ADAM_B1 * m_ref[...] + (1.0 - ADAM_B1) * gv
        vn = ADAM_B2 * v_ref[...] + (1.0 - ADAM_B2) * jnp.square(gv)
        m_hat = mn / (1.0 - ADAM_B1 ** ADAM_STEP)
        v_hat = vn / (1.0 - ADAM_B2 ** ADAM_STEP)
        d_ref[...] = -ADAM_LR * (m_hat / (jnp.sqrt(v_hat) + ADAM_EPS) + ADAM_WD * w_ref[...])
        mo_ref[...] = mn
        vo_ref[...] = vn

    return pl.pallas_call(
        body, name=name, grid=(R // tr,), in_specs=[spec] * 4, out_specs=[spec] * 3,
        out_shape=[jax.ShapeDtypeStruct((R, C), F32)] * 3, compiler_params=_params(("arbitrary",)),
    )(w, g, m, v)


WEIGHTS = ("meta_tokens", "norm_mix_w", "w_in", "ret_gn_w", "w_ret_out", "w_ssd_conv", "b_ssd_conv", "dt_bias_f",
           "dt_bias_b", "a_log_f", "a_log_b", "d_skip", "ssd_norm_w", "w_ssd_out", "w_out", "norm_ffn_w", "w_ffn_up",
           "w_ffn_conv", "b_ffn_conv", "w_ffn_down", "final_norm_w")
BIG = (("w_in", 1288, True), ("w_ffn_up", 704, True), ("w_ret_out", 128, False), ("w_ssd_out", 256, False),
       ("w_out", 128, False), ("w_ffn_down", 352, False))
REPLICATED = ("norm_mix_w", "ret_gn_w", "b_ssd_conv", "dt_bias_f", "dt_bias_b", "a_log_f", "a_log_b", "d_skip",
              "ssd_norm_w", "norm_ffn_w", "b_ffn_conv", "final_norm_w")
SMALL_SHARDED = (("meta_tokens", 16, 1024), ("w_ssd_conv", 3, 3072), ("w_ffn_conv", 3, 5632))


BIG_IN, BIG_REST = BIG[:1], BIG[1:]


def _pack_big(tree, group):
    parts = []
    for name, _, transposed in group:
        a = tree[name][0]
        parts.append(a.T if transposed else a)
    return jnp.concatenate(parts, axis=0)


def _unpack_big(slab, group):
    out, r0 = {}, 0
    for name, r, transposed in group:
        a = slab[r0:r0 + r]
        out[name] = (a.T if transposed else a)[None]
        r0 += r
    return out


def _pack_flat(arrays, rows):
    flat = jnp.concatenate([a.reshape(-1) for a in arrays])
    return jnp.pad(flat, (0, rows * D_MODEL - flat.shape[0])).reshape(rows, D_MODEL)


def _unpack_flat(slab, shapes):
    flat, out, o = slab.reshape(-1), [], 0
    for s in shapes:
        n = math.prod(s)
        out.append(flat[o:o + n].reshape(s))
        o += n
    return out


def kernel(x, meta_tokens, norm_mix_w, w_in, ret_gn_w, w_ret_out, w_ssd_conv, b_ssd_conv, dt_bias_f, dt_bias_b, a_log_f, a_log_b, d_skip, ssd_norm_w, w_ssd_out, w_out, norm_ffn_w, w_ffn_up, w_ffn_conv, b_ffn_conv, w_ffn_down, final_norm_w, loss_target, m_meta_tokens, m_norm_mix_w, m_w_in, m_ret_gn_w, m_w_ret_out, m_w_ssd_conv, m_b_ssd_conv, m_dt_bias_f, m_dt_bias_b, m_a_log_f, m_a_log_b, m_d_skip, m_ssd_norm_w, m_w_ssd_out, m_w_out, m_norm_ffn_w, m_w_ffn_up, m_w_ffn_conv, m_b_ffn_conv, m_w_ffn_down, m_final_norm_w, v_meta_tokens, v_norm_mix_w, v_w_in, v_ret_gn_w, v_w_ret_out, v_w_ssd_conv, v_b_ssd_conv, v_dt_bias_f, v_dt_bias_b, v_a_log_f, v_a_log_b, v_d_skip, v_ssd_norm_w, v_w_ssd_out, v_w_out, v_norm_ffn_w, v_w_ffn_up, v_w_ffn_conv, v_b_ffn_conv, v_w_ffn_down, v_final_norm_w):
    given = dict(locals())
    wt = {n: given[n] for n in WEIGHTS}
    mt = {n: given["m_" + n] for n in WEIGHTS}
    vt = {n: given["v_" + n] for n in WEIGHTS}
    me = 4 * lax.axis_index("x") + 2 * lax.axis_index("y") + lax.axis_index("c")

    small_names = [n for n, _, _ in SMALL_SHARDED]
    small_local = lambda tree: [tree[n].reshape(r, c // N_DEV) for n, r, c in SMALL_SHARDED]
    all_in = _gather_two_level("gather_w_in", _pack_big(wt, BIG_IN).astype(BF16))
    rest_src, all_in = lax.optimization_barrier((_pack_big(wt, BIG_REST).astype(BF16), all_in))
    rest_flight = _push_start("gather_rest_start", rest_src, False)
    all_s = _push_blocks("gather_small", _pack_flat(small_local(wt), 8), False).reshape(N_DEV, -1)
    full = {"w_in_t": all_in.reshape(-1, D_MODEL)}

    def land_with_own(flight, after, per_peer, name):
        src, land = _push_wait(name, *flight[:4], after, per_peer)
        own = lax.dynamic_slice_in_dim(src, me, 1, axis=0) if per_peer else src[None]
        return lax.dynamic_update_slice_in_dim(land, own, me, axis=0)

    def late_weights(after):
        all_rest = land_with_own(rest_flight, after, False, "gather_rest_wait")
        out, r0 = {}, 0
        for name, r, transposed in BIG_REST:
            out[name + ("_t" if transposed else "")] = all_rest[:, r0:r0 + r].reshape(N_DEV * r, D_MODEL)
            r0 += r
        return out

    flights = {}

    def start_exchange(key, group, gd):
        g_blocks = jnp.concatenate(
            [gd[name + ("_t" if t else "")].reshape(N_DEV, r, D_MODEL) for name, r, t in group], axis=1)
        flights[key] = _push_start("exchange_" + key + "_start", g_blocks.astype(BF16), True)
        return flights[key][4][0, 0]

    o = 0
    for name, r, c in SMALL_SHARDED:
        n = r * c // N_DEV
        full[name] = all_s[:, o:o + n].reshape(N_DEV, r, c // N_DEV).transpose(1, 0, 2).reshape(r, c)
        o += n
    for name in REPLICATED:
        full[name] = wt[name]

    loss, grad_x, g = _local_step(
        x[0], loss_target[0], full, rest_flight[4][0, 0], late_weights,
        lambda gd: start_exchange("rest", BIG_REST, gd), lambda gi: start_exchange("in", BIG_IN, {"w_in_t": gi}))

    last = g["norm_mix_w"]
    g_slabs = {key: _sum_blocks("sum_" + key, land_with_own(flights[key], last, True, "exchange_" + key + "_wait"))
               for key in ("rest", "in")}
    small_parts = [g[n] for n in REPLICATED] + [g[n] for n in small_names] + [loss.reshape(1)]
    g_small = _sum_blocks("sum_small", _push_blocks("gather_small_grads", _pack_flat(small_parts, 64), False))
    small_red = _unpack_flat(g_small, [wt[n].shape for n in REPLICATED] + [(r, c) for _, r, c in SMALL_SHARDED] + [(1,)])
    grads = dict(zip(REPLICATED, small_red[:len(REPLICATED)]))
    for (name, r, c), red in zip(SMALL_SHARDED, small_red[len(REPLICATED):-1]):
        grads[name] = lax.dynamic_slice(red, (0, me * (c // N_DEV)), (r, c // N_DEV)).reshape(wt[name].shape)
    loss_all = small_red[-1][0]
    delta, new_m, new_v = {}, {}, {}
    for key, group in (("in", BIG_IN), ("rest", BIG_REST)):
        grads.update(_unpack_big(g_slabs[key], group))
        for name, _, _ in group:
            d, mn, vn = _adamw("adamw_" + name, wt[name][0], grads[name][0], mt[name][0], vt[name][0])
            delta[name], new_m[name], new_v[name] = d[None], mn[None], vn[None]

    rest = list(REPLICATED) + small_names
    shapes = [wt[n].shape for n in rest]
    pack_rest = lambda tree: _pack_flat([tree[n] for n in rest], 24)
    d_rest, m_rest, v_rest = _adamw("adamw_small", pack_rest(wt), pack_rest(grads), pack_rest(mt), pack_rest(vt))
    delta.update(zip(rest, _unpack_flat(d_rest, shapes)))
    new_m.update(zip(rest, _unpack_flat(m_rest, shapes)))
    new_v.update(zip(rest, _unpack_flat(v_rest, shapes)))

    return (loss_all, grad_x[None], *[grads[n] for n in WEIGHTS], *[delta[n] for n in WEIGHTS],
            *[new_m[n] for n in WEIGHTS], *[new_v[n] for n in WEIGHTS])
```

```python
import functools
import math

import jax
import jax.numpy as jnp
from jax import lax
from jax.experimental import pallas as pl
from jax.experimental.pallas import tpu as pltpu

F32 = jnp.float32
BF16 = jnp.bfloat16

D_MODEL = 1024
CHUNK = 128
N_META = 16
PAD_ROWS = CHUNK - N_META
RET_HEADS = 4
RET_QK_DIM = 128
RET_V_DIM = 256
SSD_HEADS = 32
SSD_HEAD_DIM = 64
SSD_GROUPS = 4
SSD_STATE = 128
HEADS_PER_GROUP = SSD_HEADS // SSD_GROUPS
PAIRS_PER_GROUP = HEADS_PER_GROUP // 2
D_FF = 2816
EPS = 1e-6
ROPE_BASE = 10000.0
N_DEV = 8

ADAM_LR = 0.001
ADAM_B1 = 0.9
ADAM_B2 = 0.999
ADAM_EPS = 1e-08
ADAM_WD = 0.01
ADAM_STEP = 10

VMEM_LIMIT = 56 * 1024 * 1024
HALO = 16
HIGHEST = lax.Precision.HIGHEST

SEGMENTS = (("qk", 0, 1024), ("v", 1024, 2048), ("g", 2048, 3072), ("z", 3072, 5120), ("xs", 5120, 7168),
            ("B", 7168, 7680), ("C", 7680, 8192), ("dt", 8192, 8256), ("gates", 8256, 10304))


def _pick(n, cands):
    for c in cands:
        if n % c == 0:
            return c
    raise ValueError(f"no tile for {n}")


def _params(sem):
    return pltpu.CompilerParams(dimension_semantics=sem, vmem_limit_bytes=VMEM_LIMIT)


def _dot(a, b, dims=(((1,), (0,)), ((), ())), precision=None):
    return lax.dot_general(a, b, dims, preferred_element_type=F32, precision=precision)


def _dot_nt(a, b):
    return _dot(a, b, (((1,), (1,)), ((), ())))


def _dot_tn(a, b):
    return _dot(a, b, (((0,), (0,)), ((), ())))


def _mm(name, a, b, mode, add=None, out_dtype=F32):
    if mode == "nn":
        (M, K), N = a.shape, b.shape[1]
    elif mode == "nt":
        (M, K), N = a.shape, b.shape[0]
    else:
        (K, M), N = a.shape, b.shape[1]
    tn = _pick(N, (1408, 1024, 512, 128, 64))
    if mode == "tn":
        tm = M if M <= 1024 else _pick(M, (1408, 1024))
        tk = _pick(K, (1056, 512, 256, 128))
    else:
        tm = _pick(M, (1056, 512, 256, 128))
        tk = K if K <= 2048 else _pick(K, (1408, 1024))
    nk = K // tk
    if mode == "nn":
        a_spec = pl.BlockSpec((tm, tk), lambda n, m, k: (m, k))
        b_spec = pl.BlockSpec((tk, tn), lambda n, m, k: (k, n))
        dims = (((1,), (0,)), ((), ()))
    elif mode == "nt":
        a_spec = pl.BlockSpec((tm, tk), lambda n, m, k: (m, k))
        b_spec = pl.BlockSpec((tn, tk), lambda n, m, k: (n, k))
        dims = (((1,), (1,)), ((), ()))
    else:
        a_spec = pl.BlockSpec((tk, tm), lambda n, m, k: (k, m))
        b_spec = pl.BlockSpec((tk, tn), lambda n, m, k: (k, n))
        dims = (((0,), (0,)), ((), ()))
    o_spec = pl.BlockSpec((tm, tn), lambda n, m, k: (m, n))
    in_specs = [a_spec, b_spec] + ([o_spec] if add is not None else [])
    args = [a, b] + ([add] if add is not None else [])

    def body(*refs):
        if add is not None:
            a_ref, b_ref, r_ref, o_ref, acc = refs
        else:
            a_ref, b_ref, o_ref, acc = refs
        k = pl.program_id(2)
        p = _dot(a_ref[...].astype(BF16), b_ref[...].astype(BF16), dims)

        def finish(r):
            if add is not None:
                r = r + r_ref[...]
            o_ref[...] = r.astype(out_dtype)

        if nk == 1:
            finish(p)
        else:
            @pl.when(k == 0)
            def _():
                acc[...] = p

            @pl.when(k > 0)
            def _():
                acc[...] += p

            @pl.when(k == nk - 1)
            def _():
                finish(acc[...])

    return pl.pallas_call(
        body, name=name, grid=(N // tn, M // tm, nk), in_specs=in_specs, out_specs=o_spec,
        out_shape=jax.ShapeDtypeStruct((M, N), out_dtype),
        scratch_shapes=[pltpu.VMEM((tm, tn) if nk > 1 else (8, 128), F32)],
        compiler_params=_params(("arbitrary", "arbitrary", "arbitrary")),
    )(*args)


def _const(c):
    return lambda j: c


def _rows(name, fn, T, ncol, ins, params, outs, accs=(), halo=False):
    tm = _pick(T, (384, 256, 128))
    R = T // tm
    hb = tm // HALO
    in_specs, args = [], []
    for spec in ins:
        arr, w, cf = spec[:3]
        lead = spec[3] if len(spec) > 3 else None
        if lead is None:
            mk = lambda blk, rf, cf=cf: pl.BlockSpec(blk, lambda j, i: (rf(i), cf(j)))
            shape = lambda r, w=w: (r, w)
        else:
            mk = lambda blk, rf, cf=cf, lead=lead: pl.BlockSpec(blk, lambda j, i: (lead, rf(i), cf(j)))
            shape = lambda r, w=w: (None, r, w)
        in_specs.append(mk(shape(tm), lambda i: i))
        args.append(arr)
        if halo:
            in_specs.append(mk(shape(HALO), lambda i: jnp.maximum(i * hb - 1, 0)))
            in_specs.append(mk(shape(HALO), lambda i: jnp.minimum((i + 1) * hb, T // HALO - 1)))
            args += [arr, arr]
    for arr, w, cf in params:
        in_specs.append(pl.BlockSpec((arr.shape[0], w), lambda j, i, cf=cf: (0, cf(j))))
        args.append(arr)
    out_shape, out_specs = [], []
    for tw, w, cf, dt in outs:
        out_shape.append(jax.ShapeDtypeStruct((T, tw), dt))
        out_specs.append(pl.BlockSpec((tm, w), lambda j, i, cf=cf: (i, cf(j))))
    for r, tw, w, cf in accs:
        out_shape.append(jax.ShapeDtypeStruct((r, tw), F32))
        out_specs.append(pl.BlockSpec((r, w), lambda j, i, cf=cf: (0, cf(j))))
    n_in, n_par, n_out, n_acc = len(ins), len(params), len(outs), len(accs)

    def body(*refs):
        i = pl.program_id(1)
        vals, p = [], 0
        for _ in range(n_in):
            if halo:
                vals.append(jnp.concatenate([refs[p + 1][...], refs[p][...], refs[p + 2][...]], axis=0).astype(F32))
                p += 3
            else:
                vals.append(refs[p][...].astype(F32))
                p += 1
        pvals = [refs[p + k][...] for k in range(n_par)]
        p += n_par
        res = fn(i, *vals, *pvals)
        for k in range(n_out):
            refs[p + k][...] = res[k].astype(refs[p + k].dtype)
        p += n_out
        for k in range(n_acc):
            ref, v = refs[p + k], res[n_out + k]

            @pl.when(i == 0)
            def _(ref=ref, v=v):
                ref[...] = v

            @pl.when(i > 0)
            def _(ref=ref, v=v):
                ref[...] += v

    res = pl.pallas_call(
        body, name=name, grid=(ncol, R), in_specs=in_specs, out_specs=out_specs, out_shape=out_shape,
        compiler_params=_params(("arbitrary", "arbitrary")),
    )(*args)
    return res


def _tile_rows(T):
    return _pick(T, (384, 256, 128))


def _row_ids(i, T, halo=False):
    tm = _tile_rows(T)
    if halo:
        return i * tm - HALO + lax.broadcasted_iota(jnp.int32, (tm + 2 * HALO, 1), 0)
    return i * tm + lax.broadcasted_iota(jnp.int32, (tm, 1), 0)


def _rms(x, w):
    return x * lax.rsqrt(jnp.mean(x * x, axis=-1, keepdims=True) + EPS) * w


def _silu(x):
    return x * jax.nn.sigmoid(x)


def _conv3(x, w):
    n = x.shape[0]
    return w[0:1] * pltpu.roll(x, 1, 0) + w[1:2] * x + w[2:3] * pltpu.roll(x, n - 1, 0)


def _conv3_t(d, w):
    n = d.shape[0]
    return w[0:1] * pltpu.roll(d, n - 1, 0) + w[1:2] * d + w[2:3] * pltpu.roll(d, 1, 0)


def _center(x):
    return x[HALO:x.shape[0] - HALO]


def _retention(name, a, b, v, T):
    da = a.shape[1] // RET_HEADS
    dv = v.shape[1] // RET_HEADS
    nc = T // CHUNK
    log_gammas = [math.log(1.0 - 2.0 ** (-5.0 - h)) for h in range(RET_HEADS)]

    def body(a_ref, b_ref, v_ref, o_ref, st, st_b):
        h = pl.program_id(0)
        lg = jnp.float32(log_gammas[RET_HEADS - 1])
        for k in range(RET_HEADS - 2, -1, -1):
            lg = jnp.where(h == k, jnp.float32(log_gammas[k]), lg)
        li = lax.broadcasted_iota(jnp.int32, (CHUNK, CHUNK), 0)
        si = lax.broadcasted_iota(jnp.int32, (CHUNK, CHUNK), 1)
        dmat = jnp.exp(lg * jnp.abs(li - si).astype(F32))
        pos = lax.broadcasted_iota(jnp.int32, (CHUNK, 1), 0).astype(F32)
        kdec_f = jnp.exp((CHUNK - 1 - pos) * lg)
        qdec_f = jnp.exp((pos + 1) * lg)
        kdec_b = jnp.exp(pos * lg)
        qdec_b = jnp.exp((CHUNK - pos) * lg)
        cdec = jnp.exp(CHUNK * lg)

        def rows(n):
            return pl.ds(pl.multiple_of(n * CHUNK, CHUNK), CHUNK)

        st[...] = jnp.zeros_like(st)
        st_b[...] = jnp.zeros_like(st_b)
        o_ref[...] = jnp.zeros_like(o_ref)

        def step(m, carry):
            r = rows(m)
            av, bv, vv = a_ref[r, :], b_ref[r, :], v_ref[r, :].astype(BF16)
            s = _dot_nt(av.astype(BF16), bv.astype(BF16)) * dmat
            o_ref[r, :] += _dot(s.astype(BF16), vv) + _dot((av * qdec_f).astype(BF16), st[...].astype(BF16))
            st[...] = cdec * st[...] + _dot_tn((bv * kdec_f).astype(BF16), vv)
            r = rows(nc - 1 - m)
            av, bv, vv = a_ref[r, :], b_ref[r, :], v_ref[r, :].astype(BF16)
            o_ref[r, :] += _dot((av * qdec_b).astype(BF16), st_b[...].astype(BF16))
            st_b[...] = cdec * st_b[...] + _dot_tn((bv * kdec_b).astype(BF16), vv)
            return carry

        lax.fori_loop(0, nc, step, 0)

    return pl.pallas_call(
        body, name=name, grid=(RET_HEADS,),
        in_specs=[pl.BlockSpec((T, da), lambda h: (0, h)), pl.BlockSpec((T, da), lambda h: (0, h)),
                  pl.BlockSpec((T, dv), lambda h: (0, h))],
        out_specs=pl.BlockSpec((T, dv), lambda h: (0, h)),
        out_shape=jax.ShapeDtypeStruct((T, RET_HEADS * dv), F32),
        scratch_shapes=[pltpu.VMEM((da, dv), F32), pltpu.VMEM((da, dv), F32)],
        compiler_params=_params(("arbitrary",)),
    )(a, b, v)


def _softplus(x):
    return jnp.maximum(x, 0.0) + jnp.log1p(jnp.exp(-jnp.abs(x)))


def _lane_lo():
    return lax.broadcasted_iota(jnp.int32, (1, CHUNK), 1) < SSD_HEAD_DIM


def _pair_cols(col, j):
    return jnp.where(_lane_lo(), col[:, 2 * j:2 * j + 1], col[:, 2 * j + 1:2 * j + 2])


def _pair_rows(colr, j):
    lo = lax.broadcasted_iota(jnp.int32, (CHUNK, 1), 0) < SSD_HEAD_DIM
    return jnp.where(lo, colr[2 * j:2 * j + 1, :], colr[2 * j + 1:2 * j + 2, :])


def _onehot8(h):
    return (lax.broadcasted_iota(jnp.int32, (1, HEADS_PER_GROUP), 1) == h).astype(F32)


def _ssd_pre(d, c, rawc, rawr, bc, br, alc, alr):
    li = lax.broadcasted_iota(jnp.int32, (CHUNK, CHUNK), 0)
    si = lax.broadcasted_iota(jnp.int32, (CHUNK, CHUNK), 1)
    dif = li - si if d == 0 else si - li
    mask = dif >= 0
    mask_t = dif <= 0
    rowc = c * CHUNK + lax.broadcasted_iota(jnp.int32, (CHUNK, 1), 0)
    rowr = c * CHUNK + lax.broadcasted_iota(jnp.int32, (1, CHUNK), 1)
    dtc = jnp.where(rowc >= PAD_ROWS, _softplus(rawc + bc), 0.0)
    dtr = jnp.where(rowr >= PAD_ROWS, _softplus(rawr + br), 0.0)
    ac = -jnp.exp(alc)
    ar = -jnp.exp(alr)
    dlc = dtc * ac
    dlr = dtr * ar
    alpc = _dot(mask.astype(F32), dlc, precision=HIGHEST)
    alpr = _dot(dlr, mask_t.astype(F32), precision=HIGHEST)
    endc = jnp.sum(dlc, axis=0, keepdims=True)
    endr = jnp.sum(dlr, axis=1, keepdims=True)
    return dict(mask=mask, mask_t=mask_t, dtc=dtc, ac=ac, alpc=alpc, alpr=alpr, endc=endc, endr=endr,
                valid=rowc >= PAD_ROWS)


def _chunk_of(d, n, nc):
    return n + d * (nc - 1 - 2 * n)


GROUP_WIDTH = HEADS_PER_GROUP * SSD_HEAD_DIM


def _ssd_in_specs(d, cfn):
    return [
        pl.BlockSpec((CHUNK, GROUP_WIDTH), lambda g, n: (cfn(d, n), g)),
        pl.BlockSpec((CHUNK, SSD_STATE), lambda g, n: (cfn(d, n), g)),
        pl.BlockSpec((CHUNK, SSD_STATE), lambda g, n: (cfn(d, n), g)),
        pl.BlockSpec((None, None, CHUNK, HEADS_PER_GROUP), lambda g, n: (d, g, cfn(d, n), 0)),
        pl.BlockSpec((None, None, HEADS_PER_GROUP, CHUNK), lambda g, n: (d, g, 0, cfn(d, n))),
        pl.BlockSpec((None, None, 1, HEADS_PER_GROUP), lambda g, n: (d, g, 0, 0)),
        pl.BlockSpec((None, None, HEADS_PER_GROUP, 1), lambda g, n: (d, g, 0, 0)),
        pl.BlockSpec((None, None, 1, HEADS_PER_GROUP), lambda g, n: (d, g, 0, 0)),
        pl.BlockSpec((None, None, HEADS_PER_GROUP, 1), lambda g, n: (d, g, 0, 0)),
    ]


N_SSD_IN = 9


def _ssd_fwd(xs, bm, cm, small, T):
    nc = T // CHUNK
    cfn = lambda d, n: _chunk_of(d, n, nc)

    def one_direction(d, n, ins, y_ref, hs_ref, h_scr):
        x_ref, b_ref, c_ref, rawc_ref, rawr_ref, bc_ref, br_ref, alc_ref, alr_ref = ins
        c = cfn(d, n)
        q = _ssd_pre(d, c, rawc_ref[...], rawr_ref[...], bc_ref[...], br_ref[...], alc_ref[...], alr_ref[...])
        bv = b_ref[...].astype(BF16)
        cv = c_ref[...].astype(BF16)
        cb = _dot_nt(cv, bv)
        lo = _lane_lo()
        for j in range(PAIRS_PER_GROUP):
            xp = x_ref[:, j * CHUNK:(j + 1) * CHUNK]
            xd = xp * _pair_cols(q["dtc"], j)
            xdb = xd.astype(BF16)
            yi = []
            for e in range(2):
                h = 2 * j + e
                lm = jnp.exp(jnp.where(q["mask"], q["alpc"][:, h:h + 1] - q["alpr"][h:h + 1, :], -jnp.inf))
                yi.append(_dot((cb * lm).astype(BF16), xdb))
            alp = _pair_cols(q["alpc"], j)
            hp = h_scr[j]
            hs_ref[j] = hp
            yo = jnp.exp(alp) * _dot_nt(cv, hp.astype(BF16))
            y_ref[:, j * CHUNK:(j + 1) * CHUNK] = (jnp.where(lo, yi[0], yi[1]) + yo).astype(y_ref.dtype)
            de = jnp.exp(_pair_cols(q["endc"], j) - alp)
            h_scr[j] = jnp.exp(_pair_rows(q["endr"], j)) * hp + _dot_tn((xd * de).astype(BF16), bv)

    def body(*refs):
        n = pl.program_id(1)
        ins, (y_f, y_b, hs_f, hs_b, h_scr) = refs[:2 * N_SSD_IN], refs[2 * N_SSD_IN:]

        @pl.when(n == 0)
        def _():
            h_scr[...] = jnp.zeros_like(h_scr)

        one_direction(0, n, ins[:N_SSD_IN], y_f, hs_f, h_scr.at[0])
        one_direction(1, n, ins[N_SSD_IN:], y_b, hs_b, h_scr.at[1])

    y_spec = lambda d: pl.BlockSpec((CHUNK, GROUP_WIDTH), lambda g, n: (cfn(d, n), g))
    hs_spec = lambda d: pl.BlockSpec((None, None, PAIRS_PER_GROUP, CHUNK, SSD_STATE),
                                     lambda g, n: (g, cfn(d, n), 0, 0, 0))
    y_shape = jax.ShapeDtypeStruct((T, SSD_HEADS * SSD_HEAD_DIM), BF16)
    hs_shape = jax.ShapeDtypeStruct((SSD_GROUPS, nc, PAIRS_PER_GROUP, CHUNK, SSD_STATE), F32)
    y_f, y_b, hs_f, hs_b = pl.pallas_call(
        body, name="ssd_fwd", grid=(SSD_GROUPS, nc),
        in_specs=_ssd_in_specs(0, cfn) + _ssd_in_specs(1, cfn),
        out_specs=[y_spec(0), y_spec(1), hs_spec(0), hs_spec(1)],
        out_shape=[y_shape, y_shape, hs_shape, hs_shape],
        scratch_shapes=[pltpu.VMEM((2, PAIRS_PER_GROUP, CHUNK, SSD_STATE), F32)],
        compiler_params=_params(("arbitrary", "arbitrary")),
    )(xs, bm, cm, *small, xs, bm, cm, *small)
    return (y_f, y_b), (hs_f, hs_b)


def _ssd_bwd(xs, bm, cm, small, hs, dy, T):
    nc = T // CHUNK
    cfn = lambda d, n: _chunk_of(1 - d, n, nc)

    def one_direction(d, n, ins, outs, dh_scr):
        x_ref, b_ref, c_ref, rawc_ref, rawr_ref, bc_ref, br_ref, alc_ref, alr_ref, hs_ref, dy_ref = ins
        dx_ref, db_ref, dc_ref, draw_ref, dbias_ref, dalog_ref = outs
        c = cfn(d, n)
        rawc, bc = rawc_ref[...], bc_ref[...]
        q = _ssd_pre(d, c, rawc, rawr_ref[...], bc, br_ref[...], alc_ref[...], alr_ref[...])
        b32, c32 = b_ref[...], c_ref[...]
        bv, cv = b32.astype(BF16), c32.astype(BF16)
        cb = _dot_nt(cv, bv)
        cbt = _dot_nt(bv, cv)
        lo = _lane_lo()
        row_lo = lax.broadcasted_iota(jnp.int32, (CHUNK, 1), 0) < SSD_HEAD_DIM
        dcb = jnp.zeros((CHUNK, CHUNK), F32)
        dcp = jnp.zeros((CHUNK, SSD_STATE), F32)
        dbp = jnp.zeros((CHUNK, SSD_STATE), F32)
        dalp = jnp.zeros((CHUNK, HEADS_PER_GROUP), F32)
        dend = jnp.zeros((1, HEADS_PER_GROUP), F32)
        ddtx = jnp.zeros((CHUNK, HEADS_PER_GROUP), F32)

        def half_sums(t):
            return (jnp.sum(jnp.where(lo, t, 0.0), axis=1, keepdims=True),
                    jnp.sum(jnp.where(lo, 0.0, t), axis=1, keepdims=True))

        for j in range(PAIRS_PER_GROUP):
            xp = x_ref[:, j * CHUNK:(j + 1) * CHUNK]
            dtp = _pair_cols(q["dtc"], j)
            xd = xp * dtp
            xdb = xd.astype(BF16)
            dyp = dy_ref[:, j * CHUNK:(j + 1) * CHUNK]
            dyb = dyp.astype(BF16)
            hn = hs_ref[j]
            hnb = hn.astype(BF16)
            dh1 = dh_scr[j]
            dh1b = dh1.astype(BF16)
            alp = _pair_cols(q["alpc"], j)
            ea = jnp.exp(alp)
            de = jnp.exp(_pair_cols(q["endc"], j) - alp)
            dxi = []
            for e in range(2):
                h = 2 * j + e
                ac_, ar_ = q["alpc"][:, h:h + 1], q["alpr"][h:h + 1, :]
                lm = jnp.exp(jnp.where(q["mask"], ac_ - ar_, -jnp.inf))
                mt = cbt * jnp.exp(jnp.where(q["mask_t"], ar_ - ac_, -jnp.inf))
                dxi.append(_dot(mt.astype(BF16), dyb))
                dyeb_h = (jnp.where(lo, dyp, 0.0) if e == 0 else jnp.where(lo, 0.0, dyp)).astype(BF16)
                gl = _dot_nt(dyeb_h, xdb) * lm
                dcb = dcb + gl
                ra = jnp.sum(gl * cb, axis=1, keepdims=True) - jnp.sum(_dot_nt(xdb, dyeb_h) * mt, axis=1, keepdims=True)
                dalp = dalp + ra * _onehot8(h)
            y_off = ea * _dot_nt(cv, hnb)
            dxs_state = de * _dot_nt(bv, dh1b)
            dxd = jnp.where(lo, dxi[0], dxi[1]) + dxs_state
            dyeb = (dyp * ea).astype(BF16)
            dcp = dcp + _dot(dyeb, hnb)
            dbp = dbp + _dot((xd * de).astype(BF16), dh1b)
            dh_scr[j] = jnp.exp(_pair_rows(q["endr"], j)) * dh1 + _dot_tn(dyeb, cv)
            r0, r1 = half_sums(dyp * y_off - xd * dxs_state)
            dalp = dalp + r0 * _onehot8(2 * j) + r1 * _onehot8(2 * j + 1)
            t0, t1 = half_sums(xd * dxs_state)
            u = jnp.sum(dh1 * hn, axis=1, keepdims=True)
            u0 = jnp.sum(jnp.where(row_lo, u, 0.0), axis=0, keepdims=True)
            u1 = jnp.sum(jnp.where(row_lo, 0.0, u), axis=0, keepdims=True)
            eend = jnp.exp(q["endc"])
            dend = dend + (jnp.sum(t0, axis=0, keepdims=True) + eend * u0) * _onehot8(2 * j) \
                        + (jnp.sum(t1, axis=0, keepdims=True) + eend * u1) * _onehot8(2 * j + 1)
            dx_ref[:, j * CHUNK:(j + 1) * CHUNK] = (dxd * dtp).astype(dx_ref.dtype)
            w0, w1 = half_sums(dxd * xp)
            ddtx = ddtx + w0 * _onehot8(2 * j) + w1 * _onehot8(2 * j + 1)

        dcbb = dcb.astype(BF16)
        dc_ref[...] = (dcp + _dot(dcbb, bv)).astype(dc_ref.dtype)
        db_ref[...] = (dbp + _dot_tn(dcbb, cv)).astype(db_ref.dtype)
        ddl = _dot(q["mask_t"].astype(F32), dalp, precision=HIGHEST) + dend
        ddt = ddl * q["ac"] + ddtx
        draw = jnp.where(q["valid"], ddt * jax.nn.sigmoid(rawc + bc), 0.0)
        draw_ref[...] = draw
        dbias = jnp.sum(draw, axis=0, keepdims=True)
        dalog = jnp.sum(ddl * q["dtc"], axis=0, keepdims=True) * q["ac"]

        @pl.when(n == 0)
        def _():
            dbias_ref[...] = dbias
            dalog_ref[...] = dalog

        @pl.when(n > 0)
        def _():
            dbias_ref[...] += dbias
            dalog_ref[...] += dalog

    n_in, n_out = N_SSD_IN + 2, 6

    def body(*refs):
        n = pl.program_id(1)
        ins, outs, dh_scr = refs[:2 * n_in], refs[2 * n_in:2 * (n_in + n_out)], refs[-1]

        @pl.when(n == 0)
        def _():
            dh_scr[...] = jnp.zeros_like(dh_scr)

        one_direction(0, n, ins[:n_in], outs[:n_out], dh_scr.at[0])
        one_direction(1, n, ins[n_in:], outs[n_out:], dh_scr.at[1])

    def in_specs(d):
        return _ssd_in_specs(d, cfn) + [
            pl.BlockSpec((None, None, PAIRS_PER_GROUP, CHUNK, SSD_STATE), lambda g, n: (g, cfn(d, n), 0, 0, 0)),
            pl.BlockSpec((CHUNK, GROUP_WIDTH), lambda g, n: (cfn(d, n), g))]

    def out_specs(d):
        acc = pl.BlockSpec((None, 1, HEADS_PER_GROUP), lambda g, n: (g, 0, 0))
        return [pl.BlockSpec((CHUNK, GROUP_WIDTH), lambda g, n: (cfn(d, n), g)),
                pl.BlockSpec((CHUNK, SSD_STATE), lambda g, n: (cfn(d, n), g)),
                pl.BlockSpec((CHUNK, SSD_STATE), lambda g, n: (cfn(d, n), g)),
                pl.BlockSpec((None, CHUNK, HEADS_PER_GROUP), lambda g, n: (g, cfn(d, n), 0)), acc, acc]

    out_shape = [jax.ShapeDtypeStruct((T, SSD_HEADS * SSD_HEAD_DIM), BF16),
                 jax.ShapeDtypeStruct((T, SSD_GROUPS * SSD_STATE), BF16),
                 jax.ShapeDtypeStruct((T, SSD_GROUPS * SSD_STATE), BF16),
                 jax.ShapeDtypeStruct((SSD_GROUPS, T, HEADS_PER_GROUP), F32),
                 jax.ShapeDtypeStruct((SSD_GROUPS, 1, HEADS_PER_GROUP), F32),
                 jax.ShapeDtypeStruct((SSD_GROUPS, 1, HEADS_PER_GROUP), F32)]
    res = pl.pallas_call(
        body, name="ssd_bwd", grid=(SSD_GROUPS, nc),
        in_specs=in_specs(0) + in_specs(1), out_specs=out_specs(0) + out_specs(1), out_shape=out_shape * 2,
        scratch_shapes=[pltpu.VMEM((2, PAIRS_PER_GROUP, CHUNK, SSD_STATE), F32)],
        compiler_params=_params(("arbitrary", "arbitrary")),
    )(xs, bm, cm, *small, hs[0], dy, xs, bm, cm, *small, hs[1], dy)
    return [(res[k], res[n_out + k]) for k in range(n_out)]


def _rot(x, cs, sn):
    return x * cs + pltpu.roll(x, RET_QK_DIM // 2, 1) * sn


def _rot_t(d, cs, sn):
    return d * cs + pltpu.roll(d * sn, RET_QK_DIM // 2, 1)


def _ret_post(y, g, w):
    parts = []
    for h in range(RET_HEADS):
        yh = y[:, h * RET_V_DIM:(h + 1) * RET_V_DIM]
        mu = jnp.mean(yh, axis=-1, keepdims=True)
        var = jnp.mean(jnp.square(yh - mu), axis=-1, keepdims=True)
        parts.append((yh - mu) * lax.rsqrt(var + EPS))
    return _silu(g) * (jnp.concatenate(parts, axis=1) * w)


def _ssd_post(yf, yb, xs, z, dskip, w):
    y = (yf + yb + xs * dskip) * _silu(z)
    return y * lax.rsqrt(jnp.mean(y * y, axis=-1, keepdims=True) + EPS) * w


def _merge(gates, yr, ys, valid):
    m = jax.nn.sigmoid(gates[:, :D_MODEL]) * yr + jax.nn.sigmoid(gates[:, D_MODEL:]) * ys
    return jnp.where(valid, m, 0.0)


def _rope_tables(T):
    half = RET_QK_DIM // 2
    inv = ROPE_BASE ** (-jnp.arange(half, dtype=F32) / half)
    pos = (jnp.arange(T) - PAD_ROWS).astype(F32)
    ang = pos[:, None] * inv[None, :]
    cos, sin = jnp.cos(ang), jnp.sin(ang)
    return jnp.concatenate([cos, cos], axis=1), jnp.concatenate([-sin, sin], axis=1)


def _per_group(v):
    c = v.reshape(SSD_GROUPS, 1, HEADS_PER_GROUP)
    return c, c.reshape(SSD_GROUPS, HEADS_PER_GROUP, 1)


def _local_step(x, target, w, tick, late_weights, early_grads, in_grads):
    S = x.shape[0]
    T = S + CHUNK
    tm = _tile_rows(T)
    c0 = _const(0)

    h0 = jnp.concatenate([jnp.zeros((PAD_ROWS, D_MODEL), F32), w["meta_tokens"], x], axis=0)
    tgt = jnp.concatenate([jnp.zeros((CHUNK, D_MODEL), F32), target], axis=0)
    w_in = {name: w["w_in_t"][a:b] for name, a, b in SEGMENTS}
    w_in["dt"] = jnp.pad(w_in["dt"], ((0, CHUNK - 2 * SSD_HEADS), (0, 0)))

    def norm_cast(name, h, nw):
        return _rows(name, lambda i, hv, wv: (_rms(hv, wv),), T, 1, [(h, D_MODEL, c0)], [(nw, D_MODEL, c0)],
                     [(D_MODEL, D_MODEL, c0, BF16)])[0]

    u = norm_cast("norm_mix", h0, w["norm_mix_w"] + tick)
    proj = {name: _mm("proj_" + name, u, w_in[name], "nt", out_dtype=F32 if name == "dt" else BF16)
            for name, _, _ in SEGMENTS}

    cs, sn = _rope_tables(T)
    scale = RET_QK_DIM ** -0.5

    def rot_fn(i, qk, csv, snv):
        q = [_rot(qk[:, h * 128:(h + 1) * 128], csv, snv) for h in range(RET_HEADS)]
        k = [_rot(qk[:, (RET_HEADS + h) * 128:(RET_HEADS + h + 1) * 128], csv, snv) * scale for h in range(RET_HEADS)]
        return jnp.concatenate(q, axis=1), jnp.concatenate(k, axis=1)

    qr, kr = _rows("rotary", rot_fn, T, 1, [(proj["qk"], 1024, c0), (cs, 128, c0), (sn, 128, c0)], [],
                   [(512, 512, c0, F32), (512, 512, c0, F32)])
    y_ret = _retention("retention", qr, kr, proj["v"], T)
    a_ret = _rows("ret_post", lambda i, y, g, gw: (_ret_post(y, g, gw),), T, 1,
                  [(y_ret, 1024, c0), (proj["g"], 1024, c0)], [(w["ret_gn_w"], 1024, c0)],
                  [(1024, 1024, c0, BF16)])[0]

    conv_w = {"xs": w["w_ssd_conv"][:, :2048], "B": w["w_ssd_conv"][:, 2048:2560], "C": w["w_ssd_conv"][:, 2560:]}
    conv_b = {"xs": w["b_ssd_conv"][:, :2048], "B": w["b_ssd_conv"][:, 2048:2560], "C": w["b_ssd_conv"][:, 2560:]}

    def ssd_conv_fn(i, xe, cw, cb):
        r = _row_ids(i, T, True)
        xe = jnp.where((r >= 0) & (r < T), xe, 0.0)
        return (_center(jnp.where(r >= PAD_ROWS, _silu(_conv3(xe, cw) + cb), 0.0)),)

    act = {}
    for name in ("xs", "B", "C"):
        wd = proj[name].shape[1]
        cw = 512
        act[name] = _rows("ssd_conv_" + name, ssd_conv_fn, T, wd // cw, [(proj[name], cw, lambda j: j)],
                          [(conv_w[name], cw, lambda j: j), (conv_b[name], cw, lambda j: j)],
                          [(wd, cw, lambda j: j, BF16)], halo=True)[0]

    raw = proj["dt"][:, :2 * SSD_HEADS].reshape(T, 2, SSD_GROUPS, HEADS_PER_GROUP)
    rawc = raw.transpose(1, 2, 0, 3)
    rawr = raw.transpose(1, 2, 3, 0)
    bias = [_per_group(w["dt_bias_f"]), _per_group(w["dt_bias_b"])]
    alog = [_per_group(w["a_log_f"]), _per_group(w["a_log_b"])]
    small = (rawc, rawr, jnp.stack([bias[0][0], bias[1][0]]), jnp.stack([bias[0][1], bias[1][1]]),
             jnp.stack([alog[0][0], alog[1][0]]), jnp.stack([alog[0][1], alog[1][1]]))
    y_dir, states = _ssd_fwd(act["xs"], act["B"], act["C"], small, T)

    dskip_e = jnp.repeat(w["d_skip"], SSD_HEAD_DIM, axis=1)
    gcol = lambda j: j
    gw_ = 512
    a_ssd = _rows("ssd_post", lambda i, yf, yb, xv, zv, dk, nw: (_ssd_post(yf, yb, xv, zv, dk, nw),), T, SSD_GROUPS,
                  [(y_dir[0], gw_, gcol), (y_dir[1], gw_, gcol), (act["xs"], gw_, gcol), (proj["z"], gw_, gcol)],
                  [(dskip_e, gw_, gcol), (w["ssd_norm_w"], gw_, gcol)], [(2048, gw_, gcol, BF16)])[0]

    w = dict(w, **late_weights(a_ssd))
    w_up_g, w_up_u = w["w_ffn_up_t"][:D_FF], w["w_ffn_up_t"][D_FF:]
    y_ret_o = _mm("ret_out", a_ret, w["w_ret_out"], "nn", out_dtype=BF16)
    y_ssd_o = _mm("ssd_out", a_ssd, w["w_ssd_out"], "nn", out_dtype=BF16)

    def merge_fn(i, gates, yr, ys):
        return (_merge(gates, yr, ys, _row_ids(i, T) >= PAD_ROWS),)

    merged = _rows("merge", merge_fn, T, 1, [(proj["gates"], 2048, c0), (y_ret_o, 1024, c0), (y_ssd_o, 1024, c0)], [],
                   [(1024, 1024, c0, BF16)])[0]
    h1 = _mm("mix_out", merged, w["w_out"], "nn", add=h0)

    n2 = norm_cast("norm_ffn", h1, w["norm_ffn_w"])
    fg_pre = _mm("ffn_up_g", n2, w_up_g, "nt", out_dtype=BF16)
    fu_pre = _mm("ffn_up_u", n2, w_up_u, "nt", out_dtype=BF16)
    cwg, cwu = w["w_ffn_conv"][:, :D_FF], w["w_ffn_conv"][:, D_FF:]
    cbg, cbu = w["b_ffn_conv"][:, :D_FF], w["b_ffn_conv"][:, D_FF:]
    fcol = lambda j: j
    fw = 1408

    def ffn_act_fn(i, ge, ue, wg, wu, bg, bu):
        return (_center(_silu(_conv3(ge, wg) + bg) * (_conv3(ue, wu) + bu)),)

    def ext_valid(i):
        r = _row_ids(i, T, True)
        return (r >= 0) & (r < T)

    def ffn_act_masked(i, ge, ue, wg, wu, bg, bu):
        v = ext_valid(i)
        return ffn_act_fn(i, jnp.where(v, ge, 0.0), jnp.where(v, ue, 0.0), wg, wu, bg, bu)

    a2 = _rows("ffn_act", ffn_act_masked, T, D_FF // fw, [(fg_pre, fw, fcol), (fu_pre, fw, fcol)],
               [(cwg, fw, fcol), (cwu, fw, fcol), (cbg, fw, fcol), (cbu, fw, fcol)], [(D_FF, fw, fcol, BF16)],
               halo=True)[0]
    h2 = _mm("ffn_down", a2, w["w_ffn_down"], "nn", add=h1)

    fnw = w["final_norm_w"].reshape(1, D_MODEL)

    def loss_fn(i, hv, tv, nw):
        valid = _row_ids(i, T) >= CHUNK
        y, vjp = jax.vjp(_rms, hv, nw)
        diff = jnp.where(valid, y - tv, 0.0)
        dh, dw = vjp(diff * (1.0 / D_MODEL))
        part = 0.5 / D_MODEL * jnp.sum(jnp.sum(diff * diff, axis=1, keepdims=True), axis=0, keepdims=True)
        return dh, jnp.broadcast_to(part, (1, 128)), dw

    dh2, loss_acc, d_fnw = _rows("loss", loss_fn, T, 1, [(h2, D_MODEL, c0), (tgt, D_MODEL, c0)], [(fnw, D_MODEL, c0)],
                                 [(D_MODEL, D_MODEL, c0, F32)], [(1, 128, 128, c0), (1, D_MODEL, D_MODEL, c0)])
    loss = loss_acc[0, 0]
    grads = {"final_norm_w": d_fnw.reshape(D_MODEL)}

    da2 = _mm("d_ffn_act", dh2, w["w_ffn_down"], "nt", out_dtype=BF16)
    grads["w_ffn_down"] = _mm("g_ffn_down", a2, dh2, "tn", out_dtype=BF16)

    def ffn_bwd_fn(i, ge, ue, de, wg, wu, bg, bu):
        v = ext_valid(i)
        ge, ue, de = jnp.where(v, ge, 0.0), jnp.where(v, ue, 0.0), jnp.where(v, de, 0.0)
        fg = _conv3(ge, wg) + bg
        fu = _conv3(ue, wu) + bu
        sg = jax.nn.sigmoid(fg)
        dfg = de * fu * (sg * (1.0 + fg * (1.0 - sg)))
        dfu = de * (fg * sg)
        n = ge.shape[0]

        def wgrad(df, xe):
            df_c = _center(df)
            return jnp.concatenate([jnp.sum(df_c * _center(pltpu.roll(xe, 1, 0)), axis=0, keepdims=True),
                                    jnp.sum(df_c * _center(xe), axis=0, keepdims=True),
                                    jnp.sum(df_c * _center(pltpu.roll(xe, n - 1, 0)), axis=0, keepdims=True)], axis=0)

        return (_center(_conv3_t(dfg, wg)), _center(_conv3_t(dfu, wu)), wgrad(dfg, ge), wgrad(dfu, ue),
                jnp.sum(_center(dfg), axis=0, keepdims=True), jnp.sum(_center(dfu), axis=0, keepdims=True))

    dfg_pre, dfu_pre, g_cwg, g_cwu, g_cbg, g_cbu = _rows(
        "ffn_act_bwd", ffn_bwd_fn, T, D_FF // fw, [(fg_pre, fw, fcol), (fu_pre, fw, fcol), (da2, fw, fcol)],
        [(cwg, fw, fcol), (cwu, fw, fcol), (cbg, fw, fcol), (cbu, fw, fcol)],
        [(D_FF, fw, fcol, BF16), (D_FF, fw, fcol, BF16)],
        [(3, D_FF, fw, fcol), (3, D_FF, fw, fcol), (1, D_FF, fw, fcol), (1, D_FF, fw, fcol)], halo=True)
    grads["w_ffn_conv"] = jnp.concatenate([g_cwg, g_cwu], axis=1)
    grads["b_ffn_conv"] = jnp.concatenate([g_cbg, g_cbu], axis=1)
    dn2 = _mm("d_norm_ffn_g", dfg_pre, w_up_g, "nn")
    dn2 = _mm("d_norm_ffn_u", dfu_pre, w_up_u, "nn", add=dn2)
    grads["w_ffn_up_t"] = jnp.concatenate([_mm("g_ffn_up_g", dfg_pre, n2, "tn", out_dtype=BF16), _mm("g_ffn_up_u", dfu_pre, n2, "tn", out_dtype=BF16)],
                                          axis=0)

    def norm_bwd(name, h, nw, dn, dres):
        def fn(i, hv, dnv, drv, wv):
            _, vjp = jax.vjp(_rms, hv, wv)
            dh, dw = vjp(dnv)
            return dh + drv, dw
        return _rows(name, fn, T, 1, [(h, D_MODEL, c0), (dn, D_MODEL, c0), (dres, D_MODEL, c0)], [(nw, D_MODEL, c0)],
                     [(D_MODEL, D_MODEL, c0, F32)], [(1, D_MODEL, D_MODEL, c0)])

    dh1, grads["norm_ffn_w"] = norm_bwd("norm_ffn_bwd", h1, w["norm_ffn_w"], dn2, dh2)

    dmerged = _mm("d_merged", dh1, w["w_out"], "nt", out_dtype=BF16)
    grads["w_out"] = _mm("g_out", merged, dh1, "tn", out_dtype=BF16)

    def merge_bwd_fn(i, gates, yr, ys, dm):
        valid = _row_ids(i, T) >= PAD_ROWS
        _, vjp = jax.vjp(lambda a, b, c: _merge(a, b, c, valid), gates, yr, ys)
        return vjp(dm)

    dgates, dyr, dys = _rows("merge_bwd", merge_bwd_fn, T, 1,
                             [(proj["gates"], 2048, c0), (y_ret_o, 1024, c0), (y_ssd_o, 1024, c0), (dmerged, 1024, c0)],
                             [], [(2048, 2048, c0, BF16), (1024, 1024, c0, BF16), (1024, 1024, c0, BF16)])
    dproj = {"gates": dgates}

    da_ssd = _mm("d_ssd_act", dys, w["w_ssd_out"], "nt", out_dtype=BF16)
    grads["w_ssd_out"] = _mm("g_ssd_out", a_ssd, dys, "tn", out_dtype=BF16)

    def ssd_post_bwd_fn(i, yf, yb, xv, zv, da, dk, nw):
        _, vjp = jax.vjp(_ssd_post, yf, yb, xv, zv, dk, nw)
        dyf, _, dxv, dzv, ddk, dnw = vjp(da)
        return dyf, dxv, dzv, ddk, dnw

    dy_ssd, dxs_skip, dproj["z"], g_dskip_e, grads["ssd_norm_w"] = _rows(
        "ssd_post_bwd", ssd_post_bwd_fn, T, SSD_GROUPS,
        [(y_dir[0], gw_, gcol), (y_dir[1], gw_, gcol), (act["xs"], gw_, gcol), (proj["z"], gw_, gcol),
         (da_ssd, gw_, gcol)],
        [(dskip_e, gw_, gcol), (w["ssd_norm_w"], gw_, gcol)],
        [(2048, gw_, gcol, BF16), (2048, gw_, gcol, BF16), (2048, gw_, gcol, BF16)],
        [(1, 2048, gw_, gcol), (1, 2048, gw_, gcol)])
    grads["d_skip"] = g_dskip_e.reshape(SSD_HEADS, SSD_HEAD_DIM).sum(axis=1).reshape(1, SSD_HEADS)

    dxs_dir, db_dir, dc_dir, draw, g_bias, g_alog = _ssd_bwd(act["xs"], act["B"], act["C"], small, states, dy_ssd, T)
    grads["dt_bias_f"], grads["dt_bias_b"] = g_bias[0].reshape(1, SSD_HEADS), g_bias[1].reshape(1, SSD_HEADS)
    grads["a_log_f"], grads["a_log_b"] = g_alog[0].reshape(1, SSD_HEADS), g_alog[1].reshape(1, SSD_HEADS)
    d_dt = jnp.stack(draw).transpose(2, 0, 1, 3).reshape(T, 2 * SSD_HEADS)
    dproj["dt"] = jnp.pad(d_dt, ((0, 0), (0, CHUNK - 2 * SSD_HEADS))).astype(BF16)

    def make_conv_bwd(nsum):
        def fn(i, xe, *rest):
            ds, (cw, cb) = rest[:nsum], rest[nsum:]
            r = _row_ids(i, T, True)
            dact = ds[0]
            for t in ds[1:]:
                dact = dact + t
            dact = jnp.where((r >= PAD_ROWS) & (r < T), dact, 0.0)
            xe = jnp.where((r >= 0) & (r < T), xe, 0.0)
            pre = _conv3(xe, cw) + cb
            sg = jax.nn.sigmoid(pre)
            dpre = dact * (sg * (1.0 + pre * (1.0 - sg)))
            n = xe.shape[0]
            dpc = _center(dpre)
            dw = jnp.concatenate([jnp.sum(dpc * _center(pltpu.roll(xe, 1, 0)), axis=0, keepdims=True),
                                  jnp.sum(dpc * _center(xe), axis=0, keepdims=True),
                                  jnp.sum(dpc * _center(pltpu.roll(xe, n - 1, 0)), axis=0, keepdims=True)], axis=0)
            return _center(_conv3_t(dpre, cw)), dw, jnp.sum(dpc, axis=0, keepdims=True)
        return fn

    g_cw, g_cb = {}, {}
    cots = {"xs": [(dxs_dir[0], 512, gcol), (dxs_dir[1], 512, gcol), (dxs_skip, 512, gcol)],
            "B": [(db_dir[0], 512, gcol), (db_dir[1], 512, gcol)],
            "C": [(dc_dir[0], 512, gcol), (dc_dir[1], 512, gcol)]}
    for name in ("xs", "B", "C"):
        wd = proj[name].shape[1]
        dproj[name], g_cw[name], g_cb[name] = _rows(
            "ssd_conv_bwd_" + name, make_conv_bwd(len(cots[name])), T, wd // 512,
            [(proj[name], 512, gcol)] + cots[name], [(conv_w[name], 512, gcol), (conv_b[name], 512, gcol)],
            [(wd, 512, gcol, BF16)], [(3, wd, 512, gcol), (1, wd, 512, gcol)], halo=True)
    grads["w_ssd_conv"] = jnp.concatenate([g_cw["xs"], g_cw["B"], g_cw["C"]], axis=1)
    grads["b_ssd_conv"] = jnp.concatenate([g_cb["xs"], g_cb["B"], g_cb["C"]], axis=1)

    da_ret = _mm("d_ret_act", dyr, w["w_ret_out"], "nt", out_dtype=BF16)
    grads["w_ret_out"] = _mm("g_ret_out", a_ret, dyr, "tn", out_dtype=BF16)
    tick = early_grads({n: grads.pop(n) for n in ("w_ffn_up_t", "w_ret_out", "w_ssd_out", "w_out", "w_ffn_down")})

    def ret_post_bwd_fn(i, y, g, da, gw):
        _, vjp = jax.vjp(_ret_post, y, g, gw)
        return vjp(da)

    dy_ret, dproj["g"], grads["ret_gn_w"] = _rows(
        "ret_post_bwd", ret_post_bwd_fn, T, 1, [(y_ret, 1024, c0), (proj["g"], 1024, c0), (da_ret, 1024, c0)],
        [(w["ret_gn_w"] + tick, 1024, c0)], [(1024, 1024, c0, BF16), (1024, 1024, c0, BF16)], [(1, 1024, 1024, c0)])
    dproj["v"] = _retention("retention_dv", kr, qr, dy_ret, T)
    dqr = _retention("retention_dq", dy_ret, proj["v"], kr, T)
    dkr = _retention("retention_dk", proj["v"], dy_ret, qr, T)

    def rot_bwd_fn(i, dq, dk, csv, snv):
        parts = [_rot_t(dq[:, h * 128:(h + 1) * 128], csv, snv) for h in range(RET_HEADS)]
        parts += [_rot_t(dk[:, h * 128:(h + 1) * 128] * scale, csv, snv) for h in range(RET_HEADS)]
        return (jnp.concatenate(parts, axis=1),)

    dproj["qk"] = _rows("rotary_bwd", rot_bwd_fn, T, 1, [(dqr, 512, c0), (dkr, 512, c0), (cs, 128, c0), (sn, 128, c0)],
                        [], [(1024, 1024, c0, BF16)])[0]

    g_in = [_mm("g_in_" + name, dproj[name], u, "tn", out_dtype=BF16) for name, _, _ in SEGMENTS]
    g_in[7] = g_in[7][:2 * SSD_HEADS]
    tick = in_grads(jnp.concatenate(g_in, axis=0))
    du = _mm("d_u_dt", dproj["dt"] + tick.astype(BF16), w_in["dt"], "nn")
    for name, _, _ in SEGMENTS:
        if name != "dt":
            du = _mm("d_u_" + name, dproj[name], w_in[name], "nn", add=du)
    dh0, grads["norm_mix_w"] = norm_bwd("norm_mix_bwd", h0, w["norm_mix_w"], du, dh1)
    grads["meta_tokens"] = dh0[PAD_ROWS:CHUNK]
    return loss, dh0[CHUNK:], grads


MESH_ID = pl.DeviceIdType.MESH
ANY = pl.BlockSpec(memory_space=pl.ANY)


def _me_and_peers():
    x, y, c = lax.axis_index("x"), lax.axis_index("y"), lax.axis_index("c")
    peers = []
    for k in range(1, N_DEV):
        px = 1 - x if k & 4 else x
        py = 1 - y if k & 2 else y
        pc = 1 - c if k & 1 else c
        peers.append(((px, py, pc), 4 * px + 2 * py + pc))
    return 4 * x + 2 * y + c, peers


def _push_blocks(name, src, per_peer):
    blk = src.shape[1:] if per_peer else src.shape

    def body(src_ref, out_ref, send_sems, recv_sems, local_sem):
        me, peers = _me_and_peers()
        mine = src_ref.at[me] if per_peer else src_ref
        local = pltpu.make_async_copy(mine, out_ref.at[me], local_sem)
        local.start()
        sends = []
        for k, (dev, idx) in enumerate(peers):
            cp = pltpu.make_async_remote_copy(
                src_ref=src_ref.at[idx] if per_peer else src_ref, dst_ref=out_ref.at[me],
                send_sem=send_sems.at[k], recv_sem=recv_sems.at[k], device_id=dev, device_id_type=MESH_ID)
            cp.start()
            sends.append(cp)
        for k, (dev, idx) in enumerate(peers):
            pltpu.make_async_remote_copy(
                src_ref=mine, dst_ref=out_ref.at[idx], send_sem=send_sems.at[k], recv_sem=recv_sems.at[k],
                device_id=dev, device_id_type=MESH_ID).wait_recv()
        for cp in sends:
            cp.wait_send()
        local.wait()

    return pl.pallas_call(
        body, name=name, in_specs=[ANY], out_specs=ANY,
        out_shape=jax.ShapeDtypeStruct((N_DEV,) + tuple(blk), src.dtype),
        scratch_shapes=[pltpu.SemaphoreType.DMA((N_DEV - 1,)), pltpu.SemaphoreType.DMA((N_DEV - 1,)),
                        pltpu.SemaphoreType.DMA],
    )(src)


def _gather_two_level(name, src):
    def body(x_ref, out_ref, send_sems, recv_sems, local_sem):
        x, y, c = lax.axis_index("x"), lax.axis_index("y"), lax.axis_index("c")
        me, sibling = (x, y, c), (x, y, 1 - c)
        chips = [(1 - x, y), (x, 1 - y), (1 - x, 1 - y)]

        def rows(px, py, pc):
            return out_ref.at[4 * px + 2 * py + pc]

        def copy(k, block, to, src_ref=None):
            return pltpu.make_async_remote_copy(
                src_ref=rows(*block) if src_ref is None else src_ref, dst_ref=rows(*block),
                send_sem=send_sems.at[k], recv_sem=recv_sems.at[k], device_id=to, device_id_type=MESH_ID)

        mine = pltpu.make_async_copy(x_ref, rows(*me), local_sem)
        mine.start()
        first = [copy(0, me, sibling, x_ref)] + [copy(1 + j, me, (*chip, c), x_ref) for j, chip in enumerate(chips)]
        for cp in first:
            cp.start()
        passed = [copy(4 + j, (*chip, c), sibling) for j, chip in enumerate(chips)]
        for j, chip in enumerate(chips):
            copy(1 + j, (*chip, c), me).wait_recv()
            passed[j].start()
        copy(0, sibling, me).wait_recv()
        for j, chip in enumerate(chips):
            copy(4 + j, (*chip, 1 - c), me).wait_recv()
        for cp in first + passed:
            cp.wait_send()
        mine.wait()

    return pl.pallas_call(
        body, name=name, in_specs=[ANY], out_specs=ANY,
        out_shape=jax.ShapeDtypeStruct((N_DEV,) + tuple(src.shape), src.dtype),
        scratch_shapes=[pltpu.SemaphoreType.DMA((N_DEV - 1,)), pltpu.SemaphoreType.DMA((N_DEV - 1,)),
                        pltpu.SemaphoreType.DMA],
    )(src)


HBM = pl.BlockSpec(memory_space=pltpu.HBM)
SEM = pl.BlockSpec(memory_space=pltpu.SEMAPHORE)
EFFECT = pltpu.SideEffectType.DATAFLOW_SIDE_EFFECTING


def _peer_copy(src_ref, land_ref, send_sems, recv_sems, per_peer, me, k, dev, idx, receiving):
    return pltpu.make_async_remote_copy(
        src_ref=src_ref.at[idx] if per_peer else src_ref, dst_ref=land_ref.at[idx if receiving else me],
        send_sem=send_sems.at[k], recv_sem=recv_sems.at[k], device_id=dev, device_id_type=MESH_ID)


def _push_start(name, src, per_peer):
    blk = src.shape[1:] if per_peer else src.shape
    land_shape = (N_DEV,) + tuple(blk)

    def body(src_ref, land_ref, send_sems, recv_sems, src_thru, land_thru, token):
        me, peers = _me_and_peers()
        for k, (dev, idx) in enumerate(peers):
            _peer_copy(src_ref, land_ref, send_sems, recv_sems, per_peer, me, k, dev, idx, False).start()
        token[...] = jnp.zeros_like(token)

    return pl.pallas_call(
        body, name=name,
        out_shape=(pltpu.SemaphoreType.DMA((N_DEV - 1,)), pltpu.SemaphoreType.DMA((N_DEV - 1,)),
                   pltpu.HBM(src.shape, src.dtype), pltpu.HBM(land_shape, src.dtype),
                   jax.ShapeDtypeStruct((8, 128), F32)),
        in_specs=(HBM, HBM), out_specs=(SEM, SEM, HBM, HBM, pl.BlockSpec(memory_space=pltpu.VMEM)),
        input_output_aliases={0: 2, 1: 3}, compiler_params=pltpu.CompilerParams(has_side_effects=EFFECT),
    )(pltpu.with_memory_space_constraint(src, pltpu.HBM),
      pltpu.with_memory_space_constraint(lax.empty(land_shape, src.dtype), pltpu.HBM))


def _push_wait(name, send_sems, recv_sems, src_thru, land_thru, after, per_peer):
    def body(src_ref, land_ref, send_sems, recv_sems, after_ref, src_out, land_out):
        me, peers = _me_and_peers()
        for k, (dev, idx) in enumerate(peers):
            cp = _peer_copy(src_ref, land_ref, send_sems, recv_sems, per_peer, me, k, dev, idx, True)
            cp.wait_send()
            cp.wait_recv()

    return pl.pallas_call(
        body, name=name,
        out_shape=(pltpu.HBM(src_thru.shape, src_thru.dtype), pltpu.HBM(land_thru.shape, land_thru.dtype)),
        in_specs=(HBM, HBM, SEM, SEM, ANY), out_specs=(HBM, HBM), input_output_aliases={0: 0, 1: 1},
        compiler_params=pltpu.CompilerParams(has_side_effects=EFFECT),
    )(src_thru, land_thru, send_sems, recv_sems, after)


def _sum_blocks(name, blocks):
    _, R, C = blocks.shape
    tc = _pick(C, (128,))

    def body(b_ref, o_ref):
        acc = b_ref[0].astype(F32)
        for k in range(1, N_DEV):
            acc = acc + b_ref[k].astype(F32)
        o_ref[...] = acc

    return pl.pallas_call(
        body, name=name, grid=(C // tc,), in_specs=[pl.BlockSpec((N_DEV, R, tc), lambda j: (0, 0, j))],
        out_specs=pl.BlockSpec((R, tc), lambda j: (0, j)), out_shape=jax.ShapeDtypeStruct((R, C), F32),
        compiler_params=_params(("arbitrary",)),
    )(blocks)


def _adamw(name, w, g, m, v):
    R, C = w.shape
    tr = R if R <= 512 else _pick(R, (256, 128, 8))
    spec = pl.BlockSpec((tr, C), lambda i: (i, 0))

    def body(w_ref, g_ref, m_ref, v_ref, d_ref, mo_ref, vo_ref):
        gv = g_ref[...]
        mn = ADAM_B1 * m_ref[...] + (1.0 - ADAM_B1) * gv
        vn = ADAM_B2 * v_ref[...] + (1.0 - ADAM_B2) * jnp.square(gv)
        m_hat = mn / (1.0 - ADAM_B1 ** ADAM_STEP)
        v_hat = vn / (1.0 - ADAM_B2 ** ADAM_STEP)
        d_ref[...] = -ADAM_LR * (m_hat / (jnp.sqrt(v_hat) + ADAM_EPS) + ADAM_WD * w_ref[...])
        mo_ref[...] = mn
        vo_ref[...] = vn

    return pl.pallas_call(
        body, name=name, grid=(R // tr,), in_specs=[spec] * 4, out_specs=[spec] * 3,
        out_shape=[jax.ShapeDtypeStruct((R, C), F32)] * 3, compiler_params=_params(("arbitrary",)),
    )(w, g, m, v)


WEIGHTS = ("meta_tokens", "norm_mix_w", "w_in", "ret_gn_w", "w_ret_out", "w_ssd_conv", "b_ssd_conv", "dt_bias_f",
           "dt_bias_b", "a_log_f", "a_log_b", "d_skip", "ssd_norm_w", "w_ssd_out", "w_out", "norm_ffn_w", "w_ffn_up",
           "w_ffn_conv", "b_ffn_conv", "w_ffn_down", "final_norm_w")
BIG = (("w_in", 1288, True), ("w_ffn_up", 704, True), ("w_ret_out", 128, False), ("w_ssd_out", 256, False),
       ("w_out", 128, False), ("w_ffn_down", 352, False))
REPLICATED = ("norm_mix_w", "ret_gn_w", "b_ssd_conv", "dt_bias_f", "dt_bias_b", "a_log_f", "a_log_b", "d_skip",
              "ssd_norm_w", "norm_ffn_w", "b_ffn_conv", "final_norm_w")
SMALL_SHARDED = (("meta_tokens", 16, 1024), ("w_ssd_conv", 3, 3072), ("w_ffn_conv", 3, 5632))


BIG_IN, BIG_REST = BIG[:1], BIG[1:]


def _pack_big(tree, group):
    parts = []
    for name, _, transposed in group:
        a = tree[name][0]
        parts.append(a.T if transposed else a)
    return jnp.concatenate(parts, axis=0)


def _unpack_big(slab, group):
    out, r0 = {}, 0
    for name, r, transposed in group:
        a = slab[r0:r0 + r]
        out[name] = (a.T if transposed else a)[None]
        r0 += r
    return out


def _pack_flat(arrays, rows):
    flat = jnp.concatenate([a.reshape(-1) for a in arrays])
    return jnp.pad(flat, (0, rows * D_MODEL - flat.shape[0])).reshape(rows, D_MODEL)


def _unpack_flat(slab, shapes):
    flat, out, o = slab.reshape(-1), [], 0
    for s in shapes:
        n = math.prod(s)
        out.append(flat[o:o + n].reshape(s))
        o += n
    return out


def kernel(x, meta_tokens, norm_mix_w, w_in, ret_gn_w, w_ret_out, w_ssd_conv, b_ssd_conv, dt_bias_f, dt_bias_b, a_log_f, a_log_b, d_skip, ssd_norm_w, w_ssd_out, w_out, norm_ffn_w, w_ffn_up, w_ffn_conv, b_ffn_conv, w_ffn_down, final_norm_w, loss_target, m_meta_tokens, m_norm_mix_w, m_w_in, m_ret_gn_w, m_w_ret_out, m_w_ssd_conv, m_b_ssd_conv, m_dt_bias_f, m_dt_bias_b, m_a_log_f, m_a_log_b, m_d_skip, m_ssd_norm_w, m_w_ssd_out, m_w_out, m_norm_ffn_w, m_w_ffn_up, m_w_ffn_conv, m_b_ffn_conv, m_w_ffn_down, m_final_norm_w, v_meta_tokens, v_norm_mix_w, v_w_in, v_ret_gn_w, v_w_ret_out, v_w_ssd_conv, v_b_ssd_conv, v_dt_bias_f, v_dt_bias_b, v_a_log_f, v_a_log_b, v_d_skip, v_ssd_norm_w, v_w_ssd_out, v_w_out, v_norm_ffn_w, v_w_ffn_up, v_w_ffn_conv, v_b_ffn_conv, v_w_ffn_down, v_final_norm_w):
    given = dict(locals())
    wt = {n: given[n] for n in WEIGHTS}
    mt = {n: given["m_" + n] for n in WEIGHTS}
    vt = {n: given["v_" + n] for n in WEIGHTS}
    me = 4 * lax.axis_index("x") + 2 * lax.axis_index("y") + lax.axis_index("c")

    small_names = [n for n, _, _ in SMALL_SHARDED]
    small_local = lambda tree: [tree[n].reshape(r, c // N_DEV) for n, r, c in SMALL_SHARDED]
    all_in = _gather_two_level("gather_w_in", _pack_big(wt, BIG_IN).astype(BF16))
    all_s = _push_blocks("gather_small", _pack_flat(small_local(wt), 8), False)
    rest_src, all_in, all_s = lax.optimization_barrier((_pack_big(wt, BIG_REST).astype(BF16), all_in, all_s))
    rest_flight = _push_start("gather_rest_start", rest_src, False)
    all_s = all_s.reshape(N_DEV, -1)
    full = {"w_in_t": all_in.reshape(-1, D_MODEL)}

    def land_with_own(flight, after, per_peer, name):
        src, land = _push_wait(name, *flight[:4], after, per_peer)
        own = lax.dynamic_slice_in_dim(src, me, 1, axis=0) if per_peer else src[None]
        return lax.dynamic_update_slice_in_dim(land, own, me, axis=0)

    def late_weights(after):
        all_rest = land_with_own(rest_flight, after, False, "gather_rest_wait")
        out, r0 = {}, 0
        for name, r, transposed in BIG_REST:
            out[name + ("_t" if transposed else "")] = all_rest[:, r0:r0 + r].reshape(N_DEV * r, D_MODEL)
            r0 += r
        return out

    flights = {}

    def start_exchange(key, group, gd):
        g_blocks = jnp.concatenate(
            [gd[name + ("_t" if t else "")].reshape(N_DEV, r, D_MODEL) for name, r, t in group], axis=1)
        flights[key] = _push_start("exchange_" + key + "_start", g_blocks.astype(BF16), True)
        return flights[key][4][0, 0]

    o = 0
    for name, r, c in SMALL_SHARDED:
        n = r * c // N_DEV
        full[name] = all_s[:, o:o + n].reshape(N_DEV, r, c // N_DEV).transpose(1, 0, 2).reshape(r, c)
        o += n
    for name in REPLICATED:
        full[name] = wt[name]

    loss, grad_x, g = _local_step(
        x[0], loss_target[0], full, rest_flight[4][0, 0], late_weights,
        lambda gd: start_exchange("rest", BIG_REST, gd), lambda gi: start_exchange("in", BIG_IN, {"w_in_t": gi}))

    last = g["norm_mix_w"]
    g_slabs = {key: _sum_blocks("sum_" + key, land_with_own(flights[key], last, True, "exchange_" + key + "_wait"))
               for key in ("rest", "in")}
    small_parts = [g[n] for n in REPLICATED] + [g[n] for n in small_names] + [loss.reshape(1)]
    g_small = _sum_blocks("sum_small", _push_blocks("gather_small_grads", _pack_flat(small_parts, 64), False))
    small_red = _unpack_flat(g_small, [wt[n].shape for n in REPLICATED] + [(r, c) for _, r, c in SMALL_SHARDED] + [(1,)])
    grads = dict(zip(REPLICATED, small_red[:len(REPLICATED)]))
    for (name, r, c), red in zip(SMALL_SHARDED, small_red[len(REPLICATED):-1]):
        grads[name] = lax.dynamic_slice(red, (0, me * (c // N_DEV)), (r, c // N_DEV)).reshape(wt[name].shape)
    loss_all = small_red[-1][0]
    delta, new_m, new_v = {}, {}, {}
    for key, group in (("in", BIG_IN), ("rest", BIG_REST)):
        grads.update(_unpack_big(g_slabs[key], group))
        for name, _, transposed in group:
            view = (lambda a: a[0].T) if transposed else (lambda a: a[0])
            back = (lambda a: a.T[None]) if transposed else (lambda a: a[None])
            d, mn, vn = _adamw("adamw_" + name, view(wt[name]), view(grads[name]), view(mt[name]), view(vt[name]))
            delta[name], new_m[name], new_v[name] = back(d), back(mn), back(vn)

    rest = list(REPLICATED) + small_names
    shapes = [wt[n].shape for n in rest]
    pack_rest = lambda tree: _pack_flat([tree[n] for n in rest], 24)
    d_rest, m_rest, v_rest = _adamw("adamw_small", pack_rest(wt), pack_rest(grads), pack_rest(mt), pack_rest(vt))
    delta.update(zip(rest, _unpack_flat(d_rest, shapes)))
    new_m.update(zip(rest, _unpack_flat(m_rest, shapes)))
    new_v.update(zip(rest, _unpack_flat(v_rest, shapes)))

    return (loss_all, grad_x[None], *[grads[n] for n in WEIGHTS], *[delta[n] for n in WEIGHTS],
            *[new_m[n] for n in WEIGHTS], *[new_v[n] for n in WEIGHTS])
```

```python
import functools
import math

import jax
import jax.numpy as jnp
from jax import lax
from jax.experimental import pallas as pl
from jax.experimental.pallas import tpu as pltpu

F32 = jnp.float32
BF16 = jnp.bfloat16

D_MODEL = 1024
CHUNK = 128
N_META = 16
PAD_ROWS = CHUNK - N_META
RET_HEADS = 4
RET_QK_DIM = 128
RET_V_DIM = 256
SSD_HEADS = 32
SSD_HEAD_DIM = 64
SSD_GROUPS = 4
SSD_STATE = 128
HEADS_PER_GROUP = SSD_HEADS // SSD_GROUPS
PAIRS_PER_GROUP = HEADS_PER_GROUP // 2
D_FF = 2816
EPS = 1e-6
ROPE_BASE = 10000.0
N_DEV = 8

ADAM_LR = 0.001
ADAM_B1 = 0.9
ADAM_B2 = 0.999
ADAM_EPS = 1e-08
ADAM_WD = 0.01
ADAM_STEP = 10

VMEM_LIMIT = 56 * 1024 * 1024
HALO = 16
HIGHEST = lax.Precision.HIGHEST

SEGMENTS = (("qk", 0, 1024), ("v", 1024, 2048), ("g", 2048, 3072), ("z", 3072, 5120), ("xs", 5120, 7168),
            ("B", 7168, 7680), ("C", 7680, 8192), ("dt", 8192, 8256), ("gates", 8256, 10304))


def _pick(n, cands):
    for c in cands:
        if n % c == 0:
            return c
    raise ValueError(f"no tile for {n}")


def _params(sem):
    return pltpu.CompilerParams(dimension_semantics=sem, vmem_limit_bytes=VMEM_LIMIT)


def _dot(a, b, dims=(((1,), (0,)), ((), ())), precision=None):
    return lax.dot_general(a, b, dims, preferred_element_type=F32, precision=precision)


def _dot_nt(a, b):
    return _dot(a, b, (((1,), (1,)), ((), ())))


def _dot_tn(a, b):
    return _dot(a, b, (((0,), (0,)), ((), ())))


def _mm(name, a, b, mode, add=None, out_dtype=F32):
    if mode == "nn":
        (M, K), N = a.shape, b.shape[1]
    elif mode == "nt":
        (M, K), N = a.shape, b.shape[0]
    else:
        (K, M), N = a.shape, b.shape[1]
    tn = _pick(N, (1408, 1024, 512, 128, 64))
    if mode == "tn":
        tm = M if M <= 1024 else _pick(M, (1408, 1024))
        tk = _pick(K, (1056, 512, 256, 128))
    else:
        tm = _pick(M, (1056, 512, 256, 128))
        tk = K if K <= 2048 else _pick(K, (1408, 1024))
    nk = K // tk
    if mode == "nn":
        a_spec = pl.BlockSpec((tm, tk), lambda n, m, k: (m, k))
        b_spec = pl.BlockSpec((tk, tn), lambda n, m, k: (k, n))
        dims = (((1,), (0,)), ((), ()))
    elif mode == "nt":
        a_spec = pl.BlockSpec((tm, tk), lambda n, m, k: (m, k))
        b_spec = pl.BlockSpec((tn, tk), lambda n, m, k: (n, k))
        dims = (((1,), (1,)), ((), ()))
    else:
        a_spec = pl.BlockSpec((tk, tm), lambda n, m, k: (k, m))
        b_spec = pl.BlockSpec((tk, tn), lambda n, m, k: (k, n))
        dims = (((0,), (0,)), ((), ()))
    o_spec = pl.BlockSpec((tm, tn), lambda n, m, k: (m, n))
    in_specs = [a_spec, b_spec] + ([o_spec] if add is not None else [])
    args = [a, b] + ([add] if add is not None else [])

    def body(*refs):
        if add is not None:
            a_ref, b_ref, r_ref, o_ref, acc = refs
        else:
            a_ref, b_ref, o_ref, acc = refs
        k = pl.program_id(2)
        p = _dot(a_ref[...].astype(BF16), b_ref[...].astype(BF16), dims)

        def finish(r):
            if add is not None:
                r = r + r_ref[...]
            o_ref[...] = r.astype(out_dtype)

        if nk == 1:
            finish(p)
        else:
            @pl.when(k == 0)
            def _():
                acc[...] = p

            @pl.when(k > 0)
            def _():
                acc[...] += p

            @pl.when(k == nk - 1)
            def _():
                finish(acc[...])

    return pl.pallas_call(
        body, name=name, grid=(N // tn, M // tm, nk), in_specs=in_specs, out_specs=o_spec,
        out_shape=jax.ShapeDtypeStruct((M, N), out_dtype),
        scratch_shapes=[pltpu.VMEM((tm, tn) if nk > 1 else (8, 128), F32)],
        compiler_params=_params(("arbitrary", "arbitrary", "arbitrary")),
    )(*args)


def _const(c):
    return lambda j: c


def _rows(name, fn, T, ncol, ins, params, outs, accs=(), halo=False):
    tm = _pick(T, (384, 256, 128))
    R = T // tm
    hb = tm // HALO
    in_specs, args = [], []
    for spec in ins:
        arr, w, cf = spec[:3]
        lead = spec[3] if len(spec) > 3 else None
        if lead is None:
            mk = lambda blk, rf, cf=cf: pl.BlockSpec(blk, lambda j, i: (rf(i), cf(j)))
            shape = lambda r, w=w: (r, w)
        else:
            mk = lambda blk, rf, cf=cf, lead=lead: pl.BlockSpec(blk, lambda j, i: (lead, rf(i), cf(j)))
            shape = lambda r, w=w: (None, r, w)
        in_specs.append(mk(shape(tm), lambda i: i))
        args.append(arr)
        if halo:
            in_specs.append(mk(shape(HALO), lambda i: jnp.maximum(i * hb - 1, 0)))
            in_specs.append(mk(shape(HALO), lambda i: jnp.minimum((i + 1) * hb, T // HALO - 1)))
            args += [arr, arr]
    for arr, w, cf in params:
        in_specs.append(pl.BlockSpec((arr.shape[0], w), lambda j, i, cf=cf: (0, cf(j))))
        args.append(arr)
    out_shape, out_specs = [], []
    for tw, w, cf, dt in outs:
        out_shape.append(jax.ShapeDtypeStruct((T, tw), dt))
        out_specs.append(pl.BlockSpec((tm, w), lambda j, i, cf=cf: (i, cf(j))))
    for r, tw, w, cf in accs:
        out_shape.append(jax.ShapeDtypeStruct((r, tw), F32))
        out_specs.append(pl.BlockSpec((r, w), lambda j, i, cf=cf: (0, cf(j))))
    n_in, n_par, n_out, n_acc = len(ins), len(params), len(outs), len(accs)

    def body(*refs):
        i = pl.program_id(1)
        vals, p = [], 0
        for _ in range(n_in):
            if halo:
                vals.append(jnp.concatenate([refs[p + 1][...], refs[p][...], refs[p + 2][...]], axis=0).astype(F32))
                p += 3
            else:
                vals.append(refs[p][...].astype(F32))
                p += 1
        pvals = [refs[p + k][...] for k in range(n_par)]
        p += n_par
        res = fn(i, *vals, *pvals)
        for k in range(n_out):
            refs[p + k][...] = res[k].astype(refs[p + k].dtype)
        p += n_out
        for k in range(n_acc):
            ref, v = refs[p + k], res[n_out + k]

            @pl.when(i == 0)
            def _(ref=ref, v=v):
                ref[...] = v

            @pl.when(i > 0)
            def _(ref=ref, v=v):
                ref[...] += v

    res = pl.pallas_call(
        body, name=name, grid=(ncol, R), in_specs=in_specs, out_specs=out_specs, out_shape=out_shape,
        compiler_params=_params(("arbitrary", "arbitrary")),
    )(*args)
    return res


def _tile_rows(T):
    return _pick(T, (384, 256, 128))


def _row_ids(i, T, halo=False):
    tm = _tile_rows(T)
    if halo:
        return i * tm - HALO + lax.broadcasted_iota(jnp.int32, (tm + 2 * HALO, 1), 0)
    return i * tm + lax.broadcasted_iota(jnp.int32, (tm, 1), 0)


def _rms(x, w):
    return x * lax.rsqrt(jnp.mean(x * x, axis=-1, keepdims=True) + EPS) * w


def _silu(x):
    return x * jax.nn.sigmoid(x)


def _conv3(x, w):
    n = x.shape[0]
    return w[0:1] * pltpu.roll(x, 1, 0) + w[1:2] * x + w[2:3] * pltpu.roll(x, n - 1, 0)


def _conv3_t(d, w):
    n = d.shape[0]
    return w[0:1] * pltpu.roll(d, n - 1, 0) + w[1:2] * d + w[2:3] * pltpu.roll(d, 1, 0)


def _center(x):
    return x[HALO:x.shape[0] - HALO]


def _retention(name, a, b, v, T):
    da = a.shape[1] // RET_HEADS
    dv = v.shape[1] // RET_HEADS
    nc = T // CHUNK
    log_gammas = [math.log(1.0 - 2.0 ** (-5.0 - h)) for h in range(RET_HEADS)]

    def body(a_ref, b_ref, v_ref, o_ref, st, st_b):
        h = pl.program_id(0)
        lg = jnp.float32(log_gammas[RET_HEADS - 1])
        for k in range(RET_HEADS - 2, -1, -1):
            lg = jnp.where(h == k, jnp.float32(log_gammas[k]), lg)
        li = lax.broadcasted_iota(jnp.int32, (CHUNK, CHUNK), 0)
        si = lax.broadcasted_iota(jnp.int32, (CHUNK, CHUNK), 1)
        dmat = jnp.exp(lg * jnp.abs(li - si).astype(F32))
        pos = lax.broadcasted_iota(jnp.int32, (CHUNK, 1), 0).astype(F32)
        kdec_f = jnp.exp((CHUNK - 1 - pos) * lg)
        qdec_f = jnp.exp((pos + 1) * lg)
        kdec_b = jnp.exp(pos * lg)
        qdec_b = jnp.exp((CHUNK - pos) * lg)
        cdec = jnp.exp(CHUNK * lg)

        def rows(n):
            return pl.ds(pl.multiple_of(n * CHUNK, CHUNK), CHUNK)

        st[...] = jnp.zeros_like(st)
        st_b[...] = jnp.zeros_like(st_b)
        o_ref[...] = jnp.zeros_like(o_ref)

        def step(m, carry):
            r = rows(m)
            av, bv, vv = a_ref[r, :], b_ref[r, :], v_ref[r, :].astype(BF16)
            s = _dot_nt(av.astype(BF16), bv.astype(BF16)) * dmat
            o_ref[r, :] += _dot(s.astype(BF16), vv) + _dot((av * qdec_f).astype(BF16), st[...].astype(BF16))
            st[...] = cdec * st[...] + _dot_tn((bv * kdec_f).astype(BF16), vv)
            r = rows(nc - 1 - m)
            av, bv, vv = a_ref[r, :], b_ref[r, :], v_ref[r, :].astype(BF16)
            o_ref[r, :] += _dot((av * qdec_b).astype(BF16), st_b[...].astype(BF16))
            st_b[...] = cdec * st_b[...] + _dot_tn((bv * kdec_b).astype(BF16), vv)
            return carry

        lax.fori_loop(0, nc, step, 0, unroll=3 if nc % 3 == 0 else 1)

    return pl.pallas_call(
        body, name=name, grid=(RET_HEADS,),
        in_specs=[pl.BlockSpec((T, da), lambda h: (0, h)), pl.BlockSpec((T, da), lambda h: (0, h)),
                  pl.BlockSpec((T, dv), lambda h: (0, h))],
        out_specs=pl.BlockSpec((T, dv), lambda h: (0, h)),
        out_shape=jax.ShapeDtypeStruct((T, RET_HEADS * dv), F32),
        scratch_shapes=[pltpu.VMEM((da, dv), F32), pltpu.VMEM((da, dv), F32)],
        compiler_params=_params(("arbitrary",)),
    )(a, b, v)


def _softplus(x):
    return jnp.maximum(x, 0.0) + jnp.log1p(jnp.exp(-jnp.abs(x)))


def _lane_lo():
    return lax.broadcasted_iota(jnp.int32, (1, CHUNK), 1) < SSD_HEAD_DIM


def _pair_cols(col, j):
    return jnp.where(_lane_lo(), col[:, 2 * j:2 * j + 1], col[:, 2 * j + 1:2 * j + 2])


def _pair_rows(colr, j):
    lo = lax.broadcasted_iota(jnp.int32, (CHUNK, 1), 0) < SSD_HEAD_DIM
    return jnp.where(lo, colr[2 * j:2 * j + 1, :], colr[2 * j + 1:2 * j + 2, :])


def _onehot8(h):
    return (lax.broadcasted_iota(jnp.int32, (1, HEADS_PER_GROUP), 1) == h).astype(F32)


def _ssd_pre(d, c, rawc, rawr, bc, br, alc, alr):
    li = lax.broadcasted_iota(jnp.int32, (CHUNK, CHUNK), 0)
    si = lax.broadcasted_iota(jnp.int32, (CHUNK, CHUNK), 1)
    dif = li - si if d == 0 else si - li
    mask = dif >= 0
    mask_t = dif <= 0
    rowc = c * CHUNK + lax.broadcasted_iota(jnp.int32, (CHUNK, 1), 0)
    rowr = c * CHUNK + lax.broadcasted_iota(jnp.int32, (1, CHUNK), 1)
    dtc = jnp.where(rowc >= PAD_ROWS, _softplus(rawc + bc), 0.0)
    dtr = jnp.where(rowr >= PAD_ROWS, _softplus(rawr + br), 0.0)
    ac = -jnp.exp(alc)
    ar = -jnp.exp(alr)
    dlc = dtc * ac
    dlr = dtr * ar
    alpc = _dot(mask.astype(F32), dlc, precision=HIGHEST)
    alpr = _dot(dlr, mask_t.astype(F32), precision=HIGHEST)
    endc = jnp.sum(dlc, axis=0, keepdims=True)
    endr = jnp.sum(dlr, axis=1, keepdims=True)
    return dict(mask=mask, mask_t=mask_t, dtc=dtc, ac=ac, alpc=alpc, alpr=alpr, endc=endc, endr=endr,
                valid=rowc >= PAD_ROWS)


def _chunk_of(d, n, nc):
    return n + d * (nc - 1 - 2 * n)


GROUP_WIDTH = HEADS_PER_GROUP * SSD_HEAD_DIM


def _ssd_in_specs(d, cfn):
    return [
        pl.BlockSpec((CHUNK, GROUP_WIDTH), lambda g, n: (cfn(d, n), g)),
        pl.BlockSpec((CHUNK, SSD_STATE), lambda g, n: (cfn(d, n), g)),
        pl.BlockSpec((CHUNK, SSD_STATE), lambda g, n: (cfn(d, n), g)),
        pl.BlockSpec((None, None, CHUNK, HEADS_PER_GROUP), lambda g, n: (d, g, cfn(d, n), 0)),
        pl.BlockSpec((None, None, HEADS_PER_GROUP, CHUNK), lambda g, n: (d, g, 0, cfn(d, n))),
        pl.BlockSpec((None, None, 1, HEADS_PER_GROUP), lambda g, n: (d, g, 0, 0)),
        pl.BlockSpec((None, None, HEADS_PER_GROUP, 1), lambda g, n: (d, g, 0, 0)),
        pl.BlockSpec((None, None, 1, HEADS_PER_GROUP), lambda g, n: (d, g, 0, 0)),
        pl.BlockSpec((None, None, HEADS_PER_GROUP, 1), lambda g, n: (d, g, 0, 0)),
    ]


N_SSD_IN = 9


def _ssd_fwd(xs, bm, cm, small, T):
    nc = T // CHUNK
    cfn = lambda d, n: _chunk_of(d, n, nc)

    def one_direction(d, n, ins, y_ref, hs_ref, h_scr):
        x_ref, b_ref, c_ref, rawc_ref, rawr_ref, bc_ref, br_ref, alc_ref, alr_ref = ins
        c = cfn(d, n)
        q = _ssd_pre(d, c, rawc_ref[...], rawr_ref[...], bc_ref[...], br_ref[...], alc_ref[...], alr_ref[...])
        bv = b_ref[...].astype(BF16)
        cv = c_ref[...].astype(BF16)
        cb = _dot_nt(cv, bv)
        lo = _lane_lo()
        for j in range(PAIRS_PER_GROUP):
            xp = x_ref[:, j * CHUNK:(j + 1) * CHUNK]
            xd = xp * _pair_cols(q["dtc"], j)
            xdb = xd.astype(BF16)
            yi = []
            for e in range(2):
                h = 2 * j + e
                lm = jnp.exp(jnp.where(q["mask"], q["alpc"][:, h:h + 1] - q["alpr"][h:h + 1, :], -jnp.inf))
                yi.append(_dot((cb * lm).astype(BF16), xdb))
            alp = _pair_cols(q["alpc"], j)
            hp = h_scr[j]
            hs_ref[j] = hp
            yo = jnp.exp(alp) * _dot_nt(cv, hp.astype(BF16))
            y_ref[:, j * CHUNK:(j + 1) * CHUNK] = (jnp.where(lo, yi[0], yi[1]) + yo).astype(y_ref.dtype)
            de = jnp.exp(_pair_cols(q["endc"], j) - alp)
            h_scr[j] = jnp.exp(_pair_rows(q["endr"], j)) * hp + _dot_tn((xd * de).astype(BF16), bv)

    def body(*refs):
        n = pl.program_id(1)
        ins, (y_f, y_b, hs_f, hs_b, h_scr) = refs[:2 * N_SSD_IN], refs[2 * N_SSD_IN:]

        @pl.when(n == 0)
        def _():
            h_scr[...] = jnp.zeros_like(h_scr)

        one_direction(0, n, ins[:N_SSD_IN], y_f, hs_f, h_scr.at[0])
        one_direction(1, n, ins[N_SSD_IN:], y_b, hs_b, h_scr.at[1])

    y_spec = lambda d: pl.BlockSpec((CHUNK, GROUP_WIDTH), lambda g, n: (cfn(d, n), g))
    hs_spec = lambda d: pl.BlockSpec((None, None, PAIRS_PER_GROUP, CHUNK, SSD_STATE),
                                     lambda g, n: (g, cfn(d, n), 0, 0, 0))
    y_shape = jax.ShapeDtypeStruct((T, SSD_HEADS * SSD_HEAD_DIM), BF16)
    hs_shape = jax.ShapeDtypeStruct((SSD_GROUPS, nc, PAIRS_PER_GROUP, CHUNK, SSD_STATE), F32)
    y_f, y_b, hs_f, hs_b = pl.pallas_call(
        body, name="ssd_fwd", grid=(SSD_GROUPS, nc),
        in_specs=_ssd_in_specs(0, cfn) + _ssd_in_specs(1, cfn),
        out_specs=[y_spec(0), y_spec(1), hs_spec(0), hs_spec(1)],
        out_shape=[y_shape, y_shape, hs_shape, hs_shape],
        scratch_shapes=[pltpu.VMEM((2, PAIRS_PER_GROUP, CHUNK, SSD_STATE), F32)],
        compiler_params=_params(("arbitrary", "arbitrary")),
    )(xs, bm, cm, *small, xs, bm, cm, *small)
    return (y_f, y_b), (hs_f, hs_b)


def _ssd_bwd(xs, bm, cm, small, hs, dy, T):
    nc = T // CHUNK
    cfn = lambda d, n: _chunk_of(1 - d, n, nc)

    def one_direction(d, n, ins, outs, dh_scr):
        x_ref, b_ref, c_ref, rawc_ref, rawr_ref, bc_ref, br_ref, alc_ref, alr_ref, hs_ref, dy_ref = ins
        dx_ref, db_ref, dc_ref, draw_ref, dbias_ref, dalog_ref = outs
        c = cfn(d, n)
        rawc, bc = rawc_ref[...], bc_ref[...]
        q = _ssd_pre(d, c, rawc, rawr_ref[...], bc, br_ref[...], alc_ref[...], alr_ref[...])
        b32, c32 = b_ref[...], c_ref[...]
        bv, cv = b32.astype(BF16), c32.astype(BF16)
        cb = _dot_nt(cv, bv)
        cbt = _dot_nt(bv, cv)
        lo = _lane_lo()
        row_lo = lax.broadcasted_iota(jnp.int32, (CHUNK, 1), 0) < SSD_HEAD_DIM
        dcb = jnp.zeros((CHUNK, CHUNK), F32)
        dcp = jnp.zeros((CHUNK, SSD_STATE), F32)
        dbp = jnp.zeros((CHUNK, SSD_STATE), F32)
        dalp = jnp.zeros((CHUNK, HEADS_PER_GROUP), F32)
        dend = jnp.zeros((1, HEADS_PER_GROUP), F32)
        ddtx = jnp.zeros((CHUNK, HEADS_PER_GROUP), F32)

        def half_sums(t):
            return (jnp.sum(jnp.where(lo, t, 0.0), axis=1, keepdims=True),
                    jnp.sum(jnp.where(lo, 0.0, t), axis=1, keepdims=True))

        for j in range(PAIRS_PER_GROUP):
            xp = x_ref[:, j * CHUNK:(j + 1) * CHUNK]
            dtp = _pair_cols(q["dtc"], j)
            xd = xp * dtp
            xdb = xd.astype(BF16)
            dyp = dy_ref[:, j * CHUNK:(j + 1) * CHUNK]
            dyb = dyp.astype(BF16)
            hn = hs_ref[j]
            hnb = hn.astype(BF16)
            dh1 = dh_scr[j]
            dh1b = dh1.astype(BF16)
            alp = _pair_cols(q["alpc"], j)
            ea = jnp.exp(alp)
            de = jnp.exp(_pair_cols(q["endc"], j) - alp)
            dxi = []
            for e in range(2):
                h = 2 * j + e
                diff = q["alpc"][:, h:h + 1] - q["alpr"][h:h + 1, :]
                lm = jnp.exp(jnp.where(q["mask"], diff, -jnp.inf))
                mt = cbt * jnp.exp(jnp.where(q["mask_t"], -diff, -jnp.inf))
                dxi.append(_dot(mt.astype(BF16), dyb))
                dyeb_h = (jnp.where(lo, dyp, 0.0) if e == 0 else jnp.where(lo, 0.0, dyp)).astype(BF16)
                gl = _dot_nt(dyeb_h, xdb) * lm
                dcb = dcb + gl
                ra = jnp.sum(gl * cb - _dot_nt(xdb, dyeb_h) * mt, axis=1, keepdims=True)
                dalp = dalp + ra * _onehot8(h)
            y_off = ea * _dot_nt(cv, hnb)
            dxs_state = de * _dot_nt(bv, dh1b)
            dxd = jnp.where(lo, dxi[0], dxi[1]) + dxs_state
            dyeb = (dyp * ea).astype(BF16)
            dcp = dcp + _dot(dyeb, hnb)
            dbp = dbp + _dot((xd * de).astype(BF16), dh1b)
            dh_scr[j] = jnp.exp(_pair_rows(q["endr"], j)) * dh1 + _dot_tn(dyeb, cv)
            r0, r1 = half_sums(dyp * y_off - xd * dxs_state)
            dalp = dalp + r0 * _onehot8(2 * j) + r1 * _onehot8(2 * j + 1)
            t0, t1 = half_sums(jnp.sum(xd * dxs_state, axis=0, keepdims=True))
            u = dh1 * hn
            u0 = jnp.sum(jnp.sum(jnp.where(row_lo, u, 0.0), axis=0, keepdims=True), axis=1, keepdims=True)
            u1 = jnp.sum(jnp.sum(jnp.where(row_lo, 0.0, u), axis=0, keepdims=True), axis=1, keepdims=True)
            eend = jnp.exp(q["endc"])
            dend = dend + (t0 + eend * u0) * _onehot8(2 * j) + (t1 + eend * u1) * _onehot8(2 * j + 1)
            dx_ref[:, j * CHUNK:(j + 1) * CHUNK] = (dxd * dtp).astype(dx_ref.dtype)
            w0, w1 = half_sums(dxd * xp)
            ddtx = ddtx + w0 * _onehot8(2 * j) + w1 * _onehot8(2 * j + 1)

        dcbb = dcb.astype(BF16)
        dc_ref[...] = (dcp + _dot(dcbb, bv)).astype(dc_ref.dtype)
        db_ref[...] = (dbp + _dot_tn(dcbb, cv)).astype(db_ref.dtype)
        ddl = _dot(q["mask_t"].astype(F32), dalp, precision=HIGHEST) + dend
        ddt = ddl * q["ac"] + ddtx
        draw = jnp.where(q["valid"], ddt * jax.nn.sigmoid(rawc + bc), 0.0)
        draw_ref[...] = draw
        dbias = jnp.sum(draw, axis=0, keepdims=True)
        dalog = jnp.sum(ddl * q["dtc"], axis=0, keepdims=True) * q["ac"]

        @pl.when(n == 0)
        def _():
            dbias_ref[...] = dbias
            dalog_ref[...] = dalog

        @pl.when(n > 0)
        def _():
            dbias_ref[...] += dbias
            dalog_ref[...] += dalog

    n_in, n_out = N_SSD_IN + 2, 6

    def body(*refs):
        n = pl.program_id(1)
        ins, outs, dh_scr = refs[:2 * n_in], refs[2 * n_in:2 * (n_in + n_out)], refs[-1]

        @pl.when(n == 0)
        def _():
            dh_scr[...] = jnp.zeros_like(dh_scr)

        one_direction(0, n, ins[:n_in], outs[:n_out], dh_scr.at[0])
        one_direction(1, n, ins[n_in:], outs[n_out:], dh_scr.at[1])

    def in_specs(d):
        return _ssd_in_specs(d, cfn) + [
            pl.BlockSpec((None, None, PAIRS_PER_GROUP, CHUNK, SSD_STATE), lambda g, n: (g, cfn(d, n), 0, 0, 0)),
            pl.BlockSpec((CHUNK, GROUP_WIDTH), lambda g, n: (cfn(d, n), g))]

    def out_specs(d):
        acc = pl.BlockSpec((None, 1, HEADS_PER_GROUP), lambda g, n: (g, 0, 0))
        return [pl.BlockSpec((CHUNK, GROUP_WIDTH), lambda g, n: (cfn(d, n), g)),
                pl.BlockSpec((CHUNK, SSD_STATE), lambda g, n: (cfn(d, n), g)),
                pl.BlockSpec((CHUNK, SSD_STATE), lambda g, n: (cfn(d, n), g)),
                pl.BlockSpec((None, CHUNK, HEADS_PER_GROUP), lambda g, n: (g, cfn(d, n), 0)), acc, acc]

    out_shape = [jax.ShapeDtypeStruct((T, SSD_HEADS * SSD_HEAD_DIM), BF16),
                 jax.ShapeDtypeStruct((T, SSD_GROUPS * SSD_STATE), BF16),
                 jax.ShapeDtypeStruct((T, SSD_GROUPS * SSD_STATE), BF16),
                 jax.ShapeDtypeStruct((SSD_GROUPS, T, HEADS_PER_GROUP), F32),
                 jax.ShapeDtypeStruct((SSD_GROUPS, 1, HEADS_PER_GROUP), F32),
                 jax.ShapeDtypeStruct((SSD_GROUPS, 1, HEADS_PER_GROUP), F32)]
    res = pl.pallas_call(
        body, name="ssd_bwd", grid=(SSD_GROUPS, nc),
        in_specs=in_specs(0) + in_specs(1), out_specs=out_specs(0) + out_specs(1), out_shape=out_shape * 2,
        scratch_shapes=[pltpu.VMEM((2, PAIRS_PER_GROUP, CHUNK, SSD_STATE), F32)],
        compiler_params=_params(("arbitrary", "arbitrary")),
    )(xs, bm, cm, *small, hs[0], dy, xs, bm, cm, *small, hs[1], dy)
    return [(res[k], res[n_out + k]) for k in range(n_out)]


def _rot(x, cs, sn):
    return x * cs + pltpu.roll(x, RET_QK_DIM // 2, 1) * sn


def _rot_t(d, cs, sn):
    return d * cs + pltpu.roll(d * sn, RET_QK_DIM // 2, 1)


def _ret_post(y, g, w):
    parts = []
    for h in range(RET_HEADS):
        yh = y[:, h * RET_V_DIM:(h + 1) * RET_V_DIM]
        mu = jnp.mean(yh, axis=-1, keepdims=True)
        var = jnp.mean(jnp.square(yh - mu), axis=-1, keepdims=True)
        parts.append((yh - mu) * lax.rsqrt(var + EPS))
    return _silu(g) * (jnp.concatenate(parts, axis=1) * w)


def _ssd_post(yf, yb, xs, z, dskip, w):
    y = (yf + yb + xs * dskip) * _silu(z)
    return y * lax.rsqrt(jnp.mean(y * y, axis=-1, keepdims=True) + EPS) * w


def _merge(gates, yr, ys, valid):
    m = jax.nn.sigmoid(gates[:, :D_MODEL]) * yr + jax.nn.sigmoid(gates[:, D_MODEL:]) * ys
    return jnp.where(valid, m, 0.0)


def _rope_tables(T):
    half = RET_QK_DIM // 2
    inv = ROPE_BASE ** (-jnp.arange(half, dtype=F32) / half)
    pos = (jnp.arange(T) - PAD_ROWS).astype(F32)
    ang = pos[:, None] * inv[None, :]
    cos, sin = jnp.cos(ang), jnp.sin(ang)
    return jnp.concatenate([cos, cos], axis=1), jnp.concatenate([-sin, sin], axis=1)


def _per_group(v):
    c = v.reshape(SSD_GROUPS, 1, HEADS_PER_GROUP)
    return c, c.reshape(SSD_GROUPS, HEADS_PER_GROUP, 1)


def _local_step(x, target, w, tick, late_weights, early_grads, in_grads):
    S = x.shape[0]
    T = S + CHUNK
    tm = _tile_rows(T)
    c0 = _const(0)

    h0 = jnp.concatenate([jnp.zeros((PAD_ROWS, D_MODEL), F32), w["meta_tokens"], x], axis=0)
    tgt = jnp.concatenate([jnp.zeros((CHUNK, D_MODEL), F32), target], axis=0)
    w_in = {name: w["w_in_t"][a:b] for name, a, b in SEGMENTS}
    w_in["dt"] = jnp.pad(w_in["dt"], ((0, CHUNK - 2 * SSD_HEADS), (0, 0)))

    def norm_cast(name, h, nw):
        return _rows(name, lambda i, hv, wv: (_rms(hv, wv),), T, 1, [(h, D_MODEL, c0)], [(nw, D_MODEL, c0)],
                     [(D_MODEL, D_MODEL, c0, BF16)])[0]

    u = norm_cast("norm_mix", h0, w["norm_mix_w"] + tick)
    proj = {name: _mm("proj_" + name, u, w_in[name], "nt", out_dtype=F32 if name == "dt" else BF16)
            for name, _, _ in SEGMENTS}

    cs, sn = _rope_tables(T)
    scale = RET_QK_DIM ** -0.5

    def rot_fn(i, qk, csv, snv):
        q = [_rot(qk[:, h * 128:(h + 1) * 128], csv, snv) for h in range(RET_HEADS)]
        k = [_rot(qk[:, (RET_HEADS + h) * 128:(RET_HEADS + h + 1) * 128], csv, snv) * scale for h in range(RET_HEADS)]
        return jnp.concatenate(q, axis=1), jnp.concatenate(k, axis=1)

    qr, kr = _rows("rotary", rot_fn, T, 1, [(proj["qk"], 1024, c0), (cs, 128, c0), (sn, 128, c0)], [],
                   [(512, 512, c0, F32), (512, 512, c0, F32)])
    y_ret = _retention("retention", qr, kr, proj["v"], T)
    a_ret = _rows("ret_post", lambda i, y, g, gw: (_ret_post(y, g, gw),), T, 1,
                  [(y_ret, 1024, c0), (proj["g"], 1024, c0)], [(w["ret_gn_w"], 1024, c0)],
                  [(1024, 1024, c0, BF16)])[0]

    conv_w = {"xs": w["w_ssd_conv"][:, :2048], "B": w["w_ssd_conv"][:, 2048:2560], "C": w["w_ssd_conv"][:, 2560:]}
    conv_b = {"xs": w["b_ssd_conv"][:, :2048], "B": w["b_ssd_conv"][:, 2048:2560], "C": w["b_ssd_conv"][:, 2560:]}

    def ssd_conv_fn(i, xe, cw, cb):
        r = _row_ids(i, T, True)
        xe = jnp.where((r >= 0) & (r < T), xe, 0.0)
        return (_center(jnp.where(r >= PAD_ROWS, _silu(_conv3(xe, cw) + cb), 0.0)),)

    act = {}
    for name in ("xs", "B", "C"):
        wd = proj[name].shape[1]
        cw = 512
        act[name] = _rows("ssd_conv_" + name, ssd_conv_fn, T, wd // cw, [(proj[name], cw, lambda j: j)],
                          [(conv_w[name], cw, lambda j: j), (conv_b[name], cw, lambda j: j)],
                          [(wd, cw, lambda j: j, BF16)], halo=True)[0]

    raw = proj["dt"][:, :2 * SSD_HEADS].reshape(T, 2, SSD_GROUPS, HEADS_PER_GROUP)
    rawc = raw.transpose(1, 2, 0, 3)
    rawr = raw.transpose(1, 2, 3, 0)
    bias = [_per_group(w["dt_bias_f"]), _per_group(w["dt_bias_b"])]
    alog = [_per_group(w["a_log_f"]), _per_group(w["a_log_b"])]
    small = (rawc, rawr, jnp.stack([bias[0][0], bias[1][0]]), jnp.stack([bias[0][1], bias[1][1]]),
             jnp.stack([alog[0][0], alog[1][0]]), jnp.stack([alog[0][1], alog[1][1]]))
    y_dir, states = _ssd_fwd(act["xs"], act["B"], act["C"], small, T)

    dskip_e = jnp.repeat(w["d_skip"], SSD_HEAD_DIM, axis=1)
    gcol = lambda j: j
    gw_ = 512
    a_ssd = _rows("ssd_post", lambda i, yf, yb, xv, zv, dk, nw: (_ssd_post(yf, yb, xv, zv, dk, nw),), T, SSD_GROUPS,
                  [(y_dir[0], gw_, gcol), (y_dir[1], gw_, gcol), (act["xs"], gw_, gcol), (proj["z"], gw_, gcol)],
                  [(dskip_e, gw_, gcol), (w["ssd_norm_w"], gw_, gcol)], [(2048, gw_, gcol, BF16)])[0]

    w = dict(w, **late_weights(a_ssd))
    w_up_g, w_up_u = w["w_ffn_up_t"][:D_FF], w["w_ffn_up_t"][D_FF:]
    y_ret_o = _mm("ret_out", a_ret, w["w_ret_out"], "nn", out_dtype=BF16)
    y_ssd_o = _mm("ssd_out", a_ssd, w["w_ssd_out"], "nn", out_dtype=BF16)

    def merge_fn(i, gates, yr, ys):
        return (_merge(gates, yr, ys, _row_ids(i, T) >= PAD_ROWS),)

    merged = _rows("merge", merge_fn, T, 1, [(proj["gates"], 2048, c0), (y_ret_o, 1024, c0), (y_ssd_o, 1024, c0)], [],
                   [(1024, 1024, c0, BF16)])[0]
    h1 = _mm("mix_out", merged, w["w_out"], "nn", add=h0)

    n2 = norm_cast("norm_ffn", h1, w["norm_ffn_w"])
    fg_pre = _mm("ffn_up_g", n2, w_up_g, "nt", out_dtype=BF16)
    fu_pre = _mm("ffn_up_u", n2, w_up_u, "nt", out_dtype=BF16)
    cwg, cwu = w["w_ffn_conv"][:, :D_FF], w["w_ffn_conv"][:, D_FF:]
    cbg, cbu = w["b_ffn_conv"][:, :D_FF], w["b_ffn_conv"][:, D_FF:]
    fcol = lambda j: j
    fw = 1408

    def ffn_act_fn(i, ge, ue, wg, wu, bg, bu):
        return (_center(_silu(_conv3(ge, wg) + bg) * (_conv3(ue, wu) + bu)),)

    def ext_valid(i):
        r = _row_ids(i, T, True)
        return (r >= 0) & (r < T)

    def ffn_act_masked(i, ge, ue, wg, wu, bg, bu):
        v = ext_valid(i)
        return ffn_act_fn(i, jnp.where(v, ge, 0.0), jnp.where(v, ue, 0.0), wg, wu, bg, bu)

    a2 = _rows("ffn_act", ffn_act_masked, T, D_FF // fw, [(fg_pre, fw, fcol), (fu_pre, fw, fcol)],
               [(cwg, fw, fcol), (cwu, fw, fcol), (cbg, fw, fcol), (cbu, fw, fcol)], [(D_FF, fw, fcol, BF16)],
               halo=True)[0]
    h2 = _mm("ffn_down", a2, w["w_ffn_down"], "nn", add=h1)

    fnw = w["final_norm_w"].reshape(1, D_MODEL)

    def loss_fn(i, hv, tv, nw):
        valid = _row_ids(i, T) >= CHUNK
        y, vjp = jax.vjp(_rms, hv, nw)
        diff = jnp.where(valid, y - tv, 0.0)
        dh, dw = vjp(diff * (1.0 / D_MODEL))
        part = 0.5 / D_MODEL * jnp.sum(jnp.sum(diff * diff, axis=1, keepdims=True), axis=0, keepdims=True)
        return dh, jnp.broadcast_to(part, (1, 128)), dw

    dh2, loss_acc, d_fnw = _rows("loss", loss_fn, T, 1, [(h2, D_MODEL, c0), (tgt, D_MODEL, c0)], [(fnw, D_MODEL, c0)],
                                 [(D_MODEL, D_MODEL, c0, F32)], [(1, 128, 128, c0), (1, D_MODEL, D_MODEL, c0)])
    loss = loss_acc[0, 0]
    grads = {"final_norm_w": d_fnw.reshape(D_MODEL)}

    da2 = _mm("d_ffn_act", dh2, w["w_ffn_down"], "nt", out_dtype=BF16)
    grads["w_ffn_down"] = _mm("g_ffn_down", a2, dh2, "tn", out_dtype=BF16)

    def ffn_bwd_fn(i, ge, ue, de, wg, wu, bg, bu):
        v = ext_valid(i)
        ge, ue, de = jnp.where(v, ge, 0.0), jnp.where(v, ue, 0.0), jnp.where(v, de, 0.0)
        fg = _conv3(ge, wg) + bg
        fu = _conv3(ue, wu) + bu
        sg = jax.nn.sigmoid(fg)
        dfg = de * fu * (sg * (1.0 + fg * (1.0 - sg)))
        dfu = de * (fg * sg)
        n = ge.shape[0]

        def wgrad(df, xe):
            df_c = _center(df)
            return jnp.concatenate([jnp.sum(df_c * _center(pltpu.roll(xe, 1, 0)), axis=0, keepdims=True),
                                    jnp.sum(df_c * _center(xe), axis=0, keepdims=True),
                                    jnp.sum(df_c * _center(pltpu.roll(xe, n - 1, 0)), axis=0, keepdims=True)], axis=0)

        return (_center(_conv3_t(dfg, wg)), _center(_conv3_t(dfu, wu)), wgrad(dfg, ge), wgrad(dfu, ue),
                jnp.sum(_center(dfg), axis=0, keepdims=True), jnp.sum(_center(dfu), axis=0, keepdims=True))

    dfg_pre, dfu_pre, g_cwg, g_cwu, g_cbg, g_cbu = _rows(
        "ffn_act_bwd", ffn_bwd_fn, T, D_FF // fw, [(fg_pre, fw, fcol), (fu_pre, fw, fcol), (da2, fw, fcol)],
        [(cwg, fw, fcol), (cwu, fw, fcol), (cbg, fw, fcol), (cbu, fw, fcol)],
        [(D_FF, fw, fcol, BF16), (D_FF, fw, fcol, BF16)],
        [(3, D_FF, fw, fcol), (3, D_FF, fw, fcol), (1, D_FF, fw, fcol), (1, D_FF, fw, fcol)], halo=True)
    grads["w_ffn_conv"] = jnp.concatenate([g_cwg, g_cwu], axis=1)
    grads["b_ffn_conv"] = jnp.concatenate([g_cbg, g_cbu], axis=1)
    dn2 = _mm("d_norm_ffn_g", dfg_pre, w_up_g, "nn")
    dn2 = _mm("d_norm_ffn_u", dfu_pre, w_up_u, "nn", add=dn2)
    grads["w_ffn_up_t"] = jnp.concatenate([_mm("g_ffn_up_g", dfg_pre, n2, "tn", out_dtype=BF16), _mm("g_ffn_up_u", dfu_pre, n2, "tn", out_dtype=BF16)],
                                          axis=0)

    def norm_bwd(name, h, nw, dn, dres):
        def fn(i, hv, dnv, drv, wv):
            _, vjp = jax.vjp(_rms, hv, wv)
            dh, dw = vjp(dnv)
            return dh + drv, dw
        return _rows(name, fn, T, 1, [(h, D_MODEL, c0), (dn, D_MODEL, c0), (dres, D_MODEL, c0)], [(nw, D_MODEL, c0)],
                     [(D_MODEL, D_MODEL, c0, F32)], [(1, D_MODEL, D_MODEL, c0)])

    dh1, grads["norm_ffn_w"] = norm_bwd("norm_ffn_bwd", h1, w["norm_ffn_w"], dn2, dh2)

    dmerged = _mm("d_merged", dh1, w["w_out"], "nt", out_dtype=BF16)
    grads["w_out"] = _mm("g_out", merged, dh1, "tn", out_dtype=BF16)

    def merge_bwd_fn(i, gates, yr, ys, dm):
        valid = _row_ids(i, T) >= PAD_ROWS
        _, vjp = jax.vjp(lambda a, b, c: _merge(a, b, c, valid), gates, yr, ys)
        return vjp(dm)

    dgates, dyr, dys = _rows("merge_bwd", merge_bwd_fn, T, 1,
                             [(proj["gates"], 2048, c0), (y_ret_o, 1024, c0), (y_ssd_o, 1024, c0), (dmerged, 1024, c0)],
                             [], [(2048, 2048, c0, BF16), (1024, 1024, c0, BF16), (1024, 1024, c0, BF16)])
    dproj = {"gates": dgates}

    da_ssd = _mm("d_ssd_act", dys, w["w_ssd_out"], "nt", out_dtype=BF16)
    grads["w_ssd_out"] = _mm("g_ssd_out", a_ssd, dys, "tn", out_dtype=BF16)

    def ssd_post_bwd_fn(i, yf, yb, xv, zv, da, dk, nw):
        _, vjp = jax.vjp(_ssd_post, yf, yb, xv, zv, dk, nw)
        dyf, _, dxv, dzv, ddk, dnw = vjp(da)
        return dyf, dxv, dzv, ddk, dnw

    dy_ssd, dxs_skip, dproj["z"], g_dskip_e, grads["ssd_norm_w"] = _rows(
        "ssd_post_bwd", ssd_post_bwd_fn, T, SSD_GROUPS,
        [(y_dir[0], gw_, gcol), (y_dir[1], gw_, gcol), (act["xs"], gw_, gcol), (proj["z"], gw_, gcol),
         (da_ssd, gw_, gcol)],
        [(dskip_e, gw_, gcol), (w["ssd_norm_w"], gw_, gcol)],
        [(2048, gw_, gcol, BF16), (2048, gw_, gcol, BF16), (2048, gw_, gcol, BF16)],
        [(1, 2048, gw_, gcol), (1, 2048, gw_, gcol)])
    grads["d_skip"] = g_dskip_e.reshape(SSD_HEADS, SSD_HEAD_DIM).sum(axis=1).reshape(1, SSD_HEADS)

    dxs_dir, db_dir, dc_dir, draw, g_bias, g_alog = _ssd_bwd(act["xs"], act["B"], act["C"], small, states, dy_ssd, T)
    grads["dt_bias_f"], grads["dt_bias_b"] = g_bias[0].reshape(1, SSD_HEADS), g_bias[1].reshape(1, SSD_HEADS)
    grads["a_log_f"], grads["a_log_b"] = g_alog[0].reshape(1, SSD_HEADS), g_alog[1].reshape(1, SSD_HEADS)
    d_dt = jnp.stack(draw).transpose(2, 0, 1, 3).reshape(T, 2 * SSD_HEADS)
    dproj["dt"] = jnp.pad(d_dt, ((0, 0), (0, CHUNK - 2 * SSD_HEADS))).astype(BF16)

    def make_conv_bwd(nsum):
        def fn(i, xe, *rest):
            ds, (cw, cb) = rest[:nsum], rest[nsum:]
            r = _row_ids(i, T, True)
            dact = ds[0]
            for t in ds[1:]:
                dact = dact + t
            dact = jnp.where((r >= PAD_ROWS) & (r < T), dact, 0.0)
            xe = jnp.where((r >= 0) & (r < T), xe, 0.0)
            pre = _conv3(xe, cw) + cb
            sg = jax.nn.sigmoid(pre)
            dpre = dact * (sg * (1.0 + pre * (1.0 - sg)))
            n = xe.shape[0]
            dpc = _center(dpre)
            dw = jnp.concatenate([jnp.sum(dpc * _center(pltpu.roll(xe, 1, 0)), axis=0, keepdims=True),
                                  jnp.sum(dpc * _center(xe), axis=0, keepdims=True),
                                  jnp.sum(dpc * _center(pltpu.roll(xe, n - 1, 0)), axis=0, keepdims=True)], axis=0)
            return _center(_conv3_t(dpre, cw)), dw, jnp.sum(dpc, axis=0, keepdims=True)
        return fn

    g_cw, g_cb = {}, {}
    cots = {"xs": [(dxs_dir[0], 512, gcol), (dxs_dir[1], 512, gcol), (dxs_skip, 512, gcol)],
            "B": [(db_dir[0], 512, gcol), (db_dir[1], 512, gcol)],
            "C": [(dc_dir[0], 512, gcol), (dc_dir[1], 512, gcol)]}
    for name in ("xs", "B", "C"):
        wd = proj[name].shape[1]
        dproj[name], g_cw[name], g_cb[name] = _rows(
            "ssd_conv_bwd_" + name, make_conv_bwd(len(cots[name])), T, wd // 512,
            [(proj[name], 512, gcol)] + cots[name], [(conv_w[name], 512, gcol), (conv_b[name], 512, gcol)],
            [(wd, 512, gcol, BF16)], [(3, wd, 512, gcol), (1, wd, 512, gcol)], halo=True)
    grads["w_ssd_conv"] = jnp.concatenate([g_cw["xs"], g_cw["B"], g_cw["C"]], axis=1)
    grads["b_ssd_conv"] = jnp.concatenate([g_cb["xs"], g_cb["B"], g_cb["C"]], axis=1)

    da_ret = _mm("d_ret_act", dyr, w["w_ret_out"], "nt", out_dtype=BF16)
    grads["w_ret_out"] = _mm("g_ret_out", a_ret, dyr, "tn", out_dtype=BF16)
    tick = early_grads({n: grads.pop(n) for n in ("w_ffn_up_t", "w_ret_out", "w_ssd_out", "w_out", "w_ffn_down")})

    def ret_post_bwd_fn(i, y, g, da, gw):
        _, vjp = jax.vjp(_ret_post, y, g, gw)
        return vjp(da)

    dy_ret, dproj["g"], grads["ret_gn_w"] = _rows(
        "ret_post_bwd", ret_post_bwd_fn, T, 1, [(y_ret, 1024, c0), (proj["g"], 1024, c0), (da_ret, 1024, c0)],
        [(w["ret_gn_w"] + tick, 1024, c0)], [(1024, 1024, c0, BF16), (1024, 1024, c0, BF16)], [(1, 1024, 1024, c0)])
    dproj["v"] = _retention("retention_dv", kr, qr, dy_ret, T)
    dqr = _retention("retention_dq", dy_ret, proj["v"], kr, T)
    dkr = _retention("retention_dk", proj["v"], dy_ret, qr, T)

    def rot_bwd_fn(i, dq, dk, csv, snv):
        parts = [_rot_t(dq[:, h * 128:(h + 1) * 128], csv, snv) for h in range(RET_HEADS)]
        parts += [_rot_t(dk[:, h * 128:(h + 1) * 128] * scale, csv, snv) for h in range(RET_HEADS)]
        return (jnp.concatenate(parts, axis=1),)

    dproj["qk"] = _rows("rotary_bwd", rot_bwd_fn, T, 1, [(dqr, 512, c0), (dkr, 512, c0), (cs, 128, c0), (sn, 128, c0)],
                        [], [(1024, 1024, c0, BF16)])[0]

    g_in = [_mm("g_in_" + name, dproj[name], u, "tn", out_dtype=BF16) for name, _, _ in SEGMENTS]
    g_in[7] = g_in[7][:2 * SSD_HEADS]
    tick = in_grads(jnp.concatenate(g_in, axis=0))
    du = _mm("d_u_dt", dproj["dt"] + tick.astype(BF16), w_in["dt"], "nn")
    for name, _, _ in SEGMENTS:
        if name != "dt":
            du = _mm("d_u_" + name, dproj[name], w_in[name], "nn", add=du)
    dh0, grads["norm_mix_w"] = norm_bwd("norm_mix_bwd", h0, w["norm_mix_w"], du, dh1)
    grads["meta_tokens"] = dh0[PAD_ROWS:CHUNK]
    return loss, dh0[CHUNK:], grads


MESH_ID = pl.DeviceIdType.MESH
ANY = pl.BlockSpec(memory_space=pl.ANY)


def _me_and_peers():
    x, y, c = lax.axis_index("x"), lax.axis_index("y"), lax.axis_index("c")
    peers = []
    for k in range(1, N_DEV):
        px = 1 - x if k & 4 else x
        py = 1 - y if k & 2 else y
        pc = 1 - c if k & 1 else c
        peers.append(((px, py, pc), 4 * px + 2 * py + pc))
    return 4 * x + 2 * y + c, peers


def _push_blocks(name, src, per_peer):
    blk = src.shape[1:] if per_peer else src.shape

    def body(src_ref, out_ref, send_sems, recv_sems, local_sem):
        me, peers = _me_and_peers()
        mine = src_ref.at[me] if per_peer else src_ref
        local = pltpu.make_async_copy(mine, out_ref.at[me], local_sem)
        local.start()
        sends = []
        for k, (dev, idx) in enumerate(peers):
            cp = pltpu.make_async_remote_copy(
                src_ref=src_ref.at[idx] if per_peer else src_ref, dst_ref=out_ref.at[me],
                send_sem=send_sems.at[k], recv_sem=recv_sems.at[k], device_id=dev, device_id_type=MESH_ID)
            cp.start()
            sends.append(cp)
        for k, (dev, idx) in enumerate(peers):
            pltpu.make_async_remote_copy(
                src_ref=mine, dst_ref=out_ref.at[idx], send_sem=send_sems.at[k], recv_sem=recv_sems.at[k],
                device_id=dev, device_id_type=MESH_ID).wait_recv()
        for cp in sends:
            cp.wait_send()
        local.wait()

    return pl.pallas_call(
        body, name=name, in_specs=[ANY], out_specs=ANY,
        out_shape=jax.ShapeDtypeStruct((N_DEV,) + tuple(blk), src.dtype),
        scratch_shapes=[pltpu.SemaphoreType.DMA((N_DEV - 1,)), pltpu.SemaphoreType.DMA((N_DEV - 1,)),
                        pltpu.SemaphoreType.DMA],
    )(src)


def _gather_two_level(name, src):
    def body(x_ref, out_ref, send_sems, recv_sems, local_sem):
        x, y, c = lax.axis_index("x"), lax.axis_index("y"), lax.axis_index("c")
        me, sibling = (x, y, c), (x, y, 1 - c)
        chips = [(1 - x, y), (x, 1 - y), (1 - x, 1 - y)]

        def rows(px, py, pc):
            return out_ref.at[4 * px + 2 * py + pc]

        def copy(k, block, to, src_ref=None):
            return pltpu.make_async_remote_copy(
                src_ref=rows(*block) if src_ref is None else src_ref, dst_ref=rows(*block),
                send_sem=send_sems.at[k], recv_sem=recv_sems.at[k], device_id=to, device_id_type=MESH_ID)

        mine = pltpu.make_async_copy(x_ref, rows(*me), local_sem)
        mine.start()
        first = [copy(0, me, sibling, x_ref)] + [copy(1 + j, me, (*chip, c), x_ref) for j, chip in enumerate(chips)]
        for cp in first:
            cp.start()
        passed = [copy(4 + j, (*chip, c), sibling) for j, chip in enumerate(chips)]
        for j, chip in enumerate(chips):
            copy(1 + j, (*chip, c), me).wait_recv()
            passed[j].start()
        copy(0, sibling, me).wait_recv()
        for j, chip in enumerate(chips):
            copy(4 + j, (*chip, 1 - c), me).wait_recv()
        for cp in first + passed:
            cp.wait_send()
        mine.wait()

    return pl.pallas_call(
        body, name=name, in_specs=[ANY], out_specs=ANY,
        out_shape=jax.ShapeDtypeStruct((N_DEV,) + tuple(src.shape), src.dtype),
        scratch_shapes=[pltpu.SemaphoreType.DMA((N_DEV - 1,)), pltpu.SemaphoreType.DMA((N_DEV - 1,)),
                        pltpu.SemaphoreType.DMA],
    )(src)


HBM = pl.BlockSpec(memory_space=pltpu.HBM)
SEM = pl.BlockSpec(memory_space=pltpu.SEMAPHORE)
EFFECT = pltpu.SideEffectType.DATAFLOW_SIDE_EFFECTING


def _peer_copy(src_ref, land_ref, send_sems, recv_sems, per_peer, me, k, dev, idx, receiving):
    return pltpu.make_async_remote_copy(
        src_ref=src_ref.at[idx] if per_peer else src_ref, dst_ref=land_ref.at[idx if receiving else me],
        send_sem=send_sems.at[k], recv_sem=recv_sems.at[k], device_id=dev, device_id_type=MESH_ID)


def _push_start(name, src, per_peer):
    blk = src.shape[1:] if per_peer else src.shape
    land_shape = (N_DEV,) + tuple(blk)

    def body(src_ref, land_ref, send_sems, recv_sems, src_thru, land_thru, token):
        me, peers = _me_and_peers()
        for k, (dev, idx) in enumerate(peers):
            _peer_copy(src_ref, land_ref, send_sems, recv_sems, per_peer, me, k, dev, idx, False).start()
        token[...] = jnp.zeros_like(token)

    return pl.pallas_call(
        body, name=name,
        out_shape=(pltpu.SemaphoreType.DMA((N_DEV - 1,)), pltpu.SemaphoreType.DMA((N_DEV - 1,)),
                   pltpu.HBM(src.shape, src.dtype), pltpu.HBM(land_shape, src.dtype),
                   jax.ShapeDtypeStruct((8, 128), F32)),
        in_specs=(HBM, HBM), out_specs=(SEM, SEM, HBM, HBM, pl.BlockSpec(memory_space=pltpu.VMEM)),
        input_output_aliases={0: 2, 1: 3}, compiler_params=pltpu.CompilerParams(has_side_effects=EFFECT),
    )(pltpu.with_memory_space_constraint(src, pltpu.HBM),
      pltpu.with_memory_space_constraint(lax.empty(land_shape, src.dtype), pltpu.HBM))


def _push_wait(name, send_sems, recv_sems, src_thru, land_thru, after, per_peer):
    def body(src_ref, land_ref, send_sems, recv_sems, after_ref, src_out, land_out):
        me, peers = _me_and_peers()
        for k, (dev, idx) in enumerate(peers):
            cp = _peer_copy(src_ref, land_ref, send_sems, recv_sems, per_peer, me, k, dev, idx, True)
            cp.wait_send()
            cp.wait_recv()

    return pl.pallas_call(
        body, name=name,
        out_shape=(pltpu.HBM(src_thru.shape, src_thru.dtype), pltpu.HBM(land_thru.shape, land_thru.dtype)),
        in_specs=(HBM, HBM, SEM, SEM, ANY), out_specs=(HBM, HBM), input_output_aliases={0: 0, 1: 1},
        compiler_params=pltpu.CompilerParams(has_side_effects=EFFECT),
    )(src_thru, land_thru, send_sems, recv_sems, after)


def _sum_blocks(name, blocks):
    _, R, C = blocks.shape
    tc = _pick(C, (128,))

    def body(b_ref, o_ref):
        acc = b_ref[0].astype(F32)
        for k in range(1, N_DEV):
            acc = acc + b_ref[k].astype(F32)
        o_ref[...] = acc

    return pl.pallas_call(
        body, name=name, grid=(C // tc,), in_specs=[pl.BlockSpec((N_DEV, R, tc), lambda j: (0, 0, j))],
        out_specs=pl.BlockSpec((R, tc), lambda j: (0, j)), out_shape=jax.ShapeDtypeStruct((R, C), F32),
        compiler_params=_params(("arbitrary",)),
    )(blocks)


def _adamw(name, w, g, m, v):
    R, C = w.shape
    tr = R if R <= 512 else _pick(R, (256, 184, 176, 128, 8))
    spec = pl.BlockSpec((tr, C), lambda i: (i, 0))

    def body(w_ref, g_ref, m_ref, v_ref, d_ref, mo_ref, vo_ref):
        gv = g_ref[...]
        mn = ADAM_B1 * m_ref[...] + (1.0 - ADAM_B1) * gv
        vn = ADAM_B2 * v_ref[...] + (1.0 - ADAM_B2) * jnp.square(gv)
        m_hat = mn / (1.0 - ADAM_B1 ** ADAM_STEP)
        v_hat = vn / (1.0 - ADAM_B2 ** ADAM_STEP)
        d_ref[...] = -ADAM_LR * (m_hat / (jnp.sqrt(v_hat) + ADAM_EPS) + ADAM_WD * w_ref[...])
        mo_ref[...] = mn
        vo_ref[...] = vn

    return pl.pallas_call(
        body, name=name, grid=(R // tr,), in_specs=[spec] * 4, out_specs=[spec] * 3,
        out_shape=[jax.ShapeDtypeStruct((R, C), F32)] * 3, compiler_params=_params(("arbitrary",)),
    )(w, g, m, v)


WEIGHTS = ("meta_tokens", "norm_mix_w", "w_in", "ret_gn_w", "w_ret_out", "w_ssd_conv", "b_ssd_conv", "dt_bias_f",
           "dt_bias_b", "a_log_f", "a_log_b", "d_skip", "ssd_norm_w", "w_ssd_out", "w_out", "norm_ffn_w", "w_ffn_up",
           "w_ffn_conv", "b_ffn_conv", "w_ffn_down", "final_norm_w")
BIG = (("w_in", 1288, True), ("w_ffn_up", 704, True), ("w_ret_out", 128, False), ("w_ssd_out", 256, False),
       ("w_out", 128, False), ("w_ffn_down", 352, False))
REPLICATED = ("norm_mix_w", "ret_gn_w", "b_ssd_conv", "dt_bias_f", "dt_bias_b", "a_log_f", "a_log_b", "d_skip",
              "ssd_norm_w", "norm_ffn_w", "b_ffn_conv", "final_norm_w")
SMALL_SHARDED = (("meta_tokens", 16, 1024), ("w_ssd_conv", 3, 3072), ("w_ffn_conv", 3, 5632))


BIG_IN, BIG_REST = BIG[:1], BIG[1:]


def _pack_big(tree, group):
    parts = []
    for name, _, transposed in group:
        a = tree[name][0]
        parts.append(a.T if transposed else a)
    return jnp.concatenate(parts, axis=0)


def _unpack_big(slab, group):
    out, r0 = {}, 0
    for name, r, transposed in group:
        a = slab[r0:r0 + r]
        out[name] = (a.T if transposed else a)[None]
        r0 += r
    return out


def _pack_flat(arrays, rows):
    flat = jnp.concatenate([a.reshape(-1) for a in arrays])
    return jnp.pad(flat, (0, rows * D_MODEL - flat.shape[0])).reshape(rows, D_MODEL)


def _unpack_flat(slab, shapes):
    flat, out, o = slab.reshape(-1), [], 0
    for s in shapes:
        n = math.prod(s)
        out.append(flat[o:o + n].reshape(s))
        o += n
    return out


def kernel(x, meta_tokens, norm_mix_w, w_in, ret_gn_w, w_ret_out, w_ssd_conv, b_ssd_conv, dt_bias_f, dt_bias_b, a_log_f, a_log_b, d_skip, ssd_norm_w, w_ssd_out, w_out, norm_ffn_w, w_ffn_up, w_ffn_conv, b_ffn_conv, w_ffn_down, final_norm_w, loss_target, m_meta_tokens, m_norm_mix_w, m_w_in, m_ret_gn_w, m_w_ret_out, m_w_ssd_conv, m_b_ssd_conv, m_dt_bias_f, m_dt_bias_b, m_a_log_f, m_a_log_b, m_d_skip, m_ssd_norm_w, m_w_ssd_out, m_w_out, m_norm_ffn_w, m_w_ffn_up, m_w_ffn_conv, m_b_ffn_conv, m_w_ffn_down, m_final_norm_w, v_meta_tokens, v_norm_mix_w, v_w_in, v_ret_gn_w, v_w_ret_out, v_w_ssd_conv, v_b_ssd_conv, v_dt_bias_f, v_dt_bias_b, v_a_log_f, v_a_log_b, v_d_skip, v_ssd_norm_w, v_w_ssd_out, v_w_out, v_norm_ffn_w, v_w_ffn_up, v_w_ffn_conv, v_b_ffn_conv, v_w_ffn_down, v_final_norm_w):
    given = dict(locals())
    wt = {n: given[n] for n in WEIGHTS}
    mt = {n: given["m_" + n] for n in WEIGHTS}
    vt = {n: given["v_" + n] for n in WEIGHTS}
    me = 4 * lax.axis_index("x") + 2 * lax.axis_index("y") + lax.axis_index("c")

    small_names = [n for n, _, _ in SMALL_SHARDED]
    small_local = lambda tree: [tree[n].reshape(r, c // N_DEV) for n, r, c in SMALL_SHARDED]
    all_in = _gather_two_level("gather_w_in", _pack_big(wt, BIG_IN).astype(BF16))
    all_s = _push_blocks("gather_small", _pack_flat(small_local(wt), 8), False)
    rest_src, all_in, all_s = lax.optimization_barrier((_pack_big(wt, BIG_REST).astype(BF16), all_in, all_s))
    rest_flight = _push_start("gather_rest_start", rest_src, False)
    all_s = all_s.reshape(N_DEV, -1)
    full = {"w_in_t": all_in.reshape(-1, D_MODEL)}

    def land_with_own(flight, after, per_peer, name):
        src, land = _push_wait(name, *flight[:4], after, per_peer)
        own = lax.dynamic_slice_in_dim(src, me, 1, axis=0) if per_peer else src[None]
        return lax.dynamic_update_slice_in_dim(land, own, me, axis=0)

    def late_weights(after):
        all_rest = land_with_own(rest_flight, after, False, "gather_rest_wait")
        out, r0 = {}, 0
        for name, r, transposed in BIG_REST:
            out[name + ("_t" if transposed else "")] = all_rest[:, r0:r0 + r].reshape(N_DEV * r, D_MODEL)
            r0 += r
        return out

    flights = {}

    def start_exchange(key, group, gd):
        g_blocks = jnp.concatenate(
            [gd[name + ("_t" if t else "")].reshape(N_DEV, r, D_MODEL) for name, r, t in group], axis=1)
        flights[key] = _push_start("exchange_" + key + "_start", g_blocks.astype(BF16), True)
        return flights[key][4][0, 0]

    o = 0
    for name, r, c in SMALL_SHARDED:
        n = r * c // N_DEV
        full[name] = all_s[:, o:o + n].reshape(N_DEV, r, c // N_DEV).transpose(1, 0, 2).reshape(r, c)
        o += n
    for name in REPLICATED:
        full[name] = wt[name]

    loss, grad_x, g = _local_step(
        x[0], loss_target[0], full, rest_flight[4][0, 0], late_weights,
        lambda gd: start_exchange("rest", BIG_REST, gd), lambda gi: start_exchange("in", BIG_IN, {"w_in_t": gi}))

    last = g["norm_mix_w"]
    g_slabs = {key: _sum_blocks("sum_" + key, land_with_own(flights[key], last, True, "exchange_" + key + "_wait"))
               for key in ("rest", "in")}
    small_parts = [g[n] for n in REPLICATED] + [g[n] for n in small_names] + [loss.reshape(1)]
    g_small = _sum_blocks("sum_small", _push_blocks("gather_small_grads", _pack_flat(small_parts, 64), False))
    small_red = _unpack_flat(g_small, [wt[n].shape for n in REPLICATED] + [(r, c) for _, r, c in SMALL_SHARDED] + [(1,)])
    grads = dict(zip(REPLICATED, small_red[:len(REPLICATED)]))
    for (name, r, c), red in zip(SMALL_SHARDED, small_red[len(REPLICATED):-1]):
        grads[name] = lax.dynamic_slice(red, (0, me * (c // N_DEV)), (r, c // N_DEV)).reshape(wt[name].shape)
    loss_all = small_red[-1][0]
    delta, new_m, new_v = {}, {}, {}
    for key, group in (("in", BIG_IN), ("rest", BIG_REST)):
        grads.update(_unpack_big(g_slabs[key], group))
        for name, _, transposed in group:
            view = (lambda a: a[0].T) if transposed else (lambda a: a[0])
            back = (lambda a: a.T[None]) if transposed else (lambda a: a[None])
            d, mn, vn = _adamw("adamw_" + name, view(wt[name]), view(grads[name]), view(mt[name]), view(vt[name]))
            delta[name], new_m[name], new_v[name] = back(d), back(mn), back(vn)

    rest = list(REPLICATED) + small_names
    shapes = [wt[n].shape for n in rest]
    pack_rest = lambda tree: _pack_flat([tree[n] for n in rest], 24)
    d_rest, m_rest, v_rest = _adamw("adamw_small", pack_rest(wt), pack_rest(grads), pack_rest(mt), pack_rest(vt))
    delta.update(zip(rest, _unpack_flat(d_rest, shapes)))
    new_m.update(zip(rest, _unpack_flat(m_rest, shapes)))
    new_v.update(zip(rest, _unpack_flat(v_rest, shapes)))

    return (loss_all, grad_x[None], *[grads[n] for n in WEIGHTS], *[delta[n] for n in WEIGHTS],
            *[new_m[n] for n in WEIGHTS], *[new_v[n] for n in WEIGHTS])
```

```python
import functools
import math

import jax
import jax.numpy as jnp
from jax import lax
from jax.experimental import pallas as pl
from jax.experimental.pallas import tpu as pltpu

F32 = jnp.float32
BF16 = jnp.bfloat16

D_MODEL = 1024
CHUNK = 128
N_META = 16
PAD_ROWS = CHUNK - N_META
RET_HEADS = 4
RET_QK_DIM = 128
RET_V_DIM = 256
SSD_HEADS = 32
SSD_HEAD_DIM = 64
SSD_GROUPS = 4
SSD_STATE = 128
HEADS_PER_GROUP = SSD_HEADS // SSD_GROUPS
PAIRS_PER_GROUP = HEADS_PER_GROUP // 2
D_FF = 2816
EPS = 1e-6
ROPE_BASE = 10000.0
N_DEV = 8

ADAM_LR = 0.001
ADAM_B1 = 0.9
ADAM_B2 = 0.999
ADAM_EPS = 1e-08
ADAM_WD = 0.01
ADAM_STEP = 10

VMEM_LIMIT = 56 * 1024 * 1024
HALO = 16
HIGHEST = lax.Precision.HIGHEST

SEGMENTS = (("qk", 0, 1024), ("v", 1024, 2048), ("g", 2048, 3072), ("z", 3072, 5120), ("xs", 5120, 7168),
            ("B", 7168, 7680), ("C", 7680, 8192), ("dt", 8192, 8256), ("gates", 8256, 10304))


def _pick(n, cands):
    for c in cands:
        if n % c == 0:
            return c
    raise ValueError(f"no tile for {n}")


def _params(sem):
    return pltpu.CompilerParams(dimension_semantics=sem, vmem_limit_bytes=VMEM_LIMIT)


def _dot(a, b, dims=(((1,), (0,)), ((), ())), precision=None):
    return lax.dot_general(a, b, dims, preferred_element_type=F32, precision=precision)


def _dot_nt(a, b):
    return _dot(a, b, (((1,), (1,)), ((), ())))


def _dot_tn(a, b):
    return _dot(a, b, (((0,), (0,)), ((), ())))


def _mm(name, a, b, mode, add=None, out_dtype=F32):
    if mode == "nn":
        (M, K), N = a.shape, b.shape[1]
    elif mode == "nt":
        (M, K), N = a.shape, b.shape[0]
    else:
        (K, M), N = a.shape, b.shape[1]
    tn = _pick(N, (1408, 1024, 512, 128, 64))
    if mode == "tn":
        tm = M if M <= 1024 else _pick(M, (1408, 1024))
        tk = _pick(K, (1056, 512, 256, 128))
    else:
        tm = _pick(M, (1056, 512, 256, 128))
        tk = K if K <= 2048 else _pick(K, (1408, 1024))
    nk = K // tk
    if mode == "nn":
        a_spec = pl.BlockSpec((tm, tk), lambda n, m, k: (m, k))
        b_spec = pl.BlockSpec((tk, tn), lambda n, m, k: (k, n))
        dims = (((1,), (0,)), ((), ()))
    elif mode == "nt":
        a_spec = pl.BlockSpec((tm, tk), lambda n, m, k: (m, k))
        b_spec = pl.BlockSpec((tn, tk), lambda n, m, k: (n, k))
        dims = (((1,), (1,)), ((), ()))
    else:
        a_spec = pl.BlockSpec((tk, tm), lambda n, m, k: (k, m))
        b_spec = pl.BlockSpec((tk, tn), lambda n, m, k: (k, n))
        dims = (((0,), (0,)), ((), ()))
    o_spec = pl.BlockSpec((tm, tn), lambda n, m, k: (m, n))
    in_specs = [a_spec, b_spec] + ([o_spec] if add is not None else [])
    args = [a, b] + ([add] if add is not None else [])

    def body(*refs):
        if add is not None:
            a_ref, b_ref, r_ref, o_ref, acc = refs
        else:
            a_ref, b_ref, o_ref, acc = refs
        k = pl.program_id(2)
        p = _dot(a_ref[...].astype(BF16), b_ref[...].astype(BF16), dims)

        def finish(r):
            if add is not None:
                r = r + r_ref[...]
            o_ref[...] = r.astype(out_dtype)

        if nk == 1:
            finish(p)
        else:
            @pl.when(k == 0)
            def _():
                acc[...] = p

            @pl.when(k > 0)
            def _():
                acc[...] += p

            @pl.when(k == nk - 1)
            def _():
                finish(acc[...])

    return pl.pallas_call(
        body, name=name, grid=(N // tn, M // tm, nk), in_specs=in_specs, out_specs=o_spec,
        out_shape=jax.ShapeDtypeStruct((M, N), out_dtype),
        scratch_shapes=[pltpu.VMEM((tm, tn) if nk > 1 else (8, 128), F32)],
        compiler_params=_params(("arbitrary", "arbitrary", "arbitrary")),
    )(*args)


def _const(c):
    return lambda j: c


def _rows(name, fn, T, ncol, ins, params, outs, accs=(), halo=False):
    tm = _pick(T, (384, 256, 128))
    R = T // tm
    hb = tm // HALO
    in_specs, args = [], []
    for spec in ins:
        arr, w, cf = spec[:3]
        lead = spec[3] if len(spec) > 3 else None
        if lead is None:
            mk = lambda blk, rf, cf=cf: pl.BlockSpec(blk, lambda j, i: (rf(i), cf(j)))
            shape = lambda r, w=w: (r, w)
        else:
            mk = lambda blk, rf, cf=cf, lead=lead: pl.BlockSpec(blk, lambda j, i: (lead, rf(i), cf(j)))
            shape = lambda r, w=w: (None, r, w)
        in_specs.append(mk(shape(tm), lambda i: i))
        args.append(arr)
        if halo:
            in_specs.append(mk(shape(HALO), lambda i: jnp.maximum(i * hb - 1, 0)))
            in_specs.append(mk(shape(HALO), lambda i: jnp.minimum((i + 1) * hb, T // HALO - 1)))
            args += [arr, arr]
    for arr, w, cf in params:
        in_specs.append(pl.BlockSpec((arr.shape[0], w), lambda j, i, cf=cf: (0, cf(j))))
        args.append(arr)
    out_shape, out_specs = [], []
    for tw, w, cf, dt in outs:
        out_shape.append(jax.ShapeDtypeStruct((T, tw), dt))
        out_specs.append(pl.BlockSpec((tm, w), lambda j, i, cf=cf: (i, cf(j))))
    for r, tw, w, cf in accs:
        out_shape.append(jax.ShapeDtypeStruct((r, tw), F32))
        out_specs.append(pl.BlockSpec((r, w), lambda j, i, cf=cf: (0, cf(j))))
    n_in, n_par, n_out, n_acc = len(ins), len(params), len(outs), len(accs)

    def body(*refs):
        i = pl.program_id(1)
        vals, p = [], 0
        for _ in range(n_in):
            if halo:
                vals.append(jnp.concatenate([refs[p + 1][...], refs[p][...], refs[p + 2][...]], axis=0).astype(F32))
                p += 3
            else:
                vals.append(refs[p][...].astype(F32))
                p += 1
        pvals = [refs[p + k][...] for k in range(n_par)]
        p += n_par
        res = fn(i, *vals, *pvals)
        for k in range(n_out):
            refs[p + k][...] = res[k].astype(refs[p + k].dtype)
        p += n_out
        for k in range(n_acc):
            ref, v = refs[p + k], res[n_out + k]

            @pl.when(i == 0)
            def _(ref=ref, v=v):
                ref[...] = v

            @pl.when(i > 0)
            def _(ref=ref, v=v):
                ref[...] += v

    res = pl.pallas_call(
        body, name=name, grid=(ncol, R), in_specs=in_specs, out_specs=out_specs, out_shape=out_shape,
        compiler_params=_params(("arbitrary", "arbitrary")),
    )(*args)
    return res


def _tile_rows(T):
    return _pick(T, (384, 256, 128))


def _row_ids(i, T, halo=False):
    tm = _tile_rows(T)
    if halo:
        return i * tm - HALO + lax.broadcasted_iota(jnp.int32, (tm + 2 * HALO, 1), 0)
    return i * tm + lax.broadcasted_iota(jnp.int32, (tm, 1), 0)


def _rms(x, w):
    return x * lax.rsqrt(jnp.mean(x * x, axis=-1, keepdims=True) + EPS) * w


def _silu(x):
    return x * jax.nn.sigmoid(x)


def _conv3(x, w):
    n = x.shape[0]
    return w[0:1] * pltpu.roll(x, 1, 0) + w[1:2] * x + w[2:3] * pltpu.roll(x, n - 1, 0)


def _conv3_t(d, w):
    n = d.shape[0]
    return w[0:1] * pltpu.roll(d, n - 1, 0) + w[1:2] * d + w[2:3] * pltpu.roll(d, 1, 0)


def _center(x):
    return x[HALO:x.shape[0] - HALO]


def _retention(name, a, b, v, T, da, dv):
    (a, a0), (b, b0), (v, v0) = [t if isinstance(t, tuple) else (t, 0) for t in (a, b, v)]
    nc = T // CHUNK
    log_gammas = [math.log(1.0 - 2.0 ** (-5.0 - h)) for h in range(RET_HEADS)]

    def body(a_ref, b_ref, v_ref, o_ref, st, st_b):
        h = pl.program_id(0)
        lg = jnp.float32(log_gammas[RET_HEADS - 1])
        for k in range(RET_HEADS - 2, -1, -1):
            lg = jnp.where(h == k, jnp.float32(log_gammas[k]), lg)
        li = lax.broadcasted_iota(jnp.int32, (CHUNK, CHUNK), 0)
        si = lax.broadcasted_iota(jnp.int32, (CHUNK, CHUNK), 1)
        dmat = jnp.exp(lg * jnp.abs(li - si).astype(F32))
        pos = lax.broadcasted_iota(jnp.int32, (CHUNK, 1), 0).astype(F32)
        kdec_f = jnp.exp((CHUNK - 1 - pos) * lg)
        qdec_f = jnp.exp((pos + 1) * lg)
        kdec_b = jnp.exp(pos * lg)
        qdec_b = jnp.exp((CHUNK - pos) * lg)
        cdec = jnp.exp(CHUNK * lg)

        def rows(n):
            return pl.ds(pl.multiple_of(n * CHUNK, CHUNK), CHUNK)

        st[...] = jnp.zeros_like(st)
        st_b[...] = jnp.zeros_like(st_b)
        o_ref[...] = jnp.zeros_like(o_ref)

        def step(m, carry):
            r = rows(m)
            av, bv, vv = a_ref[r, :], b_ref[r, :], v_ref[r, :].astype(BF16)
            s = _dot_nt(av.astype(BF16), bv.astype(BF16)) * dmat
            o_ref[r, :] += _dot(s.astype(BF16), vv) + _dot((av * qdec_f).astype(BF16), st[...].astype(BF16))
            st[...] = cdec * st[...] + _dot_tn((bv * kdec_f).astype(BF16), vv)
            r = rows(nc - 1 - m)
            av, bv, vv = a_ref[r, :], b_ref[r, :], v_ref[r, :].astype(BF16)
            o_ref[r, :] += _dot((av * qdec_b).astype(BF16), st_b[...].astype(BF16))
            st_b[...] = cdec * st_b[...] + _dot_tn((bv * kdec_b).astype(BF16), vv)
            return carry

        lax.fori_loop(0, nc, step, 0, unroll=3 if nc % 3 == 0 else 1)

    return pl.pallas_call(
        body, name=name, grid=(RET_HEADS,),
        in_specs=[pl.BlockSpec((T, da), lambda h: (0, a0 // da + h)), pl.BlockSpec((T, da), lambda h: (0, b0 // da + h)),
                  pl.BlockSpec((T, dv), lambda h: (0, v0 // dv + h))],
        out_specs=pl.BlockSpec((T, dv), lambda h: (0, h)),
        out_shape=jax.ShapeDtypeStruct((T, RET_HEADS * dv), F32),
        scratch_shapes=[pltpu.VMEM((da, dv), F32), pltpu.VMEM((da, dv), F32)],
        compiler_params=_params(("arbitrary",)),
    )(a, b, v)


def _softplus(x):
    return jnp.maximum(x, 0.0) + jnp.log1p(jnp.exp(-jnp.abs(x)))


def _lane_lo():
    return lax.broadcasted_iota(jnp.int32, (1, CHUNK), 1) < SSD_HEAD_DIM


def _pair_cols(col, j):
    return jnp.where(_lane_lo(), col[:, 2 * j:2 * j + 1], col[:, 2 * j + 1:2 * j + 2])


def _pair_rows(colr, j):
    lo = lax.broadcasted_iota(jnp.int32, (CHUNK, 1), 0) < SSD_HEAD_DIM
    return jnp.where(lo, colr[2 * j:2 * j + 1, :], colr[2 * j + 1:2 * j + 2, :])


def _onehot8(h):
    return (lax.broadcasted_iota(jnp.int32, (1, HEADS_PER_GROUP), 1) == h).astype(F32)


def _ssd_pre(d, c, rawc, rawr, bc, br, alc, alr):
    li = lax.broadcasted_iota(jnp.int32, (CHUNK, CHUNK), 0)
    si = lax.broadcasted_iota(jnp.int32, (CHUNK, CHUNK), 1)
    dif = li - si if d == 0 else si - li
    mask = dif >= 0
    mask_t = dif <= 0
    rowc = c * CHUNK + lax.broadcasted_iota(jnp.int32, (CHUNK, 1), 0)
    rowr = c * CHUNK + lax.broadcasted_iota(jnp.int32, (1, CHUNK), 1)
    dtc = jnp.where(rowc >= PAD_ROWS, _softplus(rawc + bc), 0.0)
    dtr = jnp.where(rowr >= PAD_ROWS, _softplus(rawr + br), 0.0)
    ac = -jnp.exp(alc)
    ar = -jnp.exp(alr)
    dlc = dtc * ac
    dlr = dtr * ar
    alpc = _dot(mask.astype(F32), dlc, precision=HIGHEST)
    alpr = _dot(dlr, mask_t.astype(F32), precision=HIGHEST)
    endc = jnp.sum(dlc, axis=0, keepdims=True)
    endr = jnp.sum(dlr, axis=1, keepdims=True)
    return dict(mask=mask, mask_t=mask_t, dtc=dtc, ac=ac, alpc=alpc, alpr=alpr, endc=endc, endr=endr,
                valid=rowc >= PAD_ROWS)


def _chunk_of(d, n, nc):
    return n + d * (nc - 1 - 2 * n)


GROUP_WIDTH = HEADS_PER_GROUP * SSD_HEAD_DIM


def _ssd_in_specs(d, cfn):
    return [
        pl.BlockSpec((CHUNK, GROUP_WIDTH), lambda g, n: (cfn(d, n), g)),
        pl.BlockSpec((CHUNK, SSD_STATE), lambda g, n: (cfn(d, n), g)),
        pl.BlockSpec((CHUNK, SSD_STATE), lambda g, n: (cfn(d, n), g)),
        pl.BlockSpec((None, None, CHUNK, HEADS_PER_GROUP), lambda g, n: (d, g, cfn(d, n), 0)),
        pl.BlockSpec((None, None, HEADS_PER_GROUP, CHUNK), lambda g, n: (d, g, 0, cfn(d, n))),
        pl.BlockSpec((None, None, 1, HEADS_PER_GROUP), lambda g, n: (d, g, 0, 0)),
        pl.BlockSpec((None, None, HEADS_PER_GROUP, 1), lambda g, n: (d, g, 0, 0)),
        pl.BlockSpec((None, None, 1, HEADS_PER_GROUP), lambda g, n: (d, g, 0, 0)),
        pl.BlockSpec((None, None, HEADS_PER_GROUP, 1), lambda g, n: (d, g, 0, 0)),
    ]


N_SSD_IN = 9


def _ssd_fwd(xs, bm, cm, small, T):
    nc = T // CHUNK
    cfn = lambda d, n: _chunk_of(d, n, nc)

    def one_direction(d, n, ins, y_ref, hs_ref, h_scr):
        x_ref, b_ref, c_ref, rawc_ref, rawr_ref, bc_ref, br_ref, alc_ref, alr_ref = ins
        c = cfn(d, n)
        q = _ssd_pre(d, c, rawc_ref[...], rawr_ref[...], bc_ref[...], br_ref[...], alc_ref[...], alr_ref[...])
        bv = b_ref[...].astype(BF16)
        cv = c_ref[...].astype(BF16)
        cb = _dot_nt(cv, bv)
        lo = _lane_lo()
        for j in range(PAIRS_PER_GROUP):
            xp = x_ref[:, j * CHUNK:(j + 1) * CHUNK]
            xd = xp * _pair_cols(q["dtc"], j)
            xdb = xd.astype(BF16)
            yi = []
            for e in range(2):
                h = 2 * j + e
                lm = jnp.exp(jnp.where(q["mask"], q["alpc"][:, h:h + 1] - q["alpr"][h:h + 1, :], -jnp.inf))
                yi.append(_dot((cb * lm).astype(BF16), xdb))
            alp = _pair_cols(q["alpc"], j)
            hp = h_scr[j]
            hs_ref[j] = hp
            yo = jnp.exp(alp) * _dot_nt(cv, hp.astype(BF16))
            y_ref[:, j * CHUNK:(j + 1) * CHUNK] = (jnp.where(lo, yi[0], yi[1]) + yo).astype(y_ref.dtype)
            de = jnp.exp(_pair_cols(q["endc"], j) - alp)
            h_scr[j] = jnp.exp(_pair_rows(q["endr"], j)) * hp + _dot_tn((xd * de).astype(BF16), bv)

    def body(*refs):
        n = pl.program_id(1)
        ins, (y_f, y_b, hs_f, hs_b, h_scr) = refs[:2 * N_SSD_IN], refs[2 * N_SSD_IN:]

        @pl.when(n == 0)
        def _():
            h_scr[...] = jnp.zeros_like(h_scr)

        one_direction(0, n, ins[:N_SSD_IN], y_f, hs_f, h_scr.at[0])
        one_direction(1, n, ins[N_SSD_IN:], y_b, hs_b, h_scr.at[1])

    y_spec = lambda d: pl.BlockSpec((CHUNK, GROUP_WIDTH), lambda g, n: (cfn(d, n), g))
    hs_spec = lambda d: pl.BlockSpec((None, None, PAIRS_PER_GROUP, CHUNK, SSD_STATE),
                                     lambda g, n: (g, cfn(d, n), 0, 0, 0))
    y_shape = jax.ShapeDtypeStruct((T, SSD_HEADS * SSD_HEAD_DIM), BF16)
    hs_shape = jax.ShapeDtypeStruct((SSD_GROUPS, nc, PAIRS_PER_GROUP, CHUNK, SSD_STATE), F32)
    y_f, y_b, hs_f, hs_b = pl.pallas_call(
        body, name="ssd_fwd", grid=(SSD_GROUPS, nc),
        in_specs=_ssd_in_specs(0, cfn) + _ssd_in_specs(1, cfn),
        out_specs=[y_spec(0), y_spec(1), hs_spec(0), hs_spec(1)],
        out_shape=[y_shape, y_shape, hs_shape, hs_shape],
        scratch_shapes=[pltpu.VMEM((2, PAIRS_PER_GROUP, CHUNK, SSD_STATE), F32)],
        compiler_params=_params(("arbitrary", "arbitrary")),
    )(xs, bm, cm, *small, xs, bm, cm, *small)
    return (y_f, y_b), (hs_f, hs_b)


def _ssd_bwd(xs, bm, cm, small, hs, dy, T):
    nc = T // CHUNK
    cfn = lambda d, n: _chunk_of(1 - d, n, nc)

    def one_direction(d, n, ins, outs, dh_scr):
        x_ref, b_ref, c_ref, rawc_ref, rawr_ref, bc_ref, br_ref, alc_ref, alr_ref, hs_ref, dy_ref = ins
        dx_ref, db_ref, dc_ref, draw_ref, dbias_ref, dalog_ref = outs
        c = cfn(d, n)
        rawc, bc = rawc_ref[...], bc_ref[...]
        q = _ssd_pre(d, c, rawc, rawr_ref[...], bc, br_ref[...], alc_ref[...], alr_ref[...])
        b32, c32 = b_ref[...], c_ref[...]
        bv, cv = b32.astype(BF16), c32.astype(BF16)
        cb = _dot_nt(cv, bv)
        cbt = _dot_nt(bv, cv)
        lo = _lane_lo()
        row_lo = lax.broadcasted_iota(jnp.int32, (CHUNK, 1), 0) < SSD_HEAD_DIM
        dcb = jnp.zeros((CHUNK, CHUNK), F32)
        dcp = jnp.zeros((CHUNK, SSD_STATE), F32)
        dbp = jnp.zeros((CHUNK, SSD_STATE), F32)
        dalp = jnp.zeros((CHUNK, HEADS_PER_GROUP), F32)
        dend = jnp.zeros((1, HEADS_PER_GROUP), F32)
        ddtx = jnp.zeros((CHUNK, HEADS_PER_GROUP), F32)

        def half_sums(t):
            return (jnp.sum(jnp.where(lo, t, 0.0), axis=1, keepdims=True),
                    jnp.sum(jnp.where(lo, 0.0, t), axis=1, keepdims=True))

        for j in range(PAIRS_PER_GROUP):
            xp = x_ref[:, j * CHUNK:(j + 1) * CHUNK]
            dtp = _pair_cols(q["dtc"], j)
            xd = xp * dtp
            xdb = xd.astype(BF16)
            dyp = dy_ref[:, j * CHUNK:(j + 1) * CHUNK]
            dyb = dyp.astype(BF16)
            hn = hs_ref[j]
            hnb = hn.astype(BF16)
            dh1 = dh_scr[j]
            dh1b = dh1.astype(BF16)
            alp = _pair_cols(q["alpc"], j)
            ea = jnp.exp(alp)
            de = jnp.exp(_pair_cols(q["endc"], j) - alp)
            dxi = []
            for e in range(2):
                h = 2 * j + e
                diff = q["alpc"][:, h:h + 1] - q["alpr"][h:h + 1, :]
                lm = jnp.exp(jnp.where(q["mask"], diff, -jnp.inf))
                mt = cbt * jnp.exp(jnp.where(q["mask_t"], -diff, -jnp.inf))
                dxi.append(_dot(mt.astype(BF16), dyb))
                dyeb_h = (jnp.where(lo, dyp, 0.0) if e == 0 else jnp.where(lo, 0.0, dyp)).astype(BF16)
                gl = _dot_nt(dyeb_h, xdb) * lm
                dcb = dcb + gl
                ra = jnp.sum(gl * cb - _dot_nt(xdb, dyeb_h) * mt, axis=1, keepdims=True)
                dalp = dalp + ra * _onehot8(h)
            y_off = ea * _dot_nt(cv, hnb)
            dxs_state = de * _dot_nt(bv, dh1b)
            dxd = jnp.where(lo, dxi[0], dxi[1]) + dxs_state
            dyeb = (dyp * ea).astype(BF16)
            dcp = dcp + _dot(dyeb, hnb)
            dbp = dbp + _dot((xd * de).astype(BF16), dh1b)
            dh_scr[j] = jnp.exp(_pair_rows(q["endr"], j)) * dh1 + _dot_tn(dyeb, cv)
            r0, r1 = half_sums(dyp * y_off - xd * dxs_state)
            dalp = dalp + r0 * _onehot8(2 * j) + r1 * _onehot8(2 * j + 1)
            t0, t1 = half_sums(jnp.sum(xd * dxs_state, axis=0, keepdims=True))
            u = dh1 * hn
            u0 = jnp.sum(jnp.sum(jnp.where(row_lo, u, 0.0), axis=0, keepdims=True), axis=1, keepdims=True)
            u1 = jnp.sum(jnp.sum(jnp.where(row_lo, 0.0, u), axis=0, keepdims=True), axis=1, keepdims=True)
            eend = jnp.exp(q["endc"])
            dend = dend + (t0 + eend * u0) * _onehot8(2 * j) + (t1 + eend * u1) * _onehot8(2 * j + 1)
            dx_ref[:, j * CHUNK:(j + 1) * CHUNK] = (dxd * dtp).astype(dx_ref.dtype)
            w0, w1 = half_sums(dxd * xp)
            ddtx = ddtx + w0 * _onehot8(2 * j) + w1 * _onehot8(2 * j + 1)

        dcbb = dcb.astype(BF16)
        dc_ref[...] = (dcp + _dot(dcbb, bv)).astype(dc_ref.dtype)
        db_ref[...] = (dbp + _dot_tn(dcbb, cv)).astype(db_ref.dtype)
        ddl = _dot(q["mask_t"].astype(F32), dalp, precision=HIGHEST) + dend
        ddt = ddl * q["ac"] + ddtx
        draw = jnp.where(q["valid"], ddt * jax.nn.sigmoid(rawc + bc), 0.0)
        draw_ref[...] = draw
        dbias = jnp.sum(draw, axis=0, keepdims=True)
        dalog = jnp.sum(ddl * q["dtc"], axis=0, keepdims=True) * q["ac"]

        @pl.when(n == 0)
        def _():
            dbias_ref[...] = dbias
            dalog_ref[...] = dalog

        @pl.when(n > 0)
        def _():
            dbias_ref[...] += dbias
            dalog_ref[...] += dalog

    n_in, n_out = N_SSD_IN + 2, 6

    def body(*refs):
        n = pl.program_id(1)
        ins, outs, dh_scr = refs[:2 * n_in], refs[2 * n_in:2 * (n_in + n_out)], refs[-1]

        @pl.when(n == 0)
        def _():
            dh_scr[...] = jnp.zeros_like(dh_scr)

        one_direction(0, n, ins[:n_in], outs[:n_out], dh_scr.at[0])
        one_direction(1, n, ins[n_in:], outs[n_out:], dh_scr.at[1])

    def in_specs(d):
        return _ssd_in_specs(d, cfn) + [
            pl.BlockSpec((None, None, PAIRS_PER_GROUP, CHUNK, SSD_STATE), lambda g, n: (g, cfn(d, n), 0, 0, 0)),
            pl.BlockSpec((CHUNK, GROUP_WIDTH), lambda g, n: (cfn(d, n), g))]

    def out_specs(d):
        acc = pl.BlockSpec((None, 1, HEADS_PER_GROUP), lambda g, n: (g, 0, 0))
        return [pl.BlockSpec((CHUNK, GROUP_WIDTH), lambda g, n: (cfn(d, n), g)),
                pl.BlockSpec((CHUNK, SSD_STATE), lambda g, n: (cfn(d, n), g)),
                pl.BlockSpec((CHUNK, SSD_STATE), lambda g, n: (cfn(d, n), g)),
                pl.BlockSpec((None, CHUNK, HEADS_PER_GROUP), lambda g, n: (g, cfn(d, n), 0)), acc, acc]

    out_shape = [jax.ShapeDtypeStruct((T, SSD_HEADS * SSD_HEAD_DIM), BF16),
                 jax.ShapeDtypeStruct((T, SSD_GROUPS * SSD_STATE), BF16),
                 jax.ShapeDtypeStruct((T, SSD_GROUPS * SSD_STATE), BF16),
                 jax.ShapeDtypeStruct((SSD_GROUPS, T, HEADS_PER_GROUP), F32),
                 jax.ShapeDtypeStruct((SSD_GROUPS, 1, HEADS_PER_GROUP), F32),
                 jax.ShapeDtypeStruct((SSD_GROUPS, 1, HEADS_PER_GROUP), F32)]
    res = pl.pallas_call(
        body, name="ssd_bwd", grid=(SSD_GROUPS, nc),
        in_specs=in_specs(0) + in_specs(1), out_specs=out_specs(0) + out_specs(1), out_shape=out_shape * 2,
        scratch_shapes=[pltpu.VMEM((2, PAIRS_PER_GROUP, CHUNK, SSD_STATE), F32)],
        compiler_params=_params(("arbitrary", "arbitrary")),
    )(xs, bm, cm, *small, hs[0], dy, xs, bm, cm, *small, hs[1], dy)
    return [(res[k], res[n_out + k]) for k in range(n_out)]


def _rot(x, cs, sn):
    return x * cs + pltpu.roll(x, RET_QK_DIM // 2, 1) * sn


def _rot_t(d, cs, sn):
    return d * cs + pltpu.roll(d * sn, RET_QK_DIM // 2, 1)


def _ret_post(y, g, w):
    parts = []
    for h in range(RET_HEADS):
        yh = y[:, h * RET_V_DIM:(h + 1) * RET_V_DIM]
        mu = jnp.mean(yh, axis=-1, keepdims=True)
        var = jnp.mean(jnp.square(yh - mu), axis=-1, keepdims=True)
        parts.append((yh - mu) * lax.rsqrt(var + EPS))
    return _silu(g) * (jnp.concatenate(parts, axis=1) * w)


def _ssd_post(yf, yb, xs, z, dskip, w):
    y = (yf + yb + xs * dskip) * _silu(z)
    return y * lax.rsqrt(jnp.mean(y * y, axis=-1, keepdims=True) + EPS) * w


def _merge(gates, yr, ys, valid):
    m = jax.nn.sigmoid(gates[:, :D_MODEL]) * yr + jax.nn.sigmoid(gates[:, D_MODEL:]) * ys
    return jnp.where(valid, m, 0.0)


def _rope_tables(T):
    half = RET_QK_DIM // 2
    inv = ROPE_BASE ** (-jnp.arange(half, dtype=F32) / half)
    pos = (jnp.arange(T) - PAD_ROWS).astype(F32)
    ang = pos[:, None] * inv[None, :]
    cos, sin = jnp.cos(ang), jnp.sin(ang)
    return jnp.concatenate([cos, cos], axis=1), jnp.concatenate([-sin, sin], axis=1)


def _per_group(v):
    c = v.reshape(SSD_GROUPS, 1, HEADS_PER_GROUP)
    return c, c.reshape(SSD_GROUPS, HEADS_PER_GROUP, 1)


def _local_step(x, target, w, tick, late_weights, early_grads, in_grads):
    S = x.shape[0]
    T = S + CHUNK
    tm = _tile_rows(T)
    c0 = _const(0)

    h0 = jnp.concatenate([jnp.zeros((PAD_ROWS, D_MODEL), F32), w["meta_tokens"], x], axis=0)
    tgt = jnp.concatenate([jnp.zeros((CHUNK, D_MODEL), F32), target], axis=0)
    seg_at = {name: a for name, a, _ in SEGMENTS}
    w_main = w["w_in_t"][:seg_at["dt"]]
    w_dt = jnp.pad(w["w_in_t"][seg_at["dt"]:seg_at["gates"]], ((0, CHUNK - 2 * SSD_HEADS), (0, 0)))
    w_gates = w["w_in_t"][seg_at["gates"]:]

    def norm_cast(name, h, nw):
        return _rows(name, lambda i, hv, wv: (_rms(hv, wv),), T, 1, [(h, D_MODEL, c0)], [(nw, D_MODEL, c0)],
                     [(D_MODEL, D_MODEL, c0, BF16)])[0]

    u = norm_cast("norm_mix", h0, w["norm_mix_w"] + tick)
    p_main = _mm("proj_main", u, w_main, "nt", out_dtype=BF16)
    p_dt = _mm("proj_dt", u, w_dt, "nt")
    p_gates = _mm("proj_gates", u, w_gates, "nt", out_dtype=BF16)

    def seg(name, width, cf=c0):
        base = seg_at[name] // width
        return (p_main, width, lambda j: base + cf(j))

    cs, sn = _rope_tables(T)
    scale = RET_QK_DIM ** -0.5

    def rot_fn(i, qk, csv, snv):
        q = [_rot(qk[:, h * 128:(h + 1) * 128], csv, snv) for h in range(RET_HEADS)]
        k = [_rot(qk[:, (RET_HEADS + h) * 128:(RET_HEADS + h + 1) * 128], csv, snv) * scale for h in range(RET_HEADS)]
        return jnp.concatenate(q, axis=1), jnp.concatenate(k, axis=1)

    qr, kr = _rows("rotary", rot_fn, T, 1, [seg("qk", 1024), (cs, 128, c0), (sn, 128, c0)], [],
                   [(512, 512, c0, F32), (512, 512, c0, F32)])
    v_at = (p_main, seg_at["v"])
    y_ret = _retention("retention", qr, kr, v_at, T, RET_QK_DIM, RET_V_DIM)
    a_ret = _rows("ret_post", lambda i, y, g, gw: (_ret_post(y, g, gw),), T, 1,
                  [(y_ret, 1024, c0), seg("g", 1024)], [(w["ret_gn_w"], 1024, c0)],
                  [(1024, 1024, c0, BF16)])[0]

    conv_w = {"xs": w["w_ssd_conv"][:, :2048], "B": w["w_ssd_conv"][:, 2048:2560], "C": w["w_ssd_conv"][:, 2560:]}
    conv_b = {"xs": w["b_ssd_conv"][:, :2048], "B": w["b_ssd_conv"][:, 2048:2560], "C": w["b_ssd_conv"][:, 2560:]}

    def ssd_conv_fn(i, xe, cw, cb):
        r = _row_ids(i, T, True)
        xe = jnp.where((r >= 0) & (r < T), xe, 0.0)
        return (_center(jnp.where(r >= PAD_ROWS, _silu(_conv3(xe, cw) + cb), 0.0)),)

    act = {}
    for name in ("xs", "B", "C"):
        wd = conv_w[name].shape[1]
        cw = 512
        act[name] = _rows("ssd_conv_" + name, ssd_conv_fn, T, wd // cw, [seg(name, cw, lambda j: j)],
                          [(conv_w[name], cw, lambda j: j), (conv_b[name], cw, lambda j: j)],
                          [(wd, cw, lambda j: j, BF16)], halo=True)[0]

    raw = p_dt[:, :2 * SSD_HEADS].reshape(T, 2, SSD_GROUPS, HEADS_PER_GROUP)
    rawc = raw.transpose(1, 2, 0, 3)
    rawr = raw.transpose(1, 2, 3, 0)
    bias = [_per_group(w["dt_bias_f"]), _per_group(w["dt_bias_b"])]
    alog = [_per_group(w["a_log_f"]), _per_group(w["a_log_b"])]
    small = (rawc, rawr, jnp.stack([bias[0][0], bias[1][0]]), jnp.stack([bias[0][1], bias[1][1]]),
             jnp.stack([alog[0][0], alog[1][0]]), jnp.stack([alog[0][1], alog[1][1]]))
    y_dir, states = _ssd_fwd(act["xs"], act["B"], act["C"], small, T)

    dskip_e = jnp.repeat(w["d_skip"], SSD_HEAD_DIM, axis=1)
    gcol = lambda j: j
    gw_ = 512
    a_ssd = _rows("ssd_post", lambda i, yf, yb, xv, zv, dk, nw: (_ssd_post(yf, yb, xv, zv, dk, nw),), T, SSD_GROUPS,
                  [(y_dir[0], gw_, gcol), (y_dir[1], gw_, gcol), (act["xs"], gw_, gcol), seg("z", gw_, gcol)],
                  [(dskip_e, gw_, gcol), (w["ssd_norm_w"], gw_, gcol)], [(2048, gw_, gcol, BF16)])[0]

    w = dict(w, **late_weights(a_ssd))
    w_up_g, w_up_u = w["w_ffn_up_t"][:D_FF], w["w_ffn_up_t"][D_FF:]
    y_ret_o = _mm("ret_out", a_ret, w["w_ret_out"], "nn", out_dtype=BF16)
    y_ssd_o = _mm("ssd_out", a_ssd, w["w_ssd_out"], "nn", out_dtype=BF16)

    def merge_fn(i, gates, yr, ys):
        return (_merge(gates, yr, ys, _row_ids(i, T) >= PAD_ROWS),)

    merged = _rows("merge", merge_fn, T, 1, [(p_gates, 2048, c0), (y_ret_o, 1024, c0), (y_ssd_o, 1024, c0)], [],
                   [(1024, 1024, c0, BF16)])[0]
    h1 = _mm("mix_out", merged, w["w_out"], "nn", add=h0)

    n2 = norm_cast("norm_ffn", h1, w["norm_ffn_w"])
    fg_pre = _mm("ffn_up_g", n2, w_up_g, "nt", out_dtype=BF16)
    fu_pre = _mm("ffn_up_u", n2, w_up_u, "nt", out_dtype=BF16)
    cwg, cwu = w["w_ffn_conv"][:, :D_FF], w["w_ffn_conv"][:, D_FF:]
    cbg, cbu = w["b_ffn_conv"][:, :D_FF], w["b_ffn_conv"][:, D_FF:]
    fcol = lambda j: j
    fw = 1408

    def ffn_act_fn(i, ge, ue, wg, wu, bg, bu):
        return (_center(_silu(_conv3(ge, wg) + bg) * (_conv3(ue, wu) + bu)),)

    def ext_valid(i):
        r = _row_ids(i, T, True)
        return (r >= 0) & (r < T)

    def ffn_act_masked(i, ge, ue, wg, wu, bg, bu):
        v = ext_valid(i)
        return ffn_act_fn(i, jnp.where(v, ge, 0.0), jnp.where(v, ue, 0.0), wg, wu, bg, bu)

    a2 = _rows("ffn_act", ffn_act_masked, T, D_FF // fw, [(fg_pre, fw, fcol), (fu_pre, fw, fcol)],
               [(cwg, fw, fcol), (cwu, fw, fcol), (cbg, fw, fcol), (cbu, fw, fcol)], [(D_FF, fw, fcol, BF16)],
               halo=True)[0]
    h2 = _mm("ffn_down", a2, w["w_ffn_down"], "nn", add=h1)

    fnw = w["final_norm_w"].reshape(1, D_MODEL)

    def loss_fn(i, hv, tv, nw):
        valid = _row_ids(i, T) >= CHUNK
        y, vjp = jax.vjp(_rms, hv, nw)
        diff = jnp.where(valid, y - tv, 0.0)
        dh, dw = vjp(diff * (1.0 / D_MODEL))
        part = 0.5 / D_MODEL * jnp.sum(jnp.sum(diff * diff, axis=1, keepdims=True), axis=0, keepdims=True)
        return dh, jnp.broadcast_to(part, (1, 128)), dw

    dh2, loss_acc, d_fnw = _rows("loss", loss_fn, T, 1, [(h2, D_MODEL, c0), (tgt, D_MODEL, c0)], [(fnw, D_MODEL, c0)],
                                 [(D_MODEL, D_MODEL, c0, F32)], [(1, 128, 128, c0), (1, D_MODEL, D_MODEL, c0)])
    loss = loss_acc[0, 0]
    grads = {"final_norm_w": d_fnw.reshape(D_MODEL)}

    da2 = _mm("d_ffn_act", dh2, w["w_ffn_down"], "nt", out_dtype=BF16)
    grads["w_ffn_down"] = _mm("g_ffn_down", a2, dh2, "tn", out_dtype=BF16)

    def ffn_bwd_fn(i, ge, ue, de, wg, wu, bg, bu):
        v = ext_valid(i)
        ge, ue, de = jnp.where(v, ge, 0.0), jnp.where(v, ue, 0.0), jnp.where(v, de, 0.0)
        fg = _conv3(ge, wg) + bg
        fu = _conv3(ue, wu) + bu
        sg = jax.nn.sigmoid(fg)
        dfg = de * fu * (sg * (1.0 + fg * (1.0 - sg)))
        dfu = de * (fg * sg)
        n = ge.shape[0]

        def wgrad(df, xe):
            df_c = _center(df)
            return jnp.concatenate([jnp.sum(df_c * _center(pltpu.roll(xe, 1, 0)), axis=0, keepdims=True),
                                    jnp.sum(df_c * _center(xe), axis=0, keepdims=True),
                                    jnp.sum(df_c * _center(pltpu.roll(xe, n - 1, 0)), axis=0, keepdims=True)], axis=0)

        return (_center(_conv3_t(dfg, wg)), _center(_conv3_t(dfu, wu)), wgrad(dfg, ge), wgrad(dfu, ue),
                jnp.sum(_center(dfg), axis=0, keepdims=True), jnp.sum(_center(dfu), axis=0, keepdims=True))

    dfg_pre, dfu_pre, g_cwg, g_cwu, g_cbg, g_cbu = _rows(
        "ffn_act_bwd", ffn_bwd_fn, T, D_FF // fw, [(fg_pre, fw, fcol), (fu_pre, fw, fcol), (da2, fw, fcol)],
        [(cwg, fw, fcol), (cwu, fw, fcol), (cbg, fw, fcol), (cbu, fw, fcol)],
        [(D_FF, fw, fcol, BF16), (D_FF, fw, fcol, BF16)],
        [(3, D_FF, fw, fcol), (3, D_FF, fw, fcol), (1, D_FF, fw, fcol), (1, D_FF, fw, fcol)], halo=True)
    grads["w_ffn_conv"] = jnp.concatenate([g_cwg, g_cwu], axis=1)
    grads["b_ffn_conv"] = jnp.concatenate([g_cbg, g_cbu], axis=1)
    dn2 = _mm("d_norm_ffn_g", dfg_pre, w_up_g, "nn")
    dn2 = _mm("d_norm_ffn_u", dfu_pre, w_up_u, "nn", add=dn2)
    grads["w_ffn_up_t"] = jnp.concatenate([_mm("g_ffn_up_g", dfg_pre, n2, "tn", out_dtype=BF16), _mm("g_ffn_up_u", dfu_pre, n2, "tn", out_dtype=BF16)],
                                          axis=0)

    def norm_bwd(name, h, nw, dn, dres):
        def fn(i, hv, dnv, drv, wv):
            _, vjp = jax.vjp(_rms, hv, wv)
            dh, dw = vjp(dnv)
            return dh + drv, dw
        return _rows(name, fn, T, 1, [(h, D_MODEL, c0), (dn, D_MODEL, c0), (dres, D_MODEL, c0)], [(nw, D_MODEL, c0)],
                     [(D_MODEL, D_MODEL, c0, F32)], [(1, D_MODEL, D_MODEL, c0)])

    dh1, grads["norm_ffn_w"] = norm_bwd("norm_ffn_bwd", h1, w["norm_ffn_w"], dn2, dh2)

    dmerged = _mm("d_merged", dh1, w["w_out"], "nt", out_dtype=BF16)
    grads["w_out"] = _mm("g_out", merged, dh1, "tn", out_dtype=BF16)

    def merge_bwd_fn(i, gates, yr, ys, dm):
        valid = _row_ids(i, T) >= PAD_ROWS
        _, vjp = jax.vjp(lambda a, b, c: _merge(a, b, c, valid), gates, yr, ys)
        return vjp(dm)

    dgates, dyr, dys = _rows("merge_bwd", merge_bwd_fn, T, 1,
                             [(p_gates, 2048, c0), (y_ret_o, 1024, c0), (y_ssd_o, 1024, c0), (dmerged, 1024, c0)],
                             [], [(2048, 2048, c0, BF16), (1024, 1024, c0, BF16), (1024, 1024, c0, BF16)])
    dproj = {"gates": dgates}

    da_ssd = _mm("d_ssd_act", dys, w["w_ssd_out"], "nt", out_dtype=BF16)
    grads["w_ssd_out"] = _mm("g_ssd_out", a_ssd, dys, "tn", out_dtype=BF16)

    def ssd_post_bwd_fn(i, yf, yb, xv, zv, da, dk, nw):
        _, vjp = jax.vjp(_ssd_post, yf, yb, xv, zv, dk, nw)
        dyf, _, dxv, dzv, ddk, dnw = vjp(da)
        return dyf, dxv, dzv, ddk, dnw

    dy_ssd, dxs_skip, dproj["z"], g_dskip_e, grads["ssd_norm_w"] = _rows(
        "ssd_post_bwd", ssd_post_bwd_fn, T, SSD_GROUPS,
        [(y_dir[0], gw_, gcol), (y_dir[1], gw_, gcol), (act["xs"], gw_, gcol), seg("z", gw_, gcol),
         (da_ssd, gw_, gcol)],
        [(dskip_e, gw_, gcol), (w["ssd_norm_w"], gw_, gcol)],
        [(2048, gw_, gcol, BF16), (2048, gw_, gcol, BF16), (2048, gw_, gcol, BF16)],
        [(1, 2048, gw_, gcol), (1, 2048, gw_, gcol)])
    grads["d_skip"] = g_dskip_e.reshape(SSD_HEADS, SSD_HEAD_DIM).sum(axis=1).reshape(1, SSD_HEADS)

    dxs_dir, db_dir, dc_dir, draw, g_bias, g_alog = _ssd_bwd(act["xs"], act["B"], act["C"], small, states, dy_ssd, T)
    grads["dt_bias_f"], grads["dt_bias_b"] = g_bias[0].reshape(1, SSD_HEADS), g_bias[1].reshape(1, SSD_HEADS)
    grads["a_log_f"], grads["a_log_b"] = g_alog[0].reshape(1, SSD_HEADS), g_alog[1].reshape(1, SSD_HEADS)
    d_dt = jnp.stack(draw).transpose(2, 0, 1, 3).reshape(T, 2 * SSD_HEADS)
    dproj["dt"] = jnp.pad(d_dt, ((0, 0), (0, CHUNK - 2 * SSD_HEADS))).astype(BF16)

    def make_conv_bwd(nsum):
        def fn(i, xe, *rest):
            ds, (cw, cb) = rest[:nsum], rest[nsum:]
            r = _row_ids(i, T, True)
            dact = ds[0]
            for t in ds[1:]:
                dact = dact + t
            dact = jnp.where((r >= PAD_ROWS) & (r < T), dact, 0.0)
            xe = jnp.where((r >= 0) & (r < T), xe, 0.0)
            pre = _conv3(xe, cw) + cb
            sg = jax.nn.sigmoid(pre)
            dpre = dact * (sg * (1.0 + pre * (1.0 - sg)))
            n = xe.shape[0]
            dpc = _center(dpre)
            dw = jnp.concatenate([jnp.sum(dpc * _center(pltpu.roll(xe, 1, 0)), axis=0, keepdims=True),
                                  jnp.sum(dpc * _center(xe), axis=0, keepdims=True),
                                  jnp.sum(dpc * _center(pltpu.roll(xe, n - 1, 0)), axis=0, keepdims=True)], axis=0)
            return _center(_conv3_t(dpre, cw)), dw, jnp.sum(dpc, axis=0, keepdims=True)
        return fn

    g_cw, g_cb = {}, {}
    cots = {"xs": [(dxs_dir[0], 512, gcol), (dxs_dir[1], 512, gcol), (dxs_skip, 512, gcol)],
            "B": [(db_dir[0], 512, gcol), (db_dir[1], 512, gcol)],
            "C": [(dc_dir[0], 512, gcol), (dc_dir[1], 512, gcol)]}
    for name in ("xs", "B", "C"):
        wd = conv_w[name].shape[1]
        dproj[name], g_cw[name], g_cb[name] = _rows(
            "ssd_conv_bwd_" + name, make_conv_bwd(len(cots[name])), T, wd // 512,
            [seg(name, 512, gcol)] + cots[name], [(conv_w[name], 512, gcol), (conv_b[name], 512, gcol)],
            [(wd, 512, gcol, BF16)], [(3, wd, 512, gcol), (1, wd, 512, gcol)], halo=True)
    grads["w_ssd_conv"] = jnp.concatenate([g_cw["xs"], g_cw["B"], g_cw["C"]], axis=1)
    grads["b_ssd_conv"] = jnp.concatenate([g_cb["xs"], g_cb["B"], g_cb["C"]], axis=1)

    da_ret = _mm("d_ret_act", dyr, w["w_ret_out"], "nt", out_dtype=BF16)
    grads["w_ret_out"] = _mm("g_ret_out", a_ret, dyr, "tn", out_dtype=BF16)
    tick = early_grads({n: grads.pop(n) for n in ("w_ffn_up_t", "w_ret_out", "w_ssd_out", "w_out", "w_ffn_down")})

    def ret_post_bwd_fn(i, y, g, da, gw):
        _, vjp = jax.vjp(_ret_post, y, g, gw)
        return vjp(da)

    dy_ret, dproj["g"], grads["ret_gn_w"] = _rows(
        "ret_post_bwd", ret_post_bwd_fn, T, 1, [(y_ret, 1024, c0), seg("g", 1024), (da_ret, 1024, c0)],
        [(w["ret_gn_w"] + tick, 1024, c0)], [(1024, 1024, c0, BF16), (1024, 1024, c0, BF16)], [(1, 1024, 1024, c0)])
    dproj["v"] = _retention("retention_dv", kr, qr, dy_ret, T, RET_QK_DIM, RET_V_DIM)
    dqr = _retention("retention_dq", dy_ret, v_at, kr, T, RET_V_DIM, RET_QK_DIM)
    dkr = _retention("retention_dk", v_at, dy_ret, qr, T, RET_V_DIM, RET_QK_DIM)

    def rot_bwd_fn(i, dq, dk, csv, snv):
        parts = [_rot_t(dq[:, h * 128:(h + 1) * 128], csv, snv) for h in range(RET_HEADS)]
        parts += [_rot_t(dk[:, h * 128:(h + 1) * 128] * scale, csv, snv) for h in range(RET_HEADS)]
        return (jnp.concatenate(parts, axis=1),)

    dproj["qk"] = _rows("rotary_bwd", rot_bwd_fn, T, 1, [(dqr, 512, c0), (dkr, 512, c0), (cs, 128, c0), (sn, 128, c0)],
                        [], [(1024, 1024, c0, BF16)])[0]

    d_main = jnp.concatenate([dproj[name].astype(BF16) for name, _, _ in SEGMENTS[:7]], axis=1)
    g_in = [_mm("g_in_main", d_main, u, "tn", out_dtype=BF16),
            _mm("g_in_dt", dproj["dt"], u, "tn", out_dtype=BF16)[:2 * SSD_HEADS],
            _mm("g_in_gates", dproj["gates"], u, "tn", out_dtype=BF16)]
    tick = in_grads(jnp.concatenate(g_in, axis=0))
    du = _mm("d_u_dt", dproj["dt"] + tick.astype(BF16), w_dt, "nn")
    du = _mm("d_u_main", d_main, w_main, "nn", add=du)
    du = _mm("d_u_gates", dproj["gates"], w_gates, "nn", add=du)
    dh0, grads["norm_mix_w"] = norm_bwd("norm_mix_bwd", h0, w["norm_mix_w"], du, dh1)
    grads["meta_tokens"] = dh0[PAD_ROWS:CHUNK]
    return loss, dh0[CHUNK:], grads


MESH_ID = pl.DeviceIdType.MESH
ANY = pl.BlockSpec(memory_space=pl.ANY)


def _me_and_peers():
    x, y, c = lax.axis_index("x"), lax.axis_index("y"), lax.axis_index("c")
    peers = []
    for k in range(1, N_DEV):
        px = 1 - x if k & 4 else x
        py = 1 - y if k & 2 else y
        pc = 1 - c if k & 1 else c
        peers.append(((px, py, pc), 4 * px + 2 * py + pc))
    return 4 * x + 2 * y + c, peers


def _push_blocks(name, src, per_peer):
    blk = src.shape[1:] if per_peer else src.shape

    def body(src_ref, out_ref, send_sems, recv_sems, local_sem):
        me, peers = _me_and_peers()
        mine = src_ref.at[me] if per_peer else src_ref
        local = pltpu.make_async_copy(mine, out_ref.at[me], local_sem)
        local.start()
        sends = []
        for k, (dev, idx) in enumerate(peers):
            cp = pltpu.make_async_remote_copy(
                src_ref=src_ref.at[idx] if per_peer else src_ref, dst_ref=out_ref.at[me],
                send_sem=send_sems.at[k], recv_sem=recv_sems.at[k], device_id=dev, device_id_type=MESH_ID)
            cp.start()
            sends.append(cp)
        for k, (dev, idx) in enumerate(peers):
            pltpu.make_async_remote_copy(
                src_ref=mine, dst_ref=out_ref.at[idx], send_sem=send_sems.at[k], recv_sem=recv_sems.at[k],
                device_id=dev, device_id_type=MESH_ID).wait_recv()
        for cp in sends:
            cp.wait_send()
        local.wait()

    return pl.pallas_call(
        body, name=name, in_specs=[ANY], out_specs=ANY,
        out_shape=jax.ShapeDtypeStruct((N_DEV,) + tuple(blk), src.dtype),
        scratch_shapes=[pltpu.SemaphoreType.DMA((N_DEV - 1,)), pltpu.SemaphoreType.DMA((N_DEV - 1,)),
                        pltpu.SemaphoreType.DMA],
    )(src)


def _gather_two_level(name, src):
    def body(x_ref, out_ref, send_sems, recv_sems, local_sem):
        x, y, c = lax.axis_index("x"), lax.axis_index("y"), lax.axis_index("c")
        me, sibling = (x, y, c), (x, y, 1 - c)
        chips = [(1 - x, y), (x, 1 - y), (1 - x, 1 - y)]

        def rows(px, py, pc):
            return out_ref.at[4 * px + 2 * py + pc]

        def copy(k, block, to, src_ref=None):
            return pltpu.make_async_remote_copy(
                src_ref=rows(*block) if src_ref is None else src_ref, dst_ref=rows(*block),
                send_sem=send_sems.at[k], recv_sem=recv_sems.at[k], device_id=to, device_id_type=MESH_ID)

        mine = pltpu.make_async_copy(x_ref, rows(*me), local_sem)
        mine.start()
        first = [copy(0, me, sibling, x_ref)] + [copy(1 + j, me, (*chip, c), x_ref) for j, chip in enumerate(chips)]
        for cp in first:
            cp.start()
        passed = [copy(4 + j, (*chip, c), sibling) for j, chip in enumerate(chips)]
        for j, chip in enumerate(chips):
            copy(1 + j, (*chip, c), me).wait_recv()
            passed[j].start()
        copy(0, sibling, me).wait_recv()
        for j, chip in enumerate(chips):
            copy(4 + j, (*chip, 1 - c), me).wait_recv()
        for cp in first + passed:
            cp.wait_send()
        mine.wait()

    return pl.pallas_call(
        body, name=name, in_specs=[ANY], out_specs=ANY,
        out_shape=jax.ShapeDtypeStruct((N_DEV,) + tuple(src.shape), src.dtype),
        scratch_shapes=[pltpu.SemaphoreType.DMA((N_DEV - 1,)), pltpu.SemaphoreType.DMA((N_DEV - 1,)),
                        pltpu.SemaphoreType.DMA],
    )(src)


HBM = pl.BlockSpec(memory_space=pltpu.HBM)
SEM = pl.BlockSpec(memory_space=pltpu.SEMAPHORE)
EFFECT = pltpu.SideEffectType.DATAFLOW_SIDE_EFFECTING


def _peer_copy(src_ref, land_ref, send_sems, recv_sems, per_peer, me, k, dev, idx, receiving):
    return pltpu.make_async_remote_copy(
        src_ref=src_ref.at[idx] if per_peer else src_ref, dst_ref=land_ref.at[idx if receiving else me],
        send_sem=send_sems.at[k], recv_sem=recv_sems.at[k], device_id=dev, device_id_type=MESH_ID)


def _push_start(name, src, per_peer):
    blk = src.shape[1:] if per_peer else src.shape
    land_shape = (N_DEV,) + tuple(blk)

    def body(src_ref, land_ref, send_sems, recv_sems, src_thru, land_thru, token):
        me, peers = _me_and_peers()
        for k, (dev, idx) in enumerate(peers):
            _peer_copy(src_ref, land_ref, send_sems, recv_sems, per_peer, me, k, dev, idx, False).start()
        token[...] = jnp.zeros_like(token)

    return pl.pallas_call(
        body, name=name,
        out_shape=(pltpu.SemaphoreType.DMA((N_DEV - 1,)), pltpu.SemaphoreType.DMA((N_DEV - 1,)),
                   pltpu.HBM(src.shape, src.dtype), pltpu.HBM(land_shape, src.dtype),
                   jax.ShapeDtypeStruct((8, 128), F32)),
        in_specs=(HBM, HBM), out_specs=(SEM, SEM, HBM, HBM, pl.BlockSpec(memory_space=pltpu.VMEM)),
        input_output_aliases={0: 2, 1: 3}, compiler_params=pltpu.CompilerParams(has_side_effects=EFFECT),
    )(pltpu.with_memory_space_constraint(src, pltpu.HBM),
      pltpu.with_memory_space_constraint(lax.empty(land_shape, src.dtype), pltpu.HBM))


def _push_wait(name, send_sems, recv_sems, src_thru, land_thru, after, per_peer):
    def body(src_ref, land_ref, send_sems, recv_sems, after_ref, src_out, land_out):
        me, peers = _me_and_peers()
        for k, (dev, idx) in enumerate(peers):
            cp = _peer_copy(src_ref, land_ref, send_sems, recv_sems, per_peer, me, k, dev, idx, True)
            cp.wait_send()
            cp.wait_recv()

    return pl.pallas_call(
        body, name=name,
        out_shape=(pltpu.HBM(src_thru.shape, src_thru.dtype), pltpu.HBM(land_thru.shape, land_thru.dtype)),
        in_specs=(HBM, HBM, SEM, SEM, ANY), out_specs=(HBM, HBM), input_output_aliases={0: 0, 1: 1},
        compiler_params=pltpu.CompilerParams(has_side_effects=EFFECT),
    )(src_thru, land_thru, send_sems, recv_sems, after)


def _sum_blocks(name, blocks):
    _, R, C = blocks.shape
    tc = _pick(C, (128,))

    def body(b_ref, o_ref):
        acc = b_ref[0].astype(F32)
        for k in range(1, N_DEV):
            acc = acc + b_ref[k].astype(F32)
        o_ref[...] = acc

    return pl.pallas_call(
        body, name=name, grid=(C // tc,), in_specs=[pl.BlockSpec((N_DEV, R, tc), lambda j: (0, 0, j))],
        out_specs=pl.BlockSpec((R, tc), lambda j: (0, j)), out_shape=jax.ShapeDtypeStruct((R, C), F32),
        compiler_params=_params(("arbitrary",)),
    )(blocks)


def _adamw(name, w, g, m, v):
    R, C = w.shape
    tr = R if R <= 512 else _pick(R, (256, 184, 176, 128, 8))
    spec = pl.BlockSpec((tr, C), lambda i: (i, 0))

    def body(w_ref, g_ref, m_ref, v_ref, d_ref, mo_ref, vo_ref):
        gv = g_ref[...]
        mn = ADAM_B1 * m_ref[...] + (1.0 - ADAM_B1) * gv
        vn = ADAM_B2 * v_ref[...] + (1.0 - ADAM_B2) * jnp.square(gv)
        m_hat = mn / (1.0 - ADAM_B1 ** ADAM_STEP)
        v_hat = vn / (1.0 - ADAM_B2 ** ADAM_STEP)
        d_ref[...] = -ADAM_LR * (m_hat / (jnp.sqrt(v_hat) + ADAM_EPS) + ADAM_WD * w_ref[...])
        mo_ref[...] = mn
        vo_ref[...] = vn

    return pl.pallas_call(
        body, name=name, grid=(R // tr,), in_specs=[spec] * 4, out_specs=[spec] * 3,
        out_shape=[jax.ShapeDtypeStruct((R, C), F32)] * 3, compiler_params=_params(("arbitrary",)),
    )(w, g, m, v)


WEIGHTS = ("meta_tokens", "norm_mix_w", "w_in", "ret_gn_w", "w_ret_out", "w_ssd_conv", "b_ssd_conv", "dt_bias_f",
           "dt_bias_b", "a_log_f", "a_log_b", "d_skip", "ssd_norm_w", "w_ssd_out", "w_out", "norm_ffn_w", "w_ffn_up",
           "w_ffn_conv", "b_ffn_conv", "w_ffn_down", "final_norm_w")
BIG = (("w_in", 1288, True), ("w_ffn_up", 704, True), ("w_ret_out", 128, False), ("w_ssd_out", 256, False),
       ("w_out", 128, False), ("w_ffn_down", 352, False))
REPLICATED = ("norm_mix_w", "ret_gn_w", "b_ssd_conv", "dt_bias_f", "dt_bias_b", "a_log_f", "a_log_b", "d_skip",
              "ssd_norm_w", "norm_ffn_w", "b_ffn_conv", "final_norm_w")
SMALL_SHARDED = (("meta_tokens", 16, 1024), ("w_ssd_conv", 3, 3072), ("w_ffn_conv", 3, 5632))


BIG_IN, BIG_REST = BIG[:1], BIG[1:]


def _pack_big(tree, group):
    parts = []
    for name, _, transposed in group:
        a = tree[name][0]
        parts.append(a.T if transposed else a)
    return jnp.concatenate(parts, axis=0)


def _unpack_big(slab, group):
    out, r0 = {}, 0
    for name, r, transposed in group:
        a = slab[r0:r0 + r]
        out[name] = (a.T if transposed else a)[None]
        r0 += r
    return out


def _pack_flat(arrays, rows):
    flat = jnp.concatenate([a.reshape(-1) for a in arrays])
    return jnp.pad(flat, (0, rows * D_MODEL - flat.shape[0])).reshape(rows, D_MODEL)


def _unpack_flat(slab, shapes):
    flat, out, o = slab.reshape(-1), [], 0
    for s in shapes:
        n = math.prod(s)
        out.append(flat[o:o + n].reshape(s))
        o += n
    return out


def kernel(x, meta_tokens, norm_mix_w, w_in, ret_gn_w, w_ret_out, w_ssd_conv, b_ssd_conv, dt_bias_f, dt_bias_b, a_log_f, a_log_b, d_skip, ssd_norm_w, w_ssd_out, w_out, norm_ffn_w, w_ffn_up, w_ffn_conv, b_ffn_conv, w_ffn_down, final_norm_w, loss_target, m_meta_tokens, m_norm_mix_w, m_w_in, m_ret_gn_w, m_w_ret_out, m_w_ssd_conv, m_b_ssd_conv, m_dt_bias_f, m_dt_bias_b, m_a_log_f, m_a_log_b, m_d_skip, m_ssd_norm_w, m_w_ssd_out, m_w_out, m_norm_ffn_w, m_w_ffn_up, m_w_ffn_conv, m_b_ffn_conv, m_w_ffn_down, m_final_norm_w, v_meta_tokens, v_norm_mix_w, v_w_in, v_ret_gn_w, v_w_ret_out, v_w_ssd_conv, v_b_ssd_conv, v_dt_bias_f, v_dt_bias_b, v_a_log_f, v_a_log_b, v_d_skip, v_ssd_norm_w, v_w_ssd_out, v_w_out, v_norm_ffn_w, v_w_ffn_up, v_w_ffn_conv, v_b_ffn_conv, v_w_ffn_down, v_final_norm_w):
    given = dict(locals())
    wt = {n: given[n] for n in WEIGHTS}
    mt = {n: given["m_" + n] for n in WEIGHTS}
    vt = {n: given["v_" + n] for n in WEIGHTS}
    me = 4 * lax.axis_index("x") + 2 * lax.axis_index("y") + lax.axis_index("c")

    small_names = [n for n, _, _ in SMALL_SHARDED]
    small_local = lambda tree: [tree[n].reshape(r, c // N_DEV) for n, r, c in SMALL_SHARDED]
    all_in = _gather_two_level("gather_w_in", _pack_big(wt, BIG_IN).astype(BF16))
    all_s = _push_blocks("gather_small", _pack_flat(small_local(wt), 8), False)
    rest_src, all_in, all_s = lax.optimization_barrier((_pack_big(wt, BIG_REST).astype(BF16), all_in, all_s))
    rest_flight = _push_start("gather_rest_start", rest_src, False)
    all_s = all_s.reshape(N_DEV, -1)
    full = {"w_in_t": all_in.reshape(-1, D_MODEL)}

    def land_with_own(flight, after, per_peer, name):
        src, land = _push_wait(name, *flight[:4], after, per_peer)
        own = lax.dynamic_slice_in_dim(src, me, 1, axis=0) if per_peer else src[None]
        return lax.dynamic_update_slice_in_dim(land, own, me, axis=0)

    def late_weights(after):
        all_rest = land_with_own(rest_flight, after, False, "gather_rest_wait")
        out, r0 = {}, 0
        for name, r, transposed in BIG_REST:
            out[name + ("_t" if transposed else "")] = all_rest[:, r0:r0 + r].reshape(N_DEV * r, D_MODEL)
            r0 += r
        return out

    flights = {}

    def start_exchange(key, group, gd):
        g_blocks = jnp.concatenate(
            [gd[name + ("_t" if t else "")].reshape(N_DEV, r, D_MODEL) for name, r, t in group], axis=1)
        flights[key] = _push_start("exchange_" + key + "_start", g_blocks.astype(BF16), True)
        return flights[key][4][0, 0]

    o = 0
    for name, r, c in SMALL_SHARDED:
        n = r * c // N_DEV
        full[name] = all_s[:, o:o + n].reshape(N_DEV, r, c // N_DEV).transpose(1, 0, 2).reshape(r, c)
        o += n
    for name in REPLICATED:
        full[name] = wt[name]

    loss, grad_x, g = _local_step(
        x[0], loss_target[0], full, rest_flight[4][0, 0], late_weights,
        lambda gd: start_exchange("rest", BIG_REST, gd), lambda gi: start_exchange("in", BIG_IN, {"w_in_t": gi}))

    last = g["norm_mix_w"]
    g_slabs = {key: _sum_blocks("sum_" + key, land_with_own(flights[key], last, True, "exchange_" + key + "_wait"))
               for key in ("rest", "in")}
    small_parts = [g[n] for n in REPLICATED] + [g[n] for n in small_names] + [loss.reshape(1)]
    g_small = _sum_blocks("sum_small", _push_blocks("gather_small_grads", _pack_flat(small_parts, 64), False))
    small_red = _unpack_flat(g_small, [wt[n].shape for n in REPLICATED] + [(r, c) for _, r, c in SMALL_SHARDED] + [(1,)])
    grads = dict(zip(REPLICATED, small_red[:len(REPLICATED)]))
    for (name, r, c), red in zip(SMALL_SHARDED, small_red[len(REPLICATED):-1]):
        grads[name] = lax.dynamic_slice(red, (0, me * (c // N_DEV)), (r, c // N_DEV)).reshape(wt[name].shape)
    loss_all = small_red[-1][0]
    delta, new_m, new_v = {}, {}, {}
    for key, group in (("in", BIG_IN), ("rest", BIG_REST)):
        grads.update(_unpack_big(g_slabs[key], group))
        for name, _, transposed in group:
            view = (lambda a: a[0].T) if transposed else (lambda a: a[0])
            back = (lambda a: a.T[None]) if transposed else (lambda a: a[None])
            d, mn, vn = _adamw("adamw_" + name, view(wt[name]), view(grads[name]), view(mt[name]), view(vt[name]))
            delta[name], new_m[name], new_v[name] = back(d), back(mn), back(vn)

    rest = list(REPLICATED) + small_names
    shapes = [wt[n].shape for n in rest]
    pack_rest = lambda tree: _pack_flat([tree[n] for n in rest], 24)
    d_rest, m_rest, v_rest = _adamw("adamw_small", pack_rest(wt), pack_rest(grads), pack_rest(mt), pack_rest(vt))
    delta.update(zip(rest, _unpack_flat(d_rest, shapes)))
    new_m.update(zip(rest, _unpack_flat(m_rest, shapes)))
    new_v.update(zip(rest, _unpack_flat(v_rest, shapes)))

    return (loss_all, grad_x[None], *[grads[n] for n in WEIGHTS], *[delta[n] for n in WEIGHTS],
            *[new_m[n] for n in WEIGHTS], *[new_v[n] for n in WEIGHTS])
```

```python
import functools
import math

import jax
import jax.numpy as jnp
from jax import lax
from jax.experimental import pallas as pl
from jax.experimental.pallas import tpu as pltpu

F32 = jnp.float32
BF16 = jnp.bfloat16

D_MODEL = 1024
CHUNK = 128
N_META = 16
PAD_ROWS = CHUNK - N_META
RET_HEADS = 4
RET_QK_DIM = 128
RET_V_DIM = 256
SSD_HEADS = 32
SSD_HEAD_DIM = 64
SSD_GROUPS = 4
SSD_STATE = 128
HEADS_PER_GROUP = SSD_HEADS // SSD_GROUPS
PAIRS_PER_GROUP = HEADS_PER_GROUP // 2
D_FF = 2816
EPS = 1e-6
ROPE_BASE = 10000.0
N_DEV = 8

ADAM_LR = 0.001
ADAM_B1 = 0.9
ADAM_B2 = 0.999
ADAM_EPS = 1e-08
ADAM_WD = 0.01
ADAM_STEP = 10

VMEM_LIMIT = 56 * 1024 * 1024
HALO = 16
HIGHEST = lax.Precision.HIGHEST

SEGMENTS = (("qk", 0, 1024), ("v", 1024, 2048), ("g", 2048, 3072), ("z", 3072, 5120), ("xs", 5120, 7168),
            ("B", 7168, 7680), ("C", 7680, 8192), ("dt", 8192, 8256), ("gates", 8256, 10304))


def _pick(n, cands):
    for c in cands:
        if n % c == 0:
            return c
    raise ValueError(f"no tile for {n}")


def _params(sem):
    return pltpu.CompilerParams(dimension_semantics=sem, vmem_limit_bytes=VMEM_LIMIT)


def _dot(a, b, dims=(((1,), (0,)), ((), ())), precision=None):
    return lax.dot_general(a, b, dims, preferred_element_type=F32, precision=precision)


def _dot_nt(a, b):
    return _dot(a, b, (((1,), (1,)), ((), ())))


def _dot_tn(a, b):
    return _dot(a, b, (((0,), (0,)), ((), ())))


def _mm(name, a, b, mode, add=None, out_dtype=F32):
    if mode == "nn":
        (M, K), N = a.shape, b.shape[1]
    elif mode == "nt":
        (M, K), N = a.shape, b.shape[0]
    else:
        (K, M), N = a.shape, b.shape[1]
    tn = _pick(N, (1408, 1024, 512, 128, 64))
    if mode == "tn":
        tm = M if M <= 1024 else _pick(M, (1408, 1024))
        tk = _pick(K, (2112, 512, 256, 128))
    else:
        tm = _pick(M, (1056, 512, 256, 128))
        tk = K if K <= 2816 else _pick(K, (2048, 1408, 1024))
    nk = K // tk
    if mode == "nn":
        a_spec = pl.BlockSpec((tm, tk), lambda n, m, k: (m, k))
        b_spec = pl.BlockSpec((tk, tn), lambda n, m, k: (k, n))
        dims = (((1,), (0,)), ((), ()))
    elif mode == "nt":
        a_spec = pl.BlockSpec((tm, tk), lambda n, m, k: (m, k))
        b_spec = pl.BlockSpec((tn, tk), lambda n, m, k: (n, k))
        dims = (((1,), (1,)), ((), ()))
    else:
        a_spec = pl.BlockSpec((tk, tm), lambda n, m, k: (k, m))
        b_spec = pl.BlockSpec((tk, tn), lambda n, m, k: (k, n))
        dims = (((0,), (0,)), ((), ()))
    o_spec = pl.BlockSpec((tm, tn), lambda n, m, k: (m, n))
    in_specs = [a_spec, b_spec] + ([o_spec] if add is not None else [])
    args = [a, b] + ([add] if add is not None else [])

    def body(*refs):
        if add is not None:
            a_ref, b_ref, r_ref, o_ref, acc = refs
        else:
            a_ref, b_ref, o_ref, acc = refs
        k = pl.program_id(2)
        p = _dot(a_ref[...].astype(BF16), b_ref[...].astype(BF16), dims)

        def finish(r):
            if add is not None:
                r = r + r_ref[...]
            o_ref[...] = r.astype(out_dtype)

        if nk == 1:
            finish(p)
        else:
            @pl.when(k == 0)
            def _():
                acc[...] = p

            @pl.when(k > 0)
            def _():
                acc[...] += p

            @pl.when(k == nk - 1)
            def _():
                finish(acc[...])

    return pl.pallas_call(
        body, name=name, grid=(N // tn, M // tm, nk), in_specs=in_specs, out_specs=o_spec,
        out_shape=jax.ShapeDtypeStruct((M, N), out_dtype),
        scratch_shapes=[pltpu.VMEM((tm, tn) if nk > 1 else (8, 128), F32)],
        compiler_params=_params(("arbitrary", "arbitrary", "arbitrary")),
    )(*args)


def _const(c):
    return lambda j: c


def _rows(name, fn, T, ncol, ins, params, outs, accs=(), halo=False):
    tm = _pick(T, (384, 256, 128))
    R = T // tm
    hb = tm // HALO
    in_specs, args = [], []
    for spec in ins:
        arr, w, cf = spec[:3]
        lead = spec[3] if len(spec) > 3 else None
        if lead is None:
            mk = lambda blk, rf, cf=cf: pl.BlockSpec(blk, lambda j, i: (rf(i), cf(j)))
            shape = lambda r, w=w: (r, w)
        else:
            mk = lambda blk, rf, cf=cf, lead=lead: pl.BlockSpec(blk, lambda j, i: (lead, rf(i), cf(j)))
            shape = lambda r, w=w: (None, r, w)
        in_specs.append(mk(shape(tm), lambda i: i))
        args.append(arr)
        if halo:
            in_specs.append(mk(shape(HALO), lambda i: jnp.maximum(i * hb - 1, 0)))
            in_specs.append(mk(shape(HALO), lambda i: jnp.minimum((i + 1) * hb, T // HALO - 1)))
            args += [arr, arr]
    for arr, w, cf in params:
        in_specs.append(pl.BlockSpec((arr.shape[0], w), lambda j, i, cf=cf: (0, cf(j))))
        args.append(arr)
    out_shape, out_specs = [], []
    for tw, w, cf, dt in outs:
        out_shape.append(jax.ShapeDtypeStruct((T, tw), dt))
        out_specs.append(pl.BlockSpec((tm, w), lambda j, i, cf=cf: (i, cf(j))))
    for r, tw, w, cf in accs:
        out_shape.append(jax.ShapeDtypeStruct((r, tw), F32))
        out_specs.append(pl.BlockSpec((r, w), lambda j, i, cf=cf: (0, cf(j))))
    n_in, n_par, n_out, n_acc = len(ins), len(params), len(outs), len(accs)

    def body(*refs):
        i = pl.program_id(1)
        vals, p = [], 0
        for _ in range(n_in):
            if halo:
                before = jnp.where(i > 0, refs[p + 1][...], jnp.zeros_like(refs[p + 1]))
                after = jnp.where(i < R - 1, refs[p + 2][...], jnp.zeros_like(refs[p + 2]))
                vals.append(jnp.concatenate([before, refs[p][...], after], axis=0).astype(F32))
                p += 3
            else:
                vals.append(refs[p][...].astype(F32))
                p += 1
        pvals = [refs[p + k][...] for k in range(n_par)]
        p += n_par
        res = fn(i, *vals, *pvals)
        for k in range(n_out):
            refs[p + k][...] = res[k].astype(refs[p + k].dtype)
        p += n_out
        for k in range(n_acc):
            ref, v = refs[p + k], res[n_out + k]

            @pl.when(i == 0)
            def _(ref=ref, v=v):
                ref[...] = v

            @pl.when(i > 0)
            def _(ref=ref, v=v):
                ref[...] += v

    res = pl.pallas_call(
        body, name=name, grid=(ncol, R), in_specs=in_specs, out_specs=out_specs, out_shape=out_shape,
        compiler_params=_params(("arbitrary", "arbitrary")),
    )(*args)
    return res


def _tile_rows(T):
    return _pick(T, (384, 256, 128))


def _row_ids(i, T, halo=False):
    tm = _tile_rows(T)
    if halo:
        return i * tm - HALO + lax.broadcasted_iota(jnp.int32, (tm + 2 * HALO, 1), 0)
    return i * tm + lax.broadcasted_iota(jnp.int32, (tm, 1), 0)


def _rms(x, w):
    return x * lax.rsqrt(jnp.mean(x * x, axis=-1, keepdims=True) + EPS) * w


def _silu(x):
    return x * jax.nn.sigmoid(x)


def _conv3(x, w):
    n = x.shape[0]
    return w[0:1] * pltpu.roll(x, 1, 0) + w[1:2] * x + w[2:3] * pltpu.roll(x, n - 1, 0)


def _conv3_t(d, w):
    n = d.shape[0]
    return w[0:1] * pltpu.roll(d, n - 1, 0) + w[1:2] * d + w[2:3] * pltpu.roll(d, 1, 0)


def _center(x):
    return x[HALO:x.shape[0] - HALO]


def _retention(name, a, b, v, T, da, dv):
    (a, a0), (b, b0), (v, v0) = [t if isinstance(t, tuple) else (t, 0) for t in (a, b, v)]
    nc = T // CHUNK
    log_gammas = [math.log(1.0 - 2.0 ** (-5.0 - h)) for h in range(RET_HEADS)]

    def body(a_ref, b_ref, v_ref, o_ref, st, st_b):
        h = pl.program_id(0)
        lg = jnp.float32(log_gammas[RET_HEADS - 1])
        for k in range(RET_HEADS - 2, -1, -1):
            lg = jnp.where(h == k, jnp.float32(log_gammas[k]), lg)
        li = lax.broadcasted_iota(jnp.int32, (CHUNK, CHUNK), 0)
        si = lax.broadcasted_iota(jnp.int32, (CHUNK, CHUNK), 1)
        dmat = jnp.exp(lg * jnp.abs(li - si).astype(F32))
        pos = lax.broadcasted_iota(jnp.int32, (CHUNK, 1), 0).astype(F32)
        kdec_f = jnp.exp((CHUNK - 1 - pos) * lg)
        qdec_f = jnp.exp((pos + 1) * lg)
        kdec_b = jnp.exp(pos * lg)
        qdec_b = jnp.exp((CHUNK - pos) * lg)
        cdec = jnp.exp(CHUNK * lg)

        def rows(n):
            return pl.ds(pl.multiple_of(n * CHUNK, CHUNK), CHUNK)

        st[...] = jnp.zeros_like(st)
        st_b[...] = jnp.zeros_like(st_b)
        o_ref[...] = jnp.zeros_like(o_ref)

        def step(m, carry):
            r = rows(m)
            av, bv, vv = a_ref[r, :], b_ref[r, :], v_ref[r, :].astype(BF16)
            s = _dot_nt(av.astype(BF16), bv.astype(BF16)) * dmat
            o_ref[r, :] += _dot(s.astype(BF16), vv) + _dot((av * qdec_f).astype(BF16), st[...].astype(BF16))
            st[...] = cdec * st[...] + _dot_tn((bv * kdec_f).astype(BF16), vv)
            r = rows(nc - 1 - m)
            av, bv, vv = a_ref[r, :], b_ref[r, :], v_ref[r, :].astype(BF16)
            o_ref[r, :] += _dot((av * qdec_b).astype(BF16), st_b[...].astype(BF16))
            st_b[...] = cdec * st_b[...] + _dot_tn((bv * kdec_b).astype(BF16), vv)
            return carry

        lax.fori_loop(0, nc, step, 0, unroll=3 if nc % 3 == 0 else 1)

    return pl.pallas_call(
        body, name=name, grid=(RET_HEADS,),
        in_specs=[pl.BlockSpec((T, da), lambda h: (0, a0 // da + h)), pl.BlockSpec((T, da), lambda h: (0, b0 // da + h)),
                  pl.BlockSpec((T, dv), lambda h: (0, v0 // dv + h))],
        out_specs=pl.BlockSpec((T, dv), lambda h: (0, h)),
        out_shape=jax.ShapeDtypeStruct((T, RET_HEADS * dv), F32),
        scratch_shapes=[pltpu.VMEM((da, dv), F32), pltpu.VMEM((da, dv), F32)],
        compiler_params=_params(("arbitrary",)),
    )(a, b, v)


def _softplus(x):
    return jnp.maximum(x, 0.0) + jnp.log1p(jnp.exp(-jnp.abs(x)))


def _lane_lo():
    return lax.broadcasted_iota(jnp.int32, (1, CHUNK), 1) < SSD_HEAD_DIM


def _pair_cols(col, j):
    return jnp.where(_lane_lo(), col[:, 2 * j:2 * j + 1], col[:, 2 * j + 1:2 * j + 2])


def _pair_rows(colr, j):
    lo = lax.broadcasted_iota(jnp.int32, (CHUNK, 1), 0) < SSD_HEAD_DIM
    return jnp.where(lo, colr[2 * j:2 * j + 1, :], colr[2 * j + 1:2 * j + 2, :])


def _onehot8(h):
    return (lax.broadcasted_iota(jnp.int32, (1, HEADS_PER_GROUP), 1) == h).astype(F32)


def _onehot8_col(h):
    return (lax.broadcasted_iota(jnp.int32, (HEADS_PER_GROUP, 1), 0) == h).astype(F32)


def _ssd_pre(d, c, rawc, rawr, bc, br, alc, alr):
    li = lax.broadcasted_iota(jnp.int32, (CHUNK, CHUNK), 0)
    si = lax.broadcasted_iota(jnp.int32, (CHUNK, CHUNK), 1)
    dif = li - si if d == 0 else si - li
    mask = dif >= 0
    mask_t = dif <= 0
    rowc = c * CHUNK + lax.broadcasted_iota(jnp.int32, (CHUNK, 1), 0)
    rowr = c * CHUNK + lax.broadcasted_iota(jnp.int32, (1, CHUNK), 1)
    dtc = jnp.where(rowc >= PAD_ROWS, _softplus(rawc + bc), 0.0)
    dtr = jnp.where(rowr >= PAD_ROWS, _softplus(rawr + br), 0.0)
    ac = -jnp.exp(alc)
    ar = -jnp.exp(alr)
    dlc = dtc * ac
    dlr = dtr * ar
    alpc = _dot(mask.astype(F32), dlc, precision=HIGHEST)
    alpr = _dot(dlr, mask_t.astype(F32), precision=HIGHEST)
    endc = jnp.sum(dlc, axis=0, keepdims=True)
    endr = jnp.sum(dlr, axis=1, keepdims=True)
    return dict(mask=mask, mask_t=mask_t, dtc=dtc, ac=ac, alpc=alpc, alpr=alpr, endc=endc, endr=endr,
                valid=rowc >= PAD_ROWS)


def _chunk_of(d, n, nc):
    return n + d * (nc - 1 - 2 * n)


GROUP_WIDTH = HEADS_PER_GROUP * SSD_HEAD_DIM


def _ssd_in_specs(d, cfn):
    return [
        pl.BlockSpec((CHUNK, GROUP_WIDTH), lambda g, n: (cfn(d, n), g)),
        pl.BlockSpec((CHUNK, SSD_STATE), lambda g, n: (cfn(d, n), g)),
        pl.BlockSpec((CHUNK, SSD_STATE), lambda g, n: (cfn(d, n), g)),
        pl.BlockSpec((None, None, CHUNK, HEADS_PER_GROUP), lambda g, n: (d, g, cfn(d, n), 0)),
        pl.BlockSpec((None, None, HEADS_PER_GROUP, CHUNK), lambda g, n: (d, g, 0, cfn(d, n))),
        pl.BlockSpec((None, None, 1, HEADS_PER_GROUP), lambda g, n: (d, g, 0, 0)),
        pl.BlockSpec((None, None, HEADS_PER_GROUP, 1), lambda g, n: (d, g, 0, 0)),
        pl.BlockSpec((None, None, 1, HEADS_PER_GROUP), lambda g, n: (d, g, 0, 0)),
        pl.BlockSpec((None, None, HEADS_PER_GROUP, 1), lambda g, n: (d, g, 0, 0)),
    ]


N_SSD_IN = 9


def _ssd_fwd(xs, bm, cm, small, T):
    nc = T // CHUNK
    cfn = lambda d, n: _chunk_of(d, n, nc)

    def one_direction(d, n, ins, y_ref, hs_ref, h_scr):
        x_ref, b_ref, c_ref, rawc_ref, rawr_ref, bc_ref, br_ref, alc_ref, alr_ref = ins
        c = cfn(d, n)
        q = _ssd_pre(d, c, rawc_ref[...], rawr_ref[...], bc_ref[...], br_ref[...], alc_ref[...], alr_ref[...])
        bv = b_ref[...].astype(BF16)
        cv = c_ref[...].astype(BF16)
        cb = _dot_nt(cv, bv)
        lo = _lane_lo()
        for j in range(PAIRS_PER_GROUP):
            xp = x_ref[:, j * CHUNK:(j + 1) * CHUNK]
            xd = xp * _pair_cols(q["dtc"], j)
            xdb = xd.astype(BF16)
            yi = []
            for e in range(2):
                h = 2 * j + e
                lm = jnp.exp(jnp.where(q["mask"], q["alpc"][:, h:h + 1] - q["alpr"][h:h + 1, :], -jnp.inf))
                yi.append(_dot((cb * lm).astype(BF16), xdb))
            alp = _pair_cols(q["alpc"], j)
            hp = h_scr[j]
            hs_ref[j] = hp
            yo = jnp.exp(alp) * _dot_nt(cv, hp.astype(BF16))
            y_ref[:, j * CHUNK:(j + 1) * CHUNK] = (jnp.where(lo, yi[0], yi[1]) + yo).astype(y_ref.dtype)
            de = jnp.exp(_pair_cols(q["endc"], j) - alp)
            h_scr[j] = jnp.exp(_pair_rows(q["endr"], j)) * hp + _dot_tn((xd * de).astype(BF16), bv)

    def body(*refs):
        n = pl.program_id(1)
        ins, (y_f, y_b, hs_f, hs_b, h_scr) = refs[:2 * N_SSD_IN], refs[2 * N_SSD_IN:]

        @pl.when(n == 0)
        def _():
            h_scr[...] = jnp.zeros_like(h_scr)

        one_direction(0, n, ins[:N_SSD_IN], y_f, hs_f, h_scr.at[0])
        one_direction(1, n, ins[N_SSD_IN:], y_b, hs_b, h_scr.at[1])

    y_spec = lambda d: pl.BlockSpec((CHUNK, GROUP_WIDTH), lambda g, n: (cfn(d, n), g))
    hs_spec = lambda d: pl.BlockSpec((None, None, PAIRS_PER_GROUP, CHUNK, SSD_STATE),
                                     lambda g, n: (g, cfn(d, n), 0, 0, 0))
    y_shape = jax.ShapeDtypeStruct((T, SSD_HEADS * SSD_HEAD_DIM), BF16)
    hs_shape = jax.ShapeDtypeStruct((SSD_GROUPS, nc, PAIRS_PER_GROUP, CHUNK, SSD_STATE), F32)
    y_f, y_b, hs_f, hs_b = pl.pallas_call(
        body, name="ssd_fwd", grid=(SSD_GROUPS, nc),
        in_specs=_ssd_in_specs(0, cfn) + _ssd_in_specs(1, cfn),
        out_specs=[y_spec(0), y_spec(1), hs_spec(0), hs_spec(1)],
        out_shape=[y_shape, y_shape, hs_shape, hs_shape],
        scratch_shapes=[pltpu.VMEM((2, PAIRS_PER_GROUP, CHUNK, SSD_STATE), F32)],
        compiler_params=_params(("arbitrary", "arbitrary")),
    )(xs, bm, cm, *small, xs, bm, cm, *small)
    return (y_f, y_b), (hs_f, hs_b)


def _ssd_bwd(xs, bm, cm, small, hs, dy, T):
    nc = T // CHUNK
    cfn = lambda d, n: _chunk_of(1 - d, n, nc)

    def one_direction(d, n, ins, outs, dh_scr):
        x_ref, b_ref, c_ref, rawc_ref, rawr_ref, bc_ref, br_ref, alc_ref, alr_ref, hs_ref, dy_ref = ins
        dx_ref, db_ref, dc_ref, draw_ref, dbias_ref, dalog_ref = outs
        c = cfn(d, n)
        rawc, bc = rawc_ref[...], bc_ref[...]
        q = _ssd_pre(d, c, rawc, rawr_ref[...], bc, br_ref[...], alc_ref[...], alr_ref[...])
        b32, c32 = b_ref[...], c_ref[...]
        bv, cv = b32.astype(BF16), c32.astype(BF16)
        cb = _dot_nt(cv, bv)
        lo = _lane_lo()
        row_lo = lax.broadcasted_iota(jnp.int32, (CHUNK, 1), 0) < SSD_HEAD_DIM
        dcb = jnp.zeros((CHUNK, CHUNK), F32)
        dcp = jnp.zeros((CHUNK, SSD_STATE), F32)
        dbp = jnp.zeros((CHUNK, SSD_STATE), F32)
        dalp = jnp.zeros((CHUNK, HEADS_PER_GROUP), F32)
        dalp_row = jnp.zeros((HEADS_PER_GROUP, CHUNK), F32)
        dend = jnp.zeros((1, HEADS_PER_GROUP), F32)
        ddtx = jnp.zeros((CHUNK, HEADS_PER_GROUP), F32)

        def half_sums(t):
            return (jnp.sum(jnp.where(lo, t, 0.0), axis=1, keepdims=True),
                    jnp.sum(jnp.where(lo, 0.0, t), axis=1, keepdims=True))

        for j in range(PAIRS_PER_GROUP):
            xp = x_ref[:, j * CHUNK:(j + 1) * CHUNK]
            dtp = _pair_cols(q["dtc"], j)
            xd = xp * dtp
            xdb = xd.astype(BF16)
            dyp = dy_ref[:, j * CHUNK:(j + 1) * CHUNK]
            dyb = dyp.astype(BF16)
            hn = hs_ref[j]
            hnb = hn.astype(BF16)
            dh1 = dh_scr[j]
            dh1b = dh1.astype(BF16)
            alp = _pair_cols(q["alpc"], j)
            ea = jnp.exp(alp)
            de = jnp.exp(_pair_cols(q["endc"], j) - alp)
            dxi = []
            for e in range(2):
                h = 2 * j + e
                lm = jnp.exp(jnp.where(q["mask"], q["alpc"][:, h:h + 1] - q["alpr"][h:h + 1, :], -jnp.inf))
                dxi.append(_dot_tn((cb * lm).astype(BF16), dyb))
                dyeb_h = (jnp.where(lo, dyp, 0.0) if e == 0 else jnp.where(lo, 0.0, dyp)).astype(BF16)
                gl = _dot_nt(dyeb_h, xdb) * lm
                dcb = dcb + gl
                gm = gl * cb
                dalp = dalp + jnp.sum(gm, axis=1, keepdims=True) * _onehot8(h)
                dalp_row = dalp_row + _onehot8_col(h) * jnp.sum(gm, axis=0, keepdims=True)
            y_off = ea * _dot_nt(cv, hnb)
            dxs_state = de * _dot_nt(bv, dh1b)
            dxd = jnp.where(lo, dxi[0], dxi[1]) + dxs_state
            dyeb = (dyp * ea).astype(BF16)
            dcp = dcp + _dot(dyeb, hnb)
            dbp = dbp + _dot((xd * de).astype(BF16), dh1b)
            dh_scr[j] = jnp.exp(_pair_rows(q["endr"], j)) * dh1 + _dot_tn(dyeb, cv)
            r0, r1 = half_sums(dyp * y_off - xd * dxs_state)
            dalp = dalp + r0 * _onehot8(2 * j) + r1 * _onehot8(2 * j + 1)
            t0, t1 = half_sums(jnp.sum(xd * dxs_state, axis=0, keepdims=True))
            u = dh1 * hn
            u0 = jnp.sum(jnp.sum(jnp.where(row_lo, u, 0.0), axis=0, keepdims=True), axis=1, keepdims=True)
            u1 = jnp.sum(jnp.sum(jnp.where(row_lo, 0.0, u), axis=0, keepdims=True), axis=1, keepdims=True)
            eend = jnp.exp(q["endc"])
            dend = dend + (t0 + eend * u0) * _onehot8(2 * j) + (t1 + eend * u1) * _onehot8(2 * j + 1)
            dx_ref[:, j * CHUNK:(j + 1) * CHUNK] = (dxd * dtp).astype(dx_ref.dtype)
            w0, w1 = half_sums(dxd * xp)
            ddtx = ddtx + w0 * _onehot8(2 * j) + w1 * _onehot8(2 * j + 1)

        dcbb = dcb.astype(BF16)
        dc_ref[...] = (dcp + _dot(dcbb, bv)).astype(dc_ref.dtype)
        db_ref[...] = (dbp + _dot_tn(dcbb, cv)).astype(db_ref.dtype)
        tri_t = q["mask_t"].astype(F32)
        ddl = _dot(tri_t, dalp, precision=HIGHEST) + dend \
            - _dot(tri_t, dalp_row, (((1,), (1,)), ((), ())), precision=HIGHEST)
        ddt = ddl * q["ac"] + ddtx
        draw = jnp.where(q["valid"], ddt * jax.nn.sigmoid(rawc + bc), 0.0)
        draw_ref[...] = draw
        dbias = jnp.sum(draw, axis=0, keepdims=True)
        dalog = jnp.sum(ddl * q["dtc"], axis=0, keepdims=True) * q["ac"]

        @pl.when(n == 0)
        def _():
            dbias_ref[...] = dbias
            dalog_ref[...] = dalog

        @pl.when(n > 0)
        def _():
            dbias_ref[...] += dbias
            dalog_ref[...] += dalog

    n_in, n_out = N_SSD_IN + 2, 6

    def body(*refs):
        n = pl.program_id(1)
        ins, outs, dh_scr = refs[:2 * n_in], refs[2 * n_in:2 * (n_in + n_out)], refs[-1]

        @pl.when(n == 0)
        def _():
            dh_scr[...] = jnp.zeros_like(dh_scr)

        one_direction(0, n, ins[:n_in], outs[:n_out], dh_scr.at[0])
        one_direction(1, n, ins[n_in:], outs[n_out:], dh_scr.at[1])

    def in_specs(d):
        return _ssd_in_specs(d, cfn) + [
            pl.BlockSpec((None, None, PAIRS_PER_GROUP, CHUNK, SSD_STATE), lambda g, n: (g, cfn(d, n), 0, 0, 0)),
            pl.BlockSpec((CHUNK, GROUP_WIDTH), lambda g, n: (cfn(d, n), g))]

    def out_specs(d):
        acc = pl.BlockSpec((None, 1, HEADS_PER_GROUP), lambda g, n: (g, 0, 0))
        return [pl.BlockSpec((CHUNK, GROUP_WIDTH), lambda g, n: (cfn(d, n), g)),
                pl.BlockSpec((CHUNK, SSD_STATE), lambda g, n: (cfn(d, n), g)),
                pl.BlockSpec((CHUNK, SSD_STATE), lambda g, n: (cfn(d, n), g)),
                pl.BlockSpec((None, CHUNK, HEADS_PER_GROUP), lambda g, n: (g, cfn(d, n), 0)), acc, acc]

    out_shape = [jax.ShapeDtypeStruct((T, SSD_HEADS * SSD_HEAD_DIM), BF16),
                 jax.ShapeDtypeStruct((T, SSD_GROUPS * SSD_STATE), BF16),
                 jax.ShapeDtypeStruct((T, SSD_GROUPS * SSD_STATE), BF16),
                 jax.ShapeDtypeStruct((SSD_GROUPS, T, HEADS_PER_GROUP), F32),
                 jax.ShapeDtypeStruct((SSD_GROUPS, 1, HEADS_PER_GROUP), F32),
                 jax.ShapeDtypeStruct((SSD_GROUPS, 1, HEADS_PER_GROUP), F32)]
    res = pl.pallas_call(
        body, name="ssd_bwd", grid=(SSD_GROUPS, nc),
        in_specs=in_specs(0) + in_specs(1), out_specs=out_specs(0) + out_specs(1), out_shape=out_shape * 2,
        scratch_shapes=[pltpu.VMEM((2, PAIRS_PER_GROUP, CHUNK, SSD_STATE), F32)],
        compiler_params=_params(("arbitrary", "arbitrary")),
    )(xs, bm, cm, *small, hs[0], dy, xs, bm, cm, *small, hs[1], dy)
    return [(res[k], res[n_out + k]) for k in range(n_out)]


def _rot(x, cs, sn):
    return x * cs + pltpu.roll(x, RET_QK_DIM // 2, 1) * sn


def _rot_t(d, cs, sn):
    return d * cs + pltpu.roll(d * sn, RET_QK_DIM // 2, 1)


def _ret_post(y, g, w):
    parts = []
    for h in range(RET_HEADS):
        yh = y[:, h * RET_V_DIM:(h + 1) * RET_V_DIM]
        mu = jnp.mean(yh, axis=-1, keepdims=True)
        var = jnp.mean(jnp.square(yh - mu), axis=-1, keepdims=True)
        parts.append((yh - mu) * lax.rsqrt(var + EPS))
    return _silu(g) * (jnp.concatenate(parts, axis=1) * w)


def _ssd_post(yf, yb, xs, z, dskip, w):
    y = (yf + yb + xs * dskip) * _silu(z)
    return y * lax.rsqrt(jnp.mean(y * y, axis=-1, keepdims=True) + EPS) * w


def _merge(gates, yr, ys, valid):
    m = jax.nn.sigmoid(gates[:, :D_MODEL]) * yr + jax.nn.sigmoid(gates[:, D_MODEL:]) * ys
    return jnp.where(valid, m, 0.0)


def _rope_tables(T):
    half = RET_QK_DIM // 2
    inv = ROPE_BASE ** (-jnp.arange(half, dtype=F32) / half)
    pos = (jnp.arange(T) - PAD_ROWS).astype(F32)
    ang = pos[:, None] * inv[None, :]
    cos, sin = jnp.cos(ang), jnp.sin(ang)
    return jnp.concatenate([cos, cos], axis=1), jnp.concatenate([-sin, sin], axis=1)


def _per_group(v):
    c = v.reshape(SSD_GROUPS, 1, HEADS_PER_GROUP)
    return c, c.reshape(SSD_GROUPS, HEADS_PER_GROUP, 1)


def _local_step(x, target, w, tick, late_weights, early_grads, in_grads):
    S = x.shape[0]
    T = S + CHUNK
    tm = _tile_rows(T)
    c0 = _const(0)

    h0 = jnp.concatenate([jnp.zeros((PAD_ROWS, D_MODEL), F32), w["meta_tokens"], x], axis=0)
    tgt = jnp.concatenate([jnp.zeros((CHUNK, D_MODEL), F32), target], axis=0)
    seg_at = {name: a for name, a, _ in SEGMENTS}
    w_main = w["w_in_t"][:seg_at["dt"]]
    w_dt = jnp.pad(w["w_in_t"][seg_at["dt"]:seg_at["gates"]], ((0, CHUNK - 2 * SSD_HEADS), (0, 0)))
    w_gates = w["w_in_t"][seg_at["gates"]:]

    def norm_cast(name, h, nw):
        return _rows(name, lambda i, hv, wv: (_rms(hv, wv),), T, 1, [(h, D_MODEL, c0)], [(nw, D_MODEL, c0)],
                     [(D_MODEL, D_MODEL, c0, BF16)])[0]

    u = norm_cast("norm_mix", h0, w["norm_mix_w"] + tick)
    p_main = _mm("proj_main", u, w_main, "nt", out_dtype=BF16)
    p_dt = _mm("proj_dt", u, w_dt, "nt")
    p_gates = _mm("proj_gates", u, w_gates, "nt", out_dtype=BF16)

    def seg(name, width, cf=c0):
        base = seg_at[name] // width
        return (p_main, width, lambda j: base + cf(j))

    cs, sn = _rope_tables(T)
    scale = RET_QK_DIM ** -0.5

    def rot_fn(i, qk, csv, snv):
        q = [_rot(qk[:, h * 128:(h + 1) * 128], csv, snv) for h in range(RET_HEADS)]
        k = [_rot(qk[:, (RET_HEADS + h) * 128:(RET_HEADS + h + 1) * 128], csv, snv) * scale for h in range(RET_HEADS)]
        return jnp.concatenate(q, axis=1), jnp.concatenate(k, axis=1)

    qr, kr = _rows("rotary", rot_fn, T, 1, [seg("qk", 1024), (cs, 128, c0), (sn, 128, c0)], [],
                   [(512, 512, c0, F32), (512, 512, c0, F32)])
    v_at = (p_main, seg_at["v"])
    y_ret = _retention("retention", qr, kr, v_at, T, RET_QK_DIM, RET_V_DIM)
    a_ret = _rows("ret_post", lambda i, y, g, gw: (_ret_post(y, g, gw),), T, 1,
                  [(y_ret, 1024, c0), seg("g", 1024)], [(w["ret_gn_w"], 1024, c0)],
                  [(1024, 1024, c0, BF16)])[0]

    conv_w = {"xs": w["w_ssd_conv"][:, :2048], "B": w["w_ssd_conv"][:, 2048:2560], "C": w["w_ssd_conv"][:, 2560:]}
    conv_b = {"xs": w["b_ssd_conv"][:, :2048], "B": w["b_ssd_conv"][:, 2048:2560], "C": w["b_ssd_conv"][:, 2560:]}

    def ssd_conv_fn(i, xe, cw, cb):
        r = _row_ids(i, T, True)
        return (_center(jnp.where(r >= PAD_ROWS, _silu(_conv3(xe, cw) + cb), 0.0)),)

    act = {}
    for name in ("xs", "B", "C"):
        wd = conv_w[name].shape[1]
        cw = 512
        act[name] = _rows("ssd_conv_" + name, ssd_conv_fn, T, wd // cw, [seg(name, cw, lambda j: j)],
                          [(conv_w[name], cw, lambda j: j), (conv_b[name], cw, lambda j: j)],
                          [(wd, cw, lambda j: j, BF16)], halo=True)[0]

    raw = p_dt[:, :2 * SSD_HEADS].reshape(T, 2, SSD_GROUPS, HEADS_PER_GROUP)
    rawc = raw.transpose(1, 2, 0, 3)
    rawr = raw.transpose(1, 2, 3, 0)
    bias = [_per_group(w["dt_bias_f"]), _per_group(w["dt_bias_b"])]
    alog = [_per_group(w["a_log_f"]), _per_group(w["a_log_b"])]
    small = (rawc, rawr, jnp.stack([bias[0][0], bias[1][0]]), jnp.stack([bias[0][1], bias[1][1]]),
             jnp.stack([alog[0][0], alog[1][0]]), jnp.stack([alog[0][1], alog[1][1]]))
    y_dir, states = _ssd_fwd(act["xs"], act["B"], act["C"], small, T)

    dskip_e = jnp.repeat(w["d_skip"], SSD_HEAD_DIM, axis=1)
    gcol = lambda j: j
    gw_ = 512
    a_ssd = _rows("ssd_post", lambda i, yf, yb, xv, zv, dk, nw: (_ssd_post(yf, yb, xv, zv, dk, nw),), T, SSD_GROUPS,
                  [(y_dir[0], gw_, gcol), (y_dir[1], gw_, gcol), (act["xs"], gw_, gcol), seg("z", gw_, gcol)],
                  [(dskip_e, gw_, gcol), (w["ssd_norm_w"], gw_, gcol)], [(2048, gw_, gcol, BF16)])[0]

    w = dict(w, **late_weights(a_ssd))
    w_up_g, w_up_u = w["w_ffn_up_t"][:D_FF], w["w_ffn_up_t"][D_FF:]
    y_ret_o = _mm("ret_out", a_ret, w["w_ret_out"], "nn", out_dtype=BF16)
    y_ssd_o = _mm("ssd_out", a_ssd, w["w_ssd_out"], "nn", out_dtype=BF16)

    def merge_fn(i, gates, yr, ys):
        return (_merge(gates, yr, ys, _row_ids(i, T) >= PAD_ROWS),)

    merged = _rows("merge", merge_fn, T, 1, [(p_gates, 2048, c0), (y_ret_o, 1024, c0), (y_ssd_o, 1024, c0)], [],
                   [(1024, 1024, c0, BF16)])[0]
    h1 = _mm("mix_out", merged, w["w_out"], "nn", add=h0)

    n2 = norm_cast("norm_ffn", h1, w["norm_ffn_w"])
    fg_pre = _mm("ffn_up_g", n2, w_up_g, "nt", out_dtype=BF16)
    fu_pre = _mm("ffn_up_u", n2, w_up_u, "nt", out_dtype=BF16)
    cwg, cwu = w["w_ffn_conv"][:, :D_FF], w["w_ffn_conv"][:, D_FF:]
    cbg, cbu = w["b_ffn_conv"][:, :D_FF], w["b_ffn_conv"][:, D_FF:]
    fcol = lambda j: j
    fw = 1408

    def ffn_act_fn(i, ge, ue, wg, wu, bg, bu):
        return (_center(_silu(_conv3(ge, wg) + bg) * (_conv3(ue, wu) + bu)),)

    a2 = _rows("ffn_act", ffn_act_fn, T, D_FF // fw, [(fg_pre, fw, fcol), (fu_pre, fw, fcol)],
               [(cwg, fw, fcol), (cwu, fw, fcol), (cbg, fw, fcol), (cbu, fw, fcol)], [(D_FF, fw, fcol, BF16)],
               halo=True)[0]
    h2 = _mm("ffn_down", a2, w["w_ffn_down"], "nn", add=h1)

    fnw = w["final_norm_w"].reshape(1, D_MODEL)

    def loss_fn(i, hv, tv, nw):
        valid = _row_ids(i, T) >= CHUNK
        y, vjp = jax.vjp(_rms, hv, nw)
        diff = jnp.where(valid, y - tv, 0.0)
        dh, dw = vjp(diff * (1.0 / D_MODEL))
        part = 0.5 / D_MODEL * jnp.sum(jnp.sum(diff * diff, axis=1, keepdims=True), axis=0, keepdims=True)
        return dh, jnp.broadcast_to(part, (1, 128)), dw

    dh2, loss_acc, d_fnw = _rows("loss", loss_fn, T, 1, [(h2, D_MODEL, c0), (tgt, D_MODEL, c0)], [(fnw, D_MODEL, c0)],
                                 [(D_MODEL, D_MODEL, c0, F32)], [(1, 128, 128, c0), (1, D_MODEL, D_MODEL, c0)])
    loss = loss_acc[0, 0]
    grads = {"final_norm_w": d_fnw.reshape(D_MODEL)}

    da2 = _mm("d_ffn_act", dh2, w["w_ffn_down"], "nt", out_dtype=BF16)
    grads["w_ffn_down"] = _mm("g_ffn_down", a2, dh2, "tn", out_dtype=BF16)

    def ffn_bwd_fn(i, ge, ue, de, wg, wu, bg, bu):
        fg = _conv3(ge, wg) + bg
        fu = _conv3(ue, wu) + bu
        sg = jax.nn.sigmoid(fg)
        dfg = de * fu * (sg * (1.0 + fg * (1.0 - sg)))
        dfu = de * (fg * sg)
        n = ge.shape[0]

        def wgrad(df, xe):
            df_c = _center(df)
            return jnp.concatenate([jnp.sum(df_c * _center(pltpu.roll(xe, 1, 0)), axis=0, keepdims=True),
                                    jnp.sum(df_c * _center(xe), axis=0, keepdims=True),
                                    jnp.sum(df_c * _center(pltpu.roll(xe, n - 1, 0)), axis=0, keepdims=True)], axis=0)

        return (_center(_conv3_t(dfg, wg)), _center(_conv3_t(dfu, wu)), wgrad(dfg, ge), wgrad(dfu, ue),
                jnp.sum(_center(dfg), axis=0, keepdims=True), jnp.sum(_center(dfu), axis=0, keepdims=True))

    dfg_pre, dfu_pre, g_cwg, g_cwu, g_cbg, g_cbu = _rows(
        "ffn_act_bwd", ffn_bwd_fn, T, D_FF // fw, [(fg_pre, fw, fcol), (fu_pre, fw, fcol), (da2, fw, fcol)],
        [(cwg, fw, fcol), (cwu, fw, fcol), (cbg, fw, fcol), (cbu, fw, fcol)],
        [(D_FF, fw, fcol, BF16), (D_FF, fw, fcol, BF16)],
        [(3, D_FF, fw, fcol), (3, D_FF, fw, fcol), (1, D_FF, fw, fcol), (1, D_FF, fw, fcol)], halo=True)
    grads["w_ffn_conv"] = jnp.concatenate([g_cwg, g_cwu], axis=1)
    grads["b_ffn_conv"] = jnp.concatenate([g_cbg, g_cbu], axis=1)
    dn2 = _mm("d_norm_ffn_g", dfg_pre, w_up_g, "nn")
    dn2 = _mm("d_norm_ffn_u", dfu_pre, w_up_u, "nn", add=dn2)
    grads["w_ffn_up_t"] = jnp.concatenate([_mm("g_ffn_up_g", dfg_pre, n2, "tn", out_dtype=BF16), _mm("g_ffn_up_u", dfu_pre, n2, "tn", out_dtype=BF16)],
                                          axis=0)

    def norm_bwd(name, h, nw, dn, dres):
        def fn(i, hv, dnv, drv, wv):
            _, vjp = jax.vjp(_rms, hv, wv)
            dh, dw = vjp(dnv)
            return dh + drv, dw
        return _rows(name, fn, T, 1, [(h, D_MODEL, c0), (dn, D_MODEL, c0), (dres, D_MODEL, c0)], [(nw, D_MODEL, c0)],
                     [(D_MODEL, D_MODEL, c0, F32)], [(1, D_MODEL, D_MODEL, c0)])

    dh1, grads["norm_ffn_w"] = norm_bwd("norm_ffn_bwd", h1, w["norm_ffn_w"], dn2, dh2)

    dmerged = _mm("d_merged", dh1, w["w_out"], "nt", out_dtype=BF16)
    grads["w_out"] = _mm("g_out", merged, dh1, "tn", out_dtype=BF16)

    def merge_bwd_fn(i, gates, yr, ys, dm):
        valid = _row_ids(i, T) >= PAD_ROWS
        _, vjp = jax.vjp(lambda a, b, c: _merge(a, b, c, valid), gates, yr, ys)
        return vjp(dm)

    dgates, dyr, dys = _rows("merge_bwd", merge_bwd_fn, T, 1,
                             [(p_gates, 2048, c0), (y_ret_o, 1024, c0), (y_ssd_o, 1024, c0), (dmerged, 1024, c0)],
                             [], [(2048, 2048, c0, BF16), (1024, 1024, c0, BF16), (1024, 1024, c0, BF16)])
    dproj = {"gates": dgates}

    da_ssd = _mm("d_ssd_act", dys, w["w_ssd_out"], "nt", out_dtype=BF16)
    grads["w_ssd_out"] = _mm("g_ssd_out", a_ssd, dys, "tn", out_dtype=BF16)

    def ssd_post_bwd_fn(i, yf, yb, xv, zv, da, dk, nw):
        _, vjp = jax.vjp(_ssd_post, yf, yb, xv, zv, dk, nw)
        dyf, _, dxv, dzv, ddk, dnw = vjp(da)
        return dyf, dxv, dzv, ddk, dnw

    dy_ssd, dxs_skip, dproj["z"], g_dskip_e, grads["ssd_norm_w"] = _rows(
        "ssd_post_bwd", ssd_post_bwd_fn, T, SSD_GROUPS,
        [(y_dir[0], gw_, gcol), (y_dir[1], gw_, gcol), (act["xs"], gw_, gcol), seg("z", gw_, gcol),
         (da_ssd, gw_, gcol)],
        [(dskip_e, gw_, gcol), (w["ssd_norm_w"], gw_, gcol)],
        [(2048, gw_, gcol, BF16), (2048, gw_, gcol, BF16), (2048, gw_, gcol, BF16)],
        [(1, 2048, gw_, gcol), (1, 2048, gw_, gcol)])
    grads["d_skip"] = g_dskip_e.reshape(SSD_HEADS, SSD_HEAD_DIM).sum(axis=1).reshape(1, SSD_HEADS)

    dxs_dir, db_dir, dc_dir, draw, g_bias, g_alog = _ssd_bwd(act["xs"], act["B"], act["C"], small, states, dy_ssd, T)
    grads["dt_bias_f"], grads["dt_bias_b"] = g_bias[0].reshape(1, SSD_HEADS), g_bias[1].reshape(1, SSD_HEADS)
    grads["a_log_f"], grads["a_log_b"] = g_alog[0].reshape(1, SSD_HEADS), g_alog[1].reshape(1, SSD_HEADS)
    d_dt = jnp.stack(draw).transpose(2, 0, 1, 3).reshape(T, 2 * SSD_HEADS)
    dproj["dt"] = jnp.pad(d_dt, ((0, 0), (0, CHUNK - 2 * SSD_HEADS))).astype(BF16)

    def make_conv_bwd(nsum):
        def fn(i, xe, *rest):
            ds, (cw, cb) = rest[:nsum], rest[nsum:]
            r = _row_ids(i, T, True)
            dact = ds[0]
            for t in ds[1:]:
                dact = dact + t
            dact = jnp.where(r >= PAD_ROWS, dact, 0.0)
            pre = _conv3(xe, cw) + cb
            sg = jax.nn.sigmoid(pre)
            dpre = dact * (sg * (1.0 + pre * (1.0 - sg)))
            n = xe.shape[0]
            dpc = _center(dpre)
            dw = jnp.concatenate([jnp.sum(dpc * _center(pltpu.roll(xe, 1, 0)), axis=0, keepdims=True),
                                  jnp.sum(dpc * _center(xe), axis=0, keepdims=True),
                                  jnp.sum(dpc * _center(pltpu.roll(xe, n - 1, 0)), axis=0, keepdims=True)], axis=0)
            return _center(_conv3_t(dpre, cw)), dw, jnp.sum(dpc, axis=0, keepdims=True)
        return fn

    g_cw, g_cb = {}, {}
    cots = {"xs": [(dxs_dir[0], 512, gcol), (dxs_dir[1], 512, gcol), (dxs_skip, 512, gcol)],
            "B": [(db_dir[0], 512, gcol), (db_dir[1], 512, gcol)],
            "C": [(dc_dir[0], 512, gcol), (dc_dir[1], 512, gcol)]}
    for name in ("xs", "B", "C"):
        wd = conv_w[name].shape[1]
        dproj[name], g_cw[name], g_cb[name] = _rows(
            "ssd_conv_bwd_" + name, make_conv_bwd(len(cots[name])), T, wd // 512,
            [seg(name, 512, gcol)] + cots[name], [(conv_w[name], 512, gcol), (conv_b[name], 512, gcol)],
            [(wd, 512, gcol, BF16)], [(3, wd, 512, gcol), (1, wd, 512, gcol)], halo=True)
    grads["w_ssd_conv"] = jnp.concatenate([g_cw["xs"], g_cw["B"], g_cw["C"]], axis=1)
    grads["b_ssd_conv"] = jnp.concatenate([g_cb["xs"], g_cb["B"], g_cb["C"]], axis=1)

    da_ret = _mm("d_ret_act", dyr, w["w_ret_out"], "nt", out_dtype=BF16)
    grads["w_ret_out"] = _mm("g_ret_out", a_ret, dyr, "tn", out_dtype=BF16)
    tick = early_grads({n: grads.pop(n) for n in ("w_ffn_up_t", "w_ret_out", "w_ssd_out", "w_out", "w_ffn_down")})

    def ret_post_bwd_fn(i, y, g, da, gw):
        _, vjp = jax.vjp(_ret_post, y, g, gw)
        return vjp(da)

    dy_ret, dproj["g"], grads["ret_gn_w"] = _rows(
        "ret_post_bwd", ret_post_bwd_fn, T, 1, [(y_ret, 1024, c0), seg("g", 1024), (da_ret, 1024, c0)],
        [(w["ret_gn_w"] + tick, 1024, c0)], [(1024, 1024, c0, BF16), (1024, 1024, c0, BF16)], [(1, 1024, 1024, c0)])
    dproj["v"] = _retention("retention_dv", kr, qr, dy_ret, T, RET_QK_DIM, RET_V_DIM)
    dqr = _retention("retention_dq", dy_ret, v_at, kr, T, RET_V_DIM, RET_QK_DIM)
    dkr = _retention("retention_dk", v_at, dy_ret, qr, T, RET_V_DIM, RET_QK_DIM)

    def rot_bwd_fn(i, dq, dk, csv, snv):
        parts = [_rot_t(dq[:, h * 128:(h + 1) * 128], csv, snv) for h in range(RET_HEADS)]
        parts += [_rot_t(dk[:, h * 128:(h + 1) * 128] * scale, csv, snv) for h in range(RET_HEADS)]
        return (jnp.concatenate(parts, axis=1),)

    dproj["qk"] = _rows("rotary_bwd", rot_bwd_fn, T, 1, [(dqr, 512, c0), (dkr, 512, c0), (cs, 128, c0), (sn, 128, c0)],
                        [], [(1024, 1024, c0, BF16)])[0]

    d_main = jnp.concatenate([dproj[name].astype(BF16) for name, _, _ in SEGMENTS[:7]], axis=1)
    g_in = [_mm("g_in_main", d_main, u, "tn", out_dtype=BF16),
            _mm("g_in_dt", dproj["dt"], u, "tn", out_dtype=BF16)[:2 * SSD_HEADS],
            _mm("g_in_gates", dproj["gates"], u, "tn", out_dtype=BF16)]
    tick = in_grads(jnp.concatenate(g_in, axis=0))
    du = _mm("d_u_dt", dproj["dt"] + tick.astype(BF16), w_dt, "nn")
    du = _mm("d_u_main", d_main, w_main, "nn", add=du)
    du = _mm("d_u_gates", dproj["gates"], w_gates, "nn", add=du)
    dh0, grads["norm_mix_w"] = norm_bwd("norm_mix_bwd", h0, w["norm_mix_w"], du, dh1)
    grads["meta_tokens"] = dh0[PAD_ROWS:CHUNK]
    return loss, dh0[CHUNK:], grads


MESH_ID = pl.DeviceIdType.MESH
ANY = pl.BlockSpec(memory_space=pl.ANY)


def _me_and_peers():
    x, y, c = lax.axis_index("x"), lax.axis_index("y"), lax.axis_index("c")
    peers = []
    for k in range(1, N_DEV):
        px = 1 - x if k & 4 else x
        py = 1 - y if k & 2 else y
        pc = 1 - c if k & 1 else c
        peers.append(((px, py, pc), 4 * px + 2 * py + pc))
    return 4 * x + 2 * y + c, peers


def _push_blocks(name, src, per_peer):
    blk = src.shape[1:] if per_peer else src.shape

    def body(src_ref, out_ref, send_sems, recv_sems, local_sem):
        me, peers = _me_and_peers()
        mine = src_ref.at[me] if per_peer else src_ref
        local = pltpu.make_async_copy(mine, out_ref.at[me], local_sem)
        local.start()
        sends = []
        for k, (dev, idx) in enumerate(peers):
            cp = pltpu.make_async_remote_copy(
                src_ref=src_ref.at[idx] if per_peer else src_ref, dst_ref=out_ref.at[me],
                send_sem=send_sems.at[k], recv_sem=recv_sems.at[k], device_id=dev, device_id_type=MESH_ID)
            cp.start()
            sends.append(cp)
        for k, (dev, idx) in enumerate(peers):
            pltpu.make_async_remote_copy(
                src_ref=mine, dst_ref=out_ref.at[idx], send_sem=send_sems.at[k], recv_sem=recv_sems.at[k],
                device_id=dev, device_id_type=MESH_ID).wait_recv()
        for cp in sends:
            cp.wait_send()
        local.wait()

    return pl.pallas_call(
        body, name=name, in_specs=[ANY], out_specs=ANY,
        out_shape=jax.ShapeDtypeStruct((N_DEV,) + tuple(blk), src.dtype),
        scratch_shapes=[pltpu.SemaphoreType.DMA((N_DEV - 1,)), pltpu.SemaphoreType.DMA((N_DEV - 1,)),
                        pltpu.SemaphoreType.DMA],
    )(src)


def _gather_two_level(name, src):
    def body(x_ref, out_ref, send_sems, recv_sems, local_sem):
        x, y, c = lax.axis_index("x"), lax.axis_index("y"), lax.axis_index("c")
        me, sibling = (x, y, c), (x, y, 1 - c)
        chips = [(1 - x, y), (x, 1 - y), (1 - x, 1 - y)]

        def rows(px, py, pc):
            return out_ref.at[4 * px + 2 * py + pc]

        def copy(k, block, to, src_ref=None):
            return pltpu.make_async_remote_copy(
                src_ref=rows(*block) if src_ref is None else src_ref, dst_ref=rows(*block),
                send_sem=send_sems.at[k], recv_sem=recv_sems.at[k], device_id=to, device_id_type=MESH_ID)

        mine = pltpu.make_async_copy(x_ref, rows(*me), local_sem)
        mine.start()
        first = [copy(0, me, sibling, x_ref)] + [copy(1 + j, me, (*chip, c), x_ref) for j, chip in enumerate(chips)]
        for cp in first:
            cp.start()
        passed = [copy(4 + j, (*chip, c), sibling) for j, chip in enumerate(chips)]
        for j, chip in enumerate(chips):
            copy(1 + j, (*chip, c), me).wait_recv()
            passed[j].start()
        copy(0, sibling, me).wait_recv()
        for j, chip in enumerate(chips):
            copy(4 + j, (*chip, 1 - c), me).wait_recv()
        for cp in first + passed:
            cp.wait_send()
        mine.wait()

    return pl.pallas_call(
        body, name=name, in_specs=[ANY], out_specs=ANY,
        out_shape=jax.ShapeDtypeStruct((N_DEV,) + tuple(src.shape), src.dtype),
        scratch_shapes=[pltpu.SemaphoreType.DMA((N_DEV - 1,)), pltpu.SemaphoreType.DMA((N_DEV - 1,)),
                        pltpu.SemaphoreType.DMA],
    )(src)


HBM = pl.BlockSpec(memory_space=pltpu.HBM)
SEM = pl.BlockSpec(memory_space=pltpu.SEMAPHORE)
EFFECT = pltpu.SideEffectType.DATAFLOW_SIDE_EFFECTING


def _peer_copy(src_ref, land_ref, send_sems, recv_sems, per_peer, me, k, dev, idx, receiving):
    return pltpu.make_async_remote_copy(
        src_ref=src_ref.at[idx] if per_peer else src_ref, dst_ref=land_ref.at[idx if receiving else me],
        send_sem=send_sems.at[k], recv_sem=recv_sems.at[k], device_id=dev, device_id_type=MESH_ID)


def _push_start(name, src, per_peer):
    blk = src.shape[1:] if per_peer else src.shape
    land_shape = (N_DEV,) + tuple(blk)

    def body(src_ref, land_ref, send_sems, recv_sems, src_thru, land_thru, token):
        me, peers = _me_and_peers()
        for k, (dev, idx) in enumerate(peers):
            _peer_copy(src_ref, land_ref, send_sems, recv_sems, per_peer, me, k, dev, idx, False).start()
        token[...] = jnp.zeros_like(token)

    return pl.pallas_call(
        body, name=name,
        out_shape=(pltpu.SemaphoreType.DMA((N_DEV - 1,)), pltpu.SemaphoreType.DMA((N_DEV - 1,)),
                   pltpu.HBM(src.shape, src.dtype), pltpu.HBM(land_shape, src.dtype),
                   jax.ShapeDtypeStruct((8, 128), F32)),
        in_specs=(HBM, HBM), out_specs=(SEM, SEM, HBM, HBM, pl.BlockSpec(memory_space=pltpu.VMEM)),
        input_output_aliases={0: 2, 1: 3}, compiler_params=pltpu.CompilerParams(has_side_effects=EFFECT),
    )(pltpu.with_memory_space_constraint(src, pltpu.HBM),
      pltpu.with_memory_space_constraint(lax.empty(land_shape, src.dtype), pltpu.HBM))


def _push_wait(name, send_sems, recv_sems, src_thru, land_thru, after, per_peer):
    def body(src_ref, land_ref, send_sems, recv_sems, after_ref, src_out, land_out):
        me, peers = _me_and_peers()
        for k, (dev, idx) in enumerate(peers):
            cp = _peer_copy(src_ref, land_ref, send_sems, recv_sems, per_peer, me, k, dev, idx, True)
            cp.wait_send()
            cp.wait_recv()

    return pl.pallas_call(
        body, name=name,
        out_shape=(pltpu.HBM(src_thru.shape, src_thru.dtype), pltpu.HBM(land_thru.shape, land_thru.dtype)),
        in_specs=(HBM, HBM, SEM, SEM, ANY), out_specs=(HBM, HBM), input_output_aliases={0: 0, 1: 1},
        compiler_params=pltpu.CompilerParams(has_side_effects=EFFECT),
    )(src_thru, land_thru, send_sems, recv_sems, after)


def _sum_blocks(name, blocks):
    _, R, C = blocks.shape
    tc = _pick(C, (128,))

    def body(b_ref, o_ref):
        acc = b_ref[0].astype(F32)
        for k in range(1, N_DEV):
            acc = acc + b_ref[k].astype(F32)
        o_ref[...] = acc

    return pl.pallas_call(
        body, name=name, grid=(C // tc,), in_specs=[pl.BlockSpec((N_DEV, R, tc), lambda j: (0, 0, j))],
        out_specs=pl.BlockSpec((R, tc), lambda j: (0, j)), out_shape=jax.ShapeDtypeStruct((R, C), F32),
        compiler_params=_params(("arbitrary",)),
    )(blocks)


def _adamw(name, w, g, m, v):
    R, C = w.shape
    tr = R if R <= 512 else _pick(R, (256, 184, 176, 128, 8))
    spec = pl.BlockSpec((tr, C), lambda i: (i, 0))

    def body(w_ref, g_ref, m_ref, v_ref, d_ref, mo_ref, vo_ref):
        gv = g_ref[...]
        mn = ADAM_B1 * m_ref[...] + (1.0 - ADAM_B1) * gv
        vn = ADAM_B2 * v_ref[...] + (1.0 - ADAM_B2) * jnp.square(gv)
        m_hat = mn / (1.0 - ADAM_B1 ** ADAM_STEP)
        v_hat = vn / (1.0 - ADAM_B2 ** ADAM_STEP)
        d_ref[...] = -ADAM_LR * (m_hat / (jnp.sqrt(v_hat) + ADAM_EPS) + ADAM_WD * w_ref[...])
        mo_ref[...] = mn
        vo_ref[...] = vn

    return pl.pallas_call(
        body, name=name, grid=(R // tr,), in_specs=[spec] * 4, out_specs=[spec] * 3,
        out_shape=[jax.ShapeDtypeStruct((R, C), F32)] * 3, compiler_params=_params(("arbitrary",)),
    )(w, g, m, v)


WEIGHTS = ("meta_tokens", "norm_mix_w", "w_in", "ret_gn_w", "w_ret_out", "w_ssd_conv", "b_ssd_conv", "dt_bias_f",
           "dt_bias_b", "a_log_f", "a_log_b", "d_skip", "ssd_norm_w", "w_ssd_out", "w_out", "norm_ffn_w", "w_ffn_up",
           "w_ffn_conv", "b_ffn_conv", "w_ffn_down", "final_norm_w")
BIG = (("w_in", 1288, True), ("w_ffn_up", 704, True), ("w_ret_out", 128, False), ("w_ssd_out", 256, False),
       ("w_out", 128, False), ("w_ffn_down", 352, False))
REPLICATED = ("norm_mix_w", "ret_gn_w", "b_ssd_conv", "dt_bias_f", "dt_bias_b", "a_log_f", "a_log_b", "d_skip",
              "ssd_norm_w", "norm_ffn_w", "b_ffn_conv", "final_norm_w")
SMALL_SHARDED = (("meta_tokens", 16, 1024), ("w_ssd_conv", 3, 3072), ("w_ffn_conv", 3, 5632))


BIG_IN, BIG_REST = BIG[:1], BIG[1:]


def _pack_big(tree, group):
    parts = []
    for name, _, transposed in group:
        a = tree[name][0]
        parts.append(a.T if transposed else a)
    return jnp.concatenate(parts, axis=0)


def _unpack_big(slab, group):
    out, r0 = {}, 0
    for name, r, transposed in group:
        a = slab[r0:r0 + r]
        out[name] = (a.T if transposed else a)[None]
        r0 += r
    return out


def _pack_flat(arrays, rows):
    flat = jnp.concatenate([a.reshape(-1) for a in arrays])
    return jnp.pad(flat, (0, rows * D_MODEL - flat.shape[0])).reshape(rows, D_MODEL)


def _unpack_flat(slab, shapes):
    flat, out, o = slab.reshape(-1), [], 0
    for s in shapes:
        n = math.prod(s)
        out.append(flat[o:o + n].reshape(s))
        o += n
    return out


def kernel(x, meta_tokens, norm_mix_w, w_in, ret_gn_w, w_ret_out, w_ssd_conv, b_ssd_conv, dt_bias_f, dt_bias_b, a_log_f, a_log_b, d_skip, ssd_norm_w, w_ssd_out, w_out, norm_ffn_w, w_ffn_up, w_ffn_conv, b_ffn_conv, w_ffn_down, final_norm_w, loss_target, m_meta_tokens, m_norm_mix_w, m_w_in, m_ret_gn_w, m_w_ret_out, m_w_ssd_conv, m_b_ssd_conv, m_dt_bias_f, m_dt_bias_b, m_a_log_f, m_a_log_b, m_d_skip, m_ssd_norm_w, m_w_ssd_out, m_w_out, m_norm_ffn_w, m_w_ffn_up, m_w_ffn_conv, m_b_ffn_conv, m_w_ffn_down, m_final_norm_w, v_meta_tokens, v_norm_mix_w, v_w_in, v_ret_gn_w, v_w_ret_out, v_w_ssd_conv, v_b_ssd_conv, v_dt_bias_f, v_dt_bias_b, v_a_log_f, v_a_log_b, v_d_skip, v_ssd_norm_w, v_w_ssd_out, v_w_out, v_norm_ffn_w, v_w_ffn_up, v_w_ffn_conv, v_b_ffn_conv, v_w_ffn_down, v_final_norm_w):
    given = dict(locals())
    wt = {n: given[n] for n in WEIGHTS}
    mt = {n: given["m_" + n] for n in WEIGHTS}
    vt = {n: given["v_" + n] for n in WEIGHTS}
    me = 4 * lax.axis_index("x") + 2 * lax.axis_index("y") + lax.axis_index("c")

    small_names = [n for n, _, _ in SMALL_SHARDED]
    small_local = lambda tree: [tree[n].reshape(r, c // N_DEV) for n, r, c in SMALL_SHARDED]
    all_in = _gather_two_level("gather_w_in", _pack_big(wt, BIG_IN).astype(BF16))
    all_s = _push_blocks("gather_small", _pack_flat(small_local(wt), 8), False)
    rest_src, all_in, all_s = lax.optimization_barrier((_pack_big(wt, BIG_REST).astype(BF16), all_in, all_s))
    rest_flight = _push_start("gather_rest_start", rest_src, False)
    all_s = all_s.reshape(N_DEV, -1)
    full = {"w_in_t": all_in.reshape(-1, D_MODEL)}

    def land_with_own(flight, after, per_peer, name):
        src, land = _push_wait(name, *flight[:4], after, per_peer)
        own = lax.dynamic_slice_in_dim(src, me, 1, axis=0) if per_peer else src[None]
        return lax.dynamic_update_slice_in_dim(land, own, me, axis=0)

    def late_weights(after):
        all_rest = land_with_own(rest_flight, after, False, "gather_rest_wait")
        out, r0 = {}, 0
        for name, r, transposed in BIG_REST:
            out[name + ("_t" if transposed else "")] = all_rest[:, r0:r0 + r].reshape(N_DEV * r, D_MODEL)
            r0 += r
        return out

    flights = {}

    def start_exchange(key, group, gd):
        g_blocks = jnp.concatenate(
            [gd[name + ("_t" if t else "")].reshape(N_DEV, r, D_MODEL) for name, r, t in group], axis=1)
        flights[key] = _push_start("exchange_" + key + "_start", g_blocks.astype(BF16), True)
        return flights[key][4][0, 0]

    o = 0
    for name, r, c in SMALL_SHARDED:
        n = r * c // N_DEV
        full[name] = all_s[:, o:o + n].reshape(N_DEV, r, c // N_DEV).transpose(1, 0, 2).reshape(r, c)
        o += n
    for name in REPLICATED:
        full[name] = wt[name]

    loss, grad_x, g = _local_step(
        x[0], loss_target[0], full, rest_flight[4][0, 0], late_weights,
        lambda gd: start_exchange("rest", BIG_REST, gd), lambda gi: start_exchange("in", BIG_IN, {"w_in_t": gi}))

    last = g["norm_mix_w"]
    g_slabs = {key: _sum_blocks("sum_" + key, land_with_own(flights[key], last, True, "exchange_" + key + "_wait"))
               for key in ("rest", "in")}
    small_parts = [g[n] for n in REPLICATED] + [g[n] for n in small_names] + [loss.reshape(1)]
    g_small = _sum_blocks("sum_small", _push_blocks("gather_small_grads", _pack_flat(small_parts, 64), False))
    small_red = _unpack_flat(g_small, [wt[n].shape for n in REPLICATED] + [(r, c) for _, r, c in SMALL_SHARDED] + [(1,)])
    grads = dict(zip(REPLICATED, small_red[:len(REPLICATED)]))
    for (name, r, c), red in zip(SMALL_SHARDED, small_red[len(REPLICATED):-1]):
        grads[name] = lax.dynamic_slice(red, (0, me * (c // N_DEV)), (r, c // N_DEV)).reshape(wt[name].shape)
    loss_all = small_red[-1][0]
    delta, new_m, new_v = {}, {}, {}
    for key, group in (("in", BIG_IN), ("rest", BIG_REST)):
        grads.update(_unpack_big(g_slabs[key], group))
        for name, _, transposed in group:
            view = (lambda a: a[0].T) if transposed else (lambda a: a[0])
            back = (lambda a: a.T[None]) if transposed else (lambda a: a[None])
            d, mn, vn = _adamw("adamw_" + name, view(wt[name]), view(grads[name]), view(mt[name]), view(vt[name]))
            delta[name], new_m[name], new_v[name] = back(d), back(mn), back(vn)

    rest = list(REPLICATED) + small_names
    shapes = [wt[n].shape for n in rest]
    pack_rest = lambda tree: _pack_flat([tree[n] for n in rest], 24)
    d_rest, m_rest, v_rest = _adamw("adamw_small", pack_rest(wt), pack_rest(grads), pack_rest(mt), pack_rest(vt))
    delta.update(zip(rest, _unpack_flat(d_rest, shapes)))
    new_m.update(zip(rest, _unpack_flat(m_rest, shapes)))
    new_v.update(zip(rest, _unpack_flat(v_rest, shapes)))

    return (loss_all, grad_x[None], *[grads[n] for n in WEIGHTS], *[delta[n] for n in WEIGHTS],
            *[new_m[n] for n in WEIGHTS], *[new_v[n] for n in WEIGHTS])
```

```python
import functools
import math

import jax
import jax.numpy as jnp
from jax import lax
from jax.experimental import pallas as pl
from jax.experimental.pallas import tpu as pltpu

F32 = jnp.float32
BF16 = jnp.bfloat16

D_MODEL = 1024
CHUNK = 128
N_META = 16
PAD_ROWS = CHUNK - N_META
RET_HEADS = 4
RET_QK_DIM = 128
RET_V_DIM = 256
SSD_HEADS = 32
SSD_HEAD_DIM = 64
SSD_GROUPS = 4
SSD_STATE = 128
HEADS_PER_GROUP = SSD_HEADS // SSD_GROUPS
PAIRS_PER_GROUP = HEADS_PER_GROUP // 2
D_FF = 2816
EPS = 1e-6
ROPE_BASE = 10000.0
N_DEV = 8

ADAM_LR = 0.001
ADAM_B1 = 0.9
ADAM_B2 = 0.999
ADAM_EPS = 1e-08
ADAM_WD = 0.01
ADAM_STEP = 10

VMEM_LIMIT = 56 * 1024 * 1024
HALO = 16
HIGHEST = lax.Precision.HIGHEST

SEGMENTS = (("qk", 0, 1024), ("v", 1024, 2048), ("g", 2048, 3072), ("z", 3072, 5120), ("xs", 5120, 7168),
            ("B", 7168, 7680), ("C", 7680, 8192), ("dt", 8192, 8256), ("gates", 8256, 10304))


def _pick(n, cands):
    for c in cands:
        if n % c == 0:
            return c
    raise ValueError(f"no tile for {n}")


def _params(sem):
    return pltpu.CompilerParams(dimension_semantics=sem, vmem_limit_bytes=VMEM_LIMIT)


def _dot(a, b, dims=(((1,), (0,)), ((), ())), precision=None):
    return lax.dot_general(a, b, dims, preferred_element_type=F32, precision=precision)


def _dot_nt(a, b):
    return _dot(a, b, (((1,), (1,)), ((), ())))


def _dot_tn(a, b):
    return _dot(a, b, (((0,), (0,)), ((), ())))


def _mm(name, a, b, mode, add=None, out_dtype=F32):
    if mode == "nn":
        (M, K), N = a.shape, b.shape[1]
    elif mode == "nt":
        (M, K), N = a.shape, b.shape[0]
    else:
        (K, M), N = a.shape, b.shape[1]
    tn = _pick(N, (1408, 1024, 512, 128, 64))
    if mode == "tn":
        tm = M if M <= 1024 else _pick(M, (1408, 1024))
        tk = _pick(K, (2112, 512, 256, 128))
    else:
        tm = _pick(M, (1056, 512, 256, 128))
        tk = K if K <= 2816 else _pick(K, (2048, 1408, 1024))
    nk = K // tk
    if mode == "nn":
        a_spec = pl.BlockSpec((tm, tk), lambda n, m, k: (m, k))
        b_spec = pl.BlockSpec((tk, tn), lambda n, m, k: (k, n))
        dims = (((1,), (0,)), ((), ()))
    elif mode == "nt":
        a_spec = pl.BlockSpec((tm, tk), lambda n, m, k: (m, k))
        b_spec = pl.BlockSpec((tn, tk), lambda n, m, k: (n, k))
        dims = (((1,), (1,)), ((), ()))
    else:
        a_spec = pl.BlockSpec((tk, tm), lambda n, m, k: (k, m))
        b_spec = pl.BlockSpec((tk, tn), lambda n, m, k: (k, n))
        dims = (((0,), (0,)), ((), ()))
    o_spec = pl.BlockSpec((tm, tn), lambda n, m, k: (m, n))
    in_specs = [a_spec, b_spec] + ([o_spec] if add is not None else [])
    args = [a, b] + ([add] if add is not None else [])

    def body(*refs):
        if add is not None:
            a_ref, b_ref, r_ref, o_ref, acc = refs
        else:
            a_ref, b_ref, o_ref, acc = refs
        k = pl.program_id(2)
        p = _dot(a_ref[...].astype(BF16), b_ref[...].astype(BF16), dims)

        def finish(r):
            if add is not None:
                r = r + r_ref[...]
            o_ref[...] = r.astype(out_dtype)

        if nk == 1:
            finish(p)
        else:
            @pl.when(k == 0)
            def _():
                acc[...] = p

            @pl.when(k > 0)
            def _():
                acc[...] += p

            @pl.when(k == nk - 1)
            def _():
                finish(acc[...])

    return pl.pallas_call(
        body, name=name, grid=(N // tn, M // tm, nk), in_specs=in_specs, out_specs=o_spec,
        out_shape=jax.ShapeDtypeStruct((M, N), out_dtype),
        scratch_shapes=[pltpu.VMEM((tm, tn) if nk > 1 else (8, 128), F32)],
        compiler_params=_params(("arbitrary", "arbitrary", "arbitrary")),
    )(*args)


def _const(c):
    return lambda j: c


def _rows(name, fn, T, ncol, ins, params, outs, accs=(), halo=False):
    tm = _pick(T, (384, 256, 128))
    R = T // tm
    hb = tm // HALO
    in_specs, args = [], []
    for spec in ins:
        arr, w, cf = spec[:3]
        lead = spec[3] if len(spec) > 3 else None
        if lead is None:
            mk = lambda blk, rf, cf=cf: pl.BlockSpec(blk, lambda j, i: (rf(i), cf(j)))
            shape = lambda r, w=w: (r, w)
        else:
            mk = lambda blk, rf, cf=cf, lead=lead: pl.BlockSpec(blk, lambda j, i: (lead, rf(i), cf(j)))
            shape = lambda r, w=w: (None, r, w)
        in_specs.append(mk(shape(tm), lambda i: i))
        args.append(arr)
        if halo:
            in_specs.append(mk(shape(HALO), lambda i: jnp.maximum(i * hb - 1, 0)))
            in_specs.append(mk(shape(HALO), lambda i: jnp.minimum((i + 1) * hb, T // HALO - 1)))
            args += [arr, arr]
    for arr, w, cf in params:
        in_specs.append(pl.BlockSpec((arr.shape[0], w), lambda j, i, cf=cf: (0, cf(j))))
        args.append(arr)
    out_shape, out_specs = [], []
    for tw, w, cf, dt in outs:
        out_shape.append(jax.ShapeDtypeStruct((T, tw), dt))
        out_specs.append(pl.BlockSpec((tm, w), lambda j, i, cf=cf: (i, cf(j))))
    for r, tw, w, cf in accs:
        out_shape.append(jax.ShapeDtypeStruct((r, tw), F32))
        out_specs.append(pl.BlockSpec((r, w), lambda j, i, cf=cf: (0, cf(j))))
    n_in, n_par, n_out, n_acc = len(ins), len(params), len(outs), len(accs)

    def body(*refs):
        i = pl.program_id(1)
        vals, p = [], 0
        for _ in range(n_in):
            if halo:
                before = jnp.where(i > 0, refs[p + 1][...], jnp.zeros_like(refs[p + 1]))
                after = jnp.where(i < R - 1, refs[p + 2][...], jnp.zeros_like(refs[p + 2]))
                vals.append(jnp.concatenate([before, refs[p][...], after], axis=0).astype(F32))
                p += 3
            else:
                vals.append(refs[p][...].astype(F32))
                p += 1
        pvals = [refs[p + k][...] for k in range(n_par)]
        p += n_par
        res = fn(i, *vals, *pvals)
        for k in range(n_out):
            refs[p + k][...] = res[k].astype(refs[p + k].dtype)
        p += n_out
        for k in range(n_acc):
            ref, v = refs[p + k], res[n_out + k]

            @pl.when(i == 0)
            def _(ref=ref, v=v):
                ref[...] = v

            @pl.when(i > 0)
            def _(ref=ref, v=v):
                ref[...] += v

    res = pl.pallas_call(
        body, name=name, grid=(ncol, R), in_specs=in_specs, out_specs=out_specs, out_shape=out_shape,
        compiler_params=_params(("arbitrary", "arbitrary")),
    )(*args)
    return res


def _tile_rows(T):
    return _pick(T, (384, 256, 128))


def _row_ids(i, T, halo=False):
    tm = _tile_rows(T)
    if halo:
        return i * tm - HALO + lax.broadcasted_iota(jnp.int32, (tm + 2 * HALO, 1), 0)
    return i * tm + lax.broadcasted_iota(jnp.int32, (tm, 1), 0)


def _rms(x, w):
    return x * lax.rsqrt(jnp.mean(x * x, axis=-1, keepdims=True) + EPS) * w


def _silu(x):
    return x * jax.nn.sigmoid(x)


def _conv3(x, w):
    n = x.shape[0]
    return w[0:1] * pltpu.roll(x, 1, 0) + w[1:2] * x + w[2:3] * pltpu.roll(x, n - 1, 0)


def _conv3_t(d, w):
    n = d.shape[0]
    return w[0:1] * pltpu.roll(d, n - 1, 0) + w[1:2] * d + w[2:3] * pltpu.roll(d, 1, 0)


def _center(x):
    return x[HALO:x.shape[0] - HALO]


def _retention(name, a, b, v, T, da, dv):
    (a, a0), (b, b0), (v, v0) = [t if isinstance(t, tuple) else (t, 0) for t in (a, b, v)]
    nc = T // CHUNK
    log_gammas = [math.log(1.0 - 2.0 ** (-5.0 - h)) for h in range(RET_HEADS)]

    def body(a_ref, b_ref, v_ref, o_ref, st, st_b):
        h = pl.program_id(0)
        lg = jnp.float32(log_gammas[RET_HEADS - 1])
        for k in range(RET_HEADS - 2, -1, -1):
            lg = jnp.where(h == k, jnp.float32(log_gammas[k]), lg)
        li = lax.broadcasted_iota(jnp.int32, (CHUNK, CHUNK), 0)
        si = lax.broadcasted_iota(jnp.int32, (CHUNK, CHUNK), 1)
        dmat = jnp.exp(lg * jnp.abs(li - si).astype(F32))
        pos = lax.broadcasted_iota(jnp.int32, (CHUNK, 1), 0).astype(F32)
        kdec_f = jnp.exp((CHUNK - 1 - pos) * lg)
        qdec_f = jnp.exp((pos + 1) * lg)
        kdec_b = jnp.exp(pos * lg)
        qdec_b = jnp.exp((CHUNK - pos) * lg)
        cdec = jnp.exp(CHUNK * lg)

        def rows(n):
            return pl.ds(pl.multiple_of(n * CHUNK, CHUNK), CHUNK)

        st[...] = jnp.zeros_like(st)
        st_b[...] = jnp.zeros_like(st_b)
        o_ref[...] = jnp.zeros_like(o_ref)

        def step(m, carry):
            r = rows(m)
            av, bv, vv = a_ref[r, :], b_ref[r, :], v_ref[r, :].astype(BF16)
            s = _dot_nt(av.astype(BF16), bv.astype(BF16)) * dmat
            o_ref[r, :] += _dot(s.astype(BF16), vv) + _dot((av * qdec_f).astype(BF16), st[...].astype(BF16))
            st[...] = cdec * st[...] + _dot_tn((bv * kdec_f).astype(BF16), vv)
            r = rows(nc - 1 - m)
            av, bv, vv = a_ref[r, :], b_ref[r, :], v_ref[r, :].astype(BF16)
            o_ref[r, :] += _dot((av * qdec_b).astype(BF16), st_b[...].astype(BF16))
            st_b[...] = cdec * st_b[...] + _dot_tn((bv * kdec_b).astype(BF16), vv)
            return carry

        lax.fori_loop(0, nc, step, 0, unroll=3 if nc % 3 == 0 else 1)

    return pl.pallas_call(
        body, name=name, grid=(RET_HEADS,),
        in_specs=[pl.BlockSpec((T, da), lambda h: (0, a0 // da + h)), pl.BlockSpec((T, da), lambda h: (0, b0 // da + h)),
                  pl.BlockSpec((T, dv), lambda h: (0, v0 // dv + h))],
        out_specs=pl.BlockSpec((T, dv), lambda h: (0, h)),
        out_shape=jax.ShapeDtypeStruct((T, RET_HEADS * dv), F32),
        scratch_shapes=[pltpu.VMEM((da, dv), F32), pltpu.VMEM((da, dv), F32)],
        compiler_params=_params(("arbitrary",)),
    )(a, b, v)


def _softplus(x):
    return jnp.maximum(x, 0.0) + jnp.log1p(jnp.exp(-jnp.abs(x)))


def _lane_lo():
    return lax.broadcasted_iota(jnp.int32, (1, CHUNK), 1) < SSD_HEAD_DIM


def _pair_cols(col, j):
    return jnp.where(_lane_lo(), col[:, 2 * j:2 * j + 1], col[:, 2 * j + 1:2 * j + 2])


def _pair_rows(colr, j):
    lo = lax.broadcasted_iota(jnp.int32, (CHUNK, 1), 0) < SSD_HEAD_DIM
    return jnp.where(lo, colr[2 * j:2 * j + 1, :], colr[2 * j + 1:2 * j + 2, :])


def _onehot8(h):
    return (lax.broadcasted_iota(jnp.int32, (1, HEADS_PER_GROUP), 1) == h).astype(F32)


def _ssd_pre(d, c, rawc, rawr, bc, br, alc, alr):
    li = lax.broadcasted_iota(jnp.int32, (CHUNK, CHUNK), 0)
    si = lax.broadcasted_iota(jnp.int32, (CHUNK, CHUNK), 1)
    dif = li - si if d == 0 else si - li
    mask = dif >= 0
    mask_t = dif <= 0
    rowc = c * CHUNK + lax.broadcasted_iota(jnp.int32, (CHUNK, 1), 0)
    rowr = c * CHUNK + lax.broadcasted_iota(jnp.int32, (1, CHUNK), 1)
    dtc = jnp.where(rowc >= PAD_ROWS, _softplus(rawc + bc), 0.0)
    dtr = jnp.where(rowr >= PAD_ROWS, _softplus(rawr + br), 0.0)
    ac = -jnp.exp(alc)
    ar = -jnp.exp(alr)
    dlc = dtc * ac
    dlr = dtr * ar
    alpc = _dot(mask.astype(F32), dlc, precision=HIGHEST)
    alpr = _dot(dlr, mask_t.astype(F32), precision=HIGHEST)
    endc = jnp.sum(dlc, axis=0, keepdims=True)
    endr = jnp.sum(dlr, axis=1, keepdims=True)
    return dict(mask=mask, mask_t=mask_t, dtc=dtc, ac=ac, alpc=alpc, alpr=alpr, endc=endc, endr=endr,
                valid=rowc >= PAD_ROWS)


def _chunk_of(d, n, nc):
    return n + d * (nc - 1 - 2 * n)


GROUP_WIDTH = HEADS_PER_GROUP * SSD_HEAD_DIM


def _ssd_in_specs(d, cfn):
    return [
        pl.BlockSpec((CHUNK, GROUP_WIDTH), lambda g, n: (cfn(d, n), g)),
        pl.BlockSpec((CHUNK, SSD_STATE), lambda g, n: (cfn(d, n), g)),
        pl.BlockSpec((CHUNK, SSD_STATE), lambda g, n: (cfn(d, n), g)),
        pl.BlockSpec((None, None, CHUNK, HEADS_PER_GROUP), lambda g, n: (d, g, cfn(d, n), 0)),
        pl.BlockSpec((None, None, HEADS_PER_GROUP, CHUNK), lambda g, n: (d, g, 0, cfn(d, n))),
        pl.BlockSpec((None, None, 1, HEADS_PER_GROUP), lambda g, n: (d, g, 0, 0)),
        pl.BlockSpec((None, None, HEADS_PER_GROUP, 1), lambda g, n: (d, g, 0, 0)),
        pl.BlockSpec((None, None, 1, HEADS_PER_GROUP), lambda g, n: (d, g, 0, 0)),
        pl.BlockSpec((None, None, HEADS_PER_GROUP, 1), lambda g, n: (d, g, 0, 0)),
    ]


N_SSD_IN = 9


def _ssd_fwd(xs, bm, cm, small, T):
    nc = T // CHUNK
    cfn = lambda d, n: _chunk_of(d, n, nc)

    def one_direction(d, n, ins, y_ref, hs_ref, h_scr):
        x_ref, b_ref, c_ref, rawc_ref, rawr_ref, bc_ref, br_ref, alc_ref, alr_ref = ins
        c = cfn(d, n)
        q = _ssd_pre(d, c, rawc_ref[...], rawr_ref[...], bc_ref[...], br_ref[...], alc_ref[...], alr_ref[...])
        bv = b_ref[...].astype(BF16)
        cv = c_ref[...].astype(BF16)
        cb = _dot_nt(cv, bv)
        lo = _lane_lo()
        for j in range(PAIRS_PER_GROUP):
            xp = x_ref[:, j * CHUNK:(j + 1) * CHUNK]
            xd = xp * _pair_cols(q["dtc"], j)
            xdb = xd.astype(BF16)
            yi = []
            for e in range(2):
                h = 2 * j + e
                lm = jnp.exp(jnp.where(q["mask"], q["alpc"][:, h:h + 1] - q["alpr"][h:h + 1, :], -jnp.inf))
                yi.append(_dot((cb * lm).astype(BF16), xdb))
            alp = _pair_cols(q["alpc"], j)
            hp = h_scr[j]
            hs_ref[j] = hp
            yo = jnp.exp(alp) * _dot_nt(cv, hp.astype(BF16))
            y_ref[:, j * CHUNK:(j + 1) * CHUNK] = (jnp.where(lo, yi[0], yi[1]) + yo).astype(y_ref.dtype)
            de = jnp.exp(_pair_cols(q["endc"], j) - alp)
            h_scr[j] = jnp.exp(_pair_rows(q["endr"], j)) * hp + _dot_tn((xd * de).astype(BF16), bv)

    def body(*refs):
        n = pl.program_id(1)
        ins, (y_f, y_b, hs_f, hs_b, h_scr) = refs[:2 * N_SSD_IN], refs[2 * N_SSD_IN:]

        @pl.when(n == 0)
        def _():
            h_scr[...] = jnp.zeros_like(h_scr)

        one_direction(0, n, ins[:N_SSD_IN], y_f, hs_f, h_scr.at[0])
        one_direction(1, n, ins[N_SSD_IN:], y_b, hs_b, h_scr.at[1])

    y_spec = lambda d: pl.BlockSpec((CHUNK, GROUP_WIDTH), lambda g, n: (cfn(d, n), g))
    hs_spec = lambda d: pl.BlockSpec((None, None, PAIRS_PER_GROUP, CHUNK, SSD_STATE),
                                     lambda g, n: (g, cfn(d, n), 0, 0, 0))
    y_shape = jax.ShapeDtypeStruct((T, SSD_HEADS * SSD_HEAD_DIM), BF16)
    hs_shape = jax.ShapeDtypeStruct((SSD_GROUPS, nc, PAIRS_PER_GROUP, CHUNK, SSD_STATE), F32)
    y_f, y_b, hs_f, hs_b = pl.pallas_call(
        body, name="ssd_fwd", grid=(SSD_GROUPS, nc),
        in_specs=_ssd_in_specs(0, cfn) + _ssd_in_specs(1, cfn),
        out_specs=[y_spec(0), y_spec(1), hs_spec(0), hs_spec(1)],
        out_shape=[y_shape, y_shape, hs_shape, hs_shape],
        scratch_shapes=[pltpu.VMEM((2, PAIRS_PER_GROUP, CHUNK, SSD_STATE), F32)],
        compiler_params=_params(("arbitrary", "arbitrary")),
    )(xs, bm, cm, *small, xs, bm, cm, *small)
    return (y_f, y_b), (hs_f, hs_b)


def _ssd_bwd(xs, bm, cm, small, hs, dy, T):
    nc = T // CHUNK
    cfn = lambda d, n: _chunk_of(1 - d, n, nc)

    def one_direction(d, n, ins, outs, dh_scr):
        x_ref, b_ref, c_ref, rawc_ref, rawr_ref, bc_ref, br_ref, alc_ref, alr_ref, hs_ref, dy_ref = ins
        dx_ref, db_ref, dc_ref, draw_ref, dbias_ref, dalog_ref = outs
        c = cfn(d, n)
        rawc, bc = rawc_ref[...], bc_ref[...]
        q = _ssd_pre(d, c, rawc, rawr_ref[...], bc, br_ref[...], alc_ref[...], alr_ref[...])
        b32, c32 = b_ref[...], c_ref[...]
        bv, cv = b32.astype(BF16), c32.astype(BF16)
        cb = _dot_nt(cv, bv)
        cbt = _dot_nt(bv, cv)
        lo = _lane_lo()
        row_lo = lax.broadcasted_iota(jnp.int32, (CHUNK, 1), 0) < SSD_HEAD_DIM
        dcb = jnp.zeros((CHUNK, CHUNK), F32)
        dcp = jnp.zeros((CHUNK, SSD_STATE), F32)
        dbp = jnp.zeros((CHUNK, SSD_STATE), F32)
        dalp = jnp.zeros((CHUNK, HEADS_PER_GROUP), F32)
        dend = jnp.zeros((1, HEADS_PER_GROUP), F32)
        ddtx = jnp.zeros((CHUNK, HEADS_PER_GROUP), F32)

        def half_sums(t):
            return (jnp.sum(jnp.where(lo, t, 0.0), axis=1, keepdims=True),
                    jnp.sum(jnp.where(lo, 0.0, t), axis=1, keepdims=True))

        for j in range(PAIRS_PER_GROUP):
            xp = x_ref[:, j * CHUNK:(j + 1) * CHUNK]
            dtp = _pair_cols(q["dtc"], j)
            xd = xp * dtp
            xdb = xd.astype(BF16)
            dyp = dy_ref[:, j * CHUNK:(j + 1) * CHUNK]
            dyb = dyp.astype(BF16)
            hn = hs_ref[j]
            hnb = hn.astype(BF16)
            dh1 = dh_scr[j]
            dh1b = dh1.astype(BF16)
            alp = _pair_cols(q["alpc"], j)
            ea = jnp.exp(alp)
            de = jnp.exp(_pair_cols(q["endc"], j) - alp)
            dxi = []
            for e in range(2):
                h = 2 * j + e
                diff = q["alpc"][:, h:h + 1] - q["alpr"][h:h + 1, :]
                lm = jnp.exp(jnp.where(q["mask"], diff, -jnp.inf))
                mt = cbt * jnp.exp(jnp.where(q["mask_t"], -diff, -jnp.inf))
                dxi.append(_dot(mt.astype(BF16), dyb))
                dyeb_h = (jnp.where(lo, dyp, 0.0) if e == 0 else jnp.where(lo, 0.0, dyp)).astype(BF16)
                gl = _dot_nt(dyeb_h, xdb) * lm
                dcb = dcb + gl
                ra = jnp.sum(gl * cb - _dot_nt(xdb, dyeb_h) * mt, axis=1, keepdims=True)
                dalp = dalp + ra * _onehot8(h)
            y_off = ea * _dot_nt(cv, hnb)
            dxs_state = de * _dot_nt(bv, dh1b)
            dxd = jnp.where(lo, dxi[0], dxi[1]) + dxs_state
            dyeb = (dyp * ea).astype(BF16)
            dcp = dcp + _dot(dyeb, hnb)
            dbp = dbp + _dot((xd * de).astype(BF16), dh1b)
            dh_scr[j] = jnp.exp(_pair_rows(q["endr"], j)) * dh1 + _dot_tn(dyeb, cv)
            r0, r1 = half_sums(dyp * y_off - xd * dxs_state)
            dalp = dalp + r0 * _onehot8(2 * j) + r1 * _onehot8(2 * j + 1)
            t0, t1 = half_sums(jnp.sum(xd * dxs_state, axis=0, keepdims=True))
            u = dh1 * hn
            u0 = jnp.sum(jnp.sum(jnp.where(row_lo, u, 0.0), axis=0, keepdims=True), axis=1, keepdims=True)
            u1 = jnp.sum(jnp.sum(jnp.where(row_lo, 0.0, u), axis=0, keepdims=True), axis=1, keepdims=True)
            eend = jnp.exp(q["endc"])
            dend = dend + (t0 + eend * u0) * _onehot8(2 * j) + (t1 + eend * u1) * _onehot8(2 * j + 1)
            dx_ref[:, j * CHUNK:(j + 1) * CHUNK] = (dxd * dtp).astype(dx_ref.dtype)
            w0, w1 = half_sums(dxd * xp)
            ddtx = ddtx + w0 * _onehot8(2 * j) + w1 * _onehot8(2 * j + 1)

        dcbb = dcb.astype(BF16)
        dc_ref[...] = (dcp + _dot(dcbb, bv)).astype(dc_ref.dtype)
        db_ref[...] = (dbp + _dot_tn(dcbb, cv)).astype(db_ref.dtype)
        ddl = _dot(q["mask_t"].astype(F32), dalp, precision=HIGHEST) + dend
        ddt = ddl * q["ac"] + ddtx
        draw = jnp.where(q["valid"], ddt * jax.nn.sigmoid(rawc + bc), 0.0)
        draw_ref[...] = draw
        dbias = jnp.sum(draw, axis=0, keepdims=True)
        dalog = jnp.sum(ddl * q["dtc"], axis=0, keepdims=True) * q["ac"]

        @pl.when(n == 0)
        def _():
            dbias_ref[...] = dbias
            dalog_ref[...] = dalog

        @pl.when(n > 0)
        def _():
            dbias_ref[...] += dbias
            dalog_ref[...] += dalog

    n_in, n_out = N_SSD_IN + 2, 6

    def body(*refs):
        n = pl.program_id(1)
        ins, outs, dh_scr = refs[:2 * n_in], refs[2 * n_in:2 * (n_in + n_out)], refs[-1]

        @pl.when(n == 0)
        def _():
            dh_scr[...] = jnp.zeros_like(dh_scr)

        one_direction(0, n, ins[:n_in], outs[:n_out], dh_scr.at[0])
        one_direction(1, n, ins[n_in:], outs[n_out:], dh_scr.at[1])

    def in_specs(d):
        return _ssd_in_specs(d, cfn) + [
            pl.BlockSpec((None, None, PAIRS_PER_GROUP, CHUNK, SSD_STATE), lambda g, n: (g, cfn(d, n), 0, 0, 0)),
            pl.BlockSpec((CHUNK, GROUP_WIDTH), lambda g, n: (cfn(d, n), g))]

    def out_specs(d):
        acc = pl.BlockSpec((None, 1, HEADS_PER_GROUP), lambda g, n: (g, 0, 0))
        return [pl.BlockSpec((CHUNK, GROUP_WIDTH), lambda g, n: (cfn(d, n), g)),
                pl.BlockSpec((CHUNK, SSD_STATE), lambda g, n: (cfn(d, n), g)),
                pl.BlockSpec((CHUNK, SSD_STATE), lambda g, n: (cfn(d, n), g)),
                pl.BlockSpec((None, CHUNK, HEADS_PER_GROUP), lambda g, n: (g, cfn(d, n), 0)), acc, acc]

    out_shape = [jax.ShapeDtypeStruct((T, SSD_HEADS * SSD_HEAD_DIM), BF16),
                 jax.ShapeDtypeStruct((T, SSD_GROUPS * SSD_STATE), BF16),
                 jax.ShapeDtypeStruct((T, SSD_GROUPS * SSD_STATE), BF16),
                 jax.ShapeDtypeStruct((SSD_GROUPS, T, HEADS_PER_GROUP), F32),
                 jax.ShapeDtypeStruct((SSD_GROUPS, 1, HEADS_PER_GROUP), F32),
                 jax.ShapeDtypeStruct((SSD_GROUPS, 1, HEADS_PER_GROUP), F32)]
    res = pl.pallas_call(
        body, name="ssd_bwd", grid=(SSD_GROUPS, nc),
        in_specs=in_specs(0) + in_specs(1), out_specs=out_specs(0) + out_specs(1), out_shape=out_shape * 2,
        scratch_shapes=[pltpu.VMEM((2, PAIRS_PER_GROUP, CHUNK, SSD_STATE), F32)],
        compiler_params=_params(("arbitrary", "arbitrary")),
    )(xs, bm, cm, *small, hs[0], dy, xs, bm, cm, *small, hs[1], dy)
    return [(res[k], res[n_out + k]) for k in range(n_out)]


def _rot(x, cs, sn):
    return x * cs + pltpu.roll(x, RET_QK_DIM // 2, 1) * sn


def _rot_t(d, cs, sn):
    return d * cs + pltpu.roll(d * sn, RET_QK_DIM // 2, 1)


def _ret_post(y, g, w):
    parts = []
    for h in range(RET_HEADS):
        yh = y[:, h * RET_V_DIM:(h + 1) * RET_V_DIM]
        mu = jnp.mean(yh, axis=-1, keepdims=True)
        var = jnp.mean(jnp.square(yh - mu), axis=-1, keepdims=True)
        parts.append((yh - mu) * lax.rsqrt(var + EPS))
    return _silu(g) * (jnp.concatenate(parts, axis=1) * w)


def _ssd_post(yf, yb, xs, z, dskip, w):
    y = (yf + yb + xs * dskip) * _silu(z)
    return y * lax.rsqrt(jnp.mean(y * y, axis=-1, keepdims=True) + EPS) * w


def _merge(gates, yr, ys, valid):
    m = jax.nn.sigmoid(gates[:, :D_MODEL]) * yr + jax.nn.sigmoid(gates[:, D_MODEL:]) * ys
    return jnp.where(valid, m, 0.0)


def _rope_tables(T):
    half = RET_QK_DIM // 2
    inv = ROPE_BASE ** (-jnp.arange(half, dtype=F32) / half)
    pos = (jnp.arange(T) - PAD_ROWS).astype(F32)
    ang = pos[:, None] * inv[None, :]
    cos, sin = jnp.cos(ang), jnp.sin(ang)
    return jnp.concatenate([cos, cos], axis=1), jnp.concatenate([-sin, sin], axis=1)


def _per_group(v):
    c = v.reshape(SSD_GROUPS, 1, HEADS_PER_GROUP)
    return c, c.reshape(SSD_GROUPS, HEADS_PER_GROUP, 1)


def _local_step(x, target, w, tick, late_weights, early_grads, in_grads):
    S = x.shape[0]
    T = S + CHUNK
    tm = _tile_rows(T)
    c0 = _const(0)

    h0 = jnp.concatenate([jnp.zeros((PAD_ROWS, D_MODEL), F32), w["meta_tokens"], x], axis=0)
    tgt = jnp.concatenate([jnp.zeros((CHUNK, D_MODEL), F32), target], axis=0)
    seg_at = {name: a for name, a, _ in SEGMENTS}
    w_main = w["w_in_t"][:seg_at["dt"]]
    w_dt = jnp.pad(w["w_in_t"][seg_at["dt"]:seg_at["gates"]], ((0, CHUNK - 2 * SSD_HEADS), (0, 0)))
    w_gates = w["w_in_t"][seg_at["gates"]:]

    def norm_cast(name, h, nw):
        return _rows(name, lambda i, hv, wv: (_rms(hv, wv),), T, 1, [(h, D_MODEL, c0)], [(nw, D_MODEL, c0)],
                     [(D_MODEL, D_MODEL, c0, BF16)])[0]

    u = norm_cast("norm_mix", h0, w["norm_mix_w"] + tick)
    p_main = _mm("proj_main", u, w_main, "nt", out_dtype=BF16)
    p_dt = _mm("proj_dt", u, w_dt, "nt")
    p_gates = _mm("proj_gates", u, w_gates, "nt", out_dtype=BF16)

    def seg(name, width, cf=c0):
        base = seg_at[name] // width
        return (p_main, width, lambda j: base + cf(j))

    cs, sn = _rope_tables(T)
    scale = RET_QK_DIM ** -0.5

    def rot_fn(i, qk, csv, snv):
        q = [_rot(qk[:, h * 128:(h + 1) * 128], csv, snv) for h in range(RET_HEADS)]
        k = [_rot(qk[:, (RET_HEADS + h) * 128:(RET_HEADS + h + 1) * 128], csv, snv) * scale for h in range(RET_HEADS)]
        return jnp.concatenate(q, axis=1), jnp.concatenate(k, axis=1)

    qr, kr = _rows("rotary", rot_fn, T, 1, [seg("qk", 1024), (cs, 128, c0), (sn, 128, c0)], [],
                   [(512, 512, c0, F32), (512, 512, c0, F32)])
    v_at = (p_main, seg_at["v"])
    y_ret = _retention("retention", qr, kr, v_at, T, RET_QK_DIM, RET_V_DIM)
    a_ret = _rows("ret_post", lambda i, y, g, gw: (_ret_post(y, g, gw),), T, 1,
                  [(y_ret, 1024, c0), seg("g", 1024)], [(w["ret_gn_w"], 1024, c0)],
                  [(1024, 1024, c0, BF16)])[0]

    conv_w = {"xs": w["w_ssd_conv"][:, :2048], "B": w["w_ssd_conv"][:, 2048:2560], "C": w["w_ssd_conv"][:, 2560:]}
    conv_b = {"xs": w["b_ssd_conv"][:, :2048], "B": w["b_ssd_conv"][:, 2048:2560], "C": w["b_ssd_conv"][:, 2560:]}

    def ssd_conv_fn(i, xe, cw, cb):
        r = _row_ids(i, T, True)
        return (_center(jnp.where(r >= PAD_ROWS, _silu(_conv3(xe, cw) + cb), 0.0)),)

    act = {}
    for name in ("xs", "B", "C"):
        wd = conv_w[name].shape[1]
        cw = 512
        act[name] = _rows("ssd_conv_" + name, ssd_conv_fn, T, wd // cw, [seg(name, cw, lambda j: j)],
                          [(conv_w[name], cw, lambda j: j), (conv_b[name], cw, lambda j: j)],
                          [(wd, cw, lambda j: j, BF16)], halo=True)[0]

    raw = p_dt[:, :2 * SSD_HEADS].reshape(T, 2, SSD_GROUPS, HEADS_PER_GROUP)
    rawc = raw.transpose(1, 2, 0, 3)
    rawr = raw.transpose(1, 2, 3, 0)
    bias = [_per_group(w["dt_bias_f"]), _per_group(w["dt_bias_b"])]
    alog = [_per_group(w["a_log_f"]), _per_group(w["a_log_b"])]
    small = (rawc, rawr, jnp.stack([bias[0][0], bias[1][0]]), jnp.stack([bias[0][1], bias[1][1]]),
             jnp.stack([alog[0][0], alog[1][0]]), jnp.stack([alog[0][1], alog[1][1]]))
    y_dir, states = _ssd_fwd(act["xs"], act["B"], act["C"], small, T)

    dskip_e = jnp.repeat(w["d_skip"], SSD_HEAD_DIM, axis=1)
    gcol = lambda j: j
    gw_ = 512
    a_ssd = _rows("ssd_post", lambda i, yf, yb, xv, zv, dk, nw: (_ssd_post(yf, yb, xv, zv, dk, nw),), T, SSD_GROUPS,
                  [(y_dir[0], gw_, gcol), (y_dir[1], gw_, gcol), (act["xs"], gw_, gcol), seg("z", gw_, gcol)],
                  [(dskip_e, gw_, gcol), (w["ssd_norm_w"], gw_, gcol)], [(2048, gw_, gcol, BF16)])[0]

    w = dict(w, **late_weights(a_ssd))
    w_up_g, w_up_u = w["w_ffn_up_t"][:D_FF], w["w_ffn_up_t"][D_FF:]
    y_ret_o = _mm("ret_out", a_ret, w["w_ret_out"], "nn", out_dtype=BF16)
    y_ssd_o = _mm("ssd_out", a_ssd, w["w_ssd_out"], "nn", out_dtype=BF16)

    def merge_fn(i, gates, yr, ys):
        return (_merge(gates, yr, ys, _row_ids(i, T) >= PAD_ROWS),)

    merged = _rows("merge", merge_fn, T, 1, [(p_gates, 2048, c0), (y_ret_o, 1024, c0), (y_ssd_o, 1024, c0)], [],
                   [(1024, 1024, c0, BF16)])[0]
    h1 = _mm("mix_out", merged, w["w_out"], "nn", add=h0)

    n2 = norm_cast("norm_ffn", h1, w["norm_ffn_w"])
    fg_pre = _mm("ffn_up_g", n2, w_up_g, "nt", out_dtype=BF16)
    fu_pre = _mm("ffn_up_u", n2, w_up_u, "nt", out_dtype=BF16)
    cwg, cwu = w["w_ffn_conv"][:, :D_FF], w["w_ffn_conv"][:, D_FF:]
    cbg, cbu = w["b_ffn_conv"][:, :D_FF], w["b_ffn_conv"][:, D_FF:]
    fcol = lambda j: j
    fw = 1408

    def ffn_act_fn(i, ge, ue, wg, wu, bg, bu):
        return (_center(_silu(_conv3(ge, wg) + bg) * (_conv3(ue, wu) + bu)),)

    a2 = _rows("ffn_act", ffn_act_fn, T, D_FF // fw, [(fg_pre, fw, fcol), (fu_pre, fw, fcol)],
               [(cwg, fw, fcol), (cwu, fw, fcol), (cbg, fw, fcol), (cbu, fw, fcol)], [(D_FF, fw, fcol, BF16)],
               halo=True)[0]
    h2 = _mm("ffn_down", a2, w["w_ffn_down"], "nn", add=h1)

    fnw = w["final_norm_w"].reshape(1, D_MODEL)

    def loss_fn(i, hv, tv, nw):
        valid = _row_ids(i, T) >= CHUNK
        y, vjp = jax.vjp(_rms, hv, nw)
        diff = jnp.where(valid, y - tv, 0.0)
        dh, dw = vjp(diff * (1.0 / D_MODEL))
        part = 0.5 / D_MODEL * jnp.sum(jnp.sum(diff * diff, axis=1, keepdims=True), axis=0, keepdims=True)
        return dh, jnp.broadcast_to(part, (1, 128)), dw

    dh2, loss_acc, d_fnw = _rows("loss", loss_fn, T, 1, [(h2, D_MODEL, c0), (tgt, D_MODEL, c0)], [(fnw, D_MODEL, c0)],
                                 [(D_MODEL, D_MODEL, c0, F32)], [(1, 128, 128, c0), (1, D_MODEL, D_MODEL, c0)])
    loss = loss_acc[0, 0]
    grads = {"final_norm_w": d_fnw.reshape(D_MODEL)}

    da2 = _mm("d_ffn_act", dh2, w["w_ffn_down"], "nt", out_dtype=BF16)
    grads["w_ffn_down"] = _mm("g_ffn_down", a2, dh2, "tn", out_dtype=BF16)

    def ffn_bwd_fn(i, ge, ue, de, wg, wu, bg, bu):
        fg = _conv3(ge, wg) + bg
        fu = _conv3(ue, wu) + bu
        sg = jax.nn.sigmoid(fg)
        dfg = de * fu * (sg * (1.0 + fg * (1.0 - sg)))
        dfu = de * (fg * sg)
        n = ge.shape[0]

        def wgrad(df, xe):
            df_c = _center(df)
            return jnp.concatenate([jnp.sum(df_c * _center(pltpu.roll(xe, 1, 0)), axis=0, keepdims=True),
                                    jnp.sum(df_c * _center(xe), axis=0, keepdims=True),
                                    jnp.sum(df_c * _center(pltpu.roll(xe, n - 1, 0)), axis=0, keepdims=True)], axis=0)

        return (_center(_conv3_t(dfg, wg)), _center(_conv3_t(dfu, wu)), wgrad(dfg, ge), wgrad(dfu, ue),
                jnp.sum(_center(dfg), axis=0, keepdims=True), jnp.sum(_center(dfu), axis=0, keepdims=True))

    dfg_pre, dfu_pre, g_cwg, g_cwu, g_cbg, g_cbu = _rows(
        "ffn_act_bwd", ffn_bwd_fn, T, D_FF // fw, [(fg_pre, fw, fcol), (fu_pre, fw, fcol), (da2, fw, fcol)],
        [(cwg, fw, fcol), (cwu, fw, fcol), (cbg, fw, fcol), (cbu, fw, fcol)],
        [(D_FF, fw, fcol, BF16), (D_FF, fw, fcol, BF16)],
        [(3, D_FF, fw, fcol), (3, D_FF, fw, fcol), (1, D_FF, fw, fcol), (1, D_FF, fw, fcol)], halo=True)
    grads["w_ffn_conv"] = jnp.concatenate([g_cwg, g_cwu], axis=1)
    grads["b_ffn_conv"] = jnp.concatenate([g_cbg, g_cbu], axis=1)
    dn2 = _mm("d_norm_ffn_g", dfg_pre, w_up_g, "nn")
    dn2 = _mm("d_norm_ffn_u", dfu_pre, w_up_u, "nn", add=dn2)
    grads["w_ffn_up_t"] = jnp.concatenate([_mm("g_ffn_up_g", dfg_pre, n2, "tn", out_dtype=BF16), _mm("g_ffn_up_u", dfu_pre, n2, "tn", out_dtype=BF16)],
                                          axis=0)

    def norm_bwd(name, h, nw, dn, dres):
        def fn(i, hv, dnv, drv, wv):
            _, vjp = jax.vjp(_rms, hv, wv)
            dh, dw = vjp(dnv)
            return dh + drv, dw
        return _rows(name, fn, T, 1, [(h, D_MODEL, c0), (dn, D_MODEL, c0), (dres, D_MODEL, c0)], [(nw, D_MODEL, c0)],
                     [(D_MODEL, D_MODEL, c0, F32)], [(1, D_MODEL, D_MODEL, c0)])

    dh1, grads["norm_ffn_w"] = norm_bwd("norm_ffn_bwd", h1, w["norm_ffn_w"], dn2, dh2)

    dmerged = _mm("d_merged", dh1, w["w_out"], "nt", out_dtype=BF16)
    grads["w_out"] = _mm("g_out", merged, dh1, "tn", out_dtype=BF16)

    def merge_bwd_fn(i, gates, yr, ys, dm):
        valid = _row_ids(i, T) >= PAD_ROWS
        _, vjp = jax.vjp(lambda a, b, c: _merge(a, b, c, valid), gates, yr, ys)
        return vjp(dm)

    dgates, dyr, dys = _rows("merge_bwd", merge_bwd_fn, T, 1,
                             [(p_gates, 2048, c0), (y_ret_o, 1024, c0), (y_ssd_o, 1024, c0), (dmerged, 1024, c0)],
                             [], [(2048, 2048, c0, BF16), (1024, 1024, c0, BF16), (1024, 1024, c0, BF16)])
    dproj = {"gates": dgates}

    da_ssd = _mm("d_ssd_act", dys, w["w_ssd_out"], "nt", out_dtype=BF16)
    grads["w_ssd_out"] = _mm("g_ssd_out", a_ssd, dys, "tn", out_dtype=BF16)

    def ssd_post_bwd_fn(i, yf, yb, xv, zv, da, dk, nw):
        _, vjp = jax.vjp(_ssd_post, yf, yb, xv, zv, dk, nw)
        dyf, _, dxv, dzv, ddk, dnw = vjp(da)
        return dyf, dxv, dzv, ddk, dnw

    dy_ssd, dxs_skip, dproj["z"], g_dskip_e, grads["ssd_norm_w"] = _rows(
        "ssd_post_bwd", ssd_post_bwd_fn, T, SSD_GROUPS,
        [(y_dir[0], gw_, gcol), (y_dir[1], gw_, gcol), (act["xs"], gw_, gcol), seg("z", gw_, gcol),
         (da_ssd, gw_, gcol)],
        [(dskip_e, gw_, gcol), (w["ssd_norm_w"], gw_, gcol)],
        [(2048, gw_, gcol, BF16), (2048, gw_, gcol, BF16), (2048, gw_, gcol, BF16)],
        [(1, 2048, gw_, gcol), (1, 2048, gw_, gcol)])
    grads["d_skip"] = g_dskip_e.reshape(SSD_HEADS, SSD_HEAD_DIM).sum(axis=1).reshape(1, SSD_HEADS)

    dxs_dir, db_dir, dc_dir, draw, g_bias, g_alog = _ssd_bwd(act["xs"], act["B"], act["C"], small, states, dy_ssd, T)
    grads["dt_bias_f"], grads["dt_bias_b"] = g_bias[0].reshape(1, SSD_HEADS), g_bias[1].reshape(1, SSD_HEADS)
    grads["a_log_f"], grads["a_log_b"] = g_alog[0].reshape(1, SSD_HEADS), g_alog[1].reshape(1, SSD_HEADS)
    d_dt = jnp.stack(draw).transpose(2, 0, 1, 3).reshape(T, 2 * SSD_HEADS)
    dproj["dt"] = jnp.pad(d_dt, ((0, 0), (0, CHUNK - 2 * SSD_HEADS))).astype(BF16)

    def make_conv_bwd(nsum):
        def fn(i, xe, *rest):
            ds, (cw, cb) = rest[:nsum], rest[nsum:]
            r = _row_ids(i, T, True)
            dact = ds[0]
            for t in ds[1:]:
                dact = dact + t
            dact = jnp.where(r >= PAD_ROWS, dact, 0.0)
            pre = _conv3(xe, cw) + cb
            sg = jax.nn.sigmoid(pre)
            dpre = dact * (sg * (1.0 + pre * (1.0 - sg)))
            n = xe.shape[0]
            dpc = _center(dpre)
            dw = jnp.concatenate([jnp.sum(dpc * _center(pltpu.roll(xe, 1, 0)), axis=0, keepdims=True),
                                  jnp.sum(dpc * _center(xe), axis=0, keepdims=True),
                                  jnp.sum(dpc * _center(pltpu.roll(xe, n - 1, 0)), axis=0, keepdims=True)], axis=0)
            return _center(_conv3_t(dpre, cw)), dw, jnp.sum(dpc, axis=0, keepdims=True)
        return fn

    g_cw, g_cb = {}, {}
    cots = {"xs": [(dxs_dir[0], 512, gcol), (dxs_dir[1], 512, gcol), (dxs_skip, 512, gcol)],
            "B": [(db_dir[0], 512, gcol), (db_dir[1], 512, gcol)],
            "C": [(dc_dir[0], 512, gcol), (dc_dir[1], 512, gcol)]}
    for name in ("xs", "B", "C"):
        wd = conv_w[name].shape[1]
        dproj[name], g_cw[name], g_cb[name] = _rows(
            "ssd_conv_bwd_" + name, make_conv_bwd(len(cots[name])), T, wd // 512,
            [seg(name, 512, gcol)] + cots[name], [(conv_w[name], 512, gcol), (conv_b[name], 512, gcol)],
            [(wd, 512, gcol, BF16)], [(3, wd, 512, gcol), (1, wd, 512, gcol)], halo=True)
    grads["w_ssd_conv"] = jnp.concatenate([g_cw["xs"], g_cw["B"], g_cw["C"]], axis=1)
    grads["b_ssd_conv"] = jnp.concatenate([g_cb["xs"], g_cb["B"], g_cb["C"]], axis=1)

    da_ret = _mm("d_ret_act", dyr, w["w_ret_out"], "nt", out_dtype=BF16)
    grads["w_ret_out"] = _mm("g_ret_out", a_ret, dyr, "tn", out_dtype=BF16)
    tick = early_grads({n: grads.pop(n) for n in ("w_ffn_up_t", "w_ret_out", "w_ssd_out", "w_out", "w_ffn_down")})

    def ret_post_bwd_fn(i, y, g, da, gw):
        _, vjp = jax.vjp(_ret_post, y, g, gw)
        return vjp(da)

    dy_ret, dproj["g"], grads["ret_gn_w"] = _rows(
        "ret_post_bwd", ret_post_bwd_fn, T, 1, [(y_ret, 1024, c0), seg("g", 1024), (da_ret, 1024, c0)],
        [(w["ret_gn_w"] + tick, 1024, c0)], [(1024, 1024, c0, BF16), (1024, 1024, c0, BF16)], [(1, 1024, 1024, c0)])
    dproj["v"] = _retention("retention_dv", kr, qr, dy_ret, T, RET_QK_DIM, RET_V_DIM)
    dqr = _retention("retention_dq", dy_ret, v_at, kr, T, RET_V_DIM, RET_QK_DIM)
    dkr = _retention("retention_dk", v_at, dy_ret, qr, T, RET_V_DIM, RET_QK_DIM)

    def rot_bwd_fn(i, dq, dk, csv, snv):
        parts = [_rot_t(dq[:, h * 128:(h + 1) * 128], csv, snv) for h in range(RET_HEADS)]
        parts += [_rot_t(dk[:, h * 128:(h + 1) * 128] * scale, csv, snv) for h in range(RET_HEADS)]
        return (jnp.concatenate(parts, axis=1),)

    dproj["qk"] = _rows("rotary_bwd", rot_bwd_fn, T, 1, [(dqr, 512, c0), (dkr, 512, c0), (cs, 128, c0), (sn, 128, c0)],
                        [], [(1024, 1024, c0, BF16)])[0]

    d_main = jnp.concatenate([dproj[name].astype(BF16) for name, _, _ in SEGMENTS[:7]], axis=1)
    g_in = [_mm("g_in_main", d_main, u, "tn", out_dtype=BF16),
            _mm("g_in_dt", dproj["dt"], u, "tn", out_dtype=BF16)[:2 * SSD_HEADS],
            _mm("g_in_gates", dproj["gates"], u, "tn", out_dtype=BF16)]
    tick = in_grads(jnp.concatenate(g_in, axis=0))
    du = _mm("d_u_dt", dproj["dt"] + tick.astype(BF16), w_dt, "nn")
    du = _mm("d_u_main", d_main, w_main, "nn", add=du)
    du = _mm("d_u_gates", dproj["gates"], w_gates, "nn", add=du)
    dh0, grads["norm_mix_w"] = norm_bwd("norm_mix_bwd", h0, w["norm_mix_w"], du, dh1)
    grads["meta_tokens"] = dh0[PAD_ROWS:CHUNK]
    return loss, dh0[CHUNK:], grads


MESH_ID = pl.DeviceIdType.MESH
ANY = pl.BlockSpec(memory_space=pl.ANY)


def _me_and_peers():
    x, y, c = lax.axis_index("x"), lax.axis_index("y"), lax.axis_index("c")
    peers = []
    for k in range(1, N_DEV):
        px = 1 - x if k & 4 else x
        py = 1 - y if k & 2 else y
        pc = 1 - c if k & 1 else c
        peers.append(((px, py, pc), 4 * px + 2 * py + pc))
    return 4 * x + 2 * y + c, peers


def _push_blocks(name, src, per_peer):
    blk = src.shape[1:] if per_peer else src.shape

    def body(src_ref, out_ref, send_sems, recv_sems, local_sem):
        me, peers = _me_and_peers()
        mine = src_ref.at[me] if per_peer else src_ref
        local = pltpu.make_async_copy(mine, out_ref.at[me], local_sem)
        local.start()
        sends = []
        for k, (dev, idx) in enumerate(peers):
            cp = pltpu.make_async_remote_copy(
                src_ref=src_ref.at[idx] if per_peer else src_ref, dst_ref=out_ref.at[me],
                send_sem=send_sems.at[k], recv_sem=recv_sems.at[k], device_id=dev, device_id_type=MESH_ID)
            cp.start()
            sends.append(cp)
        for k, (dev, idx) in enumerate(peers):
            pltpu.make_async_remote_copy(
                src_ref=mine, dst_ref=out_ref.at[idx], send_sem=send_sems.at[k], recv_sem=recv_sems.at[k],
                device_id=dev, device_id_type=MESH_ID).wait_recv()
        for cp in sends:
            cp.wait_send()
        local.wait()

    return pl.pallas_call(
        body, name=name, in_specs=[ANY], out_specs=ANY,
        out_shape=jax.ShapeDtypeStruct((N_DEV,) + tuple(blk), src.dtype),
        scratch_shapes=[pltpu.SemaphoreType.DMA((N_DEV - 1,)), pltpu.SemaphoreType.DMA((N_DEV - 1,)),
                        pltpu.SemaphoreType.DMA],
    )(src)


def _gather_two_level(name, src):
    def body(x_ref, out_ref, send_sems, recv_sems, local_sem):
        x, y, c = lax.axis_index("x"), lax.axis_index("y"), lax.axis_index("c")
        me, sibling = (x, y, c), (x, y, 1 - c)
        chips = [(1 - x, y), (x, 1 - y), (1 - x, 1 - y)]

        def rows(px, py, pc):
            return out_ref.at[4 * px + 2 * py + pc]

        def copy(k, block, to, src_ref=None):
            return pltpu.make_async_remote_copy(
                src_ref=rows(*block) if src_ref is None else src_ref, dst_ref=rows(*block),
                send_sem=send_sems.at[k], recv_sem=recv_sems.at[k], device_id=to, device_id_type=MESH_ID)

        mine = pltpu.make_async_copy(x_ref, rows(*me), local_sem)
        mine.start()
        first = [copy(0, me, sibling, x_ref)] + [copy(1 + j, me, (*chip, c), x_ref) for j, chip in enumerate(chips)]
        for cp in first:
            cp.start()
        passed = [copy(4 + j, (*chip, c), sibling) for j, chip in enumerate(chips)]
        for j, chip in enumerate(chips):
            copy(1 + j, (*chip, c), me).wait_recv()
            passed[j].start()
        copy(0, sibling, me).wait_recv()
        for j, chip in enumerate(chips):
            copy(4 + j, (*chip, 1 - c), me).wait_recv()
        for cp in first + passed:
            cp.wait_send()
        mine.wait()

    return pl.pallas_call(
        body, name=name, in_specs=[ANY], out_specs=ANY,
        out_shape=jax.ShapeDtypeStruct((N_DEV,) + tuple(src.shape), src.dtype),
        scratch_shapes=[pltpu.SemaphoreType.DMA((N_DEV - 1,)), pltpu.SemaphoreType.DMA((N_DEV - 1,)),
                        pltpu.SemaphoreType.DMA],
    )(src)


HBM = pl.BlockSpec(memory_space=pltpu.HBM)
SEM = pl.BlockSpec(memory_space=pltpu.SEMAPHORE)
EFFECT = pltpu.SideEffectType.DATAFLOW_SIDE_EFFECTING


def _peer_copy(src_ref, land_ref, send_sems, recv_sems, per_peer, me, k, dev, idx, receiving):
    return pltpu.make_async_remote_copy(
        src_ref=src_ref.at[idx] if per_peer else src_ref, dst_ref=land_ref.at[idx if receiving else me],
        send_sem=send_sems.at[k], recv_sem=recv_sems.at[k], device_id=dev, device_id_type=MESH_ID)


def _push_start(name, src, per_peer):
    blk = src.shape[1:] if per_peer else src.shape
    land_shape = (N_DEV,) + tuple(blk)

    def body(src_ref, land_ref, send_sems, recv_sems, src_thru, land_thru, token):
        me, peers = _me_and_peers()
        for k, (dev, idx) in enumerate(peers):
            _peer_copy(src_ref, land_ref, send_sems, recv_sems, per_peer, me, k, dev, idx, False).start()
        token[...] = jnp.zeros_like(token)

    return pl.pallas_call(
        body, name=name,
        out_shape=(pltpu.SemaphoreType.DMA((N_DEV - 1,)), pltpu.SemaphoreType.DMA((N_DEV - 1,)),
                   pltpu.HBM(src.shape, src.dtype), pltpu.HBM(land_shape, src.dtype),
                   jax.ShapeDtypeStruct((8, 128), F32)),
        in_specs=(HBM, HBM), out_specs=(SEM, SEM, HBM, HBM, pl.BlockSpec(memory_space=pltpu.VMEM)),
        input_output_aliases={0: 2, 1: 3}, compiler_params=pltpu.CompilerParams(has_side_effects=EFFECT),
    )(pltpu.with_memory_space_constraint(src, pltpu.HBM),
      pltpu.with_memory_space_constraint(lax.empty(land_shape, src.dtype), pltpu.HBM))


def _push_wait(name, send_sems, recv_sems, src_thru, land_thru, after, per_peer):
    def body(src_ref, land_ref, send_sems, recv_sems, after_ref, src_out, land_out):
        me, peers = _me_and_peers()
        for k, (dev, idx) in enumerate(peers):
            cp = _peer_copy(src_ref, land_ref, send_sems, recv_sems, per_peer, me, k, dev, idx, True)
            cp.wait_send()
            cp.wait_recv()

    return pl.pallas_call(
        body, name=name,
        out_shape=(pltpu.HBM(src_thru.shape, src_thru.dtype), pltpu.HBM(land_thru.shape, land_thru.dtype)),
        in_specs=(HBM, HBM, SEM, SEM, ANY), out_specs=(HBM, HBM), input_output_aliases={0: 0, 1: 1},
        compiler_params=pltpu.CompilerParams(has_side_effects=EFFECT),
    )(src_thru, land_thru, send_sems, recv_sems, after)


def _sum_blocks(name, blocks):
    _, R, C = blocks.shape
    tc = _pick(C, (128,))

    def body(b_ref, o_ref):
        acc = b_ref[0].astype(F32)
        for k in range(1, N_DEV):
            acc = acc + b_ref[k].astype(F32)
        o_ref[...] = acc

    return pl.pallas_call(
        body, name=name, grid=(C // tc,), in_specs=[pl.BlockSpec((N_DEV, R, tc), lambda j: (0, 0, j))],
        out_specs=pl.BlockSpec((R, tc), lambda j: (0, j)), out_shape=jax.ShapeDtypeStruct((R, C), F32),
        compiler_params=_params(("arbitrary",)),
    )(blocks)


def _adamw(name, w, g, m, v):
    R, C = w.shape
    tr = R if R <= 512 else _pick(R, (256, 184, 176, 128, 8))
    spec = pl.BlockSpec((tr, C), lambda i: (i, 0))

    def body(w_ref, g_ref, m_ref, v_ref, d_ref, mo_ref, vo_ref):
        gv = g_ref[...]
        mn = ADAM_B1 * m_ref[...] + (1.0 - ADAM_B1) * gv
        vn = ADAM_B2 * v_ref[...] + (1.0 - ADAM_B2) * jnp.square(gv)
        m_hat = mn / (1.0 - ADAM_B1 ** ADAM_STEP)
        v_hat = vn / (1.0 - ADAM_B2 ** ADAM_STEP)
        d_ref[...] = -ADAM_LR * (m_hat / (jnp.sqrt(v_hat) + ADAM_EPS) + ADAM_WD * w_ref[...])
        mo_ref[...] = mn
        vo_ref[...] = vn

    return pl.pallas_call(
        body, name=name, grid=(R // tr,), in_specs=[spec] * 4, out_specs=[spec] * 3,
        out_shape=[jax.ShapeDtypeStruct((R, C), F32)] * 3, compiler_params=_params(("arbitrary",)),
    )(w, g, m, v)


WEIGHTS = ("meta_tokens", "norm_mix_w", "w_in", "ret_gn_w", "w_ret_out", "w_ssd_conv", "b_ssd_conv", "dt_bias_f",
           "dt_bias_b", "a_log_f", "a_log_b", "d_skip", "ssd_norm_w", "w_ssd_out", "w_out", "norm_ffn_w", "w_ffn_up",
           "w_ffn_conv", "b_ffn_conv", "w_ffn_down", "final_norm_w")
BIG = (("w_in", 1288, True), ("w_ffn_up", 704, True), ("w_ret_out", 128, False), ("w_ssd_out", 256, False),
       ("w_out", 128, False), ("w_ffn_down", 352, False))
REPLICATED = ("norm_mix_w", "ret_gn_w", "b_ssd_conv", "dt_bias_f", "dt_bias_b", "a_log_f", "a_log_b", "d_skip",
              "ssd_norm_w", "norm_ffn_w", "b_ffn_conv", "final_norm_w")
SMALL_SHARDED = (("meta_tokens", 16, 1024), ("w_ssd_conv", 3, 3072), ("w_ffn_conv", 3, 5632))


BIG_IN, BIG_REST = BIG[:1], BIG[1:]


def _pack_big(tree, group):
    parts = []
    for name, _, transposed in group:
        a = tree[name][0]
        parts.append(a.T if transposed else a)
    return jnp.concatenate(parts, axis=0)


def _unpack_big(slab, group):
    out, r0 = {}, 0
    for name, r, transposed in group:
        a = slab[r0:r0 + r]
        out[name] = (a.T if transposed else a)[None]
        r0 += r
    return out


def _pack_flat(arrays, rows):
    flat = jnp.concatenate([a.reshape(-1) for a in arrays])
    return jnp.pad(flat, (0, rows * D_MODEL - flat.shape[0])).reshape(rows, D_MODEL)


def _unpack_flat(slab, shapes):
    flat, out, o = slab.reshape(-1), [], 0
    for s in shapes:
        n = math.prod(s)
        out.append(flat[o:o + n].reshape(s))
        o += n
    return out


def kernel(x, meta_tokens, norm_mix_w, w_in, ret_gn_w, w_ret_out, w_ssd_conv, b_ssd_conv, dt_bias_f, dt_bias_b, a_log_f, a_log_b, d_skip, ssd_norm_w, w_ssd_out, w_out, norm_ffn_w, w_ffn_up, w_ffn_conv, b_ffn_conv, w_ffn_down, final_norm_w, loss_target, m_meta_tokens, m_norm_mix_w, m_w_in, m_ret_gn_w, m_w_ret_out, m_w_ssd_conv, m_b_ssd_conv, m_dt_bias_f, m_dt_bias_b, m_a_log_f, m_a_log_b, m_d_skip, m_ssd_norm_w, m_w_ssd_out, m_w_out, m_norm_ffn_w, m_w_ffn_up, m_w_ffn_conv, m_b_ffn_conv, m_w_ffn_down, m_final_norm_w, v_meta_tokens, v_norm_mix_w, v_w_in, v_ret_gn_w, v_w_ret_out, v_w_ssd_conv, v_b_ssd_conv, v_dt_bias_f, v_dt_bias_b, v_a_log_f, v_a_log_b, v_d_skip, v_ssd_norm_w, v_w_ssd_out, v_w_out, v_norm_ffn_w, v_w_ffn_up, v_w_ffn_conv, v_b_ffn_conv, v_w_ffn_down, v_final_norm_w):
    given = dict(locals())
    wt = {n: given[n] for n in WEIGHTS}
    mt = {n: given["m_" + n] for n in WEIGHTS}
    vt = {n: given["v_" + n] for n in WEIGHTS}
    me = 4 * lax.axis_index("x") + 2 * lax.axis_index("y") + lax.axis_index("c")

    small_names = [n for n, _, _ in SMALL_SHARDED]
    small_local = lambda tree: [tree[n].reshape(r, c // N_DEV) for n, r, c in SMALL_SHARDED]
    all_in = _gather_two_level("gather_w_in", _pack_big(wt, BIG_IN).astype(BF16))
    all_s = _push_blocks("gather_small", _pack_flat(small_local(wt), 8), False)
    rest_src, all_in, all_s = lax.optimization_barrier((_pack_big(wt, BIG_REST).astype(BF16), all_in, all_s))
    rest_flight = _push_start("gather_rest_start", rest_src, False)
    all_s = all_s.reshape(N_DEV, -1)
    full = {"w_in_t": all_in.reshape(-1, D_MODEL)}

    def land_with_own(flight, after, per_peer, name):
        src, land = _push_wait(name, *flight[:4], after, per_peer)
        own = lax.dynamic_slice_in_dim(src, me, 1, axis=0) if per_peer else src[None]
        return lax.dynamic_update_slice_in_dim(land, own, me, axis=0)

    def late_weights(after):
        all_rest = land_with_own(rest_flight, after, False, "gather_rest_wait")
        out, r0 = {}, 0
        for name, r, transposed in BIG_REST:
            out[name + ("_t" if transposed else "")] = all_rest[:, r0:r0 + r].reshape(N_DEV * r, D_MODEL)
            r0 += r
        return out

    flights = {}

    def start_exchange(key, group, gd):
        g_blocks = jnp.concatenate(
            [gd[name + ("_t" if t else "")].reshape(N_DEV, r, D_MODEL) for name, r, t in group], axis=1)
        flights[key] = _push_start("exchange_" + key + "_start", g_blocks.astype(BF16), True)
        return flights[key][4][0, 0]

    o = 0
    for name, r, c in SMALL_SHARDED:
        n = r * c // N_DEV
        full[name] = all_s[:, o:o + n].reshape(N_DEV, r, c // N_DEV).transpose(1, 0, 2).reshape(r, c)
        o += n
    for name in REPLICATED:
        full[name] = wt[name]

    grads, delta, new_m, new_v = {}, {}, {}, {}

    def finish_exchange(key, group, after):
        g_slab = _sum_blocks("sum_" + key, land_with_own(flights[key], after, True, "exchange_" + key + "_wait"))
        grads.update(_unpack_big(g_slab, group))
        for name, _, transposed in group:
            view = (lambda a: a[0].T) if transposed else (lambda a: a[0])
            back = (lambda a: a.T[None]) if transposed else (lambda a: a[None])
            d, mn, vn = _adamw("adamw_" + name, view(wt[name]), view(grads[name]), view(mt[name]), view(vt[name]))
            delta[name], new_m[name], new_v[name] = back(d), back(mn), back(vn)

    def in_grads(gi):
        tick = start_exchange("in", BIG_IN, {"w_in_t": gi})
        finish_exchange("rest", BIG_REST, flights["in"][4])
        tick, _ = lax.optimization_barrier((tick, [delta[name] for name, _, _ in BIG_REST]))
        return tick

    loss, grad_x, g = _local_step(x[0], loss_target[0], full, rest_flight[4][0, 0], late_weights,
                                  lambda gd: start_exchange("rest", BIG_REST, gd), in_grads)

    finish_exchange("in", BIG_IN, g["norm_mix_w"])
    small_parts = [g[n] for n in REPLICATED] + [g[n] for n in small_names] + [loss.reshape(1)]
    g_small = _sum_blocks("sum_small", _push_blocks("gather_small_grads", _pack_flat(small_parts, 64), False))
    small_red = _unpack_flat(g_small, [wt[n].shape for n in REPLICATED] + [(r, c) for _, r, c in SMALL_SHARDED] + [(1,)])
    grads.update(zip(REPLICATED, small_red[:len(REPLICATED)]))
    for (name, r, c), red in zip(SMALL_SHARDED, small_red[len(REPLICATED):-1]):
        grads[name] = lax.dynamic_slice(red, (0, me * (c // N_DEV)), (r, c // N_DEV)).reshape(wt[name].shape)
    loss_all = small_red[-1][0]

    rest = list(REPLICATED) + small_names
    shapes = [wt[n].shape for n in rest]
    pack_rest = lambda tree: _pack_flat([tree[n] for n in rest], 24)
    d_rest, m_rest, v_rest = _adamw("adamw_small", pack_rest(wt), pack_rest(grads), pack_rest(mt), pack_rest(vt))
    delta.update(zip(rest, _unpack_flat(d_rest, shapes)))
    new_m.update(zip(rest, _unpack_flat(m_rest, shapes)))
    new_v.update(zip(rest, _unpack_flat(v_rest, shapes)))

    return (loss_all, grad_x[None], *[grads[n] for n in WEIGHTS], *[delta[n] for n in WEIGHTS],
            *[new_m[n] for n in WEIGHTS], *[new_v[n] for n in WEIGHTS])
```

```python
import functools
import math

import jax
import jax.numpy as jnp
from jax import lax
from jax.experimental import pallas as pl
from jax.experimental.pallas import tpu as pltpu

F32 = jnp.float32
BF16 = jnp.bfloat16

D_MODEL = 1024
CHUNK = 128
N_META = 16
PAD_ROWS = CHUNK - N_META
RET_HEADS = 4
RET_QK_DIM = 128
RET_V_DIM = 256
SSD_HEADS = 32
SSD_HEAD_DIM = 64
SSD_GROUPS = 4
SSD_STATE = 128
HEADS_PER_GROUP = SSD_HEADS // SSD_GROUPS
PAIRS_PER_GROUP = HEADS_PER_GROUP // 2
D_FF = 2816
EPS = 1e-6
ROPE_BASE = 10000.0
N_DEV = 8

ADAM_LR = 0.001
ADAM_B1 = 0.9
ADAM_B2 = 0.999
ADAM_EPS = 1e-08
ADAM_WD = 0.01
ADAM_STEP = 10

VMEM_LIMIT = 56 * 1024 * 1024
HALO = 16
HIGHEST = lax.Precision.HIGHEST

SEGMENTS = (("qk", 0, 1024), ("v", 1024, 2048), ("g", 2048, 3072), ("z", 3072, 5120), ("xs", 5120, 7168),
            ("B", 7168, 7680), ("C", 7680, 8192), ("dt", 8192, 8256), ("gates", 8256, 10304))


def _pick(n, cands):
    for c in cands:
        if n % c == 0:
            return c
    raise ValueError(f"no tile for {n}")


def _params(sem):
    return pltpu.CompilerParams(dimension_semantics=sem, vmem_limit_bytes=VMEM_LIMIT)


def _dot(a, b, dims=(((1,), (0,)), ((), ())), precision=None):
    return lax.dot_general(a, b, dims, preferred_element_type=F32, precision=precision)


def _dot_nt(a, b):
    return _dot(a, b, (((1,), (1,)), ((), ())))


def _dot_tn(a, b):
    return _dot(a, b, (((0,), (0,)), ((), ())))


def _mm(name, a, b, mode, add=None, out_dtype=F32):
    if mode == "nn":
        (M, K), N = a.shape, b.shape[1]
    elif mode == "nt":
        (M, K), N = a.shape, b.shape[0]
    else:
        (K, M), N = a.shape, b.shape[1]
    tn = _pick(N, (1408, 1024, 512, 128, 64))
    if mode == "tn":
        tm = M if M <= 1024 else _pick(M, (1408, 1024))
        tk = _pick(K, (2112, 512, 256, 128))
    else:
        tm = _pick(M, (1056, 512, 256, 128))
        tk = K if K <= 2816 else _pick(K, (2048, 1408, 1024))
    nk = K // tk
    if mode == "nn":
        a_spec = pl.BlockSpec((tm, tk), lambda n, m, k: (m, k))
        b_spec = pl.BlockSpec((tk, tn), lambda n, m, k: (k, n))
        dims = (((1,), (0,)), ((), ()))
    elif mode == "nt":
        a_spec = pl.BlockSpec((tm, tk), lambda n, m, k: (m, k))
        b_spec = pl.BlockSpec((tn, tk), lambda n, m, k: (n, k))
        dims = (((1,), (1,)), ((), ()))
    else:
        a_spec = pl.BlockSpec((tk, tm), lambda n, m, k: (k, m))
        b_spec = pl.BlockSpec((tk, tn), lambda n, m, k: (k, n))
        dims = (((0,), (0,)), ((), ()))
    o_spec = pl.BlockSpec((tm, tn), lambda n, m, k: (m, n))
    in_specs = [a_spec, b_spec] + ([o_spec] if add is not None else [])
    args = [a, b] + ([add] if add is not None else [])

    def body(*refs):
        if add is not None:
            a_ref, b_ref, r_ref, o_ref, acc = refs
        else:
            a_ref, b_ref, o_ref, acc = refs
        k = pl.program_id(2)
        p = _dot(a_ref[...].astype(BF16), b_ref[...].astype(BF16), dims)

        def finish(r):
            if add is not None:
                r = r + r_ref[...]
            o_ref[...] = r.astype(out_dtype)

        if nk == 1:
            finish(p)
        else:
            @pl.when(k == 0)
            def _():
                acc[...] = p

            @pl.when(k > 0)
            def _():
                acc[...] += p

            @pl.when(k == nk - 1)
            def _():
                finish(acc[...])

    return pl.pallas_call(
        body, name=name, grid=(N // tn, M // tm, nk), in_specs=in_specs, out_specs=o_spec,
        out_shape=jax.ShapeDtypeStruct((M, N), out_dtype),
        scratch_shapes=[pltpu.VMEM((tm, tn) if nk > 1 else (8, 128), F32)],
        compiler_params=_params(("arbitrary", "arbitrary", "arbitrary")),
    )(*args)


def _const(c):
    return lambda j: c


def _rows(name, fn, T, ncol, ins, params, outs, accs=(), halo=False):
    tm = _pick(T, (384, 256, 128))
    R = T // tm
    hb = tm // HALO
    in_specs, args = [], []
    for spec in ins:
        arr, w, cf = spec[:3]
        lead = spec[3] if len(spec) > 3 else None
        if lead is None:
            mk = lambda blk, rf, cf=cf: pl.BlockSpec(blk, lambda j, i: (rf(i), cf(j)))
            shape = lambda r, w=w: (r, w)
        else:
            mk = lambda blk, rf, cf=cf, lead=lead: pl.BlockSpec(blk, lambda j, i: (lead, rf(i), cf(j)))
            shape = lambda r, w=w: (None, r, w)
        in_specs.append(mk(shape(tm), lambda i: i))
        args.append(arr)
        if halo:
            in_specs.append(mk(shape(HALO), lambda i: jnp.maximum(i * hb - 1, 0)))
            in_specs.append(mk(shape(HALO), lambda i: jnp.minimum((i + 1) * hb, T // HALO - 1)))
            args += [arr, arr]
    for arr, w, cf in params:
        in_specs.append(pl.BlockSpec((arr.shape[0], w), lambda j, i, cf=cf: (0, cf(j))))
        args.append(arr)
    out_shape, out_specs = [], []
    for tw, w, cf, dt in outs:
        out_shape.append(jax.ShapeDtypeStruct((T, tw), dt))
        out_specs.append(pl.BlockSpec((tm, w), lambda j, i, cf=cf: (i, cf(j))))
    for r, tw, w, cf in accs:
        out_shape.append(jax.ShapeDtypeStruct((r, tw), F32))
        out_specs.append(pl.BlockSpec((r, w), lambda j, i, cf=cf: (0, cf(j))))
    n_in, n_par, n_out, n_acc = len(ins), len(params), len(outs), len(accs)

    def body(*refs):
        i = pl.program_id(1)
        vals, p = [], 0
        for _ in range(n_in):
            if halo:
                before = jnp.where(i > 0, refs[p + 1][...], jnp.zeros_like(refs[p + 1]))
                after = jnp.where(i < R - 1, refs[p + 2][...], jnp.zeros_like(refs[p + 2]))
                vals.append(jnp.concatenate([before, refs[p][...], after], axis=0).astype(F32))
                p += 3
            else:
                vals.append(refs[p][...].astype(F32))
                p += 1
        pvals = [refs[p + k][...] for k in range(n_par)]
        p += n_par
        res = fn(i, *vals, *pvals)
        for k in range(n_out):
            refs[p + k][...] = res[k].astype(refs[p + k].dtype)
        p += n_out
        for k in range(n_acc):
            ref, v = refs[p + k], res[n_out + k]

            @pl.when(i == 0)
            def _(ref=ref, v=v):
                ref[...] = v

            @pl.when(i > 0)
            def _(ref=ref, v=v):
                ref[...] += v

    res = pl.pallas_call(
        body, name=name, grid=(ncol, R), in_specs=in_specs, out_specs=out_specs, out_shape=out_shape,
        compiler_params=_params(("arbitrary", "arbitrary")),
    )(*args)
    return res


def _tile_rows(T):
    return _pick(T, (384, 256, 128))


def _row_ids(i, T, halo=False):
    tm = _tile_rows(T)
    if halo:
        return i * tm - HALO + lax.broadcasted_iota(jnp.int32, (tm + 2 * HALO, 1), 0)
    return i * tm + lax.broadcasted_iota(jnp.int32, (tm, 1), 0)


def _rms(x, w):
    return x * lax.rsqrt(jnp.mean(x * x, axis=-1, keepdims=True) + EPS) * w


def _silu(x):
    return x * jax.nn.sigmoid(x)


def _conv3(x, w):
    n = x.shape[0]
    return w[0:1] * pltpu.roll(x, 1, 0) + w[1:2] * x + w[2:3] * pltpu.roll(x, n - 1, 0)


def _conv3_t(d, w):
    n = d.shape[0]
    return w[0:1] * pltpu.roll(d, n - 1, 0) + w[1:2] * d + w[2:3] * pltpu.roll(d, 1, 0)


def _center(x):
    return x[HALO:x.shape[0] - HALO]


def _retention(name, a, b, v, T, da, dv):
    (a, a0), (b, b0), (v, v0) = [t if isinstance(t, tuple) else (t, 0) for t in (a, b, v)]
    nc = T // CHUNK
    log_gammas = [math.log(1.0 - 2.0 ** (-5.0 - h)) for h in range(RET_HEADS)]

    def body(a_ref, b_ref, v_ref, o_ref, st, st_b):
        h = pl.program_id(0)
        lg = jnp.float32(log_gammas[RET_HEADS - 1])
        for k in range(RET_HEADS - 2, -1, -1):
            lg = jnp.where(h == k, jnp.float32(log_gammas[k]), lg)
        li = lax.broadcasted_iota(jnp.int32, (CHUNK, CHUNK), 0)
        si = lax.broadcasted_iota(jnp.int32, (CHUNK, CHUNK), 1)
        dmat = jnp.exp(lg * jnp.abs(li - si).astype(F32))
        pos = lax.broadcasted_iota(jnp.int32, (CHUNK, 1), 0).astype(F32)
        kdec_f = jnp.exp((CHUNK - 1 - pos) * lg)
        qdec_f = jnp.exp((pos + 1) * lg)
        kdec_b = jnp.exp(pos * lg)
        qdec_b = jnp.exp((CHUNK - pos) * lg)
        cdec = jnp.exp(CHUNK * lg)

        def rows(n):
            return pl.ds(pl.multiple_of(n * CHUNK, CHUNK), CHUNK)

        st[...] = jnp.zeros_like(st)
        st_b[...] = jnp.zeros_like(st_b)
        o_ref[...] = jnp.zeros_like(o_ref)

        def step(m, carry):
            r = rows(m)
            av, bv, vv = a_ref[r, :], b_ref[r, :], v_ref[r, :].astype(BF16)
            s = _dot_nt(av.astype(BF16), bv.astype(BF16)) * dmat
            o_ref[r, :] += _dot(s.astype(BF16), vv) + _dot((av * qdec_f).astype(BF16), st[...].astype(BF16))
            st[...] = cdec * st[...] + _dot_tn((bv * kdec_f).astype(BF16), vv)
            r = rows(nc - 1 - m)
            av, bv, vv = a_ref[r, :], b_ref[r, :], v_ref[r, :].astype(BF16)
            o_ref[r, :] += _dot((av * qdec_b).astype(BF16), st_b[...].astype(BF16))
            st_b[...] = cdec * st_b[...] + _dot_tn((bv * kdec_b).astype(BF16), vv)
            return carry

        lax.fori_loop(0, nc, step, 0, unroll=3 if nc % 3 == 0 else 1)

    return pl.pallas_call(
        body, name=name, grid=(RET_HEADS,),
        in_specs=[pl.BlockSpec((T, da), lambda h: (0, a0 // da + h)), pl.BlockSpec((T, da), lambda h: (0, b0 // da + h)),
                  pl.BlockSpec((T, dv), lambda h: (0, v0 // dv + h))],
        out_specs=pl.BlockSpec((T, dv), lambda h: (0, h)),
        out_shape=jax.ShapeDtypeStruct((T, RET_HEADS * dv), F32),
        scratch_shapes=[pltpu.VMEM((da, dv), F32), pltpu.VMEM((da, dv), F32)],
        compiler_params=_params(("arbitrary",)),
    )(a, b, v)


def _softplus(x):
    return jnp.maximum(x, 0.0) + jnp.log1p(jnp.exp(-jnp.abs(x)))


def _lane_lo():
    return lax.broadcasted_iota(jnp.int32, (1, CHUNK), 1) < SSD_HEAD_DIM


def _pair_cols(col, j):
    return jnp.where(_lane_lo(), col[:, 2 * j:2 * j + 1], col[:, 2 * j + 1:2 * j + 2])


def _pair_rows(colr, j):
    lo = lax.broadcasted_iota(jnp.int32, (CHUNK, 1), 0) < SSD_HEAD_DIM
    return jnp.where(lo, colr[2 * j:2 * j + 1, :], colr[2 * j + 1:2 * j + 2, :])


def _onehot8(h):
    return (lax.broadcasted_iota(jnp.int32, (1, HEADS_PER_GROUP), 1) == h).astype(F32)


def _ssd_pre(d, c, rawc, rawr, bc, br, alc, alr):
    li = lax.broadcasted_iota(jnp.int32, (CHUNK, CHUNK), 0)
    si = lax.broadcasted_iota(jnp.int32, (CHUNK, CHUNK), 1)
    dif = li - si if d == 0 else si - li
    mask = dif >= 0
    mask_t = dif <= 0
    rowc = c * CHUNK + lax.broadcasted_iota(jnp.int32, (CHUNK, 1), 0)
    rowr = c * CHUNK + lax.broadcasted_iota(jnp.int32, (1, CHUNK), 1)
    dtc = jnp.where(rowc >= PAD_ROWS, _softplus(rawc + bc), 0.0)
    dtr = jnp.where(rowr >= PAD_ROWS, _softplus(rawr + br), 0.0)
    ac = -jnp.exp(alc)
    ar = -jnp.exp(alr)
    dlc = dtc * ac
    dlr = dtr * ar
    alpc = _dot(mask.astype(F32), dlc, precision=HIGHEST)
    alpr = _dot(dlr, mask_t.astype(F32), precision=HIGHEST)
    endc = jnp.sum(dlc, axis=0, keepdims=True)
    endr = jnp.sum(dlr, axis=1, keepdims=True)
    return dict(mask=mask, mask_t=mask_t, dtc=dtc, ac=ac, alpc=alpc, alpr=alpr, endc=endc, endr=endr,
                valid=rowc >= PAD_ROWS)


def _chunk_of(d, n, nc):
    return n + d * (nc - 1 - 2 * n)


GROUP_WIDTH = HEADS_PER_GROUP * SSD_HEAD_DIM


def _ssd_in_specs(d, cfn):
    return [
        pl.BlockSpec((CHUNK, GROUP_WIDTH), lambda g, n: (cfn(d, n), g)),
        pl.BlockSpec((CHUNK, SSD_STATE), lambda g, n: (cfn(d, n), g)),
        pl.BlockSpec((CHUNK, SSD_STATE), lambda g, n: (cfn(d, n), g)),
        pl.BlockSpec((None, None, CHUNK, HEADS_PER_GROUP), lambda g, n: (d, g, cfn(d, n), 0)),
        pl.BlockSpec((None, None, HEADS_PER_GROUP, CHUNK), lambda g, n: (d, g, 0, cfn(d, n))),
        pl.BlockSpec((None, None, 1, HEADS_PER_GROUP), lambda g, n: (d, g, 0, 0)),
        pl.BlockSpec((None, None, HEADS_PER_GROUP, 1), lambda g, n: (d, g, 0, 0)),
        pl.BlockSpec((None, None, 1, HEADS_PER_GROUP), lambda g, n: (d, g, 0, 0)),
        pl.BlockSpec((None, None, HEADS_PER_GROUP, 1), lambda g, n: (d, g, 0, 0)),
    ]


N_SSD_IN = 9


def _ssd_fwd(xs, bm, cm, small, T):
    nc = T // CHUNK
    cfn = lambda d, n: _chunk_of(d, n, nc)

    def one_direction(d, n, ins, y_ref, hs_ref, h_scr):
        x_ref, b_ref, c_ref, rawc_ref, rawr_ref, bc_ref, br_ref, alc_ref, alr_ref = ins
        c = cfn(d, n)
        q = _ssd_pre(d, c, rawc_ref[...], rawr_ref[...], bc_ref[...], br_ref[...], alc_ref[...], alr_ref[...])
        bv = b_ref[...].astype(BF16)
        cv = c_ref[...].astype(BF16)
        cb = _dot_nt(cv, bv)
        lo = _lane_lo()
        for j in range(PAIRS_PER_GROUP):
            xp = x_ref[:, j * CHUNK:(j + 1) * CHUNK]
            xd = xp * _pair_cols(q["dtc"], j)
            xdb = xd.astype(BF16)
            yi = []
            for e in range(2):
                h = 2 * j + e
                lm = jnp.exp(jnp.where(q["mask"], q["alpc"][:, h:h + 1] - q["alpr"][h:h + 1, :], -jnp.inf))
                yi.append(_dot((cb * lm).astype(BF16), xdb))
            alp = _pair_cols(q["alpc"], j)
            hp = h_scr[j]
            hs_ref[j] = hp
            yo = jnp.exp(alp) * _dot_nt(cv, hp.astype(BF16))
            y_ref[:, j * CHUNK:(j + 1) * CHUNK] = (jnp.where(lo, yi[0], yi[1]) + yo).astype(y_ref.dtype)
            de = jnp.exp(_pair_cols(q["endc"], j) - alp)
            h_scr[j] = jnp.exp(_pair_rows(q["endr"], j)) * hp + _dot_tn((xd * de).astype(BF16), bv)

    def body(*refs):
        n = pl.program_id(1)
        ins, (y_f, y_b, hs_f, hs_b, h_scr) = refs[:2 * N_SSD_IN], refs[2 * N_SSD_IN:]

        @pl.when(n == 0)
        def _():
            h_scr[...] = jnp.zeros_like(h_scr)

        one_direction(0, n, ins[:N_SSD_IN], y_f, hs_f, h_scr.at[0])
        one_direction(1, n, ins[N_SSD_IN:], y_b, hs_b, h_scr.at[1])

    y_spec = lambda d: pl.BlockSpec((CHUNK, GROUP_WIDTH), lambda g, n: (cfn(d, n), g))
    hs_spec = lambda d: pl.BlockSpec((None, None, PAIRS_PER_GROUP, CHUNK, SSD_STATE),
                                     lambda g, n: (g, cfn(d, n), 0, 0, 0))
    y_shape = jax.ShapeDtypeStruct((T, SSD_HEADS * SSD_HEAD_DIM), BF16)
    hs_shape = jax.ShapeDtypeStruct((SSD_GROUPS, nc, PAIRS_PER_GROUP, CHUNK, SSD_STATE), F32)
    y_f, y_b, hs_f, hs_b = pl.pallas_call(
        body, name="ssd_fwd", grid=(SSD_GROUPS, nc),
        in_specs=_ssd_in_specs(0, cfn) + _ssd_in_specs(1, cfn),
        out_specs=[y_spec(0), y_spec(1), hs_spec(0), hs_spec(1)],
        out_shape=[y_shape, y_shape, hs_shape, hs_shape],
        scratch_shapes=[pltpu.VMEM((2, PAIRS_PER_GROUP, CHUNK, SSD_STATE), F32)],
        compiler_params=_params(("arbitrary", "arbitrary")),
    )(xs, bm, cm, *small, xs, bm, cm, *small)
    return (y_f, y_b), (hs_f, hs_b)


def _ssd_bwd(xs, bm, cm, small, hs, dy, T):
    nc = T // CHUNK
    cfn = lambda d, n: _chunk_of(1 - d, n, nc)

    def one_direction(d, n, ins, outs, dh_scr):
        x_ref, b_ref, c_ref, rawc_ref, rawr_ref, bc_ref, br_ref, alc_ref, alr_ref, hs_ref, dy_ref = ins
        dx_ref, db_ref, dc_ref, draw_ref, dbias_ref, dalog_ref = outs
        c = cfn(d, n)
        rawc, bc = rawc_ref[...], bc_ref[...]
        q = _ssd_pre(d, c, rawc, rawr_ref[...], bc, br_ref[...], alc_ref[...], alr_ref[...])
        b32, c32 = b_ref[...], c_ref[...]
        bv, cv = b32.astype(BF16), c32.astype(BF16)
        cb = _dot_nt(cv, bv)
        cbt = _dot_nt(bv, cv)
        lo = _lane_lo()
        row_lo = lax.broadcasted_iota(jnp.int32, (CHUNK, 1), 0) < SSD_HEAD_DIM
        dcb = jnp.zeros((CHUNK, CHUNK), F32)
        dcp = jnp.zeros((CHUNK, SSD_STATE), F32)
        dbp = jnp.zeros((CHUNK, SSD_STATE), F32)
        dalp = jnp.zeros((CHUNK, HEADS_PER_GROUP), F32)
        dend = jnp.zeros((1, HEADS_PER_GROUP), F32)
        ddtx = jnp.zeros((CHUNK, HEADS_PER_GROUP), F32)

        def half_sums(t):
            return (jnp.sum(jnp.where(lo, t, 0.0), axis=1, keepdims=True),
                    jnp.sum(jnp.where(lo, 0.0, t), axis=1, keepdims=True))

        for j in range(PAIRS_PER_GROUP):
            xp = x_ref[:, j * CHUNK:(j + 1) * CHUNK]
            dtp = _pair_cols(q["dtc"], j)
            xd = xp * dtp
            xdb = xd.astype(BF16)
            dyp = dy_ref[:, j * CHUNK:(j + 1) * CHUNK]
            dyb = dyp.astype(BF16)
            hn = hs_ref[j]
            hnb = hn.astype(BF16)
            dh1 = dh_scr[j]
            dh1b = dh1.astype(BF16)
            alp = _pair_cols(q["alpc"], j)
            ea = jnp.exp(alp)
            de = jnp.exp(_pair_cols(q["endc"], j) - alp)
            dxi = []
            for e in range(2):
                h = 2 * j + e
                diff = q["alpc"][:, h:h + 1] - q["alpr"][h:h + 1, :]
                lm = jnp.exp(jnp.where(q["mask"], diff, -jnp.inf))
                mt = cbt * jnp.exp(jnp.where(q["mask_t"], -diff, -jnp.inf))
                dxi.append(_dot(mt.astype(BF16), dyb))
                dyeb_h = (jnp.where(lo, dyp, 0.0) if e == 0 else jnp.where(lo, 0.0, dyp)).astype(BF16)
                gl = _dot_nt(dyeb_h, xdb) * lm
                dcb = dcb + gl
                ra = jnp.sum(gl * cb - _dot_nt(xdb, dyeb_h) * mt, axis=1, keepdims=True)
                dalp = dalp + ra * _onehot8(h)
            y_off = ea * _dot_nt(cv, hnb)
            dxs_state = de * _dot_nt(bv, dh1b)
            dxd = jnp.where(lo, dxi[0], dxi[1]) + dxs_state
            dyeb = (dyp * ea).astype(BF16)
            dcp = dcp + _dot(dyeb, hnb)
            dbp = dbp + _dot((xd * de).astype(BF16), dh1b)
            dh_scr[j] = jnp.exp(_pair_rows(q["endr"], j)) * dh1 + _dot_tn(dyeb, cv)
            r0, r1 = half_sums(dyp * y_off - xd * dxs_state)
            dalp = dalp + r0 * _onehot8(2 * j) + r1 * _onehot8(2 * j + 1)
            t0, t1 = half_sums(jnp.sum(xd * dxs_state, axis=0, keepdims=True))
            u = dh1 * hn
            u0 = jnp.sum(jnp.sum(jnp.where(row_lo, u, 0.0), axis=0, keepdims=True), axis=1, keepdims=True)
            u1 = jnp.sum(jnp.sum(jnp.where(row_lo, 0.0, u), axis=0, keepdims=True), axis=1, keepdims=True)
            eend = jnp.exp(q["endc"])
            dend = dend + (t0 + eend * u0) * _onehot8(2 * j) + (t1 + eend * u1) * _onehot8(2 * j + 1)
            dx_ref[:, j * CHUNK:(j + 1) * CHUNK] = (dxd * dtp).astype(dx_ref.dtype)
            w0, w1 = half_sums(dxd * xp)
            ddtx = ddtx + w0 * _onehot8(2 * j) + w1 * _onehot8(2 * j + 1)

        dcbb = dcb.astype(BF16)
        dc_ref[...] = (dcp + _dot(dcbb, bv)).astype(dc_ref.dtype)
        db_ref[...] = (dbp + _dot_tn(dcbb, cv)).astype(db_ref.dtype)
        ddl = _dot(q["mask_t"].astype(F32), dalp, precision=HIGHEST) + dend
        ddt = ddl * q["ac"] + ddtx
        draw = jnp.where(q["valid"], ddt * jax.nn.sigmoid(rawc + bc), 0.0)
        draw_ref[...] = draw
        dbias = jnp.sum(draw, axis=0, keepdims=True)
        dalog = jnp.sum(ddl * q["dtc"], axis=0, keepdims=True) * q["ac"]

        @pl.when(n == 0)
        def _():
            dbias_ref[...] = dbias
            dalog_ref[...] = dalog

        @pl.when(n > 0)
        def _():
            dbias_ref[...] += dbias
            dalog_ref[...] += dalog

    n_in, n_out = N_SSD_IN + 2, 6

    def body(*refs):
        n = pl.program_id(1)
        ins, outs, dh_scr = refs[:2 * n_in], refs[2 * n_in:2 * (n_in + n_out)], refs[-1]

        @pl.when(n == 0)
        def _():
            dh_scr[...] = jnp.zeros_like(dh_scr)

        one_direction(0, n, ins[:n_in], outs[:n_out], dh_scr.at[0])
        one_direction(1, n, ins[n_in:], outs[n_out:], dh_scr.at[1])

    def in_specs(d):
        return _ssd_in_specs(d, cfn) + [
            pl.BlockSpec((None, None, PAIRS_PER_GROUP, CHUNK, SSD_STATE), lambda g, n: (g, cfn(d, n), 0, 0, 0)),
            pl.BlockSpec((CHUNK, GROUP_WIDTH), lambda g, n: (cfn(d, n), g))]

    def out_specs(d):
        acc = pl.BlockSpec((None, 1, HEADS_PER_GROUP), lambda g, n: (g, 0, 0))
        return [pl.BlockSpec((CHUNK, GROUP_WIDTH), lambda g, n: (cfn(d, n), g)),
                pl.BlockSpec((CHUNK, SSD_STATE), lambda g, n: (cfn(d, n), g)),
                pl.BlockSpec((CHUNK, SSD_STATE), lambda g, n: (cfn(d, n), g)),
                pl.BlockSpec((None, CHUNK, HEADS_PER_GROUP), lambda g, n: (g, cfn(d, n), 0)), acc, acc]

    out_shape = [jax.ShapeDtypeStruct((T, SSD_HEADS * SSD_HEAD_DIM), BF16),
                 jax.ShapeDtypeStruct((T, SSD_GROUPS * SSD_STATE), BF16),
                 jax.ShapeDtypeStruct((T, SSD_GROUPS * SSD_STATE), BF16),
                 jax.ShapeDtypeStruct((SSD_GROUPS, T, HEADS_PER_GROUP), F32),
                 jax.ShapeDtypeStruct((SSD_GROUPS, 1, HEADS_PER_GROUP), F32),
                 jax.ShapeDtypeStruct((SSD_GROUPS, 1, HEADS_PER_GROUP), F32)]
    res = pl.pallas_call(
        body, name="ssd_bwd", grid=(SSD_GROUPS, nc),
        in_specs=in_specs(0) + in_specs(1), out_specs=out_specs(0) + out_specs(1), out_shape=out_shape * 2,
        scratch_shapes=[pltpu.VMEM((2, PAIRS_PER_GROUP, CHUNK, SSD_STATE), F32)],
        compiler_params=_params(("arbitrary", "arbitrary")),
    )(xs, bm, cm, *small, hs[0], dy, xs, bm, cm, *small, hs[1], dy)
    return [(res[k], res[n_out + k]) for k in range(n_out)]


def _rot(x, cs, sn):
    return x * cs + pltpu.roll(x, RET_QK_DIM // 2, 1) * sn


def _rot_t(d, cs, sn):
    return d * cs + pltpu.roll(d * sn, RET_QK_DIM // 2, 1)


def _ret_post(y, g, w):
    parts = []
    for h in range(RET_HEADS):
        yh = y[:, h * RET_V_DIM:(h + 1) * RET_V_DIM]
        mu = jnp.mean(yh, axis=-1, keepdims=True)
        var = jnp.mean(jnp.square(yh - mu), axis=-1, keepdims=True)
        parts.append((yh - mu) * lax.rsqrt(var + EPS))
    return _silu(g) * (jnp.concatenate(parts, axis=1) * w)


def _ssd_post(yf, yb, xs, z, dskip, w):
    y = (yf + yb + xs * dskip) * _silu(z)
    return y * lax.rsqrt(jnp.mean(y * y, axis=-1, keepdims=True) + EPS) * w


def _merge(gates, yr, ys, valid):
    m = jax.nn.sigmoid(gates[:, :D_MODEL]) * yr + jax.nn.sigmoid(gates[:, D_MODEL:]) * ys
    return jnp.where(valid, m, 0.0)


def _rope_tables(T):
    half = RET_QK_DIM // 2
    inv = ROPE_BASE ** (-jnp.arange(half, dtype=F32) / half)
    pos = (jnp.arange(T) - PAD_ROWS).astype(F32)
    ang = pos[:, None] * inv[None, :]
    cos, sin = jnp.cos(ang), jnp.sin(ang)
    return jnp.concatenate([cos, cos], axis=1), jnp.concatenate([-sin, sin], axis=1)


def _per_group(v):
    c = v.reshape(SSD_GROUPS, 1, HEADS_PER_GROUP)
    return c, c.reshape(SSD_GROUPS, HEADS_PER_GROUP, 1)


def _local_step(x, target, w, tick, late_weights, early_grads, in_grads):
    S = x.shape[0]
    T = S + CHUNK
    tm = _tile_rows(T)
    c0 = _const(0)

    h0 = jnp.concatenate([jnp.zeros((PAD_ROWS, D_MODEL), F32), w["meta_tokens"], x], axis=0)
    tgt = jnp.concatenate([jnp.zeros((CHUNK, D_MODEL), F32), target], axis=0)
    seg_at = {name: a for name, a, _ in SEGMENTS}
    w_main = w["w_in_t"][:seg_at["dt"]]
    w_dt = jnp.pad(w["w_in_t"][seg_at["dt"]:seg_at["gates"]], ((0, CHUNK - 2 * SSD_HEADS), (0, 0)))
    w_gates = w["w_in_t"][seg_at["gates"]:]

    def norm_cast(name, h, nw):
        return _rows(name, lambda i, hv, wv: (_rms(hv, wv),), T, 1, [(h, D_MODEL, c0)], [(nw, D_MODEL, c0)],
                     [(D_MODEL, D_MODEL, c0, BF16)])[0]

    u = norm_cast("norm_mix", h0, w["norm_mix_w"] + tick)
    p_main = _mm("proj_main", u, w_main, "nt", out_dtype=BF16)
    p_dt = _mm("proj_dt", u, w_dt, "nt")
    p_gates = _mm("proj_gates", u, w_gates, "nt", out_dtype=BF16)

    def seg(name, width, cf=c0):
        base = seg_at[name] // width
        return (p_main, width, lambda j: base + cf(j))

    cs, sn = _rope_tables(T)
    scale = RET_QK_DIM ** -0.5

    def rot_fn(i, qk, csv, snv):
        q = [_rot(qk[:, h * 128:(h + 1) * 128], csv, snv) for h in range(RET_HEADS)]
        k = [_rot(qk[:, (RET_HEADS + h) * 128:(RET_HEADS + h + 1) * 128], csv, snv) * scale for h in range(RET_HEADS)]
        return jnp.concatenate(q, axis=1), jnp.concatenate(k, axis=1)

    qr, kr = _rows("rotary", rot_fn, T, 1, [seg("qk", 1024), (cs, 128, c0), (sn, 128, c0)], [],
                   [(512, 512, c0, F32), (512, 512, c0, F32)])
    v_at = (p_main, seg_at["v"])
    y_ret = _retention("retention", qr, kr, v_at, T, RET_QK_DIM, RET_V_DIM)
    a_ret = _rows("ret_post", lambda i, y, g, gw: (_ret_post(y, g, gw),), T, 1,
                  [(y_ret, 1024, c0), seg("g", 1024)], [(w["ret_gn_w"], 1024, c0)],
                  [(1024, 1024, c0, BF16)])[0]

    conv_w = {"xs": w["w_ssd_conv"][:, :2048], "B": w["w_ssd_conv"][:, 2048:2560], "C": w["w_ssd_conv"][:, 2560:]}
    conv_b = {"xs": w["b_ssd_conv"][:, :2048], "B": w["b_ssd_conv"][:, 2048:2560], "C": w["b_ssd_conv"][:, 2560:]}

    def ssd_conv_fn(i, xe, cw, cb):
        r = _row_ids(i, T, True)
        return (_center(jnp.where(r >= PAD_ROWS, _silu(_conv3(xe, cw) + cb), 0.0)),)

    act = {}
    for name in ("xs", "B", "C"):
        wd = conv_w[name].shape[1]
        cw = 512
        act[name] = _rows("ssd_conv_" + name, ssd_conv_fn, T, wd // cw, [seg(name, cw, lambda j: j)],
                          [(conv_w[name], cw, lambda j: j), (conv_b[name], cw, lambda j: j)],
                          [(wd, cw, lambda j: j, BF16)], halo=True)[0]

    raw = p_dt[:, :2 * SSD_HEADS].reshape(T, 2, SSD_GROUPS, HEADS_PER_GROUP)
    rawc = raw.transpose(1, 2, 0, 3)
    rawr = raw.transpose(1, 2, 3, 0)
    bias = [_per_group(w["dt_bias_f"]), _per_group(w["dt_bias_b"])]
    alog = [_per_group(w["a_log_f"]), _per_group(w["a_log_b"])]
    small = (rawc, rawr, jnp.stack([bias[0][0], bias[1][0]]), jnp.stack([bias[0][1], bias[1][1]]),
             jnp.stack([alog[0][0], alog[1][0]]), jnp.stack([alog[0][1], alog[1][1]]))
    y_dir, states = _ssd_fwd(act["xs"], act["B"], act["C"], small, T)

    dskip_e = jnp.repeat(w["d_skip"], SSD_HEAD_DIM, axis=1)
    gcol = lambda j: j
    gw_ = 512
    a_ssd = _rows("ssd_post", lambda i, yf, yb, xv, zv, dk, nw: (_ssd_post(yf, yb, xv, zv, dk, nw),), T, SSD_GROUPS,
                  [(y_dir[0], gw_, gcol), (y_dir[1], gw_, gcol), (act["xs"], gw_, gcol), seg("z", gw_, gcol)],
                  [(dskip_e, gw_, gcol), (w["ssd_norm_w"], gw_, gcol)], [(2048, gw_, gcol, BF16)])[0]

    w = dict(w, **late_weights(a_ssd))
    w_up_g, w_up_u = w["w_ffn_up_t"][:D_FF], w["w_ffn_up_t"][D_FF:]
    y_ret_o = _mm("ret_out", a_ret, w["w_ret_out"], "nn", out_dtype=BF16)
    y_ssd_o = _mm("ssd_out", a_ssd, w["w_ssd_out"], "nn", out_dtype=BF16)

    def merge_fn(i, gates, yr, ys):
        return (_merge(gates, yr, ys, _row_ids(i, T) >= PAD_ROWS),)

    merged = _rows("merge", merge_fn, T, 1, [(p_gates, 2048, c0), (y_ret_o, 1024, c0), (y_ssd_o, 1024, c0)], [],
                   [(1024, 1024, c0, BF16)])[0]
    h1 = _mm("mix_out", merged, w["w_out"], "nn", add=h0)

    n2 = norm_cast("norm_ffn", h1, w["norm_ffn_w"])
    fg_pre = _mm("ffn_up_g", n2, w_up_g, "nt", out_dtype=BF16)
    fu_pre = _mm("ffn_up_u", n2, w_up_u, "nt", out_dtype=BF16)
    cwg, cwu = w["w_ffn_conv"][:, :D_FF], w["w_ffn_conv"][:, D_FF:]
    cbg, cbu = w["b_ffn_conv"][:, :D_FF], w["b_ffn_conv"][:, D_FF:]
    fcol = lambda j: j
    fw = 1408

    def ffn_act_fn(i, ge, ue, wg, wu, bg, bu):
        return (_center(_silu(_conv3(ge, wg) + bg) * (_conv3(ue, wu) + bu)),)

    a2 = _rows("ffn_act", ffn_act_fn, T, D_FF // fw, [(fg_pre, fw, fcol), (fu_pre, fw, fcol)],
               [(cwg, fw, fcol), (cwu, fw, fcol), (cbg, fw, fcol), (cbu, fw, fcol)], [(D_FF, fw, fcol, BF16)],
               halo=True)[0]
    h2 = _mm("ffn_down", a2, w["w_ffn_down"], "nn", add=h1)

    fnw = w["final_norm_w"].reshape(1, D_MODEL)

    def loss_fn(i, hv, tv, nw):
        valid = _row_ids(i, T) >= CHUNK
        y, vjp = jax.vjp(_rms, hv, nw)
        diff = jnp.where(valid, y - tv, 0.0)
        dh, dw = vjp(diff * (1.0 / D_MODEL))
        part = 0.5 / D_MODEL * jnp.sum(jnp.sum(diff * diff, axis=1, keepdims=True), axis=0, keepdims=True)
        return dh, jnp.broadcast_to(part, (1, 128)), dw

    dh2, loss_acc, d_fnw = _rows("loss", loss_fn, T, 1, [(h2, D_MODEL, c0), (tgt, D_MODEL, c0)], [(fnw, D_MODEL, c0)],
                                 [(D_MODEL, D_MODEL, c0, F32)], [(1, 128, 128, c0), (1, D_MODEL, D_MODEL, c0)])
    loss = loss_acc[0, 0]
    grads = {"final_norm_w": d_fnw.reshape(D_MODEL)}

    da2 = _mm("d_ffn_act", dh2, w["w_ffn_down"], "nt", out_dtype=BF16)
    grads["w_ffn_down"] = _mm("g_ffn_down", a2, dh2, "tn", out_dtype=BF16)

    def ffn_bwd_fn(i, ge, ue, de, wg, wu, bg, bu):
        fg = _conv3(ge, wg) + bg
        fu = _conv3(ue, wu) + bu
        sg = jax.nn.sigmoid(fg)
        dfg = de * fu * (sg * (1.0 + fg * (1.0 - sg)))
        dfu = de * (fg * sg)
        n = ge.shape[0]

        def wgrad(df, xe):
            df_c = _center(df)
            return jnp.concatenate([jnp.sum(df_c * _center(pltpu.roll(xe, 1, 0)), axis=0, keepdims=True),
                                    jnp.sum(df_c * _center(xe), axis=0, keepdims=True),
                                    jnp.sum(df_c * _center(pltpu.roll(xe, n - 1, 0)), axis=0, keepdims=True)], axis=0)

        return (_center(_conv3_t(dfg, wg)), _center(_conv3_t(dfu, wu)), wgrad(dfg, ge), wgrad(dfu, ue),
                jnp.sum(_center(dfg), axis=0, keepdims=True), jnp.sum(_center(dfu), axis=0, keepdims=True))

    dfg_pre, dfu_pre, g_cwg, g_cwu, g_cbg, g_cbu = _rows(
        "ffn_act_bwd", ffn_bwd_fn, T, D_FF // fw, [(fg_pre, fw, fcol), (fu_pre, fw, fcol), (da2, fw, fcol)],
        [(cwg, fw, fcol), (cwu, fw, fcol), (cbg, fw, fcol), (cbu, fw, fcol)],
        [(D_FF, fw, fcol, BF16), (D_FF, fw, fcol, BF16)],
        [(3, D_FF, fw, fcol), (3, D_FF, fw, fcol), (1, D_FF, fw, fcol), (1, D_FF, fw, fcol)], halo=True)
    grads["w_ffn_conv"] = jnp.concatenate([g_cwg, g_cwu], axis=1)
    grads["b_ffn_conv"] = jnp.concatenate([g_cbg, g_cbu], axis=1)
    dn2 = _mm("d_norm_ffn_g", dfg_pre, w_up_g, "nn")
    dn2 = _mm("d_norm_ffn_u", dfu_pre, w_up_u, "nn", add=dn2)
    grads["w_ffn_up_t"] = jnp.concatenate([_mm("g_ffn_up_g", dfg_pre, n2, "tn", out_dtype=BF16), _mm("g_ffn_up_u", dfu_pre, n2, "tn", out_dtype=BF16)],
                                          axis=0)

    def norm_bwd(name, h, nw, dn, dres):
        def fn(i, hv, dnv, drv, wv):
            _, vjp = jax.vjp(_rms, hv, wv)
            dh, dw = vjp(dnv)
            return dh + drv, dw
        return _rows(name, fn, T, 1, [(h, D_MODEL, c0), (dn, D_MODEL, c0), (dres, D_MODEL, c0)], [(nw, D_MODEL, c0)],
                     [(D_MODEL, D_MODEL, c0, F32)], [(1, D_MODEL, D_MODEL, c0)])

    dh1, grads["norm_ffn_w"] = norm_bwd("norm_ffn_bwd", h1, w["norm_ffn_w"], dn2, dh2)

    dmerged = _mm("d_merged", dh1, w["w_out"], "nt", out_dtype=BF16)
    grads["w_out"] = _mm("g_out", merged, dh1, "tn", out_dtype=BF16)

    def merge_bwd_fn(i, gates, yr, ys, dm):
        valid = _row_ids(i, T) >= PAD_ROWS
        _, vjp = jax.vjp(lambda a, b, c: _merge(a, b, c, valid), gates, yr, ys)
        return vjp(dm)

    dgates, dyr, dys = _rows("merge_bwd", merge_bwd_fn, T, 1,
                             [(p_gates, 2048, c0), (y_ret_o, 1024, c0), (y_ssd_o, 1024, c0), (dmerged, 1024, c0)],
                             [], [(2048, 2048, c0, BF16), (1024, 1024, c0, BF16), (1024, 1024, c0, BF16)])
    dproj = {"gates": dgates}

    da_ssd = _mm("d_ssd_act", dys, w["w_ssd_out"], "nt", out_dtype=BF16)
    grads["w_ssd_out"] = _mm("g_ssd_out", a_ssd, dys, "tn", out_dtype=BF16)

    def ssd_post_bwd_fn(i, yf, yb, xv, zv, da, dk, nw):
        _, vjp = jax.vjp(_ssd_post, yf, yb, xv, zv, dk, nw)
        dyf, _, dxv, dzv, ddk, dnw = vjp(da)
        return dyf, dxv, dzv, ddk, dnw

    dy_ssd, dxs_skip, dproj["z"], g_dskip_e, grads["ssd_norm_w"] = _rows(
        "ssd_post_bwd", ssd_post_bwd_fn, T, SSD_GROUPS,
        [(y_dir[0], gw_, gcol), (y_dir[1], gw_, gcol), (act["xs"], gw_, gcol), seg("z", gw_, gcol),
         (da_ssd, gw_, gcol)],
        [(dskip_e, gw_, gcol), (w["ssd_norm_w"], gw_, gcol)],
        [(2048, gw_, gcol, BF16), (2048, gw_, gcol, BF16), (2048, gw_, gcol, BF16)],
        [(1, 2048, gw_, gcol), (1, 2048, gw_, gcol)])
    grads["d_skip"] = g_dskip_e.reshape(SSD_HEADS, SSD_HEAD_DIM).sum(axis=1).reshape(1, SSD_HEADS)

    dxs_dir, db_dir, dc_dir, draw, g_bias, g_alog = _ssd_bwd(act["xs"], act["B"], act["C"], small, states, dy_ssd, T)
    grads["dt_bias_f"], grads["dt_bias_b"] = g_bias[0].reshape(1, SSD_HEADS), g_bias[1].reshape(1, SSD_HEADS)
    grads["a_log_f"], grads["a_log_b"] = g_alog[0].reshape(1, SSD_HEADS), g_alog[1].reshape(1, SSD_HEADS)
    d_dt = jnp.stack(draw).transpose(2, 0, 1, 3).reshape(T, 2 * SSD_HEADS)
    dproj["dt"] = jnp.pad(d_dt, ((0, 0), (0, CHUNK - 2 * SSD_HEADS))).astype(BF16)

    def make_conv_bwd(nsum):
        def fn(i, xe, *rest):
            ds, (cw, cb) = rest[:nsum], rest[nsum:]
            r = _row_ids(i, T, True)
            dact = ds[0]
            for t in ds[1:]:
                dact = dact + t
            dact = jnp.where(r >= PAD_ROWS, dact, 0.0)
            pre = _conv3(xe, cw) + cb
            sg = jax.nn.sigmoid(pre)
            dpre = dact * (sg * (1.0 + pre * (1.0 - sg)))
            n = xe.shape[0]
            dpc = _center(dpre)
            dw = jnp.concatenate([jnp.sum(dpc * _center(pltpu.roll(xe, 1, 0)), axis=0, keepdims=True),
                                  jnp.sum(dpc * _center(xe), axis=0, keepdims=True),
                                  jnp.sum(dpc * _center(pltpu.roll(xe, n - 1, 0)), axis=0, keepdims=True)], axis=0)
            return _center(_conv3_t(dpre, cw)), dw, jnp.sum(dpc, axis=0, keepdims=True)
        return fn

    g_cw, g_cb = {}, {}
    cots = {"xs": [(dxs_dir[0], 512, gcol), (dxs_dir[1], 512, gcol), (dxs_skip, 512, gcol)],
            "B": [(db_dir[0], 512, gcol), (db_dir[1], 512, gcol)],
            "C": [(dc_dir[0], 512, gcol), (dc_dir[1], 512, gcol)]}
    for name in ("xs", "B", "C"):
        wd = conv_w[name].shape[1]
        dproj[name], g_cw[name], g_cb[name] = _rows(
            "ssd_conv_bwd_" + name, make_conv_bwd(len(cots[name])), T, wd // 512,
            [seg(name, 512, gcol)] + cots[name], [(conv_w[name], 512, gcol), (conv_b[name], 512, gcol)],
            [(wd, 512, gcol, BF16)], [(3, wd, 512, gcol), (1, wd, 512, gcol)], halo=True)
    grads["w_ssd_conv"] = jnp.concatenate([g_cw["xs"], g_cw["B"], g_cw["C"]], axis=1)
    grads["b_ssd_conv"] = jnp.concatenate([g_cb["xs"], g_cb["B"], g_cb["C"]], axis=1)

    da_ret = _mm("d_ret_act", dyr, w["w_ret_out"], "nt", out_dtype=BF16)
    grads["w_ret_out"] = _mm("g_ret_out", a_ret, dyr, "tn", out_dtype=BF16)
    tick = early_grads({n: grads.pop(n) for n in ("w_ffn_up_t", "w_ret_out", "w_ssd_out", "w_out", "w_ffn_down")})

    def ret_post_bwd_fn(i, y, g, da, gw):
        _, vjp = jax.vjp(_ret_post, y, g, gw)
        return vjp(da)

    dy_ret, dproj["g"], grads["ret_gn_w"] = _rows(
        "ret_post_bwd", ret_post_bwd_fn, T, 1, [(y_ret, 1024, c0), seg("g", 1024), (da_ret, 1024, c0)],
        [(w["ret_gn_w"] + tick, 1024, c0)], [(1024, 1024, c0, BF16), (1024, 1024, c0, BF16)], [(1, 1024, 1024, c0)])
    dproj["v"] = _retention("retention_dv", kr, qr, dy_ret, T, RET_QK_DIM, RET_V_DIM)
    dqr = _retention("retention_dq", dy_ret, v_at, kr, T, RET_V_DIM, RET_QK_DIM)
    dkr = _retention("retention_dk", v_at, dy_ret, qr, T, RET_V_DIM, RET_QK_DIM)

    def rot_bwd_fn(i, dq, dk, csv, snv):
        parts = [_rot_t(dq[:, h * 128:(h + 1) * 128], csv, snv) for h in range(RET_HEADS)]
        parts += [_rot_t(dk[:, h * 128:(h + 1) * 128] * scale, csv, snv) for h in range(RET_HEADS)]
        return (jnp.concatenate(parts, axis=1),)

    dproj["qk"] = _rows("rotary_bwd", rot_bwd_fn, T, 1, [(dqr, 512, c0), (dkr, 512, c0), (cs, 128, c0), (sn, 128, c0)],
                        [], [(1024, 1024, c0, BF16)])[0]

    d_main = jnp.concatenate([dproj[name].astype(BF16) for name, _, _ in SEGMENTS[:7]], axis=1)
    g_in = [_mm("g_in_main", d_main, u, "tn", out_dtype=BF16),
            _mm("g_in_dt", dproj["dt"], u, "tn", out_dtype=BF16)[:2 * SSD_HEADS],
            _mm("g_in_gates", dproj["gates"], u, "tn", out_dtype=BF16)]
    tick = in_grads(jnp.concatenate(g_in, axis=0))
    du = _mm("d_u_dt", dproj["dt"] + tick.astype(BF16), w_dt, "nn")
    du = _mm("d_u_main", d_main, w_main, "nn", add=du)
    du = _mm("d_u_gates", dproj["gates"], w_gates, "nn", add=du)
    dh0, grads["norm_mix_w"] = norm_bwd("norm_mix_bwd", h0, w["norm_mix_w"], du, dh1)
    grads["meta_tokens"] = dh0[PAD_ROWS:CHUNK]
    return loss, dh0[CHUNK:], grads


MESH_ID = pl.DeviceIdType.MESH
ANY = pl.BlockSpec(memory_space=pl.ANY)


def _me_and_peers():
    x, y, c = lax.axis_index("x"), lax.axis_index("y"), lax.axis_index("c")
    peers = []
    for k in range(1, N_DEV):
        px = 1 - x if k & 4 else x
        py = 1 - y if k & 2 else y
        pc = 1 - c if k & 1 else c
        peers.append(((px, py, pc), 4 * px + 2 * py + pc))
    return 4 * x + 2 * y + c, peers


def _push_blocks(name, src, per_peer):
    blk = src.shape[1:] if per_peer else src.shape

    def body(src_ref, out_ref, send_sems, recv_sems, local_sem):
        me, peers = _me_and_peers()
        mine = src_ref.at[me] if per_peer else src_ref
        local = pltpu.make_async_copy(mine, out_ref.at[me], local_sem)
        local.start()
        sends = []
        for k, (dev, idx) in enumerate(peers):
            cp = pltpu.make_async_remote_copy(
                src_ref=src_ref.at[idx] if per_peer else src_ref, dst_ref=out_ref.at[me],
                send_sem=send_sems.at[k], recv_sem=recv_sems.at[k], device_id=dev, device_id_type=MESH_ID)
            cp.start()
            sends.append(cp)
        for k, (dev, idx) in enumerate(peers):
            pltpu.make_async_remote_copy(
                src_ref=mine, dst_ref=out_ref.at[idx], send_sem=send_sems.at[k], recv_sem=recv_sems.at[k],
                device_id=dev, device_id_type=MESH_ID).wait_recv()
        for cp in sends:
            cp.wait_send()
        local.wait()

    return pl.pallas_call(
        body, name=name, in_specs=[ANY], out_specs=ANY,
        out_shape=jax.ShapeDtypeStruct((N_DEV,) + tuple(blk), src.dtype),
        scratch_shapes=[pltpu.SemaphoreType.DMA((N_DEV - 1,)), pltpu.SemaphoreType.DMA((N_DEV - 1,)),
                        pltpu.SemaphoreType.DMA],
    )(src)


def _gather_two_level(name, src):
    def body(x_ref, out_ref, send_sems, recv_sems, local_sem):
        x, y, c = lax.axis_index("x"), lax.axis_index("y"), lax.axis_index("c")
        me, sibling = (x, y, c), (x, y, 1 - c)
        chips = [(1 - x, y), (x, 1 - y), (1 - x, 1 - y)]

        def rows(px, py, pc):
            return out_ref.at[4 * px + 2 * py + pc]

        def copy(k, block, to, src_ref=None):
            return pltpu.make_async_remote_copy(
                src_ref=rows(*block) if src_ref is None else src_ref, dst_ref=rows(*block),
                send_sem=send_sems.at[k], recv_sem=recv_sems.at[k], device_id=to, device_id_type=MESH_ID)

        mine = pltpu.make_async_copy(x_ref, rows(*me), local_sem)
        mine.start()
        first = [copy(0, me, sibling, x_ref)] + [copy(1 + j, me, (*chip, c), x_ref) for j, chip in enumerate(chips)]
        for cp in first:
            cp.start()
        passed = [copy(4 + j, (*chip, c), sibling) for j, chip in enumerate(chips)]
        for j, chip in enumerate(chips):
            copy(1 + j, (*chip, c), me).wait_recv()
            passed[j].start()
        copy(0, sibling, me).wait_recv()
        for j, chip in enumerate(chips):
            copy(4 + j, (*chip, 1 - c), me).wait_recv()
        for cp in first + passed:
            cp.wait_send()
        mine.wait()

    return pl.pallas_call(
        body, name=name, in_specs=[ANY], out_specs=ANY,
        out_shape=jax.ShapeDtypeStruct((N_DEV,) + tuple(src.shape), src.dtype),
        scratch_shapes=[pltpu.SemaphoreType.DMA((N_DEV - 1,)), pltpu.SemaphoreType.DMA((N_DEV - 1,)),
                        pltpu.SemaphoreType.DMA],
    )(src)


HBM = pl.BlockSpec(memory_space=pltpu.HBM)
SEM = pl.BlockSpec(memory_space=pltpu.SEMAPHORE)
EFFECT = pltpu.SideEffectType.DATAFLOW_SIDE_EFFECTING


def _peer_copy(src_ref, land_ref, send_sems, recv_sems, per_peer, me, a, k, dev, idx, receiving):
    s = a * (N_DEV - 1) + k
    return pltpu.make_async_remote_copy(
        src_ref=src_ref.at[idx] if per_peer else src_ref, dst_ref=land_ref.at[idx if receiving else me],
        send_sem=send_sems.at[s], recv_sem=recv_sems.at[s], device_id=dev, device_id_type=MESH_ID)


def _push_start(name, srcs, per_peer):
    n = len(srcs)
    land_shapes = [(N_DEV,) + tuple(s.shape[1:] if per_peer else s.shape) for s in srcs]

    def body(*refs):
        src_refs, land_refs, send_sems, recv_sems, token = refs[:n], refs[n:2 * n], refs[2 * n], refs[2 * n + 1], refs[-1]
        me, peers = _me_and_peers()
        for a in range(n):
            for k, (dev, idx) in enumerate(peers):
                _peer_copy(src_refs[a], land_refs[a], send_sems, recv_sems, per_peer, me, a, k, dev, idx, False).start()
        token[...] = jnp.zeros_like(token)

    sems = pltpu.SemaphoreType.DMA((n * (N_DEV - 1),))
    res = pl.pallas_call(
        body, name=name,
        out_shape=(sems, sems, *[pltpu.HBM(s.shape, s.dtype) for s in srcs],
                   *[pltpu.HBM(ls, s.dtype) for ls, s in zip(land_shapes, srcs)], jax.ShapeDtypeStruct((8, 128), F32)),
        in_specs=(HBM,) * (2 * n), out_specs=(SEM, SEM) + (HBM,) * (2 * n) + (pl.BlockSpec(memory_space=pltpu.VMEM),),
        input_output_aliases={i: 2 + i for i in range(2 * n)},
        compiler_params=pltpu.CompilerParams(has_side_effects=EFFECT),
    )(*[pltpu.with_memory_space_constraint(s, pltpu.HBM) for s in srcs],
      *[pltpu.with_memory_space_constraint(lax.empty(ls, s.dtype), pltpu.HBM) for ls, s in zip(land_shapes, srcs)])
    return res[0], res[1], res[2:2 + n], res[2 + n:2 + 2 * n], res[-1]


def _push_wait(name, send_sems, recv_sems, srcs_thru, lands_thru, after, per_peer):
    n = len(srcs_thru)

    def body(*refs):
        src_refs, land_refs, send_sems, recv_sems = refs[:n], refs[n:2 * n], refs[2 * n], refs[2 * n + 1]
        me, peers = _me_and_peers()
        for a in range(n):
            for k, (dev, idx) in enumerate(peers):
                cp = _peer_copy(src_refs[a], land_refs[a], send_sems, recv_sems, per_peer, me, a, k, dev, idx, True)
                cp.wait_send()
                cp.wait_recv()

    both = list(srcs_thru) + list(lands_thru)
    res = pl.pallas_call(
        body, name=name, out_shape=tuple(pltpu.HBM(t.shape, t.dtype) for t in both),
        in_specs=(HBM,) * (2 * n) + (SEM, SEM, ANY), out_specs=(HBM,) * (2 * n),
        input_output_aliases={i: i for i in range(2 * n)},
        compiler_params=pltpu.CompilerParams(has_side_effects=EFFECT),
    )(*both, send_sems, recv_sems, after)
    return res[:n], res[n:]


def _sum_blocks(name, blocks):
    _, R, C = blocks.shape
    tc = next(t for t in (1024, 512, 256, 128) if C % t == 0 and (N_DEV * R * t * 2 <= 6 * 2 ** 20 or t == 128))

    def body(b_ref, o_ref):
        acc = b_ref[0].astype(F32)
        for k in range(1, N_DEV):
            acc = acc + b_ref[k].astype(F32)
        o_ref[...] = acc

    return pl.pallas_call(
        body, name=name, grid=(C // tc,), in_specs=[pl.BlockSpec((N_DEV, R, tc), lambda j: (0, 0, j))],
        out_specs=pl.BlockSpec((R, tc), lambda j: (0, j)), out_shape=jax.ShapeDtypeStruct((R, C), F32),
        compiler_params=_params(("arbitrary",)),
    )(blocks)


def _adamw(name, w, g, m, v):
    R, C = w.shape
    tr = R if R <= 512 else _pick(R, (256, 184, 176, 128, 8))
    spec = pl.BlockSpec((tr, C), lambda i: (i, 0))

    def body(w_ref, g_ref, m_ref, v_ref, d_ref, mo_ref, vo_ref):
        gv = g_ref[...]
        mn = ADAM_B1 * m_ref[...] + (1.0 - ADAM_B1) * gv
        vn = ADAM_B2 * v_ref[...] + (1.0 - ADAM_B2) * jnp.square(gv)
        m_hat = mn / (1.0 - ADAM_B1 ** ADAM_STEP)
        v_hat = vn / (1.0 - ADAM_B2 ** ADAM_STEP)
        d_ref[...] = -ADAM_LR * (m_hat / (jnp.sqrt(v_hat) + ADAM_EPS) + ADAM_WD * w_ref[...])
        mo_ref[...] = mn
        vo_ref[...] = vn

    return pl.pallas_call(
        body, name=name, grid=(R // tr,), in_specs=[spec] * 4, out_specs=[spec] * 3,
        out_shape=[jax.ShapeDtypeStruct((R, C), F32)] * 3, compiler_params=_params(("arbitrary",)),
    )(w, g, m, v)


WEIGHTS = ("meta_tokens", "norm_mix_w", "w_in", "ret_gn_w", "w_ret_out", "w_ssd_conv", "b_ssd_conv", "dt_bias_f",
           "dt_bias_b", "a_log_f", "a_log_b", "d_skip", "ssd_norm_w", "w_ssd_out", "w_out", "norm_ffn_w", "w_ffn_up",
           "w_ffn_conv", "b_ffn_conv", "w_ffn_down", "final_norm_w")
BIG = (("w_in", 1288, True), ("w_ffn_up", 704, True), ("w_ret_out", 128, False), ("w_ssd_out", 256, False),
       ("w_out", 128, False), ("w_ffn_down", 352, False))
REPLICATED = ("norm_mix_w", "ret_gn_w", "b_ssd_conv", "dt_bias_f", "dt_bias_b", "a_log_f", "a_log_b", "d_skip",
              "ssd_norm_w", "norm_ffn_w", "b_ffn_conv", "final_norm_w")
SMALL_SHARDED = (("meta_tokens", 16, 1024), ("w_ssd_conv", 3, 3072), ("w_ffn_conv", 3, 5632))


BIG_IN, BIG_REST = BIG[:1], BIG[1:]


def _pack_big(tree, group):
    parts = []
    for name, _, transposed in group:
        a = tree[name][0]
        parts.append(a.T if transposed else a)
    return jnp.concatenate(parts, axis=0)


def _unpack_big(slab, group):
    out, r0 = {}, 0
    for name, r, transposed in group:
        a = slab[r0:r0 + r]
        out[name] = (a.T if transposed else a)[None]
        r0 += r
    return out


def _pack_flat(arrays, rows):
    flat = jnp.concatenate([a.reshape(-1) for a in arrays])
    return jnp.pad(flat, (0, rows * D_MODEL - flat.shape[0])).reshape(rows, D_MODEL)


def _unpack_flat(slab, shapes):
    flat, out, o = slab.reshape(-1), [], 0
    for s in shapes:
        n = math.prod(s)
        out.append(flat[o:o + n].reshape(s))
        o += n
    return out


def kernel(x, meta_tokens, norm_mix_w, w_in, ret_gn_w, w_ret_out, w_ssd_conv, b_ssd_conv, dt_bias_f, dt_bias_b, a_log_f, a_log_b, d_skip, ssd_norm_w, w_ssd_out, w_out, norm_ffn_w, w_ffn_up, w_ffn_conv, b_ffn_conv, w_ffn_down, final_norm_w, loss_target, m_meta_tokens, m_norm_mix_w, m_w_in, m_ret_gn_w, m_w_ret_out, m_w_ssd_conv, m_b_ssd_conv, m_dt_bias_f, m_dt_bias_b, m_a_log_f, m_a_log_b, m_d_skip, m_ssd_norm_w, m_w_ssd_out, m_w_out, m_norm_ffn_w, m_w_ffn_up, m_w_ffn_conv, m_b_ffn_conv, m_w_ffn_down, m_final_norm_w, v_meta_tokens, v_norm_mix_w, v_w_in, v_ret_gn_w, v_w_ret_out, v_w_ssd_conv, v_b_ssd_conv, v_dt_bias_f, v_dt_bias_b, v_a_log_f, v_a_log_b, v_d_skip, v_ssd_norm_w, v_w_ssd_out, v_w_out, v_norm_ffn_w, v_w_ffn_up, v_w_ffn_conv, v_b_ffn_conv, v_w_ffn_down, v_final_norm_w):
    given = dict(locals())
    wt = {n: given[n] for n in WEIGHTS}
    mt = {n: given["m_" + n] for n in WEIGHTS}
    vt = {n: given["v_" + n] for n in WEIGHTS}
    me = 4 * lax.axis_index("x") + 2 * lax.axis_index("y") + lax.axis_index("c")

    small_names = [n for n, _, _ in SMALL_SHARDED]
    small_local = lambda tree: [tree[n].reshape(r, c // N_DEV) for n, r, c in SMALL_SHARDED]
    all_in = _gather_two_level("gather_w_in", _pack_big(wt, BIG_IN).astype(BF16))
    all_s = _push_blocks("gather_small", _pack_flat(small_local(wt), 8), False)
    slab_view = lambda tree, name, transposed: tree[name][0].T if transposed else tree[name][0]
    rest_srcs = [slab_view(wt, name, t).astype(BF16) for name, _, t in BIG_REST]
    rest_srcs, all_in, all_s = lax.optimization_barrier((rest_srcs, all_in, all_s))
    rest_flight = _push_start("gather_rest_start", rest_srcs, False)
    all_s = all_s.reshape(N_DEV, -1)
    full = {"w_in_t": all_in.reshape(-1, D_MODEL)}

    def lands_with_own(flight, after, per_peer, name):
        srcs, lands = _push_wait(name, *flight[:4], after, per_peer)
        own = lambda s: lax.dynamic_slice_in_dim(s, me, 1, axis=0) if per_peer else s[None]
        return [lax.dynamic_update_slice_in_dim(land, own(s), me, axis=0) for s, land in zip(srcs, lands)]

    def late_weights(after):
        lands = lands_with_own(rest_flight, after, False, "gather_rest_wait")
        return {name + ("_t" if t else ""): land.reshape(N_DEV * r, D_MODEL) for (name, r, t), land in zip(BIG_REST, lands)}

    flights = {}

    def start_exchange(key, group, gd):
        srcs = [gd[name + ("_t" if t else "")].astype(BF16).reshape(N_DEV, r, D_MODEL) for name, r, t in group]
        flights[key] = _push_start("exchange_" + key + "_start", srcs, True)
        return flights[key][4][0, 0]

    o = 0
    for name, r, c in SMALL_SHARDED:
        n = r * c // N_DEV
        full[name] = all_s[:, o:o + n].reshape(N_DEV, r, c // N_DEV).transpose(1, 0, 2).reshape(r, c)
        o += n
    for name in REPLICATED:
        full[name] = wt[name]

    grads, delta, new_m, new_v = {}, {}, {}, {}

    def finish_exchange(key, group, after):
        lands = lands_with_own(flights[key], after, True, "exchange_" + key + "_wait")
        for (name, _, transposed), land in zip(group, lands):
            back = (lambda a: a.T[None]) if transposed else (lambda a: a[None])
            g_sum = _sum_blocks("sum_" + name, land)
            d, mn, vn = _adamw("adamw_" + name, slab_view(wt, name, transposed), g_sum,
                               slab_view(mt, name, transposed), slab_view(vt, name, transposed))
            grads[name], delta[name], new_m[name], new_v[name] = back(g_sum), back(d), back(mn), back(vn)

    def in_grads(gi):
        tick = start_exchange("in", BIG_IN, {"w_in_t": gi})
        finish_exchange("rest", BIG_REST, flights["in"][4])
        tick, _ = lax.optimization_barrier((tick, [delta[name] for name, _, _ in BIG_REST]))
        return tick

    loss, grad_x, g = _local_step(x[0], loss_target[0], full, rest_flight[4][0, 0], late_weights,
                                  lambda gd: start_exchange("rest", BIG_REST, gd), in_grads)

    finish_exchange("in", BIG_IN, g["norm_mix_w"])
    small_parts = [g[n] for n in REPLICATED] + [g[n] for n in small_names] + [loss.reshape(1)]
    g_small = _sum_blocks("sum_small", _push_blocks("gather_small_grads", _pack_flat(small_parts, 64), False))
    small_red = _unpack_flat(g_small, [wt[n].shape for n in REPLICATED] + [(r, c) for _, r, c in SMALL_SHARDED] + [(1,)])
    grads.update(zip(REPLICATED, small_red[:len(REPLICATED)]))
    for (name, r, c), red in zip(SMALL_SHARDED, small_red[len(REPLICATED):-1]):
        grads[name] = lax.dynamic_slice(red, (0, me * (c // N_DEV)), (r, c // N_DEV)).reshape(wt[name].shape)
    loss_all = small_red[-1][0]

    rest = list(REPLICATED) + small_names
    shapes = [wt[n].shape for n in rest]
    pack_rest = lambda tree: _pack_flat([tree[n] for n in rest], 24)
    d_rest, m_rest, v_rest = _adamw("adamw_small", pack_rest(wt), pack_rest(grads), pack_rest(mt), pack_rest(vt))
    delta.update(zip(rest, _unpack_flat(d_rest, shapes)))
    new_m.update(zip(rest, _unpack_flat(m_rest, shapes)))
    new_v.update(zip(rest, _unpack_flat(v_rest, shapes)))

    return (loss_all, grad_x[None], *[grads[n] for n in WEIGHTS], *[delta[n] for n in WEIGHTS],
            *[new_m[n] for n in WEIGHTS], *[new_v[n] for n in WEIGHTS])
```

```python
import functools
import math

import jax
import jax.numpy as jnp
from jax import lax
from jax.experimental import pallas as pl
from jax.experimental.pallas import tpu as pltpu

F32 = jnp.float32
BF16 = jnp.bfloat16

D_MODEL = 1024
CHUNK = 128
N_META = 16
PAD_ROWS = CHUNK - N_META
RET_HEADS = 4
RET_QK_DIM = 128
RET_V_DIM = 256
SSD_HEADS = 32
SSD_HEAD_DIM = 64
SSD_GROUPS = 4
SSD_STATE = 128
HEADS_PER_GROUP = SSD_HEADS // SSD_GROUPS
PAIRS_PER_GROUP = HEADS_PER_GROUP // 2
D_FF = 2816
EPS = 1e-6
ROPE_BASE = 10000.0
N_DEV = 8

ADAM_LR = 0.001
ADAM_B1 = 0.9
ADAM_B2 = 0.999
ADAM_EPS = 1e-08
ADAM_WD = 0.01
ADAM_STEP = 10

VMEM_LIMIT = 56 * 1024 * 1024
HALO = 16
HIGHEST = lax.Precision.HIGHEST

SEGMENTS = (("qk", 0, 1024), ("v", 1024, 2048), ("g", 2048, 3072), ("z", 3072, 5120), ("xs", 5120, 7168),
            ("B", 7168, 7680), ("C", 7680, 8192), ("dt", 8192, 8256), ("gates", 8256, 10304))


def _pick(n, cands):
    for c in cands:
        if n % c == 0:
            return c
    raise ValueError(f"no tile for {n}")


def _params(sem):
    return pltpu.CompilerParams(dimension_semantics=sem, vmem_limit_bytes=VMEM_LIMIT)


def _dot(a, b, dims=(((1,), (0,)), ((), ())), precision=None):
    return lax.dot_general(a, b, dims, preferred_element_type=F32, precision=precision)


def _dot_nt(a, b):
    return _dot(a, b, (((1,), (1,)), ((), ())))


def _dot_tn(a, b):
    return _dot(a, b, (((0,), (0,)), ((), ())))


def _mm(name, a, b, mode, add=None, out_dtype=F32):
    if mode == "nn":
        (M, K), N = a.shape, b.shape[1]
    elif mode == "nt":
        (M, K), N = a.shape, b.shape[0]
    else:
        (K, M), N = a.shape, b.shape[1]
    tn = _pick(N, (1408, 1024, 512, 128, 64))
    if mode == "tn":
        tm = M if M <= 1024 else _pick(M, (1408, 1024))
        tk = _pick(K, (2112, 512, 256, 128))
    else:
        tm = _pick(M, (1056, 512, 256, 128))
        tk = K if K <= 2816 else _pick(K, (2048, 1408, 1024))
    nk = K // tk
    if mode == "nn":
        a_spec = pl.BlockSpec((tm, tk), lambda n, m, k: (m, k))
        b_spec = pl.BlockSpec((tk, tn), lambda n, m, k: (k, n))
        dims = (((1,), (0,)), ((), ()))
    elif mode == "nt":
        a_spec = pl.BlockSpec((tm, tk), lambda n, m, k: (m, k))
        b_spec = pl.BlockSpec((tn, tk), lambda n, m, k: (n, k))
        dims = (((1,), (1,)), ((), ()))
    else:
        a_spec = pl.BlockSpec((tk, tm), lambda n, m, k: (k, m))
        b_spec = pl.BlockSpec((tk, tn), lambda n, m, k: (k, n))
        dims = (((0,), (0,)), ((), ()))
    o_spec = pl.BlockSpec((tm, tn), lambda n, m, k: (m, n))
    in_specs = [a_spec, b_spec] + ([o_spec] if add is not None else [])
    args = [a, b] + ([add] if add is not None else [])

    def body(*refs):
        if add is not None:
            a_ref, b_ref, r_ref, o_ref, acc = refs
        else:
            a_ref, b_ref, o_ref, acc = refs
        k = pl.program_id(2)
        p = _dot(a_ref[...].astype(BF16), b_ref[...].astype(BF16), dims)

        def finish(r):
            if add is not None:
                r = r + r_ref[...]
            o_ref[...] = r.astype(out_dtype)

        if nk == 1:
            finish(p)
        else:
            @pl.when(k == 0)
            def _():
                acc[...] = p

            @pl.when(k > 0)
            def _():
                acc[...] += p

            @pl.when(k == nk - 1)
            def _():
                finish(acc[...])

    return pl.pallas_call(
        body, name=name, grid=(N // tn, M // tm, nk), in_specs=in_specs, out_specs=o_spec,
        out_shape=jax.ShapeDtypeStruct((M, N), out_dtype),
        scratch_shapes=[pltpu.VMEM((tm, tn) if nk > 1 else (8, 128), F32)],
        compiler_params=_params(("arbitrary", "arbitrary", "arbitrary")),
    )(*args)


ANY_SPACE = pl.BlockSpec(memory_space=pl.ANY)


def _const(c):
    return lambda j: c


def _rows(name, fn, T, ncol, ins, params, outs, accs=(), halo=False):
    tm = _pick(T, (384, 256, 128))
    R = T // tm
    hb = tm // HALO
    in_specs, args = [], []
    for spec in ins:
        arr, w, cf = spec[:3]
        lead = spec[3] if len(spec) > 3 else None
        if lead is None:
            mk = lambda blk, rf, cf=cf: pl.BlockSpec(blk, lambda j, i: (rf(i), cf(j)))
            shape = lambda r, w=w: (r, w)
        else:
            mk = lambda blk, rf, cf=cf, lead=lead: pl.BlockSpec(blk, lambda j, i: (lead, rf(i), cf(j)))
            shape = lambda r, w=w: (None, r, w)
        in_specs.append(mk(shape(tm), lambda i: i))
        args.append(arr)
        if halo:
            in_specs.append(mk(shape(HALO), lambda i: jnp.maximum(i * hb - 1, 0)))
            in_specs.append(mk(shape(HALO), lambda i: jnp.minimum((i + 1) * hb, T // HALO - 1)))
            args += [arr, arr]
    for arr, w, cf in params:
        in_specs.append(pl.BlockSpec((arr.shape[0], w), lambda j, i, cf=cf: (0, cf(j))))
        args.append(arr)
    out_shape, out_specs, aliases = [], [], {}
    for k, (tw, w, cf, dt) in enumerate(outs):
        if not isinstance(tw, int):
            aliases[len(args)] = k
            in_specs.append(ANY_SPACE)
            args.append(tw)
            tw = tw.shape[1]
        out_shape.append(jax.ShapeDtypeStruct((T, tw), dt))
        out_specs.append(pl.BlockSpec((tm, w), lambda j, i, cf=cf: (i, cf(j))))
    for r, tw, w, cf in accs:
        out_shape.append(jax.ShapeDtypeStruct((r, tw), F32))
        out_specs.append(pl.BlockSpec((r, w), lambda j, i, cf=cf: (0, cf(j))))
    n_in, n_par, n_out, n_acc, n_alias = len(ins), len(params), len(outs), len(accs), len(aliases)

    def body(*refs):
        i = pl.program_id(1)
        vals, p = [], 0
        for _ in range(n_in):
            if halo:
                before = jnp.where(i > 0, refs[p + 1][...], jnp.zeros_like(refs[p + 1]))
                after = jnp.where(i < R - 1, refs[p + 2][...], jnp.zeros_like(refs[p + 2]))
                vals.append(jnp.concatenate([before, refs[p][...], after], axis=0).astype(F32))
                p += 3
            else:
                vals.append(refs[p][...].astype(F32))
                p += 1
        pvals = [refs[p + k][...] for k in range(n_par)]
        p += n_par + n_alias
        res = fn(i, *vals, *pvals)
        for k in range(n_out):
            refs[p + k][...] = res[k].astype(refs[p + k].dtype)
        p += n_out
        for k in range(n_acc):
            ref, v = refs[p + k], res[n_out + k]

            @pl.when(i == 0)
            def _(ref=ref, v=v):
                ref[...] = v

            @pl.when(i > 0)
            def _(ref=ref, v=v):
                ref[...] += v

    res = pl.pallas_call(
        body, name=name, grid=(ncol, R), in_specs=in_specs, out_specs=out_specs, out_shape=out_shape,
        input_output_aliases=aliases, compiler_params=_params(("arbitrary", "arbitrary")),
    )(*args)
    return res


def _tile_rows(T):
    return _pick(T, (384, 256, 128))


def _row_ids(i, T, halo=False):
    tm = _tile_rows(T)
    if halo:
        return i * tm - HALO + lax.broadcasted_iota(jnp.int32, (tm + 2 * HALO, 1), 0)
    return i * tm + lax.broadcasted_iota(jnp.int32, (tm, 1), 0)


def _rms(x, w):
    return x * lax.rsqrt(jnp.mean(x * x, axis=-1, keepdims=True) + EPS) * w


def _silu(x):
    return x * jax.nn.sigmoid(x)


def _conv3(x, w):
    n = x.shape[0]
    return w[0:1] * pltpu.roll(x, 1, 0) + w[1:2] * x + w[2:3] * pltpu.roll(x, n - 1, 0)


def _conv3_t(d, w):
    n = d.shape[0]
    return w[0:1] * pltpu.roll(d, n - 1, 0) + w[1:2] * d + w[2:3] * pltpu.roll(d, 1, 0)


def _center(x):
    return x[HALO:x.shape[0] - HALO]


def _retention(name, a, b, v, T, da, dv, into=None):
    (a, a0), (b, b0), (v, v0) = [t if isinstance(t, tuple) else (t, 0) for t in (a, b, v)]
    nc = T // CHUNK
    log_gammas = [math.log(1.0 - 2.0 ** (-5.0 - h)) for h in range(RET_HEADS)]

    def body(*refs):
        a_ref, b_ref, v_ref = refs[:3]
        out_ref, o_ref, st, st_b = refs[-4:]
        h = pl.program_id(0)
        lg = jnp.float32(log_gammas[RET_HEADS - 1])
        for k in range(RET_HEADS - 2, -1, -1):
            lg = jnp.where(h == k, jnp.float32(log_gammas[k]), lg)
        li = lax.broadcasted_iota(jnp.int32, (CHUNK, CHUNK), 0)
        si = lax.broadcasted_iota(jnp.int32, (CHUNK, CHUNK), 1)
        dmat = jnp.exp(lg * jnp.abs(li - si).astype(F32))
        pos = lax.broadcasted_iota(jnp.int32, (CHUNK, 1), 0).astype(F32)
        kdec_f = jnp.exp((CHUNK - 1 - pos) * lg)
        qdec_f = jnp.exp((pos + 1) * lg)
        kdec_b = jnp.exp(pos * lg)
        qdec_b = jnp.exp((CHUNK - pos) * lg)
        cdec = jnp.exp(CHUNK * lg)

        def rows(n):
            return pl.ds(pl.multiple_of(n * CHUNK, CHUNK), CHUNK)

        st[...] = jnp.zeros_like(st)
        st_b[...] = jnp.zeros_like(st_b)
        o_ref[...] = jnp.zeros_like(o_ref)

        def step(m, carry):
            r = rows(m)
            av, bv, vv = a_ref[r, :], b_ref[r, :], v_ref[r, :].astype(BF16)
            s = _dot_nt(av.astype(BF16), bv.astype(BF16)) * dmat
            o_ref[r, :] += _dot(s.astype(BF16), vv) + _dot((av * qdec_f).astype(BF16), st[...].astype(BF16))
            st[...] = cdec * st[...] + _dot_tn((bv * kdec_f).astype(BF16), vv)
            r = rows(nc - 1 - m)
            av, bv, vv = a_ref[r, :], b_ref[r, :], v_ref[r, :].astype(BF16)
            o_ref[r, :] += _dot((av * qdec_b).astype(BF16), st_b[...].astype(BF16))
            st_b[...] = cdec * st_b[...] + _dot_tn((bv * kdec_b).astype(BF16), vv)
            return carry

        lax.fori_loop(0, nc, step, 0, unroll=3 if nc % 3 == 0 else 1)
        out_ref[...] = o_ref[...].astype(out_ref.dtype)

    in_specs = [pl.BlockSpec((T, da), lambda h: (0, a0 // da + h)), pl.BlockSpec((T, da), lambda h: (0, b0 // da + h)),
                pl.BlockSpec((T, dv), lambda h: (0, v0 // dv + h))]
    if into is None:
        args, o0, aliases = (a, b, v), 0, {}
        out_shape = jax.ShapeDtypeStruct((T, RET_HEADS * dv), F32)
    else:
        args, o0, aliases = (a, b, v, into[0]), into[1], {3: 0}
        in_specs.append(ANY_SPACE)
        out_shape = jax.ShapeDtypeStruct(into[0].shape, into[0].dtype)
    return pl.pallas_call(
        body, name=name, grid=(RET_HEADS,), in_specs=in_specs,
        out_specs=pl.BlockSpec((T, dv), lambda h: (0, o0 // dv + h)), out_shape=out_shape,
        input_output_aliases=aliases,
        scratch_shapes=[pltpu.VMEM((T, dv), F32), pltpu.VMEM((da, dv), F32), pltpu.VMEM((da, dv), F32)],
        compiler_params=_params(("arbitrary",)),
    )(*args)


def _softplus(x):
    return jnp.maximum(x, 0.0) + jnp.log1p(jnp.exp(-jnp.abs(x)))


def _lane_lo():
    return lax.broadcasted_iota(jnp.int32, (1, CHUNK), 1) < SSD_HEAD_DIM


def _pair_cols(col, j):
    return jnp.where(_lane_lo(), col[:, 2 * j:2 * j + 1], col[:, 2 * j + 1:2 * j + 2])


def _pair_rows(colr, j):
    lo = lax.broadcasted_iota(jnp.int32, (CHUNK, 1), 0) < SSD_HEAD_DIM
    return jnp.where(lo, colr[2 * j:2 * j + 1, :], colr[2 * j + 1:2 * j + 2, :])


def _onehot8(h):
    return (lax.broadcasted_iota(jnp.int32, (1, HEADS_PER_GROUP), 1) == h).astype(F32)


def _ssd_pre(d, c, rawc, rawr, bc, br, alc, alr):
    li = lax.broadcasted_iota(jnp.int32, (CHUNK, CHUNK), 0)
    si = lax.broadcasted_iota(jnp.int32, (CHUNK, CHUNK), 1)
    dif = li - si if d == 0 else si - li
    mask = dif >= 0
    mask_t = dif <= 0
    rowc = c * CHUNK + lax.broadcasted_iota(jnp.int32, (CHUNK, 1), 0)
    rowr = c * CHUNK + lax.broadcasted_iota(jnp.int32, (1, CHUNK), 1)
    dtc = jnp.where(rowc >= PAD_ROWS, _softplus(rawc + bc), 0.0)
    dtr = jnp.where(rowr >= PAD_ROWS, _softplus(rawr + br), 0.0)
    ac = -jnp.exp(alc)
    ar = -jnp.exp(alr)
    dlc = dtc * ac
    dlr = dtr * ar
    alpc = _dot(mask.astype(F32), dlc, precision=HIGHEST)
    alpr = _dot(dlr, mask_t.astype(F32), precision=HIGHEST)
    endc = jnp.sum(dlc, axis=0, keepdims=True)
    endr = jnp.sum(dlr, axis=1, keepdims=True)
    return dict(mask=mask, mask_t=mask_t, dtc=dtc, ac=ac, alpc=alpc, alpr=alpr, endc=endc, endr=endr,
                valid=rowc >= PAD_ROWS)


def _chunk_of(d, n, nc):
    return n + d * (nc - 1 - 2 * n)


GROUP_WIDTH = HEADS_PER_GROUP * SSD_HEAD_DIM


def _ssd_in_specs(d, cfn):
    return [
        pl.BlockSpec((CHUNK, GROUP_WIDTH), lambda g, n: (cfn(d, n), g)),
        pl.BlockSpec((CHUNK, SSD_STATE), lambda g, n: (cfn(d, n), g)),
        pl.BlockSpec((CHUNK, SSD_STATE), lambda g, n: (cfn(d, n), g)),
        pl.BlockSpec((None, None, CHUNK, HEADS_PER_GROUP), lambda g, n: (d, g, cfn(d, n), 0)),
        pl.BlockSpec((None, None, HEADS_PER_GROUP, CHUNK), lambda g, n: (d, g, 0, cfn(d, n))),
        pl.BlockSpec((None, None, 1, HEADS_PER_GROUP), lambda g, n: (d, g, 0, 0)),
        pl.BlockSpec((None, None, HEADS_PER_GROUP, 1), lambda g, n: (d, g, 0, 0)),
        pl.BlockSpec((None, None, 1, HEADS_PER_GROUP), lambda g, n: (d, g, 0, 0)),
        pl.BlockSpec((None, None, HEADS_PER_GROUP, 1), lambda g, n: (d, g, 0, 0)),
    ]


N_SSD_IN = 9


def _ssd_fwd(xs, bm, cm, small, T):
    nc = T // CHUNK
    cfn = lambda d, n: _chunk_of(d, n, nc)

    def one_direction(d, n, ins, y_ref, hs_ref, h_scr):
        x_ref, b_ref, c_ref, rawc_ref, rawr_ref, bc_ref, br_ref, alc_ref, alr_ref = ins
        c = cfn(d, n)
        q = _ssd_pre(d, c, rawc_ref[...], rawr_ref[...], bc_ref[...], br_ref[...], alc_ref[...], alr_ref[...])
        bv = b_ref[...].astype(BF16)
        cv = c_ref[...].astype(BF16)
        cb = _dot_nt(cv, bv)
        lo = _lane_lo()
        for j in range(PAIRS_PER_GROUP):
            xp = x_ref[:, j * CHUNK:(j + 1) * CHUNK]
            xd = xp * _pair_cols(q["dtc"], j)
            xdb = xd.astype(BF16)
            yi = []
            for e in range(2):
                h = 2 * j + e
                lm = jnp.exp(jnp.where(q["mask"], q["alpc"][:, h:h + 1] - q["alpr"][h:h + 1, :], -jnp.inf))
                yi.append(_dot((cb * lm).astype(BF16), xdb))
            alp = _pair_cols(q["alpc"], j)
            hp = h_scr[j]
            hs_ref[j] = hp
            yo = jnp.exp(alp) * _dot_nt(cv, hp.astype(BF16))
            y_ref[:, j * CHUNK:(j + 1) * CHUNK] = (jnp.where(lo, yi[0], yi[1]) + yo).astype(y_ref.dtype)
            de = jnp.exp(_pair_cols(q["endc"], j) - alp)
            h_scr[j] = jnp.exp(_pair_rows(q["endr"], j)) * hp + _dot_tn((xd * de).astype(BF16), bv)

    def body(*refs):
        n = pl.program_id(1)
        ins, (y_f, y_b, hs_f, hs_b, h_scr) = refs[:2 * N_SSD_IN], refs[2 * N_SSD_IN:]

        @pl.when(n == 0)
        def _():
            h_scr[...] = jnp.zeros_like(h_scr)

        one_direction(0, n, ins[:N_SSD_IN], y_f, hs_f, h_scr.at[0])
        one_direction(1, n, ins[N_SSD_IN:], y_b, hs_b, h_scr.at[1])

    y_spec = lambda d: pl.BlockSpec((CHUNK, GROUP_WIDTH), lambda g, n: (cfn(d, n), g))
    hs_spec = lambda d: pl.BlockSpec((None, None, PAIRS_PER_GROUP, CHUNK, SSD_STATE),
                                     lambda g, n: (g, cfn(d, n), 0, 0, 0))
    y_shape = jax.ShapeDtypeStruct((T, SSD_HEADS * SSD_HEAD_DIM), BF16)
    hs_shape = jax.ShapeDtypeStruct((SSD_GROUPS, nc, PAIRS_PER_GROUP, CHUNK, SSD_STATE), F32)
    y_f, y_b, hs_f, hs_b = pl.pallas_call(
        body, name="ssd_fwd", grid=(SSD_GROUPS, nc),
        in_specs=_ssd_in_specs(0, cfn) + _ssd_in_specs(1, cfn),
        out_specs=[y_spec(0), y_spec(1), hs_spec(0), hs_spec(1)],
        out_shape=[y_shape, y_shape, hs_shape, hs_shape],
        scratch_shapes=[pltpu.VMEM((2, PAIRS_PER_GROUP, CHUNK, SSD_STATE), F32)],
        compiler_params=_params(("arbitrary", "arbitrary")),
    )(xs, bm, cm, *small, xs, bm, cm, *small)
    return (y_f, y_b), (hs_f, hs_b)


def _ssd_bwd(xs, bm, cm, small, hs, dy, T):
    nc = T // CHUNK
    cfn = lambda d, n: _chunk_of(1 - d, n, nc)

    def one_direction(d, n, ins, outs, dh_scr):
        x_ref, b_ref, c_ref, rawc_ref, rawr_ref, bc_ref, br_ref, alc_ref, alr_ref, hs_ref, dy_ref = ins
        dx_ref, db_ref, dc_ref, draw_ref, dbias_ref, dalog_ref = outs
        c = cfn(d, n)
        rawc, bc = rawc_ref[...], bc_ref[...]
        q = _ssd_pre(d, c, rawc, rawr_ref[...], bc, br_ref[...], alc_ref[...], alr_ref[...])
        b32, c32 = b_ref[...], c_ref[...]
        bv, cv = b32.astype(BF16), c32.astype(BF16)
        cb = _dot_nt(cv, bv)
        cbt = _dot_nt(bv, cv)
        lo = _lane_lo()
        row_lo = lax.broadcasted_iota(jnp.int32, (CHUNK, 1), 0) < SSD_HEAD_DIM
        dcb = jnp.zeros((CHUNK, CHUNK), F32)
        dcp = jnp.zeros((CHUNK, SSD_STATE), F32)
        dbp = jnp.zeros((CHUNK, SSD_STATE), F32)
        dalp = jnp.zeros((CHUNK, HEADS_PER_GROUP), F32)
        dend = jnp.zeros((1, HEADS_PER_GROUP), F32)
        ddtx = jnp.zeros((CHUNK, HEADS_PER_GROUP), F32)

        def half_sums(t):
            return (jnp.sum(jnp.where(lo, t, 0.0), axis=1, keepdims=True),
                    jnp.sum(jnp.where(lo, 0.0, t), axis=1, keepdims=True))

        for j in range(PAIRS_PER_GROUP):
            xp = x_ref[:, j * CHUNK:(j + 1) * CHUNK]
            dtp = _pair_cols(q["dtc"], j)
            xd = xp * dtp
            xdb = xd.astype(BF16)
            dyp = dy_ref[:, j * CHUNK:(j + 1) * CHUNK]
            dyb = dyp.astype(BF16)
            hn = hs_ref[j]
            hnb = hn.astype(BF16)
            dh1 = dh_scr[j]
            dh1b = dh1.astype(BF16)
            alp = _pair_cols(q["alpc"], j)
            ea = jnp.exp(alp)
            de = jnp.exp(_pair_cols(q["endc"], j) - alp)
            dxi = []
            for e in range(2):
                h = 2 * j + e
                diff = q["alpc"][:, h:h + 1] - q["alpr"][h:h + 1, :]
                lm = jnp.exp(jnp.where(q["mask"], diff, -jnp.inf))
                mt = cbt * jnp.exp(jnp.where(q["mask_t"], -diff, -jnp.inf))
                dxi.append(_dot(mt.astype(BF16), dyb))
                dyeb_h = (jnp.where(lo, dyp, 0.0) if e == 0 else jnp.where(lo, 0.0, dyp)).astype(BF16)
                gl = _dot_nt(dyeb_h, xdb) * lm
                dcb = dcb + gl
                ra = jnp.sum(gl * cb - _dot_nt(xdb, dyeb_h) * mt, axis=1, keepdims=True)
                dalp = dalp + ra * _onehot8(h)
            y_off = ea * _dot_nt(cv, hnb)
            dxs_state = de * _dot_nt(bv, dh1b)
            dxd = jnp.where(lo, dxi[0], dxi[1]) + dxs_state
            dyeb = (dyp * ea).astype(BF16)
            dcp = dcp + _dot(dyeb, hnb)
            dbp = dbp + _dot((xd * de).astype(BF16), dh1b)
            dh_scr[j] = jnp.exp(_pair_rows(q["endr"], j)) * dh1 + _dot_tn(dyeb, cv)
            r0, r1 = half_sums(dyp * y_off - xd * dxs_state)
            dalp = dalp + r0 * _onehot8(2 * j) + r1 * _onehot8(2 * j + 1)
            t0, t1 = half_sums(jnp.sum(xd * dxs_state, axis=0, keepdims=True))
            u = dh1 * hn
            u0 = jnp.sum(jnp.sum(jnp.where(row_lo, u, 0.0), axis=0, keepdims=True), axis=1, keepdims=True)
            u1 = jnp.sum(jnp.sum(jnp.where(row_lo, 0.0, u), axis=0, keepdims=True), axis=1, keepdims=True)
            eend = jnp.exp(q["endc"])
            dend = dend + (t0 + eend * u0) * _onehot8(2 * j) + (t1 + eend * u1) * _onehot8(2 * j + 1)
            dx_ref[:, j * CHUNK:(j + 1) * CHUNK] = (dxd * dtp).astype(dx_ref.dtype)
            w0, w1 = half_sums(dxd * xp)
            ddtx = ddtx + w0 * _onehot8(2 * j) + w1 * _onehot8(2 * j + 1)

        dcbb = dcb.astype(BF16)
        dc_ref[...] = (dcp + _dot(dcbb, bv)).astype(dc_ref.dtype)
        db_ref[...] = (dbp + _dot_tn(dcbb, cv)).astype(db_ref.dtype)
        ddl = _dot(q["mask_t"].astype(F32), dalp, precision=HIGHEST) + dend
        ddt = ddl * q["ac"] + ddtx
        draw = jnp.where(q["valid"], ddt * jax.nn.sigmoid(rawc + bc), 0.0)
        draw_ref[...] = draw
        dbias = jnp.sum(draw, axis=0, keepdims=True)
        dalog = jnp.sum(ddl * q["dtc"], axis=0, keepdims=True) * q["ac"]

        @pl.when(n == 0)
        def _():
            dbias_ref[...] = dbias
            dalog_ref[...] = dalog

        @pl.when(n > 0)
        def _():
            dbias_ref[...] += dbias
            dalog_ref[...] += dalog

    n_in, n_out = N_SSD_IN + 2, 6

    def body(*refs):
        n = pl.program_id(1)
        ins, outs, dh_scr = refs[:2 * n_in], refs[2 * n_in:2 * (n_in + n_out)], refs[-1]

        @pl.when(n == 0)
        def _():
            dh_scr[...] = jnp.zeros_like(dh_scr)

        one_direction(0, n, ins[:n_in], outs[:n_out], dh_scr.at[0])
        one_direction(1, n, ins[n_in:], outs[n_out:], dh_scr.at[1])

    def in_specs(d):
        return _ssd_in_specs(d, cfn) + [
            pl.BlockSpec((None, None, PAIRS_PER_GROUP, CHUNK, SSD_STATE), lambda g, n: (g, cfn(d, n), 0, 0, 0)),
            pl.BlockSpec((CHUNK, GROUP_WIDTH), lambda g, n: (cfn(d, n), g))]

    def out_specs(d):
        acc = pl.BlockSpec((None, 1, HEADS_PER_GROUP), lambda g, n: (g, 0, 0))
        return [pl.BlockSpec((CHUNK, GROUP_WIDTH), lambda g, n: (cfn(d, n), g)),
                pl.BlockSpec((CHUNK, SSD_STATE), lambda g, n: (cfn(d, n), g)),
                pl.BlockSpec((CHUNK, SSD_STATE), lambda g, n: (cfn(d, n), g)),
                pl.BlockSpec((None, CHUNK, HEADS_PER_GROUP), lambda g, n: (g, cfn(d, n), 0)), acc, acc]

    out_shape = [jax.ShapeDtypeStruct((T, SSD_HEADS * SSD_HEAD_DIM), BF16),
                 jax.ShapeDtypeStruct((T, SSD_GROUPS * SSD_STATE), BF16),
                 jax.ShapeDtypeStruct((T, SSD_GROUPS * SSD_STATE), BF16),
                 jax.ShapeDtypeStruct((SSD_GROUPS, T, HEADS_PER_GROUP), F32),
                 jax.ShapeDtypeStruct((SSD_GROUPS, 1, HEADS_PER_GROUP), F32),
                 jax.ShapeDtypeStruct((SSD_GROUPS, 1, HEADS_PER_GROUP), F32)]
    res = pl.pallas_call(
        body, name="ssd_bwd", grid=(SSD_GROUPS, nc),
        in_specs=in_specs(0) + in_specs(1), out_specs=out_specs(0) + out_specs(1), out_shape=out_shape * 2,
        scratch_shapes=[pltpu.VMEM((2, PAIRS_PER_GROUP, CHUNK, SSD_STATE), F32)],
        compiler_params=_params(("arbitrary", "arbitrary")),
    )(xs, bm, cm, *small, hs[0], dy, xs, bm, cm, *small, hs[1], dy)
    return [(res[k], res[n_out + k]) for k in range(n_out)]


def _rot(x, cs, sn):
    return x * cs + pltpu.roll(x, RET_QK_DIM // 2, 1) * sn


def _rot_t(d, cs, sn):
    return d * cs + pltpu.roll(d * sn, RET_QK_DIM // 2, 1)


def _ret_post(y, g, w):
    parts = []
    for h in range(RET_HEADS):
        yh = y[:, h * RET_V_DIM:(h + 1) * RET_V_DIM]
        mu = jnp.mean(yh, axis=-1, keepdims=True)
        var = jnp.mean(jnp.square(yh - mu), axis=-1, keepdims=True)
        parts.append((yh - mu) * lax.rsqrt(var + EPS))
    return _silu(g) * (jnp.concatenate(parts, axis=1) * w)


def _ssd_post(yf, yb, xs, z, dskip, w):
    y = (yf + yb + xs * dskip) * _silu(z)
    return y * lax.rsqrt(jnp.mean(y * y, axis=-1, keepdims=True) + EPS) * w


def _merge(gates, yr, ys, valid):
    m = jax.nn.sigmoid(gates[:, :D_MODEL]) * yr + jax.nn.sigmoid(gates[:, D_MODEL:]) * ys
    return jnp.where(valid, m, 0.0)


def _rope_tables(T):
    half = RET_QK_DIM // 2
    inv = ROPE_BASE ** (-jnp.arange(half, dtype=F32) / half)
    pos = (jnp.arange(T) - PAD_ROWS).astype(F32)
    ang = pos[:, None] * inv[None, :]
    cos, sin = jnp.cos(ang), jnp.sin(ang)
    return jnp.concatenate([cos, cos], axis=1), jnp.concatenate([-sin, sin], axis=1)


def _per_group(v):
    c = v.reshape(SSD_GROUPS, 1, HEADS_PER_GROUP)
    return c, c.reshape(SSD_GROUPS, HEADS_PER_GROUP, 1)


def _local_step(x, target, w, tick, late_weights, early_grads, in_grads):
    S = x.shape[0]
    T = S + CHUNK
    tm = _tile_rows(T)
    c0 = _const(0)

    h0 = jnp.concatenate([jnp.zeros((PAD_ROWS, D_MODEL), F32), w["meta_tokens"], x], axis=0)
    tgt = jnp.concatenate([jnp.zeros((CHUNK, D_MODEL), F32), target], axis=0)
    seg_at = {name: a for name, a, _ in SEGMENTS}
    w_main = w["w_in_t"][:seg_at["dt"]]
    w_dt = jnp.pad(w["w_in_t"][seg_at["dt"]:seg_at["gates"]], ((0, CHUNK - 2 * SSD_HEADS), (0, 0)))
    w_gates = w["w_in_t"][seg_at["gates"]:]

    def norm_cast(name, h, nw):
        return _rows(name, lambda i, hv, wv: (_rms(hv, wv),), T, 1, [(h, D_MODEL, c0)], [(nw, D_MODEL, c0)],
                     [(D_MODEL, D_MODEL, c0, BF16)])[0]

    u = norm_cast("norm_mix", h0, w["norm_mix_w"] + tick)
    p_main = _mm("proj_main", u, w_main, "nt", out_dtype=BF16)
    p_dt = _mm("proj_dt", u, w_dt, "nt")
    p_gates = _mm("proj_gates", u, w_gates, "nt", out_dtype=BF16)

    def seg(name, width, cf=c0):
        base = seg_at[name] // width
        return (p_main, width, lambda j: base + cf(j))

    cs, sn = _rope_tables(T)
    scale = RET_QK_DIM ** -0.5

    def rot_fn(i, qk, csv, snv):
        q = [_rot(qk[:, h * 128:(h + 1) * 128], csv, snv) for h in range(RET_HEADS)]
        k = [_rot(qk[:, (RET_HEADS + h) * 128:(RET_HEADS + h + 1) * 128], csv, snv) * scale for h in range(RET_HEADS)]
        return jnp.concatenate(q, axis=1), jnp.concatenate(k, axis=1)

    qr, kr = _rows("rotary", rot_fn, T, 1, [seg("qk", 1024), (cs, 128, c0), (sn, 128, c0)], [],
                   [(512, 512, c0, F32), (512, 512, c0, F32)])
    v_at = (p_main, seg_at["v"])
    y_ret = _retention("retention", qr, kr, v_at, T, RET_QK_DIM, RET_V_DIM)
    a_ret = _rows("ret_post", lambda i, y, g, gw: (_ret_post(y, g, gw),), T, 1,
                  [(y_ret, 1024, c0), seg("g", 1024)], [(w["ret_gn_w"], 1024, c0)],
                  [(1024, 1024, c0, BF16)])[0]

    conv_w = {"xs": w["w_ssd_conv"][:, :2048], "B": w["w_ssd_conv"][:, 2048:2560], "C": w["w_ssd_conv"][:, 2560:]}
    conv_b = {"xs": w["b_ssd_conv"][:, :2048], "B": w["b_ssd_conv"][:, 2048:2560], "C": w["b_ssd_conv"][:, 2560:]}

    def ssd_conv_fn(i, xe, cw, cb):
        r = _row_ids(i, T, True)
        return (_center(jnp.where(r >= PAD_ROWS, _silu(_conv3(xe, cw) + cb), 0.0)),)

    act = {}
    for name in ("xs", "B", "C"):
        wd = conv_w[name].shape[1]
        cw = 512
        act[name] = _rows("ssd_conv_" + name, ssd_conv_fn, T, wd // cw, [seg(name, cw, lambda j: j)],
                          [(conv_w[name], cw, lambda j: j), (conv_b[name], cw, lambda j: j)],
                          [(wd, cw, lambda j: j, BF16)], halo=True)[0]

    raw = p_dt[:, :2 * SSD_HEADS].reshape(T, 2, SSD_GROUPS, HEADS_PER_GROUP)
    rawc = raw.transpose(1, 2, 0, 3)
    rawr = raw.transpose(1, 2, 3, 0)
    bias = [_per_group(w["dt_bias_f"]), _per_group(w["dt_bias_b"])]
    alog = [_per_group(w["a_log_f"]), _per_group(w["a_log_b"])]
    small = (rawc, rawr, jnp.stack([bias[0][0], bias[1][0]]), jnp.stack([bias[0][1], bias[1][1]]),
             jnp.stack([alog[0][0], alog[1][0]]), jnp.stack([alog[0][1], alog[1][1]]))
    y_dir, states = _ssd_fwd(act["xs"], act["B"], act["C"], small, T)

    dskip_e = jnp.repeat(w["d_skip"], SSD_HEAD_DIM, axis=1)
    gcol = lambda j: j
    gw_ = 512
    a_ssd = _rows("ssd_post", lambda i, yf, yb, xv, zv, dk, nw: (_ssd_post(yf, yb, xv, zv, dk, nw),), T, SSD_GROUPS,
                  [(y_dir[0], gw_, gcol), (y_dir[1], gw_, gcol), (act["xs"], gw_, gcol), seg("z", gw_, gcol)],
                  [(dskip_e, gw_, gcol), (w["ssd_norm_w"], gw_, gcol)], [(2048, gw_, gcol, BF16)])[0]

    w = dict(w, **late_weights(a_ssd))
    w_up_g, w_up_u = w["w_ffn_up_t"][:D_FF], w["w_ffn_up_t"][D_FF:]
    y_ret_o = _mm("ret_out", a_ret, w["w_ret_out"], "nn", out_dtype=BF16)
    y_ssd_o = _mm("ssd_out", a_ssd, w["w_ssd_out"], "nn", out_dtype=BF16)

    def merge_fn(i, gates, yr, ys):
        return (_merge(gates, yr, ys, _row_ids(i, T) >= PAD_ROWS),)

    merged = _rows("merge", merge_fn, T, 1, [(p_gates, 2048, c0), (y_ret_o, 1024, c0), (y_ssd_o, 1024, c0)], [],
                   [(1024, 1024, c0, BF16)])[0]
    h1 = _mm("mix_out", merged, w["w_out"], "nn", add=h0)

    n2 = norm_cast("norm_ffn", h1, w["norm_ffn_w"])
    fg_pre = _mm("ffn_up_g", n2, w_up_g, "nt", out_dtype=BF16)
    fu_pre = _mm("ffn_up_u", n2, w_up_u, "nt", out_dtype=BF16)
    cwg, cwu = w["w_ffn_conv"][:, :D_FF], w["w_ffn_conv"][:, D_FF:]
    cbg, cbu = w["b_ffn_conv"][:, :D_FF], w["b_ffn_conv"][:, D_FF:]
    fcol = lambda j: j
    fw = 1408

    def ffn_act_fn(i, ge, ue, wg, wu, bg, bu):
        return (_center(_silu(_conv3(ge, wg) + bg) * (_conv3(ue, wu) + bu)),)

    a2 = _rows("ffn_act", ffn_act_fn, T, D_FF // fw, [(fg_pre, fw, fcol), (fu_pre, fw, fcol)],
               [(cwg, fw, fcol), (cwu, fw, fcol), (cbg, fw, fcol), (cbu, fw, fcol)], [(D_FF, fw, fcol, BF16)],
               halo=True)[0]
    h2 = _mm("ffn_down", a2, w["w_ffn_down"], "nn", add=h1)

    fnw = w["final_norm_w"].reshape(1, D_MODEL)

    def loss_fn(i, hv, tv, nw):
        valid = _row_ids(i, T) >= CHUNK
        y, vjp = jax.vjp(_rms, hv, nw)
        diff = jnp.where(valid, y - tv, 0.0)
        dh, dw = vjp(diff * (1.0 / D_MODEL))
        part = 0.5 / D_MODEL * jnp.sum(jnp.sum(diff * diff, axis=1, keepdims=True), axis=0, keepdims=True)
        return dh, jnp.broadcast_to(part, (1, 128)), dw

    dh2, loss_acc, d_fnw = _rows("loss", loss_fn, T, 1, [(h2, D_MODEL, c0), (tgt, D_MODEL, c0)], [(fnw, D_MODEL, c0)],
                                 [(D_MODEL, D_MODEL, c0, F32)], [(1, 128, 128, c0), (1, D_MODEL, D_MODEL, c0)])
    loss = loss_acc[0, 0]
    grads = {"final_norm_w": d_fnw.reshape(D_MODEL)}

    da2 = _mm("d_ffn_act", dh2, w["w_ffn_down"], "nt", out_dtype=BF16)
    grads["w_ffn_down"] = _mm("g_ffn_down", a2, dh2, "tn", out_dtype=BF16)

    def ffn_bwd_fn(i, ge, ue, de, wg, wu, bg, bu):
        fg = _conv3(ge, wg) + bg
        fu = _conv3(ue, wu) + bu
        sg = jax.nn.sigmoid(fg)
        dfg = de * fu * (sg * (1.0 + fg * (1.0 - sg)))
        dfu = de * (fg * sg)
        n = ge.shape[0]

        def wgrad(df, xe):
            df_c = _center(df)
            return jnp.concatenate([jnp.sum(df_c * _center(pltpu.roll(xe, 1, 0)), axis=0, keepdims=True),
                                    jnp.sum(df_c * _center(xe), axis=0, keepdims=True),
                                    jnp.sum(df_c * _center(pltpu.roll(xe, n - 1, 0)), axis=0, keepdims=True)], axis=0)

        return (_center(_conv3_t(dfg, wg)), _center(_conv3_t(dfu, wu)), wgrad(dfg, ge), wgrad(dfu, ue),
                jnp.sum(_center(dfg), axis=0, keepdims=True), jnp.sum(_center(dfu), axis=0, keepdims=True))

    dfg_pre, dfu_pre, g_cwg, g_cwu, g_cbg, g_cbu = _rows(
        "ffn_act_bwd", ffn_bwd_fn, T, D_FF // fw, [(fg_pre, fw, fcol), (fu_pre, fw, fcol), (da2, fw, fcol)],
        [(cwg, fw, fcol), (cwu, fw, fcol), (cbg, fw, fcol), (cbu, fw, fcol)],
        [(D_FF, fw, fcol, BF16), (D_FF, fw, fcol, BF16)],
        [(3, D_FF, fw, fcol), (3, D_FF, fw, fcol), (1, D_FF, fw, fcol), (1, D_FF, fw, fcol)], halo=True)
    grads["w_ffn_conv"] = jnp.concatenate([g_cwg, g_cwu], axis=1)
    grads["b_ffn_conv"] = jnp.concatenate([g_cbg, g_cbu], axis=1)
    dn2 = _mm("d_norm_ffn_g", dfg_pre, w_up_g, "nn")
    dn2 = _mm("d_norm_ffn_u", dfu_pre, w_up_u, "nn", add=dn2)
    grads["w_ffn_up_t"] = jnp.concatenate([_mm("g_ffn_up_g", dfg_pre, n2, "tn", out_dtype=BF16), _mm("g_ffn_up_u", dfu_pre, n2, "tn", out_dtype=BF16)],
                                          axis=0)

    def norm_bwd(name, h, nw, dn, dres):
        def fn(i, hv, dnv, drv, wv):
            _, vjp = jax.vjp(_rms, hv, wv)
            dh, dw = vjp(dnv)
            return dh + drv, dw
        return _rows(name, fn, T, 1, [(h, D_MODEL, c0), (dn, D_MODEL, c0), (dres, D_MODEL, c0)], [(nw, D_MODEL, c0)],
                     [(D_MODEL, D_MODEL, c0, F32)], [(1, D_MODEL, D_MODEL, c0)])

    dh1, grads["norm_ffn_w"] = norm_bwd("norm_ffn_bwd", h1, w["norm_ffn_w"], dn2, dh2)

    dmerged = _mm("d_merged", dh1, w["w_out"], "nt", out_dtype=BF16)
    grads["w_out"] = _mm("g_out", merged, dh1, "tn", out_dtype=BF16)

    def merge_bwd_fn(i, gates, yr, ys, dm):
        valid = _row_ids(i, T) >= PAD_ROWS
        _, vjp = jax.vjp(lambda a, b, c: _merge(a, b, c, valid), gates, yr, ys)
        return vjp(dm)

    dgates, dyr, dys = _rows("merge_bwd", merge_bwd_fn, T, 1,
                             [(p_gates, 2048, c0), (y_ret_o, 1024, c0), (y_ssd_o, 1024, c0), (dmerged, 1024, c0)],
                             [], [(2048, 2048, c0, BF16), (1024, 1024, c0, BF16), (1024, 1024, c0, BF16)])
    dproj = {"gates": dgates}

    da_ssd = _mm("d_ssd_act", dys, w["w_ssd_out"], "nt", out_dtype=BF16)
    grads["w_ssd_out"] = _mm("g_ssd_out", a_ssd, dys, "tn", out_dtype=BF16)

    def ssd_post_bwd_fn(i, yf, yb, xv, zv, da, dk, nw):
        _, vjp = jax.vjp(_ssd_post, yf, yb, xv, zv, dk, nw)
        dyf, _, dxv, dzv, ddk, dnw = vjp(da)
        return dyf, dxv, dzv, ddk, dnw

    d_main = lax.empty(p_main.shape, BF16)

    def into_main(name, width, cf=c0):
        base = seg_at[name] // width
        return (d_main, width, lambda j: base + cf(j), BF16)

    dy_ssd, dxs_skip, d_main, g_dskip_e, grads["ssd_norm_w"] = _rows(
        "ssd_post_bwd", ssd_post_bwd_fn, T, SSD_GROUPS,
        [(y_dir[0], gw_, gcol), (y_dir[1], gw_, gcol), (act["xs"], gw_, gcol), seg("z", gw_, gcol),
         (da_ssd, gw_, gcol)],
        [(dskip_e, gw_, gcol), (w["ssd_norm_w"], gw_, gcol)],
        [(2048, gw_, gcol, BF16), (2048, gw_, gcol, BF16), into_main("z", gw_, gcol)],
        [(1, 2048, gw_, gcol), (1, 2048, gw_, gcol)])
    grads["d_skip"] = g_dskip_e.reshape(SSD_HEADS, SSD_HEAD_DIM).sum(axis=1).reshape(1, SSD_HEADS)

    dxs_dir, db_dir, dc_dir, draw, g_bias, g_alog = _ssd_bwd(act["xs"], act["B"], act["C"], small, states, dy_ssd, T)
    grads["dt_bias_f"], grads["dt_bias_b"] = g_bias[0].reshape(1, SSD_HEADS), g_bias[1].reshape(1, SSD_HEADS)
    grads["a_log_f"], grads["a_log_b"] = g_alog[0].reshape(1, SSD_HEADS), g_alog[1].reshape(1, SSD_HEADS)
    d_dt = jnp.stack(draw).transpose(2, 0, 1, 3).reshape(T, 2 * SSD_HEADS)
    dproj["dt"] = jnp.pad(d_dt, ((0, 0), (0, CHUNK - 2 * SSD_HEADS))).astype(BF16)

    def make_conv_bwd(nsum):
        def fn(i, xe, *rest):
            ds, (cw, cb) = rest[:nsum], rest[nsum:]
            r = _row_ids(i, T, True)
            dact = ds[0]
            for t in ds[1:]:
                dact = dact + t
            dact = jnp.where(r >= PAD_ROWS, dact, 0.0)
            pre = _conv3(xe, cw) + cb
            sg = jax.nn.sigmoid(pre)
            dpre = dact * (sg * (1.0 + pre * (1.0 - sg)))
            n = xe.shape[0]
            dpc = _center(dpre)
            dw = jnp.concatenate([jnp.sum(dpc * _center(pltpu.roll(xe, 1, 0)), axis=0, keepdims=True),
                                  jnp.sum(dpc * _center(xe), axis=0, keepdims=True),
                                  jnp.sum(dpc * _center(pltpu.roll(xe, n - 1, 0)), axis=0, keepdims=True)], axis=0)
            return _center(_conv3_t(dpre, cw)), dw, jnp.sum(dpc, axis=0, keepdims=True)
        return fn

    g_cw, g_cb = {}, {}
    cots = {"xs": [(dxs_dir[0], 512, gcol), (dxs_dir[1], 512, gcol), (dxs_skip, 512, gcol)],
            "B": [(db_dir[0], 512, gcol), (db_dir[1], 512, gcol)],
            "C": [(dc_dir[0], 512, gcol), (dc_dir[1], 512, gcol)]}
    for name in ("xs", "B", "C"):
        wd = conv_w[name].shape[1]
        d_main, g_cw[name], g_cb[name] = _rows(
            "ssd_conv_bwd_" + name, make_conv_bwd(len(cots[name])), T, wd // 512,
            [seg(name, 512, gcol)] + cots[name], [(conv_w[name], 512, gcol), (conv_b[name], 512, gcol)],
            [into_main(name, 512, gcol)], [(3, wd, 512, gcol), (1, wd, 512, gcol)], halo=True)
    grads["w_ssd_conv"] = jnp.concatenate([g_cw["xs"], g_cw["B"], g_cw["C"]], axis=1)
    grads["b_ssd_conv"] = jnp.concatenate([g_cb["xs"], g_cb["B"], g_cb["C"]], axis=1)

    da_ret = _mm("d_ret_act", dyr, w["w_ret_out"], "nt", out_dtype=BF16)
    grads["w_ret_out"] = _mm("g_ret_out", a_ret, dyr, "tn", out_dtype=BF16)
    tick = early_grads({n: grads.pop(n) for n in ("w_ffn_up_t", "w_ret_out", "w_ssd_out", "w_out", "w_ffn_down")})

    def ret_post_bwd_fn(i, y, g, da, gw):
        _, vjp = jax.vjp(_ret_post, y, g, gw)
        return vjp(da)

    dy_ret, d_main, grads["ret_gn_w"] = _rows(
        "ret_post_bwd", ret_post_bwd_fn, T, 1, [(y_ret, 1024, c0), seg("g", 1024), (da_ret, 1024, c0)],
        [(w["ret_gn_w"] + tick, 1024, c0)], [(1024, 1024, c0, BF16), into_main("g", 1024)], [(1, 1024, 1024, c0)])
    d_main = _retention("retention_dv", kr, qr, dy_ret, T, RET_QK_DIM, RET_V_DIM, into=(d_main, seg_at["v"]))
    dqr = _retention("retention_dq", dy_ret, v_at, kr, T, RET_V_DIM, RET_QK_DIM)
    dkr = _retention("retention_dk", v_at, dy_ret, qr, T, RET_V_DIM, RET_QK_DIM)

    def rot_bwd_fn(i, dq, dk, csv, snv):
        parts = [_rot_t(dq[:, h * 128:(h + 1) * 128], csv, snv) for h in range(RET_HEADS)]
        parts += [_rot_t(dk[:, h * 128:(h + 1) * 128] * scale, csv, snv) for h in range(RET_HEADS)]
        return (jnp.concatenate(parts, axis=1),)

    d_main = _rows("rotary_bwd", rot_bwd_fn, T, 1, [(dqr, 512, c0), (dkr, 512, c0), (cs, 128, c0), (sn, 128, c0)],
                   [], [into_main("qk", 1024)])[0]

    g_in = [_mm("g_in_main", d_main, u, "tn", out_dtype=BF16),
            _mm("g_in_dt", dproj["dt"], u, "tn", out_dtype=BF16)[:2 * SSD_HEADS],
            _mm("g_in_gates", dproj["gates"], u, "tn", out_dtype=BF16)]
    tick = in_grads(jnp.concatenate(g_in, axis=0))
    du = _mm("d_u_dt", dproj["dt"] + tick.astype(BF16), w_dt, "nn")
    du = _mm("d_u_main", d_main, w_main, "nn", add=du)
    du = _mm("d_u_gates", dproj["gates"], w_gates, "nn", add=du)
    dh0, grads["norm_mix_w"] = norm_bwd("norm_mix_bwd", h0, w["norm_mix_w"], du, dh1)
    grads["meta_tokens"] = dh0[PAD_ROWS:CHUNK]
    return loss, dh0[CHUNK:], grads


MESH_ID = pl.DeviceIdType.MESH
ANY = pl.BlockSpec(memory_space=pl.ANY)


def _me_and_peers():
    x, y, c = lax.axis_index("x"), lax.axis_index("y"), lax.axis_index("c")
    peers = []
    for k in range(1, N_DEV):
        px = 1 - x if k & 4 else x
        py = 1 - y if k & 2 else y
        pc = 1 - c if k & 1 else c
        peers.append(((px, py, pc), 4 * px + 2 * py + pc))
    return 4 * x + 2 * y + c, peers


def _push_blocks(name, src, per_peer):
    blk = src.shape[1:] if per_peer else src.shape

    def body(src_ref, out_ref, send_sems, recv_sems, local_sem):
        me, peers = _me_and_peers()
        mine = src_ref.at[me] if per_peer else src_ref
        local = pltpu.make_async_copy(mine, out_ref.at[me], local_sem)
        local.start()
        sends = []
        for k, (dev, idx) in enumerate(peers):
            cp = pltpu.make_async_remote_copy(
                src_ref=src_ref.at[idx] if per_peer else src_ref, dst_ref=out_ref.at[me],
                send_sem=send_sems.at[k], recv_sem=recv_sems.at[k], device_id=dev, device_id_type=MESH_ID)
            cp.start()
            sends.append(cp)
        for k, (dev, idx) in enumerate(peers):
            pltpu.make_async_remote_copy(
                src_ref=mine, dst_ref=out_ref.at[idx], send_sem=send_sems.at[k], recv_sem=recv_sems.at[k],
                device_id=dev, device_id_type=MESH_ID).wait_recv()
        for cp in sends:
            cp.wait_send()
        local.wait()

    return pl.pallas_call(
        body, name=name, in_specs=[ANY], out_specs=ANY,
        out_shape=jax.ShapeDtypeStruct((N_DEV,) + tuple(blk), src.dtype),
        scratch_shapes=[pltpu.SemaphoreType.DMA((N_DEV - 1,)), pltpu.SemaphoreType.DMA((N_DEV - 1,)),
                        pltpu.SemaphoreType.DMA],
    )(src)


def _gather_two_level(name, src):
    def body(x_ref, out_ref, send_sems, recv_sems, local_sem):
        x, y, c = lax.axis_index("x"), lax.axis_index("y"), lax.axis_index("c")
        me, sibling = (x, y, c), (x, y, 1 - c)
        chips = [(1 - x, y), (x, 1 - y), (1 - x, 1 - y)]

        def rows(px, py, pc):
            return out_ref.at[4 * px + 2 * py + pc]

        def copy(k, block, to, src_ref=None):
            return pltpu.make_async_remote_copy(
                src_ref=rows(*block) if src_ref is None else src_ref, dst_ref=rows(*block),
                send_sem=send_sems.at[k], recv_sem=recv_sems.at[k], device_id=to, device_id_type=MESH_ID)

        mine = pltpu.make_async_copy(x_ref, rows(*me), local_sem)
        mine.start()
        first = [copy(0, me, sibling, x_ref)] + [copy(1 + j, me, (*chip, c), x_ref) for j, chip in enumerate(chips)]
        for cp in first:
            cp.start()
        passed = [copy(4 + j, (*chip, c), sibling) for j, chip in enumerate(chips)]
        for j, chip in enumerate(chips):
            copy(1 + j, (*chip, c), me).wait_recv()
            passed[j].start()
        copy(0, sibling, me).wait_recv()
        for j, chip in enumerate(chips):
            copy(4 + j, (*chip, 1 - c), me).wait_recv()
        for cp in first + passed:
            cp.wait_send()
        mine.wait()

    return pl.pallas_call(
        body, name=name, in_specs=[ANY], out_specs=ANY,
        out_shape=jax.ShapeDtypeStruct((N_DEV,) + tuple(src.shape), src.dtype),
        scratch_shapes=[pltpu.SemaphoreType.DMA((N_DEV - 1,)), pltpu.SemaphoreType.DMA((N_DEV - 1,)),
                        pltpu.SemaphoreType.DMA],
    )(src)


HBM = pl.BlockSpec(memory_space=pltpu.HBM)
SEM = pl.BlockSpec(memory_space=pltpu.SEMAPHORE)
EFFECT = pltpu.SideEffectType.DATAFLOW_SIDE_EFFECTING


def _peer_copy(src_ref, land_ref, send_sems, recv_sems, per_peer, me, a, k, dev, idx, receiving):
    s = a * (N_DEV - 1) + k
    return pltpu.make_async_remote_copy(
        src_ref=src_ref.at[idx] if per_peer else src_ref, dst_ref=land_ref.at[idx if receiving else me],
        send_sem=send_sems.at[s], recv_sem=recv_sems.at[s], device_id=dev, device_id_type=MESH_ID)


def _push_start(name, srcs, per_peer):
    n = len(srcs)
    land_shapes = [(N_DEV,) + tuple(s.shape[1:] if per_peer else s.shape) for s in srcs]

    def body(*refs):
        src_refs, land_refs, send_sems, recv_sems, token = refs[:n], refs[n:2 * n], refs[2 * n], refs[2 * n + 1], refs[-1]
        me, peers = _me_and_peers()
        for a in range(n):
            for k, (dev, idx) in enumerate(peers):
                _peer_copy(src_refs[a], land_refs[a], send_sems, recv_sems, per_peer, me, a, k, dev, idx, False).start()
        token[...] = jnp.zeros_like(token)

    sems = pltpu.SemaphoreType.DMA((n * (N_DEV - 1),))
    res = pl.pallas_call(
        body, name=name,
        out_shape=(sems, sems, *[pltpu.HBM(s.shape, s.dtype) for s in srcs],
                   *[pltpu.HBM(ls, s.dtype) for ls, s in zip(land_shapes, srcs)], jax.ShapeDtypeStruct((8, 128), F32)),
        in_specs=(HBM,) * (2 * n), out_specs=(SEM, SEM) + (HBM,) * (2 * n) + (pl.BlockSpec(memory_space=pltpu.VMEM),),
        input_output_aliases={i: 2 + i for i in range(2 * n)},
        compiler_params=pltpu.CompilerParams(has_side_effects=EFFECT),
    )(*[pltpu.with_memory_space_constraint(s, pltpu.HBM) for s in srcs],
      *[pltpu.with_memory_space_constraint(lax.empty(ls, s.dtype), pltpu.HBM) for ls, s in zip(land_shapes, srcs)])
    return res[0], res[1], res[2:2 + n], res[2 + n:2 + 2 * n], res[-1]


def _push_wait(name, send_sems, recv_sems, srcs_thru, lands_thru, after, per_peer):
    n = len(srcs_thru)

    def body(*refs):
        src_refs, land_refs, send_sems, recv_sems = refs[:n], refs[n:2 * n], refs[2 * n], refs[2 * n + 1]
        me, peers = _me_and_peers()
        for a in range(n):
            for k, (dev, idx) in enumerate(peers):
                cp = _peer_copy(src_refs[a], land_refs[a], send_sems, recv_sems, per_peer, me, a, k, dev, idx, True)
                cp.wait_send()
                cp.wait_recv()

    both = list(srcs_thru) + list(lands_thru)
    res = pl.pallas_call(
        body, name=name, out_shape=tuple(pltpu.HBM(t.shape, t.dtype) for t in both),
        in_specs=(HBM,) * (2 * n) + (SEM, SEM, ANY), out_specs=(HBM,) * (2 * n),
        input_output_aliases={i: i for i in range(2 * n)},
        compiler_params=pltpu.CompilerParams(has_side_effects=EFFECT),
    )(*both, send_sems, recv_sems, after)
    return res[:n], res[n:]


def _sum_blocks(name, blocks):
    _, R, C = blocks.shape
    tc = next(t for t in (1024, 512, 256, 128) if C % t == 0 and (N_DEV * R * t * 2 <= 6 * 2 ** 20 or t == 128))

    def body(b_ref, o_ref):
        acc = b_ref[0].astype(F32)
        for k in range(1, N_DEV):
            acc = acc + b_ref[k].astype(F32)
        o_ref[...] = acc

    return pl.pallas_call(
        body, name=name, grid=(C // tc,), in_specs=[pl.BlockSpec((N_DEV, R, tc), lambda j: (0, 0, j))],
        out_specs=pl.BlockSpec((R, tc), lambda j: (0, j)), out_shape=jax.ShapeDtypeStruct((R, C), F32),
        compiler_params=_params(("arbitrary",)),
    )(blocks)


def _adamw(name, w, g, m, v):
    R, C = w.shape
    tr = R if R <= 512 else _pick(R, (256, 184, 176, 128, 8))
    spec = pl.BlockSpec((tr, C), lambda i: (i, 0))

    def body(w_ref, g_ref, m_ref, v_ref, d_ref, mo_ref, vo_ref):
        gv = g_ref[...]
        mn = ADAM_B1 * m_ref[...] + (1.0 - ADAM_B1) * gv
        vn = ADAM_B2 * v_ref[...] + (1.0 - ADAM_B2) * jnp.square(gv)
        m_hat = mn / (1.0 - ADAM_B1 ** ADAM_STEP)
        v_hat = vn / (1.0 - ADAM_B2 ** ADAM_STEP)
        d_ref[...] = -ADAM_LR * (m_hat / (jnp.sqrt(v_hat) + ADAM_EPS) + ADAM_WD * w_ref[...])
        mo_ref[...] = mn
        vo_ref[...] = vn

    return pl.pallas_call(
        body, name=name, grid=(R // tr,), in_specs=[spec] * 4, out_specs=[spec] * 3,
        out_shape=[jax.ShapeDtypeStruct((R, C), F32)] * 3, compiler_params=_params(("arbitrary",)),
    )(w, g, m, v)


WEIGHTS = ("meta_tokens", "norm_mix_w", "w_in", "ret_gn_w", "w_ret_out", "w_ssd_conv", "b_ssd_conv", "dt_bias_f",
           "dt_bias_b", "a_log_f", "a_log_b", "d_skip", "ssd_norm_w", "w_ssd_out", "w_out", "norm_ffn_w", "w_ffn_up",
           "w_ffn_conv", "b_ffn_conv", "w_ffn_down", "final_norm_w")
BIG = (("w_in", 1288, True), ("w_ffn_up", 704, True), ("w_ret_out", 128, False), ("w_ssd_out", 256, False),
       ("w_out", 128, False), ("w_ffn_down", 352, False))
REPLICATED = ("norm_mix_w", "ret_gn_w", "b_ssd_conv", "dt_bias_f", "dt_bias_b", "a_log_f", "a_log_b", "d_skip",
              "ssd_norm_w", "norm_ffn_w", "b_ffn_conv", "final_norm_w")
SMALL_SHARDED = (("meta_tokens", 16, 1024), ("w_ssd_conv", 3, 3072), ("w_ffn_conv", 3, 5632))


BIG_IN, BIG_REST = BIG[:1], BIG[1:]


def _pack_big(tree, group):
    parts = []
    for name, _, transposed in group:
        a = tree[name][0]
        parts.append(a.T if transposed else a)
    return jnp.concatenate(parts, axis=0)


def _unpack_big(slab, group):
    out, r0 = {}, 0
    for name, r, transposed in group:
        a = slab[r0:r0 + r]
        out[name] = (a.T if transposed else a)[None]
        r0 += r
    return out


def _pack_flat(arrays, rows):
    flat = jnp.concatenate([a.reshape(-1) for a in arrays])
    return jnp.pad(flat, (0, rows * D_MODEL - flat.shape[0])).reshape(rows, D_MODEL)


def _unpack_flat(slab, shapes):
    flat, out, o = slab.reshape(-1), [], 0
    for s in shapes:
        n = math.prod(s)
        out.append(flat[o:o + n].reshape(s))
        o += n
    return out


def kernel(x, meta_tokens, norm_mix_w, w_in, ret_gn_w, w_ret_out, w_ssd_conv, b_ssd_conv, dt_bias_f, dt_bias_b, a_log_f, a_log_b, d_skip, ssd_norm_w, w_ssd_out, w_out, norm_ffn_w, w_ffn_up, w_ffn_conv, b_ffn_conv, w_ffn_down, final_norm_w, loss_target, m_meta_tokens, m_norm_mix_w, m_w_in, m_ret_gn_w, m_w_ret_out, m_w_ssd_conv, m_b_ssd_conv, m_dt_bias_f, m_dt_bias_b, m_a_log_f, m_a_log_b, m_d_skip, m_ssd_norm_w, m_w_ssd_out, m_w_out, m_norm_ffn_w, m_w_ffn_up, m_w_ffn_conv, m_b_ffn_conv, m_w_ffn_down, m_final_norm_w, v_meta_tokens, v_norm_mix_w, v_w_in, v_ret_gn_w, v_w_ret_out, v_w_ssd_conv, v_b_ssd_conv, v_dt_bias_f, v_dt_bias_b, v_a_log_f, v_a_log_b, v_d_skip, v_ssd_norm_w, v_w_ssd_out, v_w_out, v_norm_ffn_w, v_w_ffn_up, v_w_ffn_conv, v_b_ffn_conv, v_w_ffn_down, v_final_norm_w):
    given = dict(locals())
    wt = {n: given[n] for n in WEIGHTS}
    mt = {n: given["m_" + n] for n in WEIGHTS}
    vt = {n: given["v_" + n] for n in WEIGHTS}
    me = 4 * lax.axis_index("x") + 2 * lax.axis_index("y") + lax.axis_index("c")

    small_names = [n for n, _, _ in SMALL_SHARDED]
    small_local = lambda tree: [tree[n].reshape(r, c // N_DEV) for n, r, c in SMALL_SHARDED]
    all_in = _gather_two_level("gather_w_in", _pack_big(wt, BIG_IN).astype(BF16))
    all_s = _push_blocks("gather_small", _pack_flat(small_local(wt), 8), False)
    slab_view = lambda tree, name, transposed: tree[name][0].T if transposed else tree[name][0]
    rest_srcs = [slab_view(wt, name, t).astype(BF16) for name, _, t in BIG_REST]
    rest_srcs, all_in, all_s = lax.optimization_barrier((rest_srcs, all_in, all_s))
    rest_flight = _push_start("gather_rest_start", rest_srcs, False)
    all_s = all_s.reshape(N_DEV, -1)
    full = {"w_in_t": all_in.reshape(-1, D_MODEL)}

    def lands_with_own(flight, after, per_peer, name):
        srcs, lands = _push_wait(name, *flight[:4], after, per_peer)
        own = lambda s: lax.dynamic_slice_in_dim(s, me, 1, axis=0) if per_peer else s[None]
        return [lax.dynamic_update_slice_in_dim(land, own(s), me, axis=0) for s, land in zip(srcs, lands)]

    def late_weights(after):
        lands = lands_with_own(rest_flight, after, False, "gather_rest_wait")
        return {name + ("_t" if t else ""): land.reshape(N_DEV * r, D_MODEL) for (name, r, t), land in zip(BIG_REST, lands)}

    flights = {}

    def start_exchange(key, group, gd):
        srcs = [gd[name + ("_t" if t else "")].astype(BF16).reshape(N_DEV, r, D_MODEL) for name, r, t in group]
        flights[key] = _push_start("exchange_" + key + "_start", srcs, True)
        return flights[key][4][0, 0]

    o = 0
    for name, r, c in SMALL_SHARDED:
        n = r * c // N_DEV
        full[name] = all_s[:, o:o + n].reshape(N_DEV, r, c // N_DEV).transpose(1, 0, 2).reshape(r, c)
        o += n
    for name in REPLICATED:
        full[name] = wt[name]

    grads, delta, new_m, new_v = {}, {}, {}, {}

    def finish_exchange(key, group, after):
        lands = lands_with_own(flights[key], after, True, "exchange_" + key + "_wait")
        for (name, _, transposed), land in zip(group, lands):
            back = (lambda a: a.T[None]) if transposed else (lambda a: a[None])
            g_sum = _sum_blocks("sum_" + name, land)
            d, mn, vn = _adamw("adamw_" + name, slab_view(wt, name, transposed), g_sum,
                               slab_view(mt, name, transposed), slab_view(vt, name, transposed))
            grads[name], delta[name], new_m[name], new_v[name] = back(g_sum), back(d), back(mn), back(vn)

    def in_grads(gi):
        tick = start_exchange("in", BIG_IN, {"w_in_t": gi})
        finish_exchange("rest", BIG_REST, flights["in"][4])
        tick, _ = lax.optimization_barrier((tick, [delta[name] for name, _, _ in BIG_REST]))
        return tick

    loss, grad_x, g = _local_step(x[0], loss_target[0], full, rest_flight[4][0, 0], late_weights,
                                  lambda gd: start_exchange("rest", BIG_REST, gd), in_grads)

    finish_exchange("in", BIG_IN, g["norm_mix_w"])
    small_parts = [g[n] for n in REPLICATED] + [g[n] for n in small_names] + [loss.reshape(1)]
    g_small = _sum_blocks("sum_small", _push_blocks("gather_small_grads", _pack_flat(small_parts, 64), False))
    small_red = _unpack_flat(g_small, [wt[n].shape for n in REPLICATED] + [(r, c) for _, r, c in SMALL_SHARDED] + [(1,)])
    grads.update(zip(REPLICATED, small_red[:len(REPLICATED)]))
    for (name, r, c), red in zip(SMALL_SHARDED, small_red[len(REPLICATED):-1]):
        grads[name] = lax.dynamic_slice(red, (0, me * (c // N_DEV)), (r, c // N_DEV)).reshape(wt[name].shape)
    loss_all = small_red[-1][0]

    rest = list(REPLICATED) + small_names
    shapes = [wt[n].shape for n in rest]
    pack_rest = lambda tree: _pack_flat([tree[n] for n in rest], 24)
    d_rest, m_rest, v_rest = _adamw("adamw_small", pack_rest(wt), pack_rest(grads), pack_rest(mt), pack_rest(vt))
    delta.update(zip(rest, _unpack_flat(d_rest, shapes)))
    new_m.update(zip(rest, _unpack_flat(m_rest, shapes)))
    new_v.update(zip(rest, _unpack_flat(v_rest, shapes)))

    return (loss_all, grad_x[None], *[grads[n] for n in WEIGHTS], *[delta[n] for n in WEIGHTS],
            *[new_m[n] for n in WEIGHTS], *[new_v[n] for n in WEIGHTS])
```

```python
import functools
import math

import jax
import jax.numpy as jnp
from jax import lax
from jax.experimental import pallas as pl
from jax.experimental.pallas import tpu as pltpu

F32 = jnp.float32
BF16 = jnp.bfloat16

D_MODEL = 1024
CHUNK = 128
N_META = 16
PAD_ROWS = CHUNK - N_META
RET_HEADS = 4
RET_QK_DIM = 128
RET_V_DIM = 256
SSD_HEADS = 32
SSD_HEAD_DIM = 64
SSD_GROUPS = 4
SSD_STATE = 128
HEADS_PER_GROUP = SSD_HEADS // SSD_GROUPS
PAIRS_PER_GROUP = HEADS_PER_GROUP // 2
D_FF = 2816
EPS = 1e-6
ROPE_BASE = 10000.0
N_DEV = 8

ADAM_LR = 0.001
ADAM_B1 = 0.9
ADAM_B2 = 0.999
ADAM_EPS = 1e-08
ADAM_WD = 0.01
ADAM_STEP = 10

VMEM_LIMIT = 56 * 1024 * 1024
HALO = 16
HIGHEST = lax.Precision.HIGHEST

SEGMENTS = (("qk", 0, 1024), ("v", 1024, 2048), ("g", 2048, 3072), ("z", 3072, 5120), ("xs", 5120, 7168),
            ("B", 7168, 7680), ("C", 7680, 8192), ("dt", 8192, 8256), ("gates", 8256, 10304))


def _pick(n, cands):
    for c in cands:
        if n % c == 0:
            return c
    raise ValueError(f"no tile for {n}")


def _params(sem):
    return pltpu.CompilerParams(dimension_semantics=sem, vmem_limit_bytes=VMEM_LIMIT)


def _dot(a, b, dims=(((1,), (0,)), ((), ())), precision=None):
    return lax.dot_general(a, b, dims, preferred_element_type=F32, precision=precision)


def _dot_nt(a, b):
    return _dot(a, b, (((1,), (1,)), ((), ())))


def _dot_tn(a, b):
    return _dot(a, b, (((0,), (0,)), ((), ())))


def _mm(name, a, b, mode, add=None, out_dtype=F32):
    if mode == "nn":
        (M, K), N = a.shape, b.shape[1]
    elif mode == "nt":
        (M, K), N = a.shape, b.shape[0]
    else:
        (K, M), N = a.shape, b.shape[1]
    tn = _pick(N, (1408, 1024, 512, 128, 64))
    if mode == "tn":
        tm = M if M <= 1024 else _pick(M, (1408, 1024))
        tk = _pick(K, (2112, 512, 256, 128))
    else:
        tm = _pick(M, (1056, 512, 256, 128))
        tk = K if K <= 2816 else _pick(K, (2048, 1408, 1024))
    nk = K // tk
    if mode == "nn":
        a_spec = pl.BlockSpec((tm, tk), lambda n, m, k: (m, k))
        b_spec = pl.BlockSpec((tk, tn), lambda n, m, k: (k, n))
        dims = (((1,), (0,)), ((), ()))
    elif mode == "nt":
        a_spec = pl.BlockSpec((tm, tk), lambda n, m, k: (m, k))
        b_spec = pl.BlockSpec((tn, tk), lambda n, m, k: (n, k))
        dims = (((1,), (1,)), ((), ()))
    else:
        a_spec = pl.BlockSpec((tk, tm), lambda n, m, k: (k, m))
        b_spec = pl.BlockSpec((tk, tn), lambda n, m, k: (k, n))
        dims = (((0,), (0,)), ((), ()))
    o_spec = pl.BlockSpec((tm, tn), lambda n, m, k: (m, n))
    in_specs = [a_spec, b_spec] + ([o_spec] if add is not None else [])
    args = [a, b] + ([add] if add is not None else [])

    def body(*refs):
        if add is not None:
            a_ref, b_ref, r_ref, o_ref, acc = refs
        else:
            a_ref, b_ref, o_ref, acc = refs
        k = pl.program_id(2)
        p = _dot(a_ref[...].astype(BF16), b_ref[...].astype(BF16), dims)

        def finish(r):
            if add is not None:
                r = r + r_ref[...]
            o_ref[...] = r.astype(out_dtype)

        if nk == 1:
            finish(p)
        else:
            @pl.when(k == 0)
            def _():
                acc[...] = p

            @pl.when(k > 0)
            def _():
                acc[...] += p

            @pl.when(k == nk - 1)
            def _():
                finish(acc[...])

    return pl.pallas_call(
        body, name=name, grid=(N // tn, M // tm, nk), in_specs=in_specs, out_specs=o_spec,
        out_shape=jax.ShapeDtypeStruct((M, N), out_dtype),
        scratch_shapes=[pltpu.VMEM((tm, tn) if nk > 1 else (8, 128), F32)],
        compiler_params=_params(("arbitrary", "arbitrary", "arbitrary")),
    )(*args)


ANY_SPACE = pl.BlockSpec(memory_space=pl.ANY)


def _const(c):
    return lambda j: c


def _rows(name, fn, T, ncol, ins, params, outs, accs=(), halo=False):
    tm = _pick(T, (384, 256, 128))
    R = T // tm
    hb = tm // HALO
    in_specs, args = [], []
    for spec in ins:
        arr, w, cf = spec[:3]
        lead = spec[3] if len(spec) > 3 else None
        if len(spec) > 4:
            rows, rf = spec[4]
            in_specs.append(pl.BlockSpec((rows, w), lambda j, i, cf=cf, rf=rf: (rf(i), cf(j))))
            args.append(arr)
            continue
        if lead is None:
            mk = lambda blk, rf, cf=cf: pl.BlockSpec(blk, lambda j, i: (rf(i), cf(j)))
            shape = lambda r, w=w: (r, w)
        else:
            mk = lambda blk, rf, cf=cf, lead=lead: pl.BlockSpec(blk, lambda j, i: (lead, rf(i), cf(j)))
            shape = lambda r, w=w: (None, r, w)
        in_specs.append(mk(shape(tm), lambda i: i))
        args.append(arr)
        if halo:
            in_specs.append(mk(shape(HALO), lambda i: jnp.maximum(i * hb - 1, 0)))
            in_specs.append(mk(shape(HALO), lambda i: jnp.minimum((i + 1) * hb, T // HALO - 1)))
            args += [arr, arr]
    for arr, w, cf in params:
        in_specs.append(pl.BlockSpec((arr.shape[0], w), lambda j, i, cf=cf: (0, cf(j))))
        args.append(arr)
    out_shape, out_specs, aliases = [], [], {}
    for k, (tw, w, cf, dt) in enumerate(outs):
        if not isinstance(tw, int):
            aliases[len(args)] = k
            in_specs.append(ANY_SPACE)
            args.append(tw)
            tw = tw.shape[1]
        out_shape.append(jax.ShapeDtypeStruct((T, tw), dt))
        out_specs.append(pl.BlockSpec((tm, w), lambda j, i, cf=cf: (i, cf(j))))
    for r, tw, w, cf in accs:
        out_shape.append(jax.ShapeDtypeStruct((r, tw), F32))
        out_specs.append(pl.BlockSpec((r, w), lambda j, i, cf=cf: (0, cf(j))))
    n_in, n_par, n_out, n_acc, n_alias = len(ins), len(params), len(outs), len(accs), len(aliases)

    def body(*refs):
        i = pl.program_id(1)
        vals, p = [], 0
        for _ in range(n_in):
            if halo:
                before = jnp.where(i > 0, refs[p + 1][...], jnp.zeros_like(refs[p + 1]))
                after = jnp.where(i < R - 1, refs[p + 2][...], jnp.zeros_like(refs[p + 2]))
                vals.append(jnp.concatenate([before, refs[p][...], after], axis=0).astype(F32))
                p += 3
            else:
                vals.append(refs[p][...].astype(F32))
                p += 1
        pvals = [refs[p + k][...] for k in range(n_par)]
        p += n_par + n_alias
        res = fn(i, *vals, *pvals)
        for k in range(n_out):
            refs[p + k][...] = res[k].astype(refs[p + k].dtype)
        p += n_out
        for k in range(n_acc):
            ref, v = refs[p + k], res[n_out + k]

            @pl.when(i == 0)
            def _(ref=ref, v=v):
                ref[...] = v

            @pl.when(i > 0)
            def _(ref=ref, v=v):
                ref[...] += v

    res = pl.pallas_call(
        body, name=name, grid=(ncol, R), in_specs=in_specs, out_specs=out_specs, out_shape=out_shape,
        input_output_aliases=aliases, compiler_params=_params(("arbitrary", "arbitrary")),
    )(*args)
    return res


def _tile_rows(T):
    return _pick(T, (384, 256, 128))


def _row_ids(i, T, halo=False):
    tm = _tile_rows(T)
    if halo:
        return i * tm - HALO + lax.broadcasted_iota(jnp.int32, (tm + 2 * HALO, 1), 0)
    return i * tm + lax.broadcasted_iota(jnp.int32, (tm, 1), 0)


def _rms(x, w):
    return x * lax.rsqrt(jnp.mean(x * x, axis=-1, keepdims=True) + EPS) * w


def _silu(x):
    return x * jax.nn.sigmoid(x)


def _conv3(x, w):
    n = x.shape[0]
    return w[0:1] * pltpu.roll(x, 1, 0) + w[1:2] * x + w[2:3] * pltpu.roll(x, n - 1, 0)


def _conv3_t(d, w):
    n = d.shape[0]
    return w[0:1] * pltpu.roll(d, n - 1, 0) + w[1:2] * d + w[2:3] * pltpu.roll(d, 1, 0)


def _center(x):
    return x[HALO:x.shape[0] - HALO]


def _retention(name, a, b, v, T, da, dv, into=None):
    (a, a0), (b, b0), (v, v0) = [t if isinstance(t, tuple) else (t, 0) for t in (a, b, v)]
    nc = T // CHUNK
    log_gammas = [math.log(1.0 - 2.0 ** (-5.0 - h)) for h in range(RET_HEADS)]

    def body(*refs):
        a_ref, b_ref, v_ref = refs[:3]
        out_ref, o_ref, st, st_b = refs[-4:]
        h = pl.program_id(0)
        lg = jnp.float32(log_gammas[RET_HEADS - 1])
        for k in range(RET_HEADS - 2, -1, -1):
            lg = jnp.where(h == k, jnp.float32(log_gammas[k]), lg)
        li = lax.broadcasted_iota(jnp.int32, (CHUNK, CHUNK), 0)
        si = lax.broadcasted_iota(jnp.int32, (CHUNK, CHUNK), 1)
        dmat = jnp.exp(lg * jnp.abs(li - si).astype(F32))
        pos = lax.broadcasted_iota(jnp.int32, (CHUNK, 1), 0).astype(F32)
        kdec_f = jnp.exp((CHUNK - 1 - pos) * lg)
        qdec_f = jnp.exp((pos + 1) * lg)
        kdec_b = jnp.exp(pos * lg)
        qdec_b = jnp.exp((CHUNK - pos) * lg)
        cdec = jnp.exp(CHUNK * lg)

        def rows(n):
            return pl.ds(pl.multiple_of(n * CHUNK, CHUNK), CHUNK)

        st[...] = jnp.zeros_like(st)
        st_b[...] = jnp.zeros_like(st_b)
        o_ref[...] = jnp.zeros_like(o_ref)

        def step(m, carry):
            r = rows(m)
            av, bv, vv = a_ref[r, :], b_ref[r, :], v_ref[r, :].astype(BF16)
            s = _dot_nt(av.astype(BF16), bv.astype(BF16)) * dmat
            o_ref[r, :] += _dot(s.astype(BF16), vv) + _dot((av * qdec_f).astype(BF16), st[...].astype(BF16))
            st[...] = cdec * st[...] + _dot_tn((bv * kdec_f).astype(BF16), vv)
            r = rows(nc - 1 - m)
            av, bv, vv = a_ref[r, :], b_ref[r, :], v_ref[r, :].astype(BF16)
            o_ref[r, :] += _dot((av * qdec_b).astype(BF16), st_b[...].astype(BF16))
            st_b[...] = cdec * st_b[...] + _dot_tn((bv * kdec_b).astype(BF16), vv)
            return carry

        lax.fori_loop(0, nc, step, 0, unroll=3 if nc % 3 == 0 else 1)
        out_ref[...] = o_ref[...].astype(out_ref.dtype)

    in_specs = [pl.BlockSpec((T, da), lambda h: (0, a0 // da + h)), pl.BlockSpec((T, da), lambda h: (0, b0 // da + h)),
                pl.BlockSpec((T, dv), lambda h: (0, v0 // dv + h))]
    if into is None:
        args, o0, aliases = (a, b, v), 0, {}
        out_shape = jax.ShapeDtypeStruct((T, RET_HEADS * dv), F32)
    else:
        args, o0, aliases = (a, b, v, into[0]), into[1], {3: 0}
        in_specs.append(ANY_SPACE)
        out_shape = jax.ShapeDtypeStruct(into[0].shape, into[0].dtype)
    return pl.pallas_call(
        body, name=name, grid=(RET_HEADS,), in_specs=in_specs,
        out_specs=pl.BlockSpec((T, dv), lambda h: (0, o0 // dv + h)), out_shape=out_shape,
        input_output_aliases=aliases,
        scratch_shapes=[pltpu.VMEM((T, dv), F32), pltpu.VMEM((da, dv), F32), pltpu.VMEM((da, dv), F32)],
        compiler_params=_params(("arbitrary",)),
    )(*args)


def _softplus(x):
    return jnp.maximum(x, 0.0) + jnp.log1p(jnp.exp(-jnp.abs(x)))


def _lane_lo():
    return lax.broadcasted_iota(jnp.int32, (1, CHUNK), 1) < SSD_HEAD_DIM


def _pair_cols(col, j):
    return jnp.where(_lane_lo(), col[:, 2 * j:2 * j + 1], col[:, 2 * j + 1:2 * j + 2])


def _pair_rows(colr, j):
    lo = lax.broadcasted_iota(jnp.int32, (CHUNK, 1), 0) < SSD_HEAD_DIM
    return jnp.where(lo, colr[2 * j:2 * j + 1, :], colr[2 * j + 1:2 * j + 2, :])


def _onehot8(h):
    return (lax.broadcasted_iota(jnp.int32, (1, HEADS_PER_GROUP), 1) == h).astype(F32)


def _ssd_pre(d, c, rawc, rawr, bc, br, alc, alr):
    li = lax.broadcasted_iota(jnp.int32, (CHUNK, CHUNK), 0)
    si = lax.broadcasted_iota(jnp.int32, (CHUNK, CHUNK), 1)
    dif = li - si if d == 0 else si - li
    mask = dif >= 0
    mask_t = dif <= 0
    rowc = c * CHUNK + lax.broadcasted_iota(jnp.int32, (CHUNK, 1), 0)
    rowr = c * CHUNK + lax.broadcasted_iota(jnp.int32, (1, CHUNK), 1)
    dtc = jnp.where(rowc >= PAD_ROWS, _softplus(rawc + bc), 0.0)
    dtr = jnp.where(rowr >= PAD_ROWS, _softplus(rawr + br), 0.0)
    ac = -jnp.exp(alc)
    ar = -jnp.exp(alr)
    dlc = dtc * ac
    dlr = dtr * ar
    alpc = _dot(mask.astype(F32), dlc, precision=HIGHEST)
    alpr = _dot(dlr, mask_t.astype(F32), precision=HIGHEST)
    endc = jnp.sum(dlc, axis=0, keepdims=True)
    endr = jnp.sum(dlr, axis=1, keepdims=True)
    return dict(mask=mask, mask_t=mask_t, dtc=dtc, ac=ac, alpc=alpc, alpr=alpr, endc=endc, endr=endr,
                valid=rowc >= PAD_ROWS)


def _chunk_of(d, n, nc):
    return n + d * (nc - 1 - 2 * n)


GROUP_WIDTH = HEADS_PER_GROUP * SSD_HEAD_DIM


def _ssd_in_specs(d, cfn):
    return [
        pl.BlockSpec((CHUNK, GROUP_WIDTH), lambda g, n: (cfn(d, n), g)),
        pl.BlockSpec((CHUNK, SSD_STATE), lambda g, n: (cfn(d, n), g)),
        pl.BlockSpec((CHUNK, SSD_STATE), lambda g, n: (cfn(d, n), g)),
        pl.BlockSpec((None, None, CHUNK, HEADS_PER_GROUP), lambda g, n: (d, g, cfn(d, n), 0)),
        pl.BlockSpec((None, None, HEADS_PER_GROUP, CHUNK), lambda g, n: (d, g, 0, cfn(d, n))),
        pl.BlockSpec((None, None, 1, HEADS_PER_GROUP), lambda g, n: (d, g, 0, 0)),
        pl.BlockSpec((None, None, HEADS_PER_GROUP, 1), lambda g, n: (d, g, 0, 0)),
        pl.BlockSpec((None, None, 1, HEADS_PER_GROUP), lambda g, n: (d, g, 0, 0)),
        pl.BlockSpec((None, None, HEADS_PER_GROUP, 1), lambda g, n: (d, g, 0, 0)),
    ]


N_SSD_IN = 9


def _ssd_fwd(xs, bm, cm, small, T):
    nc = T // CHUNK
    cfn = lambda d, n: _chunk_of(d, n, nc)

    def one_direction(d, n, ins, y_ref, hs_ref, h_scr):
        x_ref, b_ref, c_ref, rawc_ref, rawr_ref, bc_ref, br_ref, alc_ref, alr_ref = ins
        c = cfn(d, n)
        q = _ssd_pre(d, c, rawc_ref[...], rawr_ref[...], bc_ref[...], br_ref[...], alc_ref[...], alr_ref[...])
        bv = b_ref[...].astype(BF16)
        cv = c_ref[...].astype(BF16)
        cb = _dot_nt(cv, bv)
        lo = _lane_lo()
        for j in range(PAIRS_PER_GROUP):
            xp = x_ref[:, j * CHUNK:(j + 1) * CHUNK]
            xd = xp * _pair_cols(q["dtc"], j)
            xdb = xd.astype(BF16)
            yi = []
            for e in range(2):
                h = 2 * j + e
                lm = jnp.exp(jnp.where(q["mask"], q["alpc"][:, h:h + 1] - q["alpr"][h:h + 1, :], -jnp.inf))
                yi.append(_dot((cb * lm).astype(BF16), xdb))
            alp = _pair_cols(q["alpc"], j)
            hp = h_scr[j]
            hs_ref[j] = hp
            yo = jnp.exp(alp) * _dot_nt(cv, hp.astype(BF16))
            y_ref[:, j * CHUNK:(j + 1) * CHUNK] = (jnp.where(lo, yi[0], yi[1]) + yo).astype(y_ref.dtype)
            de = jnp.exp(_pair_cols(q["endc"], j) - alp)
            h_scr[j] = jnp.exp(_pair_rows(q["endr"], j)) * hp + _dot_tn((xd * de).astype(BF16), bv)

    def body(*refs):
        n = pl.program_id(1)
        ins, (y_f, y_b, hs_f, hs_b, h_scr) = refs[:2 * N_SSD_IN], refs[2 * N_SSD_IN:]

        @pl.when(n == 0)
        def _():
            h_scr[...] = jnp.zeros_like(h_scr)

        one_direction(0, n, ins[:N_SSD_IN], y_f, hs_f, h_scr.at[0])
        one_direction(1, n, ins[N_SSD_IN:], y_b, hs_b, h_scr.at[1])

    y_spec = lambda d: pl.BlockSpec((CHUNK, GROUP_WIDTH), lambda g, n: (cfn(d, n), g))
    hs_spec = lambda d: pl.BlockSpec((None, None, PAIRS_PER_GROUP, CHUNK, SSD_STATE),
                                     lambda g, n: (g, cfn(d, n), 0, 0, 0))
    y_shape = jax.ShapeDtypeStruct((T, SSD_HEADS * SSD_HEAD_DIM), BF16)
    hs_shape = jax.ShapeDtypeStruct((SSD_GROUPS, nc, PAIRS_PER_GROUP, CHUNK, SSD_STATE), F32)
    y_f, y_b, hs_f, hs_b = pl.pallas_call(
        body, name="ssd_fwd", grid=(SSD_GROUPS, nc),
        in_specs=_ssd_in_specs(0, cfn) + _ssd_in_specs(1, cfn),
        out_specs=[y_spec(0), y_spec(1), hs_spec(0), hs_spec(1)],
        out_shape=[y_shape, y_shape, hs_shape, hs_shape],
        scratch_shapes=[pltpu.VMEM((2, PAIRS_PER_GROUP, CHUNK, SSD_STATE), F32)],
        compiler_params=_params(("arbitrary", "arbitrary")),
    )(xs, bm, cm, *small, xs, bm, cm, *small)
    return (y_f, y_b), (hs_f, hs_b)


def _ssd_bwd(xs, bm, cm, small, hs, dy, T):
    nc = T // CHUNK
    cfn = lambda d, n: _chunk_of(1 - d, n, nc)

    def one_direction(d, n, ins, outs, dh_scr):
        x_ref, b_ref, c_ref, rawc_ref, rawr_ref, bc_ref, br_ref, alc_ref, alr_ref, hs_ref, dy_ref = ins
        dx_ref, db_ref, dc_ref, draw_ref, dbias_ref, dalog_ref = outs
        c = cfn(d, n)
        rawc, bc = rawc_ref[...], bc_ref[...]
        q = _ssd_pre(d, c, rawc, rawr_ref[...], bc, br_ref[...], alc_ref[...], alr_ref[...])
        b32, c32 = b_ref[...], c_ref[...]
        bv, cv = b32.astype(BF16), c32.astype(BF16)
        cb = _dot_nt(cv, bv)
        cbt = _dot_nt(bv, cv)
        lo = _lane_lo()
        row_lo = lax.broadcasted_iota(jnp.int32, (CHUNK, 1), 0) < SSD_HEAD_DIM
        dcb = jnp.zeros((CHUNK, CHUNK), F32)
        dcp = jnp.zeros((CHUNK, SSD_STATE), F32)
        dbp = jnp.zeros((CHUNK, SSD_STATE), F32)
        dalp = jnp.zeros((CHUNK, HEADS_PER_GROUP), F32)
        dend = jnp.zeros((1, HEADS_PER_GROUP), F32)
        ddtx = jnp.zeros((CHUNK, HEADS_PER_GROUP), F32)

        def half_sums(t):
            return (jnp.sum(jnp.where(lo, t, 0.0), axis=1, keepdims=True),
                    jnp.sum(jnp.where(lo, 0.0, t), axis=1, keepdims=True))

        for j in range(PAIRS_PER_GROUP):
            xp = x_ref[:, j * CHUNK:(j + 1) * CHUNK]
            dtp = _pair_cols(q["dtc"], j)
            xd = xp * dtp
            xdb = xd.astype(BF16)
            dyp = dy_ref[:, j * CHUNK:(j + 1) * CHUNK]
            dyb = dyp.astype(BF16)
            hn = hs_ref[j]
            hnb = hn.astype(BF16)
            dh1 = dh_scr[j]
            dh1b = dh1.astype(BF16)
            alp = _pair_cols(q["alpc"], j)
            ea = jnp.exp(alp)
            de = jnp.exp(_pair_cols(q["endc"], j) - alp)
            dxi = []
            for e in range(2):
                h = 2 * j + e
                diff = q["alpc"][:, h:h + 1] - q["alpr"][h:h + 1, :]
                lm = jnp.exp(jnp.where(q["mask"], diff, -jnp.inf))
                mt = cbt * jnp.exp(jnp.where(q["mask_t"], -diff, -jnp.inf))
                dxi.append(_dot(mt.astype(BF16), dyb))
                dyeb_h = (jnp.where(lo, dyp, 0.0) if e == 0 else jnp.where(lo, 0.0, dyp)).astype(BF16)
                gl = _dot_nt(dyeb_h, xdb) * lm
                dcb = dcb + gl
                ra = jnp.sum(gl * cb - _dot_nt(xdb, dyeb_h) * mt, axis=1, keepdims=True)
                dalp = dalp + ra * _onehot8(h)
            y_off = ea * _dot_nt(cv, hnb)
            dxs_state = de * _dot_nt(bv, dh1b)
            dxd = jnp.where(lo, dxi[0], dxi[1]) + dxs_state
            dyeb = (dyp * ea).astype(BF16)
            dcp = dcp + _dot(dyeb, hnb)
            dbp = dbp + _dot((xd * de).astype(BF16), dh1b)
            dh_scr[j] = jnp.exp(_pair_rows(q["endr"], j)) * dh1 + _dot_tn(dyeb, cv)
            r0, r1 = half_sums(dyp * y_off - xd * dxs_state)
            dalp = dalp + r0 * _onehot8(2 * j) + r1 * _onehot8(2 * j + 1)
            t0, t1 = half_sums(jnp.sum(xd * dxs_state, axis=0, keepdims=True))
            u = dh1 * hn
            u0 = jnp.sum(jnp.sum(jnp.where(row_lo, u, 0.0), axis=0, keepdims=True), axis=1, keepdims=True)
            u1 = jnp.sum(jnp.sum(jnp.where(row_lo, 0.0, u), axis=0, keepdims=True), axis=1, keepdims=True)
            eend = jnp.exp(q["endc"])
            dend = dend + (t0 + eend * u0) * _onehot8(2 * j) + (t1 + eend * u1) * _onehot8(2 * j + 1)
            dx_ref[:, j * CHUNK:(j + 1) * CHUNK] = (dxd * dtp).astype(dx_ref.dtype)
            w0, w1 = half_sums(dxd * xp)
            ddtx = ddtx + w0 * _onehot8(2 * j) + w1 * _onehot8(2 * j + 1)

        dcbb = dcb.astype(BF16)
        dc_ref[...] = (dcp + _dot(dcbb, bv)).astype(dc_ref.dtype)
        db_ref[...] = (dbp + _dot_tn(dcbb, cv)).astype(db_ref.dtype)
        ddl = _dot(q["mask_t"].astype(F32), dalp, precision=HIGHEST) + dend
        ddt = ddl * q["ac"] + ddtx
        draw = jnp.where(q["valid"], ddt * jax.nn.sigmoid(rawc + bc), 0.0)
        draw_ref[...] = draw
        dbias = jnp.sum(draw, axis=0, keepdims=True)
        dalog = jnp.sum(ddl * q["dtc"], axis=0, keepdims=True) * q["ac"]

        @pl.when(n == 0)
        def _():
            dbias_ref[...] = dbias
            dalog_ref[...] = dalog

        @pl.when(n > 0)
        def _():
            dbias_ref[...] += dbias
            dalog_ref[...] += dalog

    n_in, n_out = N_SSD_IN + 2, 6

    def body(*refs):
        n = pl.program_id(1)
        ins, outs, dh_scr = refs[:2 * n_in], refs[2 * n_in:2 * (n_in + n_out)], refs[-1]

        @pl.when(n == 0)
        def _():
            dh_scr[...] = jnp.zeros_like(dh_scr)

        one_direction(0, n, ins[:n_in], outs[:n_out], dh_scr.at[0])
        one_direction(1, n, ins[n_in:], outs[n_out:], dh_scr.at[1])

    def in_specs(d):
        return _ssd_in_specs(d, cfn) + [
            pl.BlockSpec((None, None, PAIRS_PER_GROUP, CHUNK, SSD_STATE), lambda g, n: (g, cfn(d, n), 0, 0, 0)),
            pl.BlockSpec((CHUNK, GROUP_WIDTH), lambda g, n: (cfn(d, n), g))]

    def out_specs(d):
        acc = pl.BlockSpec((None, 1, HEADS_PER_GROUP), lambda g, n: (g, 0, 0))
        return [pl.BlockSpec((CHUNK, GROUP_WIDTH), lambda g, n: (cfn(d, n), g)),
                pl.BlockSpec((CHUNK, SSD_STATE), lambda g, n: (cfn(d, n), g)),
                pl.BlockSpec((CHUNK, SSD_STATE), lambda g, n: (cfn(d, n), g)),
                pl.BlockSpec((None, CHUNK, HEADS_PER_GROUP), lambda g, n: (g, cfn(d, n), 0)), acc, acc]

    out_shape = [jax.ShapeDtypeStruct((T, SSD_HEADS * SSD_HEAD_DIM), BF16),
                 jax.ShapeDtypeStruct((T, SSD_GROUPS * SSD_STATE), BF16),
                 jax.ShapeDtypeStruct((T, SSD_GROUPS * SSD_STATE), BF16),
                 jax.ShapeDtypeStruct((SSD_GROUPS, T, HEADS_PER_GROUP), F32),
                 jax.ShapeDtypeStruct((SSD_GROUPS, 1, HEADS_PER_GROUP), F32),
                 jax.ShapeDtypeStruct((SSD_GROUPS, 1, HEADS_PER_GROUP), F32)]
    res = pl.pallas_call(
        body, name="ssd_bwd", grid=(SSD_GROUPS, nc),
        in_specs=in_specs(0) + in_specs(1), out_specs=out_specs(0) + out_specs(1), out_shape=out_shape * 2,
        scratch_shapes=[pltpu.VMEM((2, PAIRS_PER_GROUP, CHUNK, SSD_STATE), F32)],
        compiler_params=_params(("arbitrary", "arbitrary")),
    )(xs, bm, cm, *small, hs[0], dy, xs, bm, cm, *small, hs[1], dy)
    return [(res[k], res[n_out + k]) for k in range(n_out)]


def _rot(x, cs, sn):
    return x * cs + pltpu.roll(x, RET_QK_DIM // 2, 1) * sn


def _rot_t(d, cs, sn):
    return d * cs + pltpu.roll(d * sn, RET_QK_DIM // 2, 1)


def _ret_post(y, g, w):
    parts = []
    for h in range(RET_HEADS):
        yh = y[:, h * RET_V_DIM:(h + 1) * RET_V_DIM]
        mu = jnp.mean(yh, axis=-1, keepdims=True)
        var = jnp.mean(jnp.square(yh - mu), axis=-1, keepdims=True)
        parts.append((yh - mu) * lax.rsqrt(var + EPS))
    return _silu(g) * (jnp.concatenate(parts, axis=1) * w)


def _ssd_post(yf, yb, xs, z, dskip, w):
    y = (yf + yb + xs * dskip) * _silu(z)
    return y * lax.rsqrt(jnp.mean(y * y, axis=-1, keepdims=True) + EPS) * w


def _merge(gates, yr, ys, valid):
    m = jax.nn.sigmoid(gates[:, :D_MODEL]) * yr + jax.nn.sigmoid(gates[:, D_MODEL:]) * ys
    return jnp.where(valid, m, 0.0)


def _rope_tables(T):
    half = RET_QK_DIM // 2
    inv = ROPE_BASE ** (-jnp.arange(half, dtype=F32) / half)
    pos = (jnp.arange(T) - PAD_ROWS).astype(F32)
    ang = pos[:, None] * inv[None, :]
    cos, sin = jnp.cos(ang), jnp.sin(ang)
    return jnp.concatenate([cos, cos], axis=1), jnp.concatenate([-sin, sin], axis=1)


def _per_group(v):
    c = v.reshape(SSD_GROUPS, 1, HEADS_PER_GROUP)
    return c, c.reshape(SSD_GROUPS, HEADS_PER_GROUP, 1)


def _local_step(x, target, w, tick, late_weights, early_grads, in_grads):
    S = x.shape[0]
    T = S + CHUNK
    tm = _tile_rows(T)
    c0 = _const(0)

    h0 = jnp.concatenate([jnp.zeros((PAD_ROWS, D_MODEL), F32), w["meta_tokens"], x], axis=0)
    seg_at = {name: a for name, a, _ in SEGMENTS}
    w_main = w["w_in_t"][:seg_at["dt"]]
    w_dt = jnp.pad(w["w_in_t"][seg_at["dt"]:seg_at["gates"]], ((0, CHUNK - 2 * SSD_HEADS), (0, 0)))
    w_gates = w["w_in_t"][seg_at["gates"]:]

    def norm_cast(name, h, nw):
        return _rows(name, lambda i, hv, wv: (_rms(hv, wv),), T, 1, [(h, D_MODEL, c0)], [(nw, D_MODEL, c0)],
                     [(D_MODEL, D_MODEL, c0, BF16)])[0]

    u = norm_cast("norm_mix", h0, w["norm_mix_w"] + tick)
    p_main = _mm("proj_main", u, w_main, "nt", out_dtype=BF16)
    p_dt = _mm("proj_dt", u, w_dt, "nt")
    p_gates = _mm("proj_gates", u, w_gates, "nt", out_dtype=BF16)

    def seg(name, width, cf=c0):
        base = seg_at[name] // width
        return (p_main, width, lambda j: base + cf(j))

    cs, sn = _rope_tables(T)
    scale = RET_QK_DIM ** -0.5

    def rot_fn(i, qk, csv, snv):
        q = [_rot(qk[:, h * 128:(h + 1) * 128], csv, snv) for h in range(RET_HEADS)]
        k = [_rot(qk[:, (RET_HEADS + h) * 128:(RET_HEADS + h + 1) * 128], csv, snv) * scale for h in range(RET_HEADS)]
        return jnp.concatenate(q, axis=1), jnp.concatenate(k, axis=1)

    qr, kr = _rows("rotary", rot_fn, T, 1, [seg("qk", 1024), (cs, 128, c0), (sn, 128, c0)], [],
                   [(512, 512, c0, F32), (512, 512, c0, F32)])
    v_at = (p_main, seg_at["v"])
    y_ret = _retention("retention", qr, kr, v_at, T, RET_QK_DIM, RET_V_DIM)
    a_ret = _rows("ret_post", lambda i, y, g, gw: (_ret_post(y, g, gw),), T, 1,
                  [(y_ret, 1024, c0), seg("g", 1024)], [(w["ret_gn_w"], 1024, c0)],
                  [(1024, 1024, c0, BF16)])[0]

    conv_w = {"xs": w["w_ssd_conv"][:, :2048], "B": w["w_ssd_conv"][:, 2048:2560], "C": w["w_ssd_conv"][:, 2560:]}
    conv_b = {"xs": w["b_ssd_conv"][:, :2048], "B": w["b_ssd_conv"][:, 2048:2560], "C": w["b_ssd_conv"][:, 2560:]}

    def ssd_conv_fn(i, xe, cw, cb):
        r = _row_ids(i, T, True)
        return (_center(jnp.where(r >= PAD_ROWS, _silu(_conv3(xe, cw) + cb), 0.0)),)

    act = {}
    for name in ("xs", "B", "C"):
        wd = conv_w[name].shape[1]
        cw = 512
        act[name] = _rows("ssd_conv_" + name, ssd_conv_fn, T, wd // cw, [seg(name, cw, lambda j: j)],
                          [(conv_w[name], cw, lambda j: j), (conv_b[name], cw, lambda j: j)],
                          [(wd, cw, lambda j: j, BF16)], halo=True)[0]

    raw = p_dt[:, :2 * SSD_HEADS].reshape(T, 2, SSD_GROUPS, HEADS_PER_GROUP)
    rawc = raw.transpose(1, 2, 0, 3)
    rawr = raw.transpose(1, 2, 3, 0)
    bias = [_per_group(w["dt_bias_f"]), _per_group(w["dt_bias_b"])]
    alog = [_per_group(w["a_log_f"]), _per_group(w["a_log_b"])]
    small = (rawc, rawr, jnp.stack([bias[0][0], bias[1][0]]), jnp.stack([bias[0][1], bias[1][1]]),
             jnp.stack([alog[0][0], alog[1][0]]), jnp.stack([alog[0][1], alog[1][1]]))
    y_dir, states = _ssd_fwd(act["xs"], act["B"], act["C"], small, T)

    dskip_e = jnp.repeat(w["d_skip"], SSD_HEAD_DIM, axis=1)
    gcol = lambda j: j
    gw_ = 512
    a_ssd = _rows("ssd_post", lambda i, yf, yb, xv, zv, dk, nw: (_ssd_post(yf, yb, xv, zv, dk, nw),), T, SSD_GROUPS,
                  [(y_dir[0], gw_, gcol), (y_dir[1], gw_, gcol), (act["xs"], gw_, gcol), seg("z", gw_, gcol)],
                  [(dskip_e, gw_, gcol), (w["ssd_norm_w"], gw_, gcol)], [(2048, gw_, gcol, BF16)])[0]

    w = dict(w, **late_weights(a_ssd))
    w_up_g, w_up_u = w["w_ffn_up_t"][:D_FF], w["w_ffn_up_t"][D_FF:]
    y_ret_o = _mm("ret_out", a_ret, w["w_ret_out"], "nn", out_dtype=BF16)
    y_ssd_o = _mm("ssd_out", a_ssd, w["w_ssd_out"], "nn", out_dtype=BF16)

    def merge_fn(i, gates, yr, ys):
        return (_merge(gates, yr, ys, _row_ids(i, T) >= PAD_ROWS),)

    merged = _rows("merge", merge_fn, T, 1, [(p_gates, 2048, c0), (y_ret_o, 1024, c0), (y_ssd_o, 1024, c0)], [],
                   [(1024, 1024, c0, BF16)])[0]
    h1 = _mm("mix_out", merged, w["w_out"], "nn", add=h0)

    n2 = norm_cast("norm_ffn", h1, w["norm_ffn_w"])
    f_pre = _mm("ffn_up", n2, w["w_ffn_up_t"], "nt", out_dtype=BF16)
    cwg, cwu = w["w_ffn_conv"][:, :D_FF], w["w_ffn_conv"][:, D_FF:]
    cbg, cbu = w["b_ffn_conv"][:, :D_FF], w["b_ffn_conv"][:, D_FF:]
    fcol = lambda j: j
    fw = 1408

    def ffn_act_fn(i, ge, ue, wg, wu, bg, bu):
        return (_center(_silu(_conv3(ge, wg) + bg) * (_conv3(ue, wu) + bu)),)

    ucol = lambda j: D_FF // fw + j
    a2 = _rows("ffn_act", ffn_act_fn, T, D_FF // fw, [(f_pre, fw, fcol), (f_pre, fw, ucol)],
               [(cwg, fw, fcol), (cwu, fw, fcol), (cbg, fw, fcol), (cbu, fw, fcol)], [(D_FF, fw, fcol, BF16)],
               halo=True)[0]
    h2 = _mm("ffn_down", a2, w["w_ffn_down"], "nn", add=h1)

    fnw = w["final_norm_w"].reshape(1, D_MODEL)

    per_tile = tm // CHUNK
    tgt_specs = [(target, D_MODEL, c0, None, (CHUNK, lambda i, k=k: jnp.maximum(per_tile * i - 1 + k, 0)))
                 for k in range(per_tile)]

    def loss_fn(i, hv, *rest):
        tv, nw = jnp.concatenate(rest[:per_tile], axis=0), rest[per_tile]
        valid = _row_ids(i, T) >= CHUNK
        y, vjp = jax.vjp(_rms, hv, nw)
        diff = jnp.where(valid, y - tv, 0.0)
        dh, dw = vjp(diff * (1.0 / D_MODEL))
        part = 0.5 / D_MODEL * jnp.sum(jnp.sum(diff * diff, axis=1, keepdims=True), axis=0, keepdims=True)
        return dh, jnp.broadcast_to(part, (1, 128)), dw

    dh2, loss_acc, d_fnw = _rows("loss", loss_fn, T, 1, [(h2, D_MODEL, c0)] + tgt_specs, [(fnw, D_MODEL, c0)],
                                 [(D_MODEL, D_MODEL, c0, F32)], [(1, 128, 128, c0), (1, D_MODEL, D_MODEL, c0)])
    loss = loss_acc[0, 0]
    grads = {"final_norm_w": d_fnw.reshape(D_MODEL)}

    da2 = _mm("d_ffn_act", dh2, w["w_ffn_down"], "nt", out_dtype=BF16)
    grads["w_ffn_down"] = _mm("g_ffn_down", a2, dh2, "tn", out_dtype=BF16)

    def ffn_bwd_fn(i, ge, ue, de, wg, wu, bg, bu):
        fg = _conv3(ge, wg) + bg
        fu = _conv3(ue, wu) + bu
        sg = jax.nn.sigmoid(fg)
        dfg = de * fu * (sg * (1.0 + fg * (1.0 - sg)))
        dfu = de * (fg * sg)
        n = ge.shape[0]

        def wgrad(df, xe):
            df_c = _center(df)
            return jnp.concatenate([jnp.sum(df_c * _center(pltpu.roll(xe, 1, 0)), axis=0, keepdims=True),
                                    jnp.sum(df_c * _center(xe), axis=0, keepdims=True),
                                    jnp.sum(df_c * _center(pltpu.roll(xe, n - 1, 0)), axis=0, keepdims=True)], axis=0)

        return (_center(_conv3_t(dfg, wg)), _center(_conv3_t(dfu, wu)), wgrad(dfg, ge), wgrad(dfu, ue),
                jnp.sum(_center(dfg), axis=0, keepdims=True), jnp.sum(_center(dfu), axis=0, keepdims=True))

    dfg_pre, dfu_pre, g_cwg, g_cwu, g_cbg, g_cbu = _rows(
        "ffn_act_bwd", ffn_bwd_fn, T, D_FF // fw, [(f_pre, fw, fcol), (f_pre, fw, ucol), (da2, fw, fcol)],
        [(cwg, fw, fcol), (cwu, fw, fcol), (cbg, fw, fcol), (cbu, fw, fcol)],
        [(D_FF, fw, fcol, BF16), (D_FF, fw, fcol, BF16)],
        [(3, D_FF, fw, fcol), (3, D_FF, fw, fcol), (1, D_FF, fw, fcol), (1, D_FF, fw, fcol)], halo=True)
    grads["w_ffn_conv"] = jnp.concatenate([g_cwg, g_cwu], axis=1)
    grads["b_ffn_conv"] = jnp.concatenate([g_cbg, g_cbu], axis=1)
    dn2 = _mm("d_norm_ffn_g", dfg_pre, w_up_g, "nn")
    dn2 = _mm("d_norm_ffn_u", dfu_pre, w_up_u, "nn", add=dn2)
    grads["w_ffn_up_t"] = jnp.concatenate([_mm("g_ffn_up_g", dfg_pre, n2, "tn", out_dtype=BF16), _mm("g_ffn_up_u", dfu_pre, n2, "tn", out_dtype=BF16)],
                                          axis=0)

    def norm_bwd(name, h, nw, dn, dres):
        def fn(i, hv, dnv, drv, wv):
            _, vjp = jax.vjp(_rms, hv, wv)
            dh, dw = vjp(dnv)
            return dh + drv, dw
        return _rows(name, fn, T, 1, [(h, D_MODEL, c0), (dn, D_MODEL, c0), (dres, D_MODEL, c0)], [(nw, D_MODEL, c0)],
                     [(D_MODEL, D_MODEL, c0, F32)], [(1, D_MODEL, D_MODEL, c0)])

    dh1, grads["norm_ffn_w"] = norm_bwd("norm_ffn_bwd", h1, w["norm_ffn_w"], dn2, dh2)

    dmerged = _mm("d_merged", dh1, w["w_out"], "nt", out_dtype=BF16)
    grads["w_out"] = _mm("g_out", merged, dh1, "tn", out_dtype=BF16)

    def merge_bwd_fn(i, gates, yr, ys, dm):
        valid = _row_ids(i, T) >= PAD_ROWS
        _, vjp = jax.vjp(lambda a, b, c: _merge(a, b, c, valid), gates, yr, ys)
        return vjp(dm)

    dgates, dyr, dys = _rows("merge_bwd", merge_bwd_fn, T, 1,
                             [(p_gates, 2048, c0), (y_ret_o, 1024, c0), (y_ssd_o, 1024, c0), (dmerged, 1024, c0)],
                             [], [(2048, 2048, c0, BF16), (1024, 1024, c0, BF16), (1024, 1024, c0, BF16)])
    dproj = {"gates": dgates}

    da_ssd = _mm("d_ssd_act", dys, w["w_ssd_out"], "nt", out_dtype=BF16)
    grads["w_ssd_out"] = _mm("g_ssd_out", a_ssd, dys, "tn", out_dtype=BF16)

    def ssd_post_bwd_fn(i, yf, yb, xv, zv, da, dk, nw):
        _, vjp = jax.vjp(_ssd_post, yf, yb, xv, zv, dk, nw)
        dyf, _, dxv, dzv, ddk, dnw = vjp(da)
        return dyf, dxv, dzv, ddk, dnw

    d_main = lax.empty(p_main.shape, BF16)

    def into_main(name, width, cf=c0):
        base = seg_at[name] // width
        return (d_main, width, lambda j: base + cf(j), BF16)

    dy_ssd, dxs_skip, d_main, g_dskip_e, grads["ssd_norm_w"] = _rows(
        "ssd_post_bwd", ssd_post_bwd_fn, T, SSD_GROUPS,
        [(y_dir[0], gw_, gcol), (y_dir[1], gw_, gcol), (act["xs"], gw_, gcol), seg("z", gw_, gcol),
         (da_ssd, gw_, gcol)],
        [(dskip_e, gw_, gcol), (w["ssd_norm_w"], gw_, gcol)],
        [(2048, gw_, gcol, BF16), (2048, gw_, gcol, BF16), into_main("z", gw_, gcol)],
        [(1, 2048, gw_, gcol), (1, 2048, gw_, gcol)])
    grads["d_skip"] = g_dskip_e.reshape(SSD_HEADS, SSD_HEAD_DIM).sum(axis=1).reshape(1, SSD_HEADS)

    dxs_dir, db_dir, dc_dir, draw, g_bias, g_alog = _ssd_bwd(act["xs"], act["B"], act["C"], small, states, dy_ssd, T)
    grads["dt_bias_f"], grads["dt_bias_b"] = g_bias[0].reshape(1, SSD_HEADS), g_bias[1].reshape(1, SSD_HEADS)
    grads["a_log_f"], grads["a_log_b"] = g_alog[0].reshape(1, SSD_HEADS), g_alog[1].reshape(1, SSD_HEADS)
    d_dt = jnp.stack(draw).transpose(2, 0, 1, 3).reshape(T, 2 * SSD_HEADS)
    dproj["dt"] = jnp.pad(d_dt, ((0, 0), (0, CHUNK - 2 * SSD_HEADS))).astype(BF16)

    def make_conv_bwd(nsum):
        def fn(i, xe, *rest):
            ds, (cw, cb) = rest[:nsum], rest[nsum:]
            r = _row_ids(i, T, True)
            dact = ds[0]
            for t in ds[1:]:
                dact = dact + t
            dact = jnp.where(r >= PAD_ROWS, dact, 0.0)
            pre = _conv3(xe, cw) + cb
            sg = jax.nn.sigmoid(pre)
            dpre = dact * (sg * (1.0 + pre * (1.0 - sg)))
            n = xe.shape[0]
            dpc = _center(dpre)
            dw = jnp.concatenate([jnp.sum(dpc * _center(pltpu.roll(xe, 1, 0)), axis=0, keepdims=True),
                                  jnp.sum(dpc * _center(xe), axis=0, keepdims=True),
                                  jnp.sum(dpc * _center(pltpu.roll(xe, n - 1, 0)), axis=0, keepdims=True)], axis=0)
            return _center(_conv3_t(dpre, cw)), dw, jnp.sum(dpc, axis=0, keepdims=True)
        return fn

    g_cw, g_cb = {}, {}
    cots = {"xs": [(dxs_dir[0], 512, gcol), (dxs_dir[1], 512, gcol), (dxs_skip, 512, gcol)],
            "B": [(db_dir[0], 512, gcol), (db_dir[1], 512, gcol)],
            "C": [(dc_dir[0], 512, gcol), (dc_dir[1], 512, gcol)]}
    for name in ("xs", "B", "C"):
        wd = conv_w[name].shape[1]
        d_main, g_cw[name], g_cb[name] = _rows(
            "ssd_conv_bwd_" + name, make_conv_bwd(len(cots[name])), T, wd // 512,
            [seg(name, 512, gcol)] + cots[name], [(conv_w[name], 512, gcol), (conv_b[name], 512, gcol)],
            [into_main(name, 512, gcol)], [(3, wd, 512, gcol), (1, wd, 512, gcol)], halo=True)
    grads["w_ssd_conv"] = jnp.concatenate([g_cw["xs"], g_cw["B"], g_cw["C"]], axis=1)
    grads["b_ssd_conv"] = jnp.concatenate([g_cb["xs"], g_cb["B"], g_cb["C"]], axis=1)

    da_ret = _mm("d_ret_act", dyr, w["w_ret_out"], "nt", out_dtype=BF16)
    grads["w_ret_out"] = _mm("g_ret_out", a_ret, dyr, "tn", out_dtype=BF16)
    tick = early_grads({n: grads.pop(n) for n in ("w_ffn_up_t", "w_ret_out", "w_ssd_out", "w_out", "w_ffn_down")})

    def ret_post_bwd_fn(i, y, g, da, gw):
        _, vjp = jax.vjp(_ret_post, y, g, gw)
        return vjp(da)

    dy_ret, d_main, grads["ret_gn_w"] = _rows(
        "ret_post_bwd", ret_post_bwd_fn, T, 1, [(y_ret, 1024, c0), seg("g", 1024), (da_ret, 1024, c0)],
        [(w["ret_gn_w"] + tick, 1024, c0)], [(1024, 1024, c0, BF16), into_main("g", 1024)], [(1, 1024, 1024, c0)])
    d_main = _retention("retention_dv", kr, qr, dy_ret, T, RET_QK_DIM, RET_V_DIM, into=(d_main, seg_at["v"]))
    dqr = _retention("retention_dq", dy_ret, v_at, kr, T, RET_V_DIM, RET_QK_DIM)
    dkr = _retention("retention_dk", v_at, dy_ret, qr, T, RET_V_DIM, RET_QK_DIM)

    def rot_bwd_fn(i, dq, dk, csv, snv):
        parts = [_rot_t(dq[:, h * 128:(h + 1) * 128], csv, snv) for h in range(RET_HEADS)]
        parts += [_rot_t(dk[:, h * 128:(h + 1) * 128] * scale, csv, snv) for h in range(RET_HEADS)]
        return (jnp.concatenate(parts, axis=1),)

    d_main = _rows("rotary_bwd", rot_bwd_fn, T, 1, [(dqr, 512, c0), (dkr, 512, c0), (cs, 128, c0), (sn, 128, c0)],
                   [], [into_main("qk", 1024)])[0]

    g_in = [_mm("g_in_main", d_main, u, "tn", out_dtype=BF16),
            _mm("g_in_dt", dproj["dt"], u, "tn", out_dtype=BF16)[:2 * SSD_HEADS],
            _mm("g_in_gates", dproj["gates"], u, "tn", out_dtype=BF16)]
    tick = in_grads(jnp.concatenate(g_in, axis=0))
    du = _mm("d_u_dt", dproj["dt"] + tick.astype(BF16), w_dt, "nn")
    du = _mm("d_u_main", d_main, w_main, "nn", add=du)
    du = _mm("d_u_gates", dproj["gates"], w_gates, "nn", add=du)
    dh0, grads["norm_mix_w"] = norm_bwd("norm_mix_bwd", h0, w["norm_mix_w"], du, dh1)
    grads["meta_tokens"] = dh0[PAD_ROWS:CHUNK]
    return loss, dh0[CHUNK:], grads


MESH_ID = pl.DeviceIdType.MESH
ANY = pl.BlockSpec(memory_space=pl.ANY)


def _me_and_peers():
    x, y, c = lax.axis_index("x"), lax.axis_index("y"), lax.axis_index("c")
    peers = []
    for k in range(1, N_DEV):
        px = 1 - x if k & 4 else x
        py = 1 - y if k & 2 else y
        pc = 1 - c if k & 1 else c
        peers.append(((px, py, pc), 4 * px + 2 * py + pc))
    return 4 * x + 2 * y + c, peers


def _push_blocks(name, src, per_peer):
    blk = src.shape[1:] if per_peer else src.shape

    def body(src_ref, out_ref, send_sems, recv_sems, local_sem):
        me, peers = _me_and_peers()
        mine = src_ref.at[me] if per_peer else src_ref
        local = pltpu.make_async_copy(mine, out_ref.at[me], local_sem)
        local.start()
        sends = []
        for k, (dev, idx) in enumerate(peers):
            cp = pltpu.make_async_remote_copy(
                src_ref=src_ref.at[idx] if per_peer else src_ref, dst_ref=out_ref.at[me],
                send_sem=send_sems.at[k], recv_sem=recv_sems.at[k], device_id=dev, device_id_type=MESH_ID)
            cp.start()
            sends.append(cp)
        for k, (dev, idx) in enumerate(peers):
            pltpu.make_async_remote_copy(
                src_ref=mine, dst_ref=out_ref.at[idx], send_sem=send_sems.at[k], recv_sem=recv_sems.at[k],
                device_id=dev, device_id_type=MESH_ID).wait_recv()
        for cp in sends:
            cp.wait_send()
        local.wait()

    return pl.pallas_call(
        body, name=name, in_specs=[ANY], out_specs=ANY,
        out_shape=jax.ShapeDtypeStruct((N_DEV,) + tuple(blk), src.dtype),
        scratch_shapes=[pltpu.SemaphoreType.DMA((N_DEV - 1,)), pltpu.SemaphoreType.DMA((N_DEV - 1,)),
                        pltpu.SemaphoreType.DMA],
    )(src)


def _gather_two_level(name, src):
    def body(x_ref, out_ref, send_sems, recv_sems, local_sem):
        x, y, c = lax.axis_index("x"), lax.axis_index("y"), lax.axis_index("c")
        me, sibling = (x, y, c), (x, y, 1 - c)
        chips = [(1 - x, y), (x, 1 - y), (1 - x, 1 - y)]

        def rows(px, py, pc):
            return out_ref.at[4 * px + 2 * py + pc]

        def copy(k, block, to, src_ref=None):
            return pltpu.make_async_remote_copy(
                src_ref=rows(*block) if src_ref is None else src_ref, dst_ref=rows(*block),
                send_sem=send_sems.at[k], recv_sem=recv_sems.at[k], device_id=to, device_id_type=MESH_ID)

        mine = pltpu.make_async_copy(x_ref, rows(*me), local_sem)
        mine.start()
        first = [copy(0, me, sibling, x_ref)] + [copy(1 + j, me, (*chip, c), x_ref) for j, chip in enumerate(chips)]
        for cp in first:
            cp.start()
        passed = [copy(4 + j, (*chip, c), sibling) for j, chip in enumerate(chips)]
        for j, chip in enumerate(chips):
            copy(1 + j, (*chip, c), me).wait_recv()
            passed[j].start()
        copy(0, sibling, me).wait_recv()
        for j, chip in enumerate(chips):
            copy(4 + j, (*chip, 1 - c), me).wait_recv()
        for cp in first + passed:
            cp.wait_send()
        mine.wait()

    return pl.pallas_call(
        body, name=name, in_specs=[ANY], out_specs=ANY,
        out_shape=jax.ShapeDtypeStruct((N_DEV,) + tuple(src.shape), src.dtype),
        scratch_shapes=[pltpu.SemaphoreType.DMA((N_DEV - 1,)), pltpu.SemaphoreType.DMA((N_DEV - 1,)),
                        pltpu.SemaphoreType.DMA],
    )(src)


HBM = pl.BlockSpec(memory_space=pltpu.HBM)
SEM = pl.BlockSpec(memory_space=pltpu.SEMAPHORE)
EFFECT = pltpu.SideEffectType.DATAFLOW_SIDE_EFFECTING


def _peer_copy(src_ref, land_ref, send_sems, recv_sems, per_peer, me, a, k, dev, idx, receiving):
    s = a * (N_DEV - 1) + k
    return pltpu.make_async_remote_copy(
        src_ref=src_ref.at[idx] if per_peer else src_ref, dst_ref=land_ref.at[idx if receiving else me],
        send_sem=send_sems.at[s], recv_sem=recv_sems.at[s], device_id=dev, device_id_type=MESH_ID)


def _push_start(name, srcs, per_peer):
    n = len(srcs)
    land_shapes = [(N_DEV,) + tuple(s.shape[1:] if per_peer else s.shape) for s in srcs]

    def body(*refs):
        src_refs, land_refs, send_sems, recv_sems, token = refs[:n], refs[n:2 * n], refs[2 * n], refs[2 * n + 1], refs[-1]
        me, peers = _me_and_peers()
        for a in range(n):
            for k, (dev, idx) in enumerate(peers):
                _peer_copy(src_refs[a], land_refs[a], send_sems, recv_sems, per_peer, me, a, k, dev, idx, False).start()
        token[...] = jnp.zeros_like(token)

    sems = pltpu.SemaphoreType.DMA((n * (N_DEV - 1),))
    res = pl.pallas_call(
        body, name=name,
        out_shape=(sems, sems, *[pltpu.HBM(s.shape, s.dtype) for s in srcs],
                   *[pltpu.HBM(ls, s.dtype) for ls, s in zip(land_shapes, srcs)], jax.ShapeDtypeStruct((8, 128), F32)),
        in_specs=(HBM,) * (2 * n), out_specs=(SEM, SEM) + (HBM,) * (2 * n) + (pl.BlockSpec(memory_space=pltpu.VMEM),),
        input_output_aliases={i: 2 + i for i in range(2 * n)},
        compiler_params=pltpu.CompilerParams(has_side_effects=EFFECT),
    )(*[pltpu.with_memory_space_constraint(s, pltpu.HBM) for s in srcs],
      *[pltpu.with_memory_space_constraint(lax.empty(ls, s.dtype), pltpu.HBM) for ls, s in zip(land_shapes, srcs)])
    return res[0], res[1], res[2:2 + n], res[2 + n:2 + 2 * n], res[-1]


def _push_wait(name, send_sems, recv_sems, srcs_thru, lands_thru, after, per_peer):
    n = len(srcs_thru)

    def body(*refs):
        src_refs, land_refs, send_sems, recv_sems = refs[:n], refs[n:2 * n], refs[2 * n], refs[2 * n + 1]
        me, peers = _me_and_peers()
        for a in range(n):
            for k, (dev, idx) in enumerate(peers):
                cp = _peer_copy(src_refs[a], land_refs[a], send_sems, recv_sems, per_peer, me, a, k, dev, idx, True)
                cp.wait_send()
                cp.wait_recv()

    both = list(srcs_thru) + list(lands_thru)
    res = pl.pallas_call(
        body, name=name, out_shape=tuple(pltpu.HBM(t.shape, t.dtype) for t in both),
        in_specs=(HBM,) * (2 * n) + (SEM, SEM, ANY), out_specs=(HBM,) * (2 * n),
        input_output_aliases={i: i for i in range(2 * n)},
        compiler_params=pltpu.CompilerParams(has_side_effects=EFFECT),
    )(*both, send_sems, recv_sems, after)
    return res[:n], res[n:]


def _sum_blocks(name, blocks):
    _, R, C = blocks.shape
    tc = next(t for t in (1024, 512, 256, 128) if C % t == 0 and (N_DEV * R * t * 2 <= 6 * 2 ** 20 or t == 128))

    def body(b_ref, o_ref):
        acc = b_ref[0].astype(F32)
        for k in range(1, N_DEV):
            acc = acc + b_ref[k].astype(F32)
        o_ref[...] = acc

    return pl.pallas_call(
        body, name=name, grid=(C // tc,), in_specs=[pl.BlockSpec((N_DEV, R, tc), lambda j: (0, 0, j))],
        out_specs=pl.BlockSpec((R, tc), lambda j: (0, j)), out_shape=jax.ShapeDtypeStruct((R, C), F32),
        compiler_params=_params(("arbitrary",)),
    )(blocks)


def _adamw(name, w, g, m, v):
    R, C = w.shape
    tr = R if R <= 512 else _pick(R, (256, 184, 176, 128, 8))
    spec = pl.BlockSpec((tr, C), lambda i: (i, 0))

    def body(w_ref, g_ref, m_ref, v_ref, d_ref, mo_ref, vo_ref):
        gv = g_ref[...]
        mn = ADAM_B1 * m_ref[...] + (1.0 - ADAM_B1) * gv
        vn = ADAM_B2 * v_ref[...] + (1.0 - ADAM_B2) * jnp.square(gv)
        m_hat = mn / (1.0 - ADAM_B1 ** ADAM_STEP)
        v_hat = vn / (1.0 - ADAM_B2 ** ADAM_STEP)
        d_ref[...] = -ADAM_LR * (m_hat / (jnp.sqrt(v_hat) + ADAM_EPS) + ADAM_WD * w_ref[...])
        mo_ref[...] = mn
        vo_ref[...] = vn

    return pl.pallas_call(
        body, name=name, grid=(R // tr,), in_specs=[spec] * 4, out_specs=[spec] * 3,
        out_shape=[jax.ShapeDtypeStruct((R, C), F32)] * 3, compiler_params=_params(("arbitrary",)),
    )(w, g, m, v)


WEIGHTS = ("meta_tokens", "norm_mix_w", "w_in", "ret_gn_w", "w_ret_out", "w_ssd_conv", "b_ssd_conv", "dt_bias_f",
           "dt_bias_b", "a_log_f", "a_log_b", "d_skip", "ssd_norm_w", "w_ssd_out", "w_out", "norm_ffn_w", "w_ffn_up",
           "w_ffn_conv", "b_ffn_conv", "w_ffn_down", "final_norm_w")
BIG = (("w_in", 1288, True), ("w_ffn_up", 704, True), ("w_ret_out", 128, False), ("w_ssd_out", 256, False),
       ("w_out", 128, False), ("w_ffn_down", 352, False))
REPLICATED = ("norm_mix_w", "ret_gn_w", "b_ssd_conv", "dt_bias_f", "dt_bias_b", "a_log_f", "a_log_b", "d_skip",
              "ssd_norm_w", "norm_ffn_w", "b_ffn_conv", "final_norm_w")
SMALL_SHARDED = (("meta_tokens", 16, 1024), ("w_ssd_conv", 3, 3072), ("w_ffn_conv", 3, 5632))


BIG_IN, BIG_REST = BIG[:1], BIG[1:]


def _pack_big(tree, group):
    parts = []
    for name, _, transposed in group:
        a = tree[name][0]
        parts.append(a.T if transposed else a)
    return jnp.concatenate(parts, axis=0)


def _unpack_big(slab, group):
    out, r0 = {}, 0
    for name, r, transposed in group:
        a = slab[r0:r0 + r]
        out[name] = (a.T if transposed else a)[None]
        r0 += r
    return out


def _pack_flat(arrays, rows):
    flat = jnp.concatenate([a.reshape(-1) for a in arrays])
    return jnp.pad(flat, (0, rows * D_MODEL - flat.shape[0])).reshape(rows, D_MODEL)


def _unpack_flat(slab, shapes):
    flat, out, o = slab.reshape(-1), [], 0
    for s in shapes:
        n = math.prod(s)
        out.append(flat[o:o + n].reshape(s))
        o += n
    return out


def kernel(x, meta_tokens, norm_mix_w, w_in, ret_gn_w, w_ret_out, w_ssd_conv, b_ssd_conv, dt_bias_f, dt_bias_b, a_log_f, a_log_b, d_skip, ssd_norm_w, w_ssd_out, w_out, norm_ffn_w, w_ffn_up, w_ffn_conv, b_ffn_conv, w_ffn_down, final_norm_w, loss_target, m_meta_tokens, m_norm_mix_w, m_w_in, m_ret_gn_w, m_w_ret_out, m_w_ssd_conv, m_b_ssd_conv, m_dt_bias_f, m_dt_bias_b, m_a_log_f, m_a_log_b, m_d_skip, m_ssd_norm_w, m_w_ssd_out, m_w_out, m_norm_ffn_w, m_w_ffn_up, m_w_ffn_conv, m_b_ffn_conv, m_w_ffn_down, m_final_norm_w, v_meta_tokens, v_norm_mix_w, v_w_in, v_ret_gn_w, v_w_ret_out, v_w_ssd_conv, v_b_ssd_conv, v_dt_bias_f, v_dt_bias_b, v_a_log_f, v_a_log_b, v_d_skip, v_ssd_norm_w, v_w_ssd_out, v_w_out, v_norm_ffn_w, v_w_ffn_up, v_w_ffn_conv, v_b_ffn_conv, v_w_ffn_down, v_final_norm_w):
    given = dict(locals())
    wt = {n: given[n] for n in WEIGHTS}
    mt = {n: given["m_" + n] for n in WEIGHTS}
    vt = {n: given["v_" + n] for n in WEIGHTS}
    me = 4 * lax.axis_index("x") + 2 * lax.axis_index("y") + lax.axis_index("c")

    small_names = [n for n, _, _ in SMALL_SHARDED]
    small_local = lambda tree: [tree[n].reshape(r, c // N_DEV) for n, r, c in SMALL_SHARDED]
    all_in = _gather_two_level("gather_w_in", _pack_big(wt, BIG_IN).astype(BF16))
    all_s = _push_blocks("gather_small", _pack_flat(small_local(wt), 8), False)
    slab_view = lambda tree, name, transposed: tree[name][0].T if transposed else tree[name][0]
    rest_srcs = [slab_view(wt, name, t).astype(BF16) for name, _, t in BIG_REST]
    rest_srcs, all_in, all_s = lax.optimization_barrier((rest_srcs, all_in, all_s))
    rest_flight = _push_start("gather_rest_start", rest_srcs, False)
    all_s = all_s.reshape(N_DEV, -1)
    full = {"w_in_t": all_in.reshape(-1, D_MODEL)}

    def lands_with_own(flight, after, per_peer, name):
        srcs, lands = _push_wait(name, *flight[:4], after, per_peer)
        own = lambda s: lax.dynamic_slice_in_dim(s, me, 1, axis=0) if per_peer else s[None]
        return [lax.dynamic_update_slice_in_dim(land, own(s), me, axis=0) for s, land in zip(srcs, lands)]

    def late_weights(after):
        lands = lands_with_own(rest_flight, after, False, "gather_rest_wait")
        return {name + ("_t" if t else ""): land.reshape(N_DEV * r, D_MODEL) for (name, r, t), land in zip(BIG_REST, lands)}

    flights = {}

    def start_exchange(key, group, gd):
        srcs = [gd[name + ("_t" if t else "")].astype(BF16).reshape(N_DEV, r, D_MODEL) for name, r, t in group]
        flights[key] = _push_start("exchange_" + key + "_start", srcs, True)
        return flights[key][4][0, 0]

    o = 0
    for name, r, c in SMALL_SHARDED:
        n = r * c // N_DEV
        full[name] = all_s[:, o:o + n].reshape(N_DEV, r, c // N_DEV).transpose(1, 0, 2).reshape(r, c)
        o += n
    for name in REPLICATED:
        full[name] = wt[name]

    grads, delta, new_m, new_v = {}, {}, {}, {}

    def finish_exchange(key, group, after):
        lands = lands_with_own(flights[key], after, True, "exchange_" + key + "_wait")
        for (name, _, transposed), land in zip(group, lands):
            back = (lambda a: a.T[None]) if transposed else (lambda a: a[None])
            g_sum = _sum_blocks("sum_" + name, land)
            d, mn, vn = _adamw("adamw_" + name, slab_view(wt, name, transposed), g_sum,
                               slab_view(mt, name, transposed), slab_view(vt, name, transposed))
            grads[name], delta[name], new_m[name], new_v[name] = back(g_sum), back(d), back(mn), back(vn)

    def in_grads(gi):
        tick = start_exchange("in", BIG_IN, {"w_in_t": gi})
        finish_exchange("rest", BIG_REST, flights["in"][4])
        tick, _ = lax.optimization_barrier((tick, [delta[name] for name, _, _ in BIG_REST]))
        return tick

    loss, grad_x, g = _local_step(x[0], loss_target[0], full, rest_flight[4][0, 0], late_weights,
                                  lambda gd: start_exchange("rest", BIG_REST, gd), in_grads)

    finish_exchange("in", BIG_IN, g["norm_mix_w"])
    small_parts = [g[n] for n in REPLICATED] + [g[n] for n in small_names] + [loss.reshape(1)]
    g_small = _sum_blocks("sum_small", _push_blocks("gather_small_grads", _pack_flat(small_parts, 64), False))
    small_red = _unpack_flat(g_small, [wt[n].shape for n in REPLICATED] + [(r, c) for _, r, c in SMALL_SHARDED] + [(1,)])
    grads.update(zip(REPLICATED, small_red[:len(REPLICATED)]))
    for (name, r, c), red in zip(SMALL_SHARDED, small_red[len(REPLICATED):-1]):
        grads[name] = lax.dynamic_slice(red, (0, me * (c // N_DEV)), (r, c // N_DEV)).reshape(wt[name].shape)
    loss_all = small_red[-1][0]

    rest = list(REPLICATED) + small_names
    shapes = [wt[n].shape for n in rest]
    pack_rest = lambda tree: _pack_flat([tree[n] for n in rest], 24)
    d_rest, m_rest, v_rest = _adamw("adamw_small", pack_rest(wt), pack_rest(grads), pack_rest(mt), pack_rest(vt))
    delta.update(zip(rest, _unpack_flat(d_rest, shapes)))
    new_m.update(zip(rest, _unpack_flat(m_rest, shapes)))
    new_v.update(zip(rest, _unpack_flat(v_rest, shapes)))

    return (loss_all, grad_x[None], *[grads[n] for n in WEIGHTS], *[delta[n] for n in WEIGHTS],
            *[new_m[n] for n in WEIGHTS], *[new_v[n] for n in WEIGHTS])
```

```python
import functools
import math

import jax
import jax.numpy as jnp
from jax import lax
from jax.experimental import pallas as pl
from jax.experimental.pallas import tpu as pltpu

F32 = jnp.float32
BF16 = jnp.bfloat16

D_MODEL = 1024
CHUNK = 128
N_META = 16
PAD_ROWS = CHUNK - N_META
RET_HEADS = 4
RET_QK_DIM = 128
RET_V_DIM = 256
SSD_HEADS = 32
SSD_HEAD_DIM = 64
SSD_GROUPS = 4
SSD_STATE = 128
HEADS_PER_GROUP = SSD_HEADS // SSD_GROUPS
PAIRS_PER_GROUP = HEADS_PER_GROUP // 2
D_FF = 2816
EPS = 1e-6
ROPE_BASE = 10000.0
N_DEV = 8

ADAM_LR = 0.001
ADAM_B1 = 0.9
ADAM_B2 = 0.999
ADAM_EPS = 1e-08
ADAM_WD = 0.01
ADAM_STEP = 10

VMEM_LIMIT = 56 * 1024 * 1024
HALO = 16
HIGHEST = lax.Precision.HIGHEST

SEGMENTS = (("qk", 0, 1024), ("v", 1024, 2048), ("g", 2048, 3072), ("z", 3072, 5120), ("xs", 5120, 7168),
            ("B", 7168, 7680), ("C", 7680, 8192), ("dt", 8192, 8256), ("gates", 8256, 10304))


def _pick(n, cands):
    for c in cands:
        if n % c == 0:
            return c
    raise ValueError(f"no tile for {n}")


def _params(sem):
    return pltpu.CompilerParams(dimension_semantics=sem, vmem_limit_bytes=VMEM_LIMIT)


def _dot(a, b, dims=(((1,), (0,)), ((), ())), precision=None):
    return lax.dot_general(a, b, dims, preferred_element_type=F32, precision=precision)


def _dot_nt(a, b):
    return _dot(a, b, (((1,), (1,)), ((), ())))


def _dot_tn(a, b):
    return _dot(a, b, (((0,), (0,)), ((), ())))


def _mm(name, a, b, mode, add=None, out_dtype=F32):
    if mode == "nn":
        (M, K), N = a.shape, b.shape[1]
    elif mode == "nt":
        (M, K), N = a.shape, b.shape[0]
    else:
        (K, M), N = a.shape, b.shape[1]
    tn = _pick(N, (1408, 1024, 512, 128, 64))
    if mode == "tn":
        tm = M if M <= 1024 else _pick(M, (1408, 1024))
        tk = _pick(K, (2112, 512, 256, 128))
    else:
        tm = _pick(M, (1056, 512, 256, 128))
        tk = K if K <= 2816 else _pick(K, (2048, 1408, 1024))
    nk = K // tk
    if mode == "nn":
        a_spec = pl.BlockSpec((tm, tk), lambda n, m, k: (m, k))
        b_spec = pl.BlockSpec((tk, tn), lambda n, m, k: (k, n))
        dims = (((1,), (0,)), ((), ()))
    elif mode == "nt":
        a_spec = pl.BlockSpec((tm, tk), lambda n, m, k: (m, k))
        b_spec = pl.BlockSpec((tn, tk), lambda n, m, k: (n, k))
        dims = (((1,), (1,)), ((), ()))
    else:
        a_spec = pl.BlockSpec((tk, tm), lambda n, m, k: (k, m))
        b_spec = pl.BlockSpec((tk, tn), lambda n, m, k: (k, n))
        dims = (((0,), (0,)), ((), ()))
    o_spec = pl.BlockSpec((tm, tn), lambda n, m, k: (m, n))
    in_specs = [a_spec, b_spec] + ([o_spec] if add is not None else [])
    args = [a, b] + ([add] if add is not None else [])

    def body(*refs):
        if add is not None:
            a_ref, b_ref, r_ref, o_ref, acc = refs
        else:
            a_ref, b_ref, o_ref, acc = refs
        k = pl.program_id(2)
        p = _dot(a_ref[...].astype(BF16), b_ref[...].astype(BF16), dims)

        def finish(r):
            if add is not None:
                r = r + r_ref[...]
            o_ref[...] = r.astype(out_dtype)

        if nk == 1:
            finish(p)
        else:
            @pl.when(k == 0)
            def _():
                acc[...] = p

            @pl.when(k > 0)
            def _():
                acc[...] += p

            @pl.when(k == nk - 1)
            def _():
                finish(acc[...])

    return pl.pallas_call(
        body, name=name, grid=(N // tn, M // tm, nk), in_specs=in_specs, out_specs=o_spec,
        out_shape=jax.ShapeDtypeStruct((M, N), out_dtype),
        scratch_shapes=[pltpu.VMEM((tm, tn) if nk > 1 else (8, 128), F32)],
        compiler_params=_params(("arbitrary", "arbitrary", "arbitrary")),
    )(*args)


ANY_SPACE = pl.BlockSpec(memory_space=pl.ANY)


def _const(c):
    return lambda j: c


def _rows(name, fn, T, ncol, ins, params, outs, accs=(), halo=False, tall=False):
    tm = _pick(T, (1056, 512, 256, 128)) if tall else _pick(T, (384, 256, 128))
    R = T // tm
    hb = tm // HALO
    in_specs, args = [], []
    for spec in ins:
        arr, w, cf = spec[:3]
        lead = spec[3] if len(spec) > 3 else None
        if len(spec) > 4:
            rows, rf = spec[4]
            in_specs.append(pl.BlockSpec((rows, w), lambda j, i, cf=cf, rf=rf: (rf(i), cf(j))))
            args.append(arr)
            continue
        if lead is None:
            mk = lambda blk, rf, cf=cf: pl.BlockSpec(blk, lambda j, i: (rf(i), cf(j)))
            shape = lambda r, w=w: (r, w)
        else:
            mk = lambda blk, rf, cf=cf, lead=lead: pl.BlockSpec(blk, lambda j, i: (lead, rf(i), cf(j)))
            shape = lambda r, w=w: (None, r, w)
        in_specs.append(mk(shape(tm), lambda i: i))
        args.append(arr)
        if halo:
            in_specs.append(mk(shape(HALO), lambda i: jnp.maximum(i * hb - 1, 0)))
            in_specs.append(mk(shape(HALO), lambda i: jnp.minimum((i + 1) * hb, T // HALO - 1)))
            args += [arr, arr]
    for arr, w, cf in params:
        in_specs.append(pl.BlockSpec((arr.shape[0], w), lambda j, i, cf=cf: (0, cf(j))))
        args.append(arr)
    out_shape, out_specs, aliases = [], [], {}
    for k, (tw, w, cf, dt) in enumerate(outs):
        if not isinstance(tw, int):
            aliases[len(args)] = k
            in_specs.append(ANY_SPACE)
            args.append(tw)
            tw = tw.shape[1]
        out_shape.append(jax.ShapeDtypeStruct((T, tw), dt))
        out_specs.append(pl.BlockSpec((tm, w), lambda j, i, cf=cf: (i, cf(j))))
    for r, tw, w, cf in accs:
        out_shape.append(jax.ShapeDtypeStruct((r, tw), F32))
        out_specs.append(pl.BlockSpec((r, w), lambda j, i, cf=cf: (0, cf(j))))
    n_in, n_par, n_out, n_acc, n_alias = len(ins), len(params), len(outs), len(accs), len(aliases)

    def body(*refs):
        i = pl.program_id(1)
        vals, p = [], 0
        for _ in range(n_in):
            if halo:
                before = jnp.where(i > 0, refs[p + 1][...], jnp.zeros_like(refs[p + 1]))
                after = jnp.where(i < R - 1, refs[p + 2][...], jnp.zeros_like(refs[p + 2]))
                vals.append(jnp.concatenate([before, refs[p][...], after], axis=0).astype(F32))
                p += 3
            else:
                vals.append(refs[p][...].astype(F32))
                p += 1
        pvals = [refs[p + k][...] for k in range(n_par)]
        p += n_par + n_alias
        res = fn(i, *vals, *pvals)
        for k in range(n_out):
            refs[p + k][...] = res[k].astype(refs[p + k].dtype)
        p += n_out
        for k in range(n_acc):
            ref, v = refs[p + k], res[n_out + k]

            @pl.when(i == 0)
            def _(ref=ref, v=v):
                ref[...] = v

            @pl.when(i > 0)
            def _(ref=ref, v=v):
                ref[...] += v

    res = pl.pallas_call(
        body, name=name, grid=(ncol, R), in_specs=in_specs, out_specs=out_specs, out_shape=out_shape,
        input_output_aliases=aliases, compiler_params=_params(("arbitrary", "arbitrary")),
    )(*args)
    return res


def _tile_rows(T):
    return _pick(T, (384, 256, 128))


def _row_ids(i, T, halo=False):
    tm = _tile_rows(T)
    if halo:
        return i * tm - HALO + lax.broadcasted_iota(jnp.int32, (tm + 2 * HALO, 1), 0)
    return i * tm + lax.broadcasted_iota(jnp.int32, (tm, 1), 0)


def _rms(x, w):
    return x * lax.rsqrt(jnp.mean(x * x, axis=-1, keepdims=True) + EPS) * w


def _silu(x):
    return x * jax.nn.sigmoid(x)


def _conv3(x, w):
    n = x.shape[0]
    return w[0:1] * pltpu.roll(x, 1, 0) + w[1:2] * x + w[2:3] * pltpu.roll(x, n - 1, 0)


def _conv3_t(d, w):
    n = d.shape[0]
    return w[0:1] * pltpu.roll(d, n - 1, 0) + w[1:2] * d + w[2:3] * pltpu.roll(d, 1, 0)


def _center(x):
    return x[HALO:x.shape[0] - HALO]


def _retention(name, a, b, v, T, da, dv, into=None):
    (a, a0), (b, b0), (v, v0) = [t if isinstance(t, tuple) else (t, 0) for t in (a, b, v)]
    nc = T // CHUNK
    log_gammas = [math.log(1.0 - 2.0 ** (-5.0 - h)) for h in range(RET_HEADS)]

    def body(*refs):
        a_ref, b_ref, v_ref = refs[:3]
        out_ref, o_ref, st, st_b = refs[-4:]
        h = pl.program_id(0)
        lg = jnp.float32(log_gammas[RET_HEADS - 1])
        for k in range(RET_HEADS - 2, -1, -1):
            lg = jnp.where(h == k, jnp.float32(log_gammas[k]), lg)
        li = lax.broadcasted_iota(jnp.int32, (CHUNK, CHUNK), 0)
        si = lax.broadcasted_iota(jnp.int32, (CHUNK, CHUNK), 1)
        dmat = jnp.exp(lg * jnp.abs(li - si).astype(F32))
        pos = lax.broadcasted_iota(jnp.int32, (CHUNK, 1), 0).astype(F32)
        kdec_f = jnp.exp((CHUNK - 1 - pos) * lg)
        qdec_f = jnp.exp((pos + 1) * lg)
        kdec_b = jnp.exp(pos * lg)
        qdec_b = jnp.exp((CHUNK - pos) * lg)
        cdec = jnp.exp(CHUNK * lg)

        def rows(n):
            return pl.ds(pl.multiple_of(n * CHUNK, CHUNK), CHUNK)

        st[...] = jnp.zeros_like(st)
        st_b[...] = jnp.zeros_like(st_b)
        o_ref[...] = jnp.zeros_like(o_ref)

        def step(m, carry):
            r = rows(m)
            av, bv, vv = a_ref[r, :], b_ref[r, :], v_ref[r, :].astype(BF16)
            s = _dot_nt(av.astype(BF16), bv.astype(BF16)) * dmat
            o_ref[r, :] += _dot(s.astype(BF16), vv) + _dot((av * qdec_f).astype(BF16), st[...].astype(BF16))
            st[...] = cdec * st[...] + _dot_tn((bv * kdec_f).astype(BF16), vv)
            r = rows(nc - 1 - m)
            av, bv, vv = a_ref[r, :], b_ref[r, :], v_ref[r, :].astype(BF16)
            o_ref[r, :] += _dot((av * qdec_b).astype(BF16), st_b[...].astype(BF16))
            st_b[...] = cdec * st_b[...] + _dot_tn((bv * kdec_b).astype(BF16), vv)
            return carry

        lax.fori_loop(0, nc, step, 0, unroll=11 if nc % 11 == 0 else 1)
        out_ref[...] = o_ref[...].astype(out_ref.dtype)

    in_specs = [pl.BlockSpec((T, da), lambda h: (0, a0 // da + h)), pl.BlockSpec((T, da), lambda h: (0, b0 // da + h)),
                pl.BlockSpec((T, dv), lambda h: (0, v0 // dv + h))]
    if into is None:
        args, o0, aliases = (a, b, v), 0, {}
        out_shape = jax.ShapeDtypeStruct((T, RET_HEADS * dv), F32)
    else:
        args, o0, aliases = (a, b, v, into[0]), into[1], {3: 0}
        in_specs.append(ANY_SPACE)
        out_shape = jax.ShapeDtypeStruct(into[0].shape, into[0].dtype)
    return pl.pallas_call(
        body, name=name, grid=(RET_HEADS,), in_specs=in_specs,
        out_specs=pl.BlockSpec((T, dv), lambda h: (0, o0 // dv + h)), out_shape=out_shape,
        input_output_aliases=aliases,
        scratch_shapes=[pltpu.VMEM((T, dv), F32), pltpu.VMEM((da, dv), F32), pltpu.VMEM((da, dv), F32)],
        compiler_params=_params(("arbitrary",)),
    )(*args)


def _softplus(x):
    return jnp.maximum(x, 0.0) + jnp.log1p(jnp.exp(-jnp.abs(x)))


def _lane_lo():
    return lax.broadcasted_iota(jnp.int32, (1, CHUNK), 1) < SSD_HEAD_DIM


def _pair_cols(col, j):
    return jnp.where(_lane_lo(), col[:, 2 * j:2 * j + 1], col[:, 2 * j + 1:2 * j + 2])


def _pair_rows(colr, j):
    lo = lax.broadcasted_iota(jnp.int32, (CHUNK, 1), 0) < SSD_HEAD_DIM
    return jnp.where(lo, colr[2 * j:2 * j + 1, :], colr[2 * j + 1:2 * j + 2, :])


def _onehot8(h):
    return (lax.broadcasted_iota(jnp.int32, (1, HEADS_PER_GROUP), 1) == h).astype(F32)


def _ssd_pre(d, c, rawc, rawr, bc, br, alc, alr):
    li = lax.broadcasted_iota(jnp.int32, (CHUNK, CHUNK), 0)
    si = lax.broadcasted_iota(jnp.int32, (CHUNK, CHUNK), 1)
    dif = li - si if d == 0 else si - li
    mask = dif >= 0
    mask_t = dif <= 0
    rowc = c * CHUNK + lax.broadcasted_iota(jnp.int32, (CHUNK, 1), 0)
    rowr = c * CHUNK + lax.broadcasted_iota(jnp.int32, (1, CHUNK), 1)
    dtc = jnp.where(rowc >= PAD_ROWS, _softplus(rawc + bc), 0.0)
    dtr = jnp.where(rowr >= PAD_ROWS, _softplus(rawr + br), 0.0)
    ac = -jnp.exp(alc)
    ar = -jnp.exp(alr)
    dlc = dtc * ac
    dlr = dtr * ar
    alpc = _dot(mask.astype(F32), dlc, precision=HIGHEST)
    alpr = _dot(dlr, mask_t.astype(F32), precision=HIGHEST)
    endc = jnp.sum(dlc, axis=0, keepdims=True)
    endr = jnp.sum(dlr, axis=1, keepdims=True)
    return dict(mask=mask, mask_t=mask_t, dtc=dtc, ac=ac, alpc=alpc, alpr=alpr, endc=endc, endr=endr,
                valid=rowc >= PAD_ROWS)


def _chunk_of(d, n, nc):
    return n + d * (nc - 1 - 2 * n)


GROUP_WIDTH = HEADS_PER_GROUP * SSD_HEAD_DIM


def _ssd_in_specs(d, cfn):
    return [
        pl.BlockSpec((CHUNK, GROUP_WIDTH), lambda g, n: (cfn(d, n), g)),
        pl.BlockSpec((CHUNK, SSD_STATE), lambda g, n: (cfn(d, n), g)),
        pl.BlockSpec((CHUNK, SSD_STATE), lambda g, n: (cfn(d, n), g)),
        pl.BlockSpec((None, None, CHUNK, HEADS_PER_GROUP), lambda g, n: (d, g, cfn(d, n), 0)),
        pl.BlockSpec((None, None, HEADS_PER_GROUP, CHUNK), lambda g, n: (d, g, 0, cfn(d, n))),
        pl.BlockSpec((None, None, 1, HEADS_PER_GROUP), lambda g, n: (d, g, 0, 0)),
        pl.BlockSpec((None, None, HEADS_PER_GROUP, 1), lambda g, n: (d, g, 0, 0)),
        pl.BlockSpec((None, None, 1, HEADS_PER_GROUP), lambda g, n: (d, g, 0, 0)),
        pl.BlockSpec((None, None, HEADS_PER_GROUP, 1), lambda g, n: (d, g, 0, 0)),
    ]


N_SSD_IN = 9


def _ssd_fwd(xs, bm, cm, small, T):
    nc = T // CHUNK
    cfn = lambda d, n: _chunk_of(d, n, nc)

    def one_direction(d, n, ins, y_ref, hs_ref, h_scr):
        x_ref, b_ref, c_ref, rawc_ref, rawr_ref, bc_ref, br_ref, alc_ref, alr_ref = ins
        c = cfn(d, n)
        q = _ssd_pre(d, c, rawc_ref[...], rawr_ref[...], bc_ref[...], br_ref[...], alc_ref[...], alr_ref[...])
        bv = b_ref[...].astype(BF16)
        cv = c_ref[...].astype(BF16)
        cb = _dot_nt(cv, bv)
        lo = _lane_lo()
        for j in range(PAIRS_PER_GROUP):
            xp = x_ref[:, j * CHUNK:(j + 1) * CHUNK]
            xd = xp * _pair_cols(q["dtc"], j)
            xdb = xd.astype(BF16)
            yi = []
            for e in range(2):
                h = 2 * j + e
                lm = jnp.exp(jnp.where(q["mask"], q["alpc"][:, h:h + 1] - q["alpr"][h:h + 1, :], -jnp.inf))
                yi.append(_dot((cb * lm).astype(BF16), xdb))
            alp = _pair_cols(q["alpc"], j)
            hp = h_scr[j]
            hs_ref[j] = hp
            yo = jnp.exp(alp) * _dot_nt(cv, hp.astype(BF16))
            y_ref[:, j * CHUNK:(j + 1) * CHUNK] = (jnp.where(lo, yi[0], yi[1]) + yo).astype(y_ref.dtype)
            de = jnp.exp(_pair_cols(q["endc"], j) - alp)
            h_scr[j] = jnp.exp(_pair_rows(q["endr"], j)) * hp + _dot_tn((xd * de).astype(BF16), bv)

    def body(*refs):
        n = pl.program_id(1)
        ins, (y_f, y_b, hs_f, hs_b, h_scr) = refs[:2 * N_SSD_IN], refs[2 * N_SSD_IN:]

        @pl.when(n == 0)
        def _():
            h_scr[...] = jnp.zeros_like(h_scr)

        one_direction(0, n, ins[:N_SSD_IN], y_f, hs_f, h_scr.at[0])
        one_direction(1, n, ins[N_SSD_IN:], y_b, hs_b, h_scr.at[1])

    y_spec = lambda d: pl.BlockSpec((CHUNK, GROUP_WIDTH), lambda g, n: (cfn(d, n), g))
    hs_spec = lambda d: pl.BlockSpec((None, None, PAIRS_PER_GROUP, CHUNK, SSD_STATE),
                                     lambda g, n: (g, cfn(d, n), 0, 0, 0))
    y_shape = jax.ShapeDtypeStruct((T, SSD_HEADS * SSD_HEAD_DIM), BF16)
    hs_shape = jax.ShapeDtypeStruct((SSD_GROUPS, nc, PAIRS_PER_GROUP, CHUNK, SSD_STATE), F32)
    y_f, y_b, hs_f, hs_b = pl.pallas_call(
        body, name="ssd_fwd", grid=(SSD_GROUPS, nc),
        in_specs=_ssd_in_specs(0, cfn) + _ssd_in_specs(1, cfn),
        out_specs=[y_spec(0), y_spec(1), hs_spec(0), hs_spec(1)],
        out_shape=[y_shape, y_shape, hs_shape, hs_shape],
        scratch_shapes=[pltpu.VMEM((2, PAIRS_PER_GROUP, CHUNK, SSD_STATE), F32)],
        compiler_params=_params(("arbitrary", "arbitrary")),
    )(xs, bm, cm, *small, xs, bm, cm, *small)
    return (y_f, y_b), (hs_f, hs_b)


def _ssd_bwd(xs, bm, cm, small, hs, dy, T):
    nc = T // CHUNK
    cfn = lambda d, n: _chunk_of(1 - d, n, nc)

    def one_direction(d, n, ins, outs, dh_scr):
        x_ref, b_ref, c_ref, rawc_ref, rawr_ref, bc_ref, br_ref, alc_ref, alr_ref, hs_ref, dy_ref = ins
        dx_ref, db_ref, dc_ref, draw_ref, dbias_ref, dalog_ref = outs
        c = cfn(d, n)
        rawc, bc = rawc_ref[...], bc_ref[...]
        q = _ssd_pre(d, c, rawc, rawr_ref[...], bc, br_ref[...], alc_ref[...], alr_ref[...])
        b32, c32 = b_ref[...], c_ref[...]
        bv, cv = b32.astype(BF16), c32.astype(BF16)
        cb = _dot_nt(cv, bv)
        cbt = _dot_nt(bv, cv)
        lo = _lane_lo()
        row_lo = lax.broadcasted_iota(jnp.int32, (CHUNK, 1), 0) < SSD_HEAD_DIM
        dcb = jnp.zeros((CHUNK, CHUNK), F32)
        dcp = jnp.zeros((CHUNK, SSD_STATE), F32)
        dbp = jnp.zeros((CHUNK, SSD_STATE), F32)
        dalp = jnp.zeros((CHUNK, HEADS_PER_GROUP), F32)
        dend = jnp.zeros((1, HEADS_PER_GROUP), F32)
        ddtx = jnp.zeros((CHUNK, HEADS_PER_GROUP), F32)

        def half_sums(t):
            return (jnp.sum(jnp.where(lo, t, 0.0), axis=1, keepdims=True),
                    jnp.sum(jnp.where(lo, 0.0, t), axis=1, keepdims=True))

        for j in range(PAIRS_PER_GROUP):
            xp = x_ref[:, j * CHUNK:(j + 1) * CHUNK]
            dtp = _pair_cols(q["dtc"], j)
            xd = xp * dtp
            xdb = xd.astype(BF16)
            dyp = dy_ref[:, j * CHUNK:(j + 1) * CHUNK]
            dyb = dyp.astype(BF16)
            hn = hs_ref[j]
            hnb = hn.astype(BF16)
            dh1 = dh_scr[j]
            dh1b = dh1.astype(BF16)
            alp = _pair_cols(q["alpc"], j)
            ea = jnp.exp(alp)
            de = jnp.exp(_pair_cols(q["endc"], j) - alp)
            dxi = []
            for e in range(2):
                h = 2 * j + e
                diff = q["alpc"][:, h:h + 1] - q["alpr"][h:h + 1, :]
                lm = jnp.exp(jnp.where(q["mask"], diff, -jnp.inf))
                mt = cbt * jnp.exp(jnp.where(q["mask_t"], -diff, -jnp.inf))
                dxi.append(_dot(mt.astype(BF16), dyb))
                dyeb_h = (jnp.where(lo, dyp, 0.0) if e == 0 else jnp.where(lo, 0.0, dyp)).astype(BF16)
                gl = _dot_nt(dyeb_h, xdb) * lm
                dcb = dcb + gl
                ra = jnp.sum(gl * cb - _dot_nt(xdb, dyeb_h) * mt, axis=1, keepdims=True)
                dalp = dalp + ra * _onehot8(h)
            y_off = ea * _dot_nt(cv, hnb)
            dxs_state = de * _dot_nt(bv, dh1b)
            dxd = jnp.where(lo, dxi[0], dxi[1]) + dxs_state
            dyeb = (dyp * ea).astype(BF16)
            dcp = dcp + _dot(dyeb, hnb)
            dbp = dbp + _dot((xd * de).astype(BF16), dh1b)
            dh_scr[j] = jnp.exp(_pair_rows(q["endr"], j)) * dh1 + _dot_tn(dyeb, cv)
            r0, r1 = half_sums(dyp * y_off - xd * dxs_state)
            dalp = dalp + r0 * _onehot8(2 * j) + r1 * _onehot8(2 * j + 1)
            t0, t1 = half_sums(jnp.sum(xd * dxs_state, axis=0, keepdims=True))
            u = dh1 * hn
            u0 = jnp.sum(jnp.sum(jnp.where(row_lo, u, 0.0), axis=0, keepdims=True), axis=1, keepdims=True)
            u1 = jnp.sum(jnp.sum(jnp.where(row_lo, 0.0, u), axis=0, keepdims=True), axis=1, keepdims=True)
            eend = jnp.exp(q["endc"])
            dend = dend + (t0 + eend * u0) * _onehot8(2 * j) + (t1 + eend * u1) * _onehot8(2 * j + 1)
            dx_ref[:, j * CHUNK:(j + 1) * CHUNK] = (dxd * dtp).astype(dx_ref.dtype)
            w0, w1 = half_sums(dxd * xp)
            ddtx = ddtx + w0 * _onehot8(2 * j) + w1 * _onehot8(2 * j + 1)

        dcbb = dcb.astype(BF16)
        dc_ref[...] = (dcp + _dot(dcbb, bv)).astype(dc_ref.dtype)
        db_ref[...] = (dbp + _dot_tn(dcbb, cv)).astype(db_ref.dtype)
        ddl = _dot(q["mask_t"].astype(F32), dalp, precision=HIGHEST) + dend
        ddt = ddl * q["ac"] + ddtx
        draw = jnp.where(q["valid"], ddt * jax.nn.sigmoid(rawc + bc), 0.0)
        draw_ref[...] = draw
        dbias = jnp.sum(draw, axis=0, keepdims=True)
        dalog = jnp.sum(ddl * q["dtc"], axis=0, keepdims=True) * q["ac"]

        @pl.when(n == 0)
        def _():
            dbias_ref[...] = dbias
            dalog_ref[...] = dalog

        @pl.when(n > 0)
        def _():
            dbias_ref[...] += dbias
            dalog_ref[...] += dalog

    n_in, n_out = N_SSD_IN + 2, 6

    def body(*refs):
        n = pl.program_id(1)
        ins, outs, dh_scr = refs[:2 * n_in], refs[2 * n_in:2 * (n_in + n_out)], refs[-1]

        @pl.when(n == 0)
        def _():
            dh_scr[...] = jnp.zeros_like(dh_scr)

        one_direction(0, n, ins[:n_in], outs[:n_out], dh_scr.at[0])
        one_direction(1, n, ins[n_in:], outs[n_out:], dh_scr.at[1])

    def in_specs(d):
        return _ssd_in_specs(d, cfn) + [
            pl.BlockSpec((None, None, PAIRS_PER_GROUP, CHUNK, SSD_STATE), lambda g, n: (g, cfn(d, n), 0, 0, 0)),
            pl.BlockSpec((CHUNK, GROUP_WIDTH), lambda g, n: (cfn(d, n), g))]

    def out_specs(d):
        acc = pl.BlockSpec((None, 1, HEADS_PER_GROUP), lambda g, n: (g, 0, 0))
        return [pl.BlockSpec((CHUNK, GROUP_WIDTH), lambda g, n: (cfn(d, n), g)),
                pl.BlockSpec((CHUNK, SSD_STATE), lambda g, n: (cfn(d, n), g)),
                pl.BlockSpec((CHUNK, SSD_STATE), lambda g, n: (cfn(d, n), g)),
                pl.BlockSpec((None, CHUNK, HEADS_PER_GROUP), lambda g, n: (g, cfn(d, n), 0)), acc, acc]

    out_shape = [jax.ShapeDtypeStruct((T, SSD_HEADS * SSD_HEAD_DIM), BF16),
                 jax.ShapeDtypeStruct((T, SSD_GROUPS * SSD_STATE), BF16),
                 jax.ShapeDtypeStruct((T, SSD_GROUPS * SSD_STATE), BF16),
                 jax.ShapeDtypeStruct((SSD_GROUPS, T, HEADS_PER_GROUP), F32),
                 jax.ShapeDtypeStruct((SSD_GROUPS, 1, HEADS_PER_GROUP), F32),
                 jax.ShapeDtypeStruct((SSD_GROUPS, 1, HEADS_PER_GROUP), F32)]
    res = pl.pallas_call(
        body, name="ssd_bwd", grid=(SSD_GROUPS, nc),
        in_specs=in_specs(0) + in_specs(1), out_specs=out_specs(0) + out_specs(1), out_shape=out_shape * 2,
        scratch_shapes=[pltpu.VMEM((2, PAIRS_PER_GROUP, CHUNK, SSD_STATE), F32)],
        compiler_params=_params(("arbitrary", "arbitrary")),
    )(xs, bm, cm, *small, hs[0], dy, xs, bm, cm, *small, hs[1], dy)
    return [(res[k], res[n_out + k]) for k in range(n_out)]


def _rot(x, cs, sn):
    return x * cs + pltpu.roll(x, RET_QK_DIM // 2, 1) * sn


def _rot_t(d, cs, sn):
    return d * cs + pltpu.roll(d * sn, RET_QK_DIM // 2, 1)


def _ret_post(y, g, w):
    parts = []
    for h in range(RET_HEADS):
        yh = y[:, h * RET_V_DIM:(h + 1) * RET_V_DIM]
        mu = jnp.mean(yh, axis=-1, keepdims=True)
        var = jnp.mean(jnp.square(yh - mu), axis=-1, keepdims=True)
        parts.append((yh - mu) * lax.rsqrt(var + EPS))
    return _silu(g) * (jnp.concatenate(parts, axis=1) * w)


def _ssd_post(yf, yb, xs, z, dskip, w):
    y = (yf + yb + xs * dskip) * _silu(z)
    return y * lax.rsqrt(jnp.mean(y * y, axis=-1, keepdims=True) + EPS) * w


def _merge(gates, yr, ys, valid):
    m = jax.nn.sigmoid(gates[:, :D_MODEL]) * yr + jax.nn.sigmoid(gates[:, D_MODEL:]) * ys
    return jnp.where(valid, m, 0.0)


def _rope_tables(T):
    half = RET_QK_DIM // 2
    inv = ROPE_BASE ** (-jnp.arange(half, dtype=F32) / half)
    pos = (jnp.arange(T) - PAD_ROWS).astype(F32)
    ang = pos[:, None] * inv[None, :]
    cos, sin = jnp.cos(ang), jnp.sin(ang)
    return jnp.concatenate([cos, cos], axis=1), jnp.concatenate([-sin, sin], axis=1)


def _per_group(v):
    c = v.reshape(SSD_GROUPS, 1, HEADS_PER_GROUP)
    return c, c.reshape(SSD_GROUPS, HEADS_PER_GROUP, 1)


def _local_step(x, target, w, tick, late_weights, early_grads, in_grads):
    S = x.shape[0]
    T = S + CHUNK
    tm = _tile_rows(T)
    c0 = _const(0)

    h0 = jnp.concatenate([jnp.zeros((PAD_ROWS, D_MODEL), F32), w["meta_tokens"], x], axis=0)
    seg_at = {name: a for name, a, _ in SEGMENTS}
    w_main = w["w_in_t"][:seg_at["dt"]]
    w_dt = jnp.pad(w["w_in_t"][seg_at["dt"]:seg_at["gates"]], ((0, CHUNK - 2 * SSD_HEADS), (0, 0)))
    w_gates = w["w_in_t"][seg_at["gates"]:]

    def norm_cast(name, h, nw):
        return _rows(name, lambda i, hv, wv: (_rms(hv, wv),), T, 1, [(h, D_MODEL, c0)], [(nw, D_MODEL, c0)],
                     [(D_MODEL, D_MODEL, c0, BF16)], tall=True)[0]

    u = norm_cast("norm_mix", h0, w["norm_mix_w"] + tick)
    p_main = _mm("proj_main", u, w_main, "nt", out_dtype=BF16)
    p_dt = _mm("proj_dt", u, w_dt, "nt")
    p_gates = _mm("proj_gates", u, w_gates, "nt", out_dtype=BF16)

    def seg(name, width, cf=c0):
        base = seg_at[name] // width
        return (p_main, width, lambda j: base + cf(j))

    cs, sn = _rope_tables(T)
    scale = RET_QK_DIM ** -0.5

    def rot_fn(i, qk, csv, snv):
        q = [_rot(qk[:, h * 128:(h + 1) * 128], csv, snv) for h in range(RET_HEADS)]
        k = [_rot(qk[:, (RET_HEADS + h) * 128:(RET_HEADS + h + 1) * 128], csv, snv) * scale for h in range(RET_HEADS)]
        return jnp.concatenate(q, axis=1), jnp.concatenate(k, axis=1)

    qr, kr = _rows("rotary", rot_fn, T, 1, [seg("qk", 1024), (cs, 128, c0), (sn, 128, c0)], [],
                   [(512, 512, c0, F32), (512, 512, c0, F32)], tall=True)
    v_at = (p_main, seg_at["v"])
    y_ret = _retention("retention", qr, kr, v_at, T, RET_QK_DIM, RET_V_DIM)
    a_ret = _rows("ret_post", lambda i, y, g, gw: (_ret_post(y, g, gw),), T, 1,
                  [(y_ret, 1024, c0), seg("g", 1024)], [(w["ret_gn_w"], 1024, c0)],
                  [(1024, 1024, c0, BF16)], tall=True)[0]

    conv_w = {"xs": w["w_ssd_conv"][:, :2048], "B": w["w_ssd_conv"][:, 2048:2560], "C": w["w_ssd_conv"][:, 2560:]}
    conv_b = {"xs": w["b_ssd_conv"][:, :2048], "B": w["b_ssd_conv"][:, 2048:2560], "C": w["b_ssd_conv"][:, 2560:]}

    def ssd_conv_fn(i, xe, cw, cb):
        r = _row_ids(i, T, True)
        return (_center(jnp.where(r >= PAD_ROWS, _silu(_conv3(xe, cw) + cb), 0.0)),)

    act = {}
    for name in ("xs", "B", "C"):
        wd = conv_w[name].shape[1]
        cw = 512
        act[name] = _rows("ssd_conv_" + name, ssd_conv_fn, T, wd // cw, [seg(name, cw, lambda j: j)],
                          [(conv_w[name], cw, lambda j: j), (conv_b[name], cw, lambda j: j)],
                          [(wd, cw, lambda j: j, BF16)], halo=True)[0]

    raw = p_dt[:, :2 * SSD_HEADS].reshape(T, 2, SSD_GROUPS, HEADS_PER_GROUP)
    rawc = raw.transpose(1, 2, 0, 3)
    rawr = raw.transpose(1, 2, 3, 0)
    bias = [_per_group(w["dt_bias_f"]), _per_group(w["dt_bias_b"])]
    alog = [_per_group(w["a_log_f"]), _per_group(w["a_log_b"])]
    small = (rawc, rawr, jnp.stack([bias[0][0], bias[1][0]]), jnp.stack([bias[0][1], bias[1][1]]),
             jnp.stack([alog[0][0], alog[1][0]]), jnp.stack([alog[0][1], alog[1][1]]))
    y_dir, states = _ssd_fwd(act["xs"], act["B"], act["C"], small, T)

    dskip_e = jnp.repeat(w["d_skip"], SSD_HEAD_DIM, axis=1)
    gcol = lambda j: j
    gw_ = 512
    a_ssd = _rows("ssd_post", lambda i, yf, yb, xv, zv, dk, nw: (_ssd_post(yf, yb, xv, zv, dk, nw),), T, SSD_GROUPS,
                  [(y_dir[0], gw_, gcol), (y_dir[1], gw_, gcol), (act["xs"], gw_, gcol), seg("z", gw_, gcol)],
                  [(dskip_e, gw_, gcol), (w["ssd_norm_w"], gw_, gcol)], [(2048, gw_, gcol, BF16)])[0]

    w = dict(w, **late_weights(a_ssd))
    w_up_g, w_up_u = w["w_ffn_up_t"][:D_FF], w["w_ffn_up_t"][D_FF:]
    y_ret_o = _mm("ret_out", a_ret, w["w_ret_out"], "nn", out_dtype=BF16)
    y_ssd_o = _mm("ssd_out", a_ssd, w["w_ssd_out"], "nn", out_dtype=BF16)

    def merge_fn(i, gates, yr, ys):
        return (_merge(gates, yr, ys, _row_ids(i, T) >= PAD_ROWS),)

    merged = _rows("merge", merge_fn, T, 1, [(p_gates, 2048, c0), (y_ret_o, 1024, c0), (y_ssd_o, 1024, c0)], [],
                   [(1024, 1024, c0, BF16)])[0]
    h1 = _mm("mix_out", merged, w["w_out"], "nn", add=h0)

    n2 = norm_cast("norm_ffn", h1, w["norm_ffn_w"])
    f_pre = _mm("ffn_up", n2, w["w_ffn_up_t"], "nt", out_dtype=BF16)
    cwg, cwu = w["w_ffn_conv"][:, :D_FF], w["w_ffn_conv"][:, D_FF:]
    cbg, cbu = w["b_ffn_conv"][:, :D_FF], w["b_ffn_conv"][:, D_FF:]
    fcol = lambda j: j
    fw = 1408

    def ffn_act_fn(i, ge, ue, wg, wu, bg, bu):
        return (_center(_silu(_conv3(ge, wg) + bg) * (_conv3(ue, wu) + bu)),)

    ucol = lambda j: D_FF // fw + j
    a2 = _rows("ffn_act", ffn_act_fn, T, D_FF // fw, [(f_pre, fw, fcol), (f_pre, fw, ucol)],
               [(cwg, fw, fcol), (cwu, fw, fcol), (cbg, fw, fcol), (cbu, fw, fcol)], [(D_FF, fw, fcol, BF16)],
               halo=True)[0]
    h2 = _mm("ffn_down", a2, w["w_ffn_down"], "nn", add=h1)

    fnw = w["final_norm_w"].reshape(1, D_MODEL)

    per_tile = tm // CHUNK
    tgt_specs = [(target, D_MODEL, c0, None, (CHUNK, lambda i, k=k: jnp.maximum(per_tile * i - 1 + k, 0)))
                 for k in range(per_tile)]

    def loss_fn(i, hv, *rest):
        tv, nw = jnp.concatenate(rest[:per_tile], axis=0), rest[per_tile]
        valid = _row_ids(i, T) >= CHUNK
        y, vjp = jax.vjp(_rms, hv, nw)
        diff = jnp.where(valid, y - tv, 0.0)
        dh, dw = vjp(diff * (1.0 / D_MODEL))
        part = 0.5 / D_MODEL * jnp.sum(jnp.sum(diff * diff, axis=1, keepdims=True), axis=0, keepdims=True)
        return dh, jnp.broadcast_to(part, (1, 128)), dw

    dh2, loss_acc, d_fnw = _rows("loss", loss_fn, T, 1, [(h2, D_MODEL, c0)] + tgt_specs, [(fnw, D_MODEL, c0)],
                                 [(D_MODEL, D_MODEL, c0, F32)], [(1, 128, 128, c0), (1, D_MODEL, D_MODEL, c0)])
    loss = loss_acc[0, 0]
    grads = {"final_norm_w": d_fnw.reshape(D_MODEL)}

    da2 = _mm("d_ffn_act", dh2, w["w_ffn_down"], "nt", out_dtype=BF16)
    grads["w_ffn_down"] = _mm("g_ffn_down", a2, dh2, "tn", out_dtype=BF16)

    def ffn_bwd_fn(i, ge, ue, de, wg, wu, bg, bu):
        fg = _conv3(ge, wg) + bg
        fu = _conv3(ue, wu) + bu
        sg = jax.nn.sigmoid(fg)
        dfg = de * fu * (sg * (1.0 + fg * (1.0 - sg)))
        dfu = de * (fg * sg)
        n = ge.shape[0]

        def wgrad(df, xe):
            df_c = _center(df)
            return jnp.concatenate([jnp.sum(df_c * _center(pltpu.roll(xe, 1, 0)), axis=0, keepdims=True),
                                    jnp.sum(df_c * _center(xe), axis=0, keepdims=True),
                                    jnp.sum(df_c * _center(pltpu.roll(xe, n - 1, 0)), axis=0, keepdims=True)], axis=0)

        return (_center(_conv3_t(dfg, wg)), _center(_conv3_t(dfu, wu)), wgrad(dfg, ge), wgrad(dfu, ue),
                jnp.sum(_center(dfg), axis=0, keepdims=True), jnp.sum(_center(dfu), axis=0, keepdims=True))

    dfg_pre, dfu_pre, g_cwg, g_cwu, g_cbg, g_cbu = _rows(
        "ffn_act_bwd", ffn_bwd_fn, T, D_FF // fw, [(f_pre, fw, fcol), (f_pre, fw, ucol), (da2, fw, fcol)],
        [(cwg, fw, fcol), (cwu, fw, fcol), (cbg, fw, fcol), (cbu, fw, fcol)],
        [(D_FF, fw, fcol, BF16), (D_FF, fw, fcol, BF16)],
        [(3, D_FF, fw, fcol), (3, D_FF, fw, fcol), (1, D_FF, fw, fcol), (1, D_FF, fw, fcol)], halo=True)
    grads["w_ffn_conv"] = jnp.concatenate([g_cwg, g_cwu], axis=1)
    grads["b_ffn_conv"] = jnp.concatenate([g_cbg, g_cbu], axis=1)
    dn2 = _mm("d_norm_ffn_g", dfg_pre, w_up_g, "nn")
    dn2 = _mm("d_norm_ffn_u", dfu_pre, w_up_u, "nn", add=dn2)
    grads["w_ffn_up_t"] = jnp.concatenate([_mm("g_ffn_up_g", dfg_pre, n2, "tn", out_dtype=BF16), _mm("g_ffn_up_u", dfu_pre, n2, "tn", out_dtype=BF16)],
                                          axis=0)

    def norm_bwd(name, h, nw, dn, dres):
        def fn(i, hv, dnv, drv, wv):
            _, vjp = jax.vjp(_rms, hv, wv)
            dh, dw = vjp(dnv)
            return dh + drv, dw
        return _rows(name, fn, T, 1, [(h, D_MODEL, c0), (dn, D_MODEL, c0), (dres, D_MODEL, c0)], [(nw, D_MODEL, c0)],
                     [(D_MODEL, D_MODEL, c0, F32)], [(1, D_MODEL, D_MODEL, c0)])

    dh1, grads["norm_ffn_w"] = norm_bwd("norm_ffn_bwd", h1, w["norm_ffn_w"], dn2, dh2)

    dmerged = _mm("d_merged", dh1, w["w_out"], "nt", out_dtype=BF16)
    grads["w_out"] = _mm("g_out", merged, dh1, "tn", out_dtype=BF16)

    def merge_bwd_fn(i, gates, yr, ys, dm):
        valid = _row_ids(i, T) >= PAD_ROWS
        _, vjp = jax.vjp(lambda a, b, c: _merge(a, b, c, valid), gates, yr, ys)
        return vjp(dm)

    dgates, dyr, dys = _rows("merge_bwd", merge_bwd_fn, T, 1,
                             [(p_gates, 2048, c0), (y_ret_o, 1024, c0), (y_ssd_o, 1024, c0), (dmerged, 1024, c0)],
                             [], [(2048, 2048, c0, BF16), (1024, 1024, c0, BF16), (1024, 1024, c0, BF16)])
    dproj = {"gates": dgates}

    da_ssd = _mm("d_ssd_act", dys, w["w_ssd_out"], "nt", out_dtype=BF16)
    grads["w_ssd_out"] = _mm("g_ssd_out", a_ssd, dys, "tn", out_dtype=BF16)

    def ssd_post_bwd_fn(i, yf, yb, xv, zv, da, dk, nw):
        _, vjp = jax.vjp(_ssd_post, yf, yb, xv, zv, dk, nw)
        dyf, _, dxv, dzv, ddk, dnw = vjp(da)
        return dyf, dxv, dzv, ddk, dnw

    d_main = lax.empty(p_main.shape, BF16)

    def into_main(name, width, cf=c0):
        base = seg_at[name] // width
        return (d_main, width, lambda j: base + cf(j), BF16)

    dy_ssd, dxs_skip, d_main, g_dskip_e, grads["ssd_norm_w"] = _rows(
        "ssd_post_bwd", ssd_post_bwd_fn, T, SSD_GROUPS,
        [(y_dir[0], gw_, gcol), (y_dir[1], gw_, gcol), (act["xs"], gw_, gcol), seg("z", gw_, gcol),
         (da_ssd, gw_, gcol)],
        [(dskip_e, gw_, gcol), (w["ssd_norm_w"], gw_, gcol)],
        [(2048, gw_, gcol, BF16), (2048, gw_, gcol, BF16), into_main("z", gw_, gcol)],
        [(1, 2048, gw_, gcol), (1, 2048, gw_, gcol)])
    grads["d_skip"] = g_dskip_e.reshape(SSD_HEADS, SSD_HEAD_DIM).sum(axis=1).reshape(1, SSD_HEADS)

    dxs_dir, db_dir, dc_dir, draw, g_bias, g_alog = _ssd_bwd(act["xs"], act["B"], act["C"], small, states, dy_ssd, T)
    grads["dt_bias_f"], grads["dt_bias_b"] = g_bias[0].reshape(1, SSD_HEADS), g_bias[1].reshape(1, SSD_HEADS)
    grads["a_log_f"], grads["a_log_b"] = g_alog[0].reshape(1, SSD_HEADS), g_alog[1].reshape(1, SSD_HEADS)
    d_dt = jnp.stack(draw).transpose(2, 0, 1, 3).reshape(T, 2 * SSD_HEADS)
    dproj["dt"] = jnp.pad(d_dt, ((0, 0), (0, CHUNK - 2 * SSD_HEADS))).astype(BF16)

    def make_conv_bwd(nsum):
        def fn(i, xe, *rest):
            ds, (cw, cb) = rest[:nsum], rest[nsum:]
            r = _row_ids(i, T, True)
            dact = ds[0]
            for t in ds[1:]:
                dact = dact + t
            dact = jnp.where(r >= PAD_ROWS, dact, 0.0)
            pre = _conv3(xe, cw) + cb
            sg = jax.nn.sigmoid(pre)
            dpre = dact * (sg * (1.0 + pre * (1.0 - sg)))
            n = xe.shape[0]
            dpc = _center(dpre)
            dw = jnp.concatenate([jnp.sum(dpc * _center(pltpu.roll(xe, 1, 0)), axis=0, keepdims=True),
                                  jnp.sum(dpc * _center(xe), axis=0, keepdims=True),
                                  jnp.sum(dpc * _center(pltpu.roll(xe, n - 1, 0)), axis=0, keepdims=True)], axis=0)
            return _center(_conv3_t(dpre, cw)), dw, jnp.sum(dpc, axis=0, keepdims=True)
        return fn

    g_cw, g_cb = {}, {}
    cots = {"xs": [(dxs_dir[0], 512, gcol), (dxs_dir[1], 512, gcol), (dxs_skip, 512, gcol)],
            "B": [(db_dir[0], 512, gcol), (db_dir[1], 512, gcol)],
            "C": [(dc_dir[0], 512, gcol), (dc_dir[1], 512, gcol)]}
    for name in ("xs", "B", "C"):
        wd = conv_w[name].shape[1]
        d_main, g_cw[name], g_cb[name] = _rows(
            "ssd_conv_bwd_" + name, make_conv_bwd(len(cots[name])), T, wd // 512,
            [seg(name, 512, gcol)] + cots[name], [(conv_w[name], 512, gcol), (conv_b[name], 512, gcol)],
            [into_main(name, 512, gcol)], [(3, wd, 512, gcol), (1, wd, 512, gcol)], halo=True)
    grads["w_ssd_conv"] = jnp.concatenate([g_cw["xs"], g_cw["B"], g_cw["C"]], axis=1)
    grads["b_ssd_conv"] = jnp.concatenate([g_cb["xs"], g_cb["B"], g_cb["C"]], axis=1)

    da_ret = _mm("d_ret_act", dyr, w["w_ret_out"], "nt", out_dtype=BF16)
    grads["w_ret_out"] = _mm("g_ret_out", a_ret, dyr, "tn", out_dtype=BF16)
    tick = early_grads({n: grads.pop(n) for n in ("w_ffn_up_t", "w_ret_out", "w_ssd_out", "w_out", "w_ffn_down")})

    def ret_post_bwd_fn(i, y, g, da, gw):
        _, vjp = jax.vjp(_ret_post, y, g, gw)
        return vjp(da)

    dy_ret, d_main, grads["ret_gn_w"] = _rows(
        "ret_post_bwd", ret_post_bwd_fn, T, 1, [(y_ret, 1024, c0), seg("g", 1024), (da_ret, 1024, c0)],
        [(w["ret_gn_w"] + tick, 1024, c0)], [(1024, 1024, c0, BF16), into_main("g", 1024)], [(1, 1024, 1024, c0)])
    d_main = _retention("retention_dv", kr, qr, dy_ret, T, RET_QK_DIM, RET_V_DIM, into=(d_main, seg_at["v"]))
    dqr = _retention("retention_dq", dy_ret, v_at, kr, T, RET_V_DIM, RET_QK_DIM)
    dkr = _retention("retention_dk", v_at, dy_ret, qr, T, RET_V_DIM, RET_QK_DIM)

    def rot_bwd_fn(i, dq, dk, csv, snv):
        parts = [_rot_t(dq[:, h * 128:(h + 1) * 128], csv, snv) for h in range(RET_HEADS)]
        parts += [_rot_t(dk[:, h * 128:(h + 1) * 128] * scale, csv, snv) for h in range(RET_HEADS)]
        return (jnp.concatenate(parts, axis=1),)

    d_main = _rows("rotary_bwd", rot_bwd_fn, T, 1, [(dqr, 512, c0), (dkr, 512, c0), (cs, 128, c0), (sn, 128, c0)],
                   [], [into_main("qk", 1024)], tall=True)[0]

    g_in = [_mm("g_in_main", d_main, u, "tn", out_dtype=BF16),
            _mm("g_in_dt", dproj["dt"], u, "tn", out_dtype=BF16)[:2 * SSD_HEADS],
            _mm("g_in_gates", dproj["gates"], u, "tn", out_dtype=BF16)]
    tick = in_grads(jnp.concatenate(g_in, axis=0))
    du = _mm("d_u_dt", dproj["dt"] + tick.astype(BF16), w_dt, "nn")
    du = _mm("d_u_main", d_main, w_main, "nn", add=du)
    du = _mm("d_u_gates", dproj["gates"], w_gates, "nn", add=du)
    dh0, grads["norm_mix_w"] = norm_bwd("norm_mix_bwd", h0, w["norm_mix_w"], du, dh1)
    grads["meta_tokens"] = dh0[PAD_ROWS:CHUNK]
    return loss, dh0[CHUNK:], grads


MESH_ID = pl.DeviceIdType.MESH
ANY = pl.BlockSpec(memory_space=pl.ANY)


def _me_and_peers():
    x, y, c = lax.axis_index("x"), lax.axis_index("y"), lax.axis_index("c")
    peers = []
    for k in range(1, N_DEV):
        px = 1 - x if k & 4 else x
        py = 1 - y if k & 2 else y
        pc = 1 - c if k & 1 else c
        peers.append(((px, py, pc), 4 * px + 2 * py + pc))
    return 4 * x + 2 * y + c, peers


def _push_blocks(name, src, per_peer):
    blk = src.shape[1:] if per_peer else src.shape

    def body(src_ref, out_ref, send_sems, recv_sems, local_sem):
        me, peers = _me_and_peers()
        mine = src_ref.at[me] if per_peer else src_ref
        local = pltpu.make_async_copy(mine, out_ref.at[me], local_sem)
        local.start()
        sends = []
        for k, (dev, idx) in enumerate(peers):
            cp = pltpu.make_async_remote_copy(
                src_ref=src_ref.at[idx] if per_peer else src_ref, dst_ref=out_ref.at[me],
                send_sem=send_sems.at[k], recv_sem=recv_sems.at[k], device_id=dev, device_id_type=MESH_ID)
            cp.start()
            sends.append(cp)
        for k, (dev, idx) in enumerate(peers):
            pltpu.make_async_remote_copy(
                src_ref=mine, dst_ref=out_ref.at[idx], send_sem=send_sems.at[k], recv_sem=recv_sems.at[k],
                device_id=dev, device_id_type=MESH_ID).wait_recv()
        for cp in sends:
            cp.wait_send()
        local.wait()

    return pl.pallas_call(
        body, name=name, in_specs=[ANY], out_specs=ANY,
        out_shape=jax.ShapeDtypeStruct((N_DEV,) + tuple(blk), src.dtype),
        scratch_shapes=[pltpu.SemaphoreType.DMA((N_DEV - 1,)), pltpu.SemaphoreType.DMA((N_DEV - 1,)),
                        pltpu.SemaphoreType.DMA],
    )(src)


def _gather_two_level(name, src):
    def body(x_ref, out_ref, send_sems, recv_sems, local_sem):
        x, y, c = lax.axis_index("x"), lax.axis_index("y"), lax.axis_index("c")
        me, sibling = (x, y, c), (x, y, 1 - c)
        chips = [(1 - x, y), (x, 1 - y), (1 - x, 1 - y)]

        def rows(px, py, pc):
            return out_ref.at[4 * px + 2 * py + pc]

        def copy(k, block, to, src_ref=None):
            return pltpu.make_async_remote_copy(
                src_ref=rows(*block) if src_ref is None else src_ref, dst_ref=rows(*block),
                send_sem=send_sems.at[k], recv_sem=recv_sems.at[k], device_id=to, device_id_type=MESH_ID)

        mine = pltpu.make_async_copy(x_ref, rows(*me), local_sem)
        mine.start()
        first = [copy(0, me, sibling, x_ref)] + [copy(1 + j, me, (*chip, c), x_ref) for j, chip in enumerate(chips)]
        for cp in first:
            cp.start()
        passed = [copy(4 + j, (*chip, c), sibling) for j, chip in enumerate(chips)]
        for j, chip in enumerate(chips):
            copy(1 + j, (*chip, c), me).wait_recv()
            passed[j].start()
        copy(0, sibling, me).wait_recv()
        for j, chip in enumerate(chips):
            copy(4 + j, (*chip, 1 - c), me).wait_recv()
        for cp in first + passed:
            cp.wait_send()
        mine.wait()

    return pl.pallas_call(
        body, name=name, in_specs=[ANY], out_specs=ANY,
        out_shape=jax.ShapeDtypeStruct((N_DEV,) + tuple(src.shape), src.dtype),
        scratch_shapes=[pltpu.SemaphoreType.DMA((N_DEV - 1,)), pltpu.SemaphoreType.DMA((N_DEV - 1,)),
                        pltpu.SemaphoreType.DMA],
    )(src)


HBM = pl.BlockSpec(memory_space=pltpu.HBM)
SEM = pl.BlockSpec(memory_space=pltpu.SEMAPHORE)
EFFECT = pltpu.SideEffectType.DATAFLOW_SIDE_EFFECTING


def _peer_copy(src_ref, land_ref, send_sems, recv_sems, per_peer, me, a, k, dev, idx, receiving):
    s = a * (N_DEV - 1) + k
    return pltpu.make_async_remote_copy(
        src_ref=src_ref.at[idx] if per_peer else src_ref, dst_ref=land_ref.at[idx if receiving else me],
        send_sem=send_sems.at[s], recv_sem=recv_sems.at[s], device_id=dev, device_id_type=MESH_ID)


def _push_start(name, srcs, per_peer):
    n = len(srcs)
    land_shapes = [(N_DEV,) + tuple(s.shape[1:] if per_peer else s.shape) for s in srcs]

    def body(*refs):
        src_refs, land_refs, send_sems, recv_sems, token = refs[:n], refs[n:2 * n], refs[2 * n], refs[2 * n + 1], refs[-1]
        me, peers = _me_and_peers()
        for a in range(n):
            for k, (dev, idx) in enumerate(peers):
                _peer_copy(src_refs[a], land_refs[a], send_sems, recv_sems, per_peer, me, a, k, dev, idx, False).start()
        token[...] = jnp.zeros_like(token)

    sems = pltpu.SemaphoreType.DMA((n * (N_DEV - 1),))
    res = pl.pallas_call(
        body, name=name,
        out_shape=(sems, sems, *[pltpu.HBM(s.shape, s.dtype) for s in srcs],
                   *[pltpu.HBM(ls, s.dtype) for ls, s in zip(land_shapes, srcs)], jax.ShapeDtypeStruct((8, 128), F32)),
        in_specs=(HBM,) * (2 * n), out_specs=(SEM, SEM) + (HBM,) * (2 * n) + (pl.BlockSpec(memory_space=pltpu.VMEM),),
        input_output_aliases={i: 2 + i for i in range(2 * n)},
        compiler_params=pltpu.CompilerParams(has_side_effects=EFFECT),
    )(*[pltpu.with_memory_space_constraint(s, pltpu.HBM) for s in srcs],
      *[pltpu.with_memory_space_constraint(lax.empty(ls, s.dtype), pltpu.HBM) for ls, s in zip(land_shapes, srcs)])
    return res[0], res[1], res[2:2 + n], res[2 + n:2 + 2 * n], res[-1]


def _push_wait(name, send_sems, recv_sems, srcs_thru, lands_thru, after, per_peer):
    n = len(srcs_thru)

    def body(*refs):
        src_refs, land_refs, send_sems, recv_sems = refs[:n], refs[n:2 * n], refs[2 * n], refs[2 * n + 1]
        me, peers = _me_and_peers()
        for a in range(n):
            for k, (dev, idx) in enumerate(peers):
                cp = _peer_copy(src_refs[a], land_refs[a], send_sems, recv_sems, per_peer, me, a, k, dev, idx, True)
                cp.wait_send()
                cp.wait_recv()

    both = list(srcs_thru) + list(lands_thru)
    res = pl.pallas_call(
        body, name=name, out_shape=tuple(pltpu.HBM(t.shape, t.dtype) for t in both),
        in_specs=(HBM,) * (2 * n) + (SEM, SEM, ANY), out_specs=(HBM,) * (2 * n),
        input_output_aliases={i: i for i in range(2 * n)},
        compiler_params=pltpu.CompilerParams(has_side_effects=EFFECT),
    )(*both, send_sems, recv_sems, after)
    return res[:n], res[n:]


def _sum_blocks(name, blocks):
    _, R, C = blocks.shape
    tc = next(t for t in (1024, 512, 256, 128) if C % t == 0 and (N_DEV * R * t * 2 <= 6 * 2 ** 20 or t == 128))

    def body(b_ref, o_ref):
        acc = b_ref[0].astype(F32)
        for k in range(1, N_DEV):
            acc = acc + b_ref[k].astype(F32)
        o_ref[...] = acc

    return pl.pallas_call(
        body, name=name, grid=(C // tc,), in_specs=[pl.BlockSpec((N_DEV, R, tc), lambda j: (0, 0, j))],
        out_specs=pl.BlockSpec((R, tc), lambda j: (0, j)), out_shape=jax.ShapeDtypeStruct((R, C), F32),
        compiler_params=_params(("arbitrary",)),
    )(blocks)


def _adamw(name, w, g, m, v):
    R, C = w.shape
    tr = R if R <= 512 else _pick(R, (256, 184, 176, 128, 8))
    spec = pl.BlockSpec((tr, C), lambda i: (i, 0))

    def body(w_ref, g_ref, m_ref, v_ref, d_ref, mo_ref, vo_ref):
        gv = g_ref[...]
        mn = ADAM_B1 * m_ref[...] + (1.0 - ADAM_B1) * gv
        vn = ADAM_B2 * v_ref[...] + (1.0 - ADAM_B2) * jnp.square(gv)
        m_hat = mn / (1.0 - ADAM_B1 ** ADAM_STEP)
        v_hat = vn / (1.0 - ADAM_B2 ** ADAM_STEP)
        d_ref[...] = -ADAM_LR * (m_hat / (jnp.sqrt(v_hat) + ADAM_EPS) + ADAM_WD * w_ref[...])
        mo_ref[...] = mn
        vo_ref[...] = vn

    return pl.pallas_call(
        body, name=name, grid=(R // tr,), in_specs=[spec] * 4, out_specs=[spec] * 3,
        out_shape=[jax.ShapeDtypeStruct((R, C), F32)] * 3, compiler_params=_params(("arbitrary",)),
    )(w, g, m, v)


WEIGHTS = ("meta_tokens", "norm_mix_w", "w_in", "ret_gn_w", "w_ret_out", "w_ssd_conv", "b_ssd_conv", "dt_bias_f",
           "dt_bias_b", "a_log_f", "a_log_b", "d_skip", "ssd_norm_w", "w_ssd_out", "w_out", "norm_ffn_w", "w_ffn_up",
           "w_ffn_conv", "b_ffn_conv", "w_ffn_down", "final_norm_w")
BIG = (("w_in", 1288, True), ("w_ffn_up", 704, True), ("w_ret_out", 128, False), ("w_ssd_out", 256, False),
       ("w_out", 128, False), ("w_ffn_down", 352, False))
REPLICATED = ("norm_mix_w", "ret_gn_w", "b_ssd_conv", "dt_bias_f", "dt_bias_b", "a_log_f", "a_log_b", "d_skip",
              "ssd_norm_w", "norm_ffn_w", "b_ffn_conv", "final_norm_w")
SMALL_SHARDED = (("meta_tokens", 16, 1024), ("w_ssd_conv", 3, 3072), ("w_ffn_conv", 3, 5632))


BIG_IN, BIG_REST = BIG[:1], BIG[1:]


def _pack_big(tree, group):
    parts = []
    for name, _, transposed in group:
        a = tree[name][0]
        parts.append(a.T if transposed else a)
    return jnp.concatenate(parts, axis=0)


def _unpack_big(slab, group):
    out, r0 = {}, 0
    for name, r, transposed in group:
        a = slab[r0:r0 + r]
        out[name] = (a.T if transposed else a)[None]
        r0 += r
    return out


def _pack_flat(arrays, rows):
    flat = jnp.concatenate([a.reshape(-1) for a in arrays])
    return jnp.pad(flat, (0, rows * D_MODEL - flat.shape[0])).reshape(rows, D_MODEL)


def _unpack_flat(slab, shapes):
    flat, out, o = slab.reshape(-1), [], 0
    for s in shapes:
        n = math.prod(s)
        out.append(flat[o:o + n].reshape(s))
        o += n
    return out


def kernel(x, meta_tokens, norm_mix_w, w_in, ret_gn_w, w_ret_out, w_ssd_conv, b_ssd_conv, dt_bias_f, dt_bias_b, a_log_f, a_log_b, d_skip, ssd_norm_w, w_ssd_out, w_out, norm_ffn_w, w_ffn_up, w_ffn_conv, b_ffn_conv, w_ffn_down, final_norm_w, loss_target, m_meta_tokens, m_norm_mix_w, m_w_in, m_ret_gn_w, m_w_ret_out, m_w_ssd_conv, m_b_ssd_conv, m_dt_bias_f, m_dt_bias_b, m_a_log_f, m_a_log_b, m_d_skip, m_ssd_norm_w, m_w_ssd_out, m_w_out, m_norm_ffn_w, m_w_ffn_up, m_w_ffn_conv, m_b_ffn_conv, m_w_ffn_down, m_final_norm_w, v_meta_tokens, v_norm_mix_w, v_w_in, v_ret_gn_w, v_w_ret_out, v_w_ssd_conv, v_b_ssd_conv, v_dt_bias_f, v_dt_bias_b, v_a_log_f, v_a_log_b, v_d_skip, v_ssd_norm_w, v_w_ssd_out, v_w_out, v_norm_ffn_w, v_w_ffn_up, v_w_ffn_conv, v_b_ffn_conv, v_w_ffn_down, v_final_norm_w):
    given = dict(locals())
    wt = {n: given[n] for n in WEIGHTS}
    mt = {n: given["m_" + n] for n in WEIGHTS}
    vt = {n: given["v_" + n] for n in WEIGHTS}
    me = 4 * lax.axis_index("x") + 2 * lax.axis_index("y") + lax.axis_index("c")

    small_names = [n for n, _, _ in SMALL_SHARDED]
    small_local = lambda tree: [tree[n].reshape(r, c // N_DEV) for n, r, c in SMALL_SHARDED]
    all_in = _gather_two_level("gather_w_in", _pack_big(wt, BIG_IN).astype(BF16))
    all_s = _push_blocks("gather_small", _pack_flat(small_local(wt), 8), False)
    slab_view = lambda tree, name, transposed: tree[name][0].T if transposed else tree[name][0]
    rest_srcs = [slab_view(wt, name, t).astype(BF16) for name, _, t in BIG_REST]
    rest_srcs, all_in, all_s = lax.optimization_barrier((rest_srcs, all_in, all_s))
    rest_flight = _push_start("gather_rest_start", rest_srcs, False)
    all_s = all_s.reshape(N_DEV, -1)
    full = {"w_in_t": all_in.reshape(-1, D_MODEL)}

    def lands_with_own(flight, after, per_peer, name):
        srcs, lands = _push_wait(name, *flight[:4], after, per_peer)
        own = lambda s: lax.dynamic_slice_in_dim(s, me, 1, axis=0) if per_peer else s[None]
        return [lax.dynamic_update_slice_in_dim(land, own(s), me, axis=0) for s, land in zip(srcs, lands)]

    def late_weights(after):
        lands = lands_with_own(rest_flight, after, False, "gather_rest_wait")
        return {name + ("_t" if t else ""): land.reshape(N_DEV * r, D_MODEL) for (name, r, t), land in zip(BIG_REST, lands)}

    flights = {}

    def start_exchange(key, group, gd):
        srcs = [gd[name + ("_t" if t else "")].astype(BF16).reshape(N_DEV, r, D_MODEL) for name, r, t in group]
        flights[key] = _push_start("exchange_" + key + "_start", srcs, True)
        return flights[key][4][0, 0]

    o = 0
    for name, r, c in SMALL_SHARDED:
        n = r * c // N_DEV
        full[name] = all_s[:, o:o + n].reshape(N_DEV, r, c // N_DEV).transpose(1, 0, 2).reshape(r, c)
        o += n
    for name in REPLICATED:
        full[name] = wt[name]

    grads, delta, new_m, new_v = {}, {}, {}, {}

    def finish_exchange(key, group, after):
        lands = lands_with_own(flights[key], after, True, "exchange_" + key + "_wait")
        for (name, _, transposed), land in zip(group, lands):
            back = (lambda a: a.T[None]) if transposed else (lambda a: a[None])
            g_sum = _sum_blocks("sum_" + name, land)
            d, mn, vn = _adamw("adamw_" + name, slab_view(wt, name, transposed), g_sum,
                               slab_view(mt, name, transposed), slab_view(vt, name, transposed))
            grads[name], delta[name], new_m[name], new_v[name] = back(g_sum), back(d), back(mn), back(vn)

    def in_grads(gi):
        tick = start_exchange("in", BIG_IN, {"w_in_t": gi})
        finish_exchange("rest", BIG_REST, flights["in"][4])
        tick, _ = lax.optimization_barrier((tick, [delta[name] for name, _, _ in BIG_REST]))
        return tick

    loss, grad_x, g = _local_step(x[0], loss_target[0], full, rest_flight[4][0, 0], late_weights,
                                  lambda gd: start_exchange("rest", BIG_REST, gd), in_grads)

    finish_exchange("in", BIG_IN, g["norm_mix_w"])
    small_parts = [g[n] for n in REPLICATED] + [g[n] for n in small_names] + [loss.reshape(1)]
    g_small = _sum_blocks("sum_small", _push_blocks("gather_small_grads", _pack_flat(small_parts, 64), False))
    small_red = _unpack_flat(g_small, [wt[n].shape for n in REPLICATED] + [(r, c) for _, r, c in SMALL_SHARDED] + [(1,)])
    grads.update(zip(REPLICATED, small_red[:len(REPLICATED)]))
    for (name, r, c), red in zip(SMALL_SHARDED, small_red[len(REPLICATED):-1]):
        grads[name] = lax.dynamic_slice(red, (0, me * (c // N_DEV)), (r, c // N_DEV)).reshape(wt[name].shape)
    loss_all = small_red[-1][0]

    rest = list(REPLICATED) + small_names
    shapes = [wt[n].shape for n in rest]
    pack_rest = lambda tree: _pack_flat([tree[n] for n in rest], 24)
    d_rest, m_rest, v_rest = _adamw("adamw_small", pack_rest(wt), pack_rest(grads), pack_rest(mt), pack_rest(vt))
    delta.update(zip(rest, _unpack_flat(d_rest, shapes)))
    new_m.update(zip(rest, _unpack_flat(m_rest, shapes)))
    new_v.update(zip(rest, _unpack_flat(v_rest, shapes)))

    return (loss_all, grad_x[None], *[grads[n] for n in WEIGHTS], *[delta[n] for n in WEIGHTS],
            *[new_m[n] for n in WEIGHTS], *[new_v[n] for n in WEIGHTS])
```

```python
import functools
import math

import jax
import jax.numpy as jnp
from jax import lax
from jax.experimental import pallas as pl
from jax.experimental.pallas import tpu as pltpu

F32 = jnp.float32
BF16 = jnp.bfloat16

D_MODEL = 1024
CHUNK = 128
N_META = 16
PAD_ROWS = CHUNK - N_META
RET_HEADS = 4
RET_QK_DIM = 128
RET_V_DIM = 256
SSD_HEADS = 32
SSD_HEAD_DIM = 64
SSD_GROUPS = 4
SSD_STATE = 128
HEADS_PER_GROUP = SSD_HEADS // SSD_GROUPS
PAIRS_PER_GROUP = HEADS_PER_GROUP // 2
D_FF = 2816
EPS = 1e-6
ROPE_BASE = 10000.0
N_DEV = 8

ADAM_LR = 0.001
ADAM_B1 = 0.9
ADAM_B2 = 0.999
ADAM_EPS = 1e-08
ADAM_WD = 0.01
ADAM_STEP = 10

VMEM_LIMIT = 56 * 1024 * 1024
HALO = 16
HIGHEST = lax.Precision.HIGHEST

SEGMENTS = (("qk", 0, 1024), ("v", 1024, 2048), ("g", 2048, 3072), ("z", 3072, 5120), ("xs", 5120, 7168),
            ("B", 7168, 7680), ("C", 7680, 8192), ("dt", 8192, 8256), ("gates", 8256, 10304))


def _pick(n, cands):
    for c in cands:
        if n % c == 0:
            return c
    raise ValueError(f"no tile for {n}")


def _params(sem):
    return pltpu.CompilerParams(dimension_semantics=sem, vmem_limit_bytes=VMEM_LIMIT)


def _dot(a, b, dims=(((1,), (0,)), ((), ())), precision=None):
    return lax.dot_general(a, b, dims, preferred_element_type=F32, precision=precision)


def _dot_nt(a, b):
    return _dot(a, b, (((1,), (1,)), ((), ())))


def _dot_tn(a, b):
    return _dot(a, b, (((0,), (0,)), ((), ())))


def _mm(name, a, b, mode, add=None, out_dtype=F32):
    if mode == "nn":
        (M, K), N = a.shape, b.shape[1]
    elif mode == "nt":
        (M, K), N = a.shape, b.shape[0]
    else:
        (K, M), N = a.shape, b.shape[1]
    tn = _pick(N, (1408, 1024, 512, 128, 64))
    if mode == "tn":
        tm = M if M <= 1024 else _pick(M, (1408, 1024))
        tk = _pick(K, (2112, 512, 256, 128))
    else:
        tm = _pick(M, (1056, 512, 256, 128))
        tk = K if K <= 2816 else _pick(K, (2048, 1408, 1024))
    nk = K // tk
    if mode == "nn":
        a_spec = pl.BlockSpec((tm, tk), lambda n, m, k: (m, k))
        b_spec = pl.BlockSpec((tk, tn), lambda n, m, k: (k, n))
        dims = (((1,), (0,)), ((), ()))
    elif mode == "nt":
        a_spec = pl.BlockSpec((tm, tk), lambda n, m, k: (m, k))
        b_spec = pl.BlockSpec((tn, tk), lambda n, m, k: (n, k))
        dims = (((1,), (1,)), ((), ()))
    else:
        a_spec = pl.BlockSpec((tk, tm), lambda n, m, k: (k, m))
        b_spec = pl.BlockSpec((tk, tn), lambda n, m, k: (k, n))
        dims = (((0,), (0,)), ((), ()))
    o_spec = pl.BlockSpec((tm, tn), lambda n, m, k: (m, n))
    in_specs = [a_spec, b_spec] + ([o_spec] if add is not None else [])
    args = [a, b] + ([add] if add is not None else [])

    def body(*refs):
        if add is not None:
            a_ref, b_ref, r_ref, o_ref, acc = refs
        else:
            a_ref, b_ref, o_ref, acc = refs
        k = pl.program_id(2)
        p = _dot(a_ref[...].astype(BF16), b_ref[...].astype(BF16), dims)

        def finish(r):
            if add is not None:
                r = r + r_ref[...]
            o_ref[...] = r.astype(out_dtype)

        if nk == 1:
            finish(p)
        else:
            @pl.when(k == 0)
            def _():
                acc[...] = p

            @pl.when(k > 0)
            def _():
                acc[...] += p

            @pl.when(k == nk - 1)
            def _():
                finish(acc[...])

    return pl.pallas_call(
        body, name=name, grid=(N // tn, M // tm, nk), in_specs=in_specs, out_specs=o_spec,
        out_shape=jax.ShapeDtypeStruct((M, N), out_dtype),
        scratch_shapes=[pltpu.VMEM((tm, tn) if nk > 1 else (8, 128), F32)],
        compiler_params=_params(("arbitrary", "arbitrary", "arbitrary")),
    )(*args)


ANY_SPACE = pl.BlockSpec(memory_space=pl.ANY)


def _const(c):
    return lambda j: c


def _rows(name, fn, T, ncol, ins, params, outs, accs=(), halo=False, tall=False):
    tm = _pick(T, (1056, 512, 256, 128)) if tall else _pick(T, (384, 256, 128))
    R = T // tm
    hb = tm // HALO
    in_specs, args = [], []
    for spec in ins:
        arr, w, cf = spec[:3]
        lead = spec[3] if len(spec) > 3 else None
        if len(spec) > 4:
            rows, rf = spec[4]
            in_specs.append(pl.BlockSpec((rows, w), lambda j, i, cf=cf, rf=rf: (rf(i), cf(j))))
            args.append(arr)
            continue
        if lead is None:
            mk = lambda blk, rf, cf=cf: pl.BlockSpec(blk, lambda j, i: (rf(i), cf(j)))
            shape = lambda r, w=w: (r, w)
        else:
            mk = lambda blk, rf, cf=cf, lead=lead: pl.BlockSpec(blk, lambda j, i: (lead, rf(i), cf(j)))
            shape = lambda r, w=w: (None, r, w)
        in_specs.append(mk(shape(tm), lambda i: i))
        args.append(arr)
        if halo:
            in_specs.append(mk(shape(HALO), lambda i: jnp.maximum(i * hb - 1, 0)))
            in_specs.append(mk(shape(HALO), lambda i: jnp.minimum((i + 1) * hb, T // HALO - 1)))
            args += [arr, arr]
    for arr, w, cf in params:
        in_specs.append(pl.BlockSpec((arr.shape[0], w), lambda j, i, cf=cf: (0, cf(j))))
        args.append(arr)
    out_shape, out_specs, aliases = [], [], {}
    for k, (tw, w, cf, dt) in enumerate(outs):
        if not isinstance(tw, int):
            aliases[len(args)] = k
            in_specs.append(ANY_SPACE)
            args.append(tw)
            tw = tw.shape[1]
        out_shape.append(jax.ShapeDtypeStruct((T, tw), dt))
        out_specs.append(pl.BlockSpec((tm, w), lambda j, i, cf=cf: (i, cf(j))))
    for r, tw, w, cf in accs:
        out_shape.append(jax.ShapeDtypeStruct((r, tw), F32))
        out_specs.append(pl.BlockSpec((r, w), lambda j, i, cf=cf: (0, cf(j))))
    n_in, n_par, n_out, n_acc, n_alias = len(ins), len(params), len(outs), len(accs), len(aliases)

    def body(*refs):
        i = pl.program_id(1)
        vals, p = [], 0
        for _ in range(n_in):
            if halo:
                before = jnp.where(i > 0, refs[p + 1][...], jnp.zeros_like(refs[p + 1]))
                after = jnp.where(i < R - 1, refs[p + 2][...], jnp.zeros_like(refs[p + 2]))
                vals.append(jnp.concatenate([before, refs[p][...], after], axis=0).astype(F32))
                p += 3
            else:
                vals.append(refs[p][...].astype(F32))
                p += 1
        pvals = [refs[p + k][...] for k in range(n_par)]
        p += n_par + n_alias
        res = fn(i, *vals, *pvals)
        for k in range(n_out):
            refs[p + k][...] = res[k].astype(refs[p + k].dtype)
        p += n_out
        for k in range(n_acc):
            ref, v = refs[p + k], res[n_out + k]

            @pl.when(i == 0)
            def _(ref=ref, v=v):
                ref[...] = v

            @pl.when(i > 0)
            def _(ref=ref, v=v):
                ref[...] += v

    res = pl.pallas_call(
        body, name=name, grid=(ncol, R), in_specs=in_specs, out_specs=out_specs, out_shape=out_shape,
        input_output_aliases=aliases, compiler_params=_params(("arbitrary", "arbitrary")),
    )(*args)
    return res


def _tile_rows(T):
    return _pick(T, (384, 256, 128))


def _row_ids(i, T, halo=False):
    tm = _tile_rows(T)
    if halo:
        return i * tm - HALO + lax.broadcasted_iota(jnp.int32, (tm + 2 * HALO, 1), 0)
    return i * tm + lax.broadcasted_iota(jnp.int32, (tm, 1), 0)


def _rms(x, w):
    return x * lax.rsqrt(jnp.mean(x * x, axis=-1, keepdims=True) + EPS) * w


def _silu(x):
    return x * jax.nn.sigmoid(x)


def _conv3(x, w):
    n = x.shape[0]
    return w[0:1] * pltpu.roll(x, 1, 0) + w[1:2] * x + w[2:3] * pltpu.roll(x, n - 1, 0)


def _conv3_t(d, w):
    n = d.shape[0]
    return w[0:1] * pltpu.roll(d, n - 1, 0) + w[1:2] * d + w[2:3] * pltpu.roll(d, 1, 0)


def _center(x):
    return x[HALO:x.shape[0] - HALO]


def _retention(name, a, b, v, T, da, dv, into=None):
    (a, a0), (b, b0), (v, v0) = [t if isinstance(t, tuple) else (t, 0) for t in (a, b, v)]
    nc = T // CHUNK
    log_gammas = [math.log(1.0 - 2.0 ** (-5.0 - h)) for h in range(RET_HEADS)]

    def body(*refs):
        a_ref, b_ref, v_ref = refs[:3]
        out_ref, o_ref, st, st_b = refs[-4:]
        h = pl.program_id(0)
        lg = jnp.float32(log_gammas[RET_HEADS - 1])
        for k in range(RET_HEADS - 2, -1, -1):
            lg = jnp.where(h == k, jnp.float32(log_gammas[k]), lg)
        li = lax.broadcasted_iota(jnp.int32, (CHUNK, CHUNK), 0)
        si = lax.broadcasted_iota(jnp.int32, (CHUNK, CHUNK), 1)
        dmat = jnp.exp(lg * jnp.abs(li - si).astype(F32))
        pos = lax.broadcasted_iota(jnp.int32, (CHUNK, 1), 0).astype(F32)
        kdec_f = jnp.exp((CHUNK - 1 - pos) * lg)
        qdec_f = jnp.exp((pos + 1) * lg)
        kdec_b = jnp.exp(pos * lg)
        qdec_b = jnp.exp((CHUNK - pos) * lg)
        cdec = jnp.exp(CHUNK * lg)

        def rows(n):
            return pl.ds(pl.multiple_of(n * CHUNK, CHUNK), CHUNK)

        st[...] = jnp.zeros_like(st)
        st_b[...] = jnp.zeros_like(st_b)
        o_ref[...] = jnp.zeros_like(o_ref)

        def step(m, carry):
            r = rows(m)
            av, bv, vv = a_ref[r, :], b_ref[r, :], v_ref[r, :].astype(BF16)
            s = _dot_nt(av.astype(BF16), bv.astype(BF16)) * dmat
            o_ref[r, :] += _dot(s.astype(BF16), vv) + _dot((av * qdec_f).astype(BF16), st[...].astype(BF16))
            st[...] = cdec * st[...] + _dot_tn((bv * kdec_f).astype(BF16), vv)
            r = rows(nc - 1 - m)
            av, bv, vv = a_ref[r, :], b_ref[r, :], v_ref[r, :].astype(BF16)
            o_ref[r, :] += _dot((av * qdec_b).astype(BF16), st_b[...].astype(BF16))
            st_b[...] = cdec * st_b[...] + _dot_tn((bv * kdec_b).astype(BF16), vv)
            return carry

        lax.fori_loop(0, nc, step, 0, unroll=11 if nc % 11 == 0 else 1)
        out_ref[...] = o_ref[...].astype(out_ref.dtype)

    in_specs = [pl.BlockSpec((T, da), lambda h: (0, a0 // da + h)), pl.BlockSpec((T, da), lambda h: (0, b0 // da + h)),
                pl.BlockSpec((T, dv), lambda h: (0, v0 // dv + h))]
    if into is None:
        args, o0, aliases = (a, b, v), 0, {}
        out_shape = jax.ShapeDtypeStruct((T, RET_HEADS * dv), F32)
    else:
        args, o0, aliases = (a, b, v, into[0]), into[1], {3: 0}
        in_specs.append(ANY_SPACE)
        out_shape = jax.ShapeDtypeStruct(into[0].shape, into[0].dtype)
    return pl.pallas_call(
        body, name=name, grid=(RET_HEADS,), in_specs=in_specs,
        out_specs=pl.BlockSpec((T, dv), lambda h: (0, o0 // dv + h)), out_shape=out_shape,
        input_output_aliases=aliases,
        scratch_shapes=[pltpu.VMEM((T, dv), F32), pltpu.VMEM((da, dv), F32), pltpu.VMEM((da, dv), F32)],
        compiler_params=_params(("arbitrary",)),
    )(*args)


def _softplus(x):
    return jnp.maximum(x, 0.0) + jnp.log1p(jnp.exp(-jnp.abs(x)))


def _lane_lo():
    return lax.broadcasted_iota(jnp.int32, (1, CHUNK), 1) < SSD_HEAD_DIM


def _pair_cols(col, j):
    return jnp.where(_lane_lo(), col[:, 2 * j:2 * j + 1], col[:, 2 * j + 1:2 * j + 2])


def _pair_rows(colr, j):
    lo = lax.broadcasted_iota(jnp.int32, (CHUNK, 1), 0) < SSD_HEAD_DIM
    return jnp.where(lo, colr[2 * j:2 * j + 1, :], colr[2 * j + 1:2 * j + 2, :])


def _onehot8(h):
    return (lax.broadcasted_iota(jnp.int32, (1, HEADS_PER_GROUP), 1) == h).astype(F32)


def _ssd_pre(d, c, rawc, rawr, bc, br, alc, alr):
    li = lax.broadcasted_iota(jnp.int32, (CHUNK, CHUNK), 0)
    si = lax.broadcasted_iota(jnp.int32, (CHUNK, CHUNK), 1)
    dif = li - si if d == 0 else si - li
    mask = dif >= 0
    mask_t = dif <= 0
    rowc = c * CHUNK + lax.broadcasted_iota(jnp.int32, (CHUNK, 1), 0)
    rowr = c * CHUNK + lax.broadcasted_iota(jnp.int32, (1, CHUNK), 1)
    dtc = jnp.where(rowc >= PAD_ROWS, _softplus(rawc + bc), 0.0)
    dtr = jnp.where(rowr >= PAD_ROWS, _softplus(rawr + br), 0.0)
    ac = -jnp.exp(alc)
    ar = -jnp.exp(alr)
    dlc = dtc * ac
    dlr = dtr * ar
    alpc = _dot(mask.astype(F32), dlc, precision=HIGHEST)
    alpr = _dot(dlr, mask_t.astype(F32), precision=HIGHEST)
    endc = jnp.sum(dlc, axis=0, keepdims=True)
    endr = jnp.sum(dlr, axis=1, keepdims=True)
    return dict(mask=mask, mask_t=mask_t, dtc=dtc, ac=ac, alpc=alpc, alpr=alpr, endc=endc, endr=endr,
                valid=rowc >= PAD_ROWS)


def _chunk_of(d, n, nc):
    return n + d * (nc - 1 - 2 * n)


GROUP_WIDTH = HEADS_PER_GROUP * SSD_HEAD_DIM


def _chunks_per_step(nc):
    return 3 if nc % 3 == 0 else 1


def _ssd_in_specs(d, cfn, rows):
    return [
        pl.BlockSpec((rows, GROUP_WIDTH), lambda g, n: (cfn(d, n), g)),
        pl.BlockSpec((rows, SSD_STATE), lambda g, n: (cfn(d, n), g)),
        pl.BlockSpec((rows, SSD_STATE), lambda g, n: (cfn(d, n), g)),
        pl.BlockSpec((None, None, rows, HEADS_PER_GROUP), lambda g, n: (d, g, cfn(d, n), 0)),
        pl.BlockSpec((None, None, HEADS_PER_GROUP, rows), lambda g, n: (d, g, 0, cfn(d, n))),
        pl.BlockSpec((None, None, 1, HEADS_PER_GROUP), lambda g, n: (d, g, 0, 0)),
        pl.BlockSpec((None, None, HEADS_PER_GROUP, 1), lambda g, n: (d, g, 0, 0)),
        pl.BlockSpec((None, None, 1, HEADS_PER_GROUP), lambda g, n: (d, g, 0, 0)),
        pl.BlockSpec((None, None, HEADS_PER_GROUP, 1), lambda g, n: (d, g, 0, 0)),
    ]


N_SSD_IN = 9


def _ssd_fwd(xs, bm, cm, small, T):
    nc = T // CHUNK
    cps = _chunks_per_step(nc)
    rows = cps * CHUNK
    cfn = lambda d, n: _chunk_of(d, n, nc // cps)

    def one_direction(d, n, ins, y_ref, hs_ref, h_scr):
        x_ref, b_ref, c_ref, rawc_ref, rawr_ref, *per_group = ins
        for kk in range(cps):
            k = kk if d == 0 else cps - 1 - kk
            r = pl.ds(k * CHUNK, CHUNK)
            one_chunk(d, cfn(d, n) * cps + k,
                      (x_ref.at[r], b_ref.at[r], c_ref.at[r], rawc_ref.at[r], rawr_ref.at[:, r], *per_group),
                      y_ref.at[r], hs_ref.at[k], h_scr)

    def one_chunk(d, c, ins, y_ref, hs_ref, h_scr):
        x_ref, b_ref, c_ref, rawc_ref, rawr_ref, bc_ref, br_ref, alc_ref, alr_ref = ins
        q = _ssd_pre(d, c, rawc_ref[...], rawr_ref[...], bc_ref[...], br_ref[...], alc_ref[...], alr_ref[...])
        bv = b_ref[...].astype(BF16)
        cv = c_ref[...].astype(BF16)
        cb = _dot_nt(cv, bv)
        lo = _lane_lo()
        for j in range(PAIRS_PER_GROUP):
            xp = x_ref[:, j * CHUNK:(j + 1) * CHUNK]
            xd = xp * _pair_cols(q["dtc"], j)
            xdb = xd.astype(BF16)
            yi = []
            for e in range(2):
                h = 2 * j + e
                lm = jnp.exp(jnp.where(q["mask"], q["alpc"][:, h:h + 1] - q["alpr"][h:h + 1, :], -jnp.inf))
                yi.append(_dot((cb * lm).astype(BF16), xdb))
            alp = _pair_cols(q["alpc"], j)
            hp = h_scr[j]
            hs_ref[j] = hp
            yo = jnp.exp(alp) * _dot_nt(cv, hp.astype(BF16))
            y_ref[:, j * CHUNK:(j + 1) * CHUNK] = (jnp.where(lo, yi[0], yi[1]) + yo).astype(y_ref.dtype)
            de = jnp.exp(_pair_cols(q["endc"], j) - alp)
            h_scr[j] = jnp.exp(_pair_rows(q["endr"], j)) * hp + _dot_tn((xd * de).astype(BF16), bv)

    def body(*refs):
        n = pl.program_id(1)
        ins, (y_f, y_b, hs_f, hs_b, h_scr) = refs[:2 * N_SSD_IN], refs[2 * N_SSD_IN:]

        @pl.when(n == 0)
        def _():
            h_scr[...] = jnp.zeros_like(h_scr)

        one_direction(0, n, ins[:N_SSD_IN], y_f, hs_f, h_scr.at[0])
        one_direction(1, n, ins[N_SSD_IN:], y_b, hs_b, h_scr.at[1])

    y_spec = lambda d: pl.BlockSpec((rows, GROUP_WIDTH), lambda g, n: (cfn(d, n), g))
    hs_spec = lambda d: pl.BlockSpec((None, cps, PAIRS_PER_GROUP, CHUNK, SSD_STATE),
                                     lambda g, n: (g, cfn(d, n), 0, 0, 0))
    y_shape = jax.ShapeDtypeStruct((T, SSD_HEADS * SSD_HEAD_DIM), BF16)
    hs_shape = jax.ShapeDtypeStruct((SSD_GROUPS, nc, PAIRS_PER_GROUP, CHUNK, SSD_STATE), F32)
    y_f, y_b, hs_f, hs_b = pl.pallas_call(
        body, name="ssd_fwd", grid=(SSD_GROUPS, nc // cps),
        in_specs=_ssd_in_specs(0, cfn, rows) + _ssd_in_specs(1, cfn, rows),
        out_specs=[y_spec(0), y_spec(1), hs_spec(0), hs_spec(1)],
        out_shape=[y_shape, y_shape, hs_shape, hs_shape],
        scratch_shapes=[pltpu.VMEM((2, PAIRS_PER_GROUP, CHUNK, SSD_STATE), F32)],
        compiler_params=_params(("arbitrary", "arbitrary")),
    )(xs, bm, cm, *small, xs, bm, cm, *small)
    return (y_f, y_b), (hs_f, hs_b)


def _ssd_bwd(xs, bm, cm, small, hs, dy, T):
    nc = T // CHUNK
    cps = _chunks_per_step(nc)
    rows = cps * CHUNK
    cfn = lambda d, n: _chunk_of(1 - d, n, nc // cps)

    def one_direction(d, n, ins, outs, dh_scr):
        x_ref, b_ref, c_ref, rawc_ref, rawr_ref, bc_ref, br_ref, alc_ref, alr_ref, hs_ref, dy_ref = ins
        dx_ref, db_ref, dc_ref, draw_ref, dbias_ref, dalog_ref = outs
        for kk in range(cps):
            k = cps - 1 - kk if d == 0 else kk
            r = pl.ds(k * CHUNK, CHUNK)
            one_chunk(d, cfn(d, n) * cps + k, n if kk == 0 else None,
                      (x_ref.at[r], b_ref.at[r], c_ref.at[r], rawc_ref.at[r], rawr_ref.at[:, r], bc_ref, br_ref,
                       alc_ref, alr_ref, hs_ref.at[k], dy_ref.at[r]),
                      (dx_ref.at[r], db_ref.at[r], dc_ref.at[r], draw_ref.at[r], dbias_ref, dalog_ref), dh_scr)

    def one_chunk(d, c, first_of_step, ins, outs, dh_scr):
        x_ref, b_ref, c_ref, rawc_ref, rawr_ref, bc_ref, br_ref, alc_ref, alr_ref, hs_ref, dy_ref = ins
        dx_ref, db_ref, dc_ref, draw_ref, dbias_ref, dalog_ref = outs
        rawc, bc = rawc_ref[...], bc_ref[...]
        q = _ssd_pre(d, c, rawc, rawr_ref[...], bc, br_ref[...], alc_ref[...], alr_ref[...])
        b32, c32 = b_ref[...], c_ref[...]
        bv, cv = b32.astype(BF16), c32.astype(BF16)
        cb = _dot_nt(cv, bv)
        cbt = _dot_nt(bv, cv)
        lo = _lane_lo()
        row_lo = lax.broadcasted_iota(jnp.int32, (CHUNK, 1), 0) < SSD_HEAD_DIM
        dcb = jnp.zeros((CHUNK, CHUNK), F32)
        dcp = jnp.zeros((CHUNK, SSD_STATE), F32)
        dbp = jnp.zeros((CHUNK, SSD_STATE), F32)
        dalp = jnp.zeros((CHUNK, HEADS_PER_GROUP), F32)
        dend = jnp.zeros((1, HEADS_PER_GROUP), F32)
        ddtx = jnp.zeros((CHUNK, HEADS_PER_GROUP), F32)

        def half_sums(t):
            return (jnp.sum(jnp.where(lo, t, 0.0), axis=1, keepdims=True),
                    jnp.sum(jnp.where(lo, 0.0, t), axis=1, keepdims=True))

        for j in range(PAIRS_PER_GROUP):
            xp = x_ref[:, j * CHUNK:(j + 1) * CHUNK]
            dtp = _pair_cols(q["dtc"], j)
            xd = xp * dtp
            xdb = xd.astype(BF16)
            dyp = dy_ref[:, j * CHUNK:(j + 1) * CHUNK]
            dyb = dyp.astype(BF16)
            hn = hs_ref[j]
            hnb = hn.astype(BF16)
            dh1 = dh_scr[j]
            dh1b = dh1.astype(BF16)
            alp = _pair_cols(q["alpc"], j)
            ea = jnp.exp(alp)
            de = jnp.exp(_pair_cols(q["endc"], j) - alp)
            dxi = []
            for e in range(2):
                h = 2 * j + e
                diff = q["alpc"][:, h:h + 1] - q["alpr"][h:h + 1, :]
                lm = jnp.exp(jnp.where(q["mask"], diff, -jnp.inf))
                mt = cbt * jnp.exp(jnp.where(q["mask_t"], -diff, -jnp.inf))
                dxi.append(_dot(mt.astype(BF16), dyb))
                dyeb_h = (jnp.where(lo, dyp, 0.0) if e == 0 else jnp.where(lo, 0.0, dyp)).astype(BF16)
                gl = _dot_nt(dyeb_h, xdb) * lm
                dcb = dcb + gl
                ra = jnp.sum(gl * cb - _dot_nt(xdb, dyeb_h) * mt, axis=1, keepdims=True)
                dalp = dalp + ra * _onehot8(h)
            y_off = ea * _dot_nt(cv, hnb)
            dxs_state = de * _dot_nt(bv, dh1b)
            dxd = jnp.where(lo, dxi[0], dxi[1]) + dxs_state
            dyeb = (dyp * ea).astype(BF16)
            dcp = dcp + _dot(dyeb, hnb)
            dbp = dbp + _dot((xd * de).astype(BF16), dh1b)
            dh_scr[j] = jnp.exp(_pair_rows(q["endr"], j)) * dh1 + _dot_tn(dyeb, cv)
            r0, r1 = half_sums(dyp * y_off - xd * dxs_state)
            dalp = dalp + r0 * _onehot8(2 * j) + r1 * _onehot8(2 * j + 1)
            t0, t1 = half_sums(jnp.sum(xd * dxs_state, axis=0, keepdims=True))
            u = dh1 * hn
            u0 = jnp.sum(jnp.sum(jnp.where(row_lo, u, 0.0), axis=0, keepdims=True), axis=1, keepdims=True)
            u1 = jnp.sum(jnp.sum(jnp.where(row_lo, 0.0, u), axis=0, keepdims=True), axis=1, keepdims=True)
            eend = jnp.exp(q["endc"])
            dend = dend + (t0 + eend * u0) * _onehot8(2 * j) + (t1 + eend * u1) * _onehot8(2 * j + 1)
            dx_ref[:, j * CHUNK:(j + 1) * CHUNK] = (dxd * dtp).astype(dx_ref.dtype)
            w0, w1 = half_sums(dxd * xp)
            ddtx = ddtx + w0 * _onehot8(2 * j) + w1 * _onehot8(2 * j + 1)

        dcbb = dcb.astype(BF16)
        dc_ref[...] = (dcp + _dot(dcbb, bv)).astype(dc_ref.dtype)
        db_ref[...] = (dbp + _dot_tn(dcbb, cv)).astype(db_ref.dtype)
        ddl = _dot(q["mask_t"].astype(F32), dalp, precision=HIGHEST) + dend
        ddt = ddl * q["ac"] + ddtx
        draw = jnp.where(q["valid"], ddt * jax.nn.sigmoid(rawc + bc), 0.0)
        draw_ref[...] = draw
        dbias = jnp.sum(draw, axis=0, keepdims=True)
        dalog = jnp.sum(ddl * q["dtc"], axis=0, keepdims=True) * q["ac"]

        def add():
            dbias_ref[...] += dbias
            dalog_ref[...] += dalog

        if first_of_step is None:
            add()
        else:
            @pl.when(first_of_step == 0)
            def _():
                dbias_ref[...] = dbias
                dalog_ref[...] = dalog

            pl.when(first_of_step > 0)(add)

    n_in, n_out = N_SSD_IN + 2, 6

    def body(*refs):
        n = pl.program_id(1)
        ins, outs, dh_scr = refs[:2 * n_in], refs[2 * n_in:2 * (n_in + n_out)], refs[-1]

        @pl.when(n == 0)
        def _():
            dh_scr[...] = jnp.zeros_like(dh_scr)

        one_direction(0, n, ins[:n_in], outs[:n_out], dh_scr.at[0])
        one_direction(1, n, ins[n_in:], outs[n_out:], dh_scr.at[1])

    def in_specs(d):
        return _ssd_in_specs(d, cfn, rows) + [
            pl.BlockSpec((None, cps, PAIRS_PER_GROUP, CHUNK, SSD_STATE), lambda g, n: (g, cfn(d, n), 0, 0, 0)),
            pl.BlockSpec((rows, GROUP_WIDTH), lambda g, n: (cfn(d, n), g))]

    def out_specs(d):
        acc = pl.BlockSpec((None, 1, HEADS_PER_GROUP), lambda g, n: (g, 0, 0))
        return [pl.BlockSpec((rows, GROUP_WIDTH), lambda g, n: (cfn(d, n), g)),
                pl.BlockSpec((rows, SSD_STATE), lambda g, n: (cfn(d, n), g)),
                pl.BlockSpec((rows, SSD_STATE), lambda g, n: (cfn(d, n), g)),
                pl.BlockSpec((None, rows, HEADS_PER_GROUP), lambda g, n: (g, cfn(d, n), 0)), acc, acc]

    out_shape = [jax.ShapeDtypeStruct((T, SSD_HEADS * SSD_HEAD_DIM), BF16),
                 jax.ShapeDtypeStruct((T, SSD_GROUPS * SSD_STATE), BF16),
                 jax.ShapeDtypeStruct((T, SSD_GROUPS * SSD_STATE), BF16),
                 jax.ShapeDtypeStruct((SSD_GROUPS, T, HEADS_PER_GROUP), F32),
                 jax.ShapeDtypeStruct((SSD_GROUPS, 1, HEADS_PER_GROUP), F32),
                 jax.ShapeDtypeStruct((SSD_GROUPS, 1, HEADS_PER_GROUP), F32)]
    res = pl.pallas_call(
        body, name="ssd_bwd", grid=(SSD_GROUPS, nc // cps),
        in_specs=in_specs(0) + in_specs(1), out_specs=out_specs(0) + out_specs(1), out_shape=out_shape * 2,
        scratch_shapes=[pltpu.VMEM((2, PAIRS_PER_GROUP, CHUNK, SSD_STATE), F32)],
        compiler_params=_params(("arbitrary", "arbitrary")),
    )(xs, bm, cm, *small, hs[0], dy, xs, bm, cm, *small, hs[1], dy)
    return [(res[k], res[n_out + k]) for k in range(n_out)]


def _rot(x, cs, sn):
    return x * cs + pltpu.roll(x, RET_QK_DIM // 2, 1) * sn


def _rot_t(d, cs, sn):
    return d * cs + pltpu.roll(d * sn, RET_QK_DIM // 2, 1)


def _ret_post(y, g, w):
    parts = []
    for h in range(RET_HEADS):
        yh = y[:, h * RET_V_DIM:(h + 1) * RET_V_DIM]
        mu = jnp.mean(yh, axis=-1, keepdims=True)
        var = jnp.mean(jnp.square(yh - mu), axis=-1, keepdims=True)
        parts.append((yh - mu) * lax.rsqrt(var + EPS))
    return _silu(g) * (jnp.concatenate(parts, axis=1) * w)


def _ssd_post(yf, yb, xs, z, dskip, w):
    y = (yf + yb + xs * dskip) * _silu(z)
    return y * lax.rsqrt(jnp.mean(y * y, axis=-1, keepdims=True) + EPS) * w


def _merge(gates, yr, ys, valid):
    m = jax.nn.sigmoid(gates[:, :D_MODEL]) * yr + jax.nn.sigmoid(gates[:, D_MODEL:]) * ys
    return jnp.where(valid, m, 0.0)


def _rope_tables(T):
    half = RET_QK_DIM // 2
    inv = ROPE_BASE ** (-jnp.arange(half, dtype=F32) / half)
    pos = (jnp.arange(T) - PAD_ROWS).astype(F32)
    ang = pos[:, None] * inv[None, :]
    cos, sin = jnp.cos(ang), jnp.sin(ang)
    return jnp.concatenate([cos, cos], axis=1), jnp.concatenate([-sin, sin], axis=1)


def _per_group(v):
    c = v.reshape(SSD_GROUPS, 1, HEADS_PER_GROUP)
    return c, c.reshape(SSD_GROUPS, HEADS_PER_GROUP, 1)


def _local_step(x, target, w, tick, late_weights, early_grads, in_grads):
    S = x.shape[0]
    T = S + CHUNK
    tm = _tile_rows(T)
    c0 = _const(0)

    h0 = jnp.concatenate([jnp.zeros((PAD_ROWS, D_MODEL), F32), w["meta_tokens"], x], axis=0)
    seg_at = {name: a for name, a, _ in SEGMENTS}
    w_main = w["w_in_t"][:seg_at["dt"]]
    w_dt = jnp.pad(w["w_in_t"][seg_at["dt"]:seg_at["gates"]], ((0, CHUNK - 2 * SSD_HEADS), (0, 0)))
    w_gates = w["w_in_t"][seg_at["gates"]:]

    def norm_cast(name, h, nw):
        return _rows(name, lambda i, hv, wv: (_rms(hv, wv),), T, 1, [(h, D_MODEL, c0)], [(nw, D_MODEL, c0)],
                     [(D_MODEL, D_MODEL, c0, BF16)], tall=True)[0]

    u = norm_cast("norm_mix", h0, w["norm_mix_w"] + tick)
    p_main = _mm("proj_main", u, w_main, "nt", out_dtype=BF16)
    p_dt = _mm("proj_dt", u, w_dt, "nt")
    p_gates = _mm("proj_gates", u, w_gates, "nt", out_dtype=BF16)

    def seg(name, width, cf=c0):
        base = seg_at[name] // width
        return (p_main, width, lambda j: base + cf(j))

    cs, sn = _rope_tables(T)
    scale = RET_QK_DIM ** -0.5

    def rot_fn(i, qk, csv, snv):
        q = [_rot(qk[:, h * 128:(h + 1) * 128], csv, snv) for h in range(RET_HEADS)]
        k = [_rot(qk[:, (RET_HEADS + h) * 128:(RET_HEADS + h + 1) * 128], csv, snv) * scale for h in range(RET_HEADS)]
        return jnp.concatenate(q, axis=1), jnp.concatenate(k, axis=1)

    qr, kr = _rows("rotary", rot_fn, T, 1, [seg("qk", 1024), (cs, 128, c0), (sn, 128, c0)], [],
                   [(512, 512, c0, F32), (512, 512, c0, F32)], tall=True)
    v_at = (p_main, seg_at["v"])
    y_ret = _retention("retention", qr, kr, v_at, T, RET_QK_DIM, RET_V_DIM)
    a_ret = _rows("ret_post", lambda i, y, g, gw: (_ret_post(y, g, gw),), T, 1,
                  [(y_ret, 1024, c0), seg("g", 1024)], [(w["ret_gn_w"], 1024, c0)],
                  [(1024, 1024, c0, BF16)], tall=True)[0]

    conv_w = {"xs": w["w_ssd_conv"][:, :2048], "B": w["w_ssd_conv"][:, 2048:2560], "C": w["w_ssd_conv"][:, 2560:]}
    conv_b = {"xs": w["b_ssd_conv"][:, :2048], "B": w["b_ssd_conv"][:, 2048:2560], "C": w["b_ssd_conv"][:, 2560:]}

    def ssd_conv_fn(i, xe, cw, cb):
        r = _row_ids(i, T, True)
        return (_center(jnp.where(r >= PAD_ROWS, _silu(_conv3(xe, cw) + cb), 0.0)),)

    act = {}
    for name in ("xs", "B", "C"):
        wd = conv_w[name].shape[1]
        cw = 512
        act[name] = _rows("ssd_conv_" + name, ssd_conv_fn, T, wd // cw, [seg(name, cw, lambda j: j)],
                          [(conv_w[name], cw, lambda j: j), (conv_b[name], cw, lambda j: j)],
                          [(wd, cw, lambda j: j, BF16)], halo=True)[0]

    raw = p_dt[:, :2 * SSD_HEADS].reshape(T, 2, SSD_GROUPS, HEADS_PER_GROUP)
    rawc = raw.transpose(1, 2, 0, 3)
    rawr = raw.transpose(1, 2, 3, 0)
    bias = [_per_group(w["dt_bias_f"]), _per_group(w["dt_bias_b"])]
    alog = [_per_group(w["a_log_f"]), _per_group(w["a_log_b"])]
    small = (rawc, rawr, jnp.stack([bias[0][0], bias[1][0]]), jnp.stack([bias[0][1], bias[1][1]]),
             jnp.stack([alog[0][0], alog[1][0]]), jnp.stack([alog[0][1], alog[1][1]]))
    y_dir, states = _ssd_fwd(act["xs"], act["B"], act["C"], small, T)

    dskip_e = jnp.repeat(w["d_skip"], SSD_HEAD_DIM, axis=1)
    gcol = lambda j: j
    gw_ = 512
    a_ssd = _rows("ssd_post", lambda i, yf, yb, xv, zv, dk, nw: (_ssd_post(yf, yb, xv, zv, dk, nw),), T, SSD_GROUPS,
                  [(y_dir[0], gw_, gcol), (y_dir[1], gw_, gcol), (act["xs"], gw_, gcol), seg("z", gw_, gcol)],
                  [(dskip_e, gw_, gcol), (w["ssd_norm_w"], gw_, gcol)], [(2048, gw_, gcol, BF16)])[0]

    w = dict(w, **late_weights(a_ssd))
    w_up_g, w_up_u = w["w_ffn_up_t"][:D_FF], w["w_ffn_up_t"][D_FF:]
    y_ret_o = _mm("ret_out", a_ret, w["w_ret_out"], "nn", out_dtype=BF16)
    y_ssd_o = _mm("ssd_out", a_ssd, w["w_ssd_out"], "nn", out_dtype=BF16)

    def merge_fn(i, gates, yr, ys):
        return (_merge(gates, yr, ys, _row_ids(i, T) >= PAD_ROWS),)

    merged = _rows("merge", merge_fn, T, 1, [(p_gates, 2048, c0), (y_ret_o, 1024, c0), (y_ssd_o, 1024, c0)], [],
                   [(1024, 1024, c0, BF16)])[0]
    h1 = _mm("mix_out", merged, w["w_out"], "nn", add=h0)

    n2 = norm_cast("norm_ffn", h1, w["norm_ffn_w"])
    f_pre = _mm("ffn_up", n2, w["w_ffn_up_t"], "nt", out_dtype=BF16)
    cwg, cwu = w["w_ffn_conv"][:, :D_FF], w["w_ffn_conv"][:, D_FF:]
    cbg, cbu = w["b_ffn_conv"][:, :D_FF], w["b_ffn_conv"][:, D_FF:]
    fcol = lambda j: j
    fw = 1408

    def ffn_act_fn(i, ge, ue, wg, wu, bg, bu):
        return (_center(_silu(_conv3(ge, wg) + bg) * (_conv3(ue, wu) + bu)),)

    ucol = lambda j: D_FF // fw + j
    a2 = _rows("ffn_act", ffn_act_fn, T, D_FF // fw, [(f_pre, fw, fcol), (f_pre, fw, ucol)],
               [(cwg, fw, fcol), (cwu, fw, fcol), (cbg, fw, fcol), (cbu, fw, fcol)], [(D_FF, fw, fcol, BF16)],
               halo=True)[0]
    h2 = _mm("ffn_down", a2, w["w_ffn_down"], "nn", add=h1)

    fnw = w["final_norm_w"].reshape(1, D_MODEL)

    per_tile = tm // CHUNK
    tgt_specs = [(target, D_MODEL, c0, None, (CHUNK, lambda i, k=k: jnp.maximum(per_tile * i - 1 + k, 0)))
                 for k in range(per_tile)]

    def loss_fn(i, hv, *rest):
        tv, nw = jnp.concatenate(rest[:per_tile], axis=0), rest[per_tile]
        valid = _row_ids(i, T) >= CHUNK
        y, vjp = jax.vjp(_rms, hv, nw)
        diff = jnp.where(valid, y - tv, 0.0)
        dh, dw = vjp(diff * (1.0 / D_MODEL))
        part = 0.5 / D_MODEL * jnp.sum(jnp.sum(diff * diff, axis=1, keepdims=True), axis=0, keepdims=True)
        return dh, jnp.broadcast_to(part, (1, 128)), dw

    dh2, loss_acc, d_fnw = _rows("loss", loss_fn, T, 1, [(h2, D_MODEL, c0)] + tgt_specs, [(fnw, D_MODEL, c0)],
                                 [(D_MODEL, D_MODEL, c0, F32)], [(1, 128, 128, c0), (1, D_MODEL, D_MODEL, c0)])
    loss = loss_acc[0, 0]
    grads = {"final_norm_w": d_fnw.reshape(D_MODEL)}

    da2 = _mm("d_ffn_act", dh2, w["w_ffn_down"], "nt", out_dtype=BF16)
    grads["w_ffn_down"] = _mm("g_ffn_down", a2, dh2, "tn", out_dtype=BF16)

    def ffn_bwd_fn(i, ge, ue, de, wg, wu, bg, bu):
        fg = _conv3(ge, wg) + bg
        fu = _conv3(ue, wu) + bu
        sg = jax.nn.sigmoid(fg)
        dfg = de * fu * (sg * (1.0 + fg * (1.0 - sg)))
        dfu = de * (fg * sg)
        n = ge.shape[0]

        def wgrad(df, xe):
            df_c = _center(df)
            return jnp.concatenate([jnp.sum(df_c * _center(pltpu.roll(xe, 1, 0)), axis=0, keepdims=True),
                                    jnp.sum(df_c * _center(xe), axis=0, keepdims=True),
                                    jnp.sum(df_c * _center(pltpu.roll(xe, n - 1, 0)), axis=0, keepdims=True)], axis=0)

        return (_center(_conv3_t(dfg, wg)), _center(_conv3_t(dfu, wu)), wgrad(dfg, ge), wgrad(dfu, ue),
                jnp.sum(_center(dfg), axis=0, keepdims=True), jnp.sum(_center(dfu), axis=0, keepdims=True))

    dfg_pre, dfu_pre, g_cwg, g_cwu, g_cbg, g_cbu = _rows(
        "ffn_act_bwd", ffn_bwd_fn, T, D_FF // fw, [(f_pre, fw, fcol), (f_pre, fw, ucol), (da2, fw, fcol)],
        [(cwg, fw, fcol), (cwu, fw, fcol), (cbg, fw, fcol), (cbu, fw, fcol)],
        [(D_FF, fw, fcol, BF16), (D_FF, fw, fcol, BF16)],
        [(3, D_FF, fw, fcol), (3, D_FF, fw, fcol), (1, D_FF, fw, fcol), (1, D_FF, fw, fcol)], halo=True)
    grads["w_ffn_conv"] = jnp.concatenate([g_cwg, g_cwu], axis=1)
    grads["b_ffn_conv"] = jnp.concatenate([g_cbg, g_cbu], axis=1)
    dn2 = _mm("d_norm_ffn_g", dfg_pre, w_up_g, "nn")
    dn2 = _mm("d_norm_ffn_u", dfu_pre, w_up_u, "nn", add=dn2)
    grads["w_ffn_up_t"] = jnp.concatenate([_mm("g_ffn_up_g", dfg_pre, n2, "tn", out_dtype=BF16), _mm("g_ffn_up_u", dfu_pre, n2, "tn", out_dtype=BF16)],
                                          axis=0)

    def norm_bwd(name, h, nw, dn, dres):
        def fn(i, hv, dnv, drv, wv):
            _, vjp = jax.vjp(_rms, hv, wv)
            dh, dw = vjp(dnv)
            return dh + drv, dw
        return _rows(name, fn, T, 1, [(h, D_MODEL, c0), (dn, D_MODEL, c0), (dres, D_MODEL, c0)], [(nw, D_MODEL, c0)],
                     [(D_MODEL, D_MODEL, c0, F32)], [(1, D_MODEL, D_MODEL, c0)])

    dh1, grads["norm_ffn_w"] = norm_bwd("norm_ffn_bwd", h1, w["norm_ffn_w"], dn2, dh2)

    dmerged = _mm("d_merged", dh1, w["w_out"], "nt", out_dtype=BF16)
    grads["w_out"] = _mm("g_out", merged, dh1, "tn", out_dtype=BF16)

    def merge_bwd_fn(i, gates, yr, ys, dm):
        valid = _row_ids(i, T) >= PAD_ROWS
        _, vjp = jax.vjp(lambda a, b, c: _merge(a, b, c, valid), gates, yr, ys)
        return vjp(dm)

    dgates, dyr, dys = _rows("merge_bwd", merge_bwd_fn, T, 1,
                             [(p_gates, 2048, c0), (y_ret_o, 1024, c0), (y_ssd_o, 1024, c0), (dmerged, 1024, c0)],
                             [], [(2048, 2048, c0, BF16), (1024, 1024, c0, BF16), (1024, 1024, c0, BF16)])
    dproj = {"gates": dgates}

    da_ssd = _mm("d_ssd_act", dys, w["w_ssd_out"], "nt", out_dtype=BF16)
    grads["w_ssd_out"] = _mm("g_ssd_out", a_ssd, dys, "tn", out_dtype=BF16)

    def ssd_post_bwd_fn(i, yf, yb, xv, zv, da, dk, nw):
        _, vjp = jax.vjp(_ssd_post, yf, yb, xv, zv, dk, nw)
        dyf, _, dxv, dzv, ddk, dnw = vjp(da)
        return dyf, dxv, dzv, ddk, dnw

    d_main = lax.empty(p_main.shape, BF16)

    def into_main(name, width, cf=c0):
        base = seg_at[name] // width
        return (d_main, width, lambda j: base + cf(j), BF16)

    dy_ssd, dxs_skip, d_main, g_dskip_e, grads["ssd_norm_w"] = _rows(
        "ssd_post_bwd", ssd_post_bwd_fn, T, SSD_GROUPS,
        [(y_dir[0], gw_, gcol), (y_dir[1], gw_, gcol), (act["xs"], gw_, gcol), seg("z", gw_, gcol),
         (da_ssd, gw_, gcol)],
        [(dskip_e, gw_, gcol), (w["ssd_norm_w"], gw_, gcol)],
        [(2048, gw_, gcol, BF16), (2048, gw_, gcol, BF16), into_main("z", gw_, gcol)],
        [(1, 2048, gw_, gcol), (1, 2048, gw_, gcol)])
    grads["d_skip"] = g_dskip_e.reshape(SSD_HEADS, SSD_HEAD_DIM).sum(axis=1).reshape(1, SSD_HEADS)

    dxs_dir, db_dir, dc_dir, draw, g_bias, g_alog = _ssd_bwd(act["xs"], act["B"], act["C"], small, states, dy_ssd, T)
    grads["dt_bias_f"], grads["dt_bias_b"] = g_bias[0].reshape(1, SSD_HEADS), g_bias[1].reshape(1, SSD_HEADS)
    grads["a_log_f"], grads["a_log_b"] = g_alog[0].reshape(1, SSD_HEADS), g_alog[1].reshape(1, SSD_HEADS)
    d_dt = jnp.stack(draw).transpose(2, 0, 1, 3).reshape(T, 2 * SSD_HEADS)
    dproj["dt"] = jnp.pad(d_dt, ((0, 0), (0, CHUNK - 2 * SSD_HEADS))).astype(BF16)

    def make_conv_bwd(nsum):
        def fn(i, xe, *rest):
            ds, (cw, cb) = rest[:nsum], rest[nsum:]
            r = _row_ids(i, T, True)
            dact = ds[0]
            for t in ds[1:]:
                dact = dact + t
            dact = jnp.where(r >= PAD_ROWS, dact, 0.0)
            pre = _conv3(xe, cw) + cb
            sg = jax.nn.sigmoid(pre)
            dpre = dact * (sg * (1.0 + pre * (1.0 - sg)))
            n = xe.shape[0]
            dpc = _center(dpre)
            dw = jnp.concatenate([jnp.sum(dpc * _center(pltpu.roll(xe, 1, 0)), axis=0, keepdims=True),
                                  jnp.sum(dpc * _center(xe), axis=0, keepdims=True),
                                  jnp.sum(dpc * _center(pltpu.roll(xe, n - 1, 0)), axis=0, keepdims=True)], axis=0)
            return _center(_conv3_t(dpre, cw)), dw, jnp.sum(dpc, axis=0, keepdims=True)
        return fn

    g_cw, g_cb = {}, {}
    cots = {"xs": [(dxs_dir[0], 512, gcol), (dxs_dir[1], 512, gcol), (dxs_skip, 512, gcol)],
            "B": [(db_dir[0], 512, gcol), (db_dir[1], 512, gcol)],
            "C": [(dc_dir[0], 512, gcol), (dc_dir[1], 512, gcol)]}
    for name in ("xs", "B", "C"):
        wd = conv_w[name].shape[1]
        d_main, g_cw[name], g_cb[name] = _rows(
            "ssd_conv_bwd_" + name, make_conv_bwd(len(cots[name])), T, wd // 512,
            [seg(name, 512, gcol)] + cots[name], [(conv_w[name], 512, gcol), (conv_b[name], 512, gcol)],
            [into_main(name, 512, gcol)], [(3, wd, 512, gcol), (1, wd, 512, gcol)], halo=True)
    grads["w_ssd_conv"] = jnp.concatenate([g_cw["xs"], g_cw["B"], g_cw["C"]], axis=1)
    grads["b_ssd_conv"] = jnp.concatenate([g_cb["xs"], g_cb["B"], g_cb["C"]], axis=1)

    da_ret = _mm("d_ret_act", dyr, w["w_ret_out"], "nt", out_dtype=BF16)
    grads["w_ret_out"] = _mm("g_ret_out", a_ret, dyr, "tn", out_dtype=BF16)
    tick = early_grads({n: grads.pop(n) for n in ("w_ffn_up_t", "w_ret_out", "w_ssd_out", "w_out", "w_ffn_down")})

    def ret_post_bwd_fn(i, y, g, da, gw):
        _, vjp = jax.vjp(_ret_post, y, g, gw)
        return vjp(da)

    dy_ret, d_main, grads["ret_gn_w"] = _rows(
        "ret_post_bwd", ret_post_bwd_fn, T, 1, [(y_ret, 1024, c0), seg("g", 1024), (da_ret, 1024, c0)],
        [(w["ret_gn_w"] + tick, 1024, c0)], [(1024, 1024, c0, BF16), into_main("g", 1024)], [(1, 1024, 1024, c0)])
    d_main = _retention("retention_dv", kr, qr, dy_ret, T, RET_QK_DIM, RET_V_DIM, into=(d_main, seg_at["v"]))
    dqr = _retention("retention_dq", dy_ret, v_at, kr, T, RET_V_DIM, RET_QK_DIM)
    dkr = _retention("retention_dk", v_at, dy_ret, qr, T, RET_V_DIM, RET_QK_DIM)

    def rot_bwd_fn(i, dq, dk, csv, snv):
        parts = [_rot_t(dq[:, h * 128:(h + 1) * 128], csv, snv) for h in range(RET_HEADS)]
        parts += [_rot_t(dk[:, h * 128:(h + 1) * 128] * scale, csv, snv) for h in range(RET_HEADS)]
        return (jnp.concatenate(parts, axis=1),)

    d_main = _rows("rotary_bwd", rot_bwd_fn, T, 1, [(dqr, 512, c0), (dkr, 512, c0), (cs, 128, c0), (sn, 128, c0)],
                   [], [into_main("qk", 1024)], tall=True)[0]

    g_in = [_mm("g_in_main", d_main, u, "tn", out_dtype=BF16),
            _mm("g_in_dt", dproj["dt"], u, "tn", out_dtype=BF16)[:2 * SSD_HEADS],
            _mm("g_in_gates", dproj["gates"], u, "tn", out_dtype=BF16)]
    tick = in_grads(jnp.concatenate(g_in, axis=0))
    du = _mm("d_u_dt", dproj["dt"] + tick.astype(BF16), w_dt, "nn")
    du = _mm("d_u_main", d_main, w_main, "nn", add=du)
    du = _mm("d_u_gates", dproj["gates"], w_gates, "nn", add=du)
    dh0, grads["norm_mix_w"] = norm_bwd("norm_mix_bwd", h0, w["norm_mix_w"], du, dh1)
    grads["meta_tokens"] = dh0[PAD_ROWS:CHUNK]
    return loss, dh0[CHUNK:], grads


MESH_ID = pl.DeviceIdType.MESH
ANY = pl.BlockSpec(memory_space=pl.ANY)


def _me_and_peers():
    x, y, c = lax.axis_index("x"), lax.axis_index("y"), lax.axis_index("c")
    peers = []
    for k in range(1, N_DEV):
        px = 1 - x if k & 4 else x
        py = 1 - y if k & 2 else y
        pc = 1 - c if k & 1 else c
        peers.append(((px, py, pc), 4 * px + 2 * py + pc))
    return 4 * x + 2 * y + c, peers


def _push_blocks(name, src, per_peer):
    blk = src.shape[1:] if per_peer else src.shape

    def body(src_ref, out_ref, send_sems, recv_sems, local_sem):
        me, peers = _me_and_peers()
        mine = src_ref.at[me] if per_peer else src_ref
        local = pltpu.make_async_copy(mine, out_ref.at[me], local_sem)
        local.start()
        sends = []
        for k, (dev, idx) in enumerate(peers):
            cp = pltpu.make_async_remote_copy(
                src_ref=src_ref.at[idx] if per_peer else src_ref, dst_ref=out_ref.at[me],
                send_sem=send_sems.at[k], recv_sem=recv_sems.at[k], device_id=dev, device_id_type=MESH_ID)
            cp.start()
            sends.append(cp)
        for k, (dev, idx) in enumerate(peers):
            pltpu.make_async_remote_copy(
                src_ref=mine, dst_ref=out_ref.at[idx], send_sem=send_sems.at[k], recv_sem=recv_sems.at[k],
                device_id=dev, device_id_type=MESH_ID).wait_recv()
        for cp in sends:
            cp.wait_send()
        local.wait()

    return pl.pallas_call(
        body, name=name, in_specs=[ANY], out_specs=ANY,
        out_shape=jax.ShapeDtypeStruct((N_DEV,) + tuple(blk), src.dtype),
        scratch_shapes=[pltpu.SemaphoreType.DMA((N_DEV - 1,)), pltpu.SemaphoreType.DMA((N_DEV - 1,)),
                        pltpu.SemaphoreType.DMA],
    )(src)


def _gather_two_level(name, src):
    def body(x_ref, out_ref, send_sems, recv_sems, local_sem):
        x, y, c = lax.axis_index("x"), lax.axis_index("y"), lax.axis_index("c")
        me, sibling = (x, y, c), (x, y, 1 - c)
        chips = [(1 - x, y), (x, 1 - y), (1 - x, 1 - y)]

        def rows(px, py, pc):
            return out_ref.at[4 * px + 2 * py + pc]

        def copy(k, block, to, src_ref=None):
            return pltpu.make_async_remote_copy(
                src_ref=rows(*block) if src_ref is None else src_ref, dst_ref=rows(*block),
                send_sem=send_sems.at[k], recv_sem=recv_sems.at[k], device_id=to, device_id_type=MESH_ID)

        mine = pltpu.make_async_copy(x_ref, rows(*me), local_sem)
        mine.start()
        first = [copy(0, me, sibling, x_ref)] + [copy(1 + j, me, (*chip, c), x_ref) for j, chip in enumerate(chips)]
        for cp in first:
            cp.start()
        passed = [copy(4 + j, (*chip, c), sibling) for j, chip in enumerate(chips)]
        for j, chip in enumerate(chips):
            copy(1 + j, (*chip, c), me).wait_recv()
            passed[j].start()
        copy(0, sibling, me).wait_recv()
        for j, chip in enumerate(chips):
            copy(4 + j, (*chip, 1 - c), me).wait_recv()
        for cp in first + passed:
            cp.wait_send()
        mine.wait()

    return pl.pallas_call(
        body, name=name, in_specs=[ANY], out_specs=ANY,
        out_shape=jax.ShapeDtypeStruct((N_DEV,) + tuple(src.shape), src.dtype),
        scratch_shapes=[pltpu.SemaphoreType.DMA((N_DEV - 1,)), pltpu.SemaphoreType.DMA((N_DEV - 1,)),
                        pltpu.SemaphoreType.DMA],
    )(src)


HBM = pl.BlockSpec(memory_space=pltpu.HBM)
SEM = pl.BlockSpec(memory_space=pltpu.SEMAPHORE)
EFFECT = pltpu.SideEffectType.DATAFLOW_SIDE_EFFECTING


def _peer_copy(src_ref, land_ref, send_sems, recv_sems, per_peer, me, a, k, dev, idx, receiving):
    s = a * (N_DEV - 1) + k
    return pltpu.make_async_remote_copy(
        src_ref=src_ref.at[idx] if per_peer else src_ref, dst_ref=land_ref.at[idx if receiving else me],
        send_sem=send_sems.at[s], recv_sem=recv_sems.at[s], device_id=dev, device_id_type=MESH_ID)


def _push_start(name, srcs, per_peer):
    n = len(srcs)
    land_shapes = [(N_DEV,) + tuple(s.shape[1:] if per_peer else s.shape) for s in srcs]

    def body(*refs):
        src_refs, land_refs, send_sems, recv_sems, token = refs[:n], refs[n:2 * n], refs[2 * n], refs[2 * n + 1], refs[-1]
        me, peers = _me_and_peers()
        for a in range(n):
            for k, (dev, idx) in enumerate(peers):
                _peer_copy(src_refs[a], land_refs[a], send_sems, recv_sems, per_peer, me, a, k, dev, idx, False).start()
        token[...] = jnp.zeros_like(token)

    sems = pltpu.SemaphoreType.DMA((n * (N_DEV - 1),))
    res = pl.pallas_call(
        body, name=name,
        out_shape=(sems, sems, *[pltpu.HBM(s.shape, s.dtype) for s in srcs],
                   *[pltpu.HBM(ls, s.dtype) for ls, s in zip(land_shapes, srcs)], jax.ShapeDtypeStruct((8, 128), F32)),
        in_specs=(HBM,) * (2 * n), out_specs=(SEM, SEM) + (HBM,) * (2 * n) + (pl.BlockSpec(memory_space=pltpu.VMEM),),
        input_output_aliases={i: 2 + i for i in range(2 * n)},
        compiler_params=pltpu.CompilerParams(has_side_effects=EFFECT),
    )(*[pltpu.with_memory_space_constraint(s, pltpu.HBM) for s in srcs],
      *[pltpu.with_memory_space_constraint(lax.empty(ls, s.dtype), pltpu.HBM) for ls, s in zip(land_shapes, srcs)])
    return res[0], res[1], res[2:2 + n], res[2 + n:2 + 2 * n], res[-1]


def _push_wait(name, send_sems, recv_sems, srcs_thru, lands_thru, after, per_peer):
    n = len(srcs_thru)

    def body(*refs):
        src_refs, land_refs, send_sems, recv_sems = refs[:n], refs[n:2 * n], refs[2 * n], refs[2 * n + 1]
        me, peers = _me_and_peers()
        for a in range(n):
            for k, (dev, idx) in enumerate(peers):
                cp = _peer_copy(src_refs[a], land_refs[a], send_sems, recv_sems, per_peer, me, a, k, dev, idx, True)
                cp.wait_send()
                cp.wait_recv()

    both = list(srcs_thru) + list(lands_thru)
    res = pl.pallas_call(
        body, name=name, out_shape=tuple(pltpu.HBM(t.shape, t.dtype) for t in both),
        in_specs=(HBM,) * (2 * n) + (SEM, SEM, ANY), out_specs=(HBM,) * (2 * n),
        input_output_aliases={i: i for i in range(2 * n)},
        compiler_params=pltpu.CompilerParams(has_side_effects=EFFECT),
    )(*both, send_sems, recv_sems, after)
    return res[:n], res[n:]


def _sum_blocks(name, blocks):
    _, R, C = blocks.shape
    tc = next(t for t in (1024, 512, 256, 128) if C % t == 0 and (N_DEV * R * t * 2 <= 6 * 2 ** 20 or t == 128))

    def body(b_ref, o_ref):
        acc = b_ref[0].astype(F32)
        for k in range(1, N_DEV):
            acc = acc + b_ref[k].astype(F32)
        o_ref[...] = acc

    return pl.pallas_call(
        body, name=name, grid=(C // tc,), in_specs=[pl.BlockSpec((N_DEV, R, tc), lambda j: (0, 0, j))],
        out_specs=pl.BlockSpec((R, tc), lambda j: (0, j)), out_shape=jax.ShapeDtypeStruct((R, C), F32),
        compiler_params=_params(("arbitrary",)),
    )(blocks)


def _adamw(name, w, g, m, v):
    R, C = w.shape
    tr = R if R <= 512 else _pick(R, (256, 184, 176, 128, 8))
    spec = pl.BlockSpec((tr, C), lambda i: (i, 0))

    def body(w_ref, g_ref, m_ref, v_ref, d_ref, mo_ref, vo_ref):
        gv = g_ref[...]
        mn = ADAM_B1 * m_ref[...] + (1.0 - ADAM_B1) * gv
        vn = ADAM_B2 * v_ref[...] + (1.0 - ADAM_B2) * jnp.square(gv)
        m_hat = mn / (1.0 - ADAM_B1 ** ADAM_STEP)
        v_hat = vn / (1.0 - ADAM_B2 ** ADAM_STEP)
        d_ref[...] = -ADAM_LR * (m_hat / (jnp.sqrt(v_hat) + ADAM_EPS) + ADAM_WD * w_ref[...])
        mo_ref[...] = mn
        vo_ref[...] = vn

    return pl.pallas_call(
        body, name=name, grid=(R // tr,), in_specs=[spec] * 4, out_specs=[spec] * 3,
        out_shape=[jax.ShapeDtypeStruct((R, C), F32)] * 3, compiler_params=_params(("arbitrary",)),
    )(w, g, m, v)


WEIGHTS = ("meta_tokens", "norm_mix_w", "w_in", "ret_gn_w", "w_ret_out", "w_ssd_conv", "b_ssd_conv", "dt_bias_f",
           "dt_bias_b", "a_log_f", "a_log_b", "d_skip", "ssd_norm_w", "w_ssd_out", "w_out", "norm_ffn_w", "w_ffn_up",
           "w_ffn_conv", "b_ffn_conv", "w_ffn_down", "final_norm_w")
BIG = (("w_in", 1288, True), ("w_ffn_up", 704, True), ("w_ret_out", 128, False), ("w_ssd_out", 256, False),
       ("w_out", 128, False), ("w_ffn_down", 352, False))
REPLICATED = ("norm_mix_w", "ret_gn_w", "b_ssd_conv", "dt_bias_f", "dt_bias_b", "a_log_f", "a_log_b", "d_skip",
              "ssd_norm_w", "norm_ffn_w", "b_ffn_conv", "final_norm_w")
SMALL_SHARDED = (("meta_tokens", 16, 1024), ("w_ssd_conv", 3, 3072), ("w_ffn_conv", 3, 5632))


BIG_IN, BIG_REST = BIG[:1], BIG[1:]


def _pack_big(tree, group):
    parts = []
    for name, _, transposed in group:
        a = tree[name][0]
        parts.append(a.T if transposed else a)
    return jnp.concatenate(parts, axis=0)


def _unpack_big(slab, group):
    out, r0 = {}, 0
    for name, r, transposed in group:
        a = slab[r0:r0 + r]
        out[name] = (a.T if transposed else a)[None]
        r0 += r
    return out


def _pack_flat(arrays, rows):
    flat = jnp.concatenate([a.reshape(-1) for a in arrays])
    return jnp.pad(flat, (0, rows * D_MODEL - flat.shape[0])).reshape(rows, D_MODEL)


def _unpack_flat(slab, shapes):
    flat, out, o = slab.reshape(-1), [], 0
    for s in shapes:
        n = math.prod(s)
        out.append(flat[o:o + n].reshape(s))
        o += n
    return out


def kernel(x, meta_tokens, norm_mix_w, w_in, ret_gn_w, w_ret_out, w_ssd_conv, b_ssd_conv, dt_bias_f, dt_bias_b, a_log_f, a_log_b, d_skip, ssd_norm_w, w_ssd_out, w_out, norm_ffn_w, w_ffn_up, w_ffn_conv, b_ffn_conv, w_ffn_down, final_norm_w, loss_target, m_meta_tokens, m_norm_mix_w, m_w_in, m_ret_gn_w, m_w_ret_out, m_w_ssd_conv, m_b_ssd_conv, m_dt_bias_f, m_dt_bias_b, m_a_log_f, m_a_log_b, m_d_skip, m_ssd_norm_w, m_w_ssd_out, m_w_out, m_norm_ffn_w, m_w_ffn_up, m_w_ffn_conv, m_b_ffn_conv, m_w_ffn_down, m_final_norm_w, v_meta_tokens, v_norm_mix_w, v_w_in, v_ret_gn_w, v_w_ret_out, v_w_ssd_conv, v_b_ssd_conv, v_dt_bias_f, v_dt_bias_b, v_a_log_f, v_a_log_b, v_d_skip, v_ssd_norm_w, v_w_ssd_out, v_w_out, v_norm_ffn_w, v_w_ffn_up, v_w_ffn_conv, v_b_ffn_conv, v_w_ffn_down, v_final_norm_w):
    given = dict(locals())
    wt = {n: given[n] for n in WEIGHTS}
    mt = {n: given["m_" + n] for n in WEIGHTS}
    vt = {n: given["v_" + n] for n in WEIGHTS}
    me = 4 * lax.axis_index("x") + 2 * lax.axis_index("y") + lax.axis_index("c")

    small_names = [n for n, _, _ in SMALL_SHARDED]
    small_local = lambda tree: [tree[n].reshape(r, c // N_DEV) for n, r, c in SMALL_SHARDED]
    all_in = _gather_two_level("gather_w_in", _pack_big(wt, BIG_IN).astype(BF16))
    all_s = _push_blocks("gather_small", _pack_flat(small_local(wt), 8), False)
    slab_view = lambda tree, name, transposed: tree[name][0].T if transposed else tree[name][0]
    rest_srcs = [slab_view(wt, name, t).astype(BF16) for name, _, t in BIG_REST]
    rest_srcs, all_in, all_s = lax.optimization_barrier((rest_srcs, all_in, all_s))
    rest_flight = _push_start("gather_rest_start", rest_srcs, False)
    all_s = all_s.reshape(N_DEV, -1)
    full = {"w_in_t": all_in.reshape(-1, D_MODEL)}

    def lands_with_own(flight, after, per_peer, name):
        srcs, lands = _push_wait(name, *flight[:4], after, per_peer)
        own = lambda s: lax.dynamic_slice_in_dim(s, me, 1, axis=0) if per_peer else s[None]
        return [lax.dynamic_update_slice_in_dim(land, own(s), me, axis=0) for s, land in zip(srcs, lands)]

    def late_weights(after):
        lands = lands_with_own(rest_flight, after, False, "gather_rest_wait")
        return {name + ("_t" if t else ""): land.reshape(N_DEV * r, D_MODEL) for (name, r, t), land in zip(BIG_REST, lands)}

    flights = {}

    def start_exchange(key, group, gd):
        srcs = [gd[name + ("_t" if t else "")].astype(BF16).reshape(N_DEV, r, D_MODEL) for name, r, t in group]
        flights[key] = _push_start("exchange_" + key + "_start", srcs, True)
        return flights[key][4][0, 0]

    o = 0
    for name, r, c in SMALL_SHARDED:
        n = r * c // N_DEV
        full[name] = all_s[:, o:o + n].reshape(N_DEV, r, c // N_DEV).transpose(1, 0, 2).reshape(r, c)
        o += n
    for name in REPLICATED:
        full[name] = wt[name]

    grads, delta, new_m, new_v = {}, {}, {}, {}

    def finish_exchange(key, group, after):
        lands = lands_with_own(flights[key], after, True, "exchange_" + key + "_wait")
        for (name, _, transposed), land in zip(group, lands):
            back = (lambda a: a.T[None]) if transposed else (lambda a: a[None])
            g_sum = _sum_blocks("sum_" + name, land)
            d, mn, vn = _adamw("adamw_" + name, slab_view(wt, name, transposed), g_sum,
                               slab_view(mt, name, transposed), slab_view(vt, name, transposed))
            grads[name], delta[name], new_m[name], new_v[name] = back(g_sum), back(d), back(mn), back(vn)

    def in_grads(gi):
        tick = start_exchange("in", BIG_IN, {"w_in_t": gi})
        finish_exchange("rest", BIG_REST, flights["in"][4])
        tick, _ = lax.optimization_barrier((tick, [delta[name] for name, _, _ in BIG_REST]))
        return tick

    loss, grad_x, g = _local_step(x[0], loss_target[0], full, rest_flight[4][0, 0], late_weights,
                                  lambda gd: start_exchange("rest", BIG_REST, gd), in_grads)

    finish_exchange("in", BIG_IN, g["norm_mix_w"])
    small_parts = [g[n] for n in REPLICATED] + [g[n] for n in small_names] + [loss.reshape(1)]
    g_small = _sum_blocks("sum_small", _push_blocks("gather_small_grads", _pack_flat(small_parts, 64), False))
    small_red = _unpack_flat(g_small, [wt[n].shape for n in REPLICATED] + [(r, c) for _, r, c in SMALL_SHARDED] + [(1,)])
    grads.update(zip(REPLICATED, small_red[:len(REPLICATED)]))
    for (name, r, c), red in zip(SMALL_SHARDED, small_red[len(REPLICATED):-1]):
        grads[name] = lax.dynamic_slice(red, (0, me * (c // N_DEV)), (r, c // N_DEV)).reshape(wt[name].shape)
    loss_all = small_red[-1][0]

    rest = list(REPLICATED) + small_names
    shapes = [wt[n].shape for n in rest]
    pack_rest = lambda tree: _pack_flat([tree[n] for n in rest], 24)
    d_rest, m_rest, v_rest = _adamw("adamw_small", pack_rest(wt), pack_rest(grads), pack_rest(mt), pack_rest(vt))
    delta.update(zip(rest, _unpack_flat(d_rest, shapes)))
    new_m.update(zip(rest, _unpack_flat(m_rest, shapes)))
    new_v.update(zip(rest, _unpack_flat(v_rest, shapes)))

    return (loss_all, grad_x[None], *[grads[n] for n in WEIGHTS], *[delta[n] for n in WEIGHTS],
            *[new_m[n] for n in WEIGHTS], *[new_v[n] for n in WEIGHTS])
```

```python
import functools
import math

import jax
import jax.numpy as jnp
from jax import lax
from jax.experimental import pallas as pl
from jax.experimental.pallas import tpu as pltpu

F32 = jnp.float32
BF16 = jnp.bfloat16

D_MODEL = 1024
CHUNK = 128
N_META = 16
PAD_ROWS = CHUNK - N_META
RET_HEADS = 4
RET_QK_DIM = 128
RET_V_DIM = 256
SSD_HEADS = 32
SSD_HEAD_DIM = 64
SSD_GROUPS = 4
SSD_STATE = 128
HEADS_PER_GROUP = SSD_HEADS // SSD_GROUPS
PAIRS_PER_GROUP = HEADS_PER_GROUP // 2
D_FF = 2816
EPS = 1e-6
ROPE_BASE = 10000.0
N_DEV = 8

ADAM_LR = 0.001
ADAM_B1 = 0.9
ADAM_B2 = 0.999
ADAM_EPS = 1e-08
ADAM_WD = 0.01
ADAM_STEP = 10

VMEM_LIMIT = 56 * 1024 * 1024
HALO = 16
HIGHEST = lax.Precision.HIGHEST

SEGMENTS = (("qk", 0, 1024), ("v", 1024, 2048), ("g", 2048, 3072), ("z", 3072, 5120), ("xs", 5120, 7168),
            ("B", 7168, 7680), ("C", 7680, 8192), ("dt", 8192, 8256), ("gates", 8256, 10304))


def _pick(n, cands):
    for c in cands:
        if n % c == 0:
            return c
    raise ValueError(f"no tile for {n}")


def _params(sem):
    return pltpu.CompilerParams(dimension_semantics=sem, vmem_limit_bytes=VMEM_LIMIT)


def _dot(a, b, dims=(((1,), (0,)), ((), ())), precision=None):
    return lax.dot_general(a, b, dims, preferred_element_type=F32, precision=precision)


def _dot_nt(a, b):
    return _dot(a, b, (((1,), (1,)), ((), ())))


def _dot_tn(a, b):
    return _dot(a, b, (((0,), (0,)), ((), ())))


def _mm(name, a, b, mode, add=None, out_dtype=F32):
    if mode == "nn":
        (M, K), N = a.shape, b.shape[1]
    elif mode == "nt":
        (M, K), N = a.shape, b.shape[0]
    else:
        (K, M), N = a.shape, b.shape[1]
    tn = _pick(N, (1408, 1024, 512, 128, 64))
    if mode == "tn":
        tm = M if M <= 1024 else _pick(M, (1408, 1024))
        tk = _pick(K, (2112, 512, 256, 128))
    else:
        tm = _pick(M, (1056, 512, 256, 128))
        tk = K if K <= 2816 else _pick(K, (2048, 1408, 1024))
    nk = K // tk
    if mode == "nn":
        a_spec = pl.BlockSpec((tm, tk), lambda n, m, k: (m, k))
        b_spec = pl.BlockSpec((tk, tn), lambda n, m, k: (k, n))
        dims = (((1,), (0,)), ((), ()))
    elif mode == "nt":
        a_spec = pl.BlockSpec((tm, tk), lambda n, m, k: (m, k))
        b_spec = pl.BlockSpec((tn, tk), lambda n, m, k: (n, k))
        dims = (((1,), (1,)), ((), ()))
    else:
        a_spec = pl.BlockSpec((tk, tm), lambda n, m, k: (k, m))
        b_spec = pl.BlockSpec((tk, tn), lambda n, m, k: (k, n))
        dims = (((0,), (0,)), ((), ()))
    o_spec = pl.BlockSpec((tm, tn), lambda n, m, k: (m, n))
    in_specs = [a_spec, b_spec] + ([o_spec] if add is not None else [])
    args = [a, b] + ([add] if add is not None else [])

    def body(*refs):
        if add is not None:
            a_ref, b_ref, r_ref, o_ref, acc = refs
        else:
            a_ref, b_ref, o_ref, acc = refs
        k = pl.program_id(2)
        p = _dot(a_ref[...].astype(BF16), b_ref[...].astype(BF16), dims)

        def finish(r):
            if add is not None:
                r = r + r_ref[...]
            o_ref[...] = r.astype(out_dtype)

        if nk == 1:
            finish(p)
        else:
            @pl.when(k == 0)
            def _():
                acc[...] = p

            @pl.when(k > 0)
            def _():
                acc[...] += p

            @pl.when(k == nk - 1)
            def _():
                finish(acc[...])

    return pl.pallas_call(
        body, name=name, grid=(N // tn, M // tm, nk), in_specs=in_specs, out_specs=o_spec,
        out_shape=jax.ShapeDtypeStruct((M, N), out_dtype),
        scratch_shapes=[pltpu.VMEM((tm, tn) if nk > 1 else (8, 128), F32)],
        compiler_params=_params(("arbitrary", "arbitrary", "arbitrary")),
    )(*args)


ANY_SPACE = pl.BlockSpec(memory_space=pl.ANY)


def _const(c):
    return lambda j: c


def _rows(name, fn, T, ncol, ins, params, outs, accs=(), halo=False, tall=False):
    tm = _pick(T, (1056, 512, 256, 128)) if tall else _pick(T, (384, 256, 128))
    R = T // tm
    hb = tm // HALO
    in_specs, args = [], []
    for spec in ins:
        arr, w, cf = spec[:3]
        lead = spec[3] if len(spec) > 3 else None
        if len(spec) > 4:
            rows, rf = spec[4]
            in_specs.append(pl.BlockSpec((rows, w), lambda j, i, cf=cf, rf=rf: (rf(i), cf(j))))
            args.append(arr)
            continue
        if lead is None:
            mk = lambda blk, rf, cf=cf: pl.BlockSpec(blk, lambda j, i: (rf(i), cf(j)))
            shape = lambda r, w=w: (r, w)
        else:
            mk = lambda blk, rf, cf=cf, lead=lead: pl.BlockSpec(blk, lambda j, i: (lead, rf(i), cf(j)))
            shape = lambda r, w=w: (None, r, w)
        in_specs.append(mk(shape(tm), lambda i: i))
        args.append(arr)
        if halo:
            in_specs.append(mk(shape(HALO), lambda i: jnp.maximum(i * hb - 1, 0)))
            in_specs.append(mk(shape(HALO), lambda i: jnp.minimum((i + 1) * hb, T // HALO - 1)))
            args += [arr, arr]
    for arr, w, cf in params:
        in_specs.append(pl.BlockSpec((arr.shape[0], w), lambda j, i, cf=cf: (0, cf(j))))
        args.append(arr)
    out_shape, out_specs, aliases = [], [], {}
    for k, (tw, w, cf, dt) in enumerate(outs):
        if not isinstance(tw, int):
            aliases[len(args)] = k
            in_specs.append(ANY_SPACE)
            args.append(tw)
            tw = tw.shape[1]
        out_shape.append(jax.ShapeDtypeStruct((T, tw), dt))
        out_specs.append(pl.BlockSpec((tm, w), lambda j, i, cf=cf: (i, cf(j))))
    for r, tw, w, cf in accs:
        out_shape.append(jax.ShapeDtypeStruct((r, tw), F32))
        out_specs.append(pl.BlockSpec((r, w), lambda j, i, cf=cf: (0, cf(j))))
    n_in, n_par, n_out, n_acc, n_alias = len(ins), len(params), len(outs), len(accs), len(aliases)

    def body(*refs):
        i = pl.program_id(1)
        vals, p = [], 0
        for _ in range(n_in):
            if halo:
                before = jnp.where(i > 0, refs[p + 1][...], jnp.zeros_like(refs[p + 1]))
                after = jnp.where(i < R - 1, refs[p + 2][...], jnp.zeros_like(refs[p + 2]))
                vals.append(jnp.concatenate([before, refs[p][...], after], axis=0).astype(F32))
                p += 3
            else:
                vals.append(refs[p][...].astype(F32))
                p += 1
        pvals = [refs[p + k][...] for k in range(n_par)]
        p += n_par + n_alias
        res = fn(i, *vals, *pvals)
        for k in range(n_out):
            refs[p + k][...] = res[k].astype(refs[p + k].dtype)
        p += n_out
        for k in range(n_acc):
            ref, v = refs[p + k], res[n_out + k]

            @pl.when(i == 0)
            def _(ref=ref, v=v):
                ref[...] = v

            @pl.when(i > 0)
            def _(ref=ref, v=v):
                ref[...] += v

    res = pl.pallas_call(
        body, name=name, grid=(ncol, R), in_specs=in_specs, out_specs=out_specs, out_shape=out_shape,
        input_output_aliases=aliases, compiler_params=_params(("arbitrary", "arbitrary")),
    )(*args)
    return res


def _tile_rows(T):
    return _pick(T, (384, 256, 128))


def _row_ids(i, T, halo=False):
    tm = _tile_rows(T)
    if halo:
        return i * tm - HALO + lax.broadcasted_iota(jnp.int32, (tm + 2 * HALO, 1), 0)
    return i * tm + lax.broadcasted_iota(jnp.int32, (tm, 1), 0)


def _rms(x, w):
    return x * lax.rsqrt(jnp.mean(x * x, axis=-1, keepdims=True) + EPS) * w


def _silu(x):
    return x * jax.nn.sigmoid(x)


def _conv3(x, w):
    n = x.shape[0]
    return w[0:1] * pltpu.roll(x, 1, 0) + w[1:2] * x + w[2:3] * pltpu.roll(x, n - 1, 0)


def _conv3_t(d, w):
    n = d.shape[0]
    return w[0:1] * pltpu.roll(d, n - 1, 0) + w[1:2] * d + w[2:3] * pltpu.roll(d, 1, 0)


def _center(x):
    return x[HALO:x.shape[0] - HALO]


def _retention(name, a, b, v, T, da, dv, into=None):
    (a, a0), (b, b0), (v, v0) = [t if isinstance(t, tuple) else (t, 0) for t in (a, b, v)]
    nc = T // CHUNK
    log_gammas = [math.log(1.0 - 2.0 ** (-5.0 - h)) for h in range(RET_HEADS)]

    def body(*refs):
        a_ref, b_ref, v_ref = refs[:3]
        out_ref, o_ref, st, st_b = refs[-4:]
        h = pl.program_id(0)
        lg = jnp.float32(log_gammas[RET_HEADS - 1])
        for k in range(RET_HEADS - 2, -1, -1):
            lg = jnp.where(h == k, jnp.float32(log_gammas[k]), lg)
        li = lax.broadcasted_iota(jnp.int32, (CHUNK, CHUNK), 0)
        si = lax.broadcasted_iota(jnp.int32, (CHUNK, CHUNK), 1)
        dmat = jnp.exp(lg * jnp.abs(li - si).astype(F32))
        pos = lax.broadcasted_iota(jnp.int32, (CHUNK, 1), 0).astype(F32)
        kdec_f = jnp.exp((CHUNK - 1 - pos) * lg)
        qdec_f = jnp.exp((pos + 1) * lg)
        kdec_b = jnp.exp(pos * lg)
        qdec_b = jnp.exp((CHUNK - pos) * lg)
        cdec = jnp.exp(CHUNK * lg)

        def rows(n):
            return pl.ds(pl.multiple_of(n * CHUNK, CHUNK), CHUNK)

        st[...] = jnp.zeros_like(st)
        st_b[...] = jnp.zeros_like(st_b)
        o_ref[...] = jnp.zeros_like(o_ref)

        def step(m, carry):
            r = rows(m)
            av, bv, vv = a_ref[r, :], b_ref[r, :], v_ref[r, :].astype(BF16)
            s = _dot_nt(av.astype(BF16), bv.astype(BF16)) * dmat
            o_ref[r, :] += _dot(s.astype(BF16), vv) + _dot((av * qdec_f).astype(BF16), st[...].astype(BF16))
            st[...] = cdec * st[...] + _dot_tn((bv * kdec_f).astype(BF16), vv)
            r = rows(nc - 1 - m)
            av, bv, vv = a_ref[r, :], b_ref[r, :], v_ref[r, :].astype(BF16)
            o_ref[r, :] += _dot((av * qdec_b).astype(BF16), st_b[...].astype(BF16))
            st_b[...] = cdec * st_b[...] + _dot_tn((bv * kdec_b).astype(BF16), vv)
            return carry

        lax.fori_loop(0, nc, step, 0, unroll=11 if nc % 11 == 0 else 1)
        out_ref[...] = o_ref[...].astype(out_ref.dtype)

    in_specs = [pl.BlockSpec((T, da), lambda h: (0, a0 // da + h)), pl.BlockSpec((T, da), lambda h: (0, b0 // da + h)),
                pl.BlockSpec((T, dv), lambda h: (0, v0 // dv + h))]
    if into is None:
        args, o0, aliases = (a, b, v), 0, {}
        out_shape = jax.ShapeDtypeStruct((T, RET_HEADS * dv), F32)
    else:
        args, o0, aliases = (a, b, v, into[0]), into[1], {3: 0}
        in_specs.append(ANY_SPACE)
        out_shape = jax.ShapeDtypeStruct(into[0].shape, into[0].dtype)
    return pl.pallas_call(
        body, name=name, grid=(RET_HEADS,), in_specs=in_specs,
        out_specs=pl.BlockSpec((T, dv), lambda h: (0, o0 // dv + h)), out_shape=out_shape,
        input_output_aliases=aliases,
        scratch_shapes=[pltpu.VMEM((T, dv), F32), pltpu.VMEM((da, dv), F32), pltpu.VMEM((da, dv), F32)],
        compiler_params=_params(("arbitrary",)),
    )(*args)


def _softplus(x):
    return jnp.maximum(x, 0.0) + jnp.log1p(jnp.exp(-jnp.abs(x)))


def _lane_lo():
    return lax.broadcasted_iota(jnp.int32, (1, CHUNK), 1) < SSD_HEAD_DIM


def _pair_cols(col, j):
    return jnp.where(_lane_lo(), col[:, 2 * j:2 * j + 1], col[:, 2 * j + 1:2 * j + 2])


def _pair_rows(colr, j):
    lo = lax.broadcasted_iota(jnp.int32, (CHUNK, 1), 0) < SSD_HEAD_DIM
    return jnp.where(lo, colr[2 * j:2 * j + 1, :], colr[2 * j + 1:2 * j + 2, :])


def _onehot8(h):
    return (lax.broadcasted_iota(jnp.int32, (1, HEADS_PER_GROUP), 1) == h).astype(F32)


def _ssd_pre(d, c, rawc, rawr, bc, br, alc, alr):
    li = lax.broadcasted_iota(jnp.int32, (CHUNK, CHUNK), 0)
    si = lax.broadcasted_iota(jnp.int32, (CHUNK, CHUNK), 1)
    dif = li - si if d == 0 else si - li
    mask = dif >= 0
    mask_t = dif <= 0
    rowc = c * CHUNK + lax.broadcasted_iota(jnp.int32, (CHUNK, 1), 0)
    rowr = c * CHUNK + lax.broadcasted_iota(jnp.int32, (1, CHUNK), 1)
    dtc = jnp.where(rowc >= PAD_ROWS, _softplus(rawc + bc), 0.0)
    dtr = jnp.where(rowr >= PAD_ROWS, _softplus(rawr + br), 0.0)
    ac = -jnp.exp(alc)
    ar = -jnp.exp(alr)
    dlc = dtc * ac
    dlr = dtr * ar
    alpc = _dot(mask.astype(F32), dlc, precision=HIGHEST)
    alpr = _dot(dlr, mask_t.astype(F32), precision=HIGHEST)
    endc = jnp.sum(dlc, axis=0, keepdims=True)
    endr = jnp.sum(dlr, axis=1, keepdims=True)
    return dict(mask=mask, mask_t=mask_t, dtc=dtc, ac=ac, alpc=alpc, alpr=alpr, endc=endc, endr=endr,
                valid=rowc >= PAD_ROWS)


def _chunk_of(d, n, nc):
    return n + d * (nc - 1 - 2 * n)


GROUP_WIDTH = HEADS_PER_GROUP * SSD_HEAD_DIM


def _chunks_per_step(nc, most=3):
    return next(c for c in (11, 3, 1) if c <= most and nc % c == 0)


def _ssd_in_specs(d, cfn, rows):
    return [
        pl.BlockSpec((rows, GROUP_WIDTH), lambda g, n: (cfn(d, n), g)),
        pl.BlockSpec((rows, SSD_STATE), lambda g, n: (cfn(d, n), g)),
        pl.BlockSpec((rows, SSD_STATE), lambda g, n: (cfn(d, n), g)),
        pl.BlockSpec((None, None, rows, HEADS_PER_GROUP), lambda g, n: (d, g, cfn(d, n), 0)),
        pl.BlockSpec((None, None, HEADS_PER_GROUP, rows), lambda g, n: (d, g, 0, cfn(d, n))),
        pl.BlockSpec((None, None, 1, HEADS_PER_GROUP), lambda g, n: (d, g, 0, 0)),
        pl.BlockSpec((None, None, HEADS_PER_GROUP, 1), lambda g, n: (d, g, 0, 0)),
        pl.BlockSpec((None, None, 1, HEADS_PER_GROUP), lambda g, n: (d, g, 0, 0)),
        pl.BlockSpec((None, None, HEADS_PER_GROUP, 1), lambda g, n: (d, g, 0, 0)),
    ]


N_SSD_IN = 9


def _ssd_fwd(xs, bm, cm, small, T):
    nc = T // CHUNK
    cps = _chunks_per_step(nc, 11)
    rows = cps * CHUNK
    cfn = lambda d, n: _chunk_of(d, n, nc // cps)

    def one_direction(d, n, ins, y_ref, hs_ref, h_scr):
        x_ref, b_ref, c_ref, rawc_ref, rawr_ref, *per_group = ins
        for kk in range(cps):
            k = kk if d == 0 else cps - 1 - kk
            r = pl.ds(k * CHUNK, CHUNK)
            one_chunk(d, cfn(d, n) * cps + k,
                      (x_ref.at[r], b_ref.at[r], c_ref.at[r], rawc_ref.at[r], rawr_ref.at[:, r], *per_group),
                      y_ref.at[r], hs_ref.at[k], h_scr)

    def one_chunk(d, c, ins, y_ref, hs_ref, h_scr):
        x_ref, b_ref, c_ref, rawc_ref, rawr_ref, bc_ref, br_ref, alc_ref, alr_ref = ins
        q = _ssd_pre(d, c, rawc_ref[...], rawr_ref[...], bc_ref[...], br_ref[...], alc_ref[...], alr_ref[...])
        bv = b_ref[...].astype(BF16)
        cv = c_ref[...].astype(BF16)
        cb = _dot_nt(cv, bv)
        lo = _lane_lo()
        for j in range(PAIRS_PER_GROUP):
            xp = x_ref[:, j * CHUNK:(j + 1) * CHUNK]
            xd = xp * _pair_cols(q["dtc"], j)
            xdb = xd.astype(BF16)
            yi = []
            for e in range(2):
                h = 2 * j + e
                lm = jnp.exp(jnp.where(q["mask"], q["alpc"][:, h:h + 1] - q["alpr"][h:h + 1, :], -jnp.inf))
                yi.append(_dot((cb * lm).astype(BF16), xdb))
            alp = _pair_cols(q["alpc"], j)
            hp = h_scr[j]
            hs_ref[j] = hp
            yo = jnp.exp(alp) * _dot_nt(cv, hp.astype(BF16))
            y_ref[:, j * CHUNK:(j + 1) * CHUNK] = (jnp.where(lo, yi[0], yi[1]) + yo).astype(y_ref.dtype)
            de = jnp.exp(_pair_cols(q["endc"], j) - alp)
            h_scr[j] = jnp.exp(_pair_rows(q["endr"], j)) * hp + _dot_tn((xd * de).astype(BF16), bv)

    def body(*refs):
        n = pl.program_id(1)
        ins, (y_f, y_b, hs_f, hs_b, h_scr) = refs[:2 * N_SSD_IN], refs[2 * N_SSD_IN:]

        @pl.when(n == 0)
        def _():
            h_scr[...] = jnp.zeros_like(h_scr)

        one_direction(0, n, ins[:N_SSD_IN], y_f, hs_f, h_scr.at[0])
        one_direction(1, n, ins[N_SSD_IN:], y_b, hs_b, h_scr.at[1])

    y_spec = lambda d: pl.BlockSpec((rows, GROUP_WIDTH), lambda g, n: (cfn(d, n), g))
    hs_spec = lambda d: pl.BlockSpec((None, cps, PAIRS_PER_GROUP, CHUNK, SSD_STATE),
                                     lambda g, n: (g, cfn(d, n), 0, 0, 0))
    y_shape = jax.ShapeDtypeStruct((T, SSD_HEADS * SSD_HEAD_DIM), BF16)
    hs_shape = jax.ShapeDtypeStruct((SSD_GROUPS, nc, PAIRS_PER_GROUP, CHUNK, SSD_STATE), F32)
    y_f, y_b, hs_f, hs_b = pl.pallas_call(
        body, name="ssd_fwd", grid=(SSD_GROUPS, nc // cps),
        in_specs=_ssd_in_specs(0, cfn, rows) + _ssd_in_specs(1, cfn, rows),
        out_specs=[y_spec(0), y_spec(1), hs_spec(0), hs_spec(1)],
        out_shape=[y_shape, y_shape, hs_shape, hs_shape],
        scratch_shapes=[pltpu.VMEM((2, PAIRS_PER_GROUP, CHUNK, SSD_STATE), F32)],
        compiler_params=_params(("arbitrary", "arbitrary")),
    )(xs, bm, cm, *small, xs, bm, cm, *small)
    return (y_f, y_b), (hs_f, hs_b)


def _ssd_bwd(xs, bm, cm, small, hs, dy, T):
    nc = T // CHUNK
    cps = _chunks_per_step(nc, 11)
    rows = cps * CHUNK
    cfn = lambda d, n: _chunk_of(1 - d, n, nc // cps)

    def one_direction(d, n, ins, outs, dh_scr):
        x_ref, b_ref, c_ref, rawc_ref, rawr_ref, bc_ref, br_ref, alc_ref, alr_ref, hs_ref, dy_ref = ins
        dx_ref, db_ref, dc_ref, draw_ref, dbias_ref, dalog_ref = outs
        for kk in range(cps):
            k = cps - 1 - kk if d == 0 else kk
            r = pl.ds(k * CHUNK, CHUNK)
            one_chunk(d, cfn(d, n) * cps + k, n if kk == 0 else None,
                      (x_ref.at[r], b_ref.at[r], c_ref.at[r], rawc_ref.at[r], rawr_ref.at[:, r], bc_ref, br_ref,
                       alc_ref, alr_ref, hs_ref.at[k], dy_ref.at[r]),
                      (dx_ref.at[r], db_ref.at[r], dc_ref.at[r], draw_ref.at[r], dbias_ref, dalog_ref), dh_scr)

    def one_chunk(d, c, first_of_step, ins, outs, dh_scr):
        x_ref, b_ref, c_ref, rawc_ref, rawr_ref, bc_ref, br_ref, alc_ref, alr_ref, hs_ref, dy_ref = ins
        dx_ref, db_ref, dc_ref, draw_ref, dbias_ref, dalog_ref = outs
        rawc, bc = rawc_ref[...], bc_ref[...]
        q = _ssd_pre(d, c, rawc, rawr_ref[...], bc, br_ref[...], alc_ref[...], alr_ref[...])
        b32, c32 = b_ref[...], c_ref[...]
        bv, cv = b32.astype(BF16), c32.astype(BF16)
        cb = _dot_nt(cv, bv)
        cbt = _dot_nt(bv, cv)
        lo = _lane_lo()
        row_lo = lax.broadcasted_iota(jnp.int32, (CHUNK, 1), 0) < SSD_HEAD_DIM
        dcb = jnp.zeros((CHUNK, CHUNK), F32)
        dcp = jnp.zeros((CHUNK, SSD_STATE), F32)
        dbp = jnp.zeros((CHUNK, SSD_STATE), F32)
        dalp = jnp.zeros((CHUNK, HEADS_PER_GROUP), F32)
        dend = jnp.zeros((1, HEADS_PER_GROUP), F32)
        ddtx = jnp.zeros((CHUNK, HEADS_PER_GROUP), F32)

        def half_sums(t):
            return (jnp.sum(jnp.where(lo, t, 0.0), axis=1, keepdims=True),
                    jnp.sum(jnp.where(lo, 0.0, t), axis=1, keepdims=True))

        for j in range(PAIRS_PER_GROUP):
            xp = x_ref[:, j * CHUNK:(j + 1) * CHUNK]
            dtp = _pair_cols(q["dtc"], j)
            xd = xp * dtp
            xdb = xd.astype(BF16)
            dyp = dy_ref[:, j * CHUNK:(j + 1) * CHUNK]
            dyb = dyp.astype(BF16)
            hn = hs_ref[j]
            hnb = hn.astype(BF16)
            dh1 = dh_scr[j]
            dh1b = dh1.astype(BF16)
            alp = _pair_cols(q["alpc"], j)
            ea = jnp.exp(alp)
            de = jnp.exp(_pair_cols(q["endc"], j) - alp)
            dxi = []
            for e in range(2):
                h = 2 * j + e
                diff = q["alpc"][:, h:h + 1] - q["alpr"][h:h + 1, :]
                lm = jnp.exp(jnp.where(q["mask"], diff, -jnp.inf))
                mt = cbt * jnp.exp(jnp.where(q["mask_t"], -diff, -jnp.inf))
                dxi.append(_dot(mt.astype(BF16), dyb))
                dyeb_h = (jnp.where(lo, dyp, 0.0) if e == 0 else jnp.where(lo, 0.0, dyp)).astype(BF16)
                gl = _dot_nt(dyeb_h, xdb) * lm
                dcb = dcb + gl
                ra = jnp.sum(gl * cb - _dot_nt(xdb, dyeb_h) * mt, axis=1, keepdims=True)
                dalp = dalp + ra * _onehot8(h)
            y_off = ea * _dot_nt(cv, hnb)
            dxs_state = de * _dot_nt(bv, dh1b)
            dxd = jnp.where(lo, dxi[0], dxi[1]) + dxs_state
            dyeb = (dyp * ea).astype(BF16)
            dcp = dcp + _dot(dyeb, hnb)
            dbp = dbp + _dot((xd * de).astype(BF16), dh1b)
            dh_scr[j] = jnp.exp(_pair_rows(q["endr"], j)) * dh1 + _dot_tn(dyeb, cv)
            r0, r1 = half_sums(dyp * y_off - xd * dxs_state)
            dalp = dalp + r0 * _onehot8(2 * j) + r1 * _onehot8(2 * j + 1)
            t0, t1 = half_sums(jnp.sum(xd * dxs_state, axis=0, keepdims=True))
            u = dh1 * hn
            u0 = jnp.sum(jnp.sum(jnp.where(row_lo, u, 0.0), axis=0, keepdims=True), axis=1, keepdims=True)
            u1 = jnp.sum(jnp.sum(jnp.where(row_lo, 0.0, u), axis=0, keepdims=True), axis=1, keepdims=True)
            eend = jnp.exp(q["endc"])
            dend = dend + (t0 + eend * u0) * _onehot8(2 * j) + (t1 + eend * u1) * _onehot8(2 * j + 1)
            dx_ref[:, j * CHUNK:(j + 1) * CHUNK] = (dxd * dtp).astype(dx_ref.dtype)
            w0, w1 = half_sums(dxd * xp)
            ddtx = ddtx + w0 * _onehot8(2 * j) + w1 * _onehot8(2 * j + 1)

        dcbb = dcb.astype(BF16)
        dc_ref[...] = (dcp + _dot(dcbb, bv)).astype(dc_ref.dtype)
        db_ref[...] = (dbp + _dot_tn(dcbb, cv)).astype(db_ref.dtype)
        ddl = _dot(q["mask_t"].astype(F32), dalp, precision=HIGHEST) + dend
        ddt = ddl * q["ac"] + ddtx
        draw = jnp.where(q["valid"], ddt * jax.nn.sigmoid(rawc + bc), 0.0)
        draw_ref[...] = draw
        dbias = jnp.sum(draw, axis=0, keepdims=True)
        dalog = jnp.sum(ddl * q["dtc"], axis=0, keepdims=True) * q["ac"]

        def add():
            dbias_ref[...] += dbias
            dalog_ref[...] += dalog

        if first_of_step is None:
            add()
        else:
            @pl.when(first_of_step == 0)
            def _():
                dbias_ref[...] = dbias
                dalog_ref[...] = dalog

            pl.when(first_of_step > 0)(add)

    n_in, n_out = N_SSD_IN + 2, 6

    def body(*refs):
        n = pl.program_id(1)
        ins, outs, dh_scr = refs[:2 * n_in], refs[2 * n_in:2 * (n_in + n_out)], refs[-1]

        @pl.when(n == 0)
        def _():
            dh_scr[...] = jnp.zeros_like(dh_scr)

        one_direction(0, n, ins[:n_in], outs[:n_out], dh_scr.at[0])
        one_direction(1, n, ins[n_in:], outs[n_out:], dh_scr.at[1])

    def in_specs(d):
        return _ssd_in_specs(d, cfn, rows) + [
            pl.BlockSpec((None, cps, PAIRS_PER_GROUP, CHUNK, SSD_STATE), lambda g, n: (g, cfn(d, n), 0, 0, 0)),
            pl.BlockSpec((rows, GROUP_WIDTH), lambda g, n: (cfn(d, n), g))]

    def out_specs(d):
        acc = pl.BlockSpec((None, 1, HEADS_PER_GROUP), lambda g, n: (g, 0, 0))
        return [pl.BlockSpec((rows, GROUP_WIDTH), lambda g, n: (cfn(d, n), g)),
                pl.BlockSpec((rows, SSD_STATE), lambda g, n: (cfn(d, n), g)),
                pl.BlockSpec((rows, SSD_STATE), lambda g, n: (cfn(d, n), g)),
                pl.BlockSpec((None, rows, HEADS_PER_GROUP), lambda g, n: (g, cfn(d, n), 0)), acc, acc]

    out_shape = [jax.ShapeDtypeStruct((T, SSD_HEADS * SSD_HEAD_DIM), BF16),
                 jax.ShapeDtypeStruct((T, SSD_GROUPS * SSD_STATE), BF16),
                 jax.ShapeDtypeStruct((T, SSD_GROUPS * SSD_STATE), BF16),
                 jax.ShapeDtypeStruct((SSD_GROUPS, T, HEADS_PER_GROUP), F32),
                 jax.ShapeDtypeStruct((SSD_GROUPS, 1, HEADS_PER_GROUP), F32),
                 jax.ShapeDtypeStruct((SSD_GROUPS, 1, HEADS_PER_GROUP), F32)]
    res = pl.pallas_call(
        body, name="ssd_bwd", grid=(SSD_GROUPS, nc // cps),
        in_specs=in_specs(0) + in_specs(1), out_specs=out_specs(0) + out_specs(1), out_shape=out_shape * 2,
        scratch_shapes=[pltpu.VMEM((2, PAIRS_PER_GROUP, CHUNK, SSD_STATE), F32)],
        compiler_params=_params(("arbitrary", "arbitrary")),
    )(xs, bm, cm, *small, hs[0], dy, xs, bm, cm, *small, hs[1], dy)
    return [(res[k], res[n_out + k]) for k in range(n_out)]


def _rot(x, cs, sn):
    return x * cs + pltpu.roll(x, RET_QK_DIM // 2, 1) * sn


def _rot_t(d, cs, sn):
    return d * cs + pltpu.roll(d * sn, RET_QK_DIM // 2, 1)


def _ret_post(y, g, w):
    parts = []
    for h in range(RET_HEADS):
        yh = y[:, h * RET_V_DIM:(h + 1) * RET_V_DIM]
        mu = jnp.mean(yh, axis=-1, keepdims=True)
        var = jnp.mean(jnp.square(yh - mu), axis=-1, keepdims=True)
        parts.append((yh - mu) * lax.rsqrt(var + EPS))
    return _silu(g) * (jnp.concatenate(parts, axis=1) * w)


def _ssd_post(yf, yb, xs, z, dskip, w):
    y = (yf + yb + xs * dskip) * _silu(z)
    return y * lax.rsqrt(jnp.mean(y * y, axis=-1, keepdims=True) + EPS) * w


def _merge(gates, yr, ys, valid):
    m = jax.nn.sigmoid(gates[:, :D_MODEL]) * yr + jax.nn.sigmoid(gates[:, D_MODEL:]) * ys
    return jnp.where(valid, m, 0.0)


def _rope_tables(T):
    half = RET_QK_DIM // 2
    inv = ROPE_BASE ** (-jnp.arange(half, dtype=F32) / half)
    pos = (jnp.arange(T) - PAD_ROWS).astype(F32)
    ang = pos[:, None] * inv[None, :]
    cos, sin = jnp.cos(ang), jnp.sin(ang)
    return jnp.concatenate([cos, cos], axis=1), jnp.concatenate([-sin, sin], axis=1)


def _per_group(v):
    c = v.reshape(SSD_GROUPS, 1, HEADS_PER_GROUP)
    return c, c.reshape(SSD_GROUPS, HEADS_PER_GROUP, 1)


def _local_step(x, target, w, tick, late_weights, early_grads, in_grads):
    S = x.shape[0]
    T = S + CHUNK
    tm = _tile_rows(T)
    c0 = _const(0)

    h0 = jnp.concatenate([jnp.zeros((PAD_ROWS, D_MODEL), F32), w["meta_tokens"], x], axis=0)
    seg_at = {name: a for name, a, _ in SEGMENTS}
    w_main = w["w_in_t"][:seg_at["dt"]]
    w_dt = jnp.pad(w["w_in_t"][seg_at["dt"]:seg_at["gates"]], ((0, CHUNK - 2 * SSD_HEADS), (0, 0)))
    w_gates = w["w_in_t"][seg_at["gates"]:]

    def norm_cast(name, h, nw):
        return _rows(name, lambda i, hv, wv: (_rms(hv, wv),), T, 1, [(h, D_MODEL, c0)], [(nw, D_MODEL, c0)],
                     [(D_MODEL, D_MODEL, c0, BF16)], tall=True)[0]

    u = norm_cast("norm_mix", h0, w["norm_mix_w"] + tick)
    p_main = _mm("proj_main", u, w_main, "nt", out_dtype=BF16)
    p_dt = _mm("proj_dt", u, w_dt, "nt")
    p_gates = _mm("proj_gates", u, w_gates, "nt", out_dtype=BF16)

    def seg(name, width, cf=c0):
        base = seg_at[name] // width
        return (p_main, width, lambda j: base + cf(j))

    cs, sn = _rope_tables(T)
    scale = RET_QK_DIM ** -0.5

    def rot_fn(i, qk, csv, snv):
        q = [_rot(qk[:, h * 128:(h + 1) * 128], csv, snv) for h in range(RET_HEADS)]
        k = [_rot(qk[:, (RET_HEADS + h) * 128:(RET_HEADS + h + 1) * 128], csv, snv) * scale for h in range(RET_HEADS)]
        return jnp.concatenate(q, axis=1), jnp.concatenate(k, axis=1)

    qr, kr = _rows("rotary", rot_fn, T, 1, [seg("qk", 1024), (cs, 128, c0), (sn, 128, c0)], [],
                   [(512, 512, c0, F32), (512, 512, c0, F32)], tall=True)
    v_at = (p_main, seg_at["v"])
    y_ret = _retention("retention", qr, kr, v_at, T, RET_QK_DIM, RET_V_DIM)
    a_ret = _rows("ret_post", lambda i, y, g, gw: (_ret_post(y, g, gw),), T, 1,
                  [(y_ret, 1024, c0), seg("g", 1024)], [(w["ret_gn_w"], 1024, c0)],
                  [(1024, 1024, c0, BF16)], tall=True)[0]

    conv_w = {"xs": w["w_ssd_conv"][:, :2048], "B": w["w_ssd_conv"][:, 2048:2560], "C": w["w_ssd_conv"][:, 2560:]}
    conv_b = {"xs": w["b_ssd_conv"][:, :2048], "B": w["b_ssd_conv"][:, 2048:2560], "C": w["b_ssd_conv"][:, 2560:]}

    def ssd_conv_fn(i, xe, cw, cb):
        r = _row_ids(i, T, True)
        return (_center(jnp.where(r >= PAD_ROWS, _silu(_conv3(xe, cw) + cb), 0.0)),)

    act = {}
    for name in ("xs", "B", "C"):
        wd = conv_w[name].shape[1]
        cw = 512
        act[name] = _rows("ssd_conv_" + name, ssd_conv_fn, T, wd // cw, [seg(name, cw, lambda j: j)],
                          [(conv_w[name], cw, lambda j: j), (conv_b[name], cw, lambda j: j)],
                          [(wd, cw, lambda j: j, BF16)], halo=True)[0]

    raw = p_dt[:, :2 * SSD_HEADS].reshape(T, 2, SSD_GROUPS, HEADS_PER_GROUP)
    rawc = raw.transpose(1, 2, 0, 3)
    rawr = raw.transpose(1, 2, 3, 0)
    bias = [_per_group(w["dt_bias_f"]), _per_group(w["dt_bias_b"])]
    alog = [_per_group(w["a_log_f"]), _per_group(w["a_log_b"])]
    small = (rawc, rawr, jnp.stack([bias[0][0], bias[1][0]]), jnp.stack([bias[0][1], bias[1][1]]),
             jnp.stack([alog[0][0], alog[1][0]]), jnp.stack([alog[0][1], alog[1][1]]))
    y_dir, states = _ssd_fwd(act["xs"], act["B"], act["C"], small, T)

    dskip_e = jnp.repeat(w["d_skip"], SSD_HEAD_DIM, axis=1)
    gcol = lambda j: j
    gw_ = 512
    a_ssd = _rows("ssd_post", lambda i, yf, yb, xv, zv, dk, nw: (_ssd_post(yf, yb, xv, zv, dk, nw),), T, SSD_GROUPS,
                  [(y_dir[0], gw_, gcol), (y_dir[1], gw_, gcol), (act["xs"], gw_, gcol), seg("z", gw_, gcol)],
                  [(dskip_e, gw_, gcol), (w["ssd_norm_w"], gw_, gcol)], [(2048, gw_, gcol, BF16)])[0]

    w = dict(w, **late_weights(a_ssd))
    w_up_g, w_up_u = w["w_ffn_up_t"][:D_FF], w["w_ffn_up_t"][D_FF:]
    y_ret_o = _mm("ret_out", a_ret, w["w_ret_out"], "nn", out_dtype=BF16)
    y_ssd_o = _mm("ssd_out", a_ssd, w["w_ssd_out"], "nn", out_dtype=BF16)

    def merge_fn(i, gates, yr, ys):
        return (_merge(gates, yr, ys, _row_ids(i, T) >= PAD_ROWS),)

    merged = _rows("merge", merge_fn, T, 1, [(p_gates, 2048, c0), (y_ret_o, 1024, c0), (y_ssd_o, 1024, c0)], [],
                   [(1024, 1024, c0, BF16)])[0]
    h1 = _mm("mix_out", merged, w["w_out"], "nn", add=h0)

    n2 = norm_cast("norm_ffn", h1, w["norm_ffn_w"])
    f_pre = _mm("ffn_up", n2, w["w_ffn_up_t"], "nt", out_dtype=BF16)
    cwg, cwu = w["w_ffn_conv"][:, :D_FF], w["w_ffn_conv"][:, D_FF:]
    cbg, cbu = w["b_ffn_conv"][:, :D_FF], w["b_ffn_conv"][:, D_FF:]
    fcol = lambda j: j
    fw = 1408

    def ffn_act_fn(i, ge, ue, wg, wu, bg, bu):
        return (_center(_silu(_conv3(ge, wg) + bg) * (_conv3(ue, wu) + bu)),)

    ucol = lambda j: D_FF // fw + j
    a2 = _rows("ffn_act", ffn_act_fn, T, D_FF // fw, [(f_pre, fw, fcol), (f_pre, fw, ucol)],
               [(cwg, fw, fcol), (cwu, fw, fcol), (cbg, fw, fcol), (cbu, fw, fcol)], [(D_FF, fw, fcol, BF16)],
               halo=True)[0]
    h2 = _mm("ffn_down", a2, w["w_ffn_down"], "nn", add=h1)

    fnw = w["final_norm_w"].reshape(1, D_MODEL)

    per_tile = tm // CHUNK
    tgt_specs = [(target, D_MODEL, c0, None, (CHUNK, lambda i, k=k: jnp.maximum(per_tile * i - 1 + k, 0)))
                 for k in range(per_tile)]

    def loss_fn(i, hv, *rest):
        tv, nw = jnp.concatenate(rest[:per_tile], axis=0), rest[per_tile]
        valid = _row_ids(i, T) >= CHUNK
        y, vjp = jax.vjp(_rms, hv, nw)
        diff = jnp.where(valid, y - tv, 0.0)
        dh, dw = vjp(diff * (1.0 / D_MODEL))
        part = 0.5 / D_MODEL * jnp.sum(jnp.sum(diff * diff, axis=1, keepdims=True), axis=0, keepdims=True)
        return dh, jnp.broadcast_to(part, (1, 128)), dw

    dh2, loss_acc, d_fnw = _rows("loss", loss_fn, T, 1, [(h2, D_MODEL, c0)] + tgt_specs, [(fnw, D_MODEL, c0)],
                                 [(D_MODEL, D_MODEL, c0, F32)], [(1, 128, 128, c0), (1, D_MODEL, D_MODEL, c0)])
    loss = loss_acc[0, 0]
    grads = {"final_norm_w": d_fnw.reshape(D_MODEL)}

    da2 = _mm("d_ffn_act", dh2, w["w_ffn_down"], "nt", out_dtype=BF16)
    grads["w_ffn_down"] = _mm("g_ffn_down", a2, dh2, "tn", out_dtype=BF16)

    def ffn_bwd_fn(i, ge, ue, de, wg, wu, bg, bu):
        fg = _conv3(ge, wg) + bg
        fu = _conv3(ue, wu) + bu
        sg = jax.nn.sigmoid(fg)
        dfg = de * fu * (sg * (1.0 + fg * (1.0 - sg)))
        dfu = de * (fg * sg)
        n = ge.shape[0]

        def wgrad(df, xe):
            df_c = _center(df)
            return jnp.concatenate([jnp.sum(df_c * _center(pltpu.roll(xe, 1, 0)), axis=0, keepdims=True),
                                    jnp.sum(df_c * _center(xe), axis=0, keepdims=True),
                                    jnp.sum(df_c * _center(pltpu.roll(xe, n - 1, 0)), axis=0, keepdims=True)], axis=0)

        return (_center(_conv3_t(dfg, wg)), _center(_conv3_t(dfu, wu)), wgrad(dfg, ge), wgrad(dfu, ue),
                jnp.sum(_center(dfg), axis=0, keepdims=True), jnp.sum(_center(dfu), axis=0, keepdims=True))

    dfg_pre, dfu_pre, g_cwg, g_cwu, g_cbg, g_cbu = _rows(
        "ffn_act_bwd", ffn_bwd_fn, T, D_FF // fw, [(f_pre, fw, fcol), (f_pre, fw, ucol), (da2, fw, fcol)],
        [(cwg, fw, fcol), (cwu, fw, fcol), (cbg, fw, fcol), (cbu, fw, fcol)],
        [(D_FF, fw, fcol, BF16), (D_FF, fw, fcol, BF16)],
        [(3, D_FF, fw, fcol), (3, D_FF, fw, fcol), (1, D_FF, fw, fcol), (1, D_FF, fw, fcol)], halo=True)
    grads["w_ffn_conv"] = jnp.concatenate([g_cwg, g_cwu], axis=1)
    grads["b_ffn_conv"] = jnp.concatenate([g_cbg, g_cbu], axis=1)
    dn2 = _mm("d_norm_ffn_g", dfg_pre, w_up_g, "nn")
    dn2 = _mm("d_norm_ffn_u", dfu_pre, w_up_u, "nn", add=dn2)
    grads["w_ffn_up_t"] = jnp.concatenate([_mm("g_ffn_up_g", dfg_pre, n2, "tn", out_dtype=BF16), _mm("g_ffn_up_u", dfu_pre, n2, "tn", out_dtype=BF16)],
                                          axis=0)

    def norm_bwd(name, h, nw, dn, dres):
        def fn(i, hv, dnv, drv, wv):
            _, vjp = jax.vjp(_rms, hv, wv)
            dh, dw = vjp(dnv)
            return dh + drv, dw
        return _rows(name, fn, T, 1, [(h, D_MODEL, c0), (dn, D_MODEL, c0), (dres, D_MODEL, c0)], [(nw, D_MODEL, c0)],
                     [(D_MODEL, D_MODEL, c0, F32)], [(1, D_MODEL, D_MODEL, c0)])

    dh1, grads["norm_ffn_w"] = norm_bwd("norm_ffn_bwd", h1, w["norm_ffn_w"], dn2, dh2)

    dmerged = _mm("d_merged", dh1, w["w_out"], "nt", out_dtype=BF16)
    grads["w_out"] = _mm("g_out", merged, dh1, "tn", out_dtype=BF16)

    def merge_bwd_fn(i, gates, yr, ys, dm):
        valid = _row_ids(i, T) >= PAD_ROWS
        _, vjp = jax.vjp(lambda a, b, c: _merge(a, b, c, valid), gates, yr, ys)
        return vjp(dm)

    dgates, dyr, dys = _rows("merge_bwd", merge_bwd_fn, T, 1,
                             [(p_gates, 2048, c0), (y_ret_o, 1024, c0), (y_ssd_o, 1024, c0), (dmerged, 1024, c0)],
                             [], [(2048, 2048, c0, BF16), (1024, 1024, c0, BF16), (1024, 1024, c0, BF16)])
    dproj = {"gates": dgates}

    da_ssd = _mm("d_ssd_act", dys, w["w_ssd_out"], "nt", out_dtype=BF16)
    grads["w_ssd_out"] = _mm("g_ssd_out", a_ssd, dys, "tn", out_dtype=BF16)

    def ssd_post_bwd_fn(i, yf, yb, xv, zv, da, dk, nw):
        _, vjp = jax.vjp(_ssd_post, yf, yb, xv, zv, dk, nw)
        dyf, _, dxv, dzv, ddk, dnw = vjp(da)
        return dyf, dxv, dzv, ddk, dnw

    d_main = lax.empty(p_main.shape, BF16)

    def into_main(name, width, cf=c0):
        base = seg_at[name] // width
        return (d_main, width, lambda j: base + cf(j), BF16)

    dy_ssd, dxs_skip, d_main, g_dskip_e, grads["ssd_norm_w"] = _rows(
        "ssd_post_bwd", ssd_post_bwd_fn, T, SSD_GROUPS,
        [(y_dir[0], gw_, gcol), (y_dir[1], gw_, gcol), (act["xs"], gw_, gcol), seg("z", gw_, gcol),
         (da_ssd, gw_, gcol)],
        [(dskip_e, gw_, gcol), (w["ssd_norm_w"], gw_, gcol)],
        [(2048, gw_, gcol, BF16), (2048, gw_, gcol, BF16), into_main("z", gw_, gcol)],
        [(1, 2048, gw_, gcol), (1, 2048, gw_, gcol)])
    grads["d_skip"] = g_dskip_e.reshape(SSD_HEADS, SSD_HEAD_DIM).sum(axis=1).reshape(1, SSD_HEADS)

    dxs_dir, db_dir, dc_dir, draw, g_bias, g_alog = _ssd_bwd(act["xs"], act["B"], act["C"], small, states, dy_ssd, T)
    grads["dt_bias_f"], grads["dt_bias_b"] = g_bias[0].reshape(1, SSD_HEADS), g_bias[1].reshape(1, SSD_HEADS)
    grads["a_log_f"], grads["a_log_b"] = g_alog[0].reshape(1, SSD_HEADS), g_alog[1].reshape(1, SSD_HEADS)
    d_dt = jnp.stack(draw).transpose(2, 0, 1, 3).reshape(T, 2 * SSD_HEADS)
    dproj["dt"] = jnp.pad(d_dt, ((0, 0), (0, CHUNK - 2 * SSD_HEADS))).astype(BF16)

    def make_conv_bwd(nsum):
        def fn(i, xe, *rest):
            ds, (cw, cb) = rest[:nsum], rest[nsum:]
            r = _row_ids(i, T, True)
            dact = ds[0]
            for t in ds[1:]:
                dact = dact + t
            dact = jnp.where(r >= PAD_ROWS, dact, 0.0)
            pre = _conv3(xe, cw) + cb
            sg = jax.nn.sigmoid(pre)
            dpre = dact * (sg * (1.0 + pre * (1.0 - sg)))
            n = xe.shape[0]
            dpc = _center(dpre)
            dw = jnp.concatenate([jnp.sum(dpc * _center(pltpu.roll(xe, 1, 0)), axis=0, keepdims=True),
                                  jnp.sum(dpc * _center(xe), axis=0, keepdims=True),
                                  jnp.sum(dpc * _center(pltpu.roll(xe, n - 1, 0)), axis=0, keepdims=True)], axis=0)
            return _center(_conv3_t(dpre, cw)), dw, jnp.sum(dpc, axis=0, keepdims=True)
        return fn

    g_cw, g_cb = {}, {}
    cots = {"xs": [(dxs_dir[0], 512, gcol), (dxs_dir[1], 512, gcol), (dxs_skip, 512, gcol)],
            "B": [(db_dir[0], 512, gcol), (db_dir[1], 512, gcol)],
            "C": [(dc_dir[0], 512, gcol), (dc_dir[1], 512, gcol)]}
    for name in ("xs", "B", "C"):
        wd = conv_w[name].shape[1]
        d_main, g_cw[name], g_cb[name] = _rows(
            "ssd_conv_bwd_" + name, make_conv_bwd(len(cots[name])), T, wd // 512,
            [seg(name, 512, gcol)] + cots[name], [(conv_w[name], 512, gcol), (conv_b[name], 512, gcol)],
            [into_main(name, 512, gcol)], [(3, wd, 512, gcol), (1, wd, 512, gcol)], halo=True)
    grads["w_ssd_conv"] = jnp.concatenate([g_cw["xs"], g_cw["B"], g_cw["C"]], axis=1)
    grads["b_ssd_conv"] = jnp.concatenate([g_cb["xs"], g_cb["B"], g_cb["C"]], axis=1)

    da_ret = _mm("d_ret_act", dyr, w["w_ret_out"], "nt", out_dtype=BF16)
    grads["w_ret_out"] = _mm("g_ret_out", a_ret, dyr, "tn", out_dtype=BF16)
    tick = early_grads({n: grads.pop(n) for n in ("w_ffn_up_t", "w_ret_out", "w_ssd_out", "w_out", "w_ffn_down")})

    def ret_post_bwd_fn(i, y, g, da, gw):
        _, vjp = jax.vjp(_ret_post, y, g, gw)
        return vjp(da)

    dy_ret, d_main, grads["ret_gn_w"] = _rows(
        "ret_post_bwd", ret_post_bwd_fn, T, 1, [(y_ret, 1024, c0), seg("g", 1024), (da_ret, 1024, c0)],
        [(w["ret_gn_w"] + tick, 1024, c0)], [(1024, 1024, c0, BF16), into_main("g", 1024)], [(1, 1024, 1024, c0)])
    d_main = _retention("retention_dv", kr, qr, dy_ret, T, RET_QK_DIM, RET_V_DIM, into=(d_main, seg_at["v"]))
    dqr = _retention("retention_dq", dy_ret, v_at, kr, T, RET_V_DIM, RET_QK_DIM)
    dkr = _retention("retention_dk", v_at, dy_ret, qr, T, RET_V_DIM, RET_QK_DIM)

    def rot_bwd_fn(i, dq, dk, csv, snv):
        parts = [_rot_t(dq[:, h * 128:(h + 1) * 128], csv, snv) for h in range(RET_HEADS)]
        parts += [_rot_t(dk[:, h * 128:(h + 1) * 128] * scale, csv, snv) for h in range(RET_HEADS)]
        return (jnp.concatenate(parts, axis=1),)

    d_main = _rows("rotary_bwd", rot_bwd_fn, T, 1, [(dqr, 512, c0), (dkr, 512, c0), (cs, 128, c0), (sn, 128, c0)],
                   [], [into_main("qk", 1024)], tall=True)[0]

    g_in = [_mm("g_in_main", d_main, u, "tn", out_dtype=BF16),
            _mm("g_in_dt", dproj["dt"], u, "tn", out_dtype=BF16)[:2 * SSD_HEADS],
            _mm("g_in_gates", dproj["gates"], u, "tn", out_dtype=BF16)]
    tick = in_grads(jnp.concatenate(g_in, axis=0))
    du = _mm("d_u_dt", dproj["dt"] + tick.astype(BF16), w_dt, "nn")
    du = _mm("d_u_main", d_main, w_main, "nn", add=du)
    du = _mm("d_u_gates", dproj["gates"], w_gates, "nn", add=du)
    dh0, grads["norm_mix_w"] = norm_bwd("norm_mix_bwd", h0, w["norm_mix_w"], du, dh1)
    grads["meta_tokens"] = dh0[PAD_ROWS:CHUNK]
    return loss, dh0[CHUNK:], grads


MESH_ID = pl.DeviceIdType.MESH
ANY = pl.BlockSpec(memory_space=pl.ANY)


def _me_and_peers():
    x, y, c = lax.axis_index("x"), lax.axis_index("y"), lax.axis_index("c")
    peers = []
    for k in range(1, N_DEV):
        px = 1 - x if k & 4 else x
        py = 1 - y if k & 2 else y
        pc = 1 - c if k & 1 else c
        peers.append(((px, py, pc), 4 * px + 2 * py + pc))
    return 4 * x + 2 * y + c, peers


def _push_blocks(name, src, per_peer):
    blk = src.shape[1:] if per_peer else src.shape

    def body(src_ref, out_ref, send_sems, recv_sems, local_sem):
        me, peers = _me_and_peers()
        mine = src_ref.at[me] if per_peer else src_ref
        local = pltpu.make_async_copy(mine, out_ref.at[me], local_sem)
        local.start()
        sends = []
        for k, (dev, idx) in enumerate(peers):
            cp = pltpu.make_async_remote_copy(
                src_ref=src_ref.at[idx] if per_peer else src_ref, dst_ref=out_ref.at[me],
                send_sem=send_sems.at[k], recv_sem=recv_sems.at[k], device_id=dev, device_id_type=MESH_ID)
            cp.start()
            sends.append(cp)
        for k, (dev, idx) in enumerate(peers):
            pltpu.make_async_remote_copy(
                src_ref=mine, dst_ref=out_ref.at[idx], send_sem=send_sems.at[k], recv_sem=recv_sems.at[k],
                device_id=dev, device_id_type=MESH_ID).wait_recv()
        for cp in sends:
            cp.wait_send()
        local.wait()

    return pl.pallas_call(
        body, name=name, in_specs=[ANY], out_specs=ANY,
        out_shape=jax.ShapeDtypeStruct((N_DEV,) + tuple(blk), src.dtype),
        scratch_shapes=[pltpu.SemaphoreType.DMA((N_DEV - 1,)), pltpu.SemaphoreType.DMA((N_DEV - 1,)),
                        pltpu.SemaphoreType.DMA],
    )(src)


def _gather_two_level(name, src):
    def body(x_ref, out_ref, send_sems, recv_sems, local_sem):
        x, y, c = lax.axis_index("x"), lax.axis_index("y"), lax.axis_index("c")
        me, sibling = (x, y, c), (x, y, 1 - c)
        chips = [(1 - x, y), (x, 1 - y), (1 - x, 1 - y)]

        def rows(px, py, pc):
            return out_ref.at[4 * px + 2 * py + pc]

        def copy(k, block, to, src_ref=None):
            return pltpu.make_async_remote_copy(
                src_ref=rows(*block) if src_ref is None else src_ref, dst_ref=rows(*block),
                send_sem=send_sems.at[k], recv_sem=recv_sems.at[k], device_id=to, device_id_type=MESH_ID)

        mine = pltpu.make_async_copy(x_ref, rows(*me), local_sem)
        mine.start()
        first = [copy(0, me, sibling, x_ref)] + [copy(1 + j, me, (*chip, c), x_ref) for j, chip in enumerate(chips)]
        for cp in first:
            cp.start()
        passed = [copy(4 + j, (*chip, c), sibling) for j, chip in enumerate(chips)]
        for j, chip in enumerate(chips):
            copy(1 + j, (*chip, c), me).wait_recv()
            passed[j].start()
        copy(0, sibling, me).wait_recv()
        for j, chip in enumerate(chips):
            copy(4 + j, (*chip, 1 - c), me).wait_recv()
        for cp in first + passed:
            cp.wait_send()
        mine.wait()

    return pl.pallas_call(
        body, name=name, in_specs=[ANY], out_specs=ANY,
        out_shape=jax.ShapeDtypeStruct((N_DEV,) + tuple(src.shape), src.dtype),
        scratch_shapes=[pltpu.SemaphoreType.DMA((N_DEV - 1,)), pltpu.SemaphoreType.DMA((N_DEV - 1,)),
                        pltpu.SemaphoreType.DMA],
    )(src)


HBM = pl.BlockSpec(memory_space=pltpu.HBM)
SEM = pl.BlockSpec(memory_space=pltpu.SEMAPHORE)
EFFECT = pltpu.SideEffectType.DATAFLOW_SIDE_EFFECTING


def _peer_copy(src_ref, land_ref, send_sems, recv_sems, per_peer, me, a, k, dev, idx, receiving):
    s = a * (N_DEV - 1) + k
    return pltpu.make_async_remote_copy(
        src_ref=src_ref.at[idx] if per_peer else src_ref, dst_ref=land_ref.at[idx if receiving else me],
        send_sem=send_sems.at[s], recv_sem=recv_sems.at[s], device_id=dev, device_id_type=MESH_ID)


def _push_start(name, srcs, per_peer):
    n = len(srcs)
    land_shapes = [(N_DEV,) + tuple(s.shape[1:] if per_peer else s.shape) for s in srcs]

    def body(*refs):
        src_refs, land_refs, send_sems, recv_sems, token = refs[:n], refs[n:2 * n], refs[2 * n], refs[2 * n + 1], refs[-1]
        me, peers = _me_and_peers()
        for a in range(n):
            for k, (dev, idx) in enumerate(peers):
                _peer_copy(src_refs[a], land_refs[a], send_sems, recv_sems, per_peer, me, a, k, dev, idx, False).start()
        token[...] = jnp.zeros_like(token)

    sems = pltpu.SemaphoreType.DMA((n * (N_DEV - 1),))
    res = pl.pallas_call(
        body, name=name,
        out_shape=(sems, sems, *[pltpu.HBM(s.shape, s.dtype) for s in srcs],
                   *[pltpu.HBM(ls, s.dtype) for ls, s in zip(land_shapes, srcs)], jax.ShapeDtypeStruct((8, 128), F32)),
        in_specs=(HBM,) * (2 * n), out_specs=(SEM, SEM) + (HBM,) * (2 * n) + (pl.BlockSpec(memory_space=pltpu.VMEM),),
        input_output_aliases={i: 2 + i for i in range(2 * n)},
        compiler_params=pltpu.CompilerParams(has_side_effects=EFFECT),
    )(*[pltpu.with_memory_space_constraint(s, pltpu.HBM) for s in srcs],
      *[pltpu.with_memory_space_constraint(lax.empty(ls, s.dtype), pltpu.HBM) for ls, s in zip(land_shapes, srcs)])
    return res[0], res[1], res[2:2 + n], res[2 + n:2 + 2 * n], res[-1]


def _push_wait(name, send_sems, recv_sems, srcs_thru, lands_thru, after, per_peer):
    n = len(srcs_thru)

    def body(*refs):
        src_refs, land_refs, send_sems, recv_sems = refs[:n], refs[n:2 * n], refs[2 * n], refs[2 * n + 1]
        me, peers = _me_and_peers()
        for a in range(n):
            for k, (dev, idx) in enumerate(peers):
                cp = _peer_copy(src_refs[a], land_refs[a], send_sems, recv_sems, per_peer, me, a, k, dev, idx, True)
                cp.wait_send()
                cp.wait_recv()

    both = list(srcs_thru) + list(lands_thru)
    res = pl.pallas_call(
        body, name=name, out_shape=tuple(pltpu.HBM(t.shape, t.dtype) for t in both),
        in_specs=(HBM,) * (2 * n) + (SEM, SEM, ANY), out_specs=(HBM,) * (2 * n),
        input_output_aliases={i: i for i in range(2 * n)},
        compiler_params=pltpu.CompilerParams(has_side_effects=EFFECT),
    )(*both, send_sems, recv_sems, after)
    return res[:n], res[n:]


def _sum_blocks(name, blocks):
    _, R, C = blocks.shape
    tc = next(t for t in (1024, 512, 256, 128) if C % t == 0 and (N_DEV * R * t * 2 <= 6 * 2 ** 20 or t == 128))

    def body(b_ref, o_ref):
        acc = b_ref[0].astype(F32)
        for k in range(1, N_DEV):
            acc = acc + b_ref[k].astype(F32)
        o_ref[...] = acc

    return pl.pallas_call(
        body, name=name, grid=(C // tc,), in_specs=[pl.BlockSpec((N_DEV, R, tc), lambda j: (0, 0, j))],
        out_specs=pl.BlockSpec((R, tc), lambda j: (0, j)), out_shape=jax.ShapeDtypeStruct((R, C), F32),
        compiler_params=_params(("arbitrary",)),
    )(blocks)


def _adamw(name, w, g, m, v):
    R, C = w.shape
    tr = R if R <= 512 else _pick(R, (256, 184, 176, 128, 8))
    spec = pl.BlockSpec((tr, C), lambda i: (i, 0))

    def body(w_ref, g_ref, m_ref, v_ref, d_ref, mo_ref, vo_ref):
        gv = g_ref[...]
        mn = ADAM_B1 * m_ref[...] + (1.0 - ADAM_B1) * gv
        vn = ADAM_B2 * v_ref[...] + (1.0 - ADAM_B2) * jnp.square(gv)
        m_hat = mn / (1.0 - ADAM_B1 ** ADAM_STEP)
        v_hat = vn / (1.0 - ADAM_B2 ** ADAM_STEP)
        d_ref[...] = -ADAM_LR * (m_hat / (jnp.sqrt(v_hat) + ADAM_EPS) + ADAM_WD * w_ref[...])
        mo_ref[...] = mn
        vo_ref[...] = vn

    return pl.pallas_call(
        body, name=name, grid=(R // tr,), in_specs=[spec] * 4, out_specs=[spec] * 3,
        out_shape=[jax.ShapeDtypeStruct((R, C), F32)] * 3, compiler_params=_params(("arbitrary",)),
    )(w, g, m, v)


WEIGHTS = ("meta_tokens", "norm_mix_w", "w_in", "ret_gn_w", "w_ret_out", "w_ssd_conv", "b_ssd_conv", "dt_bias_f",
           "dt_bias_b", "a_log_f", "a_log_b", "d_skip", "ssd_norm_w", "w_ssd_out", "w_out", "norm_ffn_w", "w_ffn_up",
           "w_ffn_conv", "b_ffn_conv", "w_ffn_down", "final_norm_w")
BIG = (("w_in", 1288, True), ("w_ffn_up", 704, True), ("w_ret_out", 128, False), ("w_ssd_out", 256, False),
       ("w_out", 128, False), ("w_ffn_down", 352, False))
REPLICATED = ("norm_mix_w", "ret_gn_w", "b_ssd_conv", "dt_bias_f", "dt_bias_b", "a_log_f", "a_log_b", "d_skip",
              "ssd_norm_w", "norm_ffn_w", "b_ffn_conv", "final_norm_w")
SMALL_SHARDED = (("meta_tokens", 16, 1024), ("w_ssd_conv", 3, 3072), ("w_ffn_conv", 3, 5632))


BIG_IN, BIG_REST = BIG[:1], BIG[1:]


def _pack_big(tree, group):
    parts = []
    for name, _, transposed in group:
        a = tree[name][0]
        parts.append(a.T if transposed else a)
    return jnp.concatenate(parts, axis=0)


def _unpack_big(slab, group):
    out, r0 = {}, 0
    for name, r, transposed in group:
        a = slab[r0:r0 + r]
        out[name] = (a.T if transposed else a)[None]
        r0 += r
    return out


def _pack_flat(arrays, rows):
    flat = jnp.concatenate([a.reshape(-1) for a in arrays])
    return jnp.pad(flat, (0, rows * D_MODEL - flat.shape[0])).reshape(rows, D_MODEL)


def _unpack_flat(slab, shapes):
    flat, out, o = slab.reshape(-1), [], 0
    for s in shapes:
        n = math.prod(s)
        out.append(flat[o:o + n].reshape(s))
        o += n
    return out


def kernel(x, meta_tokens, norm_mix_w, w_in, ret_gn_w, w_ret_out, w_ssd_conv, b_ssd_conv, dt_bias_f, dt_bias_b, a_log_f, a_log_b, d_skip, ssd_norm_w, w_ssd_out, w_out, norm_ffn_w, w_ffn_up, w_ffn_conv, b_ffn_conv, w_ffn_down, final_norm_w, loss_target, m_meta_tokens, m_norm_mix_w, m_w_in, m_ret_gn_w, m_w_ret_out, m_w_ssd_conv, m_b_ssd_conv, m_dt_bias_f, m_dt_bias_b, m_a_log_f, m_a_log_b, m_d_skip, m_ssd_norm_w, m_w_ssd_out, m_w_out, m_norm_ffn_w, m_w_ffn_up, m_w_ffn_conv, m_b_ffn_conv, m_w_ffn_down, m_final_norm_w, v_meta_tokens, v_norm_mix_w, v_w_in, v_ret_gn_w, v_w_ret_out, v_w_ssd_conv, v_b_ssd_conv, v_dt_bias_f, v_dt_bias_b, v_a_log_f, v_a_log_b, v_d_skip, v_ssd_norm_w, v_w_ssd_out, v_w_out, v_norm_ffn_w, v_w_ffn_up, v_w_ffn_conv, v_b_ffn_conv, v_w_ffn_down, v_final_norm_w):
    given = dict(locals())
    wt = {n: given[n] for n in WEIGHTS}
    mt = {n: given["m_" + n] for n in WEIGHTS}
    vt = {n: given["v_" + n] for n in WEIGHTS}
    me = 4 * lax.axis_index("x") + 2 * lax.axis_index("y") + lax.axis_index("c")

    small_names = [n for n, _, _ in SMALL_SHARDED]
    small_local = lambda tree: [tree[n].reshape(r, c // N_DEV) for n, r, c in SMALL_SHARDED]
    all_in = _gather_two_level("gather_w_in", _pack_big(wt, BIG_IN).astype(BF16))
    all_s = _push_blocks("gather_small", _pack_flat(small_local(wt), 8), False)
    slab_view = lambda tree, name, transposed: tree[name][0].T if transposed else tree[name][0]
    rest_srcs = [slab_view(wt, name, t).astype(BF16) for name, _, t in BIG_REST]
    rest_srcs, all_in, all_s = lax.optimization_barrier((rest_srcs, all_in, all_s))
    rest_flight = _push_start("gather_rest_start", rest_srcs, False)
    all_s = all_s.reshape(N_DEV, -1)
    full = {"w_in_t": all_in.reshape(-1, D_MODEL)}

    def lands_with_own(flight, after, per_peer, name):
        srcs, lands = _push_wait(name, *flight[:4], after, per_peer)
        own = lambda s: lax.dynamic_slice_in_dim(s, me, 1, axis=0) if per_peer else s[None]
        return [lax.dynamic_update_slice_in_dim(land, own(s), me, axis=0) for s, land in zip(srcs, lands)]

    def late_weights(after):
        lands = lands_with_own(rest_flight, after, False, "gather_rest_wait")
        return {name + ("_t" if t else ""): land.reshape(N_DEV * r, D_MODEL) for (name, r, t), land in zip(BIG_REST, lands)}

    flights = {}

    def start_exchange(key, group, gd):
        srcs = [gd[name + ("_t" if t else "")].astype(BF16).reshape(N_DEV, r, D_MODEL) for name, r, t in group]
        flights[key] = _push_start("exchange_" + key + "_start", srcs, True)
        return flights[key][4][0, 0]

    o = 0
    for name, r, c in SMALL_SHARDED:
        n = r * c // N_DEV
        full[name] = all_s[:, o:o + n].reshape(N_DEV, r, c // N_DEV).transpose(1, 0, 2).reshape(r, c)
        o += n
    for name in REPLICATED:
        full[name] = wt[name]

    grads, delta, new_m, new_v = {}, {}, {}, {}

    def finish_exchange(key, group, after):
        lands = lands_with_own(flights[key], after, True, "exchange_" + key + "_wait")
        for (name, _, transposed), land in zip(group, lands):
            back = (lambda a: a.T[None]) if transposed else (lambda a: a[None])
            g_sum = _sum_blocks("sum_" + name, land)
            d, mn, vn = _adamw("adamw_" + name, slab_view(wt, name, transposed), g_sum,
                               slab_view(mt, name, transposed), slab_view(vt, name, transposed))
            grads[name], delta[name], new_m[name], new_v[name] = back(g_sum), back(d), back(mn), back(vn)

    def in_grads(gi):
        tick = start_exchange("in", BIG_IN, {"w_in_t": gi})
        finish_exchange("rest", BIG_REST, flights["in"][4])
        tick, _ = lax.optimization_barrier((tick, [delta[name] for name, _, _ in BIG_REST]))
        return tick

    loss, grad_x, g = _local_step(x[0], loss_target[0], full, rest_flight[4][0, 0], late_weights,
                                  lambda gd: start_exchange("rest", BIG_REST, gd), in_grads)

    finish_exchange("in", BIG_IN, g["norm_mix_w"])
    small_parts = [g[n] for n in REPLICATED] + [g[n] for n in small_names] + [loss.reshape(1)]
    g_small = _sum_blocks("sum_small", _push_blocks("gather_small_grads", _pack_flat(small_parts, 64), False))
    small_red = _unpack_flat(g_small, [wt[n].shape for n in REPLICATED] + [(r, c) for _, r, c in SMALL_SHARDED] + [(1,)])
    grads.update(zip(REPLICATED, small_red[:len(REPLICATED)]))
    for (name, r, c), red in zip(SMALL_SHARDED, small_red[len(REPLICATED):-1]):
        grads[name] = lax.dynamic_slice(red, (0, me * (c // N_DEV)), (r, c // N_DEV)).reshape(wt[name].shape)
    loss_all = small_red[-1][0]

    rest = list(REPLICATED) + small_names
    shapes = [wt[n].shape for n in rest]
    pack_rest = lambda tree: _pack_flat([tree[n] for n in rest], 24)
    d_rest, m_rest, v_rest = _adamw("adamw_small", pack_rest(wt), pack_rest(grads), pack_rest(mt), pack_rest(vt))
    delta.update(zip(rest, _unpack_flat(d_rest, shapes)))
    new_m.update(zip(rest, _unpack_flat(m_rest, shapes)))
    new_v.update(zip(rest, _unpack_flat(v_rest, shapes)))

    return (loss_all, grad_x[None], *[grads[n] for n in WEIGHTS], *[delta[n] for n in WEIGHTS],
            *[new_m[n] for n in WEIGHTS], *[new_v[n] for n in WEIGHTS])
```

```python
import functools
import math

import jax
import jax.numpy as jnp
from jax import lax
from jax.experimental import pallas as pl
from jax.experimental.pallas import tpu as pltpu

F32 = jnp.float32
BF16 = jnp.bfloat16

D_MODEL = 1024
CHUNK = 128
N_META = 16
PAD_ROWS = CHUNK - N_META
RET_HEADS = 4
RET_QK_DIM = 128
RET_V_DIM = 256
SSD_HEADS = 32
SSD_HEAD_DIM = 64
SSD_GROUPS = 4
SSD_STATE = 128
HEADS_PER_GROUP = SSD_HEADS // SSD_GROUPS
PAIRS_PER_GROUP = HEADS_PER_GROUP // 2
D_FF = 2816
EPS = 1e-6
ROPE_BASE = 10000.0
N_DEV = 8

ADAM_LR = 0.001
ADAM_B1 = 0.9
ADAM_B2 = 0.999
ADAM_EPS = 1e-08
ADAM_WD = 0.01
ADAM_STEP = 10

VMEM_LIMIT = 56 * 1024 * 1024
HALO = 16
MM_VMEM_BUDGET = 46 * 1024 * 1024
HIGHEST = lax.Precision.HIGHEST

SEGMENTS = (("qk", 0, 1024), ("v", 1024, 2048), ("g", 2048, 3072), ("z", 3072, 5120), ("xs", 5120, 7168),
            ("B", 7168, 7680), ("C", 7680, 8192), ("dt", 8192, 8256), ("gates", 8256, 10304))


def _pick(n, cands):
    for c in cands:
        if n % c == 0:
            return c
    raise ValueError(f"no tile for {n}")


def _params(sem):
    return pltpu.CompilerParams(dimension_semantics=sem, vmem_limit_bytes=VMEM_LIMIT)


def _dot(a, b, dims=(((1,), (0,)), ((), ())), precision=None):
    return lax.dot_general(a, b, dims, preferred_element_type=F32, precision=precision)


def _dot_nt(a, b):
    return _dot(a, b, (((1,), (1,)), ((), ())))


def _dot_tn(a, b):
    return _dot(a, b, (((0,), (0,)), ((), ())))


def _mm(name, a, b, mode, add=None, out_dtype=F32):
    if mode == "nn":
        (M, K), N = a.shape, b.shape[1]
    elif mode == "nt":
        (M, K), N = a.shape, b.shape[0]
    else:
        (K, M), N = a.shape, b.shape[1]
    tn = _pick(N, (1408, 1024, 512, 128, 64))
    if mode == "tn":
        tm = M if M <= 1024 else _pick(M, (1408, 1024))
        tk = _pick(K, (2112, 512, 256, 128))
    else:
        tk = K if K <= 2816 else _pick(K, (2048, 1408, 1024))

        def vmem_need(rows):
            need = 2 * (rows * tk * a.dtype.itemsize + tk * tn * b.dtype.itemsize)
            need += 2 * rows * tn * jnp.dtype(out_dtype).itemsize + (2 * rows * tn * 4 if add is not None else 0)
            need += rows * tn * 4 if K > tk else 0
            return need + rows * tn * 4 + (rows * tk * 2 if a.dtype == F32 else 0)

        tm = next(r for r in (2112, 1056, 512, 256, 128) if M % r == 0 and (vmem_need(r) <= MM_VMEM_BUDGET or r == 128))
    nk = K // tk
    if mode == "nn":
        a_spec = pl.BlockSpec((tm, tk), lambda n, m, k: (m, k))
        b_spec = pl.BlockSpec((tk, tn), lambda n, m, k: (k, n))
        dims = (((1,), (0,)), ((), ()))
    elif mode == "nt":
        a_spec = pl.BlockSpec((tm, tk), lambda n, m, k: (m, k))
        b_spec = pl.BlockSpec((tn, tk), lambda n, m, k: (n, k))
        dims = (((1,), (1,)), ((), ()))
    else:
        a_spec = pl.BlockSpec((tk, tm), lambda n, m, k: (k, m))
        b_spec = pl.BlockSpec((tk, tn), lambda n, m, k: (k, n))
        dims = (((0,), (0,)), ((), ()))
    o_spec = pl.BlockSpec((tm, tn), lambda n, m, k: (m, n))
    in_specs = [a_spec, b_spec] + ([o_spec] if add is not None else [])
    args = [a, b] + ([add] if add is not None else [])

    def body(*refs):
        if add is not None:
            a_ref, b_ref, r_ref, o_ref, acc = refs
        else:
            a_ref, b_ref, o_ref, acc = refs
        k = pl.program_id(2)
        p = _dot(a_ref[...].astype(BF16), b_ref[...].astype(BF16), dims)

        def finish(r):
            if add is not None:
                r = r + r_ref[...]
            o_ref[...] = r.astype(out_dtype)

        if nk == 1:
            finish(p)
        else:
            @pl.when(k == 0)
            def _():
                acc[...] = p

            @pl.when(k > 0)
            def _():
                acc[...] += p

            @pl.when(k == nk - 1)
            def _():
                finish(acc[...])

    return pl.pallas_call(
        body, name=name, grid=(N // tn, M // tm, nk), in_specs=in_specs, out_specs=o_spec,
        out_shape=jax.ShapeDtypeStruct((M, N), out_dtype),
        scratch_shapes=[pltpu.VMEM((tm, tn) if nk > 1 else (8, 128), F32)],
        compiler_params=_params(("arbitrary", "arbitrary", "arbitrary")),
    )(*args)


ANY_SPACE = pl.BlockSpec(memory_space=pl.ANY)


def _const(c):
    return lambda j: c


def _rows(name, fn, T, ncol, ins, params, outs, accs=(), halo=False, tall=False):
    tm = _pick(T, (1056, 512, 256, 128)) if tall else _pick(T, (384, 256, 128))
    R = T // tm
    hb = tm // HALO
    in_specs, args = [], []
    for spec in ins:
        arr, w, cf = spec[:3]
        lead = spec[3] if len(spec) > 3 else None
        if len(spec) > 4:
            rows, rf = spec[4]
            in_specs.append(pl.BlockSpec((rows, w), lambda j, i, cf=cf, rf=rf: (rf(i), cf(j))))
            args.append(arr)
            continue
        if lead is None:
            mk = lambda blk, rf, cf=cf: pl.BlockSpec(blk, lambda j, i: (rf(i), cf(j)))
            shape = lambda r, w=w: (r, w)
        else:
            mk = lambda blk, rf, cf=cf, lead=lead: pl.BlockSpec(blk, lambda j, i: (lead, rf(i), cf(j)))
            shape = lambda r, w=w: (None, r, w)
        in_specs.append(mk(shape(tm), lambda i: i))
        args.append(arr)
        if halo:
            in_specs.append(mk(shape(HALO), lambda i: jnp.maximum(i * hb - 1, 0)))
            in_specs.append(mk(shape(HALO), lambda i: jnp.minimum((i + 1) * hb, T // HALO - 1)))
            args += [arr, arr]
    for arr, w, cf in params:
        in_specs.append(pl.BlockSpec((arr.shape[0], w), lambda j, i, cf=cf: (0, cf(j))))
        args.append(arr)
    out_shape, out_specs, aliases = [], [], {}
    for k, (tw, w, cf, dt) in enumerate(outs):
        if not isinstance(tw, int):
            aliases[len(args)] = k
            in_specs.append(ANY_SPACE)
            args.append(tw)
            tw = tw.shape[1]
        out_shape.append(jax.ShapeDtypeStruct((T, tw), dt))
        out_specs.append(pl.BlockSpec((tm, w), lambda j, i, cf=cf: (i, cf(j))))
    for r, tw, w, cf in accs:
        out_shape.append(jax.ShapeDtypeStruct((r, tw), F32))
        out_specs.append(pl.BlockSpec((r, w), lambda j, i, cf=cf: (0, cf(j))))
    n_in, n_par, n_out, n_acc, n_alias = len(ins), len(params), len(outs), len(accs), len(aliases)

    def body(*refs):
        i = pl.program_id(1)
        vals, p = [], 0
        for _ in range(n_in):
            if halo:
                before = jnp.where(i > 0, refs[p + 1][...], jnp.zeros_like(refs[p + 1]))
                after = jnp.where(i < R - 1, refs[p + 2][...], jnp.zeros_like(refs[p + 2]))
                vals.append(jnp.concatenate([before, refs[p][...], after], axis=0).astype(F32))
                p += 3
            else:
                vals.append(refs[p][...].astype(F32))
                p += 1
        pvals = [refs[p + k][...] for k in range(n_par)]
        p += n_par + n_alias
        res = fn(i, *vals, *pvals)
        for k in range(n_out):
            refs[p + k][...] = res[k].astype(refs[p + k].dtype)
        p += n_out
        for k in range(n_acc):
            ref, v = refs[p + k], res[n_out + k]

            @pl.when(i == 0)
            def _(ref=ref, v=v):
                ref[...] = v

            @pl.when(i > 0)
            def _(ref=ref, v=v):
                ref[...] += v

    res = pl.pallas_call(
        body, name=name, grid=(ncol, R), in_specs=in_specs, out_specs=out_specs, out_shape=out_shape,
        input_output_aliases=aliases, compiler_params=_params(("arbitrary", "arbitrary")),
    )(*args)
    return res


def _tile_rows(T):
    return _pick(T, (384, 256, 128))


def _row_ids(i, T, halo=False):
    tm = _tile_rows(T)
    if halo:
        return i * tm - HALO + lax.broadcasted_iota(jnp.int32, (tm + 2 * HALO, 1), 0)
    return i * tm + lax.broadcasted_iota(jnp.int32, (tm, 1), 0)


def _rms(x, w):
    return x * lax.rsqrt(jnp.mean(x * x, axis=-1, keepdims=True) + EPS) * w


def _silu(x):
    return x * jax.nn.sigmoid(x)


def _conv3(x, w):
    n = x.shape[0]
    return w[0:1] * pltpu.roll(x, 1, 0) + w[1:2] * x + w[2:3] * pltpu.roll(x, n - 1, 0)


def _conv3_t(d, w):
    n = d.shape[0]
    return w[0:1] * pltpu.roll(d, n - 1, 0) + w[1:2] * d + w[2:3] * pltpu.roll(d, 1, 0)


def _center(x):
    return x[HALO:x.shape[0] - HALO]


def _retention(name, a, b, v, T, da, dv, into=None):
    (a, a0), (b, b0), (v, v0) = [t if isinstance(t, tuple) else (t, 0) for t in (a, b, v)]
    nc = T // CHUNK
    log_gammas = [math.log(1.0 - 2.0 ** (-5.0 - h)) for h in range(RET_HEADS)]

    def body(*refs):
        a_ref, b_ref, v_ref = refs[:3]
        out_ref, o_ref, st, st_b = refs[-4:]
        h = pl.program_id(0)
        lg = jnp.float32(log_gammas[RET_HEADS - 1])
        for k in range(RET_HEADS - 2, -1, -1):
            lg = jnp.where(h == k, jnp.float32(log_gammas[k]), lg)
        li = lax.broadcasted_iota(jnp.int32, (CHUNK, CHUNK), 0)
        si = lax.broadcasted_iota(jnp.int32, (CHUNK, CHUNK), 1)
        dmat = jnp.exp(lg * jnp.abs(li - si).astype(F32))
        pos = lax.broadcasted_iota(jnp.int32, (CHUNK, 1), 0).astype(F32)
        kdec_f = jnp.exp((CHUNK - 1 - pos) * lg)
        qdec_f = jnp.exp((pos + 1) * lg)
        kdec_b = jnp.exp(pos * lg)
        qdec_b = jnp.exp((CHUNK - pos) * lg)
        cdec = jnp.exp(CHUNK * lg)

        def rows(n):
            return pl.ds(pl.multiple_of(n * CHUNK, CHUNK), CHUNK)

        st[...] = jnp.zeros_like(st)
        st_b[...] = jnp.zeros_like(st_b)
        o_ref[...] = jnp.zeros_like(o_ref)

        def step(m, carry):
            r = rows(m)
            av, bv, vv = a_ref[r, :], b_ref[r, :], v_ref[r, :].astype(BF16)
            s = _dot_nt(av.astype(BF16), bv.astype(BF16)) * dmat
            o_ref[r, :] += _dot(s.astype(BF16), vv) + _dot((av * qdec_f).astype(BF16), st[...].astype(BF16))
            st[...] = cdec * st[...] + _dot_tn((bv * kdec_f).astype(BF16), vv)
            r = rows(nc - 1 - m)
            av, bv, vv = a_ref[r, :], b_ref[r, :], v_ref[r, :].astype(BF16)
            o_ref[r, :] += _dot((av * qdec_b).astype(BF16), st_b[...].astype(BF16))
            st_b[...] = cdec * st_b[...] + _dot_tn((bv * kdec_b).astype(BF16), vv)
            return carry

        lax.fori_loop(0, nc, step, 0, unroll=11 if nc % 11 == 0 else 1)
        out_ref[...] = o_ref[...].astype(out_ref.dtype)

    in_specs = [pl.BlockSpec((T, da), lambda h: (0, a0 // da + h)), pl.BlockSpec((T, da), lambda h: (0, b0 // da + h)),
                pl.BlockSpec((T, dv), lambda h: (0, v0 // dv + h))]
    if into is None:
        args, o0, aliases = (a, b, v), 0, {}
        out_shape = jax.ShapeDtypeStruct((T, RET_HEADS * dv), F32)
    else:
        args, o0, aliases = (a, b, v, into[0]), into[1], {3: 0}
        in_specs.append(ANY_SPACE)
        out_shape = jax.ShapeDtypeStruct(into[0].shape, into[0].dtype)
    return pl.pallas_call(
        body, name=name, grid=(RET_HEADS,), in_specs=in_specs,
        out_specs=pl.BlockSpec((T, dv), lambda h: (0, o0 // dv + h)), out_shape=out_shape,
        input_output_aliases=aliases,
        scratch_shapes=[pltpu.VMEM((T, dv), F32), pltpu.VMEM((da, dv), F32), pltpu.VMEM((da, dv), F32)],
        compiler_params=_params(("arbitrary",)),
    )(*args)


def _softplus(x):
    return jnp.maximum(x, 0.0) + jnp.log1p(jnp.exp(-jnp.abs(x)))


def _lane_lo():
    return lax.broadcasted_iota(jnp.int32, (1, CHUNK), 1) < SSD_HEAD_DIM


def _pair_cols(col, j):
    return jnp.where(_lane_lo(), col[:, 2 * j:2 * j + 1], col[:, 2 * j + 1:2 * j + 2])


def _pair_rows(colr, j):
    lo = lax.broadcasted_iota(jnp.int32, (CHUNK, 1), 0) < SSD_HEAD_DIM
    return jnp.where(lo, colr[2 * j:2 * j + 1, :], colr[2 * j + 1:2 * j + 2, :])


def _onehot8(h):
    return (lax.broadcasted_iota(jnp.int32, (1, HEADS_PER_GROUP), 1) == h).astype(F32)


def _ssd_pre(d, c, rawc, rawr, bc, br, alc, alr):
    li = lax.broadcasted_iota(jnp.int32, (CHUNK, CHUNK), 0)
    si = lax.broadcasted_iota(jnp.int32, (CHUNK, CHUNK), 1)
    dif = li - si if d == 0 else si - li
    mask = dif >= 0
    mask_t = dif <= 0
    rowc = c * CHUNK + lax.broadcasted_iota(jnp.int32, (CHUNK, 1), 0)
    rowr = c * CHUNK + lax.broadcasted_iota(jnp.int32, (1, CHUNK), 1)
    dtc = jnp.where(rowc >= PAD_ROWS, _softplus(rawc + bc), 0.0)
    dtr = jnp.where(rowr >= PAD_ROWS, _softplus(rawr + br), 0.0)
    ac = -jnp.exp(alc)
    ar = -jnp.exp(alr)
    dlc = dtc * ac
    dlr = dtr * ar
    alpc = _dot(mask.astype(F32), dlc, precision=HIGHEST)
    alpr = _dot(dlr, mask_t.astype(F32), precision=HIGHEST)
    endc = jnp.sum(dlc, axis=0, keepdims=True)
    endr = jnp.sum(dlr, axis=1, keepdims=True)
    return dict(mask=mask, mask_t=mask_t, dtc=dtc, ac=ac, alpc=alpc, alpr=alpr, endc=endc, endr=endr,
                valid=rowc >= PAD_ROWS)


def _chunk_of(d, n, nc):
    return n + d * (nc - 1 - 2 * n)


GROUP_WIDTH = HEADS_PER_GROUP * SSD_HEAD_DIM


def _chunks_per_step(nc, most=3):
    return next(c for c in (11, 3, 1) if c <= most and nc % c == 0)


def _ssd_in_specs(d, cfn, rows):
    return [
        pl.BlockSpec((rows, GROUP_WIDTH), lambda g, n: (cfn(d, n), g)),
        pl.BlockSpec((rows, SSD_STATE), lambda g, n: (cfn(d, n), g)),
        pl.BlockSpec((rows, SSD_STATE), lambda g, n: (cfn(d, n), g)),
        pl.BlockSpec((None, None, rows, HEADS_PER_GROUP), lambda g, n: (d, g, cfn(d, n), 0)),
        pl.BlockSpec((None, None, HEADS_PER_GROUP, rows), lambda g, n: (d, g, 0, cfn(d, n))),
        pl.BlockSpec((None, None, 1, HEADS_PER_GROUP), lambda g, n: (d, g, 0, 0)),
        pl.BlockSpec((None, None, HEADS_PER_GROUP, 1), lambda g, n: (d, g, 0, 0)),
        pl.BlockSpec((None, None, 1, HEADS_PER_GROUP), lambda g, n: (d, g, 0, 0)),
        pl.BlockSpec((None, None, HEADS_PER_GROUP, 1), lambda g, n: (d, g, 0, 0)),
    ]


N_SSD_IN = 9


def _ssd_fwd(xs, bm, cm, small, T):
    nc = T // CHUNK
    cps = _chunks_per_step(nc, 11)
    rows = cps * CHUNK
    cfn = lambda d, n: _chunk_of(d, n, nc // cps)

    def one_direction(d, n, ins, y_ref, hs_ref, h_scr):
        x_ref, b_ref, c_ref, rawc_ref, rawr_ref, *per_group = ins
        for kk in range(cps):
            k = kk if d == 0 else cps - 1 - kk
            r = pl.ds(k * CHUNK, CHUNK)
            one_chunk(d, cfn(d, n) * cps + k,
                      (x_ref.at[r], b_ref.at[r], c_ref.at[r], rawc_ref.at[r], rawr_ref.at[:, r], *per_group),
                      y_ref.at[r], hs_ref.at[k], h_scr)

    def one_chunk(d, c, ins, y_ref, hs_ref, h_scr):
        x_ref, b_ref, c_ref, rawc_ref, rawr_ref, bc_ref, br_ref, alc_ref, alr_ref = ins
        q = _ssd_pre(d, c, rawc_ref[...], rawr_ref[...], bc_ref[...], br_ref[...], alc_ref[...], alr_ref[...])
        bv = b_ref[...].astype(BF16)
        cv = c_ref[...].astype(BF16)
        cb = _dot_nt(cv, bv)
        lo = _lane_lo()
        for j in range(PAIRS_PER_GROUP):
            xp = x_ref[:, j * CHUNK:(j + 1) * CHUNK]
            xd = xp * _pair_cols(q["dtc"], j)
            xdb = xd.astype(BF16)
            yi = []
            for e in range(2):
                h = 2 * j + e
                lm = jnp.exp(jnp.where(q["mask"], q["alpc"][:, h:h + 1] - q["alpr"][h:h + 1, :], -jnp.inf))
                yi.append(_dot((cb * lm).astype(BF16), xdb))
            alp = _pair_cols(q["alpc"], j)
            hp = h_scr[j]
            hs_ref[j] = hp
            yo = jnp.exp(alp) * _dot_nt(cv, hp.astype(BF16))
            y_ref[:, j * CHUNK:(j + 1) * CHUNK] = (jnp.where(lo, yi[0], yi[1]) + yo).astype(y_ref.dtype)
            de = jnp.exp(_pair_cols(q["endc"], j) - alp)
            h_scr[j] = jnp.exp(_pair_rows(q["endr"], j)) * hp + _dot_tn((xd * de).astype(BF16), bv)

    def body(*refs):
        n = pl.program_id(1)
        ins, (y_f, y_b, hs_f, hs_b, h_scr) = refs[:2 * N_SSD_IN], refs[2 * N_SSD_IN:]

        @pl.when(n == 0)
        def _():
            h_scr[...] = jnp.zeros_like(h_scr)

        one_direction(0, n, ins[:N_SSD_IN], y_f, hs_f, h_scr.at[0])
        one_direction(1, n, ins[N_SSD_IN:], y_b, hs_b, h_scr.at[1])

    y_spec = lambda d: pl.BlockSpec((rows, GROUP_WIDTH), lambda g, n: (cfn(d, n), g))
    hs_spec = lambda d: pl.BlockSpec((None, cps, PAIRS_PER_GROUP, CHUNK, SSD_STATE),
                                     lambda g, n: (g, cfn(d, n), 0, 0, 0))
    y_shape = jax.ShapeDtypeStruct((T, SSD_HEADS * SSD_HEAD_DIM), BF16)
    hs_shape = jax.ShapeDtypeStruct((SSD_GROUPS, nc, PAIRS_PER_GROUP, CHUNK, SSD_STATE), F32)
    y_f, y_b, hs_f, hs_b = pl.pallas_call(
        body, name="ssd_fwd", grid=(SSD_GROUPS, nc // cps),
        in_specs=_ssd_in_specs(0, cfn, rows) + _ssd_in_specs(1, cfn, rows),
        out_specs=[y_spec(0), y_spec(1), hs_spec(0), hs_spec(1)],
        out_shape=[y_shape, y_shape, hs_shape, hs_shape],
        scratch_shapes=[pltpu.VMEM((2, PAIRS_PER_GROUP, CHUNK, SSD_STATE), F32)],
        compiler_params=_params(("arbitrary", "arbitrary")),
    )(xs, bm, cm, *small, xs, bm, cm, *small)
    return (y_f, y_b), (hs_f, hs_b)


def _ssd_bwd(xs, bm, cm, small, hs, dy, T):
    nc = T // CHUNK
    cps = _chunks_per_step(nc, 11)
    rows = cps * CHUNK
    cfn = lambda d, n: _chunk_of(1 - d, n, nc // cps)

    def one_direction(d, n, ins, outs, dh_scr):
        x_ref, b_ref, c_ref, rawc_ref, rawr_ref, bc_ref, br_ref, alc_ref, alr_ref, hs_ref, dy_ref = ins
        dx_ref, db_ref, dc_ref, draw_ref, dbias_ref, dalog_ref = outs
        for kk in range(cps):
            k = cps - 1 - kk if d == 0 else kk
            r = pl.ds(k * CHUNK, CHUNK)
            one_chunk(d, cfn(d, n) * cps + k, n if kk == 0 else None,
                      (x_ref.at[r], b_ref.at[r], c_ref.at[r], rawc_ref.at[r], rawr_ref.at[:, r], bc_ref, br_ref,
                       alc_ref, alr_ref, hs_ref.at[k], dy_ref.at[r]),
                      (dx_ref.at[r], db_ref.at[r], dc_ref.at[r], draw_ref.at[r], dbias_ref, dalog_ref), dh_scr)

    def one_chunk(d, c, first_of_step, ins, outs, dh_scr):
        x_ref, b_ref, c_ref, rawc_ref, rawr_ref, bc_ref, br_ref, alc_ref, alr_ref, hs_ref, dy_ref = ins
        dx_ref, db_ref, dc_ref, draw_ref, dbias_ref, dalog_ref = outs
        rawc, bc = rawc_ref[...], bc_ref[...]
        q = _ssd_pre(d, c, rawc, rawr_ref[...], bc, br_ref[...], alc_ref[...], alr_ref[...])
        b32, c32 = b_ref[...], c_ref[...]
        bv, cv = b32.astype(BF16), c32.astype(BF16)
        cb = _dot_nt(cv, bv)
        cbt = _dot_nt(bv, cv)
        lo = _lane_lo()
        row_lo = lax.broadcasted_iota(jnp.int32, (CHUNK, 1), 0) < SSD_HEAD_DIM
        dcb = jnp.zeros((CHUNK, CHUNK), F32)
        dcp = jnp.zeros((CHUNK, SSD_STATE), F32)
        dbp = jnp.zeros((CHUNK, SSD_STATE), F32)
        dalp = jnp.zeros((CHUNK, HEADS_PER_GROUP), F32)
        dend = jnp.zeros((1, HEADS_PER_GROUP), F32)
        ddtx = jnp.zeros((CHUNK, HEADS_PER_GROUP), F32)

        def half_sums(t):
            return (jnp.sum(jnp.where(lo, t, 0.0), axis=1, keepdims=True),
                    jnp.sum(jnp.where(lo, 0.0, t), axis=1, keepdims=True))

        for j in range(PAIRS_PER_GROUP):
            xp = x_ref[:, j * CHUNK:(j + 1) * CHUNK]
            dtp = _pair_cols(q["dtc"], j)
            xd = xp * dtp
            xdb = xd.astype(BF16)
            dyp = dy_ref[:, j * CHUNK:(j + 1) * CHUNK]
            dyb = dyp.astype(BF16)
            hn = hs_ref[j]
            hnb = hn.astype(BF16)
            dh1 = dh_scr[j]
            dh1b = dh1.astype(BF16)
            alp = _pair_cols(q["alpc"], j)
            ea = jnp.exp(alp)
            de = jnp.exp(_pair_cols(q["endc"], j) - alp)
            dxi = []
            for e in range(2):
                h = 2 * j + e
                diff = q["alpc"][:, h:h + 1] - q["alpr"][h:h + 1, :]
                lm = jnp.exp(jnp.where(q["mask"], diff, -jnp.inf))
                mt = cbt * jnp.exp(jnp.where(q["mask_t"], -diff, -jnp.inf))
                dxi.append(_dot(mt.astype(BF16), dyb))
                dyeb_h = (jnp.where(lo, dyp, 0.0) if e == 0 else jnp.where(lo, 0.0, dyp)).astype(BF16)
                gl = _dot_nt(dyeb_h, xdb) * lm
                dcb = dcb + gl
                ra = jnp.sum(gl * cb - _dot_nt(xdb, dyeb_h) * mt, axis=1, keepdims=True)
                dalp = dalp + ra * _onehot8(h)
            y_off = ea * _dot_nt(cv, hnb)
            dxs_state = de * _dot_nt(bv, dh1b)
            dxd = jnp.where(lo, dxi[0], dxi[1]) + dxs_state
            dyeb = (dyp * ea).astype(BF16)
            dcp = dcp + _dot(dyeb, hnb)
            dbp = dbp + _dot((xd * de).astype(BF16), dh1b)
            dh_scr[j] = jnp.exp(_pair_rows(q["endr"], j)) * dh1 + _dot_tn(dyeb, cv)
            r0, r1 = half_sums(dyp * y_off - xd * dxs_state)
            dalp = dalp + r0 * _onehot8(2 * j) + r1 * _onehot8(2 * j + 1)
            t0, t1 = half_sums(jnp.sum(xd * dxs_state, axis=0, keepdims=True))
            u = dh1 * hn
            u0 = jnp.sum(jnp.sum(jnp.where(row_lo, u, 0.0), axis=0, keepdims=True), axis=1, keepdims=True)
            u1 = jnp.sum(jnp.sum(jnp.where(row_lo, 0.0, u), axis=0, keepdims=True), axis=1, keepdims=True)
            eend = jnp.exp(q["endc"])
            dend = dend + (t0 + eend * u0) * _onehot8(2 * j) + (t1 + eend * u1) * _onehot8(2 * j + 1)
            dx_ref[:, j * CHUNK:(j + 1) * CHUNK] = (dxd * dtp).astype(dx_ref.dtype)
            w0, w1 = half_sums(dxd * xp)
            ddtx = ddtx + w0 * _onehot8(2 * j) + w1 * _onehot8(2 * j + 1)

        dcbb = dcb.astype(BF16)
        dc_ref[...] = (dcp + _dot(dcbb, bv)).astype(dc_ref.dtype)
        db_ref[...] = (dbp + _dot_tn(dcbb, cv)).astype(db_ref.dtype)
        ddl = _dot(q["mask_t"].astype(F32), dalp, precision=HIGHEST) + dend
        ddt = ddl * q["ac"] + ddtx
        draw = jnp.where(q["valid"], ddt * jax.nn.sigmoid(rawc + bc), 0.0)
        draw_ref[...] = draw
        dbias = jnp.sum(draw, axis=0, keepdims=True)
        dalog = jnp.sum(ddl * q["dtc"], axis=0, keepdims=True) * q["ac"]

        def add():
            dbias_ref[...] += dbias
            dalog_ref[...] += dalog

        if first_of_step is None:
            add()
        else:
            @pl.when(first_of_step == 0)
            def _():
                dbias_ref[...] = dbias
                dalog_ref[...] = dalog

            pl.when(first_of_step > 0)(add)

    n_in, n_out = N_SSD_IN + 2, 6

    def body(*refs):
        n = pl.program_id(1)
        ins, outs, dh_scr = refs[:2 * n_in], refs[2 * n_in:2 * (n_in + n_out)], refs[-1]

        @pl.when(n == 0)
        def _():
            dh_scr[...] = jnp.zeros_like(dh_scr)

        one_direction(0, n, ins[:n_in], outs[:n_out], dh_scr.at[0])
        one_direction(1, n, ins[n_in:], outs[n_out:], dh_scr.at[1])

    def in_specs(d):
        return _ssd_in_specs(d, cfn, rows) + [
            pl.BlockSpec((None, cps, PAIRS_PER_GROUP, CHUNK, SSD_STATE), lambda g, n: (g, cfn(d, n), 0, 0, 0)),
            pl.BlockSpec((rows, GROUP_WIDTH), lambda g, n: (cfn(d, n), g))]

    def out_specs(d):
        acc = pl.BlockSpec((None, 1, HEADS_PER_GROUP), lambda g, n: (g, 0, 0))
        return [pl.BlockSpec((rows, GROUP_WIDTH), lambda g, n: (cfn(d, n), g)),
                pl.BlockSpec((rows, SSD_STATE), lambda g, n: (cfn(d, n), g)),
                pl.BlockSpec((rows, SSD_STATE), lambda g, n: (cfn(d, n), g)),
                pl.BlockSpec((None, rows, HEADS_PER_GROUP), lambda g, n: (g, cfn(d, n), 0)), acc, acc]

    out_shape = [jax.ShapeDtypeStruct((T, SSD_HEADS * SSD_HEAD_DIM), BF16),
                 jax.ShapeDtypeStruct((T, SSD_GROUPS * SSD_STATE), BF16),
                 jax.ShapeDtypeStruct((T, SSD_GROUPS * SSD_STATE), BF16),
                 jax.ShapeDtypeStruct((SSD_GROUPS, T, HEADS_PER_GROUP), F32),
                 jax.ShapeDtypeStruct((SSD_GROUPS, 1, HEADS_PER_GROUP), F32),
                 jax.ShapeDtypeStruct((SSD_GROUPS, 1, HEADS_PER_GROUP), F32)]
    res = pl.pallas_call(
        body, name="ssd_bwd", grid=(SSD_GROUPS, nc // cps),
        in_specs=in_specs(0) + in_specs(1), out_specs=out_specs(0) + out_specs(1), out_shape=out_shape * 2,
        scratch_shapes=[pltpu.VMEM((2, PAIRS_PER_GROUP, CHUNK, SSD_STATE), F32)],
        compiler_params=_params(("arbitrary", "arbitrary")),
    )(xs, bm, cm, *small, hs[0], dy, xs, bm, cm, *small, hs[1], dy)
    return [(res[k], res[n_out + k]) for k in range(n_out)]


def _rot(x, cs, sn):
    return x * cs + pltpu.roll(x, RET_QK_DIM // 2, 1) * sn


def _rot_t(d, cs, sn):
    return d * cs + pltpu.roll(d * sn, RET_QK_DIM // 2, 1)


def _ret_post(y, g, w):
    parts = []
    for h in range(RET_HEADS):
        yh = y[:, h * RET_V_DIM:(h + 1) * RET_V_DIM]
        mu = jnp.mean(yh, axis=-1, keepdims=True)
        var = jnp.mean(jnp.square(yh - mu), axis=-1, keepdims=True)
        parts.append((yh - mu) * lax.rsqrt(var + EPS))
    return _silu(g) * (jnp.concatenate(parts, axis=1) * w)


def _ssd_post(yf, yb, xs, z, dskip, w):
    y = (yf + yb + xs * dskip) * _silu(z)
    return y * lax.rsqrt(jnp.mean(y * y, axis=-1, keepdims=True) + EPS) * w


def _merge(gates, yr, ys, valid):
    m = jax.nn.sigmoid(gates[:, :D_MODEL]) * yr + jax.nn.sigmoid(gates[:, D_MODEL:]) * ys
    return jnp.where(valid, m, 0.0)


def _rope_tables(T):
    half = RET_QK_DIM // 2
    inv = ROPE_BASE ** (-jnp.arange(half, dtype=F32) / half)
    pos = (jnp.arange(T) - PAD_ROWS).astype(F32)
    ang = pos[:, None] * inv[None, :]
    cos, sin = jnp.cos(ang), jnp.sin(ang)
    return jnp.concatenate([cos, cos], axis=1), jnp.concatenate([-sin, sin], axis=1)


def _per_group(v):
    c = v.reshape(SSD_GROUPS, 1, HEADS_PER_GROUP)
    return c, c.reshape(SSD_GROUPS, HEADS_PER_GROUP, 1)


def _local_step(x, target, w, tick, late_weights, early_grads, in_grads):
    S = x.shape[0]
    T = S + CHUNK
    tm = _tile_rows(T)
    c0 = _const(0)

    h0 = jnp.concatenate([jnp.zeros((PAD_ROWS, D_MODEL), F32), w["meta_tokens"], x], axis=0)
    seg_at = {name: a for name, a, _ in SEGMENTS}
    w_main = w["w_in_t"][:seg_at["dt"]]
    w_dt = jnp.pad(w["w_in_t"][seg_at["dt"]:seg_at["gates"]], ((0, CHUNK - 2 * SSD_HEADS), (0, 0)))
    w_gates = w["w_in_t"][seg_at["gates"]:]

    def norm_cast(name, h, nw):
        return _rows(name, lambda i, hv, wv: (_rms(hv, wv),), T, 1, [(h, D_MODEL, c0)], [(nw, D_MODEL, c0)],
                     [(D_MODEL, D_MODEL, c0, BF16)], tall=True)[0]

    u = norm_cast("norm_mix", h0, w["norm_mix_w"] + tick)
    p_main = _mm("proj_main", u, w_main, "nt", out_dtype=BF16)
    p_dt = _mm("proj_dt", u, w_dt, "nt")
    p_gates = _mm("proj_gates", u, w_gates, "nt", out_dtype=BF16)

    def seg(name, width, cf=c0):
        base = seg_at[name] // width
        return (p_main, width, lambda j: base + cf(j))

    cs, sn = _rope_tables(T)
    scale = RET_QK_DIM ** -0.5

    def rot_fn(i, qk, csv, snv):
        q = [_rot(qk[:, h * 128:(h + 1) * 128], csv, snv) for h in range(RET_HEADS)]
        k = [_rot(qk[:, (RET_HEADS + h) * 128:(RET_HEADS + h + 1) * 128], csv, snv) * scale for h in range(RET_HEADS)]
        return jnp.concatenate(q, axis=1), jnp.concatenate(k, axis=1)

    qr, kr = _rows("rotary", rot_fn, T, 1, [seg("qk", 1024), (cs, 128, c0), (sn, 128, c0)], [],
                   [(512, 512, c0, F32), (512, 512, c0, F32)], tall=True)
    v_at = (p_main, seg_at["v"])
    y_ret = _retention("retention", qr, kr, v_at, T, RET_QK_DIM, RET_V_DIM)
    a_ret = _rows("ret_post", lambda i, y, g, gw: (_ret_post(y, g, gw),), T, 1,
                  [(y_ret, 1024, c0), seg("g", 1024)], [(w["ret_gn_w"], 1024, c0)],
                  [(1024, 1024, c0, BF16)], tall=True)[0]

    conv_w = {"xs": w["w_ssd_conv"][:, :2048], "B": w["w_ssd_conv"][:, 2048:2560], "C": w["w_ssd_conv"][:, 2560:]}
    conv_b = {"xs": w["b_ssd_conv"][:, :2048], "B": w["b_ssd_conv"][:, 2048:2560], "C": w["b_ssd_conv"][:, 2560:]}

    def ssd_conv_fn(i, xe, cw, cb):
        r = _row_ids(i, T, True)
        return (_center(jnp.where(r >= PAD_ROWS, _silu(_conv3(xe, cw) + cb), 0.0)),)

    act = {}
    for name in ("xs", "B", "C"):
        wd = conv_w[name].shape[1]
        cw = 512
        act[name] = _rows("ssd_conv_" + name, ssd_conv_fn, T, wd // cw, [seg(name, cw, lambda j: j)],
                          [(conv_w[name], cw, lambda j: j), (conv_b[name], cw, lambda j: j)],
                          [(wd, cw, lambda j: j, BF16)], halo=True)[0]

    raw = p_dt[:, :2 * SSD_HEADS].reshape(T, 2, SSD_GROUPS, HEADS_PER_GROUP)
    rawc = raw.transpose(1, 2, 0, 3)
    rawr = raw.transpose(1, 2, 3, 0)
    bias = [_per_group(w["dt_bias_f"]), _per_group(w["dt_bias_b"])]
    alog = [_per_group(w["a_log_f"]), _per_group(w["a_log_b"])]
    small = (rawc, rawr, jnp.stack([bias[0][0], bias[1][0]]), jnp.stack([bias[0][1], bias[1][1]]),
             jnp.stack([alog[0][0], alog[1][0]]), jnp.stack([alog[0][1], alog[1][1]]))
    y_dir, states = _ssd_fwd(act["xs"], act["B"], act["C"], small, T)

    dskip_e = jnp.repeat(w["d_skip"], SSD_HEAD_DIM, axis=1)
    gcol = lambda j: j
    gw_ = 512
    a_ssd = _rows("ssd_post", lambda i, yf, yb, xv, zv, dk, nw: (_ssd_post(yf, yb, xv, zv, dk, nw),), T, SSD_GROUPS,
                  [(y_dir[0], gw_, gcol), (y_dir[1], gw_, gcol), (act["xs"], gw_, gcol), seg("z", gw_, gcol)],
                  [(dskip_e, gw_, gcol), (w["ssd_norm_w"], gw_, gcol)], [(2048, gw_, gcol, BF16)])[0]

    w = dict(w, **late_weights(a_ssd))
    w_up_g, w_up_u = w["w_ffn_up_t"][:D_FF], w["w_ffn_up_t"][D_FF:]
    y_ret_o = _mm("ret_out", a_ret, w["w_ret_out"], "nn", out_dtype=BF16)
    y_ssd_o = _mm("ssd_out", a_ssd, w["w_ssd_out"], "nn", out_dtype=BF16)

    def merge_fn(i, gates, yr, ys):
        return (_merge(gates, yr, ys, _row_ids(i, T) >= PAD_ROWS),)

    merged = _rows("merge", merge_fn, T, 1, [(p_gates, 2048, c0), (y_ret_o, 1024, c0), (y_ssd_o, 1024, c0)], [],
                   [(1024, 1024, c0, BF16)])[0]
    h1 = _mm("mix_out", merged, w["w_out"], "nn", add=h0)

    n2 = norm_cast("norm_ffn", h1, w["norm_ffn_w"])
    f_pre = _mm("ffn_up", n2, w["w_ffn_up_t"], "nt", out_dtype=BF16)
    cwg, cwu = w["w_ffn_conv"][:, :D_FF], w["w_ffn_conv"][:, D_FF:]
    cbg, cbu = w["b_ffn_conv"][:, :D_FF], w["b_ffn_conv"][:, D_FF:]
    fcol = lambda j: j
    fw = 1408

    def ffn_act_fn(i, ge, ue, wg, wu, bg, bu):
        return (_center(_silu(_conv3(ge, wg) + bg) * (_conv3(ue, wu) + bu)),)

    ucol = lambda j: D_FF // fw + j
    a2 = _rows("ffn_act", ffn_act_fn, T, D_FF // fw, [(f_pre, fw, fcol), (f_pre, fw, ucol)],
               [(cwg, fw, fcol), (cwu, fw, fcol), (cbg, fw, fcol), (cbu, fw, fcol)], [(D_FF, fw, fcol, BF16)],
               halo=True)[0]
    h2 = _mm("ffn_down", a2, w["w_ffn_down"], "nn", add=h1)

    fnw = w["final_norm_w"].reshape(1, D_MODEL)

    per_tile = tm // CHUNK
    tgt_specs = [(target, D_MODEL, c0, None, (CHUNK, lambda i, k=k: jnp.maximum(per_tile * i - 1 + k, 0)))
                 for k in range(per_tile)]

    def loss_fn(i, hv, *rest):
        tv, nw = jnp.concatenate(rest[:per_tile], axis=0), rest[per_tile]
        valid = _row_ids(i, T) >= CHUNK
        y, vjp = jax.vjp(_rms, hv, nw)
        diff = jnp.where(valid, y - tv, 0.0)
        dh, dw = vjp(diff * (1.0 / D_MODEL))
        part = 0.5 / D_MODEL * jnp.sum(jnp.sum(diff * diff, axis=1, keepdims=True), axis=0, keepdims=True)
        return dh, jnp.broadcast_to(part, (1, 128)), dw

    dh2, loss_acc, d_fnw = _rows("loss", loss_fn, T, 1, [(h2, D_MODEL, c0)] + tgt_specs, [(fnw, D_MODEL, c0)],
                                 [(D_MODEL, D_MODEL, c0, F32)], [(1, 128, 128, c0), (1, D_MODEL, D_MODEL, c0)])
    loss = loss_acc[0, 0]
    grads = {"final_norm_w": d_fnw.reshape(D_MODEL)}

    da2 = _mm("d_ffn_act", dh2, w["w_ffn_down"], "nt", out_dtype=BF16)
    grads["w_ffn_down"] = _mm("g_ffn_down", a2, dh2, "tn", out_dtype=BF16)

    def ffn_bwd_fn(i, ge, ue, de, wg, wu, bg, bu):
        fg = _conv3(ge, wg) + bg
        fu = _conv3(ue, wu) + bu
        sg = jax.nn.sigmoid(fg)
        dfg = de * fu * (sg * (1.0 + fg * (1.0 - sg)))
        dfu = de * (fg * sg)
        n = ge.shape[0]

        def wgrad(df, xe):
            df_c = _center(df)
            return jnp.concatenate([jnp.sum(df_c * _center(pltpu.roll(xe, 1, 0)), axis=0, keepdims=True),
                                    jnp.sum(df_c * _center(xe), axis=0, keepdims=True),
                                    jnp.sum(df_c * _center(pltpu.roll(xe, n - 1, 0)), axis=0, keepdims=True)], axis=0)

        return (_center(_conv3_t(dfg, wg)), _center(_conv3_t(dfu, wu)), wgrad(dfg, ge), wgrad(dfu, ue),
                jnp.sum(_center(dfg), axis=0, keepdims=True), jnp.sum(_center(dfu), axis=0, keepdims=True))

    dfg_pre, dfu_pre, g_cwg, g_cwu, g_cbg, g_cbu = _rows(
        "ffn_act_bwd", ffn_bwd_fn, T, D_FF // fw, [(f_pre, fw, fcol), (f_pre, fw, ucol), (da2, fw, fcol)],
        [(cwg, fw, fcol), (cwu, fw, fcol), (cbg, fw, fcol), (cbu, fw, fcol)],
        [(D_FF, fw, fcol, BF16), (D_FF, fw, fcol, BF16)],
        [(3, D_FF, fw, fcol), (3, D_FF, fw, fcol), (1, D_FF, fw, fcol), (1, D_FF, fw, fcol)], halo=True)
    grads["w_ffn_conv"] = jnp.concatenate([g_cwg, g_cwu], axis=1)
    grads["b_ffn_conv"] = jnp.concatenate([g_cbg, g_cbu], axis=1)
    dn2 = _mm("d_norm_ffn_g", dfg_pre, w_up_g, "nn")
    dn2 = _mm("d_norm_ffn_u", dfu_pre, w_up_u, "nn", add=dn2)
    grads["w_ffn_up_t"] = jnp.concatenate([_mm("g_ffn_up_g", dfg_pre, n2, "tn", out_dtype=BF16), _mm("g_ffn_up_u", dfu_pre, n2, "tn", out_dtype=BF16)],
                                          axis=0)

    def norm_bwd(name, h, nw, dn, dres):
        def fn(i, hv, dnv, drv, wv):
            _, vjp = jax.vjp(_rms, hv, wv)
            dh, dw = vjp(dnv)
            return dh + drv, dw
        return _rows(name, fn, T, 1, [(h, D_MODEL, c0), (dn, D_MODEL, c0), (dres, D_MODEL, c0)], [(nw, D_MODEL, c0)],
                     [(D_MODEL, D_MODEL, c0, F32)], [(1, D_MODEL, D_MODEL, c0)])

    dh1, grads["norm_ffn_w"] = norm_bwd("norm_ffn_bwd", h1, w["norm_ffn_w"], dn2, dh2)

    dmerged = _mm("d_merged", dh1, w["w_out"], "nt", out_dtype=BF16)
    grads["w_out"] = _mm("g_out", merged, dh1, "tn", out_dtype=BF16)

    def merge_bwd_fn(i, gates, yr, ys, dm):
        valid = _row_ids(i, T) >= PAD_ROWS
        _, vjp = jax.vjp(lambda a, b, c: _merge(a, b, c, valid), gates, yr, ys)
        return vjp(dm)

    dgates, dyr, dys = _rows("merge_bwd", merge_bwd_fn, T, 1,
                             [(p_gates, 2048, c0), (y_ret_o, 1024, c0), (y_ssd_o, 1024, c0), (dmerged, 1024, c0)],
                             [], [(2048, 2048, c0, BF16), (1024, 1024, c0, BF16), (1024, 1024, c0, BF16)])
    dproj = {"gates": dgates}

    da_ssd = _mm("d_ssd_act", dys, w["w_ssd_out"], "nt", out_dtype=BF16)
    grads["w_ssd_out"] = _mm("g_ssd_out", a_ssd, dys, "tn", out_dtype=BF16)

    def ssd_post_bwd_fn(i, yf, yb, xv, zv, da, dk, nw):
        _, vjp = jax.vjp(_ssd_post, yf, yb, xv, zv, dk, nw)
        dyf, _, dxv, dzv, ddk, dnw = vjp(da)
        return dyf, dxv, dzv, ddk, dnw

    d_main = lax.empty(p_main.shape, BF16)

    def into_main(name, width, cf=c0):
        base = seg_at[name] // width
        return (d_main, width, lambda j: base + cf(j), BF16)

    dy_ssd, dxs_skip, d_main, g_dskip_e, grads["ssd_norm_w"] = _rows(
        "ssd_post_bwd", ssd_post_bwd_fn, T, SSD_GROUPS,
        [(y_dir[0], gw_, gcol), (y_dir[1], gw_, gcol), (act["xs"], gw_, gcol), seg("z", gw_, gcol),
         (da_ssd, gw_, gcol)],
        [(dskip_e, gw_, gcol), (w["ssd_norm_w"], gw_, gcol)],
        [(2048, gw_, gcol, BF16), (2048, gw_, gcol, BF16), into_main("z", gw_, gcol)],
        [(1, 2048, gw_, gcol), (1, 2048, gw_, gcol)])
    grads["d_skip"] = g_dskip_e.reshape(SSD_HEADS, SSD_HEAD_DIM).sum(axis=1).reshape(1, SSD_HEADS)

    dxs_dir, db_dir, dc_dir, draw, g_bias, g_alog = _ssd_bwd(act["xs"], act["B"], act["C"], small, states, dy_ssd, T)
    grads["dt_bias_f"], grads["dt_bias_b"] = g_bias[0].reshape(1, SSD_HEADS), g_bias[1].reshape(1, SSD_HEADS)
    grads["a_log_f"], grads["a_log_b"] = g_alog[0].reshape(1, SSD_HEADS), g_alog[1].reshape(1, SSD_HEADS)
    d_dt = jnp.stack(draw).transpose(2, 0, 1, 3).reshape(T, 2 * SSD_HEADS)
    dproj["dt"] = jnp.pad(d_dt, ((0, 0), (0, CHUNK - 2 * SSD_HEADS))).astype(BF16)

    def make_conv_bwd(nsum):
        def fn(i, xe, *rest):
            ds, (cw, cb) = rest[:nsum], rest[nsum:]
            r = _row_ids(i, T, True)
            dact = ds[0]
            for t in ds[1:]:
                dact = dact + t
            dact = jnp.where(r >= PAD_ROWS, dact, 0.0)
            pre = _conv3(xe, cw) + cb
            sg = jax.nn.sigmoid(pre)
            dpre = dact * (sg * (1.0 + pre * (1.0 - sg)))
            n = xe.shape[0]
            dpc = _center(dpre)
            dw = jnp.concatenate([jnp.sum(dpc * _center(pltpu.roll(xe, 1, 0)), axis=0, keepdims=True),
                                  jnp.sum(dpc * _center(xe), axis=0, keepdims=True),
                                  jnp.sum(dpc * _center(pltpu.roll(xe, n - 1, 0)), axis=0, keepdims=True)], axis=0)
            return _center(_conv3_t(dpre, cw)), dw, jnp.sum(dpc, axis=0, keepdims=True)
        return fn

    g_cw, g_cb = {}, {}
    cots = {"xs": [(dxs_dir[0], 512, gcol), (dxs_dir[1], 512, gcol), (dxs_skip, 512, gcol)],
            "B": [(db_dir[0], 512, gcol), (db_dir[1], 512, gcol)],
            "C": [(dc_dir[0], 512, gcol), (dc_dir[1], 512, gcol)]}
    for name in ("xs", "B", "C"):
        wd = conv_w[name].shape[1]
        d_main, g_cw[name], g_cb[name] = _rows(
            "ssd_conv_bwd_" + name, make_conv_bwd(len(cots[name])), T, wd // 512,
            [seg(name, 512, gcol)] + cots[name], [(conv_w[name], 512, gcol), (conv_b[name], 512, gcol)],
            [into_main(name, 512, gcol)], [(3, wd, 512, gcol), (1, wd, 512, gcol)], halo=True)
    grads["w_ssd_conv"] = jnp.concatenate([g_cw["xs"], g_cw["B"], g_cw["C"]], axis=1)
    grads["b_ssd_conv"] = jnp.concatenate([g_cb["xs"], g_cb["B"], g_cb["C"]], axis=1)

    da_ret = _mm("d_ret_act", dyr, w["w_ret_out"], "nt", out_dtype=BF16)
    grads["w_ret_out"] = _mm("g_ret_out", a_ret, dyr, "tn", out_dtype=BF16)
    tick = early_grads({n: grads.pop(n) for n in ("w_ffn_up_t", "w_ret_out", "w_ssd_out", "w_out", "w_ffn_down")})

    def ret_post_bwd_fn(i, y, g, da, gw):
        _, vjp = jax.vjp(_ret_post, y, g, gw)
        return vjp(da)

    dy_ret, d_main, grads["ret_gn_w"] = _rows(
        "ret_post_bwd", ret_post_bwd_fn, T, 1, [(y_ret, 1024, c0), seg("g", 1024), (da_ret, 1024, c0)],
        [(w["ret_gn_w"] + tick, 1024, c0)], [(1024, 1024, c0, BF16), into_main("g", 1024)], [(1, 1024, 1024, c0)])
    d_main = _retention("retention_dv", kr, qr, dy_ret, T, RET_QK_DIM, RET_V_DIM, into=(d_main, seg_at["v"]))
    dqr = _retention("retention_dq", dy_ret, v_at, kr, T, RET_V_DIM, RET_QK_DIM)
    dkr = _retention("retention_dk", v_at, dy_ret, qr, T, RET_V_DIM, RET_QK_DIM)

    def rot_bwd_fn(i, dq, dk, csv, snv):
        parts = [_rot_t(dq[:, h * 128:(h + 1) * 128], csv, snv) for h in range(RET_HEADS)]
        parts += [_rot_t(dk[:, h * 128:(h + 1) * 128] * scale, csv, snv) for h in range(RET_HEADS)]
        return (jnp.concatenate(parts, axis=1),)

    d_main = _rows("rotary_bwd", rot_bwd_fn, T, 1, [(dqr, 512, c0), (dkr, 512, c0), (cs, 128, c0), (sn, 128, c0)],
                   [], [into_main("qk", 1024)], tall=True)[0]

    g_in = [_mm("g_in_main", d_main, u, "tn", out_dtype=BF16),
            _mm("g_in_dt", dproj["dt"], u, "tn", out_dtype=BF16)[:2 * SSD_HEADS],
            _mm("g_in_gates", dproj["gates"], u, "tn", out_dtype=BF16)]
    tick = in_grads(jnp.concatenate(g_in, axis=0))
    du = _mm("d_u_dt", dproj["dt"] + tick.astype(BF16), w_dt, "nn")
    du = _mm("d_u_main", d_main, w_main, "nn", add=du)
    du = _mm("d_u_gates", dproj["gates"], w_gates, "nn", add=du)
    dh0, grads["norm_mix_w"] = norm_bwd("norm_mix_bwd", h0, w["norm_mix_w"], du, dh1)
    grads["meta_tokens"] = dh0[PAD_ROWS:CHUNK]
    return loss, dh0[CHUNK:], grads


MESH_ID = pl.DeviceIdType.MESH
ANY = pl.BlockSpec(memory_space=pl.ANY)


def _me_and_peers():
    x, y, c = lax.axis_index("x"), lax.axis_index("y"), lax.axis_index("c")
    peers = []
    for k in range(1, N_DEV):
        px = 1 - x if k & 4 else x
        py = 1 - y if k & 2 else y
        pc = 1 - c if k & 1 else c
        peers.append(((px, py, pc), 4 * px + 2 * py + pc))
    return 4 * x + 2 * y + c, peers


def _push_blocks(name, src, per_peer):
    blk = src.shape[1:] if per_peer else src.shape

    def body(src_ref, out_ref, send_sems, recv_sems, local_sem):
        me, peers = _me_and_peers()
        mine = src_ref.at[me] if per_peer else src_ref
        local = pltpu.make_async_copy(mine, out_ref.at[me], local_sem)
        local.start()
        sends = []
        for k, (dev, idx) in enumerate(peers):
            cp = pltpu.make_async_remote_copy(
                src_ref=src_ref.at[idx] if per_peer else src_ref, dst_ref=out_ref.at[me],
                send_sem=send_sems.at[k], recv_sem=recv_sems.at[k], device_id=dev, device_id_type=MESH_ID)
            cp.start()
            sends.append(cp)
        for k, (dev, idx) in enumerate(peers):
            pltpu.make_async_remote_copy(
                src_ref=mine, dst_ref=out_ref.at[idx], send_sem=send_sems.at[k], recv_sem=recv_sems.at[k],
                device_id=dev, device_id_type=MESH_ID).wait_recv()
        for cp in sends:
            cp.wait_send()
        local.wait()

    return pl.pallas_call(
        body, name=name, in_specs=[ANY], out_specs=ANY,
        out_shape=jax.ShapeDtypeStruct((N_DEV,) + tuple(blk), src.dtype),
        scratch_shapes=[pltpu.SemaphoreType.DMA((N_DEV - 1,)), pltpu.SemaphoreType.DMA((N_DEV - 1,)),
                        pltpu.SemaphoreType.DMA],
    )(src)


def _gather_two_level(name, src):
    def body(x_ref, out_ref, send_sems, recv_sems, local_sem):
        x, y, c = lax.axis_index("x"), lax.axis_index("y"), lax.axis_index("c")
        me, sibling = (x, y, c), (x, y, 1 - c)
        chips = [(1 - x, y), (x, 1 - y), (1 - x, 1 - y)]

        def rows(px, py, pc):
            return out_ref.at[4 * px + 2 * py + pc]

        def copy(k, block, to, src_ref=None):
            return pltpu.make_async_remote_copy(
                src_ref=rows(*block) if src_ref is None else src_ref, dst_ref=rows(*block),
                send_sem=send_sems.at[k], recv_sem=recv_sems.at[k], device_id=to, device_id_type=MESH_ID)

        mine = pltpu.make_async_copy(x_ref, rows(*me), local_sem)
        mine.start()
        first = [copy(0, me, sibling, x_ref)] + [copy(1 + j, me, (*chip, c), x_ref) for j, chip in enumerate(chips)]
        for cp in first:
            cp.start()
        passed = [copy(4 + j, (*chip, c), sibling) for j, chip in enumerate(chips)]
        for j, chip in enumerate(chips):
            copy(1 + j, (*chip, c), me).wait_recv()
            passed[j].start()
        copy(0, sibling, me).wait_recv()
        for j, chip in enumerate(chips):
            copy(4 + j, (*chip, 1 - c), me).wait_recv()
        for cp in first + passed:
            cp.wait_send()
        mine.wait()

    return pl.pallas_call(
        body, name=name, in_specs=[ANY], out_specs=ANY,
        out_shape=jax.ShapeDtypeStruct((N_DEV,) + tuple(src.shape), src.dtype),
        scratch_shapes=[pltpu.SemaphoreType.DMA((N_DEV - 1,)), pltpu.SemaphoreType.DMA((N_DEV - 1,)),
                        pltpu.SemaphoreType.DMA],
    )(src)


HBM = pl.BlockSpec(memory_space=pltpu.HBM)
SEM = pl.BlockSpec(memory_space=pltpu.SEMAPHORE)
EFFECT = pltpu.SideEffectType.DATAFLOW_SIDE_EFFECTING


def _peer_copy(src_ref, land_ref, send_sems, recv_sems, per_peer, me, a, k, dev, idx, receiving):
    s = a * (N_DEV - 1) + k
    return pltpu.make_async_remote_copy(
        src_ref=src_ref.at[idx] if per_peer else src_ref, dst_ref=land_ref.at[idx if receiving else me],
        send_sem=send_sems.at[s], recv_sem=recv_sems.at[s], device_id=dev, device_id_type=MESH_ID)


def _push_start(name, srcs, per_peer):
    n = len(srcs)
    land_shapes = [(N_DEV,) + tuple(s.shape[1:] if per_peer else s.shape) for s in srcs]

    def body(*refs):
        src_refs, land_refs, send_sems, recv_sems, token = refs[:n], refs[n:2 * n], refs[2 * n], refs[2 * n + 1], refs[-1]
        me, peers = _me_and_peers()
        for a in range(n):
            for k, (dev, idx) in enumerate(peers):
                _peer_copy(src_refs[a], land_refs[a], send_sems, recv_sems, per_peer, me, a, k, dev, idx, False).start()
        token[...] = jnp.zeros_like(token)

    sems = pltpu.SemaphoreType.DMA((n * (N_DEV - 1),))
    res = pl.pallas_call(
        body, name=name,
        out_shape=(sems, sems, *[pltpu.HBM(s.shape, s.dtype) for s in srcs],
                   *[pltpu.HBM(ls, s.dtype) for ls, s in zip(land_shapes, srcs)], jax.ShapeDtypeStruct((8, 128), F32)),
        in_specs=(HBM,) * (2 * n), out_specs=(SEM, SEM) + (HBM,) * (2 * n) + (pl.BlockSpec(memory_space=pltpu.VMEM),),
        input_output_aliases={i: 2 + i for i in range(2 * n)},
        compiler_params=pltpu.CompilerParams(has_side_effects=EFFECT),
    )(*[pltpu.with_memory_space_constraint(s, pltpu.HBM) for s in srcs],
      *[pltpu.with_memory_space_constraint(lax.empty(ls, s.dtype), pltpu.HBM) for ls, s in zip(land_shapes, srcs)])
    return res[0], res[1], res[2:2 + n], res[2 + n:2 + 2 * n], res[-1]


def _push_wait(name, send_sems, recv_sems, srcs_thru, lands_thru, after, per_peer):
    n = len(srcs_thru)

    def body(*refs):
        src_refs, land_refs, send_sems, recv_sems = refs[:n], refs[n:2 * n], refs[2 * n], refs[2 * n + 1]
        me, peers = _me_and_peers()
        for a in range(n):
            for k, (dev, idx) in enumerate(peers):
                cp = _peer_copy(src_refs[a], land_refs[a], send_sems, recv_sems, per_peer, me, a, k, dev, idx, True)
                cp.wait_send()
                cp.wait_recv()

    both = list(srcs_thru) + list(lands_thru)
    res = pl.pallas_call(
        body, name=name, out_shape=tuple(pltpu.HBM(t.shape, t.dtype) for t in both),
        in_specs=(HBM,) * (2 * n) + (SEM, SEM, ANY), out_specs=(HBM,) * (2 * n),
        input_output_aliases={i: i for i in range(2 * n)},
        compiler_params=pltpu.CompilerParams(has_side_effects=EFFECT),
    )(*both, send_sems, recv_sems, after)
    return res[:n], res[n:]


def _sum_blocks(name, blocks):
    _, R, C = blocks.shape
    tc = next(t for t in (1024, 512, 256, 128) if C % t == 0 and (N_DEV * R * t * 2 <= 6 * 2 ** 20 or t == 128))

    def body(b_ref, o_ref):
        acc = b_ref[0].astype(F32)
        for k in range(1, N_DEV):
            acc = acc + b_ref[k].astype(F32)
        o_ref[...] = acc

    return pl.pallas_call(
        body, name=name, grid=(C // tc,), in_specs=[pl.BlockSpec((N_DEV, R, tc), lambda j: (0, 0, j))],
        out_specs=pl.BlockSpec((R, tc), lambda j: (0, j)), out_shape=jax.ShapeDtypeStruct((R, C), F32),
        compiler_params=_params(("arbitrary",)),
    )(blocks)


def _adamw(name, w, g, m, v):
    R, C = w.shape
    tr = R if R <= 512 else _pick(R, (256, 184, 176, 128, 8))
    spec = pl.BlockSpec((tr, C), lambda i: (i, 0))

    def body(w_ref, g_ref, m_ref, v_ref, d_ref, mo_ref, vo_ref):
        gv = g_ref[...]
        mn = ADAM_B1 * m_ref[...] + (1.0 - ADAM_B1) * gv
        vn = ADAM_B2 * v_ref[...] + (1.0 - ADAM_B2) * jnp.square(gv)
        m_hat = mn / (1.0 - ADAM_B1 ** ADAM_STEP)
        v_hat = vn / (1.0 - ADAM_B2 ** ADAM_STEP)
        d_ref[...] = -ADAM_LR * (m_hat / (jnp.sqrt(v_hat) + ADAM_EPS) + ADAM_WD * w_ref[...])
        mo_ref[...] = mn
        vo_ref[...] = vn

    return pl.pallas_call(
        body, name=name, grid=(R // tr,), in_specs=[spec] * 4, out_specs=[spec] * 3,
        out_shape=[jax.ShapeDtypeStruct((R, C), F32)] * 3, compiler_params=_params(("arbitrary",)),
    )(w, g, m, v)


WEIGHTS = ("meta_tokens", "norm_mix_w", "w_in", "ret_gn_w", "w_ret_out", "w_ssd_conv", "b_ssd_conv", "dt_bias_f",
           "dt_bias_b", "a_log_f", "a_log_b", "d_skip", "ssd_norm_w", "w_ssd_out", "w_out", "norm_ffn_w", "w_ffn_up",
           "w_ffn_conv", "b_ffn_conv", "w_ffn_down", "final_norm_w")
BIG = (("w_in", 1288, True), ("w_ffn_up", 704, True), ("w_ret_out", 128, False), ("w_ssd_out", 256, False),
       ("w_out", 128, False), ("w_ffn_down", 352, False))
REPLICATED = ("norm_mix_w", "ret_gn_w", "b_ssd_conv", "dt_bias_f", "dt_bias_b", "a_log_f", "a_log_b", "d_skip",
              "ssd_norm_w", "norm_ffn_w", "b_ffn_conv", "final_norm_w")
SMALL_SHARDED = (("meta_tokens", 16, 1024), ("w_ssd_conv", 3, 3072), ("w_ffn_conv", 3, 5632))


BIG_IN, BIG_REST = BIG[:1], BIG[1:]


def _pack_flat(arrays, rows):
    flat = jnp.concatenate([a.reshape(-1) for a in arrays])
    return jnp.pad(flat, (0, rows * D_MODEL - flat.shape[0])).reshape(rows, D_MODEL)


def _unpack_flat(slab, shapes):
    flat, out, o = slab.reshape(-1), [], 0
    for s in shapes:
        n = math.prod(s)
        out.append(flat[o:o + n].reshape(s))
        o += n
    return out


def kernel(x, meta_tokens, norm_mix_w, w_in, ret_gn_w, w_ret_out, w_ssd_conv, b_ssd_conv, dt_bias_f, dt_bias_b, a_log_f, a_log_b, d_skip, ssd_norm_w, w_ssd_out, w_out, norm_ffn_w, w_ffn_up, w_ffn_conv, b_ffn_conv, w_ffn_down, final_norm_w, loss_target, m_meta_tokens, m_norm_mix_w, m_w_in, m_ret_gn_w, m_w_ret_out, m_w_ssd_conv, m_b_ssd_conv, m_dt_bias_f, m_dt_bias_b, m_a_log_f, m_a_log_b, m_d_skip, m_ssd_norm_w, m_w_ssd_out, m_w_out, m_norm_ffn_w, m_w_ffn_up, m_w_ffn_conv, m_b_ffn_conv, m_w_ffn_down, m_final_norm_w, v_meta_tokens, v_norm_mix_w, v_w_in, v_ret_gn_w, v_w_ret_out, v_w_ssd_conv, v_b_ssd_conv, v_dt_bias_f, v_dt_bias_b, v_a_log_f, v_a_log_b, v_d_skip, v_ssd_norm_w, v_w_ssd_out, v_w_out, v_norm_ffn_w, v_w_ffn_up, v_w_ffn_conv, v_b_ffn_conv, v_w_ffn_down, v_final_norm_w):
    given = dict(locals())
    wt = {n: given[n] for n in WEIGHTS}
    mt = {n: given["m_" + n] for n in WEIGHTS}
    vt = {n: given["v_" + n] for n in WEIGHTS}
    me = 4 * lax.axis_index("x") + 2 * lax.axis_index("y") + lax.axis_index("c")

    small_names = [n for n, _, _ in SMALL_SHARDED]
    small_local = lambda tree: [tree[n].reshape(r, c // N_DEV) for n, r, c in SMALL_SHARDED]
    slab_view = lambda tree, name, transposed: tree[name][0].T if transposed else tree[name][0]
    all_in = _gather_two_level("gather_w_in", slab_view(wt, "w_in", True).astype(BF16))
    all_s = _push_blocks("gather_small", _pack_flat(small_local(wt), 8), False)
    rest_srcs = [slab_view(wt, name, t).astype(BF16) for name, _, t in BIG_REST]
    rest_srcs, all_in, all_s = lax.optimization_barrier((rest_srcs, all_in, all_s))
    rest_flight = _push_start("gather_rest_start", rest_srcs, False)
    all_s = all_s.reshape(N_DEV, -1)
    full = {"w_in_t": all_in.reshape(-1, D_MODEL)}

    def lands_with_own(flight, after, per_peer, name):
        srcs, lands = _push_wait(name, *flight[:4], after, per_peer)
        own = lambda s: lax.dynamic_slice_in_dim(s, me, 1, axis=0) if per_peer else s[None]
        return [lax.dynamic_update_slice_in_dim(land, own(s), me, axis=0) for s, land in zip(srcs, lands)]

    def late_weights(after):
        lands = lands_with_own(rest_flight, after, False, "gather_rest_wait")
        return {name + ("_t" if t else ""): land.reshape(N_DEV * r, D_MODEL) for (name, r, t), land in zip(BIG_REST, lands)}

    flights = {}

    def start_exchange(key, group, gd):
        srcs = [gd[name + ("_t" if t else "")].astype(BF16).reshape(N_DEV, r, D_MODEL) for name, r, t in group]
        flights[key] = _push_start("exchange_" + key + "_start", srcs, True)
        return flights[key][4][0, 0]

    o = 0
    for name, r, c in SMALL_SHARDED:
        n = r * c // N_DEV
        full[name] = all_s[:, o:o + n].reshape(N_DEV, r, c // N_DEV).transpose(1, 0, 2).reshape(r, c)
        o += n
    for name in REPLICATED:
        full[name] = wt[name]

    grads, delta, new_m, new_v = {}, {}, {}, {}

    def finish_exchange(key, group, after):
        lands = lands_with_own(flights[key], after, True, "exchange_" + key + "_wait")
        for (name, _, transposed), land in zip(group, lands):
            back = (lambda a: a.T[None]) if transposed else (lambda a: a[None])
            g_sum = _sum_blocks("sum_" + name, land)
            d, mn, vn = _adamw("adamw_" + name, slab_view(wt, name, transposed), g_sum,
                               slab_view(mt, name, transposed), slab_view(vt, name, transposed))
            grads[name], delta[name], new_m[name], new_v[name] = back(g_sum), back(d), back(mn), back(vn)

    def in_grads(gi):
        tick = start_exchange("in", BIG_IN, {"w_in_t": gi})
        finish_exchange("rest", BIG_REST, flights["in"][4])
        tick, _ = lax.optimization_barrier((tick, [delta[name] for name, _, _ in BIG_REST]))
        return tick

    loss, grad_x, g = _local_step(x[0], loss_target[0], full, rest_flight[4][0, 0], late_weights,
                                  lambda gd: start_exchange("rest", BIG_REST, gd), in_grads)

    finish_exchange("in", BIG_IN, g["norm_mix_w"])
    small_parts = [g[n] for n in REPLICATED] + [g[n] for n in small_names] + [loss.reshape(1)]
    g_small = _sum_blocks("sum_small", _push_blocks("gather_small_grads", _pack_flat(small_parts, 64), False))
    small_red = _unpack_flat(g_small, [wt[n].shape for n in REPLICATED] + [(r, c) for _, r, c in SMALL_SHARDED] + [(1,)])
    grads.update(zip(REPLICATED, small_red[:len(REPLICATED)]))
    for (name, r, c), red in zip(SMALL_SHARDED, small_red[len(REPLICATED):-1]):
        grads[name] = lax.dynamic_slice(red, (0, me * (c // N_DEV)), (r, c // N_DEV)).reshape(wt[name].shape)
    loss_all = small_red[-1][0]

    rest = list(REPLICATED) + small_names
    shapes = [wt[n].shape for n in rest]
    pack_rest = lambda tree: _pack_flat([tree[n] for n in rest], 24)
    d_rest, m_rest, v_rest = _adamw("adamw_small", pack_rest(wt), pack_rest(grads), pack_rest(mt), pack_rest(vt))
    delta.update(zip(rest, _unpack_flat(d_rest, shapes)))
    new_m.update(zip(rest, _unpack_flat(m_rest, shapes)))
    new_v.update(zip(rest, _unpack_flat(v_rest, shapes)))

    return (loss_all, grad_x[None], *[grads[n] for n in WEIGHTS], *[delta[n] for n in WEIGHTS],
            *[new_m[n] for n in WEIGHTS], *[new_v[n] for n in WEIGHTS])
```

```python
import functools
import math

import jax
import jax.numpy as jnp
from jax import lax
from jax.experimental import pallas as pl
from jax.experimental.pallas import tpu as pltpu

F32 = jnp.float32
BF16 = jnp.bfloat16

D_MODEL = 1024
CHUNK = 128
N_META = 16
PAD_ROWS = CHUNK - N_META
RET_HEADS = 4
RET_QK_DIM = 128
RET_V_DIM = 256
SSD_HEADS = 32
SSD_HEAD_DIM = 64
SSD_GROUPS = 4
SSD_STATE = 128
HEADS_PER_GROUP = SSD_HEADS // SSD_GROUPS
PAIRS_PER_GROUP = HEADS_PER_GROUP // 2
D_FF = 2816
EPS = 1e-6
ROPE_BASE = 10000.0
N_DEV = 8

ADAM_LR = 0.001
ADAM_B1 = 0.9
ADAM_B2 = 0.999
ADAM_EPS = 1e-08
ADAM_WD = 0.01
ADAM_STEP = 10

VMEM_LIMIT = 56 * 1024 * 1024
HALO = 16
HIGHEST = lax.Precision.HIGHEST

SEGMENTS = (("qk", 0, 1024), ("v", 1024, 2048), ("g", 2048, 3072), ("z", 3072, 5120), ("xs", 5120, 7168),
            ("B", 7168, 7680), ("C", 7680, 8192), ("dt", 8192, 8256), ("gates", 8256, 10304))


def _pick(n, cands):
    for c in cands:
        if n % c == 0:
            return c
    raise ValueError(f"no tile for {n}")


def _params(sem):
    return pltpu.CompilerParams(dimension_semantics=sem, vmem_limit_bytes=VMEM_LIMIT)


def _dot(a, b, dims=(((1,), (0,)), ((), ())), precision=None):
    return lax.dot_general(a, b, dims, preferred_element_type=F32, precision=precision)


def _dot_nt(a, b):
    return _dot(a, b, (((1,), (1,)), ((), ())))


def _dot_tn(a, b):
    return _dot(a, b, (((0,), (0,)), ((), ())))


def _mm(name, a, b, mode, add=None, out_dtype=F32):
    if mode == "nn":
        (M, K), N = a.shape, b.shape[1]
    elif mode == "nt":
        (M, K), N = a.shape, b.shape[0]
    else:
        (K, M), N = a.shape, b.shape[1]
    tn = _pick(N, (1408, 1024, 512, 128, 64))
    if mode == "tn":
        tm = M if M <= 1024 else _pick(M, (1408, 1024))
        tk = _pick(K, (2112, 512, 256, 128))
    else:
        tm = _pick(M, (1056, 512, 256, 128))
        tk = K if K <= 2816 else _pick(K, (2048, 1408, 1024))
    nk = K // tk
    if mode == "nn":
        a_spec = pl.BlockSpec((tm, tk), lambda n, m, k: (m, k))
        b_spec = pl.BlockSpec((tk, tn), lambda n, m, k: (k, n))
        dims = (((1,), (0,)), ((), ()))
    elif mode == "nt":
        a_spec = pl.BlockSpec((tm, tk), lambda n, m, k: (m, k))
        b_spec = pl.BlockSpec((tn, tk), lambda n, m, k: (n, k))
        dims = (((1,), (1,)), ((), ()))
    else:
        a_spec = pl.BlockSpec((tk, tm), lambda n, m, k: (k, m))
        b_spec = pl.BlockSpec((tk, tn), lambda n, m, k: (k, n))
        dims = (((0,), (0,)), ((), ()))
    o_spec = pl.BlockSpec((tm, tn), lambda n, m, k: (m, n))
    in_specs = [a_spec, b_spec] + ([o_spec] if add is not None else [])
    args = [a, b] + ([add] if add is not None else [])

    def body(*refs):
        if add is not None:
            a_ref, b_ref, r_ref, o_ref, acc = refs
        else:
            a_ref, b_ref, o_ref, acc = refs
        k = pl.program_id(2)
        p = _dot(a_ref[...].astype(BF16), b_ref[...].astype(BF16), dims)

        def finish(r):
            if add is not None:
                r = r + r_ref[...]
            o_ref[...] = r.astype(out_dtype)

        if nk == 1:
            finish(p)
        else:
            @pl.when(k == 0)
            def _():
                acc[...] = p

            @pl.when(k > 0)
            def _():
                acc[...] += p

            @pl.when(k == nk - 1)
            def _():
                finish(acc[...])

    return pl.pallas_call(
        body, name=name, grid=(N // tn, M // tm, nk), in_specs=in_specs, out_specs=o_spec,
        out_shape=jax.ShapeDtypeStruct((M, N), out_dtype),
        scratch_shapes=[pltpu.VMEM((tm, tn) if nk > 1 else (8, 128), F32)],
        compiler_params=_params(("arbitrary", "arbitrary", "arbitrary")),
    )(*args)


ANY_SPACE = pl.BlockSpec(memory_space=pl.ANY)


def _const(c):
    return lambda j: c


def _rows(name, fn, T, ncol, ins, params, outs, accs=(), halo=False, tall=False):
    tm = _pick(T, (1056, 512, 256, 128)) if tall else _pick(T, (384, 256, 128))
    R = T // tm
    hb = tm // HALO
    in_specs, args = [], []
    for spec in ins:
        arr, w, cf = spec[:3]
        lead = spec[3] if len(spec) > 3 else None
        if len(spec) > 4:
            rows, rf = spec[4]
            in_specs.append(pl.BlockSpec((rows, w), lambda j, i, cf=cf, rf=rf: (rf(i), cf(j))))
            args.append(arr)
            continue
        if lead is None:
            mk = lambda blk, rf, cf=cf: pl.BlockSpec(blk, lambda j, i: (rf(i), cf(j)))
            shape = lambda r, w=w: (r, w)
        else:
            mk = lambda blk, rf, cf=cf, lead=lead: pl.BlockSpec(blk, lambda j, i: (lead, rf(i), cf(j)))
            shape = lambda r, w=w: (None, r, w)
        in_specs.append(mk(shape(tm), lambda i: i))
        args.append(arr)
        if halo:
            in_specs.append(mk(shape(HALO), lambda i: jnp.maximum(i * hb - 1, 0)))
            in_specs.append(mk(shape(HALO), lambda i: jnp.minimum((i + 1) * hb, T // HALO - 1)))
            args += [arr, arr]
    for arr, w, cf in params:
        in_specs.append(pl.BlockSpec((arr.shape[0], w), lambda j, i, cf=cf: (0, cf(j))))
        args.append(arr)
    out_shape, out_specs, aliases = [], [], {}
    for k, (tw, w, cf, dt) in enumerate(outs):
        if not isinstance(tw, int):
            aliases[len(args)] = k
            in_specs.append(ANY_SPACE)
            args.append(tw)
            tw = tw.shape[1]
        out_shape.append(jax.ShapeDtypeStruct((T, tw), dt))
        out_specs.append(pl.BlockSpec((tm, w), lambda j, i, cf=cf: (i, cf(j))))
    for r, tw, w, cf in accs:
        out_shape.append(jax.ShapeDtypeStruct((r, tw), F32))
        out_specs.append(pl.BlockSpec((r, w), lambda j, i, cf=cf: (0, cf(j))))
    n_in, n_par, n_out, n_acc, n_alias = len(ins), len(params), len(outs), len(accs), len(aliases)

    def body(*refs):
        i = pl.program_id(1)
        vals, p = [], 0
        for _ in range(n_in):
            if halo:
                before = jnp.where(i > 0, refs[p + 1][...], jnp.zeros_like(refs[p + 1]))
                after = jnp.where(i < R - 1, refs[p + 2][...], jnp.zeros_like(refs[p + 2]))
                vals.append(jnp.concatenate([before, refs[p][...], after], axis=0).astype(F32))
                p += 3
            else:
                vals.append(refs[p][...].astype(F32))
                p += 1
        pvals = [refs[p + k][...] for k in range(n_par)]
        p += n_par + n_alias
        res = fn(i, *vals, *pvals)
        for k in range(n_out):
            refs[p + k][...] = res[k].astype(refs[p + k].dtype)
        p += n_out
        for k in range(n_acc):
            ref, v = refs[p + k], res[n_out + k]

            @pl.when(i == 0)
            def _(ref=ref, v=v):
                ref[...] = v

            @pl.when(i > 0)
            def _(ref=ref, v=v):
                ref[...] += v

    res = pl.pallas_call(
        body, name=name, grid=(ncol, R), in_specs=in_specs, out_specs=out_specs, out_shape=out_shape,
        input_output_aliases=aliases, compiler_params=_params(("arbitrary", "arbitrary")),
    )(*args)
    return res


def _tile_rows(T):
    return _pick(T, (384, 256, 128))


def _row_ids(i, T, halo=False):
    tm = _tile_rows(T)
    if halo:
        return i * tm - HALO + lax.broadcasted_iota(jnp.int32, (tm + 2 * HALO, 1), 0)
    return i * tm + lax.broadcasted_iota(jnp.int32, (tm, 1), 0)


def _rms(x, w):
    return x * lax.rsqrt(jnp.mean(x * x, axis=-1, keepdims=True) + EPS) * w


def _silu(x):
    return x * jax.nn.sigmoid(x)


def _conv3(x, w):
    n = x.shape[0]
    return w[0:1] * pltpu.roll(x, 1, 0) + w[1:2] * x + w[2:3] * pltpu.roll(x, n - 1, 0)


def _conv3_t(d, w):
    n = d.shape[0]
    return w[0:1] * pltpu.roll(d, n - 1, 0) + w[1:2] * d + w[2:3] * pltpu.roll(d, 1, 0)


def _center(x):
    return x[HALO:x.shape[0] - HALO]


def _retention(name, a, b, v, T, da, dv, into=None):
    (a, a0), (b, b0), (v, v0) = [t if isinstance(t, tuple) else (t, 0) for t in (a, b, v)]
    nc = T // CHUNK
    log_gammas = [math.log(1.0 - 2.0 ** (-5.0 - h)) for h in range(RET_HEADS)]

    def body(*refs):
        a_ref, b_ref, v_ref = refs[:3]
        out_ref, o_ref, st, st_b = refs[-4:]
        h = pl.program_id(0)
        lg = jnp.float32(log_gammas[RET_HEADS - 1])
        for k in range(RET_HEADS - 2, -1, -1):
            lg = jnp.where(h == k, jnp.float32(log_gammas[k]), lg)
        li = lax.broadcasted_iota(jnp.int32, (CHUNK, CHUNK), 0)
        si = lax.broadcasted_iota(jnp.int32, (CHUNK, CHUNK), 1)
        dmat = jnp.exp(lg * jnp.abs(li - si).astype(F32))
        pos = lax.broadcasted_iota(jnp.int32, (CHUNK, 1), 0).astype(F32)
        kdec_f = jnp.exp((CHUNK - 1 - pos) * lg)
        qdec_f = jnp.exp((pos + 1) * lg)
        kdec_b = jnp.exp(pos * lg)
        qdec_b = jnp.exp((CHUNK - pos) * lg)
        cdec = jnp.exp(CHUNK * lg)

        def rows(n):
            return pl.ds(pl.multiple_of(n * CHUNK, CHUNK), CHUNK)

        st[...] = jnp.zeros_like(st)
        st_b[...] = jnp.zeros_like(st_b)
        o_ref[...] = jnp.zeros_like(o_ref)

        def step(m, carry):
            r = rows(m)
            av, bv, vv = a_ref[r, :], b_ref[r, :], v_ref[r, :].astype(BF16)
            s = _dot_nt(av.astype(BF16), bv.astype(BF16)) * dmat
            o_ref[r, :] += _dot(s.astype(BF16), vv) + _dot((av * qdec_f).astype(BF16), st[...].astype(BF16))
            st[...] = cdec * st[...] + _dot_tn((bv * kdec_f).astype(BF16), vv)
            r = rows(nc - 1 - m)
            av, bv, vv = a_ref[r, :], b_ref[r, :], v_ref[r, :].astype(BF16)
            o_ref[r, :] += _dot((av * qdec_b).astype(BF16), st_b[...].astype(BF16))
            st_b[...] = cdec * st_b[...] + _dot_tn((bv * kdec_b).astype(BF16), vv)
            return carry

        lax.fori_loop(0, nc, step, 0, unroll=True)
        out_ref[...] = o_ref[...].astype(out_ref.dtype)

    in_specs = [pl.BlockSpec((T, da), lambda h: (0, a0 // da + h)), pl.BlockSpec((T, da), lambda h: (0, b0 // da + h)),
                pl.BlockSpec((T, dv), lambda h: (0, v0 // dv + h))]
    if into is None:
        args, o0, aliases = (a, b, v), 0, {}
        out_shape = jax.ShapeDtypeStruct((T, RET_HEADS * dv), F32)
    else:
        args, o0, aliases = (a, b, v, into[0]), into[1], {3: 0}
        in_specs.append(ANY_SPACE)
        out_shape = jax.ShapeDtypeStruct(into[0].shape, into[0].dtype)
    return pl.pallas_call(
        body, name=name, grid=(RET_HEADS,), in_specs=in_specs,
        out_specs=pl.BlockSpec((T, dv), lambda h: (0, o0 // dv + h)), out_shape=out_shape,
        input_output_aliases=aliases,
        scratch_shapes=[pltpu.VMEM((T, dv), F32), pltpu.VMEM((da, dv), F32), pltpu.VMEM((da, dv), F32)],
        compiler_params=_params(("arbitrary",)),
    )(*args)


def _softplus(x):
    return jnp.maximum(x, 0.0) + jnp.log1p(jnp.exp(-jnp.abs(x)))


def _lane_lo():
    return lax.broadcasted_iota(jnp.int32, (1, CHUNK), 1) < SSD_HEAD_DIM


def _pair_cols(col, j):
    return jnp.where(_lane_lo(), col[:, 2 * j:2 * j + 1], col[:, 2 * j + 1:2 * j + 2])


def _pair_rows(colr, j):
    lo = lax.broadcasted_iota(jnp.int32, (CHUNK, 1), 0) < SSD_HEAD_DIM
    return jnp.where(lo, colr[2 * j:2 * j + 1, :], colr[2 * j + 1:2 * j + 2, :])


def _onehot8(h):
    return (lax.broadcasted_iota(jnp.int32, (1, HEADS_PER_GROUP), 1) == h).astype(F32)


def _ssd_pre(d, c, rawc, rawr, bc, br, alc, alr):
    li = lax.broadcasted_iota(jnp.int32, (CHUNK, CHUNK), 0)
    si = lax.broadcasted_iota(jnp.int32, (CHUNK, CHUNK), 1)
    dif = li - si if d == 0 else si - li
    mask = dif >= 0
    mask_t = dif <= 0
    rowc = c * CHUNK + lax.broadcasted_iota(jnp.int32, (CHUNK, 1), 0)
    rowr = c * CHUNK + lax.broadcasted_iota(jnp.int32, (1, CHUNK), 1)
    dtc = jnp.where(rowc >= PAD_ROWS, _softplus(rawc + bc), 0.0)
    dtr = jnp.where(rowr >= PAD_ROWS, _softplus(rawr + br), 0.0)
    ac = -jnp.exp(alc)
    ar = -jnp.exp(alr)
    dlc = dtc * ac
    dlr = dtr * ar
    alpc = _dot(mask.astype(F32), dlc, precision=HIGHEST)
    alpr = _dot(dlr, mask_t.astype(F32), precision=HIGHEST)
    endc = jnp.sum(dlc, axis=0, keepdims=True)
    endr = jnp.sum(dlr, axis=1, keepdims=True)
    return dict(mask=mask, mask_t=mask_t, dtc=dtc, ac=ac, alpc=alpc, alpr=alpr, endc=endc, endr=endr,
                valid=rowc >= PAD_ROWS)


def _chunk_of(d, n, nc):
    return n + d * (nc - 1 - 2 * n)


GROUP_WIDTH = HEADS_PER_GROUP * SSD_HEAD_DIM


def _chunks_per_step(nc, most=3):
    return next(c for c in (11, 3, 1) if c <= most and nc % c == 0)


def _ssd_in_specs(d, cfn, rows):
    return [
        pl.BlockSpec((rows, GROUP_WIDTH), lambda g, n: (cfn(d, n), g)),
        pl.BlockSpec((rows, SSD_STATE), lambda g, n: (cfn(d, n), g)),
        pl.BlockSpec((rows, SSD_STATE), lambda g, n: (cfn(d, n), g)),
        pl.BlockSpec((None, None, rows, HEADS_PER_GROUP), lambda g, n: (d, g, cfn(d, n), 0)),
        pl.BlockSpec((None, None, HEADS_PER_GROUP, rows), lambda g, n: (d, g, 0, cfn(d, n))),
        pl.BlockSpec((None, None, 1, HEADS_PER_GROUP), lambda g, n: (d, g, 0, 0)),
        pl.BlockSpec((None, None, HEADS_PER_GROUP, 1), lambda g, n: (d, g, 0, 0)),
        pl.BlockSpec((None, None, 1, HEADS_PER_GROUP), lambda g, n: (d, g, 0, 0)),
        pl.BlockSpec((None, None, HEADS_PER_GROUP, 1), lambda g, n: (d, g, 0, 0)),
    ]


N_SSD_IN = 9


def _ssd_fwd(xs, bm, cm, small, T):
    nc = T // CHUNK
    cps = _chunks_per_step(nc, 11)
    rows = cps * CHUNK
    cfn = lambda d, n: _chunk_of(d, n, nc // cps)

    def one_direction(d, n, ins, y_ref, hs_ref, h_scr):
        x_ref, b_ref, c_ref, rawc_ref, rawr_ref, *per_group = ins
        for kk in range(cps):
            k = kk if d == 0 else cps - 1 - kk
            r = pl.ds(k * CHUNK, CHUNK)
            one_chunk(d, cfn(d, n) * cps + k,
                      (x_ref.at[r], b_ref.at[r], c_ref.at[r], rawc_ref.at[r], rawr_ref.at[:, r], *per_group),
                      y_ref.at[r], hs_ref.at[k], h_scr)

    def one_chunk(d, c, ins, y_ref, hs_ref, h_scr):
        x_ref, b_ref, c_ref, rawc_ref, rawr_ref, bc_ref, br_ref, alc_ref, alr_ref = ins
        q = _ssd_pre(d, c, rawc_ref[...], rawr_ref[...], bc_ref[...], br_ref[...], alc_ref[...], alr_ref[...])
        bv = b_ref[...].astype(BF16)
        cv = c_ref[...].astype(BF16)
        cb = _dot_nt(cv, bv)
        lo = _lane_lo()
        for j in range(PAIRS_PER_GROUP):
            xp = x_ref[:, j * CHUNK:(j + 1) * CHUNK]
            xd = xp * _pair_cols(q["dtc"], j)
            xdb = xd.astype(BF16)
            yi = []
            for e in range(2):
                h = 2 * j + e
                lm = jnp.exp(jnp.where(q["mask"], q["alpc"][:, h:h + 1] - q["alpr"][h:h + 1, :], -jnp.inf))
                yi.append(_dot((cb * lm).astype(BF16), xdb))
            alp = _pair_cols(q["alpc"], j)
            hp = h_scr[j]
            hs_ref[j] = hp
            yo = jnp.exp(alp) * _dot_nt(cv, hp.astype(BF16))
            y_ref[:, j * CHUNK:(j + 1) * CHUNK] = (jnp.where(lo, yi[0], yi[1]) + yo).astype(y_ref.dtype)
            de = jnp.exp(_pair_cols(q["endc"], j) - alp)
            h_scr[j] = jnp.exp(_pair_rows(q["endr"], j)) * hp + _dot_tn((xd * de).astype(BF16), bv)

    def body(*refs):
        n = pl.program_id(1)
        ins, (y_f, y_b, hs_f, hs_b, h_scr) = refs[:2 * N_SSD_IN], refs[2 * N_SSD_IN:]

        @pl.when(n == 0)
        def _():
            h_scr[...] = jnp.zeros_like(h_scr)

        one_direction(0, n, ins[:N_SSD_IN], y_f, hs_f, h_scr.at[0])
        one_direction(1, n, ins[N_SSD_IN:], y_b, hs_b, h_scr.at[1])

    y_spec = lambda d: pl.BlockSpec((rows, GROUP_WIDTH), lambda g, n: (cfn(d, n), g))
    hs_spec = lambda d: pl.BlockSpec((None, cps, PAIRS_PER_GROUP, CHUNK, SSD_STATE),
                                     lambda g, n: (g, cfn(d, n), 0, 0, 0))
    y_shape = jax.ShapeDtypeStruct((T, SSD_HEADS * SSD_HEAD_DIM), BF16)
    hs_shape = jax.ShapeDtypeStruct((SSD_GROUPS, nc, PAIRS_PER_GROUP, CHUNK, SSD_STATE), F32)
    y_f, y_b, hs_f, hs_b = pl.pallas_call(
        body, name="ssd_fwd", grid=(SSD_GROUPS, nc // cps),
        in_specs=_ssd_in_specs(0, cfn, rows) + _ssd_in_specs(1, cfn, rows),
        out_specs=[y_spec(0), y_spec(1), hs_spec(0), hs_spec(1)],
        out_shape=[y_shape, y_shape, hs_shape, hs_shape],
        scratch_shapes=[pltpu.VMEM((2, PAIRS_PER_GROUP, CHUNK, SSD_STATE), F32)],
        compiler_params=_params(("arbitrary", "arbitrary")),
    )(xs, bm, cm, *small, xs, bm, cm, *small)
    return (y_f, y_b), (hs_f, hs_b)


def _ssd_bwd(xs, bm, cm, small, hs, dy, T):
    nc = T // CHUNK
    cps = _chunks_per_step(nc, 11)
    rows = cps * CHUNK
    cfn = lambda d, n: _chunk_of(1 - d, n, nc // cps)

    def one_direction(d, n, ins, outs, dh_scr):
        x_ref, b_ref, c_ref, rawc_ref, rawr_ref, bc_ref, br_ref, alc_ref, alr_ref, hs_ref, dy_ref = ins
        dx_ref, db_ref, dc_ref, draw_ref, dbias_ref, dalog_ref = outs
        for kk in range(cps):
            k = cps - 1 - kk if d == 0 else kk
            r = pl.ds(k * CHUNK, CHUNK)
            one_chunk(d, cfn(d, n) * cps + k, n if kk == 0 else None,
                      (x_ref.at[r], b_ref.at[r], c_ref.at[r], rawc_ref.at[r], rawr_ref.at[:, r], bc_ref, br_ref,
                       alc_ref, alr_ref, hs_ref.at[k], dy_ref.at[r]),
                      (dx_ref.at[r], db_ref.at[r], dc_ref.at[r], draw_ref.at[r], dbias_ref, dalog_ref), dh_scr)

    def one_chunk(d, c, first_of_step, ins, outs, dh_scr):
        x_ref, b_ref, c_ref, rawc_ref, rawr_ref, bc_ref, br_ref, alc_ref, alr_ref, hs_ref, dy_ref = ins
        dx_ref, db_ref, dc_ref, draw_ref, dbias_ref, dalog_ref = outs
        rawc, bc = rawc_ref[...], bc_ref[...]
        q = _ssd_pre(d, c, rawc, rawr_ref[...], bc, br_ref[...], alc_ref[...], alr_ref[...])
        b32, c32 = b_ref[...], c_ref[...]
        bv, cv = b32.astype(BF16), c32.astype(BF16)
        cb = _dot_nt(cv, bv)
        cbt = _dot_nt(bv, cv)
        lo = _lane_lo()
        row_lo = lax.broadcasted_iota(jnp.int32, (CHUNK, 1), 0) < SSD_HEAD_DIM
        dcb = jnp.zeros((CHUNK, CHUNK), F32)
        dcp = jnp.zeros((CHUNK, SSD_STATE), F32)
        dbp = jnp.zeros((CHUNK, SSD_STATE), F32)
        dalp = jnp.zeros((CHUNK, HEADS_PER_GROUP), F32)
        dend = jnp.zeros((1, HEADS_PER_GROUP), F32)
        ddtx = jnp.zeros((CHUNK, HEADS_PER_GROUP), F32)

        def half_sums(t):
            return (jnp.sum(jnp.where(lo, t, 0.0), axis=1, keepdims=True),
                    jnp.sum(jnp.where(lo, 0.0, t), axis=1, keepdims=True))

        for j in range(PAIRS_PER_GROUP):
            xp = x_ref[:, j * CHUNK:(j + 1) * CHUNK]
            dtp = _pair_cols(q["dtc"], j)
            xd = xp * dtp
            xdb = xd.astype(BF16)
            dyp = dy_ref[:, j * CHUNK:(j + 1) * CHUNK]
            dyb = dyp.astype(BF16)
            hn = hs_ref[j]
            hnb = hn.astype(BF16)
            dh1 = dh_scr[j]
            dh1b = dh1.astype(BF16)
            alp = _pair_cols(q["alpc"], j)
            ea = jnp.exp(alp)
            de = jnp.exp(_pair_cols(q["endc"], j) - alp)
            dxi = []
            for e in range(2):
                h = 2 * j + e
                diff = q["alpc"][:, h:h + 1] - q["alpr"][h:h + 1, :]
                lm = jnp.exp(jnp.where(q["mask"], diff, -jnp.inf))
                mt = cbt * jnp.exp(jnp.where(q["mask_t"], -diff, -jnp.inf))
                dxi.append(_dot(mt.astype(BF16), dyb))
                dyeb_h = (jnp.where(lo, dyp, 0.0) if e == 0 else jnp.where(lo, 0.0, dyp)).astype(BF16)
                gl = _dot_nt(dyeb_h, xdb) * lm
                dcb = dcb + gl
                ra = jnp.sum(gl * cb - _dot_nt(xdb, dyeb_h) * mt, axis=1, keepdims=True)
                dalp = dalp + ra * _onehot8(h)
            y_off = ea * _dot_nt(cv, hnb)
            dxs_state = de * _dot_nt(bv, dh1b)
            dxd = jnp.where(lo, dxi[0], dxi[1]) + dxs_state
            dyeb = (dyp * ea).astype(BF16)
            dcp = dcp + _dot(dyeb, hnb)
            dbp = dbp + _dot((xd * de).astype(BF16), dh1b)
            dh_scr[j] = jnp.exp(_pair_rows(q["endr"], j)) * dh1 + _dot_tn(dyeb, cv)
            r0, r1 = half_sums(dyp * y_off - xd * dxs_state)
            dalp = dalp + r0 * _onehot8(2 * j) + r1 * _onehot8(2 * j + 1)
            t0, t1 = half_sums(jnp.sum(xd * dxs_state, axis=0, keepdims=True))
            u = dh1 * hn
            u0 = jnp.sum(jnp.sum(jnp.where(row_lo, u, 0.0), axis=0, keepdims=True), axis=1, keepdims=True)
            u1 = jnp.sum(jnp.sum(jnp.where(row_lo, 0.0, u), axis=0, keepdims=True), axis=1, keepdims=True)
            eend = jnp.exp(q["endc"])
            dend = dend + (t0 + eend * u0) * _onehot8(2 * j) + (t1 + eend * u1) * _onehot8(2 * j + 1)
            dx_ref[:, j * CHUNK:(j + 1) * CHUNK] = (dxd * dtp).astype(dx_ref.dtype)
            w0, w1 = half_sums(dxd * xp)
            ddtx = ddtx + w0 * _onehot8(2 * j) + w1 * _onehot8(2 * j + 1)

        dcbb = dcb.astype(BF16)
        dc_ref[...] = (dcp + _dot(dcbb, bv)).astype(dc_ref.dtype)
        db_ref[...] = (dbp + _dot_tn(dcbb, cv)).astype(db_ref.dtype)
        ddl = _dot(q["mask_t"].astype(F32), dalp, precision=HIGHEST) + dend
        ddt = ddl * q["ac"] + ddtx
        draw = jnp.where(q["valid"], ddt * jax.nn.sigmoid(rawc + bc), 0.0)
        draw_ref[...] = draw
        dbias = jnp.sum(draw, axis=0, keepdims=True)
        dalog = jnp.sum(ddl * q["dtc"], axis=0, keepdims=True) * q["ac"]

        def add():
            dbias_ref[...] += dbias
            dalog_ref[...] += dalog

        if first_of_step is None:
            add()
        else:
            @pl.when(first_of_step == 0)
            def _():
                dbias_ref[...] = dbias
                dalog_ref[...] = dalog

            pl.when(first_of_step > 0)(add)

    n_in, n_out = N_SSD_IN + 2, 6

    def body(*refs):
        n = pl.program_id(1)
        ins, outs, dh_scr = refs[:2 * n_in], refs[2 * n_in:2 * (n_in + n_out)], refs[-1]

        @pl.when(n == 0)
        def _():
            dh_scr[...] = jnp.zeros_like(dh_scr)

        one_direction(0, n, ins[:n_in], outs[:n_out], dh_scr.at[0])
        one_direction(1, n, ins[n_in:], outs[n_out:], dh_scr.at[1])

    def in_specs(d):
        return _ssd_in_specs(d, cfn, rows) + [
            pl.BlockSpec((None, cps, PAIRS_PER_GROUP, CHUNK, SSD_STATE), lambda g, n: (g, cfn(d, n), 0, 0, 0)),
            pl.BlockSpec((rows, GROUP_WIDTH), lambda g, n: (cfn(d, n), g))]

    def out_specs(d):
        acc = pl.BlockSpec((None, 1, HEADS_PER_GROUP), lambda g, n: (g, 0, 0))
        return [pl.BlockSpec((rows, GROUP_WIDTH), lambda g, n: (cfn(d, n), g)),
                pl.BlockSpec((rows, SSD_STATE), lambda g, n: (cfn(d, n), g)),
                pl.BlockSpec((rows, SSD_STATE), lambda g, n: (cfn(d, n), g)),
                pl.BlockSpec((None, rows, HEADS_PER_GROUP), lambda g, n: (g, cfn(d, n), 0)), acc, acc]

    out_shape = [jax.ShapeDtypeStruct((T, SSD_HEADS * SSD_HEAD_DIM), BF16),
                 jax.ShapeDtypeStruct((T, SSD_GROUPS * SSD_STATE), BF16),
                 jax.ShapeDtypeStruct((T, SSD_GROUPS * SSD_STATE), BF16),
                 jax.ShapeDtypeStruct((SSD_GROUPS, T, HEADS_PER_GROUP), F32),
                 jax.ShapeDtypeStruct((SSD_GROUPS, 1, HEADS_PER_GROUP), F32),
                 jax.ShapeDtypeStruct((SSD_GROUPS, 1, HEADS_PER_GROUP), F32)]
    res = pl.pallas_call(
        body, name="ssd_bwd", grid=(SSD_GROUPS, nc // cps),
        in_specs=in_specs(0) + in_specs(1), out_specs=out_specs(0) + out_specs(1), out_shape=out_shape * 2,
        scratch_shapes=[pltpu.VMEM((2, PAIRS_PER_GROUP, CHUNK, SSD_STATE), F32)],
        compiler_params=_params(("arbitrary", "arbitrary")),
    )(xs, bm, cm, *small, hs[0], dy, xs, bm, cm, *small, hs[1], dy)
    return [(res[k], res[n_out + k]) for k in range(n_out)]


def _rot(x, cs, sn):
    return x * cs + pltpu.roll(x, RET_QK_DIM // 2, 1) * sn


def _rot_t(d, cs, sn):
    return d * cs + pltpu.roll(d * sn, RET_QK_DIM // 2, 1)


def _ret_post(y, g, w):
    parts = []
    for h in range(RET_HEADS):
        yh = y[:, h * RET_V_DIM:(h + 1) * RET_V_DIM]
        mu = jnp.mean(yh, axis=-1, keepdims=True)
        var = jnp.mean(jnp.square(yh - mu), axis=-1, keepdims=True)
        parts.append((yh - mu) * lax.rsqrt(var + EPS))
    return _silu(g) * (jnp.concatenate(parts, axis=1) * w)


def _ssd_post(yf, yb, xs, z, dskip, w):
    y = (yf + yb + xs * dskip) * _silu(z)
    return y * lax.rsqrt(jnp.mean(y * y, axis=-1, keepdims=True) + EPS) * w


def _merge(gates, yr, ys, valid):
    m = jax.nn.sigmoid(gates[:, :D_MODEL]) * yr + jax.nn.sigmoid(gates[:, D_MODEL:]) * ys
    return jnp.where(valid, m, 0.0)


def _rope_tables(T):
    half = RET_QK_DIM // 2
    inv = ROPE_BASE ** (-jnp.arange(half, dtype=F32) / half)
    pos = (jnp.arange(T) - PAD_ROWS).astype(F32)
    ang = pos[:, None] * inv[None, :]
    cos, sin = jnp.cos(ang), jnp.sin(ang)
    return jnp.concatenate([cos, cos], axis=1), jnp.concatenate([-sin, sin], axis=1)


def _per_group(v):
    c = v.reshape(SSD_GROUPS, 1, HEADS_PER_GROUP)
    return c, c.reshape(SSD_GROUPS, HEADS_PER_GROUP, 1)


def _local_step(x, target, w, tick, late_weights, early_grads, in_grads):
    S = x.shape[0]
    T = S + CHUNK
    tm = _tile_rows(T)
    c0 = _const(0)

    h0 = jnp.concatenate([jnp.zeros((PAD_ROWS, D_MODEL), F32), w["meta_tokens"], x], axis=0)
    seg_at = {name: a for name, a, _ in SEGMENTS}
    w_main = w["w_in_t"][:seg_at["dt"]]
    w_dt = jnp.pad(w["w_in_t"][seg_at["dt"]:seg_at["gates"]], ((0, CHUNK - 2 * SSD_HEADS), (0, 0)))
    w_gates = w["w_in_t"][seg_at["gates"]:]

    def norm_cast(name, h, nw):
        return _rows(name, lambda i, hv, wv: (_rms(hv, wv),), T, 1, [(h, D_MODEL, c0)], [(nw, D_MODEL, c0)],
                     [(D_MODEL, D_MODEL, c0, BF16)], tall=True)[0]

    u = norm_cast("norm_mix", h0, w["norm_mix_w"] + tick)
    p_main = _mm("proj_main", u, w_main, "nt", out_dtype=BF16)
    p_dt = _mm("proj_dt", u, w_dt, "nt")
    p_gates = _mm("proj_gates", u, w_gates, "nt", out_dtype=BF16)

    def seg(name, width, cf=c0):
        base = seg_at[name] // width
        return (p_main, width, lambda j: base + cf(j))

    cs, sn = _rope_tables(T)
    scale = RET_QK_DIM ** -0.5

    def rot_fn(i, qk, csv, snv):
        q = [_rot(qk[:, h * 128:(h + 1) * 128], csv, snv) for h in range(RET_HEADS)]
        k = [_rot(qk[:, (RET_HEADS + h) * 128:(RET_HEADS + h + 1) * 128], csv, snv) * scale for h in range(RET_HEADS)]
        return jnp.concatenate(q, axis=1), jnp.concatenate(k, axis=1)

    qr, kr = _rows("rotary", rot_fn, T, 1, [seg("qk", 1024), (cs, 128, c0), (sn, 128, c0)], [],
                   [(512, 512, c0, F32), (512, 512, c0, F32)], tall=True)
    v_at = (p_main, seg_at["v"])
    y_ret = _retention("retention", qr, kr, v_at, T, RET_QK_DIM, RET_V_DIM)
    a_ret = _rows("ret_post", lambda i, y, g, gw: (_ret_post(y, g, gw),), T, 1,
                  [(y_ret, 1024, c0), seg("g", 1024)], [(w["ret_gn_w"], 1024, c0)],
                  [(1024, 1024, c0, BF16)], tall=True)[0]

    conv_w = {"xs": w["w_ssd_conv"][:, :2048], "B": w["w_ssd_conv"][:, 2048:2560], "C": w["w_ssd_conv"][:, 2560:]}
    conv_b = {"xs": w["b_ssd_conv"][:, :2048], "B": w["b_ssd_conv"][:, 2048:2560], "C": w["b_ssd_conv"][:, 2560:]}

    def ssd_conv_fn(i, xe, cw, cb):
        r = _row_ids(i, T, True)
        return (_center(jnp.where(r >= PAD_ROWS, _silu(_conv3(xe, cw) + cb), 0.0)),)

    act = {}
    for name in ("xs", "B", "C"):
        wd = conv_w[name].shape[1]
        cw = 512
        act[name] = _rows("ssd_conv_" + name, ssd_conv_fn, T, wd // cw, [seg(name, cw, lambda j: j)],
                          [(conv_w[name], cw, lambda j: j), (conv_b[name], cw, lambda j: j)],
                          [(wd, cw, lambda j: j, BF16)], halo=True)[0]

    raw = p_dt[:, :2 * SSD_HEADS].reshape(T, 2, SSD_GROUPS, HEADS_PER_GROUP)
    rawc = raw.transpose(1, 2, 0, 3)
    rawr = raw.transpose(1, 2, 3, 0)
    bias = [_per_group(w["dt_bias_f"]), _per_group(w["dt_bias_b"])]
    alog = [_per_group(w["a_log_f"]), _per_group(w["a_log_b"])]
    small = (rawc, rawr, jnp.stack([bias[0][0], bias[1][0]]), jnp.stack([bias[0][1], bias[1][1]]),
             jnp.stack([alog[0][0], alog[1][0]]), jnp.stack([alog[0][1], alog[1][1]]))
    y_dir, states = _ssd_fwd(act["xs"], act["B"], act["C"], small, T)

    dskip_e = jnp.repeat(w["d_skip"], SSD_HEAD_DIM, axis=1)
    gcol = lambda j: j
    gw_ = 512
    a_ssd = _rows("ssd_post", lambda i, yf, yb, xv, zv, dk, nw: (_ssd_post(yf, yb, xv, zv, dk, nw),), T, SSD_GROUPS,
                  [(y_dir[0], gw_, gcol), (y_dir[1], gw_, gcol), (act["xs"], gw_, gcol), seg("z", gw_, gcol)],
                  [(dskip_e, gw_, gcol), (w["ssd_norm_w"], gw_, gcol)], [(2048, gw_, gcol, BF16)])[0]

    w = dict(w, **late_weights(a_ssd))
    w_up_g, w_up_u = w["w_ffn_up_t"][:D_FF], w["w_ffn_up_t"][D_FF:]
    y_ret_o = _mm("ret_out", a_ret, w["w_ret_out"], "nn", out_dtype=BF16)
    y_ssd_o = _mm("ssd_out", a_ssd, w["w_ssd_out"], "nn", out_dtype=BF16)

    def merge_fn(i, gates, yr, ys):
        return (_merge(gates, yr, ys, _row_ids(i, T) >= PAD_ROWS),)

    merged = _rows("merge", merge_fn, T, 1, [(p_gates, 2048, c0), (y_ret_o, 1024, c0), (y_ssd_o, 1024, c0)], [],
                   [(1024, 1024, c0, BF16)])[0]
    h1 = _mm("mix_out", merged, w["w_out"], "nn", add=h0)

    n2 = norm_cast("norm_ffn", h1, w["norm_ffn_w"])
    f_pre = _mm("ffn_up", n2, w["w_ffn_up_t"], "nt", out_dtype=BF16)
    cwg, cwu = w["w_ffn_conv"][:, :D_FF], w["w_ffn_conv"][:, D_FF:]
    cbg, cbu = w["b_ffn_conv"][:, :D_FF], w["b_ffn_conv"][:, D_FF:]
    fcol = lambda j: j
    fw = 1408

    def ffn_act_fn(i, ge, ue, wg, wu, bg, bu):
        return (_center(_silu(_conv3(ge, wg) + bg) * (_conv3(ue, wu) + bu)),)

    ucol = lambda j: D_FF // fw + j
    a2 = _rows("ffn_act", ffn_act_fn, T, D_FF // fw, [(f_pre, fw, fcol), (f_pre, fw, ucol)],
               [(cwg, fw, fcol), (cwu, fw, fcol), (cbg, fw, fcol), (cbu, fw, fcol)], [(D_FF, fw, fcol, BF16)],
               halo=True)[0]
    h2 = _mm("ffn_down", a2, w["w_ffn_down"], "nn", add=h1)

    fnw = w["final_norm_w"].reshape(1, D_MODEL)

    per_tile = tm // CHUNK
    tgt_specs = [(target, D_MODEL, c0, None, (CHUNK, lambda i, k=k: jnp.maximum(per_tile * i - 1 + k, 0)))
                 for k in range(per_tile)]

    def loss_fn(i, hv, *rest):
        tv, nw = jnp.concatenate(rest[:per_tile], axis=0), rest[per_tile]
        valid = _row_ids(i, T) >= CHUNK
        y, vjp = jax.vjp(_rms, hv, nw)
        diff = jnp.where(valid, y - tv, 0.0)
        dh, dw = vjp(diff * (1.0 / D_MODEL))
        part = 0.5 / D_MODEL * jnp.sum(jnp.sum(diff * diff, axis=1, keepdims=True), axis=0, keepdims=True)
        return dh, jnp.broadcast_to(part, (1, 128)), dw

    dh2, loss_acc, d_fnw = _rows("loss", loss_fn, T, 1, [(h2, D_MODEL, c0)] + tgt_specs, [(fnw, D_MODEL, c0)],
                                 [(D_MODEL, D_MODEL, c0, F32)], [(1, 128, 128, c0), (1, D_MODEL, D_MODEL, c0)])
    loss = loss_acc[0, 0]
    grads = {"final_norm_w": d_fnw.reshape(D_MODEL)}

    da2 = _mm("d_ffn_act", dh2, w["w_ffn_down"], "nt", out_dtype=BF16)
    grads["w_ffn_down"] = _mm("g_ffn_down", a2, dh2, "tn", out_dtype=BF16)

    def ffn_bwd_fn(i, ge, ue, de, wg, wu, bg, bu):
        fg = _conv3(ge, wg) + bg
        fu = _conv3(ue, wu) + bu
        sg = jax.nn.sigmoid(fg)
        dfg = de * fu * (sg * (1.0 + fg * (1.0 - sg)))
        dfu = de * (fg * sg)
        n = ge.shape[0]

        def wgrad(df, xe):
            df_c = _center(df)
            return jnp.concatenate([jnp.sum(df_c * _center(pltpu.roll(xe, 1, 0)), axis=0, keepdims=True),
                                    jnp.sum(df_c * _center(xe), axis=0, keepdims=True),
                                    jnp.sum(df_c * _center(pltpu.roll(xe, n - 1, 0)), axis=0, keepdims=True)], axis=0)

        return (_center(_conv3_t(dfg, wg)), _center(_conv3_t(dfu, wu)), wgrad(dfg, ge), wgrad(dfu, ue),
                jnp.sum(_center(dfg), axis=0, keepdims=True), jnp.sum(_center(dfu), axis=0, keepdims=True))

    dfg_pre, dfu_pre, g_cwg, g_cwu, g_cbg, g_cbu = _rows(
        "ffn_act_bwd", ffn_bwd_fn, T, D_FF // fw, [(f_pre, fw, fcol), (f_pre, fw, ucol), (da2, fw, fcol)],
        [(cwg, fw, fcol), (cwu, fw, fcol), (cbg, fw, fcol), (cbu, fw, fcol)],
        [(D_FF, fw, fcol, BF16), (D_FF, fw, fcol, BF16)],
        [(3, D_FF, fw, fcol), (3, D_FF, fw, fcol), (1, D_FF, fw, fcol), (1, D_FF, fw, fcol)], halo=True)
    grads["w_ffn_conv"] = jnp.concatenate([g_cwg, g_cwu], axis=1)
    grads["b_ffn_conv"] = jnp.concatenate([g_cbg, g_cbu], axis=1)
    dn2 = _mm("d_norm_ffn_g", dfg_pre, w_up_g, "nn")
    dn2 = _mm("d_norm_ffn_u", dfu_pre, w_up_u, "nn", add=dn2)
    grads["w_ffn_up_t"] = jnp.concatenate([_mm("g_ffn_up_g", dfg_pre, n2, "tn", out_dtype=BF16), _mm("g_ffn_up_u", dfu_pre, n2, "tn", out_dtype=BF16)],
                                          axis=0)

    def norm_bwd(name, h, nw, dn, dres):
        def fn(i, hv, dnv, drv, wv):
            _, vjp = jax.vjp(_rms, hv, wv)
            dh, dw = vjp(dnv)
            return dh + drv, dw
        return _rows(name, fn, T, 1, [(h, D_MODEL, c0), (dn, D_MODEL, c0), (dres, D_MODEL, c0)], [(nw, D_MODEL, c0)],
                     [(D_MODEL, D_MODEL, c0, F32)], [(1, D_MODEL, D_MODEL, c0)])

    dh1, grads["norm_ffn_w"] = norm_bwd("norm_ffn_bwd", h1, w["norm_ffn_w"], dn2, dh2)

    dmerged = _mm("d_merged", dh1, w["w_out"], "nt", out_dtype=BF16)
    grads["w_out"] = _mm("g_out", merged, dh1, "tn", out_dtype=BF16)

    def merge_bwd_fn(i, gates, yr, ys, dm):
        valid = _row_ids(i, T) >= PAD_ROWS
        _, vjp = jax.vjp(lambda a, b, c: _merge(a, b, c, valid), gates, yr, ys)
        return vjp(dm)

    dgates, dyr, dys = _rows("merge_bwd", merge_bwd_fn, T, 1,
                             [(p_gates, 2048, c0), (y_ret_o, 1024, c0), (y_ssd_o, 1024, c0), (dmerged, 1024, c0)],
                             [], [(2048, 2048, c0, BF16), (1024, 1024, c0, BF16), (1024, 1024, c0, BF16)])
    dproj = {"gates": dgates}

    da_ssd = _mm("d_ssd_act", dys, w["w_ssd_out"], "nt", out_dtype=BF16)
    grads["w_ssd_out"] = _mm("g_ssd_out", a_ssd, dys, "tn", out_dtype=BF16)

    def ssd_post_bwd_fn(i, yf, yb, xv, zv, da, dk, nw):
        _, vjp = jax.vjp(_ssd_post, yf, yb, xv, zv, dk, nw)
        dyf, _, dxv, dzv, ddk, dnw = vjp(da)
        return dyf, dxv, dzv, ddk, dnw

    d_main = lax.empty(p_main.shape, BF16)

    def into_main(name, width, cf=c0):
        base = seg_at[name] // width
        return (d_main, width, lambda j: base + cf(j), BF16)

    dy_ssd, dxs_skip, d_main, g_dskip_e, grads["ssd_norm_w"] = _rows(
        "ssd_post_bwd", ssd_post_bwd_fn, T, SSD_GROUPS,
        [(y_dir[0], gw_, gcol), (y_dir[1], gw_, gcol), (act["xs"], gw_, gcol), seg("z", gw_, gcol),
         (da_ssd, gw_, gcol)],
        [(dskip_e, gw_, gcol), (w["ssd_norm_w"], gw_, gcol)],
        [(2048, gw_, gcol, BF16), (2048, gw_, gcol, BF16), into_main("z", gw_, gcol)],
        [(1, 2048, gw_, gcol), (1, 2048, gw_, gcol)])
    grads["d_skip"] = g_dskip_e.reshape(SSD_HEADS, SSD_HEAD_DIM).sum(axis=1).reshape(1, SSD_HEADS)

    dxs_dir, db_dir, dc_dir, draw, g_bias, g_alog = _ssd_bwd(act["xs"], act["B"], act["C"], small, states, dy_ssd, T)
    grads["dt_bias_f"], grads["dt_bias_b"] = g_bias[0].reshape(1, SSD_HEADS), g_bias[1].reshape(1, SSD_HEADS)
    grads["a_log_f"], grads["a_log_b"] = g_alog[0].reshape(1, SSD_HEADS), g_alog[1].reshape(1, SSD_HEADS)
    d_dt = jnp.stack(draw).transpose(2, 0, 1, 3).reshape(T, 2 * SSD_HEADS)
    dproj["dt"] = jnp.pad(d_dt, ((0, 0), (0, CHUNK - 2 * SSD_HEADS))).astype(BF16)

    def make_conv_bwd(nsum):
        def fn(i, xe, *rest):
            ds, (cw, cb) = rest[:nsum], rest[nsum:]
            r = _row_ids(i, T, True)
            dact = ds[0]
            for t in ds[1:]:
                dact = dact + t
            dact = jnp.where(r >= PAD_ROWS, dact, 0.0)
            pre = _conv3(xe, cw) + cb
            sg = jax.nn.sigmoid(pre)
            dpre = dact * (sg * (1.0 + pre * (1.0 - sg)))
            n = xe.shape[0]
            dpc = _center(dpre)
            dw = jnp.concatenate([jnp.sum(dpc * _center(pltpu.roll(xe, 1, 0)), axis=0, keepdims=True),
                                  jnp.sum(dpc * _center(xe), axis=0, keepdims=True),
                                  jnp.sum(dpc * _center(pltpu.roll(xe, n - 1, 0)), axis=0, keepdims=True)], axis=0)
            return _center(_conv3_t(dpre, cw)), dw, jnp.sum(dpc, axis=0, keepdims=True)
        return fn

    g_cw, g_cb = {}, {}
    cots = {"xs": [(dxs_dir[0], 512, gcol), (dxs_dir[1], 512, gcol), (dxs_skip, 512, gcol)],
            "B": [(db_dir[0], 512, gcol), (db_dir[1], 512, gcol)],
            "C": [(dc_dir[0], 512, gcol), (dc_dir[1], 512, gcol)]}
    for name in ("xs", "B", "C"):
        wd = conv_w[name].shape[1]
        d_main, g_cw[name], g_cb[name] = _rows(
            "ssd_conv_bwd_" + name, make_conv_bwd(len(cots[name])), T, wd // 512,
            [seg(name, 512, gcol)] + cots[name], [(conv_w[name], 512, gcol), (conv_b[name], 512, gcol)],
            [into_main(name, 512, gcol)], [(3, wd, 512, gcol), (1, wd, 512, gcol)], halo=True)
    grads["w_ssd_conv"] = jnp.concatenate([g_cw["xs"], g_cw["B"], g_cw["C"]], axis=1)
    grads["b_ssd_conv"] = jnp.concatenate([g_cb["xs"], g_cb["B"], g_cb["C"]], axis=1)

    da_ret = _mm("d_ret_act", dyr, w["w_ret_out"], "nt", out_dtype=BF16)
    grads["w_ret_out"] = _mm("g_ret_out", a_ret, dyr, "tn", out_dtype=BF16)
    tick = early_grads({n: grads.pop(n) for n in ("w_ffn_up_t", "w_ret_out", "w_ssd_out", "w_out", "w_ffn_down")})

    def ret_post_bwd_fn(i, y, g, da, gw):
        _, vjp = jax.vjp(_ret_post, y, g, gw)
        return vjp(da)

    dy_ret, d_main, grads["ret_gn_w"] = _rows(
        "ret_post_bwd", ret_post_bwd_fn, T, 1, [(y_ret, 1024, c0), seg("g", 1024), (da_ret, 1024, c0)],
        [(w["ret_gn_w"] + tick, 1024, c0)], [(1024, 1024, c0, BF16), into_main("g", 1024)], [(1, 1024, 1024, c0)])
    d_main = _retention("retention_dv", kr, qr, dy_ret, T, RET_QK_DIM, RET_V_DIM, into=(d_main, seg_at["v"]))
    dqr = _retention("retention_dq", dy_ret, v_at, kr, T, RET_V_DIM, RET_QK_DIM)
    dkr = _retention("retention_dk", v_at, dy_ret, qr, T, RET_V_DIM, RET_QK_DIM)

    def rot_bwd_fn(i, dq, dk, csv, snv):
        parts = [_rot_t(dq[:, h * 128:(h + 1) * 128], csv, snv) for h in range(RET_HEADS)]
        parts += [_rot_t(dk[:, h * 128:(h + 1) * 128] * scale, csv, snv) for h in range(RET_HEADS)]
        return (jnp.concatenate(parts, axis=1),)

    d_main = _rows("rotary_bwd", rot_bwd_fn, T, 1, [(dqr, 512, c0), (dkr, 512, c0), (cs, 128, c0), (sn, 128, c0)],
                   [], [into_main("qk", 1024)], tall=True)[0]

    g_in = [_mm("g_in_main", d_main, u, "tn", out_dtype=BF16),
            _mm("g_in_dt", dproj["dt"], u, "tn", out_dtype=BF16)[:2 * SSD_HEADS],
            _mm("g_in_gates", dproj["gates"], u, "tn", out_dtype=BF16)]
    tick = in_grads(jnp.concatenate(g_in, axis=0))
    du = _mm("d_u_dt", dproj["dt"] + tick.astype(BF16), w_dt, "nn")
    du = _mm("d_u_main", d_main, w_main, "nn", add=du)
    du = _mm("d_u_gates", dproj["gates"], w_gates, "nn", add=du)
    dh0, grads["norm_mix_w"] = norm_bwd("norm_mix_bwd", h0, w["norm_mix_w"], du, dh1)
    grads["meta_tokens"] = dh0[PAD_ROWS:CHUNK]
    return loss, dh0[CHUNK:], grads


MESH_ID = pl.DeviceIdType.MESH
ANY = pl.BlockSpec(memory_space=pl.ANY)


def _me_and_peers():
    x, y, c = lax.axis_index("x"), lax.axis_index("y"), lax.axis_index("c")
    peers = []
    for k in range(1, N_DEV):
        px = 1 - x if k & 4 else x
        py = 1 - y if k & 2 else y
        pc = 1 - c if k & 1 else c
        peers.append(((px, py, pc), 4 * px + 2 * py + pc))
    return 4 * x + 2 * y + c, peers


def _push_blocks(name, src, per_peer):
    blk = src.shape[1:] if per_peer else src.shape

    def body(src_ref, out_ref, send_sems, recv_sems, local_sem):
        me, peers = _me_and_peers()
        mine = src_ref.at[me] if per_peer else src_ref
        local = pltpu.make_async_copy(mine, out_ref.at[me], local_sem)
        local.start()
        sends = []
        for k, (dev, idx) in enumerate(peers):
            cp = pltpu.make_async_remote_copy(
                src_ref=src_ref.at[idx] if per_peer else src_ref, dst_ref=out_ref.at[me],
                send_sem=send_sems.at[k], recv_sem=recv_sems.at[k], device_id=dev, device_id_type=MESH_ID)
            cp.start()
            sends.append(cp)
        for k, (dev, idx) in enumerate(peers):
            pltpu.make_async_remote_copy(
                src_ref=mine, dst_ref=out_ref.at[idx], send_sem=send_sems.at[k], recv_sem=recv_sems.at[k],
                device_id=dev, device_id_type=MESH_ID).wait_recv()
        for cp in sends:
            cp.wait_send()
        local.wait()

    return pl.pallas_call(
        body, name=name, in_specs=[ANY], out_specs=ANY,
        out_shape=jax.ShapeDtypeStruct((N_DEV,) + tuple(blk), src.dtype),
        scratch_shapes=[pltpu.SemaphoreType.DMA((N_DEV - 1,)), pltpu.SemaphoreType.DMA((N_DEV - 1,)),
                        pltpu.SemaphoreType.DMA],
    )(src)


def _gather_two_level(name, src):
    def body(x_ref, out_ref, send_sems, recv_sems, local_sem):
        x, y, c = lax.axis_index("x"), lax.axis_index("y"), lax.axis_index("c")
        me, sibling = (x, y, c), (x, y, 1 - c)
        chips = [(1 - x, y), (x, 1 - y), (1 - x, 1 - y)]

        def rows(px, py, pc):
            return out_ref.at[4 * px + 2 * py + pc]

        def copy(k, block, to, src_ref=None):
            return pltpu.make_async_remote_copy(
                src_ref=rows(*block) if src_ref is None else src_ref, dst_ref=rows(*block),
                send_sem=send_sems.at[k], recv_sem=recv_sems.at[k], device_id=to, device_id_type=MESH_ID)

        mine = pltpu.make_async_copy(x_ref, rows(*me), local_sem)
        mine.start()
        first = [copy(0, me, sibling, x_ref)] + [copy(1 + j, me, (*chip, c), x_ref) for j, chip in enumerate(chips)]
        for cp in first:
            cp.start()
        passed = [copy(4 + j, (*chip, c), sibling) for j, chip in enumerate(chips)]
        for j, chip in enumerate(chips):
            copy(1 + j, (*chip, c), me).wait_recv()
            passed[j].start()
        copy(0, sibling, me).wait_recv()
        for j, chip in enumerate(chips):
            copy(4 + j, (*chip, 1 - c), me).wait_recv()
        for cp in first + passed:
            cp.wait_send()
        mine.wait()

    return pl.pallas_call(
        body, name=name, in_specs=[ANY], out_specs=ANY,
        out_shape=jax.ShapeDtypeStruct((N_DEV,) + tuple(src.shape), src.dtype),
        scratch_shapes=[pltpu.SemaphoreType.DMA((N_DEV - 1,)), pltpu.SemaphoreType.DMA((N_DEV - 1,)),
                        pltpu.SemaphoreType.DMA],
    )(src)


HBM = pl.BlockSpec(memory_space=pltpu.HBM)
SEM = pl.BlockSpec(memory_space=pltpu.SEMAPHORE)
EFFECT = pltpu.SideEffectType.DATAFLOW_SIDE_EFFECTING


def _peer_copy(src_ref, land_ref, send_sems, recv_sems, per_peer, me, a, k, dev, idx, receiving):
    s = a * (N_DEV - 1) + k
    return pltpu.make_async_remote_copy(
        src_ref=src_ref.at[idx] if per_peer else src_ref, dst_ref=land_ref.at[idx if receiving else me],
        send_sem=send_sems.at[s], recv_sem=recv_sems.at[s], device_id=dev, device_id_type=MESH_ID)


def _push_start(name, srcs, per_peer):
    n = len(srcs)
    land_shapes = [(N_DEV,) + tuple(s.shape[1:] if per_peer else s.shape) for s in srcs]

    def body(*refs):
        src_refs, land_refs, send_sems, recv_sems, token = refs[:n], refs[n:2 * n], refs[2 * n], refs[2 * n + 1], refs[-1]
        me, peers = _me_and_peers()
        for a in range(n):
            for k, (dev, idx) in enumerate(peers):
                _peer_copy(src_refs[a], land_refs[a], send_sems, recv_sems, per_peer, me, a, k, dev, idx, False).start()
        token[...] = jnp.zeros_like(token)

    sems = pltpu.SemaphoreType.DMA((n * (N_DEV - 1),))
    res = pl.pallas_call(
        body, name=name,
        out_shape=(sems, sems, *[pltpu.HBM(s.shape, s.dtype) for s in srcs],
                   *[pltpu.HBM(ls, s.dtype) for ls, s in zip(land_shapes, srcs)], jax.ShapeDtypeStruct((8, 128), F32)),
        in_specs=(HBM,) * (2 * n), out_specs=(SEM, SEM) + (HBM,) * (2 * n) + (pl.BlockSpec(memory_space=pltpu.VMEM),),
        input_output_aliases={i: 2 + i for i in range(2 * n)},
        compiler_params=pltpu.CompilerParams(has_side_effects=EFFECT),
    )(*[pltpu.with_memory_space_constraint(s, pltpu.HBM) for s in srcs],
      *[pltpu.with_memory_space_constraint(lax.empty(ls, s.dtype), pltpu.HBM) for ls, s in zip(land_shapes, srcs)])
    return res[0], res[1], res[2:2 + n], res[2 + n:2 + 2 * n], res[-1]


def _push_wait(name, send_sems, recv_sems, srcs_thru, lands_thru, after, per_peer):
    n = len(srcs_thru)

    def body(*refs):
        src_refs, land_refs, send_sems, recv_sems = refs[:n], refs[n:2 * n], refs[2 * n], refs[2 * n + 1]
        me, peers = _me_and_peers()
        for a in range(n):
            for k, (dev, idx) in enumerate(peers):
                cp = _peer_copy(src_refs[a], land_refs[a], send_sems, recv_sems, per_peer, me, a, k, dev, idx, True)
                cp.wait_send()
                cp.wait_recv()

    both = list(srcs_thru) + list(lands_thru)
    res = pl.pallas_call(
        body, name=name, out_shape=tuple(pltpu.HBM(t.shape, t.dtype) for t in both),
        in_specs=(HBM,) * (2 * n) + (SEM, SEM, ANY), out_specs=(HBM,) * (2 * n),
        input_output_aliases={i: i for i in range(2 * n)},
        compiler_params=pltpu.CompilerParams(has_side_effects=EFFECT),
    )(*both, send_sems, recv_sems, after)
    return res[:n], res[n:]


def _sum_blocks(name, blocks):
    _, R, C = blocks.shape
    tc = next(t for t in (1024, 512, 256, 128) if C % t == 0 and (N_DEV * R * t * 2 <= 6 * 2 ** 20 or t == 128))

    def body(b_ref, o_ref):
        acc = b_ref[0].astype(F32)
        for k in range(1, N_DEV):
            acc = acc + b_ref[k].astype(F32)
        o_ref[...] = acc

    return pl.pallas_call(
        body, name=name, grid=(C // tc,), in_specs=[pl.BlockSpec((N_DEV, R, tc), lambda j: (0, 0, j))],
        out_specs=pl.BlockSpec((R, tc), lambda j: (0, j)), out_shape=jax.ShapeDtypeStruct((R, C), F32),
        compiler_params=_params(("arbitrary",)),
    )(blocks)


def _adamw(name, w, g, m, v):
    R, C = w.shape
    tr = R if R <= 512 else _pick(R, (256, 184, 176, 128, 8))
    spec = pl.BlockSpec((tr, C), lambda i: (i, 0))

    def body(w_ref, g_ref, m_ref, v_ref, d_ref, mo_ref, vo_ref):
        gv = g_ref[...]
        mn = ADAM_B1 * m_ref[...] + (1.0 - ADAM_B1) * gv
        vn = ADAM_B2 * v_ref[...] + (1.0 - ADAM_B2) * jnp.square(gv)
        m_hat = mn / (1.0 - ADAM_B1 ** ADAM_STEP)
        v_hat = vn / (1.0 - ADAM_B2 ** ADAM_STEP)
        d_ref[...] = -ADAM_LR * (m_hat / (jnp.sqrt(v_hat) + ADAM_EPS) + ADAM_WD * w_ref[...])
        mo_ref[...] = mn
        vo_ref[...] = vn

    return pl.pallas_call(
        body, name=name, grid=(R // tr,), in_specs=[spec] * 4, out_specs=[spec] * 3,
        out_shape=[jax.ShapeDtypeStruct((R, C), F32)] * 3, compiler_params=_params(("arbitrary",)),
    )(w, g, m, v)


WEIGHTS = ("meta_tokens", "norm_mix_w", "w_in", "ret_gn_w", "w_ret_out", "w_ssd_conv", "b_ssd_conv", "dt_bias_f",
           "dt_bias_b", "a_log_f", "a_log_b", "d_skip", "ssd_norm_w", "w_ssd_out", "w_out", "norm_ffn_w", "w_ffn_up",
           "w_ffn_conv", "b_ffn_conv", "w_ffn_down", "final_norm_w")
BIG = (("w_in", 1288, True), ("w_ffn_up", 704, True), ("w_ret_out", 128, False), ("w_ssd_out", 256, False),
       ("w_out", 128, False), ("w_ffn_down", 352, False))
REPLICATED = ("norm_mix_w", "ret_gn_w", "b_ssd_conv", "dt_bias_f", "dt_bias_b", "a_log_f", "a_log_b", "d_skip",
              "ssd_norm_w", "norm_ffn_w", "b_ffn_conv", "final_norm_w")
SMALL_SHARDED = (("meta_tokens", 16, 1024), ("w_ssd_conv", 3, 3072), ("w_ffn_conv", 3, 5632))


BIG_IN, BIG_REST = BIG[:1], BIG[1:]


def _pack_flat(arrays, rows):
    flat = jnp.concatenate([a.reshape(-1) for a in arrays])
    return jnp.pad(flat, (0, rows * D_MODEL - flat.shape[0])).reshape(rows, D_MODEL)


def _unpack_flat(slab, shapes):
    flat, out, o = slab.reshape(-1), [], 0
    for s in shapes:
        n = math.prod(s)
        out.append(flat[o:o + n].reshape(s))
        o += n
    return out


def kernel(x, meta_tokens, norm_mix_w, w_in, ret_gn_w, w_ret_out, w_ssd_conv, b_ssd_conv, dt_bias_f, dt_bias_b, a_log_f, a_log_b, d_skip, ssd_norm_w, w_ssd_out, w_out, norm_ffn_w, w_ffn_up, w_ffn_conv, b_ffn_conv, w_ffn_down, final_norm_w, loss_target, m_meta_tokens, m_norm_mix_w, m_w_in, m_ret_gn_w, m_w_ret_out, m_w_ssd_conv, m_b_ssd_conv, m_dt_bias_f, m_dt_bias_b, m_a_log_f, m_a_log_b, m_d_skip, m_ssd_norm_w, m_w_ssd_out, m_w_out, m_norm_ffn_w, m_w_ffn_up, m_w_ffn_conv, m_b_ffn_conv, m_w_ffn_down, m_final_norm_w, v_meta_tokens, v_norm_mix_w, v_w_in, v_ret_gn_w, v_w_ret_out, v_w_ssd_conv, v_b_ssd_conv, v_dt_bias_f, v_dt_bias_b, v_a_log_f, v_a_log_b, v_d_skip, v_ssd_norm_w, v_w_ssd_out, v_w_out, v_norm_ffn_w, v_w_ffn_up, v_w_ffn_conv, v_b_ffn_conv, v_w_ffn_down, v_final_norm_w):
    given = dict(locals())
    wt = {n: given[n] for n in WEIGHTS}
    mt = {n: given["m_" + n] for n in WEIGHTS}
    vt = {n: given["v_" + n] for n in WEIGHTS}
    me = 4 * lax.axis_index("x") + 2 * lax.axis_index("y") + lax.axis_index("c")

    small_names = [n for n, _, _ in SMALL_SHARDED]
    small_local = lambda tree: [tree[n].reshape(r, c // N_DEV) for n, r, c in SMALL_SHARDED]
    slab_view = lambda tree, name, transposed: tree[name][0].T if transposed else tree[name][0]
    all_in = _gather_two_level("gather_w_in", slab_view(wt, "w_in", True).astype(BF16))
    all_s = _push_blocks("gather_small", _pack_flat(small_local(wt), 8), False)
    rest_srcs = [slab_view(wt, name, t).astype(BF16) for name, _, t in BIG_REST]
    rest_srcs, all_in, all_s = lax.optimization_barrier((rest_srcs, all_in, all_s))
    rest_flight = _push_start("gather_rest_start", rest_srcs, False)
    all_s = all_s.reshape(N_DEV, -1)
    full = {"w_in_t": all_in.reshape(-1, D_MODEL)}

    def lands_with_own(flight, after, per_peer, name):
        srcs, lands = _push_wait(name, *flight[:4], after, per_peer)
        own = lambda s: lax.dynamic_slice_in_dim(s, me, 1, axis=0) if per_peer else s[None]
        return [lax.dynamic_update_slice_in_dim(land, own(s), me, axis=0) for s, land in zip(srcs, lands)]

    def late_weights(after):
        lands = lands_with_own(rest_flight, after, False, "gather_rest_wait")
        return {name + ("_t" if t else ""): land.reshape(N_DEV * r, D_MODEL) for (name, r, t), land in zip(BIG_REST, lands)}

    flights = {}

    def start_exchange(key, group, gd):
        srcs = [gd[name + ("_t" if t else "")].astype(BF16).reshape(N_DEV, r, D_MODEL) for name, r, t in group]
        flights[key] = _push_start("exchange_" + key + "_start", srcs, True)
        return flights[key][4][0, 0]

    o = 0
    for name, r, c in SMALL_SHARDED:
        n = r * c // N_DEV
        full[name] = all_s[:, o:o + n].reshape(N_DEV, r, c // N_DEV).transpose(1, 0, 2).reshape(r, c)
        o += n
    for name in REPLICATED:
        full[name] = wt[name]

    grads, delta, new_m, new_v = {}, {}, {}, {}

    def finish_exchange(key, group, after):
        lands = lands_with_own(flights[key], after, True, "exchange_" + key + "_wait")
        for (name, _, transposed), land in zip(group, lands):
            back = (lambda a: a.T[None]) if transposed else (lambda a: a[None])
            g_sum = _sum_blocks("sum_" + name, land)
            d, mn, vn = _adamw("adamw_" + name, slab_view(wt, name, transposed), g_sum,
                               slab_view(mt, name, transposed), slab_view(vt, name, transposed))
            grads[name], delta[name], new_m[name], new_v[name] = back(g_sum), back(d), back(mn), back(vn)

    def in_grads(gi):
        tick = start_exchange("in", BIG_IN, {"w_in_t": gi})
        finish_exchange("rest", BIG_REST, flights["in"][4])
        tick, _ = lax.optimization_barrier((tick, [delta[name] for name, _, _ in BIG_REST]))
        return tick

    loss, grad_x, g = _local_step(x[0], loss_target[0], full, rest_flight[4][0, 0], late_weights,
                                  lambda gd: start_exchange("rest", BIG_REST, gd), in_grads)

    finish_exchange("in", BIG_IN, g["norm_mix_w"])
    small_parts = [g[n] for n in REPLICATED] + [g[n] for n in small_names] + [loss.reshape(1)]
    g_small = _sum_blocks("sum_small", _push_blocks("gather_small_grads", _pack_flat(small_parts, 64), False))
    small_red = _unpack_flat(g_small, [wt[n].shape for n in REPLICATED] + [(r, c) for _, r, c in SMALL_SHARDED] + [(1,)])
    grads.update(zip(REPLICATED, small_red[:len(REPLICATED)]))
    for (name, r, c), red in zip(SMALL_SHARDED, small_red[len(REPLICATED):-1]):
        grads[name] = lax.dynamic_slice(red, (0, me * (c // N_DEV)), (r, c // N_DEV)).reshape(wt[name].shape)
    loss_all = small_red[-1][0]

    rest = list(REPLICATED) + small_names
    shapes = [wt[n].shape for n in rest]
    pack_rest = lambda tree: _pack_flat([tree[n] for n in rest], 24)
    d_rest, m_rest, v_rest = _adamw("adamw_small", pack_rest(wt), pack_rest(grads), pack_rest(mt), pack_rest(vt))
    delta.update(zip(rest, _unpack_flat(d_rest, shapes)))
    new_m.update(zip(rest, _unpack_flat(m_rest, shapes)))
    new_v.update(zip(rest, _unpack_flat(v_rest, shapes)))

    return (loss_all, grad_x[None], *[grads[n] for n in WEIGHTS], *[delta[n] for n in WEIGHTS],
            *[new_m[n] for n in WEIGHTS], *[new_v[n] for n in WEIGHTS])
```

```python
import functools
import math

import jax
import jax.numpy as jnp
from jax import lax
from jax.experimental import pallas as pl
from jax.experimental.pallas import tpu as pltpu

F32 = jnp.float32
BF16 = jnp.bfloat16

D_MODEL = 1024
CHUNK = 128
N_META = 16
PAD_ROWS = CHUNK - N_META
RET_HEADS = 4
RET_QK_DIM = 128
RET_V_DIM = 256
SSD_HEADS = 32
SSD_HEAD_DIM = 64
SSD_GROUPS = 4
SSD_STATE = 128
HEADS_PER_GROUP = SSD_HEADS // SSD_GROUPS
PAIRS_PER_GROUP = HEADS_PER_GROUP // 2
D_FF = 2816
EPS = 1e-6
ROPE_BASE = 10000.0
N_DEV = 8

ADAM_LR = 0.001
ADAM_B1 = 0.9
ADAM_B2 = 0.999
ADAM_EPS = 1e-08
ADAM_WD = 0.01
ADAM_STEP = 10

VMEM_LIMIT = 56 * 1024 * 1024
HALO = 16
HIGHEST = lax.Precision.HIGHEST

SEGMENTS = (("qk", 0, 1024), ("v", 1024, 2048), ("g", 2048, 3072), ("z", 3072, 5120), ("xs", 5120, 7168),
            ("B", 7168, 7680), ("C", 7680, 8192), ("dt", 8192, 8256), ("gates", 8256, 10304))


def _pick(n, cands):
    for c in cands:
        if n % c == 0:
            return c
    raise ValueError(f"no tile for {n}")


def _params(sem):
    return pltpu.CompilerParams(dimension_semantics=sem, vmem_limit_bytes=VMEM_LIMIT)


def _dot(a, b, dims=(((1,), (0,)), ((), ())), precision=None):
    return lax.dot_general(a, b, dims, preferred_element_type=F32, precision=precision)


def _dot_nt(a, b):
    return _dot(a, b, (((1,), (1,)), ((), ())))


def _dot_tn(a, b):
    return _dot(a, b, (((0,), (0,)), ((), ())))


def _mm(name, a, b, mode, add=None, out_dtype=F32):
    if mode == "nn":
        (M, K), N = a.shape, b.shape[1]
    elif mode == "nt":
        (M, K), N = a.shape, b.shape[0]
    else:
        (K, M), N = a.shape, b.shape[1]
    tn = _pick(N, (1408, 1024, 512, 128, 64))
    if mode == "tn":
        tm = M if M <= 1024 else _pick(M, (1408, 1024))
        tk = _pick(K, (2112, 512, 256, 128))
    else:
        tm = _pick(M, (1056, 512, 256, 128))
        tk = K if K <= 2816 else _pick(K, (2048, 1408, 1024))
    nk = K // tk
    if mode == "nn":
        a_spec = pl.BlockSpec((tm, tk), lambda n, m, k: (m, k))
        b_spec = pl.BlockSpec((tk, tn), lambda n, m, k: (k, n))
        dims = (((1,), (0,)), ((), ()))
    elif mode == "nt":
        a_spec = pl.BlockSpec((tm, tk), lambda n, m, k: (m, k))
        b_spec = pl.BlockSpec((tn, tk), lambda n, m, k: (n, k))
        dims = (((1,), (1,)), ((), ()))
    else:
        a_spec = pl.BlockSpec((tk, tm), lambda n, m, k: (k, m))
        b_spec = pl.BlockSpec((tk, tn), lambda n, m, k: (k, n))
        dims = (((0,), (0,)), ((), ()))
    o_spec = pl.BlockSpec((tm, tn), lambda n, m, k: (m, n))
    in_specs = [a_spec, b_spec] + ([o_spec] if add is not None else [])
    args = [a, b] + ([add] if add is not None else [])

    def body(*refs):
        if add is not None:
            a_ref, b_ref, r_ref, o_ref, acc = refs
        else:
            a_ref, b_ref, o_ref, acc = refs
        k = pl.program_id(2)
        p = _dot(a_ref[...].astype(BF16), b_ref[...].astype(BF16), dims)

        def finish(r):
            if add is not None:
                r = r + r_ref[...]
            o_ref[...] = r.astype(out_dtype)

        if nk == 1:
            finish(p)
        else:
            @pl.when(k == 0)
            def _():
                acc[...] = p

            @pl.when(k > 0)
            def _():
                acc[...] += p

            @pl.when(k == nk - 1)
            def _():
                finish(acc[...])

    return pl.pallas_call(
        body, name=name, grid=(N // tn, M // tm, nk), in_specs=in_specs, out_specs=o_spec,
        out_shape=jax.ShapeDtypeStruct((M, N), out_dtype),
        scratch_shapes=[pltpu.VMEM((tm, tn) if nk > 1 else (8, 128), F32)],
        compiler_params=_params(("arbitrary", "arbitrary", "arbitrary")),
    )(*args)


ANY_SPACE = pl.BlockSpec(memory_space=pl.ANY)


def _const(c):
    return lambda j: c


def _rows(name, fn, T, ncol, ins, params, outs, accs=(), halo=False, tall=False):
    tm = _pick(T, (1056, 512, 256, 128)) if tall else _pick(T, (384, 256, 128))
    R = T // tm
    hb = tm // HALO
    in_specs, args = [], []
    for spec in ins:
        arr, w, cf = spec[:3]
        lead = spec[3] if len(spec) > 3 else None
        if len(spec) > 4:
            rows, rf = spec[4]
            in_specs.append(pl.BlockSpec((rows, w), lambda j, i, cf=cf, rf=rf: (rf(i), cf(j))))
            args.append(arr)
            continue
        if lead is None:
            mk = lambda blk, rf, cf=cf: pl.BlockSpec(blk, lambda j, i: (rf(i), cf(j)))
            shape = lambda r, w=w: (r, w)
        else:
            mk = lambda blk, rf, cf=cf, lead=lead: pl.BlockSpec(blk, lambda j, i: (lead, rf(i), cf(j)))
            shape = lambda r, w=w: (None, r, w)
        in_specs.append(mk(shape(tm), lambda i: i))
        args.append(arr)
        if halo:
            in_specs.append(mk(shape(HALO), lambda i: jnp.maximum(i * hb - 1, 0)))
            in_specs.append(mk(shape(HALO), lambda i: jnp.minimum((i + 1) * hb, T // HALO - 1)))
            args += [arr, arr]
    for arr, w, cf in params:
        in_specs.append(pl.BlockSpec((arr.shape[0], w), lambda j, i, cf=cf: (0, cf(j))))
        args.append(arr)
    out_shape, out_specs, aliases = [], [], {}
    for k, (tw, w, cf, dt) in enumerate(outs):
        if not isinstance(tw, int):
            aliases[len(args)] = k
            in_specs.append(ANY_SPACE)
            args.append(tw)
            tw = tw.shape[1]
        out_shape.append(jax.ShapeDtypeStruct((T, tw), dt))
        out_specs.append(pl.BlockSpec((tm, w), lambda j, i, cf=cf: (i, cf(j))))
    for r, tw, w, cf in accs:
        out_shape.append(jax.ShapeDtypeStruct((r, tw), F32))
        out_specs.append(pl.BlockSpec((r, w), lambda j, i, cf=cf: (0, cf(j))))
    n_in, n_par, n_out, n_acc, n_alias = len(ins), len(params), len(outs), len(accs), len(aliases)

    def body(*refs):
        i = pl.program_id(1)
        vals, p = [], 0
        for _ in range(n_in):
            if halo:
                before = jnp.where(i > 0, refs[p + 1][...], jnp.zeros_like(refs[p + 1]))
                after = jnp.where(i < R - 1, refs[p + 2][...], jnp.zeros_like(refs[p + 2]))
                vals.append(jnp.concatenate([before, refs[p][...], after], axis=0).astype(F32))
                p += 3
            else:
                vals.append(refs[p][...].astype(F32))
                p += 1
        pvals = [refs[p + k][...] for k in range(n_par)]
        p += n_par + n_alias
        res = fn(i, *vals, *pvals)
        for k in range(n_out):
            refs[p + k][...] = res[k].astype(refs[p + k].dtype)
        p += n_out
        for k in range(n_acc):
            ref, v = refs[p + k], res[n_out + k]

            @pl.when(i == 0)
            def _(ref=ref, v=v):
                ref[...] = v

            @pl.when(i > 0)
            def _(ref=ref, v=v):
                ref[...] += v

    res = pl.pallas_call(
        body, name=name, grid=(ncol, R), in_specs=in_specs, out_specs=out_specs, out_shape=out_shape,
        input_output_aliases=aliases, compiler_params=_params(("arbitrary", "arbitrary")),
    )(*args)
    return res


def _tile_rows(T):
    return _pick(T, (384, 256, 128))


def _row_ids(i, T, halo=False):
    tm = _tile_rows(T)
    if halo:
        return i * tm - HALO + lax.broadcasted_iota(jnp.int32, (tm + 2 * HALO, 1), 0)
    return i * tm + lax.broadcasted_iota(jnp.int32, (tm, 1), 0)


def _rms(x, w):
    return x * lax.rsqrt(jnp.mean(x * x, axis=-1, keepdims=True) + EPS) * w


def _silu(x):
    return x * jax.nn.sigmoid(x)


def _conv3(x, w):
    n = x.shape[0]
    return w[0:1] * pltpu.roll(x, 1, 0) + w[1:2] * x + w[2:3] * pltpu.roll(x, n - 1, 0)


def _conv3_t(d, w):
    n = d.shape[0]
    return w[0:1] * pltpu.roll(d, n - 1, 0) + w[1:2] * d + w[2:3] * pltpu.roll(d, 1, 0)


def _center(x):
    return x[HALO:x.shape[0] - HALO]


def _retention(name, a, b, v, T, da, dv, into=None):
    (a, a0), (b, b0), (v, v0) = [t if isinstance(t, tuple) else (t, 0) for t in (a, b, v)]
    nc = T // CHUNK
    log_gammas = [math.log(1.0 - 2.0 ** (-5.0 - h)) for h in range(RET_HEADS)]

    def body(*refs):
        a_ref, b_ref, v_ref = refs[:3]
        out_ref, o_ref, st, st_b = refs[-4:]
        h = pl.program_id(0)
        lg = jnp.float32(log_gammas[RET_HEADS - 1])
        for k in range(RET_HEADS - 2, -1, -1):
            lg = jnp.where(h == k, jnp.float32(log_gammas[k]), lg)
        li = lax.broadcasted_iota(jnp.int32, (CHUNK, CHUNK), 0)
        si = lax.broadcasted_iota(jnp.int32, (CHUNK, CHUNK), 1)
        dmat = jnp.exp(lg * jnp.abs(li - si).astype(F32))
        pos = lax.broadcasted_iota(jnp.int32, (CHUNK, 1), 0).astype(F32)
        kdec_f = jnp.exp((CHUNK - 1 - pos) * lg)
        qdec_f = jnp.exp((pos + 1) * lg)
        kdec_b = jnp.exp(pos * lg)
        qdec_b = jnp.exp((CHUNK - pos) * lg)
        cdec = jnp.exp(CHUNK * lg)

        def rows(n):
            return pl.ds(pl.multiple_of(n * CHUNK, CHUNK), CHUNK)

        st[...] = jnp.zeros_like(st)
        st_b[...] = jnp.zeros_like(st_b)
        o_ref[...] = jnp.zeros_like(o_ref)

        def step(m, carry):
            r = rows(m)
            av, bv, vv = a_ref[r, :], b_ref[r, :], v_ref[r, :].astype(BF16)
            s = _dot_nt(av.astype(BF16), bv.astype(BF16)) * dmat
            o_ref[r, :] += _dot(s.astype(BF16), vv) + _dot((av * qdec_f).astype(BF16), st[...].astype(BF16))
            st[...] = cdec * st[...] + _dot_tn((bv * kdec_f).astype(BF16), vv)
            r = rows(nc - 1 - m)
            av, bv, vv = a_ref[r, :], b_ref[r, :], v_ref[r, :].astype(BF16)
            o_ref[r, :] += _dot((av * qdec_b).astype(BF16), st_b[...].astype(BF16))
            st_b[...] = cdec * st_b[...] + _dot_tn((bv * kdec_b).astype(BF16), vv)
            return carry

        lax.fori_loop(0, nc, step, 0, unroll=True)
        out_ref[...] = o_ref[...].astype(out_ref.dtype)

    in_specs = [pl.BlockSpec((T, da), lambda h: (0, a0 // da + h)), pl.BlockSpec((T, da), lambda h: (0, b0 // da + h)),
                pl.BlockSpec((T, dv), lambda h: (0, v0 // dv + h))]
    if into is None:
        args, o0, aliases = (a, b, v), 0, {}
        out_shape = jax.ShapeDtypeStruct((T, RET_HEADS * dv), F32)
    else:
        args, o0, aliases = (a, b, v, into[0]), into[1], {3: 0}
        in_specs.append(ANY_SPACE)
        out_shape = jax.ShapeDtypeStruct(into[0].shape, into[0].dtype)
    return pl.pallas_call(
        body, name=name, grid=(RET_HEADS,), in_specs=in_specs,
        out_specs=pl.BlockSpec((T, dv), lambda h: (0, o0 // dv + h)), out_shape=out_shape,
        input_output_aliases=aliases,
        scratch_shapes=[pltpu.VMEM((T, dv), F32), pltpu.VMEM((da, dv), F32), pltpu.VMEM((da, dv), F32)],
        compiler_params=_params(("arbitrary",)),
    )(*args)


def _softplus(x):
    return jnp.maximum(x, 0.0) + jnp.log1p(jnp.exp(-jnp.abs(x)))


def _lane_lo():
    return lax.broadcasted_iota(jnp.int32, (1, CHUNK), 1) < SSD_HEAD_DIM


def _pair_cols(col, j):
    return jnp.where(_lane_lo(), col[:, 2 * j:2 * j + 1], col[:, 2 * j + 1:2 * j + 2])


def _pair_rows(colr, j):
    lo = lax.broadcasted_iota(jnp.int32, (CHUNK, 1), 0) < SSD_HEAD_DIM
    return jnp.where(lo, colr[2 * j:2 * j + 1, :], colr[2 * j + 1:2 * j + 2, :])


def _onehot8(h):
    return (lax.broadcasted_iota(jnp.int32, (1, HEADS_PER_GROUP), 1) == h).astype(F32)


def _ssd_pre(d, c, rawc, rawr, bc, br, alc, alr):
    li = lax.broadcasted_iota(jnp.int32, (CHUNK, CHUNK), 0)
    si = lax.broadcasted_iota(jnp.int32, (CHUNK, CHUNK), 1)
    dif = li - si if d == 0 else si - li
    mask = dif >= 0
    mask_t = dif <= 0
    rowc = c * CHUNK + lax.broadcasted_iota(jnp.int32, (CHUNK, 1), 0)
    rowr = c * CHUNK + lax.broadcasted_iota(jnp.int32, (1, CHUNK), 1)
    dtc = jnp.where(rowc >= PAD_ROWS, _softplus(rawc + bc), 0.0)
    dtr = jnp.where(rowr >= PAD_ROWS, _softplus(rawr + br), 0.0)
    ac = -jnp.exp(alc)
    ar = -jnp.exp(alr)
    dlc = dtc * ac
    dlr = dtr * ar
    alpc = _dot(mask.astype(F32), dlc, precision=HIGHEST)
    alpr = _dot(dlr, mask_t.astype(F32), precision=HIGHEST)
    endc = jnp.sum(dlc, axis=0, keepdims=True)
    endr = jnp.sum(dlr, axis=1, keepdims=True)
    return dict(mask=mask, mask_t=mask_t, dtc=dtc, ac=ac, alpc=alpc, alpr=alpr, endc=endc, endr=endr,
                valid=rowc >= PAD_ROWS)


def _chunk_of(d, n, nc):
    return n + d * (nc - 1 - 2 * n)


GROUP_WIDTH = HEADS_PER_GROUP * SSD_HEAD_DIM


def _chunks_per_step(nc, most=3):
    return next(c for c in (11, 3, 1) if c <= most and nc % c == 0)


def _ssd_in_specs(d, cfn, rows):
    return [
        pl.BlockSpec((rows, GROUP_WIDTH), lambda g, n: (cfn(d, n), g)),
        pl.BlockSpec((rows, SSD_STATE), lambda g, n: (cfn(d, n), g)),
        pl.BlockSpec((rows, SSD_STATE), lambda g, n: (cfn(d, n), g)),
        pl.BlockSpec((None, None, rows, HEADS_PER_GROUP), lambda g, n: (d, g, cfn(d, n), 0)),
        pl.BlockSpec((None, None, HEADS_PER_GROUP, rows), lambda g, n: (d, g, 0, cfn(d, n))),
        pl.BlockSpec((None, None, 1, HEADS_PER_GROUP), lambda g, n: (d, g, 0, 0)),
        pl.BlockSpec((None, None, HEADS_PER_GROUP, 1), lambda g, n: (d, g, 0, 0)),
        pl.BlockSpec((None, None, 1, HEADS_PER_GROUP), lambda g, n: (d, g, 0, 0)),
        pl.BlockSpec((None, None, HEADS_PER_GROUP, 1), lambda g, n: (d, g, 0, 0)),
    ]


N_SSD_IN = 9


def _ssd_fwd(xs, bm, cm, small, T):
    nc = T // CHUNK
    cps = _chunks_per_step(nc, 11)
    rows = cps * CHUNK
    cfn = lambda d, n: _chunk_of(d, n, nc // cps)

    def one_direction(d, n, ins, y_ref, hs_ref, h_scr):
        x_ref, b_ref, c_ref, rawc_ref, rawr_ref, *per_group = ins
        for kk in range(cps):
            k = kk if d == 0 else cps - 1 - kk
            r = pl.ds(k * CHUNK, CHUNK)
            one_chunk(d, cfn(d, n) * cps + k,
                      (x_ref.at[r], b_ref.at[r], c_ref.at[r], rawc_ref.at[r], rawr_ref.at[:, r], *per_group),
                      y_ref.at[r], hs_ref.at[k], h_scr)

    def one_chunk(d, c, ins, y_ref, hs_ref, h_scr):
        x_ref, b_ref, c_ref, rawc_ref, rawr_ref, bc_ref, br_ref, alc_ref, alr_ref = ins
        q = _ssd_pre(d, c, rawc_ref[...], rawr_ref[...], bc_ref[...], br_ref[...], alc_ref[...], alr_ref[...])
        bv = b_ref[...].astype(BF16)
        cv = c_ref[...].astype(BF16)
        cb = _dot_nt(cv, bv)
        lo = _lane_lo()
        for j in range(PAIRS_PER_GROUP):
            xp = x_ref[:, j * CHUNK:(j + 1) * CHUNK]
            xd = xp * _pair_cols(q["dtc"], j)
            xdb = xd.astype(BF16)
            yi = []
            for e in range(2):
                h = 2 * j + e
                lm = jnp.exp(jnp.where(q["mask"], q["alpc"][:, h:h + 1] - q["alpr"][h:h + 1, :], -jnp.inf))
                yi.append(_dot((cb * lm).astype(BF16), xdb))
            alp = _pair_cols(q["alpc"], j)
            hp = h_scr[j]
            hs_ref[j] = hp
            yo = jnp.exp(alp) * _dot_nt(cv, hp.astype(BF16))
            y_ref[:, j * CHUNK:(j + 1) * CHUNK] = (jnp.where(lo, yi[0], yi[1]) + yo).astype(y_ref.dtype)
            de = jnp.exp(_pair_cols(q["endc"], j) - alp)
            h_scr[j] = jnp.exp(_pair_rows(q["endr"], j)) * hp + _dot_tn((xd * de).astype(BF16), bv)

    def body(*refs):
        n = pl.program_id(1)
        ins, (y_f, y_b, hs_f, hs_b, h_scr) = refs[:2 * N_SSD_IN], refs[2 * N_SSD_IN:]

        @pl.when(n == 0)
        def _():
            h_scr[...] = jnp.zeros_like(h_scr)

        one_direction(0, n, ins[:N_SSD_IN], y_f, hs_f, h_scr.at[0])
        one_direction(1, n, ins[N_SSD_IN:], y_b, hs_b, h_scr.at[1])

    y_spec = lambda d: pl.BlockSpec((rows, GROUP_WIDTH), lambda g, n: (cfn(d, n), g))
    hs_spec = lambda d: pl.BlockSpec((None, cps, PAIRS_PER_GROUP, CHUNK, SSD_STATE),
                                     lambda g, n: (g, cfn(d, n), 0, 0, 0))
    y_shape = jax.ShapeDtypeStruct((T, SSD_HEADS * SSD_HEAD_DIM), BF16)
    hs_shape = jax.ShapeDtypeStruct((SSD_GROUPS, nc, PAIRS_PER_GROUP, CHUNK, SSD_STATE), F32)
    y_f, y_b, hs_f, hs_b = pl.pallas_call(
        body, name="ssd_fwd", grid=(SSD_GROUPS, nc // cps),
        in_specs=_ssd_in_specs(0, cfn, rows) + _ssd_in_specs(1, cfn, rows),
        out_specs=[y_spec(0), y_spec(1), hs_spec(0), hs_spec(1)],
        out_shape=[y_shape, y_shape, hs_shape, hs_shape],
        scratch_shapes=[pltpu.VMEM((2, PAIRS_PER_GROUP, CHUNK, SSD_STATE), F32)],
        compiler_params=_params(("arbitrary", "arbitrary")),
    )(xs, bm, cm, *small, xs, bm, cm, *small)
    return (y_f, y_b), (hs_f, hs_b)


def _ssd_bwd(xs, bm, cm, small, hs, dy, T):
    nc = T // CHUNK
    cps = _chunks_per_step(nc, 11)
    rows = cps * CHUNK
    cfn = lambda d, n: _chunk_of(1 - d, n, nc // cps)

    def one_direction(d, n, ins, outs, dh_scr):
        x_ref, b_ref, c_ref, rawc_ref, rawr_ref, bc_ref, br_ref, alc_ref, alr_ref, hs_ref, dy_ref = ins
        dx_ref, db_ref, dc_ref, draw_ref, dbias_ref, dalog_ref = outs
        for kk in range(cps):
            k = cps - 1 - kk if d == 0 else kk
            r = pl.ds(k * CHUNK, CHUNK)
            one_chunk(d, cfn(d, n) * cps + k, n if kk == 0 else None,
                      (x_ref.at[r], b_ref.at[r], c_ref.at[r], rawc_ref.at[r], rawr_ref.at[:, r], bc_ref, br_ref,
                       alc_ref, alr_ref, hs_ref.at[k], dy_ref.at[r]),
                      (dx_ref.at[r], db_ref.at[r], dc_ref.at[r], draw_ref.at[r], dbias_ref, dalog_ref), dh_scr)

    def one_chunk(d, c, first_of_step, ins, outs, dh_scr):
        x_ref, b_ref, c_ref, rawc_ref, rawr_ref, bc_ref, br_ref, alc_ref, alr_ref, hs_ref, dy_ref = ins
        dx_ref, db_ref, dc_ref, draw_ref, dbias_ref, dalog_ref = outs
        rawc, bc = rawc_ref[...], bc_ref[...]
        q = _ssd_pre(d, c, rawc, rawr_ref[...], bc, br_ref[...], alc_ref[...], alr_ref[...])
        b32, c32 = b_ref[...], c_ref[...]
        bv, cv = b32.astype(BF16), c32.astype(BF16)
        cb = _dot_nt(cv, bv)
        cbt = _dot_nt(bv, cv)
        lo = _lane_lo()
        row_lo = lax.broadcasted_iota(jnp.int32, (CHUNK, 1), 0) < SSD_HEAD_DIM
        dcb = jnp.zeros((CHUNK, CHUNK), F32)
        dcp = jnp.zeros((CHUNK, SSD_STATE), F32)
        dbp = jnp.zeros((CHUNK, SSD_STATE), F32)
        dalp = jnp.zeros((CHUNK, HEADS_PER_GROUP), F32)
        dend = jnp.zeros((1, HEADS_PER_GROUP), F32)
        ddtx = jnp.zeros((CHUNK, HEADS_PER_GROUP), F32)

        def half_sums(t):
            return (jnp.sum(jnp.where(lo, t, 0.0), axis=1, keepdims=True),
                    jnp.sum(jnp.where(lo, 0.0, t), axis=1, keepdims=True))

        for j in range(PAIRS_PER_GROUP):
            xp = x_ref[:, j * CHUNK:(j + 1) * CHUNK]
            dtp = _pair_cols(q["dtc"], j)
            xd = xp * dtp
            xdb = xd.astype(BF16)
            dyp = dy_ref[:, j * CHUNK:(j + 1) * CHUNK]
            dyb = dyp.astype(BF16)
            hn = hs_ref[j]
            hnb = hn.astype(BF16)
            dh1 = dh_scr[j]
            dh1b = dh1.astype(BF16)
            alp = _pair_cols(q["alpc"], j)
            ea = jnp.exp(alp)
            de = jnp.exp(_pair_cols(q["endc"], j) - alp)
            dxi = []
            for e in range(2):
                h = 2 * j + e
                diff = q["alpc"][:, h:h + 1] - q["alpr"][h:h + 1, :]
                lm = jnp.exp(jnp.where(q["mask"], diff, -jnp.inf))
                mt = cbt * jnp.exp(jnp.where(q["mask_t"], -diff, -jnp.inf))
                dxi.append(_dot(mt.astype(BF16), dyb))
                dyeb_h = (jnp.where(lo, dyp, 0.0) if e == 0 else jnp.where(lo, 0.0, dyp)).astype(BF16)
                gl = _dot_nt(dyeb_h, xdb) * lm
                dcb = dcb + gl
                ra = jnp.sum(gl * cb - _dot_nt(xdb, dyeb_h) * mt, axis=1, keepdims=True)
                dalp = dalp + ra * _onehot8(h)
            y_off = ea * _dot_nt(cv, hnb)
            dxs_state = de * _dot_nt(bv, dh1b)
            dxd = jnp.where(lo, dxi[0], dxi[1]) + dxs_state
            dyeb = (dyp * ea).astype(BF16)
            dcp = dcp + _dot(dyeb, hnb)
            dbp = dbp + _dot((xd * de).astype(BF16), dh1b)
            dh_scr[j] = jnp.exp(_pair_rows(q["endr"], j)) * dh1 + _dot_tn(dyeb, cv)
            r0, r1 = half_sums(dyp * y_off - xd * dxs_state)
            dalp = dalp + r0 * _onehot8(2 * j) + r1 * _onehot8(2 * j + 1)
            t0, t1 = half_sums(jnp.sum(xd * dxs_state, axis=0, keepdims=True))
            u = dh1 * hn
            u0 = jnp.sum(jnp.sum(jnp.where(row_lo, u, 0.0), axis=0, keepdims=True), axis=1, keepdims=True)
            u1 = jnp.sum(jnp.sum(jnp.where(row_lo, 0.0, u), axis=0, keepdims=True), axis=1, keepdims=True)
            eend = jnp.exp(q["endc"])
            dend = dend + (t0 + eend * u0) * _onehot8(2 * j) + (t1 + eend * u1) * _onehot8(2 * j + 1)
            dx_ref[:, j * CHUNK:(j + 1) * CHUNK] = (dxd * dtp).astype(dx_ref.dtype)
            w0, w1 = half_sums(dxd * xp)
            ddtx = ddtx + w0 * _onehot8(2 * j) + w1 * _onehot8(2 * j + 1)

        dcbb = dcb.astype(BF16)
        dc_ref[...] = (dcp + _dot(dcbb, bv)).astype(dc_ref.dtype)
        db_ref[...] = (dbp + _dot_tn(dcbb, cv)).astype(db_ref.dtype)
        ddl = _dot(q["mask_t"].astype(F32), dalp, precision=HIGHEST) + dend
        ddt = ddl * q["ac"] + ddtx
        draw = jnp.where(q["valid"], ddt * jax.nn.sigmoid(rawc + bc), 0.0)
        draw_ref[...] = draw
        dbias = jnp.sum(draw, axis=0, keepdims=True)
        dalog = jnp.sum(ddl * q["dtc"], axis=0, keepdims=True) * q["ac"]

        def add():
            dbias_ref[...] += dbias
            dalog_ref[...] += dalog

        if first_of_step is None:
            add()
        else:
            @pl.when(first_of_step == 0)
            def _():
                dbias_ref[...] = dbias
                dalog_ref[...] = dalog

            pl.when(first_of_step > 0)(add)

    n_in, n_out = N_SSD_IN + 2, 6

    def body(*refs):
        n = pl.program_id(1)
        ins, outs, dh_scr = refs[:2 * n_in], refs[2 * n_in:2 * (n_in + n_out)], refs[-1]

        @pl.when(n == 0)
        def _():
            dh_scr[...] = jnp.zeros_like(dh_scr)

        one_direction(0, n, ins[:n_in], outs[:n_out], dh_scr.at[0])
        one_direction(1, n, ins[n_in:], outs[n_out:], dh_scr.at[1])

    def in_specs(d):
        return _ssd_in_specs(d, cfn, rows) + [
            pl.BlockSpec((None, cps, PAIRS_PER_GROUP, CHUNK, SSD_STATE), lambda g, n: (g, cfn(d, n), 0, 0, 0)),
            pl.BlockSpec((rows, GROUP_WIDTH), lambda g, n: (cfn(d, n), g))]

    def out_specs(d):
        acc = pl.BlockSpec((None, 1, HEADS_PER_GROUP), lambda g, n: (g, 0, 0))
        return [pl.BlockSpec((rows, GROUP_WIDTH), lambda g, n: (cfn(d, n), g)),
                pl.BlockSpec((rows, SSD_STATE), lambda g, n: (cfn(d, n), g)),
                pl.BlockSpec((rows, SSD_STATE), lambda g, n: (cfn(d, n), g)),
                pl.BlockSpec((None, rows, HEADS_PER_GROUP), lambda g, n: (g, cfn(d, n), 0)), acc, acc]

    out_shape = [jax.ShapeDtypeStruct((T, SSD_HEADS * SSD_HEAD_DIM), BF16),
                 jax.ShapeDtypeStruct((T, SSD_GROUPS * SSD_STATE), BF16),
                 jax.ShapeDtypeStruct((T, SSD_GROUPS * SSD_STATE), BF16),
                 jax.ShapeDtypeStruct((SSD_GROUPS, T, HEADS_PER_GROUP), F32),
                 jax.ShapeDtypeStruct((SSD_GROUPS, 1, HEADS_PER_GROUP), F32),
                 jax.ShapeDtypeStruct((SSD_GROUPS, 1, HEADS_PER_GROUP), F32)]
    res = pl.pallas_call(
        body, name="ssd_bwd", grid=(SSD_GROUPS, nc // cps),
        in_specs=in_specs(0) + in_specs(1), out_specs=out_specs(0) + out_specs(1), out_shape=out_shape * 2,
        scratch_shapes=[pltpu.VMEM((2, PAIRS_PER_GROUP, CHUNK, SSD_STATE), F32)],
        compiler_params=_params(("arbitrary", "arbitrary")),
    )(xs, bm, cm, *small, hs[0], dy, xs, bm, cm, *small, hs[1], dy)
    return [(res[k], res[n_out + k]) for k in range(n_out)]


def _rot(x, cs, sn):
    return x * cs + pltpu.roll(x, RET_QK_DIM // 2, 1) * sn


def _rot_t(d, cs, sn):
    return d * cs + pltpu.roll(d * sn, RET_QK_DIM // 2, 1)


def _ret_post(y, g, w):
    parts = []
    for h in range(RET_HEADS):
        yh = y[:, h * RET_V_DIM:(h + 1) * RET_V_DIM]
        mu = jnp.mean(yh, axis=-1, keepdims=True)
        var = jnp.mean(jnp.square(yh - mu), axis=-1, keepdims=True)
        parts.append((yh - mu) * lax.rsqrt(var + EPS))
    return _silu(g) * (jnp.concatenate(parts, axis=1) * w)


def _ssd_post(yf, yb, xs, z, dskip, w):
    y = (yf + yb + xs * dskip) * _silu(z)
    return y * lax.rsqrt(jnp.mean(y * y, axis=-1, keepdims=True) + EPS) * w


def _merge(gates, yr, ys, valid):
    m = jax.nn.sigmoid(gates[:, :D_MODEL]) * yr + jax.nn.sigmoid(gates[:, D_MODEL:]) * ys
    return jnp.where(valid, m, 0.0)


def _rope_tables(T):
    half = RET_QK_DIM // 2
    inv = ROPE_BASE ** (-jnp.arange(half, dtype=F32) / half)
    pos = (jnp.arange(T) - PAD_ROWS).astype(F32)
    ang = pos[:, None] * inv[None, :]
    cos, sin = jnp.cos(ang), jnp.sin(ang)
    return jnp.concatenate([cos, cos], axis=1), jnp.concatenate([-sin, sin], axis=1)


def _per_group(v):
    c = v.reshape(SSD_GROUPS, 1, HEADS_PER_GROUP)
    return c, c.reshape(SSD_GROUPS, HEADS_PER_GROUP, 1)


def _local_step(x, target, w, tick, late_weights, early_grads, in_grads):
    S = x.shape[0]
    T = S + CHUNK
    tm = _tile_rows(T)
    c0 = _const(0)

    h0 = jnp.concatenate([jnp.zeros((PAD_ROWS, D_MODEL), F32), w["meta_tokens"], x], axis=0)
    seg_at = {name: a for name, a, _ in SEGMENTS}
    w_main = w["w_in_t"][:seg_at["dt"]]
    w_dt = jnp.pad(w["w_in_t"][seg_at["dt"]:seg_at["gates"]], ((0, CHUNK - 2 * SSD_HEADS), (0, 0)))
    w_gates = w["w_in_t"][seg_at["gates"]:]

    def norm_cast(name, h, nw):
        return _rows(name, lambda i, hv, wv: (_rms(hv, wv),), T, 1, [(h, D_MODEL, c0)], [(nw, D_MODEL, c0)],
                     [(D_MODEL, D_MODEL, c0, BF16)], tall=True)[0]

    u = norm_cast("norm_mix", h0, w["norm_mix_w"] + tick)
    p_main = _mm("proj_main", u, w_main, "nt", out_dtype=BF16)
    p_dt = _mm("proj_dt", u, w_dt, "nt")
    p_gates = _mm("proj_gates", u, w_gates, "nt", out_dtype=BF16)

    def seg(name, width, cf=c0):
        base = seg_at[name] // width
        return (p_main, width, lambda j: base + cf(j))

    cs, sn = _rope_tables(T)
    scale = RET_QK_DIM ** -0.5

    def rot_fn(i, qk, csv, snv):
        q = [_rot(qk[:, h * 128:(h + 1) * 128], csv, snv) for h in range(RET_HEADS)]
        k = [_rot(qk[:, (RET_HEADS + h) * 128:(RET_HEADS + h + 1) * 128], csv, snv) * scale for h in range(RET_HEADS)]
        return jnp.concatenate(q, axis=1), jnp.concatenate(k, axis=1)

    qr, kr = _rows("rotary", rot_fn, T, 1, [seg("qk", 1024), (cs, 128, c0), (sn, 128, c0)], [],
                   [(512, 512, c0, F32), (512, 512, c0, F32)], tall=True)
    v_at = (p_main, seg_at["v"])
    y_ret = _retention("retention", qr, kr, v_at, T, RET_QK_DIM, RET_V_DIM)
    a_ret = _rows("ret_post", lambda i, y, g, gw: (_ret_post(y, g, gw),), T, 1,
                  [(y_ret, 1024, c0), seg("g", 1024)], [(w["ret_gn_w"], 1024, c0)],
                  [(1024, 1024, c0, BF16)], tall=True)[0]

    conv_w = {"xs": w["w_ssd_conv"][:, :2048], "B": w["w_ssd_conv"][:, 2048:2560], "C": w["w_ssd_conv"][:, 2560:]}
    conv_b = {"xs": w["b_ssd_conv"][:, :2048], "B": w["b_ssd_conv"][:, 2048:2560], "C": w["b_ssd_conv"][:, 2560:]}

    def ssd_conv_fn(i, xe, cw, cb):
        r = _row_ids(i, T, True)
        return (_center(jnp.where(r >= PAD_ROWS, _silu(_conv3(xe, cw) + cb), 0.0)),)

    act = {}
    for name in ("xs", "B", "C"):
        wd = conv_w[name].shape[1]
        cw = 512
        act[name] = _rows("ssd_conv_" + name, ssd_conv_fn, T, wd // cw, [seg(name, cw, lambda j: j)],
                          [(conv_w[name], cw, lambda j: j), (conv_b[name], cw, lambda j: j)],
                          [(wd, cw, lambda j: j, BF16)], halo=True)[0]

    raw = p_dt[:, :2 * SSD_HEADS].reshape(T, 2, SSD_GROUPS, HEADS_PER_GROUP)
    rawc = raw.transpose(1, 2, 0, 3)
    rawr = raw.transpose(1, 2, 3, 0)
    bias = [_per_group(w["dt_bias_f"]), _per_group(w["dt_bias_b"])]
    alog = [_per_group(w["a_log_f"]), _per_group(w["a_log_b"])]
    small = (rawc, rawr, jnp.stack([bias[0][0], bias[1][0]]), jnp.stack([bias[0][1], bias[1][1]]),
             jnp.stack([alog[0][0], alog[1][0]]), jnp.stack([alog[0][1], alog[1][1]]))
    y_dir, states = _ssd_fwd(act["xs"], act["B"], act["C"], small, T)

    dskip_e = jnp.repeat(w["d_skip"], SSD_HEAD_DIM, axis=1)
    gcol = lambda j: j
    gw_ = 512
    a_ssd = _rows("ssd_post", lambda i, yf, yb, xv, zv, dk, nw: (_ssd_post(yf, yb, xv, zv, dk, nw),), T, SSD_GROUPS,
                  [(y_dir[0], gw_, gcol), (y_dir[1], gw_, gcol), (act["xs"], gw_, gcol), seg("z", gw_, gcol)],
                  [(dskip_e, gw_, gcol), (w["ssd_norm_w"], gw_, gcol)], [(2048, gw_, gcol, BF16)])[0]

    w = dict(w, **late_weights(a_ssd))
    w_up_g, w_up_u = w["w_ffn_up_t"][:D_FF], w["w_ffn_up_t"][D_FF:]
    y_ret_o = _mm("ret_out", a_ret, w["w_ret_out"], "nn", out_dtype=BF16)
    y_ssd_o = _mm("ssd_out", a_ssd, w["w_ssd_out"], "nn", out_dtype=BF16)

    def merge_fn(i, gates, yr, ys):
        return (_merge(gates, yr, ys, _row_ids(i, T) >= PAD_ROWS),)

    merged = _rows("merge", merge_fn, T, 1, [(p_gates, 2048, c0), (y_ret_o, 1024, c0), (y_ssd_o, 1024, c0)], [],
                   [(1024, 1024, c0, BF16)])[0]
    h1 = _mm("mix_out", merged, w["w_out"], "nn", add=h0)

    n2 = norm_cast("norm_ffn", h1, w["norm_ffn_w"])
    f_pre = _mm("ffn_up", n2, w["w_ffn_up_t"], "nt", out_dtype=BF16)
    cwg, cwu = w["w_ffn_conv"][:, :D_FF], w["w_ffn_conv"][:, D_FF:]
    cbg, cbu = w["b_ffn_conv"][:, :D_FF], w["b_ffn_conv"][:, D_FF:]
    fcol = lambda j: j
    fw = 1408

    def ffn_act_fn(i, ge, ue, wg, wu, bg, bu):
        return (_center(_silu(_conv3(ge, wg) + bg) * (_conv3(ue, wu) + bu)),)

    ucol = lambda j: D_FF // fw + j
    a2 = _rows("ffn_act", ffn_act_fn, T, D_FF // fw, [(f_pre, fw, fcol), (f_pre, fw, ucol)],
               [(cwg, fw, fcol), (cwu, fw, fcol), (cbg, fw, fcol), (cbu, fw, fcol)], [(D_FF, fw, fcol, BF16)],
               halo=True)[0]
    h2 = _mm("ffn_down", a2, w["w_ffn_down"], "nn", add=h1)

    fnw = w["final_norm_w"].reshape(1, D_MODEL)

    per_tile = tm // CHUNK
    tgt_specs = [(target, D_MODEL, c0, None, (CHUNK, lambda i, k=k: jnp.maximum(per_tile * i - 1 + k, 0)))
                 for k in range(per_tile)]

    def loss_fn(i, hv, *rest):
        tv, nw = jnp.concatenate(rest[:per_tile], axis=0), rest[per_tile]
        valid = _row_ids(i, T) >= CHUNK
        y, vjp = jax.vjp(_rms, hv, nw)
        diff = jnp.where(valid, y - tv, 0.0)
        dh, dw = vjp(diff * (1.0 / D_MODEL))
        part = 0.5 / D_MODEL * jnp.sum(jnp.sum(diff * diff, axis=1, keepdims=True), axis=0, keepdims=True)
        return dh, jnp.broadcast_to(part, (1, 128)), dw

    dh2, loss_acc, d_fnw = _rows("loss", loss_fn, T, 1, [(h2, D_MODEL, c0)] + tgt_specs, [(fnw, D_MODEL, c0)],
                                 [(D_MODEL, D_MODEL, c0, F32)], [(1, 128, 128, c0), (1, D_MODEL, D_MODEL, c0)])
    loss = loss_acc[0, 0]
    grads = {"final_norm_w": d_fnw.reshape(D_MODEL)}

    da2 = _mm("d_ffn_act", dh2, w["w_ffn_down"], "nt", out_dtype=BF16)
    grads["w_ffn_down"] = _mm("g_ffn_down", a2, dh2, "tn", out_dtype=BF16)

    def ffn_bwd_fn(i, ge, ue, de, wg, wu, bg, bu):
        fg = _conv3(ge, wg) + bg
        fu = _conv3(ue, wu) + bu
        sg = jax.nn.sigmoid(fg)
        dfg = de * fu * (sg * (1.0 + fg * (1.0 - sg)))
        dfu = de * (fg * sg)
        n = ge.shape[0]

        def wgrad(df, xe):
            df_c = _center(df)
            return jnp.concatenate([jnp.sum(df_c * _center(pltpu.roll(xe, 1, 0)), axis=0, keepdims=True),
                                    jnp.sum(df_c * _center(xe), axis=0, keepdims=True),
                                    jnp.sum(df_c * _center(pltpu.roll(xe, n - 1, 0)), axis=0, keepdims=True)], axis=0)

        return (_center(_conv3_t(dfg, wg)), _center(_conv3_t(dfu, wu)), wgrad(dfg, ge), wgrad(dfu, ue),
                jnp.sum(_center(dfg), axis=0, keepdims=True), jnp.sum(_center(dfu), axis=0, keepdims=True))

    dfg_pre, dfu_pre, g_cwg, g_cwu, g_cbg, g_cbu = _rows(
        "ffn_act_bwd", ffn_bwd_fn, T, D_FF // fw, [(f_pre, fw, fcol), (f_pre, fw, ucol), (da2, fw, fcol)],
        [(cwg, fw, fcol), (cwu, fw, fcol), (cbg, fw, fcol), (cbu, fw, fcol)],
        [(D_FF, fw, fcol, BF16), (D_FF, fw, fcol, BF16)],
        [(3, D_FF, fw, fcol), (3, D_FF, fw, fcol), (1, D_FF, fw, fcol), (1, D_FF, fw, fcol)], halo=True)
    grads["w_ffn_conv"] = jnp.concatenate([g_cwg, g_cwu], axis=1)
    grads["b_ffn_conv"] = jnp.concatenate([g_cbg, g_cbu], axis=1)
    dn2 = _mm("d_norm_ffn_g", dfg_pre, w_up_g, "nn")
    dn2 = _mm("d_norm_ffn_u", dfu_pre, w_up_u, "nn", add=dn2)
    grads["w_ffn_up_t"] = jnp.concatenate([_mm("g_ffn_up_g", dfg_pre, n2, "tn", out_dtype=BF16), _mm("g_ffn_up_u", dfu_pre, n2, "tn", out_dtype=BF16)],
                                          axis=0)

    def norm_bwd(name, h, nw, dn, dres):
        def fn(i, hv, dnv, drv, wv):
            _, vjp = jax.vjp(_rms, hv, wv)
            dh, dw = vjp(dnv)
            return dh + drv, dw
        return _rows(name, fn, T, 1, [(h, D_MODEL, c0), (dn, D_MODEL, c0), (dres, D_MODEL, c0)], [(nw, D_MODEL, c0)],
                     [(D_MODEL, D_MODEL, c0, F32)], [(1, D_MODEL, D_MODEL, c0)])

    dh1, grads["norm_ffn_w"] = norm_bwd("norm_ffn_bwd", h1, w["norm_ffn_w"], dn2, dh2)

    dmerged = _mm("d_merged", dh1, w["w_out"], "nt", out_dtype=BF16)
    grads["w_out"] = _mm("g_out", merged, dh1, "tn", out_dtype=BF16)

    def merge_bwd_fn(i, gates, yr, ys, dm):
        valid = _row_ids(i, T) >= PAD_ROWS
        _, vjp = jax.vjp(lambda a, b, c: _merge(a, b, c, valid), gates, yr, ys)
        return vjp(dm)

    dgates, dyr, dys = _rows("merge_bwd", merge_bwd_fn, T, 1,
                             [(p_gates, 2048, c0), (y_ret_o, 1024, c0), (y_ssd_o, 1024, c0), (dmerged, 1024, c0)],
                             [], [(2048, 2048, c0, BF16), (1024, 1024, c0, BF16), (1024, 1024, c0, BF16)])
    dproj = {"gates": dgates}

    da_ssd = _mm("d_ssd_act", dys, w["w_ssd_out"], "nt", out_dtype=BF16)
    grads["w_ssd_out"] = _mm("g_ssd_out", a_ssd, dys, "tn", out_dtype=BF16)

    def ssd_post_bwd_fn(i, yf, yb, xv, zv, da, dk, nw):
        _, vjp = jax.vjp(_ssd_post, yf, yb, xv, zv, dk, nw)
        dyf, _, dxv, dzv, ddk, dnw = vjp(da)
        return dyf, dxv, dzv, ddk, dnw

    d_main = lax.empty(p_main.shape, BF16)

    def into_main(name, width, cf=c0):
        base = seg_at[name] // width
        return (d_main, width, lambda j: base + cf(j), BF16)

    dy_ssd, dxs_skip, d_main, g_dskip_e, grads["ssd_norm_w"] = _rows(
        "ssd_post_bwd", ssd_post_bwd_fn, T, SSD_GROUPS,
        [(y_dir[0], gw_, gcol), (y_dir[1], gw_, gcol), (act["xs"], gw_, gcol), seg("z", gw_, gcol),
         (da_ssd, gw_, gcol)],
        [(dskip_e, gw_, gcol), (w["ssd_norm_w"], gw_, gcol)],
        [(2048, gw_, gcol, BF16), (2048, gw_, gcol, BF16), into_main("z", gw_, gcol)],
        [(1, 2048, gw_, gcol), (1, 2048, gw_, gcol)])
    grads["d_skip"] = g_dskip_e.reshape(SSD_HEADS, SSD_HEAD_DIM).sum(axis=1).reshape(1, SSD_HEADS)

    dxs_dir, db_dir, dc_dir, draw, g_bias, g_alog = _ssd_bwd(act["xs"], act["B"], act["C"], small, states, dy_ssd, T)
    grads["dt_bias_f"], grads["dt_bias_b"] = g_bias[0].reshape(1, SSD_HEADS), g_bias[1].reshape(1, SSD_HEADS)
    grads["a_log_f"], grads["a_log_b"] = g_alog[0].reshape(1, SSD_HEADS), g_alog[1].reshape(1, SSD_HEADS)
    d_dt = jnp.stack(draw).transpose(2, 0, 1, 3).reshape(T, 2 * SSD_HEADS)
    dproj["dt"] = jnp.pad(d_dt, ((0, 0), (0, CHUNK - 2 * SSD_HEADS))).astype(BF16)

    def make_conv_bwd(nsum):
        def fn(i, xe, *rest):
            ds, (cw, cb) = rest[:nsum], rest[nsum:]
            r = _row_ids(i, T, True)
            dact = ds[0]
            for t in ds[1:]:
                dact = dact + t
            dact = jnp.where(r >= PAD_ROWS, dact, 0.0)
            pre = _conv3(xe, cw) + cb
            sg = jax.nn.sigmoid(pre)
            dpre = dact * (sg * (1.0 + pre * (1.0 - sg)))
            n = xe.shape[0]
            dpc = _center(dpre)
            dw = jnp.concatenate([jnp.sum(dpc * _center(pltpu.roll(xe, 1, 0)), axis=0, keepdims=True),
                                  jnp.sum(dpc * _center(xe), axis=0, keepdims=True),
                                  jnp.sum(dpc * _center(pltpu.roll(xe, n - 1, 0)), axis=0, keepdims=True)], axis=0)
            return _center(_conv3_t(dpre, cw)), dw, jnp.sum(dpc, axis=0, keepdims=True)
        return fn

    g_cw, g_cb = {}, {}
    cots = {"xs": [(dxs_dir[0], 512, gcol), (dxs_dir[1], 512, gcol), (dxs_skip, 512, gcol)],
            "B": [(db_dir[0], 512, gcol), (db_dir[1], 512, gcol)],
            "C": [(dc_dir[0], 512, gcol), (dc_dir[1], 512, gcol)]}
    for name in ("xs", "B", "C"):
        wd = conv_w[name].shape[1]
        d_main, g_cw[name], g_cb[name] = _rows(
            "ssd_conv_bwd_" + name, make_conv_bwd(len(cots[name])), T, wd // 512,
            [seg(name, 512, gcol)] + cots[name], [(conv_w[name], 512, gcol), (conv_b[name], 512, gcol)],
            [into_main(name, 512, gcol)], [(3, wd, 512, gcol), (1, wd, 512, gcol)], halo=True)
    grads["w_ssd_conv"] = jnp.concatenate([g_cw["xs"], g_cw["B"], g_cw["C"]], axis=1)
    grads["b_ssd_conv"] = jnp.concatenate([g_cb["xs"], g_cb["B"], g_cb["C"]], axis=1)

    da_ret = _mm("d_ret_act", dyr, w["w_ret_out"], "nt", out_dtype=BF16)
    grads["w_ret_out"] = _mm("g_ret_out", a_ret, dyr, "tn", out_dtype=BF16)
    tick = early_grads({n: grads.pop(n) for n in ("w_ffn_up_t", "w_ret_out", "w_ssd_out", "w_out", "w_ffn_down")})

    def ret_post_bwd_fn(i, y, g, da, gw):
        _, vjp = jax.vjp(_ret_post, y, g, gw)
        return vjp(da)

    dy_ret, d_main, grads["ret_gn_w"] = _rows(
        "ret_post_bwd", ret_post_bwd_fn, T, 1, [(y_ret, 1024, c0), seg("g", 1024), (da_ret, 1024, c0)],
        [(w["ret_gn_w"] + tick, 1024, c0)], [(1024, 1024, c0, BF16), into_main("g", 1024)], [(1, 1024, 1024, c0)])
    d_main = _retention("retention_dv", kr, qr, dy_ret, T, RET_QK_DIM, RET_V_DIM, into=(d_main, seg_at["v"]))
    dqr = _retention("retention_dq", dy_ret, v_at, kr, T, RET_V_DIM, RET_QK_DIM)
    dkr = _retention("retention_dk", v_at, dy_ret, qr, T, RET_V_DIM, RET_QK_DIM)

    def rot_bwd_fn(i, dq, dk, csv, snv):
        parts = [_rot_t(dq[:, h * 128:(h + 1) * 128], csv, snv) for h in range(RET_HEADS)]
        parts += [_rot_t(dk[:, h * 128:(h + 1) * 128] * scale, csv, snv) for h in range(RET_HEADS)]
        return (jnp.concatenate(parts, axis=1),)

    d_main = _rows("rotary_bwd", rot_bwd_fn, T, 1, [(dqr, 512, c0), (dkr, 512, c0), (cs, 128, c0), (sn, 128, c0)],
                   [], [into_main("qk", 1024)], tall=True)[0]

    g_in = [_mm("g_in_main", d_main, u, "tn", out_dtype=BF16),
            _mm("g_in_dt", dproj["dt"], u, "tn", out_dtype=BF16)[:2 * SSD_HEADS],
            _mm("g_in_gates", dproj["gates"], u, "tn", out_dtype=BF16)]
    tick = in_grads(jnp.concatenate(g_in, axis=0))
    du = _mm("d_u_dt", dproj["dt"] + tick.astype(BF16), w_dt, "nn")
    du = _mm("d_u_main", d_main, w_main, "nn", add=du)
    du = _mm("d_u_gates", dproj["gates"], w_gates, "nn", add=du)
    dh0, grads["norm_mix_w"] = norm_bwd("norm_mix_bwd", h0, w["norm_mix_w"], du, dh1)
    grads["meta_tokens"] = dh0[PAD_ROWS:CHUNK]
    return loss, dh0[CHUNK:], grads


MESH_ID = pl.DeviceIdType.MESH
ANY = pl.BlockSpec(memory_space=pl.ANY)


def _me_and_peers():
    x, y, c = lax.axis_index("x"), lax.axis_index("y"), lax.axis_index("c")
    peers = []
    for k in range(1, N_DEV):
        px = 1 - x if k & 4 else x
        py = 1 - y if k & 2 else y
        pc = 1 - c if k & 1 else c
        peers.append(((px, py, pc), 4 * px + 2 * py + pc))
    return 4 * x + 2 * y + c, peers


def _push_blocks(name, src, per_peer):
    blk = src.shape[1:] if per_peer else src.shape

    def body(src_ref, out_ref, send_sems, recv_sems, local_sem):
        me, peers = _me_and_peers()
        mine = src_ref.at[me] if per_peer else src_ref
        local = pltpu.make_async_copy(mine, out_ref.at[me], local_sem)
        local.start()
        sends = []
        for k, (dev, idx) in enumerate(peers):
            cp = pltpu.make_async_remote_copy(
                src_ref=src_ref.at[idx] if per_peer else src_ref, dst_ref=out_ref.at[me],
                send_sem=send_sems.at[k], recv_sem=recv_sems.at[k], device_id=dev, device_id_type=MESH_ID)
            cp.start()
            sends.append(cp)
        for k, (dev, idx) in enumerate(peers):
            pltpu.make_async_remote_copy(
                src_ref=mine, dst_ref=out_ref.at[idx], send_sem=send_sems.at[k], recv_sem=recv_sems.at[k],
                device_id=dev, device_id_type=MESH_ID).wait_recv()
        for cp in sends:
            cp.wait_send()
        local.wait()

    return pl.pallas_call(
        body, name=name, in_specs=[ANY], out_specs=ANY,
        out_shape=jax.ShapeDtypeStruct((N_DEV,) + tuple(blk), src.dtype),
        scratch_shapes=[pltpu.SemaphoreType.DMA((N_DEV - 1,)), pltpu.SemaphoreType.DMA((N_DEV - 1,)),
                        pltpu.SemaphoreType.DMA],
    )(src)


def _gather_two_level(name, src):
    def body(x_ref, out_ref, send_sems, recv_sems, local_sem):
        x, y, c = lax.axis_index("x"), lax.axis_index("y"), lax.axis_index("c")
        me, sibling = (x, y, c), (x, y, 1 - c)
        chips = [(1 - x, y), (x, 1 - y), (1 - x, 1 - y)]

        def rows(px, py, pc):
            return out_ref.at[4 * px + 2 * py + pc]

        def copy(k, block, to, src_ref=None):
            return pltpu.make_async_remote_copy(
                src_ref=rows(*block) if src_ref is None else src_ref, dst_ref=rows(*block),
                send_sem=send_sems.at[k], recv_sem=recv_sems.at[k], device_id=to, device_id_type=MESH_ID)

        mine = pltpu.make_async_copy(x_ref, rows(*me), local_sem)
        mine.start()
        first = [copy(0, me, sibling, x_ref)] + [copy(1 + j, me, (*chip, c), x_ref) for j, chip in enumerate(chips)]
        for cp in first:
            cp.start()
        passed = [copy(4 + j, (*chip, c), sibling) for j, chip in enumerate(chips)]
        for j, chip in enumerate(chips):
            copy(1 + j, (*chip, c), me).wait_recv()
            passed[j].start()
        copy(0, sibling, me).wait_recv()
        for j, chip in enumerate(chips):
            copy(4 + j, (*chip, 1 - c), me).wait_recv()
        for cp in first + passed:
            cp.wait_send()
        mine.wait()

    return pl.pallas_call(
        body, name=name, in_specs=[ANY], out_specs=ANY,
        out_shape=jax.ShapeDtypeStruct((N_DEV,) + tuple(src.shape), src.dtype),
        scratch_shapes=[pltpu.SemaphoreType.DMA((N_DEV - 1,)), pltpu.SemaphoreType.DMA((N_DEV - 1,)),
                        pltpu.SemaphoreType.DMA],
    )(src)


HBM = pl.BlockSpec(memory_space=pltpu.HBM)
SEM = pl.BlockSpec(memory_space=pltpu.SEMAPHORE)
EFFECT = pltpu.SideEffectType.DATAFLOW_SIDE_EFFECTING


def _peer_copy(src_ref, land_ref, send_sems, recv_sems, per_peer, me, a, k, dev, idx, receiving):
    s = a * (N_DEV - 1) + k
    return pltpu.make_async_remote_copy(
        src_ref=src_ref.at[idx] if per_peer else src_ref, dst_ref=land_ref.at[idx if receiving else me],
        send_sem=send_sems.at[s], recv_sem=recv_sems.at[s], device_id=dev, device_id_type=MESH_ID)


def _push_start(name, srcs, per_peer):
    n = len(srcs)
    land_shapes = [(N_DEV,) + tuple(s.shape[1:] if per_peer else s.shape) for s in srcs]

    def body(*refs):
        src_refs, land_refs, send_sems, recv_sems, token = refs[:n], refs[n:2 * n], refs[2 * n], refs[2 * n + 1], refs[-1]
        me, peers = _me_and_peers()
        for a in range(n):
            for k, (dev, idx) in enumerate(peers):
                _peer_copy(src_refs[a], land_refs[a], send_sems, recv_sems, per_peer, me, a, k, dev, idx, False).start()
        token[...] = jnp.zeros_like(token)

    sems = pltpu.SemaphoreType.DMA((n * (N_DEV - 1),))
    res = pl.pallas_call(
        body, name=name,
        out_shape=(sems, sems, *[pltpu.HBM(s.shape, s.dtype) for s in srcs],
                   *[pltpu.HBM(ls, s.dtype) for ls, s in zip(land_shapes, srcs)], jax.ShapeDtypeStruct((8, 128), F32)),
        in_specs=(HBM,) * (2 * n), out_specs=(SEM, SEM) + (HBM,) * (2 * n) + (pl.BlockSpec(memory_space=pltpu.VMEM),),
        input_output_aliases={i: 2 + i for i in range(2 * n)},
        compiler_params=pltpu.CompilerParams(has_side_effects=EFFECT),
    )(*[pltpu.with_memory_space_constraint(s, pltpu.HBM) for s in srcs],
      *[pltpu.with_memory_space_constraint(lax.empty(ls, s.dtype), pltpu.HBM) for ls, s in zip(land_shapes, srcs)])
    return res[0], res[1], res[2:2 + n], res[2 + n:2 + 2 * n], res[-1]


def _push_wait(name, send_sems, recv_sems, srcs_thru, lands_thru, after, per_peer):
    n = len(srcs_thru)

    def body(*refs):
        src_refs, land_refs, send_sems, recv_sems = refs[:n], refs[n:2 * n], refs[2 * n], refs[2 * n + 1]
        me, peers = _me_and_peers()
        for a in range(n):
            for k, (dev, idx) in enumerate(peers):
                cp = _peer_copy(src_refs[a], land_refs[a], send_sems, recv_sems, per_peer, me, a, k, dev, idx, True)
                cp.wait_send()
                cp.wait_recv()

    both = list(srcs_thru) + list(lands_thru)
    res = pl.pallas_call(
        body, name=name, out_shape=tuple(pltpu.HBM(t.shape, t.dtype) for t in both),
        in_specs=(HBM,) * (2 * n) + (SEM, SEM, ANY), out_specs=(HBM,) * (2 * n),
        input_output_aliases={i: i for i in range(2 * n)},
        compiler_params=pltpu.CompilerParams(has_side_effects=EFFECT),
    )(*both, send_sems, recv_sems, after)
    return res[:n], res[n:]


def _sum_blocks(name, blocks):
    _, R, C = blocks.shape
    tc = next(t for t in (1024, 512, 256, 128) if C % t == 0 and (N_DEV * R * t * 2 <= 6 * 2 ** 20 or t == 128))

    def body(b_ref, o_ref):
        acc = b_ref[0].astype(F32)
        for k in range(1, N_DEV):
            acc = acc + b_ref[k].astype(F32)
        o_ref[...] = acc

    return pl.pallas_call(
        body, name=name, grid=(C // tc,), in_specs=[pl.BlockSpec((N_DEV, R, tc), lambda j: (0, 0, j))],
        out_specs=pl.BlockSpec((R, tc), lambda j: (0, j)), out_shape=jax.ShapeDtypeStruct((R, C), F32),
        compiler_params=_params(("arbitrary",)),
    )(blocks)


def _adamw(name, w, g, m, v):
    R, C = w.shape
    tr = R if R <= 512 else _pick(R, (256, 184, 176, 128, 8))
    spec = pl.BlockSpec((tr, C), lambda i: (i, 0))

    def body(w_ref, g_ref, m_ref, v_ref, d_ref, mo_ref, vo_ref):
        gv = g_ref[...]
        mn = ADAM_B1 * m_ref[...] + (1.0 - ADAM_B1) * gv
        vn = ADAM_B2 * v_ref[...] + (1.0 - ADAM_B2) * jnp.square(gv)
        m_hat = mn / (1.0 - ADAM_B1 ** ADAM_STEP)
        v_hat = vn / (1.0 - ADAM_B2 ** ADAM_STEP)
        d_ref[...] = -ADAM_LR * (m_hat / (jnp.sqrt(v_hat) + ADAM_EPS) + ADAM_WD * w_ref[...])
        mo_ref[...] = mn
        vo_ref[...] = vn

    return pl.pallas_call(
        body, name=name, grid=(R // tr,), in_specs=[spec] * 4, out_specs=[spec] * 3,
        out_shape=[jax.ShapeDtypeStruct((R, C), F32)] * 3, compiler_params=_params(("arbitrary",)),
    )(w, g, m, v)


WEIGHTS = ("meta_tokens", "norm_mix_w", "w_in", "ret_gn_w", "w_ret_out", "w_ssd_conv", "b_ssd_conv", "dt_bias_f",
           "dt_bias_b", "a_log_f", "a_log_b", "d_skip", "ssd_norm_w", "w_ssd_out", "w_out", "norm_ffn_w", "w_ffn_up",
           "w_ffn_conv", "b_ffn_conv", "w_ffn_down", "final_norm_w")
BIG = (("w_in", 1288, True), ("w_ffn_up", 704, True), ("w_ret_out", 128, False), ("w_ssd_out", 256, False),
       ("w_out", 128, False), ("w_ffn_down", 352, False))
REPLICATED = ("norm_mix_w", "ret_gn_w", "b_ssd_conv", "dt_bias_f", "dt_bias_b", "a_log_f", "a_log_b", "d_skip",
              "ssd_norm_w", "norm_ffn_w", "b_ffn_conv", "final_norm_w")
SMALL_SHARDED = (("meta_tokens", 16, 1024), ("w_ssd_conv", 3, 3072), ("w_ffn_conv", 3, 5632))


BIG_IN, BIG_REST = BIG[:1], BIG[1:]


def _pack_flat(arrays, rows):
    flat = jnp.concatenate([a.reshape(-1) for a in arrays])
    return jnp.pad(flat, (0, rows * D_MODEL - flat.shape[0])).reshape(rows, D_MODEL)


def _unpack_flat(slab, shapes):
    flat, out, o = slab.reshape(-1), [], 0
    for s in shapes:
        n = math.prod(s)
        out.append(flat[o:o + n].reshape(s))
        o += n
    return out


def kernel(x, meta_tokens, norm_mix_w, w_in, ret_gn_w, w_ret_out, w_ssd_conv, b_ssd_conv, dt_bias_f, dt_bias_b, a_log_f, a_log_b, d_skip, ssd_norm_w, w_ssd_out, w_out, norm_ffn_w, w_ffn_up, w_ffn_conv, b_ffn_conv, w_ffn_down, final_norm_w, loss_target, m_meta_tokens, m_norm_mix_w, m_w_in, m_ret_gn_w, m_w_ret_out, m_w_ssd_conv, m_b_ssd_conv, m_dt_bias_f, m_dt_bias_b, m_a_log_f, m_a_log_b, m_d_skip, m_ssd_norm_w, m_w_ssd_out, m_w_out, m_norm_ffn_w, m_w_ffn_up, m_w_ffn_conv, m_b_ffn_conv, m_w_ffn_down, m_final_norm_w, v_meta_tokens, v_norm_mix_w, v_w_in, v_ret_gn_w, v_w_ret_out, v_w_ssd_conv, v_b_ssd_conv, v_dt_bias_f, v_dt_bias_b, v_a_log_f, v_a_log_b, v_d_skip, v_ssd_norm_w, v_w_ssd_out, v_w_out, v_norm_ffn_w, v_w_ffn_up, v_w_ffn_conv, v_b_ffn_conv, v_w_ffn_down, v_final_norm_w):
    given = dict(locals())
    wt = {n: given[n] for n in WEIGHTS}
    mt = {n: given["m_" + n] for n in WEIGHTS}
    vt = {n: given["v_" + n] for n in WEIGHTS}
    me = 4 * lax.axis_index("x") + 2 * lax.axis_index("y") + lax.axis_index("c")

    small_names = [n for n, _, _ in SMALL_SHARDED]
    small_local = lambda tree: [tree[n].reshape(r, c // N_DEV) for n, r, c in SMALL_SHARDED]
    slab_view = lambda tree, name, transposed: tree[name][0].T if transposed else tree[name][0]
    all_in = _gather_two_level("gather_w_in", slab_view(wt, "w_in", True).astype(BF16))
    all_s = _push_blocks("gather_small", _pack_flat(small_local(wt), 8), False)
    rest_srcs = [slab_view(wt, name, t).astype(BF16) for name, _, t in BIG_REST]
    rest_srcs, all_in, all_s = lax.optimization_barrier((rest_srcs, all_in, all_s))
    rest_flight = _push_start("gather_rest_start", rest_srcs, False)
    all_s = all_s.reshape(N_DEV, -1)
    full = {"w_in_t": all_in.reshape(-1, D_MODEL)}

    def lands_with_own(flight, after, per_peer, name):
        srcs, lands = _push_wait(name, *flight[:4], after, per_peer)
        own = lambda s: lax.dynamic_slice_in_dim(s, me, 1, axis=0) if per_peer else s[None]
        return [lax.dynamic_update_slice_in_dim(land, own(s), me, axis=0) for s, land in zip(srcs, lands)]

    def late_weights(after):
        lands = lands_with_own(rest_flight, after, False, "gather_rest_wait")
        return {name + ("_t" if t else ""): land.reshape(N_DEV * r, D_MODEL) for (name, r, t), land in zip(BIG_REST, lands)}

    flights = {}

    def start_exchange(key, group, gd):
        srcs = [gd[name + ("_t" if t else "")].astype(BF16).reshape(N_DEV, r, D_MODEL) for name, r, t in group]
        flights[key] = _push_start("exchange_" + key + "_start", srcs, True)
        return flights[key][4][0, 0]

    o = 0
    for name, r, c in SMALL_SHARDED:
        n = r * c // N_DEV
        full[name] = all_s[:, o:o + n].reshape(N_DEV, r, c // N_DEV).transpose(1, 0, 2).reshape(r, c)
        o += n
    for name in REPLICATED:
        full[name] = wt[name]

    grads, delta, new_m, new_v = {}, {}, {}, {}

    def finish_exchange(key, group, after):
        lands = lands_with_own(flights[key], after, True, "exchange_" + key + "_wait")
        for (name, _, transposed), land in zip(group, lands):
            back = (lambda a: a.T[None]) if transposed else (lambda a: a[None])
            g_sum = _sum_blocks("sum_" + name, land)
            d, mn, vn = _adamw("adamw_" + name, slab_view(wt, name, transposed), g_sum,
                               slab_view(mt, name, transposed), slab_view(vt, name, transposed))
            grads[name], delta[name], new_m[name], new_v[name] = back(g_sum), back(d), back(mn), back(vn)

    def in_grads(gi):
        tick = start_exchange("in", BIG_IN, {"w_in_t": gi})
        finish_exchange("rest", BIG_REST, flights["in"][4])
        tick, _ = lax.optimization_barrier((tick, [delta[name] for name, _, _ in BIG_REST]))
        return tick

    loss, grad_x, g = _local_step(x[0], loss_target[0], full, rest_flight[4][0, 0], late_weights,
                                  lambda gd: start_exchange("rest", BIG_REST, gd), in_grads)

    small_parts = [g[n] for n in REPLICATED] + [g[n] for n in small_names] + [loss.reshape(1)]
    small_flight = _push_start("gather_small_grads_start", [_pack_flat(small_parts, 64)], False)
    finish_exchange("in", BIG_IN, small_flight[4])
    g_small = _sum_blocks("sum_small", lands_with_own(small_flight, delta["w_in"], False, "gather_small_grads_wait")[0])
    small_red = _unpack_flat(g_small, [wt[n].shape for n in REPLICATED] + [(r, c) for _, r, c in SMALL_SHARDED] + [(1,)])
    grads.update(zip(REPLICATED, small_red[:len(REPLICATED)]))
    for (name, r, c), red in zip(SMALL_SHARDED, small_red[len(REPLICATED):-1]):
        grads[name] = lax.dynamic_slice(red, (0, me * (c // N_DEV)), (r, c // N_DEV)).reshape(wt[name].shape)
    loss_all = small_red[-1][0]

    rest = list(REPLICATED) + small_names
    shapes = [wt[n].shape for n in rest]
    pack_rest = lambda tree: _pack_flat([tree[n] for n in rest], 24)
    d_rest, m_rest, v_rest = _adamw("adamw_small", pack_rest(wt), pack_rest(grads), pack_rest(mt), pack_rest(vt))
    delta.update(zip(rest, _unpack_flat(d_rest, shapes)))
    new_m.update(zip(rest, _unpack_flat(m_rest, shapes)))
    new_v.update(zip(rest, _unpack_flat(v_rest, shapes)))

    return (loss_all, grad_x[None], *[grads[n] for n in WEIGHTS], *[delta[n] for n in WEIGHTS],
            *[new_m[n] for n in WEIGHTS], *[new_v[n] for n in WEIGHTS])
```

```python
import functools
import math

import jax
import jax.numpy as jnp
from jax import lax
from jax.experimental import pallas as pl
from jax.experimental.pallas import tpu as pltpu

F32 = jnp.float32
BF16 = jnp.bfloat16

D_MODEL = 1024
CHUNK = 128
N_META = 16
PAD_ROWS = CHUNK - N_META
RET_HEADS = 4
RET_QK_DIM = 128
RET_V_DIM = 256
SSD_HEADS = 32
SSD_HEAD_DIM = 64
SSD_GROUPS = 4
SSD_STATE = 128
HEADS_PER_GROUP = SSD_HEADS // SSD_GROUPS
PAIRS_PER_GROUP = HEADS_PER_GROUP // 2
D_FF = 2816
EPS = 1e-6
ROPE_BASE = 10000.0
N_DEV = 8

ADAM_LR = 0.001
ADAM_B1 = 0.9
ADAM_B2 = 0.999
ADAM_EPS = 1e-08
ADAM_WD = 0.01
ADAM_STEP = 10

VMEM_LIMIT = 56 * 1024 * 1024
HALO = 16
HIGHEST = lax.Precision.HIGHEST

SEGMENTS = (("qk", 0, 1024), ("v", 1024, 2048), ("g", 2048, 3072), ("z", 3072, 5120), ("xs", 5120, 7168),
            ("B", 7168, 7680), ("C", 7680, 8192), ("dt", 8192, 8256), ("gates", 8256, 10304))


def _pick(n, cands):
    for c in cands:
        if n % c == 0:
            return c
    raise ValueError(f"no tile for {n}")


def _params(sem):
    return pltpu.CompilerParams(dimension_semantics=sem, vmem_limit_bytes=VMEM_LIMIT)


def _dot(a, b, dims=(((1,), (0,)), ((), ())), precision=None):
    return lax.dot_general(a, b, dims, preferred_element_type=F32, precision=precision)


def _dot_nt(a, b):
    return _dot(a, b, (((1,), (1,)), ((), ())))


def _dot_tn(a, b):
    return _dot(a, b, (((0,), (0,)), ((), ())))


def _mm(name, a, b, mode, add=None, out_dtype=F32):
    if mode == "nn":
        (M, K), N = a.shape, b.shape[1]
    elif mode == "nt":
        (M, K), N = a.shape, b.shape[0]
    else:
        (K, M), N = a.shape, b.shape[1]
    tn = _pick(N, (1408, 1024, 512, 128, 64))
    if mode == "tn":
        tm = M if M <= 1024 else _pick(M, (1408, 1024))
        tk = _pick(K, (2112, 512, 256, 128))
    else:
        tm = _pick(M, (1056, 512, 256, 128))
        tk = K if K <= 2816 else _pick(K, (2048, 1408, 1024))
    nk = K // tk
    if mode == "nn":
        a_spec = pl.BlockSpec((tm, tk), lambda n, m, k: (m, k))
        b_spec = pl.BlockSpec((tk, tn), lambda n, m, k: (k, n))
        dims = (((1,), (0,)), ((), ()))
    elif mode == "nt":
        a_spec = pl.BlockSpec((tm, tk), lambda n, m, k: (m, k))
        b_spec = pl.BlockSpec((tn, tk), lambda n, m, k: (n, k))
        dims = (((1,), (1,)), ((), ()))
    else:
        a_spec = pl.BlockSpec((tk, tm), lambda n, m, k: (k, m))
        b_spec = pl.BlockSpec((tk, tn), lambda n, m, k: (k, n))
        dims = (((0,), (0,)), ((), ()))
    o_spec = pl.BlockSpec((tm, tn), lambda n, m, k: (m, n))
    in_specs = [a_spec, b_spec] + ([o_spec] if add is not None else [])
    args = [a, b] + ([add] if add is not None else [])

    def body(*refs):
        if add is not None:
            a_ref, b_ref, r_ref, o_ref, acc = refs
        else:
            a_ref, b_ref, o_ref, acc = refs
        k = pl.program_id(2)
        p = _dot(a_ref[...].astype(BF16), b_ref[...].astype(BF16), dims)

        def finish(r):
            if add is not None:
                r = r + r_ref[...]
            o_ref[...] = r.astype(out_dtype)

        if nk == 1:
            finish(p)
        else:
            @pl.when(k == 0)
            def _():
                acc[...] = p

            @pl.when(k > 0)
            def _():
                acc[...] += p

            @pl.when(k == nk - 1)
            def _():
                finish(acc[...])

    return pl.pallas_call(
        body, name=name, grid=(N // tn, M // tm, nk), in_specs=in_specs, out_specs=o_spec,
        out_shape=jax.ShapeDtypeStruct((M, N), out_dtype),
        scratch_shapes=[pltpu.VMEM((tm, tn) if nk > 1 else (8, 128), F32)],
        compiler_params=_params(("arbitrary", "arbitrary", "arbitrary")),
    )(*args)


ANY_SPACE = pl.BlockSpec(memory_space=pl.ANY)


def _const(c):
    return lambda j: c


def _rows(name, fn, T, ncol, ins, params, outs, accs=(), halo=False, tall=False):
    tm = _pick(T, (1056, 512, 256, 128)) if tall else _pick(T, (384, 256, 128))
    R = T // tm
    hb = tm // HALO
    in_specs, args = [], []
    for spec in ins:
        arr, w, cf = spec[:3]
        lead = spec[3] if len(spec) > 3 else None
        if len(spec) > 4:
            rows, rf = spec[4]
            in_specs.append(pl.BlockSpec((rows, w), lambda j, i, cf=cf, rf=rf: (rf(i), cf(j))))
            args.append(arr)
            continue
        if lead is None:
            mk = lambda blk, rf, cf=cf: pl.BlockSpec(blk, lambda j, i: (rf(i), cf(j)))
            shape = lambda r, w=w: (r, w)
        else:
            mk = lambda blk, rf, cf=cf, lead=lead: pl.BlockSpec(blk, lambda j, i: (lead, rf(i), cf(j)))
            shape = lambda r, w=w: (None, r, w)
        in_specs.append(mk(shape(tm), lambda i: i))
        args.append(arr)
        if halo:
            in_specs.append(mk(shape(HALO), lambda i: jnp.maximum(i * hb - 1, 0)))
            in_specs.append(mk(shape(HALO), lambda i: jnp.minimum((i + 1) * hb, T // HALO - 1)))
            args += [arr, arr]
    for arr, w, cf in params:
        in_specs.append(pl.BlockSpec((arr.shape[0], w), lambda j, i, cf=cf: (0, cf(j))))
        args.append(arr)
    out_shape, out_specs, aliases = [], [], {}
    for k, (tw, w, cf, dt) in enumerate(outs):
        if not isinstance(tw, int):
            aliases[len(args)] = k
            in_specs.append(ANY_SPACE)
            args.append(tw)
            tw = tw.shape[1]
        out_shape.append(jax.ShapeDtypeStruct((T, tw), dt))
        out_specs.append(pl.BlockSpec((tm, w), lambda j, i, cf=cf: (i, cf(j))))
    for r, tw, w, cf in accs:
        out_shape.append(jax.ShapeDtypeStruct((r, tw), F32))
        out_specs.append(pl.BlockSpec((r, w), lambda j, i, cf=cf: (0, cf(j))))
    n_in, n_par, n_out, n_acc, n_alias = len(ins), len(params), len(outs), len(accs), len(aliases)

    def body(*refs):
        i = pl.program_id(1)
        vals, p = [], 0
        for _ in range(n_in):
            if halo:
                before = jnp.where(i > 0, refs[p + 1][...], jnp.zeros_like(refs[p + 1]))
                after = jnp.where(i < R - 1, refs[p + 2][...], jnp.zeros_like(refs[p + 2]))
                vals.append(jnp.concatenate([before, refs[p][...], after], axis=0).astype(F32))
                p += 3
            else:
                vals.append(refs[p][...].astype(F32))
                p += 1
        pvals = [refs[p + k][...] for k in range(n_par)]
        p += n_par + n_alias
        res = fn(i, *vals, *pvals)
        for k in range(n_out):
            refs[p + k][...] = res[k].astype(refs[p + k].dtype)
        p += n_out
        for k in range(n_acc):
            ref, v = refs[p + k], res[n_out + k]

            @pl.when(i == 0)
            def _(ref=ref, v=v):
                ref[...] = v

            @pl.when(i > 0)
            def _(ref=ref, v=v):
                ref[...] += v

    res = pl.pallas_call(
        body, name=name, grid=(ncol, R), in_specs=in_specs, out_specs=out_specs, out_shape=out_shape,
        input_output_aliases=aliases, compiler_params=_params(("arbitrary", "arbitrary")),
    )(*args)
    return res


def _tile_rows(T):
    return _pick(T, (384, 256, 128))


def _row_ids(i, T, halo=False):
    tm = _tile_rows(T)
    if halo:
        return i * tm - HALO + lax.broadcasted_iota(jnp.int32, (tm + 2 * HALO, 1), 0)
    return i * tm + lax.broadcasted_iota(jnp.int32, (tm, 1), 0)


def _rms(x, w):
    return x * lax.rsqrt(jnp.mean(x * x, axis=-1, keepdims=True) + EPS) * w


def _silu(x):
    return x * jax.nn.sigmoid(x)


def _conv3(x, w):
    n = x.shape[0]
    return w[0:1] * pltpu.roll(x, 1, 0) + w[1:2] * x + w[2:3] * pltpu.roll(x, n - 1, 0)


def _conv3_t(d, w):
    n = d.shape[0]
    return w[0:1] * pltpu.roll(d, n - 1, 0) + w[1:2] * d + w[2:3] * pltpu.roll(d, 1, 0)


def _center(x):
    return x[HALO:x.shape[0] - HALO]


def _retention(name, a, b, v, T, da, dv, into=None):
    (a, a0), (b, b0), (v, v0) = [t if isinstance(t, tuple) else (t, 0) for t in (a, b, v)]
    nc = T // CHUNK
    log_gammas = [math.log(1.0 - 2.0 ** (-5.0 - h)) for h in range(RET_HEADS)]

    def body(*refs):
        a_ref, b_ref, v_ref = refs[:3]
        out_ref, o_ref, st, st_b = refs[-4:]
        h = pl.program_id(0)
        lg = jnp.float32(log_gammas[RET_HEADS - 1])
        for k in range(RET_HEADS - 2, -1, -1):
            lg = jnp.where(h == k, jnp.float32(log_gammas[k]), lg)
        li = lax.broadcasted_iota(jnp.int32, (CHUNK, CHUNK), 0)
        si = lax.broadcasted_iota(jnp.int32, (CHUNK, CHUNK), 1)
        dmat = jnp.exp(lg * jnp.abs(li - si).astype(F32))
        pos = lax.broadcasted_iota(jnp.int32, (CHUNK, 1), 0).astype(F32)
        kdec_f = jnp.exp((CHUNK - 1 - pos) * lg)
        qdec_f = jnp.exp((pos + 1) * lg)
        kdec_b = jnp.exp(pos * lg)
        qdec_b = jnp.exp((CHUNK - pos) * lg)
        cdec = jnp.exp(CHUNK * lg)

        def rows(n):
            return pl.ds(pl.multiple_of(n * CHUNK, CHUNK), CHUNK)

        st[...] = jnp.zeros_like(st)
        st_b[...] = jnp.zeros_like(st_b)
        o_ref[...] = jnp.zeros_like(o_ref)

        def step(m, carry):
            r = rows(m)
            av, bv, vv = a_ref[r, :], b_ref[r, :], v_ref[r, :].astype(BF16)
            s = _dot_nt(av.astype(BF16), bv.astype(BF16)) * dmat
            o_ref[r, :] += _dot(s.astype(BF16), vv) + _dot((av * qdec_f).astype(BF16), st[...].astype(BF16))
            st[...] = cdec * st[...] + _dot_tn((bv * kdec_f).astype(BF16), vv)
            r = rows(nc - 1 - m)
            av, bv, vv = a_ref[r, :], b_ref[r, :], v_ref[r, :].astype(BF16)
            o_ref[r, :] += _dot((av * qdec_b).astype(BF16), st_b[...].astype(BF16))
            st_b[...] = cdec * st_b[...] + _dot_tn((bv * kdec_b).astype(BF16), vv)
            return carry

        lax.fori_loop(0, nc, step, 0, unroll=True)
        out_ref[...] = o_ref[...].astype(out_ref.dtype)

    in_specs = [pl.BlockSpec((T, da), lambda h: (0, a0 // da + h)), pl.BlockSpec((T, da), lambda h: (0, b0 // da + h)),
                pl.BlockSpec((T, dv), lambda h: (0, v0 // dv + h))]
    if into is None:
        args, o0, aliases = (a, b, v), 0, {}
        out_shape = jax.ShapeDtypeStruct((T, RET_HEADS * dv), F32)
    else:
        args, o0, aliases = (a, b, v, into[0]), into[1], {3: 0}
        in_specs.append(ANY_SPACE)
        out_shape = jax.ShapeDtypeStruct(into[0].shape, into[0].dtype)
    return pl.pallas_call(
        body, name=name, grid=(RET_HEADS,), in_specs=in_specs,
        out_specs=pl.BlockSpec((T, dv), lambda h: (0, o0 // dv + h)), out_shape=out_shape,
        input_output_aliases=aliases,
        scratch_shapes=[pltpu.VMEM((T, dv), F32), pltpu.VMEM((da, dv), F32), pltpu.VMEM((da, dv), F32)],
        compiler_params=_params(("arbitrary",)),
    )(*args)


def _softplus(x):
    return jnp.maximum(x, 0.0) + jnp.log1p(jnp.exp(-jnp.abs(x)))


def _lane_lo():
    return lax.broadcasted_iota(jnp.int32, (1, CHUNK), 1) < SSD_HEAD_DIM


def _pair_cols(col, j):
    return jnp.where(_lane_lo(), col[:, 2 * j:2 * j + 1], col[:, 2 * j + 1:2 * j + 2])


def _pair_rows(colr, j):
    lo = lax.broadcasted_iota(jnp.int32, (CHUNK, 1), 0) < SSD_HEAD_DIM
    return jnp.where(lo, colr[2 * j:2 * j + 1, :], colr[2 * j + 1:2 * j + 2, :])


def _onehot8(h):
    return (lax.broadcasted_iota(jnp.int32, (1, HEADS_PER_GROUP), 1) == h).astype(F32)


def _ssd_pre(d, c, rawc, rawr, bc, br, alc, alr):
    li = lax.broadcasted_iota(jnp.int32, (CHUNK, CHUNK), 0)
    si = lax.broadcasted_iota(jnp.int32, (CHUNK, CHUNK), 1)
    dif = li - si if d == 0 else si - li
    mask = dif >= 0
    mask_t = dif <= 0
    rowc = c * CHUNK + lax.broadcasted_iota(jnp.int32, (CHUNK, 1), 0)
    rowr = c * CHUNK + lax.broadcasted_iota(jnp.int32, (1, CHUNK), 1)
    dtc = jnp.where(rowc >= PAD_ROWS, _softplus(rawc + bc), 0.0)
    dtr = jnp.where(rowr >= PAD_ROWS, _softplus(rawr + br), 0.0)
    ac = -jnp.exp(alc)
    ar = -jnp.exp(alr)
    dlc = dtc * ac
    dlr = dtr * ar
    alpc = _dot(mask.astype(F32), dlc, precision=HIGHEST)
    alpr = _dot(dlr, mask_t.astype(F32), precision=HIGHEST)
    endc = jnp.sum(dlc, axis=0, keepdims=True)
    endr = jnp.sum(dlr, axis=1, keepdims=True)
    return dict(mask=mask, mask_t=mask_t, dtc=dtc, ac=ac, alpc=alpc, alpr=alpr, endc=endc, endr=endr,
                valid=rowc >= PAD_ROWS)


def _chunk_of(d, n, nc):
    return n + d * (nc - 1 - 2 * n)


GROUP_WIDTH = HEADS_PER_GROUP * SSD_HEAD_DIM


def _chunks_per_step(nc, most=3):
    return next(c for c in (11, 3, 1) if c <= most and nc % c == 0)


def _ssd_in_specs(d, cfn, rows):
    return [
        pl.BlockSpec((rows, GROUP_WIDTH), lambda g, n: (cfn(d, n), g)),
        pl.BlockSpec((rows, SSD_STATE), lambda g, n: (cfn(d, n), g)),
        pl.BlockSpec((rows, SSD_STATE), lambda g, n: (cfn(d, n), g)),
        pl.BlockSpec((None, None, rows, HEADS_PER_GROUP), lambda g, n: (d, g, cfn(d, n), 0)),
        pl.BlockSpec((None, None, HEADS_PER_GROUP, rows), lambda g, n: (d, g, 0, cfn(d, n))),
        pl.BlockSpec((None, None, 1, HEADS_PER_GROUP), lambda g, n: (d, g, 0, 0)),
        pl.BlockSpec((None, None, HEADS_PER_GROUP, 1), lambda g, n: (d, g, 0, 0)),
        pl.BlockSpec((None, None, 1, HEADS_PER_GROUP), lambda g, n: (d, g, 0, 0)),
        pl.BlockSpec((None, None, HEADS_PER_GROUP, 1), lambda g, n: (d, g, 0, 0)),
    ]


N_SSD_IN = 9


def _ssd_fwd(xs, bm, cm, small, T):
    nc = T // CHUNK
    cps = _chunks_per_step(nc, 11)
    rows = cps * CHUNK
    cfn = lambda d, n: _chunk_of(d, n, nc // cps)

    def one_direction(d, n, ins, y_ref, hs_ref, h_scr):
        x_ref, b_ref, c_ref, rawc_ref, rawr_ref, *per_group = ins
        for kk in range(cps):
            k = kk if d == 0 else cps - 1 - kk
            r = pl.ds(k * CHUNK, CHUNK)
            one_chunk(d, cfn(d, n) * cps + k,
                      (x_ref.at[r], b_ref.at[r], c_ref.at[r], rawc_ref.at[r], rawr_ref.at[:, r], *per_group),
                      y_ref.at[r], hs_ref.at[k], h_scr)

    def one_chunk(d, c, ins, y_ref, hs_ref, h_scr):
        x_ref, b_ref, c_ref, rawc_ref, rawr_ref, bc_ref, br_ref, alc_ref, alr_ref = ins
        q = _ssd_pre(d, c, rawc_ref[...], rawr_ref[...], bc_ref[...], br_ref[...], alc_ref[...], alr_ref[...])
        bv = b_ref[...].astype(BF16)
        cv = c_ref[...].astype(BF16)
        cb = _dot_nt(cv, bv)
        lo = _lane_lo()
        for j in range(PAIRS_PER_GROUP):
            xp = x_ref[:, j * CHUNK:(j + 1) * CHUNK]
            xd = xp * _pair_cols(q["dtc"], j)
            xdb = xd.astype(BF16)
            yi = []
            for e in range(2):
                h = 2 * j + e
                lm = jnp.exp(jnp.where(q["mask"], q["alpc"][:, h:h + 1] - q["alpr"][h:h + 1, :], -jnp.inf))
                yi.append(_dot((cb * lm).astype(BF16), xdb))
            alp = _pair_cols(q["alpc"], j)
            hp = h_scr[j]
            hs_ref[j] = hp
            yo = jnp.exp(alp) * _dot_nt(cv, hp.astype(BF16))
            y_ref[:, j * CHUNK:(j + 1) * CHUNK] = (jnp.where(lo, yi[0], yi[1]) + yo).astype(y_ref.dtype)
            de = jnp.exp(_pair_cols(q["endc"], j) - alp)
            h_scr[j] = jnp.exp(_pair_rows(q["endr"], j)) * hp + _dot_tn((xd * de).astype(BF16), bv)

    def body(*refs):
        n = pl.program_id(1)
        ins, (y_f, y_b, hs_f, hs_b, h_scr) = refs[:2 * N_SSD_IN], refs[2 * N_SSD_IN:]

        @pl.when(n == 0)
        def _():
            h_scr[...] = jnp.zeros_like(h_scr)

        one_direction(0, n, ins[:N_SSD_IN], y_f, hs_f, h_scr.at[0])
        one_direction(1, n, ins[N_SSD_IN:], y_b, hs_b, h_scr.at[1])

    y_spec = lambda d: pl.BlockSpec((rows, GROUP_WIDTH), lambda g, n: (cfn(d, n), g))
    hs_spec = lambda d: pl.BlockSpec((None, cps, PAIRS_PER_GROUP, CHUNK, SSD_STATE),
                                     lambda g, n: (g, cfn(d, n), 0, 0, 0))
    y_shape = jax.ShapeDtypeStruct((T, SSD_HEADS * SSD_HEAD_DIM), BF16)
    hs_shape = jax.ShapeDtypeStruct((SSD_GROUPS, nc, PAIRS_PER_GROUP, CHUNK, SSD_STATE), F32)
    y_f, y_b, hs_f, hs_b = pl.pallas_call(
        body, name="ssd_fwd", grid=(SSD_GROUPS, nc // cps),
        in_specs=_ssd_in_specs(0, cfn, rows) + _ssd_in_specs(1, cfn, rows),
        out_specs=[y_spec(0), y_spec(1), hs_spec(0), hs_spec(1)],
        out_shape=[y_shape, y_shape, hs_shape, hs_shape],
        scratch_shapes=[pltpu.VMEM((2, PAIRS_PER_GROUP, CHUNK, SSD_STATE), F32)],
        compiler_params=_params(("arbitrary", "arbitrary")),
    )(xs, bm, cm, *small, xs, bm, cm, *small)
    return (y_f, y_b), (hs_f, hs_b)


def _ssd_bwd(xs, bm, cm, small, hs, dy, T):
    nc = T // CHUNK
    cps = _chunks_per_step(nc, 11)
    rows = cps * CHUNK
    cfn = lambda d, n: _chunk_of(1 - d, n, nc // cps)

    def one_direction(d, n, ins, outs, dh_scr):
        x_ref, b_ref, c_ref, rawc_ref, rawr_ref, bc_ref, br_ref, alc_ref, alr_ref, hs_ref, dy_ref = ins
        dx_ref, db_ref, dc_ref, draw_ref, dbias_ref, dalog_ref = outs
        for kk in range(cps):
            k = cps - 1 - kk if d == 0 else kk
            r = pl.ds(k * CHUNK, CHUNK)
            one_chunk(d, cfn(d, n) * cps + k, n if kk == 0 else None,
                      (x_ref.at[r], b_ref.at[r], c_ref.at[r], rawc_ref.at[r], rawr_ref.at[:, r], bc_ref, br_ref,
                       alc_ref, alr_ref, hs_ref.at[k], dy_ref.at[r]),
                      (dx_ref.at[r], db_ref.at[r], dc_ref.at[r], draw_ref.at[r], dbias_ref, dalog_ref), dh_scr)

    def one_chunk(d, c, first_of_step, ins, outs, dh_scr):
        x_ref, b_ref, c_ref, rawc_ref, rawr_ref, bc_ref, br_ref, alc_ref, alr_ref, hs_ref, dy_ref = ins
        dx_ref, db_ref, dc_ref, draw_ref, dbias_ref, dalog_ref = outs
        rawc, bc = rawc_ref[...], bc_ref[...]
        q = _ssd_pre(d, c, rawc, rawr_ref[...], bc, br_ref[...], alc_ref[...], alr_ref[...])
        b32, c32 = b_ref[...], c_ref[...]
        bv, cv = b32.astype(BF16), c32.astype(BF16)
        cb = _dot_nt(cv, bv)
        cbt = _dot_nt(bv, cv)
        lo = _lane_lo()
        row_lo = lax.broadcasted_iota(jnp.int32, (CHUNK, 1), 0) < SSD_HEAD_DIM
        dcb = jnp.zeros((CHUNK, CHUNK), F32)
        dcp = jnp.zeros((CHUNK, SSD_STATE), F32)
        dbp = jnp.zeros((CHUNK, SSD_STATE), F32)
        dalp = jnp.zeros((CHUNK, HEADS_PER_GROUP), F32)
        dend = jnp.zeros((1, HEADS_PER_GROUP), F32)
        ddtx = jnp.zeros((CHUNK, HEADS_PER_GROUP), F32)

        def half_sums(t):
            return (jnp.sum(jnp.where(lo, t, 0.0), axis=1, keepdims=True),
                    jnp.sum(jnp.where(lo, 0.0, t), axis=1, keepdims=True))

        for j in range(PAIRS_PER_GROUP):
            xp = x_ref[:, j * CHUNK:(j + 1) * CHUNK]
            dtp = _pair_cols(q["dtc"], j)
            xd = xp * dtp
            xdb = xd.astype(BF16)
            dyp = dy_ref[:, j * CHUNK:(j + 1) * CHUNK]
            dyb = dyp.astype(BF16)
            hn = hs_ref[j]
            hnb = hn.astype(BF16)
            dh1 = dh_scr[j]
            dh1b = dh1.astype(BF16)
            alp = _pair_cols(q["alpc"], j)
            ea = jnp.exp(alp)
            de = jnp.exp(_pair_cols(q["endc"], j) - alp)
            dxi = []
            for e in range(2):
                h = 2 * j + e
                diff = q["alpc"][:, h:h + 1] - q["alpr"][h:h + 1, :]
                lm = jnp.exp(jnp.where(q["mask"], diff, -jnp.inf))
                mt = cbt * jnp.exp(jnp.where(q["mask_t"], -diff, -jnp.inf))
                dxi.append(_dot(mt.astype(BF16), dyb))
                dyeb_h = (jnp.where(lo, dyp, 0.0) if e == 0 else jnp.where(lo, 0.0, dyp)).astype(BF16)
                gl = _dot_nt(dyeb_h, xdb) * lm
                dcb = dcb + gl
                ra = jnp.sum(gl * cb - _dot_nt(xdb, dyeb_h) * mt, axis=1, keepdims=True)
                dalp = dalp + ra * _onehot8(h)
            y_off = ea * _dot_nt(cv, hnb)
            dxs_state = de * _dot_nt(bv, dh1b)
            dxd = jnp.where(lo, dxi[0], dxi[1]) + dxs_state
            dyeb = (dyp * ea).astype(BF16)
            dcp = dcp + _dot(dyeb, hnb)
            dbp = dbp + _dot((xd * de).astype(BF16), dh1b)
            dh_scr[j] = jnp.exp(_pair_rows(q["endr"], j)) * dh1 + _dot_tn(dyeb, cv)
            r0, r1 = half_sums(dyp * y_off - xd * dxs_state)
            dalp = dalp + r0 * _onehot8(2 * j) + r1 * _onehot8(2 * j + 1)
            t0, t1 = half_sums(jnp.sum(xd * dxs_state, axis=0, keepdims=True))
            u = dh1 * hn
            u0 = jnp.sum(jnp.sum(jnp.where(row_lo, u, 0.0), axis=0, keepdims=True), axis=1, keepdims=True)
            u1 = jnp.sum(jnp.sum(jnp.where(row_lo, 0.0, u), axis=0, keepdims=True), axis=1, keepdims=True)
            eend = jnp.exp(q["endc"])
            dend = dend + (t0 + eend * u0) * _onehot8(2 * j) + (t1 + eend * u1) * _onehot8(2 * j + 1)
            dx_ref[:, j * CHUNK:(j + 1) * CHUNK] = (dxd * dtp).astype(dx_ref.dtype)
            w0, w1 = half_sums(dxd * xp)
            ddtx = ddtx + w0 * _onehot8(2 * j) + w1 * _onehot8(2 * j + 1)

        dcbb = dcb.astype(BF16)
        dc_ref[...] = (dcp + _dot(dcbb, bv)).astype(dc_ref.dtype)
        db_ref[...] = (dbp + _dot_tn(dcbb, cv)).astype(db_ref.dtype)
        ddl = _dot(q["mask_t"].astype(F32), dalp, precision=HIGHEST) + dend
        ddt = ddl * q["ac"] + ddtx
        draw = jnp.where(q["valid"], ddt * jax.nn.sigmoid(rawc + bc), 0.0)
        draw_ref[...] = draw
        dbias = jnp.sum(draw, axis=0, keepdims=True)
        dalog = jnp.sum(ddl * q["dtc"], axis=0, keepdims=True) * q["ac"]

        def add():
            dbias_ref[...] += dbias
            dalog_ref[...] += dalog

        if first_of_step is None:
            add()
        else:
            @pl.when(first_of_step == 0)
            def _():
                dbias_ref[...] = dbias
                dalog_ref[...] = dalog

            pl.when(first_of_step > 0)(add)

    n_in, n_out = N_SSD_IN + 2, 6

    def body(*refs):
        n = pl.program_id(1)
        ins, outs, dh_scr = refs[:2 * n_in], refs[2 * n_in:2 * (n_in + n_out)], refs[-1]

        @pl.when(n == 0)
        def _():
            dh_scr[...] = jnp.zeros_like(dh_scr)

        one_direction(0, n, ins[:n_in], outs[:n_out], dh_scr.at[0])
        one_direction(1, n, ins[n_in:], outs[n_out:], dh_scr.at[1])

    def in_specs(d):
        return _ssd_in_specs(d, cfn, rows) + [
            pl.BlockSpec((None, cps, PAIRS_PER_GROUP, CHUNK, SSD_STATE), lambda g, n: (g, cfn(d, n), 0, 0, 0)),
            pl.BlockSpec((rows, GROUP_WIDTH), lambda g, n: (cfn(d, n), g))]

    def out_specs(d):
        acc = pl.BlockSpec((None, 1, HEADS_PER_GROUP), lambda g, n: (g, 0, 0))
        return [pl.BlockSpec((rows, GROUP_WIDTH), lambda g, n: (cfn(d, n), g)),
                pl.BlockSpec((rows, SSD_STATE), lambda g, n: (cfn(d, n), g)),
                pl.BlockSpec((rows, SSD_STATE), lambda g, n: (cfn(d, n), g)),
                pl.BlockSpec((None, rows, HEADS_PER_GROUP), lambda g, n: (g, cfn(d, n), 0)), acc, acc]

    out_shape = [jax.ShapeDtypeStruct((T, SSD_HEADS * SSD_HEAD_DIM), BF16),
                 jax.ShapeDtypeStruct((T, SSD_GROUPS * SSD_STATE), BF16),
                 jax.ShapeDtypeStruct((T, SSD_GROUPS * SSD_STATE), BF16),
                 jax.ShapeDtypeStruct((SSD_GROUPS, T, HEADS_PER_GROUP), F32),
                 jax.ShapeDtypeStruct((SSD_GROUPS, 1, HEADS_PER_GROUP), F32),
                 jax.ShapeDtypeStruct((SSD_GROUPS, 1, HEADS_PER_GROUP), F32)]
    res = pl.pallas_call(
        body, name="ssd_bwd", grid=(SSD_GROUPS, nc // cps),
        in_specs=in_specs(0) + in_specs(1), out_specs=out_specs(0) + out_specs(1), out_shape=out_shape * 2,
        scratch_shapes=[pltpu.VMEM((2, PAIRS_PER_GROUP, CHUNK, SSD_STATE), F32)],
        compiler_params=_params(("arbitrary", "arbitrary")),
    )(xs, bm, cm, *small, hs[0], dy, xs, bm, cm, *small, hs[1], dy)
    return [(res[k], res[n_out + k]) for k in range(n_out)]


def _rot(x, cs, sn):
    return x * cs + pltpu.roll(x, RET_QK_DIM // 2, 1) * sn


def _rot_t(d, cs, sn):
    return d * cs + pltpu.roll(d * sn, RET_QK_DIM // 2, 1)


def _ret_post(y, g, w):
    parts = []
    for h in range(RET_HEADS):
        yh = y[:, h * RET_V_DIM:(h + 1) * RET_V_DIM]
        mu = jnp.mean(yh, axis=-1, keepdims=True)
        var = jnp.mean(jnp.square(yh - mu), axis=-1, keepdims=True)
        parts.append((yh - mu) * lax.rsqrt(var + EPS))
    return _silu(g) * (jnp.concatenate(parts, axis=1) * w)


def _ssd_post(yf, yb, xs, z, dskip, w):
    y = (yf + yb + xs * dskip) * _silu(z)
    return y * lax.rsqrt(jnp.mean(y * y, axis=-1, keepdims=True) + EPS) * w


def _merge(gates, yr, ys, valid):
    m = jax.nn.sigmoid(gates[:, :D_MODEL]) * yr + jax.nn.sigmoid(gates[:, D_MODEL:]) * ys
    return jnp.where(valid, m, 0.0)


def _rope_tables(T):
    half = RET_QK_DIM // 2
    inv = ROPE_BASE ** (-jnp.arange(half, dtype=F32) / half)
    pos = (jnp.arange(T) - PAD_ROWS).astype(F32)
    ang = pos[:, None] * inv[None, :]
    cos, sin = jnp.cos(ang), jnp.sin(ang)
    return jnp.concatenate([cos, cos], axis=1), jnp.concatenate([-sin, sin], axis=1)


def _per_group(v):
    c = v.reshape(SSD_GROUPS, 1, HEADS_PER_GROUP)
    return c, c.reshape(SSD_GROUPS, HEADS_PER_GROUP, 1)


def _local_step(x, target, w, tick, late_weights, early_grads, in_grads):
    S = x.shape[0]
    T = S + CHUNK
    tm = _tile_rows(T)
    c0 = _const(0)

    h0 = jnp.concatenate([jnp.zeros((PAD_ROWS, D_MODEL), F32), w["meta_tokens"], x], axis=0)
    seg_at = {name: a for name, a, _ in SEGMENTS}
    w_main = w["w_in_t"][:seg_at["dt"]]
    w_dt = jnp.pad(w["w_in_t"][seg_at["dt"]:seg_at["gates"]], ((0, CHUNK - 2 * SSD_HEADS), (0, 0)))
    w_gates = w["w_in_t"][seg_at["gates"]:]

    def norm_cast(name, h, nw):
        return _rows(name, lambda i, hv, wv: (_rms(hv, wv),), T, 1, [(h, D_MODEL, c0)], [(nw, D_MODEL, c0)],
                     [(D_MODEL, D_MODEL, c0, BF16)], tall=True)[0]

    u = norm_cast("norm_mix", h0, w["norm_mix_w"] + tick)
    p_main = _mm("proj_main", u, w_main, "nt", out_dtype=BF16)
    p_dt = _mm("proj_dt", u, w_dt, "nt")
    p_gates = _mm("proj_gates", u, w_gates, "nt", out_dtype=BF16)

    def seg(name, width, cf=c0):
        base = seg_at[name] // width
        return (p_main, width, lambda j: base + cf(j))

    cs, sn = _rope_tables(T)
    scale = RET_QK_DIM ** -0.5

    def rot_fn(i, qk, csv, snv):
        q = [_rot(qk[:, h * 128:(h + 1) * 128], csv, snv) for h in range(RET_HEADS)]
        k = [_rot(qk[:, (RET_HEADS + h) * 128:(RET_HEADS + h + 1) * 128], csv, snv) * scale for h in range(RET_HEADS)]
        return jnp.concatenate(q, axis=1), jnp.concatenate(k, axis=1)

    qr, kr = _rows("rotary", rot_fn, T, 1, [seg("qk", 1024), (cs, 128, c0), (sn, 128, c0)], [],
                   [(512, 512, c0, F32), (512, 512, c0, F32)], tall=True)
    v_at = (p_main, seg_at["v"])
    y_ret = _retention("retention", qr, kr, v_at, T, RET_QK_DIM, RET_V_DIM)
    a_ret = _rows("ret_post", lambda i, y, g, gw: (_ret_post(y, g, gw),), T, 1,
                  [(y_ret, 1024, c0), seg("g", 1024)], [(w["ret_gn_w"], 1024, c0)],
                  [(1024, 1024, c0, BF16)], tall=True)[0]

    conv_w = {"xs": w["w_ssd_conv"][:, :2048], "B": w["w_ssd_conv"][:, 2048:2560], "C": w["w_ssd_conv"][:, 2560:]}
    conv_b = {"xs": w["b_ssd_conv"][:, :2048], "B": w["b_ssd_conv"][:, 2048:2560], "C": w["b_ssd_conv"][:, 2560:]}

    def ssd_conv_fn(i, xe, cw, cb):
        r = _row_ids(i, T, True)
        return (_center(jnp.where(r >= PAD_ROWS, _silu(_conv3(xe, cw) + cb), 0.0)),)

    act = {}
    for name in ("xs", "B", "C"):
        wd = conv_w[name].shape[1]
        cw = 512
        act[name] = _rows("ssd_conv_" + name, ssd_conv_fn, T, wd // cw, [seg(name, cw, lambda j: j)],
                          [(conv_w[name], cw, lambda j: j), (conv_b[name], cw, lambda j: j)],
                          [(wd, cw, lambda j: j, BF16)], halo=True)[0]

    raw = p_dt[:, :2 * SSD_HEADS].reshape(T, 2, SSD_GROUPS, HEADS_PER_GROUP)
    rawc = raw.transpose(1, 2, 0, 3)
    rawr = raw.transpose(1, 2, 3, 0)
    bias = [_per_group(w["dt_bias_f"]), _per_group(w["dt_bias_b"])]
    alog = [_per_group(w["a_log_f"]), _per_group(w["a_log_b"])]
    small = (rawc, rawr, jnp.stack([bias[0][0], bias[1][0]]), jnp.stack([bias[0][1], bias[1][1]]),
             jnp.stack([alog[0][0], alog[1][0]]), jnp.stack([alog[0][1], alog[1][1]]))
    y_dir, states = _ssd_fwd(act["xs"], act["B"], act["C"], small, T)

    dskip_e = jnp.repeat(w["d_skip"], SSD_HEAD_DIM, axis=1)
    gcol = lambda j: j
    gw_ = 512
    a_ssd = _rows("ssd_post", lambda i, yf, yb, xv, zv, dk, nw: (_ssd_post(yf, yb, xv, zv, dk, nw),), T, SSD_GROUPS,
                  [(y_dir[0], gw_, gcol), (y_dir[1], gw_, gcol), (act["xs"], gw_, gcol), seg("z", gw_, gcol)],
                  [(dskip_e, gw_, gcol), (w["ssd_norm_w"], gw_, gcol)], [(2048, gw_, gcol, BF16)], tall=True)[0]

    w = dict(w, **late_weights(a_ssd))
    w_up_g, w_up_u = w["w_ffn_up_t"][:D_FF], w["w_ffn_up_t"][D_FF:]
    y_ret_o = _mm("ret_out", a_ret, w["w_ret_out"], "nn", out_dtype=BF16)
    y_ssd_o = _mm("ssd_out", a_ssd, w["w_ssd_out"], "nn", out_dtype=BF16)

    def merge_fn(i, gates, yr, ys):
        return (_merge(gates, yr, ys, _row_ids(i, T) >= PAD_ROWS),)

    merged = _rows("merge", merge_fn, T, 1, [(p_gates, 2048, c0), (y_ret_o, 1024, c0), (y_ssd_o, 1024, c0)], [],
                   [(1024, 1024, c0, BF16)])[0]
    h1 = _mm("mix_out", merged, w["w_out"], "nn", add=h0)

    n2 = norm_cast("norm_ffn", h1, w["norm_ffn_w"])
    f_pre = _mm("ffn_up", n2, w["w_ffn_up_t"], "nt", out_dtype=BF16)
    cwg, cwu = w["w_ffn_conv"][:, :D_FF], w["w_ffn_conv"][:, D_FF:]
    cbg, cbu = w["b_ffn_conv"][:, :D_FF], w["b_ffn_conv"][:, D_FF:]
    fcol = lambda j: j
    fw = 1408

    def ffn_act_fn(i, ge, ue, wg, wu, bg, bu):
        return (_center(_silu(_conv3(ge, wg) + bg) * (_conv3(ue, wu) + bu)),)

    ucol = lambda j: D_FF // fw + j
    a2 = _rows("ffn_act", ffn_act_fn, T, D_FF // fw, [(f_pre, fw, fcol), (f_pre, fw, ucol)],
               [(cwg, fw, fcol), (cwu, fw, fcol), (cbg, fw, fcol), (cbu, fw, fcol)], [(D_FF, fw, fcol, BF16)],
               halo=True)[0]
    h2 = _mm("ffn_down", a2, w["w_ffn_down"], "nn", add=h1)

    fnw = w["final_norm_w"].reshape(1, D_MODEL)

    per_tile = tm // CHUNK
    tgt_specs = [(target, D_MODEL, c0, None, (CHUNK, lambda i, k=k: jnp.maximum(per_tile * i - 1 + k, 0)))
                 for k in range(per_tile)]

    def loss_fn(i, hv, *rest):
        tv, nw = jnp.concatenate(rest[:per_tile], axis=0), rest[per_tile]
        valid = _row_ids(i, T) >= CHUNK
        y, vjp = jax.vjp(_rms, hv, nw)
        diff = jnp.where(valid, y - tv, 0.0)
        dh, dw = vjp(diff * (1.0 / D_MODEL))
        part = 0.5 / D_MODEL * jnp.sum(jnp.sum(diff * diff, axis=1, keepdims=True), axis=0, keepdims=True)
        return dh, jnp.broadcast_to(part, (1, 128)), dw

    dh2, loss_acc, d_fnw = _rows("loss", loss_fn, T, 1, [(h2, D_MODEL, c0)] + tgt_specs, [(fnw, D_MODEL, c0)],
                                 [(D_MODEL, D_MODEL, c0, F32)], [(1, 128, 128, c0), (1, D_MODEL, D_MODEL, c0)])
    loss = loss_acc[0, 0]
    grads = {"final_norm_w": d_fnw.reshape(D_MODEL)}

    da2 = _mm("d_ffn_act", dh2, w["w_ffn_down"], "nt", out_dtype=BF16)
    grads["w_ffn_down"] = _mm("g_ffn_down", a2, dh2, "tn", out_dtype=BF16)

    def ffn_bwd_fn(i, ge, ue, de, wg, wu, bg, bu):
        fg = _conv3(ge, wg) + bg
        fu = _conv3(ue, wu) + bu
        sg = jax.nn.sigmoid(fg)
        dfg = de * fu * (sg * (1.0 + fg * (1.0 - sg)))
        dfu = de * (fg * sg)
        n = ge.shape[0]

        def wgrad(df, xe):
            df_c = _center(df)
            return jnp.concatenate([jnp.sum(df_c * _center(pltpu.roll(xe, 1, 0)), axis=0, keepdims=True),
                                    jnp.sum(df_c * _center(xe), axis=0, keepdims=True),
                                    jnp.sum(df_c * _center(pltpu.roll(xe, n - 1, 0)), axis=0, keepdims=True)], axis=0)

        return (_center(_conv3_t(dfg, wg)), _center(_conv3_t(dfu, wu)), wgrad(dfg, ge), wgrad(dfu, ue),
                jnp.sum(_center(dfg), axis=0, keepdims=True), jnp.sum(_center(dfu), axis=0, keepdims=True))

    dfg_pre, dfu_pre, g_cwg, g_cwu, g_cbg, g_cbu = _rows(
        "ffn_act_bwd", ffn_bwd_fn, T, D_FF // fw, [(f_pre, fw, fcol), (f_pre, fw, ucol), (da2, fw, fcol)],
        [(cwg, fw, fcol), (cwu, fw, fcol), (cbg, fw, fcol), (cbu, fw, fcol)],
        [(D_FF, fw, fcol, BF16), (D_FF, fw, fcol, BF16)],
        [(3, D_FF, fw, fcol), (3, D_FF, fw, fcol), (1, D_FF, fw, fcol), (1, D_FF, fw, fcol)], halo=True)
    grads["w_ffn_conv"] = jnp.concatenate([g_cwg, g_cwu], axis=1)
    grads["b_ffn_conv"] = jnp.concatenate([g_cbg, g_cbu], axis=1)
    dn2 = _mm("d_norm_ffn_g", dfg_pre, w_up_g, "nn")
    dn2 = _mm("d_norm_ffn_u", dfu_pre, w_up_u, "nn", add=dn2)
    grads["w_ffn_up_t"] = jnp.concatenate([_mm("g_ffn_up_g", dfg_pre, n2, "tn", out_dtype=BF16), _mm("g_ffn_up_u", dfu_pre, n2, "tn", out_dtype=BF16)],
                                          axis=0)

    def norm_bwd(name, h, nw, dn, dres):
        def fn(i, hv, dnv, drv, wv):
            _, vjp = jax.vjp(_rms, hv, wv)
            dh, dw = vjp(dnv)
            return dh + drv, dw
        return _rows(name, fn, T, 1, [(h, D_MODEL, c0), (dn, D_MODEL, c0), (dres, D_MODEL, c0)], [(nw, D_MODEL, c0)],
                     [(D_MODEL, D_MODEL, c0, F32)], [(1, D_MODEL, D_MODEL, c0)])

    dh1, grads["norm_ffn_w"] = norm_bwd("norm_ffn_bwd", h1, w["norm_ffn_w"], dn2, dh2)

    dmerged = _mm("d_merged", dh1, w["w_out"], "nt", out_dtype=BF16)
    grads["w_out"] = _mm("g_out", merged, dh1, "tn", out_dtype=BF16)

    def merge_bwd_fn(i, gates, yr, ys, dm):
        valid = _row_ids(i, T) >= PAD_ROWS
        _, vjp = jax.vjp(lambda a, b, c: _merge(a, b, c, valid), gates, yr, ys)
        return vjp(dm)

    dgates, dyr, dys = _rows("merge_bwd", merge_bwd_fn, T, 1,
                             [(p_gates, 2048, c0), (y_ret_o, 1024, c0), (y_ssd_o, 1024, c0), (dmerged, 1024, c0)],
                             [], [(2048, 2048, c0, BF16), (1024, 1024, c0, BF16), (1024, 1024, c0, BF16)])
    dproj = {"gates": dgates}

    da_ssd = _mm("d_ssd_act", dys, w["w_ssd_out"], "nt", out_dtype=BF16)
    grads["w_ssd_out"] = _mm("g_ssd_out", a_ssd, dys, "tn", out_dtype=BF16)

    def ssd_post_bwd_fn(i, yf, yb, xv, zv, da, dk, nw):
        _, vjp = jax.vjp(_ssd_post, yf, yb, xv, zv, dk, nw)
        dyf, _, dxv, dzv, ddk, dnw = vjp(da)
        return dyf, dxv, dzv, ddk, dnw

    d_main = lax.empty(p_main.shape, BF16)

    def into_main(name, width, cf=c0):
        base = seg_at[name] // width
        return (d_main, width, lambda j: base + cf(j), BF16)

    dy_ssd, dxs_skip, d_main, g_dskip_e, grads["ssd_norm_w"] = _rows(
        "ssd_post_bwd", ssd_post_bwd_fn, T, SSD_GROUPS,
        [(y_dir[0], gw_, gcol), (y_dir[1], gw_, gcol), (act["xs"], gw_, gcol), seg("z", gw_, gcol),
         (da_ssd, gw_, gcol)],
        [(dskip_e, gw_, gcol), (w["ssd_norm_w"], gw_, gcol)],
        [(2048, gw_, gcol, BF16), (2048, gw_, gcol, BF16), into_main("z", gw_, gcol)],
        [(1, 2048, gw_, gcol), (1, 2048, gw_, gcol)], tall=True)
    grads["d_skip"] = g_dskip_e.reshape(SSD_HEADS, SSD_HEAD_DIM).sum(axis=1).reshape(1, SSD_HEADS)

    dxs_dir, db_dir, dc_dir, draw, g_bias, g_alog = _ssd_bwd(act["xs"], act["B"], act["C"], small, states, dy_ssd, T)
    grads["dt_bias_f"], grads["dt_bias_b"] = g_bias[0].reshape(1, SSD_HEADS), g_bias[1].reshape(1, SSD_HEADS)
    grads["a_log_f"], grads["a_log_b"] = g_alog[0].reshape(1, SSD_HEADS), g_alog[1].reshape(1, SSD_HEADS)
    d_dt = jnp.stack(draw).transpose(2, 0, 1, 3).reshape(T, 2 * SSD_HEADS)
    dproj["dt"] = jnp.pad(d_dt, ((0, 0), (0, CHUNK - 2 * SSD_HEADS))).astype(BF16)

    def make_conv_bwd(nsum):
        def fn(i, xe, *rest):
            ds, (cw, cb) = rest[:nsum], rest[nsum:]
            r = _row_ids(i, T, True)
            dact = ds[0]
            for t in ds[1:]:
                dact = dact + t
            dact = jnp.where(r >= PAD_ROWS, dact, 0.0)
            pre = _conv3(xe, cw) + cb
            sg = jax.nn.sigmoid(pre)
            dpre = dact * (sg * (1.0 + pre * (1.0 - sg)))
            n = xe.shape[0]
            dpc = _center(dpre)
            dw = jnp.concatenate([jnp.sum(dpc * _center(pltpu.roll(xe, 1, 0)), axis=0, keepdims=True),
                                  jnp.sum(dpc * _center(xe), axis=0, keepdims=True),
                                  jnp.sum(dpc * _center(pltpu.roll(xe, n - 1, 0)), axis=0, keepdims=True)], axis=0)
            return _center(_conv3_t(dpre, cw)), dw, jnp.sum(dpc, axis=0, keepdims=True)
        return fn

    g_cw, g_cb = {}, {}
    cots = {"xs": [(dxs_dir[0], 512, gcol), (dxs_dir[1], 512, gcol), (dxs_skip, 512, gcol)],
            "B": [(db_dir[0], 512, gcol), (db_dir[1], 512, gcol)],
            "C": [(dc_dir[0], 512, gcol), (dc_dir[1], 512, gcol)]}
    for name in ("xs", "B", "C"):
        wd = conv_w[name].shape[1]
        d_main, g_cw[name], g_cb[name] = _rows(
            "ssd_conv_bwd_" + name, make_conv_bwd(len(cots[name])), T, wd // 512,
            [seg(name, 512, gcol)] + cots[name], [(conv_w[name], 512, gcol), (conv_b[name], 512, gcol)],
            [into_main(name, 512, gcol)], [(3, wd, 512, gcol), (1, wd, 512, gcol)], halo=True)
    grads["w_ssd_conv"] = jnp.concatenate([g_cw["xs"], g_cw["B"], g_cw["C"]], axis=1)
    grads["b_ssd_conv"] = jnp.concatenate([g_cb["xs"], g_cb["B"], g_cb["C"]], axis=1)

    da_ret = _mm("d_ret_act", dyr, w["w_ret_out"], "nt", out_dtype=BF16)
    grads["w_ret_out"] = _mm("g_ret_out", a_ret, dyr, "tn", out_dtype=BF16)
    tick = early_grads({n: grads.pop(n) for n in ("w_ffn_up_t", "w_ret_out", "w_ssd_out", "w_out", "w_ffn_down")})

    def ret_post_bwd_fn(i, y, g, da, gw):
        _, vjp = jax.vjp(_ret_post, y, g, gw)
        return vjp(da)

    dy_ret, d_main, grads["ret_gn_w"] = _rows(
        "ret_post_bwd", ret_post_bwd_fn, T, 1, [(y_ret, 1024, c0), seg("g", 1024), (da_ret, 1024, c0)],
        [(w["ret_gn_w"] + tick, 1024, c0)], [(1024, 1024, c0, BF16), into_main("g", 1024)], [(1, 1024, 1024, c0)])
    d_main = _retention("retention_dv", kr, qr, dy_ret, T, RET_QK_DIM, RET_V_DIM, into=(d_main, seg_at["v"]))
    dqr = _retention("retention_dq", dy_ret, v_at, kr, T, RET_V_DIM, RET_QK_DIM)
    dkr = _retention("retention_dk", v_at, dy_ret, qr, T, RET_V_DIM, RET_QK_DIM)

    def rot_bwd_fn(i, dq, dk, csv, snv):
        parts = [_rot_t(dq[:, h * 128:(h + 1) * 128], csv, snv) for h in range(RET_HEADS)]
        parts += [_rot_t(dk[:, h * 128:(h + 1) * 128] * scale, csv, snv) for h in range(RET_HEADS)]
        return (jnp.concatenate(parts, axis=1),)

    d_main = _rows("rotary_bwd", rot_bwd_fn, T, 1, [(dqr, 512, c0), (dkr, 512, c0), (cs, 128, c0), (sn, 128, c0)],
                   [], [into_main("qk", 1024)], tall=True)[0]

    g_in = [_mm("g_in_main", d_main, u, "tn", out_dtype=BF16),
            _mm("g_in_dt", dproj["dt"], u, "tn", out_dtype=BF16)[:2 * SSD_HEADS],
            _mm("g_in_gates", dproj["gates"], u, "tn", out_dtype=BF16)]
    tick = in_grads(jnp.concatenate(g_in, axis=0))
    du = _mm("d_u_dt", dproj["dt"] + tick.astype(BF16), w_dt, "nn")
    du = _mm("d_u_main", d_main, w_main, "nn", add=du)
    du = _mm("d_u_gates", dproj["gates"], w_gates, "nn", add=du)
    dh0, grads["norm_mix_w"] = norm_bwd("norm_mix_bwd", h0, w["norm_mix_w"], du, dh1)
    grads["meta_tokens"] = dh0[PAD_ROWS:CHUNK]
    return loss, dh0[CHUNK:], grads


MESH_ID = pl.DeviceIdType.MESH
ANY = pl.BlockSpec(memory_space=pl.ANY)


def _me_and_peers():
    x, y, c = lax.axis_index("x"), lax.axis_index("y"), lax.axis_index("c")
    peers = []
    for k in range(1, N_DEV):
        px = 1 - x if k & 4 else x
        py = 1 - y if k & 2 else y
        pc = 1 - c if k & 1 else c
        peers.append(((px, py, pc), 4 * px + 2 * py + pc))
    return 4 * x + 2 * y + c, peers


def _push_blocks(name, src, per_peer):
    blk = src.shape[1:] if per_peer else src.shape

    def body(src_ref, out_ref, send_sems, recv_sems, local_sem):
        me, peers = _me_and_peers()
        mine = src_ref.at[me] if per_peer else src_ref
        local = pltpu.make_async_copy(mine, out_ref.at[me], local_sem)
        local.start()
        sends = []
        for k, (dev, idx) in enumerate(peers):
            cp = pltpu.make_async_remote_copy(
                src_ref=src_ref.at[idx] if per_peer else src_ref, dst_ref=out_ref.at[me],
                send_sem=send_sems.at[k], recv_sem=recv_sems.at[k], device_id=dev, device_id_type=MESH_ID)
            cp.start()
            sends.append(cp)
        for k, (dev, idx) in enumerate(peers):
            pltpu.make_async_remote_copy(
                src_ref=mine, dst_ref=out_ref.at[idx], send_sem=send_sems.at[k], recv_sem=recv_sems.at[k],
                device_id=dev, device_id_type=MESH_ID).wait_recv()
        for cp in sends:
            cp.wait_send()
        local.wait()

    return pl.pallas_call(
        body, name=name, in_specs=[ANY], out_specs=ANY,
        out_shape=jax.ShapeDtypeStruct((N_DEV,) + tuple(blk), src.dtype),
        scratch_shapes=[pltpu.SemaphoreType.DMA((N_DEV - 1,)), pltpu.SemaphoreType.DMA((N_DEV - 1,)),
                        pltpu.SemaphoreType.DMA],
    )(src)


def _gather_two_level(name, src):
    def body(x_ref, out_ref, send_sems, recv_sems, local_sem):
        x, y, c = lax.axis_index("x"), lax.axis_index("y"), lax.axis_index("c")
        me, sibling = (x, y, c), (x, y, 1 - c)
        chips = [(1 - x, y), (x, 1 - y), (1 - x, 1 - y)]

        def rows(px, py, pc):
            return out_ref.at[4 * px + 2 * py + pc]

        def copy(k, block, to, src_ref=None):
            return pltpu.make_async_remote_copy(
                src_ref=rows(*block) if src_ref is None else src_ref, dst_ref=rows(*block),
                send_sem=send_sems.at[k], recv_sem=recv_sems.at[k], device_id=to, device_id_type=MESH_ID)

        mine = pltpu.make_async_copy(x_ref, rows(*me), local_sem)
        mine.start()
        first = [copy(0, me, sibling, x_ref)] + [copy(1 + j, me, (*chip, c), x_ref) for j, chip in enumerate(chips)]
        for cp in first:
            cp.start()
        passed = [copy(4 + j, (*chip, c), sibling) for j, chip in enumerate(chips)]
        for j, chip in enumerate(chips):
            copy(1 + j, (*chip, c), me).wait_recv()
            passed[j].start()
        copy(0, sibling, me).wait_recv()
        for j, chip in enumerate(chips):
            copy(4 + j, (*chip, 1 - c), me).wait_recv()
        for cp in first + passed:
            cp.wait_send()
        mine.wait()

    return pl.pallas_call(
        body, name=name, in_specs=[ANY], out_specs=ANY,
        out_shape=jax.ShapeDtypeStruct((N_DEV,) + tuple(src.shape), src.dtype),
        scratch_shapes=[pltpu.SemaphoreType.DMA((N_DEV - 1,)), pltpu.SemaphoreType.DMA((N_DEV - 1,)),
                        pltpu.SemaphoreType.DMA],
    )(src)


HBM = pl.BlockSpec(memory_space=pltpu.HBM)
SEM = pl.BlockSpec(memory_space=pltpu.SEMAPHORE)
EFFECT = pltpu.SideEffectType.DATAFLOW_SIDE_EFFECTING


def _peer_copy(src_ref, land_ref, send_sems, recv_sems, per_peer, me, a, k, dev, idx, receiving):
    s = a * (N_DEV - 1) + k
    return pltpu.make_async_remote_copy(
        src_ref=src_ref.at[idx] if per_peer else src_ref, dst_ref=land_ref.at[idx if receiving else me],
        send_sem=send_sems.at[s], recv_sem=recv_sems.at[s], device_id=dev, device_id_type=MESH_ID)


def _push_start(name, srcs, per_peer):
    n = len(srcs)
    land_shapes = [(N_DEV,) + tuple(s.shape[1:] if per_peer else s.shape) for s in srcs]

    def body(*refs):
        src_refs, land_refs, send_sems, recv_sems, token = refs[:n], refs[n:2 * n], refs[2 * n], refs[2 * n + 1], refs[-1]
        me, peers = _me_and_peers()
        for a in range(n):
            for k, (dev, idx) in enumerate(peers):
                _peer_copy(src_refs[a], land_refs[a], send_sems, recv_sems, per_peer, me, a, k, dev, idx, False).start()
        token[...] = jnp.zeros_like(token)

    sems = pltpu.SemaphoreType.DMA((n * (N_DEV - 1),))
    res = pl.pallas_call(
        body, name=name,
        out_shape=(sems, sems, *[pltpu.HBM(s.shape, s.dtype) for s in srcs],
                   *[pltpu.HBM(ls, s.dtype) for ls, s in zip(land_shapes, srcs)], jax.ShapeDtypeStruct((8, 128), F32)),
        in_specs=(HBM,) * (2 * n), out_specs=(SEM, SEM) + (HBM,) * (2 * n) + (pl.BlockSpec(memory_space=pltpu.VMEM),),
        input_output_aliases={i: 2 + i for i in range(2 * n)},
        compiler_params=pltpu.CompilerParams(has_side_effects=EFFECT),
    )(*[pltpu.with_memory_space_constraint(s, pltpu.HBM) for s in srcs],
      *[pltpu.with_memory_space_constraint(lax.empty(ls, s.dtype), pltpu.HBM) for ls, s in zip(land_shapes, srcs)])
    return res[0], res[1], res[2:2 + n], res[2 + n:2 + 2 * n], res[-1]


def _push_wait(name, send_sems, recv_sems, srcs_thru, lands_thru, after, per_peer):
    n = len(srcs_thru)

    def body(*refs):
        src_refs, land_refs, send_sems, recv_sems = refs[:n], refs[n:2 * n], refs[2 * n], refs[2 * n + 1]
        me, peers = _me_and_peers()
        for a in range(n):
            for k, (dev, idx) in enumerate(peers):
                cp = _peer_copy(src_refs[a], land_refs[a], send_sems, recv_sems, per_peer, me, a, k, dev, idx, True)
                cp.wait_send()
                cp.wait_recv()

    both = list(srcs_thru) + list(lands_thru)
    res = pl.pallas_call(
        body, name=name, out_shape=tuple(pltpu.HBM(t.shape, t.dtype) for t in both),
        in_specs=(HBM,) * (2 * n) + (SEM, SEM, ANY), out_specs=(HBM,) * (2 * n),
        input_output_aliases={i: i for i in range(2 * n)},
        compiler_params=pltpu.CompilerParams(has_side_effects=EFFECT),
    )(*both, send_sems, recv_sems, after)
    return res[:n], res[n:]


def _sum_blocks(name, blocks):
    _, R, C = blocks.shape
    tc = next(t for t in (1024, 512, 256, 128) if C % t == 0 and (N_DEV * R * t * 2 <= 6 * 2 ** 20 or t == 128))

    def body(b_ref, o_ref):
        acc = b_ref[0].astype(F32)
        for k in range(1, N_DEV):
            acc = acc + b_ref[k].astype(F32)
        o_ref[...] = acc

    return pl.pallas_call(
        body, name=name, grid=(C // tc,), in_specs=[pl.BlockSpec((N_DEV, R, tc), lambda j: (0, 0, j))],
        out_specs=pl.BlockSpec((R, tc), lambda j: (0, j)), out_shape=jax.ShapeDtypeStruct((R, C), F32),
        compiler_params=_params(("arbitrary",)),
    )(blocks)


def _adamw(name, w, g, m, v):
    R, C = w.shape
    tr = R if R <= 512 else _pick(R, (256, 184, 176, 128, 8))
    spec = pl.BlockSpec((tr, C), lambda i: (i, 0))

    def body(w_ref, g_ref, m_ref, v_ref, d_ref, mo_ref, vo_ref):
        gv = g_ref[...]
        mn = ADAM_B1 * m_ref[...] + (1.0 - ADAM_B1) * gv
        vn = ADAM_B2 * v_ref[...] + (1.0 - ADAM_B2) * jnp.square(gv)
        m_hat = mn / (1.0 - ADAM_B1 ** ADAM_STEP)
        v_hat = vn / (1.0 - ADAM_B2 ** ADAM_STEP)
        d_ref[...] = -ADAM_LR * (m_hat / (jnp.sqrt(v_hat) + ADAM_EPS) + ADAM_WD * w_ref[...])
        mo_ref[...] = mn
        vo_ref[...] = vn

    return pl.pallas_call(
        body, name=name, grid=(R // tr,), in_specs=[spec] * 4, out_specs=[spec] * 3,
        out_shape=[jax.ShapeDtypeStruct((R, C), F32)] * 3, compiler_params=_params(("arbitrary",)),
    )(w, g, m, v)


WEIGHTS = ("meta_tokens", "norm_mix_w", "w_in", "ret_gn_w", "w_ret_out", "w_ssd_conv", "b_ssd_conv", "dt_bias_f",
           "dt_bias_b", "a_log_f", "a_log_b", "d_skip", "ssd_norm_w", "w_ssd_out", "w_out", "norm_ffn_w", "w_ffn_up",
           "w_ffn_conv", "b_ffn_conv", "w_ffn_down", "final_norm_w")
BIG = (("w_in", 1288, True), ("w_ffn_up", 704, True), ("w_ret_out", 128, False), ("w_ssd_out", 256, False),
       ("w_out", 128, False), ("w_ffn_down", 352, False))
REPLICATED = ("norm_mix_w", "ret_gn_w", "b_ssd_conv", "dt_bias_f", "dt_bias_b", "a_log_f", "a_log_b", "d_skip",
              "ssd_norm_w", "norm_ffn_w", "b_ffn_conv", "final_norm_w")
SMALL_SHARDED = (("meta_tokens", 16, 1024), ("w_ssd_conv", 3, 3072), ("w_ffn_conv", 3, 5632))


BIG_IN, BIG_REST = BIG[:1], BIG[1:]


def _pack_flat(arrays, rows):
    flat = jnp.concatenate([a.reshape(-1) for a in arrays])
    return jnp.pad(flat, (0, rows * D_MODEL - flat.shape[0])).reshape(rows, D_MODEL)


def _unpack_flat(slab, shapes):
    flat, out, o = slab.reshape(-1), [], 0
    for s in shapes:
        n = math.prod(s)
        out.append(flat[o:o + n].reshape(s))
        o += n
    return out


def kernel(x, meta_tokens, norm_mix_w, w_in, ret_gn_w, w_ret_out, w_ssd_conv, b_ssd_conv, dt_bias_f, dt_bias_b, a_log_f, a_log_b, d_skip, ssd_norm_w, w_ssd_out, w_out, norm_ffn_w, w_ffn_up, w_ffn_conv, b_ffn_conv, w_ffn_down, final_norm_w, loss_target, m_meta_tokens, m_norm_mix_w, m_w_in, m_ret_gn_w, m_w_ret_out, m_w_ssd_conv, m_b_ssd_conv, m_dt_bias_f, m_dt_bias_b, m_a_log_f, m_a_log_b, m_d_skip, m_ssd_norm_w, m_w_ssd_out, m_w_out, m_norm_ffn_w, m_w_ffn_up, m_w_ffn_conv, m_b_ffn_conv, m_w_ffn_down, m_final_norm_w, v_meta_tokens, v_norm_mix_w, v_w_in, v_ret_gn_w, v_w_ret_out, v_w_ssd_conv, v_b_ssd_conv, v_dt_bias_f, v_dt_bias_b, v_a_log_f, v_a_log_b, v_d_skip, v_ssd_norm_w, v_w_ssd_out, v_w_out, v_norm_ffn_w, v_w_ffn_up, v_w_ffn_conv, v_b_ffn_conv, v_w_ffn_down, v_final_norm_w):
    given = dict(locals())
    wt = {n: given[n] for n in WEIGHTS}
    mt = {n: given["m_" + n] for n in WEIGHTS}
    vt = {n: given["v_" + n] for n in WEIGHTS}
    me = 4 * lax.axis_index("x") + 2 * lax.axis_index("y") + lax.axis_index("c")

    small_names = [n for n, _, _ in SMALL_SHARDED]
    small_local = lambda tree: [tree[n].reshape(r, c // N_DEV) for n, r, c in SMALL_SHARDED]
    slab_view = lambda tree, name, transposed: tree[name][0].T if transposed else tree[name][0]
    all_in = _gather_two_level("gather_w_in", slab_view(wt, "w_in", True).astype(BF16))
    all_s = _push_blocks("gather_small", _pack_flat(small_local(wt), 8), False)
    rest_srcs = [slab_view(wt, name, t).astype(BF16) for name, _, t in BIG_REST]
    rest_srcs, all_in, all_s = lax.optimization_barrier((rest_srcs, all_in, all_s))
    rest_flight = _push_start("gather_rest_start", rest_srcs, False)
    all_s = all_s.reshape(N_DEV, -1)
    full = {"w_in_t": all_in.reshape(-1, D_MODEL)}

    def lands_with_own(flight, after, per_peer, name):
        srcs, lands = _push_wait(name, *flight[:4], after, per_peer)
        own = lambda s: lax.dynamic_slice_in_dim(s, me, 1, axis=0) if per_peer else s[None]
        return [lax.dynamic_update_slice_in_dim(land, own(s), me, axis=0) for s, land in zip(srcs, lands)]

    def late_weights(after):
        lands = lands_with_own(rest_flight, after, False, "gather_rest_wait")
        return {name + ("_t" if t else ""): land.reshape(N_DEV * r, D_MODEL) for (name, r, t), land in zip(BIG_REST, lands)}

    flights = {}

    def start_exchange(key, group, gd):
        srcs = [gd[name + ("_t" if t else "")].astype(BF16).reshape(N_DEV, r, D_MODEL) for name, r, t in group]
        flights[key] = _push_start("exchange_" + key + "_start", srcs, True)
        return flights[key][4][0, 0]

    o = 0
    for name, r, c in SMALL_SHARDED:
        n = r * c // N_DEV
        full[name] = all_s[:, o:o + n].reshape(N_DEV, r, c // N_DEV).transpose(1, 0, 2).reshape(r, c)
        o += n
    for name in REPLICATED:
        full[name] = wt[name]

    grads, delta, new_m, new_v = {}, {}, {}, {}

    def finish_exchange(key, group, after):
        lands = lands_with_own(flights[key], after, True, "exchange_" + key + "_wait")
        for (name, _, transposed), land in zip(group, lands):
            back = (lambda a: a.T[None]) if transposed else (lambda a: a[None])
            g_sum = _sum_blocks("sum_" + name, land)
            d, mn, vn = _adamw("adamw_" + name, slab_view(wt, name, transposed), g_sum,
                               slab_view(mt, name, transposed), slab_view(vt, name, transposed))
            grads[name], delta[name], new_m[name], new_v[name] = back(g_sum), back(d), back(mn), back(vn)

    def in_grads(gi):
        tick = start_exchange("in", BIG_IN, {"w_in_t": gi})
        finish_exchange("rest", BIG_REST, flights["in"][4])
        tick, _ = lax.optimization_barrier((tick, [delta[name] for name, _, _ in BIG_REST]))
        return tick

    loss, grad_x, g = _local_step(x[0], loss_target[0], full, rest_flight[4][0, 0], late_weights,
                                  lambda gd: start_exchange("rest", BIG_REST, gd), in_grads)

    small_parts = [g[n] for n in REPLICATED] + [g[n] for n in small_names] + [loss.reshape(1)]
    small_flight = _push_start("gather_small_grads_start", [_pack_flat(small_parts, 64)], False)
    finish_exchange("in", BIG_IN, small_flight[4])
    g_small = _sum_blocks("sum_small", lands_with_own(small_flight, delta["w_in"], False, "gather_small_grads_wait")[0])
    small_red = _unpack_flat(g_small, [wt[n].shape for n in REPLICATED] + [(r, c) for _, r, c in SMALL_SHARDED] + [(1,)])
    grads.update(zip(REPLICATED, small_red[:len(REPLICATED)]))
    for (name, r, c), red in zip(SMALL_SHARDED, small_red[len(REPLICATED):-1]):
        grads[name] = lax.dynamic_slice(red, (0, me * (c // N_DEV)), (r, c // N_DEV)).reshape(wt[name].shape)
    loss_all = small_red[-1][0]

    rest = list(REPLICATED) + small_names
    shapes = [wt[n].shape for n in rest]
    pack_rest = lambda tree: _pack_flat([tree[n] for n in rest], 24)
    d_rest, m_rest, v_rest = _adamw("adamw_small", pack_rest(wt), pack_rest(grads), pack_rest(mt), pack_rest(vt))
    delta.update(zip(rest, _unpack_flat(d_rest, shapes)))
    new_m.update(zip(rest, _unpack_flat(m_rest, shapes)))
    new_v.update(zip(rest, _unpack_flat(v_rest, shapes)))

    return (loss_all, grad_x[None], *[grads[n] for n in WEIGHTS], *[delta[n] for n in WEIGHTS],
            *[new_m[n] for n in WEIGHTS], *[new_v[n] for n in WEIGHTS])
```

```python
import functools
import math

import jax
import jax.numpy as jnp
from jax import lax
from jax.experimental import pallas as pl
from jax.experimental.pallas import tpu as pltpu

F32 = jnp.float32
BF16 = jnp.bfloat16

D_MODEL = 1024
CHUNK = 128
N_META = 16
PAD_ROWS = CHUNK - N_META
RET_HEADS = 4
RET_QK_DIM = 128
RET_V_DIM = 256
SSD_HEADS = 32
SSD_HEAD_DIM = 64
SSD_GROUPS = 4
SSD_STATE = 128
HEADS_PER_GROUP = SSD_HEADS // SSD_GROUPS
PAIRS_PER_GROUP = HEADS_PER_GROUP // 2
D_FF = 2816
EPS = 1e-6
ROPE_BASE = 10000.0
N_DEV = 8

ADAM_LR = 0.001
ADAM_B1 = 0.9
ADAM_B2 = 0.999
ADAM_EPS = 1e-08
ADAM_WD = 0.01
ADAM_STEP = 10

VMEM_LIMIT = 56 * 1024 * 1024
HALO = 16
HIGHEST = lax.Precision.HIGHEST

SEGMENTS = (("qk", 0, 1024), ("v", 1024, 2048), ("g", 2048, 3072), ("z", 3072, 5120), ("xs", 5120, 7168),
            ("B", 7168, 7680), ("C", 7680, 8192), ("dt", 8192, 8256), ("gates", 8256, 10304))


def _pick(n, cands):
    for c in cands:
        if n % c == 0:
            return c
    raise ValueError(f"no tile for {n}")


def _params(sem):
    return pltpu.CompilerParams(dimension_semantics=sem, vmem_limit_bytes=VMEM_LIMIT)


def _dot(a, b, dims=(((1,), (0,)), ((), ())), precision=None):
    return lax.dot_general(a, b, dims, preferred_element_type=F32, precision=precision)


def _dot_nt(a, b):
    return _dot(a, b, (((1,), (1,)), ((), ())))


def _dot_tn(a, b):
    return _dot(a, b, (((0,), (0,)), ((), ())))


def _mm(name, a, b, mode, add=None, out_dtype=F32):
    if mode == "nn":
        (M, K), N = a.shape, b.shape[1]
    elif mode == "nt":
        (M, K), N = a.shape, b.shape[0]
    else:
        (K, M), N = a.shape, b.shape[1]
    tn = _pick(N, (1408, 1024, 512, 128, 64))
    if mode == "tn":
        tm = M if M <= 1024 else _pick(M, (1408, 1024))
        tk = _pick(K, (2112, 512, 256, 128))
    else:
        tm = _pick(M, (1056, 512, 256, 128))
        tk = K if K <= 2816 else _pick(K, (2048, 1408, 1024))
    nk = K // tk
    if mode == "nn":
        a_spec = pl.BlockSpec((tm, tk), lambda n, m, k: (m, k))
        b_spec = pl.BlockSpec((tk, tn), lambda n, m, k: (k, n))
        dims = (((1,), (0,)), ((), ()))
    elif mode == "nt":
        a_spec = pl.BlockSpec((tm, tk), lambda n, m, k: (m, k))
        b_spec = pl.BlockSpec((tn, tk), lambda n, m, k: (n, k))
        dims = (((1,), (1,)), ((), ()))
    else:
        a_spec = pl.BlockSpec((tk, tm), lambda n, m, k: (k, m))
        b_spec = pl.BlockSpec((tk, tn), lambda n, m, k: (k, n))
        dims = (((0,), (0,)), ((), ()))
    o_spec = pl.BlockSpec((tm, tn), lambda n, m, k: (m, n))
    in_specs = [a_spec, b_spec] + ([o_spec] if add is not None else [])
    args = [a, b] + ([add] if add is not None else [])

    def body(*refs):
        if add is not None:
            a_ref, b_ref, r_ref, o_ref, acc = refs
        else:
            a_ref, b_ref, o_ref, acc = refs
        k = pl.program_id(2)
        p = _dot(a_ref[...].astype(BF16), b_ref[...].astype(BF16), dims)

        def finish(r):
            if add is not None:
                r = r + r_ref[...]
            o_ref[...] = r.astype(out_dtype)

        if nk == 1:
            finish(p)
        else:
            @pl.when(k == 0)
            def _():
                acc[...] = p

            @pl.when(k > 0)
            def _():
                acc[...] += p

            @pl.when(k == nk - 1)
            def _():
                finish(acc[...])

    return pl.pallas_call(
        body, name=name, grid=(N // tn, M // tm, nk), in_specs=in_specs, out_specs=o_spec,
        out_shape=jax.ShapeDtypeStruct((M, N), out_dtype),
        scratch_shapes=[pltpu.VMEM((tm, tn) if nk > 1 else (8, 128), F32)],
        compiler_params=_params(("arbitrary", "arbitrary", "arbitrary")),
    )(*args)


ANY_SPACE = pl.BlockSpec(memory_space=pl.ANY)


def _const(c):
    return lambda j: c


def _rows(name, fn, T, ncol, ins, params, outs, accs=(), halo=False, tall=False):
    tm = _pick(T, (1056, 512, 256, 128)) if tall else _pick(T, (384, 256, 128))
    R = T // tm
    hb = tm // HALO
    in_specs, args = [], []
    for spec in ins:
        arr, w, cf = spec[:3]
        lead = spec[3] if len(spec) > 3 else None
        if len(spec) > 4:
            rows, rf = spec[4]
            in_specs.append(pl.BlockSpec((rows, w), lambda j, i, cf=cf, rf=rf: (rf(i), cf(j))))
            args.append(arr)
            continue
        if lead is None:
            mk = lambda blk, rf, cf=cf: pl.BlockSpec(blk, lambda j, i: (rf(i), cf(j)))
            shape = lambda r, w=w: (r, w)
        else:
            mk = lambda blk, rf, cf=cf, lead=lead: pl.BlockSpec(blk, lambda j, i: (lead, rf(i), cf(j)))
            shape = lambda r, w=w: (None, r, w)
        in_specs.append(mk(shape(tm), lambda i: i))
        args.append(arr)
        if halo:
            in_specs.append(mk(shape(HALO), lambda i: jnp.maximum(i * hb - 1, 0)))
            in_specs.append(mk(shape(HALO), lambda i: jnp.minimum((i + 1) * hb, T // HALO - 1)))
            args += [arr, arr]
    for arr, w, cf in params:
        in_specs.append(pl.BlockSpec((arr.shape[0], w), lambda j, i, cf=cf: (0, cf(j))))
        args.append(arr)
    out_shape, out_specs, aliases = [], [], {}
    for k, (tw, w, cf, dt) in enumerate(outs):
        if not isinstance(tw, int):
            aliases[len(args)] = k
            in_specs.append(ANY_SPACE)
            args.append(tw)
            tw = tw.shape[1]
        out_shape.append(jax.ShapeDtypeStruct((T, tw), dt))
        out_specs.append(pl.BlockSpec((tm, w), lambda j, i, cf=cf: (i, cf(j))))
    for r, tw, w, cf in accs:
        out_shape.append(jax.ShapeDtypeStruct((r, tw), F32))
        out_specs.append(pl.BlockSpec((r, w), lambda j, i, cf=cf: (0, cf(j))))
    n_in, n_par, n_out, n_acc, n_alias = len(ins), len(params), len(outs), len(accs), len(aliases)

    def body(*refs):
        i = pl.program_id(1)
        vals, p = [], 0
        for _ in range(n_in):
            if halo:
                before = jnp.where(i > 0, refs[p + 1][...], jnp.zeros_like(refs[p + 1]))
                after = jnp.where(i < R - 1, refs[p + 2][...], jnp.zeros_like(refs[p + 2]))
                vals.append(jnp.concatenate([before, refs[p][...], after], axis=0).astype(F32))
                p += 3
            else:
                vals.append(refs[p][...].astype(F32))
                p += 1
        pvals = [refs[p + k][...] for k in range(n_par)]
        p += n_par + n_alias
        res = fn(i, *vals, *pvals)
        for k in range(n_out):
            refs[p + k][...] = res[k].astype(refs[p + k].dtype)
        p += n_out
        for k in range(n_acc):
            ref, v = refs[p + k], res[n_out + k]

            @pl.when(i == 0)
            def _(ref=ref, v=v):
                ref[...] = v

            @pl.when(i > 0)
            def _(ref=ref, v=v):
                ref[...] += v

    res = pl.pallas_call(
        body, name=name, grid=(ncol, R), in_specs=in_specs, out_specs=out_specs, out_shape=out_shape,
        input_output_aliases=aliases, compiler_params=_params(("arbitrary", "arbitrary")),
    )(*args)
    return res


def _tile_rows(T):
    return _pick(T, (384, 256, 128))


def _row_ids(i, T, halo=False, tall=False):
    tm = _pick(T, (1056, 512, 256, 128)) if tall else _tile_rows(T)
    if halo:
        return i * tm - HALO + lax.broadcasted_iota(jnp.int32, (tm + 2 * HALO, 1), 0)
    return i * tm + lax.broadcasted_iota(jnp.int32, (tm, 1), 0)


def _rms(x, w):
    return x * lax.rsqrt(jnp.mean(x * x, axis=-1, keepdims=True) + EPS) * w


def _silu(x):
    return x * jax.nn.sigmoid(x)


def _conv3(x, w):
    n = x.shape[0]
    return w[0:1] * pltpu.roll(x, 1, 0) + w[1:2] * x + w[2:3] * pltpu.roll(x, n - 1, 0)


def _conv3_t(d, w):
    n = d.shape[0]
    return w[0:1] * pltpu.roll(d, n - 1, 0) + w[1:2] * d + w[2:3] * pltpu.roll(d, 1, 0)


def _center(x):
    return x[HALO:x.shape[0] - HALO]


def _retention(name, a, b, v, T, da, dv, into=None):
    (a, a0), (b, b0), (v, v0) = [t if isinstance(t, tuple) else (t, 0) for t in (a, b, v)]
    nc = T // CHUNK
    log_gammas = [math.log(1.0 - 2.0 ** (-5.0 - h)) for h in range(RET_HEADS)]

    def body(*refs):
        a_ref, b_ref, v_ref = refs[:3]
        out_ref, o_ref, st, st_b = refs[-4:]
        h = pl.program_id(0)
        lg = jnp.float32(log_gammas[RET_HEADS - 1])
        for k in range(RET_HEADS - 2, -1, -1):
            lg = jnp.where(h == k, jnp.float32(log_gammas[k]), lg)
        li = lax.broadcasted_iota(jnp.int32, (CHUNK, CHUNK), 0)
        si = lax.broadcasted_iota(jnp.int32, (CHUNK, CHUNK), 1)
        dmat = jnp.exp(lg * jnp.abs(li - si).astype(F32))
        pos = lax.broadcasted_iota(jnp.int32, (CHUNK, 1), 0).astype(F32)
        kdec_f = jnp.exp((CHUNK - 1 - pos) * lg)
        qdec_f = jnp.exp((pos + 1) * lg)
        kdec_b = jnp.exp(pos * lg)
        qdec_b = jnp.exp((CHUNK - pos) * lg)
        cdec = jnp.exp(CHUNK * lg)

        def rows(n):
            return pl.ds(pl.multiple_of(n * CHUNK, CHUNK), CHUNK)

        st[...] = jnp.zeros_like(st)
        st_b[...] = jnp.zeros_like(st_b)
        o_ref[...] = jnp.zeros_like(o_ref)

        def step(m, carry):
            r = rows(m)
            av, bv, vv = a_ref[r, :], b_ref[r, :], v_ref[r, :].astype(BF16)
            s = _dot_nt(av.astype(BF16), bv.astype(BF16)) * dmat
            o_ref[r, :] += _dot(s.astype(BF16), vv) + _dot((av * qdec_f).astype(BF16), st[...].astype(BF16))
            st[...] = cdec * st[...] + _dot_tn((bv * kdec_f).astype(BF16), vv)
            r = rows(nc - 1 - m)
            av, bv, vv = a_ref[r, :], b_ref[r, :], v_ref[r, :].astype(BF16)
            o_ref[r, :] += _dot((av * qdec_b).astype(BF16), st_b[...].astype(BF16))
            st_b[...] = cdec * st_b[...] + _dot_tn((bv * kdec_b).astype(BF16), vv)
            return carry

        lax.fori_loop(0, nc, step, 0, unroll=True)
        out_ref[...] = o_ref[...].astype(out_ref.dtype)

    in_specs = [pl.BlockSpec((T, da), lambda h: (0, a0 // da + h)), pl.BlockSpec((T, da), lambda h: (0, b0 // da + h)),
                pl.BlockSpec((T, dv), lambda h: (0, v0 // dv + h))]
    if into is None:
        args, o0, aliases = (a, b, v), 0, {}
        out_shape = jax.ShapeDtypeStruct((T, RET_HEADS * dv), F32)
    else:
        args, o0, aliases = (a, b, v, into[0]), into[1], {3: 0}
        in_specs.append(ANY_SPACE)
        out_shape = jax.ShapeDtypeStruct(into[0].shape, into[0].dtype)
    return pl.pallas_call(
        body, name=name, grid=(RET_HEADS,), in_specs=in_specs,
        out_specs=pl.BlockSpec((T, dv), lambda h: (0, o0 // dv + h)), out_shape=out_shape,
        input_output_aliases=aliases,
        scratch_shapes=[pltpu.VMEM((T, dv), F32), pltpu.VMEM((da, dv), F32), pltpu.VMEM((da, dv), F32)],
        compiler_params=_params(("arbitrary",)),
    )(*args)


def _softplus(x):
    return jnp.maximum(x, 0.0) + jnp.log1p(jnp.exp(-jnp.abs(x)))


def _lane_lo():
    return lax.broadcasted_iota(jnp.int32, (1, CHUNK), 1) < SSD_HEAD_DIM


def _pair_cols(col, j):
    return jnp.where(_lane_lo(), col[:, 2 * j:2 * j + 1], col[:, 2 * j + 1:2 * j + 2])


def _pair_rows(colr, j):
    lo = lax.broadcasted_iota(jnp.int32, (CHUNK, 1), 0) < SSD_HEAD_DIM
    return jnp.where(lo, colr[2 * j:2 * j + 1, :], colr[2 * j + 1:2 * j + 2, :])


def _onehot8(h):
    return (lax.broadcasted_iota(jnp.int32, (1, HEADS_PER_GROUP), 1) == h).astype(F32)


def _ssd_pre(d, c, rawc, rawr, bc, br, alc, alr):
    li = lax.broadcasted_iota(jnp.int32, (CHUNK, CHUNK), 0)
    si = lax.broadcasted_iota(jnp.int32, (CHUNK, CHUNK), 1)
    dif = li - si if d == 0 else si - li
    mask = dif >= 0
    mask_t = dif <= 0
    rowc = c * CHUNK + lax.broadcasted_iota(jnp.int32, (CHUNK, 1), 0)
    rowr = c * CHUNK + lax.broadcasted_iota(jnp.int32, (1, CHUNK), 1)
    dtc = jnp.where(rowc >= PAD_ROWS, _softplus(rawc + bc), 0.0)
    dtr = jnp.where(rowr >= PAD_ROWS, _softplus(rawr + br), 0.0)
    ac = -jnp.exp(alc)
    ar = -jnp.exp(alr)
    dlc = dtc * ac
    dlr = dtr * ar
    alpc = _dot(mask.astype(F32), dlc, precision=HIGHEST)
    alpr = _dot(dlr, mask_t.astype(F32), precision=HIGHEST)
    endc = jnp.sum(dlc, axis=0, keepdims=True)
    endr = jnp.sum(dlr, axis=1, keepdims=True)
    return dict(mask=mask, mask_t=mask_t, dtc=dtc, ac=ac, alpc=alpc, alpr=alpr, endc=endc, endr=endr,
                valid=rowc >= PAD_ROWS)


def _chunk_of(d, n, nc):
    return n + d * (nc - 1 - 2 * n)


GROUP_WIDTH = HEADS_PER_GROUP * SSD_HEAD_DIM


def _chunks_per_step(nc, most=3):
    return next(c for c in (11, 3, 1) if c <= most and nc % c == 0)


def _ssd_in_specs(d, cfn, rows):
    return [
        pl.BlockSpec((rows, GROUP_WIDTH), lambda g, n: (cfn(d, n), g)),
        pl.BlockSpec((rows, SSD_STATE), lambda g, n: (cfn(d, n), g)),
        pl.BlockSpec((rows, SSD_STATE), lambda g, n: (cfn(d, n), g)),
        pl.BlockSpec((None, None, rows, HEADS_PER_GROUP), lambda g, n: (d, g, cfn(d, n), 0)),
        pl.BlockSpec((None, None, HEADS_PER_GROUP, rows), lambda g, n: (d, g, 0, cfn(d, n))),
        pl.BlockSpec((None, None, 1, HEADS_PER_GROUP), lambda g, n: (d, g, 0, 0)),
        pl.BlockSpec((None, None, HEADS_PER_GROUP, 1), lambda g, n: (d, g, 0, 0)),
        pl.BlockSpec((None, None, 1, HEADS_PER_GROUP), lambda g, n: (d, g, 0, 0)),
        pl.BlockSpec((None, None, HEADS_PER_GROUP, 1), lambda g, n: (d, g, 0, 0)),
    ]


N_SSD_IN = 9


def _ssd_fwd(xs, bm, cm, small, T):
    nc = T // CHUNK
    cps = _chunks_per_step(nc, 11)
    rows = cps * CHUNK
    cfn = lambda d, n: _chunk_of(d, n, nc // cps)

    def one_direction(d, n, ins, y_ref, hs_ref, h_scr):
        x_ref, b_ref, c_ref, rawc_ref, rawr_ref, *per_group = ins
        for kk in range(cps):
            k = kk if d == 0 else cps - 1 - kk
            r = pl.ds(k * CHUNK, CHUNK)
            one_chunk(d, cfn(d, n) * cps + k,
                      (x_ref.at[r], b_ref.at[r], c_ref.at[r], rawc_ref.at[r], rawr_ref.at[:, r], *per_group),
                      y_ref.at[r], hs_ref.at[k], h_scr)

    def one_chunk(d, c, ins, y_ref, hs_ref, h_scr):
        x_ref, b_ref, c_ref, rawc_ref, rawr_ref, bc_ref, br_ref, alc_ref, alr_ref = ins
        q = _ssd_pre(d, c, rawc_ref[...], rawr_ref[...], bc_ref[...], br_ref[...], alc_ref[...], alr_ref[...])
        bv = b_ref[...].astype(BF16)
        cv = c_ref[...].astype(BF16)
        cb = _dot_nt(cv, bv)
        lo = _lane_lo()
        for j in range(PAIRS_PER_GROUP):
            xp = x_ref[:, j * CHUNK:(j + 1) * CHUNK]
            xd = xp * _pair_cols(q["dtc"], j)
            xdb = xd.astype(BF16)
            yi = []
            for e in range(2):
                h = 2 * j + e
                lm = jnp.exp(jnp.where(q["mask"], q["alpc"][:, h:h + 1] - q["alpr"][h:h + 1, :], -jnp.inf))
                yi.append(_dot((cb * lm).astype(BF16), xdb))
            alp = _pair_cols(q["alpc"], j)
            hp = h_scr[j]
            hs_ref[j] = hp
            yo = jnp.exp(alp) * _dot_nt(cv, hp.astype(BF16))
            y_ref[:, j * CHUNK:(j + 1) * CHUNK] = (jnp.where(lo, yi[0], yi[1]) + yo).astype(y_ref.dtype)
            de = jnp.exp(_pair_cols(q["endc"], j) - alp)
            h_scr[j] = jnp.exp(_pair_rows(q["endr"], j)) * hp + _dot_tn((xd * de).astype(BF16), bv)

    def body(*refs):
        n = pl.program_id(1)
        ins, (y_f, y_b, hs_f, hs_b, h_scr) = refs[:2 * N_SSD_IN], refs[2 * N_SSD_IN:]

        @pl.when(n == 0)
        def _():
            h_scr[...] = jnp.zeros_like(h_scr)

        one_direction(0, n, ins[:N_SSD_IN], y_f, hs_f, h_scr.at[0])
        one_direction(1, n, ins[N_SSD_IN:], y_b, hs_b, h_scr.at[1])

    y_spec = lambda d: pl.BlockSpec((rows, GROUP_WIDTH), lambda g, n: (cfn(d, n), g))
    hs_spec = lambda d: pl.BlockSpec((None, cps, PAIRS_PER_GROUP, CHUNK, SSD_STATE),
                                     lambda g, n: (g, cfn(d, n), 0, 0, 0))
    y_shape = jax.ShapeDtypeStruct((T, SSD_HEADS * SSD_HEAD_DIM), BF16)
    hs_shape = jax.ShapeDtypeStruct((SSD_GROUPS, nc, PAIRS_PER_GROUP, CHUNK, SSD_STATE), F32)
    y_f, y_b, hs_f, hs_b = pl.pallas_call(
        body, name="ssd_fwd", grid=(SSD_GROUPS, nc // cps),
        in_specs=_ssd_in_specs(0, cfn, rows) + _ssd_in_specs(1, cfn, rows),
        out_specs=[y_spec(0), y_spec(1), hs_spec(0), hs_spec(1)],
        out_shape=[y_shape, y_shape, hs_shape, hs_shape],
        scratch_shapes=[pltpu.VMEM((2, PAIRS_PER_GROUP, CHUNK, SSD_STATE), F32)],
        compiler_params=_params(("arbitrary", "arbitrary")),
    )(xs, bm, cm, *small, xs, bm, cm, *small)
    return (y_f, y_b), (hs_f, hs_b)


def _ssd_bwd(xs, bm, cm, small, hs, dy, T):
    nc = T // CHUNK
    cps = _chunks_per_step(nc, 11)
    rows = cps * CHUNK
    cfn = lambda d, n: _chunk_of(1 - d, n, nc // cps)

    def one_direction(d, n, ins, outs, dh_scr):
        x_ref, b_ref, c_ref, rawc_ref, rawr_ref, bc_ref, br_ref, alc_ref, alr_ref, hs_ref, dy_ref = ins
        dx_ref, db_ref, dc_ref, draw_ref, dbias_ref, dalog_ref = outs
        for kk in range(cps):
            k = cps - 1 - kk if d == 0 else kk
            r = pl.ds(k * CHUNK, CHUNK)
            one_chunk(d, cfn(d, n) * cps + k, n if kk == 0 else None,
                      (x_ref.at[r], b_ref.at[r], c_ref.at[r], rawc_ref.at[r], rawr_ref.at[:, r], bc_ref, br_ref,
                       alc_ref, alr_ref, hs_ref.at[k], dy_ref.at[r]),
                      (dx_ref.at[r], db_ref.at[r], dc_ref.at[r], draw_ref.at[r], dbias_ref, dalog_ref), dh_scr)

    def one_chunk(d, c, first_of_step, ins, outs, dh_scr):
        x_ref, b_ref, c_ref, rawc_ref, rawr_ref, bc_ref, br_ref, alc_ref, alr_ref, hs_ref, dy_ref = ins
        dx_ref, db_ref, dc_ref, draw_ref, dbias_ref, dalog_ref = outs
        rawc, bc = rawc_ref[...], bc_ref[...]
        q = _ssd_pre(d, c, rawc, rawr_ref[...], bc, br_ref[...], alc_ref[...], alr_ref[...])
        b32, c32 = b_ref[...], c_ref[...]
        bv, cv = b32.astype(BF16), c32.astype(BF16)
        cb = _dot_nt(cv, bv)
        cbt = _dot_nt(bv, cv)
        lo = _lane_lo()
        row_lo = lax.broadcasted_iota(jnp.int32, (CHUNK, 1), 0) < SSD_HEAD_DIM
        dcb = jnp.zeros((CHUNK, CHUNK), F32)
        dcp = jnp.zeros((CHUNK, SSD_STATE), F32)
        dbp = jnp.zeros((CHUNK, SSD_STATE), F32)
        dalp = jnp.zeros((CHUNK, HEADS_PER_GROUP), F32)
        dend = jnp.zeros((1, HEADS_PER_GROUP), F32)
        ddtx = jnp.zeros((CHUNK, HEADS_PER_GROUP), F32)

        def half_sums(t):
            return (jnp.sum(jnp.where(lo, t, 0.0), axis=1, keepdims=True),
                    jnp.sum(jnp.where(lo, 0.0, t), axis=1, keepdims=True))

        for j in range(PAIRS_PER_GROUP):
            xp = x_ref[:, j * CHUNK:(j + 1) * CHUNK]
            dtp = _pair_cols(q["dtc"], j)
            xd = xp * dtp
            xdb = xd.astype(BF16)
            dyp = dy_ref[:, j * CHUNK:(j + 1) * CHUNK]
            dyb = dyp.astype(BF16)
            hn = hs_ref[j]
            hnb = hn.astype(BF16)
            dh1 = dh_scr[j]
            dh1b = dh1.astype(BF16)
            alp = _pair_cols(q["alpc"], j)
            ea = jnp.exp(alp)
            de = jnp.exp(_pair_cols(q["endc"], j) - alp)
            dxi = []
            for e in range(2):
                h = 2 * j + e
                diff = q["alpc"][:, h:h + 1] - q["alpr"][h:h + 1, :]
                lm = jnp.exp(jnp.where(q["mask"], diff, -jnp.inf))
                mt = cbt * jnp.exp(jnp.where(q["mask_t"], -diff, -jnp.inf))
                dxi.append(_dot(mt.astype(BF16), dyb))
                dyeb_h = (jnp.where(lo, dyp, 0.0) if e == 0 else jnp.where(lo, 0.0, dyp)).astype(BF16)
                gl = _dot_nt(dyeb_h, xdb) * lm
                dcb = dcb + gl
                ra = jnp.sum(gl * cb - _dot_nt(xdb, dyeb_h) * mt, axis=1, keepdims=True)
                dalp = dalp + ra * _onehot8(h)
            y_off = ea * _dot_nt(cv, hnb)
            dxs_state = de * _dot_nt(bv, dh1b)
            dxd = jnp.where(lo, dxi[0], dxi[1]) + dxs_state
            dyeb = (dyp * ea).astype(BF16)
            dcp = dcp + _dot(dyeb, hnb)
            dbp = dbp + _dot((xd * de).astype(BF16), dh1b)
            dh_scr[j] = jnp.exp(_pair_rows(q["endr"], j)) * dh1 + _dot_tn(dyeb, cv)
            r0, r1 = half_sums(dyp * y_off - xd * dxs_state)
            dalp = dalp + r0 * _onehot8(2 * j) + r1 * _onehot8(2 * j + 1)
            t0, t1 = half_sums(jnp.sum(xd * dxs_state, axis=0, keepdims=True))
            u = dh1 * hn
            u0 = jnp.sum(jnp.sum(jnp.where(row_lo, u, 0.0), axis=0, keepdims=True), axis=1, keepdims=True)
            u1 = jnp.sum(jnp.sum(jnp.where(row_lo, 0.0, u), axis=0, keepdims=True), axis=1, keepdims=True)
            eend = jnp.exp(q["endc"])
            dend = dend + (t0 + eend * u0) * _onehot8(2 * j) + (t1 + eend * u1) * _onehot8(2 * j + 1)
            dx_ref[:, j * CHUNK:(j + 1) * CHUNK] = (dxd * dtp).astype(dx_ref.dtype)
            w0, w1 = half_sums(dxd * xp)
            ddtx = ddtx + w0 * _onehot8(2 * j) + w1 * _onehot8(2 * j + 1)

        dcbb = dcb.astype(BF16)
        dc_ref[...] = (dcp + _dot(dcbb, bv)).astype(dc_ref.dtype)
        db_ref[...] = (dbp + _dot_tn(dcbb, cv)).astype(db_ref.dtype)
        ddl = _dot(q["mask_t"].astype(F32), dalp, precision=HIGHEST) + dend
        ddt = ddl * q["ac"] + ddtx
        draw = jnp.where(q["valid"], ddt * jax.nn.sigmoid(rawc + bc), 0.0)
        draw_ref[...] = draw
        dbias = jnp.sum(draw, axis=0, keepdims=True)
        dalog = jnp.sum(ddl * q["dtc"], axis=0, keepdims=True) * q["ac"]

        def add():
            dbias_ref[...] += dbias
            dalog_ref[...] += dalog

        if first_of_step is None:
            add()
        else:
            @pl.when(first_of_step == 0)
            def _():
                dbias_ref[...] = dbias
                dalog_ref[...] = dalog

            pl.when(first_of_step > 0)(add)

    n_in, n_out = N_SSD_IN + 2, 6

    def body(*refs):
        n = pl.program_id(1)
        ins, outs, dh_scr = refs[:2 * n_in], refs[2 * n_in:2 * (n_in + n_out)], refs[-1]

        @pl.when(n == 0)
        def _():
            dh_scr[...] = jnp.zeros_like(dh_scr)

        one_direction(0, n, ins[:n_in], outs[:n_out], dh_scr.at[0])
        one_direction(1, n, ins[n_in:], outs[n_out:], dh_scr.at[1])

    def in_specs(d):
        return _ssd_in_specs(d, cfn, rows) + [
            pl.BlockSpec((None, cps, PAIRS_PER_GROUP, CHUNK, SSD_STATE), lambda g, n: (g, cfn(d, n), 0, 0, 0)),
            pl.BlockSpec((rows, GROUP_WIDTH), lambda g, n: (cfn(d, n), g))]

    def out_specs(d):
        acc = pl.BlockSpec((None, 1, HEADS_PER_GROUP), lambda g, n: (g, 0, 0))
        return [pl.BlockSpec((rows, GROUP_WIDTH), lambda g, n: (cfn(d, n), g)),
                pl.BlockSpec((rows, SSD_STATE), lambda g, n: (cfn(d, n), g)),
                pl.BlockSpec((rows, SSD_STATE), lambda g, n: (cfn(d, n), g)),
                pl.BlockSpec((None, rows, HEADS_PER_GROUP), lambda g, n: (g, cfn(d, n), 0)), acc, acc]

    out_shape = [jax.ShapeDtypeStruct((T, SSD_HEADS * SSD_HEAD_DIM), BF16),
                 jax.ShapeDtypeStruct((T, SSD_GROUPS * SSD_STATE), BF16),
                 jax.ShapeDtypeStruct((T, SSD_GROUPS * SSD_STATE), BF16),
                 jax.ShapeDtypeStruct((SSD_GROUPS, T, HEADS_PER_GROUP), F32),
                 jax.ShapeDtypeStruct((SSD_GROUPS, 1, HEADS_PER_GROUP), F32),
                 jax.ShapeDtypeStruct((SSD_GROUPS, 1, HEADS_PER_GROUP), F32)]
    res = pl.pallas_call(
        body, name="ssd_bwd", grid=(SSD_GROUPS, nc // cps),
        in_specs=in_specs(0) + in_specs(1), out_specs=out_specs(0) + out_specs(1), out_shape=out_shape * 2,
        scratch_shapes=[pltpu.VMEM((2, PAIRS_PER_GROUP, CHUNK, SSD_STATE), F32)],
        compiler_params=_params(("arbitrary", "arbitrary")),
    )(xs, bm, cm, *small, hs[0], dy, xs, bm, cm, *small, hs[1], dy)
    return [(res[k], res[n_out + k]) for k in range(n_out)]


def _rot(x, cs, sn):
    return x * cs + pltpu.roll(x, RET_QK_DIM // 2, 1) * sn


def _rot_t(d, cs, sn):
    return d * cs + pltpu.roll(d * sn, RET_QK_DIM // 2, 1)


def _ret_post(y, g, w):
    parts = []
    for h in range(RET_HEADS):
        yh = y[:, h * RET_V_DIM:(h + 1) * RET_V_DIM]
        mu = jnp.mean(yh, axis=-1, keepdims=True)
        var = jnp.mean(jnp.square(yh - mu), axis=-1, keepdims=True)
        parts.append((yh - mu) * lax.rsqrt(var + EPS))
    return _silu(g) * (jnp.concatenate(parts, axis=1) * w)


def _ssd_post(yf, yb, xs, z, dskip, w):
    y = (yf + yb + xs * dskip) * _silu(z)
    return y * lax.rsqrt(jnp.mean(y * y, axis=-1, keepdims=True) + EPS) * w


def _merge(gates, yr, ys, valid):
    m = jax.nn.sigmoid(gates[:, :D_MODEL]) * yr + jax.nn.sigmoid(gates[:, D_MODEL:]) * ys
    return jnp.where(valid, m, 0.0)


def _rope_tables(T):
    half = RET_QK_DIM // 2
    inv = ROPE_BASE ** (-jnp.arange(half, dtype=F32) / half)
    pos = (jnp.arange(T) - PAD_ROWS).astype(F32)
    ang = pos[:, None] * inv[None, :]
    cos, sin = jnp.cos(ang), jnp.sin(ang)
    return jnp.concatenate([cos, cos], axis=1), jnp.concatenate([-sin, sin], axis=1)


def _per_group(v):
    c = v.reshape(SSD_GROUPS, 1, HEADS_PER_GROUP)
    return c, c.reshape(SSD_GROUPS, HEADS_PER_GROUP, 1)


def _local_step(x, target, w, tick, late_weights, early_grads, in_grads):
    S = x.shape[0]
    T = S + CHUNK
    tm = _tile_rows(T)
    c0 = _const(0)

    h0 = jnp.concatenate([jnp.zeros((PAD_ROWS, D_MODEL), F32), w["meta_tokens"], x], axis=0)
    seg_at = {name: a for name, a, _ in SEGMENTS}
    w_main = w["w_in_t"][:seg_at["dt"]]
    w_dt = jnp.pad(w["w_in_t"][seg_at["dt"]:seg_at["gates"]], ((0, CHUNK - 2 * SSD_HEADS), (0, 0)))
    w_gates = w["w_in_t"][seg_at["gates"]:]

    def norm_cast(name, h, nw):
        return _rows(name, lambda i, hv, wv: (_rms(hv, wv),), T, 1, [(h, D_MODEL, c0)], [(nw, D_MODEL, c0)],
                     [(D_MODEL, D_MODEL, c0, BF16)], tall=True)[0]

    u = norm_cast("norm_mix", h0, w["norm_mix_w"] + tick)
    p_main = _mm("proj_main", u, w_main, "nt", out_dtype=BF16)
    p_dt = _mm("proj_dt", u, w_dt, "nt")
    p_gates = _mm("proj_gates", u, w_gates, "nt", out_dtype=BF16)

    def seg(name, width, cf=c0):
        base = seg_at[name] // width
        return (p_main, width, lambda j: base + cf(j))

    cs, sn = _rope_tables(T)
    scale = RET_QK_DIM ** -0.5

    def rot_fn(i, qk, csv, snv):
        q = [_rot(qk[:, h * 128:(h + 1) * 128], csv, snv) for h in range(RET_HEADS)]
        k = [_rot(qk[:, (RET_HEADS + h) * 128:(RET_HEADS + h + 1) * 128], csv, snv) * scale for h in range(RET_HEADS)]
        return jnp.concatenate(q, axis=1), jnp.concatenate(k, axis=1)

    qr, kr = _rows("rotary", rot_fn, T, 1, [seg("qk", 1024), (cs, 128, c0), (sn, 128, c0)], [],
                   [(512, 512, c0, F32), (512, 512, c0, F32)], tall=True)
    v_at = (p_main, seg_at["v"])
    y_ret = _retention("retention", qr, kr, v_at, T, RET_QK_DIM, RET_V_DIM)
    a_ret = _rows("ret_post", lambda i, y, g, gw: (_ret_post(y, g, gw),), T, 1,
                  [(y_ret, 1024, c0), seg("g", 1024)], [(w["ret_gn_w"], 1024, c0)],
                  [(1024, 1024, c0, BF16)], tall=True)[0]

    conv_w = {"xs": w["w_ssd_conv"][:, :2048], "B": w["w_ssd_conv"][:, 2048:2560], "C": w["w_ssd_conv"][:, 2560:]}
    conv_b = {"xs": w["b_ssd_conv"][:, :2048], "B": w["b_ssd_conv"][:, 2048:2560], "C": w["b_ssd_conv"][:, 2560:]}

    def ssd_conv_fn(i, xe, cw, cb):
        r = _row_ids(i, T, True, tall=True)
        return (_center(jnp.where(r >= PAD_ROWS, _silu(_conv3(xe, cw) + cb), 0.0)),)

    act = {}
    for name in ("xs", "B", "C"):
        wd = conv_w[name].shape[1]
        cw = 512
        act[name] = _rows("ssd_conv_" + name, ssd_conv_fn, T, wd // cw, [seg(name, cw, lambda j: j)],
                          [(conv_w[name], cw, lambda j: j), (conv_b[name], cw, lambda j: j)],
                          [(wd, cw, lambda j: j, BF16)], halo=True, tall=True)[0]

    raw = p_dt[:, :2 * SSD_HEADS].reshape(T, 2, SSD_GROUPS, HEADS_PER_GROUP)
    rawc = raw.transpose(1, 2, 0, 3)
    rawr = raw.transpose(1, 2, 3, 0)
    bias = [_per_group(w["dt_bias_f"]), _per_group(w["dt_bias_b"])]
    alog = [_per_group(w["a_log_f"]), _per_group(w["a_log_b"])]
    small = (rawc, rawr, jnp.stack([bias[0][0], bias[1][0]]), jnp.stack([bias[0][1], bias[1][1]]),
             jnp.stack([alog[0][0], alog[1][0]]), jnp.stack([alog[0][1], alog[1][1]]))
    y_dir, states = _ssd_fwd(act["xs"], act["B"], act["C"], small, T)

    dskip_e = jnp.repeat(w["d_skip"], SSD_HEAD_DIM, axis=1)
    gcol = lambda j: j
    gw_ = 512
    a_ssd = _rows("ssd_post", lambda i, yf, yb, xv, zv, dk, nw: (_ssd_post(yf, yb, xv, zv, dk, nw),), T, SSD_GROUPS,
                  [(y_dir[0], gw_, gcol), (y_dir[1], gw_, gcol), (act["xs"], gw_, gcol), seg("z", gw_, gcol)],
                  [(dskip_e, gw_, gcol), (w["ssd_norm_w"], gw_, gcol)], [(2048, gw_, gcol, BF16)], tall=True)[0]

    w = dict(w, **late_weights(a_ssd))
    w_up_g, w_up_u = w["w_ffn_up_t"][:D_FF], w["w_ffn_up_t"][D_FF:]
    y_ret_o = _mm("ret_out", a_ret, w["w_ret_out"], "nn", out_dtype=BF16)
    y_ssd_o = _mm("ssd_out", a_ssd, w["w_ssd_out"], "nn", out_dtype=BF16)

    def merge_fn(i, gates, yr, ys):
        return (_merge(gates, yr, ys, _row_ids(i, T) >= PAD_ROWS),)

    merged = _rows("merge", merge_fn, T, 1, [(p_gates, 2048, c0), (y_ret_o, 1024, c0), (y_ssd_o, 1024, c0)], [],
                   [(1024, 1024, c0, BF16)])[0]
    h1 = _mm("mix_out", merged, w["w_out"], "nn", add=h0)

    n2 = norm_cast("norm_ffn", h1, w["norm_ffn_w"])
    f_pre = _mm("ffn_up", n2, w["w_ffn_up_t"], "nt", out_dtype=BF16)
    cwg, cwu = w["w_ffn_conv"][:, :D_FF], w["w_ffn_conv"][:, D_FF:]
    cbg, cbu = w["b_ffn_conv"][:, :D_FF], w["b_ffn_conv"][:, D_FF:]
    fcol = lambda j: j
    fw = 1408

    def ffn_act_fn(i, ge, ue, wg, wu, bg, bu):
        return (_center(_silu(_conv3(ge, wg) + bg) * (_conv3(ue, wu) + bu)),)

    ucol = lambda j: D_FF // fw + j
    a2 = _rows("ffn_act", ffn_act_fn, T, D_FF // fw, [(f_pre, fw, fcol), (f_pre, fw, ucol)],
               [(cwg, fw, fcol), (cwu, fw, fcol), (cbg, fw, fcol), (cbu, fw, fcol)], [(D_FF, fw, fcol, BF16)],
               halo=True)[0]
    h2 = _mm("ffn_down", a2, w["w_ffn_down"], "nn", add=h1)

    fnw = w["final_norm_w"].reshape(1, D_MODEL)

    per_tile = tm // CHUNK
    tgt_specs = [(target, D_MODEL, c0, None, (CHUNK, lambda i, k=k: jnp.maximum(per_tile * i - 1 + k, 0)))
                 for k in range(per_tile)]

    def loss_fn(i, hv, *rest):
        tv, nw = jnp.concatenate(rest[:per_tile], axis=0), rest[per_tile]
        valid = _row_ids(i, T) >= CHUNK
        y, vjp = jax.vjp(_rms, hv, nw)
        diff = jnp.where(valid, y - tv, 0.0)
        dh, dw = vjp(diff * (1.0 / D_MODEL))
        part = 0.5 / D_MODEL * jnp.sum(jnp.sum(diff * diff, axis=1, keepdims=True), axis=0, keepdims=True)
        return dh, jnp.broadcast_to(part, (1, 128)), dw

    dh2, loss_acc, d_fnw = _rows("loss", loss_fn, T, 1, [(h2, D_MODEL, c0)] + tgt_specs, [(fnw, D_MODEL, c0)],
                                 [(D_MODEL, D_MODEL, c0, F32)], [(1, 128, 128, c0), (1, D_MODEL, D_MODEL, c0)])
    loss = loss_acc[0, 0]
    grads = {"final_norm_w": d_fnw.reshape(D_MODEL)}

    da2 = _mm("d_ffn_act", dh2, w["w_ffn_down"], "nt", out_dtype=BF16)
    grads["w_ffn_down"] = _mm("g_ffn_down", a2, dh2, "tn", out_dtype=BF16)

    def ffn_bwd_fn(i, ge, ue, de, wg, wu, bg, bu):
        fg = _conv3(ge, wg) + bg
        fu = _conv3(ue, wu) + bu
        sg = jax.nn.sigmoid(fg)
        dfg = de * fu * (sg * (1.0 + fg * (1.0 - sg)))
        dfu = de * (fg * sg)
        n = ge.shape[0]

        def wgrad(df, xe):
            df_c = _center(df)
            return jnp.concatenate([jnp.sum(df_c * _center(pltpu.roll(xe, 1, 0)), axis=0, keepdims=True),
                                    jnp.sum(df_c * _center(xe), axis=0, keepdims=True),
                                    jnp.sum(df_c * _center(pltpu.roll(xe, n - 1, 0)), axis=0, keepdims=True)], axis=0)

        return (_center(_conv3_t(dfg, wg)), _center(_conv3_t(dfu, wu)), wgrad(dfg, ge), wgrad(dfu, ue),
                jnp.sum(_center(dfg), axis=0, keepdims=True), jnp.sum(_center(dfu), axis=0, keepdims=True))

    dfg_pre, dfu_pre, g_cwg, g_cwu, g_cbg, g_cbu = _rows(
        "ffn_act_bwd", ffn_bwd_fn, T, D_FF // fw, [(f_pre, fw, fcol), (f_pre, fw, ucol), (da2, fw, fcol)],
        [(cwg, fw, fcol), (cwu, fw, fcol), (cbg, fw, fcol), (cbu, fw, fcol)],
        [(D_FF, fw, fcol, BF16), (D_FF, fw, fcol, BF16)],
        [(3, D_FF, fw, fcol), (3, D_FF, fw, fcol), (1, D_FF, fw, fcol), (1, D_FF, fw, fcol)], halo=True)
    grads["w_ffn_conv"] = jnp.concatenate([g_cwg, g_cwu], axis=1)
    grads["b_ffn_conv"] = jnp.concatenate([g_cbg, g_cbu], axis=1)
    dn2 = _mm("d_norm_ffn_g", dfg_pre, w_up_g, "nn")
    dn2 = _mm("d_norm_ffn_u", dfu_pre, w_up_u, "nn", add=dn2)
    grads["w_ffn_up_t"] = jnp.concatenate([_mm("g_ffn_up_g", dfg_pre, n2, "tn", out_dtype=BF16), _mm("g_ffn_up_u", dfu_pre, n2, "tn", out_dtype=BF16)],
                                          axis=0)

    def norm_bwd(name, h, nw, dn, dres):
        def fn(i, hv, dnv, drv, wv):
            _, vjp = jax.vjp(_rms, hv, wv)
            dh, dw = vjp(dnv)
            return dh + drv, dw
        return _rows(name, fn, T, 1, [(h, D_MODEL, c0), (dn, D_MODEL, c0), (dres, D_MODEL, c0)], [(nw, D_MODEL, c0)],
                     [(D_MODEL, D_MODEL, c0, F32)], [(1, D_MODEL, D_MODEL, c0)])

    dh1, grads["norm_ffn_w"] = norm_bwd("norm_ffn_bwd", h1, w["norm_ffn_w"], dn2, dh2)

    dmerged = _mm("d_merged", dh1, w["w_out"], "nt", out_dtype=BF16)
    grads["w_out"] = _mm("g_out", merged, dh1, "tn", out_dtype=BF16)

    def merge_bwd_fn(i, gates, yr, ys, dm):
        valid = _row_ids(i, T) >= PAD_ROWS
        _, vjp = jax.vjp(lambda a, b, c: _merge(a, b, c, valid), gates, yr, ys)
        return vjp(dm)

    dgates, dyr, dys = _rows("merge_bwd", merge_bwd_fn, T, 1,
                             [(p_gates, 2048, c0), (y_ret_o, 1024, c0), (y_ssd_o, 1024, c0), (dmerged, 1024, c0)],
                             [], [(2048, 2048, c0, BF16), (1024, 1024, c0, BF16), (1024, 1024, c0, BF16)])
    dproj = {"gates": dgates}

    da_ssd = _mm("d_ssd_act", dys, w["w_ssd_out"], "nt", out_dtype=BF16)
    grads["w_ssd_out"] = _mm("g_ssd_out", a_ssd, dys, "tn", out_dtype=BF16)

    def ssd_post_bwd_fn(i, yf, yb, xv, zv, da, dk, nw):
        _, vjp = jax.vjp(_ssd_post, yf, yb, xv, zv, dk, nw)
        dyf, _, dxv, dzv, ddk, dnw = vjp(da)
        return dyf, dxv, dzv, ddk, dnw

    d_main = lax.empty(p_main.shape, BF16)

    def into_main(name, width, cf=c0):
        base = seg_at[name] // width
        return (d_main, width, lambda j: base + cf(j), BF16)

    dy_ssd, dxs_skip, d_main, g_dskip_e, grads["ssd_norm_w"] = _rows(
        "ssd_post_bwd", ssd_post_bwd_fn, T, SSD_GROUPS,
        [(y_dir[0], gw_, gcol), (y_dir[1], gw_, gcol), (act["xs"], gw_, gcol), seg("z", gw_, gcol),
         (da_ssd, gw_, gcol)],
        [(dskip_e, gw_, gcol), (w["ssd_norm_w"], gw_, gcol)],
        [(2048, gw_, gcol, BF16), (2048, gw_, gcol, BF16), into_main("z", gw_, gcol)],
        [(1, 2048, gw_, gcol), (1, 2048, gw_, gcol)], tall=True)
    grads["d_skip"] = g_dskip_e.reshape(SSD_HEADS, SSD_HEAD_DIM).sum(axis=1).reshape(1, SSD_HEADS)

    dxs_dir, db_dir, dc_dir, draw, g_bias, g_alog = _ssd_bwd(act["xs"], act["B"], act["C"], small, states, dy_ssd, T)
    grads["dt_bias_f"], grads["dt_bias_b"] = g_bias[0].reshape(1, SSD_HEADS), g_bias[1].reshape(1, SSD_HEADS)
    grads["a_log_f"], grads["a_log_b"] = g_alog[0].reshape(1, SSD_HEADS), g_alog[1].reshape(1, SSD_HEADS)
    d_dt = jnp.stack(draw).transpose(2, 0, 1, 3).reshape(T, 2 * SSD_HEADS)
    dproj["dt"] = jnp.pad(d_dt, ((0, 0), (0, CHUNK - 2 * SSD_HEADS))).astype(BF16)

    def make_conv_bwd(nsum):
        def fn(i, xe, *rest):
            ds, (cw, cb) = rest[:nsum], rest[nsum:]
            r = _row_ids(i, T, True, tall=True)
            dact = ds[0]
            for t in ds[1:]:
                dact = dact + t
            dact = jnp.where(r >= PAD_ROWS, dact, 0.0)
            pre = _conv3(xe, cw) + cb
            sg = jax.nn.sigmoid(pre)
            dpre = dact * (sg * (1.0 + pre * (1.0 - sg)))
            n = xe.shape[0]
            dpc = _center(dpre)
            dw = jnp.concatenate([jnp.sum(dpc * _center(pltpu.roll(xe, 1, 0)), axis=0, keepdims=True),
                                  jnp.sum(dpc * _center(xe), axis=0, keepdims=True),
                                  jnp.sum(dpc * _center(pltpu.roll(xe, n - 1, 0)), axis=0, keepdims=True)], axis=0)
            return _center(_conv3_t(dpre, cw)), dw, jnp.sum(dpc, axis=0, keepdims=True)
        return fn

    g_cw, g_cb = {}, {}
    cots = {"xs": [(dxs_dir[0], 512, gcol), (dxs_dir[1], 512, gcol), (dxs_skip, 512, gcol)],
            "B": [(db_dir[0], 512, gcol), (db_dir[1], 512, gcol)],
            "C": [(dc_dir[0], 512, gcol), (dc_dir[1], 512, gcol)]}
    for name in ("xs", "B", "C"):
        wd = conv_w[name].shape[1]
        d_main, g_cw[name], g_cb[name] = _rows(
            "ssd_conv_bwd_" + name, make_conv_bwd(len(cots[name])), T, wd // 512,
            [seg(name, 512, gcol)] + cots[name], [(conv_w[name], 512, gcol), (conv_b[name], 512, gcol)],
            [into_main(name, 512, gcol)], [(3, wd, 512, gcol), (1, wd, 512, gcol)], halo=True, tall=True)
    grads["w_ssd_conv"] = jnp.concatenate([g_cw["xs"], g_cw["B"], g_cw["C"]], axis=1)
    grads["b_ssd_conv"] = jnp.concatenate([g_cb["xs"], g_cb["B"], g_cb["C"]], axis=1)

    da_ret = _mm("d_ret_act", dyr, w["w_ret_out"], "nt", out_dtype=BF16)
    grads["w_ret_out"] = _mm("g_ret_out", a_ret, dyr, "tn", out_dtype=BF16)
    tick = early_grads({n: grads.pop(n) for n in ("w_ffn_up_t", "w_ret_out", "w_ssd_out", "w_out", "w_ffn_down")})

    def ret_post_bwd_fn(i, y, g, da, gw):
        _, vjp = jax.vjp(_ret_post, y, g, gw)
        return vjp(da)

    dy_ret, d_main, grads["ret_gn_w"] = _rows(
        "ret_post_bwd", ret_post_bwd_fn, T, 1, [(y_ret, 1024, c0), seg("g", 1024), (da_ret, 1024, c0)],
        [(w["ret_gn_w"] + tick, 1024, c0)], [(1024, 1024, c0, BF16), into_main("g", 1024)], [(1, 1024, 1024, c0)])
    d_main = _retention("retention_dv", kr, qr, dy_ret, T, RET_QK_DIM, RET_V_DIM, into=(d_main, seg_at["v"]))
    dqr = _retention("retention_dq", dy_ret, v_at, kr, T, RET_V_DIM, RET_QK_DIM)
    dkr = _retention("retention_dk", v_at, dy_ret, qr, T, RET_V_DIM, RET_QK_DIM)

    def rot_bwd_fn(i, dq, dk, csv, snv):
        parts = [_rot_t(dq[:, h * 128:(h + 1) * 128], csv, snv) for h in range(RET_HEADS)]
        parts += [_rot_t(dk[:, h * 128:(h + 1) * 128] * scale, csv, snv) for h in range(RET_HEADS)]
        return (jnp.concatenate(parts, axis=1),)

    d_main = _rows("rotary_bwd", rot_bwd_fn, T, 1, [(dqr, 512, c0), (dkr, 512, c0), (cs, 128, c0), (sn, 128, c0)],
                   [], [into_main("qk", 1024)], tall=True)[0]

    g_in = [_mm("g_in_main", d_main, u, "tn", out_dtype=BF16),
            _mm("g_in_dt", dproj["dt"], u, "tn", out_dtype=BF16)[:2 * SSD_HEADS],
            _mm("g_in_gates", dproj["gates"], u, "tn", out_dtype=BF16)]
    tick = in_grads(jnp.concatenate(g_in, axis=0))
    du = _mm("d_u_dt", dproj["dt"] + tick.astype(BF16), w_dt, "nn")
    du = _mm("d_u_main", d_main, w_main, "nn", add=du)
    du = _mm("d_u_gates", dproj["gates"], w_gates, "nn", add=du)
    dh0, grads["norm_mix_w"] = norm_bwd("norm_mix_bwd", h0, w["norm_mix_w"], du, dh1)
    grads["meta_tokens"] = dh0[PAD_ROWS:CHUNK]
    return loss, dh0[CHUNK:], grads


MESH_ID = pl.DeviceIdType.MESH
ANY = pl.BlockSpec(memory_space=pl.ANY)


def _me_and_peers():
    x, y, c = lax.axis_index("x"), lax.axis_index("y"), lax.axis_index("c")
    peers = []
    for k in range(1, N_DEV):
        px = 1 - x if k & 4 else x
        py = 1 - y if k & 2 else y
        pc = 1 - c if k & 1 else c
        peers.append(((px, py, pc), 4 * px + 2 * py + pc))
    return 4 * x + 2 * y + c, peers


def _push_blocks(name, src, per_peer):
    blk = src.shape[1:] if per_peer else src.shape

    def body(src_ref, out_ref, send_sems, recv_sems, local_sem):
        me, peers = _me_and_peers()
        mine = src_ref.at[me] if per_peer else src_ref
        local = pltpu.make_async_copy(mine, out_ref.at[me], local_sem)
        local.start()
        sends = []
        for k, (dev, idx) in enumerate(peers):
            cp = pltpu.make_async_remote_copy(
                src_ref=src_ref.at[idx] if per_peer else src_ref, dst_ref=out_ref.at[me],
                send_sem=send_sems.at[k], recv_sem=recv_sems.at[k], device_id=dev, device_id_type=MESH_ID)
            cp.start()
            sends.append(cp)
        for k, (dev, idx) in enumerate(peers):
            pltpu.make_async_remote_copy(
                src_ref=mine, dst_ref=out_ref.at[idx], send_sem=send_sems.at[k], recv_sem=recv_sems.at[k],
                device_id=dev, device_id_type=MESH_ID).wait_recv()
        for cp in sends:
            cp.wait_send()
        local.wait()

    return pl.pallas_call(
        body, name=name, in_specs=[ANY], out_specs=ANY,
        out_shape=jax.ShapeDtypeStruct((N_DEV,) + tuple(blk), src.dtype),
        scratch_shapes=[pltpu.SemaphoreType.DMA((N_DEV - 1,)), pltpu.SemaphoreType.DMA((N_DEV - 1,)),
                        pltpu.SemaphoreType.DMA],
    )(src)


def _gather_two_level(name, src):
    def body(x_ref, out_ref, send_sems, recv_sems, local_sem):
        x, y, c = lax.axis_index("x"), lax.axis_index("y"), lax.axis_index("c")
        me, sibling = (x, y, c), (x, y, 1 - c)
        chips = [(1 - x, y), (x, 1 - y), (1 - x, 1 - y)]

        def rows(px, py, pc):
            return out_ref.at[4 * px + 2 * py + pc]

        def copy(k, block, to, src_ref=None):
            return pltpu.make_async_remote_copy(
                src_ref=rows(*block) if src_ref is None else src_ref, dst_ref=rows(*block),
                send_sem=send_sems.at[k], recv_sem=recv_sems.at[k], device_id=to, device_id_type=MESH_ID)

        mine = pltpu.make_async_copy(x_ref, rows(*me), local_sem)
        mine.start()
        first = [copy(0, me, sibling, x_ref)] + [copy(1 + j, me, (*chip, c), x_ref) for j, chip in enumerate(chips)]
        for cp in first:
            cp.start()
        passed = [copy(4 + j, (*chip, c), sibling) for j, chip in enumerate(chips)]
        for j, chip in enumerate(chips):
            copy(1 + j, (*chip, c), me).wait_recv()
            passed[j].start()
        copy(0, sibling, me).wait_recv()
        for j, chip in enumerate(chips):
            copy(4 + j, (*chip, 1 - c), me).wait_recv()
        for cp in first + passed:
            cp.wait_send()
        mine.wait()

    return pl.pallas_call(
        body, name=name, in_specs=[ANY], out_specs=ANY,
        out_shape=jax.ShapeDtypeStruct((N_DEV,) + tuple(src.shape), src.dtype),
        scratch_shapes=[pltpu.SemaphoreType.DMA((N_DEV - 1,)), pltpu.SemaphoreType.DMA((N_DEV - 1,)),
                        pltpu.SemaphoreType.DMA],
    )(src)


HBM = pl.BlockSpec(memory_space=pltpu.HBM)
SEM = pl.BlockSpec(memory_space=pltpu.SEMAPHORE)
EFFECT = pltpu.SideEffectType.DATAFLOW_SIDE_EFFECTING


def _peer_copy(src_ref, land_ref, send_sems, recv_sems, per_peer, me, a, k, dev, idx, receiving):
    s = a * (N_DEV - 1) + k
    return pltpu.make_async_remote_copy(
        src_ref=src_ref.at[idx] if per_peer else src_ref, dst_ref=land_ref.at[idx if receiving else me],
        send_sem=send_sems.at[s], recv_sem=recv_sems.at[s], device_id=dev, device_id_type=MESH_ID)


def _push_start(name, srcs, per_peer):
    n = len(srcs)
    land_shapes = [(N_DEV,) + tuple(s.shape[1:] if per_peer else s.shape) for s in srcs]

    def body(*refs):
        src_refs, land_refs, send_sems, recv_sems, token = refs[:n], refs[n:2 * n], refs[2 * n], refs[2 * n + 1], refs[-1]
        me, peers = _me_and_peers()
        for a in range(n):
            for k, (dev, idx) in enumerate(peers):
                _peer_copy(src_refs[a], land_refs[a], send_sems, recv_sems, per_peer, me, a, k, dev, idx, False).start()
        token[...] = jnp.zeros_like(token)

    sems = pltpu.SemaphoreType.DMA((n * (N_DEV - 1),))
    res = pl.pallas_call(
        body, name=name,
        out_shape=(sems, sems, *[pltpu.HBM(s.shape, s.dtype) for s in srcs],
                   *[pltpu.HBM(ls, s.dtype) for ls, s in zip(land_shapes, srcs)], jax.ShapeDtypeStruct((8, 128), F32)),
        in_specs=(HBM,) * (2 * n), out_specs=(SEM, SEM) + (HBM,) * (2 * n) + (pl.BlockSpec(memory_space=pltpu.VMEM),),
        input_output_aliases={i: 2 + i for i in range(2 * n)},
        compiler_params=pltpu.CompilerParams(has_side_effects=EFFECT),
    )(*[pltpu.with_memory_space_constraint(s, pltpu.HBM) for s in srcs],
      *[pltpu.with_memory_space_constraint(lax.empty(ls, s.dtype), pltpu.HBM) for ls, s in zip(land_shapes, srcs)])
    return res[0], res[1], res[2:2 + n], res[2 + n:2 + 2 * n], res[-1]


def _push_wait(name, send_sems, recv_sems, srcs_thru, lands_thru, after, per_peer):
    n = len(srcs_thru)

    def body(*refs):
        src_refs, land_refs, send_sems, recv_sems = refs[:n], refs[n:2 * n], refs[2 * n], refs[2 * n + 1]
        me, peers = _me_and_peers()
        for a in range(n):
            for k, (dev, idx) in enumerate(peers):
                cp = _peer_copy(src_refs[a], land_refs[a], send_sems, recv_sems, per_peer, me, a, k, dev, idx, True)
                cp.wait_send()
                cp.wait_recv()

    both = list(srcs_thru) + list(lands_thru)
    res = pl.pallas_call(
        body, name=name, out_shape=tuple(pltpu.HBM(t.shape, t.dtype) for t in both),
        in_specs=(HBM,) * (2 * n) + (SEM, SEM, ANY), out_specs=(HBM,) * (2 * n),
        input_output_aliases={i: i for i in range(2 * n)},
        compiler_params=pltpu.CompilerParams(has_side_effects=EFFECT),
    )(*both, send_sems, recv_sems, after)
    return res[:n], res[n:]


def _sum_blocks(name, blocks):
    _, R, C = blocks.shape
    tc = next(t for t in (1024, 512, 256, 128) if C % t == 0 and (N_DEV * R * t * 2 <= 6 * 2 ** 20 or t == 128))

    def body(b_ref, o_ref):
        acc = b_ref[0].astype(F32)
        for k in range(1, N_DEV):
            acc = acc + b_ref[k].astype(F32)
        o_ref[...] = acc

    return pl.pallas_call(
        body, name=name, grid=(C // tc,), in_specs=[pl.BlockSpec((N_DEV, R, tc), lambda j: (0, 0, j))],
        out_specs=pl.BlockSpec((R, tc), lambda j: (0, j)), out_shape=jax.ShapeDtypeStruct((R, C), F32),
        compiler_params=_params(("arbitrary",)),
    )(blocks)


def _adamw(name, w, g, m, v):
    R, C = w.shape
    tr = R if R <= 512 else _pick(R, (256, 184, 176, 128, 8))
    spec = pl.BlockSpec((tr, C), lambda i: (i, 0))

    def body(w_ref, g_ref, m_ref, v_ref, d_ref, mo_ref, vo_ref):
        gv = g_ref[...]
        mn = ADAM_B1 * m_ref[...] + (1.0 - ADAM_B1) * gv
        vn = ADAM_B2 * v_ref[...] + (1.0 - ADAM_B2) * jnp.square(gv)
        m_hat = mn / (1.0 - ADAM_B1 ** ADAM_STEP)
        v_hat = vn / (1.0 - ADAM_B2 ** ADAM_STEP)
        d_ref[...] = -ADAM_LR * (m_hat / (jnp.sqrt(v_hat) + ADAM_EPS) + ADAM_WD * w_ref[...])
        mo_ref[...] = mn
        vo_ref[...] = vn

    return pl.pallas_call(
        body, name=name, grid=(R // tr,), in_specs=[spec] * 4, out_specs=[spec] * 3,
        out_shape=[jax.ShapeDtypeStruct((R, C), F32)] * 3, compiler_params=_params(("arbitrary",)),
    )(w, g, m, v)


WEIGHTS = ("meta_tokens", "norm_mix_w", "w_in", "ret_gn_w", "w_ret_out", "w_ssd_conv", "b_ssd_conv", "dt_bias_f",
           "dt_bias_b", "a_log_f", "a_log_b", "d_skip", "ssd_norm_w", "w_ssd_out", "w_out", "norm_ffn_w", "w_ffn_up",
           "w_ffn_conv", "b_ffn_conv", "w_ffn_down", "final_norm_w")
BIG = (("w_in", 1288, True), ("w_ffn_up", 704, True), ("w_ret_out", 128, False), ("w_ssd_out", 256, False),
       ("w_out", 128, False), ("w_ffn_down", 352, False))
REPLICATED = ("norm_mix_w", "ret_gn_w", "b_ssd_conv", "dt_bias_f", "dt_bias_b", "a_log_f", "a_log_b", "d_skip",
              "ssd_norm_w", "norm_ffn_w", "b_ffn_conv", "final_norm_w")
SMALL_SHARDED = (("meta_tokens", 16, 1024), ("w_ssd_conv", 3, 3072), ("w_ffn_conv", 3, 5632))


BIG_IN, BIG_REST = BIG[:1], BIG[1:]


def _pack_flat(arrays, rows):
    flat = jnp.concatenate([a.reshape(-1) for a in arrays])
    return jnp.pad(flat, (0, rows * D_MODEL - flat.shape[0])).reshape(rows, D_MODEL)


def _unpack_flat(slab, shapes):
    flat, out, o = slab.reshape(-1), [], 0
    for s in shapes:
        n = math.prod(s)
        out.append(flat[o:o + n].reshape(s))
        o += n
    return out


def kernel(x, meta_tokens, norm_mix_w, w_in, ret_gn_w, w_ret_out, w_ssd_conv, b_ssd_conv, dt_bias_f, dt_bias_b, a_log_f, a_log_b, d_skip, ssd_norm_w, w_ssd_out, w_out, norm_ffn_w, w_ffn_up, w_ffn_conv, b_ffn_conv, w_ffn_down, final_norm_w, loss_target, m_meta_tokens, m_norm_mix_w, m_w_in, m_ret_gn_w, m_w_ret_out, m_w_ssd_conv, m_b_ssd_conv, m_dt_bias_f, m_dt_bias_b, m_a_log_f, m_a_log_b, m_d_skip, m_ssd_norm_w, m_w_ssd_out, m_w_out, m_norm_ffn_w, m_w_ffn_up, m_w_ffn_conv, m_b_ffn_conv, m_w_ffn_down, m_final_norm_w, v_meta_tokens, v_norm_mix_w, v_w_in, v_ret_gn_w, v_w_ret_out, v_w_ssd_conv, v_b_ssd_conv, v_dt_bias_f, v_dt_bias_b, v_a_log_f, v_a_log_b, v_d_skip, v_ssd_norm_w, v_w_ssd_out, v_w_out, v_norm_ffn_w, v_w_ffn_up, v_w_ffn_conv, v_b_ffn_conv, v_w_ffn_down, v_final_norm_w):
    given = dict(locals())
    wt = {n: given[n] for n in WEIGHTS}
    mt = {n: given["m_" + n] for n in WEIGHTS}
    vt = {n: given["v_" + n] for n in WEIGHTS}
    me = 4 * lax.axis_index("x") + 2 * lax.axis_index("y") + lax.axis_index("c")

    small_names = [n for n, _, _ in SMALL_SHARDED]
    small_local = lambda tree: [tree[n].reshape(r, c // N_DEV) for n, r, c in SMALL_SHARDED]
    slab_view = lambda tree, name, transposed: tree[name][0].T if transposed else tree[name][0]
    all_in = _gather_two_level("gather_w_in", slab_view(wt, "w_in", True).astype(BF16))
    all_s = _push_blocks("gather_small", _pack_flat(small_local(wt), 8), False)
    rest_srcs = [slab_view(wt, name, t).astype(BF16) for name, _, t in BIG_REST]
    rest_srcs, all_in, all_s = lax.optimization_barrier((rest_srcs, all_in, all_s))
    rest_flight = _push_start("gather_rest_start", rest_srcs, False)
    all_s = all_s.reshape(N_DEV, -1)
    full = {"w_in_t": all_in.reshape(-1, D_MODEL)}

    def lands_with_own(flight, after, per_peer, name):
        srcs, lands = _push_wait(name, *flight[:4], after, per_peer)
        own = lambda s: lax.dynamic_slice_in_dim(s, me, 1, axis=0) if per_peer else s[None]
        return [lax.dynamic_update_slice_in_dim(land, own(s), me, axis=0) for s, land in zip(srcs, lands)]

    def late_weights(after):
        lands = lands_with_own(rest_flight, after, False, "gather_rest_wait")
        return {name + ("_t" if t else ""): land.reshape(N_DEV * r, D_MODEL) for (name, r, t), land in zip(BIG_REST, lands)}

    flights = {}

    def start_exchange(key, group, gd):
        srcs = [gd[name + ("_t" if t else "")].astype(BF16).reshape(N_DEV, r, D_MODEL) for name, r, t in group]
        flights[key] = _push_start("exchange_" + key + "_start", srcs, True)
        return flights[key][4][0, 0]

    o = 0
    for name, r, c in SMALL_SHARDED:
        n = r * c // N_DEV
        full[name] = all_s[:, o:o + n].reshape(N_DEV, r, c // N_DEV).transpose(1, 0, 2).reshape(r, c)
        o += n
    for name in REPLICATED:
        full[name] = wt[name]

    grads, delta, new_m, new_v = {}, {}, {}, {}

    def finish_exchange(key, group, after):
        lands = lands_with_own(flights[key], after, True, "exchange_" + key + "_wait")
        for (name, _, transposed), land in zip(group, lands):
            back = (lambda a: a.T[None]) if transposed else (lambda a: a[None])
            g_sum = _sum_blocks("sum_" + name, land)
            d, mn, vn = _adamw("adamw_" + name, slab_view(wt, name, transposed), g_sum,
                               slab_view(mt, name, transposed), slab_view(vt, name, transposed))
            grads[name], delta[name], new_m[name], new_v[name] = back(g_sum), back(d), back(mn), back(vn)

    def in_grads(gi):
        tick = start_exchange("in", BIG_IN, {"w_in_t": gi})
        finish_exchange("rest", BIG_REST, flights["in"][4])
        tick, _ = lax.optimization_barrier((tick, [delta[name] for name, _, _ in BIG_REST]))
        return tick

    loss, grad_x, g = _local_step(x[0], loss_target[0], full, rest_flight[4][0, 0], late_weights,
                                  lambda gd: start_exchange("rest", BIG_REST, gd), in_grads)

    small_parts = [g[n] for n in REPLICATED] + [g[n] for n in small_names] + [loss.reshape(1)]
    small_flight = _push_start("gather_small_grads_start", [_pack_flat(small_parts, 64)], False)
    finish_exchange("in", BIG_IN, small_flight[4])
    g_small = _sum_blocks("sum_small", lands_with_own(small_flight, delta["w_in"], False, "gather_small_grads_wait")[0])
    small_red = _unpack_flat(g_small, [wt[n].shape for n in REPLICATED] + [(r, c) for _, r, c in SMALL_SHARDED] + [(1,)])
    grads.update(zip(REPLICATED, small_red[:len(REPLICATED)]))
    for (name, r, c), red in zip(SMALL_SHARDED, small_red[len(REPLICATED):-1]):
        grads[name] = lax.dynamic_slice(red, (0, me * (c // N_DEV)), (r, c // N_DEV)).reshape(wt[name].shape)
    loss_all = small_red[-1][0]

    rest = list(REPLICATED) + small_names
    shapes = [wt[n].shape for n in rest]
    pack_rest = lambda tree: _pack_flat([tree[n] for n in rest], 24)
    d_rest, m_rest, v_rest = _adamw("adamw_small", pack_rest(wt), pack_rest(grads), pack_rest(mt), pack_rest(vt))
    delta.update(zip(rest, _unpack_flat(d_rest, shapes)))
    new_m.update(zip(rest, _unpack_flat(m_rest, shapes)))
    new_v.update(zip(rest, _unpack_flat(v_rest, shapes)))

    return (loss_all, grad_x[None], *[grads[n] for n in WEIGHTS], *[delta[n] for n in WEIGHTS],
            *[new_m[n] for n in WEIGHTS], *[new_v[n] for n in WEIGHTS])
```

```python
import functools
import math

import jax
import jax.numpy as jnp
from jax import lax
from jax.experimental import pallas as pl
from jax.experimental.pallas import tpu as pltpu

F32 = jnp.float32
BF16 = jnp.bfloat16

D_MODEL = 1024
CHUNK = 128
N_META = 16
PAD_ROWS = CHUNK - N_META
RET_HEADS = 4
RET_QK_DIM = 128
RET_V_DIM = 256
SSD_HEADS = 32
SSD_HEAD_DIM = 64
SSD_GROUPS = 4
SSD_STATE = 128
HEADS_PER_GROUP = SSD_HEADS // SSD_GROUPS
PAIRS_PER_GROUP = HEADS_PER_GROUP // 2
D_FF = 2816
EPS = 1e-6
ROPE_BASE = 10000.0
N_DEV = 8

ADAM_LR = 0.001
ADAM_B1 = 0.9
ADAM_B2 = 0.999
ADAM_EPS = 1e-08
ADAM_WD = 0.01
ADAM_STEP = 10

VMEM_LIMIT = 56 * 1024 * 1024
HALO = 16
HIGHEST = lax.Precision.HIGHEST

SEGMENTS = (("qk", 0, 1024), ("v", 1024, 2048), ("g", 2048, 3072), ("z", 3072, 5120), ("xs", 5120, 7168),
            ("B", 7168, 7680), ("C", 7680, 8192), ("dt", 8192, 8256), ("gates", 8256, 10304))


def _pick(n, cands):
    for c in cands:
        if n % c == 0:
            return c
    raise ValueError(f"no tile for {n}")


def _params(sem):
    return pltpu.CompilerParams(dimension_semantics=sem, vmem_limit_bytes=VMEM_LIMIT)


def _dot(a, b, dims=(((1,), (0,)), ((), ())), precision=None):
    return lax.dot_general(a, b, dims, preferred_element_type=F32, precision=precision)


def _dot_nt(a, b):
    return _dot(a, b, (((1,), (1,)), ((), ())))


def _dot_tn(a, b):
    return _dot(a, b, (((0,), (0,)), ((), ())))


def _mm(name, a, b, mode, add=None, out_dtype=F32):
    if mode == "nn":
        (M, K), N = a.shape, b.shape[1]
    elif mode == "nt":
        (M, K), N = a.shape, b.shape[0]
    else:
        (K, M), N = a.shape, b.shape[1]
    tn = _pick(N, (1408, 1024, 512, 128, 64))
    if mode == "tn":
        tm = M if M <= 1024 else _pick(M, (1408, 1024))
        tk = _pick(K, (2112, 512, 256, 128))
    else:
        tm = _pick(M, (1056, 512, 256, 128))
        tk = K if K <= 2816 else _pick(K, (2048, 1408, 1024))
    nk = K // tk
    if mode == "nn":
        a_spec = pl.BlockSpec((tm, tk), lambda n, m, k: (m, k))
        b_spec = pl.BlockSpec((tk, tn), lambda n, m, k: (k, n))
        dims = (((1,), (0,)), ((), ()))
    elif mode == "nt":
        a_spec = pl.BlockSpec((tm, tk), lambda n, m, k: (m, k))
        b_spec = pl.BlockSpec((tn, tk), lambda n, m, k: (n, k))
        dims = (((1,), (1,)), ((), ()))
    else:
        a_spec = pl.BlockSpec((tk, tm), lambda n, m, k: (k, m))
        b_spec = pl.BlockSpec((tk, tn), lambda n, m, k: (k, n))
        dims = (((0,), (0,)), ((), ()))
    o_spec = pl.BlockSpec((tm, tn), lambda n, m, k: (m, n))
    in_specs = [a_spec, b_spec] + ([o_spec] if add is not None else [])
    args = [a, b] + ([add] if add is not None else [])

    def body(*refs):
        if add is not None:
            a_ref, b_ref, r_ref, o_ref, acc = refs
        else:
            a_ref, b_ref, o_ref, acc = refs
        k = pl.program_id(2)
        p = _dot(a_ref[...].astype(BF16), b_ref[...].astype(BF16), dims)

        def finish(r):
            if add is not None:
                r = r + r_ref[...]
            o_ref[...] = r.astype(out_dtype)

        if nk == 1:
            finish(p)
        else:
            @pl.when(k == 0)
            def _():
                acc[...] = p

            @pl.when(k > 0)
            def _():
                acc[...] += p

            @pl.when(k == nk - 1)
            def _():
                finish(acc[...])

    return pl.pallas_call(
        body, name=name, grid=(N // tn, M // tm, nk), in_specs=in_specs, out_specs=o_spec,
        out_shape=jax.ShapeDtypeStruct((M, N), out_dtype),
        scratch_shapes=[pltpu.VMEM((tm, tn) if nk > 1 else (8, 128), F32)],
        compiler_params=_params(("arbitrary", "arbitrary", "arbitrary")),
    )(*args)


ANY_SPACE = pl.BlockSpec(memory_space=pl.ANY)


def _const(c):
    return lambda j: c


def _rows(name, fn, T, ncol, ins, params, outs, accs=(), halo=False, tall=False):
    tm = _pick(T, (1056, 512, 256, 128)) if tall else _pick(T, (384, 256, 128))
    R = T // tm
    hb = tm // HALO
    in_specs, args = [], []
    for spec in ins:
        arr, w, cf = spec[:3]
        lead = spec[3] if len(spec) > 3 else None
        if len(spec) > 4:
            rows, rf = spec[4]
            in_specs.append(pl.BlockSpec((rows, w), lambda j, i, cf=cf, rf=rf: (rf(i), cf(j))))
            args.append(arr)
            continue
        if lead is None:
            mk = lambda blk, rf, cf=cf: pl.BlockSpec(blk, lambda j, i: (rf(i), cf(j)))
            shape = lambda r, w=w: (r, w)
        else:
            mk = lambda blk, rf, cf=cf, lead=lead: pl.BlockSpec(blk, lambda j, i: (lead, rf(i), cf(j)))
            shape = lambda r, w=w: (None, r, w)
        in_specs.append(mk(shape(tm), lambda i: i))
        args.append(arr)
        if halo:
            in_specs.append(mk(shape(HALO), lambda i: jnp.maximum(i * hb - 1, 0)))
            in_specs.append(mk(shape(HALO), lambda i: jnp.minimum((i + 1) * hb, T // HALO - 1)))
            args += [arr, arr]
    for arr, w, cf in params:
        in_specs.append(pl.BlockSpec((arr.shape[0], w), lambda j, i, cf=cf: (0, cf(j))))
        args.append(arr)
    out_shape, out_specs, aliases = [], [], {}
    for k, (tw, w, cf, dt) in enumerate(outs):
        if not isinstance(tw, int):
            aliases[len(args)] = k
            in_specs.append(ANY_SPACE)
            args.append(tw)
            tw = tw.shape[1]
        out_shape.append(jax.ShapeDtypeStruct((T, tw), dt))
        out_specs.append(pl.BlockSpec((tm, w), lambda j, i, cf=cf: (i, cf(j))))
    for r, tw, w, cf in accs:
        out_shape.append(jax.ShapeDtypeStruct((r, tw), F32))
        out_specs.append(pl.BlockSpec((r, w), lambda j, i, cf=cf: (0, cf(j))))
    n_in, n_par, n_out, n_acc, n_alias = len(ins), len(params), len(outs), len(accs), len(aliases)

    def body(*refs):
        i = pl.program_id(1)
        vals, p = [], 0
        for _ in range(n_in):
            if halo:
                before = jnp.where(i > 0, refs[p + 1][...], jnp.zeros_like(refs[p + 1]))
                after = jnp.where(i < R - 1, refs[p + 2][...], jnp.zeros_like(refs[p + 2]))
                vals.append(jnp.concatenate([before, refs[p][...], after], axis=0).astype(F32))
                p += 3
            else:
                vals.append(refs[p][...].astype(F32))
                p += 1
        pvals = [refs[p + k][...] for k in range(n_par)]
        p += n_par + n_alias
        res = fn(i, *vals, *pvals)
        for k in range(n_out):
            refs[p + k][...] = res[k].astype(refs[p + k].dtype)
        p += n_out
        for k in range(n_acc):
            ref, v = refs[p + k], res[n_out + k]

            @pl.when(i == 0)
            def _(ref=ref, v=v):
                ref[...] = v

            @pl.when(i > 0)
            def _(ref=ref, v=v):
                ref[...] += v

    res = pl.pallas_call(
        body, name=name, grid=(ncol, R), in_specs=in_specs, out_specs=out_specs, out_shape=out_shape,
        input_output_aliases=aliases, compiler_params=_params(("arbitrary", "arbitrary")),
    )(*args)
    return res


def _tile_rows(T):
    return _pick(T, (384, 256, 128))


def _row_ids(i, T, halo=False, tall=False):
    tm = _pick(T, (1056, 512, 256, 128)) if tall else _tile_rows(T)
    if halo:
        return i * tm - HALO + lax.broadcasted_iota(jnp.int32, (tm + 2 * HALO, 1), 0)
    return i * tm + lax.broadcasted_iota(jnp.int32, (tm, 1), 0)


def _rms(x, w):
    return x * lax.rsqrt(jnp.mean(x * x, axis=-1, keepdims=True) + EPS) * w


def _silu(x):
    return x * jax.nn.sigmoid(x)


def _conv3(x, w):
    n = x.shape[0]
    return w[0:1] * pltpu.roll(x, 1, 0) + w[1:2] * x + w[2:3] * pltpu.roll(x, n - 1, 0)


def _conv3_t(d, w):
    n = d.shape[0]
    return w[0:1] * pltpu.roll(d, n - 1, 0) + w[1:2] * d + w[2:3] * pltpu.roll(d, 1, 0)


def _center(x):
    return x[HALO:x.shape[0] - HALO]


def _retention(name, a, b, v, T, da, dv, into=None):
    (a, a0), (b, b0), (v, v0) = [t if isinstance(t, tuple) else (t, 0) for t in (a, b, v)]
    nc = T // CHUNK
    log_gammas = [math.log(1.0 - 2.0 ** (-5.0 - h)) for h in range(RET_HEADS)]

    def body(*refs):
        a_ref, b_ref, v_ref = refs[:3]
        out_ref, o_ref, st, st_b = refs[-4:]
        h = pl.program_id(0)
        lg = jnp.float32(log_gammas[RET_HEADS - 1])
        for k in range(RET_HEADS - 2, -1, -1):
            lg = jnp.where(h == k, jnp.float32(log_gammas[k]), lg)
        li = lax.broadcasted_iota(jnp.int32, (CHUNK, CHUNK), 0)
        si = lax.broadcasted_iota(jnp.int32, (CHUNK, CHUNK), 1)
        dmat = jnp.exp(lg * jnp.abs(li - si).astype(F32))
        pos = lax.broadcasted_iota(jnp.int32, (CHUNK, 1), 0).astype(F32)
        kdec_f = jnp.exp((CHUNK - 1 - pos) * lg)
        qdec_f = jnp.exp((pos + 1) * lg)
        kdec_b = jnp.exp(pos * lg)
        qdec_b = jnp.exp((CHUNK - pos) * lg)
        cdec = jnp.exp(CHUNK * lg)

        def rows(n):
            return pl.ds(pl.multiple_of(n * CHUNK, CHUNK), CHUNK)

        st[...] = jnp.zeros_like(st)
        st_b[...] = jnp.zeros_like(st_b)
        o_ref[...] = jnp.zeros_like(o_ref)

        def step(m, carry):
            r = rows(m)
            av, bv, vv = a_ref[r, :], b_ref[r, :], v_ref[r, :].astype(BF16)
            s = _dot_nt(av.astype(BF16), bv.astype(BF16)) * dmat
            o_ref[r, :] += _dot(s.astype(BF16), vv) + _dot((av * qdec_f).astype(BF16), st[...].astype(BF16))
            st[...] = cdec * st[...] + _dot_tn((bv * kdec_f).astype(BF16), vv)
            r = rows(nc - 1 - m)
            av, bv, vv = a_ref[r, :], b_ref[r, :], v_ref[r, :].astype(BF16)
            o_ref[r, :] += _dot((av * qdec_b).astype(BF16), st_b[...].astype(BF16))
            st_b[...] = cdec * st_b[...] + _dot_tn((bv * kdec_b).astype(BF16), vv)
            return carry

        lax.fori_loop(0, nc, step, 0, unroll=True)
        out_ref[...] = o_ref[...].astype(out_ref.dtype)

    in_specs = [pl.BlockSpec((T, da), lambda h: (0, a0 // da + h)), pl.BlockSpec((T, da), lambda h: (0, b0 // da + h)),
                pl.BlockSpec((T, dv), lambda h: (0, v0 // dv + h))]
    if into is None:
        args, o0, aliases = (a, b, v), 0, {}
        out_shape = jax.ShapeDtypeStruct((T, RET_HEADS * dv), F32)
    else:
        args, o0, aliases = (a, b, v, into[0]), into[1], {3: 0}
        in_specs.append(ANY_SPACE)
        out_shape = jax.ShapeDtypeStruct(into[0].shape, into[0].dtype)
    return pl.pallas_call(
        body, name=name, grid=(RET_HEADS,), in_specs=in_specs,
        out_specs=pl.BlockSpec((T, dv), lambda h: (0, o0 // dv + h)), out_shape=out_shape,
        input_output_aliases=aliases,
        scratch_shapes=[pltpu.VMEM((T, dv), F32), pltpu.VMEM((da, dv), F32), pltpu.VMEM((da, dv), F32)],
        compiler_params=_params(("arbitrary",)),
    )(*args)


def _softplus(x):
    return jnp.maximum(x, 0.0) + jnp.log1p(jnp.exp(-jnp.abs(x)))


def _lane_lo():
    return lax.broadcasted_iota(jnp.int32, (1, CHUNK), 1) < SSD_HEAD_DIM


def _pair_cols(col, j):
    return jnp.where(_lane_lo(), col[:, 2 * j:2 * j + 1], col[:, 2 * j + 1:2 * j + 2])


def _pair_rows(colr, j):
    lo = lax.broadcasted_iota(jnp.int32, (CHUNK, 1), 0) < SSD_HEAD_DIM
    return jnp.where(lo, colr[2 * j:2 * j + 1, :], colr[2 * j + 1:2 * j + 2, :])


def _onehot8(h):
    return (lax.broadcasted_iota(jnp.int32, (1, HEADS_PER_GROUP), 1) == h).astype(F32)


def _ssd_pre(d, c, rawc, rawr, bc, br, alc, alr):
    li = lax.broadcasted_iota(jnp.int32, (CHUNK, CHUNK), 0)
    si = lax.broadcasted_iota(jnp.int32, (CHUNK, CHUNK), 1)
    dif = li - si if d == 0 else si - li
    mask = dif >= 0
    mask_t = dif <= 0
    rowc = c * CHUNK + lax.broadcasted_iota(jnp.int32, (CHUNK, 1), 0)
    rowr = c * CHUNK + lax.broadcasted_iota(jnp.int32, (1, CHUNK), 1)
    dtc = jnp.where(rowc >= PAD_ROWS, _softplus(rawc + bc), 0.0)
    dtr = jnp.where(rowr >= PAD_ROWS, _softplus(rawr + br), 0.0)
    ac = -jnp.exp(alc)
    ar = -jnp.exp(alr)
    dlc = dtc * ac
    dlr = dtr * ar
    alpc = _dot(mask.astype(F32), dlc, precision=HIGHEST)
    alpr = _dot(dlr, mask_t.astype(F32), precision=HIGHEST)
    endc = jnp.sum(dlc, axis=0, keepdims=True)
    endr = jnp.sum(dlr, axis=1, keepdims=True)
    return dict(mask=mask, mask_t=mask_t, dtc=dtc, ac=ac, alpc=alpc, alpr=alpr, endc=endc, endr=endr,
                valid=rowc >= PAD_ROWS)


def _chunk_of(d, n, nc):
    return n + d * (nc - 1 - 2 * n)


GROUP_WIDTH = HEADS_PER_GROUP * SSD_HEAD_DIM


def _chunks_per_step(nc, most=3):
    return next(c for c in (11, 3, 1) if c <= most and nc % c == 0)


def _ssd_in_specs(d, cfn, rows):
    return [
        pl.BlockSpec((rows, GROUP_WIDTH), lambda g, n: (cfn(d, n), g)),
        pl.BlockSpec((rows, SSD_STATE), lambda g, n: (cfn(d, n), g)),
        pl.BlockSpec((rows, SSD_STATE), lambda g, n: (cfn(d, n), g)),
        pl.BlockSpec((None, None, rows, HEADS_PER_GROUP), lambda g, n: (d, g, cfn(d, n), 0)),
        pl.BlockSpec((None, None, HEADS_PER_GROUP, rows), lambda g, n: (d, g, 0, cfn(d, n))),
        pl.BlockSpec((None, None, 1, HEADS_PER_GROUP), lambda g, n: (d, g, 0, 0)),
        pl.BlockSpec((None, None, HEADS_PER_GROUP, 1), lambda g, n: (d, g, 0, 0)),
        pl.BlockSpec((None, None, 1, HEADS_PER_GROUP), lambda g, n: (d, g, 0, 0)),
        pl.BlockSpec((None, None, HEADS_PER_GROUP, 1), lambda g, n: (d, g, 0, 0)),
    ]


N_SSD_IN = 9


def _ssd_fwd(xs, bm, cm, small, T):
    nc = T // CHUNK
    cps = _chunks_per_step(nc, 11)
    rows = cps * CHUNK
    cfn = lambda d, n: _chunk_of(d, n, nc // cps)

    def one_direction(d, n, ins, y_ref, hs_ref, h_scr):
        x_ref, b_ref, c_ref, rawc_ref, rawr_ref, *per_group = ins
        for kk in range(cps):
            k = kk if d == 0 else cps - 1 - kk
            r = pl.ds(k * CHUNK, CHUNK)
            one_chunk(d, cfn(d, n) * cps + k,
                      (x_ref.at[r], b_ref.at[r], c_ref.at[r], rawc_ref.at[r], rawr_ref.at[:, r], *per_group),
                      y_ref.at[r], hs_ref.at[k], h_scr)

    def one_chunk(d, c, ins, y_ref, hs_ref, h_scr):
        x_ref, b_ref, c_ref, rawc_ref, rawr_ref, bc_ref, br_ref, alc_ref, alr_ref = ins
        q = _ssd_pre(d, c, rawc_ref[...], rawr_ref[...], bc_ref[...], br_ref[...], alc_ref[...], alr_ref[...])
        bv = b_ref[...].astype(BF16)
        cv = c_ref[...].astype(BF16)
        cb = _dot_nt(cv, bv)
        lo = _lane_lo()
        for j in range(PAIRS_PER_GROUP):
            xp = x_ref[:, j * CHUNK:(j + 1) * CHUNK]
            xd = xp * _pair_cols(q["dtc"], j)
            xdb = xd.astype(BF16)
            yi = []
            for e in range(2):
                h = 2 * j + e
                lm = jnp.exp(jnp.where(q["mask"], q["alpc"][:, h:h + 1] - q["alpr"][h:h + 1, :], -jnp.inf))
                yi.append(_dot((cb * lm).astype(BF16), xdb))
            alp = _pair_cols(q["alpc"], j)
            hp = h_scr[j]
            hs_ref[j] = hp
            yo = jnp.exp(alp) * _dot_nt(cv, hp.astype(BF16))
            y_ref[:, j * CHUNK:(j + 1) * CHUNK] = (jnp.where(lo, yi[0], yi[1]) + yo).astype(y_ref.dtype)
            de = jnp.exp(_pair_cols(q["endc"], j) - alp)
            h_scr[j] = jnp.exp(_pair_rows(q["endr"], j)) * hp + _dot_tn((xd * de).astype(BF16), bv)

    def body(*refs):
        n = pl.program_id(1)
        ins, (y_f, y_b, hs_f, hs_b, h_scr) = refs[:2 * N_SSD_IN], refs[2 * N_SSD_IN:]

        @pl.when(n == 0)
        def _():
            h_scr[...] = jnp.zeros_like(h_scr)

        one_direction(0, n, ins[:N_SSD_IN], y_f, hs_f, h_scr.at[0])
        one_direction(1, n, ins[N_SSD_IN:], y_b, hs_b, h_scr.at[1])

    y_spec = lambda d: pl.BlockSpec((rows, GROUP_WIDTH), lambda g, n: (cfn(d, n), g))
    hs_spec = lambda d: pl.BlockSpec((None, cps, PAIRS_PER_GROUP, CHUNK, SSD_STATE),
                                     lambda g, n: (g, cfn(d, n), 0, 0, 0))
    y_shape = jax.ShapeDtypeStruct((T, SSD_HEADS * SSD_HEAD_DIM), BF16)
    hs_shape = jax.ShapeDtypeStruct((SSD_GROUPS, nc, PAIRS_PER_GROUP, CHUNK, SSD_STATE), F32)
    y_f, y_b, hs_f, hs_b = pl.pallas_call(
        body, name="ssd_fwd", grid=(SSD_GROUPS, nc // cps),
        in_specs=_ssd_in_specs(0, cfn, rows) + _ssd_in_specs(1, cfn, rows),
        out_specs=[y_spec(0), y_spec(1), hs_spec(0), hs_spec(1)],
        out_shape=[y_shape, y_shape, hs_shape, hs_shape],
        scratch_shapes=[pltpu.VMEM((2, PAIRS_PER_GROUP, CHUNK, SSD_STATE), F32)],
        compiler_params=_params(("arbitrary", "arbitrary")),
    )(xs, bm, cm, *small, xs, bm, cm, *small)
    return (y_f, y_b), (hs_f, hs_b)


def _ssd_bwd(xs, bm, cm, small, hs, dy, T):
    nc = T // CHUNK
    cps = _chunks_per_step(nc, 11)
    rows = cps * CHUNK
    cfn = lambda d, n: _chunk_of(1 - d, n, nc // cps)

    def one_direction(d, n, ins, outs, dh_scr):
        x_ref, b_ref, c_ref, rawc_ref, rawr_ref, bc_ref, br_ref, alc_ref, alr_ref, hs_ref, dy_ref = ins
        dx_ref, db_ref, dc_ref, draw_ref, dbias_ref, dalog_ref = outs
        for kk in range(cps):
            k = cps - 1 - kk if d == 0 else kk
            r = pl.ds(k * CHUNK, CHUNK)
            one_chunk(d, cfn(d, n) * cps + k, n if kk == 0 else None,
                      (x_ref.at[r], b_ref.at[r], c_ref.at[r], rawc_ref.at[r], rawr_ref.at[:, r], bc_ref, br_ref,
                       alc_ref, alr_ref, hs_ref.at[k], dy_ref.at[r]),
                      (dx_ref.at[r], db_ref.at[r], dc_ref.at[r], draw_ref.at[r], dbias_ref, dalog_ref), dh_scr)

    def one_chunk(d, c, first_of_step, ins, outs, dh_scr):
        x_ref, b_ref, c_ref, rawc_ref, rawr_ref, bc_ref, br_ref, alc_ref, alr_ref, hs_ref, dy_ref = ins
        dx_ref, db_ref, dc_ref, draw_ref, dbias_ref, dalog_ref = outs
        rawc, bc = rawc_ref[...], bc_ref[...]
        q = _ssd_pre(d, c, rawc, rawr_ref[...], bc, br_ref[...], alc_ref[...], alr_ref[...])
        b32, c32 = b_ref[...], c_ref[...]
        bv, cv = b32.astype(BF16), c32.astype(BF16)
        cb = _dot_nt(cv, bv)
        cbt = _dot_nt(bv, cv)
        lo = _lane_lo()
        row_lo = lax.broadcasted_iota(jnp.int32, (CHUNK, 1), 0) < SSD_HEAD_DIM
        dcb = jnp.zeros((CHUNK, CHUNK), F32)
        dcp = jnp.zeros((CHUNK, SSD_STATE), F32)
        dbp = jnp.zeros((CHUNK, SSD_STATE), F32)
        dalp = jnp.zeros((CHUNK, HEADS_PER_GROUP), F32)
        dend = jnp.zeros((1, HEADS_PER_GROUP), F32)
        ddtx = jnp.zeros((CHUNK, HEADS_PER_GROUP), F32)

        def half_sums(t):
            return (jnp.sum(jnp.where(lo, t, 0.0), axis=1, keepdims=True),
                    jnp.sum(jnp.where(lo, 0.0, t), axis=1, keepdims=True))

        for j in range(PAIRS_PER_GROUP):
            xp = x_ref[:, j * CHUNK:(j + 1) * CHUNK]
            dtp = _pair_cols(q["dtc"], j)
            xd = xp * dtp
            xdb = xd.astype(BF16)
            dyp = dy_ref[:, j * CHUNK:(j + 1) * CHUNK]
            dyb = dyp.astype(BF16)
            hn = hs_ref[j]
            hnb = hn.astype(BF16)
            dh1 = dh_scr[j]
            dh1b = dh1.astype(BF16)
            alp = _pair_cols(q["alpc"], j)
            ea = jnp.exp(alp)
            de = jnp.exp(_pair_cols(q["endc"], j) - alp)
            dxi = []
            for e in range(2):
                h = 2 * j + e
                diff = q["alpc"][:, h:h + 1] - q["alpr"][h:h + 1, :]
                lm = jnp.exp(jnp.where(q["mask"], diff, -jnp.inf))
                mt = cbt * jnp.exp(jnp.where(q["mask_t"], -diff, -jnp.inf))
                dxi.append(_dot(mt.astype(BF16), dyb))
                dyeb_h = (jnp.where(lo, dyp, 0.0) if e == 0 else jnp.where(lo, 0.0, dyp)).astype(BF16)
                gl = _dot_nt(dyeb_h, xdb) * lm
                dcb = dcb + gl
                ra = jnp.sum(gl * cb - _dot_nt(xdb, dyeb_h) * mt, axis=1, keepdims=True)
                dalp = dalp + ra * _onehot8(h)
            y_off = ea * _dot_nt(cv, hnb)
            dxs_state = de * _dot_nt(bv, dh1b)
            dxd = jnp.where(lo, dxi[0], dxi[1]) + dxs_state
            dyeb = (dyp * ea).astype(BF16)
            dcp = dcp + _dot(dyeb, hnb)
            dbp = dbp + _dot((xd * de).astype(BF16), dh1b)
            dh_scr[j] = jnp.exp(_pair_rows(q["endr"], j)) * dh1 + _dot_tn(dyeb, cv)
            r0, r1 = half_sums(dyp * y_off - xd * dxs_state)
            dalp = dalp + r0 * _onehot8(2 * j) + r1 * _onehot8(2 * j + 1)
            t0, t1 = half_sums(jnp.sum(xd * dxs_state, axis=0, keepdims=True))
            u = dh1 * hn
            u0 = jnp.sum(jnp.sum(jnp.where(row_lo, u, 0.0), axis=0, keepdims=True), axis=1, keepdims=True)
            u1 = jnp.sum(jnp.sum(jnp.where(row_lo, 0.0, u), axis=0, keepdims=True), axis=1, keepdims=True)
            eend = jnp.exp(q["endc"])
            dend = dend + (t0 + eend * u0) * _onehot8(2 * j) + (t1 + eend * u1) * _onehot8(2 * j + 1)
            dx_ref[:, j * CHUNK:(j + 1) * CHUNK] = (dxd * dtp).astype(dx_ref.dtype)
            w0, w1 = half_sums(dxd * xp)
            ddtx = ddtx + w0 * _onehot8(2 * j) + w1 * _onehot8(2 * j + 1)

        dcbb = dcb.astype(BF16)
        dc_ref[...] = (dcp + _dot(dcbb, bv)).astype(dc_ref.dtype)
        db_ref[...] = (dbp + _dot_tn(dcbb, cv)).astype(db_ref.dtype)
        ddl = _dot(q["mask_t"].astype(F32), dalp, precision=HIGHEST) + dend
        ddt = ddl * q["ac"] + ddtx
        draw = jnp.where(q["valid"], ddt * jax.nn.sigmoid(rawc + bc), 0.0)
        draw_ref[...] = draw
        dbias = jnp.sum(draw, axis=0, keepdims=True)
        dalog = jnp.sum(ddl * q["dtc"], axis=0, keepdims=True) * q["ac"]

        def add():
            dbias_ref[...] += dbias
            dalog_ref[...] += dalog

        if first_of_step is None:
            add()
        else:
            @pl.when(first_of_step == 0)
            def _():
                dbias_ref[...] = dbias
                dalog_ref[...] = dalog

            pl.when(first_of_step > 0)(add)

    n_in, n_out = N_SSD_IN + 2, 6

    def body(*refs):
        n = pl.program_id(1)
        ins, outs, dh_scr = refs[:2 * n_in], refs[2 * n_in:2 * (n_in + n_out)], refs[-1]

        @pl.when(n == 0)
        def _():
            dh_scr[...] = jnp.zeros_like(dh_scr)

        one_direction(0, n, ins[:n_in], outs[:n_out], dh_scr.at[0])
        one_direction(1, n, ins[n_in:], outs[n_out:], dh_scr.at[1])

    def in_specs(d):
        return _ssd_in_specs(d, cfn, rows) + [
            pl.BlockSpec((None, cps, PAIRS_PER_GROUP, CHUNK, SSD_STATE), lambda g, n: (g, cfn(d, n), 0, 0, 0)),
            pl.BlockSpec((rows, GROUP_WIDTH), lambda g, n: (cfn(d, n), g))]

    def out_specs(d):
        acc = pl.BlockSpec((None, 1, HEADS_PER_GROUP), lambda g, n: (g, 0, 0))
        return [pl.BlockSpec((rows, GROUP_WIDTH), lambda g, n: (cfn(d, n), g)),
                pl.BlockSpec((rows, SSD_STATE), lambda g, n: (cfn(d, n), g)),
                pl.BlockSpec((rows, SSD_STATE), lambda g, n: (cfn(d, n), g)),
                pl.BlockSpec((None, rows, HEADS_PER_GROUP), lambda g, n: (g, cfn(d, n), 0)), acc, acc]

    out_shape = [jax.ShapeDtypeStruct((T, SSD_HEADS * SSD_HEAD_DIM), BF16),
                 jax.ShapeDtypeStruct((T, SSD_GROUPS * SSD_STATE), BF16),
                 jax.ShapeDtypeStruct((T, SSD_GROUPS * SSD_STATE), BF16),
                 jax.ShapeDtypeStruct((SSD_GROUPS, T, HEADS_PER_GROUP), F32),
                 jax.ShapeDtypeStruct((SSD_GROUPS, 1, HEADS_PER_GROUP), F32),
                 jax.ShapeDtypeStruct((SSD_GROUPS, 1, HEADS_PER_GROUP), F32)]
    res = pl.pallas_call(
        body, name="ssd_bwd", grid=(SSD_GROUPS, nc // cps),
        in_specs=in_specs(0) + in_specs(1), out_specs=out_specs(0) + out_specs(1), out_shape=out_shape * 2,
        scratch_shapes=[pltpu.VMEM((2, PAIRS_PER_GROUP, CHUNK, SSD_STATE), F32)],
        compiler_params=_params(("arbitrary", "arbitrary")),
    )(xs, bm, cm, *small, hs[0], dy, xs, bm, cm, *small, hs[1], dy)
    return [(res[k], res[n_out + k]) for k in range(n_out)]


def _rot(x, cs, sn):
    return x * cs + pltpu.roll(x, RET_QK_DIM // 2, 1) * sn


def _rot_t(d, cs, sn):
    return d * cs + pltpu.roll(d * sn, RET_QK_DIM // 2, 1)


def _ret_post(y, g, w):
    parts = []
    for h in range(RET_HEADS):
        yh = y[:, h * RET_V_DIM:(h + 1) * RET_V_DIM]
        mu = jnp.mean(yh, axis=-1, keepdims=True)
        var = jnp.mean(jnp.square(yh - mu), axis=-1, keepdims=True)
        parts.append((yh - mu) * lax.rsqrt(var + EPS))
    return _silu(g) * (jnp.concatenate(parts, axis=1) * w)


def _ssd_post(yf, yb, xs, z, dskip, w):
    y = (yf + yb + xs * dskip) * _silu(z)
    return y * lax.rsqrt(jnp.mean(y * y, axis=-1, keepdims=True) + EPS) * w


def _merge(gates, yr, ys, valid):
    m = jax.nn.sigmoid(gates[:, :D_MODEL]) * yr + jax.nn.sigmoid(gates[:, D_MODEL:]) * ys
    return jnp.where(valid, m, 0.0)


def _rope_tables(T):
    half = RET_QK_DIM // 2
    inv = ROPE_BASE ** (-jnp.arange(half, dtype=F32) / half)
    pos = (jnp.arange(T) - PAD_ROWS).astype(F32)
    ang = pos[:, None] * inv[None, :]
    cos, sin = jnp.cos(ang), jnp.sin(ang)
    return jnp.concatenate([cos, cos], axis=1), jnp.concatenate([-sin, sin], axis=1)


def _per_group(v):
    c = v.reshape(SSD_GROUPS, 1, HEADS_PER_GROUP)
    return c, c.reshape(SSD_GROUPS, HEADS_PER_GROUP, 1)


def _local_step(x, target, w, tick, late_weights, early_grads, in_grads):
    S = x.shape[0]
    T = S + CHUNK
    tm = _tile_rows(T)
    c0 = _const(0)

    h0 = jnp.concatenate([jnp.zeros((PAD_ROWS, D_MODEL), F32), w["meta_tokens"], x], axis=0)
    seg_at = {name: a for name, a, _ in SEGMENTS}
    w_main = w["w_in_t"][:seg_at["dt"]]
    w_dt = jnp.pad(w["w_in_t"][seg_at["dt"]:seg_at["gates"]], ((0, CHUNK - 2 * SSD_HEADS), (0, 0)))
    w_gates = w["w_in_t"][seg_at["gates"]:]

    def norm_cast(name, h, nw):
        return _rows(name, lambda i, hv, wv: (_rms(hv, wv),), T, 1, [(h, D_MODEL, c0)], [(nw, D_MODEL, c0)],
                     [(D_MODEL, D_MODEL, c0, BF16)], tall=True)[0]

    u = norm_cast("norm_mix", h0, w["norm_mix_w"] + tick)
    p_main = _mm("proj_main", u, w_main, "nt", out_dtype=BF16)
    p_dt = _mm("proj_dt", u, w_dt, "nt")
    p_gates = _mm("proj_gates", u, w_gates, "nt", out_dtype=BF16)

    def seg(name, width, cf=c0):
        base = seg_at[name] // width
        return (p_main, width, lambda j: base + cf(j))

    cs, sn = _rope_tables(T)
    scale = RET_QK_DIM ** -0.5

    def rot_fn(i, qk, csv, snv):
        q = [_rot(qk[:, h * 128:(h + 1) * 128], csv, snv) for h in range(RET_HEADS)]
        k = [_rot(qk[:, (RET_HEADS + h) * 128:(RET_HEADS + h + 1) * 128], csv, snv) * scale for h in range(RET_HEADS)]
        return jnp.concatenate(q, axis=1), jnp.concatenate(k, axis=1)

    qr, kr = _rows("rotary", rot_fn, T, 1, [seg("qk", 1024), (cs, 128, c0), (sn, 128, c0)], [],
                   [(512, 512, c0, F32), (512, 512, c0, F32)], tall=True)
    v_at = (p_main, seg_at["v"])
    y_ret = _retention("retention", qr, kr, v_at, T, RET_QK_DIM, RET_V_DIM)
    a_ret = _rows("ret_post", lambda i, y, g, gw: (_ret_post(y, g, gw),), T, 1,
                  [(y_ret, 1024, c0), seg("g", 1024)], [(w["ret_gn_w"], 1024, c0)],
                  [(1024, 1024, c0, BF16)], tall=True)[0]

    conv_w = {"xs": w["w_ssd_conv"][:, :2048], "B": w["w_ssd_conv"][:, 2048:2560], "C": w["w_ssd_conv"][:, 2560:]}
    conv_b = {"xs": w["b_ssd_conv"][:, :2048], "B": w["b_ssd_conv"][:, 2048:2560], "C": w["b_ssd_conv"][:, 2560:]}

    def ssd_conv_fn(i, xe, cw, cb):
        r = _row_ids(i, T, True, tall=True)
        return (_center(jnp.where(r >= PAD_ROWS, _silu(_conv3(xe, cw) + cb), 0.0)),)

    act = {}
    for name in ("xs", "B", "C"):
        wd = conv_w[name].shape[1]
        cw = 512
        act[name] = _rows("ssd_conv_" + name, ssd_conv_fn, T, wd // cw, [seg(name, cw, lambda j: j)],
                          [(conv_w[name], cw, lambda j: j), (conv_b[name], cw, lambda j: j)],
                          [(wd, cw, lambda j: j, BF16)], halo=True, tall=True)[0]

    raw = p_dt[:, :2 * SSD_HEADS].reshape(T, 2, SSD_GROUPS, HEADS_PER_GROUP)
    rawc = raw.transpose(1, 2, 0, 3)
    rawr = raw.transpose(1, 2, 3, 0)
    bias = [_per_group(w["dt_bias_f"]), _per_group(w["dt_bias_b"])]
    alog = [_per_group(w["a_log_f"]), _per_group(w["a_log_b"])]
    small = (rawc, rawr, jnp.stack([bias[0][0], bias[1][0]]), jnp.stack([bias[0][1], bias[1][1]]),
             jnp.stack([alog[0][0], alog[1][0]]), jnp.stack([alog[0][1], alog[1][1]]))
    y_dir, states = _ssd_fwd(act["xs"], act["B"], act["C"], small, T)

    dskip_e = jnp.repeat(w["d_skip"], SSD_HEAD_DIM, axis=1)
    gcol = lambda j: j
    gw_ = 512
    a_ssd = _rows("ssd_post", lambda i, yf, yb, xv, zv, dk, nw: (_ssd_post(yf, yb, xv, zv, dk, nw),), T, SSD_GROUPS,
                  [(y_dir[0], gw_, gcol), (y_dir[1], gw_, gcol), (act["xs"], gw_, gcol), seg("z", gw_, gcol)],
                  [(dskip_e, gw_, gcol), (w["ssd_norm_w"], gw_, gcol)], [(2048, gw_, gcol, BF16)], tall=True)[0]

    w = dict(w, **late_weights(a_ssd))
    w_up_g, w_up_u = w["w_ffn_up_t"][:D_FF], w["w_ffn_up_t"][D_FF:]
    y_ret_o = _mm("ret_out", a_ret, w["w_ret_out"], "nn", out_dtype=BF16)
    y_ssd_o = _mm("ssd_out", a_ssd, w["w_ssd_out"], "nn", out_dtype=BF16)

    def merge_fn(i, gates, yr, ys):
        return (_merge(gates, yr, ys, _row_ids(i, T) >= PAD_ROWS),)

    merged = _rows("merge", merge_fn, T, 1, [(p_gates, 2048, c0), (y_ret_o, 1024, c0), (y_ssd_o, 1024, c0)], [],
                   [(1024, 1024, c0, BF16)])[0]
    h1 = _mm("mix_out", merged, w["w_out"], "nn", add=h0)

    n2 = norm_cast("norm_ffn", h1, w["norm_ffn_w"])
    f_pre = _mm("ffn_up", n2, w["w_ffn_up_t"], "nt", out_dtype=BF16)
    cwg, cwu = w["w_ffn_conv"][:, :D_FF], w["w_ffn_conv"][:, D_FF:]
    cbg, cbu = w["b_ffn_conv"][:, :D_FF], w["b_ffn_conv"][:, D_FF:]
    fcol = lambda j: j
    fw = 1408

    def ffn_act_fn(i, ge, ue, wg, wu, bg, bu):
        return (_center(_silu(_conv3(ge, wg) + bg) * (_conv3(ue, wu) + bu)),)

    ucol = lambda j: D_FF // fw + j
    a2 = _rows("ffn_act", ffn_act_fn, T, D_FF // fw, [(f_pre, fw, fcol), (f_pre, fw, ucol)],
               [(cwg, fw, fcol), (cwu, fw, fcol), (cbg, fw, fcol), (cbu, fw, fcol)], [(D_FF, fw, fcol, BF16)],
               halo=True)[0]
    h2 = _mm("ffn_down", a2, w["w_ffn_down"], "nn", add=h1)

    fnw = w["final_norm_w"].reshape(1, D_MODEL)

    per_tile = tm // CHUNK
    tgt_specs = [(target, D_MODEL, c0, None, (CHUNK, lambda i, k=k: jnp.maximum(per_tile * i - 1 + k, 0)))
                 for k in range(per_tile)]

    def loss_fn(i, hv, *rest):
        tv, nw = jnp.concatenate(rest[:per_tile], axis=0), rest[per_tile]
        valid = _row_ids(i, T) >= CHUNK
        y, vjp = jax.vjp(_rms, hv, nw)
        diff = jnp.where(valid, y - tv, 0.0)
        dh, dw = vjp(diff * (1.0 / D_MODEL))
        part = 0.5 / D_MODEL * jnp.sum(jnp.sum(diff * diff, axis=1, keepdims=True), axis=0, keepdims=True)
        return dh, jnp.broadcast_to(part, (1, 128)), dw

    dh2, loss_acc, d_fnw = _rows("loss", loss_fn, T, 1, [(h2, D_MODEL, c0)] + tgt_specs, [(fnw, D_MODEL, c0)],
                                 [(D_MODEL, D_MODEL, c0, F32)], [(1, 128, 128, c0), (1, D_MODEL, D_MODEL, c0)])
    loss = loss_acc[0, 0]
    grads = {"final_norm_w": d_fnw.reshape(D_MODEL)}

    da2 = _mm("d_ffn_act", dh2, w["w_ffn_down"], "nt", out_dtype=BF16)
    grads["w_ffn_down"] = _mm("g_ffn_down", a2, dh2, "tn", out_dtype=BF16)

    def ffn_bwd_fn(i, ge, ue, de, wg, wu, bg, bu):
        fg = _conv3(ge, wg) + bg
        fu = _conv3(ue, wu) + bu
        sg = jax.nn.sigmoid(fg)
        dfg = de * fu * (sg * (1.0 + fg * (1.0 - sg)))
        dfu = de * (fg * sg)
        n = ge.shape[0]

        def wgrad(df, xe):
            df_c = _center(df)
            return jnp.concatenate([jnp.sum(df_c * _center(pltpu.roll(xe, 1, 0)), axis=0, keepdims=True),
                                    jnp.sum(df_c * _center(xe), axis=0, keepdims=True),
                                    jnp.sum(df_c * _center(pltpu.roll(xe, n - 1, 0)), axis=0, keepdims=True)], axis=0)

        return (_center(_conv3_t(dfg, wg)), _center(_conv3_t(dfu, wu)), wgrad(dfg, ge), wgrad(dfu, ue),
                jnp.sum(_center(dfg), axis=0, keepdims=True), jnp.sum(_center(dfu), axis=0, keepdims=True))

    dfg_pre, dfu_pre, g_cwg, g_cwu, g_cbg, g_cbu = _rows(
        "ffn_act_bwd", ffn_bwd_fn, T, D_FF // fw, [(f_pre, fw, fcol), (f_pre, fw, ucol), (da2, fw, fcol)],
        [(cwg, fw, fcol), (cwu, fw, fcol), (cbg, fw, fcol), (cbu, fw, fcol)],
        [(D_FF, fw, fcol, BF16), (D_FF, fw, fcol, BF16)],
        [(3, D_FF, fw, fcol), (3, D_FF, fw, fcol), (1, D_FF, fw, fcol), (1, D_FF, fw, fcol)], halo=True)
    grads["w_ffn_conv"] = jnp.concatenate([g_cwg, g_cwu], axis=1)
    grads["b_ffn_conv"] = jnp.concatenate([g_cbg, g_cbu], axis=1)
    dn2 = _mm("d_norm_ffn_g", dfg_pre, w_up_g, "nn")
    dn2 = _mm("d_norm_ffn_u", dfu_pre, w_up_u, "nn", add=dn2)
    grads["w_ffn_up_t"] = jnp.concatenate([_mm("g_ffn_up_g", dfg_pre, n2, "tn", out_dtype=BF16), _mm("g_ffn_up_u", dfu_pre, n2, "tn", out_dtype=BF16)],
                                          axis=0)

    def norm_bwd(name, h, nw, dn, dres):
        def fn(i, hv, dnv, drv, wv):
            _, vjp = jax.vjp(_rms, hv, wv)
            dh, dw = vjp(dnv)
            return dh + drv, dw
        return _rows(name, fn, T, 1, [(h, D_MODEL, c0), (dn, D_MODEL, c0), (dres, D_MODEL, c0)], [(nw, D_MODEL, c0)],
                     [(D_MODEL, D_MODEL, c0, F32)], [(1, D_MODEL, D_MODEL, c0)])

    dh1, grads["norm_ffn_w"] = norm_bwd("norm_ffn_bwd", h1, w["norm_ffn_w"], dn2, dh2)

    dmerged = _mm("d_merged", dh1, w["w_out"], "nt", out_dtype=BF16)
    grads["w_out"] = _mm("g_out", merged, dh1, "tn", out_dtype=BF16)

    def merge_bwd_fn(i, gates, yr, ys, dm):
        valid = _row_ids(i, T) >= PAD_ROWS
        _, vjp = jax.vjp(lambda a, b, c: _merge(a, b, c, valid), gates, yr, ys)
        return vjp(dm)

    dgates, dyr, dys = _rows("merge_bwd", merge_bwd_fn, T, 1,
                             [(p_gates, 2048, c0), (y_ret_o, 1024, c0), (y_ssd_o, 1024, c0), (dmerged, 1024, c0)],
                             [], [(2048, 2048, c0, BF16), (1024, 1024, c0, BF16), (1024, 1024, c0, BF16)])
    dproj = {"gates": dgates}

    da_ssd = _mm("d_ssd_act", dys, w["w_ssd_out"], "nt", out_dtype=BF16)
    grads["w_ssd_out"] = _mm("g_ssd_out", a_ssd, dys, "tn", out_dtype=BF16)

    def ssd_post_bwd_fn(i, yf, yb, xv, zv, da, dk, nw):
        _, vjp = jax.vjp(_ssd_post, yf, yb, xv, zv, dk, nw)
        dyf, _, dxv, dzv, ddk, dnw = vjp(da)
        return dyf, dxv, dzv, ddk, dnw

    d_main = lax.empty(p_main.shape, BF16)

    def into_main(name, width, cf=c0):
        base = seg_at[name] // width
        return (d_main, width, lambda j: base + cf(j), BF16)

    dy_ssd, dxs_skip, d_main, g_dskip_e, grads["ssd_norm_w"] = _rows(
        "ssd_post_bwd", ssd_post_bwd_fn, T, SSD_GROUPS,
        [(y_dir[0], gw_, gcol), (y_dir[1], gw_, gcol), (act["xs"], gw_, gcol), seg("z", gw_, gcol),
         (da_ssd, gw_, gcol)],
        [(dskip_e, gw_, gcol), (w["ssd_norm_w"], gw_, gcol)],
        [(2048, gw_, gcol, BF16), (2048, gw_, gcol, BF16), into_main("z", gw_, gcol)],
        [(1, 2048, gw_, gcol), (1, 2048, gw_, gcol)], tall=True)
    grads["d_skip"] = g_dskip_e.reshape(SSD_HEADS, SSD_HEAD_DIM).sum(axis=1).reshape(1, SSD_HEADS)

    dxs_dir, db_dir, dc_dir, draw, g_bias, g_alog = _ssd_bwd(act["xs"], act["B"], act["C"], small, states, dy_ssd, T)
    grads["dt_bias_f"], grads["dt_bias_b"] = g_bias[0].reshape(1, SSD_HEADS), g_bias[1].reshape(1, SSD_HEADS)
    grads["a_log_f"], grads["a_log_b"] = g_alog[0].reshape(1, SSD_HEADS), g_alog[1].reshape(1, SSD_HEADS)
    d_dt = jnp.stack(draw).transpose(2, 0, 1, 3).reshape(T, 2 * SSD_HEADS)
    dproj["dt"] = jnp.pad(d_dt, ((0, 0), (0, CHUNK - 2 * SSD_HEADS))).astype(BF16)

    def make_conv_bwd(nsum):
        def fn(i, xe, *rest):
            ds, (cw, cb) = rest[:nsum], rest[nsum:]
            r = _row_ids(i, T, True, tall=True)
            dact = ds[0]
            for t in ds[1:]:
                dact = dact + t
            dact = jnp.where(r >= PAD_ROWS, dact, 0.0)
            pre = _conv3(xe, cw) + cb
            sg = jax.nn.sigmoid(pre)
            dpre = dact * (sg * (1.0 + pre * (1.0 - sg)))
            n = xe.shape[0]
            dpc = _center(dpre)
            dw = jnp.concatenate([jnp.sum(dpc * _center(pltpu.roll(xe, 1, 0)), axis=0, keepdims=True),
                                  jnp.sum(dpc * _center(xe), axis=0, keepdims=True),
                                  jnp.sum(dpc * _center(pltpu.roll(xe, n - 1, 0)), axis=0, keepdims=True)], axis=0)
            return _center(_conv3_t(dpre, cw)), dw, jnp.sum(dpc, axis=0, keepdims=True)
        return fn

    g_cw, g_cb = {}, {}
    cots = {"xs": [(dxs_dir[0], 512, gcol), (dxs_dir[1], 512, gcol), (dxs_skip, 512, gcol)],
            "B": [(db_dir[0], 512, gcol), (db_dir[1], 512, gcol)],
            "C": [(dc_dir[0], 512, gcol), (dc_dir[1], 512, gcol)]}
    for name in ("xs", "B", "C"):
        wd = conv_w[name].shape[1]
        d_main, g_cw[name], g_cb[name] = _rows(
            "ssd_conv_bwd_" + name, make_conv_bwd(len(cots[name])), T, wd // 512,
            [seg(name, 512, gcol)] + cots[name], [(conv_w[name], 512, gcol), (conv_b[name], 512, gcol)],
            [into_main(name, 512, gcol)], [(3, wd, 512, gcol), (1, wd, 512, gcol)], halo=True, tall=True)
    grads["w_ssd_conv"] = jnp.concatenate([g_cw["xs"], g_cw["B"], g_cw["C"]], axis=1)
    grads["b_ssd_conv"] = jnp.concatenate([g_cb["xs"], g_cb["B"], g_cb["C"]], axis=1)

    da_ret = _mm("d_ret_act", dyr, w["w_ret_out"], "nt", out_dtype=BF16)
    grads["w_ret_out"] = _mm("g_ret_out", a_ret, dyr, "tn", out_dtype=BF16)
    tick = early_grads({n: grads.pop(n) for n in ("w_ffn_up_t", "w_ret_out", "w_ssd_out", "w_out", "w_ffn_down")})

    def ret_post_bwd_fn(i, y, g, da, gw):
        _, vjp = jax.vjp(_ret_post, y, g, gw)
        return vjp(da)

    dy_ret, d_main, grads["ret_gn_w"] = _rows(
        "ret_post_bwd", ret_post_bwd_fn, T, 1, [(y_ret, 1024, c0), seg("g", 1024), (da_ret, 1024, c0)],
        [(w["ret_gn_w"] + tick, 1024, c0)], [(1024, 1024, c0, BF16), into_main("g", 1024)], [(1, 1024, 1024, c0)],
        tall=True)
    d_main = _retention("retention_dv", kr, qr, dy_ret, T, RET_QK_DIM, RET_V_DIM, into=(d_main, seg_at["v"]))
    dqr = _retention("retention_dq", dy_ret, v_at, kr, T, RET_V_DIM, RET_QK_DIM)
    dkr = _retention("retention_dk", v_at, dy_ret, qr, T, RET_V_DIM, RET_QK_DIM)

    def rot_bwd_fn(i, dq, dk, csv, snv):
        parts = [_rot_t(dq[:, h * 128:(h + 1) * 128], csv, snv) for h in range(RET_HEADS)]
        parts += [_rot_t(dk[:, h * 128:(h + 1) * 128] * scale, csv, snv) for h in range(RET_HEADS)]
        return (jnp.concatenate(parts, axis=1),)

    d_main = _rows("rotary_bwd", rot_bwd_fn, T, 1, [(dqr, 512, c0), (dkr, 512, c0), (cs, 128, c0), (sn, 128, c0)],
                   [], [into_main("qk", 1024)], tall=True)[0]

    g_in = [_mm("g_in_main", d_main, u, "tn", out_dtype=BF16),
            _mm("g_in_dt", dproj["dt"], u, "tn", out_dtype=BF16)[:2 * SSD_HEADS],
            _mm("g_in_gates", dproj["gates"], u, "tn", out_dtype=BF16)]
    tick = in_grads(jnp.concatenate(g_in, axis=0))
    du = _mm("d_u_dt", dproj["dt"] + tick.astype(BF16), w_dt, "nn")
    du = _mm("d_u_main", d_main, w_main, "nn", add=du)
    du = _mm("d_u_gates", dproj["gates"], w_gates, "nn", add=du)
    dh0, grads["norm_mix_w"] = norm_bwd("norm_mix_bwd", h0, w["norm_mix_w"], du, dh1)
    grads["meta_tokens"] = dh0[PAD_ROWS:CHUNK]
    return loss, dh0[CHUNK:], grads


MESH_ID = pl.DeviceIdType.MESH
ANY = pl.BlockSpec(memory_space=pl.ANY)


def _me_and_peers():
    x, y, c = lax.axis_index("x"), lax.axis_index("y"), lax.axis_index("c")
    peers = []
    for k in range(1, N_DEV):
        px = 1 - x if k & 4 else x
        py = 1 - y if k & 2 else y
        pc = 1 - c if k & 1 else c
        peers.append(((px, py, pc), 4 * px + 2 * py + pc))
    return 4 * x + 2 * y + c, peers


def _push_blocks(name, src, per_peer):
    blk = src.shape[1:] if per_peer else src.shape

    def body(src_ref, out_ref, send_sems, recv_sems, local_sem):
        me, peers = _me_and_peers()
        mine = src_ref.at[me] if per_peer else src_ref
        local = pltpu.make_async_copy(mine, out_ref.at[me], local_sem)
        local.start()
        sends = []
        for k, (dev, idx) in enumerate(peers):
            cp = pltpu.make_async_remote_copy(
                src_ref=src_ref.at[idx] if per_peer else src_ref, dst_ref=out_ref.at[me],
                send_sem=send_sems.at[k], recv_sem=recv_sems.at[k], device_id=dev, device_id_type=MESH_ID)
            cp.start()
            sends.append(cp)
        for k, (dev, idx) in enumerate(peers):
            pltpu.make_async_remote_copy(
                src_ref=mine, dst_ref=out_ref.at[idx], send_sem=send_sems.at[k], recv_sem=recv_sems.at[k],
                device_id=dev, device_id_type=MESH_ID).wait_recv()
        for cp in sends:
            cp.wait_send()
        local.wait()

    return pl.pallas_call(
        body, name=name, in_specs=[ANY], out_specs=ANY,
        out_shape=jax.ShapeDtypeStruct((N_DEV,) + tuple(blk), src.dtype),
        scratch_shapes=[pltpu.SemaphoreType.DMA((N_DEV - 1,)), pltpu.SemaphoreType.DMA((N_DEV - 1,)),
                        pltpu.SemaphoreType.DMA],
    )(src)


def _gather_two_level(name, src):
    def body(x_ref, out_ref, send_sems, recv_sems, local_sem):
        x, y, c = lax.axis_index("x"), lax.axis_index("y"), lax.axis_index("c")
        me, sibling = (x, y, c), (x, y, 1 - c)
        chips = [(1 - x, y), (x, 1 - y), (1 - x, 1 - y)]

        def rows(px, py, pc):
            return out_ref.at[4 * px + 2 * py + pc]

        def copy(k, block, to, src_ref=None):
            return pltpu.make_async_remote_copy(
                src_ref=rows(*block) if src_ref is None else src_ref, dst_ref=rows(*block),
                send_sem=send_sems.at[k], recv_sem=recv_sems.at[k], device_id=to, device_id_type=MESH_ID)

        mine = pltpu.make_async_copy(x_ref, rows(*me), local_sem)
        mine.start()
        first = [copy(0, me, sibling, x_ref)] + [copy(1 + j, me, (*chip, c), x_ref) for j, chip in enumerate(chips)]
        for cp in first:
            cp.start()
        passed = [copy(4 + j, (*chip, c), sibling) for j, chip in enumerate(chips)]
        for j, chip in enumerate(chips):
            copy(1 + j, (*chip, c), me).wait_recv()
            passed[j].start()
        copy(0, sibling, me).wait_recv()
        for j, chip in enumerate(chips):
            copy(4 + j, (*chip, 1 - c), me).wait_recv()
        for cp in first + passed:
            cp.wait_send()
        mine.wait()

    return pl.pallas_call(
        body, name=name, in_specs=[ANY], out_specs=ANY,
        out_shape=jax.ShapeDtypeStruct((N_DEV,) + tuple(src.shape), src.dtype),
        scratch_shapes=[pltpu.SemaphoreType.DMA((N_DEV - 1,)), pltpu.SemaphoreType.DMA((N_DEV - 1,)),
                        pltpu.SemaphoreType.DMA],
    )(src)


HBM = pl.BlockSpec(memory_space=pltpu.HBM)
SEM = pl.BlockSpec(memory_space=pltpu.SEMAPHORE)
EFFECT = pltpu.SideEffectType.DATAFLOW_SIDE_EFFECTING


def _peer_copy(src_ref, land_ref, send_sems, recv_sems, per_peer, me, a, k, dev, idx, receiving):
    s = a * (N_DEV - 1) + k
    return pltpu.make_async_remote_copy(
        src_ref=src_ref.at[idx] if per_peer else src_ref, dst_ref=land_ref.at[idx if receiving else me],
        send_sem=send_sems.at[s], recv_sem=recv_sems.at[s], device_id=dev, device_id_type=MESH_ID)


def _push_start(name, srcs, per_peer):
    n = len(srcs)
    land_shapes = [(N_DEV,) + tuple(s.shape[1:] if per_peer else s.shape) for s in srcs]

    def body(*refs):
        src_refs, land_refs, send_sems, recv_sems, token = refs[:n], refs[n:2 * n], refs[2 * n], refs[2 * n + 1], refs[-1]
        me, peers = _me_and_peers()
        for a in range(n):
            for k, (dev, idx) in enumerate(peers):
                _peer_copy(src_refs[a], land_refs[a], send_sems, recv_sems, per_peer, me, a, k, dev, idx, False).start()
        token[...] = jnp.zeros_like(token)

    sems = pltpu.SemaphoreType.DMA((n * (N_DEV - 1),))
    res = pl.pallas_call(
        body, name=name,
        out_shape=(sems, sems, *[pltpu.HBM(s.shape, s.dtype) for s in srcs],
                   *[pltpu.HBM(ls, s.dtype) for ls, s in zip(land_shapes, srcs)], jax.ShapeDtypeStruct((8, 128), F32)),
        in_specs=(HBM,) * (2 * n), out_specs=(SEM, SEM) + (HBM,) * (2 * n) + (pl.BlockSpec(memory_space=pltpu.VMEM),),
        input_output_aliases={i: 2 + i for i in range(2 * n)},
        compiler_params=pltpu.CompilerParams(has_side_effects=EFFECT),
    )(*[pltpu.with_memory_space_constraint(s, pltpu.HBM) for s in srcs],
      *[pltpu.with_memory_space_constraint(lax.empty(ls, s.dtype), pltpu.HBM) for ls, s in zip(land_shapes, srcs)])
    return res[0], res[1], res[2:2 + n], res[2 + n:2 + 2 * n], res[-1]


def _push_wait(name, send_sems, recv_sems, srcs_thru, lands_thru, after, per_peer):
    n = len(srcs_thru)

    def body(*refs):
        src_refs, land_refs, send_sems, recv_sems = refs[:n], refs[n:2 * n], refs[2 * n], refs[2 * n + 1]
        me, peers = _me_and_peers()
        for a in range(n):
            for k, (dev, idx) in enumerate(peers):
                cp = _peer_copy(src_refs[a], land_refs[a], send_sems, recv_sems, per_peer, me, a, k, dev, idx, True)
                cp.wait_send()
                cp.wait_recv()

    both = list(srcs_thru) + list(lands_thru)
    res = pl.pallas_call(
        body, name=name, out_shape=tuple(pltpu.HBM(t.shape, t.dtype) for t in both),
        in_specs=(HBM,) * (2 * n) + (SEM, SEM, ANY), out_specs=(HBM,) * (2 * n),
        input_output_aliases={i: i for i in range(2 * n)},
        compiler_params=pltpu.CompilerParams(has_side_effects=EFFECT),
    )(*both, send_sems, recv_sems, after)
    return res[:n], res[n:]


def _sum_blocks(name, blocks):
    _, R, C = blocks.shape
    tc = next(t for t in (1024, 512, 256, 128) if C % t == 0 and (N_DEV * R * t * 2 <= 6 * 2 ** 20 or t == 128))

    def body(b_ref, o_ref):
        acc = b_ref[0].astype(F32)
        for k in range(1, N_DEV):
            acc = acc + b_ref[k].astype(F32)
        o_ref[...] = acc

    return pl.pallas_call(
        body, name=name, grid=(C // tc,), in_specs=[pl.BlockSpec((N_DEV, R, tc), lambda j: (0, 0, j))],
        out_specs=pl.BlockSpec((R, tc), lambda j: (0, j)), out_shape=jax.ShapeDtypeStruct((R, C), F32),
        compiler_params=_params(("arbitrary",)),
    )(blocks)


def _adamw(name, w, g, m, v):
    R, C = w.shape
    tr = R if R <= 512 else _pick(R, (256, 184, 176, 128, 8))
    spec = pl.BlockSpec((tr, C), lambda i: (i, 0))

    def body(w_ref, g_ref, m_ref, v_ref, d_ref, mo_ref, vo_ref):
        gv = g_ref[...]
        mn = ADAM_B1 * m_ref[...] + (1.0 - ADAM_B1) * gv
        vn = ADAM_B2 * v_ref[...] + (1.0 - ADAM_B2) * jnp.square(gv)
        m_hat = mn / (1.0 - ADAM_B1 ** ADAM_STEP)
        v_hat = vn / (1.0 - ADAM_B2 ** ADAM_STEP)
        d_ref[...] = -ADAM_LR * (m_hat / (jnp.sqrt(v_hat) + ADAM_EPS) + ADAM_WD * w_ref[...])
        mo_ref[...] = mn
        vo_ref[...] = vn

    return pl.pallas_call(
        body, name=name, grid=(R // tr,), in_specs=[spec] * 4, out_specs=[spec] * 3,
        out_shape=[jax.ShapeDtypeStruct((R, C), F32)] * 3, compiler_params=_params(("arbitrary",)),
    )(w, g, m, v)


WEIGHTS = ("meta_tokens", "norm_mix_w", "w_in", "ret_gn_w", "w_ret_out", "w_ssd_conv", "b_ssd_conv", "dt_bias_f",
           "dt_bias_b", "a_log_f", "a_log_b", "d_skip", "ssd_norm_w", "w_ssd_out", "w_out", "norm_ffn_w", "w_ffn_up",
           "w_ffn_conv", "b_ffn_conv", "w_ffn_down", "final_norm_w")
BIG = (("w_in", 1288, True), ("w_ffn_up", 704, True), ("w_ret_out", 128, False), ("w_ssd_out", 256, False),
       ("w_out", 128, False), ("w_ffn_down", 352, False))
REPLICATED = ("norm_mix_w", "ret_gn_w", "b_ssd_conv", "dt_bias_f", "dt_bias_b", "a_log_f", "a_log_b", "d_skip",
              "ssd_norm_w", "norm_ffn_w", "b_ffn_conv", "final_norm_w")
SMALL_SHARDED = (("meta_tokens", 16, 1024), ("w_ssd_conv", 3, 3072), ("w_ffn_conv", 3, 5632))


BIG_IN, BIG_REST = BIG[:1], BIG[1:]


def _pack_flat(arrays, rows):
    flat = jnp.concatenate([a.reshape(-1) for a in arrays])
    return jnp.pad(flat, (0, rows * D_MODEL - flat.shape[0])).reshape(rows, D_MODEL)


def _unpack_flat(slab, shapes):
    flat, out, o = slab.reshape(-1), [], 0
    for s in shapes:
        n = math.prod(s)
        out.append(flat[o:o + n].reshape(s))
        o += n
    return out


def kernel(x, meta_tokens, norm_mix_w, w_in, ret_gn_w, w_ret_out, w_ssd_conv, b_ssd_conv, dt_bias_f, dt_bias_b, a_log_f, a_log_b, d_skip, ssd_norm_w, w_ssd_out, w_out, norm_ffn_w, w_ffn_up, w_ffn_conv, b_ffn_conv, w_ffn_down, final_norm_w, loss_target, m_meta_tokens, m_norm_mix_w, m_w_in, m_ret_gn_w, m_w_ret_out, m_w_ssd_conv, m_b_ssd_conv, m_dt_bias_f, m_dt_bias_b, m_a_log_f, m_a_log_b, m_d_skip, m_ssd_norm_w, m_w_ssd_out, m_w_out, m_norm_ffn_w, m_w_ffn_up, m_w_ffn_conv, m_b_ffn_conv, m_w_ffn_down, m_final_norm_w, v_meta_tokens, v_norm_mix_w, v_w_in, v_ret_gn_w, v_w_ret_out, v_w_ssd_conv, v_b_ssd_conv, v_dt_bias_f, v_dt_bias_b, v_a_log_f, v_a_log_b, v_d_skip, v_ssd_norm_w, v_w_ssd_out, v_w_out, v_norm_ffn_w, v_w_ffn_up, v_w_ffn_conv, v_b_ffn_conv, v_w_ffn_down, v_final_norm_w):
    given = dict(locals())
    wt = {n: given[n] for n in WEIGHTS}
    mt = {n: given["m_" + n] for n in WEIGHTS}
    vt = {n: given["v_" + n] for n in WEIGHTS}
    me = 4 * lax.axis_index("x") + 2 * lax.axis_index("y") + lax.axis_index("c")

    small_names = [n for n, _, _ in SMALL_SHARDED]
    small_local = lambda tree: [tree[n].reshape(r, c // N_DEV) for n, r, c in SMALL_SHARDED]
    slab_view = lambda tree, name, transposed: tree[name][0].T if transposed else tree[name][0]
    all_in = _gather_two_level("gather_w_in", slab_view(wt, "w_in", True).astype(BF16))
    all_s = _push_blocks("gather_small", _pack_flat(small_local(wt), 8), False)
    rest_srcs = [slab_view(wt, name, t).astype(BF16) for name, _, t in BIG_REST]
    rest_srcs, all_in, all_s = lax.optimization_barrier((rest_srcs, all_in, all_s))
    rest_flight = _push_start("gather_rest_start", rest_srcs, False)
    all_s = all_s.reshape(N_DEV, -1)
    full = {"w_in_t": all_in.reshape(-1, D_MODEL)}

    def lands_with_own(flight, after, per_peer, name):
        srcs, lands = _push_wait(name, *flight[:4], after, per_peer)
        own = lambda s: lax.dynamic_slice_in_dim(s, me, 1, axis=0) if per_peer else s[None]
        return [lax.dynamic_update_slice_in_dim(land, own(s), me, axis=0) for s, land in zip(srcs, lands)]

    def late_weights(after):
        lands = lands_with_own(rest_flight, after, False, "gather_rest_wait")
        return {name + ("_t" if t else ""): land.reshape(N_DEV * r, D_MODEL) for (name, r, t), land in zip(BIG_REST, lands)}

    flights = {}

    def start_exchange(key, group, gd):
        srcs = [gd[name + ("_t" if t else "")].astype(BF16).reshape(N_DEV, r, D_MODEL) for name, r, t in group]
        flights[key] = _push_start("exchange_" + key + "_start", srcs, True)
        return flights[key][4][0, 0]

    o = 0
    for name, r, c in SMALL_SHARDED:
        n = r * c // N_DEV
        full[name] = all_s[:, o:o + n].reshape(N_DEV, r, c // N_DEV).transpose(1, 0, 2).reshape(r, c)
        o += n
    for name in REPLICATED:
        full[name] = wt[name]

    grads, delta, new_m, new_v = {}, {}, {}, {}

    def finish_exchange(key, group, after):
        lands = lands_with_own(flights[key], after, True, "exchange_" + key + "_wait")
        for (name, _, transposed), land in zip(group, lands):
            back = (lambda a: a.T[None]) if transposed else (lambda a: a[None])
            g_sum = _sum_blocks("sum_" + name, land)
            d, mn, vn = _adamw("adamw_" + name, slab_view(wt, name, transposed), g_sum,
                               slab_view(mt, name, transposed), slab_view(vt, name, transposed))
            grads[name], delta[name], new_m[name], new_v[name] = back(g_sum), back(d), back(mn), back(vn)

    def in_grads(gi):
        tick = start_exchange("in", BIG_IN, {"w_in_t": gi})
        finish_exchange("rest", BIG_REST, flights["in"][4])
        tick, _ = lax.optimization_barrier((tick, [delta[name] for name, _, _ in BIG_REST]))
        return tick

    loss, grad_x, g = _local_step(x[0], loss_target[0], full, rest_flight[4][0, 0], late_weights,
                                  lambda gd: start_exchange("rest", BIG_REST, gd), in_grads)

    small_parts = [g[n] for n in REPLICATED] + [g[n] for n in small_names] + [loss.reshape(1)]
    small_flight = _push_start("gather_small_grads_start", [_pack_flat(small_parts, 64)], False)
    finish_exchange("in", BIG_IN, small_flight[4])
    g_small = _sum_blocks("sum_small", lands_with_own(small_flight, delta["w_in"], False, "gather_small_grads_wait")[0])
    small_red = _unpack_flat(g_small, [wt[n].shape for n in REPLICATED] + [(r, c) for _, r, c in SMALL_SHARDED] + [(1,)])
    grads.update(zip(REPLICATED, small_red[:len(REPLICATED)]))
    for (name, r, c), red in zip(SMALL_SHARDED, small_red[len(REPLICATED):-1]):
        grads[name] = lax.dynamic_slice(red, (0, me * (c // N_DEV)), (r, c // N_DEV)).reshape(wt[name].shape)
    loss_all = small_red[-1][0]

    rest = list(REPLICATED) + small_names
    shapes = [wt[n].shape for n in rest]
    pack_rest = lambda tree: _pack_flat([tree[n] for n in rest], 24)
    d_rest, m_rest, v_rest = _adamw("adamw_small", pack_rest(wt), pack_rest(grads), pack_rest(mt), pack_rest(vt))
    delta.update(zip(rest, _unpack_flat(d_rest, shapes)))
    new_m.update(zip(rest, _unpack_flat(m_rest, shapes)))
    new_v.update(zip(rest, _unpack_flat(v_rest, shapes)))

    return (loss_all, grad_x[None], *[grads[n] for n in WEIGHTS], *[delta[n] for n in WEIGHTS],
            *[new_m[n] for n in WEIGHTS], *[new_v[n] for n in WEIGHTS])
```

```python
import functools
import math

import jax
import jax.numpy as jnp
from jax import lax
from jax.experimental import pallas as pl
from jax.experimental.pallas import tpu as pltpu

F32 = jnp.float32
BF16 = jnp.bfloat16

D_MODEL = 1024
CHUNK = 128
N_META = 16
PAD_ROWS = CHUNK - N_META
RET_HEADS = 4
RET_QK_DIM = 128
RET_V_DIM = 256
SSD_HEADS = 32
SSD_HEAD_DIM = 64
SSD_GROUPS = 4
SSD_STATE = 128
HEADS_PER_GROUP = SSD_HEADS // SSD_GROUPS
PAIRS_PER_GROUP = HEADS_PER_GROUP // 2
D_FF = 2816
EPS = 1e-6
ROPE_BASE = 10000.0
N_DEV = 8

ADAM_LR = 0.001
ADAM_B1 = 0.9
ADAM_B2 = 0.999
ADAM_EPS = 1e-08
ADAM_WD = 0.01
ADAM_STEP = 10

VMEM_LIMIT = 56 * 1024 * 1024
HALO = 16
HIGHEST = lax.Precision.HIGHEST

SEGMENTS = (("qk", 0, 1024), ("v", 1024, 2048), ("g", 2048, 3072), ("z", 3072, 5120), ("xs", 5120, 7168),
            ("B", 7168, 7680), ("C", 7680, 8192), ("dt", 8192, 8256), ("gates", 8256, 10304))


def _pick(n, cands):
    for c in cands:
        if n % c == 0:
            return c
    raise ValueError(f"no tile for {n}")


def _params(sem):
    return pltpu.CompilerParams(dimension_semantics=sem, vmem_limit_bytes=VMEM_LIMIT)


def _dot(a, b, dims=(((1,), (0,)), ((), ())), precision=None):
    return lax.dot_general(a, b, dims, preferred_element_type=F32, precision=precision)


def _dot_nt(a, b):
    return _dot(a, b, (((1,), (1,)), ((), ())))


def _dot_tn(a, b):
    return _dot(a, b, (((0,), (0,)), ((), ())))


def _mm(name, a, b, mode, add=None, out_dtype=F32):
    if mode == "nn":
        (M, K), N = a.shape, b.shape[1]
    elif mode == "nt":
        (M, K), N = a.shape, b.shape[0]
    else:
        (K, M), N = a.shape, b.shape[1]
    tn = _pick(N, (1408, 1024, 512, 128, 64))
    if mode == "tn":
        tm = M if M <= 1024 else _pick(M, (1408, 1024))
        tk = _pick(K, (2112, 512, 256, 128))
    else:
        tm = _pick(M, (1056, 512, 256, 128))
        tk = K if K <= 2816 else _pick(K, (2048, 1408, 1024))
    nk = K // tk
    if mode == "nn":
        a_spec = pl.BlockSpec((tm, tk), lambda n, m, k: (m, k))
        b_spec = pl.BlockSpec((tk, tn), lambda n, m, k: (k, n))
        dims = (((1,), (0,)), ((), ()))
    elif mode == "nt":
        a_spec = pl.BlockSpec((tm, tk), lambda n, m, k: (m, k))
        b_spec = pl.BlockSpec((tn, tk), lambda n, m, k: (n, k))
        dims = (((1,), (1,)), ((), ()))
    else:
        a_spec = pl.BlockSpec((tk, tm), lambda n, m, k: (k, m))
        b_spec = pl.BlockSpec((tk, tn), lambda n, m, k: (k, n))
        dims = (((0,), (0,)), ((), ()))
    o_spec = pl.BlockSpec((tm, tn), lambda n, m, k: (m, n))
    in_specs = [a_spec, b_spec] + ([o_spec] if add is not None else [])
    args = [a, b] + ([add] if add is not None else [])

    def body(*refs):
        if add is not None:
            a_ref, b_ref, r_ref, o_ref, acc = refs
        else:
            a_ref, b_ref, o_ref, acc = refs
        k = pl.program_id(2)
        p = _dot(a_ref[...].astype(BF16), b_ref[...].astype(BF16), dims)

        def finish(r):
            if add is not None:
                r = r + r_ref[...]
            o_ref[...] = r.astype(out_dtype)

        if nk == 1:
            finish(p)
        else:
            @pl.when(k == 0)
            def _():
                acc[...] = p

            @pl.when(k > 0)
            def _():
                acc[...] += p

            @pl.when(k == nk - 1)
            def _():
                finish(acc[...])

    return pl.pallas_call(
        body, name=name, grid=(N // tn, M // tm, nk), in_specs=in_specs, out_specs=o_spec,
        out_shape=jax.ShapeDtypeStruct((M, N), out_dtype),
        scratch_shapes=[pltpu.VMEM((tm, tn) if nk > 1 else (8, 128), F32)],
        compiler_params=_params(("arbitrary", "arbitrary", "arbitrary")),
    )(*args)


ANY_SPACE = pl.BlockSpec(memory_space=pl.ANY)


def _const(c):
    return lambda j: c


def _rows(name, fn, T, ncol, ins, params, outs, accs=(), halo=False, tall=False):
    tm = _pick(T, (1056, 512, 256, 128)) if tall else _pick(T, (384, 256, 128))
    R = T // tm
    hb = tm // HALO
    in_specs, args = [], []
    for spec in ins:
        arr, w, cf = spec[:3]
        lead = spec[3] if len(spec) > 3 else None
        if len(spec) > 4:
            rows, rf = spec[4]
            in_specs.append(pl.BlockSpec((rows, w), lambda j, i, cf=cf, rf=rf: (rf(i), cf(j))))
            args.append(arr)
            continue
        if lead is None:
            mk = lambda blk, rf, cf=cf: pl.BlockSpec(blk, lambda j, i: (rf(i), cf(j)))
            shape = lambda r, w=w: (r, w)
        else:
            mk = lambda blk, rf, cf=cf, lead=lead: pl.BlockSpec(blk, lambda j, i: (lead, rf(i), cf(j)))
            shape = lambda r, w=w: (None, r, w)
        in_specs.append(mk(shape(tm), lambda i: i))
        args.append(arr)
        if halo:
            in_specs.append(mk(shape(HALO), lambda i: jnp.maximum(i * hb - 1, 0)))
            in_specs.append(mk(shape(HALO), lambda i: jnp.minimum((i + 1) * hb, T // HALO - 1)))
            args += [arr, arr]
    for arr, w, cf in params:
        in_specs.append(pl.BlockSpec((arr.shape[0], w), lambda j, i, cf=cf: (0, cf(j))))
        args.append(arr)
    out_shape, out_specs, aliases = [], [], {}
    for k, (tw, w, cf, dt) in enumerate(outs):
        if not isinstance(tw, int):
            aliases[len(args)] = k
            in_specs.append(ANY_SPACE)
            args.append(tw)
            tw = tw.shape[1]
        out_shape.append(jax.ShapeDtypeStruct((T, tw), dt))
        out_specs.append(pl.BlockSpec((tm, w), lambda j, i, cf=cf: (i, cf(j))))
    for r, tw, w, cf in accs:
        out_shape.append(jax.ShapeDtypeStruct((r, tw), F32))
        out_specs.append(pl.BlockSpec((r, w), lambda j, i, cf=cf: (0, cf(j))))
    n_in, n_par, n_out, n_acc, n_alias = len(ins), len(params), len(outs), len(accs), len(aliases)

    def body(*refs):
        i = pl.program_id(1)
        vals, p = [], 0
        for _ in range(n_in):
            if halo:
                before = jnp.where(i > 0, refs[p + 1][...], jnp.zeros_like(refs[p + 1]))
                after = jnp.where(i < R - 1, refs[p + 2][...], jnp.zeros_like(refs[p + 2]))
                vals.append(jnp.concatenate([before, refs[p][...], after], axis=0).astype(F32))
                p += 3
            else:
                vals.append(refs[p][...].astype(F32))
                p += 1
        pvals = [refs[p + k][...] for k in range(n_par)]
        p += n_par + n_alias
        res = fn(i, *vals, *pvals)
        for k in range(n_out):
            refs[p + k][...] = res[k].astype(refs[p + k].dtype)
        p += n_out
        for k in range(n_acc):
            ref, v = refs[p + k], res[n_out + k]

            @pl.when(i == 0)
            def _(ref=ref, v=v):
                ref[...] = v

            @pl.when(i > 0)
            def _(ref=ref, v=v):
                ref[...] += v

    res = pl.pallas_call(
        body, name=name, grid=(ncol, R), in_specs=in_specs, out_specs=out_specs, out_shape=out_shape,
        input_output_aliases=aliases, compiler_params=_params(("arbitrary", "arbitrary")),
    )(*args)
    return res


def _tile_rows(T):
    return _pick(T, (384, 256, 128))


def _row_ids(i, T, halo=False, tall=False):
    tm = _pick(T, (1056, 512, 256, 128)) if tall else _tile_rows(T)
    if halo:
        return i * tm - HALO + lax.broadcasted_iota(jnp.int32, (tm + 2 * HALO, 1), 0)
    return i * tm + lax.broadcasted_iota(jnp.int32, (tm, 1), 0)


def _rms(x, w):
    return x * lax.rsqrt(jnp.mean(x * x, axis=-1, keepdims=True) + EPS) * w


def _silu(x):
    return x * jax.nn.sigmoid(x)


def _conv3(x, w):
    n = x.shape[0]
    return w[0:1] * pltpu.roll(x, 1, 0) + w[1:2] * x + w[2:3] * pltpu.roll(x, n - 1, 0)


def _conv3_t(d, w):
    n = d.shape[0]
    return w[0:1] * pltpu.roll(d, n - 1, 0) + w[1:2] * d + w[2:3] * pltpu.roll(d, 1, 0)


def _center(x):
    return x[HALO:x.shape[0] - HALO]


def _retention(name, a, b, v, T, da, dv, into=None):
    (a, a0), (b, b0), (v, v0) = [t if isinstance(t, tuple) else (t, 0) for t in (a, b, v)]
    nc = T // CHUNK
    log_gammas = [math.log(1.0 - 2.0 ** (-5.0 - h)) for h in range(RET_HEADS)]

    def body(*refs):
        a_ref, b_ref, v_ref = refs[:3]
        out_ref, o_ref, st, st_b = refs[-4:]
        h = pl.program_id(0)
        lg = jnp.float32(log_gammas[RET_HEADS - 1])
        for k in range(RET_HEADS - 2, -1, -1):
            lg = jnp.where(h == k, jnp.float32(log_gammas[k]), lg)
        li = lax.broadcasted_iota(jnp.int32, (CHUNK, CHUNK), 0)
        si = lax.broadcasted_iota(jnp.int32, (CHUNK, CHUNK), 1)
        dmat = jnp.exp(lg * jnp.abs(li - si).astype(F32))
        pos = lax.broadcasted_iota(jnp.int32, (CHUNK, 1), 0).astype(F32)
        kdec_f = jnp.exp((CHUNK - 1 - pos) * lg)
        qdec_f = jnp.exp((pos + 1) * lg)
        kdec_b = jnp.exp(pos * lg)
        qdec_b = jnp.exp((CHUNK - pos) * lg)
        cdec = jnp.exp(CHUNK * lg)

        def rows(n):
            return pl.ds(pl.multiple_of(n * CHUNK, CHUNK), CHUNK)

        st[...] = jnp.zeros_like(st)
        st_b[...] = jnp.zeros_like(st_b)
        o_ref[...] = jnp.zeros_like(o_ref)

        def step(m, carry):
            r = rows(m)
            av, bv, vv = a_ref[r, :], b_ref[r, :], v_ref[r, :].astype(BF16)
            s = _dot_nt(av.astype(BF16), bv.astype(BF16)) * dmat
            o_ref[r, :] += _dot(s.astype(BF16), vv) + _dot((av * qdec_f).astype(BF16), st[...].astype(BF16))
            st[...] = cdec * st[...] + _dot_tn((bv * kdec_f).astype(BF16), vv)
            r = rows(nc - 1 - m)
            av, bv, vv = a_ref[r, :], b_ref[r, :], v_ref[r, :].astype(BF16)
            o_ref[r, :] += _dot((av * qdec_b).astype(BF16), st_b[...].astype(BF16))
            st_b[...] = cdec * st_b[...] + _dot_tn((bv * kdec_b).astype(BF16), vv)
            return carry

        lax.fori_loop(0, nc, step, 0, unroll=True)
        out_ref[...] = o_ref[...].astype(out_ref.dtype)

    in_specs = [pl.BlockSpec((T, da), lambda h: (0, a0 // da + h)), pl.BlockSpec((T, da), lambda h: (0, b0 // da + h)),
                pl.BlockSpec((T, dv), lambda h: (0, v0 // dv + h))]
    if into is None:
        args, o0, aliases = (a, b, v), 0, {}
        out_shape = jax.ShapeDtypeStruct((T, RET_HEADS * dv), F32)
    else:
        args, o0, aliases = (a, b, v, into[0]), into[1], {3: 0}
        in_specs.append(ANY_SPACE)
        out_shape = jax.ShapeDtypeStruct(into[0].shape, into[0].dtype)
    return pl.pallas_call(
        body, name=name, grid=(RET_HEADS,), in_specs=in_specs,
        out_specs=pl.BlockSpec((T, dv), lambda h: (0, o0 // dv + h)), out_shape=out_shape,
        input_output_aliases=aliases,
        scratch_shapes=[pltpu.VMEM((T, dv), F32), pltpu.VMEM((da, dv), F32), pltpu.VMEM((da, dv), F32)],
        compiler_params=_params(("arbitrary",)),
    )(*args)


def _softplus(x):
    return jnp.maximum(x, 0.0) + jnp.log1p(jnp.exp(-jnp.abs(x)))


def _lane_lo():
    return lax.broadcasted_iota(jnp.int32, (1, CHUNK), 1) < SSD_HEAD_DIM


def _pair_cols(col, j):
    return jnp.where(_lane_lo(), col[:, 2 * j:2 * j + 1], col[:, 2 * j + 1:2 * j + 2])


def _pair_rows(colr, j):
    lo = lax.broadcasted_iota(jnp.int32, (CHUNK, 1), 0) < SSD_HEAD_DIM
    return jnp.where(lo, colr[2 * j:2 * j + 1, :], colr[2 * j + 1:2 * j + 2, :])


def _onehot8(h):
    return (lax.broadcasted_iota(jnp.int32, (1, HEADS_PER_GROUP), 1) == h).astype(F32)


def _ssd_pre(d, c, rawc, rawr, bc, br, alc, alr):
    li = lax.broadcasted_iota(jnp.int32, (CHUNK, CHUNK), 0)
    si = lax.broadcasted_iota(jnp.int32, (CHUNK, CHUNK), 1)
    dif = li - si if d == 0 else si - li
    mask = dif >= 0
    mask_t = dif <= 0
    rowc = c * CHUNK + lax.broadcasted_iota(jnp.int32, (CHUNK, 1), 0)
    rowr = c * CHUNK + lax.broadcasted_iota(jnp.int32, (1, CHUNK), 1)
    dtc = jnp.where(rowc >= PAD_ROWS, _softplus(rawc + bc), 0.0)
    dtr = jnp.where(rowr >= PAD_ROWS, _softplus(rawr + br), 0.0)
    ac = -jnp.exp(alc)
    ar = -jnp.exp(alr)
    dlc = dtc * ac
    dlr = dtr * ar
    alpc = _dot(mask.astype(F32), dlc, precision=HIGHEST)
    alpr = _dot(dlr, mask_t.astype(F32), precision=HIGHEST)
    endc = jnp.sum(dlc, axis=0, keepdims=True)
    endr = jnp.sum(dlr, axis=1, keepdims=True)
    return dict(mask=mask, mask_t=mask_t, dtc=dtc, ac=ac, alpc=alpc, alpr=alpr, endc=endc, endr=endr,
                valid=rowc >= PAD_ROWS)


def _chunk_of(d, n, nc):
    return n + d * (nc - 1 - 2 * n)


GROUP_WIDTH = HEADS_PER_GROUP * SSD_HEAD_DIM


def _chunks_per_step(nc, most=3):
    return next(c for c in (11, 3, 1) if c <= most and nc % c == 0)


def _ssd_in_specs(d, cfn, rows):
    return [
        pl.BlockSpec((rows, GROUP_WIDTH), lambda g, n: (cfn(d, n), g)),
        pl.BlockSpec((rows, SSD_STATE), lambda g, n: (cfn(d, n), g)),
        pl.BlockSpec((rows, SSD_STATE), lambda g, n: (cfn(d, n), g)),
        pl.BlockSpec((None, None, rows, HEADS_PER_GROUP), lambda g, n: (d, g, cfn(d, n), 0)),
        pl.BlockSpec((None, None, HEADS_PER_GROUP, rows), lambda g, n: (d, g, 0, cfn(d, n))),
        pl.BlockSpec((None, None, 1, HEADS_PER_GROUP), lambda g, n: (d, g, 0, 0)),
        pl.BlockSpec((None, None, HEADS_PER_GROUP, 1), lambda g, n: (d, g, 0, 0)),
        pl.BlockSpec((None, None, 1, HEADS_PER_GROUP), lambda g, n: (d, g, 0, 0)),
        pl.BlockSpec((None, None, HEADS_PER_GROUP, 1), lambda g, n: (d, g, 0, 0)),
    ]


N_SSD_IN = 9


def _ssd_fwd(xs, bm, cm, small, T):
    nc = T // CHUNK
    cps = _chunks_per_step(nc, 11)
    rows = cps * CHUNK
    cfn = lambda d, n: _chunk_of(d, n, nc // cps)

    def one_direction(d, n, ins, y_ref, hs_ref, h_scr):
        x_ref, b_ref, c_ref, rawc_ref, rawr_ref, *per_group = ins
        for kk in range(cps):
            k = kk if d == 0 else cps - 1 - kk
            r = pl.ds(k * CHUNK, CHUNK)
            one_chunk(d, cfn(d, n) * cps + k,
                      (x_ref.at[r], b_ref.at[r], c_ref.at[r], rawc_ref.at[r], rawr_ref.at[:, r], *per_group),
                      y_ref.at[r], hs_ref.at[k], h_scr)

    def one_chunk(d, c, ins, y_ref, hs_ref, h_scr):
        x_ref, b_ref, c_ref, rawc_ref, rawr_ref, bc_ref, br_ref, alc_ref, alr_ref = ins
        q = _ssd_pre(d, c, rawc_ref[...], rawr_ref[...], bc_ref[...], br_ref[...], alc_ref[...], alr_ref[...])
        bv = b_ref[...].astype(BF16)
        cv = c_ref[...].astype(BF16)
        cb = _dot_nt(cv, bv)
        lo = _lane_lo()
        for j in range(PAIRS_PER_GROUP):
            xp = x_ref[:, j * CHUNK:(j + 1) * CHUNK]
            xd = xp * _pair_cols(q["dtc"], j)
            xdb = xd.astype(BF16)
            yi = []
            for e in range(2):
                h = 2 * j + e
                lm = jnp.exp(jnp.where(q["mask"], q["alpc"][:, h:h + 1] - q["alpr"][h:h + 1, :], -jnp.inf))
                yi.append(_dot((cb * lm).astype(BF16), xdb))
            alp = _pair_cols(q["alpc"], j)
            hp = h_scr[j]
            hs_ref[j] = hp
            yo = jnp.exp(alp) * _dot_nt(cv, hp.astype(BF16))
            y_ref[:, j * CHUNK:(j + 1) * CHUNK] = (jnp.where(lo, yi[0], yi[1]) + yo).astype(y_ref.dtype)
            de = jnp.exp(_pair_cols(q["endc"], j) - alp)
            h_scr[j] = jnp.exp(_pair_rows(q["endr"], j)) * hp + _dot_tn((xd * de).astype(BF16), bv)

    def body(*refs):
        n = pl.program_id(1)
        ins, (y_f, y_b, hs_f, hs_b, h_scr) = refs[:2 * N_SSD_IN], refs[2 * N_SSD_IN:]

        @pl.when(n == 0)
        def _():
            h_scr[...] = jnp.zeros_like(h_scr)

        one_direction(0, n, ins[:N_SSD_IN], y_f, hs_f, h_scr.at[0])
        one_direction(1, n, ins[N_SSD_IN:], y_b, hs_b, h_scr.at[1])

    y_spec = lambda d: pl.BlockSpec((rows, GROUP_WIDTH), lambda g, n: (cfn(d, n), g))
    hs_spec = lambda d: pl.BlockSpec((None, cps, PAIRS_PER_GROUP, CHUNK, SSD_STATE),
                                     lambda g, n: (g, cfn(d, n), 0, 0, 0))
    y_shape = jax.ShapeDtypeStruct((T, SSD_HEADS * SSD_HEAD_DIM), BF16)
    hs_shape = jax.ShapeDtypeStruct((SSD_GROUPS, nc, PAIRS_PER_GROUP, CHUNK, SSD_STATE), F32)
    y_f, y_b, hs_f, hs_b = pl.pallas_call(
        body, name="ssd_fwd", grid=(SSD_GROUPS, nc // cps),
        in_specs=_ssd_in_specs(0, cfn, rows) + _ssd_in_specs(1, cfn, rows),
        out_specs=[y_spec(0), y_spec(1), hs_spec(0), hs_spec(1)],
        out_shape=[y_shape, y_shape, hs_shape, hs_shape],
        scratch_shapes=[pltpu.VMEM((2, PAIRS_PER_GROUP, CHUNK, SSD_STATE), F32)],
        compiler_params=_params(("arbitrary", "arbitrary")),
    )(xs, bm, cm, *small, xs, bm, cm, *small)
    return (y_f, y_b), (hs_f, hs_b)


def _ssd_bwd(xs, bm, cm, small, hs, dy, T):
    nc = T // CHUNK
    cps = _chunks_per_step(nc, 11)
    rows = cps * CHUNK
    cfn = lambda d, n: _chunk_of(1 - d, n, nc // cps)

    def one_direction(d, n, ins, outs, dh_scr):
        x_ref, b_ref, c_ref, rawc_ref, rawr_ref, bc_ref, br_ref, alc_ref, alr_ref, hs_ref, dy_ref = ins
        dx_ref, db_ref, dc_ref, draw_ref, dbias_ref, dalog_ref = outs
        for kk in range(cps):
            k = cps - 1 - kk if d == 0 else kk
            r = pl.ds(k * CHUNK, CHUNK)
            one_chunk(d, cfn(d, n) * cps + k, n if kk == 0 else None,
                      (x_ref.at[r], b_ref.at[r], c_ref.at[r], rawc_ref.at[r], rawr_ref.at[:, r], bc_ref, br_ref,
                       alc_ref, alr_ref, hs_ref.at[k], dy_ref.at[r]),
                      (dx_ref.at[r], db_ref.at[r], dc_ref.at[r], draw_ref.at[r], dbias_ref, dalog_ref), dh_scr)

    def one_chunk(d, c, first_of_step, ins, outs, dh_scr):
        x_ref, b_ref, c_ref, rawc_ref, rawr_ref, bc_ref, br_ref, alc_ref, alr_ref, hs_ref, dy_ref = ins
        dx_ref, db_ref, dc_ref, draw_ref, dbias_ref, dalog_ref = outs
        rawc, bc = rawc_ref[...], bc_ref[...]
        q = _ssd_pre(d, c, rawc, rawr_ref[...], bc, br_ref[...], alc_ref[...], alr_ref[...])
        b32, c32 = b_ref[...], c_ref[...]
        bv, cv = b32.astype(BF16), c32.astype(BF16)
        cb = _dot_nt(cv, bv)
        cbt = _dot_nt(bv, cv)
        lo = _lane_lo()
        row_lo = lax.broadcasted_iota(jnp.int32, (CHUNK, 1), 0) < SSD_HEAD_DIM
        dcb = jnp.zeros((CHUNK, CHUNK), F32)
        dcp = jnp.zeros((CHUNK, SSD_STATE), F32)
        dbp = jnp.zeros((CHUNK, SSD_STATE), F32)
        dalp = jnp.zeros((CHUNK, HEADS_PER_GROUP), F32)
        dend = jnp.zeros((1, HEADS_PER_GROUP), F32)
        ddtx = jnp.zeros((CHUNK, HEADS_PER_GROUP), F32)

        def half_sums(t):
            return (jnp.sum(jnp.where(lo, t, 0.0), axis=1, keepdims=True),
                    jnp.sum(jnp.where(lo, 0.0, t), axis=1, keepdims=True))

        for j in range(PAIRS_PER_GROUP):
            xp = x_ref[:, j * CHUNK:(j + 1) * CHUNK]
            dtp = _pair_cols(q["dtc"], j)
            xd = xp * dtp
            xdb = xd.astype(BF16)
            dyp = dy_ref[:, j * CHUNK:(j + 1) * CHUNK]
            dyb = dyp.astype(BF16)
            hn = hs_ref[j]
            hnb = hn.astype(BF16)
            dh1 = dh_scr[j]
            dh1b = dh1.astype(BF16)
            alp = _pair_cols(q["alpc"], j)
            ea = jnp.exp(alp)
            de = jnp.exp(_pair_cols(q["endc"], j) - alp)
            dxi = []
            for e in range(2):
                h = 2 * j + e
                diff = q["alpc"][:, h:h + 1] - q["alpr"][h:h + 1, :]
                lm = jnp.exp(jnp.where(q["mask"], diff, -jnp.inf))
                mt = cbt * jnp.exp(jnp.where(q["mask_t"], -diff, -jnp.inf))
                dxi.append(_dot(mt.astype(BF16), dyb))
                dyeb_h = (jnp.where(lo, dyp, 0.0) if e == 0 else jnp.where(lo, 0.0, dyp)).astype(BF16)
                gl = _dot_nt(dyeb_h, xdb) * lm
                dcb = dcb + gl
                ra = jnp.sum(gl * cb - _dot_nt(xdb, dyeb_h) * mt, axis=1, keepdims=True)
                dalp = dalp + ra * _onehot8(h)
            y_off = ea * _dot_nt(cv, hnb)
            dxs_state = de * _dot_nt(bv, dh1b)
            dxd = jnp.where(lo, dxi[0], dxi[1]) + dxs_state
            dyeb = (dyp * ea).astype(BF16)
            dcp = dcp + _dot(dyeb, hnb)
            dbp = dbp + _dot((xd * de).astype(BF16), dh1b)
            dh_scr[j] = jnp.exp(_pair_rows(q["endr"], j)) * dh1 + _dot_tn(dyeb, cv)
            r0, r1 = half_sums(dyp * y_off - xd * dxs_state)
            dalp = dalp + r0 * _onehot8(2 * j) + r1 * _onehot8(2 * j + 1)
            t0, t1 = half_sums(jnp.sum(xd * dxs_state, axis=0, keepdims=True))
            u = dh1 * hn
            u0 = jnp.sum(jnp.sum(jnp.where(row_lo, u, 0.0), axis=0, keepdims=True), axis=1, keepdims=True)
            u1 = jnp.sum(jnp.sum(jnp.where(row_lo, 0.0, u), axis=0, keepdims=True), axis=1, keepdims=True)
            eend = jnp.exp(q["endc"])
            dend = dend + (t0 + eend * u0) * _onehot8(2 * j) + (t1 + eend * u1) * _onehot8(2 * j + 1)
            dx_ref[:, j * CHUNK:(j + 1) * CHUNK] = (dxd * dtp).astype(dx_ref.dtype)
            w0, w1 = half_sums(dxd * xp)
            ddtx = ddtx + w0 * _onehot8(2 * j) + w1 * _onehot8(2 * j + 1)

        dcbb = dcb.astype(BF16)
        dc_ref[...] = (dcp + _dot(dcbb, bv)).astype(dc_ref.dtype)
        db_ref[...] = (dbp + _dot_tn(dcbb, cv)).astype(db_ref.dtype)
        ddl = _dot(q["mask_t"].astype(F32), dalp, precision=HIGHEST) + dend
        ddt = ddl * q["ac"] + ddtx
        draw = jnp.where(q["valid"], ddt * jax.nn.sigmoid(rawc + bc), 0.0)
        draw_ref[...] = draw
        dbias = jnp.sum(draw, axis=0, keepdims=True)
        dalog = jnp.sum(ddl * q["dtc"], axis=0, keepdims=True) * q["ac"]

        def add():
            dbias_ref[...] += dbias
            dalog_ref[...] += dalog

        if first_of_step is None:
            add()
        else:
            @pl.when(first_of_step == 0)
            def _():
                dbias_ref[...] = dbias
                dalog_ref[...] = dalog

            pl.when(first_of_step > 0)(add)

    n_in, n_out = N_SSD_IN + 2, 6

    def body(*refs):
        n = pl.program_id(1)
        ins, outs, dh_scr = refs[:2 * n_in], refs[2 * n_in:2 * (n_in + n_out)], refs[-1]

        @pl.when(n == 0)
        def _():
            dh_scr[...] = jnp.zeros_like(dh_scr)

        one_direction(0, n, ins[:n_in], outs[:n_out], dh_scr.at[0])
        one_direction(1, n, ins[n_in:], outs[n_out:], dh_scr.at[1])

    def in_specs(d):
        return _ssd_in_specs(d, cfn, rows) + [
            pl.BlockSpec((None, cps, PAIRS_PER_GROUP, CHUNK, SSD_STATE), lambda g, n: (g, cfn(d, n), 0, 0, 0)),
            pl.BlockSpec((rows, GROUP_WIDTH), lambda g, n: (cfn(d, n), g))]

    def out_specs(d):
        acc = pl.BlockSpec((None, 1, HEADS_PER_GROUP), lambda g, n: (g, 0, 0))
        return [pl.BlockSpec((rows, GROUP_WIDTH), lambda g, n: (cfn(d, n), g)),
                pl.BlockSpec((rows, SSD_STATE), lambda g, n: (cfn(d, n), g)),
                pl.BlockSpec((rows, SSD_STATE), lambda g, n: (cfn(d, n), g)),
                pl.BlockSpec((None, rows, HEADS_PER_GROUP), lambda g, n: (g, cfn(d, n), 0)), acc, acc]

    out_shape = [jax.ShapeDtypeStruct((T, SSD_HEADS * SSD_HEAD_DIM), BF16),
                 jax.ShapeDtypeStruct((T, SSD_GROUPS * SSD_STATE), BF16),
                 jax.ShapeDtypeStruct((T, SSD_GROUPS * SSD_STATE), BF16),
                 jax.ShapeDtypeStruct((SSD_GROUPS, T, HEADS_PER_GROUP), F32),
                 jax.ShapeDtypeStruct((SSD_GROUPS, 1, HEADS_PER_GROUP), F32),
                 jax.ShapeDtypeStruct((SSD_GROUPS, 1, HEADS_PER_GROUP), F32)]
    res = pl.pallas_call(
        body, name="ssd_bwd", grid=(SSD_GROUPS, nc // cps),
        in_specs=in_specs(0) + in_specs(1), out_specs=out_specs(0) + out_specs(1), out_shape=out_shape * 2,
        scratch_shapes=[pltpu.VMEM((2, PAIRS_PER_GROUP, CHUNK, SSD_STATE), F32)],
        compiler_params=_params(("arbitrary", "arbitrary")),
    )(xs, bm, cm, *small, hs[0], dy, xs, bm, cm, *small, hs[1], dy)
    return [(res[k], res[n_out + k]) for k in range(n_out)]


def _rot(x, cs, sn):
    return x * cs + pltpu.roll(x, RET_QK_DIM // 2, 1) * sn


def _rot_t(d, cs, sn):
    return d * cs + pltpu.roll(d * sn, RET_QK_DIM // 2, 1)


def _ret_post(y, g, w):
    parts = []
    for h in range(RET_HEADS):
        yh = y[:, h * RET_V_DIM:(h + 1) * RET_V_DIM]
        mu = jnp.mean(yh, axis=-1, keepdims=True)
        var = jnp.mean(jnp.square(yh - mu), axis=-1, keepdims=True)
        parts.append((yh - mu) * lax.rsqrt(var + EPS))
    return _silu(g) * (jnp.concatenate(parts, axis=1) * w)


def _ssd_post(yf, yb, xs, z, dskip, w):
    y = (yf + yb + xs * dskip) * _silu(z)
    return y * lax.rsqrt(jnp.mean(y * y, axis=-1, keepdims=True) + EPS) * w


def _merge(gates, yr, ys, valid):
    m = jax.nn.sigmoid(gates[:, :D_MODEL]) * yr + jax.nn.sigmoid(gates[:, D_MODEL:]) * ys
    return jnp.where(valid, m, 0.0)


def _rope_tables(T):
    half = RET_QK_DIM // 2
    inv = ROPE_BASE ** (-jnp.arange(half, dtype=F32) / half)
    pos = (jnp.arange(T) - PAD_ROWS).astype(F32)
    ang = pos[:, None] * inv[None, :]
    cos, sin = jnp.cos(ang), jnp.sin(ang)
    return jnp.concatenate([cos, cos], axis=1), jnp.concatenate([-sin, sin], axis=1)


def _per_group(v):
    c = v.reshape(SSD_GROUPS, 1, HEADS_PER_GROUP)
    return c, c.reshape(SSD_GROUPS, HEADS_PER_GROUP, 1)


def _local_step(x, target, w, tick, late_weights, early_grads, in_grads):
    S = x.shape[0]
    T = S + CHUNK
    tm = _tile_rows(T)
    c0 = _const(0)

    h0 = jnp.concatenate([jnp.zeros((PAD_ROWS, D_MODEL), F32), w["meta_tokens"], x], axis=0)
    seg_at = {name: a for name, a, _ in SEGMENTS}
    w_main = w["w_in_t"][:seg_at["dt"]]
    w_dt = jnp.pad(w["w_in_t"][seg_at["dt"]:seg_at["gates"]], ((0, CHUNK - 2 * SSD_HEADS), (0, 0)))
    w_gates = w["w_in_t"][seg_at["gates"]:]

    def norm_cast(name, h, nw):
        return _rows(name, lambda i, hv, wv: (_rms(hv, wv),), T, 1, [(h, D_MODEL, c0)], [(nw, D_MODEL, c0)],
                     [(D_MODEL, D_MODEL, c0, BF16)], tall=True)[0]

    u = norm_cast("norm_mix", h0, w["norm_mix_w"] + tick)
    p_main = _mm("proj_main", u, w_main, "nt", out_dtype=BF16)
    p_dt = _mm("proj_dt", u, w_dt, "nt")
    p_gates = _mm("proj_gates", u, w_gates, "nt", out_dtype=BF16)

    def seg(name, width, cf=c0):
        base = seg_at[name] // width
        return (p_main, width, lambda j: base + cf(j))

    cs, sn = _rope_tables(T)
    scale = RET_QK_DIM ** -0.5

    def rot_fn(i, qk, csv, snv):
        q = [_rot(qk[:, h * 128:(h + 1) * 128], csv, snv) for h in range(RET_HEADS)]
        k = [_rot(qk[:, (RET_HEADS + h) * 128:(RET_HEADS + h + 1) * 128], csv, snv) * scale for h in range(RET_HEADS)]
        return jnp.concatenate(q, axis=1), jnp.concatenate(k, axis=1)

    qr, kr = _rows("rotary", rot_fn, T, 1, [seg("qk", 1024), (cs, 128, c0), (sn, 128, c0)], [],
                   [(512, 512, c0, F32), (512, 512, c0, F32)], tall=True)
    v_at = (p_main, seg_at["v"])
    y_ret = _retention("retention", qr, kr, v_at, T, RET_QK_DIM, RET_V_DIM)
    a_ret = _rows("ret_post", lambda i, y, g, gw: (_ret_post(y, g, gw),), T, 1,
                  [(y_ret, 1024, c0), seg("g", 1024)], [(w["ret_gn_w"], 1024, c0)],
                  [(1024, 1024, c0, BF16)], tall=True)[0]

    conv_w = {"xs": w["w_ssd_conv"][:, :2048], "B": w["w_ssd_conv"][:, 2048:2560], "C": w["w_ssd_conv"][:, 2560:]}
    conv_b = {"xs": w["b_ssd_conv"][:, :2048], "B": w["b_ssd_conv"][:, 2048:2560], "C": w["b_ssd_conv"][:, 2560:]}

    def ssd_conv_fn(i, xe, cw, cb):
        r = _row_ids(i, T, True, tall=True)
        return (_center(jnp.where(r >= PAD_ROWS, _silu(_conv3(xe, cw) + cb), 0.0)),)

    act = {}
    for name in ("xs", "B", "C"):
        wd = conv_w[name].shape[1]
        cw = 512
        act[name] = _rows("ssd_conv_" + name, ssd_conv_fn, T, wd // cw, [seg(name, cw, lambda j: j)],
                          [(conv_w[name], cw, lambda j: j), (conv_b[name], cw, lambda j: j)],
                          [(wd, cw, lambda j: j, BF16)], halo=True, tall=True)[0]

    raw = p_dt[:, :2 * SSD_HEADS].reshape(T, 2, SSD_GROUPS, HEADS_PER_GROUP)
    rawc = raw.transpose(1, 2, 0, 3)
    rawr = raw.transpose(1, 2, 3, 0)
    bias = [_per_group(w["dt_bias_f"]), _per_group(w["dt_bias_b"])]
    alog = [_per_group(w["a_log_f"]), _per_group(w["a_log_b"])]
    small = (rawc, rawr, jnp.stack([bias[0][0], bias[1][0]]), jnp.stack([bias[0][1], bias[1][1]]),
             jnp.stack([alog[0][0], alog[1][0]]), jnp.stack([alog[0][1], alog[1][1]]))
    y_dir, states = _ssd_fwd(act["xs"], act["B"], act["C"], small, T)

    dskip_e = jnp.repeat(w["d_skip"], SSD_HEAD_DIM, axis=1)
    gcol = lambda j: j
    gw_ = 512
    a_ssd = _rows("ssd_post", lambda i, yf, yb, xv, zv, dk, nw: (_ssd_post(yf, yb, xv, zv, dk, nw),), T, SSD_GROUPS,
                  [(y_dir[0], gw_, gcol), (y_dir[1], gw_, gcol), (act["xs"], gw_, gcol), seg("z", gw_, gcol)],
                  [(dskip_e, gw_, gcol), (w["ssd_norm_w"], gw_, gcol)], [(2048, gw_, gcol, BF16)], tall=True)[0]

    w = dict(w, **late_weights(a_ssd))
    w_up_g, w_up_u = w["w_ffn_up_t"][:D_FF], w["w_ffn_up_t"][D_FF:]
    y_ret_o = _mm("ret_out", a_ret, w["w_ret_out"], "nn", out_dtype=BF16)
    y_ssd_o = _mm("ssd_out", a_ssd, w["w_ssd_out"], "nn", out_dtype=BF16)

    def merge_fn(i, gates, yr, ys):
        return (_merge(gates, yr, ys, _row_ids(i, T) >= PAD_ROWS),)

    merged = _rows("merge", merge_fn, T, 1, [(p_gates, 2048, c0), (y_ret_o, 1024, c0), (y_ssd_o, 1024, c0)], [],
                   [(1024, 1024, c0, BF16)])[0]
    h1 = _mm("mix_out", merged, w["w_out"], "nn", add=h0)

    n2 = norm_cast("norm_ffn", h1, w["norm_ffn_w"])
    f_pre = _mm("ffn_up", n2, w["w_ffn_up_t"], "nt", out_dtype=BF16)
    cwg, cwu = w["w_ffn_conv"][:, :D_FF], w["w_ffn_conv"][:, D_FF:]
    cbg, cbu = w["b_ffn_conv"][:, :D_FF], w["b_ffn_conv"][:, D_FF:]
    fcol = lambda j: j
    fw = 256

    def ffn_act_fn(i, ge, ue, wg, wu, bg, bu):
        return (_center(_silu(_conv3(ge, wg) + bg) * (_conv3(ue, wu) + bu)),)

    ucol = lambda j: D_FF // fw + j
    a2 = _rows("ffn_act", ffn_act_fn, T, D_FF // fw, [(f_pre, fw, fcol), (f_pre, fw, ucol)],
               [(cwg, fw, fcol), (cwu, fw, fcol), (cbg, fw, fcol), (cbu, fw, fcol)], [(D_FF, fw, fcol, BF16)],
               halo=True, tall=True)[0]
    h2 = _mm("ffn_down", a2, w["w_ffn_down"], "nn", add=h1)

    fnw = w["final_norm_w"].reshape(1, D_MODEL)

    per_tile = tm // CHUNK
    tgt_specs = [(target, D_MODEL, c0, None, (CHUNK, lambda i, k=k: jnp.maximum(per_tile * i - 1 + k, 0)))
                 for k in range(per_tile)]

    def loss_fn(i, hv, *rest):
        tv, nw = jnp.concatenate(rest[:per_tile], axis=0), rest[per_tile]
        valid = _row_ids(i, T) >= CHUNK
        y, vjp = jax.vjp(_rms, hv, nw)
        diff = jnp.where(valid, y - tv, 0.0)
        dh, dw = vjp(diff * (1.0 / D_MODEL))
        part = 0.5 / D_MODEL * jnp.sum(jnp.sum(diff * diff, axis=1, keepdims=True), axis=0, keepdims=True)
        return dh, jnp.broadcast_to(part, (1, 128)), dw

    dh2, loss_acc, d_fnw = _rows("loss", loss_fn, T, 1, [(h2, D_MODEL, c0)] + tgt_specs, [(fnw, D_MODEL, c0)],
                                 [(D_MODEL, D_MODEL, c0, F32)], [(1, 128, 128, c0), (1, D_MODEL, D_MODEL, c0)])
    loss = loss_acc[0, 0]
    grads = {"final_norm_w": d_fnw.reshape(D_MODEL)}

    da2 = _mm("d_ffn_act", dh2, w["w_ffn_down"], "nt", out_dtype=BF16)
    grads["w_ffn_down"] = _mm("g_ffn_down", a2, dh2, "tn", out_dtype=BF16)

    def ffn_bwd_fn(i, ge, ue, de, wg, wu, bg, bu):
        fg = _conv3(ge, wg) + bg
        fu = _conv3(ue, wu) + bu
        sg = jax.nn.sigmoid(fg)
        dfg = de * fu * (sg * (1.0 + fg * (1.0 - sg)))
        dfu = de * (fg * sg)
        n = ge.shape[0]

        def wgrad(df, xe):
            df_c = _center(df)
            return jnp.concatenate([jnp.sum(df_c * _center(pltpu.roll(xe, 1, 0)), axis=0, keepdims=True),
                                    jnp.sum(df_c * _center(xe), axis=0, keepdims=True),
                                    jnp.sum(df_c * _center(pltpu.roll(xe, n - 1, 0)), axis=0, keepdims=True)], axis=0)

        return (_center(_conv3_t(dfg, wg)), _center(_conv3_t(dfu, wu)), wgrad(dfg, ge), wgrad(dfu, ue),
                jnp.sum(_center(dfg), axis=0, keepdims=True), jnp.sum(_center(dfu), axis=0, keepdims=True))

    dfg_pre, dfu_pre, g_cwg, g_cwu, g_cbg, g_cbu = _rows(
        "ffn_act_bwd", ffn_bwd_fn, T, D_FF // fw, [(f_pre, fw, fcol), (f_pre, fw, ucol), (da2, fw, fcol)],
        [(cwg, fw, fcol), (cwu, fw, fcol), (cbg, fw, fcol), (cbu, fw, fcol)],
        [(D_FF, fw, fcol, BF16), (D_FF, fw, fcol, BF16)],
        [(3, D_FF, fw, fcol), (3, D_FF, fw, fcol), (1, D_FF, fw, fcol), (1, D_FF, fw, fcol)], halo=True, tall=True)
    grads["w_ffn_conv"] = jnp.concatenate([g_cwg, g_cwu], axis=1)
    grads["b_ffn_conv"] = jnp.concatenate([g_cbg, g_cbu], axis=1)
    dn2 = _mm("d_norm_ffn_g", dfg_pre, w_up_g, "nn")
    dn2 = _mm("d_norm_ffn_u", dfu_pre, w_up_u, "nn", add=dn2)
    grads["w_ffn_up_t"] = jnp.concatenate([_mm("g_ffn_up_g", dfg_pre, n2, "tn", out_dtype=BF16), _mm("g_ffn_up_u", dfu_pre, n2, "tn", out_dtype=BF16)],
                                          axis=0)

    def norm_bwd(name, h, nw, dn, dres):
        def fn(i, hv, dnv, drv, wv):
            _, vjp = jax.vjp(_rms, hv, wv)
            dh, dw = vjp(dnv)
            return dh + drv, dw
        return _rows(name, fn, T, 1, [(h, D_MODEL, c0), (dn, D_MODEL, c0), (dres, D_MODEL, c0)], [(nw, D_MODEL, c0)],
                     [(D_MODEL, D_MODEL, c0, F32)], [(1, D_MODEL, D_MODEL, c0)])

    dh1, grads["norm_ffn_w"] = norm_bwd("norm_ffn_bwd", h1, w["norm_ffn_w"], dn2, dh2)

    dmerged = _mm("d_merged", dh1, w["w_out"], "nt", out_dtype=BF16)
    grads["w_out"] = _mm("g_out", merged, dh1, "tn", out_dtype=BF16)

    def merge_bwd_fn(i, gates, yr, ys, dm):
        valid = _row_ids(i, T) >= PAD_ROWS
        _, vjp = jax.vjp(lambda a, b, c: _merge(a, b, c, valid), gates, yr, ys)
        return vjp(dm)

    dgates, dyr, dys = _rows("merge_bwd", merge_bwd_fn, T, 1,
                             [(p_gates, 2048, c0), (y_ret_o, 1024, c0), (y_ssd_o, 1024, c0), (dmerged, 1024, c0)],
                             [], [(2048, 2048, c0, BF16), (1024, 1024, c0, BF16), (1024, 1024, c0, BF16)])
    dproj = {"gates": dgates}

    da_ssd = _mm("d_ssd_act", dys, w["w_ssd_out"], "nt", out_dtype=BF16)
    grads["w_ssd_out"] = _mm("g_ssd_out", a_ssd, dys, "tn", out_dtype=BF16)

    def ssd_post_bwd_fn(i, yf, yb, xv, zv, da, dk, nw):
        _, vjp = jax.vjp(_ssd_post, yf, yb, xv, zv, dk, nw)
        dyf, _, dxv, dzv, ddk, dnw = vjp(da)
        return dyf, dxv, dzv, ddk, dnw

    d_main = lax.empty(p_main.shape, BF16)

    def into_main(name, width, cf=c0):
        base = seg_at[name] // width
        return (d_main, width, lambda j: base + cf(j), BF16)

    dy_ssd, dxs_skip, d_main, g_dskip_e, grads["ssd_norm_w"] = _rows(
        "ssd_post_bwd", ssd_post_bwd_fn, T, SSD_GROUPS,
        [(y_dir[0], gw_, gcol), (y_dir[1], gw_, gcol), (act["xs"], gw_, gcol), seg("z", gw_, gcol),
         (da_ssd, gw_, gcol)],
        [(dskip_e, gw_, gcol), (w["ssd_norm_w"], gw_, gcol)],
        [(2048, gw_, gcol, BF16), (2048, gw_, gcol, BF16), into_main("z", gw_, gcol)],
        [(1, 2048, gw_, gcol), (1, 2048, gw_, gcol)], tall=True)
    grads["d_skip"] = g_dskip_e.reshape(SSD_HEADS, SSD_HEAD_DIM).sum(axis=1).reshape(1, SSD_HEADS)

    dxs_dir, db_dir, dc_dir, draw, g_bias, g_alog = _ssd_bwd(act["xs"], act["B"], act["C"], small, states, dy_ssd, T)
    grads["dt_bias_f"], grads["dt_bias_b"] = g_bias[0].reshape(1, SSD_HEADS), g_bias[1].reshape(1, SSD_HEADS)
    grads["a_log_f"], grads["a_log_b"] = g_alog[0].reshape(1, SSD_HEADS), g_alog[1].reshape(1, SSD_HEADS)
    d_dt = jnp.stack(draw).transpose(2, 0, 1, 3).reshape(T, 2 * SSD_HEADS)
    dproj["dt"] = jnp.pad(d_dt, ((0, 0), (0, CHUNK - 2 * SSD_HEADS))).astype(BF16)

    def make_conv_bwd(nsum):
        def fn(i, xe, *rest):
            ds, (cw, cb) = rest[:nsum], rest[nsum:]
            r = _row_ids(i, T, True, tall=True)
            dact = ds[0]
            for t in ds[1:]:
                dact = dact + t
            dact = jnp.where(r >= PAD_ROWS, dact, 0.0)
            pre = _conv3(xe, cw) + cb
            sg = jax.nn.sigmoid(pre)
            dpre = dact * (sg * (1.0 + pre * (1.0 - sg)))
            n = xe.shape[0]
            dpc = _center(dpre)
            dw = jnp.concatenate([jnp.sum(dpc * _center(pltpu.roll(xe, 1, 0)), axis=0, keepdims=True),
                                  jnp.sum(dpc * _center(xe), axis=0, keepdims=True),
                                  jnp.sum(dpc * _center(pltpu.roll(xe, n - 1, 0)), axis=0, keepdims=True)], axis=0)
            return _center(_conv3_t(dpre, cw)), dw, jnp.sum(dpc, axis=0, keepdims=True)
        return fn

    g_cw, g_cb = {}, {}
    cots = {"xs": [(dxs_dir[0], 512, gcol), (dxs_dir[1], 512, gcol), (dxs_skip, 512, gcol)],
            "B": [(db_dir[0], 512, gcol), (db_dir[1], 512, gcol)],
            "C": [(dc_dir[0], 512, gcol), (dc_dir[1], 512, gcol)]}
    for name in ("xs", "B", "C"):
        wd = conv_w[name].shape[1]
        d_main, g_cw[name], g_cb[name] = _rows(
            "ssd_conv_bwd_" + name, make_conv_bwd(len(cots[name])), T, wd // 512,
            [seg(name, 512, gcol)] + cots[name], [(conv_w[name], 512, gcol), (conv_b[name], 512, gcol)],
            [into_main(name, 512, gcol)], [(3, wd, 512, gcol), (1, wd, 512, gcol)], halo=True, tall=True)
    grads["w_ssd_conv"] = jnp.concatenate([g_cw["xs"], g_cw["B"], g_cw["C"]], axis=1)
    grads["b_ssd_conv"] = jnp.concatenate([g_cb["xs"], g_cb["B"], g_cb["C"]], axis=1)

    da_ret = _mm("d_ret_act", dyr, w["w_ret_out"], "nt", out_dtype=BF16)
    grads["w_ret_out"] = _mm("g_ret_out", a_ret, dyr, "tn", out_dtype=BF16)
    tick = early_grads({n: grads.pop(n) for n in ("w_ffn_up_t", "w_ret_out", "w_ssd_out", "w_out", "w_ffn_down")})

    def ret_post_bwd_fn(i, y, g, da, gw):
        _, vjp = jax.vjp(_ret_post, y, g, gw)
        return vjp(da)

    dy_ret, d_main, grads["ret_gn_w"] = _rows(
        "ret_post_bwd", ret_post_bwd_fn, T, 1, [(y_ret, 1024, c0), seg("g", 1024), (da_ret, 1024, c0)],
        [(w["ret_gn_w"] + tick, 1024, c0)], [(1024, 1024, c0, BF16), into_main("g", 1024)], [(1, 1024, 1024, c0)])
    d_main = _retention("retention_dv", kr, qr, dy_ret, T, RET_QK_DIM, RET_V_DIM, into=(d_main, seg_at["v"]))
    dqr = _retention("retention_dq", dy_ret, v_at, kr, T, RET_V_DIM, RET_QK_DIM)
    dkr = _retention("retention_dk", v_at, dy_ret, qr, T, RET_V_DIM, RET_QK_DIM)

    def rot_bwd_fn(i, dq, dk, csv, snv):
        parts = [_rot_t(dq[:, h * 128:(h + 1) * 128], csv, snv) for h in range(RET_HEADS)]
        parts += [_rot_t(dk[:, h * 128:(h + 1) * 128] * scale, csv, snv) for h in range(RET_HEADS)]
        return (jnp.concatenate(parts, axis=1),)

    d_main = _rows("rotary_bwd", rot_bwd_fn, T, 1, [(dqr, 512, c0), (dkr, 512, c0), (cs, 128, c0), (sn, 128, c0)],
                   [], [into_main("qk", 1024)], tall=True)[0]

    g_in = [_mm("g_in_main", d_main, u, "tn", out_dtype=BF16),
            _mm("g_in_dt", dproj["dt"], u, "tn", out_dtype=BF16)[:2 * SSD_HEADS],
            _mm("g_in_gates", dproj["gates"], u, "tn", out_dtype=BF16)]
    tick = in_grads(jnp.concatenate(g_in, axis=0))
    du = _mm("d_u_dt", dproj["dt"] + tick.astype(BF16), w_dt, "nn")
    du = _mm("d_u_main", d_main, w_main, "nn", add=du)
    du = _mm("d_u_gates", dproj["gates"], w_gates, "nn", add=du)
    dh0, grads["norm_mix_w"] = norm_bwd("norm_mix_bwd", h0, w["norm_mix_w"], du, dh1)
    grads["meta_tokens"] = dh0[PAD_ROWS:CHUNK]
    return loss, dh0[CHUNK:], grads


MESH_ID = pl.DeviceIdType.MESH
ANY = pl.BlockSpec(memory_space=pl.ANY)


def _me_and_peers():
    x, y, c = lax.axis_index("x"), lax.axis_index("y"), lax.axis_index("c")
    peers = []
    for k in range(1, N_DEV):
        px = 1 - x if k & 4 else x
        py = 1 - y if k & 2 else y
        pc = 1 - c if k & 1 else c
        peers.append(((px, py, pc), 4 * px + 2 * py + pc))
    return 4 * x + 2 * y + c, peers


def _push_blocks(name, src, per_peer):
    blk = src.shape[1:] if per_peer else src.shape

    def body(src_ref, out_ref, send_sems, recv_sems, local_sem):
        me, peers = _me_and_peers()
        mine = src_ref.at[me] if per_peer else src_ref
        local = pltpu.make_async_copy(mine, out_ref.at[me], local_sem)
        local.start()
        sends = []
        for k, (dev, idx) in enumerate(peers):
            cp = pltpu.make_async_remote_copy(
                src_ref=src_ref.at[idx] if per_peer else src_ref, dst_ref=out_ref.at[me],
                send_sem=send_sems.at[k], recv_sem=recv_sems.at[k], device_id=dev, device_id_type=MESH_ID)
            cp.start()
            sends.append(cp)
        for k, (dev, idx) in enumerate(peers):
            pltpu.make_async_remote_copy(
                src_ref=mine, dst_ref=out_ref.at[idx], send_sem=send_sems.at[k], recv_sem=recv_sems.at[k],
                device_id=dev, device_id_type=MESH_ID).wait_recv()
        for cp in sends:
            cp.wait_send()
        local.wait()

    return pl.pallas_call(
        body, name=name, in_specs=[ANY], out_specs=ANY,
        out_shape=jax.ShapeDtypeStruct((N_DEV,) + tuple(blk), src.dtype),
        scratch_shapes=[pltpu.SemaphoreType.DMA((N_DEV - 1,)), pltpu.SemaphoreType.DMA((N_DEV - 1,)),
                        pltpu.SemaphoreType.DMA],
    )(src)


def _gather_two_level(name, src):
    def body(x_ref, out_ref, send_sems, recv_sems, local_sem):
        x, y, c = lax.axis_index("x"), lax.axis_index("y"), lax.axis_index("c")
        me, sibling = (x, y, c), (x, y, 1 - c)
        chips = [(1 - x, y), (x, 1 - y), (1 - x, 1 - y)]

        def rows(px, py, pc):
            return out_ref.at[4 * px + 2 * py + pc]

        def copy(k, block, to, src_ref=None):
            return pltpu.make_async_remote_copy(
                src_ref=rows(*block) if src_ref is None else src_ref, dst_ref=rows(*block),
                send_sem=send_sems.at[k], recv_sem=recv_sems.at[k], device_id=to, device_id_type=MESH_ID)

        mine = pltpu.make_async_copy(x_ref, rows(*me), local_sem)
        mine.start()
        first = [copy(0, me, sibling, x_ref)] + [copy(1 + j, me, (*chip, c), x_ref) for j, chip in enumerate(chips)]
        for cp in first:
            cp.start()
        passed = [copy(4 + j, (*chip, c), sibling) for j, chip in enumerate(chips)]
        for j, chip in enumerate(chips):
            copy(1 + j, (*chip, c), me).wait_recv()
            passed[j].start()
        copy(0, sibling, me).wait_recv()
        for j, chip in enumerate(chips):
            copy(4 + j, (*chip, 1 - c), me).wait_recv()
        for cp in first + passed:
            cp.wait_send()
        mine.wait()

    return pl.pallas_call(
        body, name=name, in_specs=[ANY], out_specs=ANY,
        out_shape=jax.ShapeDtypeStruct((N_DEV,) + tuple(src.shape), src.dtype),
        scratch_shapes=[pltpu.SemaphoreType.DMA((N_DEV - 1,)), pltpu.SemaphoreType.DMA((N_DEV - 1,)),
                        pltpu.SemaphoreType.DMA],
    )(src)


HBM = pl.BlockSpec(memory_space=pltpu.HBM)
SEM = pl.BlockSpec(memory_space=pltpu.SEMAPHORE)
EFFECT = pltpu.SideEffectType.DATAFLOW_SIDE_EFFECTING


def _peer_copy(src_ref, land_ref, send_sems, recv_sems, per_peer, me, a, k, dev, idx, receiving):
    s = a * (N_DEV - 1) + k
    return pltpu.make_async_remote_copy(
        src_ref=src_ref.at[idx] if per_peer else src_ref, dst_ref=land_ref.at[idx if receiving else me],
        send_sem=send_sems.at[s], recv_sem=recv_sems.at[s], device_id=dev, device_id_type=MESH_ID)


def _push_start(name, srcs, per_peer):
    n = len(srcs)
    land_shapes = [(N_DEV,) + tuple(s.shape[1:] if per_peer else s.shape) for s in srcs]

    def body(*refs):
        src_refs, land_refs, send_sems, recv_sems, token = refs[:n], refs[n:2 * n], refs[2 * n], refs[2 * n + 1], refs[-1]
        me, peers = _me_and_peers()
        for a in range(n):
            for k, (dev, idx) in enumerate(peers):
                _peer_copy(src_refs[a], land_refs[a], send_sems, recv_sems, per_peer, me, a, k, dev, idx, False).start()
        token[...] = jnp.zeros_like(token)

    sems = pltpu.SemaphoreType.DMA((n * (N_DEV - 1),))
    res = pl.pallas_call(
        body, name=name,
        out_shape=(sems, sems, *[pltpu.HBM(s.shape, s.dtype) for s in srcs],
                   *[pltpu.HBM(ls, s.dtype) for ls, s in zip(land_shapes, srcs)], jax.ShapeDtypeStruct((8, 128), F32)),
        in_specs=(HBM,) * (2 * n), out_specs=(SEM, SEM) + (HBM,) * (2 * n) + (pl.BlockSpec(memory_space=pltpu.VMEM),),
        input_output_aliases={i: 2 + i for i in range(2 * n)},
        compiler_params=pltpu.CompilerParams(has_side_effects=EFFECT),
    )(*[pltpu.with_memory_space_constraint(s, pltpu.HBM) for s in srcs],
      *[pltpu.with_memory_space_constraint(lax.empty(ls, s.dtype), pltpu.HBM) for ls, s in zip(land_shapes, srcs)])
    return res[0], res[1], res[2:2 + n], res[2 + n:2 + 2 * n], res[-1]


def _push_wait(name, send_sems, recv_sems, srcs_thru, lands_thru, after, per_peer):
    n = len(srcs_thru)

    def body(*refs):
        src_refs, land_refs, send_sems, recv_sems = refs[:n], refs[n:2 * n], refs[2 * n], refs[2 * n + 1]
        me, peers = _me_and_peers()
        for a in range(n):
            for k, (dev, idx) in enumerate(peers):
                cp = _peer_copy(src_refs[a], land_refs[a], send_sems, recv_sems, per_peer, me, a, k, dev, idx, True)
                cp.wait_send()
                cp.wait_recv()

    both = list(srcs_thru) + list(lands_thru)
    res = pl.pallas_call(
        body, name=name, out_shape=tuple(pltpu.HBM(t.shape, t.dtype) for t in both),
        in_specs=(HBM,) * (2 * n) + (SEM, SEM, ANY), out_specs=(HBM,) * (2 * n),
        input_output_aliases={i: i for i in range(2 * n)},
        compiler_params=pltpu.CompilerParams(has_side_effects=EFFECT),
    )(*both, send_sems, recv_sems, after)
    return res[:n], res[n:]


def _sum_blocks(name, blocks):
    _, R, C = blocks.shape
    tc = next(t for t in (1024, 512, 256, 128) if C % t == 0 and (N_DEV * R * t * 2 <= 6 * 2 ** 20 or t == 128))

    def body(b_ref, o_ref):
        acc = b_ref[0].astype(F32)
        for k in range(1, N_DEV):
            acc = acc + b_ref[k].astype(F32)
        o_ref[...] = acc

    return pl.pallas_call(
        body, name=name, grid=(C // tc,), in_specs=[pl.BlockSpec((N_DEV, R, tc), lambda j: (0, 0, j))],
        out_specs=pl.BlockSpec((R, tc), lambda j: (0, j)), out_shape=jax.ShapeDtypeStruct((R, C), F32),
        compiler_params=_params(("arbitrary",)),
    )(blocks)


def _adamw(name, w, g, m, v):
    R, C = w.shape
    tr = R if R <= 512 else _pick(R, (256, 184, 176, 128, 8))
    spec = pl.BlockSpec((tr, C), lambda i: (i, 0))

    def body(w_ref, g_ref, m_ref, v_ref, d_ref, mo_ref, vo_ref):
        gv = g_ref[...]
        mn = ADAM_B1 * m_ref[...] + (1.0 - ADAM_B1) * gv
        vn = ADAM_B2 * v_ref[...] + (1.0 - ADAM_B2) * jnp.square(gv)
        m_hat = mn / (1.0 - ADAM_B1 ** ADAM_STEP)
        v_hat = vn / (1.0 - ADAM_B2 ** ADAM_STEP)
        d_ref[...] = -ADAM_LR * (m_hat / (jnp.sqrt(v_hat) + ADAM_EPS) + ADAM_WD * w_ref[...])
        mo_ref[...] = mn
        vo_ref[...] = vn

    return pl.pallas_call(
        body, name=name, grid=(R // tr,), in_specs=[spec] * 4, out_specs=[spec] * 3,
        out_shape=[jax.ShapeDtypeStruct((R, C), F32)] * 3, compiler_params=_params(("arbitrary",)),
    )(w, g, m, v)


WEIGHTS = ("meta_tokens", "norm_mix_w", "w_in", "ret_gn_w", "w_ret_out", "w_ssd_conv", "b_ssd_conv", "dt_bias_f",
           "dt_bias_b", "a_log_f", "a_log_b", "d_skip", "ssd_norm_w", "w_ssd_out", "w_out", "norm_ffn_w", "w_ffn_up",
           "w_ffn_conv", "b_ffn_conv", "w_ffn_down", "final_norm_w")
BIG = (("w_in", 1288, True), ("w_ffn_up", 704, True), ("w_ret_out", 128, False), ("w_ssd_out", 256, False),
       ("w_out", 128, False), ("w_ffn_down", 352, False))
REPLICATED = ("norm_mix_w", "ret_gn_w", "b_ssd_conv", "dt_bias_f", "dt_bias_b", "a_log_f", "a_log_b", "d_skip",
              "ssd_norm_w", "norm_ffn_w", "b_ffn_conv", "final_norm_w")
SMALL_SHARDED = (("meta_tokens", 16, 1024), ("w_ssd_conv", 3, 3072), ("w_ffn_conv", 3, 5632))


BIG_IN, BIG_REST = BIG[:1], BIG[1:]


def _pack_flat(arrays, rows):
    flat = jnp.concatenate([a.reshape(-1) for a in arrays])
    return jnp.pad(flat, (0, rows * D_MODEL - flat.shape[0])).reshape(rows, D_MODEL)


def _unpack_flat(slab, shapes):
    flat, out, o = slab.reshape(-1), [], 0
    for s in shapes:
        n = math.prod(s)
        out.append(flat[o:o + n].reshape(s))
        o += n
    return out


def kernel(x, meta_tokens, norm_mix_w, w_in, ret_gn_w, w_ret_out, w_ssd_conv, b_ssd_conv, dt_bias_f, dt_bias_b, a_log_f, a_log_b, d_skip, ssd_norm_w, w_ssd_out, w_out, norm_ffn_w, w_ffn_up, w_ffn_conv, b_ffn_conv, w_ffn_down, final_norm_w, loss_target, m_meta_tokens, m_norm_mix_w, m_w_in, m_ret_gn_w, m_w_ret_out, m_w_ssd_conv, m_b_ssd_conv, m_dt_bias_f, m_dt_bias_b, m_a_log_f, m_a_log_b, m_d_skip, m_ssd_norm_w, m_w_ssd_out, m_w_out, m_norm_ffn_w, m_w_ffn_up, m_w_ffn_conv, m_b_ffn_conv, m_w_ffn_down, m_final_norm_w, v_meta_tokens, v_norm_mix_w, v_w_in, v_ret_gn_w, v_w_ret_out, v_w_ssd_conv, v_b_ssd_conv, v_dt_bias_f, v_dt_bias_b, v_a_log_f, v_a_log_b, v_d_skip, v_ssd_norm_w, v_w_ssd_out, v_w_out, v_norm_ffn_w, v_w_ffn_up, v_w_ffn_conv, v_b_ffn_conv, v_w_ffn_down, v_final_norm_w):
    given = dict(locals())
    wt = {n: given[n] for n in WEIGHTS}
    mt = {n: given["m_" + n] for n in WEIGHTS}
    vt = {n: given["v_" + n] for n in WEIGHTS}
    me = 4 * lax.axis_index("x") + 2 * lax.axis_index("y") + lax.axis_index("c")

    small_names = [n for n, _, _ in SMALL_SHARDED]
    small_local = lambda tree: [tree[n].reshape(r, c // N_DEV) for n, r, c in SMALL_SHARDED]
    slab_view = lambda tree, name, transposed: tree[name][0].T if transposed else tree[name][0]
    all_in = _gather_two_level("gather_w_in", slab_view(wt, "w_in", True).astype(BF16))
    all_s = _push_blocks("gather_small", _pack_flat(small_local(wt), 8), False)
    rest_srcs = [slab_view(wt, name, t).astype(BF16) for name, _, t in BIG_REST]
    rest_srcs, all_in, all_s = lax.optimization_barrier((rest_srcs, all_in, all_s))
    rest_flight = _push_start("gather_rest_start", rest_srcs, False)
    all_s = all_s.reshape(N_DEV, -1)
    full = {"w_in_t": all_in.reshape(-1, D_MODEL)}

    def lands_with_own(flight, after, per_peer, name):
        srcs, lands = _push_wait(name, *flight[:4], after, per_peer)
        own = lambda s: lax.dynamic_slice_in_dim(s, me, 1, axis=0) if per_peer else s[None]
        return [lax.dynamic_update_slice_in_dim(land, own(s), me, axis=0) for s, land in zip(srcs, lands)]

    def late_weights(after):
        lands = lands_with_own(rest_flight, after, False, "gather_rest_wait")
        return {name + ("_t" if t else ""): land.reshape(N_DEV * r, D_MODEL) for (name, r, t), land in zip(BIG_REST, lands)}

    flights = {}

    def start_exchange(key, group, gd):
        srcs = [gd[name + ("_t" if t else "")].astype(BF16).reshape(N_DEV, r, D_MODEL) for name, r, t in group]
        flights[key] = _push_start("exchange_" + key + "_start", srcs, True)
        return flights[key][4][0, 0]

    o = 0
    for name, r, c in SMALL_SHARDED:
        n = r * c // N_DEV
        full[name] = all_s[:, o:o + n].reshape(N_DEV, r, c // N_DEV).transpose(1, 0, 2).reshape(r, c)
        o += n
    for name in REPLICATED:
        full[name] = wt[name]

    grads, delta, new_m, new_v = {}, {}, {}, {}

    def finish_exchange(key, group, after):
        lands = lands_with_own(flights[key], after, True, "exchange_" + key + "_wait")
        for (name, _, transposed), land in zip(group, lands):
            back = (lambda a: a.T[None]) if transposed else (lambda a: a[None])
            g_sum = _sum_blocks("sum_" + name, land)
            d, mn, vn = _adamw("adamw_" + name, slab_view(wt, name, transposed), g_sum,
                               slab_view(mt, name, transposed), slab_view(vt, name, transposed))
            grads[name], delta[name], new_m[name], new_v[name] = back(g_sum), back(d), back(mn), back(vn)

    def in_grads(gi):
        tick = start_exchange("in", BIG_IN, {"w_in_t": gi})
        finish_exchange("rest", BIG_REST, flights["in"][4])
        tick, _ = lax.optimization_barrier((tick, [delta[name] for name, _, _ in BIG_REST]))
        return tick

    loss, grad_x, g = _local_step(x[0], loss_target[0], full, rest_flight[4][0, 0], late_weights,
                                  lambda gd: start_exchange("rest", BIG_REST, gd), in_grads)

    small_parts = [g[n] for n in REPLICATED] + [g[n] for n in small_names] + [loss.reshape(1)]
    small_flight = _push_start("gather_small_grads_start", [_pack_flat(small_parts, 64)], False)
    finish_exchange("in", BIG_IN, small_flight[4])
    g_small = _sum_blocks("sum_small", lands_with_own(small_flight, delta["w_in"], False, "gather_small_grads_wait")[0])
    small_red = _unpack_flat(g_small, [wt[n].shape for n in REPLICATED] + [(r, c) for _, r, c in SMALL_SHARDED] + [(1,)])
    grads.update(zip(REPLICATED, small_red[:len(REPLICATED)]))
    for (name, r, c), red in zip(SMALL_SHARDED, small_red[len(REPLICATED):-1]):
        grads[name] = lax.dynamic_slice(red, (0, me * (c // N_DEV)), (r, c // N_DEV)).reshape(wt[name].shape)
    loss_all = small_red[-1][0]

    rest = list(REPLICATED) + small_names
    shapes = [wt[n].shape for n in rest]
    pack_rest = lambda tree: _pack_flat([tree[n] for n in rest], 24)
    d_rest, m_rest, v_rest = _adamw("adamw_small", pack_rest(wt), pack_rest(grads), pack_rest(mt), pack_rest(vt))
    delta.update(zip(rest, _unpack_flat(d_rest, shapes)))
    new_m.update(zip(rest, _unpack_flat(m_rest, shapes)))
    new_v.update(zip(rest, _unpack_flat(v_rest, shapes)))

    return (loss_all, grad_x[None], *[grads[n] for n in WEIGHTS], *[delta[n] for n in WEIGHTS],
            *[new_m[n] for n in WEIGHTS], *[new_v[n] for n in WEIGHTS])
```

```python
import functools
import math

import jax
import jax.numpy as jnp
from jax import lax
from jax.experimental import pallas as pl
from jax.experimental.pallas import tpu as pltpu

F32 = jnp.float32
BF16 = jnp.bfloat16

D_MODEL = 1024
CHUNK = 128
N_META = 16
PAD_ROWS = CHUNK - N_META
RET_HEADS = 4
RET_QK_DIM = 128
RET_V_DIM = 256
SSD_HEADS = 32
SSD_HEAD_DIM = 64
SSD_GROUPS = 4
SSD_STATE = 128
HEADS_PER_GROUP = SSD_HEADS // SSD_GROUPS
PAIRS_PER_GROUP = HEADS_PER_GROUP // 2
D_FF = 2816
EPS = 1e-6
ROPE_BASE = 10000.0
N_DEV = 8

ADAM_LR = 0.001
ADAM_B1 = 0.9
ADAM_B2 = 0.999
ADAM_EPS = 1e-08
ADAM_WD = 0.01
ADAM_STEP = 10

VMEM_LIMIT = 56 * 1024 * 1024
HALO = 16
HIGHEST = lax.Precision.HIGHEST

SEGMENTS = (("qk", 0, 1024), ("v", 1024, 2048), ("g", 2048, 3072), ("z", 3072, 5120), ("xs", 5120, 7168),
            ("B", 7168, 7680), ("C", 7680, 8192), ("dt", 8192, 8256), ("gates", 8256, 10304))


def _pick(n, cands):
    for c in cands:
        if n % c == 0:
            return c
    raise ValueError(f"no tile for {n}")


def _params(sem):
    return pltpu.CompilerParams(dimension_semantics=sem, vmem_limit_bytes=VMEM_LIMIT)


def _dot(a, b, dims=(((1,), (0,)), ((), ())), precision=None):
    return lax.dot_general(a, b, dims, preferred_element_type=F32, precision=precision)


def _dot_nt(a, b):
    return _dot(a, b, (((1,), (1,)), ((), ())))


def _dot_tn(a, b):
    return _dot(a, b, (((0,), (0,)), ((), ())))


def _mm(name, a, b, mode, add=None, out_dtype=F32):
    if mode == "nn":
        (M, K), N = a.shape, b.shape[1]
    elif mode == "nt":
        (M, K), N = a.shape, b.shape[0]
    else:
        (K, M), N = a.shape, b.shape[1]
    tn = _pick(N, (1408, 1024, 512, 128, 64))
    if mode == "tn":
        tm = M if M <= 1024 else _pick(M, (1408, 1024))
        tk = _pick(K, (2112, 512, 256, 128))
    else:
        tm = _pick(M, (1056, 512, 256, 128))
        tk = K if K <= 2816 else _pick(K, (2048, 1408, 1024))
    nk = K // tk
    if mode == "nn":
        a_spec = pl.BlockSpec((tm, tk), lambda n, m, k: (m, k))
        b_spec = pl.BlockSpec((tk, tn), lambda n, m, k: (k, n))
        dims = (((1,), (0,)), ((), ()))
    elif mode == "nt":
        a_spec = pl.BlockSpec((tm, tk), lambda n, m, k: (m, k))
        b_spec = pl.BlockSpec((tn, tk), lambda n, m, k: (n, k))
        dims = (((1,), (1,)), ((), ()))
    else:
        a_spec = pl.BlockSpec((tk, tm), lambda n, m, k: (k, m))
        b_spec = pl.BlockSpec((tk, tn), lambda n, m, k: (k, n))
        dims = (((0,), (0,)), ((), ()))
    o_spec = pl.BlockSpec((tm, tn), lambda n, m, k: (m, n))
    in_specs = [a_spec, b_spec] + ([o_spec] if add is not None else [])
    args = [a, b] + ([add] if add is not None else [])

    def body(*refs):
        if add is not None:
            a_ref, b_ref, r_ref, o_ref, acc = refs
        else:
            a_ref, b_ref, o_ref, acc = refs
        k = pl.program_id(2)
        p = _dot(a_ref[...].astype(BF16), b_ref[...].astype(BF16), dims)

        def finish(r):
            if add is not None:
                r = r + r_ref[...]
            o_ref[...] = r.astype(out_dtype)

        if nk == 1:
            finish(p)
        else:
            @pl.when(k == 0)
            def _():
                acc[...] = p

            @pl.when(k > 0)
            def _():
                acc[...] += p

            @pl.when(k == nk - 1)
            def _():
                finish(acc[...])

    return pl.pallas_call(
        body, name=name, grid=(N // tn, M // tm, nk), in_specs=in_specs, out_specs=o_spec,
        out_shape=jax.ShapeDtypeStruct((M, N), out_dtype),
        scratch_shapes=[pltpu.VMEM((tm, tn) if nk > 1 else (8, 128), F32)],
        compiler_params=_params(("arbitrary", "arbitrary", "arbitrary")),
    )(*args)


ANY_SPACE = pl.BlockSpec(memory_space=pl.ANY)


def _const(c):
    return lambda j: c


def _rows(name, fn, T, ncol, ins, params, outs, accs=(), halo=False, tall=False):
    tm = _pick(T, (1056, 512, 256, 128)) if tall else _pick(T, (384, 256, 128))
    R = T // tm
    hb = tm // HALO
    in_specs, args = [], []
    for spec in ins:
        arr, w, cf = spec[:3]
        lead = spec[3] if len(spec) > 3 else None
        if len(spec) > 4:
            rows, rf = spec[4]
            in_specs.append(pl.BlockSpec((rows, w), lambda j, i, cf=cf, rf=rf: (rf(i), cf(j))))
            args.append(arr)
            continue
        if lead is None:
            mk = lambda blk, rf, cf=cf: pl.BlockSpec(blk, lambda j, i: (rf(i), cf(j)))
            shape = lambda r, w=w: (r, w)
        else:
            mk = lambda blk, rf, cf=cf, lead=lead: pl.BlockSpec(blk, lambda j, i: (lead, rf(i), cf(j)))
            shape = lambda r, w=w: (None, r, w)
        in_specs.append(mk(shape(tm), lambda i: i))
        args.append(arr)
        if halo:
            in_specs.append(mk(shape(HALO), lambda i: jnp.maximum(i * hb - 1, 0)))
            in_specs.append(mk(shape(HALO), lambda i: jnp.minimum((i + 1) * hb, T // HALO - 1)))
            args += [arr, arr]
    for arr, w, cf in params:
        in_specs.append(pl.BlockSpec((arr.shape[0], w), lambda j, i, cf=cf: (0, cf(j))))
        args.append(arr)
    out_shape, out_specs, aliases = [], [], {}
    for k, (tw, w, cf, dt) in enumerate(outs):
        if not isinstance(tw, int):
            aliases[len(args)] = k
            in_specs.append(ANY_SPACE)
            args.append(tw)
            tw = tw.shape[1]
        out_shape.append(jax.ShapeDtypeStruct((T, tw), dt))
        out_specs.append(pl.BlockSpec((tm, w), lambda j, i, cf=cf: (i, cf(j))))
    for r, tw, w, cf in accs:
        out_shape.append(jax.ShapeDtypeStruct((r, tw), F32))
        out_specs.append(pl.BlockSpec((r, w), lambda j, i, cf=cf: (0, cf(j))))
    n_in, n_par, n_out, n_acc, n_alias = len(ins), len(params), len(outs), len(accs), len(aliases)

    def body(*refs):
        i = pl.program_id(1)
        vals, p = [], 0
        for _ in range(n_in):
            if halo:
                before = jnp.where(i > 0, refs[p + 1][...], jnp.zeros_like(refs[p + 1]))
                after = jnp.where(i < R - 1, refs[p + 2][...], jnp.zeros_like(refs[p + 2]))
                vals.append(jnp.concatenate([before, refs[p][...], after], axis=0).astype(F32))
                p += 3
            else:
                vals.append(refs[p][...].astype(F32))
                p += 1
        pvals = [refs[p + k][...] for k in range(n_par)]
        p += n_par + n_alias
        res = fn(i, *vals, *pvals)
        for k in range(n_out):
            refs[p + k][...] = res[k].astype(refs[p + k].dtype)
        p += n_out
        for k in range(n_acc):
            ref, v = refs[p + k], res[n_out + k]

            @pl.when(i == 0)
            def _(ref=ref, v=v):
                ref[...] = v

            @pl.when(i > 0)
            def _(ref=ref, v=v):
                ref[...] += v

    res = pl.pallas_call(
        body, name=name, grid=(ncol, R), in_specs=in_specs, out_specs=out_specs, out_shape=out_shape,
        input_output_aliases=aliases, compiler_params=_params(("arbitrary", "arbitrary")),
    )(*args)
    return res


def _tile_rows(T):
    return _pick(T, (384, 256, 128))


def _row_ids(i, T, halo=False, tall=False):
    tm = _pick(T, (1056, 512, 256, 128)) if tall else _tile_rows(T)
    if halo:
        return i * tm - HALO + lax.broadcasted_iota(jnp.int32, (tm + 2 * HALO, 1), 0)
    return i * tm + lax.broadcasted_iota(jnp.int32, (tm, 1), 0)


def _rms(x, w):
    return x * lax.rsqrt(jnp.mean(x * x, axis=-1, keepdims=True) + EPS) * w


def _silu(x):
    return x * jax.nn.sigmoid(x)


def _conv3(x, w):
    n = x.shape[0]
    return w[0:1] * pltpu.roll(x, 1, 0) + w[1:2] * x + w[2:3] * pltpu.roll(x, n - 1, 0)


def _conv3_t(d, w):
    n = d.shape[0]
    return w[0:1] * pltpu.roll(d, n - 1, 0) + w[1:2] * d + w[2:3] * pltpu.roll(d, 1, 0)


def _center(x):
    return x[HALO:x.shape[0] - HALO]


def _retention(name, a, b, v, T, da, dv, into=None):
    (a, a0), (b, b0), (v, v0) = [t if isinstance(t, tuple) else (t, 0) for t in (a, b, v)]
    nc = T // CHUNK
    log_gammas = [math.log(1.0 - 2.0 ** (-5.0 - h)) for h in range(RET_HEADS)]

    def body(*refs):
        a_ref, b_ref, v_ref = refs[:3]
        out_ref, o_ref, st, st_b = refs[-4:]
        h = pl.program_id(0)
        lg = jnp.float32(log_gammas[RET_HEADS - 1])
        for k in range(RET_HEADS - 2, -1, -1):
            lg = jnp.where(h == k, jnp.float32(log_gammas[k]), lg)
        li = lax.broadcasted_iota(jnp.int32, (CHUNK, CHUNK), 0)
        si = lax.broadcasted_iota(jnp.int32, (CHUNK, CHUNK), 1)
        dmat = jnp.exp(lg * jnp.abs(li - si).astype(F32))
        pos = lax.broadcasted_iota(jnp.int32, (CHUNK, 1), 0).astype(F32)
        kdec_f = jnp.exp((CHUNK - 1 - pos) * lg)
        qdec_f = jnp.exp((pos + 1) * lg)
        kdec_b = jnp.exp(pos * lg)
        qdec_b = jnp.exp((CHUNK - pos) * lg)
        cdec = jnp.exp(CHUNK * lg)

        def rows(n):
            return pl.ds(pl.multiple_of(n * CHUNK, CHUNK), CHUNK)

        st[...] = jnp.zeros_like(st)
        st_b[...] = jnp.zeros_like(st_b)
        o_ref[...] = jnp.zeros_like(o_ref)

        def step(m, carry):
            r = rows(m)
            av, bv, vv = a_ref[r, :], b_ref[r, :], v_ref[r, :].astype(BF16)
            s = _dot_nt(av.astype(BF16), bv.astype(BF16)) * dmat
            o_ref[r, :] += _dot(s.astype(BF16), vv) + _dot((av * qdec_f).astype(BF16), st[...].astype(BF16))
            st[...] = cdec * st[...] + _dot_tn((bv * kdec_f).astype(BF16), vv)
            r = rows(nc - 1 - m)
            av, bv, vv = a_ref[r, :], b_ref[r, :], v_ref[r, :].astype(BF16)
            o_ref[r, :] += _dot((av * qdec_b).astype(BF16), st_b[...].astype(BF16))
            st_b[...] = cdec * st_b[...] + _dot_tn((bv * kdec_b).astype(BF16), vv)
            return carry

        lax.fori_loop(0, nc, step, 0, unroll=True)
        out_ref[...] = o_ref[...].astype(out_ref.dtype)

    in_specs = [pl.BlockSpec((T, da), lambda h: (0, a0 // da + h)), pl.BlockSpec((T, da), lambda h: (0, b0 // da + h)),
                pl.BlockSpec((T, dv), lambda h: (0, v0 // dv + h))]
    if into is None:
        args, o0, aliases = (a, b, v), 0, {}
        out_shape = jax.ShapeDtypeStruct((T, RET_HEADS * dv), F32)
    else:
        args, o0, aliases = (a, b, v, into[0]), into[1], {3: 0}
        in_specs.append(ANY_SPACE)
        out_shape = jax.ShapeDtypeStruct(into[0].shape, into[0].dtype)
    return pl.pallas_call(
        body, name=name, grid=(RET_HEADS,), in_specs=in_specs,
        out_specs=pl.BlockSpec((T, dv), lambda h: (0, o0 // dv + h)), out_shape=out_shape,
        input_output_aliases=aliases,
        scratch_shapes=[pltpu.VMEM((T, dv), F32), pltpu.VMEM((da, dv), F32), pltpu.VMEM((da, dv), F32)],
        compiler_params=_params(("arbitrary",)),
    )(*args)


def _softplus(x):
    return jnp.maximum(x, 0.0) + jnp.log1p(jnp.exp(-jnp.abs(x)))


def _lane_lo():
    return lax.broadcasted_iota(jnp.int32, (1, CHUNK), 1) < SSD_HEAD_DIM


def _pair_cols(col, j):
    return jnp.where(_lane_lo(), col[:, 2 * j:2 * j + 1], col[:, 2 * j + 1:2 * j + 2])


def _pair_rows(colr, j):
    lo = lax.broadcasted_iota(jnp.int32, (CHUNK, 1), 0) < SSD_HEAD_DIM
    return jnp.where(lo, colr[2 * j:2 * j + 1, :], colr[2 * j + 1:2 * j + 2, :])


def _onehot8(h):
    return (lax.broadcasted_iota(jnp.int32, (1, HEADS_PER_GROUP), 1) == h).astype(F32)


def _ssd_pre(d, c, rawc, rawr, bc, br, alc, alr):
    li = lax.broadcasted_iota(jnp.int32, (CHUNK, CHUNK), 0)
    si = lax.broadcasted_iota(jnp.int32, (CHUNK, CHUNK), 1)
    dif = li - si if d == 0 else si - li
    mask = dif >= 0
    mask_t = dif <= 0
    rowc = c * CHUNK + lax.broadcasted_iota(jnp.int32, (CHUNK, 1), 0)
    rowr = c * CHUNK + lax.broadcasted_iota(jnp.int32, (1, CHUNK), 1)
    dtc = jnp.where(rowc >= PAD_ROWS, _softplus(rawc + bc), 0.0)
    dtr = jnp.where(rowr >= PAD_ROWS, _softplus(rawr + br), 0.0)
    ac = -jnp.exp(alc)
    ar = -jnp.exp(alr)
    dlc = dtc * ac
    dlr = dtr * ar
    alpc = _dot(mask.astype(F32), dlc, precision=HIGHEST)
    alpr = _dot(dlr, mask_t.astype(F32), precision=HIGHEST)
    endc = jnp.sum(dlc, axis=0, keepdims=True)
    endr = jnp.sum(dlr, axis=1, keepdims=True)
    return dict(mask=mask, mask_t=mask_t, dtc=dtc, ac=ac, alpc=alpc, alpr=alpr, endc=endc, endr=endr,
                valid=rowc >= PAD_ROWS)


def _chunk_of(d, n, nc):
    return n + d * (nc - 1 - 2 * n)


GROUP_WIDTH = HEADS_PER_GROUP * SSD_HEAD_DIM


def _chunks_per_step(nc, most=3):
    return next(c for c in (11, 3, 1) if c <= most and nc % c == 0)


def _ssd_in_specs(d, cfn, rows):
    return [
        pl.BlockSpec((rows, GROUP_WIDTH), lambda g, n: (cfn(d, n), g)),
        pl.BlockSpec((rows, SSD_STATE), lambda g, n: (cfn(d, n), g)),
        pl.BlockSpec((rows, SSD_STATE), lambda g, n: (cfn(d, n), g)),
        pl.BlockSpec((None, None, rows, HEADS_PER_GROUP), lambda g, n: (d, g, cfn(d, n), 0)),
        pl.BlockSpec((None, None, HEADS_PER_GROUP, rows), lambda g, n: (d, g, 0, cfn(d, n))),
        pl.BlockSpec((None, None, 1, HEADS_PER_GROUP), lambda g, n: (d, g, 0, 0)),
        pl.BlockSpec((None, None, HEADS_PER_GROUP, 1), lambda g, n: (d, g, 0, 0)),
        pl.BlockSpec((None, None, 1, HEADS_PER_GROUP), lambda g, n: (d, g, 0, 0)),
        pl.BlockSpec((None, None, HEADS_PER_GROUP, 1), lambda g, n: (d, g, 0, 0)),
    ]


N_SSD_IN = 9


def _ssd_fwd(xs, bm, cm, small, T):
    nc = T // CHUNK
    cps = _chunks_per_step(nc, 11)
    rows = cps * CHUNK
    cfn = lambda d, n: _chunk_of(d, n, nc // cps)

    def one_direction(d, n, ins, y_ref, hs_ref, h_scr):
        x_ref, b_ref, c_ref, rawc_ref, rawr_ref, *per_group = ins
        for kk in range(cps):
            k = kk if d == 0 else cps - 1 - kk
            r = pl.ds(k * CHUNK, CHUNK)
            one_chunk(d, cfn(d, n) * cps + k,
                      (x_ref.at[r], b_ref.at[r], c_ref.at[r], rawc_ref.at[r], rawr_ref.at[:, r], *per_group),
                      y_ref.at[r], hs_ref.at[k], h_scr)

    def one_chunk(d, c, ins, y_ref, hs_ref, h_scr):
        x_ref, b_ref, c_ref, rawc_ref, rawr_ref, bc_ref, br_ref, alc_ref, alr_ref = ins
        q = _ssd_pre(d, c, rawc_ref[...], rawr_ref[...], bc_ref[...], br_ref[...], alc_ref[...], alr_ref[...])
        bv = b_ref[...].astype(BF16)
        cv = c_ref[...].astype(BF16)
        cb = _dot_nt(cv, bv)
        lo = _lane_lo()
        for j in range(PAIRS_PER_GROUP):
            xp = x_ref[:, j * CHUNK:(j + 1) * CHUNK]
            xd = xp * _pair_cols(q["dtc"], j)
            xdb = xd.astype(BF16)
            yi = []
            for e in range(2):
                h = 2 * j + e
                lm = jnp.exp(jnp.where(q["mask"], q["alpc"][:, h:h + 1] - q["alpr"][h:h + 1, :], -jnp.inf))
                yi.append(_dot((cb * lm).astype(BF16), xdb))
            alp = _pair_cols(q["alpc"], j)
            hp = h_scr[j]
            hs_ref[j] = hp
            yo = jnp.exp(alp) * _dot_nt(cv, hp.astype(BF16))
            y_ref[:, j * CHUNK:(j + 1) * CHUNK] = (jnp.where(lo, yi[0], yi[1]) + yo).astype(y_ref.dtype)
            de = jnp.exp(_pair_cols(q["endc"], j) - alp)
            h_scr[j] = jnp.exp(_pair_rows(q["endr"], j)) * hp + _dot_tn((xd * de).astype(BF16), bv)

    def body(*refs):
        n = pl.program_id(1)
        ins, (y_f, y_b, hs_f, hs_b, h_scr) = refs[:2 * N_SSD_IN], refs[2 * N_SSD_IN:]

        @pl.when(n == 0)
        def _():
            h_scr[...] = jnp.zeros_like(h_scr)

        one_direction(0, n, ins[:N_SSD_IN], y_f, hs_f, h_scr.at[0])
        one_direction(1, n, ins[N_SSD_IN:], y_b, hs_b, h_scr.at[1])

    y_spec = lambda d: pl.BlockSpec((rows, GROUP_WIDTH), lambda g, n: (cfn(d, n), g))
    hs_spec = lambda d: pl.BlockSpec((None, cps, PAIRS_PER_GROUP, CHUNK, SSD_STATE),
                                     lambda g, n: (g, cfn(d, n), 0, 0, 0))
    y_shape = jax.ShapeDtypeStruct((T, SSD_HEADS * SSD_HEAD_DIM), BF16)
    hs_shape = jax.ShapeDtypeStruct((SSD_GROUPS, nc, PAIRS_PER_GROUP, CHUNK, SSD_STATE), F32)
    y_f, y_b, hs_f, hs_b = pl.pallas_call(
        body, name="ssd_fwd", grid=(SSD_GROUPS, nc // cps),
        in_specs=_ssd_in_specs(0, cfn, rows) + _ssd_in_specs(1, cfn, rows),
        out_specs=[y_spec(0), y_spec(1), hs_spec(0), hs_spec(1)],
        out_shape=[y_shape, y_shape, hs_shape, hs_shape],
        scratch_shapes=[pltpu.VMEM((2, PAIRS_PER_GROUP, CHUNK, SSD_STATE), F32)],
        compiler_params=_params(("arbitrary", "arbitrary")),
    )(xs, bm, cm, *small, xs, bm, cm, *small)
    return (y_f, y_b), (hs_f, hs_b)


def _ssd_bwd(xs, bm, cm, small, hs, dy, T):
    nc = T // CHUNK
    cps = _chunks_per_step(nc, 11)
    rows = cps * CHUNK
    cfn = lambda d, n: _chunk_of(1 - d, n, nc // cps)

    def one_direction(d, n, ins, outs, dh_scr):
        x_ref, b_ref, c_ref, rawc_ref, rawr_ref, bc_ref, br_ref, alc_ref, alr_ref, hs_ref, dy_ref = ins
        dx_ref, db_ref, dc_ref, draw_ref, dbias_ref, dalog_ref = outs
        for kk in range(cps):
            k = cps - 1 - kk if d == 0 else kk
            r = pl.ds(k * CHUNK, CHUNK)
            one_chunk(d, cfn(d, n) * cps + k, n if kk == 0 else None,
                      (x_ref.at[r], b_ref.at[r], c_ref.at[r], rawc_ref.at[r], rawr_ref.at[:, r], bc_ref, br_ref,
                       alc_ref, alr_ref, hs_ref.at[k], dy_ref.at[r]),
                      (dx_ref.at[r], db_ref.at[r], dc_ref.at[r], draw_ref.at[r], dbias_ref, dalog_ref), dh_scr)

    def one_chunk(d, c, first_of_step, ins, outs, dh_scr):
        x_ref, b_ref, c_ref, rawc_ref, rawr_ref, bc_ref, br_ref, alc_ref, alr_ref, hs_ref, dy_ref = ins
        dx_ref, db_ref, dc_ref, draw_ref, dbias_ref, dalog_ref = outs
        rawc, bc = rawc_ref[...], bc_ref[...]
        q = _ssd_pre(d, c, rawc, rawr_ref[...], bc, br_ref[...], alc_ref[...], alr_ref[...])
        b32, c32 = b_ref[...], c_ref[...]
        bv, cv = b32.astype(BF16), c32.astype(BF16)
        cb = _dot_nt(cv, bv)
        cbt = _dot_nt(bv, cv)
        lo = _lane_lo()
        row_lo = lax.broadcasted_iota(jnp.int32, (CHUNK, 1), 0) < SSD_HEAD_DIM
        dcb = jnp.zeros((CHUNK, CHUNK), F32)
        dcp = jnp.zeros((CHUNK, SSD_STATE), F32)
        dbp = jnp.zeros((CHUNK, SSD_STATE), F32)
        dalp = jnp.zeros((CHUNK, HEADS_PER_GROUP), F32)
        dend = jnp.zeros((1, HEADS_PER_GROUP), F32)
        ddtx = jnp.zeros((CHUNK, HEADS_PER_GROUP), F32)

        def half_sums(t):
            return (jnp.sum(jnp.where(lo, t, 0.0), axis=1, keepdims=True),
                    jnp.sum(jnp.where(lo, 0.0, t), axis=1, keepdims=True))

        for j in range(PAIRS_PER_GROUP):
            xp = x_ref[:, j * CHUNK:(j + 1) * CHUNK]
            dtp = _pair_cols(q["dtc"], j)
            xd = xp * dtp
            xdb = xd.astype(BF16)
            dyp = dy_ref[:, j * CHUNK:(j + 1) * CHUNK]
            dyb = dyp.astype(BF16)
            hn = hs_ref[j]
            hnb = hn.astype(BF16)
            dh1 = dh_scr[j]
            dh1b = dh1.astype(BF16)
            alp = _pair_cols(q["alpc"], j)
            ea = jnp.exp(alp)
            de = jnp.exp(_pair_cols(q["endc"], j) - alp)
            dxi = []
            for e in range(2):
                h = 2 * j + e
                diff = q["alpc"][:, h:h + 1] - q["alpr"][h:h + 1, :]
                lm = jnp.exp(jnp.where(q["mask"], diff, -jnp.inf))
                mt = cbt * jnp.exp(jnp.where(q["mask_t"], -diff, -jnp.inf))
                dxi.append(_dot(mt.astype(BF16), dyb))
                dyeb_h = (jnp.where(lo, dyp, 0.0) if e == 0 else jnp.where(lo, 0.0, dyp)).astype(BF16)
                gl = _dot_nt(dyeb_h, xdb) * lm
                dcb = dcb + gl
                ra = jnp.sum(gl * cb - _dot_nt(xdb, dyeb_h) * mt, axis=1, keepdims=True)
                dalp = dalp + ra * _onehot8(h)
            y_off = ea * _dot_nt(cv, hnb)
            dxs_state = de * _dot_nt(bv, dh1b)
            dxd = jnp.where(lo, dxi[0], dxi[1]) + dxs_state
            dyeb = (dyp * ea).astype(BF16)
            dcp = dcp + _dot(dyeb, hnb)
            dbp = dbp + _dot((xd * de).astype(BF16), dh1b)
            dh_scr[j] = jnp.exp(_pair_rows(q["endr"], j)) * dh1 + _dot_tn(dyeb, cv)
            r0, r1 = half_sums(dyp * y_off - xd * dxs_state)
            dalp = dalp + r0 * _onehot8(2 * j) + r1 * _onehot8(2 * j + 1)
            t0, t1 = half_sums(jnp.sum(xd * dxs_state, axis=0, keepdims=True))
            u = dh1 * hn
            u0 = jnp.sum(jnp.sum(jnp.where(row_lo, u, 0.0), axis=0, keepdims=True), axis=1, keepdims=True)
            u1 = jnp.sum(jnp.sum(jnp.where(row_lo, 0.0, u), axis=0, keepdims=True), axis=1, keepdims=True)
            eend = jnp.exp(q["endc"])
            dend = dend + (t0 + eend * u0) * _onehot8(2 * j) + (t1 + eend * u1) * _onehot8(2 * j + 1)
            dx_ref[:, j * CHUNK:(j + 1) * CHUNK] = (dxd * dtp).astype(dx_ref.dtype)
            w0, w1 = half_sums(dxd * xp)
            ddtx = ddtx + w0 * _onehot8(2 * j) + w1 * _onehot8(2 * j + 1)

        dcbb = dcb.astype(BF16)
        dc_ref[...] = (dcp + _dot(dcbb, bv)).astype(dc_ref.dtype)
        db_ref[...] = (dbp + _dot_tn(dcbb, cv)).astype(db_ref.dtype)
        ddl = _dot(q["mask_t"].astype(F32), dalp, precision=HIGHEST) + dend
        ddt = ddl * q["ac"] + ddtx
        draw = jnp.where(q["valid"], ddt * jax.nn.sigmoid(rawc + bc), 0.0)
        draw_ref[...] = draw
        dbias = jnp.sum(draw, axis=0, keepdims=True)
        dalog = jnp.sum(ddl * q["dtc"], axis=0, keepdims=True) * q["ac"]

        def add():
            dbias_ref[...] += dbias
            dalog_ref[...] += dalog

        if first_of_step is None:
            add()
        else:
            @pl.when(first_of_step == 0)
            def _():
                dbias_ref[...] = dbias
                dalog_ref[...] = dalog

            pl.when(first_of_step > 0)(add)

    n_in, n_out = N_SSD_IN + 2, 6

    def body(*refs):
        n = pl.program_id(1)
        ins, outs, dh_scr = refs[:2 * n_in], refs[2 * n_in:2 * (n_in + n_out)], refs[-1]

        @pl.when(n == 0)
        def _():
            dh_scr[...] = jnp.zeros_like(dh_scr)

        one_direction(0, n, ins[:n_in], outs[:n_out], dh_scr.at[0])
        one_direction(1, n, ins[n_in:], outs[n_out:], dh_scr.at[1])

    def in_specs(d):
        return _ssd_in_specs(d, cfn, rows) + [
            pl.BlockSpec((None, cps, PAIRS_PER_GROUP, CHUNK, SSD_STATE), lambda g, n: (g, cfn(d, n), 0, 0, 0)),
            pl.BlockSpec((rows, GROUP_WIDTH), lambda g, n: (cfn(d, n), g))]

    def out_specs(d):
        acc = pl.BlockSpec((None, 1, HEADS_PER_GROUP), lambda g, n: (g, 0, 0))
        return [pl.BlockSpec((rows, GROUP_WIDTH), lambda g, n: (cfn(d, n), g)),
                pl.BlockSpec((rows, SSD_STATE), lambda g, n: (cfn(d, n), g)),
                pl.BlockSpec((rows, SSD_STATE), lambda g, n: (cfn(d, n), g)),
                pl.BlockSpec((None, rows, HEADS_PER_GROUP), lambda g, n: (g, cfn(d, n), 0)), acc, acc]

    out_shape = [jax.ShapeDtypeStruct((T, SSD_HEADS * SSD_HEAD_DIM), BF16),
                 jax.ShapeDtypeStruct((T, SSD_GROUPS * SSD_STATE), BF16),
                 jax.ShapeDtypeStruct((T, SSD_GROUPS * SSD_STATE), BF16),
                 jax.ShapeDtypeStruct((SSD_GROUPS, T, HEADS_PER_GROUP), F32),
                 jax.ShapeDtypeStruct((SSD_GROUPS, 1, HEADS_PER_GROUP), F32),
                 jax.ShapeDtypeStruct((SSD_GROUPS, 1, HEADS_PER_GROUP), F32)]
    res = pl.pallas_call(
        body, name="ssd_bwd", grid=(SSD_GROUPS, nc // cps),
        in_specs=in_specs(0) + in_specs(1), out_specs=out_specs(0) + out_specs(1), out_shape=out_shape * 2,
        scratch_shapes=[pltpu.VMEM((2, PAIRS_PER_GROUP, CHUNK, SSD_STATE), F32)],
        compiler_params=_params(("arbitrary", "arbitrary")),
    )(xs, bm, cm, *small, hs[0], dy, xs, bm, cm, *small, hs[1], dy)
    return [(res[k], res[n_out + k]) for k in range(n_out)]


def _rot(x, cs, sn):
    return x * cs + pltpu.roll(x, RET_QK_DIM // 2, 1) * sn


def _rot_t(d, cs, sn):
    return d * cs + pltpu.roll(d * sn, RET_QK_DIM // 2, 1)


def _ret_post(y, g, w):
    parts = []
    for h in range(RET_HEADS):
        yh = y[:, h * RET_V_DIM:(h + 1) * RET_V_DIM]
        mu = jnp.mean(yh, axis=-1, keepdims=True)
        var = jnp.mean(jnp.square(yh - mu), axis=-1, keepdims=True)
        parts.append((yh - mu) * lax.rsqrt(var + EPS))
    return _silu(g) * (jnp.concatenate(parts, axis=1) * w)


def _ssd_post(yf, yb, xs, z, dskip, w):
    y = (yf + yb + xs * dskip) * _silu(z)
    return y * lax.rsqrt(jnp.mean(y * y, axis=-1, keepdims=True) + EPS) * w


def _merge(gates, yr, ys, valid):
    m = jax.nn.sigmoid(gates[:, :D_MODEL]) * yr + jax.nn.sigmoid(gates[:, D_MODEL:]) * ys
    return jnp.where(valid, m, 0.0)


def _rope_tables(T):
    half = RET_QK_DIM // 2
    inv = ROPE_BASE ** (-jnp.arange(half, dtype=F32) / half)
    pos = (jnp.arange(T) - PAD_ROWS).astype(F32)
    ang = pos[:, None] * inv[None, :]
    cos, sin = jnp.cos(ang), jnp.sin(ang)
    return jnp.concatenate([cos, cos], axis=1), jnp.concatenate([-sin, sin], axis=1)


def _per_group(v):
    c = v.reshape(SSD_GROUPS, 1, HEADS_PER_GROUP)
    return c, c.reshape(SSD_GROUPS, HEADS_PER_GROUP, 1)


def _local_step(x, target, w, tick, late_weights, early_grads, in_grads):
    S = x.shape[0]
    T = S + CHUNK
    tm = _tile_rows(T)
    c0 = _const(0)

    h0 = jnp.concatenate([jnp.zeros((PAD_ROWS, D_MODEL), F32), w["meta_tokens"], x], axis=0)
    seg_at = {name: a for name, a, _ in SEGMENTS}
    w_main = w["w_in_t"][:seg_at["dt"]]
    w_dt = jnp.pad(w["w_in_t"][seg_at["dt"]:seg_at["gates"]], ((0, CHUNK - 2 * SSD_HEADS), (0, 0)))
    w_gates = w["w_in_t"][seg_at["gates"]:]

    def norm_cast(name, h, nw):
        return _rows(name, lambda i, hv, wv: (_rms(hv, wv),), T, 1, [(h, D_MODEL, c0)], [(nw, D_MODEL, c0)],
                     [(D_MODEL, D_MODEL, c0, BF16)], tall=True)[0]

    u = norm_cast("norm_mix", h0, w["norm_mix_w"] + tick)
    p_main = _mm("proj_main", u, w_main, "nt", out_dtype=BF16)
    p_dt = _mm("proj_dt", u, w_dt, "nt")
    p_gates = _mm("proj_gates", u, w_gates, "nt", out_dtype=BF16)

    def seg(name, width, cf=c0):
        base = seg_at[name] // width
        return (p_main, width, lambda j: base + cf(j))

    cs, sn = _rope_tables(T)
    scale = RET_QK_DIM ** -0.5

    def rot_fn(i, qk, csv, snv):
        q = [_rot(qk[:, h * 128:(h + 1) * 128], csv, snv) for h in range(RET_HEADS)]
        k = [_rot(qk[:, (RET_HEADS + h) * 128:(RET_HEADS + h + 1) * 128], csv, snv) * scale for h in range(RET_HEADS)]
        return jnp.concatenate(q, axis=1), jnp.concatenate(k, axis=1)

    qr, kr = _rows("rotary", rot_fn, T, 1, [seg("qk", 1024), (cs, 128, c0), (sn, 128, c0)], [],
                   [(512, 512, c0, F32), (512, 512, c0, F32)], tall=True)
    v_at = (p_main, seg_at["v"])
    y_ret = _retention("retention", qr, kr, v_at, T, RET_QK_DIM, RET_V_DIM)
    a_ret = _rows("ret_post", lambda i, y, g, gw: (_ret_post(y, g, gw),), T, 1,
                  [(y_ret, 1024, c0), seg("g", 1024)], [(w["ret_gn_w"], 1024, c0)],
                  [(1024, 1024, c0, BF16)], tall=True)[0]

    conv_w = {"xs": w["w_ssd_conv"][:, :2048], "B": w["w_ssd_conv"][:, 2048:2560], "C": w["w_ssd_conv"][:, 2560:]}
    conv_b = {"xs": w["b_ssd_conv"][:, :2048], "B": w["b_ssd_conv"][:, 2048:2560], "C": w["b_ssd_conv"][:, 2560:]}

    def ssd_conv_fn(i, xe, cw, cb):
        r = _row_ids(i, T, True, tall=True)
        return (_center(jnp.where(r >= PAD_ROWS, _silu(_conv3(xe, cw) + cb), 0.0)),)

    act = {}
    for name in ("xs", "B", "C"):
        wd = conv_w[name].shape[1]
        cw = 512
        act[name] = _rows("ssd_conv_" + name, ssd_conv_fn, T, wd // cw, [seg(name, cw, lambda j: j)],
                          [(conv_w[name], cw, lambda j: j), (conv_b[name], cw, lambda j: j)],
                          [(wd, cw, lambda j: j, BF16)], halo=True, tall=True)[0]

    raw = p_dt[:, :2 * SSD_HEADS].reshape(T, 2, SSD_GROUPS, HEADS_PER_GROUP)
    rawc = raw.transpose(1, 2, 0, 3)
    rawr = raw.transpose(1, 2, 3, 0)
    bias = [_per_group(w["dt_bias_f"]), _per_group(w["dt_bias_b"])]
    alog = [_per_group(w["a_log_f"]), _per_group(w["a_log_b"])]
    small = (rawc, rawr, jnp.stack([bias[0][0], bias[1][0]]), jnp.stack([bias[0][1], bias[1][1]]),
             jnp.stack([alog[0][0], alog[1][0]]), jnp.stack([alog[0][1], alog[1][1]]))
    y_dir, states = _ssd_fwd(act["xs"], act["B"], act["C"], small, T)

    dskip_e = jnp.repeat(w["d_skip"], SSD_HEAD_DIM, axis=1)
    gcol = lambda j: j
    gw_ = 512
    a_ssd = _rows("ssd_post", lambda i, yf, yb, xv, zv, dk, nw: (_ssd_post(yf, yb, xv, zv, dk, nw),), T, SSD_GROUPS,
                  [(y_dir[0], gw_, gcol), (y_dir[1], gw_, gcol), (act["xs"], gw_, gcol), seg("z", gw_, gcol)],
                  [(dskip_e, gw_, gcol), (w["ssd_norm_w"], gw_, gcol)], [(2048, gw_, gcol, BF16)], tall=True)[0]

    w = dict(w, **late_weights(a_ssd))
    w_up_g, w_up_u = w["w_ffn_up_t"][:D_FF], w["w_ffn_up_t"][D_FF:]
    y_ret_o = _mm("ret_out", a_ret, w["w_ret_out"], "nn", out_dtype=BF16)
    y_ssd_o = _mm("ssd_out", a_ssd, w["w_ssd_out"], "nn", out_dtype=BF16)

    def merge_fn(i, gates, yr, ys):
        return (_merge(gates, yr, ys, _row_ids(i, T) >= PAD_ROWS),)

    merged = _rows("merge", merge_fn, T, 1, [(p_gates, 2048, c0), (y_ret_o, 1024, c0), (y_ssd_o, 1024, c0)], [],
                   [(1024, 1024, c0, BF16)])[0]
    h1 = _mm("mix_out", merged, w["w_out"], "nn", add=h0)

    n2 = norm_cast("norm_ffn", h1, w["norm_ffn_w"])
    f_pre = _mm("ffn_up", n2, w["w_ffn_up_t"], "nt", out_dtype=BF16)
    cwg, cwu = w["w_ffn_conv"][:, :D_FF], w["w_ffn_conv"][:, D_FF:]
    cbg, cbu = w["b_ffn_conv"][:, :D_FF], w["b_ffn_conv"][:, D_FF:]
    fcol = lambda j: j
    fw = 256

    def ffn_act_fn(i, ge, ue, wg, wu, bg, bu):
        return (_center(_silu(_conv3(ge, wg) + bg) * (_conv3(ue, wu) + bu)),)

    ucol = lambda j: D_FF // fw + j
    a2 = _rows("ffn_act", ffn_act_fn, T, D_FF // fw, [(f_pre, fw, fcol), (f_pre, fw, ucol)],
               [(cwg, fw, fcol), (cwu, fw, fcol), (cbg, fw, fcol), (cbu, fw, fcol)], [(D_FF, fw, fcol, BF16)],
               halo=True, tall=True)[0]
    h2 = _mm("ffn_down", a2, w["w_ffn_down"], "nn", add=h1)

    fnw = w["final_norm_w"].reshape(1, D_MODEL)

    per_tile = tm // CHUNK
    tgt_specs = [(target, D_MODEL, c0, None, (CHUNK, lambda i, k=k: jnp.maximum(per_tile * i - 1 + k, 0)))
                 for k in range(per_tile)]

    def loss_fn(i, hv, *rest):
        tv, nw = jnp.concatenate(rest[:per_tile], axis=0), rest[per_tile]
        valid = _row_ids(i, T) >= CHUNK
        y, vjp = jax.vjp(_rms, hv, nw)
        diff = jnp.where(valid, y - tv, 0.0)
        dh, dw = vjp(diff * (1.0 / D_MODEL))
        part = 0.5 / D_MODEL * jnp.sum(jnp.sum(diff * diff, axis=1, keepdims=True), axis=0, keepdims=True)
        return dh, jnp.broadcast_to(part, (1, 128)), dw

    dh2, loss_acc, d_fnw = _rows("loss", loss_fn, T, 1, [(h2, D_MODEL, c0)] + tgt_specs, [(fnw, D_MODEL, c0)],
                                 [(D_MODEL, D_MODEL, c0, F32)], [(1, 128, 128, c0), (1, D_MODEL, D_MODEL, c0)])
    loss = loss_acc[0, 0]
    grads = {"final_norm_w": d_fnw.reshape(D_MODEL)}

    da2 = _mm("d_ffn_act", dh2, w["w_ffn_down"], "nt", out_dtype=BF16)
    grads["w_ffn_down"] = _mm("g_ffn_down", a2, dh2, "tn", out_dtype=BF16)

    def ffn_bwd_fn(i, ge, ue, de, wg, wu, bg, bu):
        fg = _conv3(ge, wg) + bg
        fu = _conv3(ue, wu) + bu
        sg = jax.nn.sigmoid(fg)
        dfg = de * fu * (sg * (1.0 + fg * (1.0 - sg)))
        dfu = de * (fg * sg)
        n = ge.shape[0]

        def wgrad(df, xe):
            df_c = _center(df)
            return jnp.concatenate([jnp.sum(df_c * _center(pltpu.roll(xe, 1, 0)), axis=0, keepdims=True),
                                    jnp.sum(df_c * _center(xe), axis=0, keepdims=True),
                                    jnp.sum(df_c * _center(pltpu.roll(xe, n - 1, 0)), axis=0, keepdims=True)], axis=0)

        return (_center(_conv3_t(dfg, wg)), _center(_conv3_t(dfu, wu)), wgrad(dfg, ge), wgrad(dfu, ue),
                jnp.sum(_center(dfg), axis=0, keepdims=True), jnp.sum(_center(dfu), axis=0, keepdims=True))

    dfg_pre, dfu_pre, g_cwg, g_cwu, g_cbg, g_cbu = _rows(
        "ffn_act_bwd", ffn_bwd_fn, T, D_FF // fw, [(f_pre, fw, fcol), (f_pre, fw, ucol), (da2, fw, fcol)],
        [(cwg, fw, fcol), (cwu, fw, fcol), (cbg, fw, fcol), (cbu, fw, fcol)],
        [(D_FF, fw, fcol, BF16), (D_FF, fw, fcol, BF16)],
        [(3, D_FF, fw, fcol), (3, D_FF, fw, fcol), (1, D_FF, fw, fcol), (1, D_FF, fw, fcol)], halo=True, tall=True)
    grads["w_ffn_conv"] = jnp.concatenate([g_cwg, g_cwu], axis=1)
    grads["b_ffn_conv"] = jnp.concatenate([g_cbg, g_cbu], axis=1)
    dn2 = _mm("d_norm_ffn_g", dfg_pre, w_up_g, "nn")
    dn2 = _mm("d_norm_ffn_u", dfu_pre, w_up_u, "nn", add=dn2)
    grads["w_ffn_up_t"] = jnp.concatenate([_mm("g_ffn_up_g", dfg_pre, n2, "tn", out_dtype=BF16), _mm("g_ffn_up_u", dfu_pre, n2, "tn", out_dtype=BF16)],
                                          axis=0)

    def norm_bwd(name, h, nw, dn, dres):
        def fn(i, hv, dnv, drv, wv):
            _, vjp = jax.vjp(_rms, hv, wv)
            dh, dw = vjp(dnv)
            return dh + drv, dw
        return _rows(name, fn, T, 1, [(h, D_MODEL, c0), (dn, D_MODEL, c0), (dres, D_MODEL, c0)], [(nw, D_MODEL, c0)],
                     [(D_MODEL, D_MODEL, c0, F32)], [(1, D_MODEL, D_MODEL, c0)])

    dh1, grads["norm_ffn_w"] = norm_bwd("norm_ffn_bwd", h1, w["norm_ffn_w"], dn2, dh2)

    dmerged = _mm("d_merged", dh1, w["w_out"], "nt", out_dtype=BF16)
    grads["w_out"] = _mm("g_out", merged, dh1, "tn", out_dtype=BF16)

    def merge_bwd_fn(i, gates, yr, ys, dm):
        valid = _row_ids(i, T) >= PAD_ROWS
        _, vjp = jax.vjp(lambda a, b, c: _merge(a, b, c, valid), gates, yr, ys)
        return vjp(dm)

    dgates, dyr, dys = _rows("merge_bwd", merge_bwd_fn, T, 1,
                             [(p_gates, 2048, c0), (y_ret_o, 1024, c0), (y_ssd_o, 1024, c0), (dmerged, 1024, c0)],
                             [], [(2048, 2048, c0, BF16), (1024, 1024, c0, BF16), (1024, 1024, c0, BF16)])
    dproj = {"gates": dgates}

    da_ssd = _mm("d_ssd_act", dys, w["w_ssd_out"], "nt", out_dtype=BF16)
    grads["w_ssd_out"] = _mm("g_ssd_out", a_ssd, dys, "tn", out_dtype=BF16)

    def ssd_post_bwd_fn(i, yf, yb, xv, zv, da, dk, nw):
        _, vjp = jax.vjp(_ssd_post, yf, yb, xv, zv, dk, nw)
        dyf, _, dxv, dzv, ddk, dnw = vjp(da)
        return dyf, dxv, dzv, ddk, dnw

    d_main = lax.empty(p_main.shape, BF16)

    def into_main(name, width, cf=c0):
        base = seg_at[name] // width
        return (d_main, width, lambda j: base + cf(j), BF16)

    dy_ssd, dxs_skip, d_main, g_dskip_e, grads["ssd_norm_w"] = _rows(
        "ssd_post_bwd", ssd_post_bwd_fn, T, SSD_GROUPS,
        [(y_dir[0], gw_, gcol), (y_dir[1], gw_, gcol), (act["xs"], gw_, gcol), seg("z", gw_, gcol),
         (da_ssd, gw_, gcol)],
        [(dskip_e, gw_, gcol), (w["ssd_norm_w"], gw_, gcol)],
        [(2048, gw_, gcol, BF16), (2048, gw_, gcol, BF16), into_main("z", gw_, gcol)],
        [(1, 2048, gw_, gcol), (1, 2048, gw_, gcol)], tall=True)
    grads["d_skip"] = g_dskip_e.reshape(SSD_HEADS, SSD_HEAD_DIM).sum(axis=1).reshape(1, SSD_HEADS)

    dxs_dir, db_dir, dc_dir, draw, g_bias, g_alog = _ssd_bwd(act["xs"], act["B"], act["C"], small, states, dy_ssd, T)
    grads["dt_bias_f"], grads["dt_bias_b"] = g_bias[0].reshape(1, SSD_HEADS), g_bias[1].reshape(1, SSD_HEADS)
    grads["a_log_f"], grads["a_log_b"] = g_alog[0].reshape(1, SSD_HEADS), g_alog[1].reshape(1, SSD_HEADS)
    d_dt = jnp.stack(draw).transpose(2, 0, 1, 3).reshape(T, 2 * SSD_HEADS)
    dproj["dt"] = jnp.pad(d_dt, ((0, 0), (0, CHUNK - 2 * SSD_HEADS))).astype(BF16)

    def make_conv_bwd(nsum):
        def fn(i, xe, *rest):
            ds, (cw, cb) = rest[:nsum], rest[nsum:]
            r = _row_ids(i, T, True, tall=True)
            dact = ds[0]
            for t in ds[1:]:
                dact = dact + t
            dact = jnp.where(r >= PAD_ROWS, dact, 0.0)
            pre = _conv3(xe, cw) + cb
            sg = jax.nn.sigmoid(pre)
            dpre = dact * (sg * (1.0 + pre * (1.0 - sg)))
            n = xe.shape[0]
            dpc = _center(dpre)
            dw = jnp.concatenate([jnp.sum(dpc * _center(pltpu.roll(xe, 1, 0)), axis=0, keepdims=True),
                                  jnp.sum(dpc * _center(xe), axis=0, keepdims=True),
                                  jnp.sum(dpc * _center(pltpu.roll(xe, n - 1, 0)), axis=0, keepdims=True)], axis=0)
            return _center(_conv3_t(dpre, cw)), dw, jnp.sum(dpc, axis=0, keepdims=True)
        return fn

    g_cw, g_cb = {}, {}
    cots = {"xs": [(dxs_dir[0], 512, gcol), (dxs_dir[1], 512, gcol), (dxs_skip, 512, gcol)],
            "B": [(db_dir[0], 512, gcol), (db_dir[1], 512, gcol)],
            "C": [(dc_dir[0], 512, gcol), (dc_dir[1], 512, gcol)]}
    for name in ("xs", "B", "C"):
        wd = conv_w[name].shape[1]
        d_main, g_cw[name], g_cb[name] = _rows(
            "ssd_conv_bwd_" + name, make_conv_bwd(len(cots[name])), T, wd // 512,
            [seg(name, 512, gcol)] + cots[name], [(conv_w[name], 512, gcol), (conv_b[name], 512, gcol)],
            [into_main(name, 512, gcol)], [(3, wd, 512, gcol), (1, wd, 512, gcol)], halo=True, tall=True)
    grads["w_ssd_conv"] = jnp.concatenate([g_cw["xs"], g_cw["B"], g_cw["C"]], axis=1)
    grads["b_ssd_conv"] = jnp.concatenate([g_cb["xs"], g_cb["B"], g_cb["C"]], axis=1)

    da_ret = _mm("d_ret_act", dyr, w["w_ret_out"], "nt", out_dtype=BF16)
    grads["w_ret_out"] = _mm("g_ret_out", a_ret, dyr, "tn", out_dtype=BF16)
    tick = early_grads({n: grads.pop(n) for n in ("w_ffn_up_t", "w_ret_out", "w_ssd_out", "w_out", "w_ffn_down")})

    def ret_post_bwd_fn(i, y, g, da, gw):
        _, vjp = jax.vjp(_ret_post, y, g, gw)
        return vjp(da)

    dy_ret, d_main, grads["ret_gn_w"] = _rows(
        "ret_post_bwd", ret_post_bwd_fn, T, 1, [(y_ret, 1024, c0), seg("g", 1024), (da_ret, 1024, c0)],
        [(w["ret_gn_w"] + tick, 1024, c0)], [(1024, 1024, c0, BF16), into_main("g", 1024)], [(1, 1024, 1024, c0)])
    d_main = _retention("retention_dv", kr, qr, dy_ret, T, RET_QK_DIM, RET_V_DIM, into=(d_main, seg_at["v"]))
    dqr = _retention("retention_dq", dy_ret, v_at, kr, T, RET_V_DIM, RET_QK_DIM)
    dkr = _retention("retention_dk", v_at, dy_ret, qr, T, RET_V_DIM, RET_QK_DIM)

    def rot_bwd_fn(i, dq, dk, csv, snv):
        parts = [_rot_t(dq[:, h * 128:(h + 1) * 128], csv, snv) for h in range(RET_HEADS)]
        parts += [_rot_t(dk[:, h * 128:(h + 1) * 128] * scale, csv, snv) for h in range(RET_HEADS)]
        return (jnp.concatenate(parts, axis=1),)

    d_main = _rows("rotary_bwd", rot_bwd_fn, T, 1, [(dqr, 512, c0), (dkr, 512, c0), (cs, 128, c0), (sn, 128, c0)],
                   [], [into_main("qk", 1024)], tall=True)[0]

    g_in = [_mm("g_in_main", d_main, u, "tn", out_dtype=BF16),
            _mm("g_in_dt", dproj["dt"], u, "tn", out_dtype=BF16)[:2 * SSD_HEADS],
            _mm("g_in_gates", dproj["gates"], u, "tn", out_dtype=BF16)]
    tick = in_grads(jnp.concatenate(g_in, axis=0))
    du = _mm("d_u_dt", dproj["dt"] + tick.astype(BF16), w_dt, "nn")
    du = _mm("d_u_main", d_main, w_main, "nn", add=du)
    du = _mm("d_u_gates", dproj["gates"], w_gates, "nn", add=du)
    dh0, grads["norm_mix_w"] = norm_bwd("norm_mix_bwd", h0, w["norm_mix_w"], du, dh1)
    grads["meta_tokens"] = dh0[PAD_ROWS:CHUNK]
    return loss, dh0[CHUNK:], grads


MESH_ID = pl.DeviceIdType.MESH
ANY = pl.BlockSpec(memory_space=pl.ANY)


def _me_and_peers():
    x, y, c = lax.axis_index("x"), lax.axis_index("y"), lax.axis_index("c")
    peers = []
    for k in range(1, N_DEV):
        px = 1 - x if k & 4 else x
        py = 1 - y if k & 2 else y
        pc = 1 - c if k & 1 else c
        peers.append(((px, py, pc), 4 * px + 2 * py + pc))
    return 4 * x + 2 * y + c, peers


def _push_blocks(name, src, per_peer):
    blk = src.shape[1:] if per_peer else src.shape

    def body(src_ref, out_ref, send_sems, recv_sems, local_sem):
        me, peers = _me_and_peers()
        mine = src_ref.at[me] if per_peer else src_ref
        local = pltpu.make_async_copy(mine, out_ref.at[me], local_sem)
        local.start()
        sends = []
        for k, (dev, idx) in enumerate(peers):
            cp = pltpu.make_async_remote_copy(
                src_ref=src_ref.at[idx] if per_peer else src_ref, dst_ref=out_ref.at[me],
                send_sem=send_sems.at[k], recv_sem=recv_sems.at[k], device_id=dev, device_id_type=MESH_ID)
            cp.start()
            sends.append(cp)
        for k, (dev, idx) in enumerate(peers):
            pltpu.make_async_remote_copy(
                src_ref=mine, dst_ref=out_ref.at[idx], send_sem=send_sems.at[k], recv_sem=recv_sems.at[k],
                device_id=dev, device_id_type=MESH_ID).wait_recv()
        for cp in sends:
            cp.wait_send()
        local.wait()

    return pl.pallas_call(
        body, name=name, in_specs=[ANY], out_specs=ANY,
        out_shape=jax.ShapeDtypeStruct((N_DEV,) + tuple(blk), src.dtype),
        scratch_shapes=[pltpu.SemaphoreType.DMA((N_DEV - 1,)), pltpu.SemaphoreType.DMA((N_DEV - 1,)),
                        pltpu.SemaphoreType.DMA],
    )(src)


def _gather_two_level(name, src):
    def body(x_ref, out_ref, send_sems, recv_sems, local_sem):
        x, y, c = lax.axis_index("x"), lax.axis_index("y"), lax.axis_index("c")
        me, sibling = (x, y, c), (x, y, 1 - c)
        chips = [(1 - x, y), (x, 1 - y), (1 - x, 1 - y)]

        def rows(px, py, pc):
            return out_ref.at[4 * px + 2 * py + pc]

        def copy(k, block, to, src_ref=None):
            return pltpu.make_async_remote_copy(
                src_ref=rows(*block) if src_ref is None else src_ref, dst_ref=rows(*block),
                send_sem=send_sems.at[k], recv_sem=recv_sems.at[k], device_id=to, device_id_type=MESH_ID)

        mine = pltpu.make_async_copy(x_ref, rows(*me), local_sem)
        mine.start()
        first = [copy(0, me, sibling, x_ref)] + [copy(1 + j, me, (*chip, c), x_ref) for j, chip in enumerate(chips)]
        for cp in first:
            cp.start()
        passed = [copy(4 + j, (*chip, c), sibling) for j, chip in enumerate(chips)]
        for j, chip in enumerate(chips):
            copy(1 + j, (*chip, c), me).wait_recv()
            passed[j].start()
        copy(0, sibling, me).wait_recv()
        for j, chip in enumerate(chips):
            copy(4 + j, (*chip, 1 - c), me).wait_recv()
        for cp in first + passed:
            cp.wait_send()
        mine.wait()

    return pl.pallas_call(
        body, name=name, in_specs=[ANY], out_specs=ANY,
        out_shape=jax.ShapeDtypeStruct((N_DEV,) + tuple(src.shape), src.dtype),
        scratch_shapes=[pltpu.SemaphoreType.DMA((N_DEV - 1,)), pltpu.SemaphoreType.DMA((N_DEV - 1,)),
                        pltpu.SemaphoreType.DMA],
    )(src)


HBM = pl.BlockSpec(memory_space=pltpu.HBM)
SEM = pl.BlockSpec(memory_space=pltpu.SEMAPHORE)
EFFECT = pltpu.SideEffectType.DATAFLOW_SIDE_EFFECTING


def _peer_copy(src_ref, land_ref, send_sems, recv_sems, per_peer, me, a, k, dev, idx, receiving):
    s = a * (N_DEV - 1) + k
    return pltpu.make_async_remote_copy(
        src_ref=src_ref.at[idx] if per_peer else src_ref, dst_ref=land_ref.at[idx if receiving else me],
        send_sem=send_sems.at[s], recv_sem=recv_sems.at[s], device_id=dev, device_id_type=MESH_ID)


def _push_start(name, srcs, per_peer):
    n = len(srcs)
    land_shapes = [(N_DEV,) + tuple(s.shape[1:] if per_peer else s.shape) for s in srcs]

    def body(*refs):
        src_refs, land_refs, send_sems, recv_sems, token = refs[:n], refs[n:2 * n], refs[2 * n], refs[2 * n + 1], refs[-1]
        me, peers = _me_and_peers()
        for a in range(n):
            for k, (dev, idx) in enumerate(peers):
                _peer_copy(src_refs[a], land_refs[a], send_sems, recv_sems, per_peer, me, a, k, dev, idx, False).start()
        token[...] = jnp.zeros_like(token)

    sems = pltpu.SemaphoreType.DMA((n * (N_DEV - 1),))
    res = pl.pallas_call(
        body, name=name,
        out_shape=(sems, sems, *[pltpu.HBM(s.shape, s.dtype) for s in srcs],
                   *[pltpu.HBM(ls, s.dtype) for ls, s in zip(land_shapes, srcs)], jax.ShapeDtypeStruct((8, 128), F32)),
        in_specs=(HBM,) * (2 * n), out_specs=(SEM, SEM) + (HBM,) * (2 * n) + (pl.BlockSpec(memory_space=pltpu.VMEM),),
        input_output_aliases={i: 2 + i for i in range(2 * n)},
        compiler_params=pltpu.CompilerParams(has_side_effects=EFFECT),
    )(*[pltpu.with_memory_space_constraint(s, pltpu.HBM) for s in srcs],
      *[pltpu.with_memory_space_constraint(lax.empty(ls, s.dtype), pltpu.HBM) for ls, s in zip(land_shapes, srcs)])
    return res[0], res[1], res[2:2 + n], res[2 + n:2 + 2 * n], res[-1]


def _push_wait(name, send_sems, recv_sems, srcs_thru, lands_thru, after, per_peer):
    n = len(srcs_thru)

    def body(*refs):
        src_refs, land_refs, send_sems, recv_sems = refs[:n], refs[n:2 * n], refs[2 * n], refs[2 * n + 1]
        me, peers = _me_and_peers()
        for a in range(n):
            for k, (dev, idx) in enumerate(peers):
                cp = _peer_copy(src_refs[a], land_refs[a], send_sems, recv_sems, per_peer, me, a, k, dev, idx, True)
                cp.wait_send()
                cp.wait_recv()

    both = list(srcs_thru) + list(lands_thru)
    res = pl.pallas_call(
        body, name=name, out_shape=tuple(pltpu.HBM(t.shape, t.dtype) for t in both),
        in_specs=(HBM,) * (2 * n) + (SEM, SEM, ANY), out_specs=(HBM,) * (2 * n),
        input_output_aliases={i: i for i in range(2 * n)},
        compiler_params=pltpu.CompilerParams(has_side_effects=EFFECT),
    )(*both, send_sems, recv_sems, after)
    return res[:n], res[n:]


def _sum_blocks(name, blocks):
    _, R, C = blocks.shape
    tc = next(t for t in (1024, 512, 256, 128) if C % t == 0 and (N_DEV * R * t * 2 <= 6 * 2 ** 20 or t == 128))

    def body(b_ref, o_ref):
        acc = b_ref[0].astype(F32)
        for k in range(1, N_DEV):
            acc = acc + b_ref[k].astype(F32)
        o_ref[...] = acc

    return pl.pallas_call(
        body, name=name, grid=(C // tc,), in_specs=[pl.BlockSpec((N_DEV, R, tc), lambda j: (0, 0, j))],
        out_specs=pl.BlockSpec((R, tc), lambda j: (0, j)), out_shape=jax.ShapeDtypeStruct((R, C), F32),
        compiler_params=_params(("arbitrary",)),
    )(blocks)


def _adamw(name, w, g, m, v):
    R, C = w.shape
    tr = R if R <= 512 else _pick(R, (256, 184, 176, 128, 8))
    spec = pl.BlockSpec((tr, C), lambda i: (i, 0))

    def body(w_ref, g_ref, m_ref, v_ref, d_ref, mo_ref, vo_ref):
        gv = g_ref[...]
        mn = ADAM_B1 * m_ref[...] + (1.0 - ADAM_B1) * gv
        vn = ADAM_B2 * v_ref[...] + (1.0 - ADAM_B2) * jnp.square(gv)
        m_hat = mn / (1.0 - ADAM_B1 ** ADAM_STEP)
        v_hat = vn / (1.0 - ADAM_B2 ** ADAM_STEP)
        d_ref[...] = -ADAM_LR * (m_hat / (jnp.sqrt(v_hat) + ADAM_EPS) + ADAM_WD * w_ref[...])
        mo_ref[...] = mn
        vo_ref[...] = vn

    return pl.pallas_call(
        body, name=name, grid=(R // tr,), in_specs=[spec] * 4, out_specs=[spec] * 3,
        out_shape=[jax.ShapeDtypeStruct((R, C), F32)] * 3, compiler_params=_params(("arbitrary",)),
    )(w, g, m, v)


WEIGHTS = ("meta_tokens", "norm_mix_w", "w_in", "ret_gn_w", "w_ret_out", "w_ssd_conv", "b_ssd_conv", "dt_bias_f",
           "dt_bias_b", "a_log_f", "a_log_b", "d_skip", "ssd_norm_w", "w_ssd_out", "w_out", "norm_ffn_w", "w_ffn_up",
           "w_ffn_conv", "b_ffn_conv", "w_ffn_down", "final_norm_w")
BIG = (("w_in", 1288, True), ("w_ffn_up", 704, True), ("w_ret_out", 128, False), ("w_ssd_out", 256, False),
       ("w_out", 128, False), ("w_ffn_down", 352, False))
REPLICATED = ("norm_mix_w", "ret_gn_w", "b_ssd_conv", "dt_bias_f", "dt_bias_b", "a_log_f", "a_log_b", "d_skip",
              "ssd_norm_w", "norm_ffn_w", "b_ffn_conv", "final_norm_w")
SMALL_SHARDED = (("meta_tokens", 16, 1024), ("w_ssd_conv", 3, 3072), ("w_ffn_conv", 3, 5632))


BIG_IN, BIG_REST = BIG[:1], BIG[1:]


def _pack_flat(arrays, rows):
    flat = jnp.concatenate([a.reshape(-1) for a in arrays])
    return jnp.pad(flat, (0, rows * D_MODEL - flat.shape[0])).reshape(rows, D_MODEL)


def _unpack_flat(slab, shapes):
    flat, out, o = slab.reshape(-1), [], 0
    for s in shapes:
        n = math.prod(s)
        out.append(flat[o:o + n].reshape(s))
        o += n
    return out


def kernel(x, meta_tokens, norm_mix_w, w_in, ret_gn_w, w_ret_out, w_ssd_conv, b_ssd_conv, dt_bias_f, dt_bias_b, a_log_f, a_log_b, d_skip, ssd_norm_w, w_ssd_out, w_out, norm_ffn_w, w_ffn_up, w_ffn_conv, b_ffn_conv, w_ffn_down, final_norm_w, loss_target, m_meta_tokens, m_norm_mix_w, m_w_in, m_ret_gn_w, m_w_ret_out, m_w_ssd_conv, m_b_ssd_conv, m_dt_bias_f, m_dt_bias_b, m_a_log_f, m_a_log_b, m_d_skip, m_ssd_norm_w, m_w_ssd_out, m_w_out, m_norm_ffn_w, m_w_ffn_up, m_w_ffn_conv, m_b_ffn_conv, m_w_ffn_down, m_final_norm_w, v_meta_tokens, v_norm_mix_w, v_w_in, v_ret_gn_w, v_w_ret_out, v_w_ssd_conv, v_b_ssd_conv, v_dt_bias_f, v_dt_bias_b, v_a_log_f, v_a_log_b, v_d_skip, v_ssd_norm_w, v_w_ssd_out, v_w_out, v_norm_ffn_w, v_w_ffn_up, v_w_ffn_conv, v_b_ffn_conv, v_w_ffn_down, v_final_norm_w):
    given = dict(locals())
    wt = {n: given[n] for n in WEIGHTS}
    mt = {n: given["m_" + n] for n in WEIGHTS}
    vt = {n: given["v_" + n] for n in WEIGHTS}
    me = 4 * lax.axis_index("x") + 2 * lax.axis_index("y") + lax.axis_index("c")

    small_names = [n for n, _, _ in SMALL_SHARDED]
    small_local = lambda tree: [tree[n].reshape(r, c // N_DEV) for n, r, c in SMALL_SHARDED]
    slab_view = lambda tree, name, transposed: tree[name][0].T if transposed else tree[name][0]
    def lands_with_own(flight, after, per_peer, name):
        srcs, lands = _push_wait(name, *flight[:4], after, per_peer)
        own = lambda s: lax.dynamic_slice_in_dim(s, me, 1, axis=0) if per_peer else s[None]
        return [lax.dynamic_update_slice_in_dim(land, own(s), me, axis=0) for s, land in zip(srcs, lands)]

    small_flight_w = _push_start("gather_small_start", [_pack_flat(small_local(wt), 8)], False)
    w_in_src = slab_view(wt, "w_in", True).astype(BF16) + small_flight_w[4][0, 0].astype(BF16)
    all_in = _gather_two_level("gather_w_in", w_in_src)
    all_s = lands_with_own(small_flight_w, all_in, False, "gather_small_wait")[0]
    rest_srcs = [slab_view(wt, name, t).astype(BF16) for name, _, t in BIG_REST]
    rest_srcs, all_in, all_s = lax.optimization_barrier((rest_srcs, all_in, all_s))
    rest_flight = _push_start("gather_rest_start", rest_srcs, False)
    all_s = all_s.reshape(N_DEV, -1)
    full = {"w_in_t": all_in.reshape(-1, D_MODEL)}

    def late_weights(after):
        lands = lands_with_own(rest_flight, after, False, "gather_rest_wait")
        return {name + ("_t" if t else ""): land.reshape(N_DEV * r, D_MODEL) for (name, r, t), land in zip(BIG_REST, lands)}

    flights = {}

    def start_exchange(key, group, gd):
        srcs = [gd[name + ("_t" if t else "")].astype(BF16).reshape(N_DEV, r, D_MODEL) for name, r, t in group]
        flights[key] = _push_start("exchange_" + key + "_start", srcs, True)
        return flights[key][4][0, 0]

    o = 0
    for name, r, c in SMALL_SHARDED:
        n = r * c // N_DEV
        full[name] = all_s[:, o:o + n].reshape(N_DEV, r, c // N_DEV).transpose(1, 0, 2).reshape(r, c)
        o += n
    for name in REPLICATED:
        full[name] = wt[name]

    grads, delta, new_m, new_v = {}, {}, {}, {}

    def finish_exchange(key, group, after):
        lands = lands_with_own(flights[key], after, True, "exchange_" + key + "_wait")
        for (name, _, transposed), land in zip(group, lands):
            back = (lambda a: a.T[None]) if transposed else (lambda a: a[None])
            g_sum = _sum_blocks("sum_" + name, land)
            d, mn, vn = _adamw("adamw_" + name, slab_view(wt, name, transposed), g_sum,
                               slab_view(mt, name, transposed), slab_view(vt, name, transposed))
            grads[name], delta[name], new_m[name], new_v[name] = back(g_sum), back(d), back(mn), back(vn)

    def in_grads(gi):
        tick = start_exchange("in", BIG_IN, {"w_in_t": gi})
        finish_exchange("rest", BIG_REST, flights["in"][4])
        tick, _ = lax.optimization_barrier((tick, [delta[name] for name, _, _ in BIG_REST]))
        return tick

    loss, grad_x, g = _local_step(x[0], loss_target[0], full, rest_flight[4][0, 0], late_weights,
                                  lambda gd: start_exchange("rest", BIG_REST, gd), in_grads)

    small_parts = [g[n] for n in REPLICATED] + [g[n] for n in small_names] + [loss.reshape(1)]
    small_flight = _push_start("gather_small_grads_start", [_pack_flat(small_parts, 64)], False)
    finish_exchange("in", BIG_IN, small_flight[4])
    g_small = _sum_blocks("sum_small", lands_with_own(small_flight, delta["w_in"], False, "gather_small_grads_wait")[0])
    small_red = _unpack_flat(g_small, [wt[n].shape for n in REPLICATED] + [(r, c) for _, r, c in SMALL_SHARDED] + [(1,)])
    grads.update(zip(REPLICATED, small_red[:len(REPLICATED)]))
    for (name, r, c), red in zip(SMALL_SHARDED, small_red[len(REPLICATED):-1]):
        grads[name] = lax.dynamic_slice(red, (0, me * (c // N_DEV)), (r, c // N_DEV)).reshape(wt[name].shape)
    loss_all = small_red[-1][0]

    rest = list(REPLICATED) + small_names
    shapes = [wt[n].shape for n in rest]
    pack_rest = lambda tree: _pack_flat([tree[n] for n in rest], 24)
    d_rest, m_rest, v_rest = _adamw("adamw_small", pack_rest(wt), pack_rest(grads), pack_rest(mt), pack_rest(vt))
    delta.update(zip(rest, _unpack_flat(d_rest, shapes)))
    new_m.update(zip(rest, _unpack_flat(m_rest, shapes)))
    new_v.update(zip(rest, _unpack_flat(v_rest, shapes)))

    return (loss_all, grad_x[None], *[grads[n] for n in WEIGHTS], *[delta[n] for n in WEIGHTS],
            *[new_m[n] for n in WEIGHTS], *[new_v[n] for n in WEIGHTS])
```
